```python
import math
import jax, jax.numpy as jnp
from jax import lax
import numpy as np

D_MODEL = 1024
BATCH = 8
SEQ = 8192
DEPTH = 1

POOL_WINDOWS = (2, 4, 8, 16)
N_POOL_GROUPS = 4
POOL_WIDTH = D_MODEL // 2
POOL_GROUP = POOL_WIDTH // N_POOL_GROUPS
POOL_OUT_GROUP = D_MODEL // N_POOL_GROUPS
DN_HEADS = 8
DN_HEAD_DIM = 128
DN_WIDTH = DN_HEADS * DN_HEAD_DIM
CONV_K = 4
CHUNK = 64
D_FF = 4 * D_MODEL
PLE_DIM = 256
LN_EPS = 1e-5
RMS_EPS = 1e-6
L2_EPS = 1e-6
DEEPNORM_ALPHA = (2.0 * DEPTH) ** 0.25
DEEPNORM_BETA = (8.0 * DEPTH) ** -0.25
QKV_WIDTH = 3 * DN_WIDTH
IN_WIDTH = POOL_WIDTH + QKV_WIDTH + DN_WIDTH + DN_HEADS + DN_HEADS + 2 * D_MODEL
SPLIT_POINTS = (
    POOL_WIDTH,
    POOL_WIDTH + QKV_WIDTH,
    POOL_WIDTH + QKV_WIDTH + DN_WIDTH,
    POOL_WIDTH + QKV_WIDTH + DN_WIDTH + DN_HEADS,
    POOL_WIDTH + QKV_WIDTH + DN_WIDTH + 2 * DN_HEADS,
    POOL_WIDTH + QKV_WIDTH + DN_WIDTH + 2 * DN_HEADS + D_MODEL,
)

kernel_name = "hybrid_pool_gdn_deepnorm_ple"


def _layer_norm(x, g, b):
    xf = x.astype(jnp.float32)
    mu = jnp.mean(xf, axis=-1, keepdims=True)
    var = jnp.mean(jnp.square(xf - mu), axis=-1, keepdims=True)
    y = (xf - mu) * lax.rsqrt(var + LN_EPS) * g.astype(jnp.float32) + b.astype(jnp.float32)
    return y.astype(x.dtype)


def _multiscale_pool(u, pool_w, pool_scale):
    bsz, seq, _ = u.shape
    ug = u.astype(jnp.float32).reshape(bsz, seq, N_POOL_GROUPS, POOL_GROUP)
    cs = jnp.cumsum(ug, axis=1)
    outs = []
    for gi, w in enumerate(POOL_WINDOWS):
        c = cs[:, :, gi]
        prev = jnp.pad(c, ((0, 0), (w, 0), (0, 0)))[:, :seq]
        cnt = jnp.minimum(jnp.arange(1, seq + 1), w).astype(jnp.float32)[None, :, None]
        outs.append((c - prev) / cnt - ug[:, :, gi])
    d = jnp.stack(outs, axis=2).astype(u.dtype)
    y = jnp.einsum('bsgc,gcd->bsgd', d, pool_w).reshape(bsz, seq, D_MODEL)
    return y * pool_scale


def _causal_depthwise_conv_silu(x, w):
    ch = x.shape[-1]
    y = lax.conv_general_dilated(
        x, w[:, None, :].astype(x.dtype), window_strides=(1,),
        padding=((CONV_K - 1, 0),), dimension_numbers=('NWC', 'WIO', 'NWC'),
        feature_group_count=ch)
    return jax.nn.silu(y)


def _l2norm(t):
    return t * lax.rsqrt(jnp.sum(jnp.square(t), axis=-1, keepdims=True) + L2_EPS)


def _chunk_gated_delta_rule(q, k, v, beta, g):
    bsz, seq, nh, dk = q.shape
    dv = v.shape[-1]
    n_chunks = seq // CHUNK

    def to_chunks(t):
        t = t.reshape((bsz, n_chunks, CHUNK, nh) + t.shape[3:])
        return jnp.moveaxis(t, 3, 2)

    q, k, v, beta, g = (to_chunks(t) for t in (q, k, v, beta, g))
    gc = jnp.cumsum(g, axis=-1)
    idx = jnp.arange(CHUNK)
    incl = idx[:, None] >= idx[None, :]
    strict = idx[:, None] > idx[None, :]
    decay = jnp.exp(jnp.where(incl, gc[..., :, None] - gc[..., None, :], -jnp.inf))
    kb = k * beta[..., None]
    m = jnp.where(strict, jnp.einsum('bnhid,bnhjd->bnhij', kb, k) * decay, 0.0)
    a = m + jnp.eye(CHUNK, dtype=m.dtype)
    rhs = jnp.concatenate([v * beta[..., None], kb * jnp.exp(gc)[..., None]], axis=-1)
    sol = lax.linalg.triangular_solve(a, rhs, left_side=True, lower=True, unit_diagonal=True)
    u, w = sol[..., :dv], sol[..., dv:]
    attn = jnp.einsum('bnhid,bnhjd->bnhij', q, k) * decay
    qg = q * jnp.exp(gc)[..., None]
    gl = gc[..., -1]
    kg = k * jnp.exp(gl[..., None] - gc)[..., None]
    xs = tuple(jnp.moveaxis(t, 1, 0) for t in (qg, kg, u, w, attn, gl))

    def step(state, inp):
        qg_n, kg_n, u_n, w_n, attn_n, gl_n = inp
        v_new = u_n - jnp.einsum('bhcd,bhde->bhce', w_n, state)
        o = jnp.einsum('bhcd,bhde->bhce', qg_n, state) + jnp.einsum('bhij,bhje->bhie', attn_n, v_new)
        state = state * jnp.exp(gl_n)[..., None, None] + jnp.einsum('bhcd,bhce->bhde', kg_n, v_new)
        return state, o

    s0 = jnp.zeros((bsz, nh, dk, dv), jnp.float32)
    _, o = lax.scan(step, s0, xs)
    o = jnp.moveaxis(o, 0, 1)
    return jnp.moveaxis(o, 2, 3).reshape(bsz, seq, nh, dv)


def _gated_deltanet(qkv, z, beta_raw, a_raw, conv_w, a_log, dt_bias, o_norm_w):
    bsz, seq, _ = qkv.shape
    qkv = _causal_depthwise_conv_silu(qkv, conv_w)
    q, k, v = jnp.split(qkv.astype(jnp.float32), 3, axis=-1)
    shp = (bsz, seq, DN_HEADS, DN_HEAD_DIM)
    q = _l2norm(q.reshape(shp)) * (DN_HEAD_DIM ** -0.5)
    k = _l2norm(k.reshape(shp))
    v = v.reshape(shp)
    beta = jax.nn.sigmoid(beta_raw.astype(jnp.float32))
    g = -jnp.exp(a_log.astype(jnp.float32)) * jax.nn.softplus(
        a_raw.astype(jnp.float32) + dt_bias.astype(jnp.float32))
    o = _chunk_gated_delta_rule(q, k, v, beta, g)
    o = o * lax.rsqrt(jnp.mean(jnp.square(o), axis=-1, keepdims=True) + RMS_EPS)
    o = o * o_norm_w.astype(jnp.float32) * jax.nn.silu(z.astype(jnp.float32).reshape(shp))
    return o.reshape(bsz, seq, DN_WIDTH).astype(qkv.dtype)


def _fwd_setup_inputs(seed: int = 0) -> dict:
    key = jax.random.key(seed)
    ks = jax.random.split(key, 24)
    f32 = jnp.float32
    nrm = lambda k, shape, s: jax.random.normal(k, shape, f32) * s
    x = jax.random.normal(ks[0], (BATCH, SEQ, D_MODEL), f32)
    p = jax.random.normal(ks[1], (DEPTH, BATCH, SEQ, PLE_DIM), f32)
    ln_in_g = 1.0 + nrm(ks[2], (D_MODEL,), 0.02)
    ln_in_b = nrm(ks[3], (D_MODEL,), 0.02)
    s_in = D_MODEL ** -0.5
    w_pool_qk = nrm(ks[4], (DEPTH, D_MODEL, POOL_WIDTH + 2 * DN_WIDTH), s_in)
    w_v = nrm(ks[5], (DEPTH, D_MODEL, DN_WIDTH), s_in * DEEPNORM_BETA)
    w_rest = nrm(ks[6], (DEPTH, D_MODEL, IN_WIDTH - POOL_WIDTH - QKV_WIDTH), s_in)
    w_in = jnp.concatenate([w_pool_qk, w_v, w_rest], axis=-1)
    pool_w = nrm(ks[7], (DEPTH, N_POOL_GROUPS, POOL_GROUP, POOL_OUT_GROUP), POOL_GROUP ** -0.5)
    pool_scale = 1.0 + nrm(ks[8], (DEPTH, D_MODEL), 0.02)
    conv_w = nrm(ks[9], (DEPTH, CONV_K, QKV_WIDTH), CONV_K ** -0.5)
    a_log = jnp.log(jax.random.uniform(ks[10], (DEPTH, DN_HEADS), f32, 1.0, 16.0))
    dt = jnp.exp(jax.random.uniform(ks[11], (DEPTH, DN_HEADS), f32, math.log(1e-3), math.log(1e-1)))
    dt_bias = dt + jnp.log(-jnp.expm1(-dt))
    o_norm_w = 1.0 + nrm(ks[12], (DEPTH, DN_HEAD_DIM), 0.02)
    w_out = nrm(ks[13], (DEPTH, D_MODEL, D_MODEL), s_in * DEEPNORM_BETA)
    ln1_g = 1.0 + nrm(ks[14], (DEPTH, D_MODEL), 0.02)
    ln1_b = nrm(ks[15], (DEPTH, D_MODEL), 0.02)
    w_up = nrm(ks[16], (DEPTH, D_MODEL, D_FF), s_in)
    w_down = nrm(ks[17], (DEPTH, D_FF, D_MODEL), D_FF ** -0.5 * DEEPNORM_BETA)
    ple_gate_w = nrm(ks[18], (DEPTH, D_MODEL, D_MODEL), s_in)
    ple_proj_w = nrm(ks[19], (DEPTH, PLE_DIM, D_MODEL), PLE_DIM ** -0.5 * DEEPNORM_BETA)
    ln2_g = 1.0 + nrm(ks[20], (DEPTH, D_MODEL), 0.02)
    ln2_b = nrm(ks[21], (DEPTH, D_MODEL), 0.02)
    return {"x": x, "p": p, "ln_in_g": ln_in_g, "ln_in_b": ln_in_b, "w_in": w_in,
            "pool_w": pool_w, "pool_scale": pool_scale, "conv_w": conv_w, "a_log": a_log,
            "dt_bias": dt_bias, "o_norm_w": o_norm_w, "w_out": w_out, "ln1_g": ln1_g,
            "ln1_b": ln1_b, "w_up": w_up, "w_down": w_down, "ple_gate_w": ple_gate_w,
            "ple_proj_w": ple_proj_w, "ln2_g": ln2_g, "ln2_b": ln2_b}


def _fwd_reference(x, p, ln_in_g, ln_in_b, w_in, pool_w, pool_scale, conv_w, a_log, dt_bias,
              o_norm_w, w_out, ln1_g, ln1_b, w_up, w_down, ple_gate_w, ple_proj_w, ln2_g, ln2_b):
    h = _layer_norm(x, ln_in_g, ln_in_b)
    for i in range(DEPTH):
        proj = h @ w_in[i]
        pool_in, qkv, z, beta_raw, a_raw, gate_a, gate_b = jnp.split(proj, SPLIT_POINTS, axis=-1)
        y_a = _multiscale_pool(pool_in, pool_w[i], pool_scale[i])
        y_b = _gated_deltanet(qkv, z, beta_raw, a_raw, conv_w[i], a_log[i], dt_bias[i], o_norm_w[i])
        mixed = jax.nn.sigmoid(gate_a) * y_a + jax.nn.sigmoid(gate_b) * y_b
        h = _layer_norm(DEEPNORM_ALPHA * h + mixed @ w_out[i], ln1_g[i], ln1_b[i])
        mlp = jnp.square(jax.nn.relu(h @ w_up[i])) @ w_down[i]
        r = DEEPNORM_ALPHA * h + mlp
        ple = jax.nn.sigmoid(r @ ple_gate_w[i]) * (p[i] @ ple_proj_w[i])
        h = _layer_norm(r + ple, ln2_g[i], ln2_b[i])
    return h


import jax as _jax
import jax.numpy as _jnp

TWIN_FORMAT = 'train_step'
FWD_PARAMS = ['x', 'p', 'ln_in_g', 'ln_in_b', 'w_in', 'pool_w', 'pool_scale', 'conv_w', 'a_log', 'dt_bias', 'o_norm_w', 'w_out', 'ln1_g', 'ln1_b', 'w_up', 'w_down', 'ple_gate_w', 'ple_proj_w', 'ln2_g', 'ln2_b']
TWIN_WEIGHTS = ['ln_in_g', 'ln_in_b', 'w_in', 'pool_w', 'pool_scale', 'conv_w', 'a_log', 'dt_bias', 'o_norm_w', 'w_out', 'ln1_g', 'ln1_b', 'w_up', 'w_down', 'ple_gate_w', 'ple_proj_w', 'ln2_g', 'ln2_b']
TWIN_DIFF_INPUT = 'x'
TWIN_INPUTS = ['x', 'p', 'ln_in_g', 'ln_in_b', 'w_in', 'pool_w', 'pool_scale', 'conv_w', 'a_log', 'dt_bias', 'o_norm_w', 'w_out', 'ln1_g', 'ln1_b', 'w_up', 'w_down', 'ple_gate_w', 'ple_proj_w', 'ln2_g', 'ln2_b', 'loss_target', 'm_ln_in_g', 'm_ln_in_b', 'm_w_in', 'm_pool_w', 'm_pool_scale', 'm_conv_w', 'm_a_log', 'm_dt_bias', 'm_o_norm_w', 'm_w_out', 'm_ln1_g', 'm_ln1_b', 'm_w_up', 'm_w_down', 'm_ple_gate_w', 'm_ple_proj_w', 'm_ln2_g', 'm_ln2_b', 'v_ln_in_g', 'v_ln_in_b', 'v_w_in', 'v_pool_w', 'v_pool_scale', 'v_conv_w', 'v_a_log', 'v_dt_bias', 'v_o_norm_w', 'v_w_out', 'v_ln1_g', 'v_ln1_b', 'v_w_up', 'v_w_down', 'v_ple_gate_w', 'v_ple_proj_w', 'v_ln2_g', 'v_ln2_b']
TWIN_OUTPUTS = ['loss', 'grad_x', 'grad_ln_in_g', 'grad_ln_in_b', 'grad_w_in', 'grad_pool_w', 'grad_pool_scale', 'grad_conv_w', 'grad_a_log', 'grad_dt_bias', 'grad_o_norm_w', 'grad_w_out', 'grad_ln1_g', 'grad_ln1_b', 'grad_w_up', 'grad_w_down', 'grad_ple_gate_w', 'grad_ple_proj_w', 'grad_ln2_g', 'grad_ln2_b', 'delta_ln_in_g', 'delta_ln_in_b', 'delta_w_in', 'delta_pool_w', 'delta_pool_scale', 'delta_conv_w', 'delta_a_log', 'delta_dt_bias', 'delta_o_norm_w', 'delta_w_out', 'delta_ln1_g', 'delta_ln1_b', 'delta_w_up', 'delta_w_down', 'delta_ple_gate_w', 'delta_ple_proj_w', 'delta_ln2_g', 'delta_ln2_b', 'new_m_ln_in_g', 'new_m_ln_in_b', 'new_m_w_in', 'new_m_pool_w', 'new_m_pool_scale', 'new_m_conv_w', 'new_m_a_log', 'new_m_dt_bias', 'new_m_o_norm_w', 'new_m_w_out', 'new_m_ln1_g', 'new_m_ln1_b', 'new_m_w_up', 'new_m_w_down', 'new_m_ple_gate_w', 'new_m_ple_proj_w', 'new_m_ln2_g', 'new_m_ln2_b', 'new_v_ln_in_g', 'new_v_ln_in_b', 'new_v_w_in', 'new_v_pool_w', 'new_v_pool_scale', 'new_v_conv_w', 'new_v_a_log', 'new_v_dt_bias', 'new_v_o_norm_w', 'new_v_w_out', 'new_v_ln1_g', 'new_v_ln1_b', 'new_v_w_up', 'new_v_w_down', 'new_v_ple_gate_w', 'new_v_ple_proj_w', 'new_v_ln2_g', 'new_v_ln2_b']
TWIN_LEAF_KINDS = {'loss': 'loss', 'grad_x': 'grad_x', 'grad_ln_in_g': 'grad_w', 'grad_ln_in_b': 'grad_w', 'grad_w_in': 'grad_w', 'grad_pool_w': 'grad_w', 'grad_pool_scale': 'grad_w', 'grad_conv_w': 'grad_w', 'grad_a_log': 'grad_w', 'grad_dt_bias': 'grad_w', 'grad_o_norm_w': 'grad_w', 'grad_w_out': 'grad_w', 'grad_ln1_g': 'grad_w', 'grad_ln1_b': 'grad_w', 'grad_w_up': 'grad_w', 'grad_w_down': 'grad_w', 'grad_ple_gate_w': 'grad_w', 'grad_ple_proj_w': 'grad_w', 'grad_ln2_g': 'grad_w', 'grad_ln2_b': 'grad_w', 'delta_ln_in_g': 'delta_w', 'delta_ln_in_b': 'delta_w', 'delta_w_in': 'delta_w', 'delta_pool_w': 'delta_w', 'delta_pool_scale': 'delta_w', 'delta_conv_w': 'delta_w', 'delta_a_log': 'delta_w', 'delta_dt_bias': 'delta_w', 'delta_o_norm_w': 'delta_w', 'delta_w_out': 'delta_w', 'delta_ln1_g': 'delta_w', 'delta_ln1_b': 'delta_w', 'delta_w_up': 'delta_w', 'delta_w_down': 'delta_w', 'delta_ple_gate_w': 'delta_w', 'delta_ple_proj_w': 'delta_w', 'delta_ln2_g': 'delta_w', 'delta_ln2_b': 'delta_w', 'new_m_ln_in_g': 'new_m', 'new_m_ln_in_b': 'new_m', 'new_m_w_in': 'new_m', 'new_m_pool_w': 'new_m', 'new_m_pool_scale': 'new_m', 'new_m_conv_w': 'new_m', 'new_m_a_log': 'new_m', 'new_m_dt_bias': 'new_m', 'new_m_o_norm_w': 'new_m', 'new_m_w_out': 'new_m', 'new_m_ln1_g': 'new_m', 'new_m_ln1_b': 'new_m', 'new_m_w_up': 'new_m', 'new_m_w_down': 'new_m', 'new_m_ple_gate_w': 'new_m', 'new_m_ple_proj_w': 'new_m', 'new_m_ln2_g': 'new_m', 'new_m_ln2_b': 'new_m', 'new_v_ln_in_g': 'new_v', 'new_v_ln_in_b': 'new_v', 'new_v_w_in': 'new_v', 'new_v_pool_w': 'new_v', 'new_v_pool_scale': 'new_v', 'new_v_conv_w': 'new_v', 'new_v_a_log': 'new_v', 'new_v_dt_bias': 'new_v', 'new_v_o_norm_w': 'new_v', 'new_v_w_out': 'new_v', 'new_v_ln1_g': 'new_v', 'new_v_ln1_b': 'new_v', 'new_v_w_up': 'new_v', 'new_v_w_down': 'new_v', 'new_v_ple_gate_w': 'new_v', 'new_v_ple_proj_w': 'new_v', 'new_v_ln2_g': 'new_v', 'new_v_ln2_b': 'new_v'}


def _forward(args):
    return _fwd_reference(*[args[k] for k in FWD_PARAMS])


def _output_shape():
    def fwd():
        inp = _fwd_setup_inputs(0)
        return _fwd_reference(*[inp[k] for k in FWD_PARAMS])
    out = _jax.eval_shape(fwd)
    return out.shape, out.dtype

N_MICROBATCH = 1
ADAM_LR = 0.001
ADAM_B1 = 0.9
ADAM_B2 = 0.999
ADAM_EPS = 1e-08
ADAM_WD = 0.01
ADAM_STEP = 10
PER_EXAMPLE_BATCH_AXIS = {'x': 0, 'p': 1, 'loss_target': 0}
SHARED_INPUTS = []
_WEIGHT_DTYPES = {'ln_in_g': _jnp.float32, 'ln_in_b': _jnp.float32, 'w_in': _jnp.float32, 'pool_w': _jnp.float32, 'pool_scale': _jnp.float32, 'conv_w': _jnp.float32, 'a_log': _jnp.float32, 'dt_bias': _jnp.float32, 'o_norm_w': _jnp.float32, 'w_out': _jnp.float32, 'ln1_g': _jnp.float32, 'ln1_b': _jnp.float32, 'w_up': _jnp.float32, 'w_down': _jnp.float32, 'ple_gate_w': _jnp.float32, 'ple_proj_w': _jnp.float32, 'ln2_g': _jnp.float32, 'ln2_b': _jnp.float32}
MOMENT_SCALE = {'ln_in_g': 1.301525e+00, 'ln_in_b': 9.900536e-01, 'w_in': 4.056322e-02, 'pool_w': 5.853364e-02, 'pool_scale': 6.150144e-02, 'conv_w': 3.101491e-02, 'a_log': 1.229621e-01, 'dt_bias': 1.202994e-01, 'o_norm_w': 1.289798e-01, 'w_out': 1.207527e-01, 'ln1_g': 1.436507e+00, 'ln1_b': 1.034584e+00, 'w_up': 7.457915e-02, 'w_down': 3.750171e-01, 'ple_gate_w': 2.835625e-02, 'ple_proj_w': 9.802003e-02, 'ln2_g': 6.424800e+01, 'ln2_b': 1.349333e+01}


def _to_microbatches(a, axis):
    t = _jnp.moveaxis(a, axis, 0)
    t = t.reshape((N_MICROBATCH, t.shape[0] // N_MICROBATCH) + t.shape[1:])
    return _jnp.moveaxis(t, 1, axis + 1)


def setup_inputs(seed: int = 0) -> dict:
    inp = _fwd_setup_inputs(seed)
    key = _jax.random.fold_in(_jax.random.key(seed), 7919)
    shape, _ = _output_shape()
    out = dict(inp)
    out["loss_target"] = _jax.random.normal(_jax.random.fold_in(key, 0), shape, _jnp.float32)
    for i, name in enumerate(TWIN_WEIGHTS):
        w = inp[name].astype(_jnp.float32)
        if MOMENT_SCALE is None:
            s = _jnp.sqrt(_jnp.mean(_jnp.square(w)) + 1e-30)
        else:
            s = MOMENT_SCALE[name]
        km, kv = _jax.random.split(_jax.random.fold_in(key, i + 1))
        out[name] = w
        out["m_" + name] = s * _jax.random.normal(km, w.shape, _jnp.float32)
        out["v_" + name] = (s * s) * _jax.random.uniform(kv, w.shape, _jnp.float32, 0.5, 1.5)
    if N_MICROBATCH > 1:
        for name, axis in PER_EXAMPLE_BATCH_AXIS.items():
            out[name] = _to_microbatches(out[name], axis)
    return {'x': out['x'], 'p': out['p'], 'ln_in_g': out['ln_in_g'], 'ln_in_b': out['ln_in_b'], 'w_in': out['w_in'], 'pool_w': out['pool_w'], 'pool_scale': out['pool_scale'], 'conv_w': out['conv_w'], 'a_log': out['a_log'], 'dt_bias': out['dt_bias'], 'o_norm_w': out['o_norm_w'], 'w_out': out['w_out'], 'ln1_g': out['ln1_g'], 'ln1_b': out['ln1_b'], 'w_up': out['w_up'], 'w_down': out['w_down'], 'ple_gate_w': out['ple_gate_w'], 'ple_proj_w': out['ple_proj_w'], 'ln2_g': out['ln2_g'], 'ln2_b': out['ln2_b'], 'loss_target': out['loss_target'], 'm_ln_in_g': out['m_ln_in_g'], 'm_ln_in_b': out['m_ln_in_b'], 'm_w_in': out['m_w_in'], 'm_pool_w': out['m_pool_w'], 'm_pool_scale': out['m_pool_scale'], 'm_conv_w': out['m_conv_w'], 'm_a_log': out['m_a_log'], 'm_dt_bias': out['m_dt_bias'], 'm_o_norm_w': out['m_o_norm_w'], 'm_w_out': out['m_w_out'], 'm_ln1_g': out['m_ln1_g'], 'm_ln1_b': out['m_ln1_b'], 'm_w_up': out['m_w_up'], 'm_w_down': out['m_w_down'], 'm_ple_gate_w': out['m_ple_gate_w'], 'm_ple_proj_w': out['m_ple_proj_w'], 'm_ln2_g': out['m_ln2_g'], 'm_ln2_b': out['m_ln2_b'], 'v_ln_in_g': out['v_ln_in_g'], 'v_ln_in_b': out['v_ln_in_b'], 'v_w_in': out['v_w_in'], 'v_pool_w': out['v_pool_w'], 'v_pool_scale': out['v_pool_scale'], 'v_conv_w': out['v_conv_w'], 'v_a_log': out['v_a_log'], 'v_dt_bias': out['v_dt_bias'], 'v_o_norm_w': out['v_o_norm_w'], 'v_w_out': out['v_w_out'], 'v_ln1_g': out['v_ln1_g'], 'v_ln1_b': out['v_ln1_b'], 'v_w_up': out['v_w_up'], 'v_w_down': out['v_w_down'], 'v_ple_gate_w': out['v_ple_gate_w'], 'v_ple_proj_w': out['v_ple_proj_w'], 'v_ln2_g': out['v_ln2_g'], 'v_ln2_b': out['v_ln2_b']}


def _loss(weights, diff, rest, loss_target):
    with _jax.named_scope("forward"):
        args = {**rest, TWIN_DIFF_INPUT: diff, **{k: w.astype(_WEIGHT_DTYPES[k]) for k, w in weights.items()}}
        y = _forward(args)
    with _jax.named_scope("loss_head"):
        err = _jnp.square(y.astype(_jnp.float32) - loss_target)
        return 0.5 * _jnp.sum(_jnp.mean(err, axis=-1)) if err.ndim else 0.5 * err


def _adamw(w, g, m, v):
    m = ADAM_B1 * m + (1.0 - ADAM_B1) * g
    v = ADAM_B2 * v + (1.0 - ADAM_B2) * _jnp.square(g)
    m_hat = m / (1.0 - ADAM_B1 ** ADAM_STEP)
    v_hat = v / (1.0 - ADAM_B2 ** ADAM_STEP)
    delta = -ADAM_LR * (m_hat / (_jnp.sqrt(v_hat) + ADAM_EPS) + ADAM_WD * w)
    return delta, m, v


def reference(x, p, ln_in_g, ln_in_b, w_in, pool_w, pool_scale, conv_w, a_log, dt_bias, o_norm_w, w_out, ln1_g, ln1_b, w_up, w_down, ple_gate_w, ple_proj_w, ln2_g, ln2_b, loss_target, m_ln_in_g, m_ln_in_b, m_w_in, m_pool_w, m_pool_scale, m_conv_w, m_a_log, m_dt_bias, m_o_norm_w, m_w_out, m_ln1_g, m_ln1_b, m_w_up, m_w_down, m_ple_gate_w, m_ple_proj_w, m_ln2_g, m_ln2_b, v_ln_in_g, v_ln_in_b, v_w_in, v_pool_w, v_pool_scale, v_conv_w, v_a_log, v_dt_bias, v_o_norm_w, v_w_out, v_ln1_g, v_ln1_b, v_w_up, v_w_down, v_ple_gate_w, v_ple_proj_w, v_ln2_g, v_ln2_b):
    given = dict(x=x, p=p, ln_in_g=ln_in_g, ln_in_b=ln_in_b, w_in=w_in, pool_w=pool_w, pool_scale=pool_scale, conv_w=conv_w, a_log=a_log, dt_bias=dt_bias, o_norm_w=o_norm_w, w_out=w_out, ln1_g=ln1_g, ln1_b=ln1_b, w_up=w_up, w_down=w_down, ple_gate_w=ple_gate_w, ple_proj_w=ple_proj_w, ln2_g=ln2_g, ln2_b=ln2_b, loss_target=loss_target, m_ln_in_g=m_ln_in_g, m_ln_in_b=m_ln_in_b, m_w_in=m_w_in, m_pool_w=m_pool_w, m_pool_scale=m_pool_scale, m_conv_w=m_conv_w, m_a_log=m_a_log, m_dt_bias=m_dt_bias, m_o_norm_w=m_o_norm_w, m_w_out=m_w_out, m_ln1_g=m_ln1_g, m_ln1_b=m_ln1_b, m_w_up=m_w_up, m_w_down=m_w_down, m_ple_gate_w=m_ple_gate_w, m_ple_proj_w=m_ple_proj_w, m_ln2_g=m_ln2_g, m_ln2_b=m_ln2_b, v_ln_in_g=v_ln_in_g, v_ln_in_b=v_ln_in_b, v_w_in=v_w_in, v_pool_w=v_pool_w, v_pool_scale=v_pool_scale, v_conv_w=v_conv_w, v_a_log=v_a_log, v_dt_bias=v_dt_bias, v_o_norm_w=v_o_norm_w, v_w_out=v_w_out, v_ln1_g=v_ln1_g, v_ln1_b=v_ln1_b, v_w_up=v_w_up, v_w_down=v_w_down, v_ple_gate_w=v_ple_gate_w, v_ple_proj_w=v_ple_proj_w, v_ln2_g=v_ln2_g, v_ln2_b=v_ln2_b)
    weights = {n: given[n] for n in TWIN_WEIGHTS}
    shared = {n: given[n] for n in SHARED_INPUTS}
    per_example = {n: given[n] for n in ['x', 'p']}
    grad_fn = _jax.value_and_grad(_loss, argnums=(0, 1))

    def one_microbatch(ex, loss_target):
        ex = dict(ex)
        diff = ex.pop(TWIN_DIFF_INPUT)
        return grad_fn(weights, diff, {**shared, **ex}, loss_target)

    if N_MICROBATCH == 1:
        loss, (grad_w, grad_x) = one_microbatch(per_example, given["loss_target"])
    else:
        def body(carry, xs):
            loss_sum, grad_sum = carry
            l_k, (gw_k, gx_k) = one_microbatch(xs[0], xs[1])
            with _jax.named_scope("update"):
                return (loss_sum + l_k, _jax.tree.map(_jnp.add, grad_sum, gw_k)), gx_k

        init = (_jnp.zeros((), _jnp.float32), _jax.tree.map(_jnp.zeros_like, weights))
        (loss, grad_w), grad_x = _jax.lax.scan(body, init, (per_example, given["loss_target"]))
    with _jax.named_scope("update"):
        delta_w, new_m, new_v = {}, {}, {}
        for n in TWIN_WEIGHTS:
            delta_w[n], new_m[n], new_v[n] = _adamw(weights[n], grad_w[n], given["m_" + n], given["v_" + n])
    return (loss, grad_x, *[grad_w[n] for n in TWIN_WEIGHTS], *[delta_w[n] for n in TWIN_WEIGHTS],
            *[new_m[n] for n in TWIN_WEIGHTS], *[new_v[n] for n in TWIN_WEIGHTS])
```

```python
import jax
import jax.numpy as jnp
from jax import lax
from jax.experimental import pallas as pl
from jax.experimental.pallas import tpu as pltpu

F32 = jnp.float32
MXU_DTYPE = jnp.bfloat16
WIRE_DTYPE = jnp.bfloat16
SDS = jax.ShapeDtypeStruct

D_MODEL = 1024
POOL_WINDOWS = (2, 4, 8, 16)
POOL_WIDTH = 512
POOL_GROUP = 128
POOL_OUT_GROUP = 256
HEADS = 8
HEAD_DIM = 128
DN_WIDTH = HEADS * HEAD_DIM
QKV_WIDTH = 3 * DN_WIDTH
CONV_K = 4
CHUNK = 128
D_FF = 4096
PLE_DIM = 256
LN_EPS = 1e-5
RMS_EPS = 1e-6
L2_EPS = 1e-6
ALPHA = 2.0 ** 0.25
Q_SCALE = HEAD_DIM ** -0.5
IN_WIDTH = 6672
C_POOL, C_QKV, C_Z, C_BETA, C_A, C_GA, C_GB = 0, 512, 3584, 4608, 4616, 4624, 5648
K_QKV, K_Z, K_GA, K_GB, K_U, K_BA, CAT_WIDTH = 0, 3072, 4096, 5120, 6144, 6656, 6912

ADAM_LR, ADAM_B1, ADAM_B2, ADAM_EPS, ADAM_WD, ADAM_STEP = 0.001, 0.9, 0.999, 1e-08, 0.01, 10

N_CHIPS = 4
N_DEV = 8
VMEM_LIMIT = 56 * 1024 * 1024

PACK = (("w_in", 1668), ("pool_w", 32), ("w_out", 256), ("w_up", 1024), ("w_down", 1024),
        ("ple_gate_w", 256), ("ple_proj_w", 64), ("conv_w", 3))
PACK_ROWS = 4352
PACK_TILE = 256
SMALL_ROWS = 16
SMALL_NAMES = ("ln_in_g", "ln_in_b", "pool_scale", "ln1_g", "ln1_b", "ln2_g", "ln2_b", "o_norm_w", "a_log", "dt_bias")


def _mx(a):
    return a.astype(MXU_DTYPE)


def _dot(a, b):
    return lax.dot_general(_mx(a), _mx(b), (((1,), (0,)), ((), ())), preferred_element_type=F32)


def _dot_nt(a, b):
    return lax.dot_general(_mx(a), _mx(b), (((1,), (1,)), ((), ())), preferred_element_type=F32)


def _dot_tn(a, b):
    return lax.dot_general(_mx(a), _mx(b), (((0,), (0,)), ((), ())), preferred_element_type=F32)


def _sigmoid(x):
    return 1.0 / (1.0 + jnp.exp(-x))


def _softplus(x):
    return jnp.maximum(x, 0.0) + jnp.log(1.0 + jnp.exp(-jnp.abs(x)))


def _pc(body, name, grid, in_specs, out_specs, out_shape, scratch=(), sem=None):
    return pl.pallas_call(
        body, out_shape=out_shape, grid=grid, in_specs=in_specs, out_specs=out_specs,
        scratch_shapes=scratch, name=name,
        compiler_params=pltpu.CompilerParams(dimension_semantics=sem, vmem_limit_bytes=VMEM_LIMIT))


def _row(tm, n):
    return pl.BlockSpec((tm, n), lambda i: (i, 0))


def _const(shape):
    nd = len(shape)
    return pl.BlockSpec(shape, lambda *_: (0,) * nd)


def _matmul(a, b, mode, name, out_dtype=F32, tm=512, tn=512, tk=512):
    if mode == "nn":
        (m, k), n = a.shape, b.shape[1]
    elif mode == "nt":
        (m, k), n = a.shape, b.shape[0]
    else:
        (k, m), n = a.shape, b.shape[1]
    tm, tn, tk = min(tm, m), min(tn, n), min(tk, k)
    assert m % tm == 0 and n % tn == 0 and k % tk == 0, (name, m, n, k, tm, tn, tk)
    nk = k // tk
    if mode == "nn":
        a_spec = pl.BlockSpec((tm, tk), lambda i, j, kk: (i, kk))
        b_spec = pl.BlockSpec((tk, tn), lambda i, j, kk: (kk, j))
        dot = _dot
    elif mode == "nt":
        a_spec = pl.BlockSpec((tm, tk), lambda i, j, kk: (i, kk))
        b_spec = pl.BlockSpec((tn, tk), lambda i, j, kk: (j, kk))
        dot = _dot_nt
    else:
        a_spec = pl.BlockSpec((tk, tm), lambda i, j, kk: (kk, i))
        b_spec = pl.BlockSpec((tk, tn), lambda i, j, kk: (kk, j))
        dot = _dot_tn

    def body(a_ref, b_ref, o_ref, acc_ref):
        kk = pl.program_id(2)

        @pl.when(kk == 0)
        def _():
            acc_ref[...] = jnp.zeros_like(acc_ref)

        acc_ref[...] += dot(a_ref[...], b_ref[...])

        @pl.when(kk == nk - 1)
        def _():
            o_ref[...] = acc_ref[...].astype(out_dtype)

    return _pc(body, name, (m // tm, n // tn, nk), [a_spec, b_spec],
               pl.BlockSpec((tm, tn), lambda i, j, kk: (i, j)), SDS((m, n), out_dtype),
               scratch=[pltpu.VMEM((tm, tn), F32)], sem=("parallel", "parallel", "arbitrary"))(a, b)


def _ln_stats(x):
    mu = jnp.mean(x, axis=-1, keepdims=True)
    xc = x - mu
    var = jnp.mean(xc * xc, axis=-1, keepdims=True)
    rstd = lax.rsqrt(var + LN_EPS)
    return xc * rstd, rstd


def _ln_bwd(dy, xhat, rstd, g):
    dxh = dy * g
    m1 = jnp.mean(dxh, axis=-1, keepdims=True)
    m2 = jnp.mean(dxh * xhat, axis=-1, keepdims=True)
    return rstd * (dxh - m1 - xhat * m2)


def _ln_in(x, g, b, tm):
    t, d = x.shape

    def body(x_ref, g_ref, b_ref, h_ref, hb_ref):
        xhat, _ = _ln_stats(x_ref[...])
        h = xhat * g_ref[...] + b_ref[...]
        h_ref[...] = h
        hb_ref[...] = _mx(h)

    return _pc(body, "ln_in", (t // tm,), [_row(tm, d), _const((1, d)), _const((1, d))],
               [_row(tm, d), _row(tm, d)], [SDS((t, d), F32), SDS((t, d), MXU_DTYPE)], sem=("parallel",))(x, g, b)


def _pool_fwd(proj, pool_w, tm):
    t = proj.shape[0]
    ublk = K_U // POOL_WIDTH

    def body(u_ref, halo_ref, pw_ref, ypre_ref, d_ref, ext_ref):
        i = pl.program_id(0)
        ext_ref[0:16, :] = jnp.where(i > 0, halo_ref[...], 0.0)
        ext_ref[16:16 + tm, :] = u_ref[...]
        tok = i * tm + lax.broadcasted_iota(jnp.int32, (tm, POOL_GROUP), 0)
        for gi, w in enumerate(POOL_WINDOWS):
            cs = pl.ds(gi * POOL_GROUP, POOL_GROUP)
            ug = ext_ref[pl.ds(16, tm), cs]
            s = ug
            for k in range(1, w):
                s = s + ext_ref[pl.ds(16 - k, tm), cs]
            cnt = jnp.minimum(tok + 1, w).astype(F32)
            db = _mx(s / cnt - ug)
            d_ref[:, gi * POOL_GROUP:(gi + 1) * POOL_GROUP] = db
            ypre_ref[:, gi * POOL_OUT_GROUP:(gi + 1) * POOL_OUT_GROUP] = _dot(db, pw_ref[gi])

    halo = pl.BlockSpec((16, POOL_WIDTH), lambda i: (jnp.maximum(i * (tm // 16) - 1, 0), ublk))
    return _pc(body, "pool_fwd", (t // tm,),
               [pl.BlockSpec((tm, POOL_WIDTH), lambda i: (i, ublk)), halo, _const((4, POOL_GROUP, POOL_OUT_GROUP))],
               [_row(tm, D_MODEL), _row(tm, POOL_WIDTH)],
               [SDS((t, D_MODEL), F32), SDS((t, POOL_WIDTH), MXU_DTYPE)],
               scratch=[pltpu.VMEM((16 + tm, POOL_WIDTH), F32)], sem=("parallel",))(proj, proj, pool_w)


def _pool_bwd(dyp, d_bf, pool_w, tm):
    t = dyp.shape[0]
    n = t // tm

    def body(dy_ref, dyn_ref, d_ref, pw_ref, du_ref, dpw_ref, ext_ref):
        i = pl.program_id(0)

        @pl.when(i == 0)
        def _():
            dpw_ref[...] = jnp.zeros_like(dpw_ref)

        tok = i * tm + lax.broadcasted_iota(jnp.int32, (tm + 16, POOL_GROUP), 0)
        for gi, w in enumerate(POOL_WINDOWS):
            dy = dy_ref[:, gi * POOL_OUT_GROUP:(gi + 1) * POOL_OUT_GROUP]
            dyn = dyn_ref[:, gi * POOL_OUT_GROUP:(gi + 1) * POOL_OUT_GROUP]
            pw = pw_ref[gi]
            dd = _dot_nt(dy, pw)
            ddn = jnp.where(i < n - 1, _dot_nt(dyn, pw), 0.0)
            cnt = jnp.minimum(tok + 1, w).astype(F32)
            ext_ref[0:tm, :] = dd / cnt[0:tm]
            ext_ref[tm:tm + 16, :] = ddn / cnt[tm:tm + 16]
            s = ext_ref[pl.ds(0, tm), :]
            for k in range(1, w):
                s = s + ext_ref[pl.ds(k, tm), :]
            du_ref[:, gi * POOL_GROUP:(gi + 1) * POOL_GROUP] = _mx(s - dd)
            dpw_ref[gi] += _dot_tn(d_ref[:, gi * POOL_GROUP:(gi + 1) * POOL_GROUP], dy)

    nxt = pl.BlockSpec((16, D_MODEL), lambda i: (jnp.minimum((i + 1) * (tm // 16), t // 16 - 1), 0))
    return _pc(body, "pool_bwd", (n,),
               [_row(tm, D_MODEL), nxt, _row(tm, POOL_WIDTH), _const((4, POOL_GROUP, POOL_OUT_GROUP))],
               [_row(tm, POOL_WIDTH), _const((4, POOL_GROUP, POOL_OUT_GROUP))],
               [SDS((t, POOL_WIDTH), MXU_DTYPE), SDS((4, POOL_GROUP, POOL_OUT_GROUP), F32)],
               scratch=[pltpu.VMEM((tm + 16, POOL_GROUP), F32)], sem=("arbitrary",))(dyp, dyp, d_bf, pool_w)


CONV_BLK = 512


def _conv_fwd(proj, conv_w, tm):
    t = proj.shape[0]

    def body(x_ref, halo_ref, w_ref, o_ref, ext_ref):
        i = pl.program_id(0)
        ext_ref[0:8, :] = jnp.where(i > 0, halo_ref[...], 0.0)
        ext_ref[8:8 + tm, :] = x_ref[...]
        y = w_ref[pl.ds(0, 1), :] * ext_ref[pl.ds(5, tm), :]
        for k in range(1, CONV_K):
            y = y + w_ref[pl.ds(k, 1), :] * ext_ref[pl.ds(5 + k, tm), :]
        o_ref[...] = y * _sigmoid(y)

    halo = pl.BlockSpec((8, CONV_BLK), lambda i, j: (jnp.maximum(i * (tm // 8) - 1, 0), j))
    blk = pl.BlockSpec((tm, CONV_BLK), lambda i, j: (i, j))
    return _pc(body, "conv_fwd", (t // tm, QKV_WIDTH // CONV_BLK),
               [blk, halo, pl.BlockSpec((CONV_K, CONV_BLK), lambda i, j: (0, j))], blk,
               SDS((t, QKV_WIDTH), F32), scratch=[pltpu.VMEM((8 + tm, CONV_BLK), F32)],
               sem=("parallel", "parallel"))(proj, proj, conv_w)


def _conv_bwd(dact, proj, conv_w, tm):
    t = proj.shape[0]
    n = t // tm

    def body(da_ref, dan_ref, x_ref, xp_ref, xn_ref, w_ref, dx_ref, dw_ref, ext_ref, dy_ref):
        i = pl.program_id(1)

        @pl.when(i == 0)
        def _():
            dw_ref[...] = jnp.zeros_like(dw_ref)

        ext_ref[0:8, :] = jnp.where(i > 0, xp_ref[...], 0.0)
        ext_ref[8:8 + tm, :] = x_ref[...]
        ext_ref[8 + tm:16 + tm, :] = jnp.where(i < n - 1, xn_ref[...], 0.0)
        y = w_ref[pl.ds(0, 1), :] * ext_ref[pl.ds(5, tm + 8), :]
        for k in range(1, CONV_K):
            y = y + w_ref[pl.ds(k, 1), :] * ext_ref[pl.ds(5 + k, tm + 8), :]
        s = _sigmoid(y)
        dsilu = s * (1.0 + y * (1.0 - s))
        dy_ref[0:tm, :] = da_ref[...] * dsilu[0:tm]
        dy_ref[tm:tm + 8, :] = jnp.where(i < n - 1, dan_ref[...], 0.0) * dsilu[tm:tm + 8]
        dx = w_ref[pl.ds(0, 1), :] * dy_ref[pl.ds(3, tm), :]
        for k in range(1, CONV_K):
            dx = dx + w_ref[pl.ds(k, 1), :] * dy_ref[pl.ds(3 - k, tm), :]
        dx_ref[...] = _mx(dx)
        dy = dy_ref[pl.ds(0, tm), :]
        for k in range(CONV_K):
            dw_ref[pl.ds(k, 1), :] += jnp.sum(dy * ext_ref[pl.ds(5 + k, tm), :], axis=0, keepdims=True)

    blk = pl.BlockSpec((tm, CONV_BLK), lambda j, i: (i, j))
    prev = pl.BlockSpec((8, CONV_BLK), lambda j, i: (jnp.maximum(i * (tm // 8) - 1, 0), j))
    nxt = pl.BlockSpec((8, CONV_BLK), lambda j, i: (jnp.minimum((i + 1) * (tm // 8), t // 8 - 1), j))
    wspec = pl.BlockSpec((CONV_K, CONV_BLK), lambda j, i: (0, j))
    return _pc(body, "conv_bwd", (QKV_WIDTH // CONV_BLK, n),
               [blk, nxt, blk, prev, nxt, wspec],
               [blk, pl.BlockSpec((8, CONV_BLK), lambda j, i: (0, j))],
               [SDS((t, QKV_WIDTH), MXU_DTYPE), SDS((8, QKV_WIDTH), F32)],
               scratch=[pltpu.VMEM((16 + tm, CONV_BLK), F32), pltpu.VMEM((8 + tm, CONV_BLK), F32)],
               sem=("parallel", "arbitrary"))(dact, dact, proj, proj, proj, conv_w)


def _lane(shape):
    return lax.broadcasted_iota(jnp.int32, shape, 1)


def _ba_fwd(proj, al_row, dtb_row, tm):
    t = proj.shape[0]
    bablk = K_BA // 128

    def body(ba_ref, al_ref, dtb_ref, bg_ref):
        ba = ba_ref[...]
        lane = _lane(ba.shape)
        g = -jnp.exp(al_ref[...]) * _softplus(ba + dtb_ref[...])
        bg_ref[...] = jnp.where(lane < HEADS, _sigmoid(ba), jnp.where(lane < 2 * HEADS, g, 0.0))

    return _pc(body, "ba_fwd", (t // tm,),
               [pl.BlockSpec((tm, 128), lambda i: (i, bablk)), _const((1, 128)), _const((1, 128))],
               _row(tm, 128), SDS((t, 128), F32), sem=("parallel",))(proj, al_row, dtb_row)


def _ba_bwd(dbg, bg, proj, al_row, dtb_row, tm):
    t = proj.shape[0]
    bablk = K_BA // 128

    def body(dbg_ref, bg_ref, ba_ref, al_ref, dtb_ref, dba_ref, acc_ref):
        i = pl.program_id(0)

        @pl.when(i == 0)
        def _():
            acc_ref[...] = jnp.zeros_like(acc_ref)

        dbg_v, bg_v, ba = dbg_ref[...], bg_ref[...], ba_ref[...]
        lane = _lane(ba.shape)
        is_g = (lane >= HEADS) & (lane < 2 * HEADS)
        dbeta_raw = dbg_v * bg_v * (1.0 - bg_v)
        da_raw = dbg_v * (-jnp.exp(al_ref[...])) * _sigmoid(ba + dtb_ref[...])
        dba_ref[...] = _mx(jnp.where(lane < HEADS, dbeta_raw, jnp.where(is_g, da_raw, 0.0)))
        acc_ref[0:1, :] += jnp.sum(jnp.where(is_g, dbg_v * bg_v, 0.0), axis=0, keepdims=True)
        acc_ref[1:2, :] += jnp.sum(jnp.where(is_g, da_raw, 0.0), axis=0, keepdims=True)

    return _pc(body, "ba_bwd", (t // tm,),
               [_row(tm, 128), _row(tm, 128), pl.BlockSpec((tm, 128), lambda i: (i, bablk)),
                _const((1, 128)), _const((1, 128))],
               [_row(tm, 128), _const((8, 128))], [SDS((t, 128), MXU_DTYPE), SDS((8, 128), F32)],
               sem=("arbitrary",))(dbg, bg, proj, al_row, dtb_row)


def _chunk_terms(q, k, bgv, g_row, h):
    c = CHUNK
    ii = lax.broadcasted_iota(jnp.int32, (c, c), 0)
    jj = lax.broadcasted_iota(jnp.int32, (c, c), 1)
    lane = _lane(bgv.shape)
    beta = jnp.sum(jnp.where(lane == h, bgv, 0.0), axis=1, keepdims=True)
    g_col = jnp.sum(jnp.where(lane == HEADS + h, bgv, 0.0), axis=1, keepdims=True)
    rq = lax.rsqrt(jnp.sum(q * q, axis=1, keepdims=True) + L2_EPS)
    rk = lax.rsqrt(jnp.sum(k * k, axis=1, keepdims=True) + L2_EPS)
    yq = q * rq
    kn = k * rk
    qn = yq * Q_SCALE
    gc_col = jnp.sum(jnp.where(jj <= ii, g_row, 0.0), axis=1, keepdims=True)
    gc_row = jnp.sum(jnp.where(ii <= jj, g_col, 0.0), axis=0, keepdims=True)
    incl = ii >= jj
    dm = jnp.where(incl, jnp.exp(jnp.where(incl, gc_col - gc_row, 0.0)), 0.0)
    gl = jnp.sum(g_row, axis=1, keepdims=True)
    eg = jnp.exp(gc_col)
    ek = jnp.exp(gl - gc_col)
    egl = jnp.exp(gl)
    kb = kn * beta
    kk = _dot_nt(kb, kn)
    qk = _dot_nt(qn, kn)
    m = jnp.where(ii > jj, kk * dm, 0.0)
    attn = qk * dm
    return dict(ii=ii, jj=jj, beta=beta, rq=rq, rk=rk, yq=yq, kn=kn, qn=qn, dm=dm, eg=eg, ek=ek,
                egl=egl, kb=kb, kk=kk, qk=qk, m=m, attn=attn)


def _unit_lower_inverse_minus_identity(m, ii, jj):
    y = -jnp.where((ii >> 1) == (jj >> 1), m, 0.0)
    s = 1
    while (1 << s) < CHUNK:
        lb = jnp.where(((ii >> (s + 1)) == (jj >> (s + 1))) & ((ii >> s) != (jj >> s)), m, 0.0)
        z = lb + _dot(y, lb)
        y = y - z - _dot(z, y)
        s += 1
    return y


def _dn_local_fwd(qkv_act, bg, bgt):
    t = qkv_act.shape[0]
    nt = t // CHUNK
    c = CHUNK

    def body(qkv_ref, bg_ref, bgt_ref, u_ref, w_ref, qg_ref, kg_ref, attn_ref, y_ref, egl_ref):
        bgv = bg_ref[...]

        def head(h, carry):
            off = pl.multiple_of(h * HEAD_DIM, HEAD_DIM)
            q = qkv_ref[:, pl.ds(off, HEAD_DIM)]
            k = qkv_ref[:, pl.ds(pl.multiple_of(DN_WIDTH + h * HEAD_DIM, HEAD_DIM), HEAD_DIM)]
            v = qkv_ref[:, pl.ds(pl.multiple_of(2 * DN_WIDTH + h * HEAD_DIM, HEAD_DIM), HEAD_DIM)]
            g_row = bgt_ref[pl.ds(HEADS + h, 1), :]
            ct = _chunk_terms(q, k, bgv, g_row, h)
            y = _unit_lower_inverse_minus_identity(ct["m"], ct["ii"], ct["jj"])
            vb = v * ct["beta"]
            kbe = ct["kb"] * ct["eg"]
            dst = pl.ds(off, HEAD_DIM)
            u_ref[:, dst] = vb + _dot(y, vb)
            w_ref[:, dst] = _mx(kbe + _dot(y, kbe))
            qg_ref[:, dst] = _mx(ct["qn"] * ct["eg"])
            kg_ref[:, dst] = _mx(ct["kn"] * ct["ek"])
            attn_ref[:, dst] = _mx(ct["attn"])
            y_ref[:, dst] = _mx(y)
            egl_ref[0, pl.ds(h, 1), :] = jnp.broadcast_to(ct["egl"], (1, HEAD_DIM))
            return carry

        lax.fori_loop(0, HEADS, head, 0)

    wide = _row(c, DN_WIDTH)
    return _pc(body, "dn_local_fwd", (nt,),
               [_row(c, QKV_WIDTH), _row(c, 128), pl.BlockSpec((2 * HEADS, c), lambda i: (0, i))],
               [wide, wide, wide, wide, wide, wide, pl.BlockSpec((1, HEADS, HEAD_DIM), lambda i: (i, 0, 0))],
               [SDS((t, DN_WIDTH), F32)] + [SDS((t, DN_WIDTH), MXU_DTYPE)] * 5 + [SDS((nt, HEADS, HEAD_DIM), F32)],
               sem=("parallel",))(qkv_act, bg, bgt)


def _dn_scan_fwd(u, w, qg, kg, attn, egl):
    t = u.shape[0]
    nt = t // CHUNK
    c = CHUNK

    def body(u_ref, w_ref, qg_ref, kg_ref, attn_ref, egl_ref, o_ref, vn_ref, st_ref, s_ref):
        @pl.when(pl.program_id(0) == 0)
        def _():
            s_ref[...] = jnp.zeros_like(s_ref)

        for h in range(HEADS):
            sl = slice(h * HEAD_DIM, (h + 1) * HEAD_DIM)
            s = s_ref[h]
            st_ref[0, h] = s
            sb = _mx(s)
            vn = u_ref[:, sl] - _dot(w_ref[:, sl], sb)
            vnb = _mx(vn)
            vn_ref[:, sl] = vnb
            o_ref[:, sl] = _dot(qg_ref[:, sl], sb) + _dot(attn_ref[:, sl], vnb)
            s_ref[h] = s * egl_ref[0, h:h + 1, :] + _dot_tn(kg_ref[:, sl], vnb)

    wide = _row(c, DN_WIDTH)
    return _pc(body, "dn_scan_fwd", (nt,),
               [wide] * 5 + [pl.BlockSpec((1, HEADS, HEAD_DIM), lambda i: (i, 0, 0))],
               [wide, wide, pl.BlockSpec((1, HEADS, HEAD_DIM, HEAD_DIM), lambda i: (i, 0, 0, 0))],
               [SDS((t, DN_WIDTH), F32), SDS((t, DN_WIDTH), MXU_DTYPE), SDS((nt, HEADS, HEAD_DIM, HEAD_DIM), F32)],
               scratch=[pltpu.VMEM((HEADS, HEAD_DIM, HEAD_DIM), F32)], sem=("arbitrary",))(u, w, qg, kg, attn, egl)


def _dn_scan_bwd(do, qg, kg, w, attn, vn, states, egl):
    t = do.shape[0]
    nt = t // CHUNK
    c = CHUNK

    def body(do_ref, qg_ref, kg_ref, w_ref, attn_ref, vn_ref, st_ref, egl_ref,
             dvn_ref, dkg_ref, dqg_ref, dattn_ref, dw_ref, degl_ref, ds_ref):
        @pl.when(pl.program_id(0) == 0)
        def _():
            ds_ref[...] = jnp.zeros_like(ds_ref)

        for h in range(HEADS):
            sl = slice(h * HEAD_DIM, (h + 1) * HEAD_DIM)
            dsp = ds_ref[h]
            dsb = _mx(dsp)
            s = st_ref[0, h]
            sb = _mx(s)
            dob = do_ref[:, sl]
            vnb = vn_ref[:, sl]
            dvn = _dot(kg_ref[:, sl], dsb) + _dot_tn(attn_ref[:, sl], dob)
            dvnb = _mx(dvn)
            dvn_ref[:, sl] = dvn
            dkg_ref[:, sl] = _dot_nt(vnb, dsb)
            dqg_ref[:, sl] = _dot_nt(dob, sb)
            dattn_ref[:, sl] = _dot_nt(dob, vnb)
            dw_ref[:, sl] = -_dot_nt(dvnb, sb)
            degl = jnp.sum(jnp.sum(s * dsp, axis=1, keepdims=True), axis=0, keepdims=True)
            degl_ref[0, h:h + 1, :] = jnp.broadcast_to(degl, (1, HEAD_DIM))
            ds_ref[h] = dsp * egl_ref[0, h:h + 1, :] + _dot_tn(qg_ref[:, sl], dob) - _dot_tn(w_ref[:, sl], dvnb)

    rev = pl.BlockSpec((c, DN_WIDTH), lambda i: (nt - 1 - i, 0))
    rev3 = pl.BlockSpec((1, HEADS, HEAD_DIM), lambda i: (nt - 1 - i, 0, 0))
    rev4 = pl.BlockSpec((1, HEADS, HEAD_DIM, HEAD_DIM), lambda i: (nt - 1 - i, 0, 0, 0))
    return _pc(body, "dn_scan_bwd", (nt,), [rev] * 6 + [rev4, rev3], [rev] * 5 + [rev3],
               [SDS((t, DN_WIDTH), F32)] * 5 + [SDS((nt, HEADS, HEAD_DIM), F32)],
               scratch=[pltpu.VMEM((HEADS, HEAD_DIM, HEAD_DIM), F32)],
               sem=("arbitrary",))(do, qg, kg, w, attn, vn, states, egl)


def _dn_local_bwd(qkv_act, bg, bgt, u, w, ymat, dvn, dw, dqg, dkg, dattn, degl):
    t = qkv_act.shape[0]
    nt = t // CHUNK
    c = CHUNK

    def body(qkv_ref, bg_ref, bgt_ref, u_ref, w_ref, y_ref, du_ref, dw_ref, dqg_ref, dkg_ref, dattn_ref,
             degl_ref, dqkv_ref, dbg_ref):
        bgv = bg_ref[...]
        lane = _lane(bgv.shape)
        rowi = lax.broadcasted_iota(jnp.int32, (c, 1), 0)

        def head(h, dbg):
            off = pl.multiple_of(h * HEAD_DIM, HEAD_DIM)
            koff = pl.multiple_of(DN_WIDTH + h * HEAD_DIM, HEAD_DIM)
            voff = pl.multiple_of(2 * DN_WIDTH + h * HEAD_DIM, HEAD_DIM)
            src = pl.ds(off, HEAD_DIM)
            q = qkv_ref[:, src]
            k = qkv_ref[:, pl.ds(koff, HEAD_DIM)]
            v = qkv_ref[:, pl.ds(voff, HEAD_DIM)]
            g_row = bgt_ref[pl.ds(HEADS + h, 1), :]
            ct = _chunk_terms(q, k, bgv, g_row, h)
            ii, jj = ct["ii"], ct["jj"]
            beta, eg, ek, kb, kn, qn, dm = ct["beta"], ct["eg"], ct["ek"], ct["kb"], ct["kn"], ct["qn"], ct["dm"]
            y = y_ref[:, src]
            du = du_ref[:, src]
            dwv = dw_ref[:, src]
            dqg_v = dqg_ref[:, src]
            dkg_v = dkg_ref[:, src]
            dattn_v = dattn_ref[:, src]
            degl_v = jnp.max(degl_ref[0, pl.ds(h, 1), :], axis=1, keepdims=True)
            dvb = du + _dot_tn(y, du)
            dkbe = dwv + _dot_tn(y, dwv)
            dms = jnp.where(ii > jj, -(_dot_nt(dvb, u_ref[:, src]) + _dot_nt(dkbe, w_ref[:, src])), 0.0)
            dkk = dms * dm
            gmat = dms * ct["m"] + dattn_v * ct["attn"]
            dqk = dattn_v * dm
            dkb = _dot(dkk, kn) + dkbe * eg
            dk = _dot_tn(dkk, kb) + _dot_tn(dqk, qn) + dkg_v * ek
            dq = _dot(dqk, kn) + dqg_v * eg
            deg = jnp.sum(dqg_v * qn, axis=1, keepdims=True) + jnp.sum(dkbe * kb, axis=1, keepdims=True)
            dek = jnp.sum(dkg_v * kn, axis=1, keepdims=True)
            dgl = jnp.sum(dek * ek, axis=0, keepdims=True) + degl_v * ct["egl"]
            cs_row = jnp.sum(gmat, axis=0, keepdims=True)
            cs_col = jnp.sum(jnp.where(ii == jj, cs_row, 0.0), axis=1, keepdims=True)
            dgc = (deg * eg - dek * ek + jnp.sum(gmat, axis=1, keepdims=True) - cs_col
                   + jnp.where(rowi == c - 1, dgl, 0.0))
            dgc_row = jnp.sum(jnp.where(ii == jj, dgc, 0.0), axis=0, keepdims=True)
            dg = jnp.sum(jnp.where(jj >= ii, dgc_row, 0.0), axis=1, keepdims=True)
            dbeta = jnp.sum(dkb * kn, axis=1, keepdims=True) + jnp.sum(dvb * v, axis=1, keepdims=True)
            dk = dk + dkb * beta
            dv = dvb * beta
            dyq = dq * Q_SCALE
            yq = ct["yq"]
            dqkv_ref[:, src] = ct["rq"] * (dyq - yq * jnp.sum(yq * dyq, axis=1, keepdims=True))
            dqkv_ref[:, pl.ds(koff, HEAD_DIM)] = ct["rk"] * (dk - kn * jnp.sum(kn * dk, axis=1, keepdims=True))
            dqkv_ref[:, pl.ds(voff, HEAD_DIM)] = dv
            return dbg + jnp.where(lane == h, dbeta, 0.0) + jnp.where(lane == HEADS + h, dg, 0.0)

        dbg_ref[...] = lax.fori_loop(0, HEADS, head, jnp.zeros((c, 128), F32))

    wide = _row(c, DN_WIDTH)
    sc3 = pl.BlockSpec((1, HEADS, HEAD_DIM), lambda i: (i, 0, 0))
    return _pc(body, "dn_local_bwd", (nt,),
               [_row(c, QKV_WIDTH), _row(c, 128), pl.BlockSpec((2 * HEADS, c), lambda i: (0, i))] + [wide] * 8 + [sc3],
               [_row(c, QKV_WIDTH), _row(c, 128)], [SDS((t, QKV_WIDTH), F32), SDS((t, 128), F32)],
               sem=("parallel",))(qkv_act, bg, bgt, u, w, ymat, dvn, dw, dqg, dkg, dattn, degl)


def _mix_fwd(o, proj, ypre, pool_scale, wo_row, tm):
    t = o.shape[0]

    def body(o_ref, z_ref, ga_ref, gb_ref, yp_ref, ps_ref, wo_ref, mixed_ref):
        for h in range(HEADS):
            sl = slice(h * HEAD_DIM, (h + 1) * HEAD_DIM)
            oh = o_ref[:, sl]
            on = oh * lax.rsqrt(jnp.mean(oh * oh, axis=1, keepdims=True) + RMS_EPS)
            zh = z_ref[:, sl]
            yb = on * wo_ref[:, sl] * (zh * _sigmoid(zh))
            ya = yp_ref[:, sl] * ps_ref[:, sl]
            mixed_ref[:, sl] = _mx(_sigmoid(ga_ref[:, sl]) * ya + _sigmoid(gb_ref[:, sl]) * yb)

    def col(blk):
        return pl.BlockSpec((tm, D_MODEL), lambda i: (i, blk))

    return _pc(body, "mix_fwd", (t // tm,),
               [_row(tm, D_MODEL), col(K_Z // D_MODEL), col(K_GA // D_MODEL), col(K_GB // D_MODEL), _row(tm, D_MODEL),
                _const((1, D_MODEL)), _const((1, D_MODEL))],
               _row(tm, D_MODEL), SDS((t, D_MODEL), MXU_DTYPE), sem=("parallel",))(
                   o, proj, proj, proj, ypre, pool_scale, wo_row)


def _mix_bwd(da1_bf, w_out, o, proj, ypre, pool_scale, wo_row, tm):
    t = o.shape[0]

    def body(da_ref, wout_ref, o_ref, z_ref, ga_ref, gb_ref, yp_ref, ps_ref, wo_ref,
             do_ref, dz_ref, dga_ref, dgb_ref, dyp_ref, acc_ref):
        i = pl.program_id(0)

        @pl.when(i == 0)
        def _():
            acc_ref[...] = jnp.zeros_like(acc_ref)

        dmixed = _dot_nt(da_ref[...], wout_ref[...])
        for h in range(HEADS):
            sl = slice(h * HEAD_DIM, (h + 1) * HEAD_DIM)
            oh = o_ref[:, sl]
            rs = lax.rsqrt(jnp.mean(oh * oh, axis=1, keepdims=True) + RMS_EPS)
            on = oh * rs
            zh = z_ref[:, sl]
            sz = _sigmoid(zh)
            silu = zh * sz
            woh = wo_ref[:, sl]
            t1 = on * woh
            yb = t1 * silu
            sa = _sigmoid(ga_ref[:, sl])
            sb = _sigmoid(gb_ref[:, sl])
            yp = yp_ref[:, sl]
            psh = ps_ref[:, sl]
            dm = dmixed[:, sl]
            dga_ref[:, sl] = _mx(dm * (yp * psh) * sa * (1.0 - sa))
            dgb_ref[:, sl] = _mx(dm * yb * sb * (1.0 - sb))
            dya = dm * sa
            dyb = dm * sb
            dyp_ref[:, sl] = _mx(dya * psh)
            acc_ref[0:1, sl] += jnp.sum(dya * yp, axis=0, keepdims=True)
            dz_ref[:, sl] = _mx(dyb * t1 * (sz * (1.0 + zh * (1.0 - sz))))
            dt1 = dyb * silu
            acc_ref[1:2, 0:HEAD_DIM] += jnp.sum(dt1 * on, axis=0, keepdims=True)
            don = dt1 * woh
            do_ref[:, sl] = _mx(rs * (don - on * jnp.mean(don * on, axis=1, keepdims=True)))

    def col(blk):
        return pl.BlockSpec((tm, D_MODEL), lambda i: (i, blk))

    r = _row(tm, D_MODEL)
    return _pc(body, "mix_bwd", (t // tm,),
               [r, _const((D_MODEL, D_MODEL)), r, col(K_Z // D_MODEL), col(K_GA // D_MODEL), col(K_GB // D_MODEL), r,
                _const((1, D_MODEL)), _const((1, D_MODEL))],
               [r, r, r, r, r, _const((8, D_MODEL))],
               [SDS((t, D_MODEL), MXU_DTYPE)] * 5 + [SDS((8, D_MODEL), F32)],
               sem=("arbitrary",))(da1_bf, w_out, o, proj, proj, proj, ypre, pool_scale, wo_row)


def _oproj_ln1(mixed, w_out, h0, g1, b1, tm):
    t = mixed.shape[0]

    def body(m_ref, w_ref, h0_ref, g_ref, b_ref, a1_ref, h1_ref, h1b_ref):
        a1 = ALPHA * h0_ref[...] + _dot(m_ref[...], w_ref[...])
        a1_ref[...] = a1
        xhat, _ = _ln_stats(a1)
        h1 = xhat * g_ref[...] + b_ref[...]
        h1_ref[...] = h1
        h1b_ref[...] = _mx(h1)

    r = _row(tm, D_MODEL)
    v = _const((1, D_MODEL))
    return _pc(body, "oproj_ln1", (t // tm,), [r, _const((D_MODEL, D_MODEL)), r, v, v], [r, r, r],
               [SDS((t, D_MODEL), F32), SDS((t, D_MODEL), F32), SDS((t, D_MODEL), MXU_DTYPE)],
               sem=("parallel",))(mixed, w_out, h0, g1, b1)


def _mlp_up(h1_bf, w_up, tm, tn):
    t = h1_bf.shape[0]

    def body(h_ref, w_ref, up_ref, act_ref):
        up = _dot(h_ref[...], w_ref[...])
        up_ref[...] = up
        r = jnp.maximum(up, 0.0)
        act_ref[...] = _mx(r * r)

    o = pl.BlockSpec((tm, tn), lambda i, j: (i, j))
    return _pc(body, "mlp_up", (t // tm, D_FF // tn),
               [pl.BlockSpec((tm, D_MODEL), lambda i, j: (i, 0)), pl.BlockSpec((D_MODEL, tn), lambda i, j: (0, j))],
               [o, o], [SDS((t, D_FF), F32), SDS((t, D_FF), MXU_DTYPE)], sem=("parallel", "parallel"))(h1_bf, w_up)


def _tail(act, w_down, h1, w_gate, p_bf, w_proj, tgt, g2, b2, tm):
    t = act.shape[0]

    def body(act_ref, wd_ref, h1_ref, wg_ref, p_ref, wp_ref, tgt_ref, g_ref, b_ref,
             dr_ref, drb_ref, dgp_ref, dpp_ref, rb_ref, acc_ref):
        i = pl.program_id(0)

        @pl.when(i == 0)
        def _():
            acc_ref[...] = jnp.zeros_like(acc_ref)

        r = ALPHA * h1_ref[...] + _dot(act_ref[...], wd_ref[...])
        rb = _mx(r)
        rb_ref[...] = rb
        gate = _sigmoid(_dot(rb, wg_ref[...]))
        pp = _dot(p_ref[...], wp_ref[...])
        xhat, rstd = _ln_stats(r + gate * pp)
        g = g_ref[...]
        diff = xhat * g + b_ref[...] - tgt_ref[...]
        dh2 = diff * (1.0 / D_MODEL)
        rowloss = jnp.sum(diff * diff, axis=1, keepdims=True) * (0.5 / D_MODEL)
        acc_ref[0:1, :] += jnp.sum(dh2 * xhat, axis=0, keepdims=True)
        acc_ref[1:2, :] += jnp.sum(dh2, axis=0, keepdims=True)
        acc_ref[2:3, :] += jnp.broadcast_to(jnp.sum(rowloss, axis=0, keepdims=True), (1, D_MODEL))
        da2 = _ln_bwd(dh2, xhat, rstd, g)
        dpp_ref[...] = _mx(da2 * gate)
        dgp = _mx(da2 * pp * gate * (1.0 - gate))
        dgp_ref[...] = dgp
        dr = da2 + _dot_nt(dgp, wg_ref[...])
        dr_ref[...] = dr
        drb_ref[...] = _mx(dr)

    r = _row(tm, D_MODEL)
    v = _const((1, D_MODEL))
    return _pc(body, "tail", (t // tm,),
               [_row(tm, D_FF), _const((D_FF, D_MODEL)), r, _const((D_MODEL, D_MODEL)), _row(tm, PLE_DIM),
                _const((PLE_DIM, D_MODEL)), r, v, v],
               [r, r, r, r, r, _const((8, D_MODEL))],
               [SDS((t, D_MODEL), F32)] + [SDS((t, D_MODEL), MXU_DTYPE)] * 4 + [SDS((8, D_MODEL), F32)],
               sem=("arbitrary",))(act, w_down, h1, w_gate, p_bf, w_proj, tgt, g2, b2)


def _mlp_bwd1(dr_bf, w_down, up, tm, tn):
    t = up.shape[0]

    def body(dr_ref, w_ref, up_ref, dup_ref):
        dact = _dot_nt(dr_ref[...], w_ref[...])
        dup_ref[...] = _mx(dact * (2.0 * jnp.maximum(up_ref[...], 0.0)))

    o = pl.BlockSpec((tm, tn), lambda i, j: (i, j))
    return _pc(body, "mlp_bwd1", (t // tm, D_FF // tn),
               [pl.BlockSpec((tm, D_MODEL), lambda i, j: (i, 0)), pl.BlockSpec((tn, D_MODEL), lambda i, j: (j, 0)), o],
               o, SDS((t, D_FF), MXU_DTYPE), sem=("parallel", "parallel"))(dr_bf, w_down, up)


def _mlp_bwd2(dup, w_up, dr, a1, g1, tm):
    t = dr.shape[0]

    def body(dup_ref, w_ref, dr_ref, a1_ref, g_ref, da1_ref, da1b_ref, acc_ref):
        i = pl.program_id(0)

        @pl.when(i == 0)
        def _():
            acc_ref[...] = jnp.zeros_like(acc_ref)

        dh1 = ALPHA * dr_ref[...] + _dot_nt(dup_ref[...], w_ref[...])
        xhat, rstd = _ln_stats(a1_ref[...])
        acc_ref[0:1, :] += jnp.sum(dh1 * xhat, axis=0, keepdims=True)
        acc_ref[1:2, :] += jnp.sum(dh1, axis=0, keepdims=True)
        da1 = _ln_bwd(dh1, xhat, rstd, g_ref[...])
        da1_ref[...] = da1
        da1b_ref[...] = _mx(da1)

    r = _row(tm, D_MODEL)
    return _pc(body, "mlp_bwd2", (t // tm,),
               [_row(tm, D_FF), _const((D_MODEL, D_FF)), r, r, _const((1, D_MODEL))],
               [r, r, _const((8, D_MODEL))],
               [SDS((t, D_MODEL), F32), SDS((t, D_MODEL), MXU_DTYPE), SDS((8, D_MODEL), F32)],
               sem=("arbitrary",))(dup, w_up, dr, a1, g1)


def _ln_in_bwd(dproj, w_cat, da1, x, g, tm, tk):
    t = x.shape[0]
    nk = CAT_WIDTH // tk

    def body(dp_ref, w_ref, da1_ref, x_ref, g_ref, dx_ref, acc_ref, mm_ref):
        i, kk = pl.program_id(0), pl.program_id(1)

        @pl.when((i == 0) & (kk == 0))
        def _():
            acc_ref[...] = jnp.zeros_like(acc_ref)

        @pl.when(kk == 0)
        def _():
            mm_ref[...] = jnp.zeros_like(mm_ref)

        mm_ref[...] += _dot_nt(dp_ref[...], w_ref[...])

        @pl.when(kk == nk - 1)
        def _():
            dh0 = mm_ref[...] + ALPHA * da1_ref[...]
            xhat, rstd = _ln_stats(x_ref[...])
            acc_ref[0:1, :] += jnp.sum(dh0 * xhat, axis=0, keepdims=True)
            acc_ref[1:2, :] += jnp.sum(dh0, axis=0, keepdims=True)
            dx_ref[...] = _ln_bwd(dh0, xhat, rstd, g_ref[...])

    r = pl.BlockSpec((tm, D_MODEL), lambda i, kk: (i, 0))
    return _pc(body, "ln_in_bwd", (t // tm, nk),
               [pl.BlockSpec((tm, tk), lambda i, kk: (i, kk)), pl.BlockSpec((D_MODEL, tk), lambda i, kk: (0, kk)),
                r, r, pl.BlockSpec((1, D_MODEL), lambda i, kk: (0, 0))],
               [r, pl.BlockSpec((8, D_MODEL), lambda i, kk: (0, 0))],
               [SDS((t, D_MODEL), F32), SDS((8, D_MODEL), F32)],
               scratch=[pltpu.VMEM((tm, D_MODEL), F32)], sem=("arbitrary", "arbitrary"))(dproj, w_cat, da1, x, g)


def _local_step(x, p, tgt, wts):
    t = x.shape[0]
    tm = min(512, t)
    tms = min(256, t)
    row = lambda a: a.reshape(1, -1)
    w_cat = wts["w_cat"]
    pool_scale = row(wts["pool_scale"])
    wo_row = jnp.tile(row(wts["o_norm_w"]), (1, HEADS))
    pad8 = jnp.zeros((1, HEADS), F32)
    al_row = jnp.concatenate([pad8, row(wts["a_log"]), jnp.zeros((1, 128 - 2 * HEADS), F32)], axis=1)
    dtb_row = jnp.concatenate([pad8, row(wts["dt_bias"]), jnp.zeros((1, 128 - 2 * HEADS), F32)], axis=1)
    g_in, b_in = row(wts["ln_in_g"]), row(wts["ln_in_b"])
    g1, b1 = row(wts["ln1_g"]), row(wts["ln1_b"])
    g2, b2 = row(wts["ln2_g"]), row(wts["ln2_b"])

    h0, h0_bf = _ln_in(x, g_in, b_in, tm)
    proj = _matmul(h0_bf, w_cat, "nn", "proj", F32, tm=512, tn=1152, tk=1024)
    ypre, d_bf = _pool_fwd(proj, wts["pool_w"], tm)
    qkv_act = _conv_fwd(proj, wts["conv_w"], tm)
    bg = _ba_fwd(proj, al_row, dtb_row, tm)
    bgt = bg[:, :2 * HEADS].T
    u, w, qg, kg, attn, ymat, egl = _dn_local_fwd(qkv_act, bg, bgt)
    o, vn, states = _dn_scan_fwd(u, w, qg, kg, attn, egl)
    mixed = _mix_fwd(o, proj, ypre, pool_scale, wo_row, tm)
    a1, h1, h1_bf = _oproj_ln1(mixed, wts["w_out"], h0, g1, b1, tm)
    up, act = _mlp_up(h1_bf, wts["w_up"], tm, 1024)
    p_bf = _mx(p)
    dr, dr_bf, dgp, dpp, r_bf, acc_tail = _tail(act, wts["w_down"], h1, wts["ple_gate_w"], p_bf, wts["ple_proj_w"],
                                                tgt, g2, b2, tms)
    grads = {}
    grads["ple_proj_w"] = _matmul(p_bf, dpp, "tn", "dw_ple_proj", F32, tm=256, tn=1024, tk=512)
    grads["ple_gate_w"] = _matmul(r_bf, dgp, "tn", "dw_ple_gate", F32, tm=512, tn=1024, tk=512)
    grads["w_down"] = _matmul(act, dr_bf, "tn", "dw_down", F32, tm=512, tn=1024, tk=512)
    dup = _mlp_bwd1(dr_bf, wts["w_down"], up, tm, 1024)
    grads["w_up"] = _matmul(h1_bf, dup, "tn", "dw_up", F32, tm=512, tn=1024, tk=512)
    da1, da1_bf, acc_ln1 = _mlp_bwd2(dup, wts["w_up"], dr, a1, g1, tms)
    grads["w_out"] = _matmul(mixed, da1_bf, "tn", "dw_out", F32, tm=512, tn=1024, tk=512)
    do, dz, dga, dgb, dyp, acc_mix = _mix_bwd(da1_bf, wts["w_out"], o, proj, ypre, pool_scale, wo_row, tms)
    du_pool, grads["pool_w"] = _pool_bwd(dyp, d_bf, wts["pool_w"], tm)
    dvn, dkg, dqg, dattn, dw, degl = _dn_scan_bwd(do, qg, kg, w, attn, vn, states, egl)
    dqkv_act, dbg = _dn_local_bwd(qkv_act, bg, bgt, u, w, ymat, dvn, dw, dqg, dkg, dattn, degl)
    dqkv, acc_conv = _conv_bwd(dqkv_act, proj, wts["conv_w"], tm)
    dba, acc_ba = _ba_bwd(dbg, bg, proj, al_row, dtb_row, tm)
    dproj = jnp.concatenate([dqkv, dz, dga, dgb, du_pool, dba,
                             jnp.zeros((t, CAT_WIDTH - K_BA - 128), MXU_DTYPE)], axis=1)
    dw_cat = _matmul(h0_bf, dproj, "tn", "dw_in", F32, tm=512, tn=1152, tk=512)
    grad_x, acc_in = _ln_in_bwd(dproj, w_cat, da1, x, g_in, tms, 1152)

    grads["w_in"] = jnp.concatenate(
        [dw_cat[:, K_U:K_U + 512], dw_cat[:, K_QKV:K_QKV + 3072], dw_cat[:, K_Z:K_Z + 1024],
         dw_cat[:, K_BA:K_BA + 16], dw_cat[:, K_GA:K_GA + 1024], dw_cat[:, K_GB:K_GB + 1024]], axis=1)
    grads["conv_w"] = acc_conv[0:CONV_K]
    grads["ln_in_g"], grads["ln_in_b"] = acc_in[0], acc_in[1]
    grads["ln1_g"], grads["ln1_b"] = acc_ln1[0], acc_ln1[1]
    grads["ln2_g"], grads["ln2_b"] = acc_tail[0], acc_tail[1]
    grads["pool_scale"] = acc_mix[0]
    grads["o_norm_w"] = acc_mix[1, 0:HEAD_DIM]
    grads["a_log"] = acc_ba[0, HEADS:2 * HEADS]
    grads["dt_bias"] = acc_ba[1, HEADS:2 * HEADS]
    loss = acc_tail[2, 0]
    return grad_x, grads, loss


MESH = pl.DeviceIdType.MESH
ANY = pl.BlockSpec(memory_space=pl.ANY)


def _chip_of(k, x, y):
    chip = (2 * x + y + k) % N_CHIPS
    return chip // 2, chip % 2


def _gather_weights(packed, conv_shard):
    rows = packed.shape[0]

    def body(w_ref, c_ref, wout_ref, cout_ref, send_sems, recv_sems, local_sems):
        x, y, c = lax.axis_index("x"), lax.axis_index("y"), lax.axis_index("c")
        me = 2 * x + y
        own_w = pltpu.make_async_copy(w_ref, wout_ref.at[me], local_sems.at[0])
        own_c = pltpu.make_async_copy(c_ref, cout_ref.at[me], local_sems.at[1])
        own_w.start()
        own_c.start()
        copies = []
        for k in range(1, N_CHIPS):
            tx, ty = _chip_of(k, x, y)
            copies.append(pltpu.make_async_remote_copy(
                src_ref=w_ref, dst_ref=wout_ref.at[me], send_sem=send_sems.at[2 * k], recv_sem=recv_sems.at[2 * k],
                device_id=(tx, ty, c), device_id_type=MESH))
            copies.append(pltpu.make_async_remote_copy(
                src_ref=c_ref, dst_ref=cout_ref.at[me], send_sem=send_sems.at[2 * k + 1],
                recv_sem=recv_sems.at[2 * k + 1], device_id=(tx, ty, c), device_id_type=MESH))
        for cp in copies:
            cp.start()
        for k in range(1, N_CHIPS):
            src = (me + N_CHIPS - k) % N_CHIPS
            pltpu.make_async_remote_copy(
                src_ref=w_ref, dst_ref=wout_ref.at[src], send_sem=send_sems.at[2 * k], recv_sem=recv_sems.at[2 * k],
                device_id=(x, y, c), device_id_type=MESH).wait_recv()
            pltpu.make_async_remote_copy(
                src_ref=c_ref, dst_ref=cout_ref.at[src], send_sem=send_sems.at[2 * k + 1],
                recv_sem=recv_sems.at[2 * k + 1], device_id=(x, y, c), device_id_type=MESH).wait_recv()
        for cp in copies:
            cp.wait_send()
        own_w.wait()
        own_c.wait()

    return pl.pallas_call(
        body, name="gather_weights",
        out_shape=[SDS((N_CHIPS, rows, D_MODEL), packed.dtype), SDS((N_CHIPS,) + conv_shard.shape, conv_shard.dtype)],
        in_specs=[ANY, ANY], out_specs=[ANY, ANY],
        scratch_shapes=[pltpu.SemaphoreType.DMA((2 * N_CHIPS,)), pltpu.SemaphoreType.DMA((2 * N_CHIPS,)),
                        pltpu.SemaphoreType.DMA((2,))],
    )(packed, conv_shard)


def _scatter_grads(gpk):
    rows = gpk.shape[1]

    def body(g_ref, out_ref, send_sems, recv_sems, local_sem):
        x, y, c = lax.axis_index("x"), lax.axis_index("y"), lax.axis_index("c")
        me = 2 * x + y
        own = pltpu.make_async_copy(g_ref.at[me], out_ref.at[me], local_sem)
        own.start()
        copies = []
        for k in range(1, N_CHIPS):
            tx, ty = _chip_of(k, x, y)
            copies.append(pltpu.make_async_remote_copy(
                src_ref=g_ref.at[2 * tx + ty], dst_ref=out_ref.at[me], send_sem=send_sems.at[k],
                recv_sem=recv_sems.at[k], device_id=(tx, ty, c), device_id_type=MESH))
        for cp in copies:
            cp.start()
        for k in range(1, N_CHIPS):
            src = (me + N_CHIPS - k) % N_CHIPS
            pltpu.make_async_remote_copy(
                src_ref=g_ref.at[me], dst_ref=out_ref.at[src], send_sem=send_sems.at[k], recv_sem=recv_sems.at[k],
                device_id=(x, y, c), device_id_type=MESH).wait_recv()
        for cp in copies:
            cp.wait_send()
        own.wait()

    return pl.pallas_call(
        body, name="scatter_grads", out_shape=SDS((N_CHIPS, rows, D_MODEL), gpk.dtype),
        in_specs=[ANY], out_specs=ANY,
        scratch_shapes=[pltpu.SemaphoreType.DMA((N_CHIPS,)), pltpu.SemaphoreType.DMA((N_CHIPS,)),
                        pltpu.SemaphoreType.DMA],
    )(gpk)


def _sum_slices(recv):
    rows = recv.shape[1]

    def body(r_ref, o_ref):
        acc = r_ref[0].astype(F32)
        for j in range(1, N_CHIPS):
            acc = acc + r_ref[j].astype(F32)
        o_ref[...] = acc

    return _pc(body, "sum_slices", (rows // PACK_TILE,),
               [pl.BlockSpec((N_CHIPS, PACK_TILE, D_MODEL), lambda i: (0, i, 0))], _row(PACK_TILE, D_MODEL),
               SDS((rows, D_MODEL), F32), sem=("parallel",))(recv)


def _swap_with_sibling(part):
    def body(p_ref, o_ref, send_sem, recv_sem):
        x, y, c = lax.axis_index("x"), lax.axis_index("y"), lax.axis_index("c")
        cp = pltpu.make_async_remote_copy(src_ref=p_ref, dst_ref=o_ref, send_sem=send_sem, recv_sem=recv_sem,
                                          device_id=(x, y, 1 - c), device_id_type=MESH)
        cp.start()
        cp.wait()

    return pl.pallas_call(
        body, name="swap_sibling", out_shape=SDS(part.shape, part.dtype), in_specs=[ANY], out_specs=ANY,
        scratch_shapes=[pltpu.SemaphoreType.DMA, pltpu.SemaphoreType.DMA],
    )(part)


def _adamw_math(w, g, m, v):
    m = ADAM_B1 * m + (1.0 - ADAM_B1) * g
    v = ADAM_B2 * v + (1.0 - ADAM_B2) * (g * g)
    m_hat = m / (1.0 - ADAM_B1 ** ADAM_STEP)
    v_hat = v / (1.0 - ADAM_B2 ** ADAM_STEP)
    delta = -ADAM_LR * (m_hat / (jnp.sqrt(v_hat) + ADAM_EPS) + ADAM_WD * w)
    return delta, m, v


def _adamw_packed(w, ga, gb, m, v):
    rows = w.shape[0]

    def body(w_ref, ga_ref, gb_ref, m_ref, v_ref, g_out, d_out, m_out, v_out):
        g = ga_ref[...] + gb_ref[...]
        delta, mn, vn = _adamw_math(w_ref[...], g, m_ref[...], v_ref[...])
        g_out[...] = g
        d_out[...] = delta
        m_out[...] = mn
        v_out[...] = vn

    r = _row(PACK_TILE, D_MODEL)
    return _pc(body, "adamw_packed", (rows // PACK_TILE,), [r] * 5, [r] * 4, [SDS((rows, D_MODEL), F32)] * 4,
               sem=("parallel",))(w, ga, gb, m, v)


def _small_allreduce_adamw(mine, w, m, v):
    shape = mine.shape

    def body(mine_ref, w_ref, m_ref, v_ref, g_out, d_out, m_out, v_out, buf_ref, send_sems, recv_sems):
        x, y, c = lax.axis_index("x"), lax.axis_index("y"), lax.axis_index("c")
        me = 4 * x + 2 * y + c
        buf_ref[me] = mine_ref[...]
        copies = []
        for k in range(1, N_DEV):
            tgt = (me + k) % N_DEV
            copies.append(pltpu.make_async_remote_copy(
                src_ref=mine_ref, dst_ref=buf_ref.at[me], send_sem=send_sems.at[k], recv_sem=recv_sems.at[k],
                device_id=(tgt // 4, (tgt // 2) % 2, tgt % 2), device_id_type=MESH))
        for cp in copies:
            cp.start()
        for k in range(1, N_DEV):
            src = (me + N_DEV - k) % N_DEV
            pltpu.make_async_remote_copy(
                src_ref=mine_ref, dst_ref=buf_ref.at[src], send_sem=send_sems.at[k], recv_sem=recv_sems.at[k],
                device_id=(x, y, c), device_id_type=MESH).wait_recv()
        for cp in copies:
            cp.wait_send()
        g = buf_ref[0]
        for j in range(1, N_DEV):
            g = g + buf_ref[j]
        delta, mn, vn = _adamw_math(w_ref[...], g, m_ref[...], v_ref[...])
        g_out[...] = g
        d_out[...] = delta
        m_out[...] = mn
        v_out[...] = vn

    vm = pl.BlockSpec(memory_space=pltpu.VMEM)
    return pl.pallas_call(
        body, name="small_allreduce_adamw", out_shape=[SDS(shape, F32)] * 4, in_specs=[vm] * 4, out_specs=[vm] * 4,
        scratch_shapes=[pltpu.VMEM((N_DEV,) + shape, F32), pltpu.SemaphoreType.DMA((N_DEV,)),
                        pltpu.SemaphoreType.DMA((N_DEV,))],
    )(mine, w, m, v)


def _pack_shards(parts, dtype):
    rows = [parts[n].reshape(r, D_MODEL).astype(dtype) for n, r in PACK]
    used = sum(r for _, r in PACK)
    rows.append(jnp.zeros((PACK_ROWS - used, D_MODEL), dtype))
    return jnp.concatenate(rows, axis=0)


def _unpack_shards(packed, shapes):
    out, at = {}, 0
    for n, r in PACK:
        out[n] = packed[at:at + r].reshape(shapes[n])
        at += r
    return out


SHARD_AXIS = {"w_in": 1, "pool_w": 2, "w_out": 0, "w_up": 1, "w_down": 0, "ple_gate_w": 0, "ple_proj_w": 1,
              "conv_w": 1}


def _split_for_chips(name, full):
    return jnp.split(full, N_CHIPS, axis=SHARD_AXIS[name])


def _pad_row(a):
    a = a.reshape(1, -1).astype(F32)
    return jnp.pad(a, ((0, 0), (0, D_MODEL - a.shape[1])))


def kernel(x, p, ln_in_g, ln_in_b, w_in, pool_w, pool_scale, conv_w, a_log, dt_bias, o_norm_w, w_out, ln1_g, ln1_b, w_up, w_down, ple_gate_w, ple_proj_w, ln2_g, ln2_b, loss_target, m_ln_in_g, m_ln_in_b, m_w_in, m_pool_w, m_pool_scale, m_conv_w, m_a_log, m_dt_bias, m_o_norm_w, m_w_out, m_ln1_g, m_ln1_b, m_w_up, m_w_down, m_ple_gate_w, m_ple_proj_w, m_ln2_g, m_ln2_b, v_ln_in_g, v_ln_in_b, v_w_in, v_pool_w, v_pool_scale, v_conv_w, v_a_log, v_dt_bias, v_o_norm_w, v_w_out, v_ln1_g, v_ln1_b, v_w_up, v_w_down, v_ple_gate_w, v_ple_proj_w, v_ln2_g, v_ln2_b):
    given = dict(locals())
    big = [n for n, _ in PACK]
    shard_shapes = {n: given[n].shape for n in big}
    local_shapes = {n: given[n].shape[1:] for n in big}

    packed_w = _pack_shards({n: given[n] for n in big}, WIRE_DTYPE)
    conv_pad = jnp.pad(conv_w[0], ((0, 8 - CONV_K), (0, 0)))
    all_w, all_conv = _gather_weights(packed_w, conv_pad)
    full = {}
    for n in big:
        if n == "conv_w":
            continue
        shards = [_unpack_shards(all_w[j], local_shapes)[n] for j in range(N_CHIPS)]
        full[n] = jnp.concatenate(shards, axis=SHARD_AXIS[n]).astype(MXU_DTYPE)
    wi = full["w_in"]
    wts = {
        "w_cat": jnp.concatenate(
            [wi[:, C_QKV:C_Z], wi[:, C_Z:C_BETA], wi[:, C_GA:C_GB], wi[:, C_GB:IN_WIDTH], wi[:, C_POOL:C_QKV],
             wi[:, C_BETA:C_GA], jnp.zeros((D_MODEL, CAT_WIDTH - K_BA - 2 * HEADS), MXU_DTYPE)], axis=1),
        "pool_w": full["pool_w"], "w_out": full["w_out"], "w_up": full["w_up"], "w_down": full["w_down"],
        "ple_gate_w": full["ple_gate_w"], "ple_proj_w": full["ple_proj_w"],
        "conv_w": jnp.concatenate([all_conv[j, 0:CONV_K] for j in range(N_CHIPS)], axis=1),
        "ln_in_g": ln_in_g, "ln_in_b": ln_in_b, "pool_scale": pool_scale[0], "a_log": a_log[0],
        "dt_bias": dt_bias[0], "o_norm_w": o_norm_w[0], "ln1_g": ln1_g[0], "ln1_b": ln1_b[0],
        "ln2_g": ln2_g[0], "ln2_b": ln2_b[0],
    }

    grad_x, grads, loss = _local_step(x[0], p[0, 0], loss_target[0], wts)

    gpk = jnp.stack([
        _pack_shards({n: _split_for_chips(n, grads[n])[j] for n in big}, WIRE_DTYPE) for j in range(N_CHIPS)])
    part = _sum_slices(_scatter_grads(gpk))
    other = _swap_with_sibling(part)
    packed = lambda prefix: _pack_shards({n: given[prefix + n] for n in big}, F32)
    g_pk, d_pk, m_pk, v_pk = _adamw_packed(packed(""), part, other, packed("m_"), packed("v_"))
    big_out = [_unpack_shards(a, shard_shapes) for a in (g_pk, d_pk, m_pk, v_pk)]

    def small_pack(get):
        rows = [_pad_row(get(n)) for n in SMALL_NAMES]
        return rows

    mine = jnp.concatenate(small_pack(lambda n: grads[n]) + [jnp.full((1, D_MODEL), loss, F32)]
                           + [jnp.zeros((SMALL_ROWS - len(SMALL_NAMES) - 1, D_MODEL), F32)], axis=0)
    fill = [jnp.zeros((SMALL_ROWS - len(SMALL_NAMES), D_MODEL), F32)]
    sw = jnp.concatenate(small_pack(lambda n: given[n]) + fill, axis=0)
    sm = jnp.concatenate(small_pack(lambda n: given["m_" + n]) + fill, axis=0)
    sv = jnp.concatenate(small_pack(lambda n: given["v_" + n]) + fill, axis=0)
    small_out = _small_allreduce_adamw(mine, sw, sm, sv)

    def small_get(k, n):
        i = SMALL_NAMES.index(n)
        size = given[n].size
        return small_out[k][i, 0:size].reshape(given[n].shape)

    order = ["ln_in_g", "ln_in_b", "w_in", "pool_w", "pool_scale", "conv_w", "a_log", "dt_bias", "o_norm_w", "w_out",
             "ln1_g", "ln1_b", "w_up", "w_down", "ple_gate_w", "ple_proj_w", "ln2_g", "ln2_b"]
    outs = [small_out[0][len(SMALL_NAMES), 0], grad_x[None]]
    for k in range(4):
        for n in order:
            outs.append(big_out[k][n] if n in shard_shapes else small_get(k, n))
    return tuple(outs)
```

```python
import jax
import jax.numpy as jnp
from jax import lax
from jax.experimental import pallas as pl
from jax.experimental.pallas import tpu as pltpu

F32 = jnp.float32
MXU_DTYPE = jnp.bfloat16
WIRE_DTYPE = jnp.bfloat16
SDS = jax.ShapeDtypeStruct

D_MODEL = 1024
POOL_WINDOWS = (2, 4, 8, 16)
POOL_WIDTH = 512
POOL_GROUP = 128
POOL_OUT_GROUP = 256
HEADS = 8
HEAD_DIM = 128
DN_WIDTH = HEADS * HEAD_DIM
QKV_WIDTH = 3 * DN_WIDTH
CONV_K = 4
CHUNK = 128
HEAD_GROUP = 4
D_FF = 4096
PLE_DIM = 256
LN_EPS = 1e-5
RMS_EPS = 1e-6
L2_EPS = 1e-6
ALPHA = 2.0 ** 0.25
Q_SCALE = HEAD_DIM ** -0.5
IN_WIDTH = 6672
C_POOL, C_QKV, C_Z, C_BETA, C_A, C_GA, C_GB = 0, 512, 3584, 4608, 4616, 4624, 5648
K_QKV, K_Z, K_GA, K_GB, K_U, K_BA, CAT_WIDTH = 0, 3072, 4096, 5120, 6144, 6656, 6912

ADAM_LR, ADAM_B1, ADAM_B2, ADAM_EPS, ADAM_WD, ADAM_STEP = 0.001, 0.9, 0.999, 1e-08, 0.01, 10

N_CHIPS = 4
N_DEV = 8
VMEM_LIMIT = 56 * 1024 * 1024

PACK = (("w_in", 1668), ("pool_w", 32), ("w_out", 256), ("w_up", 1024), ("w_down", 1024),
        ("ple_gate_w", 256), ("ple_proj_w", 64), ("conv_w", 3))
PACK_ROWS = 4352
PACK_TILE = 256
SMALL_ROWS = 16
SMALL_NAMES = ("ln_in_g", "ln_in_b", "pool_scale", "ln1_g", "ln1_b", "ln2_g", "ln2_b", "o_norm_w", "a_log", "dt_bias")


def _mx(a):
    return a.astype(MXU_DTYPE)


def _dot(a, b):
    return lax.dot_general(_mx(a), _mx(b), (((1,), (0,)), ((), ())), preferred_element_type=F32)


def _dot_nt(a, b):
    return lax.dot_general(_mx(a), _mx(b), (((1,), (1,)), ((), ())), preferred_element_type=F32)


def _dot_tn(a, b):
    return lax.dot_general(_mx(a), _mx(b), (((0,), (0,)), ((), ())), preferred_element_type=F32)


def _sigmoid(x):
    return 1.0 / (1.0 + jnp.exp(-x))


def _softplus(x):
    return jnp.maximum(x, 0.0) + jnp.log(1.0 + jnp.exp(-jnp.abs(x)))


def _pc(body, name, grid, in_specs, out_specs, out_shape, scratch=(), sem=None):
    return pl.pallas_call(
        body, out_shape=out_shape, grid=grid, in_specs=in_specs, out_specs=out_specs,
        scratch_shapes=scratch, name=name,
        compiler_params=pltpu.CompilerParams(dimension_semantics=sem, vmem_limit_bytes=VMEM_LIMIT))


def _row(tm, n):
    return pl.BlockSpec((tm, n), lambda i: (i, 0))


def _const(shape):
    nd = len(shape)
    return pl.BlockSpec(shape, lambda *_: (0,) * nd)


def _matmul(a, b, mode, name, out_dtype=F32, tm=512, tn=512, tk=512):
    if mode == "nn":
        (m, k), n = a.shape, b.shape[1]
    elif mode == "nt":
        (m, k), n = a.shape, b.shape[0]
    else:
        (k, m), n = a.shape, b.shape[1]
    tm, tn, tk = min(tm, m), min(tn, n), min(tk, k)
    assert m % tm == 0 and n % tn == 0 and k % tk == 0, (name, m, n, k, tm, tn, tk)
    nk = k // tk
    if mode == "nn":
        a_spec = pl.BlockSpec((tm, tk), lambda i, j, kk: (i, kk))
        b_spec = pl.BlockSpec((tk, tn), lambda i, j, kk: (kk, j))
        dot = _dot
    elif mode == "nt":
        a_spec = pl.BlockSpec((tm, tk), lambda i, j, kk: (i, kk))
        b_spec = pl.BlockSpec((tn, tk), lambda i, j, kk: (j, kk))
        dot = _dot_nt
    else:
        a_spec = pl.BlockSpec((tk, tm), lambda i, j, kk: (kk, i))
        b_spec = pl.BlockSpec((tk, tn), lambda i, j, kk: (kk, j))
        dot = _dot_tn

    def body(a_ref, b_ref, o_ref, acc_ref):
        kk = pl.program_id(2)

        @pl.when(kk == 0)
        def _():
            acc_ref[...] = jnp.zeros_like(acc_ref)

        acc_ref[...] += dot(a_ref[...], b_ref[...])

        @pl.when(kk == nk - 1)
        def _():
            o_ref[...] = acc_ref[...].astype(out_dtype)

    return _pc(body, name, (m // tm, n // tn, nk), [a_spec, b_spec],
               pl.BlockSpec((tm, tn), lambda i, j, kk: (i, j)), SDS((m, n), out_dtype),
               scratch=[pltpu.VMEM((tm, tn), F32)], sem=("parallel", "parallel", "arbitrary"))(a, b)


def _ln_stats(x):
    mu = jnp.mean(x, axis=-1, keepdims=True)
    xc = x - mu
    var = jnp.mean(xc * xc, axis=-1, keepdims=True)
    rstd = lax.rsqrt(var + LN_EPS)
    return xc * rstd, rstd


def _ln_bwd(dy, xhat, rstd, g):
    dxh = dy * g
    m1 = jnp.mean(dxh, axis=-1, keepdims=True)
    m2 = jnp.mean(dxh * xhat, axis=-1, keepdims=True)
    return rstd * (dxh - m1 - xhat * m2)


def _ln_in(x, g, b, tm):
    t, d = x.shape

    def body(x_ref, g_ref, b_ref, h_ref, hb_ref):
        xhat, _ = _ln_stats(x_ref[...])
        h = xhat * g_ref[...] + b_ref[...]
        h_ref[...] = h
        hb_ref[...] = _mx(h)

    return _pc(body, "ln_in", (t // tm,), [_row(tm, d), _const((1, d)), _const((1, d))],
               [_row(tm, d), _row(tm, d)], [SDS((t, d), F32), SDS((t, d), MXU_DTYPE)], sem=("parallel",))(x, g, b)


def _pool_fwd(proj, pool_w, tm):
    t = proj.shape[0]
    ublk = K_U // POOL_WIDTH

    def body(u_ref, halo_ref, pw_ref, ypre_ref, d_ref, ext_ref):
        i = pl.program_id(0)
        ext_ref[0:16, :] = jnp.where(i > 0, halo_ref[...], 0.0)
        ext_ref[16:16 + tm, :] = u_ref[...]
        tok = i * tm + lax.broadcasted_iota(jnp.int32, (tm, POOL_GROUP), 0)
        for gi, w in enumerate(POOL_WINDOWS):
            cs = pl.ds(gi * POOL_GROUP, POOL_GROUP)
            ug = ext_ref[pl.ds(16, tm), cs]
            s = ug
            for k in range(1, w):
                s = s + ext_ref[pl.ds(16 - k, tm), cs]
            cnt = jnp.minimum(tok + 1, w).astype(F32)
            db = _mx(s / cnt - ug)
            d_ref[:, gi * POOL_GROUP:(gi + 1) * POOL_GROUP] = db
            ypre_ref[:, gi * POOL_OUT_GROUP:(gi + 1) * POOL_OUT_GROUP] = _dot(db, pw_ref[gi])

    halo = pl.BlockSpec((16, POOL_WIDTH), lambda i: (jnp.maximum(i * (tm // 16) - 1, 0), ublk))
    return _pc(body, "pool_fwd", (t // tm,),
               [pl.BlockSpec((tm, POOL_WIDTH), lambda i: (i, ublk)), halo, _const((4, POOL_GROUP, POOL_OUT_GROUP))],
               [_row(tm, D_MODEL), _row(tm, POOL_WIDTH)],
               [SDS((t, D_MODEL), F32), SDS((t, POOL_WIDTH), MXU_DTYPE)],
               scratch=[pltpu.VMEM((16 + tm, POOL_WIDTH), F32)], sem=("parallel",))(proj, proj, pool_w)


def _pool_bwd(dyp, d_bf, pool_w, tm):
    t = dyp.shape[0]
    n = t // tm

    def body(dy_ref, dyn_ref, d_ref, pw_ref, du_ref, dpw_ref, ext_ref):
        i = pl.program_id(0)

        @pl.when(i == 0)
        def _():
            dpw_ref[...] = jnp.zeros_like(dpw_ref)

        tok = i * tm + lax.broadcasted_iota(jnp.int32, (tm + 16, POOL_GROUP), 0)
        for gi, w in enumerate(POOL_WINDOWS):
            dy = dy_ref[:, gi * POOL_OUT_GROUP:(gi + 1) * POOL_OUT_GROUP]
            dyn = dyn_ref[:, gi * POOL_OUT_GROUP:(gi + 1) * POOL_OUT_GROUP]
            pw = pw_ref[gi]
            dd = _dot_nt(dy, pw)
            ddn = jnp.where(i < n - 1, _dot_nt(dyn, pw), 0.0)
            cnt = jnp.minimum(tok + 1, w).astype(F32)
            ext_ref[0:tm, :] = dd / cnt[0:tm]
            ext_ref[tm:tm + 16, :] = ddn / cnt[tm:tm + 16]
            s = ext_ref[pl.ds(0, tm), :]
            for k in range(1, w):
                s = s + ext_ref[pl.ds(k, tm), :]
            du_ref[:, gi * POOL_GROUP:(gi + 1) * POOL_GROUP] = _mx(s - dd)
            dpw_ref[gi] += _dot_tn(d_ref[:, gi * POOL_GROUP:(gi + 1) * POOL_GROUP], dy)

    nxt = pl.BlockSpec((16, D_MODEL), lambda i: (jnp.minimum((i + 1) * (tm // 16), t // 16 - 1), 0))
    return _pc(body, "pool_bwd", (n,),
               [_row(tm, D_MODEL), nxt, _row(tm, POOL_WIDTH), _const((4, POOL_GROUP, POOL_OUT_GROUP))],
               [_row(tm, POOL_WIDTH), _const((4, POOL_GROUP, POOL_OUT_GROUP))],
               [SDS((t, POOL_WIDTH), MXU_DTYPE), SDS((4, POOL_GROUP, POOL_OUT_GROUP), F32)],
               scratch=[pltpu.VMEM((tm + 16, POOL_GROUP), F32)], sem=("arbitrary",))(dyp, dyp, d_bf, pool_w)


CONV_BLK = 512


def _conv_fwd(proj, conv_w, tm):
    t = proj.shape[0]

    def body(x_ref, halo_ref, w_ref, o_ref, ext_ref):
        i = pl.program_id(0)
        ext_ref[0:8, :] = jnp.where(i > 0, halo_ref[...], 0.0)
        ext_ref[8:8 + tm, :] = x_ref[...]
        y = w_ref[pl.ds(0, 1), :] * ext_ref[pl.ds(5, tm), :]
        for k in range(1, CONV_K):
            y = y + w_ref[pl.ds(k, 1), :] * ext_ref[pl.ds(5 + k, tm), :]
        o_ref[...] = y * _sigmoid(y)

    halo = pl.BlockSpec((8, CONV_BLK), lambda i, j: (jnp.maximum(i * (tm // 8) - 1, 0), j))
    blk = pl.BlockSpec((tm, CONV_BLK), lambda i, j: (i, j))
    return _pc(body, "conv_fwd", (t // tm, QKV_WIDTH // CONV_BLK),
               [blk, halo, pl.BlockSpec((CONV_K, CONV_BLK), lambda i, j: (0, j))], blk,
               SDS((t, QKV_WIDTH), F32), scratch=[pltpu.VMEM((8 + tm, CONV_BLK), F32)],
               sem=("parallel", "parallel"))(proj, proj, conv_w)


def _conv_bwd(dact, proj, conv_w, tm):
    t = proj.shape[0]
    n = t // tm

    def body(da_ref, dan_ref, x_ref, xp_ref, xn_ref, w_ref, dx_ref, dw_ref, ext_ref, dy_ref):
        i = pl.program_id(1)

        @pl.when(i == 0)
        def _():
            dw_ref[...] = jnp.zeros_like(dw_ref)

        ext_ref[0:8, :] = jnp.where(i > 0, xp_ref[...], 0.0)
        ext_ref[8:8 + tm, :] = x_ref[...]
        ext_ref[8 + tm:16 + tm, :] = jnp.where(i < n - 1, xn_ref[...], 0.0)
        y = w_ref[pl.ds(0, 1), :] * ext_ref[pl.ds(5, tm + 8), :]
        for k in range(1, CONV_K):
            y = y + w_ref[pl.ds(k, 1), :] * ext_ref[pl.ds(5 + k, tm + 8), :]
        s = _sigmoid(y)
        dsilu = s * (1.0 + y * (1.0 - s))
        dy_ref[0:tm, :] = da_ref[...] * dsilu[0:tm]
        dy_ref[tm:tm + 8, :] = jnp.where(i < n - 1, dan_ref[...], 0.0) * dsilu[tm:tm + 8]
        dx = w_ref[pl.ds(0, 1), :] * dy_ref[pl.ds(3, tm), :]
        for k in range(1, CONV_K):
            dx = dx + w_ref[pl.ds(k, 1), :] * dy_ref[pl.ds(3 - k, tm), :]
        dx_ref[...] = _mx(dx)
        dy = dy_ref[pl.ds(0, tm), :]
        for k in range(CONV_K):
            dw_ref[pl.ds(k, 1), :] += jnp.sum(dy * ext_ref[pl.ds(5 + k, tm), :], axis=0, keepdims=True)

    blk = pl.BlockSpec((tm, CONV_BLK), lambda j, i: (i, j))
    prev = pl.BlockSpec((8, CONV_BLK), lambda j, i: (jnp.maximum(i * (tm // 8) - 1, 0), j))
    nxt = pl.BlockSpec((8, CONV_BLK), lambda j, i: (jnp.minimum((i + 1) * (tm // 8), t // 8 - 1), j))
    wspec = pl.BlockSpec((CONV_K, CONV_BLK), lambda j, i: (0, j))
    return _pc(body, "conv_bwd", (QKV_WIDTH // CONV_BLK, n),
               [blk, nxt, blk, prev, nxt, wspec],
               [blk, pl.BlockSpec((8, CONV_BLK), lambda j, i: (0, j))],
               [SDS((t, QKV_WIDTH), MXU_DTYPE), SDS((8, QKV_WIDTH), F32)],
               scratch=[pltpu.VMEM((16 + tm, CONV_BLK), F32), pltpu.VMEM((8 + tm, CONV_BLK), F32)],
               sem=("parallel", "arbitrary"))(dact, dact, proj, proj, proj, conv_w)


def _lane(shape):
    return lax.broadcasted_iota(jnp.int32, shape, 1)


def _ba_fwd(proj, al_row, dtb_row, tm):
    t = proj.shape[0]
    bablk = K_BA // 128

    def body(ba_ref, al_ref, dtb_ref, bg_ref):
        ba = ba_ref[...]
        lane = _lane(ba.shape)
        g = -jnp.exp(al_ref[...]) * _softplus(ba + dtb_ref[...])
        bg_ref[...] = jnp.where(lane < HEADS, _sigmoid(ba), jnp.where(lane < 2 * HEADS, g, 0.0))

    return _pc(body, "ba_fwd", (t // tm,),
               [pl.BlockSpec((tm, 128), lambda i: (i, bablk)), _const((1, 128)), _const((1, 128))],
               _row(tm, 128), SDS((t, 128), F32), sem=("parallel",))(proj, al_row, dtb_row)


def _ba_bwd(dbg, bg, proj, al_row, dtb_row, tm):
    t = proj.shape[0]
    bablk = K_BA // 128

    def body(dbg_ref, bg_ref, ba_ref, al_ref, dtb_ref, dba_ref, acc_ref):
        i = pl.program_id(0)

        @pl.when(i == 0)
        def _():
            acc_ref[...] = jnp.zeros_like(acc_ref)

        dbg_v, bg_v, ba = dbg_ref[...], bg_ref[...], ba_ref[...]
        lane = _lane(ba.shape)
        is_g = (lane >= HEADS) & (lane < 2 * HEADS)
        dbeta_raw = dbg_v * bg_v * (1.0 - bg_v)
        da_raw = dbg_v * (-jnp.exp(al_ref[...])) * _sigmoid(ba + dtb_ref[...])
        dba_ref[...] = _mx(jnp.where(lane < HEADS, dbeta_raw, jnp.where(is_g, da_raw, 0.0)))
        acc_ref[0:1, :] += jnp.sum(jnp.where(is_g, dbg_v * bg_v, 0.0), axis=0, keepdims=True)
        acc_ref[1:2, :] += jnp.sum(jnp.where(is_g, da_raw, 0.0), axis=0, keepdims=True)

    return _pc(body, "ba_bwd", (t // tm,),
               [_row(tm, 128), _row(tm, 128), pl.BlockSpec((tm, 128), lambda i: (i, bablk)),
                _const((1, 128)), _const((1, 128))],
               [_row(tm, 128), _const((8, 128))], [SDS((t, 128), MXU_DTYPE), SDS((8, 128), F32)],
               sem=("arbitrary",))(dbg, bg, proj, al_row, dtb_row)


def _each(f, *lists):
    return [f(*a) for a in zip(*lists)]


def _rowsum(a):
    return jnp.sum(a, axis=1, keepdims=True)


def _chunk_terms(qs, ks, bgv, g_rows, hs):
    c = CHUNK
    ii = lax.broadcasted_iota(jnp.int32, (c, c), 0)
    jj = lax.broadcasted_iota(jnp.int32, (c, c), 1)
    lane = _lane(bgv.shape)
    incl = ii >= jj
    beta = [_rowsum(jnp.where(lane == h, bgv, 0.0)) for h in hs]
    g_col = [_rowsum(jnp.where(lane == HEADS + h, bgv, 0.0)) for h in hs]
    rq = _each(lambda q: lax.rsqrt(_rowsum(q * q) + L2_EPS), qs)
    rk = _each(lambda k: lax.rsqrt(_rowsum(k * k) + L2_EPS), ks)
    yq = _each(jnp.multiply, qs, rq)
    kn = _each(jnp.multiply, ks, rk)
    qn = _each(lambda a: a * Q_SCALE, yq)
    gc_col = _each(lambda g: _rowsum(jnp.where(jj <= ii, g, 0.0)), g_rows)
    gc_row = _each(lambda g: jnp.sum(jnp.where(ii <= jj, g, 0.0), axis=0, keepdims=True), g_col)
    dm = _each(lambda a, b: jnp.where(incl, jnp.exp(jnp.where(incl, a - b, 0.0)), 0.0), gc_col, gc_row)
    gl = _each(_rowsum, g_rows)
    eg = _each(jnp.exp, gc_col)
    ek = _each(lambda a, b: jnp.exp(a - b), gl, gc_col)
    egl = _each(jnp.exp, gl)
    kb = _each(jnp.multiply, kn, beta)
    kk = _each(_dot_nt, kb, kn)
    qk = _each(_dot_nt, qn, kn)
    m = _each(lambda a, b: jnp.where(ii > jj, a * b, 0.0), kk, dm)
    attn = _each(jnp.multiply, qk, dm)
    return dict(ii=ii, jj=jj, beta=beta, rq=rq, rk=rk, yq=yq, kn=kn, qn=qn, dm=dm, eg=eg, ek=ek,
                egl=egl, kb=kb, m=m, attn=attn)


def _unit_lower_inverse_minus_identity(ms, ii, jj):
    pair = (ii >> 1) == (jj >> 1)
    ys = _each(lambda m: -jnp.where(pair, m, 0.0), ms)
    s = 1
    while (1 << s) < CHUNK:
        mask = ((ii >> (s + 1)) == (jj >> (s + 1))) & ((ii >> s) != (jj >> s))
        lbs = _each(lambda m: jnp.where(mask, m, 0.0), ms)
        zs = _each(lambda y, lb: lb + _dot(y, lb), ys, lbs)
        ys = _each(lambda y, z: y - z - _dot(z, y), ys, zs)
        s += 1
    return ys


def _head_offsets(group):
    hs = [group * HEAD_GROUP + a for a in range(HEAD_GROUP)]
    return hs, [pl.ds(pl.multiple_of(base + h * HEAD_DIM, HEAD_DIM), HEAD_DIM)
                for base in (0, DN_WIDTH, 2 * DN_WIDTH) for h in hs]


def _dn_local_fwd(qkv_act, bg, bgt):
    t = qkv_act.shape[0]
    nt = t // CHUNK
    c = CHUNK

    def body(qkv_ref, bg_ref, bgt_ref, u_ref, w_ref, qg_ref, kg_ref, attn_ref, y_ref, egl_ref):
        bgv = bg_ref[...]

        def group(gi, carry):
            hs, offs = _head_offsets(gi)
            qo, ko, vo = offs[0:HEAD_GROUP], offs[HEAD_GROUP:2 * HEAD_GROUP], offs[2 * HEAD_GROUP:]
            qs = [qkv_ref[:, o] for o in qo]
            ks = [qkv_ref[:, o] for o in ko]
            vs = [qkv_ref[:, o] for o in vo]
            g_rows = [bgt_ref[pl.ds(HEADS + h, 1), :] for h in hs]
            ct = _chunk_terms(qs, ks, bgv, g_rows, hs)
            ys = _unit_lower_inverse_minus_identity(ct["m"], ct["ii"], ct["jj"])
            vb = _each(jnp.multiply, vs, ct["beta"])
            kbe = _each(jnp.multiply, ct["kb"], ct["eg"])
            us = _each(lambda a, y: a + _dot(y, a), vb, ys)
            ws = _each(lambda a, y: a + _dot(y, a), kbe, ys)
            for a in range(HEAD_GROUP):
                dst = qo[a]
                u_ref[:, dst] = us[a]
                w_ref[:, dst] = _mx(ws[a])
                qg_ref[:, dst] = _mx(ct["qn"][a] * ct["eg"][a])
                kg_ref[:, dst] = _mx(ct["kn"][a] * ct["ek"][a])
                attn_ref[:, dst] = _mx(ct["attn"][a])
                y_ref[:, dst] = _mx(ys[a])
                egl_ref[0, pl.ds(hs[a], 1), :] = jnp.broadcast_to(ct["egl"][a], (1, HEAD_DIM))
            return carry

        lax.fori_loop(0, HEADS // HEAD_GROUP, group, 0)

    wide = _row(c, DN_WIDTH)
    return _pc(body, "dn_local_fwd", (nt,),
               [_row(c, QKV_WIDTH), _row(c, 128), pl.BlockSpec((2 * HEADS, c), lambda i: (0, i))],
               [wide, wide, wide, wide, wide, wide, pl.BlockSpec((1, HEADS, HEAD_DIM), lambda i: (i, 0, 0))],
               [SDS((t, DN_WIDTH), F32)] + [SDS((t, DN_WIDTH), MXU_DTYPE)] * 5 + [SDS((nt, HEADS, HEAD_DIM), F32)],
               sem=("parallel",))(qkv_act, bg, bgt)


def _dn_scan_fwd(u, w, qg, kg, attn, egl):
    t = u.shape[0]
    nt = t // CHUNK
    c = CHUNK
    sls = [slice(h * HEAD_DIM, (h + 1) * HEAD_DIM) for h in range(HEADS)]

    def body(u_ref, w_ref, qg_ref, kg_ref, attn_ref, egl_ref, o_ref, vn_ref, st_ref, s_ref):
        @pl.when(pl.program_id(0) == 0)
        def _():
            s_ref[...] = jnp.zeros_like(s_ref)

        ss = [s_ref[h] for h in range(HEADS)]
        sb = _each(_mx, ss)
        vn = [u_ref[:, sl] - _dot(w_ref[:, sl], b) for sl, b in zip(sls, sb)]
        vnb = _each(_mx, vn)
        oa = [_dot(qg_ref[:, sl], b) for sl, b in zip(sls, sb)]
        ob = [_dot(attn_ref[:, sl], b) for sl, b in zip(sls, vnb)]
        upd = [_dot_tn(kg_ref[:, sl], b) for sl, b in zip(sls, vnb)]
        for h, sl in enumerate(sls):
            st_ref[0, h] = ss[h]
            vn_ref[:, sl] = vnb[h]
            o_ref[:, sl] = oa[h] + ob[h]
            s_ref[h] = ss[h] * egl_ref[0, h:h + 1, :] + upd[h]

    wide = _row(c, DN_WIDTH)
    return _pc(body, "dn_scan_fwd", (nt,),
               [wide] * 5 + [pl.BlockSpec((1, HEADS, HEAD_DIM), lambda i: (i, 0, 0))],
               [wide, wide, pl.BlockSpec((1, HEADS, HEAD_DIM, HEAD_DIM), lambda i: (i, 0, 0, 0))],
               [SDS((t, DN_WIDTH), F32), SDS((t, DN_WIDTH), MXU_DTYPE), SDS((nt, HEADS, HEAD_DIM, HEAD_DIM), F32)],
               scratch=[pltpu.VMEM((HEADS, HEAD_DIM, HEAD_DIM), F32)], sem=("arbitrary",))(u, w, qg, kg, attn, egl)


def _dn_scan_bwd(do, qg, kg, w, attn, vn, states, egl):
    t = do.shape[0]
    nt = t // CHUNK
    c = CHUNK
    sls = [slice(h * HEAD_DIM, (h + 1) * HEAD_DIM) for h in range(HEADS)]

    def body(do_ref, qg_ref, kg_ref, w_ref, attn_ref, vn_ref, st_ref, egl_ref,
             dvn_ref, dkg_ref, dqg_ref, dattn_ref, dw_ref, degl_ref, ds_ref):
        @pl.when(pl.program_id(0) == 0)
        def _():
            ds_ref[...] = jnp.zeros_like(ds_ref)

        dsp = [ds_ref[h] for h in range(HEADS)]
        dsb = _each(_mx, dsp)
        ss = [st_ref[0, h] for h in range(HEADS)]
        sb = _each(_mx, ss)
        dvn = [_dot(kg_ref[:, sl], b) + _dot_tn(attn_ref[:, sl], do_ref[:, sl]) for sl, b in zip(sls, dsb)]
        dvnb = _each(_mx, dvn)
        dkg = [_dot_nt(vn_ref[:, sl], b) for sl, b in zip(sls, dsb)]
        dqg = [_dot_nt(do_ref[:, sl], b) for sl, b in zip(sls, sb)]
        dattn = [_dot_nt(do_ref[:, sl], vn_ref[:, sl]) for sl in sls]
        dwv = [-_dot_nt(a, b) for a, b in zip(dvnb, sb)]
        upd = [_dot_tn(qg_ref[:, sl], do_ref[:, sl]) - _dot_tn(w_ref[:, sl], a) for sl, a in zip(sls, dvnb)]
        for h, sl in enumerate(sls):
            dvn_ref[:, sl] = dvn[h]
            dkg_ref[:, sl] = dkg[h]
            dqg_ref[:, sl] = dqg[h]
            dattn_ref[:, sl] = dattn[h]
            dw_ref[:, sl] = dwv[h]
            degl = jnp.sum(_rowsum(ss[h] * dsp[h]), axis=0, keepdims=True)
            degl_ref[0, h:h + 1, :] = jnp.broadcast_to(degl, (1, HEAD_DIM))
            ds_ref[h] = dsp[h] * egl_ref[0, h:h + 1, :] + upd[h]

    rev = pl.BlockSpec((c, DN_WIDTH), lambda i: (nt - 1 - i, 0))
    rev3 = pl.BlockSpec((1, HEADS, HEAD_DIM), lambda i: (nt - 1 - i, 0, 0))
    rev4 = pl.BlockSpec((1, HEADS, HEAD_DIM, HEAD_DIM), lambda i: (nt - 1 - i, 0, 0, 0))
    return _pc(body, "dn_scan_bwd", (nt,), [rev] * 6 + [rev4, rev3], [rev] * 5 + [rev3],
               [SDS((t, DN_WIDTH), F32)] * 5 + [SDS((nt, HEADS, HEAD_DIM), F32)],
               scratch=[pltpu.VMEM((HEADS, HEAD_DIM, HEAD_DIM), F32)],
               sem=("arbitrary",))(do, qg, kg, w, attn, vn, states, egl)


def _dn_local_bwd(qkv_act, bg, bgt, u, w, ymat, dvn, dw, dqg, dkg, dattn, degl):
    t = qkv_act.shape[0]
    nt = t // CHUNK
    c = CHUNK

    def body(qkv_ref, bg_ref, bgt_ref, u_ref, w_ref, y_ref, du_ref, dw_ref, dqg_ref, dkg_ref, dattn_ref,
             degl_ref, dqkv_ref, dbg_ref):
        bgv = bg_ref[...]
        lane = _lane(bgv.shape)
        rowi = lax.broadcasted_iota(jnp.int32, (c, 1), 0)

        def group(gi, dbg):
            hs, offs = _head_offsets(gi)
            qo, ko, vo = offs[0:HEAD_GROUP], offs[HEAD_GROUP:2 * HEAD_GROUP], offs[2 * HEAD_GROUP:]
            qs = [qkv_ref[:, o] for o in qo]
            ks = [qkv_ref[:, o] for o in ko]
            vs = [qkv_ref[:, o] for o in vo]
            g_rows = [bgt_ref[pl.ds(HEADS + h, 1), :] for h in hs]
            ct = _chunk_terms(qs, ks, bgv, g_rows, hs)
            ii, jj = ct["ii"], ct["jj"]
            beta, eg, ek, kb, kn, qn, dm = ct["beta"], ct["eg"], ct["ek"], ct["kb"], ct["kn"], ct["qn"], ct["dm"]
            ys = [y_ref[:, o] for o in qo]
            du = [du_ref[:, o] for o in qo]
            dwv = [dw_ref[:, o] for o in qo]
            dqg_v = [dqg_ref[:, o] for o in qo]
            dkg_v = [dkg_ref[:, o] for o in qo]
            dattn_v = [dattn_ref[:, o] for o in qo]
            degl_v = [jnp.max(degl_ref[0, pl.ds(h, 1), :], axis=1, keepdims=True) for h in hs]
            dvb = _each(lambda a, y: a + _dot_tn(y, a), du, ys)
            dkbe = _each(lambda a, y: a + _dot_tn(y, a), dwv, ys)
            dm_u = [_dot_nt(a, u_ref[:, o]) for a, o in zip(dvb, qo)]
            dm_w = [_dot_nt(a, w_ref[:, o]) for a, o in zip(dkbe, qo)]
            dms = _each(lambda a, b: jnp.where(ii > jj, -(a + b), 0.0), dm_u, dm_w)
            dkk = _each(jnp.multiply, dms, dm)
            dqk = _each(jnp.multiply, dattn_v, dm)
            gmat = _each(lambda a, b, c_, d: a * b + c_ * d, dms, ct["m"], dattn_v, ct["attn"])
            dkb = _each(lambda a, b, c_, d: _dot(a, b) + c_ * d, dkk, kn, dkbe, eg)
            dk1 = _each(_dot_tn, dkk, kb)
            dk2 = _each(_dot_tn, dqk, qn)
            dq1 = _each(_dot, dqk, kn)
            dk = _each(lambda a, b, c_, d: a + b + c_ * d, dk1, dk2, dkg_v, ek)
            dq = _each(lambda a, b, c_: a + b * c_, dq1, dqg_v, eg)
            deg = _each(lambda a, b, c_, d: _rowsum(a * b) + _rowsum(c_ * d), dqg_v, qn, dkbe, kb)
            dek = _each(lambda a, b: _rowsum(a * b), dkg_v, kn)
            dgl = _each(lambda a, b, c_, d: jnp.sum(a * b, axis=0, keepdims=True) + c_ * d, dek, ek, degl_v, ct["egl"])
            cs_row = _each(lambda g: jnp.sum(g, axis=0, keepdims=True), gmat)
            cs_col = _each(lambda r: _rowsum(jnp.where(ii == jj, r, 0.0)), cs_row)
            dgc = _each(lambda a, b, c_, d, g, e, f: a * b - c_ * d + _rowsum(g) - e + jnp.where(rowi == c - 1, f, 0.0),
                        deg, eg, dek, ek, gmat, cs_col, dgl)
            dgc_row = _each(lambda a: jnp.sum(jnp.where(ii == jj, a, 0.0), axis=0, keepdims=True), dgc)
            dg = _each(lambda r: _rowsum(jnp.where(jj >= ii, r, 0.0)), dgc_row)
            dbeta = _each(lambda a, b, c_, d: _rowsum(a * b) + _rowsum(c_ * d), dkb, kn, dvb, vs)
            dk = _each(lambda a, b, c_: a + b * c_, dk, dkb, beta)
            for a in range(HEAD_GROUP):
                dyq = dq[a] * Q_SCALE
                yq = ct["yq"][a]
                dqkv_ref[:, qo[a]] = ct["rq"][a] * (dyq - yq * _rowsum(yq * dyq))
                dqkv_ref[:, ko[a]] = ct["rk"][a] * (dk[a] - kn[a] * _rowsum(kn[a] * dk[a]))
                dqkv_ref[:, vo[a]] = dvb[a] * beta[a]
                dbg = dbg + jnp.where(lane == hs[a], dbeta[a], 0.0) + jnp.where(lane == HEADS + hs[a], dg[a], 0.0)
            return dbg

        dbg_ref[...] = lax.fori_loop(0, HEADS // HEAD_GROUP, group, jnp.zeros((c, 128), F32))

    wide = _row(c, DN_WIDTH)
    sc3 = pl.BlockSpec((1, HEADS, HEAD_DIM), lambda i: (i, 0, 0))
    return _pc(body, "dn_local_bwd", (nt,),
               [_row(c, QKV_WIDTH), _row(c, 128), pl.BlockSpec((2 * HEADS, c), lambda i: (0, i))] + [wide] * 8 + [sc3],
               [_row(c, QKV_WIDTH), _row(c, 128)], [SDS((t, QKV_WIDTH), F32), SDS((t, 128), F32)],
               sem=("parallel",))(qkv_act, bg, bgt, u, w, ymat, dvn, dw, dqg, dkg, dattn, degl)


def _mix_fwd(o, proj, ypre, pool_scale, wo_row, tm):
    t = o.shape[0]

    def body(o_ref, z_ref, ga_ref, gb_ref, yp_ref, ps_ref, wo_ref, mixed_ref):
        for h in range(HEADS):
            sl = slice(h * HEAD_DIM, (h + 1) * HEAD_DIM)
            oh = o_ref[:, sl]
            on = oh * lax.rsqrt(jnp.mean(oh * oh, axis=1, keepdims=True) + RMS_EPS)
            zh = z_ref[:, sl]
            yb = on * wo_ref[:, sl] * (zh * _sigmoid(zh))
            ya = yp_ref[:, sl] * ps_ref[:, sl]
            mixed_ref[:, sl] = _mx(_sigmoid(ga_ref[:, sl]) * ya + _sigmoid(gb_ref[:, sl]) * yb)

    def col(blk):
        return pl.BlockSpec((tm, D_MODEL), lambda i: (i, blk))

    return _pc(body, "mix_fwd", (t // tm,),
               [_row(tm, D_MODEL), col(K_Z // D_MODEL), col(K_GA // D_MODEL), col(K_GB // D_MODEL), _row(tm, D_MODEL),
                _const((1, D_MODEL)), _const((1, D_MODEL))],
               _row(tm, D_MODEL), SDS((t, D_MODEL), MXU_DTYPE), sem=("parallel",))(
                   o, proj, proj, proj, ypre, pool_scale, wo_row)


def _mix_bwd(da1_bf, w_out, o, proj, ypre, pool_scale, wo_row, tm):
    t = o.shape[0]

    def body(da_ref, wout_ref, o_ref, z_ref, ga_ref, gb_ref, yp_ref, ps_ref, wo_ref,
             do_ref, dz_ref, dga_ref, dgb_ref, dyp_ref, acc_ref):
        i = pl.program_id(0)

        @pl.when(i == 0)
        def _():
            acc_ref[...] = jnp.zeros_like(acc_ref)

        dmixed = _dot_nt(da_ref[...], wout_ref[...])
        for h in range(HEADS):
            sl = slice(h * HEAD_DIM, (h + 1) * HEAD_DIM)
            oh = o_ref[:, sl]
            rs = lax.rsqrt(jnp.mean(oh * oh, axis=1, keepdims=True) + RMS_EPS)
            on = oh * rs
            zh = z_ref[:, sl]
            sz = _sigmoid(zh)
            silu = zh * sz
            woh = wo_ref[:, sl]
            t1 = on * woh
            yb = t1 * silu
            sa = _sigmoid(ga_ref[:, sl])
            sb = _sigmoid(gb_ref[:, sl])
            yp = yp_ref[:, sl]
            psh = ps_ref[:, sl]
            dm = dmixed[:, sl]
            dga_ref[:, sl] = _mx(dm * (yp * psh) * sa * (1.0 - sa))
            dgb_ref[:, sl] = _mx(dm * yb * sb * (1.0 - sb))
            dya = dm * sa
            dyb = dm * sb
            dyp_ref[:, sl] = _mx(dya * psh)
            acc_ref[0:1, sl] += jnp.sum(dya * yp, axis=0, keepdims=True)
            dz_ref[:, sl] = _mx(dyb * t1 * (sz * (1.0 + zh * (1.0 - sz))))
            dt1 = dyb * silu
            acc_ref[1:2, 0:HEAD_DIM] += jnp.sum(dt1 * on, axis=0, keepdims=True)
            don = dt1 * woh
            do_ref[:, sl] = _mx(rs * (don - on * jnp.mean(don * on, axis=1, keepdims=True)))

    def col(blk):
        return pl.BlockSpec((tm, D_MODEL), lambda i: (i, blk))

    r = _row(tm, D_MODEL)
    return _pc(body, "mix_bwd", (t // tm,),
               [r, _const((D_MODEL, D_MODEL)), r, col(K_Z // D_MODEL), col(K_GA // D_MODEL), col(K_GB // D_MODEL), r,
                _const((1, D_MODEL)), _const((1, D_MODEL))],
               [r, r, r, r, r, _const((8, D_MODEL))],
               [SDS((t, D_MODEL), MXU_DTYPE)] * 5 + [SDS((8, D_MODEL), F32)],
               sem=("arbitrary",))(da1_bf, w_out, o, proj, proj, proj, ypre, pool_scale, wo_row)


def _oproj_ln1(mixed, w_out, h0, g1, b1, tm):
    t = mixed.shape[0]

    def body(m_ref, w_ref, h0_ref, g_ref, b_ref, a1_ref, h1_ref, h1b_ref):
        a1 = ALPHA * h0_ref[...] + _dot(m_ref[...], w_ref[...])
        a1_ref[...] = a1
        xhat, _ = _ln_stats(a1)
        h1 = xhat * g_ref[...] + b_ref[...]
        h1_ref[...] = h1
        h1b_ref[...] = _mx(h1)

    r = _row(tm, D_MODEL)
    v = _const((1, D_MODEL))
    return _pc(body, "oproj_ln1", (t // tm,), [r, _const((D_MODEL, D_MODEL)), r, v, v], [r, r, r],
               [SDS((t, D_MODEL), F32), SDS((t, D_MODEL), F32), SDS((t, D_MODEL), MXU_DTYPE)],
               sem=("parallel",))(mixed, w_out, h0, g1, b1)


def _mlp_up(h1_bf, w_up, tm, tn):
    t = h1_bf.shape[0]

    def body(h_ref, w_ref, up_ref, act_ref):
        up = _dot(h_ref[...], w_ref[...])
        up_ref[...] = up
        r = jnp.maximum(up, 0.0)
        act_ref[...] = _mx(r * r)

    o = pl.BlockSpec((tm, tn), lambda i, j: (i, j))
    return _pc(body, "mlp_up", (t // tm, D_FF // tn),
               [pl.BlockSpec((tm, D_MODEL), lambda i, j: (i, 0)), pl.BlockSpec((D_MODEL, tn), lambda i, j: (0, j))],
               [o, o], [SDS((t, D_FF), F32), SDS((t, D_FF), MXU_DTYPE)], sem=("parallel", "parallel"))(h1_bf, w_up)


def _tail(act, w_down, h1, w_gate, p_bf, w_proj, tgt, g2, b2, tm):
    t = act.shape[0]

    def body(act_ref, wd_ref, h1_ref, wg_ref, p_ref, wp_ref, tgt_ref, g_ref, b_ref,
             dr_ref, drb_ref, dgp_ref, dpp_ref, rb_ref, acc_ref):
        i = pl.program_id(0)

        @pl.when(i == 0)
        def _():
            acc_ref[...] = jnp.zeros_like(acc_ref)

        r = ALPHA * h1_ref[...] + _dot(act_ref[...], wd_ref[...])
        rb = _mx(r)
        rb_ref[...] = rb
        gate = _sigmoid(_dot(rb, wg_ref[...]))
        pp = _dot(p_ref[...], wp_ref[...])
        xhat, rstd = _ln_stats(r + gate * pp)
        g = g_ref[...]
        diff = xhat * g + b_ref[...] - tgt_ref[...]
        dh2 = diff * (1.0 / D_MODEL)
        rowloss = jnp.sum(diff * diff, axis=1, keepdims=True) * (0.5 / D_MODEL)
        acc_ref[0:1, :] += jnp.sum(dh2 * xhat, axis=0, keepdims=True)
        acc_ref[1:2, :] += jnp.sum(dh2, axis=0, keepdims=True)
        acc_ref[2:3, :] += jnp.broadcast_to(jnp.sum(rowloss, axis=0, keepdims=True), (1, D_MODEL))
        da2 = _ln_bwd(dh2, xhat, rstd, g)
        dpp_ref[...] = _mx(da2 * gate)
        dgp = _mx(da2 * pp * gate * (1.0 - gate))
        dgp_ref[...] = dgp
        dr = da2 + _dot_nt(dgp, wg_ref[...])
        dr_ref[...] = dr
        drb_ref[...] = _mx(dr)

    r = _row(tm, D_MODEL)
    v = _const((1, D_MODEL))
    return _pc(body, "tail", (t // tm,),
               [_row(tm, D_FF), _const((D_FF, D_MODEL)), r, _const((D_MODEL, D_MODEL)), _row(tm, PLE_DIM),
                _const((PLE_DIM, D_MODEL)), r, v, v],
               [r, r, r, r, r, _const((8, D_MODEL))],
               [SDS((t, D_MODEL), F32)] + [SDS((t, D_MODEL), MXU_DTYPE)] * 4 + [SDS((8, D_MODEL), F32)],
               sem=("arbitrary",))(act, w_down, h1, w_gate, p_bf, w_proj, tgt, g2, b2)


def _mlp_bwd1(dr_bf, w_down, up, tm, tn):
    t = up.shape[0]

    def body(dr_ref, w_ref, up_ref, dup_ref):
        dact = _dot_nt(dr_ref[...], w_ref[...])
        dup_ref[...] = _mx(dact * (2.0 * jnp.maximum(up_ref[...], 0.0)))

    o = pl.BlockSpec((tm, tn), lambda i, j: (i, j))
    return _pc(body, "mlp_bwd1", (t // tm, D_FF // tn),
               [pl.BlockSpec((tm, D_MODEL), lambda i, j: (i, 0)), pl.BlockSpec((tn, D_MODEL), lambda i, j: (j, 0)), o],
               o, SDS((t, D_FF), MXU_DTYPE), sem=("parallel", "parallel"))(dr_bf, w_down, up)


def _mlp_bwd2(dup, w_up, dr, a1, g1, tm):
    t = dr.shape[0]

    def body(dup_ref, w_ref, dr_ref, a1_ref, g_ref, da1_ref, da1b_ref, acc_ref):
        i = pl.program_id(0)

        @pl.when(i == 0)
        def _():
            acc_ref[...] = jnp.zeros_like(acc_ref)

        dh1 = ALPHA * dr_ref[...] + _dot_nt(dup_ref[...], w_ref[...])
        xhat, rstd = _ln_stats(a1_ref[...])
        acc_ref[0:1, :] += jnp.sum(dh1 * xhat, axis=0, keepdims=True)
        acc_ref[1:2, :] += jnp.sum(dh1, axis=0, keepdims=True)
        da1 = _ln_bwd(dh1, xhat, rstd, g_ref[...])
        da1_ref[...] = da1
        da1b_ref[...] = _mx(da1)

    r = _row(tm, D_MODEL)
    return _pc(body, "mlp_bwd2", (t // tm,),
               [_row(tm, D_FF), _const((D_MODEL, D_FF)), r, r, _const((1, D_MODEL))],
               [r, r, _const((8, D_MODEL))],
               [SDS((t, D_MODEL), F32), SDS((t, D_MODEL), MXU_DTYPE), SDS((8, D_MODEL), F32)],
               sem=("arbitrary",))(dup, w_up, dr, a1, g1)


def _ln_in_bwd(dproj, w_cat, da1, x, g, tm, tk):
    t = x.shape[0]
    nk = CAT_WIDTH // tk

    def body(dp_ref, w_ref, da1_ref, x_ref, g_ref, dx_ref, acc_ref, mm_ref):
        i, kk = pl.program_id(0), pl.program_id(1)

        @pl.when((i == 0) & (kk == 0))
        def _():
            acc_ref[...] = jnp.zeros_like(acc_ref)

        @pl.when(kk == 0)
        def _():
            mm_ref[...] = jnp.zeros_like(mm_ref)

        mm_ref[...] += _dot_nt(dp_ref[...], w_ref[...])

        @pl.when(kk == nk - 1)
        def _():
            dh0 = mm_ref[...] + ALPHA * da1_ref[...]
            xhat, rstd = _ln_stats(x_ref[...])
            acc_ref[0:1, :] += jnp.sum(dh0 * xhat, axis=0, keepdims=True)
            acc_ref[1:2, :] += jnp.sum(dh0, axis=0, keepdims=True)
            dx_ref[...] = _ln_bwd(dh0, xhat, rstd, g_ref[...])

    r = pl.BlockSpec((tm, D_MODEL), lambda i, kk: (i, 0))
    return _pc(body, "ln_in_bwd", (t // tm, nk),
               [pl.BlockSpec((tm, tk), lambda i, kk: (i, kk)), pl.BlockSpec((D_MODEL, tk), lambda i, kk: (0, kk)),
                r, r, pl.BlockSpec((1, D_MODEL), lambda i, kk: (0, 0))],
               [r, pl.BlockSpec((8, D_MODEL), lambda i, kk: (0, 0))],
               [SDS((t, D_MODEL), F32), SDS((8, D_MODEL), F32)],
               scratch=[pltpu.VMEM((tm, D_MODEL), F32)], sem=("arbitrary", "arbitrary"))(dproj, w_cat, da1, x, g)


def _local_step(x, p, tgt, wts):
    t = x.shape[0]
    tm = min(512, t)
    tms = min(256, t)
    row = lambda a: a.reshape(1, -1)
    w_cat = wts["w_cat"]
    pool_scale = row(wts["pool_scale"])
    wo_row = jnp.tile(row(wts["o_norm_w"]), (1, HEADS))
    pad8 = jnp.zeros((1, HEADS), F32)
    al_row = jnp.concatenate([pad8, row(wts["a_log"]), jnp.zeros((1, 128 - 2 * HEADS), F32)], axis=1)
    dtb_row = jnp.concatenate([pad8, row(wts["dt_bias"]), jnp.zeros((1, 128 - 2 * HEADS), F32)], axis=1)
    g_in, b_in = row(wts["ln_in_g"]), row(wts["ln_in_b"])
    g1, b1 = row(wts["ln1_g"]), row(wts["ln1_b"])
    g2, b2 = row(wts["ln2_g"]), row(wts["ln2_b"])

    h0, h0_bf = _ln_in(x, g_in, b_in, tm)
    proj = _matmul(h0_bf, w_cat, "nn", "proj", F32, tm=512, tn=1152, tk=1024)
    ypre, d_bf = _pool_fwd(proj, wts["pool_w"], tm)
    qkv_act = _conv_fwd(proj, wts["conv_w"], tm)
    bg = _ba_fwd(proj, al_row, dtb_row, tm)
    bgt = bg[:, :2 * HEADS].T
    u, w, qg, kg, attn, ymat, egl = _dn_local_fwd(qkv_act, bg, bgt)
    o, vn, states = _dn_scan_fwd(u, w, qg, kg, attn, egl)
    mixed = _mix_fwd(o, proj, ypre, pool_scale, wo_row, tm)
    a1, h1, h1_bf = _oproj_ln1(mixed, wts["w_out"], h0, g1, b1, tm)
    up, act = _mlp_up(h1_bf, wts["w_up"], tm, 1024)
    p_bf = _mx(p)
    dr, dr_bf, dgp, dpp, r_bf, acc_tail = _tail(act, wts["w_down"], h1, wts["ple_gate_w"], p_bf, wts["ple_proj_w"],
                                                tgt, g2, b2, tms)
    grads = {}
    grads["ple_proj_w"] = _matmul(p_bf, dpp, "tn", "dw_ple_proj", F32, tm=256, tn=1024, tk=512)
    grads["ple_gate_w"] = _matmul(r_bf, dgp, "tn", "dw_ple_gate", F32, tm=512, tn=1024, tk=512)
    grads["w_down"] = _matmul(act, dr_bf, "tn", "dw_down", F32, tm=512, tn=1024, tk=512)
    dup = _mlp_bwd1(dr_bf, wts["w_down"], up, tm, 1024)
    grads["w_up"] = _matmul(h1_bf, dup, "tn", "dw_up", F32, tm=512, tn=1024, tk=512)
    da1, da1_bf, acc_ln1 = _mlp_bwd2(dup, wts["w_up"], dr, a1, g1, tms)
    grads["w_out"] = _matmul(mixed, da1_bf, "tn", "dw_out", F32, tm=512, tn=1024, tk=512)
    do, dz, dga, dgb, dyp, acc_mix = _mix_bwd(da1_bf, wts["w_out"], o, proj, ypre, pool_scale, wo_row, tms)
    du_pool, grads["pool_w"] = _pool_bwd(dyp, d_bf, wts["pool_w"], tm)
    dvn, dkg, dqg, dattn, dw, degl = _dn_scan_bwd(do, qg, kg, w, attn, vn, states, egl)
    dqkv_act, dbg = _dn_local_bwd(qkv_act, bg, bgt, u, w, ymat, dvn, dw, dqg, dkg, dattn, degl)
    dqkv, acc_conv = _conv_bwd(dqkv_act, proj, wts["conv_w"], tm)
    dba, acc_ba = _ba_bwd(dbg, bg, proj, al_row, dtb_row, tm)
    dproj = jnp.concatenate([dqkv, dz, dga, dgb, du_pool, dba,
                             jnp.zeros((t, CAT_WIDTH - K_BA - 128), MXU_DTYPE)], axis=1)
    dw_cat = _matmul(h0_bf, dproj, "tn", "dw_in", F32, tm=512, tn=1152, tk=512)
    grad_x, acc_in = _ln_in_bwd(dproj, w_cat, da1, x, g_in, tms, 1152)

    grads["w_in"] = jnp.concatenate(
        [dw_cat[:, K_U:K_U + 512], dw_cat[:, K_QKV:K_QKV + 3072], dw_cat[:, K_Z:K_Z + 1024],
         dw_cat[:, K_BA:K_BA + 16], dw_cat[:, K_GA:K_GA + 1024], dw_cat[:, K_GB:K_GB + 1024]], axis=1)
    grads["conv_w"] = acc_conv[0:CONV_K]
    grads["ln_in_g"], grads["ln_in_b"] = acc_in[0], acc_in[1]
    grads["ln1_g"], grads["ln1_b"] = acc_ln1[0], acc_ln1[1]
    grads["ln2_g"], grads["ln2_b"] = acc_tail[0], acc_tail[1]
    grads["pool_scale"] = acc_mix[0]
    grads["o_norm_w"] = acc_mix[1, 0:HEAD_DIM]
    grads["a_log"] = acc_ba[0, HEADS:2 * HEADS]
    grads["dt_bias"] = acc_ba[1, HEADS:2 * HEADS]
    loss = acc_tail[2, 0]
    return grad_x, grads, loss


MESH = pl.DeviceIdType.MESH
ANY = pl.BlockSpec(memory_space=pl.ANY)


def _chip_of(k, x, y):
    chip = (2 * x + y + k) % N_CHIPS
    return chip // 2, chip % 2


def _gather_weights(packed, conv_shard):
    rows = packed.shape[0]

    def body(w_ref, c_ref, wout_ref, cout_ref, send_sems, recv_sems, local_sems):
        x, y, c = lax.axis_index("x"), lax.axis_index("y"), lax.axis_index("c")
        me = 2 * x + y
        own_w = pltpu.make_async_copy(w_ref, wout_ref.at[me], local_sems.at[0])
        own_c = pltpu.make_async_copy(c_ref, cout_ref.at[me], local_sems.at[1])
        own_w.start()
        own_c.start()
        copies = []
        for k in range(1, N_CHIPS):
            tx, ty = _chip_of(k, x, y)
            copies.append(pltpu.make_async_remote_copy(
                src_ref=w_ref, dst_ref=wout_ref.at[me], send_sem=send_sems.at[2 * k], recv_sem=recv_sems.at[2 * k],
                device_id=(tx, ty, c), device_id_type=MESH))
            copies.append(pltpu.make_async_remote_copy(
                src_ref=c_ref, dst_ref=cout_ref.at[me], send_sem=send_sems.at[2 * k + 1],
                recv_sem=recv_sems.at[2 * k + 1], device_id=(tx, ty, c), device_id_type=MESH))
        for cp in copies:
            cp.start()
        for k in range(1, N_CHIPS):
            src = (me + N_CHIPS - k) % N_CHIPS
            pltpu.make_async_remote_copy(
                src_ref=w_ref, dst_ref=wout_ref.at[src], send_sem=send_sems.at[2 * k], recv_sem=recv_sems.at[2 * k],
                device_id=(x, y, c), device_id_type=MESH).wait_recv()
            pltpu.make_async_remote_copy(
                src_ref=c_ref, dst_ref=cout_ref.at[src], send_sem=send_sems.at[2 * k + 1],
                recv_sem=recv_sems.at[2 * k + 1], device_id=(x, y, c), device_id_type=MESH).wait_recv()
        for cp in copies:
            cp.wait_send()
        own_w.wait()
        own_c.wait()

    return pl.pallas_call(
        body, name="gather_weights",
        out_shape=[SDS((N_CHIPS, rows, D_MODEL), packed.dtype), SDS((N_CHIPS,) + conv_shard.shape, conv_shard.dtype)],
        in_specs=[ANY, ANY], out_specs=[ANY, ANY],
        scratch_shapes=[pltpu.SemaphoreType.DMA((2 * N_CHIPS,)), pltpu.SemaphoreType.DMA((2 * N_CHIPS,)),
                        pltpu.SemaphoreType.DMA((2,))],
    )(packed, conv_shard)


def _scatter_grads(gpk):
    rows = gpk.shape[1]

    def body(g_ref, out_ref, send_sems, recv_sems, local_sem):
        x, y, c = lax.axis_index("x"), lax.axis_index("y"), lax.axis_index("c")
        me = 2 * x + y
        own = pltpu.make_async_copy(g_ref.at[me], out_ref.at[me], local_sem)
        own.start()
        copies = []
        for k in range(1, N_CHIPS):
            tx, ty = _chip_of(k, x, y)
            copies.append(pltpu.make_async_remote_copy(
                src_ref=g_ref.at[2 * tx + ty], dst_ref=out_ref.at[me], send_sem=send_sems.at[k],
                recv_sem=recv_sems.at[k], device_id=(tx, ty, c), device_id_type=MESH))
        for cp in copies:
            cp.start()
        for k in range(1, N_CHIPS):
            src = (me + N_CHIPS - k) % N_CHIPS
            pltpu.make_async_remote_copy(
                src_ref=g_ref.at[me], dst_ref=out_ref.at[src], send_sem=send_sems.at[k], recv_sem=recv_sems.at[k],
                device_id=(x, y, c), device_id_type=MESH).wait_recv()
        for cp in copies:
            cp.wait_send()
        own.wait()

    return pl.pallas_call(
        body, name="scatter_grads", out_shape=SDS((N_CHIPS, rows, D_MODEL), gpk.dtype),
        in_specs=[ANY], out_specs=ANY,
        scratch_shapes=[pltpu.SemaphoreType.DMA((N_CHIPS,)), pltpu.SemaphoreType.DMA((N_CHIPS,)),
                        pltpu.SemaphoreType.DMA],
    )(gpk)


def _sum_slices(recv):
    rows = recv.shape[1]

    def body(r_ref, o_ref):
        acc = r_ref[0].astype(F32)
        for j in range(1, N_CHIPS):
            acc = acc + r_ref[j].astype(F32)
        o_ref[...] = acc

    return _pc(body, "sum_slices", (rows // PACK_TILE,),
               [pl.BlockSpec((N_CHIPS, PACK_TILE, D_MODEL), lambda i: (0, i, 0))], _row(PACK_TILE, D_MODEL),
               SDS((rows, D_MODEL), F32), sem=("parallel",))(recv)


def _swap_with_sibling(part):
    def body(p_ref, o_ref, send_sem, recv_sem):
        x, y, c = lax.axis_index("x"), lax.axis_index("y"), lax.axis_index("c")
        cp = pltpu.make_async_remote_copy(src_ref=p_ref, dst_ref=o_ref, send_sem=send_sem, recv_sem=recv_sem,
                                          device_id=(x, y, 1 - c), device_id_type=MESH)
        cp.start()
        cp.wait()

    return pl.pallas_call(
        body, name="swap_sibling", out_shape=SDS(part.shape, part.dtype), in_specs=[ANY], out_specs=ANY,
        scratch_shapes=[pltpu.SemaphoreType.DMA, pltpu.SemaphoreType.DMA],
    )(part)


def _adamw_math(w, g, m, v):
    m = ADAM_B1 * m + (1.0 - ADAM_B1) * g
    v = ADAM_B2 * v + (1.0 - ADAM_B2) * (g * g)
    m_hat = m / (1.0 - ADAM_B1 ** ADAM_STEP)
    v_hat = v / (1.0 - ADAM_B2 ** ADAM_STEP)
    delta = -ADAM_LR * (m_hat / (jnp.sqrt(v_hat) + ADAM_EPS) + ADAM_WD * w)
    return delta, m, v


def _adamw_packed(w, ga, gb, m, v):
    rows = w.shape[0]

    def body(w_ref, ga_ref, gb_ref, m_ref, v_ref, g_out, d_out, m_out, v_out):
        g = ga_ref[...] + gb_ref[...]
        delta, mn, vn = _adamw_math(w_ref[...], g, m_ref[...], v_ref[...])
        g_out[...] = g
        d_out[...] = delta
        m_out[...] = mn
        v_out[...] = vn

    r = _row(PACK_TILE, D_MODEL)
    return _pc(body, "adamw_packed", (rows // PACK_TILE,), [r] * 5, [r] * 4, [SDS((rows, D_MODEL), F32)] * 4,
               sem=("parallel",))(w, ga, gb, m, v)


def _small_allreduce_adamw(mine, w, m, v):
    shape = mine.shape

    def body(mine_ref, w_ref, m_ref, v_ref, g_out, d_out, m_out, v_out, buf_ref, send_sems, recv_sems):
        x, y, c = lax.axis_index("x"), lax.axis_index("y"), lax.axis_index("c")
        me = 4 * x + 2 * y + c
        buf_ref[me] = mine_ref[...]
        copies = []
        for k in range(1, N_DEV):
            tgt = (me + k) % N_DEV
            copies.append(pltpu.make_async_remote_copy(
                src_ref=mine_ref, dst_ref=buf_ref.at[me], send_sem=send_sems.at[k], recv_sem=recv_sems.at[k],
                device_id=(tgt // 4, (tgt // 2) % 2, tgt % 2), device_id_type=MESH))
        for cp in copies:
            cp.start()
        for k in range(1, N_DEV):
            src = (me + N_DEV - k) % N_DEV
            pltpu.make_async_remote_copy(
                src_ref=mine_ref, dst_ref=buf_ref.at[src], send_sem=send_sems.at[k], recv_sem=recv_sems.at[k],
                device_id=(x, y, c), device_id_type=MESH).wait_recv()
        for cp in copies:
            cp.wait_send()
        g = buf_ref[0]
        for j in range(1, N_DEV):
            g = g + buf_ref[j]
        delta, mn, vn = _adamw_math(w_ref[...], g, m_ref[...], v_ref[...])
        g_out[...] = g
        d_out[...] = delta
        m_out[...] = mn
        v_out[...] = vn

    vm = pl.BlockSpec(memory_space=pltpu.VMEM)
    return pl.pallas_call(
        body, name="small_allreduce_adamw", out_shape=[SDS(shape, F32)] * 4, in_specs=[vm] * 4, out_specs=[vm] * 4,
        scratch_shapes=[pltpu.VMEM((N_DEV,) + shape, F32), pltpu.SemaphoreType.DMA((N_DEV,)),
                        pltpu.SemaphoreType.DMA((N_DEV,))],
    )(mine, w, m, v)


def _pack_shards(parts, dtype):
    rows = [parts[n].reshape(r, D_MODEL).astype(dtype) for n, r in PACK]
    used = sum(r for _, r in PACK)
    rows.append(jnp.zeros((PACK_ROWS - used, D_MODEL), dtype))
    return jnp.concatenate(rows, axis=0)


def _unpack_shards(packed, shapes):
    out, at = {}, 0
    for n, r in PACK:
        out[n] = packed[at:at + r].reshape(shapes[n])
        at += r
    return out


SHARD_AXIS = {"w_in": 1, "pool_w": 2, "w_out": 0, "w_up": 1, "w_down": 0, "ple_gate_w": 0, "ple_proj_w": 1,
              "conv_w": 1}


def _split_for_chips(name, full):
    return jnp.split(full, N_CHIPS, axis=SHARD_AXIS[name])


def _pad_row(a):
    a = a.reshape(1, -1).astype(F32)
    return jnp.pad(a, ((0, 0), (0, D_MODEL - a.shape[1])))


def kernel(x, p, ln_in_g, ln_in_b, w_in, pool_w, pool_scale, conv_w, a_log, dt_bias, o_norm_w, w_out, ln1_g, ln1_b, w_up, w_down, ple_gate_w, ple_proj_w, ln2_g, ln2_b, loss_target, m_ln_in_g, m_ln_in_b, m_w_in, m_pool_w, m_pool_scale, m_conv_w, m_a_log, m_dt_bias, m_o_norm_w, m_w_out, m_ln1_g, m_ln1_b, m_w_up, m_w_down, m_ple_gate_w, m_ple_proj_w, m_ln2_g, m_ln2_b, v_ln_in_g, v_ln_in_b, v_w_in, v_pool_w, v_pool_scale, v_conv_w, v_a_log, v_dt_bias, v_o_norm_w, v_w_out, v_ln1_g, v_ln1_b, v_w_up, v_w_down, v_ple_gate_w, v_ple_proj_w, v_ln2_g, v_ln2_b):
    given = dict(locals())
    big = [n for n, _ in PACK]
    shard_shapes = {n: given[n].shape for n in big}
    local_shapes = {n: given[n].shape[1:] for n in big}

    packed_w = _pack_shards({n: given[n] for n in big}, WIRE_DTYPE)
    conv_pad = jnp.pad(conv_w[0], ((0, 8 - CONV_K), (0, 0)))
    all_w, all_conv = _gather_weights(packed_w, conv_pad)
    full = {}
    for n in big:
        if n == "conv_w":
            continue
        shards = [_unpack_shards(all_w[j], local_shapes)[n] for j in range(N_CHIPS)]
        full[n] = jnp.concatenate(shards, axis=SHARD_AXIS[n]).astype(MXU_DTYPE)
    wi = full["w_in"]
    wts = {
        "w_cat": jnp.concatenate(
            [wi[:, C_QKV:C_Z], wi[:, C_Z:C_BETA], wi[:, C_GA:C_GB], wi[:, C_GB:IN_WIDTH], wi[:, C_POOL:C_QKV],
             wi[:, C_BETA:C_GA], jnp.zeros((D_MODEL, CAT_WIDTH - K_BA - 2 * HEADS), MXU_DTYPE)], axis=1),
        "pool_w": full["pool_w"], "w_out": full["w_out"], "w_up": full["w_up"], "w_down": full["w_down"],
        "ple_gate_w": full["ple_gate_w"], "ple_proj_w": full["ple_proj_w"],
        "conv_w": jnp.concatenate([all_conv[j, 0:CONV_K] for j in range(N_CHIPS)], axis=1),
        "ln_in_g": ln_in_g, "ln_in_b": ln_in_b, "pool_scale": pool_scale[0], "a_log": a_log[0],
        "dt_bias": dt_bias[0], "o_norm_w": o_norm_w[0], "ln1_g": ln1_g[0], "ln1_b": ln1_b[0],
        "ln2_g": ln2_g[0], "ln2_b": ln2_b[0],
    }

    grad_x, grads, loss = _local_step(x[0], p[0, 0], loss_target[0], wts)

    gpk = jnp.stack([
        _pack_shards({n: _split_for_chips(n, grads[n])[j] for n in big}, WIRE_DTYPE) for j in range(N_CHIPS)])
    part = _sum_slices(_scatter_grads(gpk))
    other = _swap_with_sibling(part)
    packed = lambda prefix: _pack_shards({n: given[prefix + n] for n in big}, F32)
    g_pk, d_pk, m_pk, v_pk = _adamw_packed(packed(""), part, other, packed("m_"), packed("v_"))
    big_out = [_unpack_shards(a, shard_shapes) for a in (g_pk, d_pk, m_pk, v_pk)]

    def small_pack(get):
        rows = [_pad_row(get(n)) for n in SMALL_NAMES]
        return rows

    mine = jnp.concatenate(small_pack(lambda n: grads[n]) + [jnp.full((1, D_MODEL), loss, F32)]
                           + [jnp.zeros((SMALL_ROWS - len(SMALL_NAMES) - 1, D_MODEL), F32)], axis=0)
    fill = [jnp.zeros((SMALL_ROWS - len(SMALL_NAMES), D_MODEL), F32)]
    sw = jnp.concatenate(small_pack(lambda n: given[n]) + fill, axis=0)
    sm = jnp.concatenate(small_pack(lambda n: given["m_" + n]) + fill, axis=0)
    sv = jnp.concatenate(small_pack(lambda n: given["v_" + n]) + fill, axis=0)
    small_out = _small_allreduce_adamw(mine, sw, sm, sv)

    def small_get(k, n):
        i = SMALL_NAMES.index(n)
        size = given[n].size
        return small_out[k][i, 0:size].reshape(given[n].shape)

    order = ["ln_in_g", "ln_in_b", "w_in", "pool_w", "pool_scale", "conv_w", "a_log", "dt_bias", "o_norm_w", "w_out",
             "ln1_g", "ln1_b", "w_up", "w_down", "ple_gate_w", "ple_proj_w", "ln2_g", "ln2_b"]
    outs = [small_out[0][len(SMALL_NAMES), 0], grad_x[None]]
    for k in range(4):
        for n in order:
            outs.append(big_out[k][n] if n in shard_shapes else small_get(k, n))
    return tuple(outs)
```

```python
import jax
import jax.numpy as jnp
from jax import lax
from jax.experimental import pallas as pl
from jax.experimental.pallas import tpu as pltpu

F32 = jnp.float32
MXU_DTYPE = jnp.bfloat16
WIRE_DTYPE = jnp.bfloat16
SDS = jax.ShapeDtypeStruct

D_MODEL = 1024
POOL_WINDOWS = (2, 4, 8, 16)
POOL_WIDTH = 512
POOL_GROUP = 128
POOL_OUT_GROUP = 256
HEADS = 8
HEAD_DIM = 128
DN_WIDTH = HEADS * HEAD_DIM
QKV_WIDTH = 3 * DN_WIDTH
CONV_K = 4
CHUNK = 128
HEAD_GROUP = 4
D_FF = 4096
PLE_DIM = 256
LN_EPS = 1e-5
RMS_EPS = 1e-6
L2_EPS = 1e-6
ALPHA = 2.0 ** 0.25
Q_SCALE = HEAD_DIM ** -0.5
IN_WIDTH = 6672
C_POOL, C_QKV, C_Z, C_BETA, C_A, C_GA, C_GB = 0, 512, 3584, 4608, 4616, 4624, 5648
K_QKV, K_Z, K_GA, K_GB, K_U, K_BA, CAT_WIDTH = 0, 3072, 4096, 5120, 6144, 6656, 6912

ADAM_LR, ADAM_B1, ADAM_B2, ADAM_EPS, ADAM_WD, ADAM_STEP = 0.001, 0.9, 0.999, 1e-08, 0.01, 10

N_CHIPS = 4
N_DEV = 8
VMEM_LIMIT = 56 * 1024 * 1024

BIG = ("w_in", "pool_w", "w_out", "w_up", "w_down", "ple_gate_w", "ple_proj_w")
SMALL_NAMES = ("ln_in_g", "ln_in_b", "pool_scale", "ln1_g", "ln1_b", "ln2_g", "ln2_b", "o_norm_w", "a_log", "dt_bias")
SMALL_CONV_AT = 12
SMALL_CONV_ROWS = CONV_K * QKV_WIDTH // D_MODEL


def _mx(a):
    return a.astype(MXU_DTYPE)


def _dot(a, b):
    return lax.dot_general(_mx(a), _mx(b), (((1,), (0,)), ((), ())), preferred_element_type=F32)


def _dot_nt(a, b):
    return lax.dot_general(_mx(a), _mx(b), (((1,), (1,)), ((), ())), preferred_element_type=F32)


def _dot_tn(a, b):
    return lax.dot_general(_mx(a), _mx(b), (((0,), (0,)), ((), ())), preferred_element_type=F32)


def _sigmoid(x):
    return 1.0 / (1.0 + jnp.exp(-x))


def _softplus(x):
    return jnp.maximum(x, 0.0) + jnp.log(1.0 + jnp.exp(-jnp.abs(x)))


def _pc(body, name, grid, in_specs, out_specs, out_shape, scratch=(), sem=None):
    return pl.pallas_call(
        body, out_shape=out_shape, grid=grid, in_specs=in_specs, out_specs=out_specs,
        scratch_shapes=scratch, name=name,
        compiler_params=pltpu.CompilerParams(dimension_semantics=sem, vmem_limit_bytes=VMEM_LIMIT))


def _row(tm, n):
    return pl.BlockSpec((tm, n), lambda i: (i, 0))


def _const(shape):
    nd = len(shape)
    return pl.BlockSpec(shape, lambda *_: (0,) * nd)


def _matmul(a, b, mode, name, out_dtype=F32, tm=512, tn=512, tk=512, stack_out=False):
    if mode == "nn":
        (m, k), n = a.shape, b.shape[1]
    elif mode == "nt":
        (m, k), n = a.shape, b.shape[0]
    else:
        (k, m), n = a.shape, b.shape[1]
    tm, tn, tk = min(tm, m), min(tn, n), min(tk, k)
    assert m % tm == 0 and n % tn == 0 and k % tk == 0, (name, m, n, k, tm, tn, tk)
    nk = k // tk
    if mode == "nn":
        a_spec = pl.BlockSpec((tm, tk), lambda i, j, kk: (i, kk))
        b_spec = pl.BlockSpec((tk, tn), lambda i, j, kk: (kk, j))
        dot = _dot
    elif mode == "nt":
        a_spec = pl.BlockSpec((tm, tk), lambda i, j, kk: (i, kk))
        b_spec = pl.BlockSpec((tn, tk), lambda i, j, kk: (j, kk))
        dot = _dot_nt
    else:
        a_spec = pl.BlockSpec((tk, tm), lambda i, j, kk: (kk, i))
        b_spec = pl.BlockSpec((tk, tn), lambda i, j, kk: (kk, j))
        dot = _dot_tn

    def body(a_ref, b_ref, o_ref, acc_ref):
        kk = pl.program_id(2)

        @pl.when(kk == 0)
        def _():
            acc_ref[...] = jnp.zeros_like(acc_ref)

        acc_ref[...] += dot(a_ref[...], b_ref[...])

        @pl.when(kk == nk - 1)
        def _():
            o_ref[...] = acc_ref[...].astype(out_dtype)

    if stack_out:
        o_spec, o_shape = pl.BlockSpec((None, tm, tn), lambda i, j, kk: (j, i, 0)), SDS((n // tn, m, tn), out_dtype)
    else:
        o_spec, o_shape = pl.BlockSpec((tm, tn), lambda i, j, kk: (i, j)), SDS((m, n), out_dtype)
    return _pc(body, name, (m // tm, n // tn, nk), [a_spec, b_spec], o_spec, o_shape,
               scratch=[pltpu.VMEM((tm, tn), F32)], sem=("parallel", "parallel", "arbitrary"))(a, b)


def _ln_stats(x):
    mu = jnp.mean(x, axis=-1, keepdims=True)
    xc = x - mu
    var = jnp.mean(xc * xc, axis=-1, keepdims=True)
    rstd = lax.rsqrt(var + LN_EPS)
    return xc * rstd, rstd


def _ln_bwd(dy, xhat, rstd, g):
    dxh = dy * g
    m1 = jnp.mean(dxh, axis=-1, keepdims=True)
    m2 = jnp.mean(dxh * xhat, axis=-1, keepdims=True)
    return rstd * (dxh - m1 - xhat * m2)


def _ln_in(x, g, b, tm):
    t, d = x.shape

    def body(x_ref, g_ref, b_ref, h_ref, hb_ref):
        xhat, _ = _ln_stats(x_ref[...])
        h = xhat * g_ref[...] + b_ref[...]
        h_ref[...] = h
        hb_ref[...] = _mx(h)

    return _pc(body, "ln_in", (t // tm,), [_row(tm, d), _const((1, d)), _const((1, d))],
               [_row(tm, d), _row(tm, d)], [SDS((t, d), F32), SDS((t, d), MXU_DTYPE)], sem=("parallel",))(x, g, b)


def _pool_fwd(proj, pool_w, tm):
    t = proj.shape[0]
    ublk = K_U // POOL_WIDTH

    def body(u_ref, halo_ref, pw_ref, ypre_ref, d_ref, ext_ref):
        i = pl.program_id(0)
        ext_ref[0:16, :] = jnp.where(i > 0, halo_ref[...], 0.0)
        ext_ref[16:16 + tm, :] = u_ref[...]
        tok = i * tm + lax.broadcasted_iota(jnp.int32, (tm, POOL_GROUP), 0)
        for gi, w in enumerate(POOL_WINDOWS):
            cs = pl.ds(gi * POOL_GROUP, POOL_GROUP)
            ug = ext_ref[pl.ds(16, tm), cs]
            s = ug
            for k in range(1, w):
                s = s + ext_ref[pl.ds(16 - k, tm), cs]
            cnt = jnp.minimum(tok + 1, w).astype(F32)
            db = _mx(s / cnt - ug)
            d_ref[:, gi * POOL_GROUP:(gi + 1) * POOL_GROUP] = db
            ypre_ref[:, gi * POOL_OUT_GROUP:(gi + 1) * POOL_OUT_GROUP] = _dot(db, pw_ref[gi])

    halo = pl.BlockSpec((16, POOL_WIDTH), lambda i: (jnp.maximum(i * (tm // 16) - 1, 0), ublk))
    return _pc(body, "pool_fwd", (t // tm,),
               [pl.BlockSpec((tm, POOL_WIDTH), lambda i: (i, ublk)), halo, _const((4, POOL_GROUP, POOL_OUT_GROUP))],
               [_row(tm, D_MODEL), _row(tm, POOL_WIDTH)],
               [SDS((t, D_MODEL), F32), SDS((t, POOL_WIDTH), MXU_DTYPE)],
               scratch=[pltpu.VMEM((16 + tm, POOL_WIDTH), F32)], sem=("parallel",))(proj, proj, pool_w)


def _pool_bwd(dyp, d_bf, pool_w, tm):
    t = dyp.shape[0]
    n = t // tm

    def body(dy_ref, dyn_ref, d_ref, pw_ref, du_ref, dpw_ref, ext_ref):
        i = pl.program_id(0)

        @pl.when(i == 0)
        def _():
            dpw_ref[...] = jnp.zeros_like(dpw_ref)

        tok = i * tm + lax.broadcasted_iota(jnp.int32, (tm + 16, POOL_GROUP), 0)
        for gi, w in enumerate(POOL_WINDOWS):
            dy = dy_ref[:, gi * POOL_OUT_GROUP:(gi + 1) * POOL_OUT_GROUP]
            dyn = dyn_ref[:, gi * POOL_OUT_GROUP:(gi + 1) * POOL_OUT_GROUP]
            pw = pw_ref[gi]
            dd = _dot_nt(dy, pw)
            ddn = jnp.where(i < n - 1, _dot_nt(dyn, pw), 0.0)
            cnt = jnp.minimum(tok + 1, w).astype(F32)
            ext_ref[0:tm, :] = dd / cnt[0:tm]
            ext_ref[tm:tm + 16, :] = ddn / cnt[tm:tm + 16]
            s = ext_ref[pl.ds(0, tm), :]
            for k in range(1, w):
                s = s + ext_ref[pl.ds(k, tm), :]
            du_ref[:, gi * POOL_GROUP:(gi + 1) * POOL_GROUP] = _mx(s - dd)
            dpw_ref[gi] += _dot_tn(d_ref[:, gi * POOL_GROUP:(gi + 1) * POOL_GROUP], dy)

    nxt = pl.BlockSpec((16, D_MODEL), lambda i: (jnp.minimum((i + 1) * (tm // 16), t // 16 - 1), 0))
    return _pc(body, "pool_bwd", (n,),
               [_row(tm, D_MODEL), nxt, _row(tm, POOL_WIDTH), _const((4, POOL_GROUP, POOL_OUT_GROUP))],
               [_row(tm, POOL_WIDTH), _const((4, POOL_GROUP, POOL_OUT_GROUP))],
               [SDS((t, POOL_WIDTH), MXU_DTYPE), SDS((4, POOL_GROUP, POOL_OUT_GROUP), F32)],
               scratch=[pltpu.VMEM((tm + 16, POOL_GROUP), F32)], sem=("arbitrary",))(dyp, dyp, d_bf, pool_w)


CONV_BLK = 512


def _conv_fwd(proj, conv_w, tm):
    t = proj.shape[0]

    def body(x_ref, halo_ref, w_ref, o_ref, ext_ref):
        i = pl.program_id(0)
        ext_ref[0:8, :] = jnp.where(i > 0, halo_ref[...], 0.0)
        ext_ref[8:8 + tm, :] = x_ref[...]
        y = w_ref[pl.ds(0, 1), :] * ext_ref[pl.ds(5, tm), :]
        for k in range(1, CONV_K):
            y = y + w_ref[pl.ds(k, 1), :] * ext_ref[pl.ds(5 + k, tm), :]
        o_ref[...] = y * _sigmoid(y)

    halo = pl.BlockSpec((8, CONV_BLK), lambda i, j: (jnp.maximum(i * (tm // 8) - 1, 0), j))
    blk = pl.BlockSpec((tm, CONV_BLK), lambda i, j: (i, j))
    return _pc(body, "conv_fwd", (t // tm, QKV_WIDTH // CONV_BLK),
               [blk, halo, pl.BlockSpec((CONV_K, CONV_BLK), lambda i, j: (0, j))], blk,
               SDS((t, QKV_WIDTH), F32), scratch=[pltpu.VMEM((8 + tm, CONV_BLK), F32)],
               sem=("parallel", "parallel"))(proj, proj, conv_w)


def _conv_bwd(dact, proj, conv_w, tm):
    t = proj.shape[0]
    n = t // tm

    def body(da_ref, dan_ref, x_ref, xp_ref, xn_ref, w_ref, dx_ref, dw_ref, ext_ref, dy_ref):
        i = pl.program_id(1)

        @pl.when(i == 0)
        def _():
            dw_ref[...] = jnp.zeros_like(dw_ref)

        ext_ref[0:8, :] = jnp.where(i > 0, xp_ref[...], 0.0)
        ext_ref[8:8 + tm, :] = x_ref[...]
        ext_ref[8 + tm:16 + tm, :] = jnp.where(i < n - 1, xn_ref[...], 0.0)
        y = w_ref[pl.ds(0, 1), :] * ext_ref[pl.ds(5, tm + 8), :]
        for k in range(1, CONV_K):
            y = y + w_ref[pl.ds(k, 1), :] * ext_ref[pl.ds(5 + k, tm + 8), :]
        s = _sigmoid(y)
        dsilu = s * (1.0 + y * (1.0 - s))
        dy_ref[0:tm, :] = da_ref[...] * dsilu[0:tm]
        dy_ref[tm:tm + 8, :] = jnp.where(i < n - 1, dan_ref[...], 0.0) * dsilu[tm:tm + 8]
        dx = w_ref[pl.ds(0, 1), :] * dy_ref[pl.ds(3, tm), :]
        for k in range(1, CONV_K):
            dx = dx + w_ref[pl.ds(k, 1), :] * dy_ref[pl.ds(3 - k, tm), :]
        dx_ref[...] = _mx(dx)
        dy = dy_ref[pl.ds(0, tm), :]
        for k in range(CONV_K):
            dw_ref[pl.ds(k, 1), :] += jnp.sum(dy * ext_ref[pl.ds(5 + k, tm), :], axis=0, keepdims=True)

    blk = pl.BlockSpec((tm, CONV_BLK), lambda j, i: (i, j))
    prev = pl.BlockSpec((8, CONV_BLK), lambda j, i: (jnp.maximum(i * (tm // 8) - 1, 0), j))
    nxt = pl.BlockSpec((8, CONV_BLK), lambda j, i: (jnp.minimum((i + 1) * (tm // 8), t // 8 - 1), j))
    wspec = pl.BlockSpec((CONV_K, CONV_BLK), lambda j, i: (0, j))
    return _pc(body, "conv_bwd", (QKV_WIDTH // CONV_BLK, n),
               [blk, nxt, blk, prev, nxt, wspec],
               [blk, pl.BlockSpec((8, CONV_BLK), lambda j, i: (0, j))],
               [SDS((t, QKV_WIDTH), MXU_DTYPE), SDS((8, QKV_WIDTH), F32)],
               scratch=[pltpu.VMEM((16 + tm, CONV_BLK), F32), pltpu.VMEM((8 + tm, CONV_BLK), F32)],
               sem=("parallel", "arbitrary"))(dact, dact, proj, proj, proj, conv_w)


def _lane(shape):
    return lax.broadcasted_iota(jnp.int32, shape, 1)


def _ba_fwd(proj, al_row, dtb_row, tm):
    t = proj.shape[0]
    bablk = K_BA // 128

    def body(ba_ref, al_ref, dtb_ref, bg_ref):
        ba = ba_ref[...]
        lane = _lane(ba.shape)
        g = -jnp.exp(al_ref[...]) * _softplus(ba + dtb_ref[...])
        bg_ref[...] = jnp.where(lane < HEADS, _sigmoid(ba), jnp.where(lane < 2 * HEADS, g, 0.0))

    return _pc(body, "ba_fwd", (t // tm,),
               [pl.BlockSpec((tm, 128), lambda i: (i, bablk)), _const((1, 128)), _const((1, 128))],
               _row(tm, 128), SDS((t, 128), F32), sem=("parallel",))(proj, al_row, dtb_row)


def _ba_bwd(dbg, bg, proj, al_row, dtb_row, tm):
    t = proj.shape[0]
    bablk = K_BA // 128

    def body(dbg_ref, bg_ref, ba_ref, al_ref, dtb_ref, dba_ref, acc_ref):
        i = pl.program_id(0)

        @pl.when(i == 0)
        def _():
            acc_ref[...] = jnp.zeros_like(acc_ref)

        dbg_v, bg_v, ba = dbg_ref[...], bg_ref[...], ba_ref[...]
        lane = _lane(ba.shape)
        is_g = (lane >= HEADS) & (lane < 2 * HEADS)
        dbeta_raw = dbg_v * bg_v * (1.0 - bg_v)
        da_raw = dbg_v * (-jnp.exp(al_ref[...])) * _sigmoid(ba + dtb_ref[...])
        dba_ref[...] = _mx(jnp.where(lane < HEADS, dbeta_raw, jnp.where(is_g, da_raw, 0.0)))
        acc_ref[0:1, :] += jnp.sum(jnp.where(is_g, dbg_v * bg_v, 0.0), axis=0, keepdims=True)
        acc_ref[1:2, :] += jnp.sum(jnp.where(is_g, da_raw, 0.0), axis=0, keepdims=True)

    return _pc(body, "ba_bwd", (t // tm,),
               [_row(tm, 128), _row(tm, 128), pl.BlockSpec((tm, 128), lambda i: (i, bablk)),
                _const((1, 128)), _const((1, 128))],
               [_row(tm, 128), _const((8, 128))], [SDS((t, 128), MXU_DTYPE), SDS((8, 128), F32)],
               sem=("arbitrary",))(dbg, bg, proj, al_row, dtb_row)


def _each(f, *lists):
    return [f(*a) for a in zip(*lists)]


def _rowsum(a):
    return jnp.sum(a, axis=1, keepdims=True)


def _chunk_terms(qs, ks, bgv, g_rows, hs):
    c = CHUNK
    ii = lax.broadcasted_iota(jnp.int32, (c, c), 0)
    jj = lax.broadcasted_iota(jnp.int32, (c, c), 1)
    lane = _lane(bgv.shape)
    incl = ii >= jj
    beta = [_rowsum(jnp.where(lane == h, bgv, 0.0)) for h in hs]
    g_col = [_rowsum(jnp.where(lane == HEADS + h, bgv, 0.0)) for h in hs]
    rq = _each(lambda q: lax.rsqrt(_rowsum(q * q) + L2_EPS), qs)
    rk = _each(lambda k: lax.rsqrt(_rowsum(k * k) + L2_EPS), ks)
    yq = _each(jnp.multiply, qs, rq)
    kn = _each(jnp.multiply, ks, rk)
    qn = _each(lambda a: a * Q_SCALE, yq)
    gc_col = _each(lambda g: _rowsum(jnp.where(jj <= ii, g, 0.0)), g_rows)
    gc_row = _each(lambda g: jnp.sum(jnp.where(ii <= jj, g, 0.0), axis=0, keepdims=True), g_col)
    dm = _each(lambda a, b: jnp.where(incl, jnp.exp(jnp.where(incl, a - b, 0.0)), 0.0), gc_col, gc_row)
    gl = _each(_rowsum, g_rows)
    eg = _each(jnp.exp, gc_col)
    ek = _each(lambda a, b: jnp.exp(a - b), gl, gc_col)
    egl = _each(jnp.exp, gl)
    kb = _each(jnp.multiply, kn, beta)
    kk = _each(_dot_nt, kb, kn)
    qk = _each(_dot_nt, qn, kn)
    m = _each(lambda a, b: jnp.where(ii > jj, a * b, 0.0), kk, dm)
    attn = _each(jnp.multiply, qk, dm)
    return dict(ii=ii, jj=jj, beta=beta, rq=rq, rk=rk, yq=yq, kn=kn, qn=qn, dm=dm, eg=eg, ek=ek,
                egl=egl, kb=kb, m=m, attn=attn)


def _unit_lower_inverse_minus_identity(ms, ii, jj):
    pair = (ii >> 1) == (jj >> 1)
    ys = _each(lambda m: -jnp.where(pair, m, 0.0), ms)
    s = 1
    while (1 << s) < CHUNK:
        mask = ((ii >> (s + 1)) == (jj >> (s + 1))) & ((ii >> s) != (jj >> s))
        lbs = _each(lambda m: jnp.where(mask, m, 0.0), ms)
        zs = _each(lambda y, lb: lb + _dot(y, lb), ys, lbs)
        ys = _each(lambda y, z: y - z - _dot(z, y), ys, zs)
        s += 1
    return ys


def _head_offsets(group):
    hs = [group * HEAD_GROUP + a for a in range(HEAD_GROUP)]
    return hs, [pl.ds(pl.multiple_of(base + h * HEAD_DIM, HEAD_DIM), HEAD_DIM)
                for base in (0, DN_WIDTH, 2 * DN_WIDTH) for h in hs]


def _dn_local_fwd(qkv_act, bg, bgt):
    t = qkv_act.shape[0]
    nt = t // CHUNK
    c = CHUNK

    def body(qkv_ref, bg_ref, bgt_ref, u_ref, w_ref, qg_ref, kg_ref, attn_ref, y_ref, egl_ref):
        bgv = bg_ref[...]

        def group(gi, carry):
            hs, offs = _head_offsets(gi)
            qo, ko, vo = offs[0:HEAD_GROUP], offs[HEAD_GROUP:2 * HEAD_GROUP], offs[2 * HEAD_GROUP:]
            qs = [qkv_ref[:, o] for o in qo]
            ks = [qkv_ref[:, o] for o in ko]
            vs = [qkv_ref[:, o] for o in vo]
            g_rows = [bgt_ref[pl.ds(HEADS + h, 1), :] for h in hs]
            ct = _chunk_terms(qs, ks, bgv, g_rows, hs)
            ys = _unit_lower_inverse_minus_identity(ct["m"], ct["ii"], ct["jj"])
            vb = _each(jnp.multiply, vs, ct["beta"])
            kbe = _each(jnp.multiply, ct["kb"], ct["eg"])
            us = _each(lambda a, y: a + _dot(y, a), vb, ys)
            ws = _each(lambda a, y: a + _dot(y, a), kbe, ys)
            for a in range(HEAD_GROUP):
                dst = qo[a]
                u_ref[:, dst] = us[a]
                w_ref[:, dst] = _mx(ws[a])
                qg_ref[:, dst] = _mx(ct["qn"][a] * ct["eg"][a])
                kg_ref[:, dst] = _mx(ct["kn"][a] * ct["ek"][a])
                attn_ref[:, dst] = _mx(ct["attn"][a])
                y_ref[:, dst] = _mx(ys[a])
                egl_ref[0, pl.ds(hs[a], 1), :] = jnp.broadcast_to(ct["egl"][a], (1, HEAD_DIM))
            return carry

        lax.fori_loop(0, HEADS // HEAD_GROUP, group, 0)

    wide = _row(c, DN_WIDTH)
    return _pc(body, "dn_local_fwd", (nt,),
               [_row(c, QKV_WIDTH), _row(c, 128), pl.BlockSpec((2 * HEADS, c), lambda i: (0, i))],
               [wide, wide, wide, wide, wide, wide, pl.BlockSpec((1, HEADS, HEAD_DIM), lambda i: (i, 0, 0))],
               [SDS((t, DN_WIDTH), F32)] + [SDS((t, DN_WIDTH), MXU_DTYPE)] * 5 + [SDS((nt, HEADS, HEAD_DIM), F32)],
               sem=("parallel",))(qkv_act, bg, bgt)


def _dn_scan_fwd(u, w, qg, kg, attn, egl):
    t = u.shape[0]
    nt = t // CHUNK
    c = CHUNK
    sls = [slice(h * HEAD_DIM, (h + 1) * HEAD_DIM) for h in range(HEADS)]

    def body(u_ref, w_ref, qg_ref, kg_ref, attn_ref, egl_ref, o_ref, vn_ref, st_ref, s_ref):
        @pl.when(pl.program_id(0) == 0)
        def _():
            s_ref[...] = jnp.zeros_like(s_ref)

        ss = [s_ref[h] for h in range(HEADS)]
        sb = _each(_mx, ss)
        vn = [u_ref[:, sl] - _dot(w_ref[:, sl], b) for sl, b in zip(sls, sb)]
        vnb = _each(_mx, vn)
        oa = [_dot(qg_ref[:, sl], b) for sl, b in zip(sls, sb)]
        ob = [_dot(attn_ref[:, sl], b) for sl, b in zip(sls, vnb)]
        upd = [_dot_tn(kg_ref[:, sl], b) for sl, b in zip(sls, vnb)]
        for h, sl in enumerate(sls):
            st_ref[0, h] = ss[h]
            vn_ref[:, sl] = vnb[h]
            o_ref[:, sl] = oa[h] + ob[h]
            s_ref[h] = ss[h] * egl_ref[0, h:h + 1, :] + upd[h]

    wide = _row(c, DN_WIDTH)
    return _pc(body, "dn_scan_fwd", (nt,),
               [wide] * 5 + [pl.BlockSpec((1, HEADS, HEAD_DIM), lambda i: (i, 0, 0))],
               [wide, wide, pl.BlockSpec((1, HEADS, HEAD_DIM, HEAD_DIM), lambda i: (i, 0, 0, 0))],
               [SDS((t, DN_WIDTH), F32), SDS((t, DN_WIDTH), MXU_DTYPE), SDS((nt, HEADS, HEAD_DIM, HEAD_DIM), F32)],
               scratch=[pltpu.VMEM((HEADS, HEAD_DIM, HEAD_DIM), F32)], sem=("arbitrary",))(u, w, qg, kg, attn, egl)


def _dn_scan_bwd(do, qg, kg, w, attn, vn, states, egl):
    t = do.shape[0]
    nt = t // CHUNK
    c = CHUNK
    sls = [slice(h * HEAD_DIM, (h + 1) * HEAD_DIM) for h in range(HEADS)]

    def body(do_ref, qg_ref, kg_ref, w_ref, attn_ref, vn_ref, st_ref, egl_ref,
             dvn_ref, dkg_ref, dqg_ref, dattn_ref, dw_ref, degl_ref, ds_ref):
        @pl.when(pl.program_id(0) == 0)
        def _():
            ds_ref[...] = jnp.zeros_like(ds_ref)

        dsp = [ds_ref[h] for h in range(HEADS)]
        dsb = _each(_mx, dsp)
        ss = [st_ref[0, h] for h in range(HEADS)]
        sb = _each(_mx, ss)
        dvn = [_dot(kg_ref[:, sl], b) + _dot_tn(attn_ref[:, sl], do_ref[:, sl]) for sl, b in zip(sls, dsb)]
        dvnb = _each(_mx, dvn)
        dkg = [_dot_nt(vn_ref[:, sl], b) for sl, b in zip(sls, dsb)]
        dqg = [_dot_nt(do_ref[:, sl], b) for sl, b in zip(sls, sb)]
        dattn = [_dot_nt(do_ref[:, sl], vn_ref[:, sl]) for sl in sls]
        dwv = [-_dot_nt(a, b) for a, b in zip(dvnb, sb)]
        upd = [_dot_tn(qg_ref[:, sl], do_ref[:, sl]) - _dot_tn(w_ref[:, sl], a) for sl, a in zip(sls, dvnb)]
        for h, sl in enumerate(sls):
            dvn_ref[:, sl] = dvn[h]
            dkg_ref[:, sl] = dkg[h]
            dqg_ref[:, sl] = dqg[h]
            dattn_ref[:, sl] = dattn[h]
            dw_ref[:, sl] = dwv[h]
            degl = jnp.sum(_rowsum(ss[h] * dsp[h]), axis=0, keepdims=True)
            degl_ref[0, h:h + 1, :] = jnp.broadcast_to(degl, (1, HEAD_DIM))
            ds_ref[h] = dsp[h] * egl_ref[0, h:h + 1, :] + upd[h]

    rev = pl.BlockSpec((c, DN_WIDTH), lambda i: (nt - 1 - i, 0))
    rev3 = pl.BlockSpec((1, HEADS, HEAD_DIM), lambda i: (nt - 1 - i, 0, 0))
    rev4 = pl.BlockSpec((1, HEADS, HEAD_DIM, HEAD_DIM), lambda i: (nt - 1 - i, 0, 0, 0))
    return _pc(body, "dn_scan_bwd", (nt,), [rev] * 6 + [rev4, rev3], [rev] * 5 + [rev3],
               [SDS((t, DN_WIDTH), F32)] * 5 + [SDS((nt, HEADS, HEAD_DIM), F32)],
               scratch=[pltpu.VMEM((HEADS, HEAD_DIM, HEAD_DIM), F32)],
               sem=("arbitrary",))(do, qg, kg, w, attn, vn, states, egl)


def _dn_local_bwd(qkv_act, bg, bgt, u, w, ymat, dvn, dw, dqg, dkg, dattn, degl):
    t = qkv_act.shape[0]
    nt = t // CHUNK
    c = CHUNK

    def body(qkv_ref, bg_ref, bgt_ref, u_ref, w_ref, y_ref, du_ref, dw_ref, dqg_ref, dkg_ref, dattn_ref,
             degl_ref, dqkv_ref, dbg_ref):
        bgv = bg_ref[...]
        lane = _lane(bgv.shape)
        rowi = lax.broadcasted_iota(jnp.int32, (c, 1), 0)

        def group(gi, dbg):
            hs, offs = _head_offsets(gi)
            qo, ko, vo = offs[0:HEAD_GROUP], offs[HEAD_GROUP:2 * HEAD_GROUP], offs[2 * HEAD_GROUP:]
            qs = [qkv_ref[:, o] for o in qo]
            ks = [qkv_ref[:, o] for o in ko]
            vs = [qkv_ref[:, o] for o in vo]
            g_rows = [bgt_ref[pl.ds(HEADS + h, 1), :] for h in hs]
            ct = _chunk_terms(qs, ks, bgv, g_rows, hs)
            ii, jj = ct["ii"], ct["jj"]
            beta, eg, ek, kb, kn, qn, dm = ct["beta"], ct["eg"], ct["ek"], ct["kb"], ct["kn"], ct["qn"], ct["dm"]
            ys = [y_ref[:, o] for o in qo]
            du = [du_ref[:, o] for o in qo]
            dwv = [dw_ref[:, o] for o in qo]
            dqg_v = [dqg_ref[:, o] for o in qo]
            dkg_v = [dkg_ref[:, o] for o in qo]
            dattn_v = [dattn_ref[:, o] for o in qo]
            degl_v = [jnp.max(degl_ref[0, pl.ds(h, 1), :], axis=1, keepdims=True) for h in hs]
            dvb = _each(lambda a, y: a + _dot_tn(y, a), du, ys)
            dkbe = _each(lambda a, y: a + _dot_tn(y, a), dwv, ys)
            dm_u = [_dot_nt(a, u_ref[:, o]) for a, o in zip(dvb, qo)]
            dm_w = [_dot_nt(a, w_ref[:, o]) for a, o in zip(dkbe, qo)]
            dms = _each(lambda a, b: jnp.where(ii > jj, -(a + b), 0.0), dm_u, dm_w)
            dkk = _each(jnp.multiply, dms, dm)
            dqk = _each(jnp.multiply, dattn_v, dm)
            gmat = _each(lambda a, b, c_, d: a * b + c_ * d, dms, ct["m"], dattn_v, ct["attn"])
            dkb = _each(lambda a, b, c_, d: _dot(a, b) + c_ * d, dkk, kn, dkbe, eg)
            dk1 = _each(_dot_tn, dkk, kb)
            dk2 = _each(_dot_tn, dqk, qn)
            dq1 = _each(_dot, dqk, kn)
            dk = _each(lambda a, b, c_, d: a + b + c_ * d, dk1, dk2, dkg_v, ek)
            dq = _each(lambda a, b, c_: a + b * c_, dq1, dqg_v, eg)
            deg = _each(lambda a, b, c_, d: _rowsum(a * b) + _rowsum(c_ * d), dqg_v, qn, dkbe, kb)
            dek = _each(lambda a, b: _rowsum(a * b), dkg_v, kn)
            dgl = _each(lambda a, b, c_, d: jnp.sum(a * b, axis=0, keepdims=True) + c_ * d, dek, ek, degl_v, ct["egl"])
            cs_row = _each(lambda g: jnp.sum(g, axis=0, keepdims=True), gmat)
            cs_col = _each(lambda r: _rowsum(jnp.where(ii == jj, r, 0.0)), cs_row)
            dgc = _each(lambda a, b, c_, d, g, e, f: a * b - c_ * d + _rowsum(g) - e + jnp.where(rowi == c - 1, f, 0.0),
                        deg, eg, dek, ek, gmat, cs_col, dgl)
            dgc_row = _each(lambda a: jnp.sum(jnp.where(ii == jj, a, 0.0), axis=0, keepdims=True), dgc)
            dg = _each(lambda r: _rowsum(jnp.where(jj >= ii, r, 0.0)), dgc_row)
            dbeta = _each(lambda a, b, c_, d: _rowsum(a * b) + _rowsum(c_ * d), dkb, kn, dvb, vs)
            dk = _each(lambda a, b, c_: a + b * c_, dk, dkb, beta)
            for a in range(HEAD_GROUP):
                dyq = dq[a] * Q_SCALE
                yq = ct["yq"][a]
                dqkv_ref[:, qo[a]] = ct["rq"][a] * (dyq - yq * _rowsum(yq * dyq))
                dqkv_ref[:, ko[a]] = ct["rk"][a] * (dk[a] - kn[a] * _rowsum(kn[a] * dk[a]))
                dqkv_ref[:, vo[a]] = dvb[a] * beta[a]
                dbg = dbg + jnp.where(lane == hs[a], dbeta[a], 0.0) + jnp.where(lane == HEADS + hs[a], dg[a], 0.0)
            return dbg

        dbg_ref[...] = lax.fori_loop(0, HEADS // HEAD_GROUP, group, jnp.zeros((c, 128), F32))

    wide = _row(c, DN_WIDTH)
    sc3 = pl.BlockSpec((1, HEADS, HEAD_DIM), lambda i: (i, 0, 0))
    return _pc(body, "dn_local_bwd", (nt,),
               [_row(c, QKV_WIDTH), _row(c, 128), pl.BlockSpec((2 * HEADS, c), lambda i: (0, i))] + [wide] * 8 + [sc3],
               [_row(c, QKV_WIDTH), _row(c, 128)], [SDS((t, QKV_WIDTH), F32), SDS((t, 128), F32)],
               sem=("parallel",))(qkv_act, bg, bgt, u, w, ymat, dvn, dw, dqg, dkg, dattn, degl)


def _mix_fwd(o, proj, ypre, pool_scale, wo_row, tm):
    t = o.shape[0]

    def body(o_ref, z_ref, ga_ref, gb_ref, yp_ref, ps_ref, wo_ref, mixed_ref):
        for h in range(HEADS):
            sl = slice(h * HEAD_DIM, (h + 1) * HEAD_DIM)
            oh = o_ref[:, sl]
            on = oh * lax.rsqrt(jnp.mean(oh * oh, axis=1, keepdims=True) + RMS_EPS)
            zh = z_ref[:, sl]
            yb = on * wo_ref[:, sl] * (zh * _sigmoid(zh))
            ya = yp_ref[:, sl] * ps_ref[:, sl]
            mixed_ref[:, sl] = _mx(_sigmoid(ga_ref[:, sl]) * ya + _sigmoid(gb_ref[:, sl]) * yb)

    def col(blk):
        return pl.BlockSpec((tm, D_MODEL), lambda i: (i, blk))

    return _pc(body, "mix_fwd", (t // tm,),
               [_row(tm, D_MODEL), col(K_Z // D_MODEL), col(K_GA // D_MODEL), col(K_GB // D_MODEL), _row(tm, D_MODEL),
                _const((1, D_MODEL)), _const((1, D_MODEL))],
               _row(tm, D_MODEL), SDS((t, D_MODEL), MXU_DTYPE), sem=("parallel",))(
                   o, proj, proj, proj, ypre, pool_scale, wo_row)


def _mix_bwd(da1_bf, w_out, o, proj, ypre, pool_scale, wo_row, tm):
    t = o.shape[0]

    def body(da_ref, wout_ref, o_ref, z_ref, ga_ref, gb_ref, yp_ref, ps_ref, wo_ref,
             do_ref, dz_ref, dga_ref, dgb_ref, dyp_ref, acc_ref):
        i = pl.program_id(0)

        @pl.when(i == 0)
        def _():
            acc_ref[...] = jnp.zeros_like(acc_ref)

        dmixed = _dot_nt(da_ref[...], wout_ref[...])
        for h in range(HEADS):
            sl = slice(h * HEAD_DIM, (h + 1) * HEAD_DIM)
            oh = o_ref[:, sl]
            rs = lax.rsqrt(jnp.mean(oh * oh, axis=1, keepdims=True) + RMS_EPS)
            on = oh * rs
            zh = z_ref[:, sl]
            sz = _sigmoid(zh)
            silu = zh * sz
            woh = wo_ref[:, sl]
            t1 = on * woh
            yb = t1 * silu
            sa = _sigmoid(ga_ref[:, sl])
            sb = _sigmoid(gb_ref[:, sl])
            yp = yp_ref[:, sl]
            psh = ps_ref[:, sl]
            dm = dmixed[:, sl]
            dga_ref[:, sl] = _mx(dm * (yp * psh) * sa * (1.0 - sa))
            dgb_ref[:, sl] = _mx(dm * yb * sb * (1.0 - sb))
            dya = dm * sa
            dyb = dm * sb
            dyp_ref[:, sl] = _mx(dya * psh)
            acc_ref[0:1, sl] += jnp.sum(dya * yp, axis=0, keepdims=True)
            dz_ref[:, sl] = _mx(dyb * t1 * (sz * (1.0 + zh * (1.0 - sz))))
            dt1 = dyb * silu
            acc_ref[1:2, 0:HEAD_DIM] += jnp.sum(dt1 * on, axis=0, keepdims=True)
            don = dt1 * woh
            do_ref[:, sl] = _mx(rs * (don - on * jnp.mean(don * on, axis=1, keepdims=True)))

    def col(blk):
        return pl.BlockSpec((tm, D_MODEL), lambda i: (i, blk))

    r = _row(tm, D_MODEL)
    return _pc(body, "mix_bwd", (t // tm,),
               [r, _const((D_MODEL, D_MODEL)), r, col(K_Z // D_MODEL), col(K_GA // D_MODEL), col(K_GB // D_MODEL), r,
                _const((1, D_MODEL)), _const((1, D_MODEL))],
               [r, r, r, r, r, _const((8, D_MODEL))],
               [SDS((t, D_MODEL), MXU_DTYPE)] * 5 + [SDS((8, D_MODEL), F32)],
               sem=("arbitrary",))(da1_bf, w_out, o, proj, proj, proj, ypre, pool_scale, wo_row)


def _oproj_ln1(mixed, w_out, h0, g1, b1, tm):
    t = mixed.shape[0]

    def body(m_ref, w_ref, h0_ref, g_ref, b_ref, a1_ref, h1_ref, h1b_ref):
        a1 = ALPHA * h0_ref[...] + _dot(m_ref[...], w_ref[...])
        a1_ref[...] = a1
        xhat, _ = _ln_stats(a1)
        h1 = xhat * g_ref[...] + b_ref[...]
        h1_ref[...] = h1
        h1b_ref[...] = _mx(h1)

    r = _row(tm, D_MODEL)
    v = _const((1, D_MODEL))
    return _pc(body, "oproj_ln1", (t // tm,), [r, _const((D_MODEL, D_MODEL)), r, v, v], [r, r, r],
               [SDS((t, D_MODEL), F32), SDS((t, D_MODEL), F32), SDS((t, D_MODEL), MXU_DTYPE)],
               sem=("parallel",))(mixed, w_out, h0, g1, b1)


def _mlp_up(h1_bf, w_up, tm):
    t = h1_bf.shape[0]
    tn = w_up.shape[2]

    def body(h_ref, w_ref, up_ref, act_ref):
        up = _dot(h_ref[...], w_ref[...])
        up_ref[...] = up
        r = jnp.maximum(up, 0.0)
        act_ref[...] = _mx(r * r)

    o = pl.BlockSpec((tm, tn), lambda i, j: (i, j))
    return _pc(body, "mlp_up", (t // tm, D_FF // tn),
               [pl.BlockSpec((tm, D_MODEL), lambda i, j: (i, 0)),
                pl.BlockSpec((None, D_MODEL, tn), lambda i, j: (j, 0, 0))],
               [o, o], [SDS((t, D_FF), F32), SDS((t, D_FF), MXU_DTYPE)], sem=("parallel", "parallel"))(h1_bf, w_up)


def _tail(act, w_down, h1, w_gate, p_bf, w_proj, tgt, g2, b2, tm):
    t = act.shape[0]

    def body(act_ref, wd_ref, h1_ref, wg_ref, p_ref, wp_ref, tgt_ref, g_ref, b_ref,
             dr_ref, drb_ref, dgp_ref, dpp_ref, rb_ref, acc_ref):
        i = pl.program_id(0)

        @pl.when(i == 0)
        def _():
            acc_ref[...] = jnp.zeros_like(acc_ref)

        r = ALPHA * h1_ref[...] + _dot(act_ref[...], wd_ref[...])
        rb = _mx(r)
        rb_ref[...] = rb
        gate = _sigmoid(_dot(rb, wg_ref[...]))
        pp = _dot(p_ref[...], wp_ref[...])
        xhat, rstd = _ln_stats(r + gate * pp)
        g = g_ref[...]
        diff = xhat * g + b_ref[...] - tgt_ref[...]
        dh2 = diff * (1.0 / D_MODEL)
        rowloss = jnp.sum(diff * diff, axis=1, keepdims=True) * (0.5 / D_MODEL)
        acc_ref[0:1, :] += jnp.sum(dh2 * xhat, axis=0, keepdims=True)
        acc_ref[1:2, :] += jnp.sum(dh2, axis=0, keepdims=True)
        acc_ref[2:3, :] += jnp.broadcast_to(jnp.sum(rowloss, axis=0, keepdims=True), (1, D_MODEL))
        da2 = _ln_bwd(dh2, xhat, rstd, g)
        dpp_ref[...] = _mx(da2 * gate)
        dgp = _mx(da2 * pp * gate * (1.0 - gate))
        dgp_ref[...] = dgp
        dr = da2 + _dot_nt(dgp, wg_ref[...])
        dr_ref[...] = dr
        drb_ref[...] = _mx(dr)

    r = _row(tm, D_MODEL)
    v = _const((1, D_MODEL))
    return _pc(body, "tail", (t // tm,),
               [_row(tm, D_FF), _const((D_FF, D_MODEL)), r, _const((D_MODEL, D_MODEL)), _row(tm, PLE_DIM),
                _const((PLE_DIM, D_MODEL)), r, v, v],
               [r, r, r, r, r, _const((8, D_MODEL))],
               [SDS((t, D_MODEL), F32)] + [SDS((t, D_MODEL), MXU_DTYPE)] * 4 + [SDS((8, D_MODEL), F32)],
               sem=("arbitrary",))(act, w_down, h1, w_gate, p_bf, w_proj, tgt, g2, b2)


def _mlp_bwd1(dr_bf, w_down, up, tm, tn):
    t = up.shape[0]

    def body(dr_ref, w_ref, up_ref, dup_ref):
        dact = _dot_nt(dr_ref[...], w_ref[...])
        dup_ref[...] = _mx(dact * (2.0 * jnp.maximum(up_ref[...], 0.0)))

    o = pl.BlockSpec((tm, tn), lambda i, j: (i, j))
    return _pc(body, "mlp_bwd1", (t // tm, D_FF // tn),
               [pl.BlockSpec((tm, D_MODEL), lambda i, j: (i, 0)), pl.BlockSpec((tn, D_MODEL), lambda i, j: (j, 0)), o],
               o, SDS((t, D_FF), MXU_DTYPE), sem=("parallel", "parallel"))(dr_bf, w_down, up)


def _mlp_bwd2(dup, w_up, dr, a1, g1, tm):
    t = dr.shape[0]

    nk, tk = w_up.shape[0], w_up.shape[2]

    def body(dup_ref, w_ref, dr_ref, a1_ref, g_ref, da1_ref, da1b_ref, acc_ref, mm_ref):
        i, kk = pl.program_id(0), pl.program_id(1)

        @pl.when((i == 0) & (kk == 0))
        def _():
            acc_ref[...] = jnp.zeros_like(acc_ref)

        @pl.when(kk == 0)
        def _():
            mm_ref[...] = jnp.zeros_like(mm_ref)

        mm_ref[...] += _dot_nt(dup_ref[...], w_ref[...])

        @pl.when(kk == nk - 1)
        def _():
            dh1 = ALPHA * dr_ref[...] + mm_ref[...]
            xhat, rstd = _ln_stats(a1_ref[...])
            acc_ref[0:1, :] += jnp.sum(dh1 * xhat, axis=0, keepdims=True)
            acc_ref[1:2, :] += jnp.sum(dh1, axis=0, keepdims=True)
            da1 = _ln_bwd(dh1, xhat, rstd, g_ref[...])
            da1_ref[...] = da1
            da1b_ref[...] = _mx(da1)

    r = pl.BlockSpec((tm, D_MODEL), lambda i, kk: (i, 0))
    return _pc(body, "mlp_bwd2", (t // tm, nk),
               [pl.BlockSpec((tm, tk), lambda i, kk: (i, kk)), pl.BlockSpec((None, D_MODEL, tk), lambda i, kk: (kk, 0, 0)),
                r, r, pl.BlockSpec((1, D_MODEL), lambda i, kk: (0, 0))],
               [r, r, pl.BlockSpec((8, D_MODEL), lambda i, kk: (0, 0))],
               [SDS((t, D_MODEL), F32), SDS((t, D_MODEL), MXU_DTYPE), SDS((8, D_MODEL), F32)],
               scratch=[pltpu.VMEM((tm, D_MODEL), F32)], sem=("arbitrary", "arbitrary"))(dup, w_up, dr, a1, g1)


def _ln_in_bwd(dproj, w_cat, da1, x, g, tm, tk):
    t = x.shape[0]
    nk = CAT_WIDTH // tk

    def body(dp_ref, w_ref, da1_ref, x_ref, g_ref, dx_ref, acc_ref, mm_ref):
        i, kk = pl.program_id(0), pl.program_id(1)

        @pl.when((i == 0) & (kk == 0))
        def _():
            acc_ref[...] = jnp.zeros_like(acc_ref)

        @pl.when(kk == 0)
        def _():
            mm_ref[...] = jnp.zeros_like(mm_ref)

        mm_ref[...] += _dot_nt(dp_ref[...], w_ref[...])

        @pl.when(kk == nk - 1)
        def _():
            dh0 = mm_ref[...] + ALPHA * da1_ref[...]
            xhat, rstd = _ln_stats(x_ref[...])
            acc_ref[0:1, :] += jnp.sum(dh0 * xhat, axis=0, keepdims=True)
            acc_ref[1:2, :] += jnp.sum(dh0, axis=0, keepdims=True)
            dx_ref[...] = _ln_bwd(dh0, xhat, rstd, g_ref[...])

    r = pl.BlockSpec((tm, D_MODEL), lambda i, kk: (i, 0))
    return _pc(body, "ln_in_bwd", (t // tm, nk),
               [pl.BlockSpec((tm, tk), lambda i, kk: (i, kk)), pl.BlockSpec((D_MODEL, tk), lambda i, kk: (0, kk)),
                r, r, pl.BlockSpec((1, D_MODEL), lambda i, kk: (0, 0))],
               [r, pl.BlockSpec((8, D_MODEL), lambda i, kk: (0, 0))],
               [SDS((t, D_MODEL), F32), SDS((8, D_MODEL), F32)],
               scratch=[pltpu.VMEM((tm, D_MODEL), F32)], sem=("arbitrary", "arbitrary"))(dproj, w_cat, da1, x, g)


def _local_step(x, p, tgt, wts):
    t = x.shape[0]
    tm = min(512, t)
    tms = min(256, t)
    row = lambda a: a.reshape(1, -1)
    w_cat = wts["w_cat"]
    pool_scale = row(wts["pool_scale"])
    wo_row = jnp.tile(row(wts["o_norm_w"]), (1, HEADS))
    pad8 = jnp.zeros((1, HEADS), F32)
    al_row = jnp.concatenate([pad8, row(wts["a_log"]), jnp.zeros((1, 128 - 2 * HEADS), F32)], axis=1)
    dtb_row = jnp.concatenate([pad8, row(wts["dt_bias"]), jnp.zeros((1, 128 - 2 * HEADS), F32)], axis=1)
    g_in, b_in = row(wts["ln_in_g"]), row(wts["ln_in_b"])
    g1, b1 = row(wts["ln1_g"]), row(wts["ln1_b"])
    g2, b2 = row(wts["ln2_g"]), row(wts["ln2_b"])

    h0, h0_bf = _ln_in(x, g_in, b_in, tm)
    proj = _matmul(h0_bf, w_cat, "nn", "proj", F32, tm=512, tn=1152, tk=1024)
    ypre, d_bf = _pool_fwd(proj, wts["pool_w"], tm)
    qkv_act = _conv_fwd(proj, wts["conv_w"], tm)
    bg = _ba_fwd(proj, al_row, dtb_row, tm)
    bgt = bg[:, :2 * HEADS].T
    u, w, qg, kg, attn, ymat, egl = _dn_local_fwd(qkv_act, bg, bgt)
    o, vn, states = _dn_scan_fwd(u, w, qg, kg, attn, egl)
    mixed = _mix_fwd(o, proj, ypre, pool_scale, wo_row, tm)
    a1, h1, h1_bf = _oproj_ln1(mixed, wts["w_out"], h0, g1, b1, tm)
    up, act = _mlp_up(h1_bf, wts["w_up"], tm)
    p_bf = _mx(p)
    dr, dr_bf, dgp, dpp, r_bf, acc_tail = _tail(act, wts["w_down"], h1, wts["ple_gate_w"], p_bf, wts["ple_proj_w"],
                                                tgt, g2, b2, tms)
    grads = {}
    grads["ple_proj_w"] = _matmul(p_bf, dpp, "tn", "dw_ple_proj", WIRE_DTYPE, tm=256, tn=1024, tk=512)
    grads["ple_gate_w"] = _matmul(r_bf, dgp, "tn", "dw_ple_gate", WIRE_DTYPE, tm=512, tn=1024, tk=512)
    grads["w_down"] = _matmul(act, dr_bf, "tn", "dw_down", WIRE_DTYPE, tm=512, tn=1024, tk=512)
    dup = _mlp_bwd1(dr_bf, wts["w_down"], up, tm, 1024)
    grads["w_up"] = _matmul(h1_bf, dup, "tn", "dw_up", WIRE_DTYPE, tm=512, tn=1024, tk=512, stack_out=True)
    da1, da1_bf, acc_ln1 = _mlp_bwd2(dup, wts["w_up"], dr, a1, g1, tms)
    grads["w_out"] = _matmul(mixed, da1_bf, "tn", "dw_out", WIRE_DTYPE, tm=512, tn=1024, tk=512)
    do, dz, dga, dgb, dyp, acc_mix = _mix_bwd(da1_bf, wts["w_out"], o, proj, ypre, pool_scale, wo_row, tms)
    du_pool, grads["pool_w"] = _pool_bwd(dyp, d_bf, wts["pool_w"], tm)
    dvn, dkg, dqg, dattn, dw, degl = _dn_scan_bwd(do, qg, kg, w, attn, vn, states, egl)
    dqkv_act, dbg = _dn_local_bwd(qkv_act, bg, bgt, u, w, ymat, dvn, dw, dqg, dkg, dattn, degl)
    dqkv, acc_conv = _conv_bwd(dqkv_act, proj, wts["conv_w"], tm)
    dba, acc_ba = _ba_bwd(dbg, bg, proj, al_row, dtb_row, tm)
    dproj = jnp.concatenate([dqkv, dz, dga, dgb, du_pool, dba,
                             jnp.zeros((t, CAT_WIDTH - K_BA - 128), MXU_DTYPE)], axis=1)
    dw_cat = _matmul(h0_bf, dproj, "tn", "dw_in", F32, tm=512, tn=1152, tk=512)
    grad_x, acc_in = _ln_in_bwd(dproj, w_cat, da1, x, g_in, tms, 1152)

    grads["w_in"] = jnp.concatenate(
        [dw_cat[:, K_U:K_U + 512], dw_cat[:, K_QKV:K_QKV + 3072], dw_cat[:, K_Z:K_Z + 1024],
         dw_cat[:, K_BA:K_BA + 16], dw_cat[:, K_GA:K_GA + 1024], dw_cat[:, K_GB:K_GB + 1024]], axis=1)
    grads["conv_w"] = acc_conv[0:CONV_K]
    grads["ln_in_g"], grads["ln_in_b"] = acc_in[0], acc_in[1]
    grads["ln1_g"], grads["ln1_b"] = acc_ln1[0], acc_ln1[1]
    grads["ln2_g"], grads["ln2_b"] = acc_tail[0], acc_tail[1]
    grads["pool_scale"] = acc_mix[0]
    grads["o_norm_w"] = acc_mix[1, 0:HEAD_DIM]
    grads["a_log"] = acc_ba[0, HEADS:2 * HEADS]
    grads["dt_bias"] = acc_ba[1, HEADS:2 * HEADS]
    loss = acc_tail[2, 0]
    return grad_x, grads, loss


MESH = pl.DeviceIdType.MESH
ANY = pl.BlockSpec(memory_space=pl.ANY)


def _chip_of(k, x, y):
    chip = (2 * x + y + k) % N_CHIPS
    return chip // 2, chip % 2


def _place():
    x, y, c = lax.axis_index("x"), lax.axis_index("y"), lax.axis_index("c")
    return x, y, c, 2 * x + y


def _half(rows, c):
    return pl.ds(pl.multiple_of(c * (rows // 2), 16), rows // 2)


def _remote(src, dst, send_sem, recv_sem, device_id):
    return pltpu.make_async_remote_copy(src_ref=src, dst_ref=dst, send_sem=send_sem, recv_sem=recv_sem,
                                        device_id=device_id, device_id_type=MESH)


def _tile_rows(rows):
    for tr in (256, 128, 64, 32, 16):
        if rows % tr == 0:
            return tr
    raise ValueError(rows)


def _gather_weights(shards, conv_shard):
    n = len(shards)

    def body(*refs):
        ins, conv_in = refs[0:n], refs[n]
        outs, conv_out = refs[n + 1:2 * n + 1], refs[2 * n + 1]
        send, recv, fsend, frecv, csend, crecv, lsem = refs[2 * n + 2:]
        x, y, c, me = _place()
        here, sib = (x, y, c), (x, y, 1 - c)
        local = [pltpu.make_async_copy(ins[a], outs[a].at[me], lsem.at[a]) for a in range(n)]
        local.append(pltpu.make_async_copy(conv_in, conv_out.at[me], lsem.at[n]))
        for cp in local:
            cp.start()
        sends = []
        for k in range(1, N_CHIPS):
            tx, ty = _chip_of(k, x, y)
            for a in range(n):
                mine = _half(shards[a].shape[0], c)
                sends.append(_remote(ins[a].at[mine], outs[a].at[me, mine], send.at[a * N_CHIPS + k],
                                     recv.at[a * N_CHIPS + k], (tx, ty, c)))
            sends.append(_remote(conv_in, conv_out.at[me], csend.at[k], crecv.at[k], (tx, ty, c)))
        for cp in sends:
            cp.start()
        forwards = []
        for k in range(1, N_CHIPS):
            src = (me + N_CHIPS - k) % N_CHIPS
            for a in range(n):
                landed = outs[a].at[src, _half(shards[a].shape[0], c)]
                _remote(landed, landed, send.at[a * N_CHIPS + k], recv.at[a * N_CHIPS + k], here).wait_recv()
                fwd = _remote(landed, landed, fsend.at[a * N_CHIPS + k], frecv.at[a * N_CHIPS + k], sib)
                fwd.start()
                forwards.append(fwd)
            _remote(conv_in, conv_out.at[src], csend.at[k], crecv.at[k], here).wait_recv()
        for k in range(1, N_CHIPS):
            src = (me + N_CHIPS - k) % N_CHIPS
            for a in range(n):
                passed = outs[a].at[src, _half(shards[a].shape[0], 1 - c)]
                _remote(passed, passed, fsend.at[a * N_CHIPS + k], frecv.at[a * N_CHIPS + k], here).wait_recv()
        for cp in sends + forwards:
            cp.wait_send()
        for cp in local:
            cp.wait()

    sems = pltpu.SemaphoreType.DMA((n * N_CHIPS,))
    return pl.pallas_call(
        body, name="gather_weights",
        out_shape=[SDS((N_CHIPS,) + s.shape, s.dtype) for s in shards]
        + [SDS((N_CHIPS,) + conv_shard.shape, conv_shard.dtype)],
        in_specs=[ANY] * (n + 1), out_specs=[ANY] * (n + 1),
        scratch_shapes=[sems, sems, sems, sems, pltpu.SemaphoreType.DMA((N_CHIPS,)),
                        pltpu.SemaphoreType.DMA((N_CHIPS,)), pltpu.SemaphoreType.DMA((n + 1,))],
    )(*shards, conv_shard)


def _swap_halves(gs):
    n = len(gs)

    def body(*refs):
        ins, mine, theirs = refs[0:n], refs[n:2 * n], refs[2 * n:3 * n]
        send, recv, lsem = refs[3 * n:]
        x, y, c, _ = _place()
        copies = []
        for a in range(n):
            rows = gs[a].shape[1]
            copies.append(pltpu.make_async_copy(ins[a].at[:, _half(rows, c)], mine[a], lsem.at[a]))
            copies.append(_remote(ins[a].at[:, _half(rows, 1 - c)], theirs[a], send.at[a], recv.at[a], (x, y, 1 - c)))
        for cp in copies:
            cp.start()
        for cp in copies:
            cp.wait()

    halves = [SDS((N_CHIPS, g.shape[1] // 2, g.shape[2]), g.dtype) for g in gs]
    out = pl.pallas_call(
        body, name="swap_halves", out_shape=halves + halves, in_specs=[ANY] * n, out_specs=[ANY] * (2 * n),
        scratch_shapes=[pltpu.SemaphoreType.DMA((n,))] * 3,
    )(*gs)
    return out[0:n], out[n:2 * n]


def _scatter_halves(qs):
    n = len(qs)

    def body(*refs):
        ins, outs = refs[0:n], refs[n:2 * n]
        send, recv, lsem = refs[2 * n:]
        x, y, c, me = _place()
        copies = [pltpu.make_async_copy(ins[a].at[me], outs[a].at[me], lsem.at[a]) for a in range(n)]
        for k in range(1, N_CHIPS):
            tx, ty = _chip_of(k, x, y)
            for a in range(n):
                copies.append(_remote(ins[a].at[2 * tx + ty], outs[a].at[me], send.at[a * N_CHIPS + k],
                                      recv.at[a * N_CHIPS + k], (tx, ty, c)))
        for cp in copies:
            cp.start()
        for k in range(1, N_CHIPS):
            src = (me + N_CHIPS - k) % N_CHIPS
            for a in range(n):
                _remote(ins[a].at[me], outs[a].at[src], send.at[a * N_CHIPS + k], recv.at[a * N_CHIPS + k],
                        (x, y, c)).wait_recv()
        for cp in copies[n:]:
            cp.wait_send()
        for cp in copies[0:n]:
            cp.wait()

    sems = pltpu.SemaphoreType.DMA((n * N_CHIPS,))
    return pl.pallas_call(
        body, name="scatter_halves", out_shape=[SDS(q.shape, q.dtype) for q in qs], in_specs=[ANY] * n,
        out_specs=[ANY] * n, scratch_shapes=[sems, sems, pltpu.SemaphoreType.DMA((n,))],
    )(*qs)


def _join_halves(hs):
    n = len(hs)

    def body(*refs):
        ins, outs = refs[0:n], refs[n:2 * n]
        send, recv, lsem = refs[2 * n:]
        x, y, c, _ = _place()
        local, remote = [], []
        for a in range(n):
            own = _half(2 * hs[a].shape[0], c)
            local.append(pltpu.make_async_copy(ins[a], outs[a].at[own], lsem.at[a]))
            remote.append(_remote(ins[a], outs[a].at[own], send.at[a], recv.at[a], (x, y, 1 - c)))
        for cp in local + remote:
            cp.start()
        for a in range(n):
            other = outs[a].at[_half(2 * hs[a].shape[0], 1 - c)]
            _remote(ins[a], other, send.at[a], recv.at[a], (x, y, c)).wait_recv()
        for cp in remote:
            cp.wait_send()
        for cp in local:
            cp.wait()

    return pl.pallas_call(
        body, name="join_halves", out_shape=[SDS((2 * h.shape[0], h.shape[1]), h.dtype) for h in hs],
        in_specs=[ANY] * n, out_specs=[ANY] * n, scratch_shapes=[pltpu.SemaphoreType.DMA((n,))] * 3,
    )(*hs)


def _add_pair(a, b, name):
    shape = a.shape
    a2, b2 = a.reshape(-1, shape[-1]), b.reshape(-1, shape[-1])
    rows, cols = a2.shape
    tr = _tile_rows(rows)

    def body(a_ref, b_ref, o_ref):
        o_ref[...] = (a_ref[...].astype(F32) + b_ref[...].astype(F32)).astype(o_ref.dtype)

    out = _pc(body, "add_" + name, (rows // tr,), [_row(tr, cols)] * 2, _row(tr, cols), SDS((rows, cols), a.dtype),
              sem=("parallel",))(a2, b2)
    return out.reshape(shape)


def _sum_slabs(recv, name):
    _, rows, cols = recv.shape
    tr = _tile_rows(rows)

    def body(r_ref, o_ref):
        acc = r_ref[0].astype(F32)
        for j in range(1, N_CHIPS):
            acc = acc + r_ref[j].astype(F32)
        o_ref[...] = acc

    return _pc(body, "sum_" + name, (rows // tr,), [pl.BlockSpec((N_CHIPS, tr, cols), lambda i: (0, i, 0))],
               _row(tr, cols), SDS((rows, cols), F32), sem=("parallel",))(recv)


def _adamw_math(w, g, m, v):
    m = ADAM_B1 * m + (1.0 - ADAM_B1) * g
    v = ADAM_B2 * v + (1.0 - ADAM_B2) * (g * g)
    m_hat = m / (1.0 - ADAM_B1 ** ADAM_STEP)
    v_hat = v / (1.0 - ADAM_B2 ** ADAM_STEP)
    delta = -ADAM_LR * (m_hat / (jnp.sqrt(v_hat) + ADAM_EPS) + ADAM_WD * w)
    return delta, m, v


def _adamw_2d(w, g, m, v, name):
    rows, cols = w.shape
    tr = _tile_rows(rows)

    def body(w_ref, g_ref, m_ref, v_ref, g_out, d_out, m_out, v_out):
        g = g_ref[...]
        delta, mn, vn = _adamw_math(w_ref[...], g, m_ref[...], v_ref[...])
        g_out[...] = g
        d_out[...] = delta
        m_out[...] = mn
        v_out[...] = vn

    r = _row(tr, cols)
    return _pc(body, "adamw_" + name, (rows // tr,), [r] * 4, [r] * 4, [SDS((rows, cols), F32)] * 4,
               sem=("parallel",))(w, g, m, v)


def _small_allreduce_adamw(mine, w, m, v):
    shape = mine.shape

    def body(mine_ref, w_ref, m_ref, v_ref, g_out, d_out, m_out, v_out, buf_ref, send_sems, recv_sems):
        x, y, c = lax.axis_index("x"), lax.axis_index("y"), lax.axis_index("c")
        me = 4 * x + 2 * y + c
        buf_ref[me] = mine_ref[...]
        copies = []
        for k in range(1, N_DEV):
            tgt = (me + k) % N_DEV
            copies.append(pltpu.make_async_remote_copy(
                src_ref=mine_ref, dst_ref=buf_ref.at[me], send_sem=send_sems.at[k], recv_sem=recv_sems.at[k],
                device_id=(tgt // 4, (tgt // 2) % 2, tgt % 2), device_id_type=MESH))
        for cp in copies:
            cp.start()
        for k in range(1, N_DEV):
            src = (me + N_DEV - k) % N_DEV
            pltpu.make_async_remote_copy(
                src_ref=mine_ref, dst_ref=buf_ref.at[src], send_sem=send_sems.at[k], recv_sem=recv_sems.at[k],
                device_id=(x, y, c), device_id_type=MESH).wait_recv()
        for cp in copies:
            cp.wait_send()
        g = buf_ref[0]
        for j in range(1, N_DEV):
            g = g + buf_ref[j]
        delta, mn, vn = _adamw_math(w_ref[...], g, m_ref[...], v_ref[...])
        g_out[...] = g
        d_out[...] = delta
        m_out[...] = mn
        v_out[...] = vn

    vm = pl.BlockSpec(memory_space=pltpu.VMEM)
    return pl.pallas_call(
        body, name="small_allreduce_adamw", out_shape=[SDS(shape, F32)] * 4, in_specs=[vm] * 4, out_specs=[vm] * 4,
        scratch_shapes=[pltpu.VMEM((N_DEV,) + shape, F32), pltpu.SemaphoreType.DMA((N_DEV,)),
                        pltpu.SemaphoreType.DMA((N_DEV,))],
    )(mine, w, m, v)


def _as2d(a):
    return a.reshape(-1, a.shape[-1])


def _full_weights(stacks):
    wi = stacks["w_in"].transpose(1, 0, 2).reshape(D_MODEL, IN_WIDTH)
    return {
        "w_cat": jnp.concatenate(
            [wi[:, C_QKV:C_Z], wi[:, C_Z:C_BETA], wi[:, C_GA:C_GB], wi[:, C_GB:IN_WIDTH], wi[:, C_POOL:C_QKV],
             wi[:, C_BETA:C_GA], jnp.zeros((D_MODEL, CAT_WIDTH - K_BA - 2 * HEADS), wi.dtype)], axis=1),
        "pool_w": stacks["pool_w"].reshape(N_CHIPS, 4, POOL_GROUP, POOL_OUT_GROUP // N_CHIPS)
                                  .transpose(1, 2, 0, 3).reshape(4, POOL_GROUP, POOL_OUT_GROUP),
        "w_out": stacks["w_out"].reshape(D_MODEL, D_MODEL),
        "w_up": stacks["w_up"],
        "w_down": stacks["w_down"].reshape(D_FF, D_MODEL),
        "ple_gate_w": stacks["ple_gate_w"].reshape(D_MODEL, D_MODEL),
        "ple_proj_w": stacks["ple_proj_w"].transpose(1, 0, 2).reshape(PLE_DIM, D_MODEL),
    }


def _grads_by_chip(grads):
    wire = lambda a: a.astype(WIRE_DTYPE)
    return {
        "w_in": wire(grads["w_in"].reshape(D_MODEL, N_CHIPS, IN_WIDTH // N_CHIPS).transpose(1, 0, 2)),
        "pool_w": wire(grads["pool_w"].reshape(4, POOL_GROUP, N_CHIPS, POOL_OUT_GROUP // N_CHIPS)
                       .transpose(2, 0, 1, 3).reshape(N_CHIPS, 4 * POOL_GROUP, POOL_OUT_GROUP // N_CHIPS)),
        "w_out": wire(grads["w_out"]).reshape(N_CHIPS, D_MODEL // N_CHIPS, D_MODEL),
        "w_up": wire(grads["w_up"]),
        "w_down": wire(grads["w_down"]).reshape(N_CHIPS, D_FF // N_CHIPS, D_MODEL),
        "ple_gate_w": wire(grads["ple_gate_w"]).reshape(N_CHIPS, D_MODEL // N_CHIPS, D_MODEL),
        "ple_proj_w": wire(grads["ple_proj_w"]).reshape(PLE_DIM, N_CHIPS, D_MODEL // N_CHIPS).transpose(1, 0, 2),
    }


def _pad_row(a):
    a = a.reshape(1, -1).astype(F32)
    return jnp.pad(a, ((0, 0), (0, D_MODEL - a.shape[1])))


def kernel(x, p, ln_in_g, ln_in_b, w_in, pool_w, pool_scale, conv_w, a_log, dt_bias, o_norm_w, w_out, ln1_g, ln1_b, w_up, w_down, ple_gate_w, ple_proj_w, ln2_g, ln2_b, loss_target, m_ln_in_g, m_ln_in_b, m_w_in, m_pool_w, m_pool_scale, m_conv_w, m_a_log, m_dt_bias, m_o_norm_w, m_w_out, m_ln1_g, m_ln1_b, m_w_up, m_w_down, m_ple_gate_w, m_ple_proj_w, m_ln2_g, m_ln2_b, v_ln_in_g, v_ln_in_b, v_w_in, v_pool_w, v_pool_scale, v_conv_w, v_a_log, v_dt_bias, v_o_norm_w, v_w_out, v_ln1_g, v_ln1_b, v_w_up, v_w_down, v_ple_gate_w, v_ple_proj_w, v_ln2_g, v_ln2_b):
    given = dict(locals())
    chip = 2 * lax.axis_index("x") + lax.axis_index("y")

    conv_pad = jnp.pad(conv_w[0], ((0, 8 - CONV_K), (0, 0)))
    gathered = _gather_weights([_as2d(given[n]).astype(WIRE_DTYPE) for n in BIG], conv_pad)
    wts = _full_weights({n: g.astype(MXU_DTYPE) for n, g in zip(BIG, gathered[0:len(BIG)])})
    wts.update({
        "conv_w": jnp.concatenate([gathered[len(BIG)][j, 0:CONV_K] for j in range(N_CHIPS)], axis=1),
        "ln_in_g": ln_in_g, "ln_in_b": ln_in_b, "pool_scale": pool_scale[0], "a_log": a_log[0],
        "dt_bias": dt_bias[0], "o_norm_w": o_norm_w[0], "ln1_g": ln1_g[0], "ln1_b": ln1_b[0],
        "ln2_g": ln2_g[0], "ln2_b": ln2_b[0],
    })

    grad_x, grads, loss = _local_step(x[0], p[0, 0], loss_target[0], wts)

    by_chip = _grads_by_chip(grads)
    mine, theirs = _swap_halves([by_chip[n] for n in BIG])
    pair = [_add_pair(a, b, n) for a, b, n in zip(mine, theirs, BIG)]
    landed = _scatter_halves(pair)
    reduced = _join_halves([_sum_slabs(r, n) for r, n in zip(landed, BIG)])
    big_out = {}
    for n, g in zip(BIG, reduced):
        res = _adamw_2d(_as2d(given[n]), g, _as2d(given["m_" + n]), _as2d(given["v_" + n]), n)
        big_out[n] = [r.reshape(given[n].shape) for r in res]

    conv_cols = QKV_WIDTH // N_CHIPS

    def small_rows(get, conv):
        if conv.shape[1] != QKV_WIDTH:
            conv = lax.dynamic_update_slice(jnp.zeros((CONV_K, QKV_WIDTH), F32), conv, (0, chip * conv_cols))
        return [_pad_row(get(n)) for n in SMALL_NAMES], conv.reshape(SMALL_CONV_ROWS, D_MODEL)

    fill = jnp.zeros((SMALL_CONV_AT - len(SMALL_NAMES), D_MODEL), F32)
    rows, conv = small_rows(lambda n: grads[n], grads["conv_w"])
    mine_small = jnp.concatenate(rows + [jnp.full((1, D_MODEL), loss, F32), fill[1:], conv], axis=0)
    packed_small = []
    for prefix in ("", "m_", "v_"):
        rows, conv = small_rows(lambda n: given[prefix + n], given[prefix + "conv_w"][0])
        packed_small.append(jnp.concatenate(rows + [fill, conv], axis=0))
    small_out = _small_allreduce_adamw(mine_small, *packed_small)

    def small_get(k, n):
        if n == "conv_w":
            full = small_out[k][SMALL_CONV_AT:SMALL_CONV_AT + SMALL_CONV_ROWS].reshape(CONV_K, QKV_WIDTH)
            return lax.dynamic_slice(full, (0, chip * conv_cols), (CONV_K, conv_cols)).reshape(given[n].shape)
        i = SMALL_NAMES.index(n)
        return small_out[k][i, 0:given[n].size].reshape(given[n].shape)

    order = ["ln_in_g", "ln_in_b", "w_in", "pool_w", "pool_scale", "conv_w", "a_log", "dt_bias", "o_norm_w", "w_out",
             "ln1_g", "ln1_b", "w_up", "w_down", "ple_gate_w", "ple_proj_w", "ln2_g", "ln2_b"]
    outs = [small_out[0][len(SMALL_NAMES), 0], grad_x[None]]
    for k in range(4):
        for n in order:
            outs.append(big_out[n][k] if n in big_out else small_get(k, n))
    return tuple(outs)
```

```python
import jax
import jax.numpy as jnp
from jax import lax
from jax.experimental import pallas as pl
from jax.experimental.pallas import tpu as pltpu

F32 = jnp.float32
MXU_DTYPE = jnp.bfloat16
WIRE_DTYPE = jnp.bfloat16
SDS = jax.ShapeDtypeStruct

D_MODEL = 1024
POOL_WINDOWS = (2, 4, 8, 16)
POOL_WIDTH = 512
POOL_GROUP = 128
POOL_OUT_GROUP = 256
HEADS = 8
HEAD_DIM = 128
DN_WIDTH = HEADS * HEAD_DIM
QKV_WIDTH = 3 * DN_WIDTH
CONV_K = 4
CHUNK = 128
HEAD_GROUP = 4
D_FF = 4096
PLE_DIM = 256
LN_EPS = 1e-5
RMS_EPS = 1e-6
L2_EPS = 1e-6
ALPHA = 2.0 ** 0.25
Q_SCALE = HEAD_DIM ** -0.5
IN_WIDTH = 6672
C_POOL, C_QKV, C_Z, C_BETA, C_A, C_GA, C_GB = 0, 512, 3584, 4608, 4616, 4624, 5648
K_QKV, K_Z, K_GA, K_GB, K_U, K_BA, CAT_WIDTH = 0, 3072, 4096, 5120, 6144, 6656, 6912

ADAM_LR, ADAM_B1, ADAM_B2, ADAM_EPS, ADAM_WD, ADAM_STEP = 0.001, 0.9, 0.999, 1e-08, 0.01, 10

N_CHIPS = 4
N_DEV = 8
VMEM_LIMIT = 56 * 1024 * 1024

BIG = ("w_in", "pool_w", "w_out", "w_up", "w_down", "ple_gate_w", "ple_proj_w")
SMALL_NAMES = ("ln_in_g", "ln_in_b", "pool_scale", "ln1_g", "ln1_b", "ln2_g", "ln2_b", "o_norm_w", "a_log", "dt_bias")
SMALL_CONV_AT = 12
SMALL_CONV_ROWS = CONV_K * QKV_WIDTH // D_MODEL


def _mx(a):
    return a.astype(MXU_DTYPE)


def _dot(a, b):
    return lax.dot_general(_mx(a), _mx(b), (((1,), (0,)), ((), ())), preferred_element_type=F32)


def _dot_nt(a, b):
    return lax.dot_general(_mx(a), _mx(b), (((1,), (1,)), ((), ())), preferred_element_type=F32)


def _dot_tn(a, b):
    return lax.dot_general(_mx(a), _mx(b), (((0,), (0,)), ((), ())), preferred_element_type=F32)


def _sigmoid(x):
    return 1.0 / (1.0 + jnp.exp(-x))


def _softplus(x):
    return jnp.maximum(x, 0.0) + jnp.log(1.0 + jnp.exp(-jnp.abs(x)))


def _pc(body, name, grid, in_specs, out_specs, out_shape, scratch=(), sem=None):
    return pl.pallas_call(
        body, out_shape=out_shape, grid=grid, in_specs=in_specs, out_specs=out_specs,
        scratch_shapes=scratch, name=name,
        compiler_params=pltpu.CompilerParams(dimension_semantics=sem, vmem_limit_bytes=VMEM_LIMIT))


def _row(tm, n):
    return pl.BlockSpec((tm, n), lambda i: (i, 0))


def _const(shape):
    nd = len(shape)
    return pl.BlockSpec(shape, lambda *_: (0,) * nd)


def _matmul(a, b, mode, name, out_dtype=F32, tm=512, tn=512, tk=512, stack_out=False):
    if mode == "nn":
        (m, k), n = a.shape, b.shape[1]
    elif mode == "nt":
        (m, k), n = a.shape, b.shape[0]
    else:
        (k, m), n = a.shape, b.shape[1]
    tm, tn, tk = min(tm, m), min(tn, n), min(tk, k)
    assert m % tm == 0 and n % tn == 0 and k % tk == 0, (name, m, n, k, tm, tn, tk)
    nk = k // tk
    if mode == "nn":
        a_spec = pl.BlockSpec((tm, tk), lambda i, j, kk: (i, kk))
        b_spec = pl.BlockSpec((tk, tn), lambda i, j, kk: (kk, j))
        dot = _dot
    elif mode == "nt":
        a_spec = pl.BlockSpec((tm, tk), lambda i, j, kk: (i, kk))
        b_spec = pl.BlockSpec((tn, tk), lambda i, j, kk: (j, kk))
        dot = _dot_nt
    else:
        a_spec = pl.BlockSpec((tk, tm), lambda i, j, kk: (kk, i))
        b_spec = pl.BlockSpec((tk, tn), lambda i, j, kk: (kk, j))
        dot = _dot_tn

    def body(a_ref, b_ref, o_ref, acc_ref):
        kk = pl.program_id(2)

        @pl.when(kk == 0)
        def _():
            acc_ref[...] = jnp.zeros_like(acc_ref)

        acc_ref[...] += dot(a_ref[...], b_ref[...])

        @pl.when(kk == nk - 1)
        def _():
            o_ref[...] = acc_ref[...].astype(out_dtype)

    if stack_out:
        o_spec, o_shape = pl.BlockSpec((None, tm, tn), lambda i, j, kk: (j, i, 0)), SDS((n // tn, m, tn), out_dtype)
    else:
        o_spec, o_shape = pl.BlockSpec((tm, tn), lambda i, j, kk: (i, j)), SDS((m, n), out_dtype)
    return _pc(body, name, (m // tm, n // tn, nk), [a_spec, b_spec], o_spec, o_shape,
               scratch=[pltpu.VMEM((tm, tn), F32)], sem=("parallel", "parallel", "arbitrary"))(a, b)


def _ln_stats(x):
    mu = jnp.mean(x, axis=-1, keepdims=True)
    xc = x - mu
    var = jnp.mean(xc * xc, axis=-1, keepdims=True)
    rstd = lax.rsqrt(var + LN_EPS)
    return xc * rstd, rstd


def _ln_bwd(dy, xhat, rstd, g):
    dxh = dy * g
    m1 = jnp.mean(dxh, axis=-1, keepdims=True)
    m2 = jnp.mean(dxh * xhat, axis=-1, keepdims=True)
    return rstd * (dxh - m1 - xhat * m2)


def _ln_in(x, g, b, tm):
    t, d = x.shape

    def body(x_ref, g_ref, b_ref, h_ref, hb_ref):
        xhat, _ = _ln_stats(x_ref[...])
        h = xhat * g_ref[...] + b_ref[...]
        h_ref[...] = h
        hb_ref[...] = _mx(h)

    return _pc(body, "ln_in", (t // tm,), [_row(tm, d), _const((1, d)), _const((1, d))],
               [_row(tm, d), _row(tm, d)], [SDS((t, d), F32), SDS((t, d), MXU_DTYPE)], sem=("parallel",))(x, g, b)


def _pool_fwd(proj, pool_w, tm):
    t = proj.shape[0]
    ublk = K_U // POOL_WIDTH

    def body(u_ref, halo_ref, pw_ref, ypre_ref, d_ref, ext_ref):
        i = pl.program_id(0)
        ext_ref[0:16, :] = jnp.where(i > 0, halo_ref[...], 0.0)
        ext_ref[16:16 + tm, :] = u_ref[...]
        tok = i * tm + lax.broadcasted_iota(jnp.int32, (tm, POOL_GROUP), 0)
        for gi, w in enumerate(POOL_WINDOWS):
            cs = pl.ds(gi * POOL_GROUP, POOL_GROUP)
            ug = ext_ref[pl.ds(16, tm), cs]
            s = ug
            for k in range(1, w):
                s = s + ext_ref[pl.ds(16 - k, tm), cs]
            cnt = jnp.minimum(tok + 1, w).astype(F32)
            db = _mx(s / cnt - ug)
            d_ref[:, gi * POOL_GROUP:(gi + 1) * POOL_GROUP] = db
            ypre_ref[:, gi * POOL_OUT_GROUP:(gi + 1) * POOL_OUT_GROUP] = _dot(db, pw_ref[gi])

    halo = pl.BlockSpec((16, POOL_WIDTH), lambda i: (jnp.maximum(i * (tm // 16) - 1, 0), ublk))
    return _pc(body, "pool_fwd", (t // tm,),
               [pl.BlockSpec((tm, POOL_WIDTH), lambda i: (i, ublk)), halo, _const((4, POOL_GROUP, POOL_OUT_GROUP))],
               [_row(tm, D_MODEL), _row(tm, POOL_WIDTH)],
               [SDS((t, D_MODEL), F32), SDS((t, POOL_WIDTH), MXU_DTYPE)],
               scratch=[pltpu.VMEM((16 + tm, POOL_WIDTH), F32)], sem=("parallel",))(proj, proj, pool_w)


def _pool_bwd(dyp, d_bf, pool_w, tm):
    t = dyp.shape[0]
    n = t // tm

    def body(dy_ref, dyn_ref, d_ref, pw_ref, du_ref, dpw_ref, ext_ref):
        i = pl.program_id(0)

        @pl.when(i == 0)
        def _():
            dpw_ref[...] = jnp.zeros_like(dpw_ref)

        tok = i * tm + lax.broadcasted_iota(jnp.int32, (tm + 16, POOL_GROUP), 0)
        for gi, w in enumerate(POOL_WINDOWS):
            dy = dy_ref[:, gi * POOL_OUT_GROUP:(gi + 1) * POOL_OUT_GROUP]
            dyn = dyn_ref[:, gi * POOL_OUT_GROUP:(gi + 1) * POOL_OUT_GROUP]
            pw = pw_ref[gi]
            dd = _dot_nt(dy, pw)
            ddn = jnp.where(i < n - 1, _dot_nt(dyn, pw), 0.0)
            cnt = jnp.minimum(tok + 1, w).astype(F32)
            ext_ref[0:tm, :] = dd / cnt[0:tm]
            ext_ref[tm:tm + 16, :] = ddn / cnt[tm:tm + 16]
            s = ext_ref[pl.ds(0, tm), :]
            for k in range(1, w):
                s = s + ext_ref[pl.ds(k, tm), :]
            du_ref[:, gi * POOL_GROUP:(gi + 1) * POOL_GROUP] = _mx(s - dd)
            dpw_ref[gi] += _dot_tn(d_ref[:, gi * POOL_GROUP:(gi + 1) * POOL_GROUP], dy)

    nxt = pl.BlockSpec((16, D_MODEL), lambda i: (jnp.minimum((i + 1) * (tm // 16), t // 16 - 1), 0))
    return _pc(body, "pool_bwd", (n,),
               [_row(tm, D_MODEL), nxt, _row(tm, POOL_WIDTH), _const((4, POOL_GROUP, POOL_OUT_GROUP))],
               [_row(tm, POOL_WIDTH), _const((4, POOL_GROUP, POOL_OUT_GROUP))],
               [SDS((t, POOL_WIDTH), MXU_DTYPE), SDS((4, POOL_GROUP, POOL_OUT_GROUP), F32)],
               scratch=[pltpu.VMEM((tm + 16, POOL_GROUP), F32)], sem=("arbitrary",))(dyp, dyp, d_bf, pool_w)


CONV_BLK = 512


def _conv_fwd(proj, conv_w, tm):
    t = proj.shape[0]

    def body(x_ref, halo_ref, w_ref, o_ref, ext_ref):
        i = pl.program_id(0)
        ext_ref[0:8, :] = jnp.where(i > 0, halo_ref[...], 0.0)
        ext_ref[8:8 + tm, :] = x_ref[...]
        y = w_ref[pl.ds(0, 1), :] * ext_ref[pl.ds(5, tm), :]
        for k in range(1, CONV_K):
            y = y + w_ref[pl.ds(k, 1), :] * ext_ref[pl.ds(5 + k, tm), :]
        o_ref[...] = y * _sigmoid(y)

    halo = pl.BlockSpec((8, CONV_BLK), lambda i, j: (jnp.maximum(i * (tm // 8) - 1, 0), j))
    blk = pl.BlockSpec((tm, CONV_BLK), lambda i, j: (i, j))
    return _pc(body, "conv_fwd", (t // tm, QKV_WIDTH // CONV_BLK),
               [blk, halo, pl.BlockSpec((CONV_K, CONV_BLK), lambda i, j: (0, j))], blk,
               SDS((t, QKV_WIDTH), F32), scratch=[pltpu.VMEM((8 + tm, CONV_BLK), F32)],
               sem=("parallel", "parallel"))(proj, proj, conv_w)


def _conv_bwd(dact, proj, conv_w, tm):
    t = proj.shape[0]
    n = t // tm

    def body(da_ref, dan_ref, x_ref, xp_ref, xn_ref, w_ref, dx_ref, dw_ref, ext_ref, dy_ref):
        i = pl.program_id(1)

        @pl.when(i == 0)
        def _():
            dw_ref[...] = jnp.zeros_like(dw_ref)

        ext_ref[0:8, :] = jnp.where(i > 0, xp_ref[...], 0.0)
        ext_ref[8:8 + tm, :] = x_ref[...]
        ext_ref[8 + tm:16 + tm, :] = jnp.where(i < n - 1, xn_ref[...], 0.0)
        y = w_ref[pl.ds(0, 1), :] * ext_ref[pl.ds(5, tm + 8), :]
        for k in range(1, CONV_K):
            y = y + w_ref[pl.ds(k, 1), :] * ext_ref[pl.ds(5 + k, tm + 8), :]
        s = _sigmoid(y)
        dsilu = s * (1.0 + y * (1.0 - s))
        dy_ref[0:tm, :] = da_ref[...] * dsilu[0:tm]
        dy_ref[tm:tm + 8, :] = jnp.where(i < n - 1, dan_ref[...], 0.0) * dsilu[tm:tm + 8]
        dx = w_ref[pl.ds(0, 1), :] * dy_ref[pl.ds(3, tm), :]
        for k in range(1, CONV_K):
            dx = dx + w_ref[pl.ds(k, 1), :] * dy_ref[pl.ds(3 - k, tm), :]
        dx_ref[...] = _mx(dx)
        dy = dy_ref[pl.ds(0, tm), :]
        for k in range(CONV_K):
            dw_ref[pl.ds(k, 1), :] += jnp.sum(dy * ext_ref[pl.ds(5 + k, tm), :], axis=0, keepdims=True)

    blk = pl.BlockSpec((tm, CONV_BLK), lambda j, i: (i, j))
    prev = pl.BlockSpec((8, CONV_BLK), lambda j, i: (jnp.maximum(i * (tm // 8) - 1, 0), j))
    nxt = pl.BlockSpec((8, CONV_BLK), lambda j, i: (jnp.minimum((i + 1) * (tm // 8), t // 8 - 1), j))
    wspec = pl.BlockSpec((CONV_K, CONV_BLK), lambda j, i: (0, j))
    return _pc(body, "conv_bwd", (QKV_WIDTH // CONV_BLK, n),
               [blk, nxt, blk, prev, nxt, wspec],
               [blk, pl.BlockSpec((8, CONV_BLK), lambda j, i: (0, j))],
               [SDS((t, QKV_WIDTH), MXU_DTYPE), SDS((8, QKV_WIDTH), F32)],
               scratch=[pltpu.VMEM((16 + tm, CONV_BLK), F32), pltpu.VMEM((8 + tm, CONV_BLK), F32)],
               sem=("parallel", "arbitrary"))(dact, dact, proj, proj, proj, conv_w)


def _lane(shape):
    return lax.broadcasted_iota(jnp.int32, shape, 1)


def _ba_fwd(proj, al_row, dtb_row, tm):
    t = proj.shape[0]
    bablk = K_BA // 128

    def body(ba_ref, al_ref, dtb_ref, bg_ref):
        ba = ba_ref[...]
        lane = _lane(ba.shape)
        g = -jnp.exp(al_ref[...]) * _softplus(ba + dtb_ref[...])
        bg_ref[...] = jnp.where(lane < HEADS, _sigmoid(ba), jnp.where(lane < 2 * HEADS, g, 0.0))

    return _pc(body, "ba_fwd", (t // tm,),
               [pl.BlockSpec((tm, 128), lambda i: (i, bablk)), _const((1, 128)), _const((1, 128))],
               _row(tm, 128), SDS((t, 128), F32), sem=("parallel",))(proj, al_row, dtb_row)


def _ba_bwd(dbg, bg, proj, al_row, dtb_row, tm):
    t = proj.shape[0]
    bablk = K_BA // 128

    def body(dbg_ref, bg_ref, ba_ref, al_ref, dtb_ref, dba_ref, acc_ref):
        i = pl.program_id(0)

        @pl.when(i == 0)
        def _():
            acc_ref[...] = jnp.zeros_like(acc_ref)

        dbg_v, bg_v, ba = dbg_ref[...], bg_ref[...], ba_ref[...]
        lane = _lane(ba.shape)
        is_g = (lane >= HEADS) & (lane < 2 * HEADS)
        dbeta_raw = dbg_v * bg_v * (1.0 - bg_v)
        da_raw = dbg_v * (-jnp.exp(al_ref[...])) * _sigmoid(ba + dtb_ref[...])
        dba_ref[...] = _mx(jnp.where(lane < HEADS, dbeta_raw, jnp.where(is_g, da_raw, 0.0)))
        acc_ref[0:1, :] += jnp.sum(jnp.where(is_g, dbg_v * bg_v, 0.0), axis=0, keepdims=True)
        acc_ref[1:2, :] += jnp.sum(jnp.where(is_g, da_raw, 0.0), axis=0, keepdims=True)

    return _pc(body, "ba_bwd", (t // tm,),
               [_row(tm, 128), _row(tm, 128), pl.BlockSpec((tm, 128), lambda i: (i, bablk)),
                _const((1, 128)), _const((1, 128))],
               [_row(tm, 128), _const((8, 128))], [SDS((t, 128), MXU_DTYPE), SDS((8, 128), F32)],
               sem=("arbitrary",))(dbg, bg, proj, al_row, dtb_row)


def _each(f, *lists):
    return [f(*a) for a in zip(*lists)]


def _rowsum(a):
    return jnp.sum(a, axis=1, keepdims=True)


def _chunk_terms(qs, ks, bgv, g_rows, hs):
    c = CHUNK
    ii = lax.broadcasted_iota(jnp.int32, (c, c), 0)
    jj = lax.broadcasted_iota(jnp.int32, (c, c), 1)
    lane = _lane(bgv.shape)
    incl = ii >= jj
    beta = [_rowsum(jnp.where(lane == h, bgv, 0.0)) for h in hs]
    g_col = [_rowsum(jnp.where(lane == HEADS + h, bgv, 0.0)) for h in hs]
    rq = _each(lambda q: lax.rsqrt(_rowsum(q * q) + L2_EPS), qs)
    rk = _each(lambda k: lax.rsqrt(_rowsum(k * k) + L2_EPS), ks)
    yq = _each(jnp.multiply, qs, rq)
    kn = _each(jnp.multiply, ks, rk)
    qn = _each(lambda a: a * Q_SCALE, yq)
    gc_col = _each(lambda g: _rowsum(jnp.where(jj <= ii, g, 0.0)), g_rows)
    gc_row = _each(lambda g: jnp.sum(jnp.where(ii <= jj, g, 0.0), axis=0, keepdims=True), g_col)
    dm = _each(lambda a, b: jnp.where(incl, jnp.exp(jnp.where(incl, a - b, 0.0)), 0.0), gc_col, gc_row)
    gl = _each(_rowsum, g_rows)
    eg = _each(jnp.exp, gc_col)
    ek = _each(lambda a, b: jnp.exp(a - b), gl, gc_col)
    egl = _each(jnp.exp, gl)
    kb = _each(jnp.multiply, kn, beta)
    kk = _each(_dot_nt, kb, kn)
    qk = _each(_dot_nt, qn, kn)
    m = _each(lambda a, b: jnp.where(ii > jj, a * b, 0.0), kk, dm)
    attn = _each(jnp.multiply, qk, dm)
    return dict(ii=ii, jj=jj, beta=beta, rq=rq, rk=rk, yq=yq, kn=kn, qn=qn, dm=dm, eg=eg, ek=ek,
                egl=egl, kb=kb, m=m, attn=attn)


def _unit_lower_inverse_minus_identity(ms, ii, jj):
    pair = (ii >> 1) == (jj >> 1)
    ys = _each(lambda m: -jnp.where(pair, m, 0.0), ms)
    s = 1
    while (1 << s) < CHUNK:
        mask = ((ii >> (s + 1)) == (jj >> (s + 1))) & ((ii >> s) != (jj >> s))
        lbs = _each(lambda m: jnp.where(mask, m, 0.0), ms)
        zs = _each(lambda y, lb: lb + _dot(y, lb), ys, lbs)
        ys = _each(lambda y, z: y - z - _dot(z, y), ys, zs)
        s += 1
    return ys


def _head_offsets(group):
    hs = [group * HEAD_GROUP + a for a in range(HEAD_GROUP)]
    return hs, [pl.ds(pl.multiple_of(base + h * HEAD_DIM, HEAD_DIM), HEAD_DIM)
                for base in (0, DN_WIDTH, 2 * DN_WIDTH) for h in hs]


def _dn_local_fwd(qkv_act, bg, bgt):
    t = qkv_act.shape[0]
    nt = t // CHUNK
    c = CHUNK

    def body(qkv_ref, bg_ref, bgt_ref, u_ref, w_ref, qg_ref, kg_ref, attn_ref, y_ref, egl_ref):
        bgv = bg_ref[...]

        def group(gi, carry):
            hs, offs = _head_offsets(gi)
            qo, ko, vo = offs[0:HEAD_GROUP], offs[HEAD_GROUP:2 * HEAD_GROUP], offs[2 * HEAD_GROUP:]
            qs = [qkv_ref[:, o] for o in qo]
            ks = [qkv_ref[:, o] for o in ko]
            vs = [qkv_ref[:, o] for o in vo]
            g_rows = [bgt_ref[pl.ds(HEADS + h, 1), :] for h in hs]
            ct = _chunk_terms(qs, ks, bgv, g_rows, hs)
            ys = _unit_lower_inverse_minus_identity(ct["m"], ct["ii"], ct["jj"])
            vb = _each(jnp.multiply, vs, ct["beta"])
            kbe = _each(jnp.multiply, ct["kb"], ct["eg"])
            us = _each(lambda a, y: a + _dot(y, a), vb, ys)
            ws = _each(lambda a, y: a + _dot(y, a), kbe, ys)
            for a in range(HEAD_GROUP):
                dst = qo[a]
                u_ref[:, dst] = us[a]
                w_ref[:, dst] = _mx(ws[a])
                qg_ref[:, dst] = _mx(ct["qn"][a] * ct["eg"][a])
                kg_ref[:, dst] = _mx(ct["kn"][a] * ct["ek"][a])
                attn_ref[:, dst] = _mx(ct["attn"][a])
                y_ref[:, dst] = _mx(ys[a])
                egl_ref[0, pl.ds(hs[a], 1), :] = jnp.broadcast_to(ct["egl"][a], (1, HEAD_DIM))
            return carry

        lax.fori_loop(0, HEADS // HEAD_GROUP, group, 0)

    wide = _row(c, DN_WIDTH)
    return _pc(body, "dn_local_fwd", (nt,),
               [_row(c, QKV_WIDTH), _row(c, 128), pl.BlockSpec((2 * HEADS, c), lambda i: (0, i))],
               [wide, wide, wide, wide, wide, wide, pl.BlockSpec((1, HEADS, HEAD_DIM), lambda i: (i, 0, 0))],
               [SDS((t, DN_WIDTH), F32)] + [SDS((t, DN_WIDTH), MXU_DTYPE)] * 5 + [SDS((nt, HEADS, HEAD_DIM), F32)],
               sem=("parallel",))(qkv_act, bg, bgt)


def _dn_scan_fwd(u, w, qg, kg, attn, egl):
    t = u.shape[0]
    nt = t // CHUNK
    c = CHUNK
    sls = [slice(h * HEAD_DIM, (h + 1) * HEAD_DIM) for h in range(HEADS)]

    def body(u_ref, w_ref, qg_ref, kg_ref, attn_ref, egl_ref, o_ref, vn_ref, st_ref, s_ref):
        @pl.when(pl.program_id(0) == 0)
        def _():
            s_ref[...] = jnp.zeros_like(s_ref)

        ss = [s_ref[h] for h in range(HEADS)]
        sb = _each(_mx, ss)
        vn = [u_ref[:, sl] - _dot(w_ref[:, sl], b) for sl, b in zip(sls, sb)]
        vnb = _each(_mx, vn)
        oa = [_dot(qg_ref[:, sl], b) for sl, b in zip(sls, sb)]
        ob = [_dot(attn_ref[:, sl], b) for sl, b in zip(sls, vnb)]
        upd = [_dot_tn(kg_ref[:, sl], b) for sl, b in zip(sls, vnb)]
        for h, sl in enumerate(sls):
            st_ref[0, h] = ss[h]
            vn_ref[:, sl] = vnb[h]
            o_ref[:, sl] = oa[h] + ob[h]
            s_ref[h] = ss[h] * egl_ref[0, h:h + 1, :] + upd[h]

    wide = _row(c, DN_WIDTH)
    return _pc(body, "dn_scan_fwd", (nt,),
               [wide] * 5 + [pl.BlockSpec((1, HEADS, HEAD_DIM), lambda i: (i, 0, 0))],
               [wide, wide, pl.BlockSpec((1, HEADS, HEAD_DIM, HEAD_DIM), lambda i: (i, 0, 0, 0))],
               [SDS((t, DN_WIDTH), F32), SDS((t, DN_WIDTH), MXU_DTYPE), SDS((nt, HEADS, HEAD_DIM, HEAD_DIM), F32)],
               scratch=[pltpu.VMEM((HEADS, HEAD_DIM, HEAD_DIM), F32)], sem=("arbitrary",))(u, w, qg, kg, attn, egl)


def _dn_scan_bwd(do, qg, kg, w, attn, vn, states, egl):
    t = do.shape[0]
    nt = t // CHUNK
    c = CHUNK
    sls = [slice(h * HEAD_DIM, (h + 1) * HEAD_DIM) for h in range(HEADS)]

    def body(do_ref, qg_ref, kg_ref, w_ref, attn_ref, vn_ref, st_ref, egl_ref,
             dvn_ref, dkg_ref, dqg_ref, dattn_ref, dw_ref, degl_ref, ds_ref):
        @pl.when(pl.program_id(0) == 0)
        def _():
            ds_ref[...] = jnp.zeros_like(ds_ref)

        dsp = [ds_ref[h] for h in range(HEADS)]
        dsb = _each(_mx, dsp)
        ss = [st_ref[0, h] for h in range(HEADS)]
        sb = _each(_mx, ss)
        dvn = [_dot(kg_ref[:, sl], b) + _dot_tn(attn_ref[:, sl], do_ref[:, sl]) for sl, b in zip(sls, dsb)]
        dvnb = _each(_mx, dvn)
        dkg = [_dot_nt(vn_ref[:, sl], b) for sl, b in zip(sls, dsb)]
        dqg = [_dot_nt(do_ref[:, sl], b) for sl, b in zip(sls, sb)]
        dattn = [_dot_nt(do_ref[:, sl], vn_ref[:, sl]) for sl in sls]
        dwv = [-_dot_nt(a, b) for a, b in zip(dvnb, sb)]
        upd = [_dot_tn(qg_ref[:, sl], do_ref[:, sl]) - _dot_tn(w_ref[:, sl], a) for sl, a in zip(sls, dvnb)]
        for h, sl in enumerate(sls):
            dvn_ref[:, sl] = dvn[h]
            dkg_ref[:, sl] = dkg[h]
            dqg_ref[:, sl] = dqg[h]
            dattn_ref[:, sl] = dattn[h]
            dw_ref[:, sl] = dwv[h]
            degl = jnp.sum(_rowsum(ss[h] * dsp[h]), axis=0, keepdims=True)
            degl_ref[0, h:h + 1, :] = jnp.broadcast_to(degl, (1, HEAD_DIM))
            ds_ref[h] = dsp[h] * egl_ref[0, h:h + 1, :] + upd[h]

    rev = pl.BlockSpec((c, DN_WIDTH), lambda i: (nt - 1 - i, 0))
    rev3 = pl.BlockSpec((1, HEADS, HEAD_DIM), lambda i: (nt - 1 - i, 0, 0))
    rev4 = pl.BlockSpec((1, HEADS, HEAD_DIM, HEAD_DIM), lambda i: (nt - 1 - i, 0, 0, 0))
    return _pc(body, "dn_scan_bwd", (nt,), [rev] * 6 + [rev4, rev3], [rev] * 5 + [rev3],
               [SDS((t, DN_WIDTH), F32)] * 5 + [SDS((nt, HEADS, HEAD_DIM), F32)],
               scratch=[pltpu.VMEM((HEADS, HEAD_DIM, HEAD_DIM), F32)],
               sem=("arbitrary",))(do, qg, kg, w, attn, vn, states, egl)


def _dn_local_bwd(qkv_act, bg, bgt, u, w, ymat, dvn, dw, dqg, dkg, dattn, degl):
    t = qkv_act.shape[0]
    nt = t // CHUNK
    c = CHUNK

    def body(qkv_ref, bg_ref, bgt_ref, u_ref, w_ref, y_ref, du_ref, dw_ref, dqg_ref, dkg_ref, dattn_ref,
             degl_ref, dqkv_ref, dbg_ref):
        bgv = bg_ref[...]
        lane = _lane(bgv.shape)
        rowi = lax.broadcasted_iota(jnp.int32, (c, 1), 0)

        def group(gi, dbg):
            hs, offs = _head_offsets(gi)
            qo, ko, vo = offs[0:HEAD_GROUP], offs[HEAD_GROUP:2 * HEAD_GROUP], offs[2 * HEAD_GROUP:]
            qs = [qkv_ref[:, o] for o in qo]
            ks = [qkv_ref[:, o] for o in ko]
            vs = [qkv_ref[:, o] for o in vo]
            g_rows = [bgt_ref[pl.ds(HEADS + h, 1), :] for h in hs]
            ct = _chunk_terms(qs, ks, bgv, g_rows, hs)
            ii, jj = ct["ii"], ct["jj"]
            beta, eg, ek, kb, kn, qn, dm = ct["beta"], ct["eg"], ct["ek"], ct["kb"], ct["kn"], ct["qn"], ct["dm"]
            ys = [y_ref[:, o] for o in qo]
            du = [du_ref[:, o] for o in qo]
            dwv = [dw_ref[:, o] for o in qo]
            dqg_v = [dqg_ref[:, o] for o in qo]
            dkg_v = [dkg_ref[:, o] for o in qo]
            dattn_v = [dattn_ref[:, o] for o in qo]
            degl_v = [jnp.max(degl_ref[0, pl.ds(h, 1), :], axis=1, keepdims=True) for h in hs]
            dvb = _each(lambda a, y: a + _dot_tn(y, a), du, ys)
            dkbe = _each(lambda a, y: a + _dot_tn(y, a), dwv, ys)
            dm_u = [_dot_nt(a, u_ref[:, o]) for a, o in zip(dvb, qo)]
            dm_w = [_dot_nt(a, w_ref[:, o]) for a, o in zip(dkbe, qo)]
            dms = _each(lambda a, b: jnp.where(ii > jj, -(a + b), 0.0), dm_u, dm_w)
            dkk = _each(jnp.multiply, dms, dm)
            dqk = _each(jnp.multiply, dattn_v, dm)
            gmat = _each(lambda a, b, c_, d: a * b + c_ * d, dms, ct["m"], dattn_v, ct["attn"])
            dkb = _each(lambda a, b, c_, d: _dot(a, b) + c_ * d, dkk, kn, dkbe, eg)
            dk1 = _each(_dot_tn, dkk, kb)
            dk2 = _each(_dot_tn, dqk, qn)
            dq1 = _each(_dot, dqk, kn)
            dk = _each(lambda a, b, c_, d: a + b + c_ * d, dk1, dk2, dkg_v, ek)
            dq = _each(lambda a, b, c_: a + b * c_, dq1, dqg_v, eg)
            deg = _each(lambda a, b, c_, d: _rowsum(a * b) + _rowsum(c_ * d), dqg_v, qn, dkbe, kb)
            dek = _each(lambda a, b: _rowsum(a * b), dkg_v, kn)
            dgl = _each(lambda a, b, c_, d: jnp.sum(a * b, axis=0, keepdims=True) + c_ * d, dek, ek, degl_v, ct["egl"])
            cs_row = _each(lambda g: jnp.sum(g, axis=0, keepdims=True), gmat)
            cs_col = _each(lambda r: _rowsum(jnp.where(ii == jj, r, 0.0)), cs_row)
            dgc = _each(lambda a, b, c_, d, g, e, f: a * b - c_ * d + _rowsum(g) - e + jnp.where(rowi == c - 1, f, 0.0),
                        deg, eg, dek, ek, gmat, cs_col, dgl)
            dgc_row = _each(lambda a: jnp.sum(jnp.where(ii == jj, a, 0.0), axis=0, keepdims=True), dgc)
            dg = _each(lambda r: _rowsum(jnp.where(jj >= ii, r, 0.0)), dgc_row)
            dbeta = _each(lambda a, b, c_, d: _rowsum(a * b) + _rowsum(c_ * d), dkb, kn, dvb, vs)
            dk = _each(lambda a, b, c_: a + b * c_, dk, dkb, beta)
            for a in range(HEAD_GROUP):
                dyq = dq[a] * Q_SCALE
                yq = ct["yq"][a]
                dqkv_ref[:, qo[a]] = ct["rq"][a] * (dyq - yq * _rowsum(yq * dyq))
                dqkv_ref[:, ko[a]] = ct["rk"][a] * (dk[a] - kn[a] * _rowsum(kn[a] * dk[a]))
                dqkv_ref[:, vo[a]] = dvb[a] * beta[a]
                dbg = dbg + jnp.where(lane == hs[a], dbeta[a], 0.0) + jnp.where(lane == HEADS + hs[a], dg[a], 0.0)
            return dbg

        dbg_ref[...] = lax.fori_loop(0, HEADS // HEAD_GROUP, group, jnp.zeros((c, 128), F32))

    wide = _row(c, DN_WIDTH)
    sc3 = pl.BlockSpec((1, HEADS, HEAD_DIM), lambda i: (i, 0, 0))
    return _pc(body, "dn_local_bwd", (nt,),
               [_row(c, QKV_WIDTH), _row(c, 128), pl.BlockSpec((2 * HEADS, c), lambda i: (0, i))] + [wide] * 8 + [sc3],
               [_row(c, QKV_WIDTH), _row(c, 128)], [SDS((t, QKV_WIDTH), F32), SDS((t, 128), F32)],
               sem=("parallel",))(qkv_act, bg, bgt, u, w, ymat, dvn, dw, dqg, dkg, dattn, degl)


def _mix_fwd(o, proj, ypre, pool_scale, wo_row, tm):
    t = o.shape[0]

    def body(o_ref, z_ref, ga_ref, gb_ref, yp_ref, ps_ref, wo_ref, mixed_ref):
        for h in range(HEADS):
            sl = slice(h * HEAD_DIM, (h + 1) * HEAD_DIM)
            oh = o_ref[:, sl]
            on = oh * lax.rsqrt(jnp.mean(oh * oh, axis=1, keepdims=True) + RMS_EPS)
            zh = z_ref[:, sl]
            yb = on * wo_ref[:, sl] * (zh * _sigmoid(zh))
            ya = yp_ref[:, sl] * ps_ref[:, sl]
            mixed_ref[:, sl] = _mx(_sigmoid(ga_ref[:, sl]) * ya + _sigmoid(gb_ref[:, sl]) * yb)

    def col(blk):
        return pl.BlockSpec((tm, D_MODEL), lambda i: (i, blk))

    return _pc(body, "mix_fwd", (t // tm,),
               [_row(tm, D_MODEL), col(K_Z // D_MODEL), col(K_GA // D_MODEL), col(K_GB // D_MODEL), _row(tm, D_MODEL),
                _const((1, D_MODEL)), _const((1, D_MODEL))],
               _row(tm, D_MODEL), SDS((t, D_MODEL), MXU_DTYPE), sem=("parallel",))(
                   o, proj, proj, proj, ypre, pool_scale, wo_row)


def _mix_bwd(da1_bf, w_out, o, proj, ypre, pool_scale, wo_row, tm):
    t = o.shape[0]

    def body(da_ref, wout_ref, o_ref, z_ref, ga_ref, gb_ref, yp_ref, ps_ref, wo_ref,
             do_ref, dz_ref, dga_ref, dgb_ref, dyp_ref, acc_ref):
        i = pl.program_id(0)

        @pl.when(i == 0)
        def _():
            acc_ref[...] = jnp.zeros_like(acc_ref)

        dmixed = _dot_nt(da_ref[...], wout_ref[...])
        for h in range(HEADS):
            sl = slice(h * HEAD_DIM, (h + 1) * HEAD_DIM)
            oh = o_ref[:, sl]
            rs = lax.rsqrt(jnp.mean(oh * oh, axis=1, keepdims=True) + RMS_EPS)
            on = oh * rs
            zh = z_ref[:, sl]
            sz = _sigmoid(zh)
            silu = zh * sz
            woh = wo_ref[:, sl]
            t1 = on * woh
            yb = t1 * silu
            sa = _sigmoid(ga_ref[:, sl])
            sb = _sigmoid(gb_ref[:, sl])
            yp = yp_ref[:, sl]
            psh = ps_ref[:, sl]
            dm = dmixed[:, sl]
            dga_ref[:, sl] = _mx(dm * (yp * psh) * sa * (1.0 - sa))
            dgb_ref[:, sl] = _mx(dm * yb * sb * (1.0 - sb))
            dya = dm * sa
            dyb = dm * sb
            dyp_ref[:, sl] = _mx(dya * psh)
            acc_ref[0:1, sl] += jnp.sum(dya * yp, axis=0, keepdims=True)
            dz_ref[:, sl] = _mx(dyb * t1 * (sz * (1.0 + zh * (1.0 - sz))))
            dt1 = dyb * silu
            acc_ref[1:2, 0:HEAD_DIM] += jnp.sum(dt1 * on, axis=0, keepdims=True)
            don = dt1 * woh
            do_ref[:, sl] = _mx(rs * (don - on * jnp.mean(don * on, axis=1, keepdims=True)))

    def col(blk):
        return pl.BlockSpec((tm, D_MODEL), lambda i: (i, blk))

    r = _row(tm, D_MODEL)
    return _pc(body, "mix_bwd", (t // tm,),
               [r, _const((D_MODEL, D_MODEL)), r, col(K_Z // D_MODEL), col(K_GA // D_MODEL), col(K_GB // D_MODEL), r,
                _const((1, D_MODEL)), _const((1, D_MODEL))],
               [r, r, r, r, r, _const((8, D_MODEL))],
               [SDS((t, D_MODEL), MXU_DTYPE)] * 5 + [SDS((8, D_MODEL), F32)],
               sem=("arbitrary",))(da1_bf, w_out, o, proj, proj, proj, ypre, pool_scale, wo_row)


def _oproj_ln1(mixed, w_out, h0, g1, b1, tm):
    t = mixed.shape[0]

    def body(m_ref, w_ref, h0_ref, g_ref, b_ref, a1_ref, h1_ref, h1b_ref):
        a1 = ALPHA * h0_ref[...] + _dot(m_ref[...], w_ref[...])
        a1_ref[...] = a1
        xhat, _ = _ln_stats(a1)
        h1 = xhat * g_ref[...] + b_ref[...]
        h1_ref[...] = h1
        h1b_ref[...] = _mx(h1)

    r = _row(tm, D_MODEL)
    v = _const((1, D_MODEL))
    return _pc(body, "oproj_ln1", (t // tm,), [r, _const((D_MODEL, D_MODEL)), r, v, v], [r, r, r],
               [SDS((t, D_MODEL), F32), SDS((t, D_MODEL), F32), SDS((t, D_MODEL), MXU_DTYPE)],
               sem=("parallel",))(mixed, w_out, h0, g1, b1)


def _mlp_up(h1_bf, w_up, tm):
    t = h1_bf.shape[0]
    tn = w_up.shape[2]

    def body(h_ref, w_ref, up_ref, act_ref):
        up = _dot(h_ref[...], w_ref[...])
        up_ref[...] = up
        r = jnp.maximum(up, 0.0)
        act_ref[...] = _mx(r * r)

    o = pl.BlockSpec((tm, tn), lambda i, j: (i, j))
    return _pc(body, "mlp_up", (t // tm, D_FF // tn),
               [pl.BlockSpec((tm, D_MODEL), lambda i, j: (i, 0)),
                pl.BlockSpec((None, D_MODEL, tn), lambda i, j: (j, 0, 0))],
               [o, o], [SDS((t, D_FF), F32), SDS((t, D_FF), MXU_DTYPE)], sem=("parallel", "parallel"))(h1_bf, w_up)


def _tail(act, w_down, h1, w_gate, p_bf, w_proj, tgt, g2, b2, tm):
    t = act.shape[0]

    def body(act_ref, wd_ref, h1_ref, wg_ref, p_ref, wp_ref, tgt_ref, g_ref, b_ref,
             dr_ref, drb_ref, dgp_ref, dpp_ref, rb_ref, acc_ref):
        i = pl.program_id(0)

        @pl.when(i == 0)
        def _():
            acc_ref[...] = jnp.zeros_like(acc_ref)

        r = ALPHA * h1_ref[...] + _dot(act_ref[...], wd_ref[...])
        rb = _mx(r)
        rb_ref[...] = rb
        gate = _sigmoid(_dot(rb, wg_ref[...]))
        pp = _dot(p_ref[...], wp_ref[...])
        xhat, rstd = _ln_stats(r + gate * pp)
        g = g_ref[...]
        diff = xhat * g + b_ref[...] - tgt_ref[...]
        dh2 = diff * (1.0 / D_MODEL)
        rowloss = jnp.sum(diff * diff, axis=1, keepdims=True) * (0.5 / D_MODEL)
        acc_ref[0:1, :] += jnp.sum(dh2 * xhat, axis=0, keepdims=True)
        acc_ref[1:2, :] += jnp.sum(dh2, axis=0, keepdims=True)
        acc_ref[2:3, :] += jnp.broadcast_to(jnp.sum(rowloss, axis=0, keepdims=True), (1, D_MODEL))
        da2 = _ln_bwd(dh2, xhat, rstd, g)
        dpp_ref[...] = _mx(da2 * gate)
        dgp = _mx(da2 * pp * gate * (1.0 - gate))
        dgp_ref[...] = dgp
        dr = da2 + _dot_nt(dgp, wg_ref[...])
        dr_ref[...] = dr
        drb_ref[...] = _mx(dr)

    r = _row(tm, D_MODEL)
    v = _const((1, D_MODEL))
    return _pc(body, "tail", (t // tm,),
               [_row(tm, D_FF), _const((D_FF, D_MODEL)), r, _const((D_MODEL, D_MODEL)), _row(tm, PLE_DIM),
                _const((PLE_DIM, D_MODEL)), r, v, v],
               [r, r, r, r, r, _const((8, D_MODEL))],
               [SDS((t, D_MODEL), F32)] + [SDS((t, D_MODEL), MXU_DTYPE)] * 4 + [SDS((8, D_MODEL), F32)],
               sem=("arbitrary",))(act, w_down, h1, w_gate, p_bf, w_proj, tgt, g2, b2)


def _mlp_bwd1(dr_bf, w_down, up, tm, tn):
    t = up.shape[0]

    def body(dr_ref, w_ref, up_ref, dup_ref):
        dact = _dot_nt(dr_ref[...], w_ref[...])
        dup_ref[...] = _mx(dact * (2.0 * jnp.maximum(up_ref[...], 0.0)))

    o = pl.BlockSpec((tm, tn), lambda i, j: (i, j))
    return _pc(body, "mlp_bwd1", (t // tm, D_FF // tn),
               [pl.BlockSpec((tm, D_MODEL), lambda i, j: (i, 0)), pl.BlockSpec((tn, D_MODEL), lambda i, j: (j, 0)), o],
               o, SDS((t, D_FF), MXU_DTYPE), sem=("parallel", "parallel"))(dr_bf, w_down, up)


def _mlp_bwd2(dup, w_up, dr, a1, g1, tm):
    t = dr.shape[0]

    nk, tk = w_up.shape[0], w_up.shape[2]

    def body(dup_ref, w_ref, dr_ref, a1_ref, g_ref, da1_ref, da1b_ref, acc_ref):
        i = pl.program_id(0)

        @pl.when(i == 0)
        def _():
            acc_ref[...] = jnp.zeros_like(acc_ref)

        dh1 = ALPHA * dr_ref[...]
        for kk in range(nk):
            dh1 = dh1 + _dot_nt(dup_ref[:, kk * tk:(kk + 1) * tk], w_ref[kk])
        xhat, rstd = _ln_stats(a1_ref[...])
        acc_ref[0:1, :] += jnp.sum(dh1 * xhat, axis=0, keepdims=True)
        acc_ref[1:2, :] += jnp.sum(dh1, axis=0, keepdims=True)
        da1 = _ln_bwd(dh1, xhat, rstd, g_ref[...])
        da1_ref[...] = da1
        da1b_ref[...] = _mx(da1)

    r = _row(tm, D_MODEL)
    return _pc(body, "mlp_bwd2", (t // tm,),
               [_row(tm, D_FF), _const((nk, D_MODEL, tk)), r, r, _const((1, D_MODEL))],
               [r, r, _const((8, D_MODEL))],
               [SDS((t, D_MODEL), F32), SDS((t, D_MODEL), MXU_DTYPE), SDS((8, D_MODEL), F32)],
               sem=("arbitrary",))(dup, w_up, dr, a1, g1)


def _ln_in_bwd(dproj, w_cat, da1, x, g, tm):
    t = x.shape[0]

    def body(dp_ref, w_ref, da1_ref, x_ref, g_ref, dx_ref, acc_ref):
        i = pl.program_id(0)

        @pl.when(i == 0)
        def _():
            acc_ref[...] = jnp.zeros_like(acc_ref)

        dh0 = _dot_nt(dp_ref[...], w_ref[...]) + ALPHA * da1_ref[...]
        xhat, rstd = _ln_stats(x_ref[...])
        acc_ref[0:1, :] += jnp.sum(dh0 * xhat, axis=0, keepdims=True)
        acc_ref[1:2, :] += jnp.sum(dh0, axis=0, keepdims=True)
        dx_ref[...] = _ln_bwd(dh0, xhat, rstd, g_ref[...])

    r = _row(tm, D_MODEL)
    return _pc(body, "ln_in_bwd", (t // tm,),
               [_row(tm, CAT_WIDTH), _const((D_MODEL, CAT_WIDTH)), r, r, _const((1, D_MODEL))],
               [r, _const((8, D_MODEL))], [SDS((t, D_MODEL), F32), SDS((8, D_MODEL), F32)],
               sem=("arbitrary",))(dproj, w_cat, da1, x, g)


def _local_step(x, p, tgt, wts):
    t = x.shape[0]
    tm = min(512, t)
    tms = min(256, t)
    row = lambda a: a.reshape(1, -1)
    w_cat = wts["w_cat"]
    pool_scale = row(wts["pool_scale"])
    wo_row = jnp.tile(row(wts["o_norm_w"]), (1, HEADS))
    pad8 = jnp.zeros((1, HEADS), F32)
    al_row = jnp.concatenate([pad8, row(wts["a_log"]), jnp.zeros((1, 128 - 2 * HEADS), F32)], axis=1)
    dtb_row = jnp.concatenate([pad8, row(wts["dt_bias"]), jnp.zeros((1, 128 - 2 * HEADS), F32)], axis=1)
    g_in, b_in = row(wts["ln_in_g"]), row(wts["ln_in_b"])
    g1, b1 = row(wts["ln1_g"]), row(wts["ln1_b"])
    g2, b2 = row(wts["ln2_g"]), row(wts["ln2_b"])

    h0, h0_bf = _ln_in(x, g_in, b_in, tm)
    proj = _matmul(h0_bf, w_cat, "nn", "proj", F32, tm=512, tn=1152, tk=1024)
    ypre, d_bf = _pool_fwd(proj, wts["pool_w"], tm)
    qkv_act = _conv_fwd(proj, wts["conv_w"], tm)
    bg = _ba_fwd(proj, al_row, dtb_row, tm)
    bgt = bg[:, :2 * HEADS].T
    u, w, qg, kg, attn, ymat, egl = _dn_local_fwd(qkv_act, bg, bgt)
    o, vn, states = _dn_scan_fwd(u, w, qg, kg, attn, egl)
    mixed = _mix_fwd(o, proj, ypre, pool_scale, wo_row, tm)
    a1, h1, h1_bf = _oproj_ln1(mixed, wts["w_out"], h0, g1, b1, tm)
    up, act = _mlp_up(h1_bf, wts["w_up"], tm)
    p_bf = _mx(p)
    dr, dr_bf, dgp, dpp, r_bf, acc_tail = _tail(act, wts["w_down"], h1, wts["ple_gate_w"], p_bf, wts["ple_proj_w"],
                                                tgt, g2, b2, tms)
    grads = {}
    grads["ple_proj_w"] = _matmul(p_bf, dpp, "tn", "dw_ple_proj", WIRE_DTYPE, tm=256, tn=1024, tk=512)
    grads["ple_gate_w"] = _matmul(r_bf, dgp, "tn", "dw_ple_gate", WIRE_DTYPE, tm=512, tn=1024, tk=512)
    grads["w_down"] = _matmul(act, dr_bf, "tn", "dw_down", WIRE_DTYPE, tm=512, tn=1024, tk=512)
    dup = _mlp_bwd1(dr_bf, wts["w_down"], up, tm, 1024)
    grads["w_up"] = _matmul(h1_bf, dup, "tn", "dw_up", WIRE_DTYPE, tm=512, tn=1024, tk=512, stack_out=True)
    da1, da1_bf, acc_ln1 = _mlp_bwd2(dup, wts["w_up"], dr, a1, g1, tms)
    grads["w_out"] = _matmul(mixed, da1_bf, "tn", "dw_out", WIRE_DTYPE, tm=512, tn=1024, tk=512)
    do, dz, dga, dgb, dyp, acc_mix = _mix_bwd(da1_bf, wts["w_out"], o, proj, ypre, pool_scale, wo_row, tms)
    du_pool, grads["pool_w"] = _pool_bwd(dyp, d_bf, wts["pool_w"], tm)
    dvn, dkg, dqg, dattn, dw, degl = _dn_scan_bwd(do, qg, kg, w, attn, vn, states, egl)
    dqkv_act, dbg = _dn_local_bwd(qkv_act, bg, bgt, u, w, ymat, dvn, dw, dqg, dkg, dattn, degl)
    dqkv, acc_conv = _conv_bwd(dqkv_act, proj, wts["conv_w"], tm)
    dba, acc_ba = _ba_bwd(dbg, bg, proj, al_row, dtb_row, tm)
    dproj = jnp.concatenate([dqkv, dz, dga, dgb, du_pool, dba,
                             jnp.zeros((t, CAT_WIDTH - K_BA - 128), MXU_DTYPE)], axis=1)
    dw_cat = _matmul(h0_bf, dproj, "tn", "dw_in", F32, tm=512, tn=1152, tk=512)
    grad_x, acc_in = _ln_in_bwd(dproj, w_cat, da1, x, g_in, tms)

    grads["w_in"] = jnp.concatenate(
        [dw_cat[:, K_U:K_U + 512], dw_cat[:, K_QKV:K_QKV + 3072], dw_cat[:, K_Z:K_Z + 1024],
         dw_cat[:, K_BA:K_BA + 16], dw_cat[:, K_GA:K_GA + 1024], dw_cat[:, K_GB:K_GB + 1024]], axis=1)
    grads["conv_w"] = acc_conv[0:CONV_K]
    grads["ln_in_g"], grads["ln_in_b"] = acc_in[0], acc_in[1]
    grads["ln1_g"], grads["ln1_b"] = acc_ln1[0], acc_ln1[1]
    grads["ln2_g"], grads["ln2_b"] = acc_tail[0], acc_tail[1]
    grads["pool_scale"] = acc_mix[0]
    grads["o_norm_w"] = acc_mix[1, 0:HEAD_DIM]
    grads["a_log"] = acc_ba[0, HEADS:2 * HEADS]
    grads["dt_bias"] = acc_ba[1, HEADS:2 * HEADS]
    loss = acc_tail[2, 0]
    return grad_x, grads, loss


MESH = pl.DeviceIdType.MESH
ANY = pl.BlockSpec(memory_space=pl.ANY)


def _chip_of(k, x, y):
    chip = (2 * x + y + k) % N_CHIPS
    return chip // 2, chip % 2


def _place():
    x, y, c = lax.axis_index("x"), lax.axis_index("y"), lax.axis_index("c")
    return x, y, c, 2 * x + y


def _half(rows, c):
    return pl.ds(pl.multiple_of(c * (rows // 2), 16), rows // 2)


def _remote(src, dst, send_sem, recv_sem, device_id):
    return pltpu.make_async_remote_copy(src_ref=src, dst_ref=dst, send_sem=send_sem, recv_sem=recv_sem,
                                        device_id=device_id, device_id_type=MESH)


def _tile_rows(rows):
    for tr in (256, 128, 64, 32, 16):
        if rows % tr == 0:
            return tr
    raise ValueError(rows)


def _gather_weights(shards, conv_shard):
    n = len(shards)

    def body(*refs):
        ins, conv_in = refs[0:n], refs[n]
        outs, conv_out = refs[n + 1:2 * n + 1], refs[2 * n + 1]
        send, recv, fsend, frecv, csend, crecv, lsend, lrecv = refs[2 * n + 2:]
        x, y, c, me = _place()
        here, sib = (x, y, c), (x, y, 1 - c)
        own = [_remote(ins[a], outs[a].at[me], lsend.at[a], lrecv.at[a], sib) for a in range(n)]
        own.append(_remote(conv_in, conv_out.at[me], lsend.at[n], lrecv.at[n], sib))
        for cp in own:
            cp.start()
        sends = []
        for k in range(1, N_CHIPS):
            tx, ty = _chip_of(k, x, y)
            for a in range(n):
                mine = _half(shards[a].shape[0], c)
                sends.append(_remote(ins[a].at[mine], outs[a].at[me, mine], send.at[a * N_CHIPS + k],
                                     recv.at[a * N_CHIPS + k], (tx, ty, c)))
            sends.append(_remote(conv_in, conv_out.at[me], csend.at[k], crecv.at[k], (tx, ty, c)))
        for cp in sends:
            cp.start()
        forwards = []
        for k in range(1, N_CHIPS):
            src = (me + N_CHIPS - k) % N_CHIPS
            for a in range(n):
                landed = outs[a].at[src, _half(shards[a].shape[0], c)]
                _remote(landed, landed, send.at[a * N_CHIPS + k], recv.at[a * N_CHIPS + k], here).wait_recv()
                fwd = _remote(landed, landed, fsend.at[a * N_CHIPS + k], frecv.at[a * N_CHIPS + k], sib)
                fwd.start()
                forwards.append(fwd)
            _remote(conv_in, conv_out.at[src], csend.at[k], crecv.at[k], here).wait_recv()
        for k in range(1, N_CHIPS):
            src = (me + N_CHIPS - k) % N_CHIPS
            for a in range(n):
                passed = outs[a].at[src, _half(shards[a].shape[0], 1 - c)]
                _remote(passed, passed, fsend.at[a * N_CHIPS + k], frecv.at[a * N_CHIPS + k], here).wait_recv()
        for cp in sends + forwards:
            cp.wait_send()
        for cp in own:
            cp.wait()

    sems = pltpu.SemaphoreType.DMA((n * N_CHIPS,))
    return pl.pallas_call(
        body, name="gather_weights",
        out_shape=[SDS((N_CHIPS,) + s.shape, s.dtype) for s in shards]
        + [SDS((N_CHIPS,) + conv_shard.shape, conv_shard.dtype)],
        in_specs=[ANY] * (n + 1), out_specs=[ANY] * (n + 1),
        scratch_shapes=[sems, sems, sems, sems, pltpu.SemaphoreType.DMA((N_CHIPS,)),
                        pltpu.SemaphoreType.DMA((N_CHIPS,)), pltpu.SemaphoreType.DMA((n + 1,)),
                        pltpu.SemaphoreType.DMA((n + 1,))],
    )(*shards, conv_shard)


def _swap_halves(gs):
    n = len(gs)

    def body(*refs):
        ins, theirs = refs[0:n], refs[n:2 * n]
        send, recv = refs[2 * n:]
        x, y, c, _ = _place()
        copies = [_remote(ins[a].at[:, _half(gs[a].shape[1], 1 - c)], theirs[a], send.at[a], recv.at[a],
                          (x, y, 1 - c)) for a in range(n)]
        for cp in copies:
            cp.start()
        for cp in copies:
            cp.wait()

    return pl.pallas_call(
        body, name="swap_halves", out_shape=[SDS((N_CHIPS, g.shape[1] // 2, g.shape[2]), g.dtype) for g in gs],
        in_specs=[ANY] * n, out_specs=[ANY] * n, scratch_shapes=[pltpu.SemaphoreType.DMA((n,))] * 2,
    )(*gs)


def _scatter_halves(qs):
    n = len(qs)

    def body(*refs):
        ins, outs = refs[0:n], refs[n:2 * n]
        send, recv = refs[2 * n:]
        x, y, c, me = _place()
        copies = []
        for k in range(1, N_CHIPS):
            tx, ty = _chip_of(k, x, y)
            for a in range(n):
                copies.append(_remote(ins[a].at[2 * tx + ty], outs[a].at[k - 1], send.at[a * N_CHIPS + k],
                                      recv.at[a * N_CHIPS + k], (tx, ty, c)))
        for cp in copies:
            cp.start()
        for cp in copies:
            cp.wait()

    sems = pltpu.SemaphoreType.DMA((n * N_CHIPS,))
    return pl.pallas_call(
        body, name="scatter_halves", out_shape=[SDS((N_CHIPS - 1,) + q.shape[1:], q.dtype) for q in qs],
        in_specs=[ANY] * n, out_specs=[ANY] * n, scratch_shapes=[sems, sems],
    )(*qs)


def _send_to_sibling(hs):
    n = len(hs)

    def body(*refs):
        ins, outs = refs[0:n], refs[n:2 * n]
        send, recv = refs[2 * n:]
        x, y, c, _ = _place()
        copies = [_remote(ins[a], outs[a], send.at[a], recv.at[a], (x, y, 1 - c)) for a in range(n)]
        for cp in copies:
            cp.start()
        for cp in copies:
            cp.wait()

    return pl.pallas_call(
        body, name="send_to_sibling", out_shape=[SDS(h.shape, h.dtype) for h in hs],
        in_specs=[ANY] * n, out_specs=[ANY] * n, scratch_shapes=[pltpu.SemaphoreType.DMA((n,))] * 2,
    )(*hs)


def _add_pair(g, theirs, name):
    _, rows, cols = g.shape
    half = rows // 2
    tr = _tile_rows(half)

    def body(g_ref, t_ref, o_ref):
        own = g_ref[lax.axis_index("c")]
        o_ref[...] = (own.astype(F32) + t_ref[...].astype(F32)).astype(o_ref.dtype)

    blk = pl.BlockSpec((None, tr, cols), lambda j, i: (j, i, 0))
    return _pc(body, "add_" + name, (N_CHIPS, half // tr),
               [pl.BlockSpec((None, 2, tr, cols), lambda j, i: (j, 0, i, 0)), blk], blk,
               SDS((N_CHIPS, half, cols), g.dtype), sem=("parallel", "parallel"))(
                   g.reshape(N_CHIPS, 2, half, cols), theirs)


def _sum_slabs(pair, landed, name):
    _, rows, cols = pair.shape
    tr = _tile_rows(rows)

    def body(p_ref, r_ref, o_ref):
        acc = p_ref[2 * lax.axis_index("x") + lax.axis_index("y")].astype(F32)
        for k in range(N_CHIPS - 1):
            acc = acc + r_ref[k].astype(F32)
        o_ref[...] = acc

    return _pc(body, "sum_" + name, (rows // tr,),
               [pl.BlockSpec((N_CHIPS, tr, cols), lambda i: (0, i, 0)),
                pl.BlockSpec((N_CHIPS - 1, tr, cols), lambda i: (0, i, 0))],
               _row(tr, cols), SDS((rows, cols), F32), sem=("parallel",))(pair, landed)


def _adamw_math(w, g, m, v):
    m = ADAM_B1 * m + (1.0 - ADAM_B1) * g
    v = ADAM_B2 * v + (1.0 - ADAM_B2) * (g * g)
    m_hat = m / (1.0 - ADAM_B1 ** ADAM_STEP)
    v_hat = v / (1.0 - ADAM_B2 ** ADAM_STEP)
    delta = -ADAM_LR * (m_hat / (jnp.sqrt(v_hat) + ADAM_EPS) + ADAM_WD * w)
    return delta, m, v


def _adamw_2d(w, g_own, g_sib, m, v, name):
    rows, cols = w.shape
    tr = _tile_rows(rows // 2)
    nh = rows // 2 // tr

    def body(w_ref, go_ref, gs_ref, m_ref, v_ref, g_out, d_out, m_out, v_out):
        mine = (pl.program_id(0) // nh) == lax.axis_index("c")
        g = jnp.where(mine, go_ref[...], gs_ref[...])
        delta, mn, vn = _adamw_math(w_ref[...], g, m_ref[...], v_ref[...])
        g_out[...] = g
        d_out[...] = delta
        m_out[...] = mn
        v_out[...] = vn

    r = _row(tr, cols)
    h = pl.BlockSpec((tr, cols), lambda i: (i % nh, 0))
    return _pc(body, "adamw_" + name, (rows // tr,), [r, h, h, r, r], [r] * 4, [SDS((rows, cols), F32)] * 4,
               sem=("parallel",))(w, g_own, g_sib, m, v)


def _small_allreduce_adamw(mine, w, m, v):
    shape = mine.shape

    def body(mine_ref, w_ref, m_ref, v_ref, g_out, d_out, m_out, v_out, buf_ref, send_sems, recv_sems):
        x, y, c = lax.axis_index("x"), lax.axis_index("y"), lax.axis_index("c")
        me = 4 * x + 2 * y + c
        buf_ref[me] = mine_ref[...]
        copies = []
        for k in range(1, N_DEV):
            tgt = (me + k) % N_DEV
            copies.append(pltpu.make_async_remote_copy(
                src_ref=mine_ref, dst_ref=buf_ref.at[me], send_sem=send_sems.at[k], recv_sem=recv_sems.at[k],
                device_id=(tgt // 4, (tgt // 2) % 2, tgt % 2), device_id_type=MESH))
        for cp in copies:
            cp.start()
        for k in range(1, N_DEV):
            src = (me + N_DEV - k) % N_DEV
            pltpu.make_async_remote_copy(
                src_ref=mine_ref, dst_ref=buf_ref.at[src], send_sem=send_sems.at[k], recv_sem=recv_sems.at[k],
                device_id=(x, y, c), device_id_type=MESH).wait_recv()
        for cp in copies:
            cp.wait_send()
        g = buf_ref[0]
        for j in range(1, N_DEV):
            g = g + buf_ref[j]
        delta, mn, vn = _adamw_math(w_ref[...], g, m_ref[...], v_ref[...])
        g_out[...] = g
        d_out[...] = delta
        m_out[...] = mn
        v_out[...] = vn

    vm = pl.BlockSpec(memory_space=pltpu.VMEM)
    return pl.pallas_call(
        body, name="small_allreduce_adamw", out_shape=[SDS(shape, F32)] * 4, in_specs=[vm] * 4, out_specs=[vm] * 4,
        scratch_shapes=[pltpu.VMEM((N_DEV,) + shape, F32), pltpu.SemaphoreType.DMA((N_DEV,)),
                        pltpu.SemaphoreType.DMA((N_DEV,))],
    )(mine, w, m, v)


def _as2d(a):
    return a.reshape(-1, a.shape[-1])


def _full_weights(stacks):
    wi = stacks["w_in"].transpose(1, 0, 2).reshape(D_MODEL, IN_WIDTH)
    return {
        "w_cat": jnp.concatenate(
            [wi[:, C_QKV:C_Z], wi[:, C_Z:C_BETA], wi[:, C_GA:C_GB], wi[:, C_GB:IN_WIDTH], wi[:, C_POOL:C_QKV],
             wi[:, C_BETA:C_GA], jnp.zeros((D_MODEL, CAT_WIDTH - K_BA - 2 * HEADS), wi.dtype)], axis=1),
        "pool_w": stacks["pool_w"].reshape(N_CHIPS, 4, POOL_GROUP, POOL_OUT_GROUP // N_CHIPS)
                                  .transpose(1, 2, 0, 3).reshape(4, POOL_GROUP, POOL_OUT_GROUP),
        "w_out": stacks["w_out"].reshape(D_MODEL, D_MODEL),
        "w_up": stacks["w_up"],
        "w_down": stacks["w_down"].reshape(D_FF, D_MODEL),
        "ple_gate_w": stacks["ple_gate_w"].reshape(D_MODEL, D_MODEL),
        "ple_proj_w": stacks["ple_proj_w"].transpose(1, 0, 2).reshape(PLE_DIM, D_MODEL),
    }


def _grads_by_chip(grads):
    wire = lambda a: a.astype(WIRE_DTYPE)
    return {
        "w_in": wire(grads["w_in"].reshape(D_MODEL, N_CHIPS, IN_WIDTH // N_CHIPS).transpose(1, 0, 2)),
        "pool_w": wire(grads["pool_w"].reshape(4, POOL_GROUP, N_CHIPS, POOL_OUT_GROUP // N_CHIPS)
                       .transpose(2, 0, 1, 3).reshape(N_CHIPS, 4 * POOL_GROUP, POOL_OUT_GROUP // N_CHIPS)),
        "w_out": wire(grads["w_out"]).reshape(N_CHIPS, D_MODEL // N_CHIPS, D_MODEL),
        "w_up": wire(grads["w_up"]),
        "w_down": wire(grads["w_down"]).reshape(N_CHIPS, D_FF // N_CHIPS, D_MODEL),
        "ple_gate_w": wire(grads["ple_gate_w"]).reshape(N_CHIPS, D_MODEL // N_CHIPS, D_MODEL),
        "ple_proj_w": wire(grads["ple_proj_w"]).reshape(PLE_DIM, N_CHIPS, D_MODEL // N_CHIPS).transpose(1, 0, 2),
    }


def _pad_row(a):
    a = a.reshape(1, -1).astype(F32)
    return jnp.pad(a, ((0, 0), (0, D_MODEL - a.shape[1])))


def kernel(x, p, ln_in_g, ln_in_b, w_in, pool_w, pool_scale, conv_w, a_log, dt_bias, o_norm_w, w_out, ln1_g, ln1_b, w_up, w_down, ple_gate_w, ple_proj_w, ln2_g, ln2_b, loss_target, m_ln_in_g, m_ln_in_b, m_w_in, m_pool_w, m_pool_scale, m_conv_w, m_a_log, m_dt_bias, m_o_norm_w, m_w_out, m_ln1_g, m_ln1_b, m_w_up, m_w_down, m_ple_gate_w, m_ple_proj_w, m_ln2_g, m_ln2_b, v_ln_in_g, v_ln_in_b, v_w_in, v_pool_w, v_pool_scale, v_conv_w, v_a_log, v_dt_bias, v_o_norm_w, v_w_out, v_ln1_g, v_ln1_b, v_w_up, v_w_down, v_ple_gate_w, v_ple_proj_w, v_ln2_g, v_ln2_b):
    given = dict(locals())
    chip = 2 * lax.axis_index("x") + lax.axis_index("y")

    conv_pad = jnp.pad(conv_w[0], ((0, 8 - CONV_K), (0, 0)))
    gathered = _gather_weights([_as2d(given[n]).astype(WIRE_DTYPE) for n in BIG], conv_pad)
    wts = _full_weights({n: g.astype(MXU_DTYPE) for n, g in zip(BIG, gathered[0:len(BIG)])})
    wts.update({
        "conv_w": jnp.concatenate([gathered[len(BIG)][j, 0:CONV_K] for j in range(N_CHIPS)], axis=1),
        "ln_in_g": ln_in_g, "ln_in_b": ln_in_b, "pool_scale": pool_scale[0], "a_log": a_log[0],
        "dt_bias": dt_bias[0], "o_norm_w": o_norm_w[0], "ln1_g": ln1_g[0], "ln1_b": ln1_b[0],
        "ln2_g": ln2_g[0], "ln2_b": ln2_b[0],
    })

    grad_x, grads, loss = _local_step(x[0], p[0, 0], loss_target[0], wts)

    by_chip = _grads_by_chip(grads)
    theirs = _swap_halves([by_chip[n] for n in BIG])
    pair = [_add_pair(by_chip[n], t, n) for t, n in zip(theirs, BIG)]
    landed = _scatter_halves(pair)
    reduced = [_sum_slabs(q, r, n) for q, r, n in zip(pair, landed, BIG)]
    from_sibling = _send_to_sibling(reduced)
    big_out = {}
    for n, g_own, g_sib in zip(BIG, reduced, from_sibling):
        res = _adamw_2d(_as2d(given[n]), g_own, g_sib, _as2d(given["m_" + n]), _as2d(given["v_" + n]), n)
        big_out[n] = [r.reshape(given[n].shape) for r in res]

    conv_cols = QKV_WIDTH // N_CHIPS

    def small_rows(get, conv):
        if conv.shape[1] != QKV_WIDTH:
            conv = lax.dynamic_update_slice(jnp.zeros((CONV_K, QKV_WIDTH), F32), conv, (0, chip * conv_cols))
        return [_pad_row(get(n)) for n in SMALL_NAMES], conv.reshape(SMALL_CONV_ROWS, D_MODEL)

    fill = jnp.zeros((SMALL_CONV_AT - len(SMALL_NAMES), D_MODEL), F32)
    rows, conv = small_rows(lambda n: grads[n], grads["conv_w"])
    mine_small = jnp.concatenate(rows + [jnp.full((1, D_MODEL), loss, F32), fill[1:], conv], axis=0)
    packed_small = []
    for prefix in ("", "m_", "v_"):
        rows, conv = small_rows(lambda n: given[prefix + n], given[prefix + "conv_w"][0])
        packed_small.append(jnp.concatenate(rows + [fill, conv], axis=0))
    small_out = _small_allreduce_adamw(mine_small, *packed_small)

    def small_get(k, n):
        if n == "conv_w":
            full = small_out[k][SMALL_CONV_AT:SMALL_CONV_AT + SMALL_CONV_ROWS].reshape(CONV_K, QKV_WIDTH)
            return lax.dynamic_slice(full, (0, chip * conv_cols), (CONV_K, conv_cols)).reshape(given[n].shape)
        i = SMALL_NAMES.index(n)
        return small_out[k][i, 0:given[n].size].reshape(given[n].shape)

    order = ["ln_in_g", "ln_in_b", "w_in", "pool_w", "pool_scale", "conv_w", "a_log", "dt_bias", "o_norm_w", "w_out",
             "ln1_g", "ln1_b", "w_up", "w_down", "ple_gate_w", "ple_proj_w", "ln2_g", "ln2_b"]
    outs = [small_out[0][len(SMALL_NAMES), 0], grad_x[None]]
    for k in range(4):
        for n in order:
            outs.append(big_out[n][k] if n in big_out else small_get(k, n))
    return tuple(outs)
```

```python
import jax
import jax.numpy as jnp
from jax import lax
from jax.experimental import pallas as pl
from jax.experimental.pallas import tpu as pltpu

F32 = jnp.float32
MXU_DTYPE = jnp.bfloat16
WIRE_DTYPE = jnp.bfloat16
SDS = jax.ShapeDtypeStruct

D_MODEL = 1024
POOL_WINDOWS = (2, 4, 8, 16)
POOL_WIDTH = 512
POOL_GROUP = 128
POOL_OUT_GROUP = 256
HEADS = 8
HEAD_DIM = 128
DN_WIDTH = HEADS * HEAD_DIM
QKV_WIDTH = 3 * DN_WIDTH
CONV_K = 4
CHUNK = 128
DW_TK = 1024
HEAD_GROUP = 4
D_FF = 4096
PLE_DIM = 256
LN_EPS = 1e-5
RMS_EPS = 1e-6
L2_EPS = 1e-6
ALPHA = 2.0 ** 0.25
Q_SCALE = HEAD_DIM ** -0.5
IN_WIDTH = 6672
C_POOL, C_QKV, C_Z, C_BETA, C_A, C_GA, C_GB = 0, 512, 3584, 4608, 4616, 4624, 5648
K_QKV, K_Z, K_GA, K_GB, K_U, K_BA, CAT_WIDTH = 0, 3072, 4096, 5120, 6144, 6656, 6912

ADAM_LR, ADAM_B1, ADAM_B2, ADAM_EPS, ADAM_WD, ADAM_STEP = 0.001, 0.9, 0.999, 1e-08, 0.01, 10

N_CHIPS = 4
N_DEV = 8
VMEM_LIMIT = 56 * 1024 * 1024

BIG = ("w_in", "pool_w", "w_out", "w_up", "w_down", "ple_gate_w", "ple_proj_w")
SMALL_NAMES = ("ln_in_g", "ln_in_b", "pool_scale", "ln1_g", "ln1_b", "ln2_g", "ln2_b", "o_norm_w", "a_log", "dt_bias")
SMALL_CONV_AT = 12
SMALL_CONV_ROWS = CONV_K * QKV_WIDTH // D_MODEL


def _mx(a):
    return a.astype(MXU_DTYPE)


def _dot(a, b):
    return lax.dot_general(_mx(a), _mx(b), (((1,), (0,)), ((), ())), preferred_element_type=F32)


def _dot_nt(a, b):
    return lax.dot_general(_mx(a), _mx(b), (((1,), (1,)), ((), ())), preferred_element_type=F32)


def _dot_tn(a, b):
    return lax.dot_general(_mx(a), _mx(b), (((0,), (0,)), ((), ())), preferred_element_type=F32)


def _sigmoid(x):
    return 0.5 * jnp.tanh(0.5 * x) + 0.5


def _softplus(x):
    return jnp.maximum(x, 0.0) + jnp.log(1.0 + jnp.exp(-jnp.abs(x)))


def _pc(body, name, grid, in_specs, out_specs, out_shape, scratch=(), sem=None):
    return pl.pallas_call(
        body, out_shape=out_shape, grid=grid, in_specs=in_specs, out_specs=out_specs,
        scratch_shapes=scratch, name=name,
        compiler_params=pltpu.CompilerParams(dimension_semantics=sem, vmem_limit_bytes=VMEM_LIMIT))


def _row(tm, n):
    return pl.BlockSpec((tm, n), lambda i: (i, 0))


def _const(shape):
    nd = len(shape)
    return pl.BlockSpec(shape, lambda *_: (0,) * nd)


def _matmul(a, b, mode, name, out_dtype=F32, tm=512, tn=512, tk=512, stack_out=False):
    if mode == "nn":
        (m, k), n = a.shape, b.shape[1]
    elif mode == "nt":
        (m, k), n = a.shape, b.shape[0]
    else:
        (k, m), n = a.shape, b.shape[1]
    tm, tn, tk = min(tm, m), min(tn, n), min(tk, k)
    assert m % tm == 0 and n % tn == 0 and k % tk == 0, (name, m, n, k, tm, tn, tk)
    nk = k // tk
    if mode == "nn":
        a_spec = pl.BlockSpec((tm, tk), lambda i, j, kk: (i, kk))
        b_spec = pl.BlockSpec((tk, tn), lambda i, j, kk: (kk, j))
        dot = _dot
    elif mode == "nt":
        a_spec = pl.BlockSpec((tm, tk), lambda i, j, kk: (i, kk))
        b_spec = pl.BlockSpec((tn, tk), lambda i, j, kk: (j, kk))
        dot = _dot_nt
    else:
        a_spec = pl.BlockSpec((tk, tm), lambda i, j, kk: (kk, i))
        b_spec = pl.BlockSpec((tk, tn), lambda i, j, kk: (kk, j))
        dot = _dot_tn

    def body(a_ref, b_ref, o_ref, *acc):
        if nk == 1:
            o_ref[...] = dot(a_ref[...], b_ref[...]).astype(out_dtype)
            return
        acc_ref, kk = acc[0], pl.program_id(2)
        part = dot(a_ref[...], b_ref[...])

        @pl.when(kk == 0)
        def _():
            acc_ref[...] = part

        @pl.when((kk > 0) & (kk < nk - 1))
        def _():
            acc_ref[...] += part

        @pl.when(kk == nk - 1)
        def _():
            o_ref[...] = (acc_ref[...] + part).astype(out_dtype)

    if stack_out:
        o_spec, o_shape = pl.BlockSpec((None, tm, tn), lambda i, j, kk: (j, i, 0)), SDS((n // tn, m, tn), out_dtype)
    else:
        o_spec, o_shape = pl.BlockSpec((tm, tn), lambda i, j, kk: (i, j)), SDS((m, n), out_dtype)
    return _pc(body, name, (m // tm, n // tn, nk), [a_spec, b_spec], o_spec, o_shape,
               scratch=[pltpu.VMEM((tm, tn), F32)] if nk > 1 else [],
               sem=("parallel", "parallel", "arbitrary"))(a, b)


def _ln_stats(x):
    mu = jnp.mean(x, axis=-1, keepdims=True)
    xc = x - mu
    var = jnp.mean(xc * xc, axis=-1, keepdims=True)
    rstd = lax.rsqrt(var + LN_EPS)
    return xc * rstd, rstd


def _ln_bwd(dy, xhat, rstd, g):
    dxh = dy * g
    m1 = jnp.mean(dxh, axis=-1, keepdims=True)
    m2 = jnp.mean(dxh * xhat, axis=-1, keepdims=True)
    return rstd * (dxh - m1 - xhat * m2)


def _ln_in(x, g, b, tm):
    t, d = x.shape

    def body(x_ref, g_ref, b_ref, h_ref, hb_ref):
        xhat, _ = _ln_stats(x_ref[...])
        h = xhat * g_ref[...] + b_ref[...]
        h_ref[...] = h
        hb_ref[...] = _mx(h)

    return _pc(body, "ln_in", (t // tm,), [_row(tm, d), _const((1, d)), _const((1, d))],
               [_row(tm, d), _row(tm, d)], [SDS((t, d), F32), SDS((t, d), MXU_DTYPE)], sem=("parallel",))(x, g, b)


def _pool_fwd(proj, pool_w, tm):
    t = proj.shape[0]
    ublk = K_U // POOL_WIDTH

    def body(u_ref, halo_ref, pw_ref, ypre_ref, d_ref, ext_ref):
        i = pl.program_id(0)
        ext_ref[0:16, :] = jnp.where(i > 0, halo_ref[...], 0.0)
        ext_ref[16:16 + tm, :] = u_ref[...]
        tok = i * tm + lax.broadcasted_iota(jnp.int32, (tm, POOL_GROUP), 0)
        for gi, w in enumerate(POOL_WINDOWS):
            cs = pl.ds(gi * POOL_GROUP, POOL_GROUP)
            ug = ext_ref[pl.ds(16, tm), cs]
            s = ug
            for k in range(1, w):
                s = s + ext_ref[pl.ds(16 - k, tm), cs]
            cnt = jnp.minimum(tok + 1, w).astype(F32)
            db = _mx(s / cnt - ug)
            d_ref[:, gi * POOL_GROUP:(gi + 1) * POOL_GROUP] = db
            ypre_ref[:, gi * POOL_OUT_GROUP:(gi + 1) * POOL_OUT_GROUP] = _dot(db, pw_ref[gi])

    halo = pl.BlockSpec((16, POOL_WIDTH), lambda i: (jnp.maximum(i * (tm // 16) - 1, 0), ublk))
    return _pc(body, "pool_fwd", (t // tm,),
               [pl.BlockSpec((tm, POOL_WIDTH), lambda i: (i, ublk)), halo, _const((4, POOL_GROUP, POOL_OUT_GROUP))],
               [_row(tm, D_MODEL), _row(tm, POOL_WIDTH)],
               [SDS((t, D_MODEL), F32), SDS((t, POOL_WIDTH), MXU_DTYPE)],
               scratch=[pltpu.VMEM((16 + tm, POOL_WIDTH), F32)], sem=("parallel",))(proj, proj, pool_w)


def _pool_bwd(dyp, d_bf, pool_w, tm):
    t = dyp.shape[0]
    n = t // tm

    def body(dy_ref, dyn_ref, d_ref, pw_ref, du_ref, dpw_ref, ext_ref):
        i = pl.program_id(0)

        @pl.when(i == 0)
        def _():
            dpw_ref[...] = jnp.zeros_like(dpw_ref)

        tok = i * tm + lax.broadcasted_iota(jnp.int32, (tm + 16, POOL_GROUP), 0)
        for gi, w in enumerate(POOL_WINDOWS):
            dy = dy_ref[:, gi * POOL_OUT_GROUP:(gi + 1) * POOL_OUT_GROUP]
            dyn = dyn_ref[:, gi * POOL_OUT_GROUP:(gi + 1) * POOL_OUT_GROUP]
            pw = pw_ref[gi]
            dd = _dot_nt(dy, pw)
            ddn = jnp.where(i < n - 1, _dot_nt(dyn, pw), 0.0)
            cnt = jnp.minimum(tok + 1, w).astype(F32)
            ext_ref[0:tm, :] = dd / cnt[0:tm]
            ext_ref[tm:tm + 16, :] = ddn / cnt[tm:tm + 16]
            s = ext_ref[pl.ds(0, tm), :]
            for k in range(1, w):
                s = s + ext_ref[pl.ds(k, tm), :]
            du_ref[:, gi * POOL_GROUP:(gi + 1) * POOL_GROUP] = _mx(s - dd)
            dpw_ref[gi] += _dot_tn(d_ref[:, gi * POOL_GROUP:(gi + 1) * POOL_GROUP], dy)

    nxt = pl.BlockSpec((16, D_MODEL), lambda i: (jnp.minimum((i + 1) * (tm // 16), t // 16 - 1), 0))
    return _pc(body, "pool_bwd", (n,),
               [_row(tm, D_MODEL), nxt, _row(tm, POOL_WIDTH), _const((4, POOL_GROUP, POOL_OUT_GROUP))],
               [_row(tm, POOL_WIDTH), _const((4, POOL_GROUP, POOL_OUT_GROUP))],
               [SDS((t, POOL_WIDTH), MXU_DTYPE), SDS((4, POOL_GROUP, POOL_OUT_GROUP), F32)],
               scratch=[pltpu.VMEM((tm + 16, POOL_GROUP), F32)], sem=("arbitrary",))(dyp, dyp, d_bf, pool_w)


CONV_BLK = 512


CONV_ROWS = 32


def _conv_rows(ext_ref, w, r, rows):
    y = w[0] * ext_ref[pl.ds(r + 5, rows), :]
    for k in range(1, CONV_K):
        y = y + w[k] * ext_ref[pl.ds(r + 5 + k, rows), :]
    return y


def _conv_fwd(proj, conv_w, tm):
    t = proj.shape[0]

    def body(x_ref, halo_ref, w_ref, o_ref, ext_ref):
        i = pl.program_id(0)
        ext_ref[0:8, :] = jnp.where(i > 0, halo_ref[...], 0.0)
        ext_ref[8:8 + tm, :] = x_ref[...]
        w = [w_ref[pl.ds(k, 1), :] for k in range(CONV_K)]
        for r in range(0, tm, CONV_ROWS):
            y = _conv_rows(ext_ref, w, r, CONV_ROWS)
            o_ref[pl.ds(r, CONV_ROWS), :] = y * _sigmoid(y)

    halo = pl.BlockSpec((8, CONV_BLK), lambda i, j: (jnp.maximum(i * (tm // 8) - 1, 0), j))
    blk = pl.BlockSpec((tm, CONV_BLK), lambda i, j: (i, j))
    return _pc(body, "conv_fwd", (t // tm, QKV_WIDTH // CONV_BLK),
               [blk, halo, pl.BlockSpec((CONV_K, CONV_BLK), lambda i, j: (0, j))], blk,
               SDS((t, QKV_WIDTH), F32), scratch=[pltpu.VMEM((8 + tm, CONV_BLK), F32)],
               sem=("parallel", "parallel"))(proj, proj, conv_w)


def _conv_bwd(dact, proj, conv_w, tm):
    t = proj.shape[0]
    n = t // tm

    def body(da_ref, dan_ref, x_ref, xp_ref, xn_ref, w_ref, dx_ref, dw_ref, ext_ref, dy_ref):
        i = pl.program_id(1)

        @pl.when(i == 0)
        def _():
            dw_ref[...] = jnp.zeros_like(dw_ref)

        ext_ref[0:8, :] = jnp.where(i > 0, xp_ref[...], 0.0)
        ext_ref[8:8 + tm, :] = x_ref[...]
        ext_ref[8 + tm:16 + tm, :] = jnp.where(i < n - 1, xn_ref[...], 0.0)
        w = [w_ref[pl.ds(k, 1), :] for k in range(CONV_K)]

        def dsilu_rows(r, rows):
            y = _conv_rows(ext_ref, w, r, rows)
            s = _sigmoid(y)
            return s * (1.0 + y * (1.0 - s))

        for r in range(0, tm, CONV_ROWS):
            dy_ref[pl.ds(r, CONV_ROWS), :] = da_ref[pl.ds(r, CONV_ROWS), :] * dsilu_rows(r, CONV_ROWS)
        dy_ref[tm:tm + 8, :] = jnp.where(i < n - 1, dan_ref[...], 0.0) * dsilu_rows(tm, 8)
        acc = [jnp.zeros((8, CONV_BLK), F32) for _ in range(CONV_K)]
        for r in range(0, tm, CONV_ROWS):
            dx = w[0] * dy_ref[pl.ds(r + 3, CONV_ROWS), :]
            for k in range(1, CONV_K):
                dx = dx + w[k] * dy_ref[pl.ds(r + 3 - k, CONV_ROWS), :]
            dx_ref[pl.ds(r, CONV_ROWS), :] = _mx(dx)
            dy = dy_ref[pl.ds(r, CONV_ROWS), :]
            for k in range(CONV_K):
                prod = dy * ext_ref[pl.ds(r + 5 + k, CONV_ROWS), :]
                for q in range(0, CONV_ROWS, 8):
                    acc[k] = acc[k] + prod[q:q + 8]
        for k in range(CONV_K):
            dw_ref[pl.ds(k, 1), :] += jnp.sum(acc[k], axis=0, keepdims=True)

    blk = pl.BlockSpec((tm, CONV_BLK), lambda j, i: (i, j))
    prev = pl.BlockSpec((8, CONV_BLK), lambda j, i: (jnp.maximum(i * (tm // 8) - 1, 0), j))
    nxt = pl.BlockSpec((8, CONV_BLK), lambda j, i: (jnp.minimum((i + 1) * (tm // 8), t // 8 - 1), j))
    wspec = pl.BlockSpec((CONV_K, CONV_BLK), lambda j, i: (0, j))
    return _pc(body, "conv_bwd", (QKV_WIDTH // CONV_BLK, n),
               [blk, nxt, blk, prev, nxt, wspec],
               [blk, pl.BlockSpec((8, CONV_BLK), lambda j, i: (0, j))],
               [SDS((t, QKV_WIDTH), MXU_DTYPE), SDS((8, QKV_WIDTH), F32)],
               scratch=[pltpu.VMEM((16 + tm, CONV_BLK), F32), pltpu.VMEM((8 + tm, CONV_BLK), F32)],
               sem=("parallel", "arbitrary"))(dact, dact, proj, proj, proj, conv_w)


def _lane(shape):
    return lax.broadcasted_iota(jnp.int32, shape, 1)


def _ba_fwd(proj, al_row, dtb_row, tm):
    t = proj.shape[0]
    bablk = K_BA // 128

    def body(ba_ref, al_ref, dtb_ref, bg_ref):
        ba = ba_ref[...]
        lane = _lane(ba.shape)
        g = -jnp.exp(al_ref[...]) * _softplus(ba + dtb_ref[...])
        bg_ref[...] = jnp.where(lane < HEADS, _sigmoid(ba), jnp.where(lane < 2 * HEADS, g, 0.0))

    return _pc(body, "ba_fwd", (t // tm,),
               [pl.BlockSpec((tm, 128), lambda i: (i, bablk)), _const((1, 128)), _const((1, 128))],
               _row(tm, 128), SDS((t, 128), F32), sem=("parallel",))(proj, al_row, dtb_row)


def _ba_bwd(dbg, bg, proj, al_row, dtb_row, tm):
    t = proj.shape[0]
    bablk = K_BA // 128

    def body(dbg_ref, bg_ref, ba_ref, al_ref, dtb_ref, dba_ref, acc_ref):
        i = pl.program_id(0)

        @pl.when(i == 0)
        def _():
            acc_ref[...] = jnp.zeros_like(acc_ref)

        dbg_v, bg_v, ba = dbg_ref[...], bg_ref[...], ba_ref[...]
        lane = _lane(ba.shape)
        is_g = (lane >= HEADS) & (lane < 2 * HEADS)
        dbeta_raw = dbg_v * bg_v * (1.0 - bg_v)
        da_raw = dbg_v * (-jnp.exp(al_ref[...])) * _sigmoid(ba + dtb_ref[...])
        dba_ref[...] = _mx(jnp.where(lane < HEADS, dbeta_raw, jnp.where(is_g, da_raw, 0.0)))
        acc_ref[0:1, :] += jnp.sum(jnp.where(is_g, dbg_v * bg_v, 0.0), axis=0, keepdims=True)
        acc_ref[1:2, :] += jnp.sum(jnp.where(is_g, da_raw, 0.0), axis=0, keepdims=True)

    return _pc(body, "ba_bwd", (t // tm,),
               [_row(tm, 128), _row(tm, 128), pl.BlockSpec((tm, 128), lambda i: (i, bablk)),
                _const((1, 128)), _const((1, 128))],
               [_row(tm, 128), _const((8, 128))], [SDS((t, 128), MXU_DTYPE), SDS((8, 128), F32)],
               sem=("arbitrary",))(dbg, bg, proj, al_row, dtb_row)


def _each(f, *lists):
    return [f(*a) for a in zip(*lists)]


def _rowsum(a):
    return jnp.sum(a, axis=1, keepdims=True)


def _chunk_terms(qs, ks, bgv, g_rows, hs):
    c = CHUNK
    ii = lax.broadcasted_iota(jnp.int32, (c, c), 0)
    jj = lax.broadcasted_iota(jnp.int32, (c, c), 1)
    lane = _lane(bgv.shape)
    incl = ii >= jj
    beta = [_rowsum(jnp.where(lane == h, bgv, 0.0)) for h in hs]
    g_col = [_rowsum(jnp.where(lane == HEADS + h, bgv, 0.0)) for h in hs]
    rq = _each(lambda q: lax.rsqrt(_rowsum(q * q) + L2_EPS), qs)
    rk = _each(lambda k: lax.rsqrt(_rowsum(k * k) + L2_EPS), ks)
    yq = _each(jnp.multiply, qs, rq)
    kn = _each(jnp.multiply, ks, rk)
    qn = _each(lambda a: a * Q_SCALE, yq)
    gc_col = _each(lambda g: _rowsum(jnp.where(jj <= ii, g, 0.0)), g_rows)
    gc_row = _each(lambda g: jnp.sum(jnp.where(ii <= jj, g, 0.0), axis=0, keepdims=True), g_col)
    dm = _each(lambda a, b: jnp.where(incl, jnp.exp(jnp.where(incl, a - b, 0.0)), 0.0), gc_col, gc_row)
    gl = _each(_rowsum, g_rows)
    eg = _each(jnp.exp, gc_col)
    ek = _each(lambda a, b: jnp.exp(a - b), gl, gc_col)
    egl = _each(jnp.exp, gl)
    kb = _each(jnp.multiply, kn, beta)
    kk = _each(_dot_nt, kb, kn)
    qk = _each(_dot_nt, qn, kn)
    m = _each(lambda a, b: jnp.where(ii > jj, a * b, 0.0), kk, dm)
    attn = _each(jnp.multiply, qk, dm)
    return dict(ii=ii, jj=jj, beta=beta, rq=rq, rk=rk, yq=yq, kn=kn, qn=qn, dm=dm, eg=eg, ek=ek,
                egl=egl, kb=kb, m=m, attn=attn)


def _unit_lower_inverse_minus_identity(ms, ii, jj):
    pair = (ii >> 1) == (jj >> 1)
    ys = _each(lambda m: -jnp.where(pair, m, 0.0), ms)
    s = 1
    while (1 << s) < CHUNK:
        mask = ((ii >> (s + 1)) == (jj >> (s + 1))) & ((ii >> s) != (jj >> s))
        lbs = _each(lambda m: jnp.where(mask, m, 0.0), ms)
        zs = _each(lambda y, lb: lb + _dot(y, lb), ys, lbs)
        ys = _each(lambda y, z: y - z - _dot(z, y), ys, zs)
        s += 1
    return ys


def _head_offsets(group):
    hs = [group * HEAD_GROUP + a for a in range(HEAD_GROUP)]
    return hs, [pl.ds(pl.multiple_of(base + h * HEAD_DIM, HEAD_DIM), HEAD_DIM)
                for base in (0, DN_WIDTH, 2 * DN_WIDTH) for h in hs]


def _dn_local_fwd(qkv_act, bg, bgt):
    t = qkv_act.shape[0]
    nt = t // CHUNK
    c = CHUNK

    def body(qkv_ref, bg_ref, bgt_ref, u_ref, w_ref, qg_ref, kg_ref, attn_ref, y_ref, egl_ref):
        bgv = bg_ref[...]

        def group(gi, carry):
            hs, offs = _head_offsets(gi)
            qo, ko, vo = offs[0:HEAD_GROUP], offs[HEAD_GROUP:2 * HEAD_GROUP], offs[2 * HEAD_GROUP:]
            qs = [qkv_ref[:, o] for o in qo]
            ks = [qkv_ref[:, o] for o in ko]
            vs = [qkv_ref[:, o] for o in vo]
            g_rows = [bgt_ref[pl.ds(HEADS + h, 1), :] for h in hs]
            ct = _chunk_terms(qs, ks, bgv, g_rows, hs)
            ys = _unit_lower_inverse_minus_identity(ct["m"], ct["ii"], ct["jj"])
            vb = _each(jnp.multiply, vs, ct["beta"])
            kbe = _each(jnp.multiply, ct["kb"], ct["eg"])
            us = _each(lambda a, y: a + _dot(y, a), vb, ys)
            ws = _each(lambda a, y: a + _dot(y, a), kbe, ys)
            for a in range(HEAD_GROUP):
                dst = qo[a]
                u_ref[:, dst] = us[a]
                w_ref[:, dst] = _mx(ws[a])
                qg_ref[:, dst] = _mx(ct["qn"][a] * ct["eg"][a])
                kg_ref[:, dst] = _mx(ct["kn"][a] * ct["ek"][a])
                attn_ref[:, dst] = _mx(ct["attn"][a])
                y_ref[:, dst] = _mx(ys[a])
                egl_ref[0, pl.ds(hs[a], 1), :] = jnp.broadcast_to(ct["egl"][a], (1, HEAD_DIM))
            return carry

        lax.fori_loop(0, HEADS // HEAD_GROUP, group, 0)

    wide = _row(c, DN_WIDTH)
    return _pc(body, "dn_local_fwd", (nt,),
               [_row(c, QKV_WIDTH), _row(c, 128), pl.BlockSpec((2 * HEADS, c), lambda i: (0, i))],
               [wide, wide, wide, wide, wide, wide, pl.BlockSpec((1, HEADS, HEAD_DIM), lambda i: (i, 0, 0))],
               [SDS((t, DN_WIDTH), F32)] + [SDS((t, DN_WIDTH), MXU_DTYPE)] * 5 + [SDS((nt, HEADS, HEAD_DIM), F32)],
               sem=("parallel",))(qkv_act, bg, bgt)


def _dn_scan_fwd(u, w, qg, kg, attn, egl):
    t = u.shape[0]
    nt = t // CHUNK
    c = CHUNK
    sls = [slice(h * HEAD_DIM, (h + 1) * HEAD_DIM) for h in range(HEADS)]

    def body(u_ref, w_ref, qg_ref, kg_ref, attn_ref, egl_ref, o_ref, vn_ref, st_ref, s_ref):
        @pl.when(pl.program_id(0) == 0)
        def _():
            s_ref[...] = jnp.zeros_like(s_ref)

        ss = [s_ref[h] for h in range(HEADS)]
        sb = _each(_mx, ss)
        vn = [u_ref[:, sl] - _dot(w_ref[:, sl], b) for sl, b in zip(sls, sb)]
        vnb = _each(_mx, vn)
        oa = [_dot(qg_ref[:, sl], b) for sl, b in zip(sls, sb)]
        ob = [_dot(attn_ref[:, sl], b) for sl, b in zip(sls, vnb)]
        upd = [_dot_tn(kg_ref[:, sl], b) for sl, b in zip(sls, vnb)]
        for h, sl in enumerate(sls):
            st_ref[0, h] = ss[h]
            vn_ref[:, sl] = vnb[h]
            o_ref[:, sl] = oa[h] + ob[h]
            s_ref[h] = ss[h] * egl_ref[0, h:h + 1, :] + upd[h]

    wide = _row(c, DN_WIDTH)
    return _pc(body, "dn_scan_fwd", (nt,),
               [wide] * 5 + [pl.BlockSpec((1, HEADS, HEAD_DIM), lambda i: (i, 0, 0))],
               [wide, wide, pl.BlockSpec((1, HEADS, HEAD_DIM, HEAD_DIM), lambda i: (i, 0, 0, 0))],
               [SDS((t, DN_WIDTH), F32), SDS((t, DN_WIDTH), MXU_DTYPE), SDS((nt, HEADS, HEAD_DIM, HEAD_DIM), F32)],
               scratch=[pltpu.VMEM((HEADS, HEAD_DIM, HEAD_DIM), F32)], sem=("arbitrary",))(u, w, qg, kg, attn, egl)


def _dn_scan_bwd(do, qg, kg, w, attn, vn, states, egl):
    t = do.shape[0]
    nt = t // CHUNK
    c = CHUNK
    sls = [slice(h * HEAD_DIM, (h + 1) * HEAD_DIM) for h in range(HEADS)]

    def body(do_ref, qg_ref, kg_ref, w_ref, attn_ref, vn_ref, st_ref, egl_ref,
             dvn_ref, dkg_ref, dqg_ref, dattn_ref, dw_ref, degl_ref, ds_ref):
        @pl.when(pl.program_id(0) == 0)
        def _():
            ds_ref[...] = jnp.zeros_like(ds_ref)

        dsp = [ds_ref[h] for h in range(HEADS)]
        dsb = _each(_mx, dsp)
        ss = [st_ref[0, h] for h in range(HEADS)]
        sb = _each(_mx, ss)
        dvn = [_dot(kg_ref[:, sl], b) + _dot_tn(attn_ref[:, sl], do_ref[:, sl]) for sl, b in zip(sls, dsb)]
        dvnb = _each(_mx, dvn)
        dkg = [_dot_nt(vn_ref[:, sl], b) for sl, b in zip(sls, dsb)]
        dqg = [_dot_nt(do_ref[:, sl], b) for sl, b in zip(sls, sb)]
        dattn = [_dot_nt(do_ref[:, sl], vn_ref[:, sl]) for sl in sls]
        dwv = [-_dot_nt(a, b) for a, b in zip(dvnb, sb)]
        upd = [_dot_tn(qg_ref[:, sl], do_ref[:, sl]) - _dot_tn(w_ref[:, sl], a) for sl, a in zip(sls, dvnb)]
        for h, sl in enumerate(sls):
            dvn_ref[:, sl] = dvn[h]
            dkg_ref[:, sl] = dkg[h]
            dqg_ref[:, sl] = dqg[h]
            dattn_ref[:, sl] = dattn[h]
            dw_ref[:, sl] = dwv[h]
            degl = jnp.sum(_rowsum(ss[h] * dsp[h]), axis=0, keepdims=True)
            degl_ref[0, h:h + 1, :] = jnp.broadcast_to(degl, (1, HEAD_DIM))
            ds_ref[h] = dsp[h] * egl_ref[0, h:h + 1, :] + upd[h]

    rev = pl.BlockSpec((c, DN_WIDTH), lambda i: (nt - 1 - i, 0))
    rev3 = pl.BlockSpec((1, HEADS, HEAD_DIM), lambda i: (nt - 1 - i, 0, 0))
    rev4 = pl.BlockSpec((1, HEADS, HEAD_DIM, HEAD_DIM), lambda i: (nt - 1 - i, 0, 0, 0))
    return _pc(body, "dn_scan_bwd", (nt,), [rev] * 6 + [rev4, rev3], [rev] * 5 + [rev3],
               [SDS((t, DN_WIDTH), F32)] * 5 + [SDS((nt, HEADS, HEAD_DIM), F32)],
               scratch=[pltpu.VMEM((HEADS, HEAD_DIM, HEAD_DIM), F32)],
               sem=("arbitrary",))(do, qg, kg, w, attn, vn, states, egl)


def _dn_local_bwd(qkv_act, bg, bgt, u, w, ymat, dvn, dw, dqg, dkg, dattn, degl):
    t = qkv_act.shape[0]
    nt = t // CHUNK
    c = CHUNK

    def body(qkv_ref, bg_ref, bgt_ref, u_ref, w_ref, y_ref, du_ref, dw_ref, dqg_ref, dkg_ref, dattn_ref,
             degl_ref, dqkv_ref, dbg_ref):
        bgv = bg_ref[...]
        lane = _lane(bgv.shape)
        rowi = lax.broadcasted_iota(jnp.int32, (c, 1), 0)

        def group(gi, dbg):
            hs, offs = _head_offsets(gi)
            qo, ko, vo = offs[0:HEAD_GROUP], offs[HEAD_GROUP:2 * HEAD_GROUP], offs[2 * HEAD_GROUP:]
            qs = [qkv_ref[:, o] for o in qo]
            ks = [qkv_ref[:, o] for o in ko]
            vs = [qkv_ref[:, o] for o in vo]
            g_rows = [bgt_ref[pl.ds(HEADS + h, 1), :] for h in hs]
            ct = _chunk_terms(qs, ks, bgv, g_rows, hs)
            ii, jj = ct["ii"], ct["jj"]
            beta, eg, ek, kb, kn, qn, dm = ct["beta"], ct["eg"], ct["ek"], ct["kb"], ct["kn"], ct["qn"], ct["dm"]
            ys = [y_ref[:, o] for o in qo]
            du = [du_ref[:, o] for o in qo]
            dwv = [dw_ref[:, o] for o in qo]
            dqg_v = [dqg_ref[:, o] for o in qo]
            dkg_v = [dkg_ref[:, o] for o in qo]
            dattn_v = [dattn_ref[:, o] for o in qo]
            degl_v = [jnp.max(degl_ref[0, pl.ds(h, 1), :], axis=1, keepdims=True) for h in hs]
            dvb = _each(lambda a, y: a + _dot_tn(y, a), du, ys)
            dkbe = _each(lambda a, y: a + _dot_tn(y, a), dwv, ys)
            dm_u = [_dot_nt(a, u_ref[:, o]) for a, o in zip(dvb, qo)]
            dm_w = [_dot_nt(a, w_ref[:, o]) for a, o in zip(dkbe, qo)]
            dms = _each(lambda a, b: jnp.where(ii > jj, -(a + b), 0.0), dm_u, dm_w)
            dkk = _each(jnp.multiply, dms, dm)
            dqk = _each(jnp.multiply, dattn_v, dm)
            gmat = _each(lambda a, b, c_, d: a * b + c_ * d, dms, ct["m"], dattn_v, ct["attn"])
            dkb = _each(lambda a, b, c_, d: _dot(a, b) + c_ * d, dkk, kn, dkbe, eg)
            dk1 = _each(_dot_tn, dkk, kb)
            dk2 = _each(_dot_tn, dqk, qn)
            dq1 = _each(_dot, dqk, kn)
            dk = _each(lambda a, b, c_, d: a + b + c_ * d, dk1, dk2, dkg_v, ek)
            dq = _each(lambda a, b, c_: a + b * c_, dq1, dqg_v, eg)
            deg = _each(lambda a, b, c_, d: _rowsum(a * b) + _rowsum(c_ * d), dqg_v, qn, dkbe, kb)
            dek = _each(lambda a, b: _rowsum(a * b), dkg_v, kn)
            dgl = _each(lambda a, b, c_, d: jnp.sum(a * b, axis=0, keepdims=True) + c_ * d, dek, ek, degl_v, ct["egl"])
            cs_row = _each(lambda g: jnp.sum(g, axis=0, keepdims=True), gmat)
            cs_col = _each(lambda r: _rowsum(jnp.where(ii == jj, r, 0.0)), cs_row)
            dgc = _each(lambda a, b, c_, d, g, e, f: a * b - c_ * d + _rowsum(g) - e + jnp.where(rowi == c - 1, f, 0.0),
                        deg, eg, dek, ek, gmat, cs_col, dgl)
            dgc_row = _each(lambda a: jnp.sum(jnp.where(ii == jj, a, 0.0), axis=0, keepdims=True), dgc)
            dg = _each(lambda r: _rowsum(jnp.where(jj >= ii, r, 0.0)), dgc_row)
            dbeta = _each(lambda a, b, c_, d: _rowsum(a * b) + _rowsum(c_ * d), dkb, kn, dvb, vs)
            dk = _each(lambda a, b, c_: a + b * c_, dk, dkb, beta)
            for a in range(HEAD_GROUP):
                dyq = dq[a] * Q_SCALE
                yq = ct["yq"][a]
                dqkv_ref[:, qo[a]] = ct["rq"][a] * (dyq - yq * _rowsum(yq * dyq))
                dqkv_ref[:, ko[a]] = ct["rk"][a] * (dk[a] - kn[a] * _rowsum(kn[a] * dk[a]))
                dqkv_ref[:, vo[a]] = dvb[a] * beta[a]
                dbg = dbg + jnp.where(lane == hs[a], dbeta[a], 0.0) + jnp.where(lane == HEADS + hs[a], dg[a], 0.0)
            return dbg

        dbg_ref[...] = lax.fori_loop(0, HEADS // HEAD_GROUP, group, jnp.zeros((c, 128), F32))

    wide = _row(c, DN_WIDTH)
    sc3 = pl.BlockSpec((1, HEADS, HEAD_DIM), lambda i: (i, 0, 0))
    return _pc(body, "dn_local_bwd", (nt,),
               [_row(c, QKV_WIDTH), _row(c, 128), pl.BlockSpec((2 * HEADS, c), lambda i: (0, i))] + [wide] * 8 + [sc3],
               [_row(c, QKV_WIDTH), _row(c, 128)], [SDS((t, QKV_WIDTH), F32), SDS((t, 128), F32)],
               sem=("parallel",))(qkv_act, bg, bgt, u, w, ymat, dvn, dw, dqg, dkg, dattn, degl)


MIX_ROWS = 64


def _mix_fwd(o, proj, ypre, pool_scale, wo_row, tm):
    t = o.shape[0]

    def body(o_ref, z_ref, ga_ref, gb_ref, yp_ref, ps_ref, wo_ref, mixed_ref):
        for r in range(0, tm, MIX_ROWS):
            rows = pl.ds(r, MIX_ROWS)
            for h in range(HEADS):
                sl = slice(h * HEAD_DIM, (h + 1) * HEAD_DIM)
                oh = o_ref[rows, sl]
                on = oh * lax.rsqrt(jnp.mean(oh * oh, axis=1, keepdims=True) + RMS_EPS)
                zh = z_ref[rows, sl]
                yb = on * wo_ref[:, sl] * (zh * _sigmoid(zh))
                ya = yp_ref[rows, sl] * ps_ref[:, sl]
                mixed_ref[rows, sl] = _mx(_sigmoid(ga_ref[rows, sl]) * ya + _sigmoid(gb_ref[rows, sl]) * yb)

    def col(blk):
        return pl.BlockSpec((tm, D_MODEL), lambda i: (i, blk))

    return _pc(body, "mix_fwd", (t // tm,),
               [_row(tm, D_MODEL), col(K_Z // D_MODEL), col(K_GA // D_MODEL), col(K_GB // D_MODEL), _row(tm, D_MODEL),
                _const((1, D_MODEL)), _const((1, D_MODEL))],
               _row(tm, D_MODEL), SDS((t, D_MODEL), MXU_DTYPE), sem=("parallel",))(
                   o, proj, proj, proj, ypre, pool_scale, wo_row)


def _mix_bwd(da1_bf, w_out, o, proj, ypre, pool_scale, wo_row, tm):
    t = o.shape[0]

    def body(da_ref, wout_ref, o_ref, z_ref, ga_ref, gb_ref, yp_ref, ps_ref, wo_ref,
             do_ref, dz_ref, dga_ref, dgb_ref, dyp_ref, acc_ref, dm_ref):
        i = pl.program_id(0)

        @pl.when(i == 0)
        def _():
            acc_ref[...] = jnp.zeros_like(acc_ref)

        dm_ref[...] = _dot_nt(da_ref[...], wout_ref[...])
        dwo = jnp.zeros((1, HEAD_DIM), F32)
        for h in range(HEADS):
            sl = slice(h * HEAD_DIM, (h + 1) * HEAD_DIM)
            woh = wo_ref[:, sl]
            psh = ps_ref[:, sl]
            dps = jnp.zeros((1, HEAD_DIM), F32)
            for r in range(0, tm, MIX_ROWS):
                rows = pl.ds(r, MIX_ROWS)
                oh = o_ref[rows, sl]
                rs = lax.rsqrt(jnp.mean(oh * oh, axis=1, keepdims=True) + RMS_EPS)
                on = oh * rs
                zh = z_ref[rows, sl]
                sz = _sigmoid(zh)
                silu = zh * sz
                t1 = on * woh
                yb = t1 * silu
                sa = _sigmoid(ga_ref[rows, sl])
                sb = _sigmoid(gb_ref[rows, sl])
                yp = yp_ref[rows, sl]
                dm = dm_ref[rows, sl]
                dga_ref[rows, sl] = _mx(dm * (yp * psh) * sa * (1.0 - sa))
                dgb_ref[rows, sl] = _mx(dm * yb * sb * (1.0 - sb))
                dya = dm * sa
                dyb = dm * sb
                dyp_ref[rows, sl] = _mx(dya * psh)
                dps = dps + jnp.sum(dya * yp, axis=0, keepdims=True)
                dz_ref[rows, sl] = _mx(dyb * t1 * (sz * (1.0 + zh * (1.0 - sz))))
                dt1 = dyb * silu
                dwo = dwo + jnp.sum(dt1 * on, axis=0, keepdims=True)
                don = dt1 * woh
                do_ref[rows, sl] = _mx(rs * (don - on * jnp.mean(don * on, axis=1, keepdims=True)))
            acc_ref[0:1, sl] += dps
        acc_ref[1:2, 0:HEAD_DIM] += dwo

    def col(blk):
        return pl.BlockSpec((tm, D_MODEL), lambda i: (i, blk))

    r = _row(tm, D_MODEL)
    return _pc(body, "mix_bwd", (t // tm,),
               [r, _const((D_MODEL, D_MODEL)), r, col(K_Z // D_MODEL), col(K_GA // D_MODEL), col(K_GB // D_MODEL), r,
                _const((1, D_MODEL)), _const((1, D_MODEL))],
               [r, r, r, r, r, _const((8, D_MODEL))],
               [SDS((t, D_MODEL), MXU_DTYPE)] * 5 + [SDS((8, D_MODEL), F32)],
               scratch=[pltpu.VMEM((tm, D_MODEL), F32)],
               sem=("arbitrary",))(da1_bf, w_out, o, proj, proj, proj, ypre, pool_scale, wo_row)


def _oproj_ln1(mixed, w_out, h0, g1, b1, tm):
    t = mixed.shape[0]

    def body(m_ref, w_ref, h0_ref, g_ref, b_ref, a1_ref, h1_ref, h1b_ref):
        a1 = ALPHA * h0_ref[...] + _dot(m_ref[...], w_ref[...])
        a1_ref[...] = a1
        xhat, _ = _ln_stats(a1)
        h1 = xhat * g_ref[...] + b_ref[...]
        h1_ref[...] = h1
        h1b_ref[...] = _mx(h1)

    r = _row(tm, D_MODEL)
    v = _const((1, D_MODEL))
    return _pc(body, "oproj_ln1", (t // tm,), [r, _const((D_MODEL, D_MODEL)), r, v, v], [r, r, r],
               [SDS((t, D_MODEL), F32), SDS((t, D_MODEL), F32), SDS((t, D_MODEL), MXU_DTYPE)],
               sem=("parallel",))(mixed, w_out, h0, g1, b1)


def _mlp_up(h1_bf, w_up, tm):
    t = h1_bf.shape[0]
    tn = w_up.shape[2]

    def body(h_ref, w_ref, up_ref, act_ref):
        up = _dot(h_ref[...], w_ref[...])
        up_ref[...] = up
        r = jnp.maximum(up, 0.0)
        act_ref[...] = _mx(r * r)

    o = pl.BlockSpec((tm, tn), lambda i, j: (i, j))
    return _pc(body, "mlp_up", (t // tm, D_FF // tn),
               [pl.BlockSpec((tm, D_MODEL), lambda i, j: (i, 0)),
                pl.BlockSpec((None, D_MODEL, tn), lambda i, j: (j, 0, 0))],
               [o, o], [SDS((t, D_FF), F32), SDS((t, D_FF), MXU_DTYPE)], sem=("parallel", "parallel"))(h1_bf, w_up)


def _tail(act, w_down, h1, w_gate, p_bf, w_proj, tgt, g2, b2, tm):
    t = act.shape[0]

    def body(act_ref, wd_ref, h1_ref, wg_ref, p_ref, wp_ref, tgt_ref, g_ref, b_ref,
             dr_ref, drb_ref, dgp_ref, dpp_ref, rb_ref, acc_ref):
        i = pl.program_id(0)

        @pl.when(i == 0)
        def _():
            acc_ref[...] = jnp.zeros_like(acc_ref)

        r = ALPHA * h1_ref[...] + _dot(act_ref[...], wd_ref[...])
        rb = _mx(r)
        rb_ref[...] = rb
        gate = _sigmoid(_dot(rb, wg_ref[...]))
        pp = _dot(p_ref[...], wp_ref[...])
        xhat, rstd = _ln_stats(r + gate * pp)
        g = g_ref[...]
        diff = xhat * g + b_ref[...] - tgt_ref[...]
        dh2 = diff * (1.0 / D_MODEL)
        rowloss = jnp.sum(diff * diff, axis=1, keepdims=True) * (0.5 / D_MODEL)
        acc_ref[0:1, :] += jnp.sum(dh2 * xhat, axis=0, keepdims=True)
        acc_ref[1:2, :] += jnp.sum(dh2, axis=0, keepdims=True)
        acc_ref[2:3, :] += jnp.broadcast_to(jnp.sum(rowloss, axis=0, keepdims=True), (1, D_MODEL))
        da2 = _ln_bwd(dh2, xhat, rstd, g)
        dpp_ref[...] = _mx(da2 * gate)
        dgp = _mx(da2 * pp * gate * (1.0 - gate))
        dgp_ref[...] = dgp
        dr = da2 + _dot_nt(dgp, wg_ref[...])
        dr_ref[...] = dr
        drb_ref[...] = _mx(dr)

    r = _row(tm, D_MODEL)
    v = _const((1, D_MODEL))
    return _pc(body, "tail", (t // tm,),
               [_row(tm, D_FF), _const((D_FF, D_MODEL)), r, _const((D_MODEL, D_MODEL)), _row(tm, PLE_DIM),
                _const((PLE_DIM, D_MODEL)), r, v, v],
               [r, r, r, r, r, _const((8, D_MODEL))],
               [SDS((t, D_MODEL), F32)] + [SDS((t, D_MODEL), MXU_DTYPE)] * 4 + [SDS((8, D_MODEL), F32)],
               sem=("arbitrary",))(act, w_down, h1, w_gate, p_bf, w_proj, tgt, g2, b2)


def _mlp_bwd1(dr_bf, w_down, up, tm, tn):
    t = up.shape[0]

    def body(dr_ref, w_ref, up_ref, dup_ref):
        dact = _dot_nt(dr_ref[...], w_ref[...])
        dup_ref[...] = _mx(dact * (2.0 * jnp.maximum(up_ref[...], 0.0)))

    o = pl.BlockSpec((tm, tn), lambda i, j: (i, j))
    return _pc(body, "mlp_bwd1", (t // tm, D_FF // tn),
               [pl.BlockSpec((tm, D_MODEL), lambda i, j: (i, 0)), pl.BlockSpec((tn, D_MODEL), lambda i, j: (j, 0)), o],
               o, SDS((t, D_FF), MXU_DTYPE), sem=("parallel", "parallel"))(dr_bf, w_down, up)


def _mlp_bwd2(dup, w_up, dr, a1, g1, tm):
    t = dr.shape[0]

    nk, tk = w_up.shape[0], w_up.shape[2]

    def body(dup_ref, w_ref, dr_ref, a1_ref, g_ref, da1_ref, da1b_ref, acc_ref):
        i = pl.program_id(0)

        @pl.when(i == 0)
        def _():
            acc_ref[...] = jnp.zeros_like(acc_ref)

        dh1 = ALPHA * dr_ref[...]
        for kk in range(nk):
            dh1 = dh1 + _dot_nt(dup_ref[:, kk * tk:(kk + 1) * tk], w_ref[kk])
        xhat, rstd = _ln_stats(a1_ref[...])
        acc_ref[0:1, :] += jnp.sum(dh1 * xhat, axis=0, keepdims=True)
        acc_ref[1:2, :] += jnp.sum(dh1, axis=0, keepdims=True)
        da1 = _ln_bwd(dh1, xhat, rstd, g_ref[...])
        da1_ref[...] = da1
        da1b_ref[...] = _mx(da1)

    r = _row(tm, D_MODEL)
    return _pc(body, "mlp_bwd2", (t // tm,),
               [_row(tm, D_FF), _const((nk, D_MODEL, tk)), r, r, _const((1, D_MODEL))],
               [r, r, _const((8, D_MODEL))],
               [SDS((t, D_MODEL), F32), SDS((t, D_MODEL), MXU_DTYPE), SDS((8, D_MODEL), F32)],
               sem=("arbitrary",))(dup, w_up, dr, a1, g1)


def _ln_in_bwd(dproj, w_cat, da1, x, g, tm):
    t = x.shape[0]

    def body(dp_ref, w_ref, da1_ref, x_ref, g_ref, dx_ref, acc_ref):
        i = pl.program_id(0)

        @pl.when(i == 0)
        def _():
            acc_ref[...] = jnp.zeros_like(acc_ref)

        dh0 = _dot_nt(dp_ref[...], w_ref[...]) + ALPHA * da1_ref[...]
        xhat, rstd = _ln_stats(x_ref[...])
        acc_ref[0:1, :] += jnp.sum(dh0 * xhat, axis=0, keepdims=True)
        acc_ref[1:2, :] += jnp.sum(dh0, axis=0, keepdims=True)
        dx_ref[...] = _ln_bwd(dh0, xhat, rstd, g_ref[...])

    r = _row(tm, D_MODEL)
    return _pc(body, "ln_in_bwd", (t // tm,),
               [_row(tm, CAT_WIDTH), _const((D_MODEL, CAT_WIDTH)), r, r, _const((1, D_MODEL))],
               [r, _const((8, D_MODEL))], [SDS((t, D_MODEL), F32), SDS((8, D_MODEL), F32)],
               sem=("arbitrary",))(dproj, w_cat, da1, x, g)


def _local_step(x, p, tgt, wts):
    t = x.shape[0]
    tm = min(512, t)
    tms = min(256, t)
    row = lambda a: a.reshape(1, -1)
    w_cat = wts["w_cat"]
    pool_scale = row(wts["pool_scale"])
    wo_row = jnp.tile(row(wts["o_norm_w"]), (1, HEADS))
    pad8 = jnp.zeros((1, HEADS), F32)
    al_row = jnp.concatenate([pad8, row(wts["a_log"]), jnp.zeros((1, 128 - 2 * HEADS), F32)], axis=1)
    dtb_row = jnp.concatenate([pad8, row(wts["dt_bias"]), jnp.zeros((1, 128 - 2 * HEADS), F32)], axis=1)
    g_in, b_in = row(wts["ln_in_g"]), row(wts["ln_in_b"])
    g1, b1 = row(wts["ln1_g"]), row(wts["ln1_b"])
    g2, b2 = row(wts["ln2_g"]), row(wts["ln2_b"])

    h0, h0_bf = _ln_in(x, g_in, b_in, tm)
    proj = _matmul(h0_bf, w_cat, "nn", "proj", F32, tm=512, tn=1152, tk=1024)
    ypre, d_bf = _pool_fwd(proj, wts["pool_w"], tm)
    qkv_act = _conv_fwd(proj, wts["conv_w"], tm)
    bg = _ba_fwd(proj, al_row, dtb_row, tm)
    bgt = bg[:, :2 * HEADS].T
    u, w, qg, kg, attn, ymat, egl = _dn_local_fwd(qkv_act, bg, bgt)
    o, vn, states = _dn_scan_fwd(u, w, qg, kg, attn, egl)
    mixed = _mix_fwd(o, proj, ypre, pool_scale, wo_row, tm)
    a1, h1, h1_bf = _oproj_ln1(mixed, wts["w_out"], h0, g1, b1, tm)
    up, act = _mlp_up(h1_bf, wts["w_up"], tm)
    p_bf = _mx(p)
    dr, dr_bf, dgp, dpp, r_bf, acc_tail = _tail(act, wts["w_down"], h1, wts["ple_gate_w"], p_bf, wts["ple_proj_w"],
                                                tgt, g2, b2, tms)
    grads = {}
    grads["ple_proj_w"] = _matmul(p_bf, dpp, "tn", "dw_ple_proj", WIRE_DTYPE, tm=256, tn=1024, tk=DW_TK)
    grads["ple_gate_w"] = _matmul(r_bf, dgp, "tn", "dw_ple_gate", WIRE_DTYPE, tm=512, tn=1024, tk=DW_TK)
    grads["w_down"] = _matmul(act, dr_bf, "tn", "dw_down", WIRE_DTYPE, tm=512, tn=1024, tk=DW_TK)
    dup = _mlp_bwd1(dr_bf, wts["w_down"], up, tm, 1024)
    grads["w_up"] = _matmul(h1_bf, dup, "tn", "dw_up", WIRE_DTYPE, tm=512, tn=1024, tk=DW_TK, stack_out=True)
    da1, da1_bf, acc_ln1 = _mlp_bwd2(dup, wts["w_up"], dr, a1, g1, tms)
    grads["w_out"] = _matmul(mixed, da1_bf, "tn", "dw_out", WIRE_DTYPE, tm=512, tn=1024, tk=DW_TK)
    do, dz, dga, dgb, dyp, acc_mix = _mix_bwd(da1_bf, wts["w_out"], o, proj, ypre, pool_scale, wo_row, tms)
    du_pool, grads["pool_w"] = _pool_bwd(dyp, d_bf, wts["pool_w"], tm)
    dvn, dkg, dqg, dattn, dw, degl = _dn_scan_bwd(do, qg, kg, w, attn, vn, states, egl)
    dqkv_act, dbg = _dn_local_bwd(qkv_act, bg, bgt, u, w, ymat, dvn, dw, dqg, dkg, dattn, degl)
    dqkv, acc_conv = _conv_bwd(dqkv_act, proj, wts["conv_w"], tm)
    dba, acc_ba = _ba_bwd(dbg, bg, proj, al_row, dtb_row, tm)
    dproj = jnp.concatenate([dqkv, dz, dga, dgb, du_pool, dba,
                             jnp.zeros((t, CAT_WIDTH - K_BA - 128), MXU_DTYPE)], axis=1)
    dw_cat = _matmul(h0_bf, dproj, "tn", "dw_in", F32, tm=512, tn=1152, tk=DW_TK)
    grad_x, acc_in = _ln_in_bwd(dproj, w_cat, da1, x, g_in, tms)

    grads["w_in"] = jnp.concatenate(
        [dw_cat[:, K_U:K_U + 512], dw_cat[:, K_QKV:K_QKV + 3072], dw_cat[:, K_Z:K_Z + 1024],
         dw_cat[:, K_BA:K_BA + 16], dw_cat[:, K_GA:K_GA + 1024], dw_cat[:, K_GB:K_GB + 1024]], axis=1)
    grads["conv_w"] = acc_conv[0:CONV_K]
    grads["ln_in_g"], grads["ln_in_b"] = acc_in[0], acc_in[1]
    grads["ln1_g"], grads["ln1_b"] = acc_ln1[0], acc_ln1[1]
    grads["ln2_g"], grads["ln2_b"] = acc_tail[0], acc_tail[1]
    grads["pool_scale"] = acc_mix[0]
    grads["o_norm_w"] = acc_mix[1, 0:HEAD_DIM]
    grads["a_log"] = acc_ba[0, HEADS:2 * HEADS]
    grads["dt_bias"] = acc_ba[1, HEADS:2 * HEADS]
    loss = acc_tail[2, 0]
    return grad_x, grads, loss


MESH = pl.DeviceIdType.MESH
ANY = pl.BlockSpec(memory_space=pl.ANY)


def _chip_of(k, x, y):
    chip = (2 * x + y + k) % N_CHIPS
    return chip // 2, chip % 2


def _place():
    x, y, c = lax.axis_index("x"), lax.axis_index("y"), lax.axis_index("c")
    return x, y, c, 2 * x + y


def _half(rows, c):
    return pl.ds(pl.multiple_of(c * (rows // 2), 16), rows // 2)


def _remote(src, dst, send_sem, recv_sem, device_id):
    return pltpu.make_async_remote_copy(src_ref=src, dst_ref=dst, send_sem=send_sem, recv_sem=recv_sem,
                                        device_id=device_id, device_id_type=MESH)


def _tile_rows(rows):
    for tr in (256, 128, 64, 32, 16):
        if rows % tr == 0:
            return tr
    raise ValueError(rows)


def _gather_weights(shards, conv_shard):
    n = len(shards)

    def body(*refs):
        ins, conv_in = refs[0:n], refs[n]
        outs, conv_out = refs[n + 1:2 * n + 1], refs[2 * n + 1]
        send, recv, fsend, frecv, csend, crecv, lsend, lrecv = refs[2 * n + 2:]
        x, y, c, me = _place()
        here, sib = (x, y, c), (x, y, 1 - c)
        own = [_remote(ins[a], outs[a].at[me], lsend.at[a], lrecv.at[a], sib) for a in range(n)]
        own.append(_remote(conv_in, conv_out.at[me], lsend.at[n], lrecv.at[n], sib))
        for cp in own:
            cp.start()
        sends = []
        for k in range(1, N_CHIPS):
            tx, ty = _chip_of(k, x, y)
            for a in range(n):
                mine = _half(shards[a].shape[0], c)
                sends.append(_remote(ins[a].at[mine], outs[a].at[me, mine], send.at[a * N_CHIPS + k],
                                     recv.at[a * N_CHIPS + k], (tx, ty, c)))
            sends.append(_remote(conv_in, conv_out.at[me], csend.at[k], crecv.at[k], (tx, ty, c)))
        for cp in sends:
            cp.start()
        forwards = []
        for k in range(1, N_CHIPS):
            src = (me + N_CHIPS - k) % N_CHIPS
            for a in range(n):
                landed = outs[a].at[src, _half(shards[a].shape[0], c)]
                _remote(landed, landed, send.at[a * N_CHIPS + k], recv.at[a * N_CHIPS + k], here).wait_recv()
                fwd = _remote(landed, landed, fsend.at[a * N_CHIPS + k], frecv.at[a * N_CHIPS + k], sib)
                fwd.start()
                forwards.append(fwd)
            _remote(conv_in, conv_out.at[src], csend.at[k], crecv.at[k], here).wait_recv()
        for k in range(1, N_CHIPS):
            src = (me + N_CHIPS - k) % N_CHIPS
            for a in range(n):
                passed = outs[a].at[src, _half(shards[a].shape[0], 1 - c)]
                _remote(passed, passed, fsend.at[a * N_CHIPS + k], frecv.at[a * N_CHIPS + k], here).wait_recv()
        for cp in sends + forwards:
            cp.wait_send()
        for cp in own:
            cp.wait()

    sems = pltpu.SemaphoreType.DMA((n * N_CHIPS,))
    return pl.pallas_call(
        body, name="gather_weights",
        out_shape=[SDS((N_CHIPS,) + s.shape, s.dtype) for s in shards]
        + [SDS((N_CHIPS,) + conv_shard.shape, conv_shard.dtype)],
        in_specs=[ANY] * (n + 1), out_specs=[ANY] * (n + 1),
        scratch_shapes=[sems, sems, sems, sems, pltpu.SemaphoreType.DMA((N_CHIPS,)),
                        pltpu.SemaphoreType.DMA((N_CHIPS,)), pltpu.SemaphoreType.DMA((n + 1,)),
                        pltpu.SemaphoreType.DMA((n + 1,))],
    )(*shards, conv_shard)


def _swap_halves(gs):
    n = len(gs)

    def body(*refs):
        ins, theirs = refs[0:n], refs[n:2 * n]
        send, recv = refs[2 * n:]
        x, y, c, _ = _place()
        copies = [_remote(ins[a].at[:, _half(gs[a].shape[1], 1 - c)], theirs[a], send.at[a], recv.at[a],
                          (x, y, 1 - c)) for a in range(n)]
        for cp in copies:
            cp.start()
        for cp in copies:
            cp.wait()

    return pl.pallas_call(
        body, name="swap_halves", out_shape=[SDS((N_CHIPS, g.shape[1] // 2, g.shape[2]), g.dtype) for g in gs],
        in_specs=[ANY] * n, out_specs=[ANY] * n, scratch_shapes=[pltpu.SemaphoreType.DMA((n,))] * 2,
    )(*gs)


def _scatter_halves(qs):
    n = len(qs)

    def body(*refs):
        ins, outs = refs[0:n], refs[n:2 * n]
        send, recv = refs[2 * n:]
        x, y, c, me = _place()
        copies = []
        for k in range(1, N_CHIPS):
            tx, ty = _chip_of(k, x, y)
            for a in range(n):
                copies.append(_remote(ins[a].at[2 * tx + ty], outs[a].at[k - 1], send.at[a * N_CHIPS + k],
                                      recv.at[a * N_CHIPS + k], (tx, ty, c)))
        for cp in copies:
            cp.start()
        for cp in copies:
            cp.wait()

    sems = pltpu.SemaphoreType.DMA((n * N_CHIPS,))
    return pl.pallas_call(
        body, name="scatter_halves", out_shape=[SDS((N_CHIPS - 1,) + q.shape[1:], q.dtype) for q in qs],
        in_specs=[ANY] * n, out_specs=[ANY] * n, scratch_shapes=[sems, sems],
    )(*qs)


def _send_to_sibling(hs):
    n = len(hs)

    def body(*refs):
        ins, outs = refs[0:n], refs[n:2 * n]
        send, recv = refs[2 * n:]
        x, y, c, _ = _place()
        copies = [_remote(ins[a], outs[a], send.at[a], recv.at[a], (x, y, 1 - c)) for a in range(n)]
        for cp in copies:
            cp.start()
        for cp in copies:
            cp.wait()

    return pl.pallas_call(
        body, name="send_to_sibling", out_shape=[SDS(h.shape, h.dtype) for h in hs],
        in_specs=[ANY] * n, out_specs=[ANY] * n, scratch_shapes=[pltpu.SemaphoreType.DMA((n,))] * 2,
    )(*hs)


def _add_pair(g, theirs, name):
    _, rows, cols = g.shape
    half = rows // 2
    tr = _tile_rows(half)

    def body(g_ref, t_ref, o_ref):
        own = g_ref[lax.axis_index("c")]
        o_ref[...] = (own.astype(F32) + t_ref[...].astype(F32)).astype(o_ref.dtype)

    blk = pl.BlockSpec((None, tr, cols), lambda j, i: (j, i, 0))
    return _pc(body, "add_" + name, (N_CHIPS, half // tr),
               [pl.BlockSpec((None, 2, tr, cols), lambda j, i: (j, 0, i, 0)), blk], blk,
               SDS((N_CHIPS, half, cols), g.dtype), sem=("parallel", "parallel"))(
                   g.reshape(N_CHIPS, 2, half, cols), theirs)


def _sum_slabs(pair, landed, name):
    _, rows, cols = pair.shape
    tr = _tile_rows(rows)

    def body(p_ref, r_ref, o_ref):
        acc = p_ref[2 * lax.axis_index("x") + lax.axis_index("y")].astype(F32)
        for k in range(N_CHIPS - 1):
            acc = acc + r_ref[k].astype(F32)
        o_ref[...] = acc

    return _pc(body, "sum_" + name, (rows // tr,),
               [pl.BlockSpec((N_CHIPS, tr, cols), lambda i: (0, i, 0)),
                pl.BlockSpec((N_CHIPS - 1, tr, cols), lambda i: (0, i, 0))],
               _row(tr, cols), SDS((rows, cols), F32), sem=("parallel",))(pair, landed)


def _adamw_math(w, g, m, v):
    m = ADAM_B1 * m + (1.0 - ADAM_B1) * g
    v = ADAM_B2 * v + (1.0 - ADAM_B2) * (g * g)
    m_hat = m / (1.0 - ADAM_B1 ** ADAM_STEP)
    v_hat = v / (1.0 - ADAM_B2 ** ADAM_STEP)
    delta = -ADAM_LR * (m_hat / (jnp.sqrt(v_hat) + ADAM_EPS) + ADAM_WD * w)
    return delta, m, v


def _adamw_2d(w, g_own, g_sib, m, v, name):
    rows, cols = w.shape
    tr = _tile_rows(rows // 2)
    nh = rows // 2 // tr

    def body(w_ref, go_ref, gs_ref, m_ref, v_ref, g_out, d_out, m_out, v_out):
        mine = (pl.program_id(0) // nh) == lax.axis_index("c")
        g = jnp.where(mine, go_ref[...], gs_ref[...])
        delta, mn, vn = _adamw_math(w_ref[...], g, m_ref[...], v_ref[...])
        g_out[...] = g
        d_out[...] = delta
        m_out[...] = mn
        v_out[...] = vn

    r = _row(tr, cols)
    h = pl.BlockSpec((tr, cols), lambda i: (i % nh, 0))
    return _pc(body, "adamw_" + name, (rows // tr,), [r, h, h, r, r], [r] * 4, [SDS((rows, cols), F32)] * 4,
               sem=("parallel",))(w, g_own, g_sib, m, v)


def _small_allreduce_adamw(mine, w, m, v):
    shape = mine.shape

    def body(mine_ref, w_ref, m_ref, v_ref, g_out, d_out, m_out, v_out, buf_ref, send_sems, recv_sems):
        x, y, c = lax.axis_index("x"), lax.axis_index("y"), lax.axis_index("c")
        me = 4 * x + 2 * y + c
        buf_ref[me] = mine_ref[...]
        copies = []
        for k in range(1, N_DEV):
            tgt = (me + k) % N_DEV
            copies.append(pltpu.make_async_remote_copy(
                src_ref=mine_ref, dst_ref=buf_ref.at[me], send_sem=send_sems.at[k], recv_sem=recv_sems.at[k],
                device_id=(tgt // 4, (tgt // 2) % 2, tgt % 2), device_id_type=MESH))
        for cp in copies:
            cp.start()
        for k in range(1, N_DEV):
            src = (me + N_DEV - k) % N_DEV
            pltpu.make_async_remote_copy(
                src_ref=mine_ref, dst_ref=buf_ref.at[src], send_sem=send_sems.at[k], recv_sem=recv_sems.at[k],
                device_id=(x, y, c), device_id_type=MESH).wait_recv()
        for cp in copies:
            cp.wait_send()
        g = buf_ref[0]
        for j in range(1, N_DEV):
            g = g + buf_ref[j]
        delta, mn, vn = _adamw_math(w_ref[...], g, m_ref[...], v_ref[...])
        g_out[...] = g
        d_out[...] = delta
        m_out[...] = mn
        v_out[...] = vn

    vm = pl.BlockSpec(memory_space=pltpu.VMEM)
    return pl.pallas_call(
        body, name="small_allreduce_adamw", out_shape=[SDS(shape, F32)] * 4, in_specs=[vm] * 4, out_specs=[vm] * 4,
        scratch_shapes=[pltpu.VMEM((N_DEV,) + shape, F32), pltpu.SemaphoreType.DMA((N_DEV,)),
                        pltpu.SemaphoreType.DMA((N_DEV,))],
    )(mine, w, m, v)


def _as2d(a):
    return a.reshape(-1, a.shape[-1])


def _full_weights(stacks):
    wi = stacks["w_in"].transpose(1, 0, 2).reshape(D_MODEL, IN_WIDTH)
    return {
        "w_cat": jnp.concatenate(
            [wi[:, C_QKV:C_Z], wi[:, C_Z:C_BETA], wi[:, C_GA:C_GB], wi[:, C_GB:IN_WIDTH], wi[:, C_POOL:C_QKV],
             wi[:, C_BETA:C_GA], jnp.zeros((D_MODEL, CAT_WIDTH - K_BA - 2 * HEADS), wi.dtype)], axis=1),
        "pool_w": stacks["pool_w"].reshape(N_CHIPS, 4, POOL_GROUP, POOL_OUT_GROUP // N_CHIPS)
                                  .transpose(1, 2, 0, 3).reshape(4, POOL_GROUP, POOL_OUT_GROUP),
        "w_out": stacks["w_out"].reshape(D_MODEL, D_MODEL),
        "w_up": stacks["w_up"],
        "w_down": stacks["w_down"].reshape(D_FF, D_MODEL),
        "ple_gate_w": stacks["ple_gate_w"].reshape(D_MODEL, D_MODEL),
        "ple_proj_w": stacks["ple_proj_w"].transpose(1, 0, 2).reshape(PLE_DIM, D_MODEL),
    }


def _grads_by_chip(grads):
    wire = lambda a: a.astype(WIRE_DTYPE)
    return {
        "w_in": wire(grads["w_in"].reshape(D_MODEL, N_CHIPS, IN_WIDTH // N_CHIPS).transpose(1, 0, 2)),
        "pool_w": wire(grads["pool_w"].reshape(4, POOL_GROUP, N_CHIPS, POOL_OUT_GROUP // N_CHIPS)
                       .transpose(2, 0, 1, 3).reshape(N_CHIPS, 4 * POOL_GROUP, POOL_OUT_GROUP // N_CHIPS)),
        "w_out": wire(grads["w_out"]).reshape(N_CHIPS, D_MODEL // N_CHIPS, D_MODEL),
        "w_up": wire(grads["w_up"]),
        "w_down": wire(grads["w_down"]).reshape(N_CHIPS, D_FF // N_CHIPS, D_MODEL),
        "ple_gate_w": wire(grads["ple_gate_w"]).reshape(N_CHIPS, D_MODEL // N_CHIPS, D_MODEL),
        "ple_proj_w": wire(grads["ple_proj_w"]).reshape(PLE_DIM, N_CHIPS, D_MODEL // N_CHIPS).transpose(1, 0, 2),
    }


def _pad_row(a):
    a = a.reshape(1, -1).astype(F32)
    return jnp.pad(a, ((0, 0), (0, D_MODEL - a.shape[1])))


def kernel(x, p, ln_in_g, ln_in_b, w_in, pool_w, pool_scale, conv_w, a_log, dt_bias, o_norm_w, w_out, ln1_g, ln1_b, w_up, w_down, ple_gate_w, ple_proj_w, ln2_g, ln2_b, loss_target, m_ln_in_g, m_ln_in_b, m_w_in, m_pool_w, m_pool_scale, m_conv_w, m_a_log, m_dt_bias, m_o_norm_w, m_w_out, m_ln1_g, m_ln1_b, m_w_up, m_w_down, m_ple_gate_w, m_ple_proj_w, m_ln2_g, m_ln2_b, v_ln_in_g, v_ln_in_b, v_w_in, v_pool_w, v_pool_scale, v_conv_w, v_a_log, v_dt_bias, v_o_norm_w, v_w_out, v_ln1_g, v_ln1_b, v_w_up, v_w_down, v_ple_gate_w, v_ple_proj_w, v_ln2_g, v_ln2_b):
    given = dict(locals())
    chip = 2 * lax.axis_index("x") + lax.axis_index("y")

    conv_pad = jnp.pad(conv_w[0], ((0, 8 - CONV_K), (0, 0)))
    gathered = _gather_weights([_as2d(given[n]).astype(WIRE_DTYPE) for n in BIG], conv_pad)
    wts = _full_weights({n: g.astype(MXU_DTYPE) for n, g in zip(BIG, gathered[0:len(BIG)])})
    wts.update({
        "conv_w": jnp.concatenate([gathered[len(BIG)][j, 0:CONV_K] for j in range(N_CHIPS)], axis=1),
        "ln_in_g": ln_in_g, "ln_in_b": ln_in_b, "pool_scale": pool_scale[0], "a_log": a_log[0],
        "dt_bias": dt_bias[0], "o_norm_w": o_norm_w[0], "ln1_g": ln1_g[0], "ln1_b": ln1_b[0],
        "ln2_g": ln2_g[0], "ln2_b": ln2_b[0],
    })

    grad_x, grads, loss = _local_step(x[0], p[0, 0], loss_target[0], wts)

    by_chip = _grads_by_chip(grads)
    theirs = _swap_halves([by_chip[n] for n in BIG])
    pair = [_add_pair(by_chip[n], t, n) for t, n in zip(theirs, BIG)]
    landed = _scatter_halves(pair)
    reduced = [_sum_slabs(q, r, n) for q, r, n in zip(pair, landed, BIG)]
    from_sibling = _send_to_sibling(reduced)
    big_out = {}
    for n, g_own, g_sib in zip(BIG, reduced, from_sibling):
        res = _adamw_2d(_as2d(given[n]), g_own, g_sib, _as2d(given["m_" + n]), _as2d(given["v_" + n]), n)
        big_out[n] = [r.reshape(given[n].shape) for r in res]

    conv_cols = QKV_WIDTH // N_CHIPS

    def small_rows(get, conv):
        if conv.shape[1] != QKV_WIDTH:
            conv = lax.dynamic_update_slice(jnp.zeros((CONV_K, QKV_WIDTH), F32), conv, (0, chip * conv_cols))
        return [_pad_row(get(n)) for n in SMALL_NAMES], conv.reshape(SMALL_CONV_ROWS, D_MODEL)

    fill = jnp.zeros((SMALL_CONV_AT - len(SMALL_NAMES), D_MODEL), F32)
    rows, conv = small_rows(lambda n: grads[n], grads["conv_w"])
    mine_small = jnp.concatenate(rows + [jnp.full((1, D_MODEL), loss, F32), fill[1:], conv], axis=0)
    packed_small = []
    for prefix in ("", "m_", "v_"):
        rows, conv = small_rows(lambda n: given[prefix + n], given[prefix + "conv_w"][0])
        packed_small.append(jnp.concatenate(rows + [fill, conv], axis=0))
    small_out = _small_allreduce_adamw(mine_small, *packed_small)

    def small_get(k, n):
        if n == "conv_w":
            full = small_out[k][SMALL_CONV_AT:SMALL_CONV_AT + SMALL_CONV_ROWS].reshape(CONV_K, QKV_WIDTH)
            return lax.dynamic_slice(full, (0, chip * conv_cols), (CONV_K, conv_cols)).reshape(given[n].shape)
        i = SMALL_NAMES.index(n)
        return small_out[k][i, 0:given[n].size].reshape(given[n].shape)

    order = ["ln_in_g", "ln_in_b", "w_in", "pool_w", "pool_scale", "conv_w", "a_log", "dt_bias", "o_norm_w", "w_out",
             "ln1_g", "ln1_b", "w_up", "w_down", "ple_gate_w", "ple_proj_w", "ln2_g", "ln2_b"]
    outs = [small_out[0][len(SMALL_NAMES), 0], grad_x[None]]
    for k in range(4):
        for n in order:
            outs.append(big_out[n][k] if n in big_out else small_get(k, n))
    return tuple(outs)
```

```python
import jax
import jax.numpy as jnp
from jax import lax
from jax.experimental import pallas as pl
from jax.experimental.pallas import tpu as pltpu

F32 = jnp.float32
MXU_DTYPE = jnp.bfloat16
WIRE_DTYPE = jnp.bfloat16
SDS = jax.ShapeDtypeStruct

D_MODEL = 1024
POOL_WINDOWS = (2, 4, 8, 16)
POOL_WIDTH = 512
POOL_GROUP = 128
POOL_OUT_GROUP = 256
HEADS = 8
HEAD_DIM = 128
DN_WIDTH = HEADS * HEAD_DIM
QKV_WIDTH = 3 * DN_WIDTH
CONV_K = 4
CHUNK = 128
DW_TK = 1024
DW_TM = 1024
HEAD_GROUP = 4
D_FF = 4096
PLE_DIM = 256
LN_EPS = 1e-5
RMS_EPS = 1e-6
L2_EPS = 1e-6
ALPHA = 2.0 ** 0.25
Q_SCALE = HEAD_DIM ** -0.5
IN_WIDTH = 6672
C_POOL, C_QKV, C_Z, C_BETA, C_A, C_GA, C_GB = 0, 512, 3584, 4608, 4616, 4624, 5648
K_QKV, K_Z, K_GA, K_GB, K_U, K_BA, CAT_WIDTH = 0, 3072, 4096, 5120, 6144, 6656, 6912

ADAM_LR, ADAM_B1, ADAM_B2, ADAM_EPS, ADAM_WD, ADAM_STEP = 0.001, 0.9, 0.999, 1e-08, 0.01, 10

N_CHIPS = 4
N_DEV = 8
VMEM_LIMIT = 56 * 1024 * 1024

BIG = ("w_in", "pool_w", "w_out", "w_up", "w_down", "ple_gate_w", "ple_proj_w")
SMALL_NAMES = ("ln_in_g", "ln_in_b", "pool_scale", "ln1_g", "ln1_b", "ln2_g", "ln2_b", "o_norm_w", "a_log", "dt_bias")
SMALL_CONV_AT = 12
SMALL_CONV_ROWS = CONV_K * QKV_WIDTH // D_MODEL


def _mx(a):
    return a.astype(MXU_DTYPE)


def _dot(a, b):
    return lax.dot_general(_mx(a), _mx(b), (((1,), (0,)), ((), ())), preferred_element_type=F32)


def _dot_nt(a, b):
    return lax.dot_general(_mx(a), _mx(b), (((1,), (1,)), ((), ())), preferred_element_type=F32)


def _dot_tn(a, b):
    return lax.dot_general(_mx(a), _mx(b), (((0,), (0,)), ((), ())), preferred_element_type=F32)


def _sigmoid(x):
    return 0.5 * jnp.tanh(0.5 * x) + 0.5


def _softplus(x):
    return jnp.maximum(x, 0.0) + jnp.log(1.0 + jnp.exp(-jnp.abs(x)))


def _pc(body, name, grid, in_specs, out_specs, out_shape, scratch=(), sem=None):
    return pl.pallas_call(
        body, out_shape=out_shape, grid=grid, in_specs=in_specs, out_specs=out_specs,
        scratch_shapes=scratch, name=name,
        compiler_params=pltpu.CompilerParams(dimension_semantics=sem, vmem_limit_bytes=VMEM_LIMIT))


def _row(tm, n):
    return pl.BlockSpec((tm, n), lambda i: (i, 0))


def _const(shape):
    nd = len(shape)
    return pl.BlockSpec(shape, lambda *_: (0,) * nd)


def _matmul(a, b, mode, name, out_dtype=F32, tm=512, tn=512, tk=512, stack_out=False):
    if mode == "nn":
        (m, k), n = a.shape, b.shape[1]
    elif mode == "nt":
        (m, k), n = a.shape, b.shape[0]
    else:
        (k, m), n = a.shape, b.shape[1]
    tm, tn, tk = min(tm, m), min(tn, n), min(tk, k)
    assert m % tm == 0 and n % tn == 0 and k % tk == 0, (name, m, n, k, tm, tn, tk)
    nk = k // tk
    if mode == "nn":
        a_spec = pl.BlockSpec((tm, tk), lambda i, j, kk: (i, kk))
        b_spec = pl.BlockSpec((tk, tn), lambda i, j, kk: (kk, j))
        dot = _dot
    elif mode == "nt":
        a_spec = pl.BlockSpec((tm, tk), lambda i, j, kk: (i, kk))
        b_spec = pl.BlockSpec((tn, tk), lambda i, j, kk: (j, kk))
        dot = _dot_nt
    else:
        a_spec = pl.BlockSpec((tk, tm), lambda i, j, kk: (kk, i))
        b_spec = pl.BlockSpec((tk, tn), lambda i, j, kk: (kk, j))
        dot = _dot_tn

    def body(a_ref, b_ref, o_ref, *acc):
        if nk == 1:
            o_ref[...] = dot(a_ref[...], b_ref[...]).astype(out_dtype)
            return
        acc_ref, kk = acc[0], pl.program_id(2)

        @pl.when(kk == 0)
        def _():
            acc_ref[...] = dot(a_ref[...], b_ref[...])

        @pl.when((kk > 0) & (kk < nk - 1))
        def _():
            acc_ref[...] += dot(a_ref[...], b_ref[...])

        @pl.when(kk == nk - 1)
        def _():
            o_ref[...] = (acc_ref[...] + dot(a_ref[...], b_ref[...])).astype(out_dtype)

    if stack_out:
        o_spec, o_shape = pl.BlockSpec((None, tm, tn), lambda i, j, kk: (j, i, 0)), SDS((n // tn, m, tn), out_dtype)
    else:
        o_spec, o_shape = pl.BlockSpec((tm, tn), lambda i, j, kk: (i, j)), SDS((m, n), out_dtype)
    return _pc(body, name, (m // tm, n // tn, nk), [a_spec, b_spec], o_spec, o_shape,
               scratch=[pltpu.VMEM((tm, tn), F32)] if nk > 1 else [],
               sem=("parallel", "parallel", "arbitrary"))(a, b)


PROJ_TN = 1152


def _proj(h0_bf, w_cat, tm):
    t = h0_bf.shape[0]

    def body(h_ref, w_ref, o_ref):
        h = h_ref[...]
        for c0 in range(0, CAT_WIDTH, PROJ_TN):
            o_ref[:, c0:c0 + PROJ_TN] = _dot(h, w_ref[:, c0:c0 + PROJ_TN])

    return _pc(body, "proj", (t // tm,), [_row(tm, D_MODEL), _const((D_MODEL, CAT_WIDTH))], _row(tm, CAT_WIDTH),
               SDS((t, CAT_WIDTH), F32), sem=("parallel",))(h0_bf, w_cat)


def _ln_stats(x):
    mu = jnp.mean(x, axis=-1, keepdims=True)
    xc = x - mu
    var = jnp.mean(xc * xc, axis=-1, keepdims=True)
    rstd = lax.rsqrt(var + LN_EPS)
    return xc * rstd, rstd


def _ln_bwd(dy, xhat, rstd, g):
    dxh = dy * g
    m1 = jnp.mean(dxh, axis=-1, keepdims=True)
    m2 = jnp.mean(dxh * xhat, axis=-1, keepdims=True)
    return rstd * (dxh - m1 - xhat * m2)


def _ln_in(x, g, b, tm):
    t, d = x.shape

    def body(x_ref, g_ref, b_ref, h_ref, hb_ref):
        xhat, _ = _ln_stats(x_ref[...])
        h = xhat * g_ref[...] + b_ref[...]
        h_ref[...] = h
        hb_ref[...] = _mx(h)

    return _pc(body, "ln_in", (t // tm,), [_row(tm, d), _const((1, d)), _const((1, d))],
               [_row(tm, d), _row(tm, d)], [SDS((t, d), F32), SDS((t, d), MXU_DTYPE)], sem=("parallel",))(x, g, b)


def _pool_fwd(proj, pool_w, tm):
    t = proj.shape[0]
    ublk = K_U // POOL_WIDTH

    def body(u_ref, halo_ref, pw_ref, ypre_ref, d_ref, ext_ref):
        i = pl.program_id(0)
        ext_ref[0:16, :] = jnp.where(i > 0, halo_ref[...], 0.0)
        ext_ref[16:16 + tm, :] = u_ref[...]
        tok = i * tm + lax.broadcasted_iota(jnp.int32, (tm, POOL_GROUP), 0)
        for gi, w in enumerate(POOL_WINDOWS):
            cs = pl.ds(gi * POOL_GROUP, POOL_GROUP)
            ug = ext_ref[pl.ds(16, tm), cs]
            s = ug
            for k in range(1, w):
                s = s + ext_ref[pl.ds(16 - k, tm), cs]
            cnt = jnp.minimum(tok + 1, w).astype(F32)
            db = _mx(s / cnt - ug)
            d_ref[:, gi * POOL_GROUP:(gi + 1) * POOL_GROUP] = db
            ypre_ref[:, gi * POOL_OUT_GROUP:(gi + 1) * POOL_OUT_GROUP] = _dot(db, pw_ref[gi])

    halo = pl.BlockSpec((16, POOL_WIDTH), lambda i: (jnp.maximum(i * (tm // 16) - 1, 0), ublk))
    return _pc(body, "pool_fwd", (t // tm,),
               [pl.BlockSpec((tm, POOL_WIDTH), lambda i: (i, ublk)), halo, _const((4, POOL_GROUP, POOL_OUT_GROUP))],
               [_row(tm, D_MODEL), _row(tm, POOL_WIDTH)],
               [SDS((t, D_MODEL), F32), SDS((t, POOL_WIDTH), MXU_DTYPE)],
               scratch=[pltpu.VMEM((16 + tm, POOL_WIDTH), F32)], sem=("parallel",))(proj, proj, pool_w)


def _pool_bwd(dyp, d_bf, pool_w, tm):
    t = dyp.shape[0]
    n = t // tm

    def body(dy_ref, dyn_ref, d_ref, pw_ref, du_ref, dpw_ref, ext_ref):
        i = pl.program_id(0)

        @pl.when(i == 0)
        def _():
            dpw_ref[...] = jnp.zeros_like(dpw_ref)

        tok = i * tm + lax.broadcasted_iota(jnp.int32, (tm + 16, POOL_GROUP), 0)
        for gi, w in enumerate(POOL_WINDOWS):
            dy = dy_ref[:, gi * POOL_OUT_GROUP:(gi + 1) * POOL_OUT_GROUP]
            dyn = dyn_ref[:, gi * POOL_OUT_GROUP:(gi + 1) * POOL_OUT_GROUP]
            pw = pw_ref[gi]
            dd = _dot_nt(dy, pw)
            ddn = jnp.where(i < n - 1, _dot_nt(dyn, pw), 0.0)
            cnt = jnp.minimum(tok + 1, w).astype(F32)
            ext_ref[0:tm, :] = dd / cnt[0:tm]
            ext_ref[tm:tm + 16, :] = ddn / cnt[tm:tm + 16]
            s = ext_ref[pl.ds(0, tm), :]
            for k in range(1, w):
                s = s + ext_ref[pl.ds(k, tm), :]
            du_ref[:, gi * POOL_GROUP:(gi + 1) * POOL_GROUP] = _mx(s - dd)
            dpw_ref[gi] += _dot_tn(d_ref[:, gi * POOL_GROUP:(gi + 1) * POOL_GROUP], dy)

    nxt = pl.BlockSpec((16, D_MODEL), lambda i: (jnp.minimum((i + 1) * (tm // 16), t // 16 - 1), 0))
    return _pc(body, "pool_bwd", (n,),
               [_row(tm, D_MODEL), nxt, _row(tm, POOL_WIDTH), _const((4, POOL_GROUP, POOL_OUT_GROUP))],
               [_row(tm, POOL_WIDTH), _const((4, POOL_GROUP, POOL_OUT_GROUP))],
               [SDS((t, POOL_WIDTH), MXU_DTYPE), SDS((4, POOL_GROUP, POOL_OUT_GROUP), F32)],
               scratch=[pltpu.VMEM((tm + 16, POOL_GROUP), F32)], sem=("arbitrary",))(dyp, dyp, d_bf, pool_w)


CONV_BLK = 512


CONV_ROWS = 32


def _conv_rows(ext_ref, w, r, rows):
    y = w[0] * ext_ref[pl.ds(r + 5, rows), :]
    for k in range(1, CONV_K):
        y = y + w[k] * ext_ref[pl.ds(r + 5 + k, rows), :]
    return y


def _conv_fwd(proj, conv_w, tm):
    t = proj.shape[0]

    def body(x_ref, halo_ref, w_ref, o_ref, ext_ref):
        i = pl.program_id(0)
        ext_ref[0:8, :] = jnp.where(i > 0, halo_ref[...], 0.0)
        ext_ref[8:8 + tm, :] = x_ref[...]
        w = [w_ref[pl.ds(k, 1), :] for k in range(CONV_K)]
        for r in range(0, tm, CONV_ROWS):
            y = _conv_rows(ext_ref, w, r, CONV_ROWS)
            o_ref[pl.ds(r, CONV_ROWS), :] = y * _sigmoid(y)

    halo = pl.BlockSpec((8, CONV_BLK), lambda i, j: (jnp.maximum(i * (tm // 8) - 1, 0), j))
    blk = pl.BlockSpec((tm, CONV_BLK), lambda i, j: (i, j))
    return _pc(body, "conv_fwd", (t // tm, QKV_WIDTH // CONV_BLK),
               [blk, halo, pl.BlockSpec((CONV_K, CONV_BLK), lambda i, j: (0, j))], blk,
               SDS((t, QKV_WIDTH), F32), scratch=[pltpu.VMEM((8 + tm, CONV_BLK), F32)],
               sem=("parallel", "parallel"))(proj, proj, conv_w)


def _conv_bwd(dact, proj, conv_w, tm):
    t = proj.shape[0]
    n = t // tm

    def body(da_ref, dan_ref, x_ref, xp_ref, xn_ref, w_ref, dx_ref, dw_ref, ext_ref, dy_ref):
        i = pl.program_id(1)

        @pl.when(i == 0)
        def _():
            dw_ref[...] = jnp.zeros_like(dw_ref)

        ext_ref[0:8, :] = jnp.where(i > 0, xp_ref[...], 0.0)
        ext_ref[8:8 + tm, :] = x_ref[...]
        ext_ref[8 + tm:16 + tm, :] = jnp.where(i < n - 1, xn_ref[...], 0.0)
        w = [w_ref[pl.ds(k, 1), :] for k in range(CONV_K)]

        def dsilu_rows(r, rows):
            y = _conv_rows(ext_ref, w, r, rows)
            s = _sigmoid(y)
            return s * (1.0 + y * (1.0 - s))

        for r in range(0, tm, CONV_ROWS):
            dy_ref[pl.ds(r, CONV_ROWS), :] = da_ref[pl.ds(r, CONV_ROWS), :] * dsilu_rows(r, CONV_ROWS)
        dy_ref[tm:tm + 8, :] = jnp.where(i < n - 1, dan_ref[...], 0.0) * dsilu_rows(tm, 8)
        acc = [jnp.zeros((8, CONV_BLK), F32) for _ in range(CONV_K)]
        for r in range(0, tm, CONV_ROWS):
            dx = w[0] * dy_ref[pl.ds(r + 3, CONV_ROWS), :]
            for k in range(1, CONV_K):
                dx = dx + w[k] * dy_ref[pl.ds(r + 3 - k, CONV_ROWS), :]
            dx_ref[pl.ds(r, CONV_ROWS), :] = _mx(dx)
            dy = dy_ref[pl.ds(r, CONV_ROWS), :]
            for k in range(CONV_K):
                prod = dy * ext_ref[pl.ds(r + 5 + k, CONV_ROWS), :]
                for q in range(0, CONV_ROWS, 8):
                    acc[k] = acc[k] + prod[q:q + 8]
        for k in range(CONV_K):
            dw_ref[pl.ds(k, 1), :] += jnp.sum(acc[k], axis=0, keepdims=True)

    blk = pl.BlockSpec((tm, CONV_BLK), lambda j, i: (i, j))
    prev = pl.BlockSpec((8, CONV_BLK), lambda j, i: (jnp.maximum(i * (tm // 8) - 1, 0), j))
    nxt = pl.BlockSpec((8, CONV_BLK), lambda j, i: (jnp.minimum((i + 1) * (tm // 8), t // 8 - 1), j))
    wspec = pl.BlockSpec((CONV_K, CONV_BLK), lambda j, i: (0, j))
    return _pc(body, "conv_bwd", (QKV_WIDTH // CONV_BLK, n),
               [blk, nxt, blk, prev, nxt, wspec],
               [blk, pl.BlockSpec((8, CONV_BLK), lambda j, i: (0, j))],
               [SDS((t, QKV_WIDTH), MXU_DTYPE), SDS((8, QKV_WIDTH), F32)],
               scratch=[pltpu.VMEM((16 + tm, CONV_BLK), F32), pltpu.VMEM((8 + tm, CONV_BLK), F32)],
               sem=("parallel", "arbitrary"))(dact, dact, proj, proj, proj, conv_w)


def _lane(shape):
    return lax.broadcasted_iota(jnp.int32, shape, 1)


def _ba_fwd(proj, al_row, dtb_row, tm):
    t = proj.shape[0]
    bablk = K_BA // 128

    def body(ba_ref, al_ref, dtb_ref, bg_ref):
        ba = ba_ref[...]
        lane = _lane(ba.shape)
        g = -jnp.exp(al_ref[...]) * _softplus(ba + dtb_ref[...])
        bg_ref[...] = jnp.where(lane < HEADS, _sigmoid(ba), jnp.where(lane < 2 * HEADS, g, 0.0))

    return _pc(body, "ba_fwd", (t // tm,),
               [pl.BlockSpec((tm, 128), lambda i: (i, bablk)), _const((1, 128)), _const((1, 128))],
               _row(tm, 128), SDS((t, 128), F32), sem=("parallel",))(proj, al_row, dtb_row)


def _ba_bwd(dbg, bg, proj, al_row, dtb_row, tm):
    t = proj.shape[0]
    bablk = K_BA // 128

    def body(dbg_ref, bg_ref, ba_ref, al_ref, dtb_ref, dba_ref, acc_ref):
        i = pl.program_id(0)

        @pl.when(i == 0)
        def _():
            acc_ref[...] = jnp.zeros_like(acc_ref)

        dbg_v, bg_v, ba = dbg_ref[...], bg_ref[...], ba_ref[...]
        lane = _lane(ba.shape)
        is_g = (lane >= HEADS) & (lane < 2 * HEADS)
        dbeta_raw = dbg_v * bg_v * (1.0 - bg_v)
        da_raw = dbg_v * (-jnp.exp(al_ref[...])) * _sigmoid(ba + dtb_ref[...])
        dba_ref[...] = _mx(jnp.where(lane < HEADS, dbeta_raw, jnp.where(is_g, da_raw, 0.0)))
        acc_ref[0:1, :] += jnp.sum(jnp.where(is_g, dbg_v * bg_v, 0.0), axis=0, keepdims=True)
        acc_ref[1:2, :] += jnp.sum(jnp.where(is_g, da_raw, 0.0), axis=0, keepdims=True)

    return _pc(body, "ba_bwd", (t // tm,),
               [_row(tm, 128), _row(tm, 128), pl.BlockSpec((tm, 128), lambda i: (i, bablk)),
                _const((1, 128)), _const((1, 128))],
               [_row(tm, 128), _const((8, 128))], [SDS((t, 128), MXU_DTYPE), SDS((8, 128), F32)],
               sem=("arbitrary",))(dbg, bg, proj, al_row, dtb_row)


def _each(f, *lists):
    return [f(*a) for a in zip(*lists)]


def _rowsum(a):
    return jnp.sum(a, axis=1, keepdims=True)


def _chunk_terms(qs, ks, bgv, g_rows, hs):
    c = CHUNK
    ii = lax.broadcasted_iota(jnp.int32, (c, c), 0)
    jj = lax.broadcasted_iota(jnp.int32, (c, c), 1)
    lane = _lane(bgv.shape)
    incl = ii >= jj
    beta = [_rowsum(jnp.where(lane == h, bgv, 0.0)) for h in hs]
    g_col = [_rowsum(jnp.where(lane == HEADS + h, bgv, 0.0)) for h in hs]
    rq = _each(lambda q: lax.rsqrt(_rowsum(q * q) + L2_EPS), qs)
    rk = _each(lambda k: lax.rsqrt(_rowsum(k * k) + L2_EPS), ks)
    yq = _each(jnp.multiply, qs, rq)
    kn = _each(jnp.multiply, ks, rk)
    qn = _each(lambda a: a * Q_SCALE, yq)
    gc_col = _each(lambda g: _rowsum(jnp.where(jj <= ii, g, 0.0)), g_rows)
    gc_row = _each(lambda g: jnp.sum(jnp.where(ii <= jj, g, 0.0), axis=0, keepdims=True), g_col)
    dm = _each(lambda a, b: jnp.where(incl, jnp.exp(jnp.where(incl, a - b, 0.0)), 0.0), gc_col, gc_row)
    gl = _each(_rowsum, g_rows)
    eg = _each(jnp.exp, gc_col)
    ek = _each(lambda a, b: jnp.exp(a - b), gl, gc_col)
    egl = _each(jnp.exp, gl)
    kb = _each(jnp.multiply, kn, beta)
    kk = _each(_dot_nt, kb, kn)
    qk = _each(_dot_nt, qn, kn)
    m = _each(lambda a, b: jnp.where(ii > jj, a * b, 0.0), kk, dm)
    attn = _each(jnp.multiply, qk, dm)
    return dict(ii=ii, jj=jj, beta=beta, rq=rq, rk=rk, yq=yq, kn=kn, qn=qn, dm=dm, eg=eg, ek=ek,
                egl=egl, kb=kb, m=m, attn=attn)


def _unit_lower_inverse_minus_identity(ms, ii, jj):
    pair = (ii >> 1) == (jj >> 1)
    ys = _each(lambda m: -jnp.where(pair, m, 0.0), ms)
    s = 1
    while (1 << s) < CHUNK:
        mask = ((ii >> (s + 1)) == (jj >> (s + 1))) & ((ii >> s) != (jj >> s))
        lbs = _each(lambda m: jnp.where(mask, m, 0.0), ms)
        zs = _each(lambda y, lb: lb + _dot(y, lb), ys, lbs)
        ys = _each(lambda y, z: y - z - _dot(z, y), ys, zs)
        s += 1
    return ys


def _head_offsets(group):
    hs = [group * HEAD_GROUP + a for a in range(HEAD_GROUP)]
    return hs, [pl.ds(pl.multiple_of(base + h * HEAD_DIM, HEAD_DIM), HEAD_DIM)
                for base in (0, DN_WIDTH, 2 * DN_WIDTH) for h in hs]


def _dn_local_fwd(qkv_act, bg, bgt):
    t = qkv_act.shape[0]
    nt = t // CHUNK
    c = CHUNK

    def body(qkv_ref, bg_ref, bgt_ref, u_ref, w_ref, qg_ref, kg_ref, attn_ref, y_ref, egl_ref):
        bgv = bg_ref[...]

        def group(gi, carry):
            hs, offs = _head_offsets(gi)
            qo, ko, vo = offs[0:HEAD_GROUP], offs[HEAD_GROUP:2 * HEAD_GROUP], offs[2 * HEAD_GROUP:]
            qs = [qkv_ref[:, o] for o in qo]
            ks = [qkv_ref[:, o] for o in ko]
            vs = [qkv_ref[:, o] for o in vo]
            g_rows = [bgt_ref[pl.ds(HEADS + h, 1), :] for h in hs]
            ct = _chunk_terms(qs, ks, bgv, g_rows, hs)
            ys = _unit_lower_inverse_minus_identity(ct["m"], ct["ii"], ct["jj"])
            vb = _each(jnp.multiply, vs, ct["beta"])
            kbe = _each(jnp.multiply, ct["kb"], ct["eg"])
            us = _each(lambda a, y: a + _dot(y, a), vb, ys)
            ws = _each(lambda a, y: a + _dot(y, a), kbe, ys)
            for a in range(HEAD_GROUP):
                dst = qo[a]
                u_ref[:, dst] = us[a]
                w_ref[:, dst] = _mx(ws[a])
                qg_ref[:, dst] = _mx(ct["qn"][a] * ct["eg"][a])
                kg_ref[:, dst] = _mx(ct["kn"][a] * ct["ek"][a])
                attn_ref[:, dst] = _mx(ct["attn"][a])
                y_ref[:, dst] = _mx(ys[a])
                egl_ref[0, pl.ds(hs[a], 1), :] = jnp.broadcast_to(ct["egl"][a], (1, HEAD_DIM))
            return carry

        lax.fori_loop(0, HEADS // HEAD_GROUP, group, 0)

    wide = _row(c, DN_WIDTH)
    return _pc(body, "dn_local_fwd", (nt,),
               [_row(c, QKV_WIDTH), _row(c, 128), pl.BlockSpec((2 * HEADS, c), lambda i: (0, i))],
               [wide, wide, wide, wide, wide, wide, pl.BlockSpec((1, HEADS, HEAD_DIM), lambda i: (i, 0, 0))],
               [SDS((t, DN_WIDTH), F32)] + [SDS((t, DN_WIDTH), MXU_DTYPE)] * 5 + [SDS((nt, HEADS, HEAD_DIM), F32)],
               sem=("parallel",))(qkv_act, bg, bgt)


def _dn_scan_fwd(u, w, qg, kg, attn, egl):
    t = u.shape[0]
    nt = t // CHUNK
    c = CHUNK
    sls = [slice(h * HEAD_DIM, (h + 1) * HEAD_DIM) for h in range(HEADS)]

    def body(u_ref, w_ref, qg_ref, kg_ref, attn_ref, egl_ref, o_ref, vn_ref, st_ref, s_ref):
        @pl.when(pl.program_id(0) == 0)
        def _():
            s_ref[...] = jnp.zeros_like(s_ref)

        ss = [s_ref[h] for h in range(HEADS)]
        sb = _each(_mx, ss)
        vn = [u_ref[:, sl] - _dot(w_ref[:, sl], b) for sl, b in zip(sls, sb)]
        vnb = _each(_mx, vn)
        oa = [_dot(qg_ref[:, sl], b) for sl, b in zip(sls, sb)]
        ob = [_dot(attn_ref[:, sl], b) for sl, b in zip(sls, vnb)]
        upd = [_dot_tn(kg_ref[:, sl], b) for sl, b in zip(sls, vnb)]
        for h, sl in enumerate(sls):
            st_ref[0, h] = ss[h]
            vn_ref[:, sl] = vnb[h]
            o_ref[:, sl] = oa[h] + ob[h]
            s_ref[h] = ss[h] * egl_ref[0, h:h + 1, :] + upd[h]

    wide = _row(c, DN_WIDTH)
    return _pc(body, "dn_scan_fwd", (nt,),
               [wide] * 5 + [pl.BlockSpec((1, HEADS, HEAD_DIM), lambda i: (i, 0, 0))],
               [wide, wide, pl.BlockSpec((1, HEADS, HEAD_DIM, HEAD_DIM), lambda i: (i, 0, 0, 0))],
               [SDS((t, DN_WIDTH), F32), SDS((t, DN_WIDTH), MXU_DTYPE), SDS((nt, HEADS, HEAD_DIM, HEAD_DIM), F32)],
               scratch=[pltpu.VMEM((HEADS, HEAD_DIM, HEAD_DIM), F32)], sem=("arbitrary",))(u, w, qg, kg, attn, egl)


def _dn_scan_bwd(do, qg, kg, w, attn, vn, states, egl):
    t = do.shape[0]
    nt = t // CHUNK
    c = CHUNK
    sls = [slice(h * HEAD_DIM, (h + 1) * HEAD_DIM) for h in range(HEADS)]

    def body(do_ref, qg_ref, kg_ref, w_ref, attn_ref, vn_ref, st_ref, egl_ref,
             dvn_ref, dkg_ref, dqg_ref, dattn_ref, dw_ref, degl_ref, ds_ref):
        @pl.when(pl.program_id(0) == 0)
        def _():
            ds_ref[...] = jnp.zeros_like(ds_ref)

        dsp = [ds_ref[h] for h in range(HEADS)]
        dsb = _each(_mx, dsp)
        ss = [st_ref[0, h] for h in range(HEADS)]
        sb = _each(_mx, ss)
        dvn = [_dot(kg_ref[:, sl], b) + _dot_tn(attn_ref[:, sl], do_ref[:, sl]) for sl, b in zip(sls, dsb)]
        dvnb = _each(_mx, dvn)
        dkg = [_dot_nt(vn_ref[:, sl], b) for sl, b in zip(sls, dsb)]
        dqg = [_dot_nt(do_ref[:, sl], b) for sl, b in zip(sls, sb)]
        dattn = [_dot_nt(do_ref[:, sl], vn_ref[:, sl]) for sl in sls]
        dwv = [-_dot_nt(a, b) for a, b in zip(dvnb, sb)]
        upd = [_dot_tn(qg_ref[:, sl], do_ref[:, sl]) - _dot_tn(w_ref[:, sl], a) for sl, a in zip(sls, dvnb)]
        for h, sl in enumerate(sls):
            dvn_ref[:, sl] = dvn[h]
            dkg_ref[:, sl] = dkg[h]
            dqg_ref[:, sl] = dqg[h]
            dattn_ref[:, sl] = dattn[h]
            dw_ref[:, sl] = dwv[h]
            degl = jnp.sum(_rowsum(ss[h] * dsp[h]), axis=0, keepdims=True)
            degl_ref[0, h:h + 1, :] = jnp.broadcast_to(degl, (1, HEAD_DIM))
            ds_ref[h] = dsp[h] * egl_ref[0, h:h + 1, :] + upd[h]

    rev = pl.BlockSpec((c, DN_WIDTH), lambda i: (nt - 1 - i, 0))
    rev3 = pl.BlockSpec((1, HEADS, HEAD_DIM), lambda i: (nt - 1 - i, 0, 0))
    rev4 = pl.BlockSpec((1, HEADS, HEAD_DIM, HEAD_DIM), lambda i: (nt - 1 - i, 0, 0, 0))
    return _pc(body, "dn_scan_bwd", (nt,), [rev] * 6 + [rev4, rev3], [rev] * 5 + [rev3],
               [SDS((t, DN_WIDTH), F32)] * 5 + [SDS((nt, HEADS, HEAD_DIM), F32)],
               scratch=[pltpu.VMEM((HEADS, HEAD_DIM, HEAD_DIM), F32)],
               sem=("arbitrary",))(do, qg, kg, w, attn, vn, states, egl)


def _dn_local_bwd(qkv_act, bg, bgt, u, w, ymat, dvn, dw, dqg, dkg, dattn, degl):
    t = qkv_act.shape[0]
    nt = t // CHUNK
    c = CHUNK

    def body(qkv_ref, bg_ref, bgt_ref, u_ref, w_ref, y_ref, du_ref, dw_ref, dqg_ref, dkg_ref, dattn_ref,
             degl_ref, dqkv_ref, dbg_ref):
        bgv = bg_ref[...]
        lane = _lane(bgv.shape)
        rowi = lax.broadcasted_iota(jnp.int32, (c, 1), 0)

        def group(gi, dbg):
            hs, offs = _head_offsets(gi)
            qo, ko, vo = offs[0:HEAD_GROUP], offs[HEAD_GROUP:2 * HEAD_GROUP], offs[2 * HEAD_GROUP:]
            qs = [qkv_ref[:, o] for o in qo]
            ks = [qkv_ref[:, o] for o in ko]
            vs = [qkv_ref[:, o] for o in vo]
            g_rows = [bgt_ref[pl.ds(HEADS + h, 1), :] for h in hs]
            ct = _chunk_terms(qs, ks, bgv, g_rows, hs)
            ii, jj = ct["ii"], ct["jj"]
            beta, eg, ek, kb, kn, qn, dm = ct["beta"], ct["eg"], ct["ek"], ct["kb"], ct["kn"], ct["qn"], ct["dm"]
            ys = [y_ref[:, o] for o in qo]
            du = [du_ref[:, o] for o in qo]
            dwv = [dw_ref[:, o] for o in qo]
            dqg_v = [dqg_ref[:, o] for o in qo]
            dkg_v = [dkg_ref[:, o] for o in qo]
            dattn_v = [dattn_ref[:, o] for o in qo]
            degl_v = [jnp.max(degl_ref[0, pl.ds(h, 1), :], axis=1, keepdims=True) for h in hs]
            dvb = _each(lambda a, y: a + _dot_tn(y, a), du, ys)
            dkbe = _each(lambda a, y: a + _dot_tn(y, a), dwv, ys)
            dm_u = [_dot_nt(a, u_ref[:, o]) for a, o in zip(dvb, qo)]
            dm_w = [_dot_nt(a, w_ref[:, o]) for a, o in zip(dkbe, qo)]
            dms = _each(lambda a, b: jnp.where(ii > jj, -(a + b), 0.0), dm_u, dm_w)
            dkk = _each(jnp.multiply, dms, dm)
            dqk = _each(jnp.multiply, dattn_v, dm)
            gmat = _each(lambda a, b, c_, d: a * b + c_ * d, dms, ct["m"], dattn_v, ct["attn"])
            dkb = _each(lambda a, b, c_, d: _dot(a, b) + c_ * d, dkk, kn, dkbe, eg)
            dk1 = _each(_dot_tn, dkk, kb)
            dk2 = _each(_dot_tn, dqk, qn)
            dq1 = _each(_dot, dqk, kn)
            dk = _each(lambda a, b, c_, d: a + b + c_ * d, dk1, dk2, dkg_v, ek)
            dq = _each(lambda a, b, c_: a + b * c_, dq1, dqg_v, eg)
            deg = _each(lambda a, b, c_, d: _rowsum(a * b) + _rowsum(c_ * d), dqg_v, qn, dkbe, kb)
            dek = _each(lambda a, b: _rowsum(a * b), dkg_v, kn)
            dgl = _each(lambda a, b, c_, d: jnp.sum(a * b, axis=0, keepdims=True) + c_ * d, dek, ek, degl_v, ct["egl"])
            cs_row = _each(lambda g: jnp.sum(g, axis=0, keepdims=True), gmat)
            cs_col = _each(lambda r: _rowsum(jnp.where(ii == jj, r, 0.0)), cs_row)
            dgc = _each(lambda a, b, c_, d, g, e, f: a * b - c_ * d + _rowsum(g) - e + jnp.where(rowi == c - 1, f, 0.0),
                        deg, eg, dek, ek, gmat, cs_col, dgl)
            dgc_row = _each(lambda a: jnp.sum(jnp.where(ii == jj, a, 0.0), axis=0, keepdims=True), dgc)
            dg = _each(lambda r: _rowsum(jnp.where(jj >= ii, r, 0.0)), dgc_row)
            dbeta = _each(lambda a, b, c_, d: _rowsum(a * b) + _rowsum(c_ * d), dkb, kn, dvb, vs)
            dk = _each(lambda a, b, c_: a + b * c_, dk, dkb, beta)
            for a in range(HEAD_GROUP):
                dyq = dq[a] * Q_SCALE
                yq = ct["yq"][a]
                dqkv_ref[:, qo[a]] = ct["rq"][a] * (dyq - yq * _rowsum(yq * dyq))
                dqkv_ref[:, ko[a]] = ct["rk"][a] * (dk[a] - kn[a] * _rowsum(kn[a] * dk[a]))
                dqkv_ref[:, vo[a]] = dvb[a] * beta[a]
                dbg = dbg + jnp.where(lane == hs[a], dbeta[a], 0.0) + jnp.where(lane == HEADS + hs[a], dg[a], 0.0)
            return dbg

        dbg_ref[...] = lax.fori_loop(0, HEADS // HEAD_GROUP, group, jnp.zeros((c, 128), F32))

    wide = _row(c, DN_WIDTH)
    sc3 = pl.BlockSpec((1, HEADS, HEAD_DIM), lambda i: (i, 0, 0))
    return _pc(body, "dn_local_bwd", (nt,),
               [_row(c, QKV_WIDTH), _row(c, 128), pl.BlockSpec((2 * HEADS, c), lambda i: (0, i))] + [wide] * 8 + [sc3],
               [_row(c, QKV_WIDTH), _row(c, 128)], [SDS((t, QKV_WIDTH), F32), SDS((t, 128), F32)],
               sem=("parallel",))(qkv_act, bg, bgt, u, w, ymat, dvn, dw, dqg, dkg, dattn, degl)


MIX_ROWS = 64


def _mix_fwd(o, proj, ypre, pool_scale, wo_row, tm):
    t = o.shape[0]

    def body(o_ref, z_ref, ga_ref, gb_ref, yp_ref, ps_ref, wo_ref, mixed_ref):
        for r in range(0, tm, MIX_ROWS):
            rows = pl.ds(r, MIX_ROWS)
            for h in range(HEADS):
                sl = slice(h * HEAD_DIM, (h + 1) * HEAD_DIM)
                oh = o_ref[rows, sl]
                on = oh * lax.rsqrt(jnp.mean(oh * oh, axis=1, keepdims=True) + RMS_EPS)
                zh = z_ref[rows, sl]
                yb = on * wo_ref[:, sl] * (zh * _sigmoid(zh))
                ya = yp_ref[rows, sl] * ps_ref[:, sl]
                mixed_ref[rows, sl] = _mx(_sigmoid(ga_ref[rows, sl]) * ya + _sigmoid(gb_ref[rows, sl]) * yb)

    def col(blk):
        return pl.BlockSpec((tm, D_MODEL), lambda i: (i, blk))

    return _pc(body, "mix_fwd", (t // tm,),
               [_row(tm, D_MODEL), col(K_Z // D_MODEL), col(K_GA // D_MODEL), col(K_GB // D_MODEL), _row(tm, D_MODEL),
                _const((1, D_MODEL)), _const((1, D_MODEL))],
               _row(tm, D_MODEL), SDS((t, D_MODEL), MXU_DTYPE), sem=("parallel",))(
                   o, proj, proj, proj, ypre, pool_scale, wo_row)


def _mix_bwd(da1_bf, w_out, o, proj, ypre, pool_scale, wo_row, tm):
    t = o.shape[0]

    def body(da_ref, wout_ref, o_ref, z_ref, ga_ref, gb_ref, yp_ref, ps_ref, wo_ref,
             do_ref, dz_ref, dga_ref, dgb_ref, dyp_ref, acc_ref, dm_ref):
        i = pl.program_id(0)

        @pl.when(i == 0)
        def _():
            acc_ref[...] = jnp.zeros_like(acc_ref)

        dm_ref[...] = _dot_nt(da_ref[...], wout_ref[...])
        dwo = jnp.zeros((1, HEAD_DIM), F32)
        for h in range(HEADS):
            sl = slice(h * HEAD_DIM, (h + 1) * HEAD_DIM)
            woh = wo_ref[:, sl]
            psh = ps_ref[:, sl]
            dps = jnp.zeros((1, HEAD_DIM), F32)
            for r in range(0, tm, MIX_ROWS):
                rows = pl.ds(r, MIX_ROWS)
                oh = o_ref[rows, sl]
                rs = lax.rsqrt(jnp.mean(oh * oh, axis=1, keepdims=True) + RMS_EPS)
                on = oh * rs
                zh = z_ref[rows, sl]
                sz = _sigmoid(zh)
                silu = zh * sz
                t1 = on * woh
                yb = t1 * silu
                sa = _sigmoid(ga_ref[rows, sl])
                sb = _sigmoid(gb_ref[rows, sl])
                yp = yp_ref[rows, sl]
                dm = dm_ref[rows, sl]
                dga_ref[rows, sl] = _mx(dm * (yp * psh) * sa * (1.0 - sa))
                dgb_ref[rows, sl] = _mx(dm * yb * sb * (1.0 - sb))
                dya = dm * sa
                dyb = dm * sb
                dyp_ref[rows, sl] = _mx(dya * psh)
                dps = dps + jnp.sum(dya * yp, axis=0, keepdims=True)
                dz_ref[rows, sl] = _mx(dyb * t1 * (sz * (1.0 + zh * (1.0 - sz))))
                dt1 = dyb * silu
                dwo = dwo + jnp.sum(dt1 * on, axis=0, keepdims=True)
                don = dt1 * woh
                do_ref[rows, sl] = _mx(rs * (don - on * jnp.mean(don * on, axis=1, keepdims=True)))
            acc_ref[0:1, sl] += dps
        acc_ref[1:2, 0:HEAD_DIM] += dwo

    def col(blk):
        return pl.BlockSpec((tm, D_MODEL), lambda i: (i, blk))

    r = _row(tm, D_MODEL)
    return _pc(body, "mix_bwd", (t // tm,),
               [r, _const((D_MODEL, D_MODEL)), r, col(K_Z // D_MODEL), col(K_GA // D_MODEL), col(K_GB // D_MODEL), r,
                _const((1, D_MODEL)), _const((1, D_MODEL))],
               [r, r, r, r, r, _const((8, D_MODEL))],
               [SDS((t, D_MODEL), MXU_DTYPE)] * 5 + [SDS((8, D_MODEL), F32)],
               scratch=[pltpu.VMEM((tm, D_MODEL), F32)],
               sem=("arbitrary",))(da1_bf, w_out, o, proj, proj, proj, ypre, pool_scale, wo_row)


def _oproj_ln1(mixed, w_out, h0, g1, b1, tm):
    t = mixed.shape[0]

    def body(m_ref, w_ref, h0_ref, g_ref, b_ref, a1_ref, h1_ref, h1b_ref):
        a1 = ALPHA * h0_ref[...] + _dot(m_ref[...], w_ref[...])
        a1_ref[...] = a1
        xhat, _ = _ln_stats(a1)
        h1 = xhat * g_ref[...] + b_ref[...]
        h1_ref[...] = h1
        h1b_ref[...] = _mx(h1)

    r = _row(tm, D_MODEL)
    v = _const((1, D_MODEL))
    return _pc(body, "oproj_ln1", (t // tm,), [r, _const((D_MODEL, D_MODEL)), r, v, v], [r, r, r],
               [SDS((t, D_MODEL), F32), SDS((t, D_MODEL), F32), SDS((t, D_MODEL), MXU_DTYPE)],
               sem=("parallel",))(mixed, w_out, h0, g1, b1)


def _mlp_up(h1_bf, w_up, tm):
    t = h1_bf.shape[0]
    tn = w_up.shape[2]

    def body(h_ref, w_ref, act_ref):
        r = jnp.maximum(_dot(h_ref[...], w_ref[...]), 0.0)
        act_ref[...] = _mx(r * r)

    return _pc(body, "mlp_up", (D_FF // tn, t // tm),
               [pl.BlockSpec((tm, D_MODEL), lambda j, i: (i, 0)),
                pl.BlockSpec((None, D_MODEL, tn), lambda j, i: (j, 0, 0))],
               pl.BlockSpec((tm, tn), lambda j, i: (i, j)), SDS((t, D_FF), MXU_DTYPE),
               sem=("parallel", "parallel"))(h1_bf, w_up)


def _tail(act, w_down, h1, w_gate, p_bf, w_proj, tgt, g2, b2, tm):
    t = act.shape[0]

    def body(act_ref, wd_ref, h1_ref, wg_ref, p_ref, wp_ref, tgt_ref, g_ref, b_ref,
             dr_ref, drb_ref, dgp_ref, dpp_ref, rb_ref, acc_ref):
        i = pl.program_id(0)

        @pl.when(i == 0)
        def _():
            acc_ref[...] = jnp.zeros_like(acc_ref)

        r = ALPHA * h1_ref[...] + _dot(act_ref[...], wd_ref[...])
        rb = _mx(r)
        rb_ref[...] = rb
        gate = _sigmoid(_dot(rb, wg_ref[...]))
        pp = _dot(p_ref[...], wp_ref[...])
        xhat, rstd = _ln_stats(r + gate * pp)
        g = g_ref[...]
        diff = xhat * g + b_ref[...] - tgt_ref[...]
        dh2 = diff * (1.0 / D_MODEL)
        rowloss = jnp.sum(diff * diff, axis=1, keepdims=True) * (0.5 / D_MODEL)
        acc_ref[0:1, :] += jnp.sum(dh2 * xhat, axis=0, keepdims=True)
        acc_ref[1:2, :] += jnp.sum(dh2, axis=0, keepdims=True)
        acc_ref[2:3, :] += jnp.broadcast_to(jnp.sum(rowloss, axis=0, keepdims=True), (1, D_MODEL))
        da2 = _ln_bwd(dh2, xhat, rstd, g)
        dpp_ref[...] = _mx(da2 * gate)
        dgp = _mx(da2 * pp * gate * (1.0 - gate))
        dgp_ref[...] = dgp
        dr = da2 + _dot_nt(dgp, wg_ref[...])
        dr_ref[...] = dr
        drb_ref[...] = _mx(dr)

    r = _row(tm, D_MODEL)
    v = _const((1, D_MODEL))
    return _pc(body, "tail", (t // tm,),
               [_row(tm, D_FF), _const((D_FF, D_MODEL)), r, _const((D_MODEL, D_MODEL)), _row(tm, PLE_DIM),
                _const((PLE_DIM, D_MODEL)), r, v, v],
               [r, r, r, r, r, _const((8, D_MODEL))],
               [SDS((t, D_MODEL), F32)] + [SDS((t, D_MODEL), MXU_DTYPE)] * 4 + [SDS((8, D_MODEL), F32)],
               sem=("arbitrary",))(act, w_down, h1, w_gate, p_bf, w_proj, tgt, g2, b2)


def _mlp_bwd1(dr_bf, w_down, act, tm, tn):
    t = act.shape[0]

    def body(dr_ref, w_ref, act_ref, dup_ref):
        dact = _dot_nt(dr_ref[...], w_ref[...])
        dup_ref[...] = _mx(dact * (2.0 * jnp.sqrt(act_ref[...].astype(F32))))

    o = pl.BlockSpec((tm, tn), lambda j, i: (i, j))
    return _pc(body, "mlp_bwd1", (D_FF // tn, t // tm),
               [pl.BlockSpec((tm, D_MODEL), lambda j, i: (i, 0)), pl.BlockSpec((tn, D_MODEL), lambda j, i: (j, 0)), o],
               o, SDS((t, D_FF), MXU_DTYPE), sem=("parallel", "parallel"))(dr_bf, w_down, act)


def _mlp_bwd2(dup, w_up, dr, a1, g1, tm):
    t = dr.shape[0]

    nk, tk = w_up.shape[0], w_up.shape[2]

    def body(dup_ref, w_ref, dr_ref, a1_ref, g_ref, da1_ref, da1b_ref, acc_ref):
        i = pl.program_id(0)

        @pl.when(i == 0)
        def _():
            acc_ref[...] = jnp.zeros_like(acc_ref)

        dh1 = ALPHA * dr_ref[...]
        for kk in range(nk):
            dh1 = dh1 + _dot_nt(dup_ref[:, kk * tk:(kk + 1) * tk], w_ref[kk])
        xhat, rstd = _ln_stats(a1_ref[...])
        acc_ref[0:1, :] += jnp.sum(dh1 * xhat, axis=0, keepdims=True)
        acc_ref[1:2, :] += jnp.sum(dh1, axis=0, keepdims=True)
        da1 = _ln_bwd(dh1, xhat, rstd, g_ref[...])
        da1_ref[...] = da1
        da1b_ref[...] = _mx(da1)

    r = _row(tm, D_MODEL)
    return _pc(body, "mlp_bwd2", (t // tm,),
               [_row(tm, D_FF), _const((nk, D_MODEL, tk)), r, r, _const((1, D_MODEL))],
               [r, r, _const((8, D_MODEL))],
               [SDS((t, D_MODEL), F32), SDS((t, D_MODEL), MXU_DTYPE), SDS((8, D_MODEL), F32)],
               sem=("arbitrary",))(dup, w_up, dr, a1, g1)


def _ln_in_bwd(dproj, w_cat, da1, x, g, tm):
    t = x.shape[0]

    def body(dp_ref, w_ref, da1_ref, x_ref, g_ref, dx_ref, acc_ref):
        i = pl.program_id(0)

        @pl.when(i == 0)
        def _():
            acc_ref[...] = jnp.zeros_like(acc_ref)

        dh0 = _dot_nt(dp_ref[...], w_ref[...]) + ALPHA * da1_ref[...]
        xhat, rstd = _ln_stats(x_ref[...])
        acc_ref[0:1, :] += jnp.sum(dh0 * xhat, axis=0, keepdims=True)
        acc_ref[1:2, :] += jnp.sum(dh0, axis=0, keepdims=True)
        dx_ref[...] = _ln_bwd(dh0, xhat, rstd, g_ref[...])

    r = _row(tm, D_MODEL)
    return _pc(body, "ln_in_bwd", (t // tm,),
               [_row(tm, CAT_WIDTH), _const((D_MODEL, CAT_WIDTH)), r, r, _const((1, D_MODEL))],
               [r, _const((8, D_MODEL))], [SDS((t, D_MODEL), F32), SDS((8, D_MODEL), F32)],
               sem=("arbitrary",))(dproj, w_cat, da1, x, g)


def _local_step(x, p, tgt, wts):
    t = x.shape[0]
    tm = min(512, t)
    tms = min(256, t)
    row = lambda a: a.reshape(1, -1)
    w_cat = wts["w_cat"]
    pool_scale = row(wts["pool_scale"])
    wo_row = jnp.tile(row(wts["o_norm_w"]), (1, HEADS))
    pad8 = jnp.zeros((1, HEADS), F32)
    al_row = jnp.concatenate([pad8, row(wts["a_log"]), jnp.zeros((1, 128 - 2 * HEADS), F32)], axis=1)
    dtb_row = jnp.concatenate([pad8, row(wts["dt_bias"]), jnp.zeros((1, 128 - 2 * HEADS), F32)], axis=1)
    g_in, b_in = row(wts["ln_in_g"]), row(wts["ln_in_b"])
    g1, b1 = row(wts["ln1_g"]), row(wts["ln1_b"])
    g2, b2 = row(wts["ln2_g"]), row(wts["ln2_b"])

    h0, h0_bf = _ln_in(x, g_in, b_in, tm)
    proj = _proj(h0_bf, w_cat, tms)
    ypre, d_bf = _pool_fwd(proj, wts["pool_w"], tm)
    qkv_act = _conv_fwd(proj, wts["conv_w"], tm)
    bg = _ba_fwd(proj, al_row, dtb_row, tm)
    bgt = bg[:, :2 * HEADS].T
    u, w, qg, kg, attn, ymat, egl = _dn_local_fwd(qkv_act, bg, bgt)
    o, vn, states = _dn_scan_fwd(u, w, qg, kg, attn, egl)
    mixed = _mix_fwd(o, proj, ypre, pool_scale, wo_row, tm)
    a1, h1, h1_bf = _oproj_ln1(mixed, wts["w_out"], h0, g1, b1, tm)
    act = _mlp_up(h1_bf, wts["w_up"], tm)
    p_bf = _mx(p)
    dr, dr_bf, dgp, dpp, r_bf, acc_tail = _tail(act, wts["w_down"], h1, wts["ple_gate_w"], p_bf, wts["ple_proj_w"],
                                                tgt, g2, b2, tms)
    grads = {}
    grads["ple_proj_w"] = _matmul(p_bf, dpp, "tn", "dw_ple_proj", WIRE_DTYPE, tm=256, tn=1024, tk=DW_TK)
    grads["ple_gate_w"] = _matmul(r_bf, dgp, "tn", "dw_ple_gate", WIRE_DTYPE, tm=DW_TM, tn=1024, tk=DW_TK)
    grads["w_down"] = _matmul(act, dr_bf, "tn", "dw_down", WIRE_DTYPE, tm=DW_TM, tn=1024, tk=DW_TK)
    dup = _mlp_bwd1(dr_bf, wts["w_down"], act, tm, 1024)
    grads["w_up"] = _matmul(h1_bf, dup, "tn", "dw_up", WIRE_DTYPE, tm=DW_TM, tn=1024, tk=DW_TK, stack_out=True)
    da1, da1_bf, acc_ln1 = _mlp_bwd2(dup, wts["w_up"], dr, a1, g1, tms)
    grads["w_out"] = _matmul(mixed, da1_bf, "tn", "dw_out", WIRE_DTYPE, tm=DW_TM, tn=1024, tk=DW_TK)
    do, dz, dga, dgb, dyp, acc_mix = _mix_bwd(da1_bf, wts["w_out"], o, proj, ypre, pool_scale, wo_row, tms)
    du_pool, grads["pool_w"] = _pool_bwd(dyp, d_bf, wts["pool_w"], tm)
    dvn, dkg, dqg, dattn, dw, degl = _dn_scan_bwd(do, qg, kg, w, attn, vn, states, egl)
    dqkv_act, dbg = _dn_local_bwd(qkv_act, bg, bgt, u, w, ymat, dvn, dw, dqg, dkg, dattn, degl)
    dqkv, acc_conv = _conv_bwd(dqkv_act, proj, wts["conv_w"], tm)
    dba, acc_ba = _ba_bwd(dbg, bg, proj, al_row, dtb_row, tm)
    dproj = jnp.concatenate([dqkv, dz, dga, dgb, du_pool, dba,
                             jnp.zeros((t, CAT_WIDTH - K_BA - 128), MXU_DTYPE)], axis=1)
    dw_cat = _matmul(h0_bf, dproj, "tn", "dw_in", F32, tm=DW_TM, tn=1152, tk=DW_TK)
    grad_x, acc_in = _ln_in_bwd(dproj, w_cat, da1, x, g_in, tms)

    grads["w_in"] = jnp.concatenate(
        [dw_cat[:, K_U:K_U + 512], dw_cat[:, K_QKV:K_QKV + 3072], dw_cat[:, K_Z:K_Z + 1024],
         dw_cat[:, K_BA:K_BA + 16], dw_cat[:, K_GA:K_GA + 1024], dw_cat[:, K_GB:K_GB + 1024]], axis=1)
    grads["conv_w"] = acc_conv[0:CONV_K]
    grads["ln_in_g"], grads["ln_in_b"] = acc_in[0], acc_in[1]
    grads["ln1_g"], grads["ln1_b"] = acc_ln1[0], acc_ln1[1]
    grads["ln2_g"], grads["ln2_b"] = acc_tail[0], acc_tail[1]
    grads["pool_scale"] = acc_mix[0]
    grads["o_norm_w"] = acc_mix[1, 0:HEAD_DIM]
    grads["a_log"] = acc_ba[0, HEADS:2 * HEADS]
    grads["dt_bias"] = acc_ba[1, HEADS:2 * HEADS]
    loss = acc_tail[2, 0]
    return grad_x, grads, loss


MESH = pl.DeviceIdType.MESH
ANY = pl.BlockSpec(memory_space=pl.ANY)


def _chip_of(k, x, y):
    chip = (2 * x + y + k) % N_CHIPS
    return chip // 2, chip % 2


def _place():
    x, y, c = lax.axis_index("x"), lax.axis_index("y"), lax.axis_index("c")
    return x, y, c, 2 * x + y


def _half(rows, c):
    return pl.ds(pl.multiple_of(c * (rows // 2), 16), rows // 2)


def _remote(src, dst, send_sem, recv_sem, device_id):
    return pltpu.make_async_remote_copy(src_ref=src, dst_ref=dst, send_sem=send_sem, recv_sem=recv_sem,
                                        device_id=device_id, device_id_type=MESH)


def _tile_rows(rows):
    for tr in (256, 128, 64, 32, 16):
        if rows % tr == 0:
            return tr
    raise ValueError(rows)


def _gather_weights(shards, conv_shard):
    n = len(shards)

    def body(*refs):
        ins, conv_in = refs[0:n], refs[n]
        outs, conv_out = refs[n + 1:2 * n + 1], refs[2 * n + 1]
        send, recv, fsend, frecv, csend, crecv, lsend, lrecv = refs[2 * n + 2:]
        x, y, c, me = _place()
        here, sib = (x, y, c), (x, y, 1 - c)
        own = [_remote(ins[a], outs[a].at[me], lsend.at[a], lrecv.at[a], sib) for a in range(n)]
        own.append(_remote(conv_in, conv_out.at[me], lsend.at[n], lrecv.at[n], sib))
        for cp in own:
            cp.start()
        sends = []
        for k in range(1, N_CHIPS):
            tx, ty = _chip_of(k, x, y)
            for a in range(n):
                mine = _half(shards[a].shape[0], c)
                sends.append(_remote(ins[a].at[mine], outs[a].at[me, mine], send.at[a * N_CHIPS + k],
                                     recv.at[a * N_CHIPS + k], (tx, ty, c)))
            sends.append(_remote(conv_in, conv_out.at[me], csend.at[k], crecv.at[k], (tx, ty, c)))
        for cp in sends:
            cp.start()
        forwards = []
        for k in range(1, N_CHIPS):
            src = (me + N_CHIPS - k) % N_CHIPS
            for a in range(n):
                landed = outs[a].at[src, _half(shards[a].shape[0], c)]
                _remote(landed, landed, send.at[a * N_CHIPS + k], recv.at[a * N_CHIPS + k], here).wait_recv()
                fwd = _remote(landed, landed, fsend.at[a * N_CHIPS + k], frecv.at[a * N_CHIPS + k], sib)
                fwd.start()
                forwards.append(fwd)
            _remote(conv_in, conv_out.at[src], csend.at[k], crecv.at[k], here).wait_recv()
        for k in range(1, N_CHIPS):
            src = (me + N_CHIPS - k) % N_CHIPS
            for a in range(n):
                passed = outs[a].at[src, _half(shards[a].shape[0], 1 - c)]
                _remote(passed, passed, fsend.at[a * N_CHIPS + k], frecv.at[a * N_CHIPS + k], here).wait_recv()
        for cp in sends + forwards:
            cp.wait_send()
        for cp in own:
            cp.wait()

    sems = pltpu.SemaphoreType.DMA((n * N_CHIPS,))
    return pl.pallas_call(
        body, name="gather_weights",
        out_shape=[SDS((N_CHIPS,) + s.shape, s.dtype) for s in shards]
        + [SDS((N_CHIPS,) + conv_shard.shape, conv_shard.dtype)],
        in_specs=[ANY] * (n + 1), out_specs=[ANY] * (n + 1),
        scratch_shapes=[sems, sems, sems, sems, pltpu.SemaphoreType.DMA((N_CHIPS,)),
                        pltpu.SemaphoreType.DMA((N_CHIPS,)), pltpu.SemaphoreType.DMA((n + 1,)),
                        pltpu.SemaphoreType.DMA((n + 1,))],
    )(*shards, conv_shard)


def _swap_halves(gs):
    n = len(gs)

    def body(*refs):
        ins, theirs = refs[0:n], refs[n:2 * n]
        send, recv = refs[2 * n:]
        x, y, c, _ = _place()
        copies = [_remote(ins[a].at[:, _half(gs[a].shape[1], 1 - c)], theirs[a], send.at[a], recv.at[a],
                          (x, y, 1 - c)) for a in range(n)]
        for cp in copies:
            cp.start()
        for cp in copies:
            cp.wait()

    return pl.pallas_call(
        body, name="swap_halves", out_shape=[SDS((N_CHIPS, g.shape[1] // 2, g.shape[2]), g.dtype) for g in gs],
        in_specs=[ANY] * n, out_specs=[ANY] * n, scratch_shapes=[pltpu.SemaphoreType.DMA((n,))] * 2,
    )(*gs)


def _scatter_halves(qs):
    n = len(qs)

    def body(*refs):
        ins, outs = refs[0:n], refs[n:2 * n]
        send, recv = refs[2 * n:]
        x, y, c, me = _place()
        copies = []
        for k in range(1, N_CHIPS):
            tx, ty = _chip_of(k, x, y)
            for a in range(n):
                copies.append(_remote(ins[a].at[2 * tx + ty], outs[a].at[k - 1], send.at[a * N_CHIPS + k],
                                      recv.at[a * N_CHIPS + k], (tx, ty, c)))
        for cp in copies:
            cp.start()
        for cp in copies:
            cp.wait()

    sems = pltpu.SemaphoreType.DMA((n * N_CHIPS,))
    return pl.pallas_call(
        body, name="scatter_halves", out_shape=[SDS((N_CHIPS - 1,) + q.shape[1:], q.dtype) for q in qs],
        in_specs=[ANY] * n, out_specs=[ANY] * n, scratch_shapes=[sems, sems],
    )(*qs)


def _send_to_sibling(hs):
    n = len(hs)

    def body(*refs):
        ins, outs = refs[0:n], refs[n:2 * n]
        send, recv = refs[2 * n:]
        x, y, c, _ = _place()
        copies = [_remote(ins[a], outs[a], send.at[a], recv.at[a], (x, y, 1 - c)) for a in range(n)]
        for cp in copies:
            cp.start()
        for cp in copies:
            cp.wait()

    return pl.pallas_call(
        body, name="send_to_sibling", out_shape=[SDS(h.shape, h.dtype) for h in hs],
        in_specs=[ANY] * n, out_specs=[ANY] * n, scratch_shapes=[pltpu.SemaphoreType.DMA((n,))] * 2,
    )(*hs)


def _add_pair(g, theirs, name):
    _, rows, cols = g.shape
    half = rows // 2
    tr = _tile_rows(half)

    def body(g_ref, t_ref, o_ref):
        own = g_ref[lax.axis_index("c")]
        o_ref[...] = (own.astype(F32) + t_ref[...].astype(F32)).astype(o_ref.dtype)

    blk = pl.BlockSpec((None, tr, cols), lambda j, i: (j, i, 0))
    return _pc(body, "add_" + name, (N_CHIPS, half // tr),
               [pl.BlockSpec((None, 2, tr, cols), lambda j, i: (j, 0, i, 0)), blk], blk,
               SDS((N_CHIPS, half, cols), g.dtype), sem=("parallel", "parallel"))(
                   g.reshape(N_CHIPS, 2, half, cols), theirs)


def _sum_slabs(pair, landed, name):
    _, rows, cols = pair.shape
    tr = _tile_rows(rows)

    def body(p_ref, r_ref, o_ref):
        acc = p_ref[2 * lax.axis_index("x") + lax.axis_index("y")].astype(F32)
        for k in range(N_CHIPS - 1):
            acc = acc + r_ref[k].astype(F32)
        o_ref[...] = acc

    return _pc(body, "sum_" + name, (rows // tr,),
               [pl.BlockSpec((N_CHIPS, tr, cols), lambda i: (0, i, 0)),
                pl.BlockSpec((N_CHIPS - 1, tr, cols), lambda i: (0, i, 0))],
               _row(tr, cols), SDS((rows, cols), F32), sem=("parallel",))(pair, landed)


def _adamw_math(w, g, m, v):
    m = ADAM_B1 * m + (1.0 - ADAM_B1) * g
    v = ADAM_B2 * v + (1.0 - ADAM_B2) * (g * g)
    m_hat = m / (1.0 - ADAM_B1 ** ADAM_STEP)
    v_hat = v / (1.0 - ADAM_B2 ** ADAM_STEP)
    delta = -ADAM_LR * (m_hat / (jnp.sqrt(v_hat) + ADAM_EPS) + ADAM_WD * w)
    return delta, m, v


def _adamw_2d(w, g_own, g_sib, m, v, name):
    rows, cols = w.shape
    tr = _tile_rows(rows // 2)
    nh = rows // 2 // tr

    def body(w_ref, go_ref, gs_ref, m_ref, v_ref, g_out, d_out, m_out, v_out):
        mine = (pl.program_id(0) // nh) == lax.axis_index("c")
        g = jnp.where(mine, go_ref[...], gs_ref[...])
        delta, mn, vn = _adamw_math(w_ref[...], g, m_ref[...], v_ref[...])
        g_out[...] = g
        d_out[...] = delta
        m_out[...] = mn
        v_out[...] = vn

    r = _row(tr, cols)
    h = pl.BlockSpec((tr, cols), lambda i: (i % nh, 0))
    return _pc(body, "adamw_" + name, (rows // tr,), [r, h, h, r, r], [r] * 4, [SDS((rows, cols), F32)] * 4,
               sem=("parallel",))(w, g_own, g_sib, m, v)


def _small_allreduce_adamw(mine, w, m, v):
    shape = mine.shape

    def body(mine_ref, w_ref, m_ref, v_ref, g_out, d_out, m_out, v_out, buf_ref, send_sems, recv_sems):
        x, y, c = lax.axis_index("x"), lax.axis_index("y"), lax.axis_index("c")
        me = 4 * x + 2 * y + c
        buf_ref[me] = mine_ref[...]
        copies = []
        for k in range(1, N_DEV):
            tgt = (me + k) % N_DEV
            copies.append(pltpu.make_async_remote_copy(
                src_ref=mine_ref, dst_ref=buf_ref.at[me], send_sem=send_sems.at[k], recv_sem=recv_sems.at[k],
                device_id=(tgt // 4, (tgt // 2) % 2, tgt % 2), device_id_type=MESH))
        for cp in copies:
            cp.start()
        for k in range(1, N_DEV):
            src = (me + N_DEV - k) % N_DEV
            pltpu.make_async_remote_copy(
                src_ref=mine_ref, dst_ref=buf_ref.at[src], send_sem=send_sems.at[k], recv_sem=recv_sems.at[k],
                device_id=(x, y, c), device_id_type=MESH).wait_recv()
        for cp in copies:
            cp.wait_send()
        g = buf_ref[0]
        for j in range(1, N_DEV):
            g = g + buf_ref[j]
        delta, mn, vn = _adamw_math(w_ref[...], g, m_ref[...], v_ref[...])
        g_out[...] = g
        d_out[...] = delta
        m_out[...] = mn
        v_out[...] = vn

    vm = pl.BlockSpec(memory_space=pltpu.VMEM)
    return pl.pallas_call(
        body, name="small_allreduce_adamw", out_shape=[SDS(shape, F32)] * 4, in_specs=[vm] * 4, out_specs=[vm] * 4,
        scratch_shapes=[pltpu.VMEM((N_DEV,) + shape, F32), pltpu.SemaphoreType.DMA((N_DEV,)),
                        pltpu.SemaphoreType.DMA((N_DEV,))],
    )(mine, w, m, v)


def _as2d(a):
    return a.reshape(-1, a.shape[-1])


def _full_weights(stacks):
    wi = stacks["w_in"].transpose(1, 0, 2).reshape(D_MODEL, IN_WIDTH)
    return {
        "w_cat": jnp.concatenate(
            [wi[:, C_QKV:C_Z], wi[:, C_Z:C_BETA], wi[:, C_GA:C_GB], wi[:, C_GB:IN_WIDTH], wi[:, C_POOL:C_QKV],
             wi[:, C_BETA:C_GA], jnp.zeros((D_MODEL, CAT_WIDTH - K_BA - 2 * HEADS), wi.dtype)], axis=1),
        "pool_w": stacks["pool_w"].reshape(N_CHIPS, 4, POOL_GROUP, POOL_OUT_GROUP // N_CHIPS)
                                  .transpose(1, 2, 0, 3).reshape(4, POOL_GROUP, POOL_OUT_GROUP),
        "w_out": stacks["w_out"].reshape(D_MODEL, D_MODEL),
        "w_up": stacks["w_up"],
        "w_down": stacks["w_down"].reshape(D_FF, D_MODEL),
        "ple_gate_w": stacks["ple_gate_w"].reshape(D_MODEL, D_MODEL),
        "ple_proj_w": stacks["ple_proj_w"].transpose(1, 0, 2).reshape(PLE_DIM, D_MODEL),
    }


def _grads_by_chip(grads):
    wire = lambda a: a.astype(WIRE_DTYPE)
    return {
        "w_in": wire(grads["w_in"].reshape(D_MODEL, N_CHIPS, IN_WIDTH // N_CHIPS).transpose(1, 0, 2)),
        "pool_w": wire(grads["pool_w"].reshape(4, POOL_GROUP, N_CHIPS, POOL_OUT_GROUP // N_CHIPS)
                       .transpose(2, 0, 1, 3).reshape(N_CHIPS, 4 * POOL_GROUP, POOL_OUT_GROUP // N_CHIPS)),
        "w_out": wire(grads["w_out"]).reshape(N_CHIPS, D_MODEL // N_CHIPS, D_MODEL),
        "w_up": wire(grads["w_up"]),
        "w_down": wire(grads["w_down"]).reshape(N_CHIPS, D_FF // N_CHIPS, D_MODEL),
        "ple_gate_w": wire(grads["ple_gate_w"]).reshape(N_CHIPS, D_MODEL // N_CHIPS, D_MODEL),
        "ple_proj_w": wire(grads["ple_proj_w"]).reshape(PLE_DIM, N_CHIPS, D_MODEL // N_CHIPS).transpose(1, 0, 2),
    }


def _pad_row(a):
    a = a.reshape(1, -1).astype(F32)
    return jnp.pad(a, ((0, 0), (0, D_MODEL - a.shape[1])))


def kernel(x, p, ln_in_g, ln_in_b, w_in, pool_w, pool_scale, conv_w, a_log, dt_bias, o_norm_w, w_out, ln1_g, ln1_b, w_up, w_down, ple_gate_w, ple_proj_w, ln2_g, ln2_b, loss_target, m_ln_in_g, m_ln_in_b, m_w_in, m_pool_w, m_pool_scale, m_conv_w, m_a_log, m_dt_bias, m_o_norm_w, m_w_out, m_ln1_g, m_ln1_b, m_w_up, m_w_down, m_ple_gate_w, m_ple_proj_w, m_ln2_g, m_ln2_b, v_ln_in_g, v_ln_in_b, v_w_in, v_pool_w, v_pool_scale, v_conv_w, v_a_log, v_dt_bias, v_o_norm_w, v_w_out, v_ln1_g, v_ln1_b, v_w_up, v_w_down, v_ple_gate_w, v_ple_proj_w, v_ln2_g, v_ln2_b):
    given = dict(locals())
    chip = 2 * lax.axis_index("x") + lax.axis_index("y")

    conv_pad = jnp.pad(conv_w[0], ((0, 8 - CONV_K), (0, 0)))
    gathered = _gather_weights([_as2d(given[n]).astype(WIRE_DTYPE) for n in BIG], conv_pad)
    wts = _full_weights({n: g.astype(MXU_DTYPE) for n, g in zip(BIG, gathered[0:len(BIG)])})
    wts.update({
        "conv_w": jnp.concatenate([gathered[len(BIG)][j, 0:CONV_K] for j in range(N_CHIPS)], axis=1),
        "ln_in_g": ln_in_g, "ln_in_b": ln_in_b, "pool_scale": pool_scale[0], "a_log": a_log[0],
        "dt_bias": dt_bias[0], "o_norm_w": o_norm_w[0], "ln1_g": ln1_g[0], "ln1_b": ln1_b[0],
        "ln2_g": ln2_g[0], "ln2_b": ln2_b[0],
    })

    grad_x, grads, loss = _local_step(x[0], p[0, 0], loss_target[0], wts)

    by_chip = _grads_by_chip(grads)
    theirs = _swap_halves([by_chip[n] for n in BIG])
    pair = [_add_pair(by_chip[n], t, n) for t, n in zip(theirs, BIG)]
    landed = _scatter_halves(pair)
    reduced = [_sum_slabs(q, r, n) for q, r, n in zip(pair, landed, BIG)]
    from_sibling = _send_to_sibling(reduced)
    big_out = {}
    for n, g_own, g_sib in zip(BIG, reduced, from_sibling):
        res = _adamw_2d(_as2d(given[n]), g_own, g_sib, _as2d(given["m_" + n]), _as2d(given["v_" + n]), n)
        big_out[n] = [r.reshape(given[n].shape) for r in res]

    conv_cols = QKV_WIDTH // N_CHIPS

    def small_rows(get, conv):
        if conv.shape[1] != QKV_WIDTH:
            conv = lax.dynamic_update_slice(jnp.zeros((CONV_K, QKV_WIDTH), F32), conv, (0, chip * conv_cols))
        return [_pad_row(get(n)) for n in SMALL_NAMES], conv.reshape(SMALL_CONV_ROWS, D_MODEL)

    fill = jnp.zeros((SMALL_CONV_AT - len(SMALL_NAMES), D_MODEL), F32)
    rows, conv = small_rows(lambda n: grads[n], grads["conv_w"])
    mine_small = jnp.concatenate(rows + [jnp.full((1, D_MODEL), loss, F32), fill[1:], conv], axis=0)
    packed_small = []
    for prefix in ("", "m_", "v_"):
        rows, conv = small_rows(lambda n: given[prefix + n], given[prefix + "conv_w"][0])
        packed_small.append(jnp.concatenate(rows + [fill, conv], axis=0))
    small_out = _small_allreduce_adamw(mine_small, *packed_small)

    def small_get(k, n):
        if n == "conv_w":
            full = small_out[k][SMALL_CONV_AT:SMALL_CONV_AT + SMALL_CONV_ROWS].reshape(CONV_K, QKV_WIDTH)
            return lax.dynamic_slice(full, (0, chip * conv_cols), (CONV_K, conv_cols)).reshape(given[n].shape)
        i = SMALL_NAMES.index(n)
        return small_out[k][i, 0:given[n].size].reshape(given[n].shape)

    order = ["ln_in_g", "ln_in_b", "w_in", "pool_w", "pool_scale", "conv_w", "a_log", "dt_bias", "o_norm_w", "w_out",
             "ln1_g", "ln1_b", "w_up", "w_down", "ple_gate_w", "ple_proj_w", "ln2_g", "ln2_b"]
    outs = [small_out[0][len(SMALL_NAMES), 0], grad_x[None]]
    for k in range(4):
        for n in order:
            outs.append(big_out[n][k] if n in big_out else small_get(k, n))
    return tuple(outs)
```

```python
import jax
import jax.numpy as jnp
from jax import lax
from jax.experimental import pallas as pl
from jax.experimental.pallas import tpu as pltpu

F32 = jnp.float32
MXU_DTYPE = jnp.bfloat16
WIRE_DTYPE = jnp.bfloat16
SDS = jax.ShapeDtypeStruct

D_MODEL = 1024
POOL_WINDOWS = (2, 4, 8, 16)
POOL_WIDTH = 512
POOL_GROUP = 128
POOL_OUT_GROUP = 256
HEADS = 8
HEAD_DIM = 128
DN_WIDTH = HEADS * HEAD_DIM
QKV_WIDTH = 3 * DN_WIDTH
CONV_K = 4
CHUNK = 128
DW_TK = 1024
DW_TM = 1024
HEAD_GROUP = 8
D_FF = 4096
PLE_DIM = 256
LN_EPS = 1e-5
RMS_EPS = 1e-6
L2_EPS = 1e-6
ALPHA = 2.0 ** 0.25
Q_SCALE = HEAD_DIM ** -0.5
IN_WIDTH = 6672
C_POOL, C_QKV, C_Z, C_BETA, C_A, C_GA, C_GB = 0, 512, 3584, 4608, 4616, 4624, 5648
K_QKV, K_Z, K_GA, K_GB, K_U, K_BA, CAT_WIDTH = 0, 3072, 4096, 5120, 6144, 6656, 6912

ADAM_LR, ADAM_B1, ADAM_B2, ADAM_EPS, ADAM_WD, ADAM_STEP = 0.001, 0.9, 0.999, 1e-08, 0.01, 10

N_CHIPS = 4
N_DEV = 8
VMEM_LIMIT = 56 * 1024 * 1024

EARLY = ("w_in", "pool_w")
LATE = ("w_out", "w_up", "w_down", "ple_gate_w", "ple_proj_w")
SMALL_NAMES = ("ln_in_g", "ln_in_b", "pool_scale", "ln1_g", "ln1_b", "ln2_g", "ln2_b", "o_norm_w", "a_log", "dt_bias")
SMALL_CONV_AT = 12
SMALL_CONV_ROWS = CONV_K * QKV_WIDTH // D_MODEL


def _mx(a):
    return a.astype(MXU_DTYPE)


def _dot(a, b):
    return lax.dot_general(_mx(a), _mx(b), (((1,), (0,)), ((), ())), preferred_element_type=F32)


def _dot_nt(a, b):
    return lax.dot_general(_mx(a), _mx(b), (((1,), (1,)), ((), ())), preferred_element_type=F32)


def _dot_tn(a, b):
    return lax.dot_general(_mx(a), _mx(b), (((0,), (0,)), ((), ())), preferred_element_type=F32)


def _sigmoid(x):
    return 0.5 * jnp.tanh(0.5 * x) + 0.5


def _softplus(x):
    return jnp.maximum(x, 0.0) + jnp.log(1.0 + jnp.exp(-jnp.abs(x)))


def _pc(body, name, grid, in_specs, out_specs, out_shape, scratch=(), sem=None):
    return pl.pallas_call(
        body, out_shape=out_shape, grid=grid, in_specs=in_specs, out_specs=out_specs,
        scratch_shapes=scratch, name=name,
        compiler_params=pltpu.CompilerParams(dimension_semantics=sem, vmem_limit_bytes=VMEM_LIMIT))


def _row(tm, n):
    return pl.BlockSpec((tm, n), lambda i: (i, 0))


def _const(shape):
    nd = len(shape)
    return pl.BlockSpec(shape, lambda *_: (0,) * nd)


def _matmul(a, b, mode, name, out_dtype=F32, tm=512, tn=512, tk=512, stack_out=False):
    if mode == "nn":
        (m, k), n = a.shape, b.shape[1]
    elif mode == "nt":
        (m, k), n = a.shape, b.shape[0]
    else:
        (k, m), n = a.shape, b.shape[1]
    tm, tn, tk = min(tm, m), min(tn, n), min(tk, k)
    assert m % tm == 0 and n % tn == 0 and k % tk == 0, (name, m, n, k, tm, tn, tk)
    nk = k // tk
    if mode == "nn":
        a_spec = pl.BlockSpec((tm, tk), lambda i, j, kk: (i, kk))
        b_spec = pl.BlockSpec((tk, tn), lambda i, j, kk: (kk, j))
        dot = _dot
    elif mode == "nt":
        a_spec = pl.BlockSpec((tm, tk), lambda i, j, kk: (i, kk))
        b_spec = pl.BlockSpec((tn, tk), lambda i, j, kk: (j, kk))
        dot = _dot_nt
    else:
        a_spec = pl.BlockSpec((tk, tm), lambda i, j, kk: (kk, i))
        b_spec = pl.BlockSpec((tk, tn), lambda i, j, kk: (kk, j))
        dot = _dot_tn

    def body(a_ref, b_ref, o_ref, *acc):
        if nk == 1:
            o_ref[...] = dot(a_ref[...], b_ref[...]).astype(out_dtype)
            return
        acc_ref, kk = acc[0], pl.program_id(2)

        @pl.when(kk == 0)
        def _():
            acc_ref[...] = dot(a_ref[...], b_ref[...])

        @pl.when((kk > 0) & (kk < nk - 1))
        def _():
            acc_ref[...] += dot(a_ref[...], b_ref[...])

        @pl.when(kk == nk - 1)
        def _():
            o_ref[...] = (acc_ref[...] + dot(a_ref[...], b_ref[...])).astype(out_dtype)

    if stack_out:
        o_spec, o_shape = pl.BlockSpec((None, tm, tn), lambda i, j, kk: (j, i, 0)), SDS((n // tn, m, tn), out_dtype)
    else:
        o_spec, o_shape = pl.BlockSpec((tm, tn), lambda i, j, kk: (i, j)), SDS((m, n), out_dtype)
    return _pc(body, name, (m // tm, n // tn, nk), [a_spec, b_spec], o_spec, o_shape,
               scratch=[pltpu.VMEM((tm, tn), F32)] if nk > 1 else [],
               sem=("parallel", "parallel", "arbitrary"))(a, b)


PROJ_TN = 1152


def _proj(h0_bf, w_cat, tm):
    t = h0_bf.shape[0]

    def body(h_ref, w_ref, o_ref):
        h = h_ref[...]
        for c0 in range(0, CAT_WIDTH, PROJ_TN):
            o_ref[:, c0:c0 + PROJ_TN] = _dot(h, w_ref[:, c0:c0 + PROJ_TN])

    return _pc(body, "proj", (t // tm,), [_row(tm, D_MODEL), _const((D_MODEL, CAT_WIDTH))], _row(tm, CAT_WIDTH),
               SDS((t, CAT_WIDTH), F32), sem=("parallel",))(h0_bf, w_cat)


def _ln_stats(x):
    mu = jnp.mean(x, axis=-1, keepdims=True)
    xc = x - mu
    var = jnp.mean(xc * xc, axis=-1, keepdims=True)
    rstd = lax.rsqrt(var + LN_EPS)
    return xc * rstd, rstd


def _ln_bwd(dy, xhat, rstd, g):
    dxh = dy * g
    m1 = jnp.mean(dxh, axis=-1, keepdims=True)
    m2 = jnp.mean(dxh * xhat, axis=-1, keepdims=True)
    return rstd * (dxh - m1 - xhat * m2)


def _ln_in(x, g, b, tm, after):
    t, d = x.shape

    def body(x_ref, g_ref, b_ref, after_ref, h_ref, hb_ref):
        xhat, _ = _ln_stats(x_ref[...])
        h = xhat * g_ref[...] + b_ref[...]
        h_ref[...] = h
        hb_ref[...] = _mx(h)

    return _pc(body, "ln_in", (t // tm,), [_row(tm, d), _const((1, d)), _const((1, d)), ANY],
               [_row(tm, d), _row(tm, d)], [SDS((t, d), F32), SDS((t, d), MXU_DTYPE)],
               sem=("parallel",))(x, g, b, after)


def _pool_fwd(proj, pool_w, tm):
    t = proj.shape[0]
    ublk = K_U // POOL_WIDTH

    def body(u_ref, halo_ref, pw_ref, ypre_ref, d_ref, ext_ref):
        i = pl.program_id(0)
        ext_ref[0:16, :] = jnp.where(i > 0, halo_ref[...], 0.0)
        ext_ref[16:16 + tm, :] = u_ref[...]
        tok = i * tm + lax.broadcasted_iota(jnp.int32, (tm, POOL_GROUP), 0)
        for gi, w in enumerate(POOL_WINDOWS):
            cs = pl.ds(gi * POOL_GROUP, POOL_GROUP)
            ug = ext_ref[pl.ds(16, tm), cs]
            s = ug
            for k in range(1, w):
                s = s + ext_ref[pl.ds(16 - k, tm), cs]
            cnt = jnp.minimum(tok + 1, w).astype(F32)
            db = _mx(s / cnt - ug)
            d_ref[:, gi * POOL_GROUP:(gi + 1) * POOL_GROUP] = db
            ypre_ref[:, gi * POOL_OUT_GROUP:(gi + 1) * POOL_OUT_GROUP] = _dot(db, pw_ref[gi])

    halo = pl.BlockSpec((16, POOL_WIDTH), lambda i: (jnp.maximum(i * (tm // 16) - 1, 0), ublk))
    return _pc(body, "pool_fwd", (t // tm,),
               [pl.BlockSpec((tm, POOL_WIDTH), lambda i: (i, ublk)), halo, _const((4, POOL_GROUP, POOL_OUT_GROUP))],
               [_row(tm, D_MODEL), _row(tm, POOL_WIDTH)],
               [SDS((t, D_MODEL), F32), SDS((t, POOL_WIDTH), MXU_DTYPE)],
               scratch=[pltpu.VMEM((16 + tm, POOL_WIDTH), F32)], sem=("parallel",))(proj, proj, pool_w)


def _pool_bwd(dyp, d_bf, pool_w, tm):
    t = dyp.shape[0]
    n = t // tm

    def body(dy_ref, dyn_ref, d_ref, pw_ref, du_ref, dpw_ref, ext_ref):
        i = pl.program_id(0)

        @pl.when(i == 0)
        def _():
            dpw_ref[...] = jnp.zeros_like(dpw_ref)

        tok = i * tm + lax.broadcasted_iota(jnp.int32, (tm + 16, POOL_GROUP), 0)
        for gi, w in enumerate(POOL_WINDOWS):
            dy = dy_ref[:, gi * POOL_OUT_GROUP:(gi + 1) * POOL_OUT_GROUP]
            dyn = dyn_ref[:, gi * POOL_OUT_GROUP:(gi + 1) * POOL_OUT_GROUP]
            pw = pw_ref[gi]
            dd = _dot_nt(dy, pw)
            ddn = jnp.where(i < n - 1, _dot_nt(dyn, pw), 0.0)
            cnt = jnp.minimum(tok + 1, w).astype(F32)
            ext_ref[0:tm, :] = dd / cnt[0:tm]
            ext_ref[tm:tm + 16, :] = ddn / cnt[tm:tm + 16]
            s = ext_ref[pl.ds(0, tm), :]
            for k in range(1, w):
                s = s + ext_ref[pl.ds(k, tm), :]
            du_ref[:, gi * POOL_GROUP:(gi + 1) * POOL_GROUP] = _mx(s - dd)
            dpw_ref[gi] += _dot_tn(d_ref[:, gi * POOL_GROUP:(gi + 1) * POOL_GROUP], dy)

    nxt = pl.BlockSpec((16, D_MODEL), lambda i: (jnp.minimum((i + 1) * (tm // 16), t // 16 - 1), 0))
    return _pc(body, "pool_bwd", (n,),
               [_row(tm, D_MODEL), nxt, _row(tm, POOL_WIDTH), _const((4, POOL_GROUP, POOL_OUT_GROUP))],
               [_row(tm, POOL_WIDTH), _const((4, POOL_GROUP, POOL_OUT_GROUP))],
               [SDS((t, POOL_WIDTH), MXU_DTYPE), SDS((4, POOL_GROUP, POOL_OUT_GROUP), F32)],
               scratch=[pltpu.VMEM((tm + 16, POOL_GROUP), F32)], sem=("arbitrary",))(dyp, dyp, d_bf, pool_w)


CONV_BLK = 512


CONV_ROWS = 32


def _conv_rows(ext_ref, w, r, rows):
    y = w[0] * ext_ref[pl.ds(r + 5, rows), :]
    for k in range(1, CONV_K):
        y = y + w[k] * ext_ref[pl.ds(r + 5 + k, rows), :]
    return y


def _conv_fwd(proj, conv_w, tm):
    t = proj.shape[0]

    def body(x_ref, halo_ref, w_ref, o_ref, ext_ref):
        i = pl.program_id(0)
        ext_ref[0:8, :] = jnp.where(i > 0, halo_ref[...], 0.0)
        ext_ref[8:8 + tm, :] = x_ref[...]
        w = [w_ref[pl.ds(k, 1), :] for k in range(CONV_K)]
        for r in range(0, tm, CONV_ROWS):
            y = _conv_rows(ext_ref, w, r, CONV_ROWS)
            o_ref[pl.ds(r, CONV_ROWS), :] = y * _sigmoid(y)

    halo = pl.BlockSpec((8, CONV_BLK), lambda i, j: (jnp.maximum(i * (tm // 8) - 1, 0), j))
    blk = pl.BlockSpec((tm, CONV_BLK), lambda i, j: (i, j))
    return _pc(body, "conv_fwd", (t // tm, QKV_WIDTH // CONV_BLK),
               [blk, halo, pl.BlockSpec((CONV_K, CONV_BLK), lambda i, j: (0, j))], blk,
               SDS((t, QKV_WIDTH), F32), scratch=[pltpu.VMEM((8 + tm, CONV_BLK), F32)],
               sem=("parallel", "parallel"))(proj, proj, conv_w)


def _conv_bwd(dact, proj, conv_w, tm):
    t = proj.shape[0]
    n = t // tm

    def body(da_ref, dan_ref, x_ref, xp_ref, xn_ref, w_ref, dx_ref, dw_ref, ext_ref, dy_ref):
        i = pl.program_id(1)

        @pl.when(i == 0)
        def _():
            dw_ref[...] = jnp.zeros_like(dw_ref)

        ext_ref[0:8, :] = jnp.where(i > 0, xp_ref[...], 0.0)
        ext_ref[8:8 + tm, :] = x_ref[...]
        ext_ref[8 + tm:16 + tm, :] = jnp.where(i < n - 1, xn_ref[...], 0.0)
        w = [w_ref[pl.ds(k, 1), :] for k in range(CONV_K)]

        def dsilu_of(y):
            s = _sigmoid(y)
            return s * (1.0 + y * (1.0 - s))

        acc = [jnp.zeros((8, CONV_BLK), F32) for _ in range(CONV_K)]
        for r in range(0, tm, CONV_ROWS):
            win = [ext_ref[pl.ds(r + 5 + k, CONV_ROWS), :] for k in range(CONV_K)]
            y = w[0] * win[0]
            for k in range(1, CONV_K):
                y = y + w[k] * win[k]
            dy = da_ref[pl.ds(r, CONV_ROWS), :] * dsilu_of(y)
            dy_ref[pl.ds(r, CONV_ROWS), :] = dy
            for k in range(CONV_K):
                prod = dy * win[k]
                for q in range(0, CONV_ROWS, 8):
                    acc[k] = acc[k] + prod[q:q + 8]
        dy_ref[tm:tm + 8, :] = jnp.where(i < n - 1, dan_ref[...], 0.0) * dsilu_of(_conv_rows(ext_ref, w, tm, 8))
        for k in range(CONV_K):
            dw_ref[pl.ds(k, 1), :] += jnp.sum(acc[k], axis=0, keepdims=True)
        for r in range(0, tm, CONV_ROWS):
            dx = w[0] * dy_ref[pl.ds(r + 3, CONV_ROWS), :]
            for k in range(1, CONV_K):
                dx = dx + w[k] * dy_ref[pl.ds(r + 3 - k, CONV_ROWS), :]
            dx_ref[pl.ds(r, CONV_ROWS), :] = _mx(dx)

    blk = pl.BlockSpec((tm, CONV_BLK), lambda j, i: (i, j))
    prev = pl.BlockSpec((8, CONV_BLK), lambda j, i: (jnp.maximum(i * (tm // 8) - 1, 0), j))
    nxt = pl.BlockSpec((8, CONV_BLK), lambda j, i: (jnp.minimum((i + 1) * (tm // 8), t // 8 - 1), j))
    wspec = pl.BlockSpec((CONV_K, CONV_BLK), lambda j, i: (0, j))
    return _pc(body, "conv_bwd", (QKV_WIDTH // CONV_BLK, n),
               [blk, nxt, blk, prev, nxt, wspec],
               [blk, pl.BlockSpec((8, CONV_BLK), lambda j, i: (0, j))],
               [SDS((t, QKV_WIDTH), MXU_DTYPE), SDS((8, QKV_WIDTH), F32)],
               scratch=[pltpu.VMEM((16 + tm, CONV_BLK), F32), pltpu.VMEM((8 + tm, CONV_BLK), F32)],
               sem=("parallel", "arbitrary"))(dact, dact, proj, proj, proj, conv_w)


def _lane(shape):
    return lax.broadcasted_iota(jnp.int32, shape, 1)


def _ba_fwd(proj, al_row, dtb_row, tm):
    t = proj.shape[0]
    bablk = K_BA // 128

    def body(ba_ref, al_ref, dtb_ref, bg_ref):
        ba = ba_ref[...]
        lane = _lane(ba.shape)
        g = -jnp.exp(al_ref[...]) * _softplus(ba + dtb_ref[...])
        bg_ref[...] = jnp.where(lane < HEADS, _sigmoid(ba), jnp.where(lane < 2 * HEADS, g, 0.0))

    return _pc(body, "ba_fwd", (t // tm,),
               [pl.BlockSpec((tm, 128), lambda i: (i, bablk)), _const((1, 128)), _const((1, 128))],
               _row(tm, 128), SDS((t, 128), F32), sem=("parallel",))(proj, al_row, dtb_row)


def _ba_bwd(dbg, bg, proj, al_row, dtb_row, tm):
    t = proj.shape[0]
    bablk = K_BA // 128

    def body(dbg_ref, bg_ref, ba_ref, al_ref, dtb_ref, dba_ref, acc_ref):
        i = pl.program_id(0)

        @pl.when(i == 0)
        def _():
            acc_ref[...] = jnp.zeros_like(acc_ref)

        dbg_v, bg_v, ba = dbg_ref[...], bg_ref[...], ba_ref[...]
        lane = _lane(ba.shape)
        is_g = (lane >= HEADS) & (lane < 2 * HEADS)
        dbeta_raw = dbg_v * bg_v * (1.0 - bg_v)
        da_raw = dbg_v * (-jnp.exp(al_ref[...])) * _sigmoid(ba + dtb_ref[...])
        dba_ref[...] = _mx(jnp.where(lane < HEADS, dbeta_raw, jnp.where(is_g, da_raw, 0.0)))
        acc_ref[0:1, :] += jnp.sum(jnp.where(is_g, dbg_v * bg_v, 0.0), axis=0, keepdims=True)
        acc_ref[1:2, :] += jnp.sum(jnp.where(is_g, da_raw, 0.0), axis=0, keepdims=True)

    return _pc(body, "ba_bwd", (t // tm,),
               [_row(tm, 128), _row(tm, 128), pl.BlockSpec((tm, 128), lambda i: (i, bablk)),
                _const((1, 128)), _const((1, 128))],
               [_row(tm, 128), _const((8, 128))], [SDS((t, 128), MXU_DTYPE), SDS((8, 128), F32)],
               sem=("arbitrary",))(dbg, bg, proj, al_row, dtb_row)


def _each(f, *lists):
    return [f(*a) for a in zip(*lists)]


def _rowsum(a):
    return jnp.sum(a, axis=1, keepdims=True)


def _chunk_terms(qs, ks, bgv, g_rows, hs):
    c = CHUNK
    ii = lax.broadcasted_iota(jnp.int32, (c, c), 0)
    jj = lax.broadcasted_iota(jnp.int32, (c, c), 1)
    lane = _lane(bgv.shape)
    incl = ii >= jj
    beta = [_rowsum(jnp.where(lane == h, bgv, 0.0)) for h in hs]
    g_col = [_rowsum(jnp.where(lane == HEADS + h, bgv, 0.0)) for h in hs]
    rq = _each(lambda q: lax.rsqrt(_rowsum(q * q) + L2_EPS), qs)
    rk = _each(lambda k: lax.rsqrt(_rowsum(k * k) + L2_EPS), ks)
    yq = _each(jnp.multiply, qs, rq)
    kn = _each(jnp.multiply, ks, rk)
    qn = _each(lambda a: a * Q_SCALE, yq)
    gc_col = _each(lambda g: _rowsum(jnp.where(jj <= ii, g, 0.0)), g_rows)
    gc_row = _each(lambda g: jnp.sum(jnp.where(ii <= jj, g, 0.0), axis=0, keepdims=True), g_col)
    dm = _each(lambda a, b: jnp.where(incl, jnp.exp(jnp.where(incl, a - b, 0.0)), 0.0), gc_col, gc_row)
    gl = _each(_rowsum, g_rows)
    eg = _each(jnp.exp, gc_col)
    ek = _each(lambda a, b: jnp.exp(a - b), gl, gc_col)
    egl = _each(jnp.exp, gl)
    kb = _each(jnp.multiply, kn, beta)
    kk = _each(_dot_nt, kb, kn)
    qk = _each(_dot_nt, qn, kn)
    m = _each(lambda a, b: jnp.where(ii > jj, a * b, 0.0), kk, dm)
    attn = _each(jnp.multiply, qk, dm)
    return dict(ii=ii, jj=jj, beta=beta, rq=rq, rk=rk, yq=yq, kn=kn, qn=qn, dm=dm, eg=eg, ek=ek,
                egl=egl, kb=kb, m=m, attn=attn)


def _unit_lower_inverse_minus_identity(ms, ii, jj):
    pair = (ii >> 1) == (jj >> 1)
    ys = _each(lambda m: -jnp.where(pair, m, 0.0), ms)
    s = 1
    while (1 << s) < CHUNK:
        mask = ((ii >> (s + 1)) == (jj >> (s + 1))) & ((ii >> s) != (jj >> s))
        lbs = _each(lambda m: jnp.where(mask, m, 0.0), ms)
        zs = _each(lambda y, lb: lb + _dot(y, lb), ys, lbs)
        ys = _each(lambda y, z: y - z - _dot(z, y), ys, zs)
        s += 1
    return ys


def _head_offsets(group):
    hs = [group * HEAD_GROUP + a for a in range(HEAD_GROUP)]
    return hs, [pl.ds(pl.multiple_of(base + h * HEAD_DIM, HEAD_DIM), HEAD_DIM)
                for base in (0, DN_WIDTH, 2 * DN_WIDTH) for h in hs]


def _dn_local_fwd(qkv_act, bg, bgt):
    t = qkv_act.shape[0]
    nt = t // CHUNK
    c = CHUNK

    def body(qkv_ref, bg_ref, bgt_ref, u_ref, w_ref, qg_ref, kg_ref, attn_ref, y_ref, egl_ref):
        bgv = bg_ref[...]

        def group(gi, carry):
            hs, offs = _head_offsets(gi)
            qo, ko, vo = offs[0:HEAD_GROUP], offs[HEAD_GROUP:2 * HEAD_GROUP], offs[2 * HEAD_GROUP:]
            qs = [qkv_ref[:, o] for o in qo]
            ks = [qkv_ref[:, o] for o in ko]
            vs = [qkv_ref[:, o] for o in vo]
            g_rows = [bgt_ref[pl.ds(HEADS + h, 1), :] for h in hs]
            ct = _chunk_terms(qs, ks, bgv, g_rows, hs)
            ys = _unit_lower_inverse_minus_identity(ct["m"], ct["ii"], ct["jj"])
            vb = _each(jnp.multiply, vs, ct["beta"])
            kbe = _each(jnp.multiply, ct["kb"], ct["eg"])
            us = _each(lambda a, y: a + _dot(y, a), vb, ys)
            ws = _each(lambda a, y: a + _dot(y, a), kbe, ys)
            for a in range(HEAD_GROUP):
                dst = qo[a]
                u_ref[:, dst] = us[a]
                w_ref[:, dst] = _mx(ws[a])
                qg_ref[:, dst] = _mx(ct["qn"][a] * ct["eg"][a])
                kg_ref[:, dst] = _mx(ct["kn"][a] * ct["ek"][a])
                attn_ref[:, dst] = _mx(ct["attn"][a])
                y_ref[:, dst] = _mx(ys[a])
                egl_ref[0, pl.ds(hs[a], 1), :] = jnp.broadcast_to(ct["egl"][a], (1, HEAD_DIM))
            return carry

        lax.fori_loop(0, HEADS // HEAD_GROUP, group, 0)

    wide = _row(c, DN_WIDTH)
    return _pc(body, "dn_local_fwd", (nt,),
               [_row(c, QKV_WIDTH), _row(c, 128), pl.BlockSpec((2 * HEADS, c), lambda i: (0, i))],
               [wide, wide, wide, wide, wide, wide, pl.BlockSpec((1, HEADS, HEAD_DIM), lambda i: (i, 0, 0))],
               [SDS((t, DN_WIDTH), F32)] + [SDS((t, DN_WIDTH), MXU_DTYPE)] * 5 + [SDS((nt, HEADS, HEAD_DIM), F32)],
               sem=("parallel",))(qkv_act, bg, bgt)


def _dn_scan_fwd(u, w, qg, kg, attn, egl):
    t = u.shape[0]
    nt = t // CHUNK
    c = CHUNK
    sls = [slice(h * HEAD_DIM, (h + 1) * HEAD_DIM) for h in range(HEADS)]

    def body(u_ref, w_ref, qg_ref, kg_ref, attn_ref, egl_ref, o_ref, vn_ref, st_ref, s_ref):
        @pl.when(pl.program_id(0) == 0)
        def _():
            s_ref[...] = jnp.zeros_like(s_ref)

        ss = [s_ref[h] for h in range(HEADS)]
        sb = _each(_mx, ss)
        vn = [u_ref[:, sl] - _dot(w_ref[:, sl], b) for sl, b in zip(sls, sb)]
        vnb = _each(_mx, vn)
        oa = [_dot(qg_ref[:, sl], b) for sl, b in zip(sls, sb)]
        ob = [_dot(attn_ref[:, sl], b) for sl, b in zip(sls, vnb)]
        upd = [_dot_tn(kg_ref[:, sl], b) for sl, b in zip(sls, vnb)]
        for h, sl in enumerate(sls):
            st_ref[0, h] = ss[h]
            vn_ref[:, sl] = vnb[h]
            o_ref[:, sl] = oa[h] + ob[h]
            s_ref[h] = ss[h] * egl_ref[0, h:h + 1, :] + upd[h]

    wide = _row(c, DN_WIDTH)
    return _pc(body, "dn_scan_fwd", (nt,),
               [wide] * 5 + [pl.BlockSpec((1, HEADS, HEAD_DIM), lambda i: (i, 0, 0))],
               [wide, wide, pl.BlockSpec((1, HEADS, HEAD_DIM, HEAD_DIM), lambda i: (i, 0, 0, 0))],
               [SDS((t, DN_WIDTH), F32), SDS((t, DN_WIDTH), MXU_DTYPE), SDS((nt, HEADS, HEAD_DIM, HEAD_DIM), F32)],
               scratch=[pltpu.VMEM((HEADS, HEAD_DIM, HEAD_DIM), F32)], sem=("arbitrary",))(u, w, qg, kg, attn, egl)


def _dn_scan_bwd(do, qg, kg, w, attn, vn, states, egl):
    t = do.shape[0]
    nt = t // CHUNK
    c = CHUNK
    sls = [slice(h * HEAD_DIM, (h + 1) * HEAD_DIM) for h in range(HEADS)]

    def body(do_ref, qg_ref, kg_ref, w_ref, attn_ref, vn_ref, st_ref, egl_ref,
             dvn_ref, dkg_ref, dqg_ref, dattn_ref, dw_ref, degl_ref, ds_ref):
        @pl.when(pl.program_id(0) == 0)
        def _():
            ds_ref[...] = jnp.zeros_like(ds_ref)

        dsp = [ds_ref[h] for h in range(HEADS)]
        dsb = _each(_mx, dsp)
        ss = [st_ref[0, h] for h in range(HEADS)]
        sb = _each(_mx, ss)
        dvn = [_dot(kg_ref[:, sl], b) + _dot_tn(attn_ref[:, sl], do_ref[:, sl]) for sl, b in zip(sls, dsb)]
        dvnb = _each(_mx, dvn)
        dkg = [_dot_nt(vn_ref[:, sl], b) for sl, b in zip(sls, dsb)]
        dqg = [_dot_nt(do_ref[:, sl], b) for sl, b in zip(sls, sb)]
        dattn = [_dot_nt(do_ref[:, sl], vn_ref[:, sl]) for sl in sls]
        dwv = [-_dot_nt(a, b) for a, b in zip(dvnb, sb)]
        upd = [_dot_tn(qg_ref[:, sl], do_ref[:, sl]) - _dot_tn(w_ref[:, sl], a) for sl, a in zip(sls, dvnb)]
        for h, sl in enumerate(sls):
            dvn_ref[:, sl] = dvn[h]
            dkg_ref[:, sl] = dkg[h]
            dqg_ref[:, sl] = dqg[h]
            dattn_ref[:, sl] = dattn[h]
            dw_ref[:, sl] = dwv[h]
            degl = jnp.sum(_rowsum(ss[h] * dsp[h]), axis=0, keepdims=True)
            degl_ref[0, h:h + 1, :] = jnp.broadcast_to(degl, (1, HEAD_DIM))
            ds_ref[h] = dsp[h] * egl_ref[0, h:h + 1, :] + upd[h]

    rev = pl.BlockSpec((c, DN_WIDTH), lambda i: (nt - 1 - i, 0))
    rev3 = pl.BlockSpec((1, HEADS, HEAD_DIM), lambda i: (nt - 1 - i, 0, 0))
    rev4 = pl.BlockSpec((1, HEADS, HEAD_DIM, HEAD_DIM), lambda i: (nt - 1 - i, 0, 0, 0))
    return _pc(body, "dn_scan_bwd", (nt,), [rev] * 6 + [rev4, rev3], [rev] * 5 + [rev3],
               [SDS((t, DN_WIDTH), F32)] * 5 + [SDS((nt, HEADS, HEAD_DIM), F32)],
               scratch=[pltpu.VMEM((HEADS, HEAD_DIM, HEAD_DIM), F32)],
               sem=("arbitrary",))(do, qg, kg, w, attn, vn, states, egl)


def _dn_local_bwd(qkv_act, bg, bgt, u, w, ymat, dvn, dw, dqg, dkg, dattn, degl):
    t = qkv_act.shape[0]
    nt = t // CHUNK
    c = CHUNK

    def body(qkv_ref, bg_ref, bgt_ref, u_ref, w_ref, y_ref, du_ref, dw_ref, dqg_ref, dkg_ref, dattn_ref,
             degl_ref, dqkv_ref, dbg_ref):
        bgv = bg_ref[...]
        lane = _lane(bgv.shape)
        rowi = lax.broadcasted_iota(jnp.int32, (c, 1), 0)

        def group(gi, dbg):
            hs, offs = _head_offsets(gi)
            qo, ko, vo = offs[0:HEAD_GROUP], offs[HEAD_GROUP:2 * HEAD_GROUP], offs[2 * HEAD_GROUP:]
            qs = [qkv_ref[:, o] for o in qo]
            ks = [qkv_ref[:, o] for o in ko]
            vs = [qkv_ref[:, o] for o in vo]
            g_rows = [bgt_ref[pl.ds(HEADS + h, 1), :] for h in hs]
            ct = _chunk_terms(qs, ks, bgv, g_rows, hs)
            ii, jj = ct["ii"], ct["jj"]
            beta, eg, ek, kb, kn, qn, dm = ct["beta"], ct["eg"], ct["ek"], ct["kb"], ct["kn"], ct["qn"], ct["dm"]
            ys = [y_ref[:, o] for o in qo]
            du = [du_ref[:, o] for o in qo]
            dwv = [dw_ref[:, o] for o in qo]
            dqg_v = [dqg_ref[:, o] for o in qo]
            dkg_v = [dkg_ref[:, o] for o in qo]
            dattn_v = [dattn_ref[:, o] for o in qo]
            degl_v = [jnp.max(degl_ref[0, pl.ds(h, 1), :], axis=1, keepdims=True) for h in hs]
            dvb = _each(lambda a, y: a + _dot_tn(y, a), du, ys)
            dkbe = _each(lambda a, y: a + _dot_tn(y, a), dwv, ys)
            dm_u = [_dot_nt(a, u_ref[:, o]) for a, o in zip(dvb, qo)]
            dm_w = [_dot_nt(a, w_ref[:, o]) for a, o in zip(dkbe, qo)]
            dms = _each(lambda a, b: jnp.where(ii > jj, -(a + b), 0.0), dm_u, dm_w)
            dkk = _each(jnp.multiply, dms, dm)
            dqk = _each(jnp.multiply, dattn_v, dm)
            gmat = _each(lambda a, b, c_, d: a * b + c_ * d, dms, ct["m"], dattn_v, ct["attn"])
            dkb = _each(lambda a, b, c_, d: _dot(a, b) + c_ * d, dkk, kn, dkbe, eg)
            dk1 = _each(_dot_tn, dkk, kb)
            dk2 = _each(_dot_tn, dqk, qn)
            dq1 = _each(_dot, dqk, kn)
            dk = _each(lambda a, b, c_, d: a + b + c_ * d, dk1, dk2, dkg_v, ek)
            dq = _each(lambda a, b, c_: a + b * c_, dq1, dqg_v, eg)
            deg = _each(lambda a, b, c_, d: _rowsum(a * b) + _rowsum(c_ * d), dqg_v, qn, dkbe, kb)
            dek = _each(lambda a, b: _rowsum(a * b), dkg_v, kn)
            dgl = _each(lambda a, b, c_, d: jnp.sum(a * b, axis=0, keepdims=True) + c_ * d, dek, ek, degl_v, ct["egl"])
            cs_row = _each(lambda g: jnp.sum(g, axis=0, keepdims=True), gmat)
            cs_col = _each(lambda r: _rowsum(jnp.where(ii == jj, r, 0.0)), cs_row)
            dgc = _each(lambda a, b, c_, d, g, e, f: a * b - c_ * d + _rowsum(g) - e + jnp.where(rowi == c - 1, f, 0.0),
                        deg, eg, dek, ek, gmat, cs_col, dgl)
            dgc_row = _each(lambda a: jnp.sum(jnp.where(ii == jj, a, 0.0), axis=0, keepdims=True), dgc)
            dg = _each(lambda r: _rowsum(jnp.where(jj >= ii, r, 0.0)), dgc_row)
            dbeta = _each(lambda a, b, c_, d: _rowsum(a * b) + _rowsum(c_ * d), dkb, kn, dvb, vs)
            dk = _each(lambda a, b, c_: a + b * c_, dk, dkb, beta)
            for a in range(HEAD_GROUP):
                dyq = dq[a] * Q_SCALE
                yq = ct["yq"][a]
                dqkv_ref[:, qo[a]] = ct["rq"][a] * (dyq - yq * _rowsum(yq * dyq))
                dqkv_ref[:, ko[a]] = ct["rk"][a] * (dk[a] - kn[a] * _rowsum(kn[a] * dk[a]))
                dqkv_ref[:, vo[a]] = dvb[a] * beta[a]
                dbg = dbg + jnp.where(lane == hs[a], dbeta[a], 0.0) + jnp.where(lane == HEADS + hs[a], dg[a], 0.0)
            return dbg

        dbg_ref[...] = lax.fori_loop(0, HEADS // HEAD_GROUP, group, jnp.zeros((c, 128), F32))

    wide = _row(c, DN_WIDTH)
    sc3 = pl.BlockSpec((1, HEADS, HEAD_DIM), lambda i: (i, 0, 0))
    return _pc(body, "dn_local_bwd", (nt,),
               [_row(c, QKV_WIDTH), _row(c, 128), pl.BlockSpec((2 * HEADS, c), lambda i: (0, i))] + [wide] * 8 + [sc3],
               [_row(c, QKV_WIDTH), _row(c, 128)], [SDS((t, QKV_WIDTH), F32), SDS((t, 128), F32)],
               sem=("parallel",))(qkv_act, bg, bgt, u, w, ymat, dvn, dw, dqg, dkg, dattn, degl)


MIX_ROWS = 64


def _mix_fwd(o, proj, ypre, pool_scale, wo_row, tm):
    t = o.shape[0]

    def body(o_ref, z_ref, ga_ref, gb_ref, yp_ref, ps_ref, wo_ref, mixed_ref):
        for r in range(0, tm, MIX_ROWS):
            rows = pl.ds(r, MIX_ROWS)
            for h in range(HEADS):
                sl = slice(h * HEAD_DIM, (h + 1) * HEAD_DIM)
                oh = o_ref[rows, sl]
                on = oh * lax.rsqrt(jnp.mean(oh * oh, axis=1, keepdims=True) + RMS_EPS)
                zh = z_ref[rows, sl]
                yb = on * wo_ref[:, sl] * (zh * _sigmoid(zh))
                ya = yp_ref[rows, sl] * ps_ref[:, sl]
                mixed_ref[rows, sl] = _mx(_sigmoid(ga_ref[rows, sl]) * ya + _sigmoid(gb_ref[rows, sl]) * yb)

    def col(blk):
        return pl.BlockSpec((tm, D_MODEL), lambda i: (i, blk))

    return _pc(body, "mix_fwd", (t // tm,),
               [_row(tm, D_MODEL), col(K_Z // D_MODEL), col(K_GA // D_MODEL), col(K_GB // D_MODEL), _row(tm, D_MODEL),
                _const((1, D_MODEL)), _const((1, D_MODEL))],
               _row(tm, D_MODEL), SDS((t, D_MODEL), MXU_DTYPE), sem=("parallel",))(
                   o, proj, proj, proj, ypre, pool_scale, wo_row)


def _mix_bwd(da1_bf, w_out, o, proj, ypre, pool_scale, wo_row, tm, after):
    t = o.shape[0]

    def body(da_ref, wout_ref, o_ref, z_ref, ga_ref, gb_ref, yp_ref, ps_ref, wo_ref, after_ref,
             do_ref, dz_ref, dga_ref, dgb_ref, dyp_ref, acc_ref, dm_ref):
        i = pl.program_id(0)

        @pl.when(i == 0)
        def _():
            acc_ref[...] = jnp.zeros_like(acc_ref)

        dm_ref[...] = _dot_nt(da_ref[...], wout_ref[...])
        dwo = jnp.zeros((1, HEAD_DIM), F32)
        for h in range(HEADS):
            sl = slice(h * HEAD_DIM, (h + 1) * HEAD_DIM)
            woh = wo_ref[:, sl]
            psh = ps_ref[:, sl]
            dps = jnp.zeros((1, HEAD_DIM), F32)
            for r in range(0, tm, MIX_ROWS):
                rows = pl.ds(r, MIX_ROWS)
                oh = o_ref[rows, sl]
                rs = lax.rsqrt(jnp.mean(oh * oh, axis=1, keepdims=True) + RMS_EPS)
                on = oh * rs
                zh = z_ref[rows, sl]
                sz = _sigmoid(zh)
                silu = zh * sz
                t1 = on * woh
                yb = t1 * silu
                sa = _sigmoid(ga_ref[rows, sl])
                sb = _sigmoid(gb_ref[rows, sl])
                yp = yp_ref[rows, sl]
                dm = dm_ref[rows, sl]
                dga_ref[rows, sl] = _mx(dm * (yp * psh) * sa * (1.0 - sa))
                dgb_ref[rows, sl] = _mx(dm * yb * sb * (1.0 - sb))
                dya = dm * sa
                dyb = dm * sb
                dyp_ref[rows, sl] = _mx(dya * psh)
                dps = dps + jnp.sum(dya * yp, axis=0, keepdims=True)
                dz_ref[rows, sl] = _mx(dyb * t1 * (sz * (1.0 + zh * (1.0 - sz))))
                dt1 = dyb * silu
                dwo = dwo + jnp.sum(dt1 * on, axis=0, keepdims=True)
                don = dt1 * woh
                do_ref[rows, sl] = _mx(rs * (don - on * jnp.mean(don * on, axis=1, keepdims=True)))
            acc_ref[0:1, sl] += dps
        acc_ref[1:2, 0:HEAD_DIM] += dwo

    def col(blk):
        return pl.BlockSpec((tm, D_MODEL), lambda i: (i, blk))

    r = _row(tm, D_MODEL)
    return _pc(body, "mix_bwd", (t // tm,),
               [r, _const((D_MODEL, D_MODEL)), r, col(K_Z // D_MODEL), col(K_GA // D_MODEL), col(K_GB // D_MODEL), r,
                _const((1, D_MODEL)), _const((1, D_MODEL)), ANY],
               [r, r, r, r, r, _const((8, D_MODEL))],
               [SDS((t, D_MODEL), MXU_DTYPE)] * 5 + [SDS((8, D_MODEL), F32)],
               scratch=[pltpu.VMEM((tm, D_MODEL), F32)],
               sem=("arbitrary",))(da1_bf, w_out, o, proj, proj, proj, ypre, pool_scale, wo_row, after)


def _oproj_ln1(mixed, w_out, h0, g1, b1, tm):
    t = mixed.shape[0]

    def body(m_ref, w_ref, h0_ref, g_ref, b_ref, a1_ref, h1_ref, h1b_ref):
        a1 = ALPHA * h0_ref[...] + _dot(m_ref[...], w_ref[...])
        a1_ref[...] = a1
        xhat, _ = _ln_stats(a1)
        h1 = xhat * g_ref[...] + b_ref[...]
        h1_ref[...] = h1
        h1b_ref[...] = _mx(h1)

    r = _row(tm, D_MODEL)
    v = _const((1, D_MODEL))
    return _pc(body, "oproj_ln1", (t // tm,), [r, _const((D_MODEL, D_MODEL)), r, v, v], [r, r, r],
               [SDS((t, D_MODEL), F32), SDS((t, D_MODEL), F32), SDS((t, D_MODEL), MXU_DTYPE)],
               sem=("parallel",))(mixed, w_out, h0, g1, b1)


def _mlp_up(h1_bf, w_up, tm):
    t = h1_bf.shape[0]
    tn = w_up.shape[2]

    def body(h_ref, w_ref, act_ref):
        r = jnp.maximum(_dot(h_ref[...], w_ref[...]), 0.0)
        act_ref[...] = _mx(r * r)

    return _pc(body, "mlp_up", (D_FF // tn, t // tm),
               [pl.BlockSpec((tm, D_MODEL), lambda j, i: (i, 0)),
                pl.BlockSpec((None, D_MODEL, tn), lambda j, i: (j, 0, 0))],
               pl.BlockSpec((tm, tn), lambda j, i: (i, j)), SDS((t, D_FF), MXU_DTYPE),
               sem=("parallel", "parallel"))(h1_bf, w_up)


def _tail(act, w_down, h1, w_gate, p_bf, w_proj, tgt, g2, b2, tm):
    t = act.shape[0]

    def body(act_ref, wd_ref, h1_ref, wg_ref, p_ref, wp_ref, tgt_ref, g_ref, b_ref,
             dr_ref, drb_ref, dgp_ref, dpp_ref, rb_ref, acc_ref):
        i = pl.program_id(0)

        @pl.when(i == 0)
        def _():
            acc_ref[...] = jnp.zeros_like(acc_ref)

        r = ALPHA * h1_ref[...] + _dot(act_ref[...], wd_ref[...])
        rb = _mx(r)
        rb_ref[...] = rb
        gate = _sigmoid(_dot(rb, wg_ref[...]))
        pp = _dot(p_ref[...], wp_ref[...])
        xhat, rstd = _ln_stats(r + gate * pp)
        g = g_ref[...]
        diff = xhat * g + b_ref[...] - tgt_ref[...]
        dh2 = diff * (1.0 / D_MODEL)
        rowloss = jnp.sum(diff * diff, axis=1, keepdims=True) * (0.5 / D_MODEL)
        acc_ref[0:1, :] += jnp.sum(dh2 * xhat, axis=0, keepdims=True)
        acc_ref[1:2, :] += jnp.sum(dh2, axis=0, keepdims=True)
        acc_ref[2:3, :] += jnp.broadcast_to(jnp.sum(rowloss, axis=0, keepdims=True), (1, D_MODEL))
        da2 = _ln_bwd(dh2, xhat, rstd, g)
        dpp_ref[...] = _mx(da2 * gate)
        dgp = _mx(da2 * pp * gate * (1.0 - gate))
        dgp_ref[...] = dgp
        dr = da2 + _dot_nt(dgp, wg_ref[...])
        dr_ref[...] = dr
        drb_ref[...] = _mx(dr)

    r = _row(tm, D_MODEL)
    v = _const((1, D_MODEL))
    return _pc(body, "tail", (t // tm,),
               [_row(tm, D_FF), _const((D_FF, D_MODEL)), r, _const((D_MODEL, D_MODEL)), _row(tm, PLE_DIM),
                _const((PLE_DIM, D_MODEL)), r, v, v],
               [r, r, r, r, r, _const((8, D_MODEL))],
               [SDS((t, D_MODEL), F32)] + [SDS((t, D_MODEL), MXU_DTYPE)] * 4 + [SDS((8, D_MODEL), F32)],
               sem=("arbitrary",))(act, w_down, h1, w_gate, p_bf, w_proj, tgt, g2, b2)


def _mlp_bwd1(dr_bf, w_down, act, tm, tn):
    t = act.shape[0]

    def body(dr_ref, w_ref, act_ref, dup_ref):
        dact = _dot_nt(dr_ref[...], w_ref[...])
        dup_ref[...] = _mx(dact * (2.0 * jnp.sqrt(act_ref[...].astype(F32))))

    o = pl.BlockSpec((tm, tn), lambda j, i: (i, j))
    return _pc(body, "mlp_bwd1", (D_FF // tn, t // tm),
               [pl.BlockSpec((tm, D_MODEL), lambda j, i: (i, 0)), pl.BlockSpec((tn, D_MODEL), lambda j, i: (j, 0)), o],
               o, SDS((t, D_FF), MXU_DTYPE), sem=("parallel", "parallel"))(dr_bf, w_down, act)


def _mlp_bwd2(dup, w_up, dr, a1, g1, tm):
    t = dr.shape[0]

    nk, tk = w_up.shape[0], w_up.shape[2]

    def body(dup_ref, w_ref, dr_ref, a1_ref, g_ref, da1_ref, da1b_ref, acc_ref):
        i = pl.program_id(0)

        @pl.when(i == 0)
        def _():
            acc_ref[...] = jnp.zeros_like(acc_ref)

        dh1 = ALPHA * dr_ref[...]
        for kk in range(nk):
            dh1 = dh1 + _dot_nt(dup_ref[:, kk * tk:(kk + 1) * tk], w_ref[kk])
        xhat, rstd = _ln_stats(a1_ref[...])
        acc_ref[0:1, :] += jnp.sum(dh1 * xhat, axis=0, keepdims=True)
        acc_ref[1:2, :] += jnp.sum(dh1, axis=0, keepdims=True)
        da1 = _ln_bwd(dh1, xhat, rstd, g_ref[...])
        da1_ref[...] = da1
        da1b_ref[...] = _mx(da1)

    r = _row(tm, D_MODEL)
    return _pc(body, "mlp_bwd2", (t // tm,),
               [_row(tm, D_FF), _const((nk, D_MODEL, tk)), r, r, _const((1, D_MODEL))],
               [r, r, _const((8, D_MODEL))],
               [SDS((t, D_MODEL), F32), SDS((t, D_MODEL), MXU_DTYPE), SDS((8, D_MODEL), F32)],
               sem=("arbitrary",))(dup, w_up, dr, a1, g1)


def _ln_in_bwd(dproj, w_cat, da1, x, g, tm):
    t = x.shape[0]

    def body(dp_ref, w_ref, da1_ref, x_ref, g_ref, dx_ref, acc_ref):
        i = pl.program_id(0)

        @pl.when(i == 0)
        def _():
            acc_ref[...] = jnp.zeros_like(acc_ref)

        dh0 = _dot_nt(dp_ref[...], w_ref[...]) + ALPHA * da1_ref[...]
        xhat, rstd = _ln_stats(x_ref[...])
        acc_ref[0:1, :] += jnp.sum(dh0 * xhat, axis=0, keepdims=True)
        acc_ref[1:2, :] += jnp.sum(dh0, axis=0, keepdims=True)
        dx_ref[...] = _ln_bwd(dh0, xhat, rstd, g_ref[...])

    r = _row(tm, D_MODEL)
    return _pc(body, "ln_in_bwd", (t // tm,),
               [_row(tm, CAT_WIDTH), _const((D_MODEL, CAT_WIDTH)), r, r, _const((1, D_MODEL))],
               [r, _const((8, D_MODEL))], [SDS((t, D_MODEL), F32), SDS((8, D_MODEL), F32)],
               sem=("arbitrary",))(dproj, w_cat, da1, x, g)


def _local_step(x, p, tgt, wts, start_token, late_weights, send_late_grads):
    t = x.shape[0]
    tm = min(512, t)
    tms = min(256, t)
    row = lambda a: a.reshape(1, -1)
    w_cat = wts["w_cat"]
    pool_scale = row(wts["pool_scale"])
    wo_row = jnp.tile(row(wts["o_norm_w"]), (1, HEADS))
    pad8 = jnp.zeros((1, HEADS), F32)
    al_row = jnp.concatenate([pad8, row(wts["a_log"]), jnp.zeros((1, 128 - 2 * HEADS), F32)], axis=1)
    dtb_row = jnp.concatenate([pad8, row(wts["dt_bias"]), jnp.zeros((1, 128 - 2 * HEADS), F32)], axis=1)
    g_in, b_in = row(wts["ln_in_g"]), row(wts["ln_in_b"])
    g1, b1 = row(wts["ln1_g"]), row(wts["ln1_b"])
    g2, b2 = row(wts["ln2_g"]), row(wts["ln2_b"])

    h0, h0_bf = _ln_in(x, g_in, b_in, tm, start_token)
    proj = _proj(h0_bf, w_cat, tms)
    ypre, d_bf = _pool_fwd(proj, wts["pool_w"], tm)
    qkv_act = _conv_fwd(proj, wts["conv_w"], tm)
    bg = _ba_fwd(proj, al_row, dtb_row, tm)
    bgt = bg[:, :2 * HEADS].T
    u, w, qg, kg, attn, ymat, egl = _dn_local_fwd(qkv_act, bg, bgt)
    o, vn, states = _dn_scan_fwd(u, w, qg, kg, attn, egl)
    mixed = _mix_fwd(o, proj, ypre, pool_scale, wo_row, tm)
    wts = {**wts, **late_weights(mixed)}
    a1, h1, h1_bf = _oproj_ln1(mixed, wts["w_out"], h0, g1, b1, tm)
    act = _mlp_up(h1_bf, wts["w_up"], tm)
    p_bf = _mx(p)
    dr, dr_bf, dgp, dpp, r_bf, acc_tail = _tail(act, wts["w_down"], h1, wts["ple_gate_w"], p_bf, wts["ple_proj_w"],
                                                tgt, g2, b2, tms)
    grads = {}
    grads["ple_proj_w"] = _matmul(p_bf, dpp, "tn", "dw_ple_proj", WIRE_DTYPE, tm=256, tn=1024, tk=DW_TK)
    grads["ple_gate_w"] = _matmul(r_bf, dgp, "tn", "dw_ple_gate", WIRE_DTYPE, tm=DW_TM, tn=1024, tk=DW_TK)
    grads["w_down"] = _matmul(act, dr_bf, "tn", "dw_down", WIRE_DTYPE, tm=DW_TM, tn=1024, tk=DW_TK)
    dup = _mlp_bwd1(dr_bf, wts["w_down"], act, tm, 1024)
    grads["w_up"] = _matmul(h1_bf, dup, "tn", "dw_up", WIRE_DTYPE, tm=DW_TM, tn=1024, tk=DW_TK, stack_out=True)
    da1, da1_bf, acc_ln1 = _mlp_bwd2(dup, wts["w_up"], dr, a1, g1, tms)
    grads["w_out"] = _matmul(mixed, da1_bf, "tn", "dw_out", WIRE_DTYPE, tm=DW_TM, tn=1024, tk=DW_TK)
    sent = send_late_grads(grads)
    do, dz, dga, dgb, dyp, acc_mix = _mix_bwd(da1_bf, wts["w_out"], o, proj, ypre, pool_scale, wo_row, tms, sent)
    du_pool, grads["pool_w"] = _pool_bwd(dyp, d_bf, wts["pool_w"], tm)
    dvn, dkg, dqg, dattn, dw, degl = _dn_scan_bwd(do, qg, kg, w, attn, vn, states, egl)
    dqkv_act, dbg = _dn_local_bwd(qkv_act, bg, bgt, u, w, ymat, dvn, dw, dqg, dkg, dattn, degl)
    dqkv, acc_conv = _conv_bwd(dqkv_act, proj, wts["conv_w"], tm)
    dba, acc_ba = _ba_bwd(dbg, bg, proj, al_row, dtb_row, tm)
    dproj = jnp.concatenate([dqkv, dz, dga, dgb, du_pool, dba,
                             jnp.zeros((t, CAT_WIDTH - K_BA - 128), MXU_DTYPE)], axis=1)
    dw_cat = _matmul(h0_bf, dproj, "tn", "dw_in", F32, tm=DW_TM, tn=1152, tk=DW_TK)
    grad_x, acc_in = _ln_in_bwd(dproj, w_cat, da1, x, g_in, tms)

    grads["w_in"] = jnp.concatenate(
        [dw_cat[:, K_U:K_U + 512], dw_cat[:, K_QKV:K_QKV + 3072], dw_cat[:, K_Z:K_Z + 1024],
         dw_cat[:, K_BA:K_BA + 16], dw_cat[:, K_GA:K_GA + 1024], dw_cat[:, K_GB:K_GB + 1024]], axis=1)
    grads["conv_w"] = acc_conv[0:CONV_K]
    grads["ln_in_g"], grads["ln_in_b"] = acc_in[0], acc_in[1]
    grads["ln1_g"], grads["ln1_b"] = acc_ln1[0], acc_ln1[1]
    grads["ln2_g"], grads["ln2_b"] = acc_tail[0], acc_tail[1]
    grads["pool_scale"] = acc_mix[0]
    grads["o_norm_w"] = acc_mix[1, 0:HEAD_DIM]
    grads["a_log"] = acc_ba[0, HEADS:2 * HEADS]
    grads["dt_bias"] = acc_ba[1, HEADS:2 * HEADS]
    loss = acc_tail[2, 0]
    return grad_x, grads, loss


MESH = pl.DeviceIdType.MESH
ANY = pl.BlockSpec(memory_space=pl.ANY)


def _chip_of(k, x, y):
    chip = (2 * x + y + k) % N_CHIPS
    return chip // 2, chip % 2


def _place():
    x, y, c = lax.axis_index("x"), lax.axis_index("y"), lax.axis_index("c")
    return x, y, c, 2 * x + y


def _half(rows, c):
    return pl.ds(pl.multiple_of(c * (rows // 2), 16), rows // 2)


def _remote(src, dst, send_sem, recv_sem, device_id):
    return pltpu.make_async_remote_copy(src_ref=src, dst_ref=dst, send_sem=send_sem, recv_sem=recv_sem,
                                        device_id=device_id, device_id_type=MESH)


def _tile_rows(rows):
    for tr in (256, 128, 64, 32, 16):
        if rows % tr == 0:
            return tr
    raise ValueError(rows)


def _gather_weights(shards, conv_shard):
    n = len(shards)

    def body(*refs):
        ins, conv_in = refs[0:n], refs[n]
        outs, conv_out = refs[n + 1:2 * n + 1], refs[2 * n + 1]
        send, recv, fsend, frecv, csend, crecv, lsend, lrecv = refs[2 * n + 2:]
        x, y, c, me = _place()
        here, sib = (x, y, c), (x, y, 1 - c)
        own = [_remote(ins[a], outs[a].at[me], lsend.at[a], lrecv.at[a], sib) for a in range(n)]
        own.append(_remote(conv_in, conv_out.at[me], lsend.at[n], lrecv.at[n], sib))
        for cp in own:
            cp.start()
        sends = []
        for k in range(1, N_CHIPS):
            tx, ty = _chip_of(k, x, y)
            for a in range(n):
                mine = _half(shards[a].shape[0], c)
                sends.append(_remote(ins[a].at[mine], outs[a].at[me, mine], send.at[a * N_CHIPS + k],
                                     recv.at[a * N_CHIPS + k], (tx, ty, c)))
            sends.append(_remote(conv_in, conv_out.at[me], csend.at[k], crecv.at[k], (tx, ty, c)))
        for cp in sends:
            cp.start()
        forwards = []
        for k in range(1, N_CHIPS):
            src = (me + N_CHIPS - k) % N_CHIPS
            for a in range(n):
                landed = outs[a].at[src, _half(shards[a].shape[0], c)]
                _remote(landed, landed, send.at[a * N_CHIPS + k], recv.at[a * N_CHIPS + k], here).wait_recv()
                fwd = _remote(landed, landed, fsend.at[a * N_CHIPS + k], frecv.at[a * N_CHIPS + k], sib)
                fwd.start()
                forwards.append(fwd)
            _remote(conv_in, conv_out.at[src], csend.at[k], crecv.at[k], here).wait_recv()
        for k in range(1, N_CHIPS):
            src = (me + N_CHIPS - k) % N_CHIPS
            for a in range(n):
                passed = outs[a].at[src, _half(shards[a].shape[0], 1 - c)]
                _remote(passed, passed, fsend.at[a * N_CHIPS + k], frecv.at[a * N_CHIPS + k], here).wait_recv()
        for cp in sends + forwards:
            cp.wait_send()
        for cp in own:
            cp.wait()

    sems = pltpu.SemaphoreType.DMA((n * N_CHIPS,))
    return pl.pallas_call(
        body, name="gather_weights",
        out_shape=[SDS((N_CHIPS,) + s.shape, s.dtype) for s in shards]
        + [SDS((N_CHIPS,) + conv_shard.shape, conv_shard.dtype)],
        in_specs=[ANY] * (n + 1), out_specs=[ANY] * (n + 1),
        scratch_shapes=[sems, sems, sems, sems, pltpu.SemaphoreType.DMA((N_CHIPS,)),
                        pltpu.SemaphoreType.DMA((N_CHIPS,)), pltpu.SemaphoreType.DMA((n + 1,)),
                        pltpu.SemaphoreType.DMA((n + 1,))],
    )(*shards, conv_shard)


def _swap_halves(gs):
    n = len(gs)

    def body(*refs):
        ins, theirs = refs[0:n], refs[n:2 * n]
        send, recv = refs[2 * n:]
        x, y, c, _ = _place()
        copies = [_remote(ins[a].at[:, _half(gs[a].shape[1], 1 - c)], theirs[a], send.at[a], recv.at[a],
                          (x, y, 1 - c)) for a in range(n)]
        for cp in copies:
            cp.start()
        for cp in copies:
            cp.wait()

    return pl.pallas_call(
        body, name="swap_halves", out_shape=[SDS((N_CHIPS, g.shape[1] // 2, g.shape[2]), g.dtype) for g in gs],
        in_specs=[ANY] * n, out_specs=[ANY] * n, scratch_shapes=[pltpu.SemaphoreType.DMA((n,))] * 2,
    )(*gs)


def _scatter_halves(qs):
    n = len(qs)

    def body(*refs):
        ins, outs = refs[0:n], refs[n:2 * n]
        send, recv = refs[2 * n:]
        x, y, c, me = _place()
        copies = []
        for k in range(1, N_CHIPS):
            tx, ty = _chip_of(k, x, y)
            for a in range(n):
                copies.append(_remote(ins[a].at[2 * tx + ty], outs[a].at[k - 1], send.at[a * N_CHIPS + k],
                                      recv.at[a * N_CHIPS + k], (tx, ty, c)))
        for cp in copies:
            cp.start()
        for cp in copies:
            cp.wait()

    sems = pltpu.SemaphoreType.DMA((n * N_CHIPS,))
    return pl.pallas_call(
        body, name="scatter_halves", out_shape=[SDS((N_CHIPS - 1,) + q.shape[1:], q.dtype) for q in qs],
        in_specs=[ANY] * n, out_specs=[ANY] * n, scratch_shapes=[sems, sems],
    )(*qs)


def _send_to_sibling(hs):
    n = len(hs)

    def body(*refs):
        ins, outs = refs[0:n], refs[n:2 * n]
        send, recv = refs[2 * n:]
        x, y, c, _ = _place()
        copies = [_remote(ins[a], outs[a], send.at[a], recv.at[a], (x, y, 1 - c)) for a in range(n)]
        for cp in copies:
            cp.start()
        for cp in copies:
            cp.wait()

    return pl.pallas_call(
        body, name="send_to_sibling", out_shape=[SDS(h.shape, h.dtype) for h in hs],
        in_specs=[ANY] * n, out_specs=[ANY] * n, scratch_shapes=[pltpu.SemaphoreType.DMA((n,))] * 2,
    )(*hs)


HBM = pl.BlockSpec(memory_space=pltpu.HBM)
SEM = pl.BlockSpec(memory_space=pltpu.SEMAPHORE)
EFFECT = pltpu.SideEffectType.DATAFLOW_SIDE_EFFECTING


def _in_hbm(a):
    return pltpu.with_memory_space_constraint(a, pltpu.HBM)


def _split_copy_start(name, srcs, lands, copies_of, after):
    n = len(srcs)
    n_copies = len(copies_of(srcs, lands, None, None, None))

    def body(*refs):
        src_refs, land_refs = refs[0:n], refs[n:2 * n]
        send, recv = refs[2 * n + 1], refs[2 * n + 2]
        token = refs[-1]
        for cp in copies_of(src_refs, land_refs, send, recv, _place()):
            cp.start()
        token[...] = jnp.zeros_like(token)

    sems = pltpu.SemaphoreType.DMA((n_copies,))
    out = pl.pallas_call(
        body, name=name,
        out_shape=[sems, sems] + [pltpu.HBM(a.shape, a.dtype) for a in list(srcs) + list(lands)] + [SDS((8, 128), F32)],
        in_specs=[HBM] * (2 * n) + [ANY],
        out_specs=[SEM, SEM] + [HBM] * (2 * n) + [pl.BlockSpec(memory_space=pltpu.VMEM)],
        input_output_aliases={i: 2 + i for i in range(2 * n)},
        compiler_params=pltpu.CompilerParams(has_side_effects=EFFECT),
    )(*[_in_hbm(a) for a in list(srcs) + list(lands)], after)
    return out[0], out[1], out[2:2 + n], out[2 + n:2 + 2 * n], out[-1]


def _split_copy_wait(name, send, recv, srcs, lands, after, copies_of):
    n = len(srcs)

    def body(*refs):
        src_refs, land_refs = refs[0:n], refs[n:2 * n]
        send_ref, recv_ref = refs[2 * n], refs[2 * n + 1]
        for cp in copies_of(src_refs, land_refs, send_ref, recv_ref, _place()):
            cp.wait_send()
            cp.wait_recv()

    out = pl.pallas_call(
        body, name=name, out_shape=[pltpu.HBM(a.shape, a.dtype) for a in list(srcs) + list(lands)],
        in_specs=[HBM] * (2 * n) + [SEM, SEM, ANY], out_specs=[HBM] * (2 * n),
        input_output_aliases={i: i for i in range(2 * n)},
        compiler_params=pltpu.CompilerParams(has_side_effects=EFFECT),
    )(*srcs, *lands, send, recv, after)
    return out[0:n], out[n:2 * n]


def _late_gather_copies(srcs, lands, send, recv, place):
    copies = []
    for a in range(len(srcs)):
        for k in range(N_CHIPS):
            if place is None:
                copies.append(None)
                continue
            x, y, c, me = place
            if k == 0:
                target = (x, y, 1 - c)
            else:
                tx, ty = _chip_of(k, x, y)
                target = (tx, ty, c)
            copies.append(_remote(srcs[a], lands[a].at[me], send.at[a * N_CHIPS + k], recv.at[a * N_CHIPS + k], target))
    return copies


def _late_scatter_copies(srcs, lands, send, recv, place):
    copies = []
    for a in range(len(srcs)):
        for k in range(1, N_CHIPS):
            if place is None:
                copies.append(None)
                continue
            x, y, c, _ = place
            tx, ty = _chip_of(k, x, y)
            copies.append(_remote(srcs[a].at[2 * tx + ty], lands[a].at[k - 1], send.at[a * (N_CHIPS - 1) + k - 1],
                                  recv.at[a * (N_CHIPS - 1) + k - 1], (tx, ty, c)))
    return copies


def _add_pair(g, theirs, name):
    _, rows, cols = g.shape
    half = rows // 2
    tr = _tile_rows(half)

    def body(g_ref, t_ref, o_ref):
        own = g_ref[lax.axis_index("c")]
        o_ref[...] = (own.astype(F32) + t_ref[...].astype(F32)).astype(o_ref.dtype)

    blk = pl.BlockSpec((None, tr, cols), lambda j, i: (j, i, 0))
    return _pc(body, "add_" + name, (N_CHIPS, half // tr),
               [pl.BlockSpec((None, 2, tr, cols), lambda j, i: (j, 0, i, 0)), blk], blk,
               SDS((N_CHIPS, half, cols), g.dtype), sem=("parallel", "parallel"))(
                   g.reshape(N_CHIPS, 2, half, cols), theirs)


def _sum_slabs(pair, landed, name):
    _, rows, cols = pair.shape
    tr = _tile_rows(rows)

    def body(p_ref, r_ref, o_ref):
        acc = p_ref[2 * lax.axis_index("x") + lax.axis_index("y")].astype(F32)
        for k in range(N_CHIPS - 1):
            acc = acc + r_ref[k].astype(F32)
        o_ref[...] = acc

    return _pc(body, "sum_" + name, (rows // tr,),
               [pl.BlockSpec((N_CHIPS, tr, cols), lambda i: (0, i, 0)),
                pl.BlockSpec((N_CHIPS - 1, tr, cols), lambda i: (0, i, 0))],
               _row(tr, cols), SDS((rows, cols), F32), sem=("parallel",))(pair, landed)


def _adamw_math(w, g, m, v):
    m = ADAM_B1 * m + (1.0 - ADAM_B1) * g
    v = ADAM_B2 * v + (1.0 - ADAM_B2) * (g * g)
    m_hat = m / (1.0 - ADAM_B1 ** ADAM_STEP)
    v_hat = v / (1.0 - ADAM_B2 ** ADAM_STEP)
    delta = -ADAM_LR * (m_hat / (jnp.sqrt(v_hat) + ADAM_EPS) + ADAM_WD * w)
    return delta, m, v


def _adamw_2d(w, g_own, g_sib, m, v, name, halves):
    rows, cols = w.shape
    tr = _tile_rows(rows // 2)
    nh = rows // 2 // tr if halves else rows // tr

    def body(w_ref, go_ref, gs_ref, m_ref, v_ref, g_out, d_out, m_out, v_out):
        if halves:
            mine = (pl.program_id(0) // nh) == lax.axis_index("c")
            g = jnp.where(mine, go_ref[...], gs_ref[...])
        else:
            g = go_ref[...] + gs_ref[...]
        delta, mn, vn = _adamw_math(w_ref[...], g, m_ref[...], v_ref[...])
        g_out[...] = g
        d_out[...] = delta
        m_out[...] = mn
        v_out[...] = vn

    r = _row(tr, cols)
    h = pl.BlockSpec((tr, cols), lambda i: (i % nh, 0))
    return _pc(body, "adamw_" + name, (rows // tr,), [r, h, h, r, r], [r] * 4, [SDS((rows, cols), F32)] * 4,
               sem=("parallel",))(w, g_own, g_sib, m, v)


def _small_allreduce_adamw(mine, w, m, v):
    shape = mine.shape

    def body(mine_ref, w_ref, m_ref, v_ref, g_out, d_out, m_out, v_out, buf_ref, send_sems, recv_sems):
        x, y, c = lax.axis_index("x"), lax.axis_index("y"), lax.axis_index("c")
        me = 4 * x + 2 * y + c
        buf_ref[me] = mine_ref[...]
        copies = []
        for k in range(1, N_DEV):
            tgt = (me + k) % N_DEV
            copies.append(pltpu.make_async_remote_copy(
                src_ref=mine_ref, dst_ref=buf_ref.at[me], send_sem=send_sems.at[k], recv_sem=recv_sems.at[k],
                device_id=(tgt // 4, (tgt // 2) % 2, tgt % 2), device_id_type=MESH))
        for cp in copies:
            cp.start()
        for k in range(1, N_DEV):
            src = (me + N_DEV - k) % N_DEV
            pltpu.make_async_remote_copy(
                src_ref=mine_ref, dst_ref=buf_ref.at[src], send_sem=send_sems.at[k], recv_sem=recv_sems.at[k],
                device_id=(x, y, c), device_id_type=MESH).wait_recv()
        for cp in copies:
            cp.wait_send()
        g = buf_ref[0]
        for j in range(1, N_DEV):
            g = g + buf_ref[j]
        delta, mn, vn = _adamw_math(w_ref[...], g, m_ref[...], v_ref[...])
        g_out[...] = g
        d_out[...] = delta
        m_out[...] = mn
        v_out[...] = vn

    vm = pl.BlockSpec(memory_space=pltpu.VMEM)
    return pl.pallas_call(
        body, name="small_allreduce_adamw", out_shape=[SDS(shape, F32)] * 4, in_specs=[vm] * 4, out_specs=[vm] * 4,
        scratch_shapes=[pltpu.VMEM((N_DEV,) + shape, F32), pltpu.SemaphoreType.DMA((N_DEV,)),
                        pltpu.SemaphoreType.DMA((N_DEV,))],
    )(mine, w, m, v)


def _as2d(a):
    return a.reshape(-1, a.shape[-1])


def _w_cat(stack):
    wi = stack.transpose(1, 0, 2).reshape(D_MODEL, IN_WIDTH)
    return jnp.concatenate(
        [wi[:, C_QKV:C_Z], wi[:, C_Z:C_BETA], wi[:, C_GA:C_GB], wi[:, C_GB:IN_WIDTH], wi[:, C_POOL:C_QKV],
         wi[:, C_BETA:C_GA], jnp.zeros((D_MODEL, CAT_WIDTH - K_BA - 2 * HEADS), wi.dtype)], axis=1)


WEIGHT_LAYOUT = {
    "w_in": lambda s: ("w_cat", _w_cat(s)),
    "pool_w": lambda s: ("pool_w", s.reshape(N_CHIPS, 4, POOL_GROUP, POOL_OUT_GROUP // N_CHIPS)
                         .transpose(1, 2, 0, 3).reshape(4, POOL_GROUP, POOL_OUT_GROUP)),
    "w_out": lambda s: ("w_out", s.reshape(D_MODEL, D_MODEL)),
    "w_up": lambda s: ("w_up", s),
    "w_down": lambda s: ("w_down", s.reshape(D_FF, D_MODEL)),
    "ple_gate_w": lambda s: ("ple_gate_w", s.reshape(D_MODEL, D_MODEL)),
    "ple_proj_w": lambda s: ("ple_proj_w", s.transpose(1, 0, 2).reshape(PLE_DIM, D_MODEL)),
}

GRAD_LAYOUT = {
    "w_in": lambda g: g.reshape(D_MODEL, N_CHIPS, IN_WIDTH // N_CHIPS).transpose(1, 0, 2),
    "pool_w": lambda g: g.reshape(4, POOL_GROUP, N_CHIPS, POOL_OUT_GROUP // N_CHIPS)
                         .transpose(2, 0, 1, 3).reshape(N_CHIPS, 4 * POOL_GROUP, POOL_OUT_GROUP // N_CHIPS),
    "w_out": lambda g: g.reshape(N_CHIPS, D_MODEL // N_CHIPS, D_MODEL),
    "w_up": lambda g: g,
    "w_down": lambda g: g.reshape(N_CHIPS, D_FF // N_CHIPS, D_MODEL),
    "ple_gate_w": lambda g: g.reshape(N_CHIPS, D_MODEL // N_CHIPS, D_MODEL),
    "ple_proj_w": lambda g: g.reshape(PLE_DIM, N_CHIPS, D_MODEL // N_CHIPS).transpose(1, 0, 2),
}


def _full_weights(names, stacks):
    return dict(WEIGHT_LAYOUT[n](s.astype(MXU_DTYPE)) for n, s in zip(names, stacks))


def _grads_by_chip(names, grads):
    return [GRAD_LAYOUT[n](grads[n]).astype(WIRE_DTYPE) for n in names]


def _pack_small(rows, conv, name):
    n = len(rows)

    def body(*refs):
        out = refs[n + 1]
        out[...] = jnp.zeros_like(out)
        for i in range(n):
            out[i:i + 1, :] = refs[i][...]
        out[SMALL_CONV_AT:SMALL_CONV_AT + SMALL_CONV_ROWS, :] = refs[n][...]

    vm = pl.BlockSpec(memory_space=pltpu.VMEM)
    return pl.pallas_call(body, name=name, out_shape=SDS((SMALL_CONV_AT + SMALL_CONV_ROWS, D_MODEL), F32),
                          in_specs=[vm] * (n + 1), out_specs=vm)(*rows, conv)


def _pad_row(a):
    a = a.reshape(1, -1).astype(F32)
    return jnp.pad(a, ((0, 0), (0, D_MODEL - a.shape[1])))


def kernel(x, p, ln_in_g, ln_in_b, w_in, pool_w, pool_scale, conv_w, a_log, dt_bias, o_norm_w, w_out, ln1_g, ln1_b, w_up, w_down, ple_gate_w, ple_proj_w, ln2_g, ln2_b, loss_target, m_ln_in_g, m_ln_in_b, m_w_in, m_pool_w, m_pool_scale, m_conv_w, m_a_log, m_dt_bias, m_o_norm_w, m_w_out, m_ln1_g, m_ln1_b, m_w_up, m_w_down, m_ple_gate_w, m_ple_proj_w, m_ln2_g, m_ln2_b, v_ln_in_g, v_ln_in_b, v_w_in, v_pool_w, v_pool_scale, v_conv_w, v_a_log, v_dt_bias, v_o_norm_w, v_w_out, v_ln1_g, v_ln1_b, v_w_up, v_w_down, v_ple_gate_w, v_ple_proj_w, v_ln2_g, v_ln2_b):
    given = dict(locals())
    chip = 2 * lax.axis_index("x") + lax.axis_index("y")

    shard = lambda n: _as2d(given[n]).astype(WIRE_DTYPE)

    conv_pad = jnp.pad(conv_w[0], ((0, 8 - CONV_K), (0, 0)))
    gathered = _gather_weights([shard(n) for n in EARLY], conv_pad)
    wts = _full_weights(EARLY, gathered[0:len(EARLY)])
    wts.update({
        "conv_w": jnp.concatenate([gathered[len(EARLY)][j, 0:CONV_K] for j in range(N_CHIPS)], axis=1),
        "ln_in_g": ln_in_g, "ln_in_b": ln_in_b, "pool_scale": pool_scale[0], "a_log": a_log[0],
        "dt_bias": dt_bias[0], "o_norm_w": o_norm_w[0], "ln1_g": ln1_g[0], "ln1_b": ln1_b[0],
        "ln2_g": ln2_g[0], "ln2_b": ln2_b[0],
    })

    late_srcs = [shard(n) for n in LATE]
    late_lands = [lax.empty((N_CHIPS,) + s.shape, s.dtype) for s in late_srcs]
    gsend, grecv, gsrcs, glands, start_token = _split_copy_start(
        "late_gather_start", late_srcs, late_lands, _late_gather_copies, gathered[0])

    def late_weights(after):
        _, stacks = _split_copy_wait("late_gather_wait", gsend, grecv, gsrcs, glands, after, _late_gather_copies)
        return _full_weights(LATE, stacks)

    scatter = {}

    def send_late_grads(grads):
        srcs = _grads_by_chip(LATE, grads)
        lands = [lax.empty((N_CHIPS - 1,) + g.shape[1:], g.dtype) for g in srcs]
        scatter["send"], scatter["recv"], scatter["srcs"], scatter["lands"], token = _split_copy_start(
            "late_scatter_start", srcs, lands, _late_scatter_copies, srcs[0])
        return token

    grad_x, grads, loss = _local_step(x[0], p[0, 0], loss_target[0], wts, start_token, late_weights, send_late_grads)

    late_mine, late_landed = _split_copy_wait("late_scatter_wait", scatter["send"], scatter["recv"], scatter["srcs"],
                                              scatter["lands"], grad_x, _late_scatter_copies)
    late_part = [_sum_slabs(q, r, n) for q, r, n in zip(late_mine, late_landed, LATE)]
    by_chip = _grads_by_chip(EARLY, grads)
    theirs = _swap_halves(by_chip)
    pair = [_add_pair(g, t, n) for g, t, n in zip(by_chip, theirs, EARLY)]
    landed = _scatter_halves(pair)
    reduced = [_sum_slabs(q, r, n) for q, r, n in zip(pair, landed, EARLY)]
    from_sibling = _send_to_sibling(reduced + late_part)
    big_out = {}
    for n, g_own, g_sib in zip(EARLY + LATE, reduced + late_part, from_sibling):
        res = _adamw_2d(_as2d(given[n]), g_own, g_sib, _as2d(given["m_" + n]), _as2d(given["v_" + n]), n,
                        halves=n in EARLY)
        big_out[n] = [r.reshape(given[n].shape) for r in res]

    conv_cols = QKV_WIDTH // N_CHIPS

    def small_pack(get, conv, extra, name):
        if conv.shape[1] != QKV_WIDTH:
            conv = lax.dynamic_update_slice(jnp.zeros((CONV_K, QKV_WIDTH), F32), conv, (0, chip * conv_cols))
        return _pack_small([_pad_row(get(n)) for n in SMALL_NAMES] + extra, conv.reshape(SMALL_CONV_ROWS, D_MODEL), name)

    mine_small = small_pack(lambda n: grads[n], grads["conv_w"], [jnp.full((1, D_MODEL), loss, F32)], "pack_small_g")
    packed_small = [small_pack(lambda n: given[prefix + n], given[prefix + "conv_w"][0], [], "pack_small_" + tag)
                    for prefix, tag in (("", "w"), ("m_", "m"), ("v_", "v"))]
    small_out = _small_allreduce_adamw(mine_small, *packed_small)

    def small_get(k, n):
        if n == "conv_w":
            full = small_out[k][SMALL_CONV_AT:SMALL_CONV_AT + SMALL_CONV_ROWS].reshape(CONV_K, QKV_WIDTH)
            return lax.dynamic_slice(full, (0, chip * conv_cols), (CONV_K, conv_cols)).reshape(given[n].shape)
        i = SMALL_NAMES.index(n)
        return small_out[k][i, 0:given[n].size].reshape(given[n].shape)

    order = ["ln_in_g", "ln_in_b", "w_in", "pool_w", "pool_scale", "conv_w", "a_log", "dt_bias", "o_norm_w", "w_out",
             "ln1_g", "ln1_b", "w_up", "w_down", "ple_gate_w", "ple_proj_w", "ln2_g", "ln2_b"]
    outs = [small_out[0][len(SMALL_NAMES), 0], grad_x[None]]
    for k in range(4):
        for n in order:
            outs.append(big_out[n][k] if n in big_out else small_get(k, n))
    return tuple(outs)
```

```python
import jax
import jax.numpy as jnp
from jax import lax
from jax.experimental import pallas as pl
from jax.experimental.pallas import tpu as pltpu

F32 = jnp.float32
MXU_DTYPE = jnp.bfloat16
WIRE_DTYPE = jnp.bfloat16
SDS = jax.ShapeDtypeStruct

D_MODEL = 1024
POOL_WINDOWS = (2, 4, 8, 16)
POOL_WIDTH = 512
POOL_GROUP = 128
POOL_OUT_GROUP = 256
HEADS = 8
HEAD_DIM = 128
DN_WIDTH = HEADS * HEAD_DIM
QKV_WIDTH = 3 * DN_WIDTH
CONV_K = 4
CHUNK = 128
DW_TK = 1024
DW_TM = 1024
HEAD_GROUP = 8
D_FF = 4096
PLE_DIM = 256
LN_EPS = 1e-5
RMS_EPS = 1e-6
L2_EPS = 1e-6
ALPHA = 2.0 ** 0.25
Q_SCALE = HEAD_DIM ** -0.5
IN_WIDTH = 6672
C_POOL, C_QKV, C_Z, C_BETA, C_A, C_GA, C_GB = 0, 512, 3584, 4608, 4616, 4624, 5648
K_QKV, K_Z, K_GA, K_GB, K_U, K_BA, CAT_WIDTH = 0, 3072, 4096, 5120, 6144, 6656, 6912

ADAM_LR, ADAM_B1, ADAM_B2, ADAM_EPS, ADAM_WD, ADAM_STEP = 0.001, 0.9, 0.999, 1e-08, 0.01, 10

N_CHIPS = 4
N_DEV = 8
VMEM_LIMIT = 56 * 1024 * 1024

EARLY = ("w_in", "pool_w")
LATE = ("w_out", "w_up", "w_down", "ple_gate_w", "ple_proj_w")
SMALL_NAMES = ("ln_in_g", "ln_in_b", "pool_scale", "ln1_g", "ln1_b", "ln2_g", "ln2_b", "o_norm_w", "a_log", "dt_bias")
SMALL_CONV_AT = 12
SMALL_CONV_ROWS = CONV_K * QKV_WIDTH // D_MODEL


def _mx(a):
    return a.astype(MXU_DTYPE)


def _dot(a, b):
    return lax.dot_general(_mx(a), _mx(b), (((1,), (0,)), ((), ())), preferred_element_type=F32)


def _dot_nt(a, b):
    return lax.dot_general(_mx(a), _mx(b), (((1,), (1,)), ((), ())), preferred_element_type=F32)


def _dot_tn(a, b):
    return lax.dot_general(_mx(a), _mx(b), (((0,), (0,)), ((), ())), preferred_element_type=F32)


def _sigmoid(x):
    return 0.5 * jnp.tanh(0.5 * x) + 0.5


def _softplus(x):
    return jnp.maximum(x, 0.0) + jnp.log(1.0 + jnp.exp(-jnp.abs(x)))


def _pc(body, name, grid, in_specs, out_specs, out_shape, scratch=(), sem=None, aliases=None):
    return pl.pallas_call(
        body, out_shape=out_shape, grid=grid, in_specs=in_specs, out_specs=out_specs,
        scratch_shapes=scratch, name=name, input_output_aliases=aliases or {},
        compiler_params=pltpu.CompilerParams(dimension_semantics=sem, vmem_limit_bytes=VMEM_LIMIT))


def _row(tm, n):
    return pl.BlockSpec((tm, n), lambda i: (i, 0))


def _const(shape):
    nd = len(shape)
    return pl.BlockSpec(shape, lambda *_: (0,) * nd)


def _matmul(a, b, mode, name, out_dtype=F32, tm=512, tn=512, tk=512, stack_out=False):
    if mode == "nn":
        (m, k), n = a.shape, b.shape[1]
    elif mode == "nt":
        (m, k), n = a.shape, b.shape[0]
    else:
        (k, m), n = a.shape, b.shape[1]
    tm, tn, tk = min(tm, m), min(tn, n), min(tk, k)
    assert m % tm == 0 and n % tn == 0 and k % tk == 0, (name, m, n, k, tm, tn, tk)
    nk = k // tk
    if mode == "nn":
        a_spec = pl.BlockSpec((tm, tk), lambda i, j, kk: (i, kk))
        b_spec = pl.BlockSpec((tk, tn), lambda i, j, kk: (kk, j))
        dot = _dot
    elif mode == "nt":
        a_spec = pl.BlockSpec((tm, tk), lambda i, j, kk: (i, kk))
        b_spec = pl.BlockSpec((tn, tk), lambda i, j, kk: (j, kk))
        dot = _dot_nt
    else:
        a_spec = pl.BlockSpec((tk, tm), lambda i, j, kk: (kk, i))
        b_spec = pl.BlockSpec((tk, tn), lambda i, j, kk: (kk, j))
        dot = _dot_tn

    def body(a_ref, b_ref, o_ref, *acc):
        if nk == 1:
            o_ref[...] = dot(a_ref[...], b_ref[...]).astype(out_dtype)
            return
        acc_ref, kk = acc[0], pl.program_id(2)

        @pl.when(kk == 0)
        def _():
            acc_ref[...] = dot(a_ref[...], b_ref[...])

        @pl.when((kk > 0) & (kk < nk - 1))
        def _():
            acc_ref[...] += dot(a_ref[...], b_ref[...])

        @pl.when(kk == nk - 1)
        def _():
            o_ref[...] = (acc_ref[...] + dot(a_ref[...], b_ref[...])).astype(out_dtype)

    if stack_out:
        o_spec, o_shape = pl.BlockSpec((None, tm, tn), lambda i, j, kk: (j, i, 0)), SDS((n // tn, m, tn), out_dtype)
    else:
        o_spec, o_shape = pl.BlockSpec((tm, tn), lambda i, j, kk: (i, j)), SDS((m, n), out_dtype)
    return _pc(body, name, (m // tm, n // tn, nk), [a_spec, b_spec], o_spec, o_shape,
               scratch=[pltpu.VMEM((tm, tn), F32)] if nk > 1 else [],
               sem=("parallel", "parallel", "arbitrary"))(a, b)


PROJ_TN = 1152


def _proj(h0_bf, w_cat, tm):
    t = h0_bf.shape[0]

    def body(h_ref, w_ref, o_ref):
        h = h_ref[...]
        for c0 in range(0, CAT_WIDTH, PROJ_TN):
            o_ref[:, c0:c0 + PROJ_TN] = _dot(h, w_ref[:, c0:c0 + PROJ_TN])

    return _pc(body, "proj", (t // tm,), [_row(tm, D_MODEL), _const((D_MODEL, CAT_WIDTH))], _row(tm, CAT_WIDTH),
               SDS((t, CAT_WIDTH), F32), sem=("parallel",))(h0_bf, w_cat)


def _ln_stats(x):
    mu = jnp.mean(x, axis=-1, keepdims=True)
    xc = x - mu
    var = jnp.mean(xc * xc, axis=-1, keepdims=True)
    rstd = lax.rsqrt(var + LN_EPS)
    return xc * rstd, rstd


def _ln_bwd(dy, xhat, rstd, g):
    dxh = dy * g
    m1 = jnp.mean(dxh, axis=-1, keepdims=True)
    m2 = jnp.mean(dxh * xhat, axis=-1, keepdims=True)
    return rstd * (dxh - m1 - xhat * m2)


def _ln_in(x, g, b, tm, after):
    t, d = x.shape

    def body(x_ref, g_ref, b_ref, after_ref, h_ref, hb_ref):
        xhat, _ = _ln_stats(x_ref[...])
        h = xhat * g_ref[...] + b_ref[...]
        h_ref[...] = h
        hb_ref[...] = _mx(h)

    return _pc(body, "ln_in", (t // tm,), [_row(tm, d), _const((1, d)), _const((1, d)), ANY],
               [_row(tm, d), _row(tm, d)], [SDS((t, d), F32), SDS((t, d), MXU_DTYPE)],
               sem=("parallel",))(x, g, b, after)


def _pool_fwd(proj, pool_w, tm):
    t = proj.shape[0]
    ublk = K_U // POOL_WIDTH

    def body(u_ref, halo_ref, pw_ref, ypre_ref, d_ref, ext_ref):
        i = pl.program_id(0)
        ext_ref[0:16, :] = jnp.where(i > 0, halo_ref[...], 0.0)
        ext_ref[16:16 + tm, :] = u_ref[...]
        tok = i * tm + lax.broadcasted_iota(jnp.int32, (tm, POOL_GROUP), 0)
        for gi, w in enumerate(POOL_WINDOWS):
            cs = pl.ds(gi * POOL_GROUP, POOL_GROUP)
            ug = ext_ref[pl.ds(16, tm), cs]
            s = ug
            for k in range(1, w):
                s = s + ext_ref[pl.ds(16 - k, tm), cs]
            cnt = jnp.minimum(tok + 1, w).astype(F32)
            db = _mx(s / cnt - ug)
            d_ref[:, gi * POOL_GROUP:(gi + 1) * POOL_GROUP] = db
            ypre_ref[:, gi * POOL_OUT_GROUP:(gi + 1) * POOL_OUT_GROUP] = _dot(db, pw_ref[gi])

    halo = pl.BlockSpec((16, POOL_WIDTH), lambda i: (jnp.maximum(i * (tm // 16) - 1, 0), ublk))
    return _pc(body, "pool_fwd", (t // tm,),
               [pl.BlockSpec((tm, POOL_WIDTH), lambda i: (i, ublk)), halo, _const((4, POOL_GROUP, POOL_OUT_GROUP))],
               [_row(tm, D_MODEL), _row(tm, POOL_WIDTH)],
               [SDS((t, D_MODEL), F32), SDS((t, POOL_WIDTH), MXU_DTYPE)],
               scratch=[pltpu.VMEM((16 + tm, POOL_WIDTH), F32)], sem=("parallel",))(proj, proj, pool_w)


def _pool_bwd(dyp, d_bf, pool_w, dproj, tm):
    t = dyp.shape[0]
    n = t // tm

    def body(dy_ref, dyn_ref, d_ref, pw_ref, dproj_ref, du_ref, dpw_ref, ext_ref):
        i = pl.program_id(0)

        @pl.when(i == 0)
        def _():
            dpw_ref[...] = jnp.zeros_like(dpw_ref)

        tok = i * tm + lax.broadcasted_iota(jnp.int32, (tm + 16, POOL_GROUP), 0)
        for gi, w in enumerate(POOL_WINDOWS):
            dy = dy_ref[:, gi * POOL_OUT_GROUP:(gi + 1) * POOL_OUT_GROUP]
            dyn = dyn_ref[:, gi * POOL_OUT_GROUP:(gi + 1) * POOL_OUT_GROUP]
            pw = pw_ref[gi]
            dd = _dot_nt(dy, pw)
            ddn = jnp.where(i < n - 1, _dot_nt(dyn, pw), 0.0)
            cnt = jnp.minimum(tok + 1, w).astype(F32)
            ext_ref[0:tm, :] = dd / cnt[0:tm]
            ext_ref[tm:tm + 16, :] = ddn / cnt[tm:tm + 16]
            s = ext_ref[pl.ds(0, tm), :]
            for k in range(1, w):
                s = s + ext_ref[pl.ds(k, tm), :]
            du_ref[:, gi * POOL_GROUP:(gi + 1) * POOL_GROUP] = _mx(s - dd)
            dpw_ref[gi] += _dot_tn(d_ref[:, gi * POOL_GROUP:(gi + 1) * POOL_GROUP], dy)

    nxt = pl.BlockSpec((16, D_MODEL), lambda i: (jnp.minimum((i + 1) * (tm // 16), t // 16 - 1), 0))
    return _pc(body, "pool_bwd", (n,),
               [_row(tm, D_MODEL), nxt, _row(tm, POOL_WIDTH), _const((4, POOL_GROUP, POOL_OUT_GROUP)), ANY],
               [pl.BlockSpec((tm, POOL_WIDTH), lambda i: (i, K_U // POOL_WIDTH)),
                _const((4, POOL_GROUP, POOL_OUT_GROUP))],
               [SDS(dproj.shape, dproj.dtype), SDS((4, POOL_GROUP, POOL_OUT_GROUP), F32)],
               scratch=[pltpu.VMEM((tm + 16, POOL_GROUP), F32)], sem=("arbitrary",),
               aliases={4: 0})(dyp, dyp, d_bf, pool_w, dproj)


CONV_BLK = 512


CONV_ROWS = 32


def _conv_rows(ext_ref, w, r, rows):
    y = w[0] * ext_ref[pl.ds(r + 5, rows), :]
    for k in range(1, CONV_K):
        y = y + w[k] * ext_ref[pl.ds(r + 5 + k, rows), :]
    return y


def _conv_fwd(proj, conv_w, tm):
    t = proj.shape[0]

    def body(x_ref, halo_ref, w_ref, o_ref, ext_ref):
        i = pl.program_id(0)
        ext_ref[0:8, :] = jnp.where(i > 0, halo_ref[...], 0.0)
        ext_ref[8:8 + tm, :] = x_ref[...]
        w = [w_ref[pl.ds(k, 1), :] for k in range(CONV_K)]
        for r in range(0, tm, CONV_ROWS):
            y = _conv_rows(ext_ref, w, r, CONV_ROWS)
            o_ref[pl.ds(r, CONV_ROWS), :] = y * _sigmoid(y)

    halo = pl.BlockSpec((8, CONV_BLK), lambda i, j: (jnp.maximum(i * (tm // 8) - 1, 0), j))
    blk = pl.BlockSpec((tm, CONV_BLK), lambda i, j: (i, j))
    return _pc(body, "conv_fwd", (t // tm, QKV_WIDTH // CONV_BLK),
               [blk, halo, pl.BlockSpec((CONV_K, CONV_BLK), lambda i, j: (0, j))], blk,
               SDS((t, QKV_WIDTH), F32), scratch=[pltpu.VMEM((8 + tm, CONV_BLK), F32)],
               sem=("parallel", "parallel"))(proj, proj, conv_w)


def _conv_bwd(dact, proj, conv_w, dproj, tm):
    t = proj.shape[0]
    n = t // tm

    def body(da_ref, dan_ref, x_ref, xp_ref, xn_ref, w_ref, dproj_ref, dx_ref, dw_ref, ext_ref, dy_ref):
        i = pl.program_id(1)

        @pl.when(i == 0)
        def _():
            dw_ref[...] = jnp.zeros_like(dw_ref)

        ext_ref[0:8, :] = jnp.where(i > 0, xp_ref[...], 0.0)
        ext_ref[8:8 + tm, :] = x_ref[...]
        ext_ref[8 + tm:16 + tm, :] = jnp.where(i < n - 1, xn_ref[...], 0.0)
        w = [w_ref[pl.ds(k, 1), :] for k in range(CONV_K)]

        def dsilu_of(y):
            s = _sigmoid(y)
            return s * (1.0 + y * (1.0 - s))

        acc = [jnp.zeros((8, CONV_BLK), F32) for _ in range(CONV_K)]
        for r in range(0, tm, CONV_ROWS):
            win = [ext_ref[pl.ds(r + 5 + k, CONV_ROWS), :] for k in range(CONV_K)]
            y = w[0] * win[0]
            for k in range(1, CONV_K):
                y = y + w[k] * win[k]
            dy = da_ref[pl.ds(r, CONV_ROWS), :] * dsilu_of(y)
            dy_ref[pl.ds(r, CONV_ROWS), :] = dy
            for k in range(CONV_K):
                prod = dy * win[k]
                for q in range(0, CONV_ROWS, 8):
                    acc[k] = acc[k] + prod[q:q + 8]
        dy_ref[tm:tm + 8, :] = jnp.where(i < n - 1, dan_ref[...], 0.0) * dsilu_of(_conv_rows(ext_ref, w, tm, 8))
        for k in range(CONV_K):
            dw_ref[pl.ds(k, 1), :] += jnp.sum(acc[k], axis=0, keepdims=True)
        for r in range(0, tm, CONV_ROWS):
            dx = w[0] * dy_ref[pl.ds(r + 3, CONV_ROWS), :]
            for k in range(1, CONV_K):
                dx = dx + w[k] * dy_ref[pl.ds(r + 3 - k, CONV_ROWS), :]
            dx_ref[pl.ds(r, CONV_ROWS), :] = _mx(dx)

    blk = pl.BlockSpec((tm, CONV_BLK), lambda j, i: (i, j))
    prev = pl.BlockSpec((8, CONV_BLK), lambda j, i: (jnp.maximum(i * (tm // 8) - 1, 0), j))
    nxt = pl.BlockSpec((8, CONV_BLK), lambda j, i: (jnp.minimum((i + 1) * (tm // 8), t // 8 - 1), j))
    wspec = pl.BlockSpec((CONV_K, CONV_BLK), lambda j, i: (0, j))
    return _pc(body, "conv_bwd", (QKV_WIDTH // CONV_BLK, n),
               [blk, nxt, blk, prev, nxt, wspec, ANY],
               [blk, pl.BlockSpec((8, CONV_BLK), lambda j, i: (0, j))],
               [SDS(dproj.shape, dproj.dtype), SDS((8, QKV_WIDTH), F32)],
               scratch=[pltpu.VMEM((16 + tm, CONV_BLK), F32), pltpu.VMEM((8 + tm, CONV_BLK), F32)],
               sem=("parallel", "arbitrary"), aliases={6: 0})(dact, dact, proj, proj, proj, conv_w, dproj)


def _lane(shape):
    return lax.broadcasted_iota(jnp.int32, shape, 1)


def _ba_fwd(proj, al_row, dtb_row, tm):
    t = proj.shape[0]
    bablk = K_BA // 128

    def body(ba_ref, al_ref, dtb_ref, bg_ref):
        ba = ba_ref[...]
        lane = _lane(ba.shape)
        g = -jnp.exp(al_ref[...]) * _softplus(ba + dtb_ref[...])
        bg_ref[...] = jnp.where(lane < HEADS, _sigmoid(ba), jnp.where(lane < 2 * HEADS, g, 0.0))

    return _pc(body, "ba_fwd", (t // tm,),
               [pl.BlockSpec((tm, 128), lambda i: (i, bablk)), _const((1, 128)), _const((1, 128))],
               _row(tm, 128), SDS((t, 128), F32), sem=("parallel",))(proj, al_row, dtb_row)


def _ba_bwd(dbg, bg, proj, al_row, dtb_row, dproj, tm):
    t = proj.shape[0]
    bablk = K_BA // 128

    def body(dbg_ref, bg_ref, ba_ref, al_ref, dtb_ref, dproj_ref, dba_ref, acc_ref):
        i = pl.program_id(0)

        @pl.when(i == 0)
        def _():
            acc_ref[...] = jnp.zeros_like(acc_ref)

        dbg_v, bg_v, ba = dbg_ref[...], bg_ref[...], ba_ref[...]
        lane = _lane(ba.shape)
        is_g = (lane >= HEADS) & (lane < 2 * HEADS)
        dbeta_raw = dbg_v * bg_v * (1.0 - bg_v)
        da_raw = dbg_v * (-jnp.exp(al_ref[...])) * _sigmoid(ba + dtb_ref[...])
        dba_ref[:, 0:128] = _mx(jnp.where(lane < HEADS, dbeta_raw, jnp.where(is_g, da_raw, 0.0)))
        dba_ref[:, 128:CAT_WIDTH - K_BA] = jnp.zeros((tm, CAT_WIDTH - K_BA - 128), dba_ref.dtype)
        acc_ref[0:1, :] += jnp.sum(jnp.where(is_g, dbg_v * bg_v, 0.0), axis=0, keepdims=True)
        acc_ref[1:2, :] += jnp.sum(jnp.where(is_g, da_raw, 0.0), axis=0, keepdims=True)

    tail = CAT_WIDTH - K_BA
    return _pc(body, "ba_bwd", (t // tm,),
               [_row(tm, 128), _row(tm, 128), pl.BlockSpec((tm, 128), lambda i: (i, bablk)),
                _const((1, 128)), _const((1, 128)), ANY],
               [pl.BlockSpec((tm, tail), lambda i: (i, K_BA // tail)), _const((8, 128))],
               [SDS(dproj.shape, dproj.dtype), SDS((8, 128), F32)],
               sem=("arbitrary",), aliases={5: 0})(dbg, bg, proj, al_row, dtb_row, dproj)


def _each(f, *lists):
    return [f(*a) for a in zip(*lists)]


def _rowsum(a):
    return jnp.sum(a, axis=1, keepdims=True)


def _chunk_terms(qs, ks, bgv, g_rows, hs):
    c = CHUNK
    ii = lax.broadcasted_iota(jnp.int32, (c, c), 0)
    jj = lax.broadcasted_iota(jnp.int32, (c, c), 1)
    lane = _lane(bgv.shape)
    incl = ii >= jj
    beta = [_rowsum(jnp.where(lane == h, bgv, 0.0)) for h in hs]
    g_col = [_rowsum(jnp.where(lane == HEADS + h, bgv, 0.0)) for h in hs]
    rq = _each(lambda q: lax.rsqrt(_rowsum(q * q) + L2_EPS), qs)
    rk = _each(lambda k: lax.rsqrt(_rowsum(k * k) + L2_EPS), ks)
    yq = _each(jnp.multiply, qs, rq)
    kn = _each(jnp.multiply, ks, rk)
    qn = _each(lambda a: a * Q_SCALE, yq)
    gc_col = _each(lambda g: _rowsum(jnp.where(jj <= ii, g, 0.0)), g_rows)
    gc_row = _each(lambda g: jnp.sum(jnp.where(ii <= jj, g, 0.0), axis=0, keepdims=True), g_col)
    dm = _each(lambda a, b: jnp.where(incl, jnp.exp(jnp.where(incl, a - b, 0.0)), 0.0), gc_col, gc_row)
    gl = _each(_rowsum, g_rows)
    eg = _each(jnp.exp, gc_col)
    ek = _each(lambda a, b: jnp.exp(a - b), gl, gc_col)
    egl = _each(jnp.exp, gl)
    kb = _each(jnp.multiply, kn, beta)
    kk = _each(_dot_nt, kb, kn)
    qk = _each(_dot_nt, qn, kn)
    m = _each(lambda a, b: jnp.where(ii > jj, a * b, 0.0), kk, dm)
    attn = _each(jnp.multiply, qk, dm)
    return dict(ii=ii, jj=jj, beta=beta, rq=rq, rk=rk, yq=yq, kn=kn, qn=qn, dm=dm, eg=eg, ek=ek,
                egl=egl, kb=kb, m=m, attn=attn)


def _unit_lower_inverse_minus_identity(ms, ii, jj):
    pair = (ii >> 1) == (jj >> 1)
    ys = _each(lambda m: -jnp.where(pair, m, 0.0), ms)
    s = 1
    while (1 << s) < CHUNK:
        mask = ((ii >> (s + 1)) == (jj >> (s + 1))) & ((ii >> s) != (jj >> s))
        lbs = _each(lambda m: jnp.where(mask, m, 0.0), ms)
        zs = _each(lambda y, lb: lb + _dot(y, lb), ys, lbs)
        ys = _each(lambda y, z: y - z - _dot(z, y), ys, zs)
        s += 1
    return ys


def _head_offsets(group):
    hs = [group * HEAD_GROUP + a for a in range(HEAD_GROUP)]
    return hs, [pl.ds(pl.multiple_of(base + h * HEAD_DIM, HEAD_DIM), HEAD_DIM)
                for base in (0, DN_WIDTH, 2 * DN_WIDTH) for h in hs]


def _dn_local_fwd(qkv_act, bg, bgt):
    t = qkv_act.shape[0]
    nt = t // CHUNK
    c = CHUNK

    def body(qkv_ref, bg_ref, bgt_ref, u_ref, w_ref, qg_ref, kg_ref, attn_ref, y_ref, egl_ref):
        bgv = bg_ref[...]

        def group(gi, carry):
            hs, offs = _head_offsets(gi)
            qo, ko, vo = offs[0:HEAD_GROUP], offs[HEAD_GROUP:2 * HEAD_GROUP], offs[2 * HEAD_GROUP:]
            qs = [qkv_ref[:, o] for o in qo]
            ks = [qkv_ref[:, o] for o in ko]
            vs = [qkv_ref[:, o] for o in vo]
            g_rows = [bgt_ref[pl.ds(HEADS + h, 1), :] for h in hs]
            ct = _chunk_terms(qs, ks, bgv, g_rows, hs)
            ys = _unit_lower_inverse_minus_identity(ct["m"], ct["ii"], ct["jj"])
            vb = _each(jnp.multiply, vs, ct["beta"])
            kbe = _each(jnp.multiply, ct["kb"], ct["eg"])
            us = _each(lambda a, y: a + _dot(y, a), vb, ys)
            ws = _each(lambda a, y: a + _dot(y, a), kbe, ys)
            for a in range(HEAD_GROUP):
                dst = qo[a]
                u_ref[:, dst] = us[a]
                w_ref[:, dst] = _mx(ws[a])
                qg_ref[:, dst] = _mx(ct["qn"][a] * ct["eg"][a])
                kg_ref[:, dst] = _mx(ct["kn"][a] * ct["ek"][a])
                attn_ref[:, dst] = _mx(ct["attn"][a])
                y_ref[:, dst] = _mx(ys[a])
                egl_ref[0, pl.ds(hs[a], 1), :] = jnp.broadcast_to(ct["egl"][a], (1, HEAD_DIM))
            return carry

        lax.fori_loop(0, HEADS // HEAD_GROUP, group, 0)

    wide = _row(c, DN_WIDTH)
    return _pc(body, "dn_local_fwd", (nt,),
               [_row(c, QKV_WIDTH), _row(c, 128), pl.BlockSpec((2 * HEADS, c), lambda i: (0, i))],
               [wide, wide, wide, wide, wide, wide, pl.BlockSpec((1, HEADS, HEAD_DIM), lambda i: (i, 0, 0))],
               [SDS((t, DN_WIDTH), F32)] + [SDS((t, DN_WIDTH), MXU_DTYPE)] * 5 + [SDS((nt, HEADS, HEAD_DIM), F32)],
               sem=("parallel",))(qkv_act, bg, bgt)


def _dn_scan_fwd(u, w, qg, kg, attn, egl):
    t = u.shape[0]
    nt = t // CHUNK
    c = CHUNK
    sls = [slice(h * HEAD_DIM, (h + 1) * HEAD_DIM) for h in range(HEADS)]

    def body(u_ref, w_ref, qg_ref, kg_ref, attn_ref, egl_ref, o_ref, vn_ref, st_ref, s_ref):
        @pl.when(pl.program_id(0) == 0)
        def _():
            s_ref[...] = jnp.zeros_like(s_ref)

        ss = [s_ref[h] for h in range(HEADS)]
        sb = _each(_mx, ss)
        vn = [u_ref[:, sl] - _dot(w_ref[:, sl], b) for sl, b in zip(sls, sb)]
        vnb = _each(_mx, vn)
        oa = [_dot(qg_ref[:, sl], b) for sl, b in zip(sls, sb)]
        ob = [_dot(attn_ref[:, sl], b) for sl, b in zip(sls, vnb)]
        upd = [_dot_tn(kg_ref[:, sl], b) for sl, b in zip(sls, vnb)]
        for h, sl in enumerate(sls):
            st_ref[0, h] = ss[h]
            vn_ref[:, sl] = vnb[h]
            o_ref[:, sl] = oa[h] + ob[h]
            s_ref[h] = ss[h] * egl_ref[0, h:h + 1, :] + upd[h]

    wide = _row(c, DN_WIDTH)
    return _pc(body, "dn_scan_fwd", (nt,),
               [wide] * 5 + [pl.BlockSpec((1, HEADS, HEAD_DIM), lambda i: (i, 0, 0))],
               [wide, wide, pl.BlockSpec((1, HEADS, HEAD_DIM, HEAD_DIM), lambda i: (i, 0, 0, 0))],
               [SDS((t, DN_WIDTH), F32), SDS((t, DN_WIDTH), MXU_DTYPE), SDS((nt, HEADS, HEAD_DIM, HEAD_DIM), F32)],
               scratch=[pltpu.VMEM((HEADS, HEAD_DIM, HEAD_DIM), F32)], sem=("arbitrary",))(u, w, qg, kg, attn, egl)


def _dn_scan_bwd(do, qg, kg, w, attn, vn, states, egl):
    t = do.shape[0]
    nt = t // CHUNK
    c = CHUNK
    sls = [slice(h * HEAD_DIM, (h + 1) * HEAD_DIM) for h in range(HEADS)]

    def body(do_ref, qg_ref, kg_ref, w_ref, attn_ref, vn_ref, st_ref, egl_ref,
             dvn_ref, dkg_ref, dqg_ref, dattn_ref, dw_ref, degl_ref, ds_ref):
        @pl.when(pl.program_id(0) == 0)
        def _():
            ds_ref[...] = jnp.zeros_like(ds_ref)

        dsp = [ds_ref[h] for h in range(HEADS)]
        dsb = _each(_mx, dsp)
        ss = [st_ref[0, h] for h in range(HEADS)]
        sb = _each(_mx, ss)
        dvn = [_dot(kg_ref[:, sl], b) + _dot_tn(attn_ref[:, sl], do_ref[:, sl]) for sl, b in zip(sls, dsb)]
        dvnb = _each(_mx, dvn)
        dkg = [_dot_nt(vn_ref[:, sl], b) for sl, b in zip(sls, dsb)]
        dqg = [_dot_nt(do_ref[:, sl], b) for sl, b in zip(sls, sb)]
        dattn = [_dot_nt(do_ref[:, sl], vn_ref[:, sl]) for sl in sls]
        dwv = [-_dot_nt(a, b) for a, b in zip(dvnb, sb)]
        upd = [_dot_tn(qg_ref[:, sl], do_ref[:, sl]) - _dot_tn(w_ref[:, sl], a) for sl, a in zip(sls, dvnb)]
        for h, sl in enumerate(sls):
            dvn_ref[:, sl] = dvn[h]
            dkg_ref[:, sl] = dkg[h]
            dqg_ref[:, sl] = dqg[h]
            dattn_ref[:, sl] = dattn[h]
            dw_ref[:, sl] = dwv[h]
            degl = jnp.sum(_rowsum(ss[h] * dsp[h]), axis=0, keepdims=True)
            degl_ref[0, h:h + 1, :] = jnp.broadcast_to(degl, (1, HEAD_DIM))
            ds_ref[h] = dsp[h] * egl_ref[0, h:h + 1, :] + upd[h]

    rev = pl.BlockSpec((c, DN_WIDTH), lambda i: (nt - 1 - i, 0))
    rev3 = pl.BlockSpec((1, HEADS, HEAD_DIM), lambda i: (nt - 1 - i, 0, 0))
    rev4 = pl.BlockSpec((1, HEADS, HEAD_DIM, HEAD_DIM), lambda i: (nt - 1 - i, 0, 0, 0))
    return _pc(body, "dn_scan_bwd", (nt,), [rev] * 6 + [rev4, rev3], [rev] * 5 + [rev3],
               [SDS((t, DN_WIDTH), F32)] * 5 + [SDS((nt, HEADS, HEAD_DIM), F32)],
               scratch=[pltpu.VMEM((HEADS, HEAD_DIM, HEAD_DIM), F32)],
               sem=("arbitrary",))(do, qg, kg, w, attn, vn, states, egl)


def _dn_local_bwd(qkv_act, bg, bgt, u, w, ymat, dvn, dw, dqg, dkg, dattn, degl):
    t = qkv_act.shape[0]
    nt = t // CHUNK
    c = CHUNK

    def body(qkv_ref, bg_ref, bgt_ref, u_ref, w_ref, y_ref, du_ref, dw_ref, dqg_ref, dkg_ref, dattn_ref,
             degl_ref, dqkv_ref, dbg_ref):
        bgv = bg_ref[...]
        lane = _lane(bgv.shape)
        rowi = lax.broadcasted_iota(jnp.int32, (c, 1), 0)

        def group(gi, dbg):
            hs, offs = _head_offsets(gi)
            qo, ko, vo = offs[0:HEAD_GROUP], offs[HEAD_GROUP:2 * HEAD_GROUP], offs[2 * HEAD_GROUP:]
            qs = [qkv_ref[:, o] for o in qo]
            ks = [qkv_ref[:, o] for o in ko]
            vs = [qkv_ref[:, o] for o in vo]
            g_rows = [bgt_ref[pl.ds(HEADS + h, 1), :] for h in hs]
            ct = _chunk_terms(qs, ks, bgv, g_rows, hs)
            ii, jj = ct["ii"], ct["jj"]
            beta, eg, ek, kb, kn, qn, dm = ct["beta"], ct["eg"], ct["ek"], ct["kb"], ct["kn"], ct["qn"], ct["dm"]
            ys = [y_ref[:, o] for o in qo]
            du = [du_ref[:, o] for o in qo]
            dwv = [dw_ref[:, o] for o in qo]
            dqg_v = [dqg_ref[:, o] for o in qo]
            dkg_v = [dkg_ref[:, o] for o in qo]
            dattn_v = [dattn_ref[:, o] for o in qo]
            degl_v = [jnp.max(degl_ref[0, pl.ds(h, 1), :], axis=1, keepdims=True) for h in hs]
            dvb = _each(lambda a, y: a + _dot_tn(y, a), du, ys)
            dkbe = _each(lambda a, y: a + _dot_tn(y, a), dwv, ys)
            dm_u = [_dot_nt(a, u_ref[:, o]) for a, o in zip(dvb, qo)]
            dm_w = [_dot_nt(a, w_ref[:, o]) for a, o in zip(dkbe, qo)]
            dms = _each(lambda a, b: jnp.where(ii > jj, -(a + b), 0.0), dm_u, dm_w)
            dkk = _each(jnp.multiply, dms, dm)
            dqk = _each(jnp.multiply, dattn_v, dm)
            gmat = _each(lambda a, b, c_, d: a * b + c_ * d, dms, ct["m"], dattn_v, ct["attn"])
            dkb = _each(lambda a, b, c_, d: _dot(a, b) + c_ * d, dkk, kn, dkbe, eg)
            dk1 = _each(_dot_tn, dkk, kb)
            dk2 = _each(_dot_tn, dqk, qn)
            dq1 = _each(_dot, dqk, kn)
            dk = _each(lambda a, b, c_, d: a + b + c_ * d, dk1, dk2, dkg_v, ek)
            dq = _each(lambda a, b, c_: a + b * c_, dq1, dqg_v, eg)
            deg = _each(lambda a, b, c_, d: _rowsum(a * b) + _rowsum(c_ * d), dqg_v, qn, dkbe, kb)
            dek = _each(lambda a, b: _rowsum(a * b), dkg_v, kn)
            dgl = _each(lambda a, b, c_, d: jnp.sum(a * b, axis=0, keepdims=True) + c_ * d, dek, ek, degl_v, ct["egl"])
            cs_row = _each(lambda g: jnp.sum(g, axis=0, keepdims=True), gmat)
            cs_col = _each(lambda r: _rowsum(jnp.where(ii == jj, r, 0.0)), cs_row)
            dgc = _each(lambda a, b, c_, d, g, e, f: a * b - c_ * d + _rowsum(g) - e + jnp.where(rowi == c - 1, f, 0.0),
                        deg, eg, dek, ek, gmat, cs_col, dgl)
            dgc_row = _each(lambda a: jnp.sum(jnp.where(ii == jj, a, 0.0), axis=0, keepdims=True), dgc)
            dg = _each(lambda r: _rowsum(jnp.where(jj >= ii, r, 0.0)), dgc_row)
            dbeta = _each(lambda a, b, c_, d: _rowsum(a * b) + _rowsum(c_ * d), dkb, kn, dvb, vs)
            dk = _each(lambda a, b, c_: a + b * c_, dk, dkb, beta)
            for a in range(HEAD_GROUP):
                dyq = dq[a] * Q_SCALE
                yq = ct["yq"][a]
                dqkv_ref[:, qo[a]] = ct["rq"][a] * (dyq - yq * _rowsum(yq * dyq))
                dqkv_ref[:, ko[a]] = ct["rk"][a] * (dk[a] - kn[a] * _rowsum(kn[a] * dk[a]))
                dqkv_ref[:, vo[a]] = dvb[a] * beta[a]
                dbg = dbg + jnp.where(lane == hs[a], dbeta[a], 0.0) + jnp.where(lane == HEADS + hs[a], dg[a], 0.0)
            return dbg

        dbg_ref[...] = lax.fori_loop(0, HEADS // HEAD_GROUP, group, jnp.zeros((c, 128), F32))

    wide = _row(c, DN_WIDTH)
    sc3 = pl.BlockSpec((1, HEADS, HEAD_DIM), lambda i: (i, 0, 0))
    return _pc(body, "dn_local_bwd", (nt,),
               [_row(c, QKV_WIDTH), _row(c, 128), pl.BlockSpec((2 * HEADS, c), lambda i: (0, i))] + [wide] * 8 + [sc3],
               [_row(c, QKV_WIDTH), _row(c, 128)], [SDS((t, QKV_WIDTH), F32), SDS((t, 128), F32)],
               sem=("parallel",))(qkv_act, bg, bgt, u, w, ymat, dvn, dw, dqg, dkg, dattn, degl)


MIX_ROWS = 64


def _mix_fwd(o, proj, ypre, pool_scale, wo_row, tm):
    t = o.shape[0]

    def body(o_ref, z_ref, ga_ref, gb_ref, yp_ref, ps_ref, wo_ref, mixed_ref):
        for r in range(0, tm, MIX_ROWS):
            rows = pl.ds(r, MIX_ROWS)
            for h in range(HEADS):
                sl = slice(h * HEAD_DIM, (h + 1) * HEAD_DIM)
                oh = o_ref[rows, sl]
                on = oh * lax.rsqrt(jnp.mean(oh * oh, axis=1, keepdims=True) + RMS_EPS)
                zh = z_ref[rows, sl]
                yb = on * wo_ref[:, sl] * (zh * _sigmoid(zh))
                ya = yp_ref[rows, sl] * ps_ref[:, sl]
                mixed_ref[rows, sl] = _mx(_sigmoid(ga_ref[rows, sl]) * ya + _sigmoid(gb_ref[rows, sl]) * yb)

    def col(blk):
        return pl.BlockSpec((tm, D_MODEL), lambda i: (i, blk))

    return _pc(body, "mix_fwd", (t // tm,),
               [_row(tm, D_MODEL), col(K_Z // D_MODEL), col(K_GA // D_MODEL), col(K_GB // D_MODEL), _row(tm, D_MODEL),
                _const((1, D_MODEL)), _const((1, D_MODEL))],
               _row(tm, D_MODEL), SDS((t, D_MODEL), MXU_DTYPE), sem=("parallel",))(
                   o, proj, proj, proj, ypre, pool_scale, wo_row)


def _mix_bwd(da1_bf, w_out, o, proj, ypre, pool_scale, wo_row, tm, after):
    t = o.shape[0]

    def body(da_ref, wout_ref, o_ref, z_ref, ga_ref, gb_ref, yp_ref, ps_ref, wo_ref, after_ref,
             do_ref, dp_ref, dyp_ref, acc_ref, dm_ref):
        i = pl.program_id(0)

        @pl.when(i == 0)
        def _():
            acc_ref[...] = jnp.zeros_like(acc_ref)

        dm_ref[...] = _dot_nt(da_ref[...], wout_ref[...])
        dwo = jnp.zeros((1, HEAD_DIM), F32)
        for h in range(HEADS):
            sl = slice(h * HEAD_DIM, (h + 1) * HEAD_DIM)
            woh = wo_ref[:, sl]
            psh = ps_ref[:, sl]
            dps = jnp.zeros((1, HEAD_DIM), F32)
            for r in range(0, tm, MIX_ROWS):
                rows = pl.ds(r, MIX_ROWS)
                oh = o_ref[rows, sl]
                rs = lax.rsqrt(jnp.mean(oh * oh, axis=1, keepdims=True) + RMS_EPS)
                on = oh * rs
                zh = z_ref[rows, sl]
                sz = _sigmoid(zh)
                silu = zh * sz
                t1 = on * woh
                yb = t1 * silu
                sa = _sigmoid(ga_ref[rows, sl])
                sb = _sigmoid(gb_ref[rows, sl])
                yp = yp_ref[rows, sl]
                dm = dm_ref[rows, sl]
                ga_sl = slice(D_MODEL + h * HEAD_DIM, D_MODEL + (h + 1) * HEAD_DIM)
                gb_sl = slice(2 * D_MODEL + h * HEAD_DIM, 2 * D_MODEL + (h + 1) * HEAD_DIM)
                dp_ref[rows, ga_sl] = _mx(dm * (yp * psh) * sa * (1.0 - sa))
                dp_ref[rows, gb_sl] = _mx(dm * yb * sb * (1.0 - sb))
                dya = dm * sa
                dyb = dm * sb
                dyp_ref[rows, sl] = _mx(dya * psh)
                dps = dps + jnp.sum(dya * yp, axis=0, keepdims=True)
                dp_ref[rows, sl] = _mx(dyb * t1 * (sz * (1.0 + zh * (1.0 - sz))))
                dt1 = dyb * silu
                dwo = dwo + jnp.sum(dt1 * on, axis=0, keepdims=True)
                don = dt1 * woh
                do_ref[rows, sl] = _mx(rs * (don - on * jnp.mean(don * on, axis=1, keepdims=True)))
            acc_ref[0:1, sl] += dps
        acc_ref[1:2, 0:HEAD_DIM] += dwo

    def col(blk):
        return pl.BlockSpec((tm, D_MODEL), lambda i: (i, blk))

    r = _row(tm, D_MODEL)
    return _pc(body, "mix_bwd", (t // tm,),
               [r, _const((D_MODEL, D_MODEL)), r, col(K_Z // D_MODEL), col(K_GA // D_MODEL), col(K_GB // D_MODEL), r,
                _const((1, D_MODEL)), _const((1, D_MODEL)), ANY],
               [r, pl.BlockSpec((tm, 3 * D_MODEL), lambda i: (i, K_Z // (3 * D_MODEL))), r, _const((8, D_MODEL))],
               [SDS((t, D_MODEL), MXU_DTYPE), SDS((t, CAT_WIDTH), MXU_DTYPE), SDS((t, D_MODEL), MXU_DTYPE),
                SDS((8, D_MODEL), F32)],
               scratch=[pltpu.VMEM((tm, D_MODEL), F32)],
               sem=("arbitrary",))(da1_bf, w_out, o, proj, proj, proj, ypre, pool_scale, wo_row, after)


def _oproj_ln1(mixed, w_out, h0, g1, b1, tm):
    t = mixed.shape[0]

    def body(m_ref, w_ref, h0_ref, g_ref, b_ref, a1_ref, h1_ref, h1b_ref):
        a1 = ALPHA * h0_ref[...] + _dot(m_ref[...], w_ref[...])
        a1_ref[...] = a1
        xhat, _ = _ln_stats(a1)
        h1 = xhat * g_ref[...] + b_ref[...]
        h1_ref[...] = h1
        h1b_ref[...] = _mx(h1)

    r = _row(tm, D_MODEL)
    v = _const((1, D_MODEL))
    return _pc(body, "oproj_ln1", (t // tm,), [r, _const((D_MODEL, D_MODEL)), r, v, v], [r, r, r],
               [SDS((t, D_MODEL), F32), SDS((t, D_MODEL), F32), SDS((t, D_MODEL), MXU_DTYPE)],
               sem=("parallel",))(mixed, w_out, h0, g1, b1)


def _mlp_up(h1_bf, w_up, tm):
    t = h1_bf.shape[0]
    tn = w_up.shape[2]

    def body(h_ref, w_ref, act_ref):
        r = jnp.maximum(_dot(h_ref[...], w_ref[...]), 0.0)
        act_ref[...] = _mx(r * r)

    return _pc(body, "mlp_up", (D_FF // tn, t // tm),
               [pl.BlockSpec((tm, D_MODEL), lambda j, i: (i, 0)),
                pl.BlockSpec((None, D_MODEL, tn), lambda j, i: (j, 0, 0))],
               pl.BlockSpec((tm, tn), lambda j, i: (i, j)), SDS((t, D_FF), MXU_DTYPE),
               sem=("parallel", "parallel"))(h1_bf, w_up)


def _tail(act, w_down, h1, w_gate, p_bf, w_proj, tgt, g2, b2, tm):
    t = act.shape[0]

    def body(act_ref, wd_ref, h1_ref, wg_ref, p_ref, wp_ref, tgt_ref, g_ref, b_ref,
             dr_ref, drb_ref, dgp_ref, dpp_ref, rb_ref, acc_ref):
        i = pl.program_id(0)

        @pl.when(i == 0)
        def _():
            acc_ref[...] = jnp.zeros_like(acc_ref)

        r = ALPHA * h1_ref[...] + _dot(act_ref[...], wd_ref[...])
        rb = _mx(r)
        rb_ref[...] = rb
        gate = _sigmoid(_dot(rb, wg_ref[...]))
        pp = _dot(p_ref[...], wp_ref[...])
        xhat, rstd = _ln_stats(r + gate * pp)
        g = g_ref[...]
        diff = xhat * g + b_ref[...] - tgt_ref[...]
        dh2 = diff * (1.0 / D_MODEL)
        rowloss = jnp.sum(diff * diff, axis=1, keepdims=True) * (0.5 / D_MODEL)
        acc_ref[0:1, :] += jnp.sum(dh2 * xhat, axis=0, keepdims=True)
        acc_ref[1:2, :] += jnp.sum(dh2, axis=0, keepdims=True)
        acc_ref[2:3, :] += jnp.broadcast_to(jnp.sum(rowloss, axis=0, keepdims=True), (1, D_MODEL))
        da2 = _ln_bwd(dh2, xhat, rstd, g)
        dpp_ref[...] = _mx(da2 * gate)
        dgp = _mx(da2 * pp * gate * (1.0 - gate))
        dgp_ref[...] = dgp
        dr = da2 + _dot_nt(dgp, wg_ref[...])
        dr_ref[...] = dr
        drb_ref[...] = _mx(dr)

    r = _row(tm, D_MODEL)
    v = _const((1, D_MODEL))
    return _pc(body, "tail", (t // tm,),
               [_row(tm, D_FF), _const((D_FF, D_MODEL)), r, _const((D_MODEL, D_MODEL)), _row(tm, PLE_DIM),
                _const((PLE_DIM, D_MODEL)), r, v, v],
               [r, r, r, r, r, _const((8, D_MODEL))],
               [SDS((t, D_MODEL), F32)] + [SDS((t, D_MODEL), MXU_DTYPE)] * 4 + [SDS((8, D_MODEL), F32)],
               sem=("arbitrary",))(act, w_down, h1, w_gate, p_bf, w_proj, tgt, g2, b2)


def _mlp_bwd1(dr_bf, w_down, act, tm, tn):
    t = act.shape[0]

    def body(dr_ref, w_ref, act_ref, dup_ref):
        dact = _dot_nt(dr_ref[...], w_ref[...])
        dup_ref[...] = _mx(dact * (2.0 * jnp.sqrt(act_ref[...].astype(F32))))

    o = pl.BlockSpec((tm, tn), lambda j, i: (i, j))
    return _pc(body, "mlp_bwd1", (D_FF // tn, t // tm),
               [pl.BlockSpec((tm, D_MODEL), lambda j, i: (i, 0)), pl.BlockSpec((tn, D_MODEL), lambda j, i: (j, 0)), o],
               o, SDS((t, D_FF), MXU_DTYPE), sem=("parallel", "parallel"))(dr_bf, w_down, act)


def _mlp_bwd2(dup, w_up, dr, a1, g1, tm):
    t = dr.shape[0]

    nk, tk = w_up.shape[0], w_up.shape[2]

    def body(dup_ref, w_ref, dr_ref, a1_ref, g_ref, da1_ref, da1b_ref, acc_ref):
        i = pl.program_id(0)

        @pl.when(i == 0)
        def _():
            acc_ref[...] = jnp.zeros_like(acc_ref)

        dh1 = ALPHA * dr_ref[...]
        for kk in range(nk):
            dh1 = dh1 + _dot_nt(dup_ref[:, kk * tk:(kk + 1) * tk], w_ref[kk])
        xhat, rstd = _ln_stats(a1_ref[...])
        acc_ref[0:1, :] += jnp.sum(dh1 * xhat, axis=0, keepdims=True)
        acc_ref[1:2, :] += jnp.sum(dh1, axis=0, keepdims=True)
        da1 = _ln_bwd(dh1, xhat, rstd, g_ref[...])
        da1_ref[...] = da1
        da1b_ref[...] = _mx(da1)

    r = _row(tm, D_MODEL)
    return _pc(body, "mlp_bwd2", (t // tm,),
               [_row(tm, D_FF), _const((nk, D_MODEL, tk)), r, r, _const((1, D_MODEL))],
               [r, r, _const((8, D_MODEL))],
               [SDS((t, D_MODEL), F32), SDS((t, D_MODEL), MXU_DTYPE), SDS((8, D_MODEL), F32)],
               sem=("arbitrary",))(dup, w_up, dr, a1, g1)


def _ln_in_bwd(dproj, w_cat, da1, x, g, tm, after):
    t = x.shape[0]

    def body(dp_ref, w_ref, da1_ref, x_ref, g_ref, after_ref, dx_ref, acc_ref):
        i = pl.program_id(0)

        @pl.when(i == 0)
        def _():
            acc_ref[...] = jnp.zeros_like(acc_ref)

        dh0 = _dot_nt(dp_ref[...], w_ref[...]) + ALPHA * da1_ref[...]
        xhat, rstd = _ln_stats(x_ref[...])
        acc_ref[0:1, :] += jnp.sum(dh0 * xhat, axis=0, keepdims=True)
        acc_ref[1:2, :] += jnp.sum(dh0, axis=0, keepdims=True)
        dx_ref[...] = _ln_bwd(dh0, xhat, rstd, g_ref[...])

    r = _row(tm, D_MODEL)
    return _pc(body, "ln_in_bwd", (t // tm,),
               [_row(tm, CAT_WIDTH), _const((D_MODEL, CAT_WIDTH)), r, r, _const((1, D_MODEL)), ANY],
               [r, _const((8, D_MODEL))], [SDS((t, D_MODEL), F32), SDS((8, D_MODEL), F32)],
               sem=("arbitrary",))(dproj, w_cat, da1, x, g, after)


def _local_step(x, p, tgt, wts, start_token, late_weights, send_late_grads, send_early_grads):
    t = x.shape[0]
    tm = min(512, t)
    tms = min(256, t)
    row = lambda a: a.reshape(1, -1)
    w_cat = wts["w_cat"]
    pool_scale = row(wts["pool_scale"])
    wo_row = jnp.tile(row(wts["o_norm_w"]), (1, HEADS))
    pad8 = jnp.zeros((1, HEADS), F32)
    al_row = jnp.concatenate([pad8, row(wts["a_log"]), jnp.zeros((1, 128 - 2 * HEADS), F32)], axis=1)
    dtb_row = jnp.concatenate([pad8, row(wts["dt_bias"]), jnp.zeros((1, 128 - 2 * HEADS), F32)], axis=1)
    g_in, b_in = row(wts["ln_in_g"]), row(wts["ln_in_b"])
    g1, b1 = row(wts["ln1_g"]), row(wts["ln1_b"])
    g2, b2 = row(wts["ln2_g"]), row(wts["ln2_b"])

    h0, h0_bf = _ln_in(x, g_in, b_in, tm, start_token)
    proj = _proj(h0_bf, w_cat, tms)
    ypre, d_bf = _pool_fwd(proj, wts["pool_w"], tm)
    qkv_act = _conv_fwd(proj, wts["conv_w"], tm)
    bg = _ba_fwd(proj, al_row, dtb_row, tm)
    bgt = bg[:, :2 * HEADS].T
    u, w, qg, kg, attn, ymat, egl = _dn_local_fwd(qkv_act, bg, bgt)
    o, vn, states = _dn_scan_fwd(u, w, qg, kg, attn, egl)
    mixed = _mix_fwd(o, proj, ypre, pool_scale, wo_row, tm)
    wts = {**wts, **late_weights(mixed)}
    a1, h1, h1_bf = _oproj_ln1(mixed, wts["w_out"], h0, g1, b1, tm)
    act = _mlp_up(h1_bf, wts["w_up"], tm)
    p_bf = _mx(p)
    dr, dr_bf, dgp, dpp, r_bf, acc_tail = _tail(act, wts["w_down"], h1, wts["ple_gate_w"], p_bf, wts["ple_proj_w"],
                                                tgt, g2, b2, tms)
    grads = {}
    grads["ple_proj_w"] = _matmul(p_bf, dpp, "tn", "dw_ple_proj", WIRE_DTYPE, tm=256, tn=1024, tk=DW_TK)
    grads["ple_gate_w"] = _matmul(r_bf, dgp, "tn", "dw_ple_gate", WIRE_DTYPE, tm=DW_TM, tn=1024, tk=DW_TK)
    grads["w_down"] = _matmul(act, dr_bf, "tn", "dw_down", WIRE_DTYPE, tm=DW_TM, tn=1024, tk=DW_TK)
    dup = _mlp_bwd1(dr_bf, wts["w_down"], act, tm, 1024)
    grads["w_up"] = _matmul(h1_bf, dup, "tn", "dw_up", WIRE_DTYPE, tm=DW_TM, tn=1024, tk=DW_TK, stack_out=True)
    da1, da1_bf, acc_ln1 = _mlp_bwd2(dup, wts["w_up"], dr, a1, g1, tms)
    grads["w_out"] = _matmul(mixed, da1_bf, "tn", "dw_out", WIRE_DTYPE, tm=DW_TM, tn=1024, tk=DW_TK)
    sent = send_late_grads(grads)
    do, dproj, dyp, acc_mix = _mix_bwd(da1_bf, wts["w_out"], o, proj, ypre, pool_scale, wo_row, tms, sent)
    dproj, grads["pool_w"] = _pool_bwd(dyp, d_bf, wts["pool_w"], dproj, tm)
    dvn, dkg, dqg, dattn, dw, degl = _dn_scan_bwd(do, qg, kg, w, attn, vn, states, egl)
    dqkv_act, dbg = _dn_local_bwd(qkv_act, bg, bgt, u, w, ymat, dvn, dw, dqg, dkg, dattn, degl)
    dproj, acc_conv = _conv_bwd(dqkv_act, proj, wts["conv_w"], dproj, tm)
    dproj, acc_ba = _ba_bwd(dbg, bg, proj, al_row, dtb_row, dproj, tm)
    dw_cat = _matmul(h0_bf, dproj, "tn", "dw_in", F32, tm=DW_TM, tn=1152, tk=DW_TK)
    grads["w_in"] = jnp.concatenate(
        [dw_cat[:, K_U:K_U + 512], dw_cat[:, K_QKV:K_QKV + 3072], dw_cat[:, K_Z:K_Z + 1024],
         dw_cat[:, K_BA:K_BA + 16], dw_cat[:, K_GA:K_GA + 1024], dw_cat[:, K_GB:K_GB + 1024]], axis=1)
    sent = send_early_grads(grads)
    grad_x, acc_in = _ln_in_bwd(dproj, w_cat, da1, x, g_in, tms, sent)

    grads["conv_w"] = acc_conv[0:CONV_K]
    grads["ln_in_g"], grads["ln_in_b"] = acc_in[0], acc_in[1]
    grads["ln1_g"], grads["ln1_b"] = acc_ln1[0], acc_ln1[1]
    grads["ln2_g"], grads["ln2_b"] = acc_tail[0], acc_tail[1]
    grads["pool_scale"] = acc_mix[0]
    grads["o_norm_w"] = acc_mix[1, 0:HEAD_DIM]
    grads["a_log"] = acc_ba[0, HEADS:2 * HEADS]
    grads["dt_bias"] = acc_ba[1, HEADS:2 * HEADS]
    loss = acc_tail[2, 0]
    return grad_x, grads, loss


MESH = pl.DeviceIdType.MESH
ANY = pl.BlockSpec(memory_space=pl.ANY)


def _chip_of(k, x, y):
    chip = (2 * x + y + k) % N_CHIPS
    return chip // 2, chip % 2


def _place():
    x, y, c = lax.axis_index("x"), lax.axis_index("y"), lax.axis_index("c")
    return x, y, c, 2 * x + y


def _half(rows, c):
    return pl.ds(pl.multiple_of(c * (rows // 2), 16), rows // 2)


def _remote(src, dst, send_sem, recv_sem, device_id):
    return pltpu.make_async_remote_copy(src_ref=src, dst_ref=dst, send_sem=send_sem, recv_sem=recv_sem,
                                        device_id=device_id, device_id_type=MESH)


def _tile_rows(rows):
    for tr in (256, 128, 64, 32, 16):
        if rows % tr == 0:
            return tr
    raise ValueError(rows)


def _gather_weights(shards, conv_shard):
    n = len(shards)

    def body(*refs):
        ins, conv_in = refs[0:n], refs[n]
        outs, conv_out = refs[n + 1:2 * n + 1], refs[2 * n + 1]
        send, recv, fsend, frecv, csend, crecv, lsend, lrecv = refs[2 * n + 2:]
        x, y, c, me = _place()
        here, sib = (x, y, c), (x, y, 1 - c)
        own = [_remote(ins[a], outs[a].at[me], lsend.at[a], lrecv.at[a], sib) for a in range(n)]
        own.append(_remote(conv_in, conv_out.at[me], lsend.at[n], lrecv.at[n], sib))
        for cp in own:
            cp.start()
        sends = []
        for k in range(1, N_CHIPS):
            tx, ty = _chip_of(k, x, y)
            for a in range(n):
                mine = _half(shards[a].shape[0], c)
                sends.append(_remote(ins[a].at[mine], outs[a].at[me, mine], send.at[a * N_CHIPS + k],
                                     recv.at[a * N_CHIPS + k], (tx, ty, c)))
            sends.append(_remote(conv_in, conv_out.at[me], csend.at[k], crecv.at[k], (tx, ty, c)))
        for cp in sends:
            cp.start()
        forwards = []
        for k in range(1, N_CHIPS):
            src = (me + N_CHIPS - k) % N_CHIPS
            for a in range(n):
                landed = outs[a].at[src, _half(shards[a].shape[0], c)]
                _remote(landed, landed, send.at[a * N_CHIPS + k], recv.at[a * N_CHIPS + k], here).wait_recv()
                fwd = _remote(landed, landed, fsend.at[a * N_CHIPS + k], frecv.at[a * N_CHIPS + k], sib)
                fwd.start()
                forwards.append(fwd)
            _remote(conv_in, conv_out.at[src], csend.at[k], crecv.at[k], here).wait_recv()
        for k in range(1, N_CHIPS):
            src = (me + N_CHIPS - k) % N_CHIPS
            for a in range(n):
                passed = outs[a].at[src, _half(shards[a].shape[0], 1 - c)]
                _remote(passed, passed, fsend.at[a * N_CHIPS + k], frecv.at[a * N_CHIPS + k], here).wait_recv()
        for cp in sends + forwards:
            cp.wait_send()
        for cp in own:
            cp.wait()

    sems = pltpu.SemaphoreType.DMA((n * N_CHIPS,))
    return pl.pallas_call(
        body, name="gather_weights",
        out_shape=[SDS((N_CHIPS,) + s.shape, s.dtype) for s in shards]
        + [SDS((N_CHIPS,) + conv_shard.shape, conv_shard.dtype)],
        in_specs=[ANY] * (n + 1), out_specs=[ANY] * (n + 1),
        scratch_shapes=[sems, sems, sems, sems, pltpu.SemaphoreType.DMA((N_CHIPS,)),
                        pltpu.SemaphoreType.DMA((N_CHIPS,)), pltpu.SemaphoreType.DMA((n + 1,)),
                        pltpu.SemaphoreType.DMA((n + 1,))],
    )(*shards, conv_shard)


def _swap_halves(gs):
    n = len(gs)

    def body(*refs):
        ins, theirs = refs[0:n], refs[n:2 * n]
        send, recv = refs[2 * n:]
        x, y, c, _ = _place()
        copies = [_remote(ins[a].at[:, _half(gs[a].shape[1], 1 - c)], theirs[a], send.at[a], recv.at[a],
                          (x, y, 1 - c)) for a in range(n)]
        for cp in copies:
            cp.start()
        for cp in copies:
            cp.wait()

    return pl.pallas_call(
        body, name="swap_halves", out_shape=[SDS((N_CHIPS, g.shape[1] // 2, g.shape[2]), g.dtype) for g in gs],
        in_specs=[ANY] * n, out_specs=[ANY] * n, scratch_shapes=[pltpu.SemaphoreType.DMA((n,))] * 2,
    )(*gs)


def _send_to_sibling(hs):
    n = len(hs)

    def body(*refs):
        ins, outs = refs[0:n], refs[n:2 * n]
        send, recv = refs[2 * n:]
        x, y, c, _ = _place()
        copies = [_remote(ins[a], outs[a], send.at[a], recv.at[a], (x, y, 1 - c)) for a in range(n)]
        for cp in copies:
            cp.start()
        for cp in copies:
            cp.wait()

    return pl.pallas_call(
        body, name="send_to_sibling", out_shape=[SDS(h.shape, h.dtype) for h in hs],
        in_specs=[ANY] * n, out_specs=[ANY] * n, scratch_shapes=[pltpu.SemaphoreType.DMA((n,))] * 2,
    )(*hs)


HBM = pl.BlockSpec(memory_space=pltpu.HBM)
SEM = pl.BlockSpec(memory_space=pltpu.SEMAPHORE)
EFFECT = pltpu.SideEffectType.DATAFLOW_SIDE_EFFECTING


def _in_hbm(a):
    return pltpu.with_memory_space_constraint(a, pltpu.HBM)


def _split_copy_start(name, srcs, lands, copies_of, after):
    n = len(srcs)
    n_copies = len(copies_of(srcs, lands, None, None, None))

    def body(*refs):
        src_refs, land_refs = refs[0:n], refs[n:2 * n]
        send, recv = refs[2 * n + 1], refs[2 * n + 2]
        token = refs[-1]
        for cp in copies_of(src_refs, land_refs, send, recv, _place()):
            cp.start()
        token[...] = jnp.zeros_like(token)

    sems = pltpu.SemaphoreType.DMA((n_copies,))
    out = pl.pallas_call(
        body, name=name,
        out_shape=[sems, sems] + [pltpu.HBM(a.shape, a.dtype) for a in list(srcs) + list(lands)] + [SDS((8, 128), F32)],
        in_specs=[HBM] * (2 * n) + [ANY],
        out_specs=[SEM, SEM] + [HBM] * (2 * n) + [pl.BlockSpec(memory_space=pltpu.VMEM)],
        input_output_aliases={i: 2 + i for i in range(2 * n)},
        compiler_params=pltpu.CompilerParams(has_side_effects=EFFECT),
    )(*[_in_hbm(a) for a in list(srcs) + list(lands)], after)
    return out[0], out[1], out[2:2 + n], out[2 + n:2 + 2 * n], out[-1]


def _split_copy_wait(name, send, recv, srcs, lands, after, copies_of):
    n = len(srcs)

    def body(*refs):
        src_refs, land_refs = refs[0:n], refs[n:2 * n]
        send_ref, recv_ref = refs[2 * n], refs[2 * n + 1]
        for cp in copies_of(src_refs, land_refs, send_ref, recv_ref, _place()):
            cp.wait_send()
            cp.wait_recv()

    out = pl.pallas_call(
        body, name=name, out_shape=[pltpu.HBM(a.shape, a.dtype) for a in list(srcs) + list(lands)],
        in_specs=[HBM] * (2 * n) + [SEM, SEM, ANY], out_specs=[HBM] * (2 * n),
        input_output_aliases={i: i for i in range(2 * n)},
        compiler_params=pltpu.CompilerParams(has_side_effects=EFFECT),
    )(*srcs, *lands, send, recv, after)
    return out[0:n], out[n:2 * n]


def _late_gather_copies(srcs, lands, send, recv, place):
    copies = []
    for a in range(len(srcs)):
        for k in range(N_CHIPS):
            if place is None:
                copies.append(None)
                continue
            x, y, c, me = place
            if k == 0:
                target = (x, y, 1 - c)
            else:
                tx, ty = _chip_of(k, x, y)
                target = (tx, ty, c)
            copies.append(_remote(srcs[a], lands[a].at[me], send.at[a * N_CHIPS + k], recv.at[a * N_CHIPS + k], target))
    return copies


def _late_scatter_copies(srcs, lands, send, recv, place):
    copies = []
    for a in range(len(srcs)):
        for k in range(1, N_CHIPS):
            if place is None:
                copies.append(None)
                continue
            x, y, c, _ = place
            tx, ty = _chip_of(k, x, y)
            copies.append(_remote(srcs[a].at[2 * tx + ty], lands[a].at[k - 1], send.at[a * (N_CHIPS - 1) + k - 1],
                                  recv.at[a * (N_CHIPS - 1) + k - 1], (tx, ty, c)))
    return copies


def _add_pair(g, theirs, name):
    _, rows, cols = g.shape
    half = rows // 2
    tr = _tile_rows(half)

    def body(g_ref, t_ref, o_ref):
        own = g_ref[lax.axis_index("c")]
        o_ref[...] = (own.astype(F32) + t_ref[...].astype(F32)).astype(o_ref.dtype)

    blk = pl.BlockSpec((None, tr, cols), lambda j, i: (j, i, 0))
    return _pc(body, "add_" + name, (N_CHIPS, half // tr),
               [pl.BlockSpec((None, 2, tr, cols), lambda j, i: (j, 0, i, 0)), blk], blk,
               SDS((N_CHIPS, half, cols), g.dtype), sem=("parallel", "parallel"))(
                   g.reshape(N_CHIPS, 2, half, cols), theirs)


def _sum_slabs(pair, landed, name):
    _, rows, cols = pair.shape
    tr = _tile_rows(rows)

    def body(p_ref, r_ref, o_ref):
        acc = p_ref[2 * lax.axis_index("x") + lax.axis_index("y")].astype(F32)
        for k in range(N_CHIPS - 1):
            acc = acc + r_ref[k].astype(F32)
        o_ref[...] = acc

    return _pc(body, "sum_" + name, (rows // tr,),
               [pl.BlockSpec((N_CHIPS, tr, cols), lambda i: (0, i, 0)),
                pl.BlockSpec((N_CHIPS - 1, tr, cols), lambda i: (0, i, 0))],
               _row(tr, cols), SDS((rows, cols), F32), sem=("parallel",))(pair, landed)


def _adamw_math(w, g, m, v):
    m = ADAM_B1 * m + (1.0 - ADAM_B1) * g
    v = ADAM_B2 * v + (1.0 - ADAM_B2) * (g * g)
    m_hat = m / (1.0 - ADAM_B1 ** ADAM_STEP)
    v_hat = v / (1.0 - ADAM_B2 ** ADAM_STEP)
    delta = -ADAM_LR * (m_hat / (jnp.sqrt(v_hat) + ADAM_EPS) + ADAM_WD * w)
    return delta, m, v


def _adamw_2d(w, g_own, g_sib, m, v, name, halves):
    lead = w.ndim == 3
    rows, cols = w.shape[-2:]
    tr = _tile_rows(rows // 2)
    nh = rows // 2 // tr if halves else rows // tr

    def body(w_ref, go_ref, gs_ref, m_ref, v_ref, g_out, d_out, m_out, v_out):
        if halves:
            mine = (pl.program_id(0) // nh) == lax.axis_index("c")
            g = jnp.where(mine, go_ref[...], gs_ref[...])
        else:
            g = go_ref[...] + gs_ref[...]
        delta, mn, vn = _adamw_math(w_ref[...], g, m_ref[...], v_ref[...])
        g_out[...] = g
        d_out[...] = delta
        m_out[...] = mn
        v_out[...] = vn

    r = _row(tr, cols)
    p = pl.BlockSpec((None, tr, cols), lambda i: (0, i, 0)) if lead else r
    h = pl.BlockSpec((tr, cols), lambda i: (i % nh, 0))
    return _pc(body, "adamw_" + name, (rows // tr,), [p, h, h, p, p], [r] * 4, [SDS((rows, cols), F32)] * 4,
               sem=("parallel",))(w, g_own, g_sib, m, v)


def _small_allreduce_adamw(mine, w, m, v):
    shape = mine.shape

    def body(mine_ref, w_ref, m_ref, v_ref, g_out, d_out, m_out, v_out, buf_ref, send_sems, recv_sems):
        x, y, c = lax.axis_index("x"), lax.axis_index("y"), lax.axis_index("c")
        me = 4 * x + 2 * y + c
        buf_ref[me] = mine_ref[...]
        copies = []
        for k in range(1, N_DEV):
            tgt = (me + k) % N_DEV
            copies.append(pltpu.make_async_remote_copy(
                src_ref=mine_ref, dst_ref=buf_ref.at[me], send_sem=send_sems.at[k], recv_sem=recv_sems.at[k],
                device_id=(tgt // 4, (tgt // 2) % 2, tgt % 2), device_id_type=MESH))
        for cp in copies:
            cp.start()
        for k in range(1, N_DEV):
            src = (me + N_DEV - k) % N_DEV
            pltpu.make_async_remote_copy(
                src_ref=mine_ref, dst_ref=buf_ref.at[src], send_sem=send_sems.at[k], recv_sem=recv_sems.at[k],
                device_id=(x, y, c), device_id_type=MESH).wait_recv()
        for cp in copies:
            cp.wait_send()
        g = buf_ref[0]
        for j in range(1, N_DEV):
            g = g + buf_ref[j]
        delta, mn, vn = _adamw_math(w_ref[...], g, m_ref[...], v_ref[...])
        g_out[...] = g
        d_out[...] = delta
        m_out[...] = mn
        v_out[...] = vn

    vm = pl.BlockSpec(memory_space=pltpu.VMEM)
    return pl.pallas_call(
        body, name="small_allreduce_adamw", out_shape=[SDS(shape, F32)] * 4, in_specs=[vm] * 4, out_specs=[vm] * 4,
        scratch_shapes=[pltpu.VMEM((N_DEV,) + shape, F32), pltpu.SemaphoreType.DMA((N_DEV,)),
                        pltpu.SemaphoreType.DMA((N_DEV,))],
    )(mine, w, m, v)


def _as2d(a):
    return a.reshape(-1, a.shape[-1])


def _w_cat(stack):
    wi = stack.transpose(1, 0, 2).reshape(D_MODEL, IN_WIDTH)
    return jnp.concatenate(
        [wi[:, C_QKV:C_Z], wi[:, C_Z:C_BETA], wi[:, C_GA:C_GB], wi[:, C_GB:IN_WIDTH], wi[:, C_POOL:C_QKV],
         wi[:, C_BETA:C_GA], jnp.zeros((D_MODEL, CAT_WIDTH - K_BA - 2 * HEADS), wi.dtype)], axis=1)


WEIGHT_LAYOUT = {
    "w_in": lambda s: ("w_cat", _w_cat(s)),
    "pool_w": lambda s: ("pool_w", s.reshape(N_CHIPS, 4, POOL_GROUP, POOL_OUT_GROUP // N_CHIPS)
                         .transpose(1, 2, 0, 3).reshape(4, POOL_GROUP, POOL_OUT_GROUP)),
    "w_out": lambda s: ("w_out", s.reshape(D_MODEL, D_MODEL)),
    "w_up": lambda s: ("w_up", s),
    "w_down": lambda s: ("w_down", s.reshape(D_FF, D_MODEL)),
    "ple_gate_w": lambda s: ("ple_gate_w", s.reshape(D_MODEL, D_MODEL)),
    "ple_proj_w": lambda s: ("ple_proj_w", s.transpose(1, 0, 2).reshape(PLE_DIM, D_MODEL)),
}

GRAD_LAYOUT = {
    "w_in": lambda g: g.reshape(D_MODEL, N_CHIPS, IN_WIDTH // N_CHIPS).transpose(1, 0, 2),
    "pool_w": lambda g: g.reshape(4, POOL_GROUP, N_CHIPS, POOL_OUT_GROUP // N_CHIPS)
                         .transpose(2, 0, 1, 3).reshape(N_CHIPS, 4 * POOL_GROUP, POOL_OUT_GROUP // N_CHIPS),
    "w_out": lambda g: g.reshape(N_CHIPS, D_MODEL // N_CHIPS, D_MODEL),
    "w_up": lambda g: g,
    "w_down": lambda g: g.reshape(N_CHIPS, D_FF // N_CHIPS, D_MODEL),
    "ple_gate_w": lambda g: g.reshape(N_CHIPS, D_MODEL // N_CHIPS, D_MODEL),
    "ple_proj_w": lambda g: g.reshape(PLE_DIM, N_CHIPS, D_MODEL // N_CHIPS).transpose(1, 0, 2),
}


def _full_weights(names, stacks):
    return dict(WEIGHT_LAYOUT[n](s.astype(MXU_DTYPE)) for n, s in zip(names, stacks))


def _grads_by_chip(names, grads):
    return [GRAD_LAYOUT[n](grads[n]).astype(WIRE_DTYPE) for n in names]


def _pack_small(rows, conv, name):
    n = len(rows)

    def body(*refs):
        out = refs[n + 1]
        out[...] = jnp.zeros_like(out)
        for i in range(n):
            out[i:i + 1, :] = refs[i][...]
        out[SMALL_CONV_AT:SMALL_CONV_AT + SMALL_CONV_ROWS, :] = refs[n][...]

    vm = pl.BlockSpec(memory_space=pltpu.VMEM)
    return pl.pallas_call(body, name=name, out_shape=SDS((SMALL_CONV_AT + SMALL_CONV_ROWS, D_MODEL), F32),
                          in_specs=[vm] * (n + 1), out_specs=vm)(*rows, conv)


def _pad_row(a):
    a = a.reshape(1, -1).astype(F32)
    return jnp.pad(a, ((0, 0), (0, D_MODEL - a.shape[1])))


def kernel(x, p, ln_in_g, ln_in_b, w_in, pool_w, pool_scale, conv_w, a_log, dt_bias, o_norm_w, w_out, ln1_g, ln1_b, w_up, w_down, ple_gate_w, ple_proj_w, ln2_g, ln2_b, loss_target, m_ln_in_g, m_ln_in_b, m_w_in, m_pool_w, m_pool_scale, m_conv_w, m_a_log, m_dt_bias, m_o_norm_w, m_w_out, m_ln1_g, m_ln1_b, m_w_up, m_w_down, m_ple_gate_w, m_ple_proj_w, m_ln2_g, m_ln2_b, v_ln_in_g, v_ln_in_b, v_w_in, v_pool_w, v_pool_scale, v_conv_w, v_a_log, v_dt_bias, v_o_norm_w, v_w_out, v_ln1_g, v_ln1_b, v_w_up, v_w_down, v_ple_gate_w, v_ple_proj_w, v_ln2_g, v_ln2_b):
    given = dict(locals())
    chip = 2 * lax.axis_index("x") + lax.axis_index("y")

    shard = lambda n: _as2d(given[n]).astype(WIRE_DTYPE)

    conv_pad = jnp.pad(conv_w[0], ((0, 8 - CONV_K), (0, 0)))
    gathered = _gather_weights([shard(n) for n in EARLY], conv_pad)
    wts = _full_weights(EARLY, gathered[0:len(EARLY)])
    wts.update({
        "conv_w": jnp.concatenate([gathered[len(EARLY)][j, 0:CONV_K] for j in range(N_CHIPS)], axis=1),
        "ln_in_g": ln_in_g, "ln_in_b": ln_in_b, "pool_scale": pool_scale[0], "a_log": a_log[0],
        "dt_bias": dt_bias[0], "o_norm_w": o_norm_w[0], "ln1_g": ln1_g[0], "ln1_b": ln1_b[0],
        "ln2_g": ln2_g[0], "ln2_b": ln2_b[0],
    })

    late_srcs = [shard(n) for n in LATE]
    late_lands = [lax.empty((N_CHIPS,) + s.shape, s.dtype) for s in late_srcs]
    gsend, grecv, gsrcs, glands, start_token = _split_copy_start(
        "late_gather_start", late_srcs, late_lands, _late_gather_copies, gathered[0])

    def late_weights(after):
        _, stacks = _split_copy_wait("late_gather_wait", gsend, grecv, gsrcs, glands, after, _late_gather_copies)
        return _full_weights(LATE, stacks)

    scatter = {}

    def send_late_grads(grads):
        srcs = _grads_by_chip(LATE, grads)
        lands = [lax.empty((N_CHIPS - 1,) + g.shape[1:], g.dtype) for g in srcs]
        scatter["send"], scatter["recv"], scatter["srcs"], scatter["lands"], token = _split_copy_start(
            "late_scatter_start", srcs, lands, _late_scatter_copies, srcs[0])
        return token

    last = {}

    def send_early_grads(grads):
        by_chip = _grads_by_chip(EARLY, grads)
        theirs = _swap_halves(by_chip)
        pair = [_add_pair(g, t, n) for g, t, n in zip(by_chip, theirs, EARLY)]
        lands = [lax.empty((N_CHIPS - 1,) + q.shape[1:], q.dtype) for q in pair]
        last["send"], last["recv"], last["srcs"], last["lands"], token = _split_copy_start(
            "early_scatter_start", pair, lands, _late_scatter_copies, pair[0])
        return token

    grad_x, grads, loss = _local_step(x[0], p[0, 0], loss_target[0], wts, start_token, late_weights, send_late_grads,
                                      send_early_grads)

    late_mine, late_landed = _split_copy_wait("late_scatter_wait", scatter["send"], scatter["recv"], scatter["srcs"],
                                              scatter["lands"], grad_x, _late_scatter_copies)
    late_part = [_sum_slabs(q, r, n) for q, r, n in zip(late_mine, late_landed, LATE)]
    pair, landed = _split_copy_wait("early_scatter_wait", last["send"], last["recv"], last["srcs"], last["lands"],
                                    grad_x, _late_scatter_copies)
    reduced = [_sum_slabs(q, r, n) for q, r, n in zip(pair, landed, EARLY)]
    from_sibling = _send_to_sibling(reduced + late_part)
    big_out = {}
    for n, g_own, g_sib in zip(EARLY + LATE, reduced + late_part, from_sibling):
        view = (lambda a: a) if given[n].ndim == 3 else _as2d
        res = _adamw_2d(view(given[n]), g_own, g_sib, view(given["m_" + n]), view(given["v_" + n]), n,
                        halves=n in EARLY)
        big_out[n] = [r.reshape(given[n].shape) for r in res]

    conv_cols = QKV_WIDTH // N_CHIPS

    def small_pack(get, conv, extra, name):
        if conv.shape[1] != QKV_WIDTH:
            conv = lax.dynamic_update_slice(jnp.zeros((CONV_K, QKV_WIDTH), F32), conv, (0, chip * conv_cols))
        return _pack_small([_pad_row(get(n)) for n in SMALL_NAMES] + extra, conv.reshape(SMALL_CONV_ROWS, D_MODEL), name)

    mine_small = small_pack(lambda n: grads[n], grads["conv_w"], [jnp.full((1, D_MODEL), loss, F32)], "pack_small_g")
    packed_small = [small_pack(lambda n: given[prefix + n], given[prefix + "conv_w"][0], [], "pack_small_" + tag)
                    for prefix, tag in (("", "w"), ("m_", "m"), ("v_", "v"))]
    small_out = _small_allreduce_adamw(mine_small, *packed_small)

    def small_get(k, n):
        if n == "conv_w":
            full = small_out[k][SMALL_CONV_AT:SMALL_CONV_AT + SMALL_CONV_ROWS].reshape(CONV_K, QKV_WIDTH)
            return lax.dynamic_slice(full, (0, chip * conv_cols), (CONV_K, conv_cols)).reshape(given[n].shape)
        i = SMALL_NAMES.index(n)
        return small_out[k][i, 0:given[n].size].reshape(given[n].shape)

    order = ["ln_in_g", "ln_in_b", "w_in", "pool_w", "pool_scale", "conv_w", "a_log", "dt_bias", "o_norm_w", "w_out",
             "ln1_g", "ln1_b", "w_up", "w_down", "ple_gate_w", "ple_proj_w", "ln2_g", "ln2_b"]
    outs = [small_out[0][len(SMALL_NAMES), 0], grad_x[None]]
    for k in range(4):
        for n in order:
            outs.append(big_out[n][k] if n in big_out else small_get(k, n))
    return tuple(outs)
```

```python
import jax
import jax.numpy as jnp
from jax import lax
from jax.experimental import pallas as pl
from jax.experimental.pallas import tpu as pltpu

F32 = jnp.float32
MXU_DTYPE = jnp.bfloat16
WIRE_DTYPE = jnp.bfloat16
SDS = jax.ShapeDtypeStruct

D_MODEL = 1024
POOL_WINDOWS = (2, 4, 8, 16)
POOL_WIDTH = 512
POOL_GROUP = 128
POOL_OUT_GROUP = 256
HEADS = 8
HEAD_DIM = 128
DN_WIDTH = HEADS * HEAD_DIM
QKV_WIDTH = 3 * DN_WIDTH
CONV_K = 4
CHUNK = 128
DW_TK = 1024
DW_TM = 1024
HEAD_GROUP = 8
D_FF = 4096
PLE_DIM = 256
LN_EPS = 1e-5
RMS_EPS = 1e-6
L2_EPS = 1e-6
ALPHA = 2.0 ** 0.25
Q_SCALE = HEAD_DIM ** -0.5
IN_WIDTH = 6672
C_POOL, C_QKV, C_Z, C_BETA, C_A, C_GA, C_GB = 0, 512, 3584, 4608, 4616, 4624, 5648
K_QKV, K_Z, K_GA, K_GB, K_U, K_BA, CAT_WIDTH = 0, 3072, 4096, 5120, 6144, 6656, 6912

ADAM_LR, ADAM_B1, ADAM_B2, ADAM_EPS, ADAM_WD, ADAM_STEP = 0.001, 0.9, 0.999, 1e-08, 0.01, 10

N_CHIPS = 4
N_DEV = 8
VMEM_LIMIT = 56 * 1024 * 1024

EARLY = ("w_in", "pool_w")
LATE = ("w_out", "w_up", "w_down", "ple_gate_w", "ple_proj_w")
SMALL_NAMES = ("ln_in_g", "ln_in_b", "pool_scale", "ln1_g", "ln1_b", "ln2_g", "ln2_b", "o_norm_w", "a_log", "dt_bias")
SMALL_CONV_AT = 12
SMALL_CONV_ROWS = CONV_K * QKV_WIDTH // D_MODEL


def _mx(a):
    return a.astype(MXU_DTYPE)


def _dot(a, b):
    return lax.dot_general(_mx(a), _mx(b), (((1,), (0,)), ((), ())), preferred_element_type=F32)


def _dot_nt(a, b):
    return lax.dot_general(_mx(a), _mx(b), (((1,), (1,)), ((), ())), preferred_element_type=F32)


def _dot_tn(a, b):
    return lax.dot_general(_mx(a), _mx(b), (((0,), (0,)), ((), ())), preferred_element_type=F32)


def _sigmoid(x):
    return 0.5 * jnp.tanh(0.5 * x) + 0.5


def _softplus(x):
    return jnp.maximum(x, 0.0) + jnp.log(1.0 + jnp.exp(-jnp.abs(x)))


def _pc(body, name, grid, in_specs, out_specs, out_shape, scratch=(), sem=None, aliases=None):
    return pl.pallas_call(
        body, out_shape=out_shape, grid=grid, in_specs=in_specs, out_specs=out_specs,
        scratch_shapes=scratch, name=name, input_output_aliases=aliases or {},
        compiler_params=pltpu.CompilerParams(dimension_semantics=sem, vmem_limit_bytes=VMEM_LIMIT))


def _row(tm, n):
    return pl.BlockSpec((tm, n), lambda i: (i, 0))


def _const(shape):
    nd = len(shape)
    return pl.BlockSpec(shape, lambda *_: (0,) * nd)


def _matmul(a, b, mode, name, out_dtype=F32, tm=512, tn=512, tk=512, stack_out=False):
    if mode == "nn":
        (m, k), n = a.shape, b.shape[1]
    elif mode == "nt":
        (m, k), n = a.shape, b.shape[0]
    else:
        (k, m), n = a.shape, b.shape[1]
    tm, tn, tk = min(tm, m), min(tn, n), min(tk, k)
    assert m % tm == 0 and n % tn == 0 and k % tk == 0, (name, m, n, k, tm, tn, tk)
    nk = k // tk
    if mode == "nn":
        a_spec = pl.BlockSpec((tm, tk), lambda i, j, kk: (i, kk))
        b_spec = pl.BlockSpec((tk, tn), lambda i, j, kk: (kk, j))
        dot = _dot
    elif mode == "nt":
        a_spec = pl.BlockSpec((tm, tk), lambda i, j, kk: (i, kk))
        b_spec = pl.BlockSpec((tn, tk), lambda i, j, kk: (j, kk))
        dot = _dot_nt
    else:
        a_spec = pl.BlockSpec((tk, tm), lambda i, j, kk: (kk, i))
        b_spec = pl.BlockSpec((tk, tn), lambda i, j, kk: (kk, j))
        dot = _dot_tn

    def body(a_ref, b_ref, o_ref, *acc):
        if nk == 1:
            o_ref[...] = dot(a_ref[...], b_ref[...]).astype(out_dtype)
            return
        acc_ref, kk = acc[0], pl.program_id(2)

        @pl.when(kk == 0)
        def _():
            acc_ref[...] = dot(a_ref[...], b_ref[...])

        @pl.when((kk > 0) & (kk < nk - 1))
        def _():
            acc_ref[...] += dot(a_ref[...], b_ref[...])

        @pl.when(kk == nk - 1)
        def _():
            o_ref[...] = (acc_ref[...] + dot(a_ref[...], b_ref[...])).astype(out_dtype)

    if stack_out:
        o_spec, o_shape = pl.BlockSpec((None, tm, tn), lambda i, j, kk: (j, i, 0)), SDS((n // tn, m, tn), out_dtype)
    else:
        o_spec, o_shape = pl.BlockSpec((tm, tn), lambda i, j, kk: (i, j)), SDS((m, n), out_dtype)
    return _pc(body, name, (m // tm, n // tn, nk), [a_spec, b_spec], o_spec, o_shape,
               scratch=[pltpu.VMEM((tm, tn), F32)] if nk > 1 else [],
               sem=("parallel", "parallel", "arbitrary"))(a, b)


PROJ_TN = 1152


def _proj(h0_bf, w_cat, tm, after):
    t = h0_bf.shape[0]

    def body(h_ref, w_ref, after_ref, o_ref):
        h = h_ref[...]
        for c0 in range(0, CAT_WIDTH, PROJ_TN):
            o_ref[:, c0:c0 + PROJ_TN] = _dot(h, w_ref[:, c0:c0 + PROJ_TN])

    return _pc(body, "proj", (t // tm,), [_row(tm, D_MODEL), _const((D_MODEL, CAT_WIDTH)), ANY],
               _row(tm, CAT_WIDTH), SDS((t, CAT_WIDTH), F32), sem=("parallel",))(h0_bf, w_cat, after)


def _ln_stats(x):
    mu = jnp.mean(x, axis=-1, keepdims=True)
    xc = x - mu
    var = jnp.mean(xc * xc, axis=-1, keepdims=True)
    rstd = lax.rsqrt(var + LN_EPS)
    return xc * rstd, rstd


def _ln_bwd(dy, xhat, rstd, g):
    dxh = dy * g
    m1 = jnp.mean(dxh, axis=-1, keepdims=True)
    m2 = jnp.mean(dxh * xhat, axis=-1, keepdims=True)
    return rstd * (dxh - m1 - xhat * m2)


def _ln_in(x, g, b, tm, after):
    t, d = x.shape

    def body(x_ref, g_ref, b_ref, after_ref, h_ref, hb_ref):
        xhat, _ = _ln_stats(x_ref[...])
        h = xhat * g_ref[...] + b_ref[...]
        h_ref[...] = h
        hb_ref[...] = _mx(h)

    return _pc(body, "ln_in", (t // tm,), [_row(tm, d), _const((1, d)), _const((1, d)), ANY],
               [_row(tm, d), _row(tm, d)], [SDS((t, d), F32), SDS((t, d), MXU_DTYPE)],
               sem=("parallel",))(x, g, b, after)


def _pool_fwd(proj, pool_w, tm):
    t = proj.shape[0]
    ublk = K_U // POOL_WIDTH

    def body(u_ref, halo_ref, pw_ref, ypre_ref, d_ref, ext_ref):
        i = pl.program_id(0)
        ext_ref[0:16, :] = jnp.where(i > 0, halo_ref[...], 0.0)
        ext_ref[16:16 + tm, :] = u_ref[...]
        tok = i * tm + lax.broadcasted_iota(jnp.int32, (tm, POOL_GROUP), 0)
        for gi, w in enumerate(POOL_WINDOWS):
            cs = pl.ds(gi * POOL_GROUP, POOL_GROUP)
            ug = ext_ref[pl.ds(16, tm), cs]
            s = ug
            for k in range(1, w):
                s = s + ext_ref[pl.ds(16 - k, tm), cs]
            cnt = jnp.minimum(tok + 1, w).astype(F32)
            db = _mx(s / cnt - ug)
            d_ref[:, gi * POOL_GROUP:(gi + 1) * POOL_GROUP] = db
            ypre_ref[:, gi * POOL_OUT_GROUP:(gi + 1) * POOL_OUT_GROUP] = _dot(db, pw_ref[gi])

    halo = pl.BlockSpec((16, POOL_WIDTH), lambda i: (jnp.maximum(i * (tm // 16) - 1, 0), ublk))
    return _pc(body, "pool_fwd", (t // tm,),
               [pl.BlockSpec((tm, POOL_WIDTH), lambda i: (i, ublk)), halo, _const((4, POOL_GROUP, POOL_OUT_GROUP))],
               [_row(tm, D_MODEL), _row(tm, POOL_WIDTH)],
               [SDS((t, D_MODEL), F32), SDS((t, POOL_WIDTH), MXU_DTYPE)],
               scratch=[pltpu.VMEM((16 + tm, POOL_WIDTH), F32)], sem=("parallel",))(proj, proj, pool_w)


def _pool_bwd(dyp, d_bf, pool_w, dproj, tm):
    t = dyp.shape[0]
    n = t // tm

    def body(dy_ref, dyn_ref, d_ref, pw_ref, dproj_ref, du_ref, dpw_ref, ext_ref):
        i = pl.program_id(0)

        @pl.when(i == 0)
        def _():
            dpw_ref[...] = jnp.zeros_like(dpw_ref)

        tok = i * tm + lax.broadcasted_iota(jnp.int32, (tm + 16, POOL_GROUP), 0)
        for gi, w in enumerate(POOL_WINDOWS):
            dy = dy_ref[:, gi * POOL_OUT_GROUP:(gi + 1) * POOL_OUT_GROUP]
            dyn = dyn_ref[:, gi * POOL_OUT_GROUP:(gi + 1) * POOL_OUT_GROUP]
            pw = pw_ref[gi]
            dd = _dot_nt(dy, pw)
            ddn = jnp.where(i < n - 1, _dot_nt(dyn, pw), 0.0)
            cnt = jnp.minimum(tok + 1, w).astype(F32)
            ext_ref[0:tm, :] = dd / cnt[0:tm]
            ext_ref[tm:tm + 16, :] = ddn / cnt[tm:tm + 16]
            s = ext_ref[pl.ds(0, tm), :]
            for k in range(1, w):
                s = s + ext_ref[pl.ds(k, tm), :]
            du_ref[:, gi * POOL_GROUP:(gi + 1) * POOL_GROUP] = _mx(s - dd)
            dpw_ref[gi] += _dot_tn(d_ref[:, gi * POOL_GROUP:(gi + 1) * POOL_GROUP], dy)

    nxt = pl.BlockSpec((16, D_MODEL), lambda i: (jnp.minimum((i + 1) * (tm // 16), t // 16 - 1), 0))
    return _pc(body, "pool_bwd", (n,),
               [_row(tm, D_MODEL), nxt, _row(tm, POOL_WIDTH), _const((4, POOL_GROUP, POOL_OUT_GROUP)), ANY],
               [pl.BlockSpec((tm, POOL_WIDTH), lambda i: (i, K_U // POOL_WIDTH)),
                _const((4, POOL_GROUP, POOL_OUT_GROUP))],
               [SDS(dproj.shape, dproj.dtype), SDS((4, POOL_GROUP, POOL_OUT_GROUP), F32)],
               scratch=[pltpu.VMEM((tm + 16, POOL_GROUP), F32)], sem=("arbitrary",),
               aliases={4: 0})(dyp, dyp, d_bf, pool_w, dproj)


CONV_BLK = 512


CONV_ROWS = 32


def _conv_rows(ext_ref, w, r, rows):
    y = w[0] * ext_ref[pl.ds(r + 5, rows), :]
    for k in range(1, CONV_K):
        y = y + w[k] * ext_ref[pl.ds(r + 5 + k, rows), :]
    return y


def _conv_fwd(proj, conv_w, tm):
    t = proj.shape[0]

    def body(x_ref, halo_ref, w_ref, o_ref, ds_ref, ext_ref):
        i = pl.program_id(0)
        ext_ref[0:8, :] = jnp.where(i > 0, halo_ref[...], 0.0)
        ext_ref[8:8 + tm, :] = x_ref[...]
        w = [w_ref[pl.ds(k, 1), :] for k in range(CONV_K)]
        for r in range(0, tm, CONV_ROWS):
            y = _conv_rows(ext_ref, w, r, CONV_ROWS)
            s = _sigmoid(y)
            o_ref[pl.ds(r, CONV_ROWS), :] = y * s
            ds_ref[pl.ds(r, CONV_ROWS), :] = _mx(s * (1.0 + y * (1.0 - s)))

    halo = pl.BlockSpec((8, CONV_BLK), lambda i, j: (jnp.maximum(i * (tm // 8) - 1, 0), j))
    blk = pl.BlockSpec((tm, CONV_BLK), lambda i, j: (i, j))
    return _pc(body, "conv_fwd", (t // tm, QKV_WIDTH // CONV_BLK),
               [blk, halo, pl.BlockSpec((CONV_K, CONV_BLK), lambda i, j: (0, j))], [blk, blk],
               [SDS((t, QKV_WIDTH), F32), SDS((t, QKV_WIDTH), MXU_DTYPE)],
               scratch=[pltpu.VMEM((8 + tm, CONV_BLK), F32)], sem=("parallel", "parallel"))(proj, proj, conv_w)


def _conv_bwd(dact, dsilu, proj, conv_w, dproj, tm):
    t = proj.shape[0]
    n = t // tm

    def body(da_ref, dan_ref, ds_ref, dsn_ref, x_ref, xp_ref, w_ref, dproj_ref, dx_ref, dw_ref, ext_ref, dy_ref):
        i = pl.program_id(1)

        @pl.when(i == 0)
        def _():
            dw_ref[...] = jnp.zeros_like(dw_ref)

        ext_ref[0:8, :] = jnp.where(i > 0, xp_ref[...], 0.0)
        ext_ref[8:8 + tm, :] = x_ref[...]
        w = [w_ref[pl.ds(k, 1), :] for k in range(CONV_K)]

        acc = [jnp.zeros((8, CONV_BLK), F32) for _ in range(CONV_K)]
        for r in range(0, tm, CONV_ROWS):
            dy = da_ref[pl.ds(r, CONV_ROWS), :] * ds_ref[pl.ds(r, CONV_ROWS), :].astype(F32)
            dy_ref[pl.ds(r, CONV_ROWS), :] = dy
            for k in range(CONV_K):
                prod = dy * ext_ref[pl.ds(r + 5 + k, CONV_ROWS), :]
                for q in range(0, CONV_ROWS, 8):
                    acc[k] = acc[k] + prod[q:q + 8]
        dy_ref[tm:tm + 8, :] = jnp.where(i < n - 1, dan_ref[...] * dsn_ref[0:8, :].astype(F32), 0.0)
        for k in range(CONV_K):
            dw_ref[pl.ds(k, 1), :] += jnp.sum(acc[k], axis=0, keepdims=True)
        for r in range(0, tm, CONV_ROWS):
            dx = w[0] * dy_ref[pl.ds(r + 3, CONV_ROWS), :]
            for k in range(1, CONV_K):
                dx = dx + w[k] * dy_ref[pl.ds(r + 3 - k, CONV_ROWS), :]
            dx_ref[pl.ds(r, CONV_ROWS), :] = _mx(dx)

    blk = pl.BlockSpec((tm, CONV_BLK), lambda j, i: (i, j))
    prev = pl.BlockSpec((8, CONV_BLK), lambda j, i: (jnp.maximum(i * (tm // 8) - 1, 0), j))
    nxt = pl.BlockSpec((8, CONV_BLK), lambda j, i: (jnp.minimum((i + 1) * (tm // 8), t // 8 - 1), j))
    nxt16 = pl.BlockSpec((16, CONV_BLK), lambda j, i: (jnp.minimum((i + 1) * (tm // 16), t // 16 - 1), j))
    wspec = pl.BlockSpec((CONV_K, CONV_BLK), lambda j, i: (0, j))
    return _pc(body, "conv_bwd", (QKV_WIDTH // CONV_BLK, n),
               [blk, nxt, blk, nxt16, blk, prev, wspec, ANY],
               [blk, pl.BlockSpec((8, CONV_BLK), lambda j, i: (0, j))],
               [SDS(dproj.shape, dproj.dtype), SDS((8, QKV_WIDTH), F32)],
               scratch=[pltpu.VMEM((8 + tm, CONV_BLK), F32), pltpu.VMEM((8 + tm, CONV_BLK), F32)],
               sem=("parallel", "arbitrary"), aliases={7: 0})(dact, dact, dsilu, dsilu, proj, proj, conv_w, dproj)


def _lane(shape):
    return lax.broadcasted_iota(jnp.int32, shape, 1)


def _ba_fwd(proj, al_row, dtb_row, tm):
    t = proj.shape[0]
    bablk = K_BA // 128

    def body(ba_ref, al_ref, dtb_ref, bg_ref):
        ba = ba_ref[...]
        lane = _lane(ba.shape)
        g = -jnp.exp(al_ref[...]) * _softplus(ba + dtb_ref[...])
        bg_ref[...] = jnp.where(lane < HEADS, _sigmoid(ba), jnp.where(lane < 2 * HEADS, g, 0.0))

    return _pc(body, "ba_fwd", (t // tm,),
               [pl.BlockSpec((tm, 128), lambda i: (i, bablk)), _const((1, 128)), _const((1, 128))],
               _row(tm, 128), SDS((t, 128), F32), sem=("parallel",))(proj, al_row, dtb_row)


def _ba_bwd(dbg, bg, proj, al_row, dtb_row, dproj, tm):
    t = proj.shape[0]
    bablk = K_BA // 128

    def body(dbg_ref, bg_ref, ba_ref, al_ref, dtb_ref, dproj_ref, dba_ref, acc_ref):
        i = pl.program_id(0)

        @pl.when(i == 0)
        def _():
            acc_ref[...] = jnp.zeros_like(acc_ref)

        dbg_v, bg_v, ba = dbg_ref[...], bg_ref[...], ba_ref[...]
        lane = _lane(ba.shape)
        is_g = (lane >= HEADS) & (lane < 2 * HEADS)
        dbeta_raw = dbg_v * bg_v * (1.0 - bg_v)
        da_raw = dbg_v * (-jnp.exp(al_ref[...])) * _sigmoid(ba + dtb_ref[...])
        dba_ref[:, 0:128] = _mx(jnp.where(lane < HEADS, dbeta_raw, jnp.where(is_g, da_raw, 0.0)))
        dba_ref[:, 128:CAT_WIDTH - K_BA] = jnp.zeros((tm, CAT_WIDTH - K_BA - 128), dba_ref.dtype)
        acc_ref[0:1, :] += jnp.sum(jnp.where(is_g, dbg_v * bg_v, 0.0), axis=0, keepdims=True)
        acc_ref[1:2, :] += jnp.sum(jnp.where(is_g, da_raw, 0.0), axis=0, keepdims=True)

    tail = CAT_WIDTH - K_BA
    return _pc(body, "ba_bwd", (t // tm,),
               [_row(tm, 128), _row(tm, 128), pl.BlockSpec((tm, 128), lambda i: (i, bablk)),
                _const((1, 128)), _const((1, 128)), ANY],
               [pl.BlockSpec((tm, tail), lambda i: (i, K_BA // tail)), _const((8, 128))],
               [SDS(dproj.shape, dproj.dtype), SDS((8, 128), F32)],
               sem=("arbitrary",), aliases={5: 0})(dbg, bg, proj, al_row, dtb_row, dproj)


def _each(f, *lists):
    return [f(*a) for a in zip(*lists)]


def _rowsum(a):
    return jnp.sum(a, axis=1, keepdims=True)


def _chunk_terms(qs, ks, bgv, g_rows, hs):
    c = CHUNK
    ii = lax.broadcasted_iota(jnp.int32, (c, c), 0)
    jj = lax.broadcasted_iota(jnp.int32, (c, c), 1)
    lane = _lane(bgv.shape)
    incl = ii >= jj
    beta = [_rowsum(jnp.where(lane == h, bgv, 0.0)) for h in hs]
    g_col = [_rowsum(jnp.where(lane == HEADS + h, bgv, 0.0)) for h in hs]
    rq = _each(lambda q: lax.rsqrt(_rowsum(q * q) + L2_EPS), qs)
    rk = _each(lambda k: lax.rsqrt(_rowsum(k * k) + L2_EPS), ks)
    yq = _each(jnp.multiply, qs, rq)
    kn = _each(jnp.multiply, ks, rk)
    qn = _each(lambda a: a * Q_SCALE, yq)
    gc_col = _each(lambda g: _rowsum(jnp.where(jj <= ii, g, 0.0)), g_rows)
    gc_row = _each(lambda g: jnp.sum(jnp.where(ii <= jj, g, 0.0), axis=0, keepdims=True), g_col)
    dm = _each(lambda a, b: jnp.where(incl, jnp.exp(jnp.where(incl, a - b, 0.0)), 0.0), gc_col, gc_row)
    gl = _each(_rowsum, g_rows)
    eg = _each(jnp.exp, gc_col)
    ek = _each(lambda a, b: jnp.exp(a - b), gl, gc_col)
    egl = _each(jnp.exp, gl)
    kb = _each(jnp.multiply, kn, beta)
    kk = _each(_dot_nt, kb, kn)
    qk = _each(_dot_nt, qn, kn)
    m = _each(lambda a, b: jnp.where(ii > jj, a * b, 0.0), kk, dm)
    attn = _each(jnp.multiply, qk, dm)
    return dict(ii=ii, jj=jj, beta=beta, rq=rq, rk=rk, yq=yq, kn=kn, qn=qn, dm=dm, eg=eg, ek=ek,
                egl=egl, kb=kb, m=m, attn=attn)


def _unit_lower_inverse_minus_identity(ms, ii, jj):
    pair = (ii >> 1) == (jj >> 1)
    ys = _each(lambda m: -jnp.where(pair, m, 0.0), ms)
    s = 1
    while (1 << s) < CHUNK:
        mask = ((ii >> (s + 1)) == (jj >> (s + 1))) & ((ii >> s) != (jj >> s))
        lbs = _each(lambda m: jnp.where(mask, m, 0.0), ms)
        zs = _each(lambda y, lb: lb + _dot(y, lb), ys, lbs)
        ys = _each(lambda y, z: y - z - _dot(z, y), ys, zs)
        s += 1
    return ys


def _head_offsets(group):
    hs = [group * HEAD_GROUP + a for a in range(HEAD_GROUP)]
    return hs, [pl.ds(pl.multiple_of(base + h * HEAD_DIM, HEAD_DIM), HEAD_DIM)
                for base in (0, DN_WIDTH, 2 * DN_WIDTH) for h in hs]


def _dn_local_fwd(qkv_act, bg, bgt):
    t = qkv_act.shape[0]
    nt = t // CHUNK
    c = CHUNK

    def body(qkv_ref, bg_ref, bgt_ref, u_ref, w_ref, qg_ref, kg_ref, attn_ref, y_ref, egl_ref):
        bgv = bg_ref[...]

        def group(gi, carry):
            hs, offs = _head_offsets(gi)
            qo, ko, vo = offs[0:HEAD_GROUP], offs[HEAD_GROUP:2 * HEAD_GROUP], offs[2 * HEAD_GROUP:]
            qs = [qkv_ref[:, o] for o in qo]
            ks = [qkv_ref[:, o] for o in ko]
            vs = [qkv_ref[:, o] for o in vo]
            g_rows = [bgt_ref[pl.ds(HEADS + h, 1), :] for h in hs]
            ct = _chunk_terms(qs, ks, bgv, g_rows, hs)
            ys = _unit_lower_inverse_minus_identity(ct["m"], ct["ii"], ct["jj"])
            vb = _each(jnp.multiply, vs, ct["beta"])
            kbe = _each(jnp.multiply, ct["kb"], ct["eg"])
            us = _each(lambda a, y: a + _dot(y, a), vb, ys)
            ws = _each(lambda a, y: a + _dot(y, a), kbe, ys)
            for a in range(HEAD_GROUP):
                dst = qo[a]
                u_ref[:, dst] = us[a]
                w_ref[:, dst] = _mx(ws[a])
                qg_ref[:, dst] = _mx(ct["qn"][a] * ct["eg"][a])
                kg_ref[:, dst] = _mx(ct["kn"][a] * ct["ek"][a])
                attn_ref[:, dst] = _mx(ct["attn"][a])
                y_ref[:, dst] = _mx(ys[a])
                egl_ref[0, pl.ds(hs[a], 1), :] = jnp.broadcast_to(ct["egl"][a], (1, HEAD_DIM))
            return carry

        lax.fori_loop(0, HEADS // HEAD_GROUP, group, 0)

    wide = _row(c, DN_WIDTH)
    return _pc(body, "dn_local_fwd", (nt,),
               [_row(c, QKV_WIDTH), _row(c, 128), pl.BlockSpec((2 * HEADS, c), lambda i: (0, i))],
               [wide, wide, wide, wide, wide, wide, pl.BlockSpec((1, HEADS, HEAD_DIM), lambda i: (i, 0, 0))],
               [SDS((t, DN_WIDTH), F32)] + [SDS((t, DN_WIDTH), MXU_DTYPE)] * 5 + [SDS((nt, HEADS, HEAD_DIM), F32)],
               sem=("parallel",))(qkv_act, bg, bgt)


def _dn_scan_fwd(u, w, qg, kg, attn, egl):
    t = u.shape[0]
    nt = t // CHUNK
    c = CHUNK
    sls = [slice(h * HEAD_DIM, (h + 1) * HEAD_DIM) for h in range(HEADS)]

    def body(u_ref, w_ref, qg_ref, kg_ref, attn_ref, egl_ref, o_ref, vn_ref, st_ref, s_ref):
        @pl.when(pl.program_id(0) == 0)
        def _():
            s_ref[...] = jnp.zeros_like(s_ref)

        ss = [s_ref[h] for h in range(HEADS)]
        sb = _each(_mx, ss)
        vn = [u_ref[:, sl] - _dot(w_ref[:, sl], b) for sl, b in zip(sls, sb)]
        vnb = _each(_mx, vn)
        oa = [_dot(qg_ref[:, sl], b) for sl, b in zip(sls, sb)]
        ob = [_dot(attn_ref[:, sl], b) for sl, b in zip(sls, vnb)]
        upd = [_dot_tn(kg_ref[:, sl], b) for sl, b in zip(sls, vnb)]
        for h, sl in enumerate(sls):
            st_ref[0, h] = ss[h]
            vn_ref[:, sl] = vnb[h]
            o_ref[:, sl] = oa[h] + ob[h]
            s_ref[h] = ss[h] * egl_ref[0, h:h + 1, :] + upd[h]

    wide = _row(c, DN_WIDTH)
    return _pc(body, "dn_scan_fwd", (nt,),
               [wide] * 5 + [pl.BlockSpec((1, HEADS, HEAD_DIM), lambda i: (i, 0, 0))],
               [wide, wide, pl.BlockSpec((1, HEADS, HEAD_DIM, HEAD_DIM), lambda i: (i, 0, 0, 0))],
               [SDS((t, DN_WIDTH), F32), SDS((t, DN_WIDTH), MXU_DTYPE), SDS((nt, HEADS, HEAD_DIM, HEAD_DIM), F32)],
               scratch=[pltpu.VMEM((HEADS, HEAD_DIM, HEAD_DIM), F32)], sem=("arbitrary",))(u, w, qg, kg, attn, egl)


def _dn_scan_bwd(do, qg, kg, w, attn, vn, states, egl):
    t = do.shape[0]
    nt = t // CHUNK
    c = CHUNK
    sls = [slice(h * HEAD_DIM, (h + 1) * HEAD_DIM) for h in range(HEADS)]

    def body(do_ref, qg_ref, kg_ref, w_ref, attn_ref, vn_ref, st_ref, egl_ref,
             dvn_ref, dkg_ref, dqg_ref, dattn_ref, dw_ref, degl_ref, ds_ref):
        @pl.when(pl.program_id(0) == 0)
        def _():
            ds_ref[...] = jnp.zeros_like(ds_ref)

        dsp = [ds_ref[h] for h in range(HEADS)]
        dsb = _each(_mx, dsp)
        ss = [st_ref[0, h] for h in range(HEADS)]
        sb = _each(_mx, ss)
        dvn = [_dot(kg_ref[:, sl], b) + _dot_tn(attn_ref[:, sl], do_ref[:, sl]) for sl, b in zip(sls, dsb)]
        dvnb = _each(_mx, dvn)
        dkg = [_dot_nt(vn_ref[:, sl], b) for sl, b in zip(sls, dsb)]
        dqg = [_dot_nt(do_ref[:, sl], b) for sl, b in zip(sls, sb)]
        dattn = [_dot_nt(do_ref[:, sl], vn_ref[:, sl]) for sl in sls]
        dwv = [-_dot_nt(a, b) for a, b in zip(dvnb, sb)]
        upd = [_dot_tn(qg_ref[:, sl], do_ref[:, sl]) - _dot_tn(w_ref[:, sl], a) for sl, a in zip(sls, dvnb)]
        for h, sl in enumerate(sls):
            dvn_ref[:, sl] = dvn[h]
            dkg_ref[:, sl] = dkg[h]
            dqg_ref[:, sl] = dqg[h]
            dattn_ref[:, sl] = dattn[h]
            dw_ref[:, sl] = dwv[h]
            degl = jnp.sum(_rowsum(ss[h] * dsp[h]), axis=0, keepdims=True)
            degl_ref[0, h:h + 1, :] = jnp.broadcast_to(degl, (1, HEAD_DIM))
            ds_ref[h] = dsp[h] * egl_ref[0, h:h + 1, :] + upd[h]

    rev = pl.BlockSpec((c, DN_WIDTH), lambda i: (nt - 1 - i, 0))
    rev3 = pl.BlockSpec((1, HEADS, HEAD_DIM), lambda i: (nt - 1 - i, 0, 0))
    rev4 = pl.BlockSpec((1, HEADS, HEAD_DIM, HEAD_DIM), lambda i: (nt - 1 - i, 0, 0, 0))
    return _pc(body, "dn_scan_bwd", (nt,), [rev] * 6 + [rev4, rev3], [rev] * 5 + [rev3],
               [SDS((t, DN_WIDTH), F32)] * 5 + [SDS((nt, HEADS, HEAD_DIM), F32)],
               scratch=[pltpu.VMEM((HEADS, HEAD_DIM, HEAD_DIM), F32)],
               sem=("arbitrary",))(do, qg, kg, w, attn, vn, states, egl)


def _dn_local_bwd(qkv_act, bg, bgt, u, w, ymat, dvn, dw, dqg, dkg, dattn, degl):
    t = qkv_act.shape[0]
    nt = t // CHUNK
    c = CHUNK

    def body(qkv_ref, bg_ref, bgt_ref, u_ref, w_ref, y_ref, du_ref, dw_ref, dqg_ref, dkg_ref, dattn_ref,
             degl_ref, dqkv_ref, dbg_ref):
        bgv = bg_ref[...]
        lane = _lane(bgv.shape)
        rowi = lax.broadcasted_iota(jnp.int32, (c, 1), 0)

        def group(gi, dbg):
            hs, offs = _head_offsets(gi)
            qo, ko, vo = offs[0:HEAD_GROUP], offs[HEAD_GROUP:2 * HEAD_GROUP], offs[2 * HEAD_GROUP:]
            qs = [qkv_ref[:, o] for o in qo]
            ks = [qkv_ref[:, o] for o in ko]
            vs = [qkv_ref[:, o] for o in vo]
            g_rows = [bgt_ref[pl.ds(HEADS + h, 1), :] for h in hs]
            ct = _chunk_terms(qs, ks, bgv, g_rows, hs)
            ii, jj = ct["ii"], ct["jj"]
            beta, eg, ek, kb, kn, qn, dm = ct["beta"], ct["eg"], ct["ek"], ct["kb"], ct["kn"], ct["qn"], ct["dm"]
            ys = [y_ref[:, o] for o in qo]
            du = [du_ref[:, o] for o in qo]
            dwv = [dw_ref[:, o] for o in qo]
            dqg_v = [dqg_ref[:, o] for o in qo]
            dkg_v = [dkg_ref[:, o] for o in qo]
            dattn_v = [dattn_ref[:, o] for o in qo]
            degl_v = [jnp.max(degl_ref[0, pl.ds(h, 1), :], axis=1, keepdims=True) for h in hs]
            dvb = _each(lambda a, y: a + _dot_tn(y, a), du, ys)
            dkbe = _each(lambda a, y: a + _dot_tn(y, a), dwv, ys)
            dm_u = [_dot_nt(a, u_ref[:, o]) for a, o in zip(dvb, qo)]
            dm_w = [_dot_nt(a, w_ref[:, o]) for a, o in zip(dkbe, qo)]
            dms = _each(lambda a, b: jnp.where(ii > jj, -(a + b), 0.0), dm_u, dm_w)
            dkk = _each(jnp.multiply, dms, dm)
            dqk = _each(jnp.multiply, dattn_v, dm)
            gmat = _each(lambda a, b, c_, d: a * b + c_ * d, dms, ct["m"], dattn_v, ct["attn"])
            dkb = _each(lambda a, b, c_, d: _dot(a, b) + c_ * d, dkk, kn, dkbe, eg)
            dk1 = _each(_dot_tn, dkk, kb)
            dk2 = _each(_dot_tn, dqk, qn)
            dq1 = _each(_dot, dqk, kn)
            dk = _each(lambda a, b, c_, d: a + b + c_ * d, dk1, dk2, dkg_v, ek)
            dq = _each(lambda a, b, c_: a + b * c_, dq1, dqg_v, eg)
            deg = _each(lambda a, b, c_, d: _rowsum(a * b) + _rowsum(c_ * d), dqg_v, qn, dkbe, kb)
            dek = _each(lambda a, b: _rowsum(a * b), dkg_v, kn)
            dgl = _each(lambda a, b, c_, d: jnp.sum(a * b, axis=0, keepdims=True) + c_ * d, dek, ek, degl_v, ct["egl"])
            cs_row = _each(lambda g: jnp.sum(g, axis=0, keepdims=True), gmat)
            cs_col = _each(lambda r: _rowsum(jnp.where(ii == jj, r, 0.0)), cs_row)
            dgc = _each(lambda a, b, c_, d, g, e, f: a * b - c_ * d + _rowsum(g) - e + jnp.where(rowi == c - 1, f, 0.0),
                        deg, eg, dek, ek, gmat, cs_col, dgl)
            dgc_row = _each(lambda a: jnp.sum(jnp.where(ii == jj, a, 0.0), axis=0, keepdims=True), dgc)
            dg = _each(lambda r: _rowsum(jnp.where(jj >= ii, r, 0.0)), dgc_row)
            dbeta = _each(lambda a, b, c_, d: _rowsum(a * b) + _rowsum(c_ * d), dkb, kn, dvb, vs)
            dk = _each(lambda a, b, c_: a + b * c_, dk, dkb, beta)
            for a in range(HEAD_GROUP):
                dyq = dq[a] * Q_SCALE
                yq = ct["yq"][a]
                dqkv_ref[:, qo[a]] = ct["rq"][a] * (dyq - yq * _rowsum(yq * dyq))
                dqkv_ref[:, ko[a]] = ct["rk"][a] * (dk[a] - kn[a] * _rowsum(kn[a] * dk[a]))
                dqkv_ref[:, vo[a]] = dvb[a] * beta[a]
                dbg = dbg + jnp.where(lane == hs[a], dbeta[a], 0.0) + jnp.where(lane == HEADS + hs[a], dg[a], 0.0)
            return dbg

        dbg_ref[...] = lax.fori_loop(0, HEADS // HEAD_GROUP, group, jnp.zeros((c, 128), F32))

    wide = _row(c, DN_WIDTH)
    sc3 = pl.BlockSpec((1, HEADS, HEAD_DIM), lambda i: (i, 0, 0))
    return _pc(body, "dn_local_bwd", (nt,),
               [_row(c, QKV_WIDTH), _row(c, 128), pl.BlockSpec((2 * HEADS, c), lambda i: (0, i))] + [wide] * 8 + [sc3],
               [_row(c, QKV_WIDTH), _row(c, 128)], [SDS((t, QKV_WIDTH), F32), SDS((t, 128), F32)],
               sem=("parallel",))(qkv_act, bg, bgt, u, w, ymat, dvn, dw, dqg, dkg, dattn, degl)


MIX_ROWS = 64


def _mix_fwd(o, proj, ypre, pool_scale, wo_row, tm):
    t = o.shape[0]

    def body(o_ref, z_ref, ga_ref, gb_ref, yp_ref, ps_ref, wo_ref, mixed_ref):
        for r in range(0, tm, MIX_ROWS):
            rows = pl.ds(r, MIX_ROWS)
            for h in range(HEADS):
                sl = slice(h * HEAD_DIM, (h + 1) * HEAD_DIM)
                oh = o_ref[rows, sl]
                on = oh * lax.rsqrt(jnp.mean(oh * oh, axis=1, keepdims=True) + RMS_EPS)
                zh = z_ref[rows, sl]
                yb = on * wo_ref[:, sl] * (zh * _sigmoid(zh))
                ya = yp_ref[rows, sl] * ps_ref[:, sl]
                mixed_ref[rows, sl] = _mx(_sigmoid(ga_ref[rows, sl]) * ya + _sigmoid(gb_ref[rows, sl]) * yb)

    def col(blk):
        return pl.BlockSpec((tm, D_MODEL), lambda i: (i, blk))

    return _pc(body, "mix_fwd", (t // tm,),
               [_row(tm, D_MODEL), col(K_Z // D_MODEL), col(K_GA // D_MODEL), col(K_GB // D_MODEL), _row(tm, D_MODEL),
                _const((1, D_MODEL)), _const((1, D_MODEL))],
               _row(tm, D_MODEL), SDS((t, D_MODEL), MXU_DTYPE), sem=("parallel",))(
                   o, proj, proj, proj, ypre, pool_scale, wo_row)


def _mix_bwd(da1_bf, w_out, o, proj, ypre, pool_scale, wo_row, tm, after):
    t = o.shape[0]

    def body(da_ref, wout_ref, o_ref, z_ref, ga_ref, gb_ref, yp_ref, ps_ref, wo_ref, after_ref,
             do_ref, dp_ref, dyp_ref, acc_ref, dm_ref):
        i = pl.program_id(0)

        @pl.when(i == 0)
        def _():
            acc_ref[...] = jnp.zeros_like(acc_ref)

        dm_ref[...] = _dot_nt(da_ref[...], wout_ref[...])
        dwo = jnp.zeros((1, HEAD_DIM), F32)
        for h in range(HEADS):
            sl = slice(h * HEAD_DIM, (h + 1) * HEAD_DIM)
            woh = wo_ref[:, sl]
            psh = ps_ref[:, sl]
            dps = jnp.zeros((1, HEAD_DIM), F32)
            for r in range(0, tm, MIX_ROWS):
                rows = pl.ds(r, MIX_ROWS)
                oh = o_ref[rows, sl]
                rs = lax.rsqrt(jnp.mean(oh * oh, axis=1, keepdims=True) + RMS_EPS)
                on = oh * rs
                zh = z_ref[rows, sl]
                sz = _sigmoid(zh)
                silu = zh * sz
                t1 = on * woh
                yb = t1 * silu
                sa = _sigmoid(ga_ref[rows, sl])
                sb = _sigmoid(gb_ref[rows, sl])
                yp = yp_ref[rows, sl]
                dm = dm_ref[rows, sl]
                ga_sl = slice(D_MODEL + h * HEAD_DIM, D_MODEL + (h + 1) * HEAD_DIM)
                gb_sl = slice(2 * D_MODEL + h * HEAD_DIM, 2 * D_MODEL + (h + 1) * HEAD_DIM)
                dp_ref[rows, ga_sl] = _mx(dm * (yp * psh) * sa * (1.0 - sa))
                dp_ref[rows, gb_sl] = _mx(dm * yb * sb * (1.0 - sb))
                dya = dm * sa
                dyb = dm * sb
                dyp_ref[rows, sl] = _mx(dya * psh)
                dps = dps + jnp.sum(dya * yp, axis=0, keepdims=True)
                dp_ref[rows, sl] = _mx(dyb * t1 * (sz * (1.0 + zh * (1.0 - sz))))
                dt1 = dyb * silu
                dwo = dwo + jnp.sum(dt1 * on, axis=0, keepdims=True)
                don = dt1 * woh
                do_ref[rows, sl] = _mx(rs * (don - on * jnp.mean(don * on, axis=1, keepdims=True)))
            acc_ref[0:1, sl] += dps
        acc_ref[1:2, 0:HEAD_DIM] += dwo

    def col(blk):
        return pl.BlockSpec((tm, D_MODEL), lambda i: (i, blk))

    r = _row(tm, D_MODEL)
    return _pc(body, "mix_bwd", (t // tm,),
               [r, _const((D_MODEL, D_MODEL)), r, col(K_Z // D_MODEL), col(K_GA // D_MODEL), col(K_GB // D_MODEL), r,
                _const((1, D_MODEL)), _const((1, D_MODEL)), ANY],
               [r, pl.BlockSpec((tm, 3 * D_MODEL), lambda i: (i, K_Z // (3 * D_MODEL))), r, _const((8, D_MODEL))],
               [SDS((t, D_MODEL), MXU_DTYPE), SDS((t, CAT_WIDTH), MXU_DTYPE), SDS((t, D_MODEL), MXU_DTYPE),
                SDS((8, D_MODEL), F32)],
               scratch=[pltpu.VMEM((tm, D_MODEL), F32)],
               sem=("arbitrary",))(da1_bf, w_out, o, proj, proj, proj, ypre, pool_scale, wo_row, after)


def _oproj_ln1(mixed, w_out, h0, g1, b1, tm):
    t = mixed.shape[0]

    def body(m_ref, w_ref, h0_ref, g_ref, b_ref, a1_ref, h1_ref, h1b_ref):
        a1 = ALPHA * h0_ref[...] + _dot(m_ref[...], w_ref[...])
        a1_ref[...] = a1
        xhat, _ = _ln_stats(a1)
        h1 = xhat * g_ref[...] + b_ref[...]
        h1_ref[...] = h1
        h1b_ref[...] = _mx(h1)

    r = _row(tm, D_MODEL)
    v = _const((1, D_MODEL))
    return _pc(body, "oproj_ln1", (t // tm,), [r, _const((D_MODEL, D_MODEL)), r, v, v], [r, r, r],
               [SDS((t, D_MODEL), F32), SDS((t, D_MODEL), F32), SDS((t, D_MODEL), MXU_DTYPE)],
               sem=("parallel",))(mixed, w_out, h0, g1, b1)


def _mlp_up(h1_bf, w_up, tm):
    t = h1_bf.shape[0]
    tn = w_up.shape[2]

    def body(h_ref, w_ref, act_ref):
        r = jnp.maximum(_dot(h_ref[...], w_ref[...]), 0.0)
        act_ref[...] = _mx(r * r)

    return _pc(body, "mlp_up", (D_FF // tn, t // tm),
               [pl.BlockSpec((tm, D_MODEL), lambda j, i: (i, 0)),
                pl.BlockSpec((None, D_MODEL, tn), lambda j, i: (j, 0, 0))],
               pl.BlockSpec((tm, tn), lambda j, i: (i, j)), SDS((t, D_FF), MXU_DTYPE),
               sem=("parallel", "parallel"))(h1_bf, w_up)


def _tail(act, w_down, h1, w_gate, p_bf, w_proj, tgt, g2, b2, tm):
    t = act.shape[0]

    def body(act_ref, wd_ref, h1_ref, wg_ref, p_ref, wp_ref, tgt_ref, g_ref, b_ref,
             dr_ref, drb_ref, dgp_ref, dpp_ref, rb_ref, acc_ref):
        i = pl.program_id(0)

        @pl.when(i == 0)
        def _():
            acc_ref[...] = jnp.zeros_like(acc_ref)

        r = ALPHA * h1_ref[...] + _dot(act_ref[...], wd_ref[...])
        rb = _mx(r)
        rb_ref[...] = rb
        gate = _sigmoid(_dot(rb, wg_ref[...]))
        pp = _dot(p_ref[...], wp_ref[...])
        xhat, rstd = _ln_stats(r + gate * pp)
        g = g_ref[...]
        diff = xhat * g + b_ref[...] - tgt_ref[...]
        dh2 = diff * (1.0 / D_MODEL)
        rowloss = jnp.sum(diff * diff, axis=1, keepdims=True) * (0.5 / D_MODEL)
        acc_ref[0:1, :] += jnp.sum(dh2 * xhat, axis=0, keepdims=True)
        acc_ref[1:2, :] += jnp.sum(dh2, axis=0, keepdims=True)
        acc_ref[2:3, :] += jnp.broadcast_to(jnp.sum(rowloss, axis=0, keepdims=True), (1, D_MODEL))
        da2 = _ln_bwd(dh2, xhat, rstd, g)
        dpp_ref[...] = _mx(da2 * gate)
        dgp = _mx(da2 * pp * gate * (1.0 - gate))
        dgp_ref[...] = dgp
        dr = da2 + _dot_nt(dgp, wg_ref[...])
        dr_ref[...] = dr
        drb_ref[...] = _mx(dr)

    r = _row(tm, D_MODEL)
    v = _const((1, D_MODEL))
    return _pc(body, "tail", (t // tm,),
               [_row(tm, D_FF), _const((D_FF, D_MODEL)), r, _const((D_MODEL, D_MODEL)), _row(tm, PLE_DIM),
                _const((PLE_DIM, D_MODEL)), r, v, v],
               [r, r, r, r, r, _const((8, D_MODEL))],
               [SDS((t, D_MODEL), F32)] + [SDS((t, D_MODEL), MXU_DTYPE)] * 4 + [SDS((8, D_MODEL), F32)],
               sem=("arbitrary",))(act, w_down, h1, w_gate, p_bf, w_proj, tgt, g2, b2)


SQRT_GUARD = 1e-30


def _mlp_bwd1(dr_bf, w_down, act, tm, tn):
    t = act.shape[0]

    def body(dr_ref, w_ref, act_ref, dup_ref):
        dact = _dot_nt(dr_ref[...], w_ref[...])
        a = act_ref[...].astype(F32)
        dup_ref[...] = _mx(dact * (2.0 * a * lax.rsqrt(a + SQRT_GUARD)))

    o = pl.BlockSpec((tm, tn), lambda j, i: (i, j))
    return _pc(body, "mlp_bwd1", (D_FF // tn, t // tm),
               [pl.BlockSpec((tm, D_MODEL), lambda j, i: (i, 0)), pl.BlockSpec((tn, D_MODEL), lambda j, i: (j, 0)), o],
               o, SDS((t, D_FF), MXU_DTYPE), sem=("parallel", "parallel"))(dr_bf, w_down, act)


def _mlp_bwd2(dup, w_up, dr, a1, g1, tm):
    t = dr.shape[0]

    nk, tk = w_up.shape[0], w_up.shape[2]

    def body(dup_ref, w_ref, dr_ref, a1_ref, g_ref, da1_ref, da1b_ref, acc_ref):
        i = pl.program_id(0)

        @pl.when(i == 0)
        def _():
            acc_ref[...] = jnp.zeros_like(acc_ref)

        dh1 = ALPHA * dr_ref[...]
        for kk in range(nk):
            dh1 = dh1 + _dot_nt(dup_ref[:, kk * tk:(kk + 1) * tk], w_ref[kk])
        xhat, rstd = _ln_stats(a1_ref[...])
        acc_ref[0:1, :] += jnp.sum(dh1 * xhat, axis=0, keepdims=True)
        acc_ref[1:2, :] += jnp.sum(dh1, axis=0, keepdims=True)
        da1 = _ln_bwd(dh1, xhat, rstd, g_ref[...])
        da1_ref[...] = da1
        da1b_ref[...] = _mx(da1)

    r = _row(tm, D_MODEL)
    return _pc(body, "mlp_bwd2", (t // tm,),
               [_row(tm, D_FF), _const((nk, D_MODEL, tk)), r, r, _const((1, D_MODEL))],
               [r, r, _const((8, D_MODEL))],
               [SDS((t, D_MODEL), F32), SDS((t, D_MODEL), MXU_DTYPE), SDS((8, D_MODEL), F32)],
               sem=("arbitrary",))(dup, w_up, dr, a1, g1)


def _ln_in_bwd(dproj, w_cat, da1, x, g, tm, after):
    t = x.shape[0]

    def body(dp_ref, w_ref, da1_ref, x_ref, g_ref, after_ref, dx_ref, acc_ref):
        i = pl.program_id(0)

        @pl.when(i == 0)
        def _():
            acc_ref[...] = jnp.zeros_like(acc_ref)

        dh0 = _dot_nt(dp_ref[...], w_ref[...]) + ALPHA * da1_ref[...]
        xhat, rstd = _ln_stats(x_ref[...])
        acc_ref[0:1, :] += jnp.sum(dh0 * xhat, axis=0, keepdims=True)
        acc_ref[1:2, :] += jnp.sum(dh0, axis=0, keepdims=True)
        dx_ref[...] = _ln_bwd(dh0, xhat, rstd, g_ref[...])

    r = _row(tm, D_MODEL)
    return _pc(body, "ln_in_bwd", (t // tm,),
               [_row(tm, CAT_WIDTH), _const((D_MODEL, CAT_WIDTH)), r, r, _const((1, D_MODEL)), ANY],
               [r, _const((8, D_MODEL))], [SDS((t, D_MODEL), F32), SDS((8, D_MODEL), F32)],
               sem=("arbitrary",))(dproj, w_cat, da1, x, g, after)


def _local_step(x, p, tgt, wts, start_token, first_weights, late_weights, send_late_grads, send_early_grads):
    t = x.shape[0]
    tm = min(512, t)
    tms = min(256, t)
    row = lambda a: a.reshape(1, -1)
    pool_scale = row(wts["pool_scale"])
    wo_row = jnp.tile(row(wts["o_norm_w"]), (1, HEADS))
    pad8 = jnp.zeros((1, HEADS), F32)
    al_row = jnp.concatenate([pad8, row(wts["a_log"]), jnp.zeros((1, 128 - 2 * HEADS), F32)], axis=1)
    dtb_row = jnp.concatenate([pad8, row(wts["dt_bias"]), jnp.zeros((1, 128 - 2 * HEADS), F32)], axis=1)
    g_in, b_in = row(wts["ln_in_g"]), row(wts["ln_in_b"])
    g1, b1 = row(wts["ln1_g"]), row(wts["ln1_b"])
    g2, b2 = row(wts["ln2_g"]), row(wts["ln2_b"])

    h0, h0_bf = _ln_in(x, g_in, b_in, tm, start_token)
    first, first_token = first_weights(h0_bf)
    wts = {**wts, **first}
    w_cat = wts["w_cat"]
    proj = _proj(h0_bf, w_cat, tms, first_token)
    ypre, d_bf = _pool_fwd(proj, wts["pool_w"], tm)
    qkv_act, dsilu = _conv_fwd(proj, wts["conv_w"], tm)
    bg = _ba_fwd(proj, al_row, dtb_row, tm)
    bgt = bg[:, :2 * HEADS].T
    u, w, qg, kg, attn, ymat, egl = _dn_local_fwd(qkv_act, bg, bgt)
    o, vn, states = _dn_scan_fwd(u, w, qg, kg, attn, egl)
    mixed = _mix_fwd(o, proj, ypre, pool_scale, wo_row, tm)
    wts = {**wts, **late_weights(mixed)}
    a1, h1, h1_bf = _oproj_ln1(mixed, wts["w_out"], h0, g1, b1, tm)
    act = _mlp_up(h1_bf, wts["w_up"], tm)
    p_bf = _mx(p)
    dr, dr_bf, dgp, dpp, r_bf, acc_tail = _tail(act, wts["w_down"], h1, wts["ple_gate_w"], p_bf, wts["ple_proj_w"],
                                                tgt, g2, b2, tms)
    grads = {}
    grads["ple_proj_w"] = _matmul(p_bf, dpp, "tn", "dw_ple_proj", WIRE_DTYPE, tm=256, tn=1024, tk=DW_TK)
    grads["ple_gate_w"] = _matmul(r_bf, dgp, "tn", "dw_ple_gate", WIRE_DTYPE, tm=DW_TM, tn=1024, tk=DW_TK)
    grads["w_down"] = _matmul(act, dr_bf, "tn", "dw_down", WIRE_DTYPE, tm=DW_TM, tn=1024, tk=DW_TK)
    dup = _mlp_bwd1(dr_bf, wts["w_down"], act, tm, 1024)
    grads["w_up"] = _matmul(h1_bf, dup, "tn", "dw_up", WIRE_DTYPE, tm=DW_TM, tn=1024, tk=DW_TK, stack_out=True)
    da1, da1_bf, acc_ln1 = _mlp_bwd2(dup, wts["w_up"], dr, a1, g1, tms)
    grads["w_out"] = _matmul(mixed, da1_bf, "tn", "dw_out", WIRE_DTYPE, tm=DW_TM, tn=1024, tk=DW_TK)
    sent = send_late_grads(grads)
    do, dproj, dyp, acc_mix = _mix_bwd(da1_bf, wts["w_out"], o, proj, ypre, pool_scale, wo_row, tms, sent)
    dproj, grads["pool_w"] = _pool_bwd(dyp, d_bf, wts["pool_w"], dproj, tm)
    dvn, dkg, dqg, dattn, dw, degl = _dn_scan_bwd(do, qg, kg, w, attn, vn, states, egl)
    dqkv_act, dbg = _dn_local_bwd(qkv_act, bg, bgt, u, w, ymat, dvn, dw, dqg, dkg, dattn, degl)
    dproj, acc_conv = _conv_bwd(dqkv_act, dsilu, proj, wts["conv_w"], dproj, tm)
    dproj, acc_ba = _ba_bwd(dbg, bg, proj, al_row, dtb_row, dproj, tm)
    dw_cat = _matmul(h0_bf, dproj, "tn", "dw_in", F32, tm=DW_TM, tn=1152, tk=DW_TK)
    grads["w_in"] = jnp.concatenate(
        [dw_cat[:, K_U:K_U + 512], dw_cat[:, K_QKV:K_QKV + 3072], dw_cat[:, K_Z:K_Z + 1024],
         dw_cat[:, K_BA:K_BA + 16], dw_cat[:, K_GA:K_GA + 1024], dw_cat[:, K_GB:K_GB + 1024]], axis=1)
    sent = send_early_grads(grads)
    grad_x, acc_in = _ln_in_bwd(dproj, w_cat, da1, x, g_in, tms, sent)

    grads["conv_w"] = acc_conv[0:CONV_K]
    grads["ln_in_g"], grads["ln_in_b"] = acc_in[0], acc_in[1]
    grads["ln1_g"], grads["ln1_b"] = acc_ln1[0], acc_ln1[1]
    grads["ln2_g"], grads["ln2_b"] = acc_tail[0], acc_tail[1]
    grads["pool_scale"] = acc_mix[0]
    grads["o_norm_w"] = acc_mix[1, 0:HEAD_DIM]
    grads["a_log"] = acc_ba[0, HEADS:2 * HEADS]
    grads["dt_bias"] = acc_ba[1, HEADS:2 * HEADS]
    loss = acc_tail[2, 0]
    return grad_x, grads, loss


MESH = pl.DeviceIdType.MESH
ANY = pl.BlockSpec(memory_space=pl.ANY)


def _chip_of(k, x, y):
    chip = (2 * x + y + k) % N_CHIPS
    return chip // 2, chip % 2


def _place():
    x, y, c = lax.axis_index("x"), lax.axis_index("y"), lax.axis_index("c")
    return x, y, c, 2 * x + y


def _half(rows, c):
    return pl.ds(pl.multiple_of(c * (rows // 2), 16), rows // 2)


def _remote(src, dst, send_sem, recv_sem, device_id):
    return pltpu.make_async_remote_copy(src_ref=src, dst_ref=dst, send_sem=send_sem, recv_sem=recv_sem,
                                        device_id=device_id, device_id_type=MESH)


def _tile_rows(rows):
    for tr in (256, 128, 64, 32, 16):
        if rows % tr == 0:
            return tr
    raise ValueError(rows)


def _first_gather_copies(srcs, lands, send, recv, place):
    copies = []
    for a in range(len(srcs)):
        whole = a == len(srcs) - 1
        for k in range(N_CHIPS):
            if place is None:
                copies.append(None)
                continue
            x, y, c, me = place
            sems = (send.at[a * N_CHIPS + k], recv.at[a * N_CHIPS + k])
            if k == 0:
                copies.append(_remote(srcs[a], lands[a].at[me], *sems, (x, y, 1 - c)))
                continue
            tx, ty = _chip_of(k, x, y)
            if whole:
                copies.append(_remote(srcs[a], lands[a].at[me], *sems, (tx, ty, c)))
            else:
                mine = _half(srcs[a].shape[0], c)
                copies.append(_remote(srcs[a].at[mine], lands[a].at[me, mine], *sems, (tx, ty, c)))
    return copies


def _pass_halves(stacks):
    n = len(stacks)

    def body(*refs):
        outs = refs[n:2 * n]
        send, recv = refs[2 * n:]
        x, y, c, me = _place()
        copies = []
        for a in range(n):
            for k in range(1, N_CHIPS):
                landed = outs[a].at[(me + N_CHIPS - k) % N_CHIPS, _half(stacks[a].shape[1], c)]
                copies.append(_remote(landed, landed, send.at[a * N_CHIPS + k], recv.at[a * N_CHIPS + k],
                                      (x, y, 1 - c)))
        for cp in copies:
            cp.start()
        for cp in copies:
            cp.wait_send()
        for a in range(n):
            for k in range(1, N_CHIPS):
                passed = outs[a].at[(me + N_CHIPS - k) % N_CHIPS, _half(stacks[a].shape[1], 1 - c)]
                _remote(passed, passed, send.at[a * N_CHIPS + k], recv.at[a * N_CHIPS + k], (x, y, c)).wait_recv()

    sems = pltpu.SemaphoreType.DMA((n * N_CHIPS,))
    return pl.pallas_call(
        body, name="pass_halves", out_shape=[SDS(s.shape, s.dtype) for s in stacks],
        in_specs=[ANY] * n, out_specs=[ANY] * n, scratch_shapes=[sems, sems],
        input_output_aliases={a: a for a in range(n)},
    )(*stacks)


def _swap_halves(gs):
    n = len(gs)

    def body(*refs):
        ins, theirs = refs[0:n], refs[n:2 * n]
        send, recv = refs[2 * n:]
        x, y, c, _ = _place()
        copies = [_remote(ins[a].at[:, _half(gs[a].shape[1], 1 - c)], theirs[a], send.at[a], recv.at[a],
                          (x, y, 1 - c)) for a in range(n)]
        for cp in copies:
            cp.start()
        for cp in copies:
            cp.wait()

    return pl.pallas_call(
        body, name="swap_halves", out_shape=[SDS((N_CHIPS, g.shape[1] // 2, g.shape[2]), g.dtype) for g in gs],
        in_specs=[ANY] * n, out_specs=[ANY] * n, scratch_shapes=[pltpu.SemaphoreType.DMA((n,))] * 2,
    )(*gs)


def _send_to_sibling(hs):
    n = len(hs)

    def body(*refs):
        ins, outs = refs[0:n], refs[n:2 * n]
        send, recv = refs[2 * n:]
        x, y, c, _ = _place()
        copies = [_remote(ins[a], outs[a], send.at[a], recv.at[a], (x, y, 1 - c)) for a in range(n)]
        for cp in copies:
            cp.start()
        for cp in copies:
            cp.wait()

    return pl.pallas_call(
        body, name="send_to_sibling", out_shape=[SDS(h.shape, h.dtype) for h in hs],
        in_specs=[ANY] * n, out_specs=[ANY] * n, scratch_shapes=[pltpu.SemaphoreType.DMA((n,))] * 2,
    )(*hs)


HBM = pl.BlockSpec(memory_space=pltpu.HBM)
SEM = pl.BlockSpec(memory_space=pltpu.SEMAPHORE)
EFFECT = pltpu.SideEffectType.DATAFLOW_SIDE_EFFECTING


def _in_hbm(a):
    return pltpu.with_memory_space_constraint(a, pltpu.HBM)


def _split_copy_start(name, srcs, lands, copies_of, after):
    n = len(srcs)
    n_copies = len(copies_of(srcs, lands, None, None, None))

    def body(*refs):
        src_refs, land_refs = refs[0:n], refs[n:2 * n]
        send, recv = refs[2 * n + 1], refs[2 * n + 2]
        token = refs[-1]
        for cp in copies_of(src_refs, land_refs, send, recv, _place()):
            cp.start()
        token[...] = jnp.zeros_like(token)

    sems = pltpu.SemaphoreType.DMA((n_copies,))
    out = pl.pallas_call(
        body, name=name,
        out_shape=[sems, sems] + [pltpu.HBM(a.shape, a.dtype) for a in list(srcs) + list(lands)] + [SDS((8, 128), F32)],
        in_specs=[HBM] * (2 * n) + [ANY],
        out_specs=[SEM, SEM] + [HBM] * (2 * n) + [pl.BlockSpec(memory_space=pltpu.VMEM)],
        input_output_aliases={i: 2 + i for i in range(2 * n)},
        compiler_params=pltpu.CompilerParams(has_side_effects=EFFECT),
    )(*[_in_hbm(a) for a in list(srcs) + list(lands)], after)
    return out[0], out[1], out[2:2 + n], out[2 + n:2 + 2 * n], out[-1]


def _split_copy_wait(name, send, recv, srcs, lands, after, copies_of):
    n = len(srcs)

    def body(*refs):
        src_refs, land_refs = refs[0:n], refs[n:2 * n]
        send_ref, recv_ref = refs[2 * n], refs[2 * n + 1]
        for cp in copies_of(src_refs, land_refs, send_ref, recv_ref, _place()):
            cp.wait_send()
            cp.wait_recv()

    out = pl.pallas_call(
        body, name=name, out_shape=[pltpu.HBM(a.shape, a.dtype) for a in list(srcs) + list(lands)],
        in_specs=[HBM] * (2 * n) + [SEM, SEM, ANY], out_specs=[HBM] * (2 * n),
        input_output_aliases={i: i for i in range(2 * n)},
        compiler_params=pltpu.CompilerParams(has_side_effects=EFFECT),
    )(*srcs, *lands, send, recv, after)
    return out[0:n], out[n:2 * n]


def _late_gather_copies(srcs, lands, send, recv, place):
    copies = []
    for a in range(len(srcs)):
        for k in range(N_CHIPS):
            if place is None:
                copies.append(None)
                continue
            x, y, c, me = place
            if k == 0:
                target = (x, y, 1 - c)
            else:
                tx, ty = _chip_of(k, x, y)
                target = (tx, ty, c)
            copies.append(_remote(srcs[a], lands[a].at[me], send.at[a * N_CHIPS + k], recv.at[a * N_CHIPS + k], target))
    return copies


def _late_scatter_copies(srcs, lands, send, recv, place):
    copies = []
    for a in range(len(srcs)):
        for k in range(1, N_CHIPS):
            if place is None:
                copies.append(None)
                continue
            x, y, c, _ = place
            tx, ty = _chip_of(k, x, y)
            copies.append(_remote(srcs[a].at[2 * tx + ty], lands[a].at[k - 1], send.at[a * (N_CHIPS - 1) + k - 1],
                                  recv.at[a * (N_CHIPS - 1) + k - 1], (tx, ty, c)))
    return copies


def _add_pair(g, theirs, name):
    _, rows, cols = g.shape
    half = rows // 2
    tr = _tile_rows(half)

    def body(g_ref, t_ref, o_ref):
        own = g_ref[lax.axis_index("c")]
        o_ref[...] = (own.astype(F32) + t_ref[...].astype(F32)).astype(o_ref.dtype)

    blk = pl.BlockSpec((None, tr, cols), lambda j, i: (j, i, 0))
    return _pc(body, "add_" + name, (N_CHIPS, half // tr),
               [pl.BlockSpec((None, 2, tr, cols), lambda j, i: (j, 0, i, 0)), blk], blk,
               SDS((N_CHIPS, half, cols), g.dtype), sem=("parallel", "parallel"))(
                   g.reshape(N_CHIPS, 2, half, cols), theirs)


def _sum_slabs(pair, landed, name):
    _, rows, cols = pair.shape
    tr = _tile_rows(rows)

    def body(p_ref, r_ref, o_ref):
        acc = p_ref[2 * lax.axis_index("x") + lax.axis_index("y")].astype(F32)
        for k in range(N_CHIPS - 1):
            acc = acc + r_ref[k].astype(F32)
        o_ref[...] = acc

    return _pc(body, "sum_" + name, (rows // tr,),
               [pl.BlockSpec((N_CHIPS, tr, cols), lambda i: (0, i, 0)),
                pl.BlockSpec((N_CHIPS - 1, tr, cols), lambda i: (0, i, 0))],
               _row(tr, cols), SDS((rows, cols), F32), sem=("parallel",))(pair, landed)


def _adamw_math(w, g, m, v):
    m = ADAM_B1 * m + (1.0 - ADAM_B1) * g
    v = ADAM_B2 * v + (1.0 - ADAM_B2) * (g * g)
    m_hat = m / (1.0 - ADAM_B1 ** ADAM_STEP)
    v_hat = v / (1.0 - ADAM_B2 ** ADAM_STEP)
    delta = -ADAM_LR * (m_hat / (jnp.sqrt(v_hat) + ADAM_EPS) + ADAM_WD * w)
    return delta, m, v


def _adamw_2d(w, g_own, g_sib, m, v, name, halves):
    lead = w.ndim == 3
    rows, cols = w.shape[-2:]
    tr = _tile_rows(rows // 2)
    nh = rows // 2 // tr if halves else rows // tr

    def body(w_ref, go_ref, gs_ref, m_ref, v_ref, g_out, d_out, m_out, v_out):
        if halves:
            mine = (pl.program_id(0) // nh) == lax.axis_index("c")
            g = jnp.where(mine, go_ref[...], gs_ref[...])
        else:
            g = go_ref[...] + gs_ref[...]
        delta, mn, vn = _adamw_math(w_ref[...], g, m_ref[...], v_ref[...])
        g_out[...] = g
        d_out[...] = delta
        m_out[...] = mn
        v_out[...] = vn

    r = _row(tr, cols)
    p = pl.BlockSpec((None, tr, cols), lambda i: (0, i, 0)) if lead else r
    h = pl.BlockSpec((tr, cols), lambda i: (i % nh, 0))
    return _pc(body, "adamw_" + name, (rows // tr,), [p, h, h, p, p], [r] * 4, [SDS((rows, cols), F32)] * 4,
               sem=("parallel",))(w, g_own, g_sib, m, v)


def _small_allreduce_adamw(mine, w, m, v):
    shape = mine.shape

    def body(mine_ref, w_ref, m_ref, v_ref, g_out, d_out, m_out, v_out, buf_ref, send_sems, recv_sems):
        x, y, c = lax.axis_index("x"), lax.axis_index("y"), lax.axis_index("c")
        me = 4 * x + 2 * y + c
        buf_ref[me] = mine_ref[...]
        copies = []
        for k in range(1, N_DEV):
            tgt = (me + k) % N_DEV
            copies.append(pltpu.make_async_remote_copy(
                src_ref=mine_ref, dst_ref=buf_ref.at[me], send_sem=send_sems.at[k], recv_sem=recv_sems.at[k],
                device_id=(tgt // 4, (tgt // 2) % 2, tgt % 2), device_id_type=MESH))
        for cp in copies:
            cp.start()
        for k in range(1, N_DEV):
            src = (me + N_DEV - k) % N_DEV
            pltpu.make_async_remote_copy(
                src_ref=mine_ref, dst_ref=buf_ref.at[src], send_sem=send_sems.at[k], recv_sem=recv_sems.at[k],
                device_id=(x, y, c), device_id_type=MESH).wait_recv()
        for cp in copies:
            cp.wait_send()
        g = buf_ref[0]
        for j in range(1, N_DEV):
            g = g + buf_ref[j]
        delta, mn, vn = _adamw_math(w_ref[...], g, m_ref[...], v_ref[...])
        g_out[...] = g
        d_out[...] = delta
        m_out[...] = mn
        v_out[...] = vn

    vm = pl.BlockSpec(memory_space=pltpu.VMEM)
    return pl.pallas_call(
        body, name="small_allreduce_adamw", out_shape=[SDS(shape, F32)] * 4, in_specs=[vm] * 4, out_specs=[vm] * 4,
        scratch_shapes=[pltpu.VMEM((N_DEV,) + shape, F32), pltpu.SemaphoreType.DMA((N_DEV,)),
                        pltpu.SemaphoreType.DMA((N_DEV,))],
    )(mine, w, m, v)


def _as2d(a):
    return a.reshape(-1, a.shape[-1])


def _w_cat(stack):
    wi = stack.transpose(1, 0, 2).reshape(D_MODEL, IN_WIDTH)
    return jnp.concatenate(
        [wi[:, C_QKV:C_Z], wi[:, C_Z:C_BETA], wi[:, C_GA:C_GB], wi[:, C_GB:IN_WIDTH], wi[:, C_POOL:C_QKV],
         wi[:, C_BETA:C_GA], jnp.zeros((D_MODEL, CAT_WIDTH - K_BA - 2 * HEADS), wi.dtype)], axis=1)


WEIGHT_LAYOUT = {
    "w_in": lambda s: ("w_cat", _w_cat(s)),
    "pool_w": lambda s: ("pool_w", s.reshape(N_CHIPS, 4, POOL_GROUP, POOL_OUT_GROUP // N_CHIPS)
                         .transpose(1, 2, 0, 3).reshape(4, POOL_GROUP, POOL_OUT_GROUP)),
    "w_out": lambda s: ("w_out", s.reshape(D_MODEL, D_MODEL)),
    "w_up": lambda s: ("w_up", s),
    "w_down": lambda s: ("w_down", s.reshape(D_FF, D_MODEL)),
    "ple_gate_w": lambda s: ("ple_gate_w", s.reshape(D_MODEL, D_MODEL)),
    "ple_proj_w": lambda s: ("ple_proj_w", s.transpose(1, 0, 2).reshape(PLE_DIM, D_MODEL)),
}

GRAD_LAYOUT = {
    "w_in": lambda g: g.reshape(D_MODEL, N_CHIPS, IN_WIDTH // N_CHIPS).transpose(1, 0, 2),
    "pool_w": lambda g: g.reshape(4, POOL_GROUP, N_CHIPS, POOL_OUT_GROUP // N_CHIPS)
                         .transpose(2, 0, 1, 3).reshape(N_CHIPS, 4 * POOL_GROUP, POOL_OUT_GROUP // N_CHIPS),
    "w_out": lambda g: g.reshape(N_CHIPS, D_MODEL // N_CHIPS, D_MODEL),
    "w_up": lambda g: g,
    "w_down": lambda g: g.reshape(N_CHIPS, D_FF // N_CHIPS, D_MODEL),
    "ple_gate_w": lambda g: g.reshape(N_CHIPS, D_MODEL // N_CHIPS, D_MODEL),
    "ple_proj_w": lambda g: g.reshape(PLE_DIM, N_CHIPS, D_MODEL // N_CHIPS).transpose(1, 0, 2),
}


def _full_weights(names, stacks):
    return dict(WEIGHT_LAYOUT[n](s.astype(MXU_DTYPE)) for n, s in zip(names, stacks))


def _grads_by_chip(names, grads):
    return [GRAD_LAYOUT[n](grads[n]).astype(WIRE_DTYPE) for n in names]


def _pack_small(rows, conv, name):
    n = len(rows)

    def body(*refs):
        out = refs[n + 1]
        out[...] = jnp.zeros_like(out)
        for i in range(n):
            out[i:i + 1, :] = refs[i][...]
        out[SMALL_CONV_AT:SMALL_CONV_AT + SMALL_CONV_ROWS, :] = refs[n][...]

    vm = pl.BlockSpec(memory_space=pltpu.VMEM)
    return pl.pallas_call(body, name=name, out_shape=SDS((SMALL_CONV_AT + SMALL_CONV_ROWS, D_MODEL), F32),
                          in_specs=[vm] * (n + 1), out_specs=vm)(*rows, conv)


def _pad_row(a):
    a = a.reshape(1, -1).astype(F32)
    return jnp.pad(a, ((0, 0), (0, D_MODEL - a.shape[1])))


def kernel(x, p, ln_in_g, ln_in_b, w_in, pool_w, pool_scale, conv_w, a_log, dt_bias, o_norm_w, w_out, ln1_g, ln1_b, w_up, w_down, ple_gate_w, ple_proj_w, ln2_g, ln2_b, loss_target, m_ln_in_g, m_ln_in_b, m_w_in, m_pool_w, m_pool_scale, m_conv_w, m_a_log, m_dt_bias, m_o_norm_w, m_w_out, m_ln1_g, m_ln1_b, m_w_up, m_w_down, m_ple_gate_w, m_ple_proj_w, m_ln2_g, m_ln2_b, v_ln_in_g, v_ln_in_b, v_w_in, v_pool_w, v_pool_scale, v_conv_w, v_a_log, v_dt_bias, v_o_norm_w, v_w_out, v_ln1_g, v_ln1_b, v_w_up, v_w_down, v_ple_gate_w, v_ple_proj_w, v_ln2_g, v_ln2_b):
    given = dict(locals())
    chip = 2 * lax.axis_index("x") + lax.axis_index("y")

    shard = lambda n: _as2d(given[n]).astype(WIRE_DTYPE)

    wts = {"ln_in_g": ln_in_g, "ln_in_b": ln_in_b, "pool_scale": pool_scale[0], "a_log": a_log[0],
           "dt_bias": dt_bias[0], "o_norm_w": o_norm_w[0], "ln1_g": ln1_g[0], "ln1_b": ln1_b[0],
           "ln2_g": ln2_g[0], "ln2_b": ln2_b[0]}

    conv_pad = jnp.pad(conv_w[0], ((0, 8 - CONV_K), (0, 0)))
    first_srcs = [shard(n) for n in EARLY] + [conv_pad]
    first_lands = [lax.empty((N_CHIPS,) + s.shape, s.dtype) for s in first_srcs]
    fsend, frecv, fsrcs, flands, start_token = _split_copy_start(
        "first_gather_start", first_srcs, first_lands, _first_gather_copies, first_srcs[0])
    late = {}

    def first_weights(after):
        _, lands = _split_copy_wait("first_gather_wait", fsend, frecv, fsrcs, flands, after, _first_gather_copies)
        stacks = _pass_halves(lands[0:len(EARLY)])
        first = _full_weights(EARLY, stacks)
        first["conv_w"] = jnp.concatenate([lands[len(EARLY)][j, 0:CONV_K] for j in range(N_CHIPS)], axis=1)
        late_srcs = [shard(n) for n in LATE]
        late_lands = [lax.empty((N_CHIPS,) + s.shape, s.dtype) for s in late_srcs]
        late["send"], late["recv"], late["srcs"], late["lands"], token = _split_copy_start(
            "late_gather_start", late_srcs, late_lands, _late_gather_copies, stacks[0])
        return first, token

    def late_weights(after):
        _, stacks = _split_copy_wait("late_gather_wait", late["send"], late["recv"], late["srcs"], late["lands"],
                                     after, _late_gather_copies)
        return _full_weights(LATE, stacks)

    scatter = {}

    def send_late_grads(grads):
        srcs = _grads_by_chip(LATE, grads)
        lands = [lax.empty((N_CHIPS - 1,) + g.shape[1:], g.dtype) for g in srcs]
        scatter["send"], scatter["recv"], scatter["srcs"], scatter["lands"], token = _split_copy_start(
            "late_scatter_start", srcs, lands, _late_scatter_copies, srcs[0])
        return token

    last = {}

    def send_early_grads(grads):
        by_chip = _grads_by_chip(EARLY, grads)
        theirs = _swap_halves(by_chip)
        pair = [_add_pair(g, t, n) for g, t, n in zip(by_chip, theirs, EARLY)]
        lands = [lax.empty((N_CHIPS - 1,) + q.shape[1:], q.dtype) for q in pair]
        last["send"], last["recv"], last["srcs"], last["lands"], token = _split_copy_start(
            "early_scatter_start", pair, lands, _late_scatter_copies, pair[0])
        return token

    grad_x, grads, loss = _local_step(x[0], p[0, 0], loss_target[0], wts, start_token, first_weights, late_weights,
                                      send_late_grads, send_early_grads)

    late_mine, late_landed = _split_copy_wait("late_scatter_wait", scatter["send"], scatter["recv"], scatter["srcs"],
                                              scatter["lands"], grad_x, _late_scatter_copies)
    late_part = [_sum_slabs(q, r, n) for q, r, n in zip(late_mine, late_landed, LATE)]
    pair, landed = _split_copy_wait("early_scatter_wait", last["send"], last["recv"], last["srcs"], last["lands"],
                                    grad_x, _late_scatter_copies)
    reduced = [_sum_slabs(q, r, n) for q, r, n in zip(pair, landed, EARLY)]
    from_sibling = _send_to_sibling(reduced + late_part)
    big_out = {}
    for n, g_own, g_sib in zip(EARLY + LATE, reduced + late_part, from_sibling):
        view = (lambda a: a) if given[n].ndim == 3 else _as2d
        res = _adamw_2d(view(given[n]), g_own, g_sib, view(given["m_" + n]), view(given["v_" + n]), n,
                        halves=n in EARLY)
        big_out[n] = [r.reshape(given[n].shape) for r in res]

    conv_cols = QKV_WIDTH // N_CHIPS

    def small_pack(get, conv, extra, name):
        if conv.shape[1] != QKV_WIDTH:
            conv = lax.dynamic_update_slice(jnp.zeros((CONV_K, QKV_WIDTH), F32), conv, (0, chip * conv_cols))
        return _pack_small([_pad_row(get(n)) for n in SMALL_NAMES] + extra, conv.reshape(SMALL_CONV_ROWS, D_MODEL), name)

    mine_small = small_pack(lambda n: grads[n], grads["conv_w"], [jnp.full((1, D_MODEL), loss, F32)], "pack_small_g")
    packed_small = [small_pack(lambda n: given[prefix + n], given[prefix + "conv_w"][0], [], "pack_small_" + tag)
                    for prefix, tag in (("", "w"), ("m_", "m"), ("v_", "v"))]
    small_out = _small_allreduce_adamw(mine_small, *packed_small)

    def small_get(k, n):
        if n == "conv_w":
            full = small_out[k][SMALL_CONV_AT:SMALL_CONV_AT + SMALL_CONV_ROWS].reshape(CONV_K, QKV_WIDTH)
            return lax.dynamic_slice(full, (0, chip * conv_cols), (CONV_K, conv_cols)).reshape(given[n].shape)
        i = SMALL_NAMES.index(n)
        return small_out[k][i, 0:given[n].size].reshape(given[n].shape)

    order = ["ln_in_g", "ln_in_b", "w_in", "pool_w", "pool_scale", "conv_w", "a_log", "dt_bias", "o_norm_w", "w_out",
             "ln1_g", "ln1_b", "w_up", "w_down", "ple_gate_w", "ple_proj_w", "ln2_g", "ln2_b"]
    outs = [small_out[0][len(SMALL_NAMES), 0], grad_x[None]]
    for k in range(4):
        for n in order:
            outs.append(big_out[n][k] if n in big_out else small_get(k, n))
    return tuple(outs)
```

```python
import jax
import jax.numpy as jnp
from jax import lax
from jax.experimental import pallas as pl
from jax.experimental.pallas import tpu as pltpu

F32 = jnp.float32
MXU_DTYPE = jnp.bfloat16
WIRE_DTYPE = jnp.bfloat16
SDS = jax.ShapeDtypeStruct

D_MODEL = 1024
POOL_WINDOWS = (2, 4, 8, 16)
POOL_WIDTH = 512
POOL_GROUP = 128
POOL_OUT_GROUP = 256
HEADS = 8
HEAD_DIM = 128
DN_WIDTH = HEADS * HEAD_DIM
QKV_WIDTH = 3 * DN_WIDTH
CONV_K = 4
CHUNK = 128
DW_TK = 1024
DW_TM = 1024
D_FF = 4096
PLE_DIM = 256
LN_EPS = 1e-5
RMS_EPS = 1e-6
L2_EPS = 1e-6
ALPHA = 2.0 ** 0.25
Q_SCALE = HEAD_DIM ** -0.5
IN_WIDTH = 6672
C_POOL, C_QKV, C_Z, C_BETA, C_A, C_GA, C_GB = 0, 512, 3584, 4608, 4616, 4624, 5648
K_QKV, K_Z, K_GA, K_GB, K_U, K_BA, CAT_WIDTH = 0, 3072, 4096, 5120, 6144, 6656, 6912

ADAM_LR, ADAM_B1, ADAM_B2, ADAM_EPS, ADAM_WD, ADAM_STEP = 0.001, 0.9, 0.999, 1e-08, 0.01, 10

N_CHIPS = 4
N_DEV = 8
VMEM_LIMIT = 56 * 1024 * 1024

EARLY = ("w_in", "pool_w")
LATE = ("w_out", "w_up", "w_down", "ple_gate_w", "ple_proj_w")
SMALL_NAMES = ("ln_in_g", "ln_in_b", "pool_scale", "ln1_g", "ln1_b", "ln2_g", "ln2_b", "o_norm_w", "a_log", "dt_bias")
SMALL_CONV_AT = 12
SMALL_CONV_ROWS = CONV_K * QKV_WIDTH // D_MODEL


def _mx(a):
    return a.astype(MXU_DTYPE)


def _dot(a, b):
    return lax.dot_general(_mx(a), _mx(b), (((1,), (0,)), ((), ())), preferred_element_type=F32)


def _dot_nt(a, b):
    return lax.dot_general(_mx(a), _mx(b), (((1,), (1,)), ((), ())), preferred_element_type=F32)


def _dot_tn(a, b):
    return lax.dot_general(_mx(a), _mx(b), (((0,), (0,)), ((), ())), preferred_element_type=F32)


def _sigmoid(x):
    return 0.5 * jnp.tanh(0.5 * x) + 0.5


def _softplus(x):
    return jnp.maximum(x, 0.0) + jnp.log(1.0 + jnp.exp(-jnp.abs(x)))


def _pc(body, name, grid, in_specs, out_specs, out_shape, scratch=(), sem=None, aliases=None):
    return pl.pallas_call(
        body, out_shape=out_shape, grid=grid, in_specs=in_specs, out_specs=out_specs,
        scratch_shapes=scratch, name=name, input_output_aliases=aliases or {},
        compiler_params=pltpu.CompilerParams(dimension_semantics=sem, vmem_limit_bytes=VMEM_LIMIT))


def _row(tm, n):
    return pl.BlockSpec((tm, n), lambda i: (i, 0))


def _const(shape):
    nd = len(shape)
    return pl.BlockSpec(shape, lambda *_: (0,) * nd)


def _matmul(a, b, mode, name, out_dtype=F32, tm=512, tn=512, tk=512, stack_out=False):
    if mode == "nn":
        (m, k), n = a.shape, b.shape[1]
    elif mode == "nt":
        (m, k), n = a.shape, b.shape[0]
    else:
        (k, m), n = a.shape, b.shape[1]
    tm, tn, tk = min(tm, m), min(tn, n), min(tk, k)
    assert m % tm == 0 and n % tn == 0 and k % tk == 0, (name, m, n, k, tm, tn, tk)
    nk = k // tk
    if mode == "nn":
        a_spec = pl.BlockSpec((tm, tk), lambda i, j, kk: (i, kk))
        b_spec = pl.BlockSpec((tk, tn), lambda i, j, kk: (kk, j))
        dot = _dot
    elif mode == "nt":
        a_spec = pl.BlockSpec((tm, tk), lambda i, j, kk: (i, kk))
        b_spec = pl.BlockSpec((tn, tk), lambda i, j, kk: (j, kk))
        dot = _dot_nt
    else:
        a_spec = pl.BlockSpec((tk, tm), lambda i, j, kk: (kk, i))
        b_spec = pl.BlockSpec((tk, tn), lambda i, j, kk: (kk, j))
        dot = _dot_tn

    def body(a_ref, b_ref, o_ref, *acc):
        if nk == 1:
            o_ref[...] = dot(a_ref[...], b_ref[...]).astype(out_dtype)
            return
        acc_ref, kk = acc[0], pl.program_id(2)

        @pl.when(kk == 0)
        def _():
            acc_ref[...] = dot(a_ref[...], b_ref[...])

        @pl.when((kk > 0) & (kk < nk - 1))
        def _():
            acc_ref[...] += dot(a_ref[...], b_ref[...])

        @pl.when(kk == nk - 1)
        def _():
            o_ref[...] = (acc_ref[...] + dot(a_ref[...], b_ref[...])).astype(out_dtype)

    if stack_out:
        o_spec, o_shape = pl.BlockSpec((None, tm, tn), lambda i, j, kk: (j, i, 0)), SDS((n // tn, m, tn), out_dtype)
    else:
        o_spec, o_shape = pl.BlockSpec((tm, tn), lambda i, j, kk: (i, j)), SDS((m, n), out_dtype)
    return _pc(body, name, (m // tm, n // tn, nk), [a_spec, b_spec], o_spec, o_shape,
               scratch=[pltpu.VMEM((tm, tn), F32)] if nk > 1 else [],
               sem=("parallel", "parallel", "arbitrary"))(a, b)


PROJ_TN = 1152


def _proj(h0_bf, w_cat, tm, after):
    t = h0_bf.shape[0]

    def body(h_ref, w_ref, after_ref, o_ref):
        h = h_ref[...]
        for c0 in range(0, CAT_WIDTH, PROJ_TN):
            o_ref[:, c0:c0 + PROJ_TN] = _dot(h, w_ref[:, c0:c0 + PROJ_TN])

    return _pc(body, "proj", (t // tm,), [_row(tm, D_MODEL), _const((D_MODEL, CAT_WIDTH)), ANY],
               _row(tm, CAT_WIDTH), SDS((t, CAT_WIDTH), F32), sem=("parallel",))(h0_bf, w_cat, after)


def _ln_stats(x):
    mu = jnp.mean(x, axis=-1, keepdims=True)
    xc = x - mu
    var = jnp.mean(xc * xc, axis=-1, keepdims=True)
    rstd = lax.rsqrt(var + LN_EPS)
    return xc * rstd, rstd


def _ln_bwd(dy, xhat, rstd, g):
    dxh = dy * g
    m1 = jnp.mean(dxh, axis=-1, keepdims=True)
    m2 = jnp.mean(dxh * xhat, axis=-1, keepdims=True)
    return rstd * (dxh - m1 - xhat * m2)


def _ln_in(x, g, b, tm, after):
    t, d = x.shape

    def body(x_ref, g_ref, b_ref, after_ref, h_ref, hb_ref):
        xhat, _ = _ln_stats(x_ref[...])
        h = xhat * g_ref[...] + b_ref[...]
        h_ref[...] = h
        hb_ref[...] = _mx(h)

    return _pc(body, "ln_in", (t // tm,), [_row(tm, d), _const((1, d)), _const((1, d)), ANY],
               [_row(tm, d), _row(tm, d)], [SDS((t, d), F32), SDS((t, d), MXU_DTYPE)],
               sem=("parallel",))(x, g, b, after)


def _pool_fwd(proj, pool_w, tm):
    t = proj.shape[0]
    ublk = K_U // POOL_WIDTH

    def body(u_ref, halo_ref, pw_ref, ypre_ref, d_ref, ext_ref):
        i = pl.program_id(0)
        ext_ref[0:16, :] = jnp.where(i > 0, halo_ref[...], 0.0)
        ext_ref[16:16 + tm, :] = u_ref[...]
        tok = i * tm + lax.broadcasted_iota(jnp.int32, (tm, POOL_GROUP), 0)
        for gi, w in enumerate(POOL_WINDOWS):
            cs = pl.ds(gi * POOL_GROUP, POOL_GROUP)
            ug = ext_ref[pl.ds(16, tm), cs]
            s = ug
            for k in range(1, w):
                s = s + ext_ref[pl.ds(16 - k, tm), cs]
            cnt = jnp.minimum(tok + 1, w).astype(F32)
            db = _mx(s / cnt - ug)
            d_ref[:, gi * POOL_GROUP:(gi + 1) * POOL_GROUP] = db
            ypre_ref[:, gi * POOL_OUT_GROUP:(gi + 1) * POOL_OUT_GROUP] = _dot(db, pw_ref[gi])

    halo = pl.BlockSpec((16, POOL_WIDTH), lambda i: (jnp.maximum(i * (tm // 16) - 1, 0), ublk))
    return _pc(body, "pool_fwd", (t // tm,),
               [pl.BlockSpec((tm, POOL_WIDTH), lambda i: (i, ublk)), halo, _const((4, POOL_GROUP, POOL_OUT_GROUP))],
               [_row(tm, D_MODEL), _row(tm, POOL_WIDTH)],
               [SDS((t, D_MODEL), F32), SDS((t, POOL_WIDTH), MXU_DTYPE)],
               scratch=[pltpu.VMEM((16 + tm, POOL_WIDTH), F32)], sem=("parallel",))(proj, proj, pool_w)


def _pool_bwd(dyp, d_bf, pool_w, dproj, tm):
    t = dyp.shape[0]
    n = t // tm

    def body(dy_ref, dyn_ref, d_ref, pw_ref, dproj_ref, du_ref, dpw_ref, ext_ref):
        i = pl.program_id(0)

        @pl.when(i == 0)
        def _():
            dpw_ref[...] = jnp.zeros_like(dpw_ref)

        tok = i * tm + lax.broadcasted_iota(jnp.int32, (tm + 16, POOL_GROUP), 0)
        for gi, w in enumerate(POOL_WINDOWS):
            dy = dy_ref[:, gi * POOL_OUT_GROUP:(gi + 1) * POOL_OUT_GROUP]
            dyn = dyn_ref[:, gi * POOL_OUT_GROUP:(gi + 1) * POOL_OUT_GROUP]
            pw = pw_ref[gi]
            dd = _dot_nt(dy, pw)
            ddn = jnp.where(i < n - 1, _dot_nt(dyn, pw), 0.0)
            cnt = jnp.minimum(tok + 1, w).astype(F32)
            ext_ref[0:tm, :] = dd / cnt[0:tm]
            ext_ref[tm:tm + 16, :] = ddn / cnt[tm:tm + 16]
            s = ext_ref[pl.ds(0, tm), :]
            for k in range(1, w):
                s = s + ext_ref[pl.ds(k, tm), :]
            du_ref[:, gi * POOL_GROUP:(gi + 1) * POOL_GROUP] = _mx(s - dd)
            dpw_ref[gi] += _dot_tn(d_ref[:, gi * POOL_GROUP:(gi + 1) * POOL_GROUP], dy)

    nxt = pl.BlockSpec((16, D_MODEL), lambda i: (jnp.minimum((i + 1) * (tm // 16), t // 16 - 1), 0))
    return _pc(body, "pool_bwd", (n,),
               [_row(tm, D_MODEL), nxt, _row(tm, POOL_WIDTH), _const((4, POOL_GROUP, POOL_OUT_GROUP)), ANY],
               [pl.BlockSpec((tm, POOL_WIDTH), lambda i: (i, K_U // POOL_WIDTH)),
                _const((4, POOL_GROUP, POOL_OUT_GROUP))],
               [SDS(dproj.shape, dproj.dtype), SDS((4, POOL_GROUP, POOL_OUT_GROUP), F32)],
               scratch=[pltpu.VMEM((tm + 16, POOL_GROUP), F32)], sem=("arbitrary",),
               aliases={4: 0})(dyp, dyp, d_bf, pool_w, dproj)


CONV_BLK = 512


CONV_ROWS = 32


def _conv_rows(ext_ref, w, r, rows):
    y = w[0] * ext_ref[pl.ds(r + 5, rows), :]
    for k in range(1, CONV_K):
        y = y + w[k] * ext_ref[pl.ds(r + 5 + k, rows), :]
    return y


def _conv_fwd(proj, conv_w, tm):
    t = proj.shape[0]

    def body(x_ref, halo_ref, w_ref, o_ref, ds_ref, ext_ref):
        i = pl.program_id(0)
        ext_ref[0:8, :] = jnp.where(i > 0, halo_ref[...], 0.0)
        ext_ref[8:8 + tm, :] = x_ref[...]
        w = [w_ref[pl.ds(k, 1), :] for k in range(CONV_K)]
        for r in range(0, tm, CONV_ROWS):
            y = _conv_rows(ext_ref, w, r, CONV_ROWS)
            s = _sigmoid(y)
            o_ref[pl.ds(r, CONV_ROWS), :] = y * s
            ds_ref[pl.ds(r, CONV_ROWS), :] = _mx(s * (1.0 + y * (1.0 - s)))

    halo = pl.BlockSpec((8, CONV_BLK), lambda i, j: (jnp.maximum(i * (tm // 8) - 1, 0), j))
    blk = pl.BlockSpec((tm, CONV_BLK), lambda i, j: (i, j))
    return _pc(body, "conv_fwd", (t // tm, QKV_WIDTH // CONV_BLK),
               [blk, halo, pl.BlockSpec((CONV_K, CONV_BLK), lambda i, j: (0, j))], [blk, blk],
               [SDS((t, QKV_WIDTH), F32), SDS((t, QKV_WIDTH), MXU_DTYPE)],
               scratch=[pltpu.VMEM((8 + tm, CONV_BLK), F32)], sem=("parallel", "parallel"))(proj, proj, conv_w)


def _conv_bwd(dact, dsilu, proj, conv_w, dproj, tm):
    t = proj.shape[0]
    n = t // tm

    def body(da_ref, dan_ref, ds_ref, dsn_ref, x_ref, xp_ref, w_ref, dproj_ref, dx_ref, dw_ref, ext_ref, dy_ref):
        i = pl.program_id(1)

        @pl.when(i == 0)
        def _():
            dw_ref[...] = jnp.zeros_like(dw_ref)

        ext_ref[0:8, :] = jnp.where(i > 0, xp_ref[...], 0.0)
        ext_ref[8:8 + tm, :] = x_ref[...]
        w = [w_ref[pl.ds(k, 1), :] for k in range(CONV_K)]

        acc = [jnp.zeros((8, CONV_BLK), F32) for _ in range(CONV_K)]
        for r in range(0, tm, CONV_ROWS):
            dy = da_ref[pl.ds(r, CONV_ROWS), :] * ds_ref[pl.ds(r, CONV_ROWS), :].astype(F32)
            dy_ref[pl.ds(r, CONV_ROWS), :] = dy
            for k in range(CONV_K):
                prod = dy * ext_ref[pl.ds(r + 5 + k, CONV_ROWS), :]
                for q in range(0, CONV_ROWS, 8):
                    acc[k] = acc[k] + prod[q:q + 8]
        dy_ref[tm:tm + 8, :] = jnp.where(i < n - 1, dan_ref[...] * dsn_ref[0:8, :].astype(F32), 0.0)
        for k in range(CONV_K):
            dw_ref[pl.ds(k, 1), :] += jnp.sum(acc[k], axis=0, keepdims=True)
        for r in range(0, tm, CONV_ROWS):
            dx = w[0] * dy_ref[pl.ds(r + 3, CONV_ROWS), :]
            for k in range(1, CONV_K):
                dx = dx + w[k] * dy_ref[pl.ds(r + 3 - k, CONV_ROWS), :]
            dx_ref[pl.ds(r, CONV_ROWS), :] = _mx(dx)

    blk = pl.BlockSpec((tm, CONV_BLK), lambda j, i: (i, j))
    prev = pl.BlockSpec((8, CONV_BLK), lambda j, i: (jnp.maximum(i * (tm // 8) - 1, 0), j))
    nxt = pl.BlockSpec((8, CONV_BLK), lambda j, i: (jnp.minimum((i + 1) * (tm // 8), t // 8 - 1), j))
    nxt16 = pl.BlockSpec((16, CONV_BLK), lambda j, i: (jnp.minimum((i + 1) * (tm // 16), t // 16 - 1), j))
    wspec = pl.BlockSpec((CONV_K, CONV_BLK), lambda j, i: (0, j))
    return _pc(body, "conv_bwd", (QKV_WIDTH // CONV_BLK, n),
               [blk, nxt, blk, nxt16, blk, prev, wspec, ANY],
               [blk, pl.BlockSpec((8, CONV_BLK), lambda j, i: (0, j))],
               [SDS(dproj.shape, dproj.dtype), SDS((8, QKV_WIDTH), F32)],
               scratch=[pltpu.VMEM((8 + tm, CONV_BLK), F32), pltpu.VMEM((8 + tm, CONV_BLK), F32)],
               sem=("parallel", "arbitrary"), aliases={7: 0})(dact, dact, dsilu, dsilu, proj, proj, conv_w, dproj)


def _lane(shape):
    return lax.broadcasted_iota(jnp.int32, shape, 1)


def _ba_fwd(proj, al_row, dtb_row, tm):
    t = proj.shape[0]
    bablk = K_BA // 128

    def body(ba_ref, al_ref, dtb_ref, bg_ref):
        ba = ba_ref[...]
        lane = _lane(ba.shape)
        g = -jnp.exp(al_ref[...]) * _softplus(ba + dtb_ref[...])
        bg_ref[...] = jnp.where(lane < HEADS, _sigmoid(ba), jnp.where(lane < 2 * HEADS, g, 0.0))

    return _pc(body, "ba_fwd", (t // tm,),
               [pl.BlockSpec((tm, 128), lambda i: (i, bablk)), _const((1, 128)), _const((1, 128))],
               _row(tm, 128), SDS((t, 128), F32), sem=("parallel",))(proj, al_row, dtb_row)


def _ba_bwd(dbg, bg, proj, al_row, dtb_row, dproj, tm):
    t = proj.shape[0]
    bablk = K_BA // 128

    def body(dbg_ref, bg_ref, ba_ref, al_ref, dtb_ref, dproj_ref, dba_ref, acc_ref):
        i = pl.program_id(0)

        @pl.when(i == 0)
        def _():
            acc_ref[...] = jnp.zeros_like(acc_ref)

        dbg_v, bg_v, ba = dbg_ref[...], bg_ref[...], ba_ref[...]
        lane = _lane(ba.shape)
        is_g = (lane >= HEADS) & (lane < 2 * HEADS)
        dbeta_raw = dbg_v * bg_v * (1.0 - bg_v)
        da_raw = dbg_v * (-jnp.exp(al_ref[...])) * _sigmoid(ba + dtb_ref[...])
        dba_ref[:, 0:128] = _mx(jnp.where(lane < HEADS, dbeta_raw, jnp.where(is_g, da_raw, 0.0)))
        dba_ref[:, 128:CAT_WIDTH - K_BA] = jnp.zeros((tm, CAT_WIDTH - K_BA - 128), dba_ref.dtype)
        acc_ref[0:1, :] += jnp.sum(jnp.where(is_g, dbg_v * bg_v, 0.0), axis=0, keepdims=True)
        acc_ref[1:2, :] += jnp.sum(jnp.where(is_g, da_raw, 0.0), axis=0, keepdims=True)

    tail = CAT_WIDTH - K_BA
    return _pc(body, "ba_bwd", (t // tm,),
               [_row(tm, 128), _row(tm, 128), pl.BlockSpec((tm, 128), lambda i: (i, bablk)),
                _const((1, 128)), _const((1, 128)), ANY],
               [pl.BlockSpec((tm, tail), lambda i: (i, K_BA // tail)), _const((8, 128))],
               [SDS(dproj.shape, dproj.dtype), SDS((8, 128), F32)],
               sem=("arbitrary",), aliases={5: 0})(dbg, bg, proj, al_row, dtb_row, dproj)


def _each(f, *lists):
    return [f(*a) for a in zip(*lists)]


def _rowsum(a):
    return jnp.sum(a, axis=1, keepdims=True)


def _chunk_terms(qs, ks, bgv, g_rows, hs):
    c = CHUNK
    ii = lax.broadcasted_iota(jnp.int32, (c, c), 0)
    jj = lax.broadcasted_iota(jnp.int32, (c, c), 1)
    lane = _lane(bgv.shape)
    incl = ii >= jj
    beta = [_rowsum(jnp.where(lane == h, bgv, 0.0)) for h in hs]
    g_col = [_rowsum(jnp.where(lane == HEADS + h, bgv, 0.0)) for h in hs]
    rq = _each(lambda q: lax.rsqrt(_rowsum(q * q) + L2_EPS), qs)
    rk = _each(lambda k: lax.rsqrt(_rowsum(k * k) + L2_EPS), ks)
    yq = _each(jnp.multiply, qs, rq)
    kn = _each(jnp.multiply, ks, rk)
    qn = _each(lambda a: a * Q_SCALE, yq)
    gc_col = _each(lambda g: _rowsum(jnp.where(jj <= ii, g, 0.0)), g_rows)
    gc_row = _each(lambda g: jnp.sum(jnp.where(ii <= jj, g, 0.0), axis=0, keepdims=True), g_col)
    dm = _each(lambda a, b: jnp.where(incl, jnp.exp(jnp.where(incl, a - b, 0.0)), 0.0), gc_col, gc_row)
    gl = _each(_rowsum, g_rows)
    eg = _each(jnp.exp, gc_col)
    ek = _each(lambda a, b: jnp.exp(a - b), gl, gc_col)
    egl = _each(jnp.exp, gl)
    kb = _each(jnp.multiply, kn, beta)
    kk = _each(_dot_nt, kb, kn)
    qk = _each(_dot_nt, qn, kn)
    m = _each(lambda a, b: jnp.where(ii > jj, a * b, 0.0), kk, dm)
    attn = _each(jnp.multiply, qk, dm)
    return dict(ii=ii, jj=jj, beta=beta, rq=rq, rk=rk, yq=yq, kn=kn, qn=qn, dm=dm, eg=eg, ek=ek,
                egl=egl, kb=kb, m=m, attn=attn)


def _unit_lower_inverse_minus_identity(ms, ii, jj):
    pair = (ii >> 1) == (jj >> 1)
    ys = _each(lambda m: -jnp.where(pair, m, 0.0), ms)
    s = 1
    while (1 << s) < CHUNK:
        mask = ((ii >> (s + 1)) == (jj >> (s + 1))) & ((ii >> s) != (jj >> s))
        lbs = _each(lambda m: jnp.where(mask, m, 0.0), ms)
        zs = _each(lambda y, lb: lb + _dot(y, lb), ys, lbs)
        ys = _each(lambda y, z: y - z - _dot(z, y), ys, zs)
        s += 1
    return ys


def _dn_fwd(qkv_act, bg, bgt):
    t = qkv_act.shape[0]
    nt = t // CHUNK
    c = CHUNK
    hs = list(range(HEADS))
    qo = [slice(h * HEAD_DIM, (h + 1) * HEAD_DIM) for h in hs]
    ko = [slice(DN_WIDTH + h * HEAD_DIM, DN_WIDTH + (h + 1) * HEAD_DIM) for h in hs]
    vo = [slice(2 * DN_WIDTH + h * HEAD_DIM, 2 * DN_WIDTH + (h + 1) * HEAD_DIM) for h in hs]

    def body(qkv_ref, bg_ref, bgt_ref, o_ref, u_ref, w_ref, qg_ref, kg_ref, attn_ref, y_ref, vn_ref, st_ref, egl_ref,
             s_ref):
        @pl.when(pl.program_id(0) == 0)
        def _():
            s_ref[...] = jnp.zeros_like(s_ref)

        bgv = bg_ref[...]
        qs = [qkv_ref[:, o] for o in qo]
        ks = [qkv_ref[:, o] for o in ko]
        vs = [qkv_ref[:, o] for o in vo]
        g_rows = [bgt_ref[pl.ds(HEADS + h, 1), :] for h in hs]
        ct = _chunk_terms(qs, ks, bgv, g_rows, hs)
        ys = _unit_lower_inverse_minus_identity(ct["m"], ct["ii"], ct["jj"])
        vb = _each(jnp.multiply, vs, ct["beta"])
        kbe = _each(jnp.multiply, ct["kb"], ct["eg"])
        us = _each(lambda a, y: a + _dot(y, a), vb, ys)
        ws = _each(lambda a, y: _mx(a + _dot(y, a)), kbe, ys)
        qg = _each(lambda a, b: _mx(a * b), ct["qn"], ct["eg"])
        kg = _each(lambda a, b: _mx(a * b), ct["kn"], ct["ek"])
        attn = _each(_mx, ct["attn"])
        ss = [s_ref[h] for h in hs]
        sb = _each(_mx, ss)
        vn = _each(lambda a, b, s_: a - _dot(b, s_), us, ws, sb)
        vnb = _each(_mx, vn)
        oa = _each(_dot, qg, sb)
        ob = _each(_dot, attn, vnb)
        upd = _each(_dot_tn, kg, vnb)
        for h, sl in enumerate(qo):
            u_ref[:, sl] = us[h]
            w_ref[:, sl] = ws[h]
            qg_ref[:, sl] = qg[h]
            kg_ref[:, sl] = kg[h]
            attn_ref[:, sl] = attn[h]
            y_ref[:, sl] = _mx(ys[h])
            egl_ref[0, h:h + 1, :] = jnp.broadcast_to(ct["egl"][h], (1, HEAD_DIM))
            st_ref[0, h] = ss[h]
            vn_ref[:, sl] = vnb[h]
            o_ref[:, sl] = oa[h] + ob[h]
            s_ref[h] = ss[h] * ct["egl"][h] + upd[h]

    wide = _row(c, DN_WIDTH)
    return _pc(body, "dn_fwd", (nt,),
               [_row(c, QKV_WIDTH), _row(c, 128), pl.BlockSpec((2 * HEADS, c), lambda i: (0, i))],
               [wide] * 8 + [pl.BlockSpec((1, HEADS, HEAD_DIM, HEAD_DIM), lambda i: (i, 0, 0, 0)),
                             pl.BlockSpec((1, HEADS, HEAD_DIM), lambda i: (i, 0, 0))],
               [SDS((t, DN_WIDTH), F32), SDS((t, DN_WIDTH), F32)] + [SDS((t, DN_WIDTH), MXU_DTYPE)] * 6
               + [SDS((nt, HEADS, HEAD_DIM, HEAD_DIM), F32), SDS((nt, HEADS, HEAD_DIM), F32)],
               scratch=[pltpu.VMEM((HEADS, HEAD_DIM, HEAD_DIM), F32)], sem=("arbitrary",))(qkv_act, bg, bgt)


def _dn_bwd(do, qkv_act, bg, bgt, u, w, qg, kg, attn, ymat, vn, states, egl):
    t = do.shape[0]
    nt = t // CHUNK
    c = CHUNK
    hs = list(range(HEADS))
    qo = [slice(h * HEAD_DIM, (h + 1) * HEAD_DIM) for h in hs]
    ko = [slice(DN_WIDTH + h * HEAD_DIM, DN_WIDTH + (h + 1) * HEAD_DIM) for h in hs]
    vo = [slice(2 * DN_WIDTH + h * HEAD_DIM, 2 * DN_WIDTH + (h + 1) * HEAD_DIM) for h in hs]

    def body(do_ref, qkv_ref, bg_ref, bgt_ref, u_ref, w_ref, qg_ref, kg_ref, attn_ref, y_ref, vn_ref, st_ref, egl_ref,
             dqkv_ref, dbg_ref, ds_ref):
        @pl.when(pl.program_id(0) == 0)
        def _():
            ds_ref[...] = jnp.zeros_like(ds_ref)

        dsp = [ds_ref[h] for h in hs]
        dsb = _each(_mx, dsp)
        ss = [st_ref[0, h] for h in hs]
        sb = _each(_mx, ss)
        du = [_dot(kg_ref[:, sl], b) + _dot_tn(attn_ref[:, sl], do_ref[:, sl]) for sl, b in zip(qo, dsb)]
        dub = _each(_mx, du)
        dkg_v = [_dot_nt(vn_ref[:, sl], b) for sl, b in zip(qo, dsb)]
        dqg_v = [_dot_nt(do_ref[:, sl], b) for sl, b in zip(qo, sb)]
        dattn_v = [_dot_nt(do_ref[:, sl], vn_ref[:, sl]) for sl in qo]
        dwv = [-_dot_nt(a, b) for a, b in zip(dub, sb)]
        upd = [_dot_tn(qg_ref[:, sl], do_ref[:, sl]) - _dot_tn(w_ref[:, sl], a) for sl, a in zip(qo, dub)]
        degl_v = [jnp.sum(_rowsum(a * b), axis=0, keepdims=True) for a, b in zip(ss, dsp)]
        for h in hs:
            ds_ref[h] = dsp[h] * egl_ref[0, h:h + 1, :] + upd[h]

        bgv = bg_ref[...]
        lane = _lane(bgv.shape)
        rowi = lax.broadcasted_iota(jnp.int32, (c, 1), 0)
        qs = [qkv_ref[:, o] for o in qo]
        ks = [qkv_ref[:, o] for o in ko]
        vs = [qkv_ref[:, o] for o in vo]
        g_rows = [bgt_ref[pl.ds(HEADS + h, 1), :] for h in hs]
        ct = _chunk_terms(qs, ks, bgv, g_rows, hs)
        ii, jj = ct["ii"], ct["jj"]
        beta, eg, ek, kb, kn, qn, dm = ct["beta"], ct["eg"], ct["ek"], ct["kb"], ct["kn"], ct["qn"], ct["dm"]
        ys = [y_ref[:, o] for o in qo]
        dvb = _each(lambda a, y: a + _dot_tn(y, a), du, ys)
        dkbe = _each(lambda a, y: a + _dot_tn(y, a), dwv, ys)
        dm_u = [_dot_nt(a, u_ref[:, o]) for a, o in zip(dvb, qo)]
        dm_w = [_dot_nt(a, w_ref[:, o]) for a, o in zip(dkbe, qo)]
        dms = _each(lambda a, b: jnp.where(ii > jj, -(a + b), 0.0), dm_u, dm_w)
        dkk = _each(jnp.multiply, dms, dm)
        dqk = _each(jnp.multiply, dattn_v, dm)
        gmat = _each(lambda a, b, c_, d: a * b + c_ * d, dms, ct["m"], dattn_v, ct["attn"])
        dkb = _each(lambda a, b, c_, d: _dot(a, b) + c_ * d, dkk, kn, dkbe, eg)
        dk1 = _each(_dot_tn, dkk, kb)
        dk2 = _each(_dot_tn, dqk, qn)
        dq1 = _each(_dot, dqk, kn)
        dk = _each(lambda a, b, c_, d: a + b + c_ * d, dk1, dk2, dkg_v, ek)
        dq = _each(lambda a, b, c_: a + b * c_, dq1, dqg_v, eg)
        deg = _each(lambda a, b, c_, d: _rowsum(a * b) + _rowsum(c_ * d), dqg_v, qn, dkbe, kb)
        dek = _each(lambda a, b: _rowsum(a * b), dkg_v, kn)
        dgl = _each(lambda a, b, c_, d: jnp.sum(a * b, axis=0, keepdims=True) + c_ * d, dek, ek, degl_v, ct["egl"])
        cs_row = _each(lambda g: jnp.sum(g, axis=0, keepdims=True), gmat)
        cs_col = _each(lambda r: _rowsum(jnp.where(ii == jj, r, 0.0)), cs_row)
        dgc = _each(lambda a, b, c_, d, g, e, f: a * b - c_ * d + _rowsum(g) - e + jnp.where(rowi == c - 1, f, 0.0),
                    deg, eg, dek, ek, gmat, cs_col, dgl)
        dgc_row = _each(lambda a: jnp.sum(jnp.where(ii == jj, a, 0.0), axis=0, keepdims=True), dgc)
        dg = _each(lambda r: _rowsum(jnp.where(jj >= ii, r, 0.0)), dgc_row)
        dbeta = _each(lambda a, b, c_, d: _rowsum(a * b) + _rowsum(c_ * d), dkb, kn, dvb, vs)
        dk = _each(lambda a, b, c_: a + b * c_, dk, dkb, beta)
        dbg = jnp.zeros((c, 128), F32)
        for h in hs:
            dyq = dq[h] * Q_SCALE
            yq = ct["yq"][h]
            dqkv_ref[:, qo[h]] = ct["rq"][h] * (dyq - yq * _rowsum(yq * dyq))
            dqkv_ref[:, ko[h]] = ct["rk"][h] * (dk[h] - kn[h] * _rowsum(kn[h] * dk[h]))
            dqkv_ref[:, vo[h]] = dvb[h] * beta[h]
            dbg = dbg + jnp.where(lane == h, dbeta[h], 0.0) + jnp.where(lane == HEADS + h, dg[h], 0.0)
        dbg_ref[...] = dbg

    rev = pl.BlockSpec((c, DN_WIDTH), lambda i: (nt - 1 - i, 0))
    return _pc(body, "dn_bwd", (nt,),
               [rev, pl.BlockSpec((c, QKV_WIDTH), lambda i: (nt - 1 - i, 0)),
                pl.BlockSpec((c, 128), lambda i: (nt - 1 - i, 0)), pl.BlockSpec((2 * HEADS, c), lambda i: (0, nt - 1 - i))]
               + [rev] * 7
               + [pl.BlockSpec((1, HEADS, HEAD_DIM, HEAD_DIM), lambda i: (nt - 1 - i, 0, 0, 0)),
                  pl.BlockSpec((1, HEADS, HEAD_DIM), lambda i: (nt - 1 - i, 0, 0))],
               [pl.BlockSpec((c, QKV_WIDTH), lambda i: (nt - 1 - i, 0)), pl.BlockSpec((c, 128), lambda i: (nt - 1 - i, 0))],
               [SDS((t, QKV_WIDTH), F32), SDS((t, 128), F32)],
               scratch=[pltpu.VMEM((HEADS, HEAD_DIM, HEAD_DIM), F32)],
               sem=("arbitrary",))(do, qkv_act, bg, bgt, u, w, qg, kg, attn, ymat, vn, states, egl)


MIX_ROWS = 64


def _mix_fwd(o, proj, ypre, pool_scale, wo_row, tm):
    t = o.shape[0]

    def body(o_ref, z_ref, ga_ref, gb_ref, yp_ref, ps_ref, wo_ref, mixed_ref):
        for r in range(0, tm, MIX_ROWS):
            rows = pl.ds(r, MIX_ROWS)
            for h in range(HEADS):
                sl = slice(h * HEAD_DIM, (h + 1) * HEAD_DIM)
                oh = o_ref[rows, sl]
                on = oh * lax.rsqrt(jnp.mean(oh * oh, axis=1, keepdims=True) + RMS_EPS)
                zh = z_ref[rows, sl]
                yb = on * wo_ref[:, sl] * (zh * _sigmoid(zh))
                ya = yp_ref[rows, sl] * ps_ref[:, sl]
                mixed_ref[rows, sl] = _mx(_sigmoid(ga_ref[rows, sl]) * ya + _sigmoid(gb_ref[rows, sl]) * yb)

    def col(blk):
        return pl.BlockSpec((tm, D_MODEL), lambda i: (i, blk))

    return _pc(body, "mix_fwd", (t // tm,),
               [_row(tm, D_MODEL), col(K_Z // D_MODEL), col(K_GA // D_MODEL), col(K_GB // D_MODEL), _row(tm, D_MODEL),
                _const((1, D_MODEL)), _const((1, D_MODEL))],
               _row(tm, D_MODEL), SDS((t, D_MODEL), MXU_DTYPE), sem=("parallel",))(
                   o, proj, proj, proj, ypre, pool_scale, wo_row)


def _mix_bwd(da1_bf, w_out, o, proj, ypre, pool_scale, wo_row, tm, after):
    t = o.shape[0]

    def body(da_ref, wout_ref, o_ref, z_ref, ga_ref, gb_ref, yp_ref, ps_ref, wo_ref, after_ref,
             do_ref, dp_ref, dyp_ref, acc_ref, dm_ref):
        i = pl.program_id(0)

        @pl.when(i == 0)
        def _():
            acc_ref[...] = jnp.zeros_like(acc_ref)

        dm_ref[...] = _dot_nt(da_ref[...], wout_ref[...])
        dwo = jnp.zeros((1, HEAD_DIM), F32)
        for h in range(HEADS):
            sl = slice(h * HEAD_DIM, (h + 1) * HEAD_DIM)
            woh = wo_ref[:, sl]
            psh = ps_ref[:, sl]
            dps = jnp.zeros((1, HEAD_DIM), F32)
            for r in range(0, tm, MIX_ROWS):
                rows = pl.ds(r, MIX_ROWS)
                oh = o_ref[rows, sl]
                rs = lax.rsqrt(jnp.mean(oh * oh, axis=1, keepdims=True) + RMS_EPS)
                on = oh * rs
                zh = z_ref[rows, sl]
                sz = _sigmoid(zh)
                silu = zh * sz
                t1 = on * woh
                yb = t1 * silu
                sa = _sigmoid(ga_ref[rows, sl])
                sb = _sigmoid(gb_ref[rows, sl])
                yp = yp_ref[rows, sl]
                dm = dm_ref[rows, sl]
                ga_sl = slice(D_MODEL + h * HEAD_DIM, D_MODEL + (h + 1) * HEAD_DIM)
                gb_sl = slice(2 * D_MODEL + h * HEAD_DIM, 2 * D_MODEL + (h + 1) * HEAD_DIM)
                dp_ref[rows, ga_sl] = _mx(dm * (yp * psh) * sa * (1.0 - sa))
                dp_ref[rows, gb_sl] = _mx(dm * yb * sb * (1.0 - sb))
                dya = dm * sa
                dyb = dm * sb
                dyp_ref[rows, sl] = _mx(dya * psh)
                dps = dps + jnp.sum(dya * yp, axis=0, keepdims=True)
                dp_ref[rows, sl] = _mx(dyb * t1 * (sz * (1.0 + zh * (1.0 - sz))))
                dt1 = dyb * silu
                dwo = dwo + jnp.sum(dt1 * on, axis=0, keepdims=True)
                don = dt1 * woh
                do_ref[rows, sl] = _mx(rs * (don - on * jnp.mean(don * on, axis=1, keepdims=True)))
            acc_ref[0:1, sl] += dps
        acc_ref[1:2, 0:HEAD_DIM] += dwo

    def col(blk):
        return pl.BlockSpec((tm, D_MODEL), lambda i: (i, blk))

    r = _row(tm, D_MODEL)
    return _pc(body, "mix_bwd", (t // tm,),
               [r, _const((D_MODEL, D_MODEL)), r, col(K_Z // D_MODEL), col(K_GA // D_MODEL), col(K_GB // D_MODEL), r,
                _const((1, D_MODEL)), _const((1, D_MODEL)), ANY],
               [r, pl.BlockSpec((tm, 3 * D_MODEL), lambda i: (i, K_Z // (3 * D_MODEL))), r, _const((8, D_MODEL))],
               [SDS((t, D_MODEL), MXU_DTYPE), SDS((t, CAT_WIDTH), MXU_DTYPE), SDS((t, D_MODEL), MXU_DTYPE),
                SDS((8, D_MODEL), F32)],
               scratch=[pltpu.VMEM((tm, D_MODEL), F32)],
               sem=("arbitrary",))(da1_bf, w_out, o, proj, proj, proj, ypre, pool_scale, wo_row, after)


def _oproj_ln1(mixed, w_out, h0, g1, b1, tm):
    t = mixed.shape[0]

    def body(m_ref, w_ref, h0_ref, g_ref, b_ref, a1_ref, h1_ref, h1b_ref):
        a1 = ALPHA * h0_ref[...] + _dot(m_ref[...], w_ref[...])
        a1_ref[...] = a1
        xhat, _ = _ln_stats(a1)
        h1 = xhat * g_ref[...] + b_ref[...]
        h1_ref[...] = h1
        h1b_ref[...] = _mx(h1)

    r = _row(tm, D_MODEL)
    v = _const((1, D_MODEL))
    return _pc(body, "oproj_ln1", (t // tm,), [r, _const((D_MODEL, D_MODEL)), r, v, v], [r, r, r],
               [SDS((t, D_MODEL), F32), SDS((t, D_MODEL), F32), SDS((t, D_MODEL), MXU_DTYPE)],
               sem=("parallel",))(mixed, w_out, h0, g1, b1)


def _mlp_up(h1_bf, w_up, tm):
    t = h1_bf.shape[0]
    tn = w_up.shape[2]

    def body(h_ref, w_ref, act_ref):
        r = jnp.maximum(_dot(h_ref[...], w_ref[...]), 0.0)
        act_ref[...] = _mx(r * r)

    return _pc(body, "mlp_up", (D_FF // tn, t // tm),
               [pl.BlockSpec((tm, D_MODEL), lambda j, i: (i, 0)),
                pl.BlockSpec((None, D_MODEL, tn), lambda j, i: (j, 0, 0))],
               pl.BlockSpec((tm, tn), lambda j, i: (i, j)), SDS((t, D_FF), MXU_DTYPE),
               sem=("parallel", "parallel"))(h1_bf, w_up)


def _tail(act, w_down, h1, w_gate, p_bf, w_proj, tgt, g2, b2, tm):
    t = act.shape[0]

    def body(act_ref, wd_ref, h1_ref, wg_ref, p_ref, wp_ref, tgt_ref, g_ref, b_ref,
             dr_ref, drb_ref, dgp_ref, dpp_ref, rb_ref, acc_ref):
        i = pl.program_id(0)

        @pl.when(i == 0)
        def _():
            acc_ref[...] = jnp.zeros_like(acc_ref)

        r = ALPHA * h1_ref[...] + _dot(act_ref[...], wd_ref[...])
        rb = _mx(r)
        rb_ref[...] = rb
        gate = _sigmoid(_dot(rb, wg_ref[...]))
        pp = _dot(p_ref[...], wp_ref[...])
        xhat, rstd = _ln_stats(r + gate * pp)
        g = g_ref[...]
        diff = xhat * g + b_ref[...] - tgt_ref[...]
        dh2 = diff * (1.0 / D_MODEL)
        rowloss = jnp.sum(diff * diff, axis=1, keepdims=True) * (0.5 / D_MODEL)
        acc_ref[0:1, :] += jnp.sum(dh2 * xhat, axis=0, keepdims=True)
        acc_ref[1:2, :] += jnp.sum(dh2, axis=0, keepdims=True)
        acc_ref[2:3, :] += jnp.broadcast_to(jnp.sum(rowloss, axis=0, keepdims=True), (1, D_MODEL))
        da2 = _ln_bwd(dh2, xhat, rstd, g)
        dpp_ref[...] = _mx(da2 * gate)
        dgp = _mx(da2 * pp * gate * (1.0 - gate))
        dgp_ref[...] = dgp
        dr = da2 + _dot_nt(dgp, wg_ref[...])
        dr_ref[...] = dr
        drb_ref[...] = _mx(dr)

    r = _row(tm, D_MODEL)
    v = _const((1, D_MODEL))
    return _pc(body, "tail", (t // tm,),
               [_row(tm, D_FF), _const((D_FF, D_MODEL)), r, _const((D_MODEL, D_MODEL)), _row(tm, PLE_DIM),
                _const((PLE_DIM, D_MODEL)), r, v, v],
               [r, r, r, r, r, _const((8, D_MODEL))],
               [SDS((t, D_MODEL), F32)] + [SDS((t, D_MODEL), MXU_DTYPE)] * 4 + [SDS((8, D_MODEL), F32)],
               sem=("arbitrary",))(act, w_down, h1, w_gate, p_bf, w_proj, tgt, g2, b2)


SQRT_GUARD = 1e-30


def _mlp_bwd1(dr_bf, w_down, act, tm, tn):
    t = act.shape[0]

    def body(dr_ref, w_ref, act_ref, dup_ref):
        dact = _dot_nt(dr_ref[...], w_ref[...])
        a = act_ref[...].astype(F32)
        dup_ref[...] = _mx(dact * (2.0 * a * lax.rsqrt(a + SQRT_GUARD)))

    o = pl.BlockSpec((tm, tn), lambda j, i: (i, j))
    return _pc(body, "mlp_bwd1", (D_FF // tn, t // tm),
               [pl.BlockSpec((tm, D_MODEL), lambda j, i: (i, 0)), pl.BlockSpec((tn, D_MODEL), lambda j, i: (j, 0)), o],
               o, SDS((t, D_FF), MXU_DTYPE), sem=("parallel", "parallel"))(dr_bf, w_down, act)


def _mlp_bwd2(dup, w_up, dr, a1, g1, tm):
    t = dr.shape[0]

    nk, tk = w_up.shape[0], w_up.shape[2]

    def body(dup_ref, w_ref, dr_ref, a1_ref, g_ref, da1_ref, da1b_ref, acc_ref):
        i = pl.program_id(0)

        @pl.when(i == 0)
        def _():
            acc_ref[...] = jnp.zeros_like(acc_ref)

        dh1 = ALPHA * dr_ref[...]
        for kk in range(nk):
            dh1 = dh1 + _dot_nt(dup_ref[:, kk * tk:(kk + 1) * tk], w_ref[kk])
        xhat, rstd = _ln_stats(a1_ref[...])
        acc_ref[0:1, :] += jnp.sum(dh1 * xhat, axis=0, keepdims=True)
        acc_ref[1:2, :] += jnp.sum(dh1, axis=0, keepdims=True)
        da1 = _ln_bwd(dh1, xhat, rstd, g_ref[...])
        da1_ref[...] = da1
        da1b_ref[...] = _mx(da1)

    r = _row(tm, D_MODEL)
    return _pc(body, "mlp_bwd2", (t // tm,),
               [_row(tm, D_FF), _const((nk, D_MODEL, tk)), r, r, _const((1, D_MODEL))],
               [r, r, _const((8, D_MODEL))],
               [SDS((t, D_MODEL), F32), SDS((t, D_MODEL), MXU_DTYPE), SDS((8, D_MODEL), F32)],
               sem=("arbitrary",))(dup, w_up, dr, a1, g1)


def _ln_in_bwd(dproj, w_cat, da1, x, g, tm, after):
    t = x.shape[0]

    def body(dp_ref, w_ref, da1_ref, x_ref, g_ref, after_ref, dx_ref, acc_ref):
        i = pl.program_id(0)

        @pl.when(i == 0)
        def _():
            acc_ref[...] = jnp.zeros_like(acc_ref)

        dh0 = _dot_nt(dp_ref[...], w_ref[...]) + ALPHA * da1_ref[...]
        xhat, rstd = _ln_stats(x_ref[...])
        acc_ref[0:1, :] += jnp.sum(dh0 * xhat, axis=0, keepdims=True)
        acc_ref[1:2, :] += jnp.sum(dh0, axis=0, keepdims=True)
        dx_ref[...] = _ln_bwd(dh0, xhat, rstd, g_ref[...])

    r = _row(tm, D_MODEL)
    return _pc(body, "ln_in_bwd", (t // tm,),
               [_row(tm, CAT_WIDTH), _const((D_MODEL, CAT_WIDTH)), r, r, _const((1, D_MODEL)), ANY],
               [r, _const((8, D_MODEL))], [SDS((t, D_MODEL), F32), SDS((8, D_MODEL), F32)],
               sem=("arbitrary",))(dproj, w_cat, da1, x, g, after)


def _local_step(x, p, tgt, wts, start_token, first_weights, late_weights, send_late_grads, send_early_grads):
    t = x.shape[0]
    tm = min(512, t)
    tms = min(256, t)
    row = lambda a: a.reshape(1, -1)
    pool_scale = row(wts["pool_scale"])
    wo_row = jnp.tile(row(wts["o_norm_w"]), (1, HEADS))
    pad8 = jnp.zeros((1, HEADS), F32)
    al_row = jnp.concatenate([pad8, row(wts["a_log"]), jnp.zeros((1, 128 - 2 * HEADS), F32)], axis=1)
    dtb_row = jnp.concatenate([pad8, row(wts["dt_bias"]), jnp.zeros((1, 128 - 2 * HEADS), F32)], axis=1)
    g_in, b_in = row(wts["ln_in_g"]), row(wts["ln_in_b"])
    g1, b1 = row(wts["ln1_g"]), row(wts["ln1_b"])
    g2, b2 = row(wts["ln2_g"]), row(wts["ln2_b"])

    h0, h0_bf = _ln_in(x, g_in, b_in, tm, start_token)
    first, first_token = first_weights(h0_bf)
    wts = {**wts, **first}
    w_cat = wts["w_cat"]
    proj = _proj(h0_bf, w_cat, tms, first_token)
    ypre, d_bf = _pool_fwd(proj, wts["pool_w"], tm)
    qkv_act, dsilu = _conv_fwd(proj, wts["conv_w"], tm)
    bg = _ba_fwd(proj, al_row, dtb_row, tm)
    bgt = bg[:, :2 * HEADS].T
    o, u, w, qg, kg, attn, ymat, vn, states, egl = _dn_fwd(qkv_act, bg, bgt)
    mixed = _mix_fwd(o, proj, ypre, pool_scale, wo_row, tm)
    wts = {**wts, **late_weights(mixed)}
    a1, h1, h1_bf = _oproj_ln1(mixed, wts["w_out"], h0, g1, b1, tm)
    act = _mlp_up(h1_bf, wts["w_up"], tm)
    p_bf = _mx(p)
    dr, dr_bf, dgp, dpp, r_bf, acc_tail = _tail(act, wts["w_down"], h1, wts["ple_gate_w"], p_bf, wts["ple_proj_w"],
                                                tgt, g2, b2, tms)
    grads = {}
    grads["ple_proj_w"] = _matmul(p_bf, dpp, "tn", "dw_ple_proj", WIRE_DTYPE, tm=256, tn=1024, tk=DW_TK)
    grads["ple_gate_w"] = _matmul(r_bf, dgp, "tn", "dw_ple_gate", WIRE_DTYPE, tm=DW_TM, tn=1024, tk=DW_TK)
    grads["w_down"] = _matmul(act, dr_bf, "tn", "dw_down", WIRE_DTYPE, tm=DW_TM, tn=1024, tk=DW_TK)
    dup = _mlp_bwd1(dr_bf, wts["w_down"], act, tm, 1024)
    grads["w_up"] = _matmul(h1_bf, dup, "tn", "dw_up", WIRE_DTYPE, tm=DW_TM, tn=1024, tk=DW_TK, stack_out=True)
    da1, da1_bf, acc_ln1 = _mlp_bwd2(dup, wts["w_up"], dr, a1, g1, tms)
    grads["w_out"] = _matmul(mixed, da1_bf, "tn", "dw_out", WIRE_DTYPE, tm=DW_TM, tn=1024, tk=DW_TK)
    sent = send_late_grads(grads)
    do, dproj, dyp, acc_mix = _mix_bwd(da1_bf, wts["w_out"], o, proj, ypre, pool_scale, wo_row, tms, sent)
    dproj, grads["pool_w"] = _pool_bwd(dyp, d_bf, wts["pool_w"], dproj, tm)
    dqkv_act, dbg = _dn_bwd(do, qkv_act, bg, bgt, u, w, qg, kg, attn, ymat, vn, states, egl)
    dproj, acc_conv = _conv_bwd(dqkv_act, dsilu, proj, wts["conv_w"], dproj, tm)
    dproj, acc_ba = _ba_bwd(dbg, bg, proj, al_row, dtb_row, dproj, tm)
    dw_cat = _matmul(h0_bf, dproj, "tn", "dw_in", F32, tm=DW_TM, tn=1152, tk=DW_TK)
    grads["w_in"] = jnp.concatenate(
        [dw_cat[:, K_U:K_U + 512], dw_cat[:, K_QKV:K_QKV + 3072], dw_cat[:, K_Z:K_Z + 1024],
         dw_cat[:, K_BA:K_BA + 16], dw_cat[:, K_GA:K_GA + 1024], dw_cat[:, K_GB:K_GB + 1024]], axis=1)
    sent = send_early_grads(grads)
    grad_x, acc_in = _ln_in_bwd(dproj, w_cat, da1, x, g_in, tms, sent)

    grads["conv_w"] = acc_conv[0:CONV_K]
    grads["ln_in_g"], grads["ln_in_b"] = acc_in[0], acc_in[1]
    grads["ln1_g"], grads["ln1_b"] = acc_ln1[0], acc_ln1[1]
    grads["ln2_g"], grads["ln2_b"] = acc_tail[0], acc_tail[1]
    grads["pool_scale"] = acc_mix[0]
    grads["o_norm_w"] = acc_mix[1, 0:HEAD_DIM]
    grads["a_log"] = acc_ba[0, HEADS:2 * HEADS]
    grads["dt_bias"] = acc_ba[1, HEADS:2 * HEADS]
    loss = acc_tail[2, 0]
    return grad_x, grads, loss


MESH = pl.DeviceIdType.MESH
ANY = pl.BlockSpec(memory_space=pl.ANY)


def _chip_of(k, x, y):
    chip = (2 * x + y + k) % N_CHIPS
    return chip // 2, chip % 2


def _place():
    x, y, c = lax.axis_index("x"), lax.axis_index("y"), lax.axis_index("c")
    return x, y, c, 2 * x + y


def _half(rows, c):
    return pl.ds(pl.multiple_of(c * (rows // 2), 16), rows // 2)


def _remote(src, dst, send_sem, recv_sem, device_id):
    return pltpu.make_async_remote_copy(src_ref=src, dst_ref=dst, send_sem=send_sem, recv_sem=recv_sem,
                                        device_id=device_id, device_id_type=MESH)


def _tile_rows(rows):
    for tr in (256, 128, 64, 32, 16):
        if rows % tr == 0:
            return tr
    raise ValueError(rows)


def _first_gather_copies(srcs, lands, send, recv, place):
    copies = []
    for a in range(len(srcs)):
        whole = a == len(srcs) - 1
        for k in range(N_CHIPS):
            if place is None:
                copies.append(None)
                continue
            x, y, c, me = place
            sems = (send.at[a * N_CHIPS + k], recv.at[a * N_CHIPS + k])
            if k == 0:
                copies.append(_remote(srcs[a], lands[a].at[me], *sems, (x, y, 1 - c)))
                continue
            tx, ty = _chip_of(k, x, y)
            if whole:
                copies.append(_remote(srcs[a], lands[a].at[me], *sems, (tx, ty, c)))
            else:
                mine = _half(srcs[a].shape[0], c)
                copies.append(_remote(srcs[a].at[mine], lands[a].at[me, mine], *sems, (tx, ty, c)))
    return copies


def _pass_halves(stacks):
    n = len(stacks)

    def body(*refs):
        outs = refs[n:2 * n]
        send, recv = refs[2 * n:]
        x, y, c, me = _place()
        copies = []
        for a in range(n):
            for k in range(1, N_CHIPS):
                landed = outs[a].at[(me + N_CHIPS - k) % N_CHIPS, _half(stacks[a].shape[1], c)]
                copies.append(_remote(landed, landed, send.at[a * N_CHIPS + k], recv.at[a * N_CHIPS + k],
                                      (x, y, 1 - c)))
        for cp in copies:
            cp.start()
        for cp in copies:
            cp.wait_send()
        for a in range(n):
            for k in range(1, N_CHIPS):
                passed = outs[a].at[(me + N_CHIPS - k) % N_CHIPS, _half(stacks[a].shape[1], 1 - c)]
                _remote(passed, passed, send.at[a * N_CHIPS + k], recv.at[a * N_CHIPS + k], (x, y, c)).wait_recv()

    sems = pltpu.SemaphoreType.DMA((n * N_CHIPS,))
    return pl.pallas_call(
        body, name="pass_halves", out_shape=[SDS(s.shape, s.dtype) for s in stacks],
        in_specs=[ANY] * n, out_specs=[ANY] * n, scratch_shapes=[sems, sems],
        input_output_aliases={a: a for a in range(n)},
    )(*stacks)


def _swap_halves(gs):
    n = len(gs)

    def body(*refs):
        ins, theirs = refs[0:n], refs[n:2 * n]
        send, recv = refs[2 * n:]
        x, y, c, _ = _place()
        copies = [_remote(ins[a].at[:, _half(gs[a].shape[1], 1 - c)], theirs[a], send.at[a], recv.at[a],
                          (x, y, 1 - c)) for a in range(n)]
        for cp in copies:
            cp.start()
        for cp in copies:
            cp.wait()

    return pl.pallas_call(
        body, name="swap_halves", out_shape=[SDS((N_CHIPS, g.shape[1] // 2, g.shape[2]), g.dtype) for g in gs],
        in_specs=[ANY] * n, out_specs=[ANY] * n, scratch_shapes=[pltpu.SemaphoreType.DMA((n,))] * 2,
    )(*gs)


def _send_to_sibling(hs):
    n = len(hs)

    def body(*refs):
        ins, outs = refs[0:n], refs[n:2 * n]
        send, recv = refs[2 * n:]
        x, y, c, _ = _place()
        copies = [_remote(ins[a], outs[a], send.at[a], recv.at[a], (x, y, 1 - c)) for a in range(n)]
        for cp in copies:
            cp.start()
        for cp in copies:
            cp.wait()

    return pl.pallas_call(
        body, name="send_to_sibling", out_shape=[SDS(h.shape, h.dtype) for h in hs],
        in_specs=[ANY] * n, out_specs=[ANY] * n, scratch_shapes=[pltpu.SemaphoreType.DMA((n,))] * 2,
    )(*hs)


HBM = pl.BlockSpec(memory_space=pltpu.HBM)
SEM = pl.BlockSpec(memory_space=pltpu.SEMAPHORE)
EFFECT = pltpu.SideEffectType.DATAFLOW_SIDE_EFFECTING


def _in_hbm(a):
    return pltpu.with_memory_space_constraint(a, pltpu.HBM)


def _split_copy_start(name, srcs, lands, copies_of, after):
    n = len(srcs)
    n_copies = len(copies_of(srcs, lands, None, None, None))

    def body(*refs):
        src_refs, land_refs = refs[0:n], refs[n:2 * n]
        send, recv = refs[2 * n + 1], refs[2 * n + 2]
        token = refs[-1]
        for cp in copies_of(src_refs, land_refs, send, recv, _place()):
            cp.start()
        token[...] = jnp.zeros_like(token)

    sems = pltpu.SemaphoreType.DMA((n_copies,))
    out = pl.pallas_call(
        body, name=name,
        out_shape=[sems, sems] + [pltpu.HBM(a.shape, a.dtype) for a in list(srcs) + list(lands)] + [SDS((8, 128), F32)],
        in_specs=[HBM] * (2 * n) + [ANY],
        out_specs=[SEM, SEM] + [HBM] * (2 * n) + [pl.BlockSpec(memory_space=pltpu.VMEM)],
        input_output_aliases={i: 2 + i for i in range(2 * n)},
        compiler_params=pltpu.CompilerParams(has_side_effects=EFFECT),
    )(*[_in_hbm(a) for a in list(srcs) + list(lands)], after)
    return out[0], out[1], out[2:2 + n], out[2 + n:2 + 2 * n], out[-1]


def _split_copy_wait(name, send, recv, srcs, lands, after, copies_of):
    n = len(srcs)

    def body(*refs):
        src_refs, land_refs = refs[0:n], refs[n:2 * n]
        send_ref, recv_ref = refs[2 * n], refs[2 * n + 1]
        for cp in copies_of(src_refs, land_refs, send_ref, recv_ref, _place()):
            cp.wait_send()
            cp.wait_recv()

    out = pl.pallas_call(
        body, name=name, out_shape=[pltpu.HBM(a.shape, a.dtype) for a in list(srcs) + list(lands)],
        in_specs=[HBM] * (2 * n) + [SEM, SEM, ANY], out_specs=[HBM] * (2 * n),
        input_output_aliases={i: i for i in range(2 * n)},
        compiler_params=pltpu.CompilerParams(has_side_effects=EFFECT),
    )(*srcs, *lands, send, recv, after)
    return out[0:n], out[n:2 * n]


def _late_gather_copies(srcs, lands, send, recv, place):
    copies = []
    for a in range(len(srcs)):
        for k in range(N_CHIPS):
            if place is None:
                copies.append(None)
                continue
            x, y, c, me = place
            if k == 0:
                target = (x, y, 1 - c)
            else:
                tx, ty = _chip_of(k, x, y)
                target = (tx, ty, c)
            copies.append(_remote(srcs[a], lands[a].at[me], send.at[a * N_CHIPS + k], recv.at[a * N_CHIPS + k], target))
    return copies


def _late_scatter_copies(srcs, lands, send, recv, place):
    copies = []
    for a in range(len(srcs)):
        for k in range(1, N_CHIPS):
            if place is None:
                copies.append(None)
                continue
            x, y, c, _ = place
            tx, ty = _chip_of(k, x, y)
            copies.append(_remote(srcs[a].at[2 * tx + ty], lands[a].at[k - 1], send.at[a * (N_CHIPS - 1) + k - 1],
                                  recv.at[a * (N_CHIPS - 1) + k - 1], (tx, ty, c)))
    return copies


def _add_pair(g, theirs, name):
    _, rows, cols = g.shape
    half = rows // 2
    tr = _tile_rows(half)

    def body(g_ref, t_ref, o_ref):
        own = g_ref[lax.axis_index("c")]
        o_ref[...] = (own.astype(F32) + t_ref[...].astype(F32)).astype(o_ref.dtype)

    blk = pl.BlockSpec((None, tr, cols), lambda j, i: (j, i, 0))
    return _pc(body, "add_" + name, (N_CHIPS, half // tr),
               [pl.BlockSpec((None, 2, tr, cols), lambda j, i: (j, 0, i, 0)), blk], blk,
               SDS((N_CHIPS, half, cols), g.dtype), sem=("parallel", "parallel"))(
                   g.reshape(N_CHIPS, 2, half, cols), theirs)


def _sum_slabs(pair, landed, name):
    _, rows, cols = pair.shape
    tr = _tile_rows(rows)

    def body(p_ref, r_ref, o_ref):
        acc = p_ref[2 * lax.axis_index("x") + lax.axis_index("y")].astype(F32)
        for k in range(N_CHIPS - 1):
            acc = acc + r_ref[k].astype(F32)
        o_ref[...] = acc

    return _pc(body, "sum_" + name, (rows // tr,),
               [pl.BlockSpec((N_CHIPS, tr, cols), lambda i: (0, i, 0)),
                pl.BlockSpec((N_CHIPS - 1, tr, cols), lambda i: (0, i, 0))],
               _row(tr, cols), SDS((rows, cols), F32), sem=("parallel",))(pair, landed)


def _adamw_math(w, g, m, v):
    m = ADAM_B1 * m + (1.0 - ADAM_B1) * g
    v = ADAM_B2 * v + (1.0 - ADAM_B2) * (g * g)
    m_hat = m / (1.0 - ADAM_B1 ** ADAM_STEP)
    v_hat = v / (1.0 - ADAM_B2 ** ADAM_STEP)
    delta = -ADAM_LR * (m_hat / (jnp.sqrt(v_hat) + ADAM_EPS) + ADAM_WD * w)
    return delta, m, v


def _adamw_2d(w, g_own, g_sib, m, v, name, halves):
    lead = w.ndim == 3
    rows, cols = w.shape[-2:]
    tr = _tile_rows(rows // 2)
    nh = rows // 2 // tr if halves else rows // tr

    def body(w_ref, go_ref, gs_ref, m_ref, v_ref, g_out, d_out, m_out, v_out):
        if halves:
            mine = (pl.program_id(0) // nh) == lax.axis_index("c")
            g = jnp.where(mine, go_ref[...], gs_ref[...])
        else:
            g = go_ref[...] + gs_ref[...]
        delta, mn, vn = _adamw_math(w_ref[...], g, m_ref[...], v_ref[...])
        g_out[...] = g
        d_out[...] = delta
        m_out[...] = mn
        v_out[...] = vn

    r = _row(tr, cols)
    p = pl.BlockSpec((None, tr, cols), lambda i: (0, i, 0)) if lead else r
    h = pl.BlockSpec((tr, cols), lambda i: (i % nh, 0))
    return _pc(body, "adamw_" + name, (rows // tr,), [p, h, h, p, p], [r] * 4, [SDS((rows, cols), F32)] * 4,
               sem=("parallel",))(w, g_own, g_sib, m, v)


def _small_allreduce_adamw(mine, w, m, v):
    shape = mine.shape

    def body(mine_ref, w_ref, m_ref, v_ref, g_out, d_out, m_out, v_out, buf_ref, send_sems, recv_sems):
        x, y, c = lax.axis_index("x"), lax.axis_index("y"), lax.axis_index("c")
        me = 4 * x + 2 * y + c
        buf_ref[me] = mine_ref[...]
        copies = []
        for k in range(1, N_DEV):
            tgt = (me + k) % N_DEV
            copies.append(pltpu.make_async_remote_copy(
                src_ref=mine_ref, dst_ref=buf_ref.at[me], send_sem=send_sems.at[k], recv_sem=recv_sems.at[k],
                device_id=(tgt // 4, (tgt // 2) % 2, tgt % 2), device_id_type=MESH))
        for cp in copies:
            cp.start()
        for k in range(1, N_DEV):
            src = (me + N_DEV - k) % N_DEV
            pltpu.make_async_remote_copy(
                src_ref=mine_ref, dst_ref=buf_ref.at[src], send_sem=send_sems.at[k], recv_sem=recv_sems.at[k],
                device_id=(x, y, c), device_id_type=MESH).wait_recv()
        for cp in copies:
            cp.wait_send()
        g = buf_ref[0]
        for j in range(1, N_DEV):
            g = g + buf_ref[j]
        delta, mn, vn = _adamw_math(w_ref[...], g, m_ref[...], v_ref[...])
        g_out[...] = g
        d_out[...] = delta
        m_out[...] = mn
        v_out[...] = vn

    vm = pl.BlockSpec(memory_space=pltpu.VMEM)
    return pl.pallas_call(
        body, name="small_allreduce_adamw", out_shape=[SDS(shape, F32)] * 4, in_specs=[vm] * 4, out_specs=[vm] * 4,
        scratch_shapes=[pltpu.VMEM((N_DEV,) + shape, F32), pltpu.SemaphoreType.DMA((N_DEV,)),
                        pltpu.SemaphoreType.DMA((N_DEV,))],
    )(mine, w, m, v)


def _as2d(a):
    return a.reshape(-1, a.shape[-1])


def _w_cat(stack):
    wi = stack.transpose(1, 0, 2).reshape(D_MODEL, IN_WIDTH)
    return jnp.concatenate(
        [wi[:, C_QKV:C_Z], wi[:, C_Z:C_BETA], wi[:, C_GA:C_GB], wi[:, C_GB:IN_WIDTH], wi[:, C_POOL:C_QKV],
         wi[:, C_BETA:C_GA], jnp.zeros((D_MODEL, CAT_WIDTH - K_BA - 2 * HEADS), wi.dtype)], axis=1)


WEIGHT_LAYOUT = {
    "w_in": lambda s: ("w_cat", _w_cat(s)),
    "pool_w": lambda s: ("pool_w", s.reshape(N_CHIPS, 4, POOL_GROUP, POOL_OUT_GROUP // N_CHIPS)
                         .transpose(1, 2, 0, 3).reshape(4, POOL_GROUP, POOL_OUT_GROUP)),
    "w_out": lambda s: ("w_out", s.reshape(D_MODEL, D_MODEL)),
    "w_up": lambda s: ("w_up", s),
    "w_down": lambda s: ("w_down", s.reshape(D_FF, D_MODEL)),
    "ple_gate_w": lambda s: ("ple_gate_w", s.reshape(D_MODEL, D_MODEL)),
    "ple_proj_w": lambda s: ("ple_proj_w", s.transpose(1, 0, 2).reshape(PLE_DIM, D_MODEL)),
}

GRAD_LAYOUT = {
    "w_in": lambda g: g.reshape(D_MODEL, N_CHIPS, IN_WIDTH // N_CHIPS).transpose(1, 0, 2),
    "pool_w": lambda g: g.reshape(4, POOL_GROUP, N_CHIPS, POOL_OUT_GROUP // N_CHIPS)
                         .transpose(2, 0, 1, 3).reshape(N_CHIPS, 4 * POOL_GROUP, POOL_OUT_GROUP // N_CHIPS),
    "w_out": lambda g: g.reshape(N_CHIPS, D_MODEL // N_CHIPS, D_MODEL),
    "w_up": lambda g: g,
    "w_down": lambda g: g.reshape(N_CHIPS, D_FF // N_CHIPS, D_MODEL),
    "ple_gate_w": lambda g: g.reshape(N_CHIPS, D_MODEL // N_CHIPS, D_MODEL),
    "ple_proj_w": lambda g: g.reshape(PLE_DIM, N_CHIPS, D_MODEL // N_CHIPS).transpose(1, 0, 2),
}


def _full_weights(names, stacks):
    return dict(WEIGHT_LAYOUT[n](s.astype(MXU_DTYPE)) for n, s in zip(names, stacks))


def _grads_by_chip(names, grads):
    return [GRAD_LAYOUT[n](grads[n]).astype(WIRE_DTYPE) for n in names]


def _pack_small(rows, conv, name):
    n = len(rows)

    def body(*refs):
        out = refs[n + 1]
        out[...] = jnp.zeros_like(out)
        for i in range(n):
            out[i:i + 1, :] = refs[i][...]
        out[SMALL_CONV_AT:SMALL_CONV_AT + SMALL_CONV_ROWS, :] = refs[n][...]

    vm = pl.BlockSpec(memory_space=pltpu.VMEM)
    return pl.pallas_call(body, name=name, out_shape=SDS((SMALL_CONV_AT + SMALL_CONV_ROWS, D_MODEL), F32),
                          in_specs=[vm] * (n + 1), out_specs=vm)(*rows, conv)


def _pad_row(a):
    a = a.reshape(1, -1).astype(F32)
    return jnp.pad(a, ((0, 0), (0, D_MODEL - a.shape[1])))


def kernel(x, p, ln_in_g, ln_in_b, w_in, pool_w, pool_scale, conv_w, a_log, dt_bias, o_norm_w, w_out, ln1_g, ln1_b, w_up, w_down, ple_gate_w, ple_proj_w, ln2_g, ln2_b, loss_target, m_ln_in_g, m_ln_in_b, m_w_in, m_pool_w, m_pool_scale, m_conv_w, m_a_log, m_dt_bias, m_o_norm_w, m_w_out, m_ln1_g, m_ln1_b, m_w_up, m_w_down, m_ple_gate_w, m_ple_proj_w, m_ln2_g, m_ln2_b, v_ln_in_g, v_ln_in_b, v_w_in, v_pool_w, v_pool_scale, v_conv_w, v_a_log, v_dt_bias, v_o_norm_w, v_w_out, v_ln1_g, v_ln1_b, v_w_up, v_w_down, v_ple_gate_w, v_ple_proj_w, v_ln2_g, v_ln2_b):
    given = dict(locals())
    chip = 2 * lax.axis_index("x") + lax.axis_index("y")

    shard = lambda n: _as2d(given[n]).astype(WIRE_DTYPE)

    wts = {"ln_in_g": ln_in_g, "ln_in_b": ln_in_b, "pool_scale": pool_scale[0], "a_log": a_log[0],
           "dt_bias": dt_bias[0], "o_norm_w": o_norm_w[0], "ln1_g": ln1_g[0], "ln1_b": ln1_b[0],
           "ln2_g": ln2_g[0], "ln2_b": ln2_b[0]}

    conv_pad = jnp.pad(conv_w[0], ((0, 8 - CONV_K), (0, 0)))
    first_srcs = [shard(n) for n in EARLY] + [conv_pad]
    first_lands = [lax.empty((N_CHIPS,) + s.shape, s.dtype) for s in first_srcs]
    fsend, frecv, fsrcs, flands, start_token = _split_copy_start(
        "first_gather_start", first_srcs, first_lands, _first_gather_copies, first_srcs[0])
    late = {}

    def first_weights(after):
        _, lands = _split_copy_wait("first_gather_wait", fsend, frecv, fsrcs, flands, after, _first_gather_copies)
        stacks = _pass_halves(lands[0:len(EARLY)])
        first = _full_weights(EARLY, stacks)
        first["conv_w"] = jnp.concatenate([lands[len(EARLY)][j, 0:CONV_K] for j in range(N_CHIPS)], axis=1)
        late_srcs = [shard(n) for n in LATE]
        late_lands = [lax.empty((N_CHIPS,) + s.shape, s.dtype) for s in late_srcs]
        late["send"], late["recv"], late["srcs"], late["lands"], token = _split_copy_start(
            "late_gather_start", late_srcs, late_lands, _late_gather_copies, stacks[0])
        return first, token

    def late_weights(after):
        _, stacks = _split_copy_wait("late_gather_wait", late["send"], late["recv"], late["srcs"], late["lands"],
                                     after, _late_gather_copies)
        return _full_weights(LATE, stacks)

    scatter = {}

    def send_late_grads(grads):
        srcs = _grads_by_chip(LATE, grads)
        lands = [lax.empty((N_CHIPS - 1,) + g.shape[1:], g.dtype) for g in srcs]
        scatter["send"], scatter["recv"], scatter["srcs"], scatter["lands"], token = _split_copy_start(
            "late_scatter_start", srcs, lands, _late_scatter_copies, srcs[0])
        return token

    last = {}

    def send_early_grads(grads):
        by_chip = _grads_by_chip(EARLY, grads)
        theirs = _swap_halves(by_chip)
        pair = [_add_pair(g, t, n) for g, t, n in zip(by_chip, theirs, EARLY)]
        lands = [lax.empty((N_CHIPS - 1,) + q.shape[1:], q.dtype) for q in pair]
        last["send"], last["recv"], last["srcs"], last["lands"], token = _split_copy_start(
            "early_scatter_start", pair, lands, _late_scatter_copies, pair[0])
        return token

    grad_x, grads, loss = _local_step(x[0], p[0, 0], loss_target[0], wts, start_token, first_weights, late_weights,
                                      send_late_grads, send_early_grads)

    late_mine, late_landed = _split_copy_wait("late_scatter_wait", scatter["send"], scatter["recv"], scatter["srcs"],
                                              scatter["lands"], grad_x, _late_scatter_copies)
    late_part = [_sum_slabs(q, r, n) for q, r, n in zip(late_mine, late_landed, LATE)]
    pair, landed = _split_copy_wait("early_scatter_wait", last["send"], last["recv"], last["srcs"], last["lands"],
                                    grad_x, _late_scatter_copies)
    reduced = [_sum_slabs(q, r, n) for q, r, n in zip(pair, landed, EARLY)]
    from_sibling = _send_to_sibling(reduced + late_part)
    big_out = {}
    for n, g_own, g_sib in zip(EARLY + LATE, reduced + late_part, from_sibling):
        view = (lambda a: a) if given[n].ndim == 3 else _as2d
        res = _adamw_2d(view(given[n]), g_own, g_sib, view(given["m_" + n]), view(given["v_" + n]), n,
                        halves=n in EARLY)
        big_out[n] = [r.reshape(given[n].shape) for r in res]

    conv_cols = QKV_WIDTH // N_CHIPS

    def small_pack(get, conv, extra, name):
        if conv.shape[1] != QKV_WIDTH:
            conv = lax.dynamic_update_slice(jnp.zeros((CONV_K, QKV_WIDTH), F32), conv, (0, chip * conv_cols))
        return _pack_small([_pad_row(get(n)) for n in SMALL_NAMES] + extra, conv.reshape(SMALL_CONV_ROWS, D_MODEL), name)

    mine_small = small_pack(lambda n: grads[n], grads["conv_w"], [jnp.full((1, D_MODEL), loss, F32)], "pack_small_g")
    packed_small = [small_pack(lambda n: given[prefix + n], given[prefix + "conv_w"][0], [], "pack_small_" + tag)
                    for prefix, tag in (("", "w"), ("m_", "m"), ("v_", "v"))]
    small_out = _small_allreduce_adamw(mine_small, *packed_small)

    def small_get(k, n):
        if n == "conv_w":
            full = small_out[k][SMALL_CONV_AT:SMALL_CONV_AT + SMALL_CONV_ROWS].reshape(CONV_K, QKV_WIDTH)
            return lax.dynamic_slice(full, (0, chip * conv_cols), (CONV_K, conv_cols)).reshape(given[n].shape)
        i = SMALL_NAMES.index(n)
        return small_out[k][i, 0:given[n].size].reshape(given[n].shape)

    order = ["ln_in_g", "ln_in_b", "w_in", "pool_w", "pool_scale", "conv_w", "a_log", "dt_bias", "o_norm_w", "w_out",
             "ln1_g", "ln1_b", "w_up", "w_down", "ple_gate_w", "ple_proj_w", "ln2_g", "ln2_b"]
    outs = [small_out[0][len(SMALL_NAMES), 0], grad_x[None]]
    for k in range(4):
        for n in order:
            outs.append(big_out[n][k] if n in big_out else small_get(k, n))
    return tuple(outs)
```

```python
import jax
import jax.numpy as jnp
from jax import lax
from jax.experimental import pallas as pl
from jax.experimental.pallas import tpu as pltpu

F32 = jnp.float32
MXU_DTYPE = jnp.bfloat16
WIRE_DTYPE = jnp.bfloat16
SDS = jax.ShapeDtypeStruct

D_MODEL = 1024
POOL_WINDOWS = (2, 4, 8, 16)
POOL_WIDTH = 512
POOL_GROUP = 128
POOL_OUT_GROUP = 256
HEADS = 8
HEAD_DIM = 128
DN_WIDTH = HEADS * HEAD_DIM
QKV_WIDTH = 3 * DN_WIDTH
CONV_K = 4
CHUNK = 128
DW_TK = 1024
DW_TM = 1024
D_FF = 4096
PLE_DIM = 256
LN_EPS = 1e-5
RMS_EPS = 1e-6
L2_EPS = 1e-6
ALPHA = 2.0 ** 0.25
Q_SCALE = HEAD_DIM ** -0.5
IN_WIDTH = 6672
C_POOL, C_QKV, C_Z, C_BETA, C_A, C_GA, C_GB = 0, 512, 3584, 4608, 4616, 4624, 5648
K_QKV, K_Z, K_GA, K_GB, K_U, K_BA, CAT_WIDTH = 0, 3072, 4096, 5120, 6144, 6656, 6912

ADAM_LR, ADAM_B1, ADAM_B2, ADAM_EPS, ADAM_WD, ADAM_STEP = 0.001, 0.9, 0.999, 1e-08, 0.01, 10

N_CHIPS = 4
N_DEV = 8
VMEM_LIMIT = 56 * 1024 * 1024

EARLY = ("w_in", "pool_w")
LATE = ("w_out", "w_up", "w_down", "ple_gate_w", "ple_proj_w")
SMALL_NAMES = ("ln_in_g", "ln_in_b", "pool_scale", "ln1_g", "ln1_b", "ln2_g", "ln2_b", "o_norm_w", "a_log", "dt_bias")
SMALL_CONV_AT = 12
SMALL_CONV_ROWS = CONV_K * QKV_WIDTH // D_MODEL


def _mx(a):
    return a.astype(MXU_DTYPE)


def _dot(a, b):
    return lax.dot_general(_mx(a), _mx(b), (((1,), (0,)), ((), ())), preferred_element_type=F32)


def _dot_nt(a, b):
    return lax.dot_general(_mx(a), _mx(b), (((1,), (1,)), ((), ())), preferred_element_type=F32)


def _dot_tn(a, b):
    return lax.dot_general(_mx(a), _mx(b), (((0,), (0,)), ((), ())), preferred_element_type=F32)


def _sigmoid(x):
    return 0.5 * jnp.tanh(0.5 * x) + 0.5


def _softplus(x):
    return jnp.maximum(x, 0.0) + jnp.log(1.0 + jnp.exp(-jnp.abs(x)))


def _pc(body, name, grid, in_specs, out_specs, out_shape, scratch=(), sem=None, aliases=None):
    return pl.pallas_call(
        body, out_shape=out_shape, grid=grid, in_specs=in_specs, out_specs=out_specs,
        scratch_shapes=scratch, name=name, input_output_aliases=aliases or {},
        compiler_params=pltpu.CompilerParams(dimension_semantics=sem, vmem_limit_bytes=VMEM_LIMIT))


def _row(tm, n):
    return pl.BlockSpec((tm, n), lambda i: (i, 0))


def _const(shape):
    nd = len(shape)
    return pl.BlockSpec(shape, lambda *_: (0,) * nd)


def _matmul(a, b, mode, name, out_dtype=F32, tm=512, tn=512, tk=512, stack_out=False):
    if mode == "nn":
        (m, k), n = a.shape, b.shape[1]
    elif mode == "nt":
        (m, k), n = a.shape, b.shape[0]
    else:
        (k, m), n = a.shape, b.shape[1]
    tm, tn, tk = min(tm, m), min(tn, n), min(tk, k)
    assert m % tm == 0 and n % tn == 0 and k % tk == 0, (name, m, n, k, tm, tn, tk)
    nk = k // tk
    if mode == "nn":
        a_spec = pl.BlockSpec((tm, tk), lambda i, j, kk: (i, kk))
        b_spec = pl.BlockSpec((tk, tn), lambda i, j, kk: (kk, j))
        dot = _dot
    elif mode == "nt":
        a_spec = pl.BlockSpec((tm, tk), lambda i, j, kk: (i, kk))
        b_spec = pl.BlockSpec((tn, tk), lambda i, j, kk: (j, kk))
        dot = _dot_nt
    else:
        a_spec = pl.BlockSpec((tk, tm), lambda i, j, kk: (kk, i))
        b_spec = pl.BlockSpec((tk, tn), lambda i, j, kk: (kk, j))
        dot = _dot_tn

    def body(a_ref, b_ref, o_ref, *acc):
        if nk == 1:
            o_ref[...] = dot(a_ref[...], b_ref[...]).astype(out_dtype)
            return
        acc_ref, kk = acc[0], pl.program_id(2)

        @pl.when(kk == 0)
        def _():
            acc_ref[...] = dot(a_ref[...], b_ref[...])

        @pl.when((kk > 0) & (kk < nk - 1))
        def _():
            acc_ref[...] += dot(a_ref[...], b_ref[...])

        @pl.when(kk == nk - 1)
        def _():
            o_ref[...] = (acc_ref[...] + dot(a_ref[...], b_ref[...])).astype(out_dtype)

    if stack_out:
        o_spec, o_shape = pl.BlockSpec((None, tm, tn), lambda i, j, kk: (j, i, 0)), SDS((n // tn, m, tn), out_dtype)
    else:
        o_spec, o_shape = pl.BlockSpec((tm, tn), lambda i, j, kk: (i, j)), SDS((m, n), out_dtype)
    return _pc(body, name, (m // tm, n // tn, nk), [a_spec, b_spec], o_spec, o_shape,
               scratch=[pltpu.VMEM((tm, tn), F32)] if nk > 1 else [],
               sem=("parallel", "parallel", "arbitrary"))(a, b)


PROJ_TN = 768


def _proj_conv(h0_bf, w_cat, conv_w, tm, after):
    t = h0_bf.shape[0]
    n_qkv = QKV_WIDTH // PROJ_TN

    def body(h_ref, w_ref, cw_ref, after_ref, o_ref, act_ref, ds_ref, carry_ref, ext_ref):
        @pl.when(pl.program_id(0) == 0)
        def _():
            carry_ref[...] = jnp.zeros_like(carry_ref)

        h = h_ref[...]

        def project(cb):
            cols = slice(cb * PROJ_TN, (cb + 1) * PROJ_TN)
            o_ref[:, cols] = _dot(h, w_ref[:, cols])

        def conv(cb):
            cols = slice(cb * PROJ_TN, (cb + 1) * PROJ_TN)
            ext_ref[cb, 0:8, :] = carry_ref[:, cols]
            ext_ref[cb, 8:8 + tm, :] = o_ref[:, cols]
            carry_ref[:, cols] = o_ref[tm - 8:tm, cols]
            w = [cw_ref[pl.ds(k, 1), cols] for k in range(CONV_K)]
            for r in range(0, tm, CONV_ROWS):
                y = _conv_rows(ext_ref.at[cb], w, r, CONV_ROWS)
                s = _sigmoid(y)
                act_ref[pl.ds(r, CONV_ROWS), cols] = y * s
                ds_ref[pl.ds(r, CONV_ROWS), cols] = _mx(s * (1.0 + y * (1.0 - s)))

        project(0)
        for cb in range(1, CAT_WIDTH // PROJ_TN):
            project(cb)
            if cb - 1 < n_qkv:
                conv(cb - 1)

    return _pc(body, "proj_conv", (t // tm,),
               [_row(tm, D_MODEL), _const((D_MODEL, CAT_WIDTH)), _const((CONV_K, QKV_WIDTH)), ANY],
               [_row(tm, CAT_WIDTH), _row(tm, QKV_WIDTH), _row(tm, QKV_WIDTH)],
               [SDS((t, CAT_WIDTH), F32), SDS((t, QKV_WIDTH), F32), SDS((t, QKV_WIDTH), MXU_DTYPE)],
               scratch=[pltpu.VMEM((8, QKV_WIDTH), F32), pltpu.VMEM((n_qkv, 8 + tm, PROJ_TN), F32)],
               sem=("arbitrary",))(h0_bf, w_cat, conv_w, after)


def _ln_stats(x):
    mu = jnp.mean(x, axis=-1, keepdims=True)
    xc = x - mu
    var = jnp.mean(xc * xc, axis=-1, keepdims=True)
    rstd = lax.rsqrt(var + LN_EPS)
    return xc * rstd, rstd


def _ln_bwd(dy, xhat, rstd, g):
    dxh = dy * g
    m1 = jnp.mean(dxh, axis=-1, keepdims=True)
    m2 = jnp.mean(dxh * xhat, axis=-1, keepdims=True)
    return rstd * (dxh - m1 - xhat * m2)


def _ln_in(x, g, b, tm, after):
    t, d = x.shape

    def body(x_ref, g_ref, b_ref, after_ref, h_ref, hb_ref):
        xhat, _ = _ln_stats(x_ref[...])
        h = xhat * g_ref[...] + b_ref[...]
        h_ref[...] = h
        hb_ref[...] = _mx(h)

    return _pc(body, "ln_in", (t // tm,), [_row(tm, d), _const((1, d)), _const((1, d)), ANY],
               [_row(tm, d), _row(tm, d)], [SDS((t, d), F32), SDS((t, d), MXU_DTYPE)],
               sem=("parallel",))(x, g, b, after)


def _pool_fwd(proj, pool_w, tm):
    t = proj.shape[0]
    ublk = K_U // POOL_WIDTH

    def body(u_ref, halo_ref, pw_ref, ypre_ref, d_ref, ext_ref):
        i = pl.program_id(0)
        ext_ref[0:16, :] = jnp.where(i > 0, halo_ref[...], 0.0)
        ext_ref[16:16 + tm, :] = u_ref[...]
        tok = i * tm + lax.broadcasted_iota(jnp.int32, (tm, POOL_GROUP), 0)
        for gi, w in enumerate(POOL_WINDOWS):
            cs = pl.ds(gi * POOL_GROUP, POOL_GROUP)
            ug = ext_ref[pl.ds(16, tm), cs]
            s = ug
            for k in range(1, w):
                s = s + ext_ref[pl.ds(16 - k, tm), cs]
            cnt = jnp.minimum(tok + 1, w).astype(F32)
            db = _mx(s / cnt - ug)
            d_ref[:, gi * POOL_GROUP:(gi + 1) * POOL_GROUP] = db
            ypre_ref[:, gi * POOL_OUT_GROUP:(gi + 1) * POOL_OUT_GROUP] = _dot(db, pw_ref[gi])

    halo = pl.BlockSpec((16, POOL_WIDTH), lambda i: (jnp.maximum(i * (tm // 16) - 1, 0), ublk))
    return _pc(body, "pool_fwd", (t // tm,),
               [pl.BlockSpec((tm, POOL_WIDTH), lambda i: (i, ublk)), halo, _const((4, POOL_GROUP, POOL_OUT_GROUP))],
               [_row(tm, D_MODEL), _row(tm, POOL_WIDTH)],
               [SDS((t, D_MODEL), F32), SDS((t, POOL_WIDTH), MXU_DTYPE)],
               scratch=[pltpu.VMEM((16 + tm, POOL_WIDTH), F32)], sem=("parallel",))(proj, proj, pool_w)


def _pool_bwd(dyp, d_bf, pool_w, dproj, tm):
    t = dyp.shape[0]
    n = t // tm

    def body(dy_ref, dyn_ref, d_ref, pw_ref, dproj_ref, du_ref, dpw_ref, ext_ref):
        i = pl.program_id(0)

        @pl.when(i == 0)
        def _():
            dpw_ref[...] = jnp.zeros_like(dpw_ref)

        tok = i * tm + lax.broadcasted_iota(jnp.int32, (tm + 16, POOL_GROUP), 0)
        for gi, w in enumerate(POOL_WINDOWS):
            dy = dy_ref[:, gi * POOL_OUT_GROUP:(gi + 1) * POOL_OUT_GROUP]
            dyn = dyn_ref[:, gi * POOL_OUT_GROUP:(gi + 1) * POOL_OUT_GROUP]
            pw = pw_ref[gi]
            dd = _dot_nt(dy, pw)
            ddn = jnp.where(i < n - 1, _dot_nt(dyn, pw), 0.0)
            cnt = jnp.minimum(tok + 1, w).astype(F32)
            ext_ref[0:tm, :] = dd / cnt[0:tm]
            ext_ref[tm:tm + 16, :] = ddn / cnt[tm:tm + 16]
            s = ext_ref[pl.ds(0, tm), :]
            for k in range(1, w):
                s = s + ext_ref[pl.ds(k, tm), :]
            du_ref[:, gi * POOL_GROUP:(gi + 1) * POOL_GROUP] = _mx(s - dd)
            dpw_ref[gi] += _dot_tn(d_ref[:, gi * POOL_GROUP:(gi + 1) * POOL_GROUP], dy)

    nxt = pl.BlockSpec((16, D_MODEL), lambda i: (jnp.minimum((i + 1) * (tm // 16), t // 16 - 1), 0))
    return _pc(body, "pool_bwd", (n,),
               [_row(tm, D_MODEL), nxt, _row(tm, POOL_WIDTH), _const((4, POOL_GROUP, POOL_OUT_GROUP)), ANY],
               [pl.BlockSpec((tm, POOL_WIDTH), lambda i: (i, K_U // POOL_WIDTH)),
                _const((4, POOL_GROUP, POOL_OUT_GROUP))],
               [SDS(dproj.shape, dproj.dtype), SDS((4, POOL_GROUP, POOL_OUT_GROUP), F32)],
               scratch=[pltpu.VMEM((tm + 16, POOL_GROUP), F32)], sem=("arbitrary",),
               aliases={4: 0})(dyp, dyp, d_bf, pool_w, dproj)


CONV_BLK = 512


CONV_ROWS = 32


def _conv_rows(ext_ref, w, r, rows):
    y = w[0] * ext_ref[pl.ds(r + 5, rows), :]
    for k in range(1, CONV_K):
        y = y + w[k] * ext_ref[pl.ds(r + 5 + k, rows), :]
    return y


def _conv_bwd(dact, dsilu, proj, conv_w, dproj, tm):
    t = proj.shape[0]
    n = t // tm

    def body(da_ref, dan_ref, ds_ref, dsn_ref, x_ref, xp_ref, w_ref, dproj_ref, dx_ref, dw_ref, ext_ref, dy_ref):
        i = pl.program_id(1)

        @pl.when(i == 0)
        def _():
            dw_ref[...] = jnp.zeros_like(dw_ref)

        ext_ref[0:8, :] = jnp.where(i > 0, xp_ref[...], 0.0)
        ext_ref[8:8 + tm, :] = x_ref[...]
        w = [w_ref[pl.ds(k, 1), :] for k in range(CONV_K)]

        acc = [jnp.zeros((8, CONV_BLK), F32) for _ in range(CONV_K)]
        for r in range(0, tm, CONV_ROWS):
            dy = da_ref[pl.ds(r, CONV_ROWS), :] * ds_ref[pl.ds(r, CONV_ROWS), :].astype(F32)
            dy_ref[pl.ds(r, CONV_ROWS), :] = dy
            for k in range(CONV_K):
                prod = dy * ext_ref[pl.ds(r + 5 + k, CONV_ROWS), :]
                for q in range(0, CONV_ROWS, 8):
                    acc[k] = acc[k] + prod[q:q + 8]
        dy_ref[tm:tm + 8, :] = jnp.where(i < n - 1, dan_ref[...] * dsn_ref[0:8, :].astype(F32), 0.0)
        for k in range(CONV_K):
            dw_ref[pl.ds(k, 1), :] += jnp.sum(acc[k], axis=0, keepdims=True)
        for r in range(0, tm, CONV_ROWS):
            dx = w[0] * dy_ref[pl.ds(r + 3, CONV_ROWS), :]
            for k in range(1, CONV_K):
                dx = dx + w[k] * dy_ref[pl.ds(r + 3 - k, CONV_ROWS), :]
            dx_ref[pl.ds(r, CONV_ROWS), :] = _mx(dx)

    blk = pl.BlockSpec((tm, CONV_BLK), lambda j, i: (i, j))
    prev = pl.BlockSpec((8, CONV_BLK), lambda j, i: (jnp.maximum(i * (tm // 8) - 1, 0), j))
    nxt = pl.BlockSpec((8, CONV_BLK), lambda j, i: (jnp.minimum((i + 1) * (tm // 8), t // 8 - 1), j))
    nxt16 = pl.BlockSpec((16, CONV_BLK), lambda j, i: (jnp.minimum((i + 1) * (tm // 16), t // 16 - 1), j))
    wspec = pl.BlockSpec((CONV_K, CONV_BLK), lambda j, i: (0, j))
    return _pc(body, "conv_bwd", (QKV_WIDTH // CONV_BLK, n),
               [blk, nxt, blk, nxt16, blk, prev, wspec, ANY],
               [blk, pl.BlockSpec((8, CONV_BLK), lambda j, i: (0, j))],
               [SDS(dproj.shape, dproj.dtype), SDS((8, QKV_WIDTH), F32)],
               scratch=[pltpu.VMEM((8 + tm, CONV_BLK), F32), pltpu.VMEM((8 + tm, CONV_BLK), F32)],
               sem=("parallel", "arbitrary"), aliases={7: 0})(dact, dact, dsilu, dsilu, proj, proj, conv_w, dproj)


def _lane(shape):
    return lax.broadcasted_iota(jnp.int32, shape, 1)


def _ba_fwd(proj, al_row, dtb_row, tm):
    t = proj.shape[0]
    bablk = K_BA // 128

    def body(ba_ref, al_ref, dtb_ref, bg_ref):
        ba = ba_ref[...]
        lane = _lane(ba.shape)
        g = -jnp.exp(al_ref[...]) * _softplus(ba + dtb_ref[...])
        bg_ref[...] = jnp.where(lane < HEADS, _sigmoid(ba), jnp.where(lane < 2 * HEADS, g, 0.0))

    return _pc(body, "ba_fwd", (t // tm,),
               [pl.BlockSpec((tm, 128), lambda i: (i, bablk)), _const((1, 128)), _const((1, 128))],
               _row(tm, 128), SDS((t, 128), F32), sem=("parallel",))(proj, al_row, dtb_row)


def _ba_bwd(dbg, bg, proj, al_row, dtb_row, dproj, tm):
    t = proj.shape[0]
    bablk = K_BA // 128

    def body(dbg_ref, bg_ref, ba_ref, al_ref, dtb_ref, dproj_ref, dba_ref, acc_ref):
        i = pl.program_id(0)

        @pl.when(i == 0)
        def _():
            acc_ref[...] = jnp.zeros_like(acc_ref)

        dbg_v, bg_v, ba = dbg_ref[...], bg_ref[...], ba_ref[...]
        lane = _lane(ba.shape)
        is_g = (lane >= HEADS) & (lane < 2 * HEADS)
        dbeta_raw = dbg_v * bg_v * (1.0 - bg_v)
        da_raw = dbg_v * (-jnp.exp(al_ref[...])) * _sigmoid(ba + dtb_ref[...])
        dba_ref[:, 0:128] = _mx(jnp.where(lane < HEADS, dbeta_raw, jnp.where(is_g, da_raw, 0.0)))
        dba_ref[:, 128:CAT_WIDTH - K_BA] = jnp.zeros((tm, CAT_WIDTH - K_BA - 128), dba_ref.dtype)
        acc_ref[0:1, :] += jnp.sum(jnp.where(is_g, dbg_v * bg_v, 0.0), axis=0, keepdims=True)
        acc_ref[1:2, :] += jnp.sum(jnp.where(is_g, da_raw, 0.0), axis=0, keepdims=True)

    tail = CAT_WIDTH - K_BA
    return _pc(body, "ba_bwd", (t // tm,),
               [_row(tm, 128), _row(tm, 128), pl.BlockSpec((tm, 128), lambda i: (i, bablk)),
                _const((1, 128)), _const((1, 128)), ANY],
               [pl.BlockSpec((tm, tail), lambda i: (i, K_BA // tail)), _const((8, 128))],
               [SDS(dproj.shape, dproj.dtype), SDS((8, 128), F32)],
               sem=("arbitrary",), aliases={5: 0})(dbg, bg, proj, al_row, dtb_row, dproj)


def _each(f, *lists):
    return [f(*a) for a in zip(*lists)]


def _rowsum(a):
    return jnp.sum(a, axis=1, keepdims=True)


def _chunk_terms(qs, ks, bgv, g_rows, hs):
    c = CHUNK
    ii = lax.broadcasted_iota(jnp.int32, (c, c), 0)
    jj = lax.broadcasted_iota(jnp.int32, (c, c), 1)
    lane = _lane(bgv.shape)
    incl = ii >= jj
    beta = [_rowsum(jnp.where(lane == h, bgv, 0.0)) for h in hs]
    g_col = [_rowsum(jnp.where(lane == HEADS + h, bgv, 0.0)) for h in hs]
    rq = _each(lambda q: lax.rsqrt(_rowsum(q * q) + L2_EPS), qs)
    rk = _each(lambda k: lax.rsqrt(_rowsum(k * k) + L2_EPS), ks)
    yq = _each(jnp.multiply, qs, rq)
    kn = _each(jnp.multiply, ks, rk)
    qn = _each(lambda a: a * Q_SCALE, yq)
    gc_col = _each(lambda g: _rowsum(jnp.where(jj <= ii, g, 0.0)), g_rows)
    gc_row = _each(lambda g: jnp.sum(jnp.where(ii <= jj, g, 0.0), axis=0, keepdims=True), g_col)
    dm = _each(lambda a, b: jnp.where(incl, jnp.exp(jnp.where(incl, a - b, 0.0)), 0.0), gc_col, gc_row)
    gl = _each(_rowsum, g_rows)
    eg = _each(jnp.exp, gc_col)
    ek = _each(lambda a, b: jnp.exp(a - b), gl, gc_col)
    egl = _each(jnp.exp, gl)
    kb = _each(jnp.multiply, kn, beta)
    kk = _each(_dot_nt, kb, kn)
    qk = _each(_dot_nt, qn, kn)
    m = _each(lambda a, b: jnp.where(ii > jj, a * b, 0.0), kk, dm)
    attn = _each(jnp.multiply, qk, dm)
    return dict(ii=ii, jj=jj, beta=beta, rq=rq, rk=rk, yq=yq, kn=kn, qn=qn, dm=dm, eg=eg, ek=ek,
                egl=egl, kb=kb, m=m, attn=attn)


def _unit_lower_inverse_minus_identity(ms, ii, jj):
    pair = (ii >> 1) == (jj >> 1)
    ys = _each(lambda m: -jnp.where(pair, m, 0.0), ms)
    s = 1
    while (1 << s) < CHUNK:
        mask = ((ii >> (s + 1)) == (jj >> (s + 1))) & ((ii >> s) != (jj >> s))
        lbs = _each(lambda m: jnp.where(mask, m, 0.0), ms)
        zs = _each(lambda y, lb: lb + _dot(y, lb), ys, lbs)
        ys = _each(lambda y, z: y - z - _dot(z, y), ys, zs)
        s += 1
    return ys


def _dn_fwd(qkv_act, bg, bgt):
    t = qkv_act.shape[0]
    nt = t // CHUNK
    c = CHUNK
    hs = list(range(HEADS))
    qo = [slice(h * HEAD_DIM, (h + 1) * HEAD_DIM) for h in hs]
    ko = [slice(DN_WIDTH + h * HEAD_DIM, DN_WIDTH + (h + 1) * HEAD_DIM) for h in hs]
    vo = [slice(2 * DN_WIDTH + h * HEAD_DIM, 2 * DN_WIDTH + (h + 1) * HEAD_DIM) for h in hs]

    def body(qkv_ref, bg_ref, bgt_ref, o_ref, u_ref, w_ref, qg_ref, kg_ref, attn_ref, y_ref, vn_ref, st_ref, egl_ref,
             s_ref):
        @pl.when(pl.program_id(0) == 0)
        def _():
            s_ref[...] = jnp.zeros_like(s_ref)

        bgv = bg_ref[...]
        qs = [qkv_ref[:, o] for o in qo]
        ks = [qkv_ref[:, o] for o in ko]
        vs = [qkv_ref[:, o] for o in vo]
        g_rows = [bgt_ref[pl.ds(HEADS + h, 1), :] for h in hs]
        ct = _chunk_terms(qs, ks, bgv, g_rows, hs)
        ys = _unit_lower_inverse_minus_identity(ct["m"], ct["ii"], ct["jj"])
        vb = _each(jnp.multiply, vs, ct["beta"])
        kbe = _each(jnp.multiply, ct["kb"], ct["eg"])
        us = _each(lambda a, y: a + _dot(y, a), vb, ys)
        ws = _each(lambda a, y: _mx(a + _dot(y, a)), kbe, ys)
        qg = _each(lambda a, b: _mx(a * b), ct["qn"], ct["eg"])
        kg = _each(lambda a, b: _mx(a * b), ct["kn"], ct["ek"])
        attn = _each(_mx, ct["attn"])
        ss = [s_ref[h] for h in hs]
        sb = _each(_mx, ss)
        vn = _each(lambda a, b, s_: a - _dot(b, s_), us, ws, sb)
        vnb = _each(_mx, vn)
        oa = _each(_dot, qg, sb)
        ob = _each(_dot, attn, vnb)
        upd = _each(_dot_tn, kg, vnb)
        for h, sl in enumerate(qo):
            u_ref[:, sl] = us[h]
            w_ref[:, sl] = ws[h]
            qg_ref[:, sl] = qg[h]
            kg_ref[:, sl] = kg[h]
            attn_ref[:, sl] = attn[h]
            y_ref[:, sl] = _mx(ys[h])
            egl_ref[0, h:h + 1, :] = jnp.broadcast_to(ct["egl"][h], (1, HEAD_DIM))
            st_ref[0, h] = ss[h]
            vn_ref[:, sl] = vnb[h]
            o_ref[:, sl] = oa[h] + ob[h]
            s_ref[h] = ss[h] * ct["egl"][h] + upd[h]

    wide = _row(c, DN_WIDTH)
    return _pc(body, "dn_fwd", (nt,),
               [_row(c, QKV_WIDTH), _row(c, 128), pl.BlockSpec((2 * HEADS, c), lambda i: (0, i))],
               [wide] * 8 + [pl.BlockSpec((1, HEADS, HEAD_DIM, HEAD_DIM), lambda i: (i, 0, 0, 0)),
                             pl.BlockSpec((1, HEADS, HEAD_DIM), lambda i: (i, 0, 0))],
               [SDS((t, DN_WIDTH), F32), SDS((t, DN_WIDTH), F32)] + [SDS((t, DN_WIDTH), MXU_DTYPE)] * 6
               + [SDS((nt, HEADS, HEAD_DIM, HEAD_DIM), F32), SDS((nt, HEADS, HEAD_DIM), F32)],
               scratch=[pltpu.VMEM((HEADS, HEAD_DIM, HEAD_DIM), F32)], sem=("arbitrary",))(qkv_act, bg, bgt)


def _dn_bwd(do, qkv_act, bg, bgt, u, w, qg, kg, attn, ymat, vn, states, egl):
    t = do.shape[0]
    nt = t // CHUNK
    c = CHUNK
    hs = list(range(HEADS))
    qo = [slice(h * HEAD_DIM, (h + 1) * HEAD_DIM) for h in hs]
    ko = [slice(DN_WIDTH + h * HEAD_DIM, DN_WIDTH + (h + 1) * HEAD_DIM) for h in hs]
    vo = [slice(2 * DN_WIDTH + h * HEAD_DIM, 2 * DN_WIDTH + (h + 1) * HEAD_DIM) for h in hs]

    def body(do_ref, qkv_ref, bg_ref, bgt_ref, u_ref, w_ref, qg_ref, kg_ref, attn_ref, y_ref, vn_ref, st_ref, egl_ref,
             dqkv_ref, dbg_ref, ds_ref):
        @pl.when(pl.program_id(0) == 0)
        def _():
            ds_ref[...] = jnp.zeros_like(ds_ref)

        dsp = [ds_ref[h] for h in hs]
        dsb = _each(_mx, dsp)
        ss = [st_ref[0, h] for h in hs]
        sb = _each(_mx, ss)
        du = [_dot(kg_ref[:, sl], b) + _dot_tn(attn_ref[:, sl], do_ref[:, sl]) for sl, b in zip(qo, dsb)]
        dub = _each(_mx, du)
        dkg_v = [_dot_nt(vn_ref[:, sl], b) for sl, b in zip(qo, dsb)]
        dqg_v = [_dot_nt(do_ref[:, sl], b) for sl, b in zip(qo, sb)]
        dattn_v = [_dot_nt(do_ref[:, sl], vn_ref[:, sl]) for sl in qo]
        dwv = [-_dot_nt(a, b) for a, b in zip(dub, sb)]
        upd = [_dot_tn(qg_ref[:, sl], do_ref[:, sl]) - _dot_tn(w_ref[:, sl], a) for sl, a in zip(qo, dub)]
        degl_v = [jnp.sum(_rowsum(a * b), axis=0, keepdims=True) for a, b in zip(ss, dsp)]
        for h in hs:
            ds_ref[h] = dsp[h] * egl_ref[0, h:h + 1, :] + upd[h]

        bgv = bg_ref[...]
        lane = _lane(bgv.shape)
        rowi = lax.broadcasted_iota(jnp.int32, (c, 1), 0)
        qs = [qkv_ref[:, o] for o in qo]
        ks = [qkv_ref[:, o] for o in ko]
        vs = [qkv_ref[:, o] for o in vo]
        g_rows = [bgt_ref[pl.ds(HEADS + h, 1), :] for h in hs]
        ct = _chunk_terms(qs, ks, bgv, g_rows, hs)
        ii, jj = ct["ii"], ct["jj"]
        beta, eg, ek, kb, kn, qn, dm = ct["beta"], ct["eg"], ct["ek"], ct["kb"], ct["kn"], ct["qn"], ct["dm"]
        ys = [y_ref[:, o] for o in qo]
        dvb = _each(lambda a, y: a + _dot_tn(y, a), du, ys)
        dkbe = _each(lambda a, y: a + _dot_tn(y, a), dwv, ys)
        dm_u = [_dot_nt(a, u_ref[:, o]) for a, o in zip(dvb, qo)]
        dm_w = [_dot_nt(a, w_ref[:, o]) for a, o in zip(dkbe, qo)]
        dms = _each(lambda a, b: jnp.where(ii > jj, -(a + b), 0.0), dm_u, dm_w)
        dkk = _each(jnp.multiply, dms, dm)
        dqk = _each(jnp.multiply, dattn_v, dm)
        gmat = _each(lambda a, b, c_, d: a * b + c_ * d, dms, ct["m"], dattn_v, ct["attn"])
        dkb = _each(lambda a, b, c_, d: _dot(a, b) + c_ * d, dkk, kn, dkbe, eg)
        dk1 = _each(_dot_tn, dkk, kb)
        dk2 = _each(_dot_tn, dqk, qn)
        dq1 = _each(_dot, dqk, kn)
        dk = _each(lambda a, b, c_, d: a + b + c_ * d, dk1, dk2, dkg_v, ek)
        dq = _each(lambda a, b, c_: a + b * c_, dq1, dqg_v, eg)
        deg = _each(lambda a, b, c_, d: _rowsum(a * b) + _rowsum(c_ * d), dqg_v, qn, dkbe, kb)
        dek = _each(lambda a, b: _rowsum(a * b), dkg_v, kn)
        dgl = _each(lambda a, b, c_, d: jnp.sum(a * b, axis=0, keepdims=True) + c_ * d, dek, ek, degl_v, ct["egl"])
        cs_row = _each(lambda g: jnp.sum(g, axis=0, keepdims=True), gmat)
        cs_col = _each(lambda r: _rowsum(jnp.where(ii == jj, r, 0.0)), cs_row)
        dgc = _each(lambda a, b, c_, d, g, e, f: a * b - c_ * d + _rowsum(g) - e + jnp.where(rowi == c - 1, f, 0.0),
                    deg, eg, dek, ek, gmat, cs_col, dgl)
        dgc_row = _each(lambda a: jnp.sum(jnp.where(ii == jj, a, 0.0), axis=0, keepdims=True), dgc)
        dg = _each(lambda r: _rowsum(jnp.where(jj >= ii, r, 0.0)), dgc_row)
        dbeta = _each(lambda a, b, c_, d: _rowsum(a * b) + _rowsum(c_ * d), dkb, kn, dvb, vs)
        dk = _each(lambda a, b, c_: a + b * c_, dk, dkb, beta)
        dbg = jnp.zeros((c, 128), F32)
        for h in hs:
            dyq = dq[h] * Q_SCALE
            yq = ct["yq"][h]
            dqkv_ref[:, qo[h]] = ct["rq"][h] * (dyq - yq * _rowsum(yq * dyq))
            dqkv_ref[:, ko[h]] = ct["rk"][h] * (dk[h] - kn[h] * _rowsum(kn[h] * dk[h]))
            dqkv_ref[:, vo[h]] = dvb[h] * beta[h]
            dbg = dbg + jnp.where(lane == h, dbeta[h], 0.0) + jnp.where(lane == HEADS + h, dg[h], 0.0)
        dbg_ref[...] = dbg

    rev = pl.BlockSpec((c, DN_WIDTH), lambda i: (nt - 1 - i, 0))
    return _pc(body, "dn_bwd", (nt,),
               [rev, pl.BlockSpec((c, QKV_WIDTH), lambda i: (nt - 1 - i, 0)),
                pl.BlockSpec((c, 128), lambda i: (nt - 1 - i, 0)), pl.BlockSpec((2 * HEADS, c), lambda i: (0, nt - 1 - i))]
               + [rev] * 7
               + [pl.BlockSpec((1, HEADS, HEAD_DIM, HEAD_DIM), lambda i: (nt - 1 - i, 0, 0, 0)),
                  pl.BlockSpec((1, HEADS, HEAD_DIM), lambda i: (nt - 1 - i, 0, 0))],
               [pl.BlockSpec((c, QKV_WIDTH), lambda i: (nt - 1 - i, 0)), pl.BlockSpec((c, 128), lambda i: (nt - 1 - i, 0))],
               [SDS((t, QKV_WIDTH), F32), SDS((t, 128), F32)],
               scratch=[pltpu.VMEM((HEADS, HEAD_DIM, HEAD_DIM), F32)],
               sem=("arbitrary",))(do, qkv_act, bg, bgt, u, w, qg, kg, attn, ymat, vn, states, egl)


MIX_ROWS = 64


def _mix_oproj_ln1(o, proj, ypre, pool_scale, wo_row, w_out, h0, g1, b1, tm):
    t = o.shape[0]

    def body(o_ref, z_ref, ga_ref, gb_ref, yp_ref, ps_ref, wo_ref, w_ref, h0_ref, g_ref, b_ref,
             mixed_ref, a1_ref, h1_ref, h1b_ref):
        for r in range(0, tm, MIX_ROWS):
            rows = pl.ds(r, MIX_ROWS)
            for h in range(HEADS):
                sl = slice(h * HEAD_DIM, (h + 1) * HEAD_DIM)
                oh = o_ref[rows, sl]
                on = oh * lax.rsqrt(jnp.mean(oh * oh, axis=1, keepdims=True) + RMS_EPS)
                zh = z_ref[rows, sl]
                yb = on * wo_ref[:, sl] * (zh * _sigmoid(zh))
                ya = yp_ref[rows, sl] * ps_ref[:, sl]
                mixed_ref[rows, sl] = _mx(_sigmoid(ga_ref[rows, sl]) * ya + _sigmoid(gb_ref[rows, sl]) * yb)
        a1 = ALPHA * h0_ref[...] + _dot(mixed_ref[...], w_ref[...])
        a1_ref[...] = a1
        xhat, _ = _ln_stats(a1)
        h1 = xhat * g_ref[...] + b_ref[...]
        h1_ref[...] = h1
        h1b_ref[...] = _mx(h1)

    def col(blk):
        return pl.BlockSpec((tm, D_MODEL), lambda i: (i, blk))

    r = _row(tm, D_MODEL)
    v = _const((1, D_MODEL))
    return _pc(body, "mix_oproj_ln1", (t // tm,),
               [r, col(K_Z // D_MODEL), col(K_GA // D_MODEL), col(K_GB // D_MODEL), r, v, v,
                _const((D_MODEL, D_MODEL)), r, v, v],
               [r, r, r, r],
               [SDS((t, D_MODEL), MXU_DTYPE), SDS((t, D_MODEL), F32), SDS((t, D_MODEL), F32),
                SDS((t, D_MODEL), MXU_DTYPE)],
               sem=("parallel",))(o, proj, proj, proj, ypre, pool_scale, wo_row, w_out, h0, g1, b1)


def _mix_bwd(da1_bf, w_out, o, proj, ypre, pool_scale, wo_row, tm, after):
    t = o.shape[0]

    def body(da_ref, wout_ref, o_ref, z_ref, ga_ref, gb_ref, yp_ref, ps_ref, wo_ref, after_ref,
             do_ref, dp_ref, dyp_ref, acc_ref, dm_ref):
        i = pl.program_id(0)

        @pl.when(i == 0)
        def _():
            acc_ref[...] = jnp.zeros_like(acc_ref)

        dm_ref[...] = _dot_nt(da_ref[...], wout_ref[...])
        dwo = jnp.zeros((1, HEAD_DIM), F32)
        for h in range(HEADS):
            sl = slice(h * HEAD_DIM, (h + 1) * HEAD_DIM)
            woh = wo_ref[:, sl]
            psh = ps_ref[:, sl]
            dps = jnp.zeros((1, HEAD_DIM), F32)
            for r in range(0, tm, MIX_ROWS):
                rows = pl.ds(r, MIX_ROWS)
                oh = o_ref[rows, sl]
                rs = lax.rsqrt(jnp.mean(oh * oh, axis=1, keepdims=True) + RMS_EPS)
                on = oh * rs
                zh = z_ref[rows, sl]
                sz = _sigmoid(zh)
                silu = zh * sz
                t1 = on * woh
                yb = t1 * silu
                sa = _sigmoid(ga_ref[rows, sl])
                sb = _sigmoid(gb_ref[rows, sl])
                yp = yp_ref[rows, sl]
                dm = dm_ref[rows, sl]
                ga_sl = slice(D_MODEL + h * HEAD_DIM, D_MODEL + (h + 1) * HEAD_DIM)
                gb_sl = slice(2 * D_MODEL + h * HEAD_DIM, 2 * D_MODEL + (h + 1) * HEAD_DIM)
                dp_ref[rows, ga_sl] = _mx(dm * (yp * psh) * sa * (1.0 - sa))
                dp_ref[rows, gb_sl] = _mx(dm * yb * sb * (1.0 - sb))
                dya = dm * sa
                dyb = dm * sb
                dyp_ref[rows, sl] = _mx(dya * psh)
                dps = dps + jnp.sum(dya * yp, axis=0, keepdims=True)
                dp_ref[rows, sl] = _mx(dyb * t1 * (sz * (1.0 + zh * (1.0 - sz))))
                dt1 = dyb * silu
                dwo = dwo + jnp.sum(dt1 * on, axis=0, keepdims=True)
                don = dt1 * woh
                do_ref[rows, sl] = _mx(rs * (don - on * jnp.mean(don * on, axis=1, keepdims=True)))
            acc_ref[0:1, sl] += dps
        acc_ref[1:2, 0:HEAD_DIM] += dwo

    def col(blk):
        return pl.BlockSpec((tm, D_MODEL), lambda i: (i, blk))

    r = _row(tm, D_MODEL)
    return _pc(body, "mix_bwd", (t // tm,),
               [r, _const((D_MODEL, D_MODEL)), r, col(K_Z // D_MODEL), col(K_GA // D_MODEL), col(K_GB // D_MODEL), r,
                _const((1, D_MODEL)), _const((1, D_MODEL)), ANY],
               [r, pl.BlockSpec((tm, 3 * D_MODEL), lambda i: (i, K_Z // (3 * D_MODEL))), r, _const((8, D_MODEL))],
               [SDS((t, D_MODEL), MXU_DTYPE), SDS((t, CAT_WIDTH), MXU_DTYPE), SDS((t, D_MODEL), MXU_DTYPE),
                SDS((8, D_MODEL), F32)],
               scratch=[pltpu.VMEM((tm, D_MODEL), F32)],
               sem=("arbitrary",))(da1_bf, w_out, o, proj, proj, proj, ypre, pool_scale, wo_row, after)


def _mlp_up(h1_bf, w_up, tm):
    t = h1_bf.shape[0]
    tn = w_up.shape[2]

    def body(h_ref, w_ref, act_ref):
        r = jnp.maximum(_dot(h_ref[...], w_ref[...]), 0.0)
        act_ref[...] = _mx(r * r)

    return _pc(body, "mlp_up", (D_FF // tn, t // tm),
               [pl.BlockSpec((tm, D_MODEL), lambda j, i: (i, 0)),
                pl.BlockSpec((None, D_MODEL, tn), lambda j, i: (j, 0, 0))],
               pl.BlockSpec((tm, tn), lambda j, i: (i, j)), SDS((t, D_FF), MXU_DTYPE),
               sem=("parallel", "parallel"))(h1_bf, w_up)


def _tail(act, w_down, h1, w_gate, p_bf, w_proj, tgt, g2, b2, tm):
    t = act.shape[0]

    def body(act_ref, wd_ref, h1_ref, wg_ref, p_ref, wp_ref, tgt_ref, g_ref, b_ref,
             dr_ref, drb_ref, dgp_ref, dpp_ref, rb_ref, acc_ref):
        i = pl.program_id(0)

        @pl.when(i == 0)
        def _():
            acc_ref[...] = jnp.zeros_like(acc_ref)

        r = ALPHA * h1_ref[...] + _dot(act_ref[...], wd_ref[...])
        rb = _mx(r)
        rb_ref[...] = rb
        gate = _sigmoid(_dot(rb, wg_ref[...]))
        pp = _dot(p_ref[...], wp_ref[...])
        xhat, rstd = _ln_stats(r + gate * pp)
        g = g_ref[...]
        diff = xhat * g + b_ref[...] - tgt_ref[...]
        dh2 = diff * (1.0 / D_MODEL)
        rowloss = jnp.sum(diff * diff, axis=1, keepdims=True) * (0.5 / D_MODEL)
        acc_ref[0:1, :] += jnp.sum(dh2 * xhat, axis=0, keepdims=True)
        acc_ref[1:2, :] += jnp.sum(dh2, axis=0, keepdims=True)
        acc_ref[2:3, :] += jnp.broadcast_to(jnp.sum(rowloss, axis=0, keepdims=True), (1, D_MODEL))
        da2 = _ln_bwd(dh2, xhat, rstd, g)
        dpp_ref[...] = _mx(da2 * gate)
        dgp = _mx(da2 * pp * gate * (1.0 - gate))
        dgp_ref[...] = dgp
        dr = da2 + _dot_nt(dgp, wg_ref[...])
        dr_ref[...] = dr
        drb_ref[...] = _mx(dr)

    r = _row(tm, D_MODEL)
    v = _const((1, D_MODEL))
    return _pc(body, "tail", (t // tm,),
               [_row(tm, D_FF), _const((D_FF, D_MODEL)), r, _const((D_MODEL, D_MODEL)), _row(tm, PLE_DIM),
                _const((PLE_DIM, D_MODEL)), r, v, v],
               [r, r, r, r, r, _const((8, D_MODEL))],
               [SDS((t, D_MODEL), F32)] + [SDS((t, D_MODEL), MXU_DTYPE)] * 4 + [SDS((8, D_MODEL), F32)],
               sem=("arbitrary",))(act, w_down, h1, w_gate, p_bf, w_proj, tgt, g2, b2)


SQRT_GUARD = 1e-30


def _mlp_bwd1(dr_bf, w_down, act, tm, tn):
    t = act.shape[0]

    def body(dr_ref, w_ref, act_ref, dup_ref):
        dact = _dot_nt(dr_ref[...], w_ref[...])
        a = act_ref[...].astype(F32)
        dup_ref[...] = _mx(dact * (2.0 * a * lax.rsqrt(a + SQRT_GUARD)))

    o = pl.BlockSpec((tm, tn), lambda j, i: (i, j))
    return _pc(body, "mlp_bwd1", (D_FF // tn, t // tm),
               [pl.BlockSpec((tm, D_MODEL), lambda j, i: (i, 0)), pl.BlockSpec((tn, D_MODEL), lambda j, i: (j, 0)), o],
               o, SDS((t, D_FF), MXU_DTYPE), sem=("parallel", "parallel"))(dr_bf, w_down, act)


def _mlp_bwd2(dup, w_up, dr, a1, g1, tm):
    t = dr.shape[0]

    nk, tk = w_up.shape[0], w_up.shape[2]

    def body(dup_ref, w_ref, dr_ref, a1_ref, g_ref, da1_ref, da1b_ref, acc_ref):
        i = pl.program_id(0)

        @pl.when(i == 0)
        def _():
            acc_ref[...] = jnp.zeros_like(acc_ref)

        dh1 = ALPHA * dr_ref[...]
        for kk in range(nk):
            dh1 = dh1 + _dot_nt(dup_ref[:, kk * tk:(kk + 1) * tk], w_ref[kk])
        xhat, rstd = _ln_stats(a1_ref[...])
        acc_ref[0:1, :] += jnp.sum(dh1 * xhat, axis=0, keepdims=True)
        acc_ref[1:2, :] += jnp.sum(dh1, axis=0, keepdims=True)
        da1 = _ln_bwd(dh1, xhat, rstd, g_ref[...])
        da1_ref[...] = da1
        da1b_ref[...] = _mx(da1)

    r = _row(tm, D_MODEL)
    return _pc(body, "mlp_bwd2", (t // tm,),
               [_row(tm, D_FF), _const((nk, D_MODEL, tk)), r, r, _const((1, D_MODEL))],
               [r, r, _const((8, D_MODEL))],
               [SDS((t, D_MODEL), F32), SDS((t, D_MODEL), MXU_DTYPE), SDS((8, D_MODEL), F32)],
               sem=("arbitrary",))(dup, w_up, dr, a1, g1)


def _ln_in_bwd(dproj, w_cat, da1, x, g, tm, after):
    t = x.shape[0]

    def body(dp_ref, w_ref, da1_ref, x_ref, g_ref, after_ref, dx_ref, acc_ref):
        i = pl.program_id(0)

        @pl.when(i == 0)
        def _():
            acc_ref[...] = jnp.zeros_like(acc_ref)

        dh0 = _dot_nt(dp_ref[...], w_ref[...]) + ALPHA * da1_ref[...]
        xhat, rstd = _ln_stats(x_ref[...])
        acc_ref[0:1, :] += jnp.sum(dh0 * xhat, axis=0, keepdims=True)
        acc_ref[1:2, :] += jnp.sum(dh0, axis=0, keepdims=True)
        dx_ref[...] = _ln_bwd(dh0, xhat, rstd, g_ref[...])

    r = _row(tm, D_MODEL)
    return _pc(body, "ln_in_bwd", (t // tm,),
               [_row(tm, CAT_WIDTH), _const((D_MODEL, CAT_WIDTH)), r, r, _const((1, D_MODEL)), ANY],
               [r, _const((8, D_MODEL))], [SDS((t, D_MODEL), F32), SDS((8, D_MODEL), F32)],
               sem=("arbitrary",))(dproj, w_cat, da1, x, g, after)


def _local_step(x, p, tgt, wts, start_token, first_weights, late_weights, send_late_grads, send_early_grads):
    t = x.shape[0]
    tm = min(512, t)
    tms = min(256, t)
    row = lambda a: a.reshape(1, -1)
    pool_scale = row(wts["pool_scale"])
    wo_row = jnp.tile(row(wts["o_norm_w"]), (1, HEADS))
    pad8 = jnp.zeros((1, HEADS), F32)
    al_row = jnp.concatenate([pad8, row(wts["a_log"]), jnp.zeros((1, 128 - 2 * HEADS), F32)], axis=1)
    dtb_row = jnp.concatenate([pad8, row(wts["dt_bias"]), jnp.zeros((1, 128 - 2 * HEADS), F32)], axis=1)
    g_in, b_in = row(wts["ln_in_g"]), row(wts["ln_in_b"])
    g1, b1 = row(wts["ln1_g"]), row(wts["ln1_b"])
    g2, b2 = row(wts["ln2_g"]), row(wts["ln2_b"])

    h0, h0_bf = _ln_in(x, g_in, b_in, tm, start_token)
    first, first_token = first_weights(h0_bf)
    wts = {**wts, **first}
    w_cat = wts["w_cat"]
    proj, qkv_act, dsilu = _proj_conv(h0_bf, w_cat, wts["conv_w"], tms, first_token)
    ypre, d_bf = _pool_fwd(proj, wts["pool_w"], tm)
    bg = _ba_fwd(proj, al_row, dtb_row, tm)
    bgt = bg[:, :2 * HEADS].T
    o, u, w, qg, kg, attn, ymat, vn, states, egl = _dn_fwd(qkv_act, bg, bgt)
    wts = {**wts, **late_weights(o)}
    mixed, a1, h1, h1_bf = _mix_oproj_ln1(o, proj, ypre, pool_scale, wo_row, wts["w_out"], h0, g1, b1, tms)
    act = _mlp_up(h1_bf, wts["w_up"], tm)
    p_bf = _mx(p)
    dr, dr_bf, dgp, dpp, r_bf, acc_tail = _tail(act, wts["w_down"], h1, wts["ple_gate_w"], p_bf, wts["ple_proj_w"],
                                                tgt, g2, b2, tms)
    grads = {}
    grads["ple_proj_w"] = _matmul(p_bf, dpp, "tn", "dw_ple_proj", WIRE_DTYPE, tm=256, tn=1024, tk=DW_TK)
    grads["ple_gate_w"] = _matmul(r_bf, dgp, "tn", "dw_ple_gate", WIRE_DTYPE, tm=DW_TM, tn=1024, tk=DW_TK)
    grads["w_down"] = _matmul(act, dr_bf, "tn", "dw_down", WIRE_DTYPE, tm=DW_TM, tn=1024, tk=DW_TK)
    dup = _mlp_bwd1(dr_bf, wts["w_down"], act, tm, 1024)
    grads["w_up"] = _matmul(h1_bf, dup, "tn", "dw_up", WIRE_DTYPE, tm=DW_TM, tn=1024, tk=DW_TK, stack_out=True)
    da1, da1_bf, acc_ln1 = _mlp_bwd2(dup, wts["w_up"], dr, a1, g1, tms)
    grads["w_out"] = _matmul(mixed, da1_bf, "tn", "dw_out", WIRE_DTYPE, tm=DW_TM, tn=1024, tk=DW_TK)
    sent = send_late_grads(grads)
    do, dproj, dyp, acc_mix = _mix_bwd(da1_bf, wts["w_out"], o, proj, ypre, pool_scale, wo_row, tms, sent)
    dproj, grads["pool_w"] = _pool_bwd(dyp, d_bf, wts["pool_w"], dproj, tm)
    dqkv_act, dbg = _dn_bwd(do, qkv_act, bg, bgt, u, w, qg, kg, attn, ymat, vn, states, egl)
    dproj, acc_conv = _conv_bwd(dqkv_act, dsilu, proj, wts["conv_w"], dproj, tm)
    dproj, acc_ba = _ba_bwd(dbg, bg, proj, al_row, dtb_row, dproj, tm)
    dw_cat = _matmul(h0_bf, dproj, "tn", "dw_in", F32, tm=DW_TM, tn=1152, tk=DW_TK)
    grads["w_in"] = jnp.concatenate(
        [dw_cat[:, K_U:K_U + 512], dw_cat[:, K_QKV:K_QKV + 3072], dw_cat[:, K_Z:K_Z + 1024],
         dw_cat[:, K_BA:K_BA + 16], dw_cat[:, K_GA:K_GA + 1024], dw_cat[:, K_GB:K_GB + 1024]], axis=1)
    sent = send_early_grads(grads)
    grad_x, acc_in = _ln_in_bwd(dproj, w_cat, da1, x, g_in, tms, sent)

    grads["conv_w"] = acc_conv[0:CONV_K]
    grads["ln_in_g"], grads["ln_in_b"] = acc_in[0], acc_in[1]
    grads["ln1_g"], grads["ln1_b"] = acc_ln1[0], acc_ln1[1]
    grads["ln2_g"], grads["ln2_b"] = acc_tail[0], acc_tail[1]
    grads["pool_scale"] = acc_mix[0]
    grads["o_norm_w"] = acc_mix[1, 0:HEAD_DIM]
    grads["a_log"] = acc_ba[0, HEADS:2 * HEADS]
    grads["dt_bias"] = acc_ba[1, HEADS:2 * HEADS]
    loss = acc_tail[2, 0]
    return grad_x, grads, loss


MESH = pl.DeviceIdType.MESH
ANY = pl.BlockSpec(memory_space=pl.ANY)


def _chip_of(k, x, y):
    chip = (2 * x + y + k) % N_CHIPS
    return chip // 2, chip % 2


def _place():
    x, y, c = lax.axis_index("x"), lax.axis_index("y"), lax.axis_index("c")
    return x, y, c, 2 * x + y


def _half(rows, c):
    return pl.ds(pl.multiple_of(c * (rows // 2), 16), rows // 2)


def _remote(src, dst, send_sem, recv_sem, device_id):
    return pltpu.make_async_remote_copy(src_ref=src, dst_ref=dst, send_sem=send_sem, recv_sem=recv_sem,
                                        device_id=device_id, device_id_type=MESH)


def _tile_rows(rows):
    for tr in (256, 128, 64, 32, 16):
        if rows % tr == 0:
            return tr
    raise ValueError(rows)


def _first_gather_copies(srcs, lands, send, recv, place):
    copies = []
    for a in range(len(srcs)):
        whole = a == len(srcs) - 1
        for k in range(N_CHIPS):
            if place is None:
                copies.append(None)
                continue
            x, y, c, me = place
            sems = (send.at[a * N_CHIPS + k], recv.at[a * N_CHIPS + k])
            if k == 0:
                copies.append(_remote(srcs[a], lands[a].at[me], *sems, (x, y, 1 - c)))
                continue
            tx, ty = _chip_of(k, x, y)
            if whole:
                copies.append(_remote(srcs[a], lands[a].at[me], *sems, (tx, ty, c)))
            else:
                mine = _half(srcs[a].shape[0], c)
                copies.append(_remote(srcs[a].at[mine], lands[a].at[me, mine], *sems, (tx, ty, c)))
    return copies


def _pass_halves(stacks):
    n = len(stacks)

    def body(*refs):
        outs = refs[n:2 * n]
        send, recv = refs[2 * n:]
        x, y, c, me = _place()
        copies = []
        for a in range(n):
            for k in range(1, N_CHIPS):
                landed = outs[a].at[(me + N_CHIPS - k) % N_CHIPS, _half(stacks[a].shape[1], c)]
                copies.append(_remote(landed, landed, send.at[a * N_CHIPS + k], recv.at[a * N_CHIPS + k],
                                      (x, y, 1 - c)))
        for cp in copies:
            cp.start()
        for cp in copies:
            cp.wait_send()
        for a in range(n):
            for k in range(1, N_CHIPS):
                passed = outs[a].at[(me + N_CHIPS - k) % N_CHIPS, _half(stacks[a].shape[1], 1 - c)]
                _remote(passed, passed, send.at[a * N_CHIPS + k], recv.at[a * N_CHIPS + k], (x, y, c)).wait_recv()

    sems = pltpu.SemaphoreType.DMA((n * N_CHIPS,))
    return pl.pallas_call(
        body, name="pass_halves", out_shape=[SDS(s.shape, s.dtype) for s in stacks],
        in_specs=[ANY] * n, out_specs=[ANY] * n, scratch_shapes=[sems, sems],
        input_output_aliases={a: a for a in range(n)},
    )(*stacks)


def _swap_halves(gs):
    n = len(gs)

    def body(*refs):
        ins, theirs = refs[0:n], refs[n:2 * n]
        send, recv = refs[2 * n:]
        x, y, c, _ = _place()
        copies = [_remote(ins[a].at[:, _half(gs[a].shape[1], 1 - c)], theirs[a], send.at[a], recv.at[a],
                          (x, y, 1 - c)) for a in range(n)]
        for cp in copies:
            cp.start()
        for cp in copies:
            cp.wait()

    return pl.pallas_call(
        body, name="swap_halves", out_shape=[SDS((N_CHIPS, g.shape[1] // 2, g.shape[2]), g.dtype) for g in gs],
        in_specs=[ANY] * n, out_specs=[ANY] * n, scratch_shapes=[pltpu.SemaphoreType.DMA((n,))] * 2,
    )(*gs)


def _send_to_sibling(hs):
    n = len(hs)

    def body(*refs):
        ins, outs = refs[0:n], refs[n:2 * n]
        send, recv = refs[2 * n:]
        x, y, c, _ = _place()
        copies = [_remote(ins[a], outs[a], send.at[a], recv.at[a], (x, y, 1 - c)) for a in range(n)]
        for cp in copies:
            cp.start()
        for cp in copies:
            cp.wait()

    return pl.pallas_call(
        body, name="send_to_sibling", out_shape=[SDS(h.shape, h.dtype) for h in hs],
        in_specs=[ANY] * n, out_specs=[ANY] * n, scratch_shapes=[pltpu.SemaphoreType.DMA((n,))] * 2,
    )(*hs)


HBM = pl.BlockSpec(memory_space=pltpu.HBM)
SEM = pl.BlockSpec(memory_space=pltpu.SEMAPHORE)
EFFECT = pltpu.SideEffectType.DATAFLOW_SIDE_EFFECTING


def _in_hbm(a):
    return pltpu.with_memory_space_constraint(a, pltpu.HBM)


def _split_copy_start(name, srcs, lands, copies_of, after):
    n = len(srcs)
    n_copies = len(copies_of(srcs, lands, None, None, None))

    def body(*refs):
        src_refs, land_refs = refs[0:n], refs[n:2 * n]
        send, recv = refs[2 * n + 1], refs[2 * n + 2]
        token = refs[-1]
        for cp in copies_of(src_refs, land_refs, send, recv, _place()):
            cp.start()
        token[...] = jnp.zeros_like(token)

    sems = pltpu.SemaphoreType.DMA((n_copies,))
    out = pl.pallas_call(
        body, name=name,
        out_shape=[sems, sems] + [pltpu.HBM(a.shape, a.dtype) for a in list(srcs) + list(lands)] + [SDS((8, 128), F32)],
        in_specs=[HBM] * (2 * n) + [ANY],
        out_specs=[SEM, SEM] + [HBM] * (2 * n) + [pl.BlockSpec(memory_space=pltpu.VMEM)],
        input_output_aliases={i: 2 + i for i in range(2 * n)},
        compiler_params=pltpu.CompilerParams(has_side_effects=EFFECT),
    )(*[_in_hbm(a) for a in list(srcs) + list(lands)], after)
    return out[0], out[1], out[2:2 + n], out[2 + n:2 + 2 * n], out[-1]


def _split_copy_wait(name, send, recv, srcs, lands, after, copies_of):
    n = len(srcs)

    def body(*refs):
        src_refs, land_refs = refs[0:n], refs[n:2 * n]
        send_ref, recv_ref = refs[2 * n], refs[2 * n + 1]
        for cp in copies_of(src_refs, land_refs, send_ref, recv_ref, _place()):
            cp.wait_send()
            cp.wait_recv()

    out = pl.pallas_call(
        body, name=name, out_shape=[pltpu.HBM(a.shape, a.dtype) for a in list(srcs) + list(lands)],
        in_specs=[HBM] * (2 * n) + [SEM, SEM, ANY], out_specs=[HBM] * (2 * n),
        input_output_aliases={i: i for i in range(2 * n)},
        compiler_params=pltpu.CompilerParams(has_side_effects=EFFECT),
    )(*srcs, *lands, send, recv, after)
    return out[0:n], out[n:2 * n]


def _late_gather_copies(srcs, lands, send, recv, place):
    copies = []
    for a in range(len(srcs)):
        for k in range(N_CHIPS):
            if place is None:
                copies.append(None)
                continue
            x, y, c, me = place
            if k == 0:
                target = (x, y, 1 - c)
            else:
                tx, ty = _chip_of(k, x, y)
                target = (tx, ty, c)
            copies.append(_remote(srcs[a], lands[a].at[me], send.at[a * N_CHIPS + k], recv.at[a * N_CHIPS + k], target))
    return copies


def _late_scatter_copies(srcs, lands, send, recv, place):
    copies = []
    for a in range(len(srcs)):
        for k in range(1, N_CHIPS):
            if place is None:
                copies.append(None)
                continue
            x, y, c, _ = place
            tx, ty = _chip_of(k, x, y)
            copies.append(_remote(srcs[a].at[2 * tx + ty], lands[a].at[k - 1], send.at[a * (N_CHIPS - 1) + k - 1],
                                  recv.at[a * (N_CHIPS - 1) + k - 1], (tx, ty, c)))
    return copies


def _add_pair(g, theirs, name):
    _, rows, cols = g.shape
    half = rows // 2
    tr = _tile_rows(half)

    def body(g_ref, t_ref, o_ref):
        own = g_ref[lax.axis_index("c")]
        o_ref[...] = (own.astype(F32) + t_ref[...].astype(F32)).astype(o_ref.dtype)

    blk = pl.BlockSpec((None, tr, cols), lambda j, i: (j, i, 0))
    return _pc(body, "add_" + name, (N_CHIPS, half // tr),
               [pl.BlockSpec((None, 2, tr, cols), lambda j, i: (j, 0, i, 0)), blk], blk,
               SDS((N_CHIPS, half, cols), g.dtype), sem=("parallel", "parallel"))(
                   g.reshape(N_CHIPS, 2, half, cols), theirs)


def _sum_slabs(pair, landed, name):
    _, rows, cols = pair.shape
    tr = _tile_rows(rows)

    def body(p_ref, r_ref, o_ref):
        acc = p_ref[2 * lax.axis_index("x") + lax.axis_index("y")].astype(F32)
        for k in range(N_CHIPS - 1):
            acc = acc + r_ref[k].astype(F32)
        o_ref[...] = acc

    return _pc(body, "sum_" + name, (rows // tr,),
               [pl.BlockSpec((N_CHIPS, tr, cols), lambda i: (0, i, 0)),
                pl.BlockSpec((N_CHIPS - 1, tr, cols), lambda i: (0, i, 0))],
               _row(tr, cols), SDS((rows, cols), F32), sem=("parallel",))(pair, landed)


def _adamw_math(w, g, m, v):
    m = ADAM_B1 * m + (1.0 - ADAM_B1) * g
    v = ADAM_B2 * v + (1.0 - ADAM_B2) * (g * g)
    m_hat = m / (1.0 - ADAM_B1 ** ADAM_STEP)
    v_hat = v / (1.0 - ADAM_B2 ** ADAM_STEP)
    delta = -ADAM_LR * (m_hat / (jnp.sqrt(v_hat) + ADAM_EPS) + ADAM_WD * w)
    return delta, m, v


def _adamw_2d(w, g_own, g_sib, m, v, name, halves):
    lead = w.ndim == 3
    rows, cols = w.shape[-2:]
    tr = _tile_rows(rows // 2)
    nh = rows // 2 // tr if halves else rows // tr

    def body(w_ref, go_ref, gs_ref, m_ref, v_ref, g_out, d_out, m_out, v_out):
        if halves:
            mine = (pl.program_id(0) // nh) == lax.axis_index("c")
            g = jnp.where(mine, go_ref[...], gs_ref[...])
        else:
            g = go_ref[...] + gs_ref[...]
        delta, mn, vn = _adamw_math(w_ref[...], g, m_ref[...], v_ref[...])
        g_out[...] = g
        d_out[...] = delta
        m_out[...] = mn
        v_out[...] = vn

    r = _row(tr, cols)
    p = pl.BlockSpec((None, tr, cols), lambda i: (0, i, 0)) if lead else r
    h = pl.BlockSpec((tr, cols), lambda i: (i % nh, 0))
    return _pc(body, "adamw_" + name, (rows // tr,), [p, h, h, p, p], [r] * 4, [SDS((rows, cols), F32)] * 4,
               sem=("parallel",))(w, g_own, g_sib, m, v)


def _small_allreduce_adamw(mine, w, m, v):
    shape = mine.shape

    def body(mine_ref, w_ref, m_ref, v_ref, g_out, d_out, m_out, v_out, buf_ref, send_sems, recv_sems):
        x, y, c = lax.axis_index("x"), lax.axis_index("y"), lax.axis_index("c")
        me = 4 * x + 2 * y + c
        buf_ref[me] = mine_ref[...]
        copies = []
        for k in range(1, N_DEV):
            tgt = (me + k) % N_DEV
            copies.append(pltpu.make_async_remote_copy(
                src_ref=mine_ref, dst_ref=buf_ref.at[me], send_sem=send_sems.at[k], recv_sem=recv_sems.at[k],
                device_id=(tgt // 4, (tgt // 2) % 2, tgt % 2), device_id_type=MESH))
        for cp in copies:
            cp.start()
        for k in range(1, N_DEV):
            src = (me + N_DEV - k) % N_DEV
            pltpu.make_async_remote_copy(
                src_ref=mine_ref, dst_ref=buf_ref.at[src], send_sem=send_sems.at[k], recv_sem=recv_sems.at[k],
                device_id=(x, y, c), device_id_type=MESH).wait_recv()
        for cp in copies:
            cp.wait_send()
        g = buf_ref[0]
        for j in range(1, N_DEV):
            g = g + buf_ref[j]
        delta, mn, vn = _adamw_math(w_ref[...], g, m_ref[...], v_ref[...])
        g_out[...] = g
        d_out[...] = delta
        m_out[...] = mn
        v_out[...] = vn

    vm = pl.BlockSpec(memory_space=pltpu.VMEM)
    return pl.pallas_call(
        body, name="small_allreduce_adamw", out_shape=[SDS(shape, F32)] * 4, in_specs=[vm] * 4, out_specs=[vm] * 4,
        scratch_shapes=[pltpu.VMEM((N_DEV,) + shape, F32), pltpu.SemaphoreType.DMA((N_DEV,)),
                        pltpu.SemaphoreType.DMA((N_DEV,))],
    )(mine, w, m, v)


def _as2d(a):
    return a.reshape(-1, a.shape[-1])


def _w_cat(stack):
    wi = stack.transpose(1, 0, 2).reshape(D_MODEL, IN_WIDTH)
    return jnp.concatenate(
        [wi[:, C_QKV:C_Z], wi[:, C_Z:C_BETA], wi[:, C_GA:C_GB], wi[:, C_GB:IN_WIDTH], wi[:, C_POOL:C_QKV],
         wi[:, C_BETA:C_GA], jnp.zeros((D_MODEL, CAT_WIDTH - K_BA - 2 * HEADS), wi.dtype)], axis=1)


WEIGHT_LAYOUT = {
    "w_in": lambda s: ("w_cat", _w_cat(s)),
    "pool_w": lambda s: ("pool_w", s.reshape(N_CHIPS, 4, POOL_GROUP, POOL_OUT_GROUP // N_CHIPS)
                         .transpose(1, 2, 0, 3).reshape(4, POOL_GROUP, POOL_OUT_GROUP)),
    "w_out": lambda s: ("w_out", s.reshape(D_MODEL, D_MODEL)),
    "w_up": lambda s: ("w_up", s),
    "w_down": lambda s: ("w_down", s.reshape(D_FF, D_MODEL)),
    "ple_gate_w": lambda s: ("ple_gate_w", s.reshape(D_MODEL, D_MODEL)),
    "ple_proj_w": lambda s: ("ple_proj_w", s.transpose(1, 0, 2).reshape(PLE_DIM, D_MODEL)),
}

GRAD_LAYOUT = {
    "w_in": lambda g: g.reshape(D_MODEL, N_CHIPS, IN_WIDTH // N_CHIPS).transpose(1, 0, 2),
    "pool_w": lambda g: g.reshape(4, POOL_GROUP, N_CHIPS, POOL_OUT_GROUP // N_CHIPS)
                         .transpose(2, 0, 1, 3).reshape(N_CHIPS, 4 * POOL_GROUP, POOL_OUT_GROUP // N_CHIPS),
    "w_out": lambda g: g.reshape(N_CHIPS, D_MODEL // N_CHIPS, D_MODEL),
    "w_up": lambda g: g,
    "w_down": lambda g: g.reshape(N_CHIPS, D_FF // N_CHIPS, D_MODEL),
    "ple_gate_w": lambda g: g.reshape(N_CHIPS, D_MODEL // N_CHIPS, D_MODEL),
    "ple_proj_w": lambda g: g.reshape(PLE_DIM, N_CHIPS, D_MODEL // N_CHIPS).transpose(1, 0, 2),
}


def _full_weights(names, stacks):
    return dict(WEIGHT_LAYOUT[n](s.astype(MXU_DTYPE)) for n, s in zip(names, stacks))


def _grads_by_chip(names, grads):
    return [GRAD_LAYOUT[n](grads[n]).astype(WIRE_DTYPE) for n in names]


def _pack_small(rows, conv, name):
    n = len(rows)

    def body(*refs):
        out = refs[n + 1]
        out[...] = jnp.zeros_like(out)
        for i in range(n):
            out[i:i + 1, :] = refs[i][...]
        out[SMALL_CONV_AT:SMALL_CONV_AT + SMALL_CONV_ROWS, :] = refs[n][...]

    vm = pl.BlockSpec(memory_space=pltpu.VMEM)
    return pl.pallas_call(body, name=name, out_shape=SDS((SMALL_CONV_AT + SMALL_CONV_ROWS, D_MODEL), F32),
                          in_specs=[vm] * (n + 1), out_specs=vm)(*rows, conv)


def _pad_row(a):
    a = a.reshape(1, -1).astype(F32)
    return jnp.pad(a, ((0, 0), (0, D_MODEL - a.shape[1])))


def kernel(x, p, ln_in_g, ln_in_b, w_in, pool_w, pool_scale, conv_w, a_log, dt_bias, o_norm_w, w_out, ln1_g, ln1_b, w_up, w_down, ple_gate_w, ple_proj_w, ln2_g, ln2_b, loss_target, m_ln_in_g, m_ln_in_b, m_w_in, m_pool_w, m_pool_scale, m_conv_w, m_a_log, m_dt_bias, m_o_norm_w, m_w_out, m_ln1_g, m_ln1_b, m_w_up, m_w_down, m_ple_gate_w, m_ple_proj_w, m_ln2_g, m_ln2_b, v_ln_in_g, v_ln_in_b, v_w_in, v_pool_w, v_pool_scale, v_conv_w, v_a_log, v_dt_bias, v_o_norm_w, v_w_out, v_ln1_g, v_ln1_b, v_w_up, v_w_down, v_ple_gate_w, v_ple_proj_w, v_ln2_g, v_ln2_b):
    given = dict(locals())
    chip = 2 * lax.axis_index("x") + lax.axis_index("y")

    shard = lambda n: _as2d(given[n]).astype(WIRE_DTYPE)

    wts = {"ln_in_g": ln_in_g, "ln_in_b": ln_in_b, "pool_scale": pool_scale[0], "a_log": a_log[0],
           "dt_bias": dt_bias[0], "o_norm_w": o_norm_w[0], "ln1_g": ln1_g[0], "ln1_b": ln1_b[0],
           "ln2_g": ln2_g[0], "ln2_b": ln2_b[0]}

    conv_pad = jnp.pad(conv_w[0], ((0, 8 - CONV_K), (0, 0)))
    first_srcs = [shard(n) for n in EARLY] + [conv_pad]
    first_lands = [lax.empty((N_CHIPS,) + s.shape, s.dtype) for s in first_srcs]
    fsend, frecv, fsrcs, flands, start_token = _split_copy_start(
        "first_gather_start", first_srcs, first_lands, _first_gather_copies, first_srcs[0])
    late = {}

    def first_weights(after):
        _, lands = _split_copy_wait("first_gather_wait", fsend, frecv, fsrcs, flands, after, _first_gather_copies)
        stacks = _pass_halves(lands[0:len(EARLY)])
        first = _full_weights(EARLY, stacks)
        first["conv_w"] = jnp.concatenate([lands[len(EARLY)][j, 0:CONV_K] for j in range(N_CHIPS)], axis=1)
        late_srcs = [shard(n) for n in LATE]
        late_lands = [lax.empty((N_CHIPS,) + s.shape, s.dtype) for s in late_srcs]
        late["send"], late["recv"], late["srcs"], late["lands"], token = _split_copy_start(
            "late_gather_start", late_srcs, late_lands, _late_gather_copies, stacks[0])
        return first, token

    def late_weights(after):
        _, stacks = _split_copy_wait("late_gather_wait", late["send"], late["recv"], late["srcs"], late["lands"],
                                     after, _late_gather_copies)
        return _full_weights(LATE, stacks)

    scatter = {}

    def send_late_grads(grads):
        srcs = _grads_by_chip(LATE, grads)
        lands = [lax.empty((N_CHIPS - 1,) + g.shape[1:], g.dtype) for g in srcs]
        scatter["send"], scatter["recv"], scatter["srcs"], scatter["lands"], token = _split_copy_start(
            "late_scatter_start", srcs, lands, _late_scatter_copies, srcs[0])
        return token

    last = {}

    def send_early_grads(grads):
        by_chip = _grads_by_chip(EARLY, grads)
        theirs = _swap_halves(by_chip)
        pair = [_add_pair(g, t, n) for g, t, n in zip(by_chip, theirs, EARLY)]
        lands = [lax.empty((N_CHIPS - 1,) + q.shape[1:], q.dtype) for q in pair]
        last["send"], last["recv"], last["srcs"], last["lands"], token = _split_copy_start(
            "early_scatter_start", pair, lands, _late_scatter_copies, pair[0])
        return token

    grad_x, grads, loss = _local_step(x[0], p[0, 0], loss_target[0], wts, start_token, first_weights, late_weights,
                                      send_late_grads, send_early_grads)

    late_mine, late_landed = _split_copy_wait("late_scatter_wait", scatter["send"], scatter["recv"], scatter["srcs"],
                                              scatter["lands"], grad_x, _late_scatter_copies)
    late_part = [_sum_slabs(q, r, n) for q, r, n in zip(late_mine, late_landed, LATE)]
    pair, landed = _split_copy_wait("early_scatter_wait", last["send"], last["recv"], last["srcs"], last["lands"],
                                    grad_x, _late_scatter_copies)
    reduced = [_sum_slabs(q, r, n) for q, r, n in zip(pair, landed, EARLY)]
    from_sibling = _send_to_sibling(reduced + late_part)
    big_out = {}
    for n, g_own, g_sib in zip(EARLY + LATE, reduced + late_part, from_sibling):
        view = (lambda a: a) if given[n].ndim == 3 else _as2d
        res = _adamw_2d(view(given[n]), g_own, g_sib, view(given["m_" + n]), view(given["v_" + n]), n,
                        halves=n in EARLY)
        big_out[n] = [r.reshape(given[n].shape) for r in res]

    conv_cols = QKV_WIDTH // N_CHIPS

    def small_pack(get, conv, extra, name):
        if conv.shape[1] != QKV_WIDTH:
            conv = lax.dynamic_update_slice(jnp.zeros((CONV_K, QKV_WIDTH), F32), conv, (0, chip * conv_cols))
        return _pack_small([_pad_row(get(n)) for n in SMALL_NAMES] + extra, conv.reshape(SMALL_CONV_ROWS, D_MODEL), name)

    mine_small = small_pack(lambda n: grads[n], grads["conv_w"], [jnp.full((1, D_MODEL), loss, F32)], "pack_small_g")
    packed_small = [small_pack(lambda n: given[prefix + n], given[prefix + "conv_w"][0], [], "pack_small_" + tag)
                    for prefix, tag in (("", "w"), ("m_", "m"), ("v_", "v"))]
    small_out = _small_allreduce_adamw(mine_small, *packed_small)

    def small_get(k, n):
        if n == "conv_w":
            full = small_out[k][SMALL_CONV_AT:SMALL_CONV_AT + SMALL_CONV_ROWS].reshape(CONV_K, QKV_WIDTH)
            return lax.dynamic_slice(full, (0, chip * conv_cols), (CONV_K, conv_cols)).reshape(given[n].shape)
        i = SMALL_NAMES.index(n)
        return small_out[k][i, 0:given[n].size].reshape(given[n].shape)

    order = ["ln_in_g", "ln_in_b", "w_in", "pool_w", "pool_scale", "conv_w", "a_log", "dt_bias", "o_norm_w", "w_out",
             "ln1_g", "ln1_b", "w_up", "w_down", "ple_gate_w", "ple_proj_w", "ln2_g", "ln2_b"]
    outs = [small_out[0][len(SMALL_NAMES), 0], grad_x[None]]
    for k in range(4):
        for n in order:
            outs.append(big_out[n][k] if n in big_out else small_get(k, n))
    return tuple(outs)
```

```python
import jax
import jax.numpy as jnp
from jax import lax
from jax.experimental import pallas as pl
from jax.experimental.pallas import tpu as pltpu

F32 = jnp.float32
MXU_DTYPE = jnp.bfloat16
WIRE_DTYPE = jnp.bfloat16
SDS = jax.ShapeDtypeStruct

D_MODEL = 1024
POOL_WINDOWS = (2, 4, 8, 16)
POOL_WIDTH = 512
POOL_GROUP = 128
POOL_OUT_GROUP = 256
HEADS = 8
HEAD_DIM = 128
DN_WIDTH = HEADS * HEAD_DIM
QKV_WIDTH = 3 * DN_WIDTH
CONV_K = 4
CHUNK = 128
DW_TK = 1024
DW_TM = 1024
D_FF = 4096
PLE_DIM = 256
LN_EPS = 1e-5
RMS_EPS = 1e-6
L2_EPS = 1e-6
ALPHA = 2.0 ** 0.25
Q_SCALE = HEAD_DIM ** -0.5
IN_WIDTH = 6672
C_POOL, C_QKV, C_Z, C_BETA, C_A, C_GA, C_GB = 0, 512, 3584, 4608, 4616, 4624, 5648
K_QKV, K_Z, K_GA, K_GB, K_U, K_BA, CAT_WIDTH = 0, 3072, 4096, 5120, 6144, 6656, 6912

ADAM_LR, ADAM_B1, ADAM_B2, ADAM_EPS, ADAM_WD, ADAM_STEP = 0.001, 0.9, 0.999, 1e-08, 0.01, 10

N_CHIPS = 4
N_DEV = 8
VMEM_LIMIT = 56 * 1024 * 1024

EARLY = ("w_in", "pool_w")
LATE = ("w_out", "w_up", "w_down", "ple_gate_w", "ple_proj_w")
SMALL_NAMES = ("ln_in_g", "ln_in_b", "pool_scale", "ln1_g", "ln1_b", "ln2_g", "ln2_b", "o_norm_w", "a_log", "dt_bias")
SMALL_CONV_AT = 12
SMALL_CONV_ROWS = CONV_K * QKV_WIDTH // D_MODEL


def _mx(a):
    return a.astype(MXU_DTYPE)


def _dot(a, b):
    return lax.dot_general(_mx(a), _mx(b), (((1,), (0,)), ((), ())), preferred_element_type=F32)


def _dot_nt(a, b):
    return lax.dot_general(_mx(a), _mx(b), (((1,), (1,)), ((), ())), preferred_element_type=F32)


def _dot_tn(a, b):
    return lax.dot_general(_mx(a), _mx(b), (((0,), (0,)), ((), ())), preferred_element_type=F32)


def _sigmoid(x):
    return 0.5 * jnp.tanh(0.5 * x) + 0.5


def _softplus(x):
    return jnp.maximum(x, 0.0) + jnp.log(1.0 + jnp.exp(-jnp.abs(x)))


def _pc(body, name, grid, in_specs, out_specs, out_shape, scratch=(), sem=None, aliases=None):
    return pl.pallas_call(
        body, out_shape=out_shape, grid=grid, in_specs=in_specs, out_specs=out_specs,
        scratch_shapes=scratch, name=name, input_output_aliases=aliases or {},
        compiler_params=pltpu.CompilerParams(dimension_semantics=sem, vmem_limit_bytes=VMEM_LIMIT))


def _row(tm, n):
    return pl.BlockSpec((tm, n), lambda i: (i, 0))


def _const(shape):
    nd = len(shape)
    return pl.BlockSpec(shape, lambda *_: (0,) * nd)


def _matmul(a, b, mode, name, out_dtype=F32, tm=512, tn=512, tk=512, stack_out=False):
    if mode == "nn":
        (m, k), n = a.shape, b.shape[1]
    elif mode == "nt":
        (m, k), n = a.shape, b.shape[0]
    else:
        (k, m), n = a.shape, b.shape[1]
    tm, tn, tk = min(tm, m), min(tn, n), min(tk, k)
    assert m % tm == 0 and n % tn == 0 and k % tk == 0, (name, m, n, k, tm, tn, tk)
    nk = k // tk
    if mode == "nn":
        a_spec = pl.BlockSpec((tm, tk), lambda i, j, kk: (i, kk))
        b_spec = pl.BlockSpec((tk, tn), lambda i, j, kk: (kk, j))
        dot = _dot
    elif mode == "nt":
        a_spec = pl.BlockSpec((tm, tk), lambda i, j, kk: (i, kk))
        b_spec = pl.BlockSpec((tn, tk), lambda i, j, kk: (j, kk))
        dot = _dot_nt
    else:
        a_spec = pl.BlockSpec((tk, tm), lambda i, j, kk: (kk, i))
        b_spec = pl.BlockSpec((tk, tn), lambda i, j, kk: (kk, j))
        dot = _dot_tn

    def body(a_ref, b_ref, o_ref, *acc):
        if nk == 1:
            o_ref[...] = dot(a_ref[...], b_ref[...]).astype(out_dtype)
            return
        acc_ref, kk = acc[0], pl.program_id(2)

        @pl.when(kk == 0)
        def _():
            acc_ref[...] = dot(a_ref[...], b_ref[...])

        @pl.when((kk > 0) & (kk < nk - 1))
        def _():
            acc_ref[...] += dot(a_ref[...], b_ref[...])

        @pl.when(kk == nk - 1)
        def _():
            o_ref[...] = (acc_ref[...] + dot(a_ref[...], b_ref[...])).astype(out_dtype)

    if stack_out:
        o_spec, o_shape = pl.BlockSpec((None, tm, tn), lambda i, j, kk: (j, i, 0)), SDS((n // tn, m, tn), out_dtype)
    else:
        o_spec, o_shape = pl.BlockSpec((tm, tn), lambda i, j, kk: (i, j)), SDS((m, n), out_dtype)
    return _pc(body, name, (m // tm, n // tn, nk), [a_spec, b_spec], o_spec, o_shape,
               scratch=[pltpu.VMEM((tm, tn), F32)] if nk > 1 else [],
               sem=("parallel", "parallel", "arbitrary"))(a, b)


PROJ_TN = 768


def _proj_conv(h0_bf, w_cat, conv_w, tm, after):
    t = h0_bf.shape[0]
    n_qkv = QKV_WIDTH // PROJ_TN

    def body(h_ref, w_ref, cw_ref, after_ref, o_ref, act_ref, ds_ref, carry_ref, ext_ref):
        @pl.when(pl.program_id(0) == 0)
        def _():
            carry_ref[...] = jnp.zeros_like(carry_ref)

        h = h_ref[...]

        def project(cb):
            cols = slice(cb * PROJ_TN, (cb + 1) * PROJ_TN)
            o_ref[:, cols] = _dot(h, w_ref[:, cols])

        def conv(cb):
            cols = slice(cb * PROJ_TN, (cb + 1) * PROJ_TN)
            ext_ref[cb, 0:8, :] = carry_ref[:, cols]
            ext_ref[cb, 8:8 + tm, :] = o_ref[:, cols]
            carry_ref[:, cols] = o_ref[tm - 8:tm, cols]
            w = [cw_ref[pl.ds(k, 1), cols] for k in range(CONV_K)]
            for r in range(0, tm, CONV_ROWS):
                y = _conv_rows(ext_ref.at[cb], w, r, CONV_ROWS)
                s = _sigmoid(y)
                act_ref[pl.ds(r, CONV_ROWS), cols] = _mx(y * s)
                ds_ref[pl.ds(r, CONV_ROWS), cols] = _mx(s * (1.0 + y * (1.0 - s)))

        project(0)
        for cb in range(1, CAT_WIDTH // PROJ_TN):
            project(cb)
            if cb - 1 < n_qkv:
                conv(cb - 1)

    return _pc(body, "proj_conv", (t // tm,),
               [_row(tm, D_MODEL), _const((D_MODEL, CAT_WIDTH)), _const((CONV_K, QKV_WIDTH)), ANY],
               [_row(tm, CAT_WIDTH), _row(tm, QKV_WIDTH), _row(tm, QKV_WIDTH)],
               [SDS((t, CAT_WIDTH), F32), SDS((t, QKV_WIDTH), MXU_DTYPE), SDS((t, QKV_WIDTH), MXU_DTYPE)],
               scratch=[pltpu.VMEM((8, QKV_WIDTH), F32), pltpu.VMEM((n_qkv, 8 + tm, PROJ_TN), F32)],
               sem=("arbitrary",))(h0_bf, w_cat, conv_w, after)


def _ln_stats(x):
    mu = jnp.mean(x, axis=-1, keepdims=True)
    xc = x - mu
    var = jnp.mean(xc * xc, axis=-1, keepdims=True)
    rstd = lax.rsqrt(var + LN_EPS)
    return xc * rstd, rstd


def _ln_bwd(dy, xhat, rstd, g):
    dxh = dy * g
    m1 = jnp.mean(dxh, axis=-1, keepdims=True)
    m2 = jnp.mean(dxh * xhat, axis=-1, keepdims=True)
    return rstd * (dxh - m1 - xhat * m2)


def _ln_in(x, g, b, tm, after):
    t, d = x.shape

    def body(x_ref, g_ref, b_ref, after_ref, h_ref, hb_ref):
        xhat, _ = _ln_stats(x_ref[...])
        h = xhat * g_ref[...] + b_ref[...]
        h_ref[...] = h
        hb_ref[...] = _mx(h)

    return _pc(body, "ln_in", (t // tm,), [_row(tm, d), _const((1, d)), _const((1, d)), ANY],
               [_row(tm, d), _row(tm, d)], [SDS((t, d), F32), SDS((t, d), MXU_DTYPE)],
               sem=("parallel",))(x, g, b, after)


def _pool_fwd(proj, pool_w, tm):
    t = proj.shape[0]
    ublk = K_U // POOL_WIDTH

    def body(u_ref, halo_ref, pw_ref, ypre_ref, d_ref, ext_ref):
        i = pl.program_id(0)
        ext_ref[0:16, :] = jnp.where(i > 0, halo_ref[...], 0.0)
        ext_ref[16:16 + tm, :] = u_ref[...]
        tok = i * tm + lax.broadcasted_iota(jnp.int32, (tm, POOL_GROUP), 0)
        for gi, w in enumerate(POOL_WINDOWS):
            cs = pl.ds(gi * POOL_GROUP, POOL_GROUP)
            ug = ext_ref[pl.ds(16, tm), cs]
            s = ug
            for k in range(1, w):
                s = s + ext_ref[pl.ds(16 - k, tm), cs]
            cnt = jnp.minimum(tok + 1, w).astype(F32)
            db = _mx(s / cnt - ug)
            d_ref[:, gi * POOL_GROUP:(gi + 1) * POOL_GROUP] = db
            ypre_ref[:, gi * POOL_OUT_GROUP:(gi + 1) * POOL_OUT_GROUP] = _dot(db, pw_ref[gi])

    halo = pl.BlockSpec((16, POOL_WIDTH), lambda i: (jnp.maximum(i * (tm // 16) - 1, 0), ublk))
    return _pc(body, "pool_fwd", (t // tm,),
               [pl.BlockSpec((tm, POOL_WIDTH), lambda i: (i, ublk)), halo, _const((4, POOL_GROUP, POOL_OUT_GROUP))],
               [_row(tm, D_MODEL), _row(tm, POOL_WIDTH)],
               [SDS((t, D_MODEL), F32), SDS((t, POOL_WIDTH), MXU_DTYPE)],
               scratch=[pltpu.VMEM((16 + tm, POOL_WIDTH), F32)], sem=("parallel",))(proj, proj, pool_w)


def _pool_bwd(dyp, d_bf, pool_w, dproj, tm):
    t = dyp.shape[0]
    n = t // tm

    def body(dy_ref, dyn_ref, d_ref, pw_ref, dproj_ref, du_ref, dpw_ref, ext_ref):
        i = pl.program_id(0)

        @pl.when(i == 0)
        def _():
            dpw_ref[...] = jnp.zeros_like(dpw_ref)

        tok = i * tm + lax.broadcasted_iota(jnp.int32, (tm + 16, POOL_GROUP), 0)
        for gi, w in enumerate(POOL_WINDOWS):
            dy = dy_ref[:, gi * POOL_OUT_GROUP:(gi + 1) * POOL_OUT_GROUP]
            dyn = dyn_ref[:, gi * POOL_OUT_GROUP:(gi + 1) * POOL_OUT_GROUP]
            pw = pw_ref[gi]
            dd = _dot_nt(dy, pw)
            ddn = jnp.where(i < n - 1, _dot_nt(dyn, pw), 0.0)
            cnt = jnp.minimum(tok + 1, w).astype(F32)
            ext_ref[0:tm, :] = dd / cnt[0:tm]
            ext_ref[tm:tm + 16, :] = ddn / cnt[tm:tm + 16]
            s = ext_ref[pl.ds(0, tm), :]
            for k in range(1, w):
                s = s + ext_ref[pl.ds(k, tm), :]
            du_ref[:, gi * POOL_GROUP:(gi + 1) * POOL_GROUP] = _mx(s - dd)
            dpw_ref[gi] += _dot_tn(d_ref[:, gi * POOL_GROUP:(gi + 1) * POOL_GROUP], dy)

    nxt = pl.BlockSpec((16, D_MODEL), lambda i: (jnp.minimum((i + 1) * (tm // 16), t // 16 - 1), 0))
    return _pc(body, "pool_bwd", (n,),
               [_row(tm, D_MODEL), nxt, _row(tm, POOL_WIDTH), _const((4, POOL_GROUP, POOL_OUT_GROUP)), ANY],
               [pl.BlockSpec((tm, POOL_WIDTH), lambda i: (i, K_U // POOL_WIDTH)),
                _const((4, POOL_GROUP, POOL_OUT_GROUP))],
               [SDS(dproj.shape, dproj.dtype), SDS((4, POOL_GROUP, POOL_OUT_GROUP), F32)],
               scratch=[pltpu.VMEM((tm + 16, POOL_GROUP), F32)], sem=("arbitrary",),
               aliases={4: 0})(dyp, dyp, d_bf, pool_w, dproj)


CONV_BLK = 512


CONV_ROWS = 32


def _conv_rows(ext_ref, w, r, rows):
    y = w[0] * ext_ref[pl.ds(r + 5, rows), :]
    for k in range(1, CONV_K):
        y = y + w[k] * ext_ref[pl.ds(r + 5 + k, rows), :]
    return y


def _conv_bwd(dact, dsilu, proj, conv_w, dproj, tm):
    t = proj.shape[0]
    n = t // tm

    def body(da_ref, dan_ref, ds_ref, dsn_ref, x_ref, xp_ref, w_ref, dproj_ref, dx_ref, dw_ref, ext_ref, dy_ref):
        i = pl.program_id(1)

        @pl.when(i == 0)
        def _():
            dw_ref[...] = jnp.zeros_like(dw_ref)

        ext_ref[0:8, :] = jnp.where(i > 0, xp_ref[...], 0.0)
        ext_ref[8:8 + tm, :] = x_ref[...]
        w = [w_ref[pl.ds(k, 1), :] for k in range(CONV_K)]

        acc = [jnp.zeros((8, CONV_BLK), F32) for _ in range(CONV_K)]
        for r in range(0, tm, CONV_ROWS):
            dy = da_ref[pl.ds(r, CONV_ROWS), :].astype(F32) * ds_ref[pl.ds(r, CONV_ROWS), :].astype(F32)
            dy_ref[pl.ds(r, CONV_ROWS), :] = dy
            for k in range(CONV_K):
                prod = dy * ext_ref[pl.ds(r + 5 + k, CONV_ROWS), :]
                for q in range(0, CONV_ROWS, 8):
                    acc[k] = acc[k] + prod[q:q + 8]
        dy_ref[tm:tm + 8, :] = jnp.where(i < n - 1, dan_ref[0:8, :].astype(F32) * dsn_ref[0:8, :].astype(F32), 0.0)
        for k in range(CONV_K):
            dw_ref[pl.ds(k, 1), :] += jnp.sum(acc[k], axis=0, keepdims=True)
        for r in range(0, tm, CONV_ROWS):
            dx = w[0] * dy_ref[pl.ds(r + 3, CONV_ROWS), :]
            for k in range(1, CONV_K):
                dx = dx + w[k] * dy_ref[pl.ds(r + 3 - k, CONV_ROWS), :]
            dx_ref[pl.ds(r, CONV_ROWS), :] = _mx(dx)

    blk = pl.BlockSpec((tm, CONV_BLK), lambda j, i: (i, j))
    prev = pl.BlockSpec((8, CONV_BLK), lambda j, i: (jnp.maximum(i * (tm // 8) - 1, 0), j))
    nxt = pl.BlockSpec((8, CONV_BLK), lambda j, i: (jnp.minimum((i + 1) * (tm // 8), t // 8 - 1), j))
    nxt16 = pl.BlockSpec((16, CONV_BLK), lambda j, i: (jnp.minimum((i + 1) * (tm // 16), t // 16 - 1), j))
    wspec = pl.BlockSpec((CONV_K, CONV_BLK), lambda j, i: (0, j))
    return _pc(body, "conv_bwd", (QKV_WIDTH // CONV_BLK, n),
               [blk, nxt16, blk, nxt16, blk, prev, wspec, ANY],
               [blk, pl.BlockSpec((8, CONV_BLK), lambda j, i: (0, j))],
               [SDS(dproj.shape, dproj.dtype), SDS((8, QKV_WIDTH), F32)],
               scratch=[pltpu.VMEM((8 + tm, CONV_BLK), F32), pltpu.VMEM((8 + tm, CONV_BLK), F32)],
               sem=("parallel", "arbitrary"), aliases={7: 0})(dact, dact, dsilu, dsilu, proj, proj, conv_w, dproj)


def _lane(shape):
    return lax.broadcasted_iota(jnp.int32, shape, 1)


def _ba_fwd(proj, al_row, dtb_row, tm):
    t = proj.shape[0]
    bablk = K_BA // 128

    def body(ba_ref, al_ref, dtb_ref, bg_ref):
        ba = ba_ref[...]
        lane = _lane(ba.shape)
        g = -jnp.exp(al_ref[...]) * _softplus(ba + dtb_ref[...])
        bg_ref[...] = jnp.where(lane < HEADS, _sigmoid(ba), jnp.where(lane < 2 * HEADS, g, 0.0))

    return _pc(body, "ba_fwd", (t // tm,),
               [pl.BlockSpec((tm, 128), lambda i: (i, bablk)), _const((1, 128)), _const((1, 128))],
               _row(tm, 128), SDS((t, 128), F32), sem=("parallel",))(proj, al_row, dtb_row)


def _ba_bwd(dbg, bg, proj, al_row, dtb_row, dproj, tm):
    t = proj.shape[0]
    bablk = K_BA // 128

    def body(dbg_ref, bg_ref, ba_ref, al_ref, dtb_ref, dproj_ref, dba_ref, acc_ref):
        i = pl.program_id(0)

        @pl.when(i == 0)
        def _():
            acc_ref[...] = jnp.zeros_like(acc_ref)

        dbg_v, bg_v, ba = dbg_ref[...], bg_ref[...], ba_ref[...]
        lane = _lane(ba.shape)
        is_g = (lane >= HEADS) & (lane < 2 * HEADS)
        dbeta_raw = dbg_v * bg_v * (1.0 - bg_v)
        da_raw = dbg_v * (-jnp.exp(al_ref[...])) * _sigmoid(ba + dtb_ref[...])
        dba_ref[:, 0:128] = _mx(jnp.where(lane < HEADS, dbeta_raw, jnp.where(is_g, da_raw, 0.0)))
        dba_ref[:, 128:CAT_WIDTH - K_BA] = jnp.zeros((tm, CAT_WIDTH - K_BA - 128), dba_ref.dtype)
        acc_ref[0:1, :] += jnp.sum(jnp.where(is_g, dbg_v * bg_v, 0.0), axis=0, keepdims=True)
        acc_ref[1:2, :] += jnp.sum(jnp.where(is_g, da_raw, 0.0), axis=0, keepdims=True)

    tail = CAT_WIDTH - K_BA
    return _pc(body, "ba_bwd", (t // tm,),
               [_row(tm, 128), _row(tm, 128), pl.BlockSpec((tm, 128), lambda i: (i, bablk)),
                _const((1, 128)), _const((1, 128)), ANY],
               [pl.BlockSpec((tm, tail), lambda i: (i, K_BA // tail)), _const((8, 128))],
               [SDS(dproj.shape, dproj.dtype), SDS((8, 128), F32)],
               sem=("arbitrary",), aliases={5: 0})(dbg, bg, proj, al_row, dtb_row, dproj)


def _each(f, *lists):
    return [f(*a) for a in zip(*lists)]


def _rowsum(a):
    return jnp.sum(a, axis=1, keepdims=True)


def _chunk_terms(qs, ks, bgv, g_rows, hs):
    c = CHUNK
    ii = lax.broadcasted_iota(jnp.int32, (c, c), 0)
    jj = lax.broadcasted_iota(jnp.int32, (c, c), 1)
    lane = _lane(bgv.shape)
    incl = ii >= jj
    beta = [_rowsum(jnp.where(lane == h, bgv, 0.0)) for h in hs]
    g_col = [_rowsum(jnp.where(lane == HEADS + h, bgv, 0.0)) for h in hs]
    rq = _each(lambda q: lax.rsqrt(_rowsum(q * q) + L2_EPS), qs)
    rk = _each(lambda k: lax.rsqrt(_rowsum(k * k) + L2_EPS), ks)
    yq = _each(jnp.multiply, qs, rq)
    kn = _each(jnp.multiply, ks, rk)
    qn = _each(lambda a: a * Q_SCALE, yq)
    gc_col = _each(lambda g: _rowsum(jnp.where(jj <= ii, g, 0.0)), g_rows)
    gc_row = _each(lambda g: jnp.sum(jnp.where(ii <= jj, g, 0.0), axis=0, keepdims=True), g_col)
    dm = _each(lambda a, b: jnp.where(incl, jnp.exp(jnp.where(incl, a - b, 0.0)), 0.0), gc_col, gc_row)
    gl = _each(_rowsum, g_rows)
    eg = _each(jnp.exp, gc_col)
    ek = _each(lambda a, b: jnp.exp(a - b), gl, gc_col)
    egl = _each(jnp.exp, gl)
    kb = _each(jnp.multiply, kn, beta)
    kk = _each(_dot_nt, kb, kn)
    qk = _each(_dot_nt, qn, kn)
    m = _each(lambda a, b: jnp.where(ii > jj, a * b, 0.0), kk, dm)
    attn = _each(jnp.multiply, qk, dm)
    return dict(ii=ii, jj=jj, beta=beta, rq=rq, rk=rk, yq=yq, kn=kn, qn=qn, dm=dm, eg=eg, ek=ek,
                egl=egl, kb=kb, m=m, attn=attn)


def _unit_lower_inverse_minus_identity(ms, ii, jj):
    pair = (ii >> 1) == (jj >> 1)
    ys = _each(lambda m: -jnp.where(pair, m, 0.0), ms)
    s = 1
    while (1 << s) < CHUNK:
        mask = ((ii >> (s + 1)) == (jj >> (s + 1))) & ((ii >> s) != (jj >> s))
        lbs = _each(lambda m: jnp.where(mask, m, 0.0), ms)
        zs = _each(lambda y, lb: lb + _dot(y, lb), ys, lbs)
        ys = _each(lambda y, z: y - z - _dot(z, y), ys, zs)
        s += 1
    return ys


def _dn_fwd(qkv_act, bg, bgt):
    t = qkv_act.shape[0]
    nt = t // CHUNK
    c = CHUNK
    hs = list(range(HEADS))
    qo = [slice(h * HEAD_DIM, (h + 1) * HEAD_DIM) for h in hs]
    ko = [slice(DN_WIDTH + h * HEAD_DIM, DN_WIDTH + (h + 1) * HEAD_DIM) for h in hs]
    vo = [slice(2 * DN_WIDTH + h * HEAD_DIM, 2 * DN_WIDTH + (h + 1) * HEAD_DIM) for h in hs]

    def body(qkv_ref, bg_ref, bgt_ref, o_ref, u_ref, w_ref, qg_ref, kg_ref, attn_ref, y_ref, vn_ref, st_ref, egl_ref,
             s_ref):
        @pl.when(pl.program_id(0) == 0)
        def _():
            s_ref[...] = jnp.zeros_like(s_ref)

        bgv = bg_ref[...]
        qs = [qkv_ref[:, o].astype(F32) for o in qo]
        ks = [qkv_ref[:, o].astype(F32) for o in ko]
        vs = [qkv_ref[:, o].astype(F32) for o in vo]
        g_rows = [bgt_ref[pl.ds(HEADS + h, 1), :] for h in hs]
        ct = _chunk_terms(qs, ks, bgv, g_rows, hs)
        ys = _unit_lower_inverse_minus_identity(ct["m"], ct["ii"], ct["jj"])
        vb = _each(jnp.multiply, vs, ct["beta"])
        kbe = _each(jnp.multiply, ct["kb"], ct["eg"])
        us = _each(lambda a, y: a + _dot(y, a), vb, ys)
        ws = _each(lambda a, y: _mx(a + _dot(y, a)), kbe, ys)
        qg = _each(lambda a, b: _mx(a * b), ct["qn"], ct["eg"])
        kg = _each(lambda a, b: _mx(a * b), ct["kn"], ct["ek"])
        attn = _each(_mx, ct["attn"])
        ss = [s_ref[h] for h in hs]
        sb = _each(_mx, ss)
        vn = _each(lambda a, b, s_: a - _dot(b, s_), us, ws, sb)
        vnb = _each(_mx, vn)
        oa = _each(_dot, qg, sb)
        ob = _each(_dot, attn, vnb)
        upd = _each(_dot_tn, kg, vnb)
        for h, sl in enumerate(qo):
            u_ref[:, sl] = us[h]
            w_ref[:, sl] = ws[h]
            qg_ref[:, sl] = qg[h]
            kg_ref[:, sl] = kg[h]
            attn_ref[:, sl] = attn[h]
            y_ref[:, sl] = _mx(ys[h])
            egl_ref[0, h:h + 1, :] = jnp.broadcast_to(ct["egl"][h], (1, HEAD_DIM))
            st_ref[0, h] = sb[h]
            vn_ref[:, sl] = vnb[h]
            o_ref[:, sl] = oa[h] + ob[h]
            s_ref[h] = ss[h] * ct["egl"][h] + upd[h]

    wide = _row(c, DN_WIDTH)
    return _pc(body, "dn_fwd", (nt,),
               [_row(c, QKV_WIDTH), _row(c, 128), pl.BlockSpec((2 * HEADS, c), lambda i: (0, i))],
               [wide] * 8 + [pl.BlockSpec((1, HEADS, HEAD_DIM, HEAD_DIM), lambda i: (i, 0, 0, 0)),
                             pl.BlockSpec((1, HEADS, HEAD_DIM), lambda i: (i, 0, 0))],
               [SDS((t, DN_WIDTH), F32), SDS((t, DN_WIDTH), F32)] + [SDS((t, DN_WIDTH), MXU_DTYPE)] * 6
               + [SDS((nt, HEADS, HEAD_DIM, HEAD_DIM), MXU_DTYPE), SDS((nt, HEADS, HEAD_DIM), F32)],
               scratch=[pltpu.VMEM((HEADS, HEAD_DIM, HEAD_DIM), F32)], sem=("arbitrary",))(qkv_act, bg, bgt)


def _dn_bwd(do, qkv_act, bg, bgt, u, w, qg, kg, attn, ymat, vn, states, egl):
    t = do.shape[0]
    nt = t // CHUNK
    c = CHUNK
    hs = list(range(HEADS))
    qo = [slice(h * HEAD_DIM, (h + 1) * HEAD_DIM) for h in hs]
    ko = [slice(DN_WIDTH + h * HEAD_DIM, DN_WIDTH + (h + 1) * HEAD_DIM) for h in hs]
    vo = [slice(2 * DN_WIDTH + h * HEAD_DIM, 2 * DN_WIDTH + (h + 1) * HEAD_DIM) for h in hs]

    def body(do_ref, qkv_ref, bg_ref, bgt_ref, u_ref, w_ref, qg_ref, kg_ref, attn_ref, y_ref, vn_ref, st_ref, egl_ref,
             dqkv_ref, dbg_ref, ds_ref):
        @pl.when(pl.program_id(0) == 0)
        def _():
            ds_ref[...] = jnp.zeros_like(ds_ref)

        dsp = [ds_ref[h] for h in hs]
        dsb = _each(_mx, dsp)
        sb = [st_ref[0, h] for h in hs]
        ss = _each(lambda a: a.astype(F32), sb)
        du = [_dot(kg_ref[:, sl], b) + _dot_tn(attn_ref[:, sl], do_ref[:, sl]) for sl, b in zip(qo, dsb)]
        dub = _each(_mx, du)
        dkg_v = [_dot_nt(vn_ref[:, sl], b) for sl, b in zip(qo, dsb)]
        dqg_v = [_dot_nt(do_ref[:, sl], b) for sl, b in zip(qo, sb)]
        dattn_v = [_dot_nt(do_ref[:, sl], vn_ref[:, sl]) for sl in qo]
        dwv = [-_dot_nt(a, b) for a, b in zip(dub, sb)]
        upd = [_dot_tn(qg_ref[:, sl], do_ref[:, sl]) - _dot_tn(w_ref[:, sl], a) for sl, a in zip(qo, dub)]
        degl_v = [jnp.sum(_rowsum(a * b), axis=0, keepdims=True) for a, b in zip(ss, dsp)]
        for h in hs:
            ds_ref[h] = dsp[h] * egl_ref[0, h:h + 1, :] + upd[h]

        bgv = bg_ref[...]
        lane = _lane(bgv.shape)
        rowi = lax.broadcasted_iota(jnp.int32, (c, 1), 0)
        qs = [qkv_ref[:, o].astype(F32) for o in qo]
        ks = [qkv_ref[:, o].astype(F32) for o in ko]
        vs = [qkv_ref[:, o].astype(F32) for o in vo]
        g_rows = [bgt_ref[pl.ds(HEADS + h, 1), :] for h in hs]
        ct = _chunk_terms(qs, ks, bgv, g_rows, hs)
        ii, jj = ct["ii"], ct["jj"]
        beta, eg, ek, kb, kn, qn, dm = ct["beta"], ct["eg"], ct["ek"], ct["kb"], ct["kn"], ct["qn"], ct["dm"]
        ys = [y_ref[:, o] for o in qo]
        dvb = _each(lambda a, y: a + _dot_tn(y, a), du, ys)
        dkbe = _each(lambda a, y: a + _dot_tn(y, a), dwv, ys)
        dm_u = [_dot_nt(a, u_ref[:, o]) for a, o in zip(dvb, qo)]
        dm_w = [_dot_nt(a, w_ref[:, o]) for a, o in zip(dkbe, qo)]
        dms = _each(lambda a, b: jnp.where(ii > jj, -(a + b), 0.0), dm_u, dm_w)
        dkk = _each(jnp.multiply, dms, dm)
        dqk = _each(jnp.multiply, dattn_v, dm)
        gmat = _each(lambda a, b, c_, d: a * b + c_ * d, dms, ct["m"], dattn_v, ct["attn"])
        dkb = _each(lambda a, b, c_, d: _dot(a, b) + c_ * d, dkk, kn, dkbe, eg)
        dk1 = _each(_dot_tn, dkk, kb)
        dk2 = _each(_dot_tn, dqk, qn)
        dq1 = _each(_dot, dqk, kn)
        dk = _each(lambda a, b, c_, d: a + b + c_ * d, dk1, dk2, dkg_v, ek)
        dq = _each(lambda a, b, c_: a + b * c_, dq1, dqg_v, eg)
        deg = _each(lambda a, b, c_, d: _rowsum(a * b) + _rowsum(c_ * d), dqg_v, qn, dkbe, kb)
        dek = _each(lambda a, b: _rowsum(a * b), dkg_v, kn)
        dgl = _each(lambda a, b, c_, d: jnp.sum(a * b, axis=0, keepdims=True) + c_ * d, dek, ek, degl_v, ct["egl"])
        cs_row = _each(lambda g: jnp.sum(g, axis=0, keepdims=True), gmat)
        cs_col = _each(lambda r: _rowsum(jnp.where(ii == jj, r, 0.0)), cs_row)
        dgc = _each(lambda a, b, c_, d, g, e, f: a * b - c_ * d + _rowsum(g) - e + jnp.where(rowi == c - 1, f, 0.0),
                    deg, eg, dek, ek, gmat, cs_col, dgl)
        dgc_row = _each(lambda a: jnp.sum(jnp.where(ii == jj, a, 0.0), axis=0, keepdims=True), dgc)
        dg = _each(lambda r: _rowsum(jnp.where(jj >= ii, r, 0.0)), dgc_row)
        dbeta = _each(lambda a, b, c_, d: _rowsum(a * b) + _rowsum(c_ * d), dkb, kn, dvb, vs)
        dk = _each(lambda a, b, c_: a + b * c_, dk, dkb, beta)
        dbg = jnp.zeros((c, 128), F32)
        for h in hs:
            dyq = dq[h] * Q_SCALE
            yq = ct["yq"][h]
            dqkv_ref[:, qo[h]] = _mx(ct["rq"][h] * (dyq - yq * _rowsum(yq * dyq)))
            dqkv_ref[:, ko[h]] = _mx(ct["rk"][h] * (dk[h] - kn[h] * _rowsum(kn[h] * dk[h])))
            dqkv_ref[:, vo[h]] = _mx(dvb[h] * beta[h])
            dbg = dbg + jnp.where(lane == h, dbeta[h], 0.0) + jnp.where(lane == HEADS + h, dg[h], 0.0)
        dbg_ref[...] = dbg

    rev = pl.BlockSpec((c, DN_WIDTH), lambda i: (nt - 1 - i, 0))
    return _pc(body, "dn_bwd", (nt,),
               [rev, pl.BlockSpec((c, QKV_WIDTH), lambda i: (nt - 1 - i, 0)),
                pl.BlockSpec((c, 128), lambda i: (nt - 1 - i, 0)), pl.BlockSpec((2 * HEADS, c), lambda i: (0, nt - 1 - i))]
               + [rev] * 7
               + [pl.BlockSpec((1, HEADS, HEAD_DIM, HEAD_DIM), lambda i: (nt - 1 - i, 0, 0, 0)),
                  pl.BlockSpec((1, HEADS, HEAD_DIM), lambda i: (nt - 1 - i, 0, 0))],
               [pl.BlockSpec((c, QKV_WIDTH), lambda i: (nt - 1 - i, 0)), pl.BlockSpec((c, 128), lambda i: (nt - 1 - i, 0))],
               [SDS((t, QKV_WIDTH), MXU_DTYPE), SDS((t, 128), F32)],
               scratch=[pltpu.VMEM((HEADS, HEAD_DIM, HEAD_DIM), F32)],
               sem=("arbitrary",))(do, qkv_act, bg, bgt, u, w, qg, kg, attn, ymat, vn, states, egl)


MIX_ROWS = 64


def _mix_oproj_ln1(o, proj, ypre, pool_scale, wo_row, w_out, h0, g1, b1, tm):
    t = o.shape[0]

    def body(o_ref, z_ref, ga_ref, gb_ref, yp_ref, ps_ref, wo_ref, w_ref, h0_ref, g_ref, b_ref,
             mixed_ref, a1_ref, h1_ref, h1b_ref):
        for r in range(0, tm, MIX_ROWS):
            rows = pl.ds(r, MIX_ROWS)
            for h in range(HEADS):
                sl = slice(h * HEAD_DIM, (h + 1) * HEAD_DIM)
                oh = o_ref[rows, sl]
                on = oh * lax.rsqrt(jnp.mean(oh * oh, axis=1, keepdims=True) + RMS_EPS)
                zh = z_ref[rows, sl]
                yb = on * wo_ref[:, sl] * (zh * _sigmoid(zh))
                ya = yp_ref[rows, sl] * ps_ref[:, sl]
                mixed_ref[rows, sl] = _mx(_sigmoid(ga_ref[rows, sl]) * ya + _sigmoid(gb_ref[rows, sl]) * yb)
        a1 = ALPHA * h0_ref[...] + _dot(mixed_ref[...], w_ref[...])
        a1_ref[...] = a1
        xhat, _ = _ln_stats(a1)
        h1 = xhat * g_ref[...] + b_ref[...]
        h1_ref[...] = h1
        h1b_ref[...] = _mx(h1)

    def col(blk):
        return pl.BlockSpec((tm, D_MODEL), lambda i: (i, blk))

    r = _row(tm, D_MODEL)
    v = _const((1, D_MODEL))
    return _pc(body, "mix_oproj_ln1", (t // tm,),
               [r, col(K_Z // D_MODEL), col(K_GA // D_MODEL), col(K_GB // D_MODEL), r, v, v,
                _const((D_MODEL, D_MODEL)), r, v, v],
               [r, r, r, r],
               [SDS((t, D_MODEL), MXU_DTYPE), SDS((t, D_MODEL), F32), SDS((t, D_MODEL), F32),
                SDS((t, D_MODEL), MXU_DTYPE)],
               sem=("parallel",))(o, proj, proj, proj, ypre, pool_scale, wo_row, w_out, h0, g1, b1)


def _mix_bwd(da1_bf, w_out, o, proj, ypre, pool_scale, wo_row, tm, after):
    t = o.shape[0]

    def body(da_ref, wout_ref, o_ref, z_ref, ga_ref, gb_ref, yp_ref, ps_ref, wo_ref, after_ref,
             do_ref, dp_ref, dyp_ref, acc_ref, dm_ref):
        i = pl.program_id(0)

        @pl.when(i == 0)
        def _():
            acc_ref[...] = jnp.zeros_like(acc_ref)

        dm_ref[...] = _dot_nt(da_ref[...], wout_ref[...])
        dwo = jnp.zeros((1, HEAD_DIM), F32)
        for h in range(HEADS):
            sl = slice(h * HEAD_DIM, (h + 1) * HEAD_DIM)
            woh = wo_ref[:, sl]
            psh = ps_ref[:, sl]
            dps = jnp.zeros((1, HEAD_DIM), F32)
            for r in range(0, tm, MIX_ROWS):
                rows = pl.ds(r, MIX_ROWS)
                oh = o_ref[rows, sl]
                rs = lax.rsqrt(jnp.mean(oh * oh, axis=1, keepdims=True) + RMS_EPS)
                on = oh * rs
                zh = z_ref[rows, sl]
                sz = _sigmoid(zh)
                silu = zh * sz
                t1 = on * woh
                yb = t1 * silu
                sa = _sigmoid(ga_ref[rows, sl])
                sb = _sigmoid(gb_ref[rows, sl])
                yp = yp_ref[rows, sl]
                dm = dm_ref[rows, sl]
                ga_sl = slice(D_MODEL + h * HEAD_DIM, D_MODEL + (h + 1) * HEAD_DIM)
                gb_sl = slice(2 * D_MODEL + h * HEAD_DIM, 2 * D_MODEL + (h + 1) * HEAD_DIM)
                dp_ref[rows, ga_sl] = _mx(dm * (yp * psh) * sa * (1.0 - sa))
                dp_ref[rows, gb_sl] = _mx(dm * yb * sb * (1.0 - sb))
                dya = dm * sa
                dyb = dm * sb
                dyp_ref[rows, sl] = _mx(dya * psh)
                dps = dps + jnp.sum(dya * yp, axis=0, keepdims=True)
                dp_ref[rows, sl] = _mx(dyb * t1 * (sz * (1.0 + zh * (1.0 - sz))))
                dt1 = dyb * silu
                dwo = dwo + jnp.sum(dt1 * on, axis=0, keepdims=True)
                don = dt1 * woh
                do_ref[rows, sl] = _mx(rs * (don - on * jnp.mean(don * on, axis=1, keepdims=True)))
            acc_ref[0:1, sl] += dps
        acc_ref[1:2, 0:HEAD_DIM] += dwo

    def col(blk):
        return pl.BlockSpec((tm, D_MODEL), lambda i: (i, blk))

    r = _row(tm, D_MODEL)
    return _pc(body, "mix_bwd", (t // tm,),
               [r, _const((D_MODEL, D_MODEL)), r, col(K_Z // D_MODEL), col(K_GA // D_MODEL), col(K_GB // D_MODEL), r,
                _const((1, D_MODEL)), _const((1, D_MODEL)), ANY],
               [r, pl.BlockSpec((tm, 3 * D_MODEL), lambda i: (i, K_Z // (3 * D_MODEL))), r, _const((8, D_MODEL))],
               [SDS((t, D_MODEL), MXU_DTYPE), SDS((t, CAT_WIDTH), MXU_DTYPE), SDS((t, D_MODEL), MXU_DTYPE),
                SDS((8, D_MODEL), F32)],
               scratch=[pltpu.VMEM((tm, D_MODEL), F32)],
               sem=("arbitrary",))(da1_bf, w_out, o, proj, proj, proj, ypre, pool_scale, wo_row, after)


def _mlp_up(h1_bf, w_up, tm):
    t = h1_bf.shape[0]
    tn = w_up.shape[2]

    def body(h_ref, w_ref, act_ref):
        r = jnp.maximum(_dot(h_ref[...], w_ref[...]), 0.0)
        act_ref[...] = _mx(r * r)

    return _pc(body, "mlp_up", (D_FF // tn, t // tm),
               [pl.BlockSpec((tm, D_MODEL), lambda j, i: (i, 0)),
                pl.BlockSpec((None, D_MODEL, tn), lambda j, i: (j, 0, 0))],
               pl.BlockSpec((tm, tn), lambda j, i: (i, j)), SDS((t, D_FF), MXU_DTYPE),
               sem=("parallel", "parallel"))(h1_bf, w_up)


def _tail(act, w_down, h1, w_gate, p_bf, w_proj, tgt, g2, b2, tm):
    t = act.shape[0]

    def body(act_ref, wd_ref, h1_ref, wg_ref, p_ref, wp_ref, tgt_ref, g_ref, b_ref,
             dr_ref, drb_ref, dgp_ref, dpp_ref, rb_ref, acc_ref):
        i = pl.program_id(0)

        @pl.when(i == 0)
        def _():
            acc_ref[...] = jnp.zeros_like(acc_ref)

        r = ALPHA * h1_ref[...] + _dot(act_ref[...], wd_ref[...])
        rb = _mx(r)
        rb_ref[...] = rb
        gate = _sigmoid(_dot(rb, wg_ref[...]))
        pp = _dot(p_ref[...], wp_ref[...])
        xhat, rstd = _ln_stats(r + gate * pp)
        g = g_ref[...]
        diff = xhat * g + b_ref[...] - tgt_ref[...]
        dh2 = diff * (1.0 / D_MODEL)
        rowloss = jnp.sum(diff * diff, axis=1, keepdims=True) * (0.5 / D_MODEL)
        acc_ref[0:1, :] += jnp.sum(dh2 * xhat, axis=0, keepdims=True)
        acc_ref[1:2, :] += jnp.sum(dh2, axis=0, keepdims=True)
        acc_ref[2:3, :] += jnp.broadcast_to(jnp.sum(rowloss, axis=0, keepdims=True), (1, D_MODEL))
        da2 = _ln_bwd(dh2, xhat, rstd, g)
        dpp_ref[...] = _mx(da2 * gate)
        dgp = _mx(da2 * pp * gate * (1.0 - gate))
        dgp_ref[...] = dgp
        dr = da2 + _dot_nt(dgp, wg_ref[...])
        dr_ref[...] = dr
        drb_ref[...] = _mx(dr)

    r = _row(tm, D_MODEL)
    v = _const((1, D_MODEL))
    return _pc(body, "tail", (t // tm,),
               [_row(tm, D_FF), _const((D_FF, D_MODEL)), r, _const((D_MODEL, D_MODEL)), _row(tm, PLE_DIM),
                _const((PLE_DIM, D_MODEL)), r, v, v],
               [r, r, r, r, r, _const((8, D_MODEL))],
               [SDS((t, D_MODEL), F32)] + [SDS((t, D_MODEL), MXU_DTYPE)] * 4 + [SDS((8, D_MODEL), F32)],
               sem=("arbitrary",))(act, w_down, h1, w_gate, p_bf, w_proj, tgt, g2, b2)


SQRT_GUARD = 1e-30


def _mlp_bwd1(dr_bf, w_down, act, tm, tn):
    t = act.shape[0]

    def body(dr_ref, w_ref, act_ref, dup_ref):
        dact = _dot_nt(dr_ref[...], w_ref[...])
        a = act_ref[...].astype(F32)
        dup_ref[...] = _mx(dact * (2.0 * a * lax.rsqrt(a + SQRT_GUARD)))

    o = pl.BlockSpec((tm, tn), lambda j, i: (i, j))
    return _pc(body, "mlp_bwd1", (D_FF // tn, t // tm),
               [pl.BlockSpec((tm, D_MODEL), lambda j, i: (i, 0)), pl.BlockSpec((tn, D_MODEL), lambda j, i: (j, 0)), o],
               o, SDS((t, D_FF), MXU_DTYPE), sem=("parallel", "parallel"))(dr_bf, w_down, act)


def _mlp_bwd2(dup, w_up, dr, a1, g1, tm):
    t = dr.shape[0]

    nk, tk = w_up.shape[0], w_up.shape[2]

    def body(dup_ref, w_ref, dr_ref, a1_ref, g_ref, da1_ref, da1b_ref, acc_ref):
        i = pl.program_id(0)

        @pl.when(i == 0)
        def _():
            acc_ref[...] = jnp.zeros_like(acc_ref)

        dh1 = ALPHA * dr_ref[...]
        for kk in range(nk):
            dh1 = dh1 + _dot_nt(dup_ref[:, kk * tk:(kk + 1) * tk], w_ref[kk])
        xhat, rstd = _ln_stats(a1_ref[...])
        acc_ref[0:1, :] += jnp.sum(dh1 * xhat, axis=0, keepdims=True)
        acc_ref[1:2, :] += jnp.sum(dh1, axis=0, keepdims=True)
        da1 = _ln_bwd(dh1, xhat, rstd, g_ref[...])
        da1_ref[...] = da1
        da1b_ref[...] = _mx(da1)

    r = _row(tm, D_MODEL)
    return _pc(body, "mlp_bwd2", (t // tm,),
               [_row(tm, D_FF), _const((nk, D_MODEL, tk)), r, r, _const((1, D_MODEL))],
               [r, r, _const((8, D_MODEL))],
               [SDS((t, D_MODEL), F32), SDS((t, D_MODEL), MXU_DTYPE), SDS((8, D_MODEL), F32)],
               sem=("arbitrary",))(dup, w_up, dr, a1, g1)


def _ln_in_bwd(dproj, w_cat, da1, x, g, tm, after):
    t = x.shape[0]

    def body(dp_ref, w_ref, da1_ref, x_ref, g_ref, after_ref, dx_ref, acc_ref):
        i = pl.program_id(0)

        @pl.when(i == 0)
        def _():
            acc_ref[...] = jnp.zeros_like(acc_ref)

        dh0 = _dot_nt(dp_ref[...], w_ref[...]) + ALPHA * da1_ref[...]
        xhat, rstd = _ln_stats(x_ref[...])
        acc_ref[0:1, :] += jnp.sum(dh0 * xhat, axis=0, keepdims=True)
        acc_ref[1:2, :] += jnp.sum(dh0, axis=0, keepdims=True)
        dx_ref[...] = _ln_bwd(dh0, xhat, rstd, g_ref[...])

    r = _row(tm, D_MODEL)
    return _pc(body, "ln_in_bwd", (t // tm,),
               [_row(tm, CAT_WIDTH), _const((D_MODEL, CAT_WIDTH)), r, r, _const((1, D_MODEL)), ANY],
               [r, _const((8, D_MODEL))], [SDS((t, D_MODEL), F32), SDS((8, D_MODEL), F32)],
               sem=("arbitrary",))(dproj, w_cat, da1, x, g, after)


def _local_step(x, p, tgt, wts, start_token, first_weights, late_weights, send_late_grads, send_early_grads):
    t = x.shape[0]
    tm = min(512, t)
    tms = min(256, t)
    row = lambda a: a.reshape(1, -1)
    pool_scale = row(wts["pool_scale"])
    wo_row = jnp.tile(row(wts["o_norm_w"]), (1, HEADS))
    pad8 = jnp.zeros((1, HEADS), F32)
    al_row = jnp.concatenate([pad8, row(wts["a_log"]), jnp.zeros((1, 128 - 2 * HEADS), F32)], axis=1)
    dtb_row = jnp.concatenate([pad8, row(wts["dt_bias"]), jnp.zeros((1, 128 - 2 * HEADS), F32)], axis=1)
    g_in, b_in = row(wts["ln_in_g"]), row(wts["ln_in_b"])
    g1, b1 = row(wts["ln1_g"]), row(wts["ln1_b"])
    g2, b2 = row(wts["ln2_g"]), row(wts["ln2_b"])

    h0, h0_bf = _ln_in(x, g_in, b_in, tm, start_token)
    first, first_token = first_weights(h0_bf)
    wts = {**wts, **first}
    w_cat = wts["w_cat"]
    proj, qkv_act, dsilu = _proj_conv(h0_bf, w_cat, wts["conv_w"], tms, first_token)
    ypre, d_bf = _pool_fwd(proj, wts["pool_w"], tm)
    bg = _ba_fwd(proj, al_row, dtb_row, tm)
    bgt = bg[:, :2 * HEADS].T
    o, u, w, qg, kg, attn, ymat, vn, states, egl = _dn_fwd(qkv_act, bg, bgt)
    wts = {**wts, **late_weights(o)}
    mixed, a1, h1, h1_bf = _mix_oproj_ln1(o, proj, ypre, pool_scale, wo_row, wts["w_out"], h0, g1, b1, tms)
    act = _mlp_up(h1_bf, wts["w_up"], tm)
    p_bf = _mx(p)
    dr, dr_bf, dgp, dpp, r_bf, acc_tail = _tail(act, wts["w_down"], h1, wts["ple_gate_w"], p_bf, wts["ple_proj_w"],
                                                tgt, g2, b2, tms)
    grads = {}
    grads["ple_proj_w"] = _matmul(p_bf, dpp, "tn", "dw_ple_proj", WIRE_DTYPE, tm=256, tn=1024, tk=DW_TK)
    grads["ple_gate_w"] = _matmul(r_bf, dgp, "tn", "dw_ple_gate", WIRE_DTYPE, tm=DW_TM, tn=1024, tk=DW_TK)
    grads["w_down"] = _matmul(act, dr_bf, "tn", "dw_down", WIRE_DTYPE, tm=DW_TM, tn=1024, tk=DW_TK)
    dup = _mlp_bwd1(dr_bf, wts["w_down"], act, tm, 1024)
    grads["w_up"] = _matmul(h1_bf, dup, "tn", "dw_up", WIRE_DTYPE, tm=DW_TM, tn=1024, tk=DW_TK, stack_out=True)
    da1, da1_bf, acc_ln1 = _mlp_bwd2(dup, wts["w_up"], dr, a1, g1, tms)
    grads["w_out"] = _matmul(mixed, da1_bf, "tn", "dw_out", WIRE_DTYPE, tm=DW_TM, tn=1024, tk=DW_TK)
    sent = send_late_grads(grads)
    do, dproj, dyp, acc_mix = _mix_bwd(da1_bf, wts["w_out"], o, proj, ypre, pool_scale, wo_row, tms, sent)
    dproj, grads["pool_w"] = _pool_bwd(dyp, d_bf, wts["pool_w"], dproj, tm)
    dqkv_act, dbg = _dn_bwd(do, qkv_act, bg, bgt, u, w, qg, kg, attn, ymat, vn, states, egl)
    dproj, acc_conv = _conv_bwd(dqkv_act, dsilu, proj, wts["conv_w"], dproj, tm)
    dproj, acc_ba = _ba_bwd(dbg, bg, proj, al_row, dtb_row, dproj, tm)
    dw_cat = _matmul(h0_bf, dproj, "tn", "dw_in", F32, tm=DW_TM, tn=1152, tk=DW_TK)
    grads["w_in"] = jnp.concatenate(
        [dw_cat[:, K_U:K_U + 512], dw_cat[:, K_QKV:K_QKV + 3072], dw_cat[:, K_Z:K_Z + 1024],
         dw_cat[:, K_BA:K_BA + 16], dw_cat[:, K_GA:K_GA + 1024], dw_cat[:, K_GB:K_GB + 1024]], axis=1)
    sent = send_early_grads(grads)
    grad_x, acc_in = _ln_in_bwd(dproj, w_cat, da1, x, g_in, tms, sent)

    grads["conv_w"] = acc_conv[0:CONV_K]
    grads["ln_in_g"], grads["ln_in_b"] = acc_in[0], acc_in[1]
    grads["ln1_g"], grads["ln1_b"] = acc_ln1[0], acc_ln1[1]
    grads["ln2_g"], grads["ln2_b"] = acc_tail[0], acc_tail[1]
    grads["pool_scale"] = acc_mix[0]
    grads["o_norm_w"] = acc_mix[1, 0:HEAD_DIM]
    grads["a_log"] = acc_ba[0, HEADS:2 * HEADS]
    grads["dt_bias"] = acc_ba[1, HEADS:2 * HEADS]
    loss = acc_tail[2, 0]
    return grad_x, grads, loss


MESH = pl.DeviceIdType.MESH
ANY = pl.BlockSpec(memory_space=pl.ANY)


def _chip_of(k, x, y):
    chip = (2 * x + y + k) % N_CHIPS
    return chip // 2, chip % 2


def _place():
    x, y, c = lax.axis_index("x"), lax.axis_index("y"), lax.axis_index("c")
    return x, y, c, 2 * x + y


def _half(rows, c):
    return pl.ds(pl.multiple_of(c * (rows // 2), 16), rows // 2)


def _remote(src, dst, send_sem, recv_sem, device_id):
    return pltpu.make_async_remote_copy(src_ref=src, dst_ref=dst, send_sem=send_sem, recv_sem=recv_sem,
                                        device_id=device_id, device_id_type=MESH)


def _tile_rows(rows):
    for tr in (256, 128, 64, 32, 16):
        if rows % tr == 0:
            return tr
    raise ValueError(rows)


def _first_gather_copies(srcs, lands, send, recv, place):
    copies = []
    for a in range(len(srcs)):
        whole = a == len(srcs) - 1
        for k in range(N_CHIPS):
            if place is None:
                copies.append(None)
                continue
            x, y, c, me = place
            sems = (send.at[a * N_CHIPS + k], recv.at[a * N_CHIPS + k])
            if k == 0:
                copies.append(_remote(srcs[a], lands[a].at[me], *sems, (x, y, 1 - c)))
                continue
            tx, ty = _chip_of(k, x, y)
            if whole:
                copies.append(_remote(srcs[a], lands[a].at[me], *sems, (tx, ty, c)))
            else:
                mine = _half(srcs[a].shape[0], c)
                copies.append(_remote(srcs[a].at[mine], lands[a].at[me, mine], *sems, (tx, ty, c)))
    return copies


def _pass_halves(stacks):
    n = len(stacks)

    def body(*refs):
        outs = refs[n:2 * n]
        send, recv = refs[2 * n:]
        x, y, c, me = _place()
        copies = []
        for a in range(n):
            for k in range(1, N_CHIPS):
                landed = outs[a].at[(me + N_CHIPS - k) % N_CHIPS, _half(stacks[a].shape[1], c)]
                copies.append(_remote(landed, landed, send.at[a * N_CHIPS + k], recv.at[a * N_CHIPS + k],
                                      (x, y, 1 - c)))
        for cp in copies:
            cp.start()
        for cp in copies:
            cp.wait_send()
        for a in range(n):
            for k in range(1, N_CHIPS):
                passed = outs[a].at[(me + N_CHIPS - k) % N_CHIPS, _half(stacks[a].shape[1], 1 - c)]
                _remote(passed, passed, send.at[a * N_CHIPS + k], recv.at[a * N_CHIPS + k], (x, y, c)).wait_recv()

    sems = pltpu.SemaphoreType.DMA((n * N_CHIPS,))
    return pl.pallas_call(
        body, name="pass_halves", out_shape=[SDS(s.shape, s.dtype) for s in stacks],
        in_specs=[ANY] * n, out_specs=[ANY] * n, scratch_shapes=[sems, sems],
        input_output_aliases={a: a for a in range(n)},
    )(*stacks)


def _swap_halves(gs):
    n = len(gs)

    def body(*refs):
        ins, theirs = refs[0:n], refs[n:2 * n]
        send, recv = refs[2 * n:]
        x, y, c, _ = _place()
        copies = [_remote(ins[a].at[:, _half(gs[a].shape[1], 1 - c)], theirs[a], send.at[a], recv.at[a],
                          (x, y, 1 - c)) for a in range(n)]
        for cp in copies:
            cp.start()
        for cp in copies:
            cp.wait()

    return pl.pallas_call(
        body, name="swap_halves", out_shape=[SDS((N_CHIPS, g.shape[1] // 2, g.shape[2]), g.dtype) for g in gs],
        in_specs=[ANY] * n, out_specs=[ANY] * n, scratch_shapes=[pltpu.SemaphoreType.DMA((n,))] * 2,
    )(*gs)


def _send_to_sibling(hs):
    n = len(hs)

    def body(*refs):
        ins, outs = refs[0:n], refs[n:2 * n]
        send, recv = refs[2 * n:]
        x, y, c, _ = _place()
        copies = [_remote(ins[a], outs[a], send.at[a], recv.at[a], (x, y, 1 - c)) for a in range(n)]
        for cp in copies:
            cp.start()
        for cp in copies:
            cp.wait()

    return pl.pallas_call(
        body, name="send_to_sibling", out_shape=[SDS(h.shape, h.dtype) for h in hs],
        in_specs=[ANY] * n, out_specs=[ANY] * n, scratch_shapes=[pltpu.SemaphoreType.DMA((n,))] * 2,
    )(*hs)


HBM = pl.BlockSpec(memory_space=pltpu.HBM)
SEM = pl.BlockSpec(memory_space=pltpu.SEMAPHORE)
EFFECT = pltpu.SideEffectType.DATAFLOW_SIDE_EFFECTING


def _in_hbm(a):
    return pltpu.with_memory_space_constraint(a, pltpu.HBM)


def _split_copy_start(name, srcs, lands, copies_of, after):
    n = len(srcs)
    n_copies = len(copies_of(srcs, lands, None, None, None))

    def body(*refs):
        src_refs, land_refs = refs[0:n], refs[n:2 * n]
        send, recv = refs[2 * n + 1], refs[2 * n + 2]
        token = refs[-1]
        for cp in copies_of(src_refs, land_refs, send, recv, _place()):
            cp.start()
        token[...] = jnp.zeros_like(token)

    sems = pltpu.SemaphoreType.DMA((n_copies,))
    out = pl.pallas_call(
        body, name=name,
        out_shape=[sems, sems] + [pltpu.HBM(a.shape, a.dtype) for a in list(srcs) + list(lands)] + [SDS((8, 128), F32)],
        in_specs=[HBM] * (2 * n) + [ANY],
        out_specs=[SEM, SEM] + [HBM] * (2 * n) + [pl.BlockSpec(memory_space=pltpu.VMEM)],
        input_output_aliases={i: 2 + i for i in range(2 * n)},
        compiler_params=pltpu.CompilerParams(has_side_effects=EFFECT),
    )(*[_in_hbm(a) for a in list(srcs) + list(lands)], after)
    return out[0], out[1], out[2:2 + n], out[2 + n:2 + 2 * n], out[-1]


def _split_copy_wait(name, send, recv, srcs, lands, after, copies_of):
    n = len(srcs)

    def body(*refs):
        src_refs, land_refs = refs[0:n], refs[n:2 * n]
        send_ref, recv_ref = refs[2 * n], refs[2 * n + 1]
        for cp in copies_of(src_refs, land_refs, send_ref, recv_ref, _place()):
            cp.wait_send()
            cp.wait_recv()

    out = pl.pallas_call(
        body, name=name, out_shape=[pltpu.HBM(a.shape, a.dtype) for a in list(srcs) + list(lands)],
        in_specs=[HBM] * (2 * n) + [SEM, SEM, ANY], out_specs=[HBM] * (2 * n),
        input_output_aliases={i: i for i in range(2 * n)},
        compiler_params=pltpu.CompilerParams(has_side_effects=EFFECT),
    )(*srcs, *lands, send, recv, after)
    return out[0:n], out[n:2 * n]


def _late_gather_copies(srcs, lands, send, recv, place):
    copies = []
    for a in range(len(srcs)):
        for k in range(N_CHIPS):
            if place is None:
                copies.append(None)
                continue
            x, y, c, me = place
            if k == 0:
                target = (x, y, 1 - c)
            else:
                tx, ty = _chip_of(k, x, y)
                target = (tx, ty, c)
            copies.append(_remote(srcs[a], lands[a].at[me], send.at[a * N_CHIPS + k], recv.at[a * N_CHIPS + k], target))
    return copies


def _late_scatter_copies(srcs, lands, send, recv, place):
    copies = []
    for a in range(len(srcs)):
        for k in range(1, N_CHIPS):
            if place is None:
                copies.append(None)
                continue
            x, y, c, _ = place
            tx, ty = _chip_of(k, x, y)
            copies.append(_remote(srcs[a].at[2 * tx + ty], lands[a].at[k - 1], send.at[a * (N_CHIPS - 1) + k - 1],
                                  recv.at[a * (N_CHIPS - 1) + k - 1], (tx, ty, c)))
    return copies


def _add_pair(g, theirs, name):
    _, rows, cols = g.shape
    half = rows // 2
    tr = _tile_rows(half)

    def body(g_ref, t_ref, o_ref):
        own = g_ref[lax.axis_index("c")]
        o_ref[...] = (own.astype(F32) + t_ref[...].astype(F32)).astype(o_ref.dtype)

    blk = pl.BlockSpec((None, tr, cols), lambda j, i: (j, i, 0))
    return _pc(body, "add_" + name, (N_CHIPS, half // tr),
               [pl.BlockSpec((None, 2, tr, cols), lambda j, i: (j, 0, i, 0)), blk], blk,
               SDS((N_CHIPS, half, cols), g.dtype), sem=("parallel", "parallel"))(
                   g.reshape(N_CHIPS, 2, half, cols), theirs)


def _sum_slabs(pair, landed, name):
    _, rows, cols = pair.shape
    tr = _tile_rows(rows)

    def body(p_ref, r_ref, o_ref):
        acc = p_ref[2 * lax.axis_index("x") + lax.axis_index("y")].astype(F32)
        for k in range(N_CHIPS - 1):
            acc = acc + r_ref[k].astype(F32)
        o_ref[...] = acc

    return _pc(body, "sum_" + name, (rows // tr,),
               [pl.BlockSpec((N_CHIPS, tr, cols), lambda i: (0, i, 0)),
                pl.BlockSpec((N_CHIPS - 1, tr, cols), lambda i: (0, i, 0))],
               _row(tr, cols), SDS((rows, cols), F32), sem=("parallel",))(pair, landed)


def _adamw_math(w, g, m, v):
    m = ADAM_B1 * m + (1.0 - ADAM_B1) * g
    v = ADAM_B2 * v + (1.0 - ADAM_B2) * (g * g)
    m_hat = m / (1.0 - ADAM_B1 ** ADAM_STEP)
    v_hat = v / (1.0 - ADAM_B2 ** ADAM_STEP)
    delta = -ADAM_LR * (m_hat / (jnp.sqrt(v_hat) + ADAM_EPS) + ADAM_WD * w)
    return delta, m, v


def _adamw_2d(w, g_own, g_sib, m, v, name, halves):
    lead = w.ndim == 3
    rows, cols = w.shape[-2:]
    tr = _tile_rows(rows // 2)
    nh = rows // 2 // tr if halves else rows // tr

    def body(w_ref, go_ref, gs_ref, m_ref, v_ref, g_out, d_out, m_out, v_out):
        if halves:
            mine = (pl.program_id(0) // nh) == lax.axis_index("c")
            g = jnp.where(mine, go_ref[...], gs_ref[...])
        else:
            g = go_ref[...] + gs_ref[...]
        delta, mn, vn = _adamw_math(w_ref[...], g, m_ref[...], v_ref[...])
        g_out[...] = g
        d_out[...] = delta
        m_out[...] = mn
        v_out[...] = vn

    r = _row(tr, cols)
    p = pl.BlockSpec((None, tr, cols), lambda i: (0, i, 0)) if lead else r
    h = pl.BlockSpec((tr, cols), lambda i: (i % nh, 0))
    return _pc(body, "adamw_" + name, (rows // tr,), [p, h, h, p, p], [r] * 4, [SDS((rows, cols), F32)] * 4,
               sem=("parallel",))(w, g_own, g_sib, m, v)


def _small_allreduce_adamw(mine, w, m, v):
    shape = mine.shape

    def body(mine_ref, w_ref, m_ref, v_ref, g_out, d_out, m_out, v_out, buf_ref, send_sems, recv_sems):
        x, y, c = lax.axis_index("x"), lax.axis_index("y"), lax.axis_index("c")
        me = 4 * x + 2 * y + c
        buf_ref[me] = mine_ref[...]
        copies = []
        for k in range(1, N_DEV):
            tgt = (me + k) % N_DEV
            copies.append(pltpu.make_async_remote_copy(
                src_ref=mine_ref, dst_ref=buf_ref.at[me], send_sem=send_sems.at[k], recv_sem=recv_sems.at[k],
                device_id=(tgt // 4, (tgt // 2) % 2, tgt % 2), device_id_type=MESH))
        for cp in copies:
            cp.start()
        for k in range(1, N_DEV):
            src = (me + N_DEV - k) % N_DEV
            pltpu.make_async_remote_copy(
                src_ref=mine_ref, dst_ref=buf_ref.at[src], send_sem=send_sems.at[k], recv_sem=recv_sems.at[k],
                device_id=(x, y, c), device_id_type=MESH).wait_recv()
        for cp in copies:
            cp.wait_send()
        g = buf_ref[0]
        for j in range(1, N_DEV):
            g = g + buf_ref[j]
        delta, mn, vn = _adamw_math(w_ref[...], g, m_ref[...], v_ref[...])
        g_out[...] = g
        d_out[...] = delta
        m_out[...] = mn
        v_out[...] = vn

    vm = pl.BlockSpec(memory_space=pltpu.VMEM)
    return pl.pallas_call(
        body, name="small_allreduce_adamw", out_shape=[SDS(shape, F32)] * 4, in_specs=[vm] * 4, out_specs=[vm] * 4,
        scratch_shapes=[pltpu.VMEM((N_DEV,) + shape, F32), pltpu.SemaphoreType.DMA((N_DEV,)),
                        pltpu.SemaphoreType.DMA((N_DEV,))],
    )(mine, w, m, v)


def _as2d(a):
    return a.reshape(-1, a.shape[-1])


def _w_cat(stack):
    wi = stack.transpose(1, 0, 2).reshape(D_MODEL, IN_WIDTH)
    return jnp.concatenate(
        [wi[:, C_QKV:C_Z], wi[:, C_Z:C_BETA], wi[:, C_GA:C_GB], wi[:, C_GB:IN_WIDTH], wi[:, C_POOL:C_QKV],
         wi[:, C_BETA:C_GA], jnp.zeros((D_MODEL, CAT_WIDTH - K_BA - 2 * HEADS), wi.dtype)], axis=1)


WEIGHT_LAYOUT = {
    "w_in": lambda s: ("w_cat", _w_cat(s)),
    "pool_w": lambda s: ("pool_w", s.reshape(N_CHIPS, 4, POOL_GROUP, POOL_OUT_GROUP // N_CHIPS)
                         .transpose(1, 2, 0, 3).reshape(4, POOL_GROUP, POOL_OUT_GROUP)),
    "w_out": lambda s: ("w_out", s.reshape(D_MODEL, D_MODEL)),
    "w_up": lambda s: ("w_up", s),
    "w_down": lambda s: ("w_down", s.reshape(D_FF, D_MODEL)),
    "ple_gate_w": lambda s: ("ple_gate_w", s.reshape(D_MODEL, D_MODEL)),
    "ple_proj_w": lambda s: ("ple_proj_w", s.transpose(1, 0, 2).reshape(PLE_DIM, D_MODEL)),
}

GRAD_LAYOUT = {
    "w_in": lambda g: g.reshape(D_MODEL, N_CHIPS, IN_WIDTH // N_CHIPS).transpose(1, 0, 2),
    "pool_w": lambda g: g.reshape(4, POOL_GROUP, N_CHIPS, POOL_OUT_GROUP // N_CHIPS)
                         .transpose(2, 0, 1, 3).reshape(N_CHIPS, 4 * POOL_GROUP, POOL_OUT_GROUP // N_CHIPS),
    "w_out": lambda g: g.reshape(N_CHIPS, D_MODEL // N_CHIPS, D_MODEL),
    "w_up": lambda g: g,
    "w_down": lambda g: g.reshape(N_CHIPS, D_FF // N_CHIPS, D_MODEL),
    "ple_gate_w": lambda g: g.reshape(N_CHIPS, D_MODEL // N_CHIPS, D_MODEL),
    "ple_proj_w": lambda g: g.reshape(PLE_DIM, N_CHIPS, D_MODEL // N_CHIPS).transpose(1, 0, 2),
}


def _full_weights(names, stacks):
    return dict(WEIGHT_LAYOUT[n](s.astype(MXU_DTYPE)) for n, s in zip(names, stacks))


def _grads_by_chip(names, grads):
    return [GRAD_LAYOUT[n](grads[n]).astype(WIRE_DTYPE) for n in names]


def _pack_small(rows, conv, name):
    n = len(rows)

    def body(*refs):
        out = refs[n + 1]
        out[...] = jnp.zeros_like(out)
        for i in range(n):
            out[i:i + 1, :] = refs[i][...]
        out[SMALL_CONV_AT:SMALL_CONV_AT + SMALL_CONV_ROWS, :] = refs[n][...]

    vm = pl.BlockSpec(memory_space=pltpu.VMEM)
    return pl.pallas_call(body, name=name, out_shape=SDS((SMALL_CONV_AT + SMALL_CONV_ROWS, D_MODEL), F32),
                          in_specs=[vm] * (n + 1), out_specs=vm)(*rows, conv)


def _pad_row(a):
    a = a.reshape(1, -1).astype(F32)
    return jnp.pad(a, ((0, 0), (0, D_MODEL - a.shape[1])))


def kernel(x, p, ln_in_g, ln_in_b, w_in, pool_w, pool_scale, conv_w, a_log, dt_bias, o_norm_w, w_out, ln1_g, ln1_b, w_up, w_down, ple_gate_w, ple_proj_w, ln2_g, ln2_b, loss_target, m_ln_in_g, m_ln_in_b, m_w_in, m_pool_w, m_pool_scale, m_conv_w, m_a_log, m_dt_bias, m_o_norm_w, m_w_out, m_ln1_g, m_ln1_b, m_w_up, m_w_down, m_ple_gate_w, m_ple_proj_w, m_ln2_g, m_ln2_b, v_ln_in_g, v_ln_in_b, v_w_in, v_pool_w, v_pool_scale, v_conv_w, v_a_log, v_dt_bias, v_o_norm_w, v_w_out, v_ln1_g, v_ln1_b, v_w_up, v_w_down, v_ple_gate_w, v_ple_proj_w, v_ln2_g, v_ln2_b):
    given = dict(locals())
    chip = 2 * lax.axis_index("x") + lax.axis_index("y")

    shard = lambda n: _as2d(given[n]).astype(WIRE_DTYPE)

    wts = {"ln_in_g": ln_in_g, "ln_in_b": ln_in_b, "pool_scale": pool_scale[0], "a_log": a_log[0],
           "dt_bias": dt_bias[0], "o_norm_w": o_norm_w[0], "ln1_g": ln1_g[0], "ln1_b": ln1_b[0],
           "ln2_g": ln2_g[0], "ln2_b": ln2_b[0]}

    conv_pad = jnp.pad(conv_w[0], ((0, 8 - CONV_K), (0, 0)))
    first_srcs = [shard(n) for n in EARLY] + [conv_pad]
    first_lands = [lax.empty((N_CHIPS,) + s.shape, s.dtype) for s in first_srcs]
    fsend, frecv, fsrcs, flands, start_token = _split_copy_start(
        "first_gather_start", first_srcs, first_lands, _first_gather_copies, first_srcs[0])
    late = {}

    def first_weights(after):
        _, lands = _split_copy_wait("first_gather_wait", fsend, frecv, fsrcs, flands, after, _first_gather_copies)
        stacks = _pass_halves(lands[0:len(EARLY)])
        first = _full_weights(EARLY, stacks)
        first["conv_w"] = jnp.concatenate([lands[len(EARLY)][j, 0:CONV_K] for j in range(N_CHIPS)], axis=1)
        late_srcs = [shard(n) for n in LATE]
        late_lands = [lax.empty((N_CHIPS,) + s.shape, s.dtype) for s in late_srcs]
        late["send"], late["recv"], late["srcs"], late["lands"], token = _split_copy_start(
            "late_gather_start", late_srcs, late_lands, _late_gather_copies, stacks[0])
        return first, token

    def late_weights(after):
        _, stacks = _split_copy_wait("late_gather_wait", late["send"], late["recv"], late["srcs"], late["lands"],
                                     after, _late_gather_copies)
        return _full_weights(LATE, stacks)

    scatter = {}

    def send_late_grads(grads):
        srcs = _grads_by_chip(LATE, grads)
        lands = [lax.empty((N_CHIPS - 1,) + g.shape[1:], g.dtype) for g in srcs]
        scatter["send"], scatter["recv"], scatter["srcs"], scatter["lands"], token = _split_copy_start(
            "late_scatter_start", srcs, lands, _late_scatter_copies, srcs[0])
        return token

    last = {}

    def send_early_grads(grads):
        by_chip = _grads_by_chip(EARLY, grads)
        theirs = _swap_halves(by_chip)
        pair = [_add_pair(g, t, n) for g, t, n in zip(by_chip, theirs, EARLY)]
        lands = [lax.empty((N_CHIPS - 1,) + q.shape[1:], q.dtype) for q in pair]
        last["send"], last["recv"], last["srcs"], last["lands"], token = _split_copy_start(
            "early_scatter_start", pair, lands, _late_scatter_copies, pair[0])
        return token

    grad_x, grads, loss = _local_step(x[0], p[0, 0], loss_target[0], wts, start_token, first_weights, late_weights,
                                      send_late_grads, send_early_grads)

    late_mine, late_landed = _split_copy_wait("late_scatter_wait", scatter["send"], scatter["recv"], scatter["srcs"],
                                              scatter["lands"], grad_x, _late_scatter_copies)
    late_part = [_sum_slabs(q, r, n) for q, r, n in zip(late_mine, late_landed, LATE)]
    pair, landed = _split_copy_wait("early_scatter_wait", last["send"], last["recv"], last["srcs"], last["lands"],
                                    grad_x, _late_scatter_copies)
    reduced = [_sum_slabs(q, r, n) for q, r, n in zip(pair, landed, EARLY)]
    from_sibling = _send_to_sibling(reduced + late_part)
    big_out = {}
    for n, g_own, g_sib in zip(EARLY + LATE, reduced + late_part, from_sibling):
        view = (lambda a: a) if given[n].ndim == 3 else _as2d
        res = _adamw_2d(view(given[n]), g_own, g_sib, view(given["m_" + n]), view(given["v_" + n]), n,
                        halves=n in EARLY)
        big_out[n] = [r.reshape(given[n].shape) for r in res]

    conv_cols = QKV_WIDTH // N_CHIPS

    def small_pack(get, conv, extra, name):
        if conv.shape[1] != QKV_WIDTH:
            conv = lax.dynamic_update_slice(jnp.zeros((CONV_K, QKV_WIDTH), F32), conv, (0, chip * conv_cols))
        return _pack_small([_pad_row(get(n)) for n in SMALL_NAMES] + extra, conv.reshape(SMALL_CONV_ROWS, D_MODEL), name)

    mine_small = small_pack(lambda n: grads[n], grads["conv_w"], [jnp.full((1, D_MODEL), loss, F32)], "pack_small_g")
    packed_small = [small_pack(lambda n: given[prefix + n], given[prefix + "conv_w"][0], [], "pack_small_" + tag)
                    for prefix, tag in (("", "w"), ("m_", "m"), ("v_", "v"))]
    small_out = _small_allreduce_adamw(mine_small, *packed_small)

    def small_get(k, n):
        if n == "conv_w":
            full = small_out[k][SMALL_CONV_AT:SMALL_CONV_AT + SMALL_CONV_ROWS].reshape(CONV_K, QKV_WIDTH)
            return lax.dynamic_slice(full, (0, chip * conv_cols), (CONV_K, conv_cols)).reshape(given[n].shape)
        i = SMALL_NAMES.index(n)
        return small_out[k][i, 0:given[n].size].reshape(given[n].shape)

    order = ["ln_in_g", "ln_in_b", "w_in", "pool_w", "pool_scale", "conv_w", "a_log", "dt_bias", "o_norm_w", "w_out",
             "ln1_g", "ln1_b", "w_up", "w_down", "ple_gate_w", "ple_proj_w", "ln2_g", "ln2_b"]
    outs = [small_out[0][len(SMALL_NAMES), 0], grad_x[None]]
    for k in range(4):
        for n in order:
            outs.append(big_out[n][k] if n in big_out else small_get(k, n))
    return tuple(outs)
```

```python
import jax
import jax.numpy as jnp
from jax import lax
from jax.experimental import pallas as pl
from jax.experimental.pallas import tpu as pltpu

F32 = jnp.float32
MXU_DTYPE = jnp.bfloat16
WIRE_DTYPE = jnp.bfloat16
SDS = jax.ShapeDtypeStruct

D_MODEL = 1024
POOL_WINDOWS = (2, 4, 8, 16)
POOL_WIDTH = 512
POOL_GROUP = 128
POOL_OUT_GROUP = 256
HEADS = 8
HEAD_DIM = 128
DN_WIDTH = HEADS * HEAD_DIM
QKV_WIDTH = 3 * DN_WIDTH
CONV_K = 4
CHUNK = 128
DW_TK = 1024
DW_TM = 1024
D_FF = 4096
PLE_DIM = 256
LN_EPS = 1e-5
RMS_EPS = 1e-6
L2_EPS = 1e-6
ALPHA = 2.0 ** 0.25
Q_SCALE = HEAD_DIM ** -0.5
IN_WIDTH = 6672
C_POOL, C_QKV, C_Z, C_BETA, C_A, C_GA, C_GB = 0, 512, 3584, 4608, 4616, 4624, 5648
K_QKV, K_Z, K_GA, K_GB, K_U, K_BA, CAT_WIDTH = 0, 3072, 4096, 5120, 6144, 6656, 6912

ADAM_LR, ADAM_B1, ADAM_B2, ADAM_EPS, ADAM_WD, ADAM_STEP = 0.001, 0.9, 0.999, 1e-08, 0.01, 10

N_CHIPS = 4
N_DEV = 8
VMEM_LIMIT = 56 * 1024 * 1024

EARLY = ("w_in", "pool_w")
LATE = ("w_out", "w_up", "w_down", "ple_gate_w", "ple_proj_w")
SMALL_NAMES = ("ln_in_g", "ln_in_b", "pool_scale", "ln1_g", "ln1_b", "ln2_g", "ln2_b", "o_norm_w", "a_log", "dt_bias")
SMALL_CONV_AT = 12
SMALL_CONV_ROWS = CONV_K * QKV_WIDTH // D_MODEL


def _mx(a):
    return a.astype(MXU_DTYPE)


def _dot(a, b):
    return lax.dot_general(_mx(a), _mx(b), (((1,), (0,)), ((), ())), preferred_element_type=F32)


def _dot_nt(a, b):
    return lax.dot_general(_mx(a), _mx(b), (((1,), (1,)), ((), ())), preferred_element_type=F32)


def _dot_tn(a, b):
    return lax.dot_general(_mx(a), _mx(b), (((0,), (0,)), ((), ())), preferred_element_type=F32)


def _sigmoid(x):
    return 0.5 * jnp.tanh(0.5 * x) + 0.5


def _softplus(x):
    return jnp.maximum(x, 0.0) + jnp.log(1.0 + jnp.exp(-jnp.abs(x)))


def _pc(body, name, grid, in_specs, out_specs, out_shape, scratch=(), sem=None, aliases=None):
    return pl.pallas_call(
        body, out_shape=out_shape, grid=grid, in_specs=in_specs, out_specs=out_specs,
        scratch_shapes=scratch, name=name, input_output_aliases=aliases or {},
        compiler_params=pltpu.CompilerParams(dimension_semantics=sem, vmem_limit_bytes=VMEM_LIMIT))


def _row(tm, n):
    return pl.BlockSpec((tm, n), lambda i: (i, 0))


def _const(shape):
    nd = len(shape)
    return pl.BlockSpec(shape, lambda *_: (0,) * nd)


def _matmul(a, b, mode, name, out_dtype=F32, tm=512, tn=512, tk=512, stack_out=False):
    if mode == "nn":
        (m, k), n = a.shape, b.shape[1]
    elif mode == "nt":
        (m, k), n = a.shape, b.shape[0]
    else:
        (k, m), n = a.shape, b.shape[1]
    tm, tn, tk = min(tm, m), min(tn, n), min(tk, k)
    assert m % tm == 0 and n % tn == 0 and k % tk == 0, (name, m, n, k, tm, tn, tk)
    nk = k // tk
    if mode == "nn":
        a_spec = pl.BlockSpec((tm, tk), lambda i, j, kk: (i, kk))
        b_spec = pl.BlockSpec((tk, tn), lambda i, j, kk: (kk, j))
        dot = _dot
    elif mode == "nt":
        a_spec = pl.BlockSpec((tm, tk), lambda i, j, kk: (i, kk))
        b_spec = pl.BlockSpec((tn, tk), lambda i, j, kk: (j, kk))
        dot = _dot_nt
    else:
        a_spec = pl.BlockSpec((tk, tm), lambda i, j, kk: (kk, i))
        b_spec = pl.BlockSpec((tk, tn), lambda i, j, kk: (kk, j))
        dot = _dot_tn

    def body(a_ref, b_ref, o_ref, *acc):
        if nk == 1:
            o_ref[...] = dot(a_ref[...], b_ref[...]).astype(out_dtype)
            return
        acc_ref, kk = acc[0], pl.program_id(2)

        @pl.when(kk == 0)
        def _():
            acc_ref[...] = dot(a_ref[...], b_ref[...])

        @pl.when((kk > 0) & (kk < nk - 1))
        def _():
            acc_ref[...] += dot(a_ref[...], b_ref[...])

        @pl.when(kk == nk - 1)
        def _():
            o_ref[...] = (acc_ref[...] + dot(a_ref[...], b_ref[...])).astype(out_dtype)

    if stack_out:
        o_spec, o_shape = pl.BlockSpec((None, tm, tn), lambda i, j, kk: (j, i, 0)), SDS((n // tn, m, tn), out_dtype)
    else:
        o_spec, o_shape = pl.BlockSpec((tm, tn), lambda i, j, kk: (i, j)), SDS((m, n), out_dtype)
    return _pc(body, name, (m // tm, n // tn, nk), [a_spec, b_spec], o_spec, o_shape,
               scratch=[pltpu.VMEM((tm, tn), F32)] if nk > 1 else [],
               sem=("parallel", "parallel", "arbitrary"))(a, b)


PROJ_TN = 768


def _proj_conv(h0_bf, w_cat, conv_w, tm, after):
    t = h0_bf.shape[0]
    n_qkv = QKV_WIDTH // PROJ_TN

    def body(h_ref, w_ref, cw_ref, after_ref, o_ref, act_ref, ds_ref, carry_ref, ext_ref):
        @pl.when(pl.program_id(0) == 0)
        def _():
            carry_ref[...] = jnp.zeros_like(carry_ref)

        h = h_ref[...]

        def project(cb):
            cols = slice(cb * PROJ_TN, (cb + 1) * PROJ_TN)
            o_ref[:, cols] = _dot(h, w_ref[:, cols])

        def conv(cb, part):
            cols = slice(cb * PROJ_TN, (cb + 1) * PROJ_TN)
            if part == 0:
                ext_ref[cb, 0:8, :] = carry_ref[:, cols]
                ext_ref[cb, 8:8 + tm, :] = o_ref[:, cols]
                carry_ref[:, cols] = o_ref[tm - 8:tm, cols]
            w = [cw_ref[pl.ds(k, 1), cols] for k in range(CONV_K)]
            for r in range(part * (tm // 2), (part + 1) * (tm // 2), CONV_ROWS):
                y = _conv_rows(ext_ref.at[cb], w, r, CONV_ROWS)
                s = _sigmoid(y)
                act_ref[pl.ds(r, CONV_ROWS), cols] = y * s
                ds_ref[pl.ds(r, CONV_ROWS), cols] = _mx(s * (1.0 + y * (1.0 - s)))

        pending = [(cb, part) for cb in range(n_qkv) for part in range(2)]
        project(0)
        for cb in range(1, CAT_WIDTH // PROJ_TN):
            project(cb)
            if pending and pending[0][0] < cb:
                conv(*pending.pop(0))
        for cb, part in pending:
            conv(cb, part)

    return _pc(body, "proj_conv", (t // tm,),
               [_row(tm, D_MODEL), _const((D_MODEL, CAT_WIDTH)), _const((CONV_K, QKV_WIDTH)), ANY],
               [_row(tm, CAT_WIDTH), _row(tm, QKV_WIDTH), _row(tm, QKV_WIDTH)],
               [SDS((t, CAT_WIDTH), F32), SDS((t, QKV_WIDTH), F32), SDS((t, QKV_WIDTH), MXU_DTYPE)],
               scratch=[pltpu.VMEM((8, QKV_WIDTH), F32), pltpu.VMEM((n_qkv, 8 + tm, PROJ_TN), F32)],
               sem=("arbitrary",))(h0_bf, w_cat, conv_w, after)


def _ln_stats(x):
    mu = jnp.mean(x, axis=-1, keepdims=True)
    xc = x - mu
    var = jnp.mean(xc * xc, axis=-1, keepdims=True)
    rstd = lax.rsqrt(var + LN_EPS)
    return xc * rstd, rstd


def _ln_bwd(dy, xhat, rstd, g):
    dxh = dy * g
    m1 = jnp.mean(dxh, axis=-1, keepdims=True)
    m2 = jnp.mean(dxh * xhat, axis=-1, keepdims=True)
    return rstd * (dxh - m1 - xhat * m2)


def _ln_in(x, g, b, tm, after):
    t, d = x.shape

    def body(x_ref, g_ref, b_ref, after_ref, h_ref, hb_ref):
        xhat, _ = _ln_stats(x_ref[...])
        h = xhat * g_ref[...] + b_ref[...]
        h_ref[...] = h
        hb_ref[...] = _mx(h)

    return _pc(body, "ln_in", (t // tm,), [_row(tm, d), _const((1, d)), _const((1, d)), ANY],
               [_row(tm, d), _row(tm, d)], [SDS((t, d), F32), SDS((t, d), MXU_DTYPE)],
               sem=("parallel",))(x, g, b, after)


def _pool_fwd(proj, pool_w, tm):
    t = proj.shape[0]
    ublk = K_U // POOL_WIDTH

    def body(u_ref, halo_ref, pw_ref, ypre_ref, d_ref, ext_ref):
        i = pl.program_id(0)
        ext_ref[0:16, :] = jnp.where(i > 0, halo_ref[...], 0.0)
        ext_ref[16:16 + tm, :] = u_ref[...]
        tok = i * tm + lax.broadcasted_iota(jnp.int32, (tm, POOL_GROUP), 0)
        for gi, w in enumerate(POOL_WINDOWS):
            cs = pl.ds(gi * POOL_GROUP, POOL_GROUP)
            ug = ext_ref[pl.ds(16, tm), cs]
            s = ug
            for k in range(1, w):
                s = s + ext_ref[pl.ds(16 - k, tm), cs]
            cnt = jnp.minimum(tok + 1, w).astype(F32)
            db = _mx(s / cnt - ug)
            d_ref[:, gi * POOL_GROUP:(gi + 1) * POOL_GROUP] = db
            ypre_ref[:, gi * POOL_OUT_GROUP:(gi + 1) * POOL_OUT_GROUP] = _dot(db, pw_ref[gi])

    halo = pl.BlockSpec((16, POOL_WIDTH), lambda i: (jnp.maximum(i * (tm // 16) - 1, 0), ublk))
    return _pc(body, "pool_fwd", (t // tm,),
               [pl.BlockSpec((tm, POOL_WIDTH), lambda i: (i, ublk)), halo, _const((4, POOL_GROUP, POOL_OUT_GROUP))],
               [_row(tm, D_MODEL), _row(tm, POOL_WIDTH)],
               [SDS((t, D_MODEL), F32), SDS((t, POOL_WIDTH), MXU_DTYPE)],
               scratch=[pltpu.VMEM((16 + tm, POOL_WIDTH), F32)], sem=("parallel",))(proj, proj, pool_w)


def _pool_bwd(dyp, d_bf, pool_w, dproj, tm):
    t = dyp.shape[0]
    n = t // tm

    def body(dy_ref, dyn_ref, d_ref, pw_ref, dproj_ref, du_ref, dpw_ref, ext_ref):
        i = pl.program_id(0)

        @pl.when(i == 0)
        def _():
            dpw_ref[...] = jnp.zeros_like(dpw_ref)

        tok = i * tm + lax.broadcasted_iota(jnp.int32, (tm + 16, POOL_GROUP), 0)
        for gi, w in enumerate(POOL_WINDOWS):
            dy = dy_ref[:, gi * POOL_OUT_GROUP:(gi + 1) * POOL_OUT_GROUP]
            dyn = dyn_ref[:, gi * POOL_OUT_GROUP:(gi + 1) * POOL_OUT_GROUP]
            pw = pw_ref[gi]
            dd = _dot_nt(dy, pw)
            ddn = jnp.where(i < n - 1, _dot_nt(dyn, pw), 0.0)
            cnt = jnp.minimum(tok + 1, w).astype(F32)
            ext_ref[0:tm, :] = dd / cnt[0:tm]
            ext_ref[tm:tm + 16, :] = ddn / cnt[tm:tm + 16]
            s = ext_ref[pl.ds(0, tm), :]
            for k in range(1, w):
                s = s + ext_ref[pl.ds(k, tm), :]
            du_ref[:, gi * POOL_GROUP:(gi + 1) * POOL_GROUP] = _mx(s - dd)
            dpw_ref[gi] += _dot_tn(d_ref[:, gi * POOL_GROUP:(gi + 1) * POOL_GROUP], dy)

    nxt = pl.BlockSpec((16, D_MODEL), lambda i: (jnp.minimum((i + 1) * (tm // 16), t // 16 - 1), 0))
    return _pc(body, "pool_bwd", (n,),
               [_row(tm, D_MODEL), nxt, _row(tm, POOL_WIDTH), _const((4, POOL_GROUP, POOL_OUT_GROUP)), ANY],
               [pl.BlockSpec((tm, POOL_WIDTH), lambda i: (i, K_U // POOL_WIDTH)),
                _const((4, POOL_GROUP, POOL_OUT_GROUP))],
               [SDS(dproj.shape, dproj.dtype), SDS((4, POOL_GROUP, POOL_OUT_GROUP), F32)],
               scratch=[pltpu.VMEM((tm + 16, POOL_GROUP), F32)], sem=("arbitrary",),
               aliases={4: 0})(dyp, dyp, d_bf, pool_w, dproj)


CONV_BLK = 512


CONV_ROWS = 32


def _conv_rows(ext_ref, w, r, rows):
    y = w[0] * ext_ref[pl.ds(r + 5, rows), :]
    for k in range(1, CONV_K):
        y = y + w[k] * ext_ref[pl.ds(r + 5 + k, rows), :]
    return y


def _conv_bwd(dact, dsilu, proj, conv_w, dproj, tm):
    t = proj.shape[0]
    n = t // tm

    def body(da_ref, dan_ref, ds_ref, dsn_ref, x_ref, xp_ref, w_ref, dproj_ref, dx_ref, dw_ref, ext_ref, dy_ref):
        i = pl.program_id(1)

        @pl.when(i == 0)
        def _():
            dw_ref[...] = jnp.zeros_like(dw_ref)

        ext_ref[0:8, :] = jnp.where(i > 0, xp_ref[...], 0.0)
        ext_ref[8:8 + tm, :] = x_ref[...]
        w = [w_ref[pl.ds(k, 1), :] for k in range(CONV_K)]

        acc = [jnp.zeros((8, CONV_BLK), F32) for _ in range(CONV_K)]
        for r in range(0, tm, CONV_ROWS):
            dy = da_ref[pl.ds(r, CONV_ROWS), :] * ds_ref[pl.ds(r, CONV_ROWS), :].astype(F32)
            dy_ref[pl.ds(r, CONV_ROWS), :] = dy
            for k in range(CONV_K):
                prod = dy * ext_ref[pl.ds(r + 5 + k, CONV_ROWS), :]
                for q in range(0, CONV_ROWS, 8):
                    acc[k] = acc[k] + prod[q:q + 8]
        dy_ref[tm:tm + 8, :] = jnp.where(i < n - 1, dan_ref[...] * dsn_ref[0:8, :].astype(F32), 0.0)
        for k in range(CONV_K):
            dw_ref[pl.ds(k, 1), :] += jnp.sum(acc[k], axis=0, keepdims=True)
        for r in range(0, tm, CONV_ROWS):
            dx = w[0] * dy_ref[pl.ds(r + 3, CONV_ROWS), :]
            for k in range(1, CONV_K):
                dx = dx + w[k] * dy_ref[pl.ds(r + 3 - k, CONV_ROWS), :]
            dx_ref[pl.ds(r, CONV_ROWS), :] = _mx(dx)

    blk = pl.BlockSpec((tm, CONV_BLK), lambda j, i: (i, j))
    prev = pl.BlockSpec((8, CONV_BLK), lambda j, i: (jnp.maximum(i * (tm // 8) - 1, 0), j))
    nxt = pl.BlockSpec((8, CONV_BLK), lambda j, i: (jnp.minimum((i + 1) * (tm // 8), t // 8 - 1), j))
    nxt16 = pl.BlockSpec((16, CONV_BLK), lambda j, i: (jnp.minimum((i + 1) * (tm // 16), t // 16 - 1), j))
    wspec = pl.BlockSpec((CONV_K, CONV_BLK), lambda j, i: (0, j))
    return _pc(body, "conv_bwd", (QKV_WIDTH // CONV_BLK, n),
               [blk, nxt, blk, nxt16, blk, prev, wspec, ANY],
               [blk, pl.BlockSpec((8, CONV_BLK), lambda j, i: (0, j))],
               [SDS(dproj.shape, dproj.dtype), SDS((8, QKV_WIDTH), F32)],
               scratch=[pltpu.VMEM((8 + tm, CONV_BLK), F32), pltpu.VMEM((8 + tm, CONV_BLK), F32)],
               sem=("parallel", "arbitrary"), aliases={7: 0})(dact, dact, dsilu, dsilu, proj, proj, conv_w, dproj)


def _lane(shape):
    return lax.broadcasted_iota(jnp.int32, shape, 1)


def _ba_fwd(proj, al_row, dtb_row, tm):
    t = proj.shape[0]
    bablk = K_BA // 128

    def body(ba_ref, al_ref, dtb_ref, bg_ref):
        ba = ba_ref[...]
        lane = _lane(ba.shape)
        g = -jnp.exp(al_ref[...]) * _softplus(ba + dtb_ref[...])
        bg_ref[...] = jnp.where(lane < HEADS, _sigmoid(ba), jnp.where(lane < 2 * HEADS, g, 0.0))

    return _pc(body, "ba_fwd", (t // tm,),
               [pl.BlockSpec((tm, 128), lambda i: (i, bablk)), _const((1, 128)), _const((1, 128))],
               _row(tm, 128), SDS((t, 128), F32), sem=("parallel",))(proj, al_row, dtb_row)


def _ba_bwd(dbg, bg, proj, al_row, dtb_row, dproj, tm):
    t = proj.shape[0]
    bablk = K_BA // 128

    def body(dbg_ref, bg_ref, ba_ref, al_ref, dtb_ref, dproj_ref, dba_ref, acc_ref):
        i = pl.program_id(0)

        @pl.when(i == 0)
        def _():
            acc_ref[...] = jnp.zeros_like(acc_ref)

        dbg_v, bg_v, ba = dbg_ref[...], bg_ref[...], ba_ref[...]
        lane = _lane(ba.shape)
        is_g = (lane >= HEADS) & (lane < 2 * HEADS)
        dbeta_raw = dbg_v * bg_v * (1.0 - bg_v)
        da_raw = dbg_v * (-jnp.exp(al_ref[...])) * _sigmoid(ba + dtb_ref[...])
        dba_ref[:, 0:128] = _mx(jnp.where(lane < HEADS, dbeta_raw, jnp.where(is_g, da_raw, 0.0)))
        dba_ref[:, 128:CAT_WIDTH - K_BA] = jnp.zeros((tm, CAT_WIDTH - K_BA - 128), dba_ref.dtype)
        acc_ref[0:1, :] += jnp.sum(jnp.where(is_g, dbg_v * bg_v, 0.0), axis=0, keepdims=True)
        acc_ref[1:2, :] += jnp.sum(jnp.where(is_g, da_raw, 0.0), axis=0, keepdims=True)

    tail = CAT_WIDTH - K_BA
    return _pc(body, "ba_bwd", (t // tm,),
               [_row(tm, 128), _row(tm, 128), pl.BlockSpec((tm, 128), lambda i: (i, bablk)),
                _const((1, 128)), _const((1, 128)), ANY],
               [pl.BlockSpec((tm, tail), lambda i: (i, K_BA // tail)), _const((8, 128))],
               [SDS(dproj.shape, dproj.dtype), SDS((8, 128), F32)],
               sem=("arbitrary",), aliases={5: 0})(dbg, bg, proj, al_row, dtb_row, dproj)


def _each(f, *lists):
    return [f(*a) for a in zip(*lists)]


def _rowsum(a):
    return jnp.sum(a, axis=1, keepdims=True)


def _chunk_terms(qs, ks, bgv, g_rows, hs):
    c = CHUNK
    ii = lax.broadcasted_iota(jnp.int32, (c, c), 0)
    jj = lax.broadcasted_iota(jnp.int32, (c, c), 1)
    lane = _lane(bgv.shape)
    incl = ii >= jj
    beta = [_rowsum(jnp.where(lane == h, bgv, 0.0)) for h in hs]
    g_col = [_rowsum(jnp.where(lane == HEADS + h, bgv, 0.0)) for h in hs]
    rq = _each(lambda q: lax.rsqrt(_rowsum(q * q) + L2_EPS), qs)
    rk = _each(lambda k: lax.rsqrt(_rowsum(k * k) + L2_EPS), ks)
    yq = _each(jnp.multiply, qs, rq)
    kn = _each(jnp.multiply, ks, rk)
    qn = _each(lambda a: a * Q_SCALE, yq)
    gc_col = _each(lambda g: _rowsum(jnp.where(jj <= ii, g, 0.0)), g_rows)
    gc_row = _each(lambda g: jnp.sum(jnp.where(ii <= jj, g, 0.0), axis=0, keepdims=True), g_col)
    dm = _each(lambda a, b: jnp.where(incl, jnp.exp(jnp.where(incl, a - b, 0.0)), 0.0), gc_col, gc_row)
    gl = _each(_rowsum, g_rows)
    eg = _each(jnp.exp, gc_col)
    ek = _each(lambda a, b: jnp.exp(a - b), gl, gc_col)
    egl = _each(jnp.exp, gl)
    kb = _each(jnp.multiply, kn, beta)
    kk = _each(_dot_nt, kb, kn)
    qk = _each(_dot_nt, qn, kn)
    m = _each(lambda a, b: jnp.where(ii > jj, a * b, 0.0), kk, dm)
    attn = _each(jnp.multiply, qk, dm)
    return dict(ii=ii, jj=jj, beta=beta, rq=rq, rk=rk, yq=yq, kn=kn, qn=qn, dm=dm, eg=eg, ek=ek,
                egl=egl, kb=kb, m=m, attn=attn)


def _unit_lower_inverse_minus_identity(ms, ii, jj):
    pair = (ii >> 1) == (jj >> 1)
    ys = _each(lambda m: -jnp.where(pair, m, 0.0), ms)
    s = 1
    while (1 << s) < CHUNK:
        mask = ((ii >> (s + 1)) == (jj >> (s + 1))) & ((ii >> s) != (jj >> s))
        lbs = _each(lambda m: jnp.where(mask, m, 0.0), ms)
        zs = _each(lambda y, lb: lb + _dot(y, lb), ys, lbs)
        ys = _each(lambda y, z: y - z - _dot(z, y), ys, zs)
        s += 1
    return ys


def _dn_fwd(qkv_act, bg, bgt):
    t = qkv_act.shape[0]
    nt = t // CHUNK
    c = CHUNK
    hs = list(range(HEADS))
    qo = [slice(h * HEAD_DIM, (h + 1) * HEAD_DIM) for h in hs]
    ko = [slice(DN_WIDTH + h * HEAD_DIM, DN_WIDTH + (h + 1) * HEAD_DIM) for h in hs]
    vo = [slice(2 * DN_WIDTH + h * HEAD_DIM, 2 * DN_WIDTH + (h + 1) * HEAD_DIM) for h in hs]

    def body(qkv_ref, bg_ref, bgt_ref, o_ref, u_ref, w_ref, qg_ref, kg_ref, attn_ref, y_ref, vn_ref, st_ref, egl_ref,
             s_ref):
        @pl.when(pl.program_id(0) == 0)
        def _():
            s_ref[...] = jnp.zeros_like(s_ref)

        bgv = bg_ref[...]
        qs = [qkv_ref[:, o] for o in qo]
        ks = [qkv_ref[:, o] for o in ko]
        vs = [qkv_ref[:, o] for o in vo]
        g_rows = [bgt_ref[pl.ds(HEADS + h, 1), :] for h in hs]
        ct = _chunk_terms(qs, ks, bgv, g_rows, hs)
        ys = _unit_lower_inverse_minus_identity(ct["m"], ct["ii"], ct["jj"])
        vb = _each(jnp.multiply, vs, ct["beta"])
        kbe = _each(jnp.multiply, ct["kb"], ct["eg"])
        us = _each(lambda a, y: a + _dot(y, a), vb, ys)
        ws = _each(lambda a, y: _mx(a + _dot(y, a)), kbe, ys)
        qg = _each(lambda a, b: _mx(a * b), ct["qn"], ct["eg"])
        kg = _each(lambda a, b: _mx(a * b), ct["kn"], ct["ek"])
        attn = _each(_mx, ct["attn"])
        ss = [s_ref[h] for h in hs]
        sb = _each(_mx, ss)
        vn = _each(lambda a, b, s_: a - _dot(b, s_), us, ws, sb)
        vnb = _each(_mx, vn)
        oa = _each(_dot, qg, sb)
        ob = _each(_dot, attn, vnb)
        upd = _each(_dot_tn, kg, vnb)
        for h, sl in enumerate(qo):
            u_ref[:, sl] = us[h]
            w_ref[:, sl] = ws[h]
            qg_ref[:, sl] = qg[h]
            kg_ref[:, sl] = kg[h]
            attn_ref[:, sl] = attn[h]
            y_ref[:, sl] = _mx(ys[h])
            egl_ref[0, h:h + 1, :] = jnp.broadcast_to(ct["egl"][h], (1, HEAD_DIM))
            st_ref[0, h] = ss[h]
            vn_ref[:, sl] = vnb[h]
            o_ref[:, sl] = oa[h] + ob[h]
            s_ref[h] = ss[h] * ct["egl"][h] + upd[h]

    wide = _row(c, DN_WIDTH)
    return _pc(body, "dn_fwd", (nt,),
               [_row(c, QKV_WIDTH), _row(c, 128), pl.BlockSpec((2 * HEADS, c), lambda i: (0, i))],
               [wide] * 8 + [pl.BlockSpec((1, HEADS, HEAD_DIM, HEAD_DIM), lambda i: (i, 0, 0, 0)),
                             pl.BlockSpec((1, HEADS, HEAD_DIM), lambda i: (i, 0, 0))],
               [SDS((t, DN_WIDTH), F32), SDS((t, DN_WIDTH), F32)] + [SDS((t, DN_WIDTH), MXU_DTYPE)] * 6
               + [SDS((nt, HEADS, HEAD_DIM, HEAD_DIM), F32), SDS((nt, HEADS, HEAD_DIM), F32)],
               scratch=[pltpu.VMEM((HEADS, HEAD_DIM, HEAD_DIM), F32)], sem=("arbitrary",))(qkv_act, bg, bgt)


def _dn_bwd(do, qkv_act, bg, bgt, u, w, qg, kg, attn, ymat, vn, states, egl):
    t = do.shape[0]
    nt = t // CHUNK
    c = CHUNK
    hs = list(range(HEADS))
    qo = [slice(h * HEAD_DIM, (h + 1) * HEAD_DIM) for h in hs]
    ko = [slice(DN_WIDTH + h * HEAD_DIM, DN_WIDTH + (h + 1) * HEAD_DIM) for h in hs]
    vo = [slice(2 * DN_WIDTH + h * HEAD_DIM, 2 * DN_WIDTH + (h + 1) * HEAD_DIM) for h in hs]

    def body(do_ref, qkv_ref, bg_ref, bgt_ref, u_ref, w_ref, qg_ref, kg_ref, attn_ref, y_ref, vn_ref, st_ref, egl_ref,
             dqkv_ref, dbg_ref, ds_ref):
        @pl.when(pl.program_id(0) == 0)
        def _():
            ds_ref[...] = jnp.zeros_like(ds_ref)

        dsp = [ds_ref[h] for h in hs]
        dsb = _each(_mx, dsp)
        ss = [st_ref[0, h] for h in hs]
        sb = _each(_mx, ss)
        du = [_dot(kg_ref[:, sl], b) + _dot_tn(attn_ref[:, sl], do_ref[:, sl]) for sl, b in zip(qo, dsb)]
        dub = _each(_mx, du)
        dkg_v = [_dot_nt(vn_ref[:, sl], b) for sl, b in zip(qo, dsb)]
        dqg_v = [_dot_nt(do_ref[:, sl], b) for sl, b in zip(qo, sb)]
        dattn_v = [_dot_nt(do_ref[:, sl], vn_ref[:, sl]) for sl in qo]
        dwv = [-_dot_nt(a, b) for a, b in zip(dub, sb)]
        upd = [_dot_tn(qg_ref[:, sl], do_ref[:, sl]) - _dot_tn(w_ref[:, sl], a) for sl, a in zip(qo, dub)]
        degl_v = [jnp.sum(_rowsum(a * b), axis=0, keepdims=True) for a, b in zip(ss, dsp)]
        for h in hs:
            ds_ref[h] = dsp[h] * egl_ref[0, h:h + 1, :] + upd[h]

        bgv = bg_ref[...]
        lane = _lane(bgv.shape)
        rowi = lax.broadcasted_iota(jnp.int32, (c, 1), 0)
        qs = [qkv_ref[:, o] for o in qo]
        ks = [qkv_ref[:, o] for o in ko]
        vs = [qkv_ref[:, o] for o in vo]
        g_rows = [bgt_ref[pl.ds(HEADS + h, 1), :] for h in hs]
        ct = _chunk_terms(qs, ks, bgv, g_rows, hs)
        ii, jj = ct["ii"], ct["jj"]
        beta, eg, ek, kb, kn, qn, dm = ct["beta"], ct["eg"], ct["ek"], ct["kb"], ct["kn"], ct["qn"], ct["dm"]
        ys = [y_ref[:, o] for o in qo]
        dvb = _each(lambda a, y: a + _dot_tn(y, a), du, ys)
        dkbe = _each(lambda a, y: a + _dot_tn(y, a), dwv, ys)
        dm_u = [_dot_nt(a, u_ref[:, o]) for a, o in zip(dvb, qo)]
        dm_w = [_dot_nt(a, w_ref[:, o]) for a, o in zip(dkbe, qo)]
        dms = _each(lambda a, b: jnp.where(ii > jj, -(a + b), 0.0), dm_u, dm_w)
        dkk = _each(jnp.multiply, dms, dm)
        dqk = _each(jnp.multiply, dattn_v, dm)
        gmat = _each(lambda a, b, c_, d: a * b + c_ * d, dms, ct["m"], dattn_v, ct["attn"])
        dkb = _each(lambda a, b, c_, d: _dot(a, b) + c_ * d, dkk, kn, dkbe, eg)
        dk1 = _each(_dot_tn, dkk, kb)
        dk2 = _each(_dot_tn, dqk, qn)
        dq1 = _each(_dot, dqk, kn)
        dk = _each(lambda a, b, c_, d: a + b + c_ * d, dk1, dk2, dkg_v, ek)
        dq = _each(lambda a, b, c_: a + b * c_, dq1, dqg_v, eg)
        deg = _each(lambda a, b, c_, d: _rowsum(a * b) + _rowsum(c_ * d), dqg_v, qn, dkbe, kb)
        dek = _each(lambda a, b: _rowsum(a * b), dkg_v, kn)
        dgl = _each(lambda a, b, c_, d: jnp.sum(a * b, axis=0, keepdims=True) + c_ * d, dek, ek, degl_v, ct["egl"])
        cs_row = _each(lambda g: jnp.sum(g, axis=0, keepdims=True), gmat)
        cs_col = _each(lambda r: _rowsum(jnp.where(ii == jj, r, 0.0)), cs_row)
        dgc = _each(lambda a, b, c_, d, g, e, f: a * b - c_ * d + _rowsum(g) - e + jnp.where(rowi == c - 1, f, 0.0),
                    deg, eg, dek, ek, gmat, cs_col, dgl)
        dgc_row = _each(lambda a: jnp.sum(jnp.where(ii == jj, a, 0.0), axis=0, keepdims=True), dgc)
        dg = _each(lambda r: _rowsum(jnp.where(jj >= ii, r, 0.0)), dgc_row)
        dbeta = _each(lambda a, b, c_, d: _rowsum(a * b) + _rowsum(c_ * d), dkb, kn, dvb, vs)
        dk = _each(lambda a, b, c_: a + b * c_, dk, dkb, beta)
        dbg = jnp.zeros((c, 128), F32)
        for h in hs:
            dyq = dq[h] * Q_SCALE
            yq = ct["yq"][h]
            dqkv_ref[:, qo[h]] = ct["rq"][h] * (dyq - yq * _rowsum(yq * dyq))
            dqkv_ref[:, ko[h]] = ct["rk"][h] * (dk[h] - kn[h] * _rowsum(kn[h] * dk[h]))
            dqkv_ref[:, vo[h]] = dvb[h] * beta[h]
            dbg = dbg + jnp.where(lane == h, dbeta[h], 0.0) + jnp.where(lane == HEADS + h, dg[h], 0.0)
        dbg_ref[...] = dbg

    rev = pl.BlockSpec((c, DN_WIDTH), lambda i: (nt - 1 - i, 0))
    return _pc(body, "dn_bwd", (nt,),
               [rev, pl.BlockSpec((c, QKV_WIDTH), lambda i: (nt - 1 - i, 0)),
                pl.BlockSpec((c, 128), lambda i: (nt - 1 - i, 0)), pl.BlockSpec((2 * HEADS, c), lambda i: (0, nt - 1 - i))]
               + [rev] * 7
               + [pl.BlockSpec((1, HEADS, HEAD_DIM, HEAD_DIM), lambda i: (nt - 1 - i, 0, 0, 0)),
                  pl.BlockSpec((1, HEADS, HEAD_DIM), lambda i: (nt - 1 - i, 0, 0))],
               [pl.BlockSpec((c, QKV_WIDTH), lambda i: (nt - 1 - i, 0)), pl.BlockSpec((c, 128), lambda i: (nt - 1 - i, 0))],
               [SDS((t, QKV_WIDTH), F32), SDS((t, 128), F32)],
               scratch=[pltpu.VMEM((HEADS, HEAD_DIM, HEAD_DIM), F32)],
               sem=("arbitrary",))(do, qkv_act, bg, bgt, u, w, qg, kg, attn, ymat, vn, states, egl)


MIX_ROWS = 64


def _mix_oproj_ln1(o, proj, ypre, pool_scale, wo_row, w_out, h0, g1, b1, tm):
    t = o.shape[0]

    def body(o_ref, z_ref, ga_ref, gb_ref, yp_ref, ps_ref, wo_ref, w_ref, h0_ref, g_ref, b_ref,
             mixed_ref, a1_ref, h1_ref, h1b_ref):
        for r in range(0, tm, MIX_ROWS):
            rows = pl.ds(r, MIX_ROWS)
            for h in range(HEADS):
                sl = slice(h * HEAD_DIM, (h + 1) * HEAD_DIM)
                oh = o_ref[rows, sl]
                on = oh * lax.rsqrt(jnp.mean(oh * oh, axis=1, keepdims=True) + RMS_EPS)
                zh = z_ref[rows, sl]
                yb = on * wo_ref[:, sl] * (zh * _sigmoid(zh))
                ya = yp_ref[rows, sl] * ps_ref[:, sl]
                mixed_ref[rows, sl] = _mx(_sigmoid(ga_ref[rows, sl]) * ya + _sigmoid(gb_ref[rows, sl]) * yb)
        a1 = ALPHA * h0_ref[...] + _dot(mixed_ref[...], w_ref[...])
        a1_ref[...] = a1
        xhat, _ = _ln_stats(a1)
        h1 = xhat * g_ref[...] + b_ref[...]
        h1_ref[...] = h1
        h1b_ref[...] = _mx(h1)

    def col(blk):
        return pl.BlockSpec((tm, D_MODEL), lambda i: (i, blk))

    r = _row(tm, D_MODEL)
    v = _const((1, D_MODEL))
    return _pc(body, "mix_oproj_ln1", (t // tm,),
               [r, col(K_Z // D_MODEL), col(K_GA // D_MODEL), col(K_GB // D_MODEL), r, v, v,
                _const((D_MODEL, D_MODEL)), r, v, v],
               [r, r, r, r],
               [SDS((t, D_MODEL), MXU_DTYPE), SDS((t, D_MODEL), F32), SDS((t, D_MODEL), F32),
                SDS((t, D_MODEL), MXU_DTYPE)],
               sem=("parallel",))(o, proj, proj, proj, ypre, pool_scale, wo_row, w_out, h0, g1, b1)


def _mix_bwd(da1_bf, w_out, o, proj, ypre, pool_scale, wo_row, tm, after):
    t = o.shape[0]

    def body(da_ref, wout_ref, o_ref, z_ref, ga_ref, gb_ref, yp_ref, ps_ref, wo_ref, after_ref,
             do_ref, dp_ref, dyp_ref, acc_ref, dm_ref):
        i = pl.program_id(0)

        @pl.when(i == 0)
        def _():
            acc_ref[...] = jnp.zeros_like(acc_ref)

        dm_ref[...] = _dot_nt(da_ref[...], wout_ref[...])
        dwo = jnp.zeros((1, HEAD_DIM), F32)
        for h in range(HEADS):
            sl = slice(h * HEAD_DIM, (h + 1) * HEAD_DIM)
            woh = wo_ref[:, sl]
            psh = ps_ref[:, sl]
            dps = jnp.zeros((1, HEAD_DIM), F32)
            for r in range(0, tm, MIX_ROWS):
                rows = pl.ds(r, MIX_ROWS)
                oh = o_ref[rows, sl]
                rs = lax.rsqrt(jnp.mean(oh * oh, axis=1, keepdims=True) + RMS_EPS)
                on = oh * rs
                zh = z_ref[rows, sl]
                sz = _sigmoid(zh)
                silu = zh * sz
                t1 = on * woh
                yb = t1 * silu
                sa = _sigmoid(ga_ref[rows, sl])
                sb = _sigmoid(gb_ref[rows, sl])
                yp = yp_ref[rows, sl]
                dm = dm_ref[rows, sl]
                ga_sl = slice(D_MODEL + h * HEAD_DIM, D_MODEL + (h + 1) * HEAD_DIM)
                gb_sl = slice(2 * D_MODEL + h * HEAD_DIM, 2 * D_MODEL + (h + 1) * HEAD_DIM)
                dp_ref[rows, ga_sl] = _mx(dm * (yp * psh) * sa * (1.0 - sa))
                dp_ref[rows, gb_sl] = _mx(dm * yb * sb * (1.0 - sb))
                dya = dm * sa
                dyb = dm * sb
                dyp_ref[rows, sl] = _mx(dya * psh)
                dps = dps + jnp.sum(dya * yp, axis=0, keepdims=True)
                dp_ref[rows, sl] = _mx(dyb * t1 * (sz * (1.0 + zh * (1.0 - sz))))
                dt1 = dyb * silu
                dwo = dwo + jnp.sum(dt1 * on, axis=0, keepdims=True)
                don = dt1 * woh
                do_ref[rows, sl] = _mx(rs * (don - on * jnp.mean(don * on, axis=1, keepdims=True)))
            acc_ref[0:1, sl] += dps
        acc_ref[1:2, 0:HEAD_DIM] += dwo

    def col(blk):
        return pl.BlockSpec((tm, D_MODEL), lambda i: (i, blk))

    r = _row(tm, D_MODEL)
    return _pc(body, "mix_bwd", (t // tm,),
               [r, _const((D_MODEL, D_MODEL)), r, col(K_Z // D_MODEL), col(K_GA // D_MODEL), col(K_GB // D_MODEL), r,
                _const((1, D_MODEL)), _const((1, D_MODEL)), ANY],
               [r, pl.BlockSpec((tm, 3 * D_MODEL), lambda i: (i, K_Z // (3 * D_MODEL))), r, _const((8, D_MODEL))],
               [SDS((t, D_MODEL), MXU_DTYPE), SDS((t, CAT_WIDTH), MXU_DTYPE), SDS((t, D_MODEL), MXU_DTYPE),
                SDS((8, D_MODEL), F32)],
               scratch=[pltpu.VMEM((tm, D_MODEL), F32)],
               sem=("arbitrary",))(da1_bf, w_out, o, proj, proj, proj, ypre, pool_scale, wo_row, after)


def _mlp_up(h1_bf, w_up, tm):
    t = h1_bf.shape[0]
    tn = w_up.shape[2]

    def body(h_ref, w_ref, act_ref):
        r = jnp.maximum(_dot(h_ref[...], w_ref[...]), 0.0)
        act_ref[...] = _mx(r * r)

    return _pc(body, "mlp_up", (D_FF // tn, t // tm),
               [pl.BlockSpec((tm, D_MODEL), lambda j, i: (i, 0)),
                pl.BlockSpec((None, D_MODEL, tn), lambda j, i: (j, 0, 0))],
               pl.BlockSpec((tm, tn), lambda j, i: (i, j)), SDS((t, D_FF), MXU_DTYPE),
               sem=("parallel", "parallel"))(h1_bf, w_up)


def _tail(act, w_down, h1, w_gate, p, w_proj, tgt, g2, b2, tm):
    t = act.shape[0]

    def body(act_ref, wd_ref, h1_ref, wg_ref, p_ref, wp_ref, tgt_ref, g_ref, b_ref,
             dr_ref, drb_ref, dgp_ref, dpp_ref, rb_ref, acc_ref):
        i = pl.program_id(0)

        @pl.when(i == 0)
        def _():
            acc_ref[...] = jnp.zeros_like(acc_ref)

        r = ALPHA * h1_ref[...] + _dot(act_ref[...], wd_ref[...])
        rb = _mx(r)
        rb_ref[...] = rb
        gate = _sigmoid(_dot(rb, wg_ref[...]))
        pp = _dot(p_ref[...], wp_ref[...])
        xhat, rstd = _ln_stats(r + gate * pp)
        g = g_ref[...]
        diff = xhat * g + b_ref[...] - tgt_ref[...]
        dh2 = diff * (1.0 / D_MODEL)
        rowloss = jnp.sum(diff * diff, axis=1, keepdims=True) * (0.5 / D_MODEL)
        acc_ref[0:1, :] += jnp.sum(dh2 * xhat, axis=0, keepdims=True)
        acc_ref[1:2, :] += jnp.sum(dh2, axis=0, keepdims=True)
        acc_ref[2:3, :] += jnp.broadcast_to(jnp.sum(rowloss, axis=0, keepdims=True), (1, D_MODEL))
        da2 = _ln_bwd(dh2, xhat, rstd, g)
        dpp_ref[...] = _mx(da2 * gate)
        dgp = _mx(da2 * pp * gate * (1.0 - gate))
        dgp_ref[...] = dgp
        dr = da2 + _dot_nt(dgp, wg_ref[...])
        dr_ref[...] = dr
        drb_ref[...] = _mx(dr)

    r = _row(tm, D_MODEL)
    v = _const((1, D_MODEL))
    return _pc(body, "tail", (t // tm,),
               [_row(tm, D_FF), _const((D_FF, D_MODEL)), r, _const((D_MODEL, D_MODEL)), _row(tm, PLE_DIM),
                _const((PLE_DIM, D_MODEL)), r, v, v],
               [r, r, r, r, r, _const((8, D_MODEL))],
               [SDS((t, D_MODEL), F32)] + [SDS((t, D_MODEL), MXU_DTYPE)] * 4 + [SDS((8, D_MODEL), F32)],
               sem=("arbitrary",))(act, w_down, h1, w_gate, p, w_proj, tgt, g2, b2)


SQRT_GUARD = 1e-30


def _mlp_bwd1(dr_bf, w_down, act, tm, tn):
    t = act.shape[0]

    def body(dr_ref, w_ref, act_ref, dup_ref):
        dact = _dot_nt(dr_ref[...], w_ref[...])
        a = act_ref[...].astype(F32)
        dup_ref[...] = _mx(dact * (2.0 * a * lax.rsqrt(a + SQRT_GUARD)))

    o = pl.BlockSpec((tm, tn), lambda j, i: (i, j))
    return _pc(body, "mlp_bwd1", (D_FF // tn, t // tm),
               [pl.BlockSpec((tm, D_MODEL), lambda j, i: (i, 0)), pl.BlockSpec((tn, D_MODEL), lambda j, i: (j, 0)), o],
               o, SDS((t, D_FF), MXU_DTYPE), sem=("parallel", "parallel"))(dr_bf, w_down, act)


def _mlp_bwd2(dup, w_up, dr, a1, g1, tm):
    t = dr.shape[0]

    nk, tk = w_up.shape[0], w_up.shape[2]

    def body(dup_ref, w_ref, dr_ref, a1_ref, g_ref, da1_ref, da1b_ref, acc_ref):
        i = pl.program_id(0)

        @pl.when(i == 0)
        def _():
            acc_ref[...] = jnp.zeros_like(acc_ref)

        dh1 = ALPHA * dr_ref[...]
        for kk in range(nk):
            dh1 = dh1 + _dot_nt(dup_ref[:, kk * tk:(kk + 1) * tk], w_ref[kk])
        xhat, rstd = _ln_stats(a1_ref[...])
        acc_ref[0:1, :] += jnp.sum(dh1 * xhat, axis=0, keepdims=True)
        acc_ref[1:2, :] += jnp.sum(dh1, axis=0, keepdims=True)
        da1 = _ln_bwd(dh1, xhat, rstd, g_ref[...])
        da1_ref[...] = da1
        da1b_ref[...] = _mx(da1)

    r = _row(tm, D_MODEL)
    return _pc(body, "mlp_bwd2", (t // tm,),
               [_row(tm, D_FF), _const((nk, D_MODEL, tk)), r, r, _const((1, D_MODEL))],
               [r, r, _const((8, D_MODEL))],
               [SDS((t, D_MODEL), F32), SDS((t, D_MODEL), MXU_DTYPE), SDS((8, D_MODEL), F32)],
               sem=("arbitrary",))(dup, w_up, dr, a1, g1)


def _ln_in_bwd(dproj, w_cat, da1, x, g, tm, after):
    t = x.shape[0]

    def body(dp_ref, w_ref, da1_ref, x_ref, g_ref, after_ref, dx_ref, acc_ref):
        i = pl.program_id(0)

        @pl.when(i == 0)
        def _():
            acc_ref[...] = jnp.zeros_like(acc_ref)

        dh0 = _dot_nt(dp_ref[...], w_ref[...]) + ALPHA * da1_ref[...]
        xhat, rstd = _ln_stats(x_ref[...])
        acc_ref[0:1, :] += jnp.sum(dh0 * xhat, axis=0, keepdims=True)
        acc_ref[1:2, :] += jnp.sum(dh0, axis=0, keepdims=True)
        dx_ref[...] = _ln_bwd(dh0, xhat, rstd, g_ref[...])

    r = _row(tm, D_MODEL)
    return _pc(body, "ln_in_bwd", (t // tm,),
               [_row(tm, CAT_WIDTH), _const((D_MODEL, CAT_WIDTH)), r, r, _const((1, D_MODEL)), ANY],
               [r, _const((8, D_MODEL))], [SDS((t, D_MODEL), F32), SDS((8, D_MODEL), F32)],
               sem=("arbitrary",))(dproj, w_cat, da1, x, g, after)


def _local_step(x, p, tgt, wts, start_token, first_weights, late_weights, send_late_grads, send_early_grads):
    t = x.shape[0]
    tm = min(512, t)
    tms = min(256, t)
    row = lambda a: a.reshape(1, -1)
    pool_scale = row(wts["pool_scale"])
    wo_row = jnp.tile(row(wts["o_norm_w"]), (1, HEADS))
    pad8 = jnp.zeros((1, HEADS), F32)
    al_row = jnp.concatenate([pad8, row(wts["a_log"]), jnp.zeros((1, 128 - 2 * HEADS), F32)], axis=1)
    dtb_row = jnp.concatenate([pad8, row(wts["dt_bias"]), jnp.zeros((1, 128 - 2 * HEADS), F32)], axis=1)
    g_in, b_in = row(wts["ln_in_g"]), row(wts["ln_in_b"])
    g1, b1 = row(wts["ln1_g"]), row(wts["ln1_b"])
    g2, b2 = row(wts["ln2_g"]), row(wts["ln2_b"])

    h0, h0_bf = _ln_in(x, g_in, b_in, tm, start_token)
    first, first_token = first_weights(h0_bf)
    wts = {**wts, **first}
    w_cat = wts["w_cat"]
    proj, qkv_act, dsilu = _proj_conv(h0_bf, w_cat, wts["conv_w"], tms, first_token)
    ypre, d_bf = _pool_fwd(proj, wts["pool_w"], tm)
    bg = _ba_fwd(proj, al_row, dtb_row, tm)
    bgt = bg[:, :2 * HEADS].T
    o, u, w, qg, kg, attn, ymat, vn, states, egl = _dn_fwd(qkv_act, bg, bgt)
    wts = {**wts, **late_weights(o)}
    mixed, a1, h1, h1_bf = _mix_oproj_ln1(o, proj, ypre, pool_scale, wo_row, wts["w_out"], h0, g1, b1, tms)
    act = _mlp_up(h1_bf, wts["w_up"], tm)
    dr, dr_bf, dgp, dpp, r_bf, acc_tail = _tail(act, wts["w_down"], h1, wts["ple_gate_w"], p, wts["ple_proj_w"],
                                                tgt, g2, b2, tms)
    grads = {}
    grads["ple_proj_w"] = _matmul(p, dpp, "tn", "dw_ple_proj", WIRE_DTYPE, tm=256, tn=1024, tk=DW_TK)
    grads["ple_gate_w"] = _matmul(r_bf, dgp, "tn", "dw_ple_gate", WIRE_DTYPE, tm=DW_TM, tn=1024, tk=DW_TK)
    grads["w_down"] = _matmul(act, dr_bf, "tn", "dw_down", WIRE_DTYPE, tm=DW_TM, tn=1024, tk=DW_TK)
    dup = _mlp_bwd1(dr_bf, wts["w_down"], act, tm, 1024)
    grads["w_up"] = _matmul(h1_bf, dup, "tn", "dw_up", WIRE_DTYPE, tm=DW_TM, tn=1024, tk=DW_TK, stack_out=True)
    da1, da1_bf, acc_ln1 = _mlp_bwd2(dup, wts["w_up"], dr, a1, g1, tms)
    grads["w_out"] = _matmul(mixed, da1_bf, "tn", "dw_out", WIRE_DTYPE, tm=DW_TM, tn=1024, tk=DW_TK)
    sent = send_late_grads(grads)
    do, dproj, dyp, acc_mix = _mix_bwd(da1_bf, wts["w_out"], o, proj, ypre, pool_scale, wo_row, tms, sent)
    dproj, grads["pool_w"] = _pool_bwd(dyp, d_bf, wts["pool_w"], dproj, tm)
    dqkv_act, dbg = _dn_bwd(do, qkv_act, bg, bgt, u, w, qg, kg, attn, ymat, vn, states, egl)
    dproj, acc_conv = _conv_bwd(dqkv_act, dsilu, proj, wts["conv_w"], dproj, tm)
    dproj, acc_ba = _ba_bwd(dbg, bg, proj, al_row, dtb_row, dproj, tm)
    dw_cat = _matmul(h0_bf, dproj, "tn", "dw_in", WIRE_DTYPE, tm=DW_TM, tn=1152, tk=DW_TK)
    grads["w_in"] = jnp.concatenate(
        [dw_cat[:, K_U:K_U + 512], dw_cat[:, K_QKV:K_QKV + 3072], dw_cat[:, K_Z:K_Z + 1024],
         dw_cat[:, K_BA:K_BA + 16], dw_cat[:, K_GA:K_GA + 1024], dw_cat[:, K_GB:K_GB + 1024]], axis=1)
    sent = send_early_grads(grads)
    grad_x, acc_in = _ln_in_bwd(dproj, w_cat, da1, x, g_in, tms, sent)

    grads["conv_w"] = acc_conv[0:CONV_K]
    grads["ln_in_g"], grads["ln_in_b"] = acc_in[0], acc_in[1]
    grads["ln1_g"], grads["ln1_b"] = acc_ln1[0], acc_ln1[1]
    grads["ln2_g"], grads["ln2_b"] = acc_tail[0], acc_tail[1]
    grads["pool_scale"] = acc_mix[0]
    grads["o_norm_w"] = acc_mix[1, 0:HEAD_DIM]
    grads["a_log"] = acc_ba[0, HEADS:2 * HEADS]
    grads["dt_bias"] = acc_ba[1, HEADS:2 * HEADS]
    loss = acc_tail[2, 0]
    return grad_x, grads, loss


MESH = pl.DeviceIdType.MESH
ANY = pl.BlockSpec(memory_space=pl.ANY)


def _chip_of(k, x, y):
    chip = (2 * x + y + k) % N_CHIPS
    return chip // 2, chip % 2


def _place():
    x, y, c = lax.axis_index("x"), lax.axis_index("y"), lax.axis_index("c")
    return x, y, c, 2 * x + y


def _half(rows, c):
    return pl.ds(pl.multiple_of(c * (rows // 2), 16), rows // 2)


def _remote(src, dst, send_sem, recv_sem, device_id):
    return pltpu.make_async_remote_copy(src_ref=src, dst_ref=dst, send_sem=send_sem, recv_sem=recv_sem,
                                        device_id=device_id, device_id_type=MESH)


def _tile_rows(rows):
    for tr in (256, 128, 64, 32, 16):
        if rows % tr == 0:
            return tr
    raise ValueError(rows)


def _first_gather_copies(srcs, lands, send, recv, place):
    copies = []
    for a in range(len(srcs)):
        whole = a == len(srcs) - 1
        for k in range(N_CHIPS):
            if place is None:
                copies.append(None)
                continue
            x, y, c, me = place
            sems = (send.at[a * N_CHIPS + k], recv.at[a * N_CHIPS + k])
            if k == 0:
                copies.append(_remote(srcs[a], lands[a].at[me], *sems, (x, y, 1 - c)))
                continue
            tx, ty = _chip_of(k, x, y)
            if whole:
                copies.append(_remote(srcs[a], lands[a].at[me], *sems, (tx, ty, c)))
            else:
                mine = _half(srcs[a].shape[0], c)
                copies.append(_remote(srcs[a].at[mine], lands[a].at[me, mine], *sems, (tx, ty, c)))
    return copies


def _pass_halves(stacks):
    n = len(stacks)

    def body(*refs):
        outs = refs[n:2 * n]
        send, recv = refs[2 * n:]
        x, y, c, me = _place()
        copies = []
        for a in range(n):
            for k in range(1, N_CHIPS):
                landed = outs[a].at[(me + N_CHIPS - k) % N_CHIPS, _half(stacks[a].shape[1], c)]
                copies.append(_remote(landed, landed, send.at[a * N_CHIPS + k], recv.at[a * N_CHIPS + k],
                                      (x, y, 1 - c)))
        for cp in copies:
            cp.start()
        for cp in copies:
            cp.wait_send()
        for a in range(n):
            for k in range(1, N_CHIPS):
                passed = outs[a].at[(me + N_CHIPS - k) % N_CHIPS, _half(stacks[a].shape[1], 1 - c)]
                _remote(passed, passed, send.at[a * N_CHIPS + k], recv.at[a * N_CHIPS + k], (x, y, c)).wait_recv()

    sems = pltpu.SemaphoreType.DMA((n * N_CHIPS,))
    return pl.pallas_call(
        body, name="pass_halves", out_shape=[SDS(s.shape, s.dtype) for s in stacks],
        in_specs=[ANY] * n, out_specs=[ANY] * n, scratch_shapes=[sems, sems],
        input_output_aliases={a: a for a in range(n)},
    )(*stacks)


def _swap_halves(gs):
    n = len(gs)

    def body(*refs):
        ins, theirs = refs[0:n], refs[n:2 * n]
        send, recv = refs[2 * n:]
        x, y, c, _ = _place()
        copies = [_remote(ins[a].at[:, _half(gs[a].shape[1], 1 - c)], theirs[a], send.at[a], recv.at[a],
                          (x, y, 1 - c)) for a in range(n)]
        for cp in copies:
            cp.start()
        for cp in copies:
            cp.wait()

    return pl.pallas_call(
        body, name="swap_halves", out_shape=[SDS((N_CHIPS, g.shape[1] // 2, g.shape[2]), g.dtype) for g in gs],
        in_specs=[ANY] * n, out_specs=[ANY] * n, scratch_shapes=[pltpu.SemaphoreType.DMA((n,))] * 2,
    )(*gs)


def _send_to_sibling(hs):
    n = len(hs)

    def body(*refs):
        ins, outs = refs[0:n], refs[n:2 * n]
        send, recv = refs[2 * n:]
        x, y, c, _ = _place()
        copies = [_remote(ins[a], outs[a], send.at[a], recv.at[a], (x, y, 1 - c)) for a in range(n)]
        for cp in copies:
            cp.start()
        for cp in copies:
            cp.wait()

    return pl.pallas_call(
        body, name="send_to_sibling", out_shape=[SDS(h.shape, h.dtype) for h in hs],
        in_specs=[ANY] * n, out_specs=[ANY] * n, scratch_shapes=[pltpu.SemaphoreType.DMA((n,))] * 2,
    )(*hs)


HBM = pl.BlockSpec(memory_space=pltpu.HBM)
SEM = pl.BlockSpec(memory_space=pltpu.SEMAPHORE)
EFFECT = pltpu.SideEffectType.DATAFLOW_SIDE_EFFECTING


def _in_hbm(a):
    return pltpu.with_memory_space_constraint(a, pltpu.HBM)


def _split_copy_start(name, srcs, lands, copies_of, after):
    n = len(srcs)
    n_copies = len(copies_of(srcs, lands, None, None, None))

    def body(*refs):
        src_refs, land_refs = refs[0:n], refs[n:2 * n]
        send, recv = refs[2 * n + 1], refs[2 * n + 2]
        token = refs[-1]
        for cp in copies_of(src_refs, land_refs, send, recv, _place()):
            cp.start()
        token[...] = jnp.zeros_like(token)

    sems = pltpu.SemaphoreType.DMA((n_copies,))
    out = pl.pallas_call(
        body, name=name,
        out_shape=[sems, sems] + [pltpu.HBM(a.shape, a.dtype) for a in list(srcs) + list(lands)] + [SDS((8, 128), F32)],
        in_specs=[HBM] * (2 * n) + [ANY],
        out_specs=[SEM, SEM] + [HBM] * (2 * n) + [pl.BlockSpec(memory_space=pltpu.VMEM)],
        input_output_aliases={i: 2 + i for i in range(2 * n)},
        compiler_params=pltpu.CompilerParams(has_side_effects=EFFECT),
    )(*[_in_hbm(a) for a in list(srcs) + list(lands)], after)
    return out[0], out[1], out[2:2 + n], out[2 + n:2 + 2 * n], out[-1]


def _split_copy_wait(name, send, recv, srcs, lands, after, copies_of):
    n = len(srcs)

    def body(*refs):
        src_refs, land_refs = refs[0:n], refs[n:2 * n]
        send_ref, recv_ref = refs[2 * n], refs[2 * n + 1]
        for cp in copies_of(src_refs, land_refs, send_ref, recv_ref, _place()):
            cp.wait_send()
            cp.wait_recv()

    out = pl.pallas_call(
        body, name=name, out_shape=[pltpu.HBM(a.shape, a.dtype) for a in list(srcs) + list(lands)],
        in_specs=[HBM] * (2 * n) + [SEM, SEM, ANY], out_specs=[HBM] * (2 * n),
        input_output_aliases={i: i for i in range(2 * n)},
        compiler_params=pltpu.CompilerParams(has_side_effects=EFFECT),
    )(*srcs, *lands, send, recv, after)
    return out[0:n], out[n:2 * n]


def _late_gather_copies(srcs, lands, send, recv, place):
    copies = []
    for a in range(len(srcs)):
        for k in range(N_CHIPS):
            if place is None:
                copies.append(None)
                continue
            x, y, c, me = place
            if k == 0:
                target = (x, y, 1 - c)
            else:
                tx, ty = _chip_of(k, x, y)
                target = (tx, ty, c)
            copies.append(_remote(srcs[a], lands[a].at[me], send.at[a * N_CHIPS + k], recv.at[a * N_CHIPS + k], target))
    return copies


def _late_scatter_copies(srcs, lands, send, recv, place):
    copies = []
    for a in range(len(srcs)):
        for k in range(1, N_CHIPS):
            if place is None:
                copies.append(None)
                continue
            x, y, c, _ = place
            tx, ty = _chip_of(k, x, y)
            copies.append(_remote(srcs[a].at[2 * tx + ty], lands[a].at[k - 1], send.at[a * (N_CHIPS - 1) + k - 1],
                                  recv.at[a * (N_CHIPS - 1) + k - 1], (tx, ty, c)))
    return copies


def _add_pair(g, theirs, name):
    _, rows, cols = g.shape
    half = rows // 2
    tr = _tile_rows(half)

    def body(g_ref, t_ref, o_ref):
        own = g_ref[lax.axis_index("c")]
        o_ref[...] = (own.astype(F32) + t_ref[...].astype(F32)).astype(o_ref.dtype)

    blk = pl.BlockSpec((None, tr, cols), lambda j, i: (j, i, 0))
    return _pc(body, "add_" + name, (N_CHIPS, half // tr),
               [pl.BlockSpec((None, 2, tr, cols), lambda j, i: (j, 0, i, 0)), blk], blk,
               SDS((N_CHIPS, half, cols), g.dtype), sem=("parallel", "parallel"))(
                   g.reshape(N_CHIPS, 2, half, cols), theirs)


def _sum_slabs(pair, landed, name):
    _, rows, cols = pair.shape
    tr = _tile_rows(rows)

    def body(p_ref, r_ref, o_ref):
        acc = p_ref[2 * lax.axis_index("x") + lax.axis_index("y")].astype(F32)
        for k in range(N_CHIPS - 1):
            acc = acc + r_ref[k].astype(F32)
        o_ref[...] = acc

    return _pc(body, "sum_" + name, (rows // tr,),
               [pl.BlockSpec((N_CHIPS, tr, cols), lambda i: (0, i, 0)),
                pl.BlockSpec((N_CHIPS - 1, tr, cols), lambda i: (0, i, 0))],
               _row(tr, cols), SDS((rows, cols), F32), sem=("parallel",))(pair, landed)


def _adamw_math(w, g, m, v):
    m = ADAM_B1 * m + (1.0 - ADAM_B1) * g
    v = ADAM_B2 * v + (1.0 - ADAM_B2) * (g * g)
    m_hat = m / (1.0 - ADAM_B1 ** ADAM_STEP)
    v_hat = v / (1.0 - ADAM_B2 ** ADAM_STEP)
    delta = -ADAM_LR * (m_hat / (jnp.sqrt(v_hat) + ADAM_EPS) + ADAM_WD * w)
    return delta, m, v


def _adamw_2d(w, g_own, g_sib, m, v, name, halves):
    lead = w.ndim == 3
    rows, cols = w.shape[-2:]
    tr = _tile_rows(rows // 2)
    nh = rows // 2 // tr if halves else rows // tr

    def body(w_ref, go_ref, gs_ref, m_ref, v_ref, g_out, d_out, m_out, v_out):
        if halves:
            mine = (pl.program_id(0) // nh) == lax.axis_index("c")
            g = jnp.where(mine, go_ref[...], gs_ref[...])
        else:
            g = go_ref[...] + gs_ref[...]
        delta, mn, vn = _adamw_math(w_ref[...], g, m_ref[...], v_ref[...])
        g_out[...] = g
        d_out[...] = delta
        m_out[...] = mn
        v_out[...] = vn

    r = _row(tr, cols)
    p = pl.BlockSpec((None, tr, cols), lambda i: (0, i, 0)) if lead else r
    h = pl.BlockSpec((tr, cols), lambda i: (i % nh, 0))
    return _pc(body, "adamw_" + name, (rows // tr,), [p, h, h, p, p], [r] * 4, [SDS((rows, cols), F32)] * 4,
               sem=("parallel",))(w, g_own, g_sib, m, v)


def _small_allreduce_adamw(mine, w, m, v, sizes):
    shape = mine.shape
    n = len(sizes)

    def body(mine_ref, w_ref, m_ref, v_ref, *rest):
        outs, (buf_ref, res_ref, send_sems, recv_sems) = rest[:-4], rest[-4:]
        x, y, c = lax.axis_index("x"), lax.axis_index("y"), lax.axis_index("c")
        me = 4 * x + 2 * y + c
        buf_ref[me] = mine_ref[...]
        copies = []
        for k in range(1, N_DEV):
            tgt = (me + k) % N_DEV
            copies.append(pltpu.make_async_remote_copy(
                src_ref=mine_ref, dst_ref=buf_ref.at[me], send_sem=send_sems.at[k], recv_sem=recv_sems.at[k],
                device_id=(tgt // 4, (tgt // 2) % 2, tgt % 2), device_id_type=MESH))
        for cp in copies:
            cp.start()
        for k in range(1, N_DEV):
            src = (me + N_DEV - k) % N_DEV
            pltpu.make_async_remote_copy(
                src_ref=mine_ref, dst_ref=buf_ref.at[src], send_sem=send_sems.at[k], recv_sem=recv_sems.at[k],
                device_id=(x, y, c), device_id_type=MESH).wait_recv()
        for cp in copies:
            cp.wait_send()
        g = buf_ref[0]
        for j in range(1, N_DEV):
            g = g + buf_ref[j]
        delta, mn, vn = _adamw_math(w_ref[...], g, m_ref[...], v_ref[...])
        for kind, val in enumerate((g, delta, mn, vn)):
            res_ref[kind] = val
            for i, size in enumerate(sizes):
                outs[kind * (n + 1) + i][...] = res_ref[kind, i:i + 1, 0:size]
            outs[kind * (n + 1) + n][...] = res_ref[kind, SMALL_CONV_AT:SMALL_CONV_AT + SMALL_CONV_ROWS, :]
        outs[-1][...] = res_ref[0, n:n + 1, 0:1]

    vm = pl.BlockSpec(memory_space=pltpu.VMEM)
    per_kind = [SDS((1, size), F32) for size in sizes] + [SDS((SMALL_CONV_ROWS, D_MODEL), F32)]
    out_shape = per_kind * 4 + [SDS((1, 1), F32)]
    out = pl.pallas_call(
        body, name="small_allreduce_adamw", out_shape=out_shape, in_specs=[vm] * 4, out_specs=[vm] * len(out_shape),
        scratch_shapes=[pltpu.VMEM((N_DEV,) + shape, F32), pltpu.VMEM((4,) + shape, F32),
                        pltpu.SemaphoreType.DMA((N_DEV,)), pltpu.SemaphoreType.DMA((N_DEV,))],
    )(mine, w, m, v)
    return [out[kind * (n + 1):(kind + 1) * (n + 1)] for kind in range(4)], out[-1]


def _as2d(a):
    return a.reshape(-1, a.shape[-1])


def _w_cat(stack):
    wi = stack.transpose(1, 0, 2).reshape(D_MODEL, IN_WIDTH)
    return jnp.concatenate(
        [wi[:, C_QKV:C_Z], wi[:, C_Z:C_BETA], wi[:, C_GA:C_GB], wi[:, C_GB:IN_WIDTH], wi[:, C_POOL:C_QKV],
         wi[:, C_BETA:C_GA], jnp.zeros((D_MODEL, CAT_WIDTH - K_BA - 2 * HEADS), wi.dtype)], axis=1)


WEIGHT_LAYOUT = {
    "w_in": lambda s: ("w_cat", _w_cat(s)),
    "pool_w": lambda s: ("pool_w", s.reshape(N_CHIPS, 4, POOL_GROUP, POOL_OUT_GROUP // N_CHIPS)
                         .transpose(1, 2, 0, 3).reshape(4, POOL_GROUP, POOL_OUT_GROUP)),
    "w_out": lambda s: ("w_out", s.reshape(D_MODEL, D_MODEL)),
    "w_up": lambda s: ("w_up", s),
    "w_down": lambda s: ("w_down", s.reshape(D_FF, D_MODEL)),
    "ple_gate_w": lambda s: ("ple_gate_w", s.reshape(D_MODEL, D_MODEL)),
    "ple_proj_w": lambda s: ("ple_proj_w", s.transpose(1, 0, 2).reshape(PLE_DIM, D_MODEL)),
}

GRAD_LAYOUT = {
    "w_in": lambda g: g.reshape(D_MODEL, N_CHIPS, IN_WIDTH // N_CHIPS).transpose(1, 0, 2),
    "pool_w": lambda g: g.reshape(4, POOL_GROUP, N_CHIPS, POOL_OUT_GROUP // N_CHIPS)
                         .transpose(2, 0, 1, 3).reshape(N_CHIPS, 4 * POOL_GROUP, POOL_OUT_GROUP // N_CHIPS),
    "w_out": lambda g: g.reshape(N_CHIPS, D_MODEL // N_CHIPS, D_MODEL),
    "w_up": lambda g: g,
    "w_down": lambda g: g.reshape(N_CHIPS, D_FF // N_CHIPS, D_MODEL),
    "ple_gate_w": lambda g: g.reshape(N_CHIPS, D_MODEL // N_CHIPS, D_MODEL),
    "ple_proj_w": lambda g: g.reshape(PLE_DIM, N_CHIPS, D_MODEL // N_CHIPS).transpose(1, 0, 2),
}


def _full_weights(names, stacks):
    return dict(WEIGHT_LAYOUT[n](s.astype(MXU_DTYPE)) for n, s in zip(names, stacks))


def _grads_by_chip(names, grads):
    return [GRAD_LAYOUT[n](grads[n]).astype(WIRE_DTYPE) for n in names]


def _pack_small(rows, conv, name):
    n = len(rows)

    def body(*refs):
        out = refs[n + 1]
        out[...] = jnp.zeros_like(out)
        for i in range(n):
            out[i:i + 1, :] = refs[i][...]
        out[SMALL_CONV_AT:SMALL_CONV_AT + SMALL_CONV_ROWS, :] = refs[n][...]

    vm = pl.BlockSpec(memory_space=pltpu.VMEM)
    return pl.pallas_call(body, name=name, out_shape=SDS((SMALL_CONV_AT + SMALL_CONV_ROWS, D_MODEL), F32),
                          in_specs=[vm] * (n + 1), out_specs=vm)(*rows, conv)


def _pad_row(a):
    a = a.reshape(1, -1).astype(F32)
    return jnp.pad(a, ((0, 0), (0, D_MODEL - a.shape[1])))


def kernel(x, p, ln_in_g, ln_in_b, w_in, pool_w, pool_scale, conv_w, a_log, dt_bias, o_norm_w, w_out, ln1_g, ln1_b, w_up, w_down, ple_gate_w, ple_proj_w, ln2_g, ln2_b, loss_target, m_ln_in_g, m_ln_in_b, m_w_in, m_pool_w, m_pool_scale, m_conv_w, m_a_log, m_dt_bias, m_o_norm_w, m_w_out, m_ln1_g, m_ln1_b, m_w_up, m_w_down, m_ple_gate_w, m_ple_proj_w, m_ln2_g, m_ln2_b, v_ln_in_g, v_ln_in_b, v_w_in, v_pool_w, v_pool_scale, v_conv_w, v_a_log, v_dt_bias, v_o_norm_w, v_w_out, v_ln1_g, v_ln1_b, v_w_up, v_w_down, v_ple_gate_w, v_ple_proj_w, v_ln2_g, v_ln2_b):
    given = dict(locals())
    chip = 2 * lax.axis_index("x") + lax.axis_index("y")

    shard = lambda n: _as2d(given[n]).astype(WIRE_DTYPE)

    wts = {"ln_in_g": ln_in_g, "ln_in_b": ln_in_b, "pool_scale": pool_scale[0], "a_log": a_log[0],
           "dt_bias": dt_bias[0], "o_norm_w": o_norm_w[0], "ln1_g": ln1_g[0], "ln1_b": ln1_b[0],
           "ln2_g": ln2_g[0], "ln2_b": ln2_b[0]}

    conv_pad = jnp.pad(conv_w[0], ((0, 8 - CONV_K), (0, 0)))
    first_srcs = [shard(n) for n in EARLY] + [conv_pad]
    first_lands = [lax.empty((N_CHIPS,) + s.shape, s.dtype) for s in first_srcs]
    fsend, frecv, fsrcs, flands, start_token = _split_copy_start(
        "first_gather_start", first_srcs, first_lands, _first_gather_copies, first_srcs[0])
    late = {}

    def first_weights(after):
        _, lands = _split_copy_wait("first_gather_wait", fsend, frecv, fsrcs, flands, after, _first_gather_copies)
        stacks = _pass_halves(lands[0:len(EARLY)])
        first = _full_weights(EARLY, stacks)
        first["conv_w"] = jnp.concatenate([lands[len(EARLY)][j, 0:CONV_K] for j in range(N_CHIPS)], axis=1)
        late_srcs = [shard(n) for n in LATE]
        late_lands = [lax.empty((N_CHIPS,) + s.shape, s.dtype) for s in late_srcs]
        late["send"], late["recv"], late["srcs"], late["lands"], token = _split_copy_start(
            "late_gather_start", late_srcs, late_lands, _late_gather_copies, stacks[0])
        return first, token

    def late_weights(after):
        _, stacks = _split_copy_wait("late_gather_wait", late["send"], late["recv"], late["srcs"], late["lands"],
                                     after, _late_gather_copies)
        return _full_weights(LATE, stacks)

    scatter = {}

    def send_late_grads(grads):
        srcs = _grads_by_chip(LATE, grads)
        lands = [lax.empty((N_CHIPS - 1,) + g.shape[1:], g.dtype) for g in srcs]
        scatter["send"], scatter["recv"], scatter["srcs"], scatter["lands"], token = _split_copy_start(
            "late_scatter_start", srcs, lands, _late_scatter_copies, srcs[0])
        return token

    last = {}

    def send_early_grads(grads):
        by_chip = _grads_by_chip(EARLY, grads)
        theirs = _swap_halves(by_chip)
        pair = [_add_pair(g, t, n) for g, t, n in zip(by_chip, theirs, EARLY)]
        lands = [lax.empty((N_CHIPS - 1,) + q.shape[1:], q.dtype) for q in pair]
        last["send"], last["recv"], last["srcs"], last["lands"], token = _split_copy_start(
            "early_scatter_start", pair, lands, _late_scatter_copies, pair[0])
        return token

    grad_x, grads, loss = _local_step(x[0], p[0, 0], loss_target[0], wts, start_token, first_weights, late_weights,
                                      send_late_grads, send_early_grads)

    late_mine, late_landed = _split_copy_wait("late_scatter_wait", scatter["send"], scatter["recv"], scatter["srcs"],
                                              scatter["lands"], grad_x, _late_scatter_copies)
    late_part = [_sum_slabs(q, r, n) for q, r, n in zip(late_mine, late_landed, LATE)]
    pair, landed = _split_copy_wait("early_scatter_wait", last["send"], last["recv"], last["srcs"], last["lands"],
                                    grad_x, _late_scatter_copies)
    reduced = [_sum_slabs(q, r, n) for q, r, n in zip(pair, landed, EARLY)]
    from_sibling = _send_to_sibling(reduced + late_part)
    big_out = {}
    for n, g_own, g_sib in zip(EARLY + LATE, reduced + late_part, from_sibling):
        view = (lambda a: a) if given[n].ndim == 3 else _as2d
        res = _adamw_2d(view(given[n]), g_own, g_sib, view(given["m_" + n]), view(given["v_" + n]), n,
                        halves=n in EARLY)
        big_out[n] = [r.reshape(given[n].shape) for r in res]

    conv_cols = QKV_WIDTH // N_CHIPS

    def small_pack(get, conv, extra, name):
        if conv.shape[1] != QKV_WIDTH:
            conv = lax.dynamic_update_slice(jnp.zeros((CONV_K, QKV_WIDTH), F32), conv, (0, chip * conv_cols))
        return _pack_small([_pad_row(get(n)) for n in SMALL_NAMES] + extra, conv.reshape(SMALL_CONV_ROWS, D_MODEL), name)

    mine_small = small_pack(lambda n: grads[n], grads["conv_w"], [jnp.full((1, D_MODEL), loss, F32)], "pack_small_g")
    packed_small = [small_pack(lambda n: given[prefix + n], given[prefix + "conv_w"][0], [], "pack_small_" + tag)
                    for prefix, tag in (("", "w"), ("m_", "m"), ("v_", "v"))]
    small_out, loss_sum = _small_allreduce_adamw(mine_small, *packed_small, [given[n].size for n in SMALL_NAMES])

    def small_get(k, n):
        if n == "conv_w":
            full = small_out[k][len(SMALL_NAMES)].reshape(CONV_K, QKV_WIDTH)
            return lax.dynamic_slice(full, (0, chip * conv_cols), (CONV_K, conv_cols)).reshape(given[n].shape)
        return small_out[k][SMALL_NAMES.index(n)].reshape(given[n].shape)

    order = ["ln_in_g", "ln_in_b", "w_in", "pool_w", "pool_scale", "conv_w", "a_log", "dt_bias", "o_norm_w", "w_out",
             "ln1_g", "ln1_b", "w_up", "w_down", "ple_gate_w", "ple_proj_w", "ln2_g", "ln2_b"]
    outs = [loss_sum.reshape(()), grad_x[None]]
    for k in range(4):
        for n in order:
            outs.append(big_out[n][k] if n in big_out else small_get(k, n))
    return tuple(outs)
```

```python
import jax
import jax.numpy as jnp
from jax import lax
from jax.experimental import pallas as pl
from jax.experimental.pallas import tpu as pltpu

F32 = jnp.float32
MXU_DTYPE = jnp.bfloat16
WIRE_DTYPE = jnp.bfloat16
SDS = jax.ShapeDtypeStruct

D_MODEL = 1024
POOL_WINDOWS = (2, 4, 8, 16)
POOL_WIDTH = 512
POOL_GROUP = 128
POOL_OUT_GROUP = 256
HEADS = 8
HEAD_DIM = 128
DN_WIDTH = HEADS * HEAD_DIM
QKV_WIDTH = 3 * DN_WIDTH
CONV_K = 4
CHUNK = 128
DW_TK = 1024
DW_TM = 1024
D_FF = 4096
PLE_DIM = 256
LN_EPS = 1e-5
RMS_EPS = 1e-6
L2_EPS = 1e-6
ALPHA = 2.0 ** 0.25
Q_SCALE = HEAD_DIM ** -0.5
IN_WIDTH = 6672
C_POOL, C_QKV, C_Z, C_BETA, C_A, C_GA, C_GB = 0, 512, 3584, 4608, 4616, 4624, 5648
K_QKV, K_Z, K_GA, K_GB, K_U, K_BA, CAT_WIDTH = 0, 3072, 4096, 5120, 6144, 6656, 6912

ADAM_LR, ADAM_B1, ADAM_B2, ADAM_EPS, ADAM_WD, ADAM_STEP = 0.001, 0.9, 0.999, 1e-08, 0.01, 10

N_CHIPS = 4
N_DEV = 8
VMEM_LIMIT = 56 * 1024 * 1024

EARLY = ("w_in", "pool_w")
LATE = ("w_out", "w_up", "w_down", "ple_gate_w", "ple_proj_w")
SMALL_NAMES = ("ln_in_g", "ln_in_b", "pool_scale", "ln1_g", "ln1_b", "ln2_g", "ln2_b", "o_norm_w", "a_log", "dt_bias")
SMALL_CONV_AT = 12
SMALL_CONV_ROWS = CONV_K * QKV_WIDTH // D_MODEL


def _mx(a):
    return a.astype(MXU_DTYPE)


def _dot(a, b):
    return lax.dot_general(_mx(a), _mx(b), (((1,), (0,)), ((), ())), preferred_element_type=F32)


def _dot_nt(a, b):
    return lax.dot_general(_mx(a), _mx(b), (((1,), (1,)), ((), ())), preferred_element_type=F32)


def _dot_tn(a, b):
    return lax.dot_general(_mx(a), _mx(b), (((0,), (0,)), ((), ())), preferred_element_type=F32)


def _sigmoid(x):
    return 0.5 * jnp.tanh(0.5 * x) + 0.5


def _softplus(x):
    return jnp.maximum(x, 0.0) + jnp.log(1.0 + jnp.exp(-jnp.abs(x)))


def _pc(body, name, grid, in_specs, out_specs, out_shape, scratch=(), sem=None, aliases=None):
    return pl.pallas_call(
        body, out_shape=out_shape, grid=grid, in_specs=in_specs, out_specs=out_specs,
        scratch_shapes=scratch, name=name, input_output_aliases=aliases or {},
        compiler_params=pltpu.CompilerParams(dimension_semantics=sem, vmem_limit_bytes=VMEM_LIMIT))


def _row(tm, n):
    return pl.BlockSpec((tm, n), lambda i: (i, 0))


def _const(shape):
    nd = len(shape)
    return pl.BlockSpec(shape, lambda *_: (0,) * nd)


def _matmul(a, b, mode, name, out_dtype=F32, tm=512, tn=512, tk=512, stack_out=False):
    if mode == "nn":
        (m, k), n = a.shape, b.shape[1]
    elif mode == "nt":
        (m, k), n = a.shape, b.shape[0]
    else:
        (k, m), n = a.shape, b.shape[1]
    tm, tn, tk = min(tm, m), min(tn, n), min(tk, k)
    assert m % tm == 0 and n % tn == 0 and k % tk == 0, (name, m, n, k, tm, tn, tk)
    nk = k // tk
    if mode == "nn":
        a_spec = pl.BlockSpec((tm, tk), lambda i, j, kk: (i, kk))
        b_spec = pl.BlockSpec((tk, tn), lambda i, j, kk: (kk, j))
        dot = _dot
    elif mode == "nt":
        a_spec = pl.BlockSpec((tm, tk), lambda i, j, kk: (i, kk))
        b_spec = pl.BlockSpec((tn, tk), lambda i, j, kk: (j, kk))
        dot = _dot_nt
    else:
        a_spec = pl.BlockSpec((tk, tm), lambda i, j, kk: (kk, i))
        b_spec = pl.BlockSpec((tk, tn), lambda i, j, kk: (kk, j))
        dot = _dot_tn

    def body(a_ref, b_ref, o_ref, *acc):
        if nk == 1:
            o_ref[...] = dot(a_ref[...], b_ref[...]).astype(out_dtype)
            return
        acc_ref, kk = acc[0], pl.program_id(2)

        @pl.when(kk == 0)
        def _():
            acc_ref[...] = dot(a_ref[...], b_ref[...])

        @pl.when((kk > 0) & (kk < nk - 1))
        def _():
            acc_ref[...] += dot(a_ref[...], b_ref[...])

        @pl.when(kk == nk - 1)
        def _():
            o_ref[...] = (acc_ref[...] + dot(a_ref[...], b_ref[...])).astype(out_dtype)

    if stack_out:
        o_spec, o_shape = pl.BlockSpec((None, tm, tn), lambda i, j, kk: (j, i, 0)), SDS((n // tn, m, tn), out_dtype)
    else:
        o_spec, o_shape = pl.BlockSpec((tm, tn), lambda i, j, kk: (i, j)), SDS((m, n), out_dtype)
    return _pc(body, name, (m // tm, n // tn, nk), [a_spec, b_spec], o_spec, o_shape,
               scratch=[pltpu.VMEM((tm, tn), F32)] if nk > 1 else [],
               sem=("parallel", "parallel", "arbitrary"))(a, b)


PROJ_TN = 768


def _proj_conv(h0_bf, w_cat, conv_w, tm, after):
    t = h0_bf.shape[0]
    n_qkv = QKV_WIDTH // PROJ_TN

    def body(h_ref, w_ref, cw_ref, after_ref, o_ref, act_ref, ds_ref, carry_ref, ext_ref):
        @pl.when(pl.program_id(0) == 0)
        def _():
            carry_ref[...] = jnp.zeros_like(carry_ref)

        h = h_ref[...]

        def project(cb):
            cols = slice(cb * PROJ_TN, (cb + 1) * PROJ_TN)
            o_ref[:, cols] = _dot(h, w_ref[:, cols])

        def conv(cb, part):
            cols = slice(cb * PROJ_TN, (cb + 1) * PROJ_TN)
            if part == 0:
                ext_ref[cb, 0:8, :] = carry_ref[:, cols]
                ext_ref[cb, 8:8 + tm, :] = o_ref[:, cols]
                carry_ref[:, cols] = o_ref[tm - 8:tm, cols]
            w = [cw_ref[pl.ds(k, 1), cols] for k in range(CONV_K)]
            for r in range(part * (tm // 2), (part + 1) * (tm // 2), CONV_ROWS):
                y = _conv_rows(ext_ref.at[cb], w, r, CONV_ROWS)
                s = _sigmoid(y)
                act_ref[pl.ds(r, CONV_ROWS), cols] = y * s
                ds_ref[pl.ds(r, CONV_ROWS), cols] = _mx(s * (1.0 + y * (1.0 - s)))

        pending = [(cb, part) for cb in range(n_qkv) for part in range(2)]
        project(0)
        for cb in range(1, CAT_WIDTH // PROJ_TN):
            project(cb)
            if pending and pending[0][0] < cb:
                conv(*pending.pop(0))
        for cb, part in pending:
            conv(cb, part)

    return _pc(body, "proj_conv", (t // tm,),
               [_row(tm, D_MODEL), _const((D_MODEL, CAT_WIDTH)), _const((CONV_K, QKV_WIDTH)), ANY],
               [_row(tm, CAT_WIDTH), _row(tm, QKV_WIDTH), _row(tm, QKV_WIDTH)],
               [SDS((t, CAT_WIDTH), F32), SDS((t, QKV_WIDTH), F32), SDS((t, QKV_WIDTH), MXU_DTYPE)],
               scratch=[pltpu.VMEM((8, QKV_WIDTH), F32), pltpu.VMEM((n_qkv, 8 + tm, PROJ_TN), F32)],
               sem=("arbitrary",))(h0_bf, w_cat, conv_w, after)


def _ln_stats(x):
    mu = jnp.mean(x, axis=-1, keepdims=True)
    xc = x - mu
    var = jnp.mean(xc * xc, axis=-1, keepdims=True)
    rstd = lax.rsqrt(var + LN_EPS)
    return xc * rstd, rstd


def _ln_bwd(dy, xhat, rstd, g):
    dxh = dy * g
    m1 = jnp.mean(dxh, axis=-1, keepdims=True)
    m2 = jnp.mean(dxh * xhat, axis=-1, keepdims=True)
    return rstd * (dxh - m1 - xhat * m2)


def _ln_in(x, g, b, tm, after):
    t, d = x.shape

    def body(x_ref, g_ref, b_ref, after_ref, h_ref, hb_ref):
        xhat, _ = _ln_stats(x_ref[...])
        h = xhat * g_ref[...] + b_ref[...]
        h_ref[...] = h
        hb_ref[...] = _mx(h)

    return _pc(body, "ln_in", (t // tm,), [_row(tm, d), _const((1, d)), _const((1, d)), ANY],
               [_row(tm, d), _row(tm, d)], [SDS((t, d), F32), SDS((t, d), MXU_DTYPE)],
               sem=("parallel",))(x, g, b, after)


def _pool_fwd(proj, pool_w, tm):
    t = proj.shape[0]
    ublk = K_U // POOL_WIDTH

    def body(u_ref, halo_ref, pw_ref, ypre_ref, d_ref, ext_ref):
        i = pl.program_id(0)
        ext_ref[0:16, :] = jnp.where(i > 0, halo_ref[...], 0.0)
        ext_ref[16:16 + tm, :] = u_ref[...]
        tok = i * tm + lax.broadcasted_iota(jnp.int32, (tm, POOL_GROUP), 0)
        for gi, w in enumerate(POOL_WINDOWS):
            cs = pl.ds(gi * POOL_GROUP, POOL_GROUP)
            ug = ext_ref[pl.ds(16, tm), cs]
            s = ug
            for k in range(1, w):
                s = s + ext_ref[pl.ds(16 - k, tm), cs]
            cnt = jnp.minimum(tok + 1, w).astype(F32)
            db = _mx(s / cnt - ug)
            d_ref[:, gi * POOL_GROUP:(gi + 1) * POOL_GROUP] = db
            ypre_ref[:, gi * POOL_OUT_GROUP:(gi + 1) * POOL_OUT_GROUP] = _dot(db, pw_ref[gi])

    halo = pl.BlockSpec((16, POOL_WIDTH), lambda i: (jnp.maximum(i * (tm // 16) - 1, 0), ublk))
    return _pc(body, "pool_fwd", (t // tm,),
               [pl.BlockSpec((tm, POOL_WIDTH), lambda i: (i, ublk)), halo, _const((4, POOL_GROUP, POOL_OUT_GROUP))],
               [_row(tm, D_MODEL), _row(tm, POOL_WIDTH)],
               [SDS((t, D_MODEL), F32), SDS((t, POOL_WIDTH), MXU_DTYPE)],
               scratch=[pltpu.VMEM((16 + tm, POOL_WIDTH), F32)], sem=("parallel",))(proj, proj, pool_w)


def _pool_bwd(dyp, d_bf, pool_w, dproj, tm):
    t = dyp.shape[0]
    n = t // tm

    def body(dy_ref, dyn_ref, d_ref, pw_ref, dproj_ref, du_ref, dpw_ref, ext_ref):
        i = pl.program_id(0)

        @pl.when(i == 0)
        def _():
            dpw_ref[...] = jnp.zeros_like(dpw_ref)

        tok = i * tm + lax.broadcasted_iota(jnp.int32, (tm + 16, POOL_GROUP), 0)
        for gi, w in enumerate(POOL_WINDOWS):
            dy = dy_ref[:, gi * POOL_OUT_GROUP:(gi + 1) * POOL_OUT_GROUP]
            dyn = dyn_ref[:, gi * POOL_OUT_GROUP:(gi + 1) * POOL_OUT_GROUP]
            pw = pw_ref[gi]
            dd = _dot_nt(dy, pw)
            ddn = jnp.where(i < n - 1, _dot_nt(dyn, pw), 0.0)
            cnt = jnp.minimum(tok + 1, w).astype(F32)
            ext_ref[0:tm, :] = dd / cnt[0:tm]
            ext_ref[tm:tm + 16, :] = ddn / cnt[tm:tm + 16]
            s = ext_ref[pl.ds(0, tm), :]
            for k in range(1, w):
                s = s + ext_ref[pl.ds(k, tm), :]
            du_ref[:, gi * POOL_GROUP:(gi + 1) * POOL_GROUP] = _mx(s - dd)
            dpw_ref[gi] += _dot_tn(d_ref[:, gi * POOL_GROUP:(gi + 1) * POOL_GROUP], dy)

    nxt = pl.BlockSpec((16, D_MODEL), lambda i: (jnp.minimum((i + 1) * (tm // 16), t // 16 - 1), 0))
    return _pc(body, "pool_bwd", (n,),
               [_row(tm, D_MODEL), nxt, _row(tm, POOL_WIDTH), _const((4, POOL_GROUP, POOL_OUT_GROUP)), ANY],
               [pl.BlockSpec((tm, POOL_WIDTH), lambda i: (i, K_U // POOL_WIDTH)),
                _const((4, POOL_GROUP, POOL_OUT_GROUP))],
               [SDS(dproj.shape, dproj.dtype), SDS((4, POOL_GROUP, POOL_OUT_GROUP), F32)],
               scratch=[pltpu.VMEM((tm + 16, POOL_GROUP), F32)], sem=("arbitrary",),
               aliases={4: 0})(dyp, dyp, d_bf, pool_w, dproj)


CONV_BLK = 512


CONV_ROWS = 32


def _conv_rows(ext_ref, w, r, rows):
    y = w[0] * ext_ref[pl.ds(r + 5, rows), :]
    for k in range(1, CONV_K):
        y = y + w[k] * ext_ref[pl.ds(r + 5 + k, rows), :]
    return y


def _conv_bwd(dact, dsilu, proj, conv_w, dproj, tm):
    t = proj.shape[0]
    n = t // tm

    def body(da_ref, dan_ref, ds_ref, dsn_ref, x_ref, xp_ref, w_ref, dproj_ref, dx_ref, dw_ref, ext_ref, dy_ref):
        i = pl.program_id(1)

        @pl.when(i == 0)
        def _():
            dw_ref[...] = jnp.zeros_like(dw_ref)

        ext_ref[0:8, :] = jnp.where(i > 0, xp_ref[...], 0.0)
        ext_ref[8:8 + tm, :] = x_ref[...]
        w = [w_ref[pl.ds(k, 1), :] for k in range(CONV_K)]

        acc = [jnp.zeros((8, CONV_BLK), F32) for _ in range(CONV_K)]
        for r in range(0, tm, CONV_ROWS):
            dy = da_ref[pl.ds(r, CONV_ROWS), :] * ds_ref[pl.ds(r, CONV_ROWS), :].astype(F32)
            dy_ref[pl.ds(r, CONV_ROWS), :] = dy
            for k in range(CONV_K):
                prod = dy * ext_ref[pl.ds(r + 5 + k, CONV_ROWS), :]
                for q in range(0, CONV_ROWS, 8):
                    acc[k] = acc[k] + prod[q:q + 8]
        dy_ref[tm:tm + 8, :] = jnp.where(i < n - 1, dan_ref[...] * dsn_ref[0:8, :].astype(F32), 0.0)
        for k in range(CONV_K):
            dw_ref[pl.ds(k, 1), :] += jnp.sum(acc[k], axis=0, keepdims=True)
        for r in range(0, tm, CONV_ROWS):
            dx = w[0] * dy_ref[pl.ds(r + 3, CONV_ROWS), :]
            for k in range(1, CONV_K):
                dx = dx + w[k] * dy_ref[pl.ds(r + 3 - k, CONV_ROWS), :]
            dx_ref[pl.ds(r, CONV_ROWS), :] = _mx(dx)

    blk = pl.BlockSpec((tm, CONV_BLK), lambda j, i: (i, j))
    prev = pl.BlockSpec((8, CONV_BLK), lambda j, i: (jnp.maximum(i * (tm // 8) - 1, 0), j))
    nxt = pl.BlockSpec((8, CONV_BLK), lambda j, i: (jnp.minimum((i + 1) * (tm // 8), t // 8 - 1), j))
    nxt16 = pl.BlockSpec((16, CONV_BLK), lambda j, i: (jnp.minimum((i + 1) * (tm // 16), t // 16 - 1), j))
    wspec = pl.BlockSpec((CONV_K, CONV_BLK), lambda j, i: (0, j))
    return _pc(body, "conv_bwd", (QKV_WIDTH // CONV_BLK, n),
               [blk, nxt, blk, nxt16, blk, prev, wspec, ANY],
               [blk, pl.BlockSpec((8, CONV_BLK), lambda j, i: (0, j))],
               [SDS(dproj.shape, dproj.dtype), SDS((8, QKV_WIDTH), F32)],
               scratch=[pltpu.VMEM((8 + tm, CONV_BLK), F32), pltpu.VMEM((8 + tm, CONV_BLK), F32)],
               sem=("parallel", "arbitrary"), aliases={7: 0})(dact, dact, dsilu, dsilu, proj, proj, conv_w, dproj)


def _lane(shape):
    return lax.broadcasted_iota(jnp.int32, shape, 1)


def _ba_fwd(proj, al_row, dtb_row, tm):
    t = proj.shape[0]
    bablk = K_BA // 128

    def body(ba_ref, al_ref, dtb_ref, bg_ref):
        ba = ba_ref[...]
        lane = _lane(ba.shape)
        g = -jnp.exp(al_ref[...]) * _softplus(ba + dtb_ref[...])
        bg_ref[...] = jnp.where(lane < HEADS, _sigmoid(ba), jnp.where(lane < 2 * HEADS, g, 0.0))

    return _pc(body, "ba_fwd", (t // tm,),
               [pl.BlockSpec((tm, 128), lambda i: (i, bablk)), _const((1, 128)), _const((1, 128))],
               _row(tm, 128), SDS((t, 128), F32), sem=("parallel",))(proj, al_row, dtb_row)


def _ba_bwd(dbg, bg, proj, al_row, dtb_row, dproj, tm):
    t = proj.shape[0]
    bablk = K_BA // 128

    def body(dbg_ref, bg_ref, ba_ref, al_ref, dtb_ref, dproj_ref, dba_ref, acc_ref):
        i = pl.program_id(0)

        @pl.when(i == 0)
        def _():
            acc_ref[...] = jnp.zeros_like(acc_ref)

        dbg_v, bg_v, ba = dbg_ref[...], bg_ref[...], ba_ref[...]
        lane = _lane(ba.shape)
        is_g = (lane >= HEADS) & (lane < 2 * HEADS)
        dbeta_raw = dbg_v * bg_v * (1.0 - bg_v)
        da_raw = dbg_v * (-jnp.exp(al_ref[...])) * _sigmoid(ba + dtb_ref[...])
        dba_ref[:, 0:128] = _mx(jnp.where(lane < HEADS, dbeta_raw, jnp.where(is_g, da_raw, 0.0)))
        dba_ref[:, 128:CAT_WIDTH - K_BA] = jnp.zeros((tm, CAT_WIDTH - K_BA - 128), dba_ref.dtype)
        acc_ref[0:1, :] += jnp.sum(jnp.where(is_g, dbg_v * bg_v, 0.0), axis=0, keepdims=True)
        acc_ref[1:2, :] += jnp.sum(jnp.where(is_g, da_raw, 0.0), axis=0, keepdims=True)

    tail = CAT_WIDTH - K_BA
    return _pc(body, "ba_bwd", (t // tm,),
               [_row(tm, 128), _row(tm, 128), pl.BlockSpec((tm, 128), lambda i: (i, bablk)),
                _const((1, 128)), _const((1, 128)), ANY],
               [pl.BlockSpec((tm, tail), lambda i: (i, K_BA // tail)), _const((8, 128))],
               [SDS(dproj.shape, dproj.dtype), SDS((8, 128), F32)],
               sem=("arbitrary",), aliases={5: 0})(dbg, bg, proj, al_row, dtb_row, dproj)


def _each(f, *lists):
    return [f(*a) for a in zip(*lists)]


def _rowsum(a):
    return jnp.sum(a, axis=1, keepdims=True)


def _chunk_terms(qs, ks, bgv, g_rows, hs):
    c = CHUNK
    ii = lax.broadcasted_iota(jnp.int32, (c, c), 0)
    jj = lax.broadcasted_iota(jnp.int32, (c, c), 1)
    lane = _lane(bgv.shape)
    incl = ii >= jj
    beta = [_rowsum(jnp.where(lane == h, bgv, 0.0)) for h in hs]
    g_col = [_rowsum(jnp.where(lane == HEADS + h, bgv, 0.0)) for h in hs]
    rq = _each(lambda q: lax.rsqrt(_rowsum(q * q) + L2_EPS), qs)
    rk = _each(lambda k: lax.rsqrt(_rowsum(k * k) + L2_EPS), ks)
    yq = _each(jnp.multiply, qs, rq)
    kn = _each(jnp.multiply, ks, rk)
    qn = _each(lambda a: a * Q_SCALE, yq)
    gc_col = _each(lambda g: _rowsum(jnp.where(jj <= ii, g, 0.0)), g_rows)
    gc_row = _each(lambda g: jnp.sum(jnp.where(ii <= jj, g, 0.0), axis=0, keepdims=True), g_col)
    dm = _each(lambda a, b: jnp.where(incl, jnp.exp(jnp.where(incl, a - b, 0.0)), 0.0), gc_col, gc_row)
    gl = _each(_rowsum, g_rows)
    eg = _each(jnp.exp, gc_col)
    ek = _each(lambda a, b: jnp.exp(a - b), gl, gc_col)
    egl = _each(jnp.exp, gl)
    kb = _each(jnp.multiply, kn, beta)
    kk = _each(_dot_nt, kb, kn)
    qk = _each(_dot_nt, qn, kn)
    m = _each(lambda a, b: jnp.where(ii > jj, a * b, 0.0), kk, dm)
    attn = _each(jnp.multiply, qk, dm)
    return dict(ii=ii, jj=jj, beta=beta, rq=rq, rk=rk, yq=yq, kn=kn, qn=qn, dm=dm, eg=eg, ek=ek,
                egl=egl, kb=kb, m=m, attn=attn)


def _unit_lower_inverse_minus_identity(ms, ii, jj):
    pair = (ii >> 1) == (jj >> 1)
    ys = _each(lambda m: -jnp.where(pair, m, 0.0), ms)
    s = 1
    while (1 << s) < CHUNK:
        mask = ((ii >> (s + 1)) == (jj >> (s + 1))) & ((ii >> s) != (jj >> s))
        lbs = _each(lambda m: jnp.where(mask, m, 0.0), ms)
        zs = _each(lambda y, lb: lb + _dot(y, lb), ys, lbs)
        ys = _each(lambda y, z: y - z - _dot(z, y), ys, zs)
        s += 1
    return ys


def _dn_fwd(qkv_act, bg, bgt):
    t = qkv_act.shape[0]
    nt = t // CHUNK
    c = CHUNK
    hs = list(range(HEADS))
    qo = [slice(h * HEAD_DIM, (h + 1) * HEAD_DIM) for h in hs]
    ko = [slice(DN_WIDTH + h * HEAD_DIM, DN_WIDTH + (h + 1) * HEAD_DIM) for h in hs]
    vo = [slice(2 * DN_WIDTH + h * HEAD_DIM, 2 * DN_WIDTH + (h + 1) * HEAD_DIM) for h in hs]

    def body(qkv_ref, bg_ref, bgt_ref, o_ref, u_ref, w_ref, qg_ref, kg_ref, attn_ref, y_ref, vn_ref, st_ref, egl_ref,
             s_ref):
        @pl.when(pl.program_id(0) == 0)
        def _():
            s_ref[...] = jnp.zeros_like(s_ref)

        bgv = bg_ref[...]
        qs = [qkv_ref[:, o] for o in qo]
        ks = [qkv_ref[:, o] for o in ko]
        vs = [qkv_ref[:, o] for o in vo]
        g_rows = [bgt_ref[pl.ds(HEADS + h, 1), :] for h in hs]
        ct = _chunk_terms(qs, ks, bgv, g_rows, hs)
        ys = _unit_lower_inverse_minus_identity(ct["m"], ct["ii"], ct["jj"])
        vb = _each(jnp.multiply, vs, ct["beta"])
        kbe = _each(jnp.multiply, ct["kb"], ct["eg"])
        us = _each(lambda a, y: a + _dot(y, a), vb, ys)
        ws = _each(lambda a, y: _mx(a + _dot(y, a)), kbe, ys)
        qg = _each(lambda a, b: _mx(a * b), ct["qn"], ct["eg"])
        kg = _each(lambda a, b: _mx(a * b), ct["kn"], ct["ek"])
        attn = _each(_mx, ct["attn"])
        ss = [s_ref[h] for h in hs]
        sb = _each(_mx, ss)
        vn = _each(lambda a, b, s_: a - _dot(b, s_), us, ws, sb)
        vnb = _each(_mx, vn)
        oa = _each(_dot, qg, sb)
        ob = _each(_dot, attn, vnb)
        upd = _each(_dot_tn, kg, vnb)
        for h, sl in enumerate(qo):
            u_ref[:, sl] = us[h]
            w_ref[:, sl] = ws[h]
            qg_ref[:, sl] = qg[h]
            kg_ref[:, sl] = kg[h]
            attn_ref[:, sl] = attn[h]
            y_ref[:, sl] = _mx(ys[h])
            egl_ref[0, h:h + 1, :] = jnp.broadcast_to(ct["egl"][h], (1, HEAD_DIM))
            st_ref[0, h] = ss[h]
            vn_ref[:, sl] = vnb[h]
            o_ref[:, sl] = oa[h] + ob[h]
            s_ref[h] = ss[h] * ct["egl"][h] + upd[h]

    wide = _row(c, DN_WIDTH)
    return _pc(body, "dn_fwd", (nt,),
               [_row(c, QKV_WIDTH), _row(c, 128), pl.BlockSpec((2 * HEADS, c), lambda i: (0, i))],
               [wide] * 8 + [pl.BlockSpec((1, HEADS, HEAD_DIM, HEAD_DIM), lambda i: (i, 0, 0, 0)),
                             pl.BlockSpec((1, HEADS, HEAD_DIM), lambda i: (i, 0, 0))],
               [SDS((t, DN_WIDTH), F32), SDS((t, DN_WIDTH), F32)] + [SDS((t, DN_WIDTH), MXU_DTYPE)] * 6
               + [SDS((nt, HEADS, HEAD_DIM, HEAD_DIM), F32), SDS((nt, HEADS, HEAD_DIM), F32)],
               scratch=[pltpu.VMEM((HEADS, HEAD_DIM, HEAD_DIM), F32)], sem=("arbitrary",))(qkv_act, bg, bgt)


def _dn_bwd(do, qkv_act, bg, bgt, u, w, qg, kg, attn, ymat, vn, states, egl):
    t = do.shape[0]
    nt = t // CHUNK
    c = CHUNK
    hs = list(range(HEADS))
    qo = [slice(h * HEAD_DIM, (h + 1) * HEAD_DIM) for h in hs]
    ko = [slice(DN_WIDTH + h * HEAD_DIM, DN_WIDTH + (h + 1) * HEAD_DIM) for h in hs]
    vo = [slice(2 * DN_WIDTH + h * HEAD_DIM, 2 * DN_WIDTH + (h + 1) * HEAD_DIM) for h in hs]

    def body(do_ref, qkv_ref, bg_ref, bgt_ref, u_ref, w_ref, qg_ref, kg_ref, attn_ref, y_ref, vn_ref, st_ref, egl_ref,
             dqkv_ref, dbg_ref, ds_ref):
        @pl.when(pl.program_id(0) == 0)
        def _():
            ds_ref[...] = jnp.zeros_like(ds_ref)

        dsp = [ds_ref[h] for h in hs]
        dsb = _each(_mx, dsp)
        ss = [st_ref[0, h] for h in hs]
        sb = _each(_mx, ss)
        du = [_dot(kg_ref[:, sl], b) + _dot_tn(attn_ref[:, sl], do_ref[:, sl]) for sl, b in zip(qo, dsb)]
        dub = _each(_mx, du)
        dkg_v = [_dot_nt(vn_ref[:, sl], b) for sl, b in zip(qo, dsb)]
        dqg_v = [_dot_nt(do_ref[:, sl], b) for sl, b in zip(qo, sb)]
        dattn_v = [_dot_nt(do_ref[:, sl], vn_ref[:, sl]) for sl in qo]
        dwv = [-_dot_nt(a, b) for a, b in zip(dub, sb)]
        upd = [_dot_tn(qg_ref[:, sl], do_ref[:, sl]) - _dot_tn(w_ref[:, sl], a) for sl, a in zip(qo, dub)]
        degl_v = [jnp.sum(_rowsum(a * b), axis=0, keepdims=True) for a, b in zip(ss, dsp)]
        for h in hs:
            ds_ref[h] = dsp[h] * egl_ref[0, h:h + 1, :] + upd[h]

        bgv = bg_ref[...]
        lane = _lane(bgv.shape)
        rowi = lax.broadcasted_iota(jnp.int32, (c, 1), 0)
        qs = [qkv_ref[:, o] for o in qo]
        ks = [qkv_ref[:, o] for o in ko]
        vs = [qkv_ref[:, o] for o in vo]
        g_rows = [bgt_ref[pl.ds(HEADS + h, 1), :] for h in hs]
        ct = _chunk_terms(qs, ks, bgv, g_rows, hs)
        ii, jj = ct["ii"], ct["jj"]
        beta, eg, ek, kb, kn, qn, dm = ct["beta"], ct["eg"], ct["ek"], ct["kb"], ct["kn"], ct["qn"], ct["dm"]
        ys = [y_ref[:, o] for o in qo]
        dvb = _each(lambda a, y: a + _dot_tn(y, a), du, ys)
        dkbe = _each(lambda a, y: a + _dot_tn(y, a), dwv, ys)
        dm_u = [_dot_nt(a, u_ref[:, o]) for a, o in zip(dvb, qo)]
        dm_w = [_dot_nt(a, w_ref[:, o]) for a, o in zip(dkbe, qo)]
        dms = _each(lambda a, b: jnp.where(ii > jj, -(a + b), 0.0), dm_u, dm_w)
        dkk = _each(jnp.multiply, dms, dm)
        dqk = _each(jnp.multiply, dattn_v, dm)
        gmat = _each(lambda a, b, c_, d: a * b + c_ * d, dms, ct["m"], dattn_v, ct["attn"])
        dkb = _each(lambda a, b, c_, d: _dot(a, b) + c_ * d, dkk, kn, dkbe, eg)
        dk1 = _each(_dot_tn, dkk, kb)
        dk2 = _each(_dot_tn, dqk, qn)
        dq1 = _each(_dot, dqk, kn)
        dk = _each(lambda a, b, c_, d: a + b + c_ * d, dk1, dk2, dkg_v, ek)
        dq = _each(lambda a, b, c_: a + b * c_, dq1, dqg_v, eg)
        deg = _each(lambda a, b, c_, d: _rowsum(a * b) + _rowsum(c_ * d), dqg_v, qn, dkbe, kb)
        dek = _each(lambda a, b: _rowsum(a * b), dkg_v, kn)
        dgl = _each(lambda a, b, c_, d: jnp.sum(a * b, axis=0, keepdims=True) + c_ * d, dek, ek, degl_v, ct["egl"])
        cs_row = _each(lambda g: jnp.sum(g, axis=0, keepdims=True), gmat)
        cs_col = _each(lambda r: _rowsum(jnp.where(ii == jj, r, 0.0)), cs_row)
        dgc = _each(lambda a, b, c_, d, g, e, f: a * b - c_ * d + _rowsum(g) - e + jnp.where(rowi == c - 1, f, 0.0),
                    deg, eg, dek, ek, gmat, cs_col, dgl)
        dgc_row = _each(lambda a: jnp.sum(jnp.where(ii == jj, a, 0.0), axis=0, keepdims=True), dgc)
        dg = _each(lambda r: _rowsum(jnp.where(jj >= ii, r, 0.0)), dgc_row)
        dbeta = _each(lambda a, b, c_, d: _rowsum(a * b) + _rowsum(c_ * d), dkb, kn, dvb, vs)
        dk = _each(lambda a, b, c_: a + b * c_, dk, dkb, beta)
        dbg = jnp.zeros((c, 128), F32)
        for h in hs:
            dyq = dq[h] * Q_SCALE
            yq = ct["yq"][h]
            dqkv_ref[:, qo[h]] = ct["rq"][h] * (dyq - yq * _rowsum(yq * dyq))
            dqkv_ref[:, ko[h]] = ct["rk"][h] * (dk[h] - kn[h] * _rowsum(kn[h] * dk[h]))
            dqkv_ref[:, vo[h]] = dvb[h] * beta[h]
            dbg = dbg + jnp.where(lane == h, dbeta[h], 0.0) + jnp.where(lane == HEADS + h, dg[h], 0.0)
        dbg_ref[...] = dbg

    rev = pl.BlockSpec((c, DN_WIDTH), lambda i: (nt - 1 - i, 0))
    return _pc(body, "dn_bwd", (nt,),
               [rev, pl.BlockSpec((c, QKV_WIDTH), lambda i: (nt - 1 - i, 0)),
                pl.BlockSpec((c, 128), lambda i: (nt - 1 - i, 0)), pl.BlockSpec((2 * HEADS, c), lambda i: (0, nt - 1 - i))]
               + [rev] * 7
               + [pl.BlockSpec((1, HEADS, HEAD_DIM, HEAD_DIM), lambda i: (nt - 1 - i, 0, 0, 0)),
                  pl.BlockSpec((1, HEADS, HEAD_DIM), lambda i: (nt - 1 - i, 0, 0))],
               [pl.BlockSpec((c, QKV_WIDTH), lambda i: (nt - 1 - i, 0)), pl.BlockSpec((c, 128), lambda i: (nt - 1 - i, 0))],
               [SDS((t, QKV_WIDTH), F32), SDS((t, 128), F32)],
               scratch=[pltpu.VMEM((HEADS, HEAD_DIM, HEAD_DIM), F32)],
               sem=("arbitrary",))(do, qkv_act, bg, bgt, u, w, qg, kg, attn, ymat, vn, states, egl)


MIX_ROWS = 64


def _mix_oproj_ln1(o, proj, ypre, pool_scale, wo_row, w_out, h0, g1, b1, tm):
    t = o.shape[0]

    def body(o_ref, z_ref, ga_ref, gb_ref, yp_ref, ps_ref, wo_ref, w_ref, h0_ref, g_ref, b_ref,
             mixed_ref, a1_ref, h1_ref, h1b_ref):
        for r in range(0, tm, MIX_ROWS):
            rows = pl.ds(r, MIX_ROWS)
            for h in range(HEADS):
                sl = slice(h * HEAD_DIM, (h + 1) * HEAD_DIM)
                oh = o_ref[rows, sl]
                on = oh * lax.rsqrt(jnp.mean(oh * oh, axis=1, keepdims=True) + RMS_EPS)
                zh = z_ref[rows, sl]
                yb = on * wo_ref[:, sl] * (zh * _sigmoid(zh))
                ya = yp_ref[rows, sl] * ps_ref[:, sl]
                mixed_ref[rows, sl] = _mx(_sigmoid(ga_ref[rows, sl]) * ya + _sigmoid(gb_ref[rows, sl]) * yb)
        a1 = ALPHA * h0_ref[...] + _dot(mixed_ref[...], w_ref[...])
        a1_ref[...] = a1
        xhat, _ = _ln_stats(a1)
        h1 = xhat * g_ref[...] + b_ref[...]
        h1_ref[...] = h1
        h1b_ref[...] = _mx(h1)

    def col(blk):
        return pl.BlockSpec((tm, D_MODEL), lambda i: (i, blk))

    r = _row(tm, D_MODEL)
    v = _const((1, D_MODEL))
    return _pc(body, "mix_oproj_ln1", (t // tm,),
               [r, col(K_Z // D_MODEL), col(K_GA // D_MODEL), col(K_GB // D_MODEL), r, v, v,
                _const((D_MODEL, D_MODEL)), r, v, v],
               [r, r, r, r],
               [SDS((t, D_MODEL), MXU_DTYPE), SDS((t, D_MODEL), F32), SDS((t, D_MODEL), F32),
                SDS((t, D_MODEL), MXU_DTYPE)],
               sem=("parallel",))(o, proj, proj, proj, ypre, pool_scale, wo_row, w_out, h0, g1, b1)


def _mix_bwd(da1_bf, w_out, o, proj, ypre, pool_scale, wo_row, tm, after):
    t = o.shape[0]

    def body(da_ref, wout_ref, o_ref, z_ref, ga_ref, gb_ref, yp_ref, ps_ref, wo_ref, after_ref,
             do_ref, dp_ref, dyp_ref, acc_ref, dm_ref):
        i = pl.program_id(0)

        @pl.when(i == 0)
        def _():
            acc_ref[...] = jnp.zeros_like(acc_ref)

        dm_ref[...] = _dot_nt(da_ref[...], wout_ref[...])
        dwo = jnp.zeros((1, HEAD_DIM), F32)
        for h in range(HEADS):
            sl = slice(h * HEAD_DIM, (h + 1) * HEAD_DIM)
            woh = wo_ref[:, sl]
            psh = ps_ref[:, sl]
            dps = jnp.zeros((1, HEAD_DIM), F32)
            for r in range(0, tm, MIX_ROWS):
                rows = pl.ds(r, MIX_ROWS)
                oh = o_ref[rows, sl]
                rs = lax.rsqrt(jnp.mean(oh * oh, axis=1, keepdims=True) + RMS_EPS)
                on = oh * rs
                zh = z_ref[rows, sl]
                sz = _sigmoid(zh)
                silu = zh * sz
                t1 = on * woh
                yb = t1 * silu
                sa = _sigmoid(ga_ref[rows, sl])
                sb = _sigmoid(gb_ref[rows, sl])
                yp = yp_ref[rows, sl]
                dm = dm_ref[rows, sl]
                ga_sl = slice(D_MODEL + h * HEAD_DIM, D_MODEL + (h + 1) * HEAD_DIM)
                gb_sl = slice(2 * D_MODEL + h * HEAD_DIM, 2 * D_MODEL + (h + 1) * HEAD_DIM)
                dp_ref[rows, ga_sl] = _mx(dm * (yp * psh) * sa * (1.0 - sa))
                dp_ref[rows, gb_sl] = _mx(dm * yb * sb * (1.0 - sb))
                dya = dm * sa
                dyb = dm * sb
                dyp_ref[rows, sl] = _mx(dya * psh)
                dps = dps + jnp.sum(dya * yp, axis=0, keepdims=True)
                dp_ref[rows, sl] = _mx(dyb * t1 * (sz * (1.0 + zh * (1.0 - sz))))
                dt1 = dyb * silu
                dwo = dwo + jnp.sum(dt1 * on, axis=0, keepdims=True)
                don = dt1 * woh
                do_ref[rows, sl] = _mx(rs * (don - on * jnp.mean(don * on, axis=1, keepdims=True)))
            acc_ref[0:1, sl] += dps
        acc_ref[1:2, 0:HEAD_DIM] += dwo

    def col(blk):
        return pl.BlockSpec((tm, D_MODEL), lambda i: (i, blk))

    r = _row(tm, D_MODEL)
    return _pc(body, "mix_bwd", (t // tm,),
               [r, _const((D_MODEL, D_MODEL)), r, col(K_Z // D_MODEL), col(K_GA // D_MODEL), col(K_GB // D_MODEL), r,
                _const((1, D_MODEL)), _const((1, D_MODEL)), ANY],
               [r, pl.BlockSpec((tm, 3 * D_MODEL), lambda i: (i, K_Z // (3 * D_MODEL))), r, _const((8, D_MODEL))],
               [SDS((t, D_MODEL), MXU_DTYPE), SDS((t, CAT_WIDTH), MXU_DTYPE), SDS((t, D_MODEL), MXU_DTYPE),
                SDS((8, D_MODEL), F32)],
               scratch=[pltpu.VMEM((tm, D_MODEL), F32)],
               sem=("arbitrary",))(da1_bf, w_out, o, proj, proj, proj, ypre, pool_scale, wo_row, after)


def _mlp_up(h1_bf, w_up, tm):
    t = h1_bf.shape[0]
    tn = w_up.shape[2]

    def body(h_ref, w_ref, act_ref):
        r = jnp.maximum(_dot(h_ref[...], w_ref[...]), 0.0)
        act_ref[...] = _mx(r * r)

    return _pc(body, "mlp_up", (D_FF // tn, t // tm),
               [pl.BlockSpec((tm, D_MODEL), lambda j, i: (i, 0)),
                pl.BlockSpec((None, D_MODEL, tn), lambda j, i: (j, 0, 0))],
               pl.BlockSpec((tm, tn), lambda j, i: (i, j)), SDS((t, D_FF), MXU_DTYPE),
               sem=("parallel", "parallel"))(h1_bf, w_up)


def _tail(act, w_down, h1, w_gate, p, w_proj, tgt, g2, b2, tm):
    t = act.shape[0]

    def body(act_ref, wd_ref, h1_ref, wg_ref, p_ref, wp_ref, tgt_ref, g_ref, b_ref,
             dr_ref, drb_ref, dgp_ref, dpp_ref, rb_ref, acc_ref):
        i = pl.program_id(0)

        @pl.when(i == 0)
        def _():
            acc_ref[...] = jnp.zeros_like(acc_ref)

        r = ALPHA * h1_ref[...] + _dot(act_ref[...], wd_ref[...])
        rb = _mx(r)
        rb_ref[...] = rb
        gate = _sigmoid(_dot(rb, wg_ref[...]))
        pp = _dot(p_ref[...], wp_ref[...])
        xhat, rstd = _ln_stats(r + gate * pp)
        g = g_ref[...]
        diff = xhat * g + b_ref[...] - tgt_ref[...]
        dh2 = diff * (1.0 / D_MODEL)
        rowloss = jnp.sum(diff * diff, axis=1, keepdims=True) * (0.5 / D_MODEL)
        acc_ref[0:1, :] += jnp.sum(dh2 * xhat, axis=0, keepdims=True)
        acc_ref[1:2, :] += jnp.sum(dh2, axis=0, keepdims=True)
        acc_ref[2:3, :] += jnp.broadcast_to(jnp.sum(rowloss, axis=0, keepdims=True), (1, D_MODEL))
        da2 = _ln_bwd(dh2, xhat, rstd, g)
        dpp_ref[...] = _mx(da2 * gate)
        dgp = _mx(da2 * pp * gate * (1.0 - gate))
        dgp_ref[...] = dgp
        dr = da2 + _dot_nt(dgp, wg_ref[...])
        dr_ref[...] = dr
        drb_ref[...] = _mx(dr)

    r = _row(tm, D_MODEL)
    v = _const((1, D_MODEL))
    return _pc(body, "tail", (t // tm,),
               [_row(tm, D_FF), _const((D_FF, D_MODEL)), r, _const((D_MODEL, D_MODEL)), _row(tm, PLE_DIM),
                _const((PLE_DIM, D_MODEL)), r, v, v],
               [r, r, r, r, r, _const((8, D_MODEL))],
               [SDS((t, D_MODEL), F32)] + [SDS((t, D_MODEL), MXU_DTYPE)] * 4 + [SDS((8, D_MODEL), F32)],
               sem=("arbitrary",))(act, w_down, h1, w_gate, p, w_proj, tgt, g2, b2)


SQRT_GUARD = 1e-30


def _mlp_bwd1(dr_bf, w_down, act, tm, tn):
    t = act.shape[0]

    def body(dr_ref, w_ref, act_ref, dup_ref):
        dact = _dot_nt(dr_ref[...], w_ref[...])
        a = act_ref[...].astype(F32)
        dup_ref[...] = _mx(dact * (2.0 * a * lax.rsqrt(a + SQRT_GUARD)))

    o = pl.BlockSpec((tm, tn), lambda j, i: (i, j))
    return _pc(body, "mlp_bwd1", (D_FF // tn, t // tm),
               [pl.BlockSpec((tm, D_MODEL), lambda j, i: (i, 0)), pl.BlockSpec((tn, D_MODEL), lambda j, i: (j, 0)), o],
               o, SDS((t, D_FF), MXU_DTYPE), sem=("parallel", "parallel"))(dr_bf, w_down, act)


def _mlp_bwd2(dup, w_up, dr, a1, g1, tm):
    t = dr.shape[0]

    nk, tk = w_up.shape[0], w_up.shape[2]

    def body(dup_ref, w_ref, dr_ref, a1_ref, g_ref, da1_ref, da1b_ref, acc_ref):
        i = pl.program_id(0)

        @pl.when(i == 0)
        def _():
            acc_ref[...] = jnp.zeros_like(acc_ref)

        dh1 = ALPHA * dr_ref[...]
        for kk in range(nk):
            dh1 = dh1 + _dot_nt(dup_ref[:, kk * tk:(kk + 1) * tk], w_ref[kk])
        xhat, rstd = _ln_stats(a1_ref[...])
        acc_ref[0:1, :] += jnp.sum(dh1 * xhat, axis=0, keepdims=True)
        acc_ref[1:2, :] += jnp.sum(dh1, axis=0, keepdims=True)
        da1 = _ln_bwd(dh1, xhat, rstd, g_ref[...])
        da1_ref[...] = da1
        da1b_ref[...] = _mx(da1)

    r = _row(tm, D_MODEL)
    return _pc(body, "mlp_bwd2", (t // tm,),
               [_row(tm, D_FF), _const((nk, D_MODEL, tk)), r, r, _const((1, D_MODEL))],
               [r, r, _const((8, D_MODEL))],
               [SDS((t, D_MODEL), F32), SDS((t, D_MODEL), MXU_DTYPE), SDS((8, D_MODEL), F32)],
               sem=("arbitrary",))(dup, w_up, dr, a1, g1)


def _ln_in_bwd(dproj, w_cat, da1, x, g, tm, after):
    t = x.shape[0]

    def body(dp_ref, w_ref, da1_ref, x_ref, g_ref, after_ref, dx_ref, acc_ref):
        i = pl.program_id(0)

        @pl.when(i == 0)
        def _():
            acc_ref[...] = jnp.zeros_like(acc_ref)

        dh0 = _dot_nt(dp_ref[...], w_ref[...]) + ALPHA * da1_ref[...]
        xhat, rstd = _ln_stats(x_ref[...])
        acc_ref[0:1, :] += jnp.sum(dh0 * xhat, axis=0, keepdims=True)
        acc_ref[1:2, :] += jnp.sum(dh0, axis=0, keepdims=True)
        dx_ref[...] = _ln_bwd(dh0, xhat, rstd, g_ref[...])

    r = _row(tm, D_MODEL)
    return _pc(body, "ln_in_bwd", (t // tm,),
               [_row(tm, CAT_WIDTH), _const((D_MODEL, CAT_WIDTH)), r, r, _const((1, D_MODEL)), ANY],
               [r, _const((8, D_MODEL))], [SDS((t, D_MODEL), F32), SDS((8, D_MODEL), F32)],
               sem=("arbitrary",))(dproj, w_cat, da1, x, g, after)


def _local_step(x, p, tgt, wts, start_token, first_weights, late_weights, send_late_grads, send_early_grads):
    t = x.shape[0]
    tm = min(512, t)
    tms = min(256, t)
    row = lambda a: a.reshape(1, -1)
    pool_scale = row(wts["pool_scale"])
    wo_row = jnp.tile(row(wts["o_norm_w"]), (1, HEADS))
    pad8 = jnp.zeros((1, HEADS), F32)
    al_row = jnp.concatenate([pad8, row(wts["a_log"]), jnp.zeros((1, 128 - 2 * HEADS), F32)], axis=1)
    dtb_row = jnp.concatenate([pad8, row(wts["dt_bias"]), jnp.zeros((1, 128 - 2 * HEADS), F32)], axis=1)
    g_in, b_in = row(wts["ln_in_g"]), row(wts["ln_in_b"])
    g1, b1 = row(wts["ln1_g"]), row(wts["ln1_b"])
    g2, b2 = row(wts["ln2_g"]), row(wts["ln2_b"])

    h0, h0_bf = _ln_in(x, g_in, b_in, tm, start_token)
    first, first_token = first_weights(h0_bf)
    wts = {**wts, **first}
    w_cat = wts["w_cat"]
    proj, qkv_act, dsilu = _proj_conv(h0_bf, w_cat, wts["conv_w"], tms, first_token)
    ypre, d_bf = _pool_fwd(proj, wts["pool_w"], tm)
    bg = _ba_fwd(proj, al_row, dtb_row, tm)
    bgt = bg[:, :2 * HEADS].T
    o, u, w, qg, kg, attn, ymat, vn, states, egl = _dn_fwd(qkv_act, bg, bgt)
    wts = {**wts, **late_weights(o)}
    mixed, a1, h1, h1_bf = _mix_oproj_ln1(o, proj, ypre, pool_scale, wo_row, wts["w_out"], h0, g1, b1, tms)
    act = _mlp_up(h1_bf, wts["w_up"], tm)
    dr, dr_bf, dgp, dpp, r_bf, acc_tail = _tail(act, wts["w_down"], h1, wts["ple_gate_w"], p, wts["ple_proj_w"],
                                                tgt, g2, b2, tms)
    grads = {}
    grads["ple_proj_w"] = _matmul(p, dpp, "tn", "dw_ple_proj", WIRE_DTYPE, tm=256, tn=1024, tk=DW_TK)
    grads["ple_gate_w"] = _matmul(r_bf, dgp, "tn", "dw_ple_gate", WIRE_DTYPE, tm=DW_TM, tn=1024, tk=DW_TK)
    grads["w_down"] = _matmul(act, dr_bf, "tn", "dw_down", WIRE_DTYPE, tm=DW_TM, tn=1024, tk=DW_TK)
    dup = _mlp_bwd1(dr_bf, wts["w_down"], act, tm, 1024)
    grads["w_up"] = _matmul(h1_bf, dup, "tn", "dw_up", WIRE_DTYPE, tm=DW_TM, tn=1024, tk=DW_TK, stack_out=True)
    da1, da1_bf, acc_ln1 = _mlp_bwd2(dup, wts["w_up"], dr, a1, g1, tms)
    grads["w_out"] = _matmul(mixed, da1_bf, "tn", "dw_out", WIRE_DTYPE, tm=DW_TM, tn=1024, tk=DW_TK)
    sent = send_late_grads(grads)
    do, dproj, dyp, acc_mix = _mix_bwd(da1_bf, wts["w_out"], o, proj, ypre, pool_scale, wo_row, tms, sent)
    dproj, grads["pool_w"] = _pool_bwd(dyp, d_bf, wts["pool_w"], dproj, tm)
    dqkv_act, dbg = _dn_bwd(do, qkv_act, bg, bgt, u, w, qg, kg, attn, ymat, vn, states, egl)
    dproj, acc_conv = _conv_bwd(dqkv_act, dsilu, proj, wts["conv_w"], dproj, tm)
    dproj, acc_ba = _ba_bwd(dbg, bg, proj, al_row, dtb_row, dproj, tm)
    dw_cat = _matmul(h0_bf, dproj, "tn", "dw_in", WIRE_DTYPE, tm=DW_TM, tn=1152, tk=DW_TK)
    grads["w_in"] = jnp.concatenate(
        [dw_cat[:, K_U:K_U + 512], dw_cat[:, K_QKV:K_QKV + 3072], dw_cat[:, K_Z:K_Z + 1024],
         dw_cat[:, K_BA:K_BA + 16], dw_cat[:, K_GA:K_GA + 1024], dw_cat[:, K_GB:K_GB + 1024]], axis=1)
    sent = send_early_grads(grads)
    grad_x, acc_in = _ln_in_bwd(dproj, w_cat, da1, x, g_in, tms, sent)

    grads["conv_w"] = acc_conv[0:CONV_K]
    grads["ln_in_g"], grads["ln_in_b"] = acc_in[0], acc_in[1]
    grads["ln1_g"], grads["ln1_b"] = acc_ln1[0], acc_ln1[1]
    grads["ln2_g"], grads["ln2_b"] = acc_tail[0], acc_tail[1]
    grads["pool_scale"] = acc_mix[0]
    grads["o_norm_w"] = acc_mix[1, 0:HEAD_DIM]
    grads["a_log"] = acc_ba[0, HEADS:2 * HEADS]
    grads["dt_bias"] = acc_ba[1, HEADS:2 * HEADS]
    loss = acc_tail[2, 0]
    return grad_x, grads, loss


MESH = pl.DeviceIdType.MESH
ANY = pl.BlockSpec(memory_space=pl.ANY)


def _chip_of(k, x, y):
    chip = (2 * x + y + k) % N_CHIPS
    return chip // 2, chip % 2


def _place():
    x, y, c = lax.axis_index("x"), lax.axis_index("y"), lax.axis_index("c")
    return x, y, c, 2 * x + y


def _half(rows, c):
    return pl.ds(pl.multiple_of(c * (rows // 2), 16), rows // 2)


def _remote(src, dst, send_sem, recv_sem, device_id):
    return pltpu.make_async_remote_copy(src_ref=src, dst_ref=dst, send_sem=send_sem, recv_sem=recv_sem,
                                        device_id=device_id, device_id_type=MESH)


def _tile_rows(rows):
    for tr in (256, 128, 64, 32, 16):
        if rows % tr == 0:
            return tr
    raise ValueError(rows)


def _first_gather_copies(srcs, lands, send, recv, place):
    copies = []
    for a in range(len(srcs)):
        whole = a == len(srcs) - 1
        for k in range(N_CHIPS):
            if place is None:
                copies.append(None)
                continue
            x, y, c, me = place
            sems = (send.at[a * N_CHIPS + k], recv.at[a * N_CHIPS + k])
            if k == 0:
                copies.append(_remote(srcs[a], lands[a].at[me], *sems, (x, y, 1 - c)))
                continue
            tx, ty = _chip_of(k, x, y)
            if whole:
                copies.append(_remote(srcs[a], lands[a].at[me], *sems, (tx, ty, c)))
            else:
                mine = _half(srcs[a].shape[0], c)
                copies.append(_remote(srcs[a].at[mine], lands[a].at[me, mine], *sems, (tx, ty, c)))
    return copies


def _pass_halves(stacks):
    n = len(stacks)

    def body(*refs):
        outs = refs[n:2 * n]
        send, recv = refs[2 * n:]
        x, y, c, me = _place()
        copies = []
        for a in range(n):
            for k in range(1, N_CHIPS):
                landed = outs[a].at[(me + N_CHIPS - k) % N_CHIPS, _half(stacks[a].shape[1], c)]
                copies.append(_remote(landed, landed, send.at[a * N_CHIPS + k], recv.at[a * N_CHIPS + k],
                                      (x, y, 1 - c)))
        for cp in copies:
            cp.start()
        for cp in copies:
            cp.wait_send()
        for a in range(n):
            for k in range(1, N_CHIPS):
                passed = outs[a].at[(me + N_CHIPS - k) % N_CHIPS, _half(stacks[a].shape[1], 1 - c)]
                _remote(passed, passed, send.at[a * N_CHIPS + k], recv.at[a * N_CHIPS + k], (x, y, c)).wait_recv()

    sems = pltpu.SemaphoreType.DMA((n * N_CHIPS,))
    return pl.pallas_call(
        body, name="pass_halves", out_shape=[SDS(s.shape, s.dtype) for s in stacks],
        in_specs=[ANY] * n, out_specs=[ANY] * n, scratch_shapes=[sems, sems],
        input_output_aliases={a: a for a in range(n)},
    )(*stacks)


def _swap_halves(gs):
    n = len(gs)

    def body(*refs):
        ins, theirs = refs[0:n], refs[n:2 * n]
        send, recv = refs[2 * n:]
        x, y, c, _ = _place()
        copies = [_remote(ins[a].at[:, _half(gs[a].shape[1], 1 - c)], theirs[a], send.at[a], recv.at[a],
                          (x, y, 1 - c)) for a in range(n)]
        for cp in copies:
            cp.start()
        for cp in copies:
            cp.wait()

    return pl.pallas_call(
        body, name="swap_halves", out_shape=[SDS((N_CHIPS, g.shape[1] // 2, g.shape[2]), g.dtype) for g in gs],
        in_specs=[ANY] * n, out_specs=[ANY] * n, scratch_shapes=[pltpu.SemaphoreType.DMA((n,))] * 2,
    )(*gs)


def _send_to_sibling(hs):
    n = len(hs)

    def body(*refs):
        ins, outs = refs[0:n], refs[n:2 * n]
        send, recv = refs[2 * n:]
        x, y, c, _ = _place()
        copies = [_remote(ins[a], outs[a], send.at[a], recv.at[a], (x, y, 1 - c)) for a in range(n)]
        for cp in copies:
            cp.start()
        for cp in copies:
            cp.wait()

    return pl.pallas_call(
        body, name="send_to_sibling", out_shape=[SDS(h.shape, h.dtype) for h in hs],
        in_specs=[ANY] * n, out_specs=[ANY] * n, scratch_shapes=[pltpu.SemaphoreType.DMA((n,))] * 2,
    )(*hs)


HBM = pl.BlockSpec(memory_space=pltpu.HBM)
SEM = pl.BlockSpec(memory_space=pltpu.SEMAPHORE)
EFFECT = pltpu.SideEffectType.DATAFLOW_SIDE_EFFECTING


def _in_hbm(a):
    return pltpu.with_memory_space_constraint(a, pltpu.HBM)


def _split_copy_start(name, srcs, lands, copies_of, after):
    n = len(srcs)
    n_copies = len(copies_of(srcs, lands, None, None, None))

    def body(*refs):
        src_refs, land_refs = refs[0:n], refs[n:2 * n]
        send, recv = refs[2 * n + 1], refs[2 * n + 2]
        token = refs[-1]
        for cp in copies_of(src_refs, land_refs, send, recv, _place()):
            cp.start()
        token[...] = jnp.zeros_like(token)

    sems = pltpu.SemaphoreType.DMA((n_copies,))
    out = pl.pallas_call(
        body, name=name,
        out_shape=[sems, sems] + [pltpu.HBM(a.shape, a.dtype) for a in list(srcs) + list(lands)] + [SDS((8, 128), F32)],
        in_specs=[HBM] * (2 * n) + [ANY],
        out_specs=[SEM, SEM] + [HBM] * (2 * n) + [pl.BlockSpec(memory_space=pltpu.VMEM)],
        input_output_aliases={i: 2 + i for i in range(2 * n)},
        compiler_params=pltpu.CompilerParams(has_side_effects=EFFECT),
    )(*[_in_hbm(a) for a in list(srcs) + list(lands)], after)
    return out[0], out[1], out[2:2 + n], out[2 + n:2 + 2 * n], out[-1]


def _split_copy_wait(name, send, recv, srcs, lands, after, copies_of):
    n = len(srcs)
    after = list(after) if isinstance(after, (list, tuple)) else [after]

    def body(*refs):
        src_refs, land_refs = refs[0:n], refs[n:2 * n]
        send_ref, recv_ref = refs[2 * n], refs[2 * n + 1]
        for cp in copies_of(src_refs, land_refs, send_ref, recv_ref, _place()):
            cp.wait_send()
            cp.wait_recv()

    out = pl.pallas_call(
        body, name=name, out_shape=[pltpu.HBM(a.shape, a.dtype) for a in list(srcs) + list(lands)],
        in_specs=[HBM] * (2 * n) + [SEM, SEM] + [ANY] * len(after), out_specs=[HBM] * (2 * n),
        input_output_aliases={i: i for i in range(2 * n)},
        compiler_params=pltpu.CompilerParams(has_side_effects=EFFECT),
    )(*srcs, *lands, send, recv, *after)
    return out[0:n], out[n:2 * n]


def _late_gather_copies(srcs, lands, send, recv, place):
    copies = []
    for a in range(len(srcs)):
        for k in range(N_CHIPS):
            if place is None:
                copies.append(None)
                continue
            x, y, c, me = place
            if k == 0:
                target = (x, y, 1 - c)
            else:
                tx, ty = _chip_of(k, x, y)
                target = (tx, ty, c)
            copies.append(_remote(srcs[a], lands[a].at[me], send.at[a * N_CHIPS + k], recv.at[a * N_CHIPS + k], target))
    return copies


def _late_scatter_copies(srcs, lands, send, recv, place):
    copies = []
    for a in range(len(srcs)):
        for k in range(1, N_CHIPS):
            if place is None:
                copies.append(None)
                continue
            x, y, c, _ = place
            tx, ty = _chip_of(k, x, y)
            copies.append(_remote(srcs[a].at[2 * tx + ty], lands[a].at[k - 1], send.at[a * (N_CHIPS - 1) + k - 1],
                                  recv.at[a * (N_CHIPS - 1) + k - 1], (tx, ty, c)))
    return copies


def _add_pair(g, theirs, name):
    _, rows, cols = g.shape
    half = rows // 2
    tr = _tile_rows(half)

    def body(g_ref, t_ref, o_ref):
        own = g_ref[lax.axis_index("c")]
        o_ref[...] = (own.astype(F32) + t_ref[...].astype(F32)).astype(o_ref.dtype)

    blk = pl.BlockSpec((None, tr, cols), lambda j, i: (j, i, 0))
    return _pc(body, "add_" + name, (N_CHIPS, half // tr),
               [pl.BlockSpec((None, 2, tr, cols), lambda j, i: (j, 0, i, 0)), blk], blk,
               SDS((N_CHIPS, half, cols), g.dtype), sem=("parallel", "parallel"))(
                   g.reshape(N_CHIPS, 2, half, cols), theirs)


def _sum_slabs(pair, landed, name):
    _, rows, cols = pair.shape
    tr = _tile_rows(rows)

    def body(p_ref, r_ref, o_ref):
        acc = p_ref[2 * lax.axis_index("x") + lax.axis_index("y")].astype(F32)
        for k in range(N_CHIPS - 1):
            acc = acc + r_ref[k].astype(F32)
        o_ref[...] = acc

    return _pc(body, "sum_" + name, (rows // tr,),
               [pl.BlockSpec((N_CHIPS, tr, cols), lambda i: (0, i, 0)),
                pl.BlockSpec((N_CHIPS - 1, tr, cols), lambda i: (0, i, 0))],
               _row(tr, cols), SDS((rows, cols), F32), sem=("parallel",))(pair, landed)


def _adamw_math(w, g, m, v):
    m = ADAM_B1 * m + (1.0 - ADAM_B1) * g
    v = ADAM_B2 * v + (1.0 - ADAM_B2) * (g * g)
    m_hat = m / (1.0 - ADAM_B1 ** ADAM_STEP)
    v_hat = v / (1.0 - ADAM_B2 ** ADAM_STEP)
    delta = -ADAM_LR * (m_hat / (jnp.sqrt(v_hat) + ADAM_EPS) + ADAM_WD * w)
    return delta, m, v


def _adamw_2d(w, g_own, g_sib, m, v, name, halves):
    lead = w.ndim == 3
    rows, cols = w.shape[-2:]
    tr = _tile_rows(rows // 2)
    nh = rows // 2 // tr if halves else rows // tr

    def body(w_ref, go_ref, gs_ref, m_ref, v_ref, g_out, d_out, m_out, v_out):
        if halves:
            mine = (pl.program_id(0) // nh) == lax.axis_index("c")
            g = jnp.where(mine, go_ref[...], gs_ref[...])
        else:
            g = go_ref[...] + gs_ref[...]
        delta, mn, vn = _adamw_math(w_ref[...], g, m_ref[...], v_ref[...])
        g_out[...] = g
        d_out[...] = delta
        m_out[...] = mn
        v_out[...] = vn

    r = _row(tr, cols)
    p = pl.BlockSpec((None, tr, cols), lambda i: (0, i, 0)) if lead else r
    h = pl.BlockSpec((tr, cols), lambda i: (i % nh, 0))
    return _pc(body, "adamw_" + name, (rows // tr,), [p, h, h, p, p], [r] * 4, [SDS((rows, cols), F32)] * 4,
               sem=("parallel",))(w, g_own, g_sib, m, v)


def _small_allreduce_adamw(mine, w, m, v, sizes):
    shape = mine.shape
    n = len(sizes)

    def body(mine_ref, w_ref, m_ref, v_ref, *rest):
        outs, (buf_ref, res_ref, send_sems, recv_sems) = rest[:-4], rest[-4:]
        x, y, c = lax.axis_index("x"), lax.axis_index("y"), lax.axis_index("c")
        me = 4 * x + 2 * y + c
        buf_ref[me] = mine_ref[...]
        copies = []
        for k in range(1, N_DEV):
            tgt = (me + k) % N_DEV
            copies.append(pltpu.make_async_remote_copy(
                src_ref=mine_ref, dst_ref=buf_ref.at[me], send_sem=send_sems.at[k], recv_sem=recv_sems.at[k],
                device_id=(tgt // 4, (tgt // 2) % 2, tgt % 2), device_id_type=MESH))
        for cp in copies:
            cp.start()
        for k in range(1, N_DEV):
            src = (me + N_DEV - k) % N_DEV
            pltpu.make_async_remote_copy(
                src_ref=mine_ref, dst_ref=buf_ref.at[src], send_sem=send_sems.at[k], recv_sem=recv_sems.at[k],
                device_id=(x, y, c), device_id_type=MESH).wait_recv()
        for cp in copies:
            cp.wait_send()
        g = buf_ref[0]
        for j in range(1, N_DEV):
            g = g + buf_ref[j]
        delta, mn, vn = _adamw_math(w_ref[...], g, m_ref[...], v_ref[...])
        for kind, val in enumerate((g, delta, mn, vn)):
            res_ref[kind] = val
            for i, size in enumerate(sizes):
                outs[kind * (n + 1) + i][...] = res_ref[kind, i:i + 1, 0:size]
            outs[kind * (n + 1) + n][...] = res_ref[kind, SMALL_CONV_AT:SMALL_CONV_AT + SMALL_CONV_ROWS, :]
        outs[-1][...] = res_ref[0, n:n + 1, 0:1]

    vm = pl.BlockSpec(memory_space=pltpu.VMEM)
    per_kind = [SDS((1, size), F32) for size in sizes] + [SDS((SMALL_CONV_ROWS, D_MODEL), F32)]
    out_shape = per_kind * 4 + [SDS((1, 1), F32)]
    out = pl.pallas_call(
        body, name="small_allreduce_adamw", out_shape=out_shape, in_specs=[vm] * 4, out_specs=[vm] * len(out_shape),
        scratch_shapes=[pltpu.VMEM((N_DEV,) + shape, F32), pltpu.VMEM((4,) + shape, F32),
                        pltpu.SemaphoreType.DMA((N_DEV,)), pltpu.SemaphoreType.DMA((N_DEV,))],
    )(mine, w, m, v)
    return [out[kind * (n + 1):(kind + 1) * (n + 1)] for kind in range(4)], out[-1]


def _as2d(a):
    return a.reshape(-1, a.shape[-1])


def _w_cat(stack):
    wi = stack.transpose(1, 0, 2).reshape(D_MODEL, IN_WIDTH)
    return jnp.concatenate(
        [wi[:, C_QKV:C_Z], wi[:, C_Z:C_BETA], wi[:, C_GA:C_GB], wi[:, C_GB:IN_WIDTH], wi[:, C_POOL:C_QKV],
         wi[:, C_BETA:C_GA], jnp.zeros((D_MODEL, CAT_WIDTH - K_BA - 2 * HEADS), wi.dtype)], axis=1)


WEIGHT_LAYOUT = {
    "w_in": lambda s: ("w_cat", _w_cat(s)),
    "pool_w": lambda s: ("pool_w", s.reshape(N_CHIPS, 4, POOL_GROUP, POOL_OUT_GROUP // N_CHIPS)
                         .transpose(1, 2, 0, 3).reshape(4, POOL_GROUP, POOL_OUT_GROUP)),
    "w_out": lambda s: ("w_out", s.reshape(D_MODEL, D_MODEL)),
    "w_up": lambda s: ("w_up", s),
    "w_down": lambda s: ("w_down", s.reshape(D_FF, D_MODEL)),
    "ple_gate_w": lambda s: ("ple_gate_w", s.reshape(D_MODEL, D_MODEL)),
    "ple_proj_w": lambda s: ("ple_proj_w", s.transpose(1, 0, 2).reshape(PLE_DIM, D_MODEL)),
}

GRAD_LAYOUT = {
    "w_in": lambda g: g.reshape(D_MODEL, N_CHIPS, IN_WIDTH // N_CHIPS).transpose(1, 0, 2),
    "pool_w": lambda g: g.reshape(4, POOL_GROUP, N_CHIPS, POOL_OUT_GROUP // N_CHIPS)
                         .transpose(2, 0, 1, 3).reshape(N_CHIPS, 4 * POOL_GROUP, POOL_OUT_GROUP // N_CHIPS),
    "w_out": lambda g: g.reshape(N_CHIPS, D_MODEL // N_CHIPS, D_MODEL),
    "w_up": lambda g: g,
    "w_down": lambda g: g.reshape(N_CHIPS, D_FF // N_CHIPS, D_MODEL),
    "ple_gate_w": lambda g: g.reshape(N_CHIPS, D_MODEL // N_CHIPS, D_MODEL),
    "ple_proj_w": lambda g: g.reshape(PLE_DIM, N_CHIPS, D_MODEL // N_CHIPS).transpose(1, 0, 2),
}


def _full_weights(names, stacks):
    return dict(WEIGHT_LAYOUT[n](s.astype(MXU_DTYPE)) for n, s in zip(names, stacks))


def _grads_by_chip(names, grads):
    return [GRAD_LAYOUT[n](grads[n]).astype(WIRE_DTYPE) for n in names]


def _pack_small(rows, conv, name):
    n = len(rows)

    def body(*refs):
        out = refs[n + 1]
        out[...] = jnp.zeros_like(out)
        for i in range(n):
            out[i:i + 1, :] = refs[i][...]
        out[SMALL_CONV_AT:SMALL_CONV_AT + SMALL_CONV_ROWS, :] = refs[n][...]

    vm = pl.BlockSpec(memory_space=pltpu.VMEM)
    return pl.pallas_call(body, name=name, out_shape=SDS((SMALL_CONV_AT + SMALL_CONV_ROWS, D_MODEL), F32),
                          in_specs=[vm] * (n + 1), out_specs=vm)(*rows, conv)


def _pad_row(a):
    a = a.reshape(1, -1).astype(F32)
    return jnp.pad(a, ((0, 0), (0, D_MODEL - a.shape[1])))


def kernel(x, p, ln_in_g, ln_in_b, w_in, pool_w, pool_scale, conv_w, a_log, dt_bias, o_norm_w, w_out, ln1_g, ln1_b, w_up, w_down, ple_gate_w, ple_proj_w, ln2_g, ln2_b, loss_target, m_ln_in_g, m_ln_in_b, m_w_in, m_pool_w, m_pool_scale, m_conv_w, m_a_log, m_dt_bias, m_o_norm_w, m_w_out, m_ln1_g, m_ln1_b, m_w_up, m_w_down, m_ple_gate_w, m_ple_proj_w, m_ln2_g, m_ln2_b, v_ln_in_g, v_ln_in_b, v_w_in, v_pool_w, v_pool_scale, v_conv_w, v_a_log, v_dt_bias, v_o_norm_w, v_w_out, v_ln1_g, v_ln1_b, v_w_up, v_w_down, v_ple_gate_w, v_ple_proj_w, v_ln2_g, v_ln2_b):
    given = dict(locals())
    chip = 2 * lax.axis_index("x") + lax.axis_index("y")

    shard = lambda n: _as2d(given[n]).astype(WIRE_DTYPE)

    wts = {"ln_in_g": ln_in_g, "ln_in_b": ln_in_b, "pool_scale": pool_scale[0], "a_log": a_log[0],
           "dt_bias": dt_bias[0], "o_norm_w": o_norm_w[0], "ln1_g": ln1_g[0], "ln1_b": ln1_b[0],
           "ln2_g": ln2_g[0], "ln2_b": ln2_b[0]}

    conv_pad = jnp.pad(conv_w[0], ((0, 8 - CONV_K), (0, 0)))
    first_srcs = [shard(n) for n in EARLY] + [conv_pad]
    first_lands = [lax.empty((N_CHIPS,) + s.shape, s.dtype) for s in first_srcs]
    fsend, frecv, fsrcs, flands, start_token = _split_copy_start(
        "first_gather_start", first_srcs, first_lands, _first_gather_copies, first_srcs[0])
    late = {}
    for n in ("w_in", "m_w_in", "v_w_in"):
        given[n], _ = lax.optimization_barrier((given[n], start_token))

    def first_weights(after):
        _, lands = _split_copy_wait("first_gather_wait", fsend, frecv, fsrcs, flands,
                                    [after, given["w_in"], given["m_w_in"], given["v_w_in"]], _first_gather_copies)
        stacks = _pass_halves(lands[0:len(EARLY)])
        first = _full_weights(EARLY, stacks)
        first["conv_w"] = jnp.concatenate([lands[len(EARLY)][j, 0:CONV_K] for j in range(N_CHIPS)], axis=1)
        late_srcs = [shard(n) for n in LATE]
        late_lands = [lax.empty((N_CHIPS,) + s.shape, s.dtype) for s in late_srcs]
        late["send"], late["recv"], late["srcs"], late["lands"], token = _split_copy_start(
            "late_gather_start", late_srcs, late_lands, _late_gather_copies, stacks[0])
        return first, token

    def late_weights(after):
        _, stacks = _split_copy_wait("late_gather_wait", late["send"], late["recv"], late["srcs"], late["lands"],
                                     after, _late_gather_copies)
        return _full_weights(LATE, stacks)

    scatter = {}

    def send_late_grads(grads):
        srcs = _grads_by_chip(LATE, grads)
        lands = [lax.empty((N_CHIPS - 1,) + g.shape[1:], g.dtype) for g in srcs]
        scatter["send"], scatter["recv"], scatter["srcs"], scatter["lands"], token = _split_copy_start(
            "late_scatter_start", srcs, lands, _late_scatter_copies, srcs[0])
        return token

    last = {}

    def send_early_grads(grads):
        by_chip = _grads_by_chip(EARLY, grads)
        theirs = _swap_halves(by_chip)
        pair = [_add_pair(g, t, n) for g, t, n in zip(by_chip, theirs, EARLY)]
        lands = [lax.empty((N_CHIPS - 1,) + q.shape[1:], q.dtype) for q in pair]
        last["send"], last["recv"], last["srcs"], last["lands"], token = _split_copy_start(
            "early_scatter_start", pair, lands, _late_scatter_copies, pair[0])
        return token

    grad_x, grads, loss = _local_step(x[0], p[0, 0], loss_target[0], wts, start_token, first_weights, late_weights,
                                      send_late_grads, send_early_grads)

    late_mine, late_landed = _split_copy_wait("late_scatter_wait", scatter["send"], scatter["recv"], scatter["srcs"],
                                              scatter["lands"], grad_x, _late_scatter_copies)
    late_part = [_sum_slabs(q, r, n) for q, r, n in zip(late_mine, late_landed, LATE)]
    pair, landed = _split_copy_wait("early_scatter_wait", last["send"], last["recv"], last["srcs"], last["lands"],
                                    grad_x, _late_scatter_copies)
    reduced = [_sum_slabs(q, r, n) for q, r, n in zip(pair, landed, EARLY)]
    from_sibling = _send_to_sibling(reduced + late_part)
    big_out = {}
    for n, g_own, g_sib in zip(EARLY + LATE, reduced + late_part, from_sibling):
        view = (lambda a: a) if given[n].ndim == 3 else _as2d
        res = _adamw_2d(view(given[n]), g_own, g_sib, view(given["m_" + n]), view(given["v_" + n]), n,
                        halves=n in EARLY)
        big_out[n] = [r.reshape(given[n].shape) for r in res]

    conv_cols = QKV_WIDTH // N_CHIPS

    def small_pack(get, conv, extra, name):
        if conv.shape[1] != QKV_WIDTH:
            conv = lax.dynamic_update_slice(jnp.zeros((CONV_K, QKV_WIDTH), F32), conv, (0, chip * conv_cols))
        return _pack_small([_pad_row(get(n)) for n in SMALL_NAMES] + extra, conv.reshape(SMALL_CONV_ROWS, D_MODEL), name)

    mine_small = small_pack(lambda n: grads[n], grads["conv_w"], [jnp.full((1, D_MODEL), loss, F32)], "pack_small_g")
    packed_small = [small_pack(lambda n: given[prefix + n], given[prefix + "conv_w"][0], [], "pack_small_" + tag)
                    for prefix, tag in (("", "w"), ("m_", "m"), ("v_", "v"))]
    small_out, loss_sum = _small_allreduce_adamw(mine_small, *packed_small, [given[n].size for n in SMALL_NAMES])

    def small_get(k, n):
        if n == "conv_w":
            full = small_out[k][len(SMALL_NAMES)].reshape(CONV_K, QKV_WIDTH)
            return lax.dynamic_slice(full, (0, chip * conv_cols), (CONV_K, conv_cols)).reshape(given[n].shape)
        return small_out[k][SMALL_NAMES.index(n)].reshape(given[n].shape)

    order = ["ln_in_g", "ln_in_b", "w_in", "pool_w", "pool_scale", "conv_w", "a_log", "dt_bias", "o_norm_w", "w_out",
             "ln1_g", "ln1_b", "w_up", "w_down", "ple_gate_w", "ple_proj_w", "ln2_g", "ln2_b"]
    outs = [loss_sum.reshape(()), grad_x[None]]
    for k in range(4):
        for n in order:
            outs.append(big_out[n][k] if n in big_out else small_get(k, n))
    return tuple(outs)
```

```python
import jax
import jax.numpy as jnp
from jax import lax
from jax.experimental import pallas as pl
from jax.experimental.pallas import tpu as pltpu

F32 = jnp.float32
MXU_DTYPE = jnp.bfloat16
WIRE_DTYPE = jnp.bfloat16
SDS = jax.ShapeDtypeStruct

D_MODEL = 1024
POOL_WINDOWS = (2, 4, 8, 16)
POOL_WIDTH = 512
POOL_GROUP = 128
POOL_OUT_GROUP = 256
HEADS = 8
HEAD_DIM = 128
DN_WIDTH = HEADS * HEAD_DIM
QKV_WIDTH = 3 * DN_WIDTH
CONV_K = 4
CHUNK = 128
DW_TK = 1024
DW_TM = 1024
D_FF = 4096
PLE_DIM = 256
LN_EPS = 1e-5
RMS_EPS = 1e-6
L2_EPS = 1e-6
ALPHA = 2.0 ** 0.25
Q_SCALE = HEAD_DIM ** -0.5
IN_WIDTH = 6672
C_POOL, C_QKV, C_Z, C_BETA, C_A, C_GA, C_GB = 0, 512, 3584, 4608, 4616, 4624, 5648
K_QKV, K_Z, K_GA, K_GB, K_U, K_BA, CAT_WIDTH = 0, 3072, 4096, 5120, 6144, 6656, 6912

ADAM_LR, ADAM_B1, ADAM_B2, ADAM_EPS, ADAM_WD, ADAM_STEP = 0.001, 0.9, 0.999, 1e-08, 0.01, 10

N_CHIPS = 4
N_DEV = 8
VMEM_LIMIT = 56 * 1024 * 1024

EARLY = ("w_in", "pool_w")
LATE = ("w_out", "w_up", "w_down", "ple_gate_w", "ple_proj_w")
SMALL_NAMES = ("ln_in_g", "ln_in_b", "pool_scale", "ln1_g", "ln1_b", "ln2_g", "ln2_b", "o_norm_w", "a_log", "dt_bias")
SMALL_CONV_AT = 12
SMALL_CONV_ROWS = CONV_K * QKV_WIDTH // D_MODEL


def _mx(a):
    return a.astype(MXU_DTYPE)


def _dot(a, b):
    return lax.dot_general(_mx(a), _mx(b), (((1,), (0,)), ((), ())), preferred_element_type=F32)


def _dot_nt(a, b):
    return lax.dot_general(_mx(a), _mx(b), (((1,), (1,)), ((), ())), preferred_element_type=F32)


def _dot_tn(a, b):
    return lax.dot_general(_mx(a), _mx(b), (((0,), (0,)), ((), ())), preferred_element_type=F32)


def _sigmoid(x):
    return 0.5 * jnp.tanh(0.5 * x) + 0.5


def _softplus(x):
    return jnp.maximum(x, 0.0) + jnp.log(1.0 + jnp.exp(-jnp.abs(x)))


def _pc(body, name, grid, in_specs, out_specs, out_shape, scratch=(), sem=None, aliases=None):
    return pl.pallas_call(
        body, out_shape=out_shape, grid=grid, in_specs=in_specs, out_specs=out_specs,
        scratch_shapes=scratch, name=name, input_output_aliases=aliases or {},
        compiler_params=pltpu.CompilerParams(dimension_semantics=sem, vmem_limit_bytes=VMEM_LIMIT))


def _row(tm, n):
    return pl.BlockSpec((tm, n), lambda i: (i, 0))


def _const(shape):
    nd = len(shape)
    return pl.BlockSpec(shape, lambda *_: (0,) * nd)


def _matmul(a, b, mode, name, out_dtype=F32, tm=512, tn=512, tk=512, stack_out=False):
    if mode == "nn":
        (m, k), n = a.shape, b.shape[1]
    elif mode == "nt":
        (m, k), n = a.shape, b.shape[0]
    else:
        (k, m), n = a.shape, b.shape[1]
    tm, tn, tk = min(tm, m), min(tn, n), min(tk, k)
    assert m % tm == 0 and n % tn == 0 and k % tk == 0, (name, m, n, k, tm, tn, tk)
    nk = k // tk
    if mode == "nn":
        a_spec = pl.BlockSpec((tm, tk), lambda i, j, kk: (i, kk))
        b_spec = pl.BlockSpec((tk, tn), lambda i, j, kk: (kk, j))
        dot = _dot
    elif mode == "nt":
        a_spec = pl.BlockSpec((tm, tk), lambda i, j, kk: (i, kk))
        b_spec = pl.BlockSpec((tn, tk), lambda i, j, kk: (j, kk))
        dot = _dot_nt
    else:
        a_spec = pl.BlockSpec((tk, tm), lambda i, j, kk: (kk, i))
        b_spec = pl.BlockSpec((tk, tn), lambda i, j, kk: (kk, j))
        dot = _dot_tn

    def body(a_ref, b_ref, o_ref, *acc):
        if nk == 1:
            o_ref[...] = dot(a_ref[...], b_ref[...]).astype(out_dtype)
            return
        acc_ref, kk = acc[0], pl.program_id(2)

        @pl.when(kk == 0)
        def _():
            acc_ref[...] = dot(a_ref[...], b_ref[...])

        @pl.when((kk > 0) & (kk < nk - 1))
        def _():
            acc_ref[...] += dot(a_ref[...], b_ref[...])

        @pl.when(kk == nk - 1)
        def _():
            o_ref[...] = (acc_ref[...] + dot(a_ref[...], b_ref[...])).astype(out_dtype)

    if stack_out:
        o_spec, o_shape = pl.BlockSpec((None, tm, tn), lambda i, j, kk: (j, i, 0)), SDS((n // tn, m, tn), out_dtype)
    else:
        o_spec, o_shape = pl.BlockSpec((tm, tn), lambda i, j, kk: (i, j)), SDS((m, n), out_dtype)
    return _pc(body, name, (m // tm, n // tn, nk), [a_spec, b_spec], o_spec, o_shape,
               scratch=[pltpu.VMEM((tm, tn), F32)] if nk > 1 else [],
               sem=("parallel", "parallel", "arbitrary"))(a, b)


PROJ_TN = 768


def _proj_conv(h0_bf, w_cat, conv_w, tm, after):
    t = h0_bf.shape[0]
    n_qkv = QKV_WIDTH // PROJ_TN

    def body(h_ref, w_ref, cw_ref, after_ref, o_ref, act_ref, ds_ref, carry_ref, ext_ref):
        @pl.when(pl.program_id(0) == 0)
        def _():
            carry_ref[...] = jnp.zeros_like(carry_ref)

        h = h_ref[...]

        def project(cb):
            cols = slice(cb * PROJ_TN, (cb + 1) * PROJ_TN)
            o_ref[:, cols] = _dot(h, w_ref[:, cols])

        def conv(cb, part):
            cols = slice(cb * PROJ_TN, (cb + 1) * PROJ_TN)
            if part == 0:
                ext_ref[cb, 0:8, :] = carry_ref[:, cols]
                ext_ref[cb, 8:8 + tm, :] = o_ref[:, cols]
                carry_ref[:, cols] = o_ref[tm - 8:tm, cols]
            w = [cw_ref[pl.ds(k, 1), cols] for k in range(CONV_K)]
            for r in range(part * (tm // 2), (part + 1) * (tm // 2), CONV_ROWS):
                y = _conv_rows(ext_ref.at[cb], w, r, CONV_ROWS)
                s = _sigmoid(y)
                act_ref[pl.ds(r, CONV_ROWS), cols] = y * s
                ds_ref[pl.ds(r, CONV_ROWS), cols] = _mx(s * (1.0 + y * (1.0 - s)))

        pending = [(cb, part) for cb in range(n_qkv) for part in range(2)]
        project(0)
        for cb in range(1, CAT_WIDTH // PROJ_TN):
            project(cb)
            if pending and pending[0][0] < cb:
                conv(*pending.pop(0))
        for cb, part in pending:
            conv(cb, part)

    return _pc(body, "proj_conv", (t // tm,),
               [_row(tm, D_MODEL), _const((D_MODEL, CAT_WIDTH)), _const((CONV_K, QKV_WIDTH)), ANY],
               [_row(tm, CAT_WIDTH), _row(tm, QKV_WIDTH), _row(tm, QKV_WIDTH)],
               [SDS((t, CAT_WIDTH), F32), SDS((t, QKV_WIDTH), F32), SDS((t, QKV_WIDTH), MXU_DTYPE)],
               scratch=[pltpu.VMEM((8, QKV_WIDTH), F32), pltpu.VMEM((n_qkv, 8 + tm, PROJ_TN), F32)],
               sem=("arbitrary",))(h0_bf, w_cat, conv_w, after)


def _ln_stats(x):
    mu = jnp.mean(x, axis=-1, keepdims=True)
    xc = x - mu
    var = jnp.mean(xc * xc, axis=-1, keepdims=True)
    rstd = lax.rsqrt(var + LN_EPS)
    return xc * rstd, rstd


def _ln_bwd(dy, xhat, rstd, g):
    dxh = dy * g
    m1 = jnp.mean(dxh, axis=-1, keepdims=True)
    m2 = jnp.mean(dxh * xhat, axis=-1, keepdims=True)
    return rstd * (dxh - m1 - xhat * m2)


def _ln_in(x, g, b, tm, after):
    t, d = x.shape

    def body(x_ref, g_ref, b_ref, after_ref, h_ref, hb_ref):
        xhat, _ = _ln_stats(x_ref[...])
        h = xhat * g_ref[...] + b_ref[...]
        h_ref[...] = h
        hb_ref[...] = _mx(h)

    return _pc(body, "ln_in", (t // tm,), [_row(tm, d), _const((1, d)), _const((1, d)), ANY],
               [_row(tm, d), _row(tm, d)], [SDS((t, d), F32), SDS((t, d), MXU_DTYPE)],
               sem=("parallel",))(x, g, b, after)


def _pool_fwd(proj, pool_w, tm):
    t = proj.shape[0]
    ublk = K_U // POOL_WIDTH

    def body(u_ref, halo_ref, pw_ref, ypre_ref, d_ref, ext_ref):
        i = pl.program_id(0)
        ext_ref[0:16, :] = jnp.where(i > 0, halo_ref[...], 0.0)
        ext_ref[16:16 + tm, :] = u_ref[...]
        tok = i * tm + lax.broadcasted_iota(jnp.int32, (tm, POOL_GROUP), 0)
        for gi, w in enumerate(POOL_WINDOWS):
            cs = pl.ds(gi * POOL_GROUP, POOL_GROUP)
            ug = ext_ref[pl.ds(16, tm), cs]
            s = ug
            for k in range(1, w):
                s = s + ext_ref[pl.ds(16 - k, tm), cs]
            cnt = jnp.minimum(tok + 1, w).astype(F32)
            db = _mx(s / cnt - ug)
            d_ref[:, gi * POOL_GROUP:(gi + 1) * POOL_GROUP] = db
            ypre_ref[:, gi * POOL_OUT_GROUP:(gi + 1) * POOL_OUT_GROUP] = _dot(db, pw_ref[gi])

    halo = pl.BlockSpec((16, POOL_WIDTH), lambda i: (jnp.maximum(i * (tm // 16) - 1, 0), ublk))
    return _pc(body, "pool_fwd", (t // tm,),
               [pl.BlockSpec((tm, POOL_WIDTH), lambda i: (i, ublk)), halo, _const((4, POOL_GROUP, POOL_OUT_GROUP))],
               [_row(tm, D_MODEL), _row(tm, POOL_WIDTH)],
               [SDS((t, D_MODEL), F32), SDS((t, POOL_WIDTH), MXU_DTYPE)],
               scratch=[pltpu.VMEM((16 + tm, POOL_WIDTH), F32)], sem=("parallel",))(proj, proj, pool_w)


def _pool_bwd(dyp, d_bf, pool_w, dproj, tm):
    t = dyp.shape[0]
    n = t // tm

    def body(dy_ref, dyn_ref, d_ref, pw_ref, dproj_ref, du_ref, dpw_ref, ext_ref):
        i = pl.program_id(0)

        @pl.when(i == 0)
        def _():
            dpw_ref[...] = jnp.zeros_like(dpw_ref)

        tok = i * tm + lax.broadcasted_iota(jnp.int32, (tm + 16, POOL_GROUP), 0)
        for gi, w in enumerate(POOL_WINDOWS):
            dy = dy_ref[:, gi * POOL_OUT_GROUP:(gi + 1) * POOL_OUT_GROUP]
            dyn = dyn_ref[:, gi * POOL_OUT_GROUP:(gi + 1) * POOL_OUT_GROUP]
            pw = pw_ref[gi]
            dd = _dot_nt(dy, pw)
            ddn = jnp.where(i < n - 1, _dot_nt(dyn, pw), 0.0)
            cnt = jnp.minimum(tok + 1, w).astype(F32)
            ext_ref[0:tm, :] = dd / cnt[0:tm]
            ext_ref[tm:tm + 16, :] = ddn / cnt[tm:tm + 16]
            s = ext_ref[pl.ds(0, tm), :]
            for k in range(1, w):
                s = s + ext_ref[pl.ds(k, tm), :]
            du_ref[:, gi * POOL_GROUP:(gi + 1) * POOL_GROUP] = _mx(s - dd)
            dpw_ref[gi] += _dot_tn(d_ref[:, gi * POOL_GROUP:(gi + 1) * POOL_GROUP], dy)

    nxt = pl.BlockSpec((16, D_MODEL), lambda i: (jnp.minimum((i + 1) * (tm // 16), t // 16 - 1), 0))
    return _pc(body, "pool_bwd", (n,),
               [_row(tm, D_MODEL), nxt, _row(tm, POOL_WIDTH), _const((4, POOL_GROUP, POOL_OUT_GROUP)), ANY],
               [pl.BlockSpec((tm, POOL_WIDTH), lambda i: (i, K_U // POOL_WIDTH)),
                _const((4, POOL_GROUP, POOL_OUT_GROUP))],
               [SDS(dproj.shape, dproj.dtype), SDS((4, POOL_GROUP, POOL_OUT_GROUP), F32)],
               scratch=[pltpu.VMEM((tm + 16, POOL_GROUP), F32)], sem=("arbitrary",),
               aliases={4: 0})(dyp, dyp, d_bf, pool_w, dproj)


CONV_BLK = 512


CONV_ROWS = 32


def _conv_rows(ext_ref, w, r, rows):
    y = w[0] * ext_ref[pl.ds(r + 5, rows), :]
    for k in range(1, CONV_K):
        y = y + w[k] * ext_ref[pl.ds(r + 5 + k, rows), :]
    return y


def _conv_bwd(dact, dsilu, proj, conv_w, dproj, tm):
    t = proj.shape[0]
    n = t // tm

    def body(da_ref, dan_ref, ds_ref, dsn_ref, x_ref, xp_ref, w_ref, dproj_ref, dx_ref, dw_ref, ext_ref, dy_ref):
        i = pl.program_id(1)

        @pl.when(i == 0)
        def _():
            dw_ref[...] = jnp.zeros_like(dw_ref)

        ext_ref[0:8, :] = jnp.where(i > 0, xp_ref[...], 0.0)
        ext_ref[8:8 + tm, :] = x_ref[...]
        w = [w_ref[pl.ds(k, 1), :] for k in range(CONV_K)]

        acc = [jnp.zeros((8, CONV_BLK), F32) for _ in range(CONV_K)]
        for r in range(0, tm, CONV_ROWS):
            dy = da_ref[pl.ds(r, CONV_ROWS), :] * ds_ref[pl.ds(r, CONV_ROWS), :].astype(F32)
            dy_ref[pl.ds(r, CONV_ROWS), :] = dy
            for k in range(CONV_K):
                prod = dy * ext_ref[pl.ds(r + 5 + k, CONV_ROWS), :]
                for q in range(0, CONV_ROWS, 8):
                    acc[k] = acc[k] + prod[q:q + 8]
        dy_ref[tm:tm + 8, :] = jnp.where(i < n - 1, dan_ref[...] * dsn_ref[0:8, :].astype(F32), 0.0)
        for k in range(CONV_K):
            dw_ref[pl.ds(k, 1), :] += jnp.sum(acc[k], axis=0, keepdims=True)
        for r in range(0, tm, CONV_ROWS):
            dx = w[0] * dy_ref[pl.ds(r + 3, CONV_ROWS), :]
            for k in range(1, CONV_K):
                dx = dx + w[k] * dy_ref[pl.ds(r + 3 - k, CONV_ROWS), :]
            dx_ref[pl.ds(r, CONV_ROWS), :] = _mx(dx)

    blk = pl.BlockSpec((tm, CONV_BLK), lambda j, i: (i, j))
    prev = pl.BlockSpec((8, CONV_BLK), lambda j, i: (jnp.maximum(i * (tm // 8) - 1, 0), j))
    nxt = pl.BlockSpec((8, CONV_BLK), lambda j, i: (jnp.minimum((i + 1) * (tm // 8), t // 8 - 1), j))
    nxt16 = pl.BlockSpec((16, CONV_BLK), lambda j, i: (jnp.minimum((i + 1) * (tm // 16), t // 16 - 1), j))
    wspec = pl.BlockSpec((CONV_K, CONV_BLK), lambda j, i: (0, j))
    return _pc(body, "conv_bwd", (QKV_WIDTH // CONV_BLK, n),
               [blk, nxt, blk, nxt16, blk, prev, wspec, ANY],
               [blk, pl.BlockSpec((8, CONV_BLK), lambda j, i: (0, j))],
               [SDS(dproj.shape, dproj.dtype), SDS((8, QKV_WIDTH), F32)],
               scratch=[pltpu.VMEM((8 + tm, CONV_BLK), F32), pltpu.VMEM((8 + tm, CONV_BLK), F32)],
               sem=("parallel", "arbitrary"), aliases={7: 0})(dact, dact, dsilu, dsilu, proj, proj, conv_w, dproj)


def _lane(shape):
    return lax.broadcasted_iota(jnp.int32, shape, 1)


def _ba_fwd(proj, al_row, dtb_row, tm):
    t = proj.shape[0]
    bablk = K_BA // 128

    def body(ba_ref, al_ref, dtb_ref, bg_ref):
        ba = ba_ref[...]
        lane = _lane(ba.shape)
        g = -jnp.exp(al_ref[...]) * _softplus(ba + dtb_ref[...])
        bg_ref[...] = jnp.where(lane < HEADS, _sigmoid(ba), jnp.where(lane < 2 * HEADS, g, 0.0))

    return _pc(body, "ba_fwd", (t // tm,),
               [pl.BlockSpec((tm, 128), lambda i: (i, bablk)), _const((1, 128)), _const((1, 128))],
               _row(tm, 128), SDS((t, 128), F32), sem=("parallel",))(proj, al_row, dtb_row)


def _ba_bwd(dbg, bg, proj, al_row, dtb_row, dproj, tm):
    t = proj.shape[0]
    bablk = K_BA // 128

    def body(dbg_ref, bg_ref, ba_ref, al_ref, dtb_ref, dproj_ref, dba_ref, acc_ref):
        i = pl.program_id(0)

        @pl.when(i == 0)
        def _():
            acc_ref[...] = jnp.zeros_like(acc_ref)

        dbg_v, bg_v, ba = dbg_ref[...], bg_ref[...], ba_ref[...]
        lane = _lane(ba.shape)
        is_g = (lane >= HEADS) & (lane < 2 * HEADS)
        dbeta_raw = dbg_v * bg_v * (1.0 - bg_v)
        da_raw = dbg_v * (-jnp.exp(al_ref[...])) * _sigmoid(ba + dtb_ref[...])
        dba_ref[:, 0:128] = _mx(jnp.where(lane < HEADS, dbeta_raw, jnp.where(is_g, da_raw, 0.0)))
        dba_ref[:, 128:CAT_WIDTH - K_BA] = jnp.zeros((tm, CAT_WIDTH - K_BA - 128), dba_ref.dtype)
        acc_ref[0:1, :] += jnp.sum(jnp.where(is_g, dbg_v * bg_v, 0.0), axis=0, keepdims=True)
        acc_ref[1:2, :] += jnp.sum(jnp.where(is_g, da_raw, 0.0), axis=0, keepdims=True)

    tail = CAT_WIDTH - K_BA
    return _pc(body, "ba_bwd", (t // tm,),
               [_row(tm, 128), _row(tm, 128), pl.BlockSpec((tm, 128), lambda i: (i, bablk)),
                _const((1, 128)), _const((1, 128)), ANY],
               [pl.BlockSpec((tm, tail), lambda i: (i, K_BA // tail)), _const((8, 128))],
               [SDS(dproj.shape, dproj.dtype), SDS((8, 128), F32)],
               sem=("arbitrary",), aliases={5: 0})(dbg, bg, proj, al_row, dtb_row, dproj)


def _each(f, *lists):
    return [f(*a) for a in zip(*lists)]


def _rowsum(a):
    return jnp.sum(a, axis=1, keepdims=True)


def _chunk_terms(qs, ks, bgv, g_rows, hs):
    c = CHUNK
    ii = lax.broadcasted_iota(jnp.int32, (c, c), 0)
    jj = lax.broadcasted_iota(jnp.int32, (c, c), 1)
    lane = _lane(bgv.shape)
    incl = ii >= jj
    beta = [_rowsum(jnp.where(lane == h, bgv, 0.0)) for h in hs]
    g_col = [_rowsum(jnp.where(lane == HEADS + h, bgv, 0.0)) for h in hs]
    rq = _each(lambda q: lax.rsqrt(_rowsum(q * q) + L2_EPS), qs)
    rk = _each(lambda k: lax.rsqrt(_rowsum(k * k) + L2_EPS), ks)
    yq = _each(jnp.multiply, qs, rq)
    kn = _each(jnp.multiply, ks, rk)
    qn = _each(lambda a: a * Q_SCALE, yq)
    gc_col = _each(lambda g: _rowsum(jnp.where(jj <= ii, g, 0.0)), g_rows)
    gc_row = _each(lambda g: jnp.sum(jnp.where(ii <= jj, g, 0.0), axis=0, keepdims=True), g_col)
    dm = _each(lambda a, b: jnp.where(incl, jnp.exp(jnp.where(incl, a - b, 0.0)), 0.0), gc_col, gc_row)
    gl = _each(_rowsum, g_rows)
    eg = _each(jnp.exp, gc_col)
    ek = _each(lambda a, b: jnp.exp(a - b), gl, gc_col)
    egl = _each(jnp.exp, gl)
    kb = _each(jnp.multiply, kn, beta)
    kk = _each(_dot_nt, kb, kn)
    qk = _each(_dot_nt, qn, kn)
    m = _each(lambda a, b: jnp.where(ii > jj, a * b, 0.0), kk, dm)
    attn = _each(jnp.multiply, qk, dm)
    return dict(ii=ii, jj=jj, beta=beta, rq=rq, rk=rk, yq=yq, kn=kn, qn=qn, dm=dm, eg=eg, ek=ek,
                egl=egl, kb=kb, m=m, attn=attn)


def _unit_lower_inverse_minus_identity(ms, ii, jj):
    pair = (ii >> 1) == (jj >> 1)
    ys = _each(lambda m: -jnp.where(pair, m, 0.0), ms)
    s = 1
    while (1 << s) < CHUNK:
        mask = ((ii >> (s + 1)) == (jj >> (s + 1))) & ((ii >> s) != (jj >> s))
        lbs = _each(lambda m: jnp.where(mask, m, 0.0), ms)
        zs = _each(lambda y, lb: lb + _dot(y, lb), ys, lbs)
        ys = _each(lambda y, z: y - z - _dot(z, y), ys, zs)
        s += 1
    return ys


def _dn_fwd(qkv_act, bg, bgt):
    t = qkv_act.shape[0]
    nt = t // CHUNK
    c = CHUNK
    hs = list(range(HEADS))
    qo = [slice(h * HEAD_DIM, (h + 1) * HEAD_DIM) for h in hs]
    ko = [slice(DN_WIDTH + h * HEAD_DIM, DN_WIDTH + (h + 1) * HEAD_DIM) for h in hs]
    vo = [slice(2 * DN_WIDTH + h * HEAD_DIM, 2 * DN_WIDTH + (h + 1) * HEAD_DIM) for h in hs]

    def body(qkv_ref, bg_ref, bgt_ref, o_ref, u_ref, w_ref, qg_ref, kg_ref, attn_ref, y_ref, vn_ref, st_ref, egl_ref,
             s_ref):
        @pl.when(pl.program_id(0) == 0)
        def _():
            s_ref[...] = jnp.zeros_like(s_ref)

        bgv = bg_ref[...]
        qs = [qkv_ref[:, o] for o in qo]
        ks = [qkv_ref[:, o] for o in ko]
        vs = [qkv_ref[:, o] for o in vo]
        g_rows = [bgt_ref[pl.ds(HEADS + h, 1), :] for h in hs]
        ct = _chunk_terms(qs, ks, bgv, g_rows, hs)
        ys = _unit_lower_inverse_minus_identity(ct["m"], ct["ii"], ct["jj"])
        vb = _each(jnp.multiply, vs, ct["beta"])
        kbe = _each(jnp.multiply, ct["kb"], ct["eg"])
        us = _each(lambda a, y: a + _dot(y, a), vb, ys)
        ws = _each(lambda a, y: _mx(a + _dot(y, a)), kbe, ys)
        qg = _each(lambda a, b: _mx(a * b), ct["qn"], ct["eg"])
        kg = _each(lambda a, b: _mx(a * b), ct["kn"], ct["ek"])
        attn = _each(_mx, ct["attn"])
        ss = [s_ref[h] for h in hs]
        sb = _each(_mx, ss)
        vn = _each(lambda a, b, s_: a - _dot(b, s_), us, ws, sb)
        vnb = _each(_mx, vn)
        oa = _each(_dot, qg, sb)
        ob = _each(_dot, attn, vnb)
        upd = _each(_dot_tn, kg, vnb)
        for h, sl in enumerate(qo):
            u_ref[:, sl] = us[h]
            w_ref[:, sl] = ws[h]
            qg_ref[:, sl] = qg[h]
            kg_ref[:, sl] = kg[h]
            attn_ref[:, sl] = attn[h]
            y_ref[:, sl] = _mx(ys[h])
            egl_ref[0, h:h + 1, :] = jnp.broadcast_to(ct["egl"][h], (1, HEAD_DIM))
            st_ref[0, h] = ss[h]
            vn_ref[:, sl] = vnb[h]
            o_ref[:, sl] = oa[h] + ob[h]
            s_ref[h] = ss[h] * ct["egl"][h] + upd[h]

    wide = _row(c, DN_WIDTH)
    return _pc(body, "dn_fwd", (nt,),
               [_row(c, QKV_WIDTH), _row(c, 128), pl.BlockSpec((2 * HEADS, c), lambda i: (0, i))],
               [wide] * 8 + [pl.BlockSpec((1, HEADS, HEAD_DIM, HEAD_DIM), lambda i: (i, 0, 0, 0)),
                             pl.BlockSpec((1, HEADS, HEAD_DIM), lambda i: (i, 0, 0))],
               [SDS((t, DN_WIDTH), F32), SDS((t, DN_WIDTH), F32)] + [SDS((t, DN_WIDTH), MXU_DTYPE)] * 6
               + [SDS((nt, HEADS, HEAD_DIM, HEAD_DIM), F32), SDS((nt, HEADS, HEAD_DIM), F32)],
               scratch=[pltpu.VMEM((HEADS, HEAD_DIM, HEAD_DIM), F32)], sem=("arbitrary",))(qkv_act, bg, bgt)


def _dn_bwd(do, qkv_act, bg, bgt, u, w, qg, kg, attn, ymat, vn, states, egl):
    t = do.shape[0]
    nt = t // CHUNK
    c = CHUNK
    hs = list(range(HEADS))
    qo = [slice(h * HEAD_DIM, (h + 1) * HEAD_DIM) for h in hs]
    ko = [slice(DN_WIDTH + h * HEAD_DIM, DN_WIDTH + (h + 1) * HEAD_DIM) for h in hs]
    vo = [slice(2 * DN_WIDTH + h * HEAD_DIM, 2 * DN_WIDTH + (h + 1) * HEAD_DIM) for h in hs]

    def body(do_ref, qkv_ref, bg_ref, bgt_ref, u_ref, w_ref, qg_ref, kg_ref, attn_ref, y_ref, vn_ref, st_ref, egl_ref,
             dqkv_ref, dbg_ref, ds_ref):
        @pl.when(pl.program_id(0) == 0)
        def _():
            ds_ref[...] = jnp.zeros_like(ds_ref)

        dsp = [ds_ref[h] for h in hs]
        dsb = _each(_mx, dsp)
        ss = [st_ref[0, h] for h in hs]
        sb = _each(_mx, ss)
        du = [_dot(kg_ref[:, sl], b) + _dot_tn(attn_ref[:, sl], do_ref[:, sl]) for sl, b in zip(qo, dsb)]
        dub = _each(_mx, du)
        dkg_v = [_dot_nt(vn_ref[:, sl], b) for sl, b in zip(qo, dsb)]
        dqg_v = [_dot_nt(do_ref[:, sl], b) for sl, b in zip(qo, sb)]
        dattn_v = [_dot_nt(do_ref[:, sl], vn_ref[:, sl]) for sl in qo]
        dwv = [-_dot_nt(a, b) for a, b in zip(dub, sb)]
        upd = [_dot_tn(qg_ref[:, sl], do_ref[:, sl]) - _dot_tn(w_ref[:, sl], a) for sl, a in zip(qo, dub)]
        degl_v = [jnp.sum(_rowsum(a * b), axis=0, keepdims=True) for a, b in zip(ss, dsp)]
        for h in hs:
            ds_ref[h] = dsp[h] * egl_ref[0, h:h + 1, :] + upd[h]

        bgv = bg_ref[...]
        lane = _lane(bgv.shape)
        rowi = lax.broadcasted_iota(jnp.int32, (c, 1), 0)
        qs = [qkv_ref[:, o] for o in qo]
        ks = [qkv_ref[:, o] for o in ko]
        vs = [qkv_ref[:, o] for o in vo]
        g_rows = [bgt_ref[pl.ds(HEADS + h, 1), :] for h in hs]
        ct = _chunk_terms(qs, ks, bgv, g_rows, hs)
        ii, jj = ct["ii"], ct["jj"]
        beta, eg, ek, kb, kn, qn, dm = ct["beta"], ct["eg"], ct["ek"], ct["kb"], ct["kn"], ct["qn"], ct["dm"]
        ys = [y_ref[:, o] for o in qo]
        dvb = _each(lambda a, y: a + _dot_tn(y, a), du, ys)
        dkbe = _each(lambda a, y: a + _dot_tn(y, a), dwv, ys)
        dm_u = [_dot_nt(a, u_ref[:, o]) for a, o in zip(dvb, qo)]
        dm_w = [_dot_nt(a, w_ref[:, o]) for a, o in zip(dkbe, qo)]
        dms = _each(lambda a, b: jnp.where(ii > jj, -(a + b), 0.0), dm_u, dm_w)
        dkk = _each(jnp.multiply, dms, dm)
        dqk = _each(jnp.multiply, dattn_v, dm)
        gmat = _each(lambda a, b, c_, d: a * b + c_ * d, dms, ct["m"], dattn_v, ct["attn"])
        dkb = _each(lambda a, b, c_, d: _dot(a, b) + c_ * d, dkk, kn, dkbe, eg)
        dk1 = _each(_dot_tn, dkk, kb)
        dk2 = _each(_dot_tn, dqk, qn)
        dq1 = _each(_dot, dqk, kn)
        dk = _each(lambda a, b, c_, d: a + b + c_ * d, dk1, dk2, dkg_v, ek)
        dq = _each(lambda a, b, c_: a + b * c_, dq1, dqg_v, eg)
        deg = _each(lambda a, b, c_, d: _rowsum(a * b) + _rowsum(c_ * d), dqg_v, qn, dkbe, kb)
        dek = _each(lambda a, b: _rowsum(a * b), dkg_v, kn)
        dgl = _each(lambda a, b, c_, d: jnp.sum(a * b, axis=0, keepdims=True) + c_ * d, dek, ek, degl_v, ct["egl"])
        cs_row = _each(lambda g: jnp.sum(g, axis=0, keepdims=True), gmat)
        cs_col = _each(lambda r: _rowsum(jnp.where(ii == jj, r, 0.0)), cs_row)
        dgc = _each(lambda a, b, c_, d, g, e, f: a * b - c_ * d + _rowsum(g) - e + jnp.where(rowi == c - 1, f, 0.0),
                    deg, eg, dek, ek, gmat, cs_col, dgl)
        dgc_row = _each(lambda a: jnp.sum(jnp.where(ii == jj, a, 0.0), axis=0, keepdims=True), dgc)
        dg = _each(lambda r: _rowsum(jnp.where(jj >= ii, r, 0.0)), dgc_row)
        dbeta = _each(lambda a, b, c_, d: _rowsum(a * b) + _rowsum(c_ * d), dkb, kn, dvb, vs)
        dk = _each(lambda a, b, c_: a + b * c_, dk, dkb, beta)
        dbg = jnp.zeros((c, 128), F32)
        for h in hs:
            dyq = dq[h] * Q_SCALE
            yq = ct["yq"][h]
            dqkv_ref[:, qo[h]] = ct["rq"][h] * (dyq - yq * _rowsum(yq * dyq))
            dqkv_ref[:, ko[h]] = ct["rk"][h] * (dk[h] - kn[h] * _rowsum(kn[h] * dk[h]))
            dqkv_ref[:, vo[h]] = dvb[h] * beta[h]
            dbg = dbg + jnp.where(lane == h, dbeta[h], 0.0) + jnp.where(lane == HEADS + h, dg[h], 0.0)
        dbg_ref[...] = dbg

    rev = pl.BlockSpec((c, DN_WIDTH), lambda i: (nt - 1 - i, 0))
    return _pc(body, "dn_bwd", (nt,),
               [rev, pl.BlockSpec((c, QKV_WIDTH), lambda i: (nt - 1 - i, 0)),
                pl.BlockSpec((c, 128), lambda i: (nt - 1 - i, 0)), pl.BlockSpec((2 * HEADS, c), lambda i: (0, nt - 1 - i))]
               + [rev] * 7
               + [pl.BlockSpec((1, HEADS, HEAD_DIM, HEAD_DIM), lambda i: (nt - 1 - i, 0, 0, 0)),
                  pl.BlockSpec((1, HEADS, HEAD_DIM), lambda i: (nt - 1 - i, 0, 0))],
               [pl.BlockSpec((c, QKV_WIDTH), lambda i: (nt - 1 - i, 0)), pl.BlockSpec((c, 128), lambda i: (nt - 1 - i, 0))],
               [SDS((t, QKV_WIDTH), F32), SDS((t, 128), F32)],
               scratch=[pltpu.VMEM((HEADS, HEAD_DIM, HEAD_DIM), F32)],
               sem=("arbitrary",))(do, qkv_act, bg, bgt, u, w, qg, kg, attn, ymat, vn, states, egl)


MIX_ROWS = 64


def _mix_oproj_ln1(o, proj, ypre, pool_scale, wo_row, w_out, h0, g1, b1, tm):
    t = o.shape[0]

    def body(o_ref, z_ref, ga_ref, gb_ref, yp_ref, ps_ref, wo_ref, w_ref, h0_ref, g_ref, b_ref,
             mixed_ref, a1_ref, h1_ref, h1b_ref):
        for r in range(0, tm, MIX_ROWS):
            rows = pl.ds(r, MIX_ROWS)
            for h in range(HEADS):
                sl = slice(h * HEAD_DIM, (h + 1) * HEAD_DIM)
                oh = o_ref[rows, sl]
                on = oh * lax.rsqrt(jnp.mean(oh * oh, axis=1, keepdims=True) + RMS_EPS)
                zh = z_ref[rows, sl]
                yb = on * wo_ref[:, sl] * (zh * _sigmoid(zh))
                ya = yp_ref[rows, sl] * ps_ref[:, sl]
                mixed_ref[rows, sl] = _mx(_sigmoid(ga_ref[rows, sl]) * ya + _sigmoid(gb_ref[rows, sl]) * yb)
        a1 = ALPHA * h0_ref[...] + _dot(mixed_ref[...], w_ref[...])
        a1_ref[...] = a1
        xhat, _ = _ln_stats(a1)
        h1 = xhat * g_ref[...] + b_ref[...]
        h1_ref[...] = h1
        h1b_ref[...] = _mx(h1)

    def col(blk):
        return pl.BlockSpec((tm, D_MODEL), lambda i: (i, blk))

    r = _row(tm, D_MODEL)
    v = _const((1, D_MODEL))
    return _pc(body, "mix_oproj_ln1", (t // tm,),
               [r, col(K_Z // D_MODEL), col(K_GA // D_MODEL), col(K_GB // D_MODEL), r, v, v,
                _const((D_MODEL, D_MODEL)), r, v, v],
               [r, r, r, r],
               [SDS((t, D_MODEL), MXU_DTYPE), SDS((t, D_MODEL), F32), SDS((t, D_MODEL), F32),
                SDS((t, D_MODEL), MXU_DTYPE)],
               sem=("parallel",))(o, proj, proj, proj, ypre, pool_scale, wo_row, w_out, h0, g1, b1)


def _mix_bwd(da1_bf, w_out, o, proj, ypre, pool_scale, wo_row, tm, after):
    t = o.shape[0]

    def body(da_ref, wout_ref, o_ref, z_ref, ga_ref, gb_ref, yp_ref, ps_ref, wo_ref, after_ref,
             do_ref, dp_ref, dyp_ref, acc_ref, dm_ref):
        i = pl.program_id(0)

        @pl.when(i == 0)
        def _():
            acc_ref[...] = jnp.zeros_like(acc_ref)

        dm_ref[...] = _dot_nt(da_ref[...], wout_ref[...])
        dwo = jnp.zeros((1, HEAD_DIM), F32)
        for h in range(HEADS):
            sl = slice(h * HEAD_DIM, (h + 1) * HEAD_DIM)
            woh = wo_ref[:, sl]
            psh = ps_ref[:, sl]
            dps = jnp.zeros((1, HEAD_DIM), F32)
            for r in range(0, tm, MIX_ROWS):
                rows = pl.ds(r, MIX_ROWS)
                oh = o_ref[rows, sl]
                rs = lax.rsqrt(jnp.mean(oh * oh, axis=1, keepdims=True) + RMS_EPS)
                on = oh * rs
                zh = z_ref[rows, sl]
                sz = _sigmoid(zh)
                silu = zh * sz
                t1 = on * woh
                yb = t1 * silu
                sa = _sigmoid(ga_ref[rows, sl])
                sb = _sigmoid(gb_ref[rows, sl])
                yp = yp_ref[rows, sl]
                dm = dm_ref[rows, sl]
                ga_sl = slice(D_MODEL + h * HEAD_DIM, D_MODEL + (h + 1) * HEAD_DIM)
                gb_sl = slice(2 * D_MODEL + h * HEAD_DIM, 2 * D_MODEL + (h + 1) * HEAD_DIM)
                dp_ref[rows, ga_sl] = _mx(dm * (yp * psh) * sa * (1.0 - sa))
                dp_ref[rows, gb_sl] = _mx(dm * yb * sb * (1.0 - sb))
                dya = dm * sa
                dyb = dm * sb
                dyp_ref[rows, sl] = _mx(dya * psh)
                dps = dps + jnp.sum(dya * yp, axis=0, keepdims=True)
                dp_ref[rows, sl] = _mx(dyb * t1 * (sz * (1.0 + zh * (1.0 - sz))))
                dt1 = dyb * silu
                dwo = dwo + jnp.sum(dt1 * on, axis=0, keepdims=True)
                don = dt1 * woh
                do_ref[rows, sl] = _mx(rs * (don - on * jnp.mean(don * on, axis=1, keepdims=True)))
            acc_ref[0:1, sl] += dps
        acc_ref[1:2, 0:HEAD_DIM] += dwo

    def col(blk):
        return pl.BlockSpec((tm, D_MODEL), lambda i: (i, blk))

    r = _row(tm, D_MODEL)
    return _pc(body, "mix_bwd", (t // tm,),
               [r, _const((D_MODEL, D_MODEL)), r, col(K_Z // D_MODEL), col(K_GA // D_MODEL), col(K_GB // D_MODEL), r,
                _const((1, D_MODEL)), _const((1, D_MODEL)), ANY],
               [r, pl.BlockSpec((tm, 3 * D_MODEL), lambda i: (i, K_Z // (3 * D_MODEL))), r, _const((8, D_MODEL))],
               [SDS((t, D_MODEL), MXU_DTYPE), SDS((t, CAT_WIDTH), MXU_DTYPE), SDS((t, D_MODEL), MXU_DTYPE),
                SDS((8, D_MODEL), F32)],
               scratch=[pltpu.VMEM((tm, D_MODEL), F32)],
               sem=("arbitrary",))(da1_bf, w_out, o, proj, proj, proj, ypre, pool_scale, wo_row, after)


def _mlp_up(h1_bf, w_up, tm):
    t = h1_bf.shape[0]
    tn = w_up.shape[2]

    def body(h_ref, w_ref, act_ref):
        r = jnp.maximum(_dot(h_ref[...], w_ref[...]), 0.0)
        act_ref[...] = _mx(r * r)

    return _pc(body, "mlp_up", (D_FF // tn, t // tm),
               [pl.BlockSpec((tm, D_MODEL), lambda j, i: (i, 0)),
                pl.BlockSpec((None, D_MODEL, tn), lambda j, i: (j, 0, 0))],
               pl.BlockSpec((tm, tn), lambda j, i: (i, j)), SDS((t, D_FF), MXU_DTYPE),
               sem=("parallel", "parallel"))(h1_bf, w_up)


def _tail(act, w_down, h1, w_gate, p, w_proj, tgt, g2, b2, tm):
    t = act.shape[0]

    def body(act_ref, wd_ref, h1_ref, wg_ref, p_ref, wp_ref, tgt_ref, g_ref, b_ref,
             dr_ref, drb_ref, dgp_ref, dpp_ref, rb_ref, acc_ref):
        i = pl.program_id(0)

        @pl.when(i == 0)
        def _():
            acc_ref[...] = jnp.zeros_like(acc_ref)

        r = ALPHA * h1_ref[...] + _dot(act_ref[...], wd_ref[...])
        rb = _mx(r)
        rb_ref[...] = rb
        gate = _sigmoid(_dot(rb, wg_ref[...]))
        pp = _dot(p_ref[...], wp_ref[...])
        xhat, rstd = _ln_stats(r + gate * pp)
        g = g_ref[...]
        diff = xhat * g + b_ref[...] - tgt_ref[...]
        dh2 = diff * (1.0 / D_MODEL)
        rowloss = jnp.sum(diff * diff, axis=1, keepdims=True) * (0.5 / D_MODEL)
        acc_ref[0:1, :] += jnp.sum(dh2 * xhat, axis=0, keepdims=True)
        acc_ref[1:2, :] += jnp.sum(dh2, axis=0, keepdims=True)
        acc_ref[2:3, :] += jnp.broadcast_to(jnp.sum(rowloss, axis=0, keepdims=True), (1, D_MODEL))
        da2 = _ln_bwd(dh2, xhat, rstd, g)
        dpp_ref[...] = _mx(da2 * gate)
        dgp = _mx(da2 * pp * gate * (1.0 - gate))
        dgp_ref[...] = dgp
        dr = da2 + _dot_nt(dgp, wg_ref[...])
        dr_ref[...] = dr
        drb_ref[...] = _mx(dr)

    r = _row(tm, D_MODEL)
    v = _const((1, D_MODEL))
    return _pc(body, "tail", (t // tm,),
               [_row(tm, D_FF), _const((D_FF, D_MODEL)), r, _const((D_MODEL, D_MODEL)), _row(tm, PLE_DIM),
                _const((PLE_DIM, D_MODEL)), r, v, v],
               [r, r, r, r, r, _const((8, D_MODEL))],
               [SDS((t, D_MODEL), F32)] + [SDS((t, D_MODEL), MXU_DTYPE)] * 4 + [SDS((8, D_MODEL), F32)],
               sem=("arbitrary",))(act, w_down, h1, w_gate, p, w_proj, tgt, g2, b2)


SQRT_GUARD = 1e-30


def _mlp_bwd1(dr_bf, w_down, act, tm, tn):
    t = act.shape[0]

    def body(dr_ref, w_ref, act_ref, dup_ref):
        dact = _dot_nt(dr_ref[...], w_ref[...])
        a = act_ref[...].astype(F32)
        dup_ref[...] = _mx(dact * (2.0 * a * lax.rsqrt(a + SQRT_GUARD)))

    o = pl.BlockSpec((tm, tn), lambda j, i: (i, j))
    return _pc(body, "mlp_bwd1", (D_FF // tn, t // tm),
               [pl.BlockSpec((tm, D_MODEL), lambda j, i: (i, 0)), pl.BlockSpec((tn, D_MODEL), lambda j, i: (j, 0)), o],
               o, SDS((t, D_FF), MXU_DTYPE), sem=("parallel", "parallel"))(dr_bf, w_down, act)


def _mlp_bwd2(dup, w_up, dr, a1, g1, tm):
    t = dr.shape[0]

    nk, tk = w_up.shape[0], w_up.shape[2]

    def body(dup_ref, w_ref, dr_ref, a1_ref, g_ref, da1_ref, da1b_ref, acc_ref):
        i = pl.program_id(0)

        @pl.when(i == 0)
        def _():
            acc_ref[...] = jnp.zeros_like(acc_ref)

        dh1 = ALPHA * dr_ref[...]
        for kk in range(nk):
            dh1 = dh1 + _dot_nt(dup_ref[:, kk * tk:(kk + 1) * tk], w_ref[kk])
        xhat, rstd = _ln_stats(a1_ref[...])
        acc_ref[0:1, :] += jnp.sum(dh1 * xhat, axis=0, keepdims=True)
        acc_ref[1:2, :] += jnp.sum(dh1, axis=0, keepdims=True)
        da1 = _ln_bwd(dh1, xhat, rstd, g_ref[...])
        da1_ref[...] = da1
        da1b_ref[...] = _mx(da1)

    r = _row(tm, D_MODEL)
    return _pc(body, "mlp_bwd2", (t // tm,),
               [_row(tm, D_FF), _const((nk, D_MODEL, tk)), r, r, _const((1, D_MODEL))],
               [r, r, _const((8, D_MODEL))],
               [SDS((t, D_MODEL), F32), SDS((t, D_MODEL), MXU_DTYPE), SDS((8, D_MODEL), F32)],
               sem=("arbitrary",))(dup, w_up, dr, a1, g1)


def _ln_in_bwd(dproj, w_cat, da1, x, g, tm, after):
    t = x.shape[0]

    def body(dp_ref, w_ref, da1_ref, x_ref, g_ref, after_ref, dx_ref, acc_ref):
        i = pl.program_id(0)

        @pl.when(i == 0)
        def _():
            acc_ref[...] = jnp.zeros_like(acc_ref)

        dh0 = _dot_nt(dp_ref[...], w_ref[...]) + ALPHA * da1_ref[...]
        xhat, rstd = _ln_stats(x_ref[...])
        acc_ref[0:1, :] += jnp.sum(dh0 * xhat, axis=0, keepdims=True)
        acc_ref[1:2, :] += jnp.sum(dh0, axis=0, keepdims=True)
        dx_ref[...] = _ln_bwd(dh0, xhat, rstd, g_ref[...])

    r = _row(tm, D_MODEL)
    return _pc(body, "ln_in_bwd", (t // tm,),
               [_row(tm, CAT_WIDTH), _const((D_MODEL, CAT_WIDTH)), r, r, _const((1, D_MODEL)), ANY],
               [r, _const((8, D_MODEL))], [SDS((t, D_MODEL), F32), SDS((8, D_MODEL), F32)],
               sem=("arbitrary",))(dproj, w_cat, da1, x, g, after)


def _local_step(x, p, tgt, wts, start_token, first_weights, late_weights, send_late_grads, send_early_grads):
    t = x.shape[0]
    tm = min(512, t)
    tms = min(256, t)
    row = lambda a: a.reshape(1, -1)
    pool_scale = row(wts["pool_scale"])
    wo_row = jnp.tile(row(wts["o_norm_w"]), (1, HEADS))
    pad8 = jnp.zeros((1, HEADS), F32)
    al_row = jnp.concatenate([pad8, row(wts["a_log"]), jnp.zeros((1, 128 - 2 * HEADS), F32)], axis=1)
    dtb_row = jnp.concatenate([pad8, row(wts["dt_bias"]), jnp.zeros((1, 128 - 2 * HEADS), F32)], axis=1)
    g_in, b_in = row(wts["ln_in_g"]), row(wts["ln_in_b"])
    g1, b1 = row(wts["ln1_g"]), row(wts["ln1_b"])
    g2, b2 = row(wts["ln2_g"]), row(wts["ln2_b"])

    h0, h0_bf = _ln_in(x, g_in, b_in, tm, start_token)
    first, first_token = first_weights(h0_bf)
    wts = {**wts, **first}
    w_cat = wts["w_cat"]
    proj, qkv_act, dsilu = _proj_conv(h0_bf, w_cat, wts["conv_w"], tms, first_token)
    ypre, d_bf = _pool_fwd(proj, wts["pool_w"], tm)
    bg = _ba_fwd(proj, al_row, dtb_row, tm)
    bgt = bg[:, :2 * HEADS].T
    o, u, w, qg, kg, attn, ymat, vn, states, egl = _dn_fwd(qkv_act, bg, bgt)
    wts = {**wts, **late_weights(o)}
    mixed, a1, h1, h1_bf = _mix_oproj_ln1(o, proj, ypre, pool_scale, wo_row, wts["w_out"], h0, g1, b1, tms)
    act = _mlp_up(h1_bf, wts["w_up"], tm)
    dr, dr_bf, dgp, dpp, r_bf, acc_tail = _tail(act, wts["w_down"], h1, wts["ple_gate_w"], p, wts["ple_proj_w"],
                                                tgt, g2, b2, tms)
    grads = {}
    grads["ple_proj_w"] = _matmul(p, dpp, "tn", "dw_ple_proj", WIRE_DTYPE, tm=256, tn=1024, tk=DW_TK)
    grads["ple_gate_w"] = _matmul(r_bf, dgp, "tn", "dw_ple_gate", WIRE_DTYPE, tm=DW_TM, tn=1024, tk=DW_TK)
    grads["w_down"] = _matmul(act, dr_bf, "tn", "dw_down", WIRE_DTYPE, tm=DW_TM, tn=1024, tk=DW_TK)
    dup = _mlp_bwd1(dr_bf, wts["w_down"], act, tm, 1024)
    grads["w_up"] = _matmul(h1_bf, dup, "tn", "dw_up", WIRE_DTYPE, tm=DW_TM, tn=1024, tk=DW_TK, stack_out=True)
    da1, da1_bf, acc_ln1 = _mlp_bwd2(dup, wts["w_up"], dr, a1, g1, tms)
    grads["w_out"] = _matmul(mixed, da1_bf, "tn", "dw_out", WIRE_DTYPE, tm=DW_TM, tn=1024, tk=DW_TK)
    sent = send_late_grads(grads)
    do, dproj, dyp, acc_mix = _mix_bwd(da1_bf, wts["w_out"], o, proj, ypre, pool_scale, wo_row, tms, sent)
    dproj, grads["pool_w"] = _pool_bwd(dyp, d_bf, wts["pool_w"], dproj, tm)
    dqkv_act, dbg = _dn_bwd(do, qkv_act, bg, bgt, u, w, qg, kg, attn, ymat, vn, states, egl)
    dproj, acc_conv = _conv_bwd(dqkv_act, dsilu, proj, wts["conv_w"], dproj, tm)
    dproj, acc_ba = _ba_bwd(dbg, bg, proj, al_row, dtb_row, dproj, tm)
    dw_cat = _matmul(h0_bf, dproj, "tn", "dw_in", WIRE_DTYPE, tm=DW_TM, tn=1152, tk=DW_TK)
    grads["w_in"] = _w_in_by_chip(dw_cat)
    sent = send_early_grads(grads)
    grad_x, acc_in = _ln_in_bwd(dproj, w_cat, da1, x, g_in, tms, sent)

    grads["conv_w"] = acc_conv[0:CONV_K]
    grads["ln_in_g"], grads["ln_in_b"] = acc_in[0], acc_in[1]
    grads["ln1_g"], grads["ln1_b"] = acc_ln1[0], acc_ln1[1]
    grads["ln2_g"], grads["ln2_b"] = acc_tail[0], acc_tail[1]
    grads["pool_scale"] = acc_mix[0]
    grads["o_norm_w"] = acc_mix[1, 0:HEAD_DIM]
    grads["a_log"] = acc_ba[0, HEADS:2 * HEADS]
    grads["dt_bias"] = acc_ba[1, HEADS:2 * HEADS]
    loss = acc_tail[2, 0]
    return grad_x, grads, loss


MESH = pl.DeviceIdType.MESH
ANY = pl.BlockSpec(memory_space=pl.ANY)


def _chip_of(k, x, y):
    chip = (2 * x + y + k) % N_CHIPS
    return chip // 2, chip % 2


def _place():
    x, y, c = lax.axis_index("x"), lax.axis_index("y"), lax.axis_index("c")
    return x, y, c, 2 * x + y


def _half(rows, c):
    return pl.ds(pl.multiple_of(c * (rows // 2), 16), rows // 2)


def _remote(src, dst, send_sem, recv_sem, device_id):
    return pltpu.make_async_remote_copy(src_ref=src, dst_ref=dst, send_sem=send_sem, recv_sem=recv_sem,
                                        device_id=device_id, device_id_type=MESH)


def _tile_rows(rows):
    for tr in (256, 128, 64, 32, 16):
        if rows % tr == 0:
            return tr
    raise ValueError(rows)


def _first_gather_copies(srcs, lands, send, recv, place):
    copies = []
    for a in range(len(srcs)):
        whole = a == len(srcs) - 1
        for k in range(N_CHIPS):
            if place is None:
                copies.append(None)
                continue
            x, y, c, me = place
            sems = (send.at[a * N_CHIPS + k], recv.at[a * N_CHIPS + k])
            if k == 0:
                copies.append(_remote(srcs[a], lands[a].at[me], *sems, (x, y, 1 - c)))
                continue
            tx, ty = _chip_of(k, x, y)
            if whole:
                copies.append(_remote(srcs[a], lands[a].at[me], *sems, (tx, ty, c)))
            else:
                mine = _half(srcs[a].shape[0], c)
                copies.append(_remote(srcs[a].at[mine], lands[a].at[me, mine], *sems, (tx, ty, c)))
    return copies


def _pass_halves(stacks):
    n = len(stacks)

    def body(*refs):
        outs = refs[n:2 * n]
        send, recv = refs[2 * n:]
        x, y, c, me = _place()
        copies = []
        for a in range(n):
            for k in range(1, N_CHIPS):
                landed = outs[a].at[(me + N_CHIPS - k) % N_CHIPS, _half(stacks[a].shape[1], c)]
                copies.append(_remote(landed, landed, send.at[a * N_CHIPS + k], recv.at[a * N_CHIPS + k],
                                      (x, y, 1 - c)))
        for cp in copies:
            cp.start()
        for cp in copies:
            cp.wait_send()
        for a in range(n):
            for k in range(1, N_CHIPS):
                passed = outs[a].at[(me + N_CHIPS - k) % N_CHIPS, _half(stacks[a].shape[1], 1 - c)]
                _remote(passed, passed, send.at[a * N_CHIPS + k], recv.at[a * N_CHIPS + k], (x, y, c)).wait_recv()

    sems = pltpu.SemaphoreType.DMA((n * N_CHIPS,))
    return pl.pallas_call(
        body, name="pass_halves", out_shape=[SDS(s.shape, s.dtype) for s in stacks],
        in_specs=[ANY] * n, out_specs=[ANY] * n, scratch_shapes=[sems, sems],
        input_output_aliases={a: a for a in range(n)},
    )(*stacks)


def _swap_halves(gs):
    n = len(gs)

    def body(*refs):
        ins, theirs = refs[0:n], refs[n:2 * n]
        send, recv = refs[2 * n:]
        x, y, c, _ = _place()
        copies = [_remote(ins[a].at[:, _half(gs[a].shape[1], 1 - c)], theirs[a], send.at[a], recv.at[a],
                          (x, y, 1 - c)) for a in range(n)]
        for cp in copies:
            cp.start()
        for cp in copies:
            cp.wait()

    return pl.pallas_call(
        body, name="swap_halves", out_shape=[SDS((N_CHIPS, g.shape[1] // 2, g.shape[2]), g.dtype) for g in gs],
        in_specs=[ANY] * n, out_specs=[ANY] * n, scratch_shapes=[pltpu.SemaphoreType.DMA((n,))] * 2,
    )(*gs)


def _send_to_sibling(hs):
    n = len(hs)

    def body(*refs):
        ins, outs = refs[0:n], refs[n:2 * n]
        send, recv = refs[2 * n:]
        x, y, c, _ = _place()
        copies = [_remote(ins[a], outs[a], send.at[a], recv.at[a], (x, y, 1 - c)) for a in range(n)]
        for cp in copies:
            cp.start()
        for cp in copies:
            cp.wait()

    return pl.pallas_call(
        body, name="send_to_sibling", out_shape=[SDS(h.shape, h.dtype) for h in hs],
        in_specs=[ANY] * n, out_specs=[ANY] * n, scratch_shapes=[pltpu.SemaphoreType.DMA((n,))] * 2,
    )(*hs)


HBM = pl.BlockSpec(memory_space=pltpu.HBM)
SEM = pl.BlockSpec(memory_space=pltpu.SEMAPHORE)
EFFECT = pltpu.SideEffectType.DATAFLOW_SIDE_EFFECTING


def _in_hbm(a):
    return pltpu.with_memory_space_constraint(a, pltpu.HBM)


def _split_copy_start(name, srcs, lands, copies_of, after):
    n = len(srcs)
    n_copies = len(copies_of(srcs, lands, None, None, None))

    def body(*refs):
        src_refs, land_refs = refs[0:n], refs[n:2 * n]
        send, recv = refs[2 * n + 1], refs[2 * n + 2]
        token = refs[-1]
        for cp in copies_of(src_refs, land_refs, send, recv, _place()):
            cp.start()
        token[...] = jnp.zeros_like(token)

    sems = pltpu.SemaphoreType.DMA((n_copies,))
    out = pl.pallas_call(
        body, name=name,
        out_shape=[sems, sems] + [pltpu.HBM(a.shape, a.dtype) for a in list(srcs) + list(lands)] + [SDS((8, 128), F32)],
        in_specs=[HBM] * (2 * n) + [ANY],
        out_specs=[SEM, SEM] + [HBM] * (2 * n) + [pl.BlockSpec(memory_space=pltpu.VMEM)],
        input_output_aliases={i: 2 + i for i in range(2 * n)},
        compiler_params=pltpu.CompilerParams(has_side_effects=EFFECT),
    )(*[_in_hbm(a) for a in list(srcs) + list(lands)], after)
    return out[0], out[1], out[2:2 + n], out[2 + n:2 + 2 * n], out[-1]


def _split_copy_wait(name, send, recv, srcs, lands, after, copies_of):
    n = len(srcs)
    after = list(after) if isinstance(after, (list, tuple)) else [after]

    def body(*refs):
        src_refs, land_refs = refs[0:n], refs[n:2 * n]
        send_ref, recv_ref = refs[2 * n], refs[2 * n + 1]
        for cp in copies_of(src_refs, land_refs, send_ref, recv_ref, _place()):
            cp.wait_send()
            cp.wait_recv()

    out = pl.pallas_call(
        body, name=name, out_shape=[pltpu.HBM(a.shape, a.dtype) for a in list(srcs) + list(lands)],
        in_specs=[HBM] * (2 * n) + [SEM, SEM] + [ANY] * len(after), out_specs=[HBM] * (2 * n),
        input_output_aliases={i: i for i in range(2 * n)},
        compiler_params=pltpu.CompilerParams(has_side_effects=EFFECT),
    )(*srcs, *lands, send, recv, *after)
    return out[0:n], out[n:2 * n]


def _late_gather_copies(srcs, lands, send, recv, place):
    copies = []
    for a in range(len(srcs)):
        for k in range(N_CHIPS):
            if place is None:
                copies.append(None)
                continue
            x, y, c, me = place
            if k == 0:
                target = (x, y, 1 - c)
            else:
                tx, ty = _chip_of(k, x, y)
                target = (tx, ty, c)
            copies.append(_remote(srcs[a], lands[a].at[me], send.at[a * N_CHIPS + k], recv.at[a * N_CHIPS + k], target))
    return copies


def _late_scatter_copies(srcs, lands, send, recv, place):
    copies = []
    for a in range(len(srcs)):
        for k in range(1, N_CHIPS):
            if place is None:
                copies.append(None)
                continue
            x, y, c, _ = place
            tx, ty = _chip_of(k, x, y)
            copies.append(_remote(srcs[a].at[2 * tx + ty], lands[a].at[k - 1], send.at[a * (N_CHIPS - 1) + k - 1],
                                  recv.at[a * (N_CHIPS - 1) + k - 1], (tx, ty, c)))
    return copies


def _add_pair(g, theirs, name):
    _, rows, cols = g.shape
    half = rows // 2
    tr = _tile_rows(half)

    def body(g_ref, t_ref, o_ref):
        own = g_ref[lax.axis_index("c")]
        o_ref[...] = (own.astype(F32) + t_ref[...].astype(F32)).astype(o_ref.dtype)

    blk = pl.BlockSpec((None, tr, cols), lambda j, i: (j, i, 0))
    return _pc(body, "add_" + name, (N_CHIPS, half // tr),
               [pl.BlockSpec((None, 2, tr, cols), lambda j, i: (j, 0, i, 0)), blk], blk,
               SDS((N_CHIPS, half, cols), g.dtype), sem=("parallel", "parallel"))(
                   g.reshape(N_CHIPS, 2, half, cols), theirs)


def _sum_slabs(pair, landed, name):
    _, rows, cols = pair.shape
    tr = _tile_rows(rows)

    def body(p_ref, r_ref, o_ref):
        acc = p_ref[2 * lax.axis_index("x") + lax.axis_index("y")].astype(F32)
        for k in range(N_CHIPS - 1):
            acc = acc + r_ref[k].astype(F32)
        o_ref[...] = acc

    return _pc(body, "sum_" + name, (rows // tr,),
               [pl.BlockSpec((N_CHIPS, tr, cols), lambda i: (0, i, 0)),
                pl.BlockSpec((N_CHIPS - 1, tr, cols), lambda i: (0, i, 0))],
               _row(tr, cols), SDS((rows, cols), F32), sem=("parallel",))(pair, landed)


def _adamw_math(w, g, m, v):
    m = ADAM_B1 * m + (1.0 - ADAM_B1) * g
    v = ADAM_B2 * v + (1.0 - ADAM_B2) * (g * g)
    m_hat = m / (1.0 - ADAM_B1 ** ADAM_STEP)
    v_hat = v / (1.0 - ADAM_B2 ** ADAM_STEP)
    delta = -ADAM_LR * (m_hat / (jnp.sqrt(v_hat) + ADAM_EPS) + ADAM_WD * w)
    return delta, m, v


def _adamw_2d(w, g_own, g_sib, m, v, name, halves):
    lead = w.ndim == 3
    rows, cols = w.shape[-2:]
    tr = _tile_rows(rows // 2)
    nh = rows // 2 // tr if halves else rows // tr

    def body(w_ref, go_ref, gs_ref, m_ref, v_ref, g_out, d_out, m_out, v_out):
        if halves:
            mine = (pl.program_id(0) // nh) == lax.axis_index("c")
            g = jnp.where(mine, go_ref[...], gs_ref[...])
        else:
            g = go_ref[...] + gs_ref[...]
        delta, mn, vn = _adamw_math(w_ref[...], g, m_ref[...], v_ref[...])
        g_out[...] = g
        d_out[...] = delta
        m_out[...] = mn
        v_out[...] = vn

    r = _row(tr, cols)
    p = pl.BlockSpec((None, tr, cols), lambda i: (0, i, 0)) if lead else r
    h = pl.BlockSpec((tr, cols), lambda i: (i % nh, 0))
    return _pc(body, "adamw_" + name, (rows // tr,), [p, h, h, p, p], [r] * 4, [SDS((rows, cols), F32)] * 4,
               sem=("parallel",))(w, g_own, g_sib, m, v)


def _small_allreduce_adamw(mine, w, m, v, sizes):
    shape = mine.shape
    n = len(sizes)

    def body(mine_ref, w_ref, m_ref, v_ref, *rest):
        outs, (buf_ref, res_ref, send_sems, recv_sems) = rest[:-4], rest[-4:]
        x, y, c = lax.axis_index("x"), lax.axis_index("y"), lax.axis_index("c")
        me = 4 * x + 2 * y + c
        buf_ref[me] = mine_ref[...]
        copies = []
        for k in range(1, N_DEV):
            tgt = (me + k) % N_DEV
            copies.append(pltpu.make_async_remote_copy(
                src_ref=mine_ref, dst_ref=buf_ref.at[me], send_sem=send_sems.at[k], recv_sem=recv_sems.at[k],
                device_id=(tgt // 4, (tgt // 2) % 2, tgt % 2), device_id_type=MESH))
        for cp in copies:
            cp.start()
        for k in range(1, N_DEV):
            src = (me + N_DEV - k) % N_DEV
            pltpu.make_async_remote_copy(
                src_ref=mine_ref, dst_ref=buf_ref.at[src], send_sem=send_sems.at[k], recv_sem=recv_sems.at[k],
                device_id=(x, y, c), device_id_type=MESH).wait_recv()
        for cp in copies:
            cp.wait_send()
        g = buf_ref[0]
        for j in range(1, N_DEV):
            g = g + buf_ref[j]
        delta, mn, vn = _adamw_math(w_ref[...], g, m_ref[...], v_ref[...])
        for kind, val in enumerate((g, delta, mn, vn)):
            res_ref[kind] = val
            for i, size in enumerate(sizes):
                outs[kind * (n + 1) + i][...] = res_ref[kind, i:i + 1, 0:size]
            outs[kind * (n + 1) + n][...] = res_ref[kind, SMALL_CONV_AT:SMALL_CONV_AT + SMALL_CONV_ROWS, :]
        outs[-1][...] = res_ref[0, n:n + 1, 0:1]

    vm = pl.BlockSpec(memory_space=pltpu.VMEM)
    per_kind = [SDS((1, size), F32) for size in sizes] + [SDS((SMALL_CONV_ROWS, D_MODEL), F32)]
    out_shape = per_kind * 4 + [SDS((1, 1), F32)]
    out = pl.pallas_call(
        body, name="small_allreduce_adamw", out_shape=out_shape, in_specs=[vm] * 4, out_specs=[vm] * len(out_shape),
        scratch_shapes=[pltpu.VMEM((N_DEV,) + shape, F32), pltpu.VMEM((4,) + shape, F32),
                        pltpu.SemaphoreType.DMA((N_DEV,)), pltpu.SemaphoreType.DMA((N_DEV,))],
    )(mine, w, m, v)
    return [out[kind * (n + 1):(kind + 1) * (n + 1)] for kind in range(4)], out[-1]


def _as2d(a):
    return a.reshape(-1, a.shape[-1])


SEGMENTS = ((C_POOL, K_U, POOL_WIDTH), (C_QKV, K_QKV, QKV_WIDTH), (C_Z, K_Z, DN_WIDTH), (C_BETA, K_BA, 2 * HEADS),
            (C_GA, K_GA, D_MODEL), (C_GB, K_GB, D_MODEL))
SHARD_COLS = IN_WIDTH // N_CHIPS


def _w_cat(stack):
    pieces = []
    for c0, _, width in sorted(SEGMENTS, key=lambda seg: seg[1]):
        a = c0
        while a < c0 + width:
            chip = a // SHARD_COLS
            b = min(c0 + width, (chip + 1) * SHARD_COLS)
            pieces.append(stack[chip][:, a - chip * SHARD_COLS:b - chip * SHARD_COLS])
            a = b
    pieces.append(jnp.zeros((D_MODEL, CAT_WIDTH - K_BA - 2 * HEADS), stack.dtype))
    return jnp.concatenate(pieces, axis=1)


def _w_in_by_chip(dw_cat):
    slabs = []
    for chip in range(N_CHIPS):
        lo, hi = chip * SHARD_COLS, (chip + 1) * SHARD_COLS
        pieces = []
        for c0, k0, width in sorted(SEGMENTS):
            a, b = max(c0, lo), min(c0 + width, hi)
            if a < b:
                pieces.append(dw_cat[:, k0 + a - c0:k0 + b - c0])
        slabs.append(jnp.concatenate(pieces, axis=1))
    return jnp.stack(slabs)


WEIGHT_LAYOUT = {
    "w_in": lambda s: ("w_cat", _w_cat(s)),
    "pool_w": lambda s: ("pool_w", s.reshape(N_CHIPS, 4, POOL_GROUP, POOL_OUT_GROUP // N_CHIPS)
                         .transpose(1, 2, 0, 3).reshape(4, POOL_GROUP, POOL_OUT_GROUP)),
    "w_out": lambda s: ("w_out", s.reshape(D_MODEL, D_MODEL)),
    "w_up": lambda s: ("w_up", s),
    "w_down": lambda s: ("w_down", s.reshape(D_FF, D_MODEL)),
    "ple_gate_w": lambda s: ("ple_gate_w", s.reshape(D_MODEL, D_MODEL)),
    "ple_proj_w": lambda s: ("ple_proj_w", s.transpose(1, 0, 2).reshape(PLE_DIM, D_MODEL)),
}

GRAD_LAYOUT = {
    "w_in": lambda g: g,
    "pool_w": lambda g: g.reshape(4, POOL_GROUP, N_CHIPS, POOL_OUT_GROUP // N_CHIPS)
                         .transpose(2, 0, 1, 3).reshape(N_CHIPS, 4 * POOL_GROUP, POOL_OUT_GROUP // N_CHIPS),
    "w_out": lambda g: g.reshape(N_CHIPS, D_MODEL // N_CHIPS, D_MODEL),
    "w_up": lambda g: g,
    "w_down": lambda g: g.reshape(N_CHIPS, D_FF // N_CHIPS, D_MODEL),
    "ple_gate_w": lambda g: g.reshape(N_CHIPS, D_MODEL // N_CHIPS, D_MODEL),
    "ple_proj_w": lambda g: g.reshape(PLE_DIM, N_CHIPS, D_MODEL // N_CHIPS).transpose(1, 0, 2),
}


def _full_weights(names, stacks):
    return dict(WEIGHT_LAYOUT[n](s.astype(MXU_DTYPE)) for n, s in zip(names, stacks))


def _grads_by_chip(names, grads):
    return [GRAD_LAYOUT[n](grads[n]).astype(WIRE_DTYPE) for n in names]


def _pack_small(rows, conv, name):
    n = len(rows)

    def body(*refs):
        out = refs[n + 1]
        out[...] = jnp.zeros_like(out)
        for i in range(n):
            out[i:i + 1, :] = refs[i][...]
        out[SMALL_CONV_AT:SMALL_CONV_AT + SMALL_CONV_ROWS, :] = refs[n][...]

    vm = pl.BlockSpec(memory_space=pltpu.VMEM)
    return pl.pallas_call(body, name=name, out_shape=SDS((SMALL_CONV_AT + SMALL_CONV_ROWS, D_MODEL), F32),
                          in_specs=[vm] * (n + 1), out_specs=vm)(*rows, conv)


def _pad_row(a):
    a = a.reshape(1, -1).astype(F32)
    return jnp.pad(a, ((0, 0), (0, D_MODEL - a.shape[1])))


def kernel(x, p, ln_in_g, ln_in_b, w_in, pool_w, pool_scale, conv_w, a_log, dt_bias, o_norm_w, w_out, ln1_g, ln1_b, w_up, w_down, ple_gate_w, ple_proj_w, ln2_g, ln2_b, loss_target, m_ln_in_g, m_ln_in_b, m_w_in, m_pool_w, m_pool_scale, m_conv_w, m_a_log, m_dt_bias, m_o_norm_w, m_w_out, m_ln1_g, m_ln1_b, m_w_up, m_w_down, m_ple_gate_w, m_ple_proj_w, m_ln2_g, m_ln2_b, v_ln_in_g, v_ln_in_b, v_w_in, v_pool_w, v_pool_scale, v_conv_w, v_a_log, v_dt_bias, v_o_norm_w, v_w_out, v_ln1_g, v_ln1_b, v_w_up, v_w_down, v_ple_gate_w, v_ple_proj_w, v_ln2_g, v_ln2_b):
    given = dict(locals())
    chip = 2 * lax.axis_index("x") + lax.axis_index("y")

    shard = lambda n: _as2d(given[n]).astype(WIRE_DTYPE)

    wts = {"ln_in_g": ln_in_g, "ln_in_b": ln_in_b, "pool_scale": pool_scale[0], "a_log": a_log[0],
           "dt_bias": dt_bias[0], "o_norm_w": o_norm_w[0], "ln1_g": ln1_g[0], "ln1_b": ln1_b[0],
           "ln2_g": ln2_g[0], "ln2_b": ln2_b[0]}

    conv_pad = jnp.pad(conv_w[0], ((0, 8 - CONV_K), (0, 0)))
    first_srcs = [shard(n) for n in EARLY] + [conv_pad]
    first_lands = [lax.empty((N_CHIPS,) + s.shape, s.dtype) for s in first_srcs]
    fsend, frecv, fsrcs, flands, start_token = _split_copy_start(
        "first_gather_start", first_srcs, first_lands, _first_gather_copies, first_srcs[0])
    late = {}
    for n in ("w_in", "m_w_in", "v_w_in"):
        given[n], _ = lax.optimization_barrier((given[n], start_token))

    def first_weights(after):
        _, lands = _split_copy_wait("first_gather_wait", fsend, frecv, fsrcs, flands,
                                    [after, given["w_in"], given["m_w_in"], given["v_w_in"]], _first_gather_copies)
        stacks = _pass_halves(lands[0:len(EARLY)])
        first = _full_weights(EARLY, stacks)
        first["conv_w"] = jnp.concatenate([lands[len(EARLY)][j, 0:CONV_K] for j in range(N_CHIPS)], axis=1)
        late_srcs = [shard(n) for n in LATE]
        late_lands = [lax.empty((N_CHIPS,) + s.shape, s.dtype) for s in late_srcs]
        late["send"], late["recv"], late["srcs"], late["lands"], token = _split_copy_start(
            "late_gather_start", late_srcs, late_lands, _late_gather_copies, stacks[0])
        return first, token

    def late_weights(after):
        _, stacks = _split_copy_wait("late_gather_wait", late["send"], late["recv"], late["srcs"], late["lands"],
                                     after, _late_gather_copies)
        return _full_weights(LATE, stacks)

    scatter = {}

    def send_late_grads(grads):
        srcs = _grads_by_chip(LATE, grads)
        lands = [lax.empty((N_CHIPS - 1,) + g.shape[1:], g.dtype) for g in srcs]
        scatter["send"], scatter["recv"], scatter["srcs"], scatter["lands"], token = _split_copy_start(
            "late_scatter_start", srcs, lands, _late_scatter_copies, srcs[0])
        return token

    last = {}

    def send_early_grads(grads):
        by_chip = _grads_by_chip(EARLY, grads)
        theirs = _swap_halves(by_chip)
        pair = [_add_pair(g, t, n) for g, t, n in zip(by_chip, theirs, EARLY)]
        lands = [lax.empty((N_CHIPS - 1,) + q.shape[1:], q.dtype) for q in pair]
        last["send"], last["recv"], last["srcs"], last["lands"], token = _split_copy_start(
            "early_scatter_start", pair, lands, _late_scatter_copies, pair[0])
        return token

    grad_x, grads, loss = _local_step(x[0], p[0, 0], loss_target[0], wts, start_token, first_weights, late_weights,
                                      send_late_grads, send_early_grads)

    late_mine, late_landed = _split_copy_wait("late_scatter_wait", scatter["send"], scatter["recv"], scatter["srcs"],
                                              scatter["lands"], grad_x, _late_scatter_copies)
    late_part = [_sum_slabs(q, r, n) for q, r, n in zip(late_mine, late_landed, LATE)]
    pair, landed = _split_copy_wait("early_scatter_wait", last["send"], last["recv"], last["srcs"], last["lands"],
                                    grad_x, _late_scatter_copies)
    reduced = [_sum_slabs(q, r, n) for q, r, n in zip(pair, landed, EARLY)]
    from_sibling = _send_to_sibling(reduced + late_part)
    big_out = {}
    for n, g_own, g_sib in zip(EARLY + LATE, reduced + late_part, from_sibling):
        view = (lambda a: a) if given[n].ndim == 3 else _as2d
        res = _adamw_2d(view(given[n]), g_own, g_sib, view(given["m_" + n]), view(given["v_" + n]), n,
                        halves=n in EARLY)
        big_out[n] = [r.reshape(given[n].shape) for r in res]

    conv_cols = QKV_WIDTH // N_CHIPS

    def small_pack(get, conv, extra, name):
        if conv.shape[1] != QKV_WIDTH:
            conv = lax.dynamic_update_slice(jnp.zeros((CONV_K, QKV_WIDTH), F32), conv, (0, chip * conv_cols))
        return _pack_small([_pad_row(get(n)) for n in SMALL_NAMES] + extra, conv.reshape(SMALL_CONV_ROWS, D_MODEL), name)

    mine_small = small_pack(lambda n: grads[n], grads["conv_w"], [jnp.full((1, D_MODEL), loss, F32)], "pack_small_g")
    packed_small = [small_pack(lambda n: given[prefix + n], given[prefix + "conv_w"][0], [], "pack_small_" + tag)
                    for prefix, tag in (("", "w"), ("m_", "m"), ("v_", "v"))]
    small_out, loss_sum = _small_allreduce_adamw(mine_small, *packed_small, [given[n].size for n in SMALL_NAMES])

    def small_get(k, n):
        if n == "conv_w":
            full = small_out[k][len(SMALL_NAMES)].reshape(CONV_K, QKV_WIDTH)
            return lax.dynamic_slice(full, (0, chip * conv_cols), (CONV_K, conv_cols)).reshape(given[n].shape)
        return small_out[k][SMALL_NAMES.index(n)].reshape(given[n].shape)

    order = ["ln_in_g", "ln_in_b", "w_in", "pool_w", "pool_scale", "conv_w", "a_log", "dt_bias", "o_norm_w", "w_out",
             "ln1_g", "ln1_b", "w_up", "w_down", "ple_gate_w", "ple_proj_w", "ln2_g", "ln2_b"]
    outs = [loss_sum.reshape(()), grad_x[None]]
    for k in range(4):
        for n in order:
            outs.append(big_out[n][k] if n in big_out else small_get(k, n))
    return tuple(outs)
```

```python
import jax
import jax.numpy as jnp
from jax import lax
from jax.experimental import pallas as pl
from jax.experimental.pallas import tpu as pltpu

F32 = jnp.float32
MXU_DTYPE = jnp.bfloat16
WIRE_DTYPE = jnp.bfloat16
SDS = jax.ShapeDtypeStruct

D_MODEL = 1024
POOL_WINDOWS = (2, 4, 8, 16)
POOL_WIDTH = 512
POOL_GROUP = 128
POOL_OUT_GROUP = 256
HEADS = 8
HEAD_DIM = 128
DN_WIDTH = HEADS * HEAD_DIM
QKV_WIDTH = 3 * DN_WIDTH
CONV_K = 4
CHUNK = 128
DN_FWD_CHUNKS = 2
DW_TK = 1024
DW_TM = 1024
D_FF = 4096
PLE_DIM = 256
LN_EPS = 1e-5
RMS_EPS = 1e-6
L2_EPS = 1e-6
ALPHA = 2.0 ** 0.25
Q_SCALE = HEAD_DIM ** -0.5
IN_WIDTH = 6672
C_POOL, C_QKV, C_Z, C_BETA, C_A, C_GA, C_GB = 0, 512, 3584, 4608, 4616, 4624, 5648
K_QKV, K_Z, K_GA, K_GB, K_U, K_BA, CAT_WIDTH = 0, 3072, 4096, 5120, 6144, 6656, 6912

ADAM_LR, ADAM_B1, ADAM_B2, ADAM_EPS, ADAM_WD, ADAM_STEP = 0.001, 0.9, 0.999, 1e-08, 0.01, 10

N_CHIPS = 4
N_DEV = 8
VMEM_LIMIT = 56 * 1024 * 1024

EARLY = ("w_in", "pool_w")
LATE = ("w_out", "w_up", "w_down", "ple_gate_w", "ple_proj_w")
SMALL_NAMES = ("ln_in_g", "ln_in_b", "pool_scale", "ln1_g", "ln1_b", "ln2_g", "ln2_b", "o_norm_w", "a_log", "dt_bias")
SMALL_CONV_AT = 12
SMALL_CONV_ROWS = CONV_K * QKV_WIDTH // D_MODEL


def _mx(a):
    return a.astype(MXU_DTYPE)


def _dot(a, b):
    return lax.dot_general(_mx(a), _mx(b), (((1,), (0,)), ((), ())), preferred_element_type=F32)


def _dot_nt(a, b):
    return lax.dot_general(_mx(a), _mx(b), (((1,), (1,)), ((), ())), preferred_element_type=F32)


def _dot_tn(a, b):
    return lax.dot_general(_mx(a), _mx(b), (((0,), (0,)), ((), ())), preferred_element_type=F32)


def _sigmoid(x):
    return 0.5 * jnp.tanh(0.5 * x) + 0.5


def _softplus(x):
    return jnp.maximum(x, 0.0) + jnp.log(1.0 + jnp.exp(-jnp.abs(x)))


def _pc(body, name, grid, in_specs, out_specs, out_shape, scratch=(), sem=None, aliases=None):
    return pl.pallas_call(
        body, out_shape=out_shape, grid=grid, in_specs=in_specs, out_specs=out_specs,
        scratch_shapes=scratch, name=name, input_output_aliases=aliases or {},
        compiler_params=pltpu.CompilerParams(dimension_semantics=sem, vmem_limit_bytes=VMEM_LIMIT))


def _row(tm, n):
    return pl.BlockSpec((tm, n), lambda i: (i, 0))


def _const(shape):
    nd = len(shape)
    return pl.BlockSpec(shape, lambda *_: (0,) * nd)


def _matmul(a, b, mode, name, out_dtype=F32, tm=512, tn=512, tk=512, stack_out=False):
    if mode == "nn":
        (m, k), n = a.shape, b.shape[1]
    elif mode == "nt":
        (m, k), n = a.shape, b.shape[0]
    else:
        (k, m), n = a.shape, b.shape[1]
    tm, tn, tk = min(tm, m), min(tn, n), min(tk, k)
    assert m % tm == 0 and n % tn == 0 and k % tk == 0, (name, m, n, k, tm, tn, tk)
    nk = k // tk
    if mode == "nn":
        a_spec = pl.BlockSpec((tm, tk), lambda i, j, kk: (i, kk))
        b_spec = pl.BlockSpec((tk, tn), lambda i, j, kk: (kk, j))
        dot = _dot
    elif mode == "nt":
        a_spec = pl.BlockSpec((tm, tk), lambda i, j, kk: (i, kk))
        b_spec = pl.BlockSpec((tn, tk), lambda i, j, kk: (j, kk))
        dot = _dot_nt
    else:
        a_spec = pl.BlockSpec((tk, tm), lambda i, j, kk: (kk, i))
        b_spec = pl.BlockSpec((tk, tn), lambda i, j, kk: (kk, j))
        dot = _dot_tn

    def body(a_ref, b_ref, o_ref, *acc):
        if nk == 1:
            o_ref[...] = dot(a_ref[...], b_ref[...]).astype(out_dtype)
            return
        acc_ref, kk = acc[0], pl.program_id(2)

        @pl.when(kk == 0)
        def _():
            acc_ref[...] = dot(a_ref[...], b_ref[...])

        @pl.when((kk > 0) & (kk < nk - 1))
        def _():
            acc_ref[...] += dot(a_ref[...], b_ref[...])

        @pl.when(kk == nk - 1)
        def _():
            o_ref[...] = (acc_ref[...] + dot(a_ref[...], b_ref[...])).astype(out_dtype)

    if stack_out:
        o_spec, o_shape = pl.BlockSpec((None, tm, tn), lambda i, j, kk: (j, i, 0)), SDS((n // tn, m, tn), out_dtype)
    else:
        o_spec, o_shape = pl.BlockSpec((tm, tn), lambda i, j, kk: (i, j)), SDS((m, n), out_dtype)
    return _pc(body, name, (m // tm, n // tn, nk), [a_spec, b_spec], o_spec, o_shape,
               scratch=[pltpu.VMEM((tm, tn), F32)] if nk > 1 else [],
               sem=("parallel", "parallel", "arbitrary"))(a, b)


PROJ_TN = 768


def _proj_conv(h0_bf, w_cat, conv_w, tm, after):
    t = h0_bf.shape[0]
    n_qkv = QKV_WIDTH // PROJ_TN

    def body(h_ref, w_ref, cw_ref, after_ref, o_ref, act_ref, ds_ref, carry_ref, ext_ref):
        @pl.when(pl.program_id(0) == 0)
        def _():
            carry_ref[...] = jnp.zeros_like(carry_ref)

        h = h_ref[...]

        def project(cb):
            cols = slice(cb * PROJ_TN, (cb + 1) * PROJ_TN)
            o_ref[:, cols] = _dot(h, w_ref[:, cols])

        def conv(cb, part):
            cols = slice(cb * PROJ_TN, (cb + 1) * PROJ_TN)
            if part == 0:
                ext_ref[cb, 0:8, :] = carry_ref[:, cols]
                ext_ref[cb, 8:8 + tm, :] = o_ref[:, cols]
                carry_ref[:, cols] = o_ref[tm - 8:tm, cols]
            w = [cw_ref[pl.ds(k, 1), cols] for k in range(CONV_K)]
            for r in range(part * (tm // 2), (part + 1) * (tm // 2), CONV_ROWS):
                y = _conv_rows(ext_ref.at[cb], w, r, CONV_ROWS)
                s = _sigmoid(y)
                act_ref[pl.ds(r, CONV_ROWS), cols] = y * s
                ds_ref[pl.ds(r, CONV_ROWS), cols] = _mx(s * (1.0 + y * (1.0 - s)))

        pending = [(cb, part) for cb in range(n_qkv) for part in range(2)]
        project(0)
        for cb in range(1, CAT_WIDTH // PROJ_TN):
            project(cb)
            if pending and pending[0][0] < cb:
                conv(*pending.pop(0))
        for cb, part in pending:
            conv(cb, part)

    return _pc(body, "proj_conv", (t // tm,),
               [_row(tm, D_MODEL), _const((D_MODEL, CAT_WIDTH)), _const((CONV_K, QKV_WIDTH)), ANY],
               [_row(tm, CAT_WIDTH), _row(tm, QKV_WIDTH), _row(tm, QKV_WIDTH)],
               [SDS((t, CAT_WIDTH), F32), SDS((t, QKV_WIDTH), F32), SDS((t, QKV_WIDTH), MXU_DTYPE)],
               scratch=[pltpu.VMEM((8, QKV_WIDTH), F32), pltpu.VMEM((n_qkv, 8 + tm, PROJ_TN), F32)],
               sem=("arbitrary",))(h0_bf, w_cat, conv_w, after)


def _ln_stats(x):
    mu = jnp.mean(x, axis=-1, keepdims=True)
    xc = x - mu
    var = jnp.mean(xc * xc, axis=-1, keepdims=True)
    rstd = lax.rsqrt(var + LN_EPS)
    return xc * rstd, rstd


def _ln_bwd(dy, xhat, rstd, g):
    dxh = dy * g
    m1 = jnp.mean(dxh, axis=-1, keepdims=True)
    m2 = jnp.mean(dxh * xhat, axis=-1, keepdims=True)
    return rstd * (dxh - m1 - xhat * m2)


def _ln_in(x, g, b, tm, after):
    t, d = x.shape

    def body(x_ref, g_ref, b_ref, after_ref, h_ref, hb_ref):
        xhat, _ = _ln_stats(x_ref[...])
        h = xhat * g_ref[...] + b_ref[...]
        h_ref[...] = h
        hb_ref[...] = _mx(h)

    return _pc(body, "ln_in", (t // tm,), [_row(tm, d), _const((1, d)), _const((1, d)), ANY],
               [_row(tm, d), _row(tm, d)], [SDS((t, d), F32), SDS((t, d), MXU_DTYPE)],
               sem=("parallel",))(x, g, b, after)


def _pool_fwd(proj, pool_w, tm):
    t = proj.shape[0]
    ublk = K_U // POOL_WIDTH

    def body(u_ref, halo_ref, pw_ref, ypre_ref, d_ref, ext_ref):
        i = pl.program_id(0)
        ext_ref[0:16, :] = jnp.where(i > 0, halo_ref[...], 0.0)
        ext_ref[16:16 + tm, :] = u_ref[...]
        tok = i * tm + lax.broadcasted_iota(jnp.int32, (tm, POOL_GROUP), 0)
        for gi, w in enumerate(POOL_WINDOWS):
            cs = pl.ds(gi * POOL_GROUP, POOL_GROUP)
            ug = ext_ref[pl.ds(16, tm), cs]
            s = ug
            for k in range(1, w):
                s = s + ext_ref[pl.ds(16 - k, tm), cs]
            cnt = jnp.minimum(tok + 1, w).astype(F32)
            db = _mx(s / cnt - ug)
            d_ref[:, gi * POOL_GROUP:(gi + 1) * POOL_GROUP] = db
            ypre_ref[:, gi * POOL_OUT_GROUP:(gi + 1) * POOL_OUT_GROUP] = _dot(db, pw_ref[gi])

    halo = pl.BlockSpec((16, POOL_WIDTH), lambda i: (jnp.maximum(i * (tm // 16) - 1, 0), ublk))
    return _pc(body, "pool_fwd", (t // tm,),
               [pl.BlockSpec((tm, POOL_WIDTH), lambda i: (i, ublk)), halo, _const((4, POOL_GROUP, POOL_OUT_GROUP))],
               [_row(tm, D_MODEL), _row(tm, POOL_WIDTH)],
               [SDS((t, D_MODEL), F32), SDS((t, POOL_WIDTH), MXU_DTYPE)],
               scratch=[pltpu.VMEM((16 + tm, POOL_WIDTH), F32)], sem=("parallel",))(proj, proj, pool_w)


def _pool_bwd(dyp, d_bf, pool_w, dproj, tm):
    t = dyp.shape[0]
    n = t // tm

    def body(dy_ref, dyn_ref, d_ref, pw_ref, dproj_ref, du_ref, dpw_ref, ext_ref):
        i = pl.program_id(0)

        @pl.when(i == 0)
        def _():
            dpw_ref[...] = jnp.zeros_like(dpw_ref)

        tok = i * tm + lax.broadcasted_iota(jnp.int32, (tm + 16, POOL_GROUP), 0)
        for gi, w in enumerate(POOL_WINDOWS):
            dy = dy_ref[:, gi * POOL_OUT_GROUP:(gi + 1) * POOL_OUT_GROUP]
            dyn = dyn_ref[:, gi * POOL_OUT_GROUP:(gi + 1) * POOL_OUT_GROUP]
            pw = pw_ref[gi]
            dd = _dot_nt(dy, pw)
            ddn = jnp.where(i < n - 1, _dot_nt(dyn, pw), 0.0)
            cnt = jnp.minimum(tok + 1, w).astype(F32)
            ext_ref[0:tm, :] = dd / cnt[0:tm]
            ext_ref[tm:tm + 16, :] = ddn / cnt[tm:tm + 16]
            s = ext_ref[pl.ds(0, tm), :]
            for k in range(1, w):
                s = s + ext_ref[pl.ds(k, tm), :]
            du_ref[:, gi * POOL_GROUP:(gi + 1) * POOL_GROUP] = _mx(s - dd)
            dpw_ref[gi] += _dot_tn(d_ref[:, gi * POOL_GROUP:(gi + 1) * POOL_GROUP], dy)

    nxt = pl.BlockSpec((16, D_MODEL), lambda i: (jnp.minimum((i + 1) * (tm // 16), t // 16 - 1), 0))
    return _pc(body, "pool_bwd", (n,),
               [_row(tm, D_MODEL), nxt, _row(tm, POOL_WIDTH), _const((4, POOL_GROUP, POOL_OUT_GROUP)), ANY],
               [pl.BlockSpec((tm, POOL_WIDTH), lambda i: (i, K_U // POOL_WIDTH)),
                _const((4, POOL_GROUP, POOL_OUT_GROUP))],
               [SDS(dproj.shape, dproj.dtype), SDS((4, POOL_GROUP, POOL_OUT_GROUP), F32)],
               scratch=[pltpu.VMEM((tm + 16, POOL_GROUP), F32)], sem=("arbitrary",),
               aliases={4: 0})(dyp, dyp, d_bf, pool_w, dproj)


CONV_BLK = 512


CONV_ROWS = 32


def _conv_rows(ext_ref, w, r, rows):
    y = w[0] * ext_ref[pl.ds(r + 5, rows), :]
    for k in range(1, CONV_K):
        y = y + w[k] * ext_ref[pl.ds(r + 5 + k, rows), :]
    return y


def _conv_bwd(dact, dsilu, proj, conv_w, dproj, tm):
    t = proj.shape[0]
    n = t // tm

    def body(da_ref, dan_ref, ds_ref, dsn_ref, x_ref, xp_ref, w_ref, dproj_ref, dx_ref, dw_ref, ext_ref, dy_ref):
        i = pl.program_id(1)

        @pl.when(i == 0)
        def _():
            dw_ref[...] = jnp.zeros_like(dw_ref)

        ext_ref[0:8, :] = jnp.where(i > 0, xp_ref[...], 0.0)
        ext_ref[8:8 + tm, :] = x_ref[...]
        w = [w_ref[pl.ds(k, 1), :] for k in range(CONV_K)]

        acc = [jnp.zeros((8, CONV_BLK), F32) for _ in range(CONV_K)]
        for r in range(0, tm, CONV_ROWS):
            dy = da_ref[pl.ds(r, CONV_ROWS), :] * ds_ref[pl.ds(r, CONV_ROWS), :].astype(F32)
            dy_ref[pl.ds(r, CONV_ROWS), :] = dy
            for k in range(CONV_K):
                prod = dy * ext_ref[pl.ds(r + 5 + k, CONV_ROWS), :]
                for q in range(0, CONV_ROWS, 8):
                    acc[k] = acc[k] + prod[q:q + 8]
        dy_ref[tm:tm + 8, :] = jnp.where(i < n - 1, dan_ref[...] * dsn_ref[0:8, :].astype(F32), 0.0)
        for k in range(CONV_K):
            dw_ref[pl.ds(k, 1), :] += jnp.sum(acc[k], axis=0, keepdims=True)
        for r in range(0, tm, CONV_ROWS):
            dx = w[0] * dy_ref[pl.ds(r + 3, CONV_ROWS), :]
            for k in range(1, CONV_K):
                dx = dx + w[k] * dy_ref[pl.ds(r + 3 - k, CONV_ROWS), :]
            dx_ref[pl.ds(r, CONV_ROWS), :] = _mx(dx)

    blk = pl.BlockSpec((tm, CONV_BLK), lambda j, i: (i, j))
    prev = pl.BlockSpec((8, CONV_BLK), lambda j, i: (jnp.maximum(i * (tm // 8) - 1, 0), j))
    nxt = pl.BlockSpec((8, CONV_BLK), lambda j, i: (jnp.minimum((i + 1) * (tm // 8), t // 8 - 1), j))
    nxt16 = pl.BlockSpec((16, CONV_BLK), lambda j, i: (jnp.minimum((i + 1) * (tm // 16), t // 16 - 1), j))
    wspec = pl.BlockSpec((CONV_K, CONV_BLK), lambda j, i: (0, j))
    return _pc(body, "conv_bwd", (QKV_WIDTH // CONV_BLK, n),
               [blk, nxt, blk, nxt16, blk, prev, wspec, ANY],
               [blk, pl.BlockSpec((8, CONV_BLK), lambda j, i: (0, j))],
               [SDS(dproj.shape, dproj.dtype), SDS((8, QKV_WIDTH), F32)],
               scratch=[pltpu.VMEM((8 + tm, CONV_BLK), F32), pltpu.VMEM((8 + tm, CONV_BLK), F32)],
               sem=("parallel", "arbitrary"), aliases={7: 0})(dact, dact, dsilu, dsilu, proj, proj, conv_w, dproj)


def _lane(shape):
    return lax.broadcasted_iota(jnp.int32, shape, 1)


def _ba_fwd(proj, al_row, dtb_row, tm):
    t = proj.shape[0]
    bablk = K_BA // 128

    def body(ba_ref, al_ref, dtb_ref, bg_ref):
        ba = ba_ref[...]
        lane = _lane(ba.shape)
        g = -jnp.exp(al_ref[...]) * _softplus(ba + dtb_ref[...])
        bg_ref[...] = jnp.where(lane < HEADS, _sigmoid(ba), jnp.where(lane < 2 * HEADS, g, 0.0))

    return _pc(body, "ba_fwd", (t // tm,),
               [pl.BlockSpec((tm, 128), lambda i: (i, bablk)), _const((1, 128)), _const((1, 128))],
               _row(tm, 128), SDS((t, 128), F32), sem=("parallel",))(proj, al_row, dtb_row)


def _ba_bwd(dbg, bg, proj, al_row, dtb_row, dproj, tm):
    t = proj.shape[0]
    bablk = K_BA // 128

    def body(dbg_ref, bg_ref, ba_ref, al_ref, dtb_ref, dproj_ref, dba_ref, acc_ref):
        i = pl.program_id(0)

        @pl.when(i == 0)
        def _():
            acc_ref[...] = jnp.zeros_like(acc_ref)

        dbg_v, bg_v, ba = dbg_ref[...], bg_ref[...], ba_ref[...]
        lane = _lane(ba.shape)
        is_g = (lane >= HEADS) & (lane < 2 * HEADS)
        dbeta_raw = dbg_v * bg_v * (1.0 - bg_v)
        da_raw = dbg_v * (-jnp.exp(al_ref[...])) * _sigmoid(ba + dtb_ref[...])
        dba_ref[:, 0:128] = _mx(jnp.where(lane < HEADS, dbeta_raw, jnp.where(is_g, da_raw, 0.0)))
        dba_ref[:, 128:CAT_WIDTH - K_BA] = jnp.zeros((tm, CAT_WIDTH - K_BA - 128), dba_ref.dtype)
        acc_ref[0:1, :] += jnp.sum(jnp.where(is_g, dbg_v * bg_v, 0.0), axis=0, keepdims=True)
        acc_ref[1:2, :] += jnp.sum(jnp.where(is_g, da_raw, 0.0), axis=0, keepdims=True)

    tail = CAT_WIDTH - K_BA
    return _pc(body, "ba_bwd", (t // tm,),
               [_row(tm, 128), _row(tm, 128), pl.BlockSpec((tm, 128), lambda i: (i, bablk)),
                _const((1, 128)), _const((1, 128)), ANY],
               [pl.BlockSpec((tm, tail), lambda i: (i, K_BA // tail)), _const((8, 128))],
               [SDS(dproj.shape, dproj.dtype), SDS((8, 128), F32)],
               sem=("arbitrary",), aliases={5: 0})(dbg, bg, proj, al_row, dtb_row, dproj)


def _each(f, *lists):
    return [f(*a) for a in zip(*lists)]


def _rowsum(a):
    return jnp.sum(a, axis=1, keepdims=True)


def _chunk_terms(qs, ks, bgvs, g_rows, hs):
    c = CHUNK
    ii = lax.broadcasted_iota(jnp.int32, (c, c), 0)
    jj = lax.broadcasted_iota(jnp.int32, (c, c), 1)
    lane = _lane((c, 128))
    incl = ii >= jj
    beta = [_rowsum(jnp.where(lane == h, bgv, 0.0)) for h, bgv in zip(hs, bgvs)]
    g_col = [_rowsum(jnp.where(lane == HEADS + h, bgv, 0.0)) for h, bgv in zip(hs, bgvs)]
    rq = _each(lambda q: lax.rsqrt(_rowsum(q * q) + L2_EPS), qs)
    rk = _each(lambda k: lax.rsqrt(_rowsum(k * k) + L2_EPS), ks)
    yq = _each(jnp.multiply, qs, rq)
    kn = _each(jnp.multiply, ks, rk)
    qn = _each(lambda a: a * Q_SCALE, yq)
    gc_col = _each(lambda g: _rowsum(jnp.where(jj <= ii, g, 0.0)), g_rows)
    gc_row = _each(lambda g: jnp.sum(jnp.where(ii <= jj, g, 0.0), axis=0, keepdims=True), g_col)
    dm = _each(lambda a, b: jnp.where(incl, jnp.exp(jnp.where(incl, a - b, 0.0)), 0.0), gc_col, gc_row)
    gl = _each(_rowsum, g_rows)
    eg = _each(jnp.exp, gc_col)
    ek = _each(lambda a, b: jnp.exp(a - b), gl, gc_col)
    egl = _each(jnp.exp, gl)
    kb = _each(jnp.multiply, kn, beta)
    kk = _each(_dot_nt, kb, kn)
    qk = _each(_dot_nt, qn, kn)
    m = _each(lambda a, b: jnp.where(ii > jj, a * b, 0.0), kk, dm)
    attn = _each(jnp.multiply, qk, dm)
    return dict(ii=ii, jj=jj, beta=beta, rq=rq, rk=rk, yq=yq, kn=kn, qn=qn, dm=dm, eg=eg, ek=ek,
                egl=egl, kb=kb, m=m, attn=attn)


def _unit_lower_inverse_minus_identity(ms, ii, jj):
    pair = (ii >> 1) == (jj >> 1)
    ys = _each(lambda m: -jnp.where(pair, m, 0.0), ms)
    s = 1
    while (1 << s) < CHUNK:
        mask = ((ii >> (s + 1)) == (jj >> (s + 1))) & ((ii >> s) != (jj >> s))
        lbs = _each(lambda m: jnp.where(mask, m, 0.0), ms)
        zs = _each(lambda y, lb: lb + _dot(y, lb), ys, lbs)
        ys = _each(lambda y, z: y - z - _dot(z, y), ys, zs)
        s += 1
    return ys


def _dn_fwd(qkv_act, bg, bgt):
    t = qkv_act.shape[0]
    c = CHUNK
    per = DN_FWD_CHUNKS if t % (DN_FWD_CHUNKS * c) == 0 else 1
    nt = t // c
    hs = list(range(HEADS))
    entries = [(s_, h) for s_ in range(per) for h in hs]
    qo = [slice(h * HEAD_DIM, (h + 1) * HEAD_DIM) for h in hs]
    ko = [slice(DN_WIDTH + h * HEAD_DIM, DN_WIDTH + (h + 1) * HEAD_DIM) for h in hs]
    vo = [slice(2 * DN_WIDTH + h * HEAD_DIM, 2 * DN_WIDTH + (h + 1) * HEAD_DIM) for h in hs]

    def body(qkv_ref, bg_ref, bgt_ref, o_ref, u_ref, w_ref, qg_ref, kg_ref, attn_ref, y_ref, vn_ref, st_ref, egl_ref,
             s_ref):
        @pl.when(pl.program_id(0) == 0)
        def _():
            s_ref[...] = jnp.zeros_like(s_ref)

        rows = [pl.ds(s_ * c, c) for s_ in range(per)]
        qs = [qkv_ref[rows[s_], qo[h]] for s_, h in entries]
        ks = [qkv_ref[rows[s_], ko[h]] for s_, h in entries]
        vs = [qkv_ref[rows[s_], vo[h]] for s_, h in entries]
        bgvs = [bg_ref[rows[s_], :] for s_, _ in entries]
        g_rows = [bgt_ref[pl.ds(HEADS + h, 1), rows[s_]] for s_, h in entries]
        ct = _chunk_terms(qs, ks, bgvs, g_rows, [h for _, h in entries])
        ys = _unit_lower_inverse_minus_identity(ct["m"], ct["ii"], ct["jj"])
        vb = _each(jnp.multiply, vs, ct["beta"])
        kbe = _each(jnp.multiply, ct["kb"], ct["eg"])
        us = _each(lambda a, y: a + _dot(y, a), vb, ys)
        ws = _each(lambda a, y: _mx(a + _dot(y, a)), kbe, ys)
        qg = _each(lambda a, b: _mx(a * b), ct["qn"], ct["eg"])
        kg = _each(lambda a, b: _mx(a * b), ct["kn"], ct["ek"])
        attn = _each(_mx, ct["attn"])
        for e, (s_, h) in enumerate(entries):
            u_ref[rows[s_], qo[h]] = us[e]
            w_ref[rows[s_], qo[h]] = ws[e]
            qg_ref[rows[s_], qo[h]] = qg[e]
            kg_ref[rows[s_], qo[h]] = kg[e]
            attn_ref[rows[s_], qo[h]] = attn[e]
            y_ref[rows[s_], qo[h]] = _mx(ys[e])
            egl_ref[s_, h:h + 1, :] = jnp.broadcast_to(ct["egl"][e], (1, HEAD_DIM))
        ss = [s_ref[h] for h in hs]
        for s_ in range(per):
            pick = lambda xs: xs[s_ * HEADS:(s_ + 1) * HEADS]
            sb = _each(_mx, ss)
            vn = _each(lambda a, b, st: a - _dot(b, st), pick(us), pick(ws), sb)
            vnb = _each(_mx, vn)
            oa = _each(_dot, pick(qg), sb)
            ob = _each(_dot, pick(attn), vnb)
            upd = _each(_dot_tn, pick(kg), vnb)
            for h in hs:
                st_ref[s_, h] = ss[h]
                vn_ref[rows[s_], qo[h]] = vnb[h]
                o_ref[rows[s_], qo[h]] = oa[h] + ob[h]
            ss = _each(lambda st, g, d: st * g + d, ss, pick(ct["egl"]), upd)
        for h in hs:
            s_ref[h] = ss[h]

    wide = _row(per * c, DN_WIDTH)
    return _pc(body, "dn_fwd", (nt // per,),
               [_row(per * c, QKV_WIDTH), _row(per * c, 128), pl.BlockSpec((2 * HEADS, per * c), lambda i: (0, i))],
               [wide] * 8 + [pl.BlockSpec((per, HEADS, HEAD_DIM, HEAD_DIM), lambda i: (i, 0, 0, 0)),
                             pl.BlockSpec((per, HEADS, HEAD_DIM), lambda i: (i, 0, 0))],
               [SDS((t, DN_WIDTH), F32), SDS((t, DN_WIDTH), F32)] + [SDS((t, DN_WIDTH), MXU_DTYPE)] * 6
               + [SDS((nt, HEADS, HEAD_DIM, HEAD_DIM), F32), SDS((nt, HEADS, HEAD_DIM), F32)],
               scratch=[pltpu.VMEM((HEADS, HEAD_DIM, HEAD_DIM), F32)], sem=("arbitrary",))(qkv_act, bg, bgt)


def _dn_bwd(do, qkv_act, bg, bgt, u, w, qg, kg, attn, ymat, vn, states, egl):
    t = do.shape[0]
    nt = t // CHUNK
    c = CHUNK
    hs = list(range(HEADS))
    qo = [slice(h * HEAD_DIM, (h + 1) * HEAD_DIM) for h in hs]
    ko = [slice(DN_WIDTH + h * HEAD_DIM, DN_WIDTH + (h + 1) * HEAD_DIM) for h in hs]
    vo = [slice(2 * DN_WIDTH + h * HEAD_DIM, 2 * DN_WIDTH + (h + 1) * HEAD_DIM) for h in hs]

    def body(do_ref, qkv_ref, bg_ref, bgt_ref, u_ref, w_ref, qg_ref, kg_ref, attn_ref, y_ref, vn_ref, st_ref, egl_ref,
             dqkv_ref, dbg_ref, ds_ref):
        @pl.when(pl.program_id(0) == 0)
        def _():
            ds_ref[...] = jnp.zeros_like(ds_ref)

        dsp = [ds_ref[h] for h in hs]
        dsb = _each(_mx, dsp)
        ss = [st_ref[0, h] for h in hs]
        sb = _each(_mx, ss)
        du = [_dot(kg_ref[:, sl], b) + _dot_tn(attn_ref[:, sl], do_ref[:, sl]) for sl, b in zip(qo, dsb)]
        dub = _each(_mx, du)
        dkg_v = [_dot_nt(vn_ref[:, sl], b) for sl, b in zip(qo, dsb)]
        dqg_v = [_dot_nt(do_ref[:, sl], b) for sl, b in zip(qo, sb)]
        dattn_v = [_dot_nt(do_ref[:, sl], vn_ref[:, sl]) for sl in qo]
        dwv = [-_dot_nt(a, b) for a, b in zip(dub, sb)]
        upd = [_dot_tn(qg_ref[:, sl], do_ref[:, sl]) - _dot_tn(w_ref[:, sl], a) for sl, a in zip(qo, dub)]
        degl_v = [jnp.sum(_rowsum(a * b), axis=0, keepdims=True) for a, b in zip(ss, dsp)]
        for h in hs:
            ds_ref[h] = dsp[h] * egl_ref[0, h:h + 1, :] + upd[h]

        bgv = bg_ref[...]
        lane = _lane(bgv.shape)
        rowi = lax.broadcasted_iota(jnp.int32, (c, 1), 0)
        qs = [qkv_ref[:, o] for o in qo]
        ks = [qkv_ref[:, o] for o in ko]
        vs = [qkv_ref[:, o] for o in vo]
        g_rows = [bgt_ref[pl.ds(HEADS + h, 1), :] for h in hs]
        ct = _chunk_terms(qs, ks, [bgv] * HEADS, g_rows, hs)
        ii, jj = ct["ii"], ct["jj"]
        beta, eg, ek, kb, kn, qn, dm = ct["beta"], ct["eg"], ct["ek"], ct["kb"], ct["kn"], ct["qn"], ct["dm"]
        ys = [y_ref[:, o] for o in qo]
        dvb = _each(lambda a, y: a + _dot_tn(y, a), du, ys)
        dkbe = _each(lambda a, y: a + _dot_tn(y, a), dwv, ys)
        dm_u = [_dot_nt(a, u_ref[:, o]) for a, o in zip(dvb, qo)]
        dm_w = [_dot_nt(a, w_ref[:, o]) for a, o in zip(dkbe, qo)]
        dms = _each(lambda a, b: jnp.where(ii > jj, -(a + b), 0.0), dm_u, dm_w)
        dkk = _each(jnp.multiply, dms, dm)
        dqk = _each(jnp.multiply, dattn_v, dm)
        gmat = _each(lambda a, b, c_, d: a * b + c_ * d, dms, ct["m"], dattn_v, ct["attn"])
        dkb = _each(lambda a, b, c_, d: _dot(a, b) + c_ * d, dkk, kn, dkbe, eg)
        dk1 = _each(_dot_tn, dkk, kb)
        dk2 = _each(_dot_tn, dqk, qn)
        dq1 = _each(_dot, dqk, kn)
        dk = _each(lambda a, b, c_, d: a + b + c_ * d, dk1, dk2, dkg_v, ek)
        dq = _each(lambda a, b, c_: a + b * c_, dq1, dqg_v, eg)
        deg = _each(lambda a, b, c_, d: _rowsum(a * b) + _rowsum(c_ * d), dqg_v, qn, dkbe, kb)
        dek = _each(lambda a, b: _rowsum(a * b), dkg_v, kn)
        dgl = _each(lambda a, b, c_, d: jnp.sum(a * b, axis=0, keepdims=True) + c_ * d, dek, ek, degl_v, ct["egl"])
        cs_row = _each(lambda g: jnp.sum(g, axis=0, keepdims=True), gmat)
        cs_col = _each(lambda r: _rowsum(jnp.where(ii == jj, r, 0.0)), cs_row)
        dgc = _each(lambda a, b, c_, d, g, e, f: a * b - c_ * d + _rowsum(g) - e + jnp.where(rowi == c - 1, f, 0.0),
                    deg, eg, dek, ek, gmat, cs_col, dgl)
        dgc_row = _each(lambda a: jnp.sum(jnp.where(ii == jj, a, 0.0), axis=0, keepdims=True), dgc)
        dg = _each(lambda r: _rowsum(jnp.where(jj >= ii, r, 0.0)), dgc_row)
        dbeta = _each(lambda a, b, c_, d: _rowsum(a * b) + _rowsum(c_ * d), dkb, kn, dvb, vs)
        dk = _each(lambda a, b, c_: a + b * c_, dk, dkb, beta)
        dbg = jnp.zeros((c, 128), F32)
        for h in hs:
            dyq = dq[h] * Q_SCALE
            yq = ct["yq"][h]
            dqkv_ref[:, qo[h]] = ct["rq"][h] * (dyq - yq * _rowsum(yq * dyq))
            dqkv_ref[:, ko[h]] = ct["rk"][h] * (dk[h] - kn[h] * _rowsum(kn[h] * dk[h]))
            dqkv_ref[:, vo[h]] = dvb[h] * beta[h]
            dbg = dbg + jnp.where(lane == h, dbeta[h], 0.0) + jnp.where(lane == HEADS + h, dg[h], 0.0)
        dbg_ref[...] = dbg

    rev = pl.BlockSpec((c, DN_WIDTH), lambda i: (nt - 1 - i, 0))
    return _pc(body, "dn_bwd", (nt,),
               [rev, pl.BlockSpec((c, QKV_WIDTH), lambda i: (nt - 1 - i, 0)),
                pl.BlockSpec((c, 128), lambda i: (nt - 1 - i, 0)), pl.BlockSpec((2 * HEADS, c), lambda i: (0, nt - 1 - i))]
               + [rev] * 7
               + [pl.BlockSpec((1, HEADS, HEAD_DIM, HEAD_DIM), lambda i: (nt - 1 - i, 0, 0, 0)),
                  pl.BlockSpec((1, HEADS, HEAD_DIM), lambda i: (nt - 1 - i, 0, 0))],
               [pl.BlockSpec((c, QKV_WIDTH), lambda i: (nt - 1 - i, 0)), pl.BlockSpec((c, 128), lambda i: (nt - 1 - i, 0))],
               [SDS((t, QKV_WIDTH), F32), SDS((t, 128), F32)],
               scratch=[pltpu.VMEM((HEADS, HEAD_DIM, HEAD_DIM), F32)],
               sem=("arbitrary",))(do, qkv_act, bg, bgt, u, w, qg, kg, attn, ymat, vn, states, egl)


MIX_ROWS = 64


def _mix_oproj_ln1(o, proj, ypre, pool_scale, wo_row, w_out, h0, g1, b1, tm):
    t = o.shape[0]

    def body(o_ref, z_ref, ga_ref, gb_ref, yp_ref, ps_ref, wo_ref, w_ref, h0_ref, g_ref, b_ref,
             mixed_ref, a1_ref, h1_ref, h1b_ref):
        for r in range(0, tm, MIX_ROWS):
            rows = pl.ds(r, MIX_ROWS)
            for h in range(HEADS):
                sl = slice(h * HEAD_DIM, (h + 1) * HEAD_DIM)
                oh = o_ref[rows, sl]
                on = oh * lax.rsqrt(jnp.mean(oh * oh, axis=1, keepdims=True) + RMS_EPS)
                zh = z_ref[rows, sl]
                yb = on * wo_ref[:, sl] * (zh * _sigmoid(zh))
                ya = yp_ref[rows, sl] * ps_ref[:, sl]
                mixed_ref[rows, sl] = _mx(_sigmoid(ga_ref[rows, sl]) * ya + _sigmoid(gb_ref[rows, sl]) * yb)
        a1 = ALPHA * h0_ref[...] + _dot(mixed_ref[...], w_ref[...])
        a1_ref[...] = a1
        xhat, _ = _ln_stats(a1)
        h1 = xhat * g_ref[...] + b_ref[...]
        h1_ref[...] = h1
        h1b_ref[...] = _mx(h1)

    def col(blk):
        return pl.BlockSpec((tm, D_MODEL), lambda i: (i, blk))

    r = _row(tm, D_MODEL)
    v = _const((1, D_MODEL))
    return _pc(body, "mix_oproj_ln1", (t // tm,),
               [r, col(K_Z // D_MODEL), col(K_GA // D_MODEL), col(K_GB // D_MODEL), r, v, v,
                _const((D_MODEL, D_MODEL)), r, v, v],
               [r, r, r, r],
               [SDS((t, D_MODEL), MXU_DTYPE), SDS((t, D_MODEL), F32), SDS((t, D_MODEL), F32),
                SDS((t, D_MODEL), MXU_DTYPE)],
               sem=("parallel",))(o, proj, proj, proj, ypre, pool_scale, wo_row, w_out, h0, g1, b1)


def _mix_bwd(da1_bf, w_out, o, proj, ypre, pool_scale, wo_row, tm, after):
    t = o.shape[0]

    def body(da_ref, wout_ref, o_ref, z_ref, ga_ref, gb_ref, yp_ref, ps_ref, wo_ref, after_ref,
             do_ref, dp_ref, dyp_ref, acc_ref, dm_ref):
        i = pl.program_id(0)

        @pl.when(i == 0)
        def _():
            acc_ref[...] = jnp.zeros_like(acc_ref)

        dm_ref[...] = _dot_nt(da_ref[...], wout_ref[...])
        dwo = jnp.zeros((1, HEAD_DIM), F32)
        for h in range(HEADS):
            sl = slice(h * HEAD_DIM, (h + 1) * HEAD_DIM)
            woh = wo_ref[:, sl]
            psh = ps_ref[:, sl]
            dps = jnp.zeros((1, HEAD_DIM), F32)
            for r in range(0, tm, MIX_ROWS):
                rows = pl.ds(r, MIX_ROWS)
                oh = o_ref[rows, sl]
                rs = lax.rsqrt(jnp.mean(oh * oh, axis=1, keepdims=True) + RMS_EPS)
                on = oh * rs
                zh = z_ref[rows, sl]
                sz = _sigmoid(zh)
                silu = zh * sz
                t1 = on * woh
                yb = t1 * silu
                sa = _sigmoid(ga_ref[rows, sl])
                sb = _sigmoid(gb_ref[rows, sl])
                yp = yp_ref[rows, sl]
                dm = dm_ref[rows, sl]
                ga_sl = slice(D_MODEL + h * HEAD_DIM, D_MODEL + (h + 1) * HEAD_DIM)
                gb_sl = slice(2 * D_MODEL + h * HEAD_DIM, 2 * D_MODEL + (h + 1) * HEAD_DIM)
                dp_ref[rows, ga_sl] = _mx(dm * (yp * psh) * sa * (1.0 - sa))
                dp_ref[rows, gb_sl] = _mx(dm * yb * sb * (1.0 - sb))
                dya = dm * sa
                dyb = dm * sb
                dyp_ref[rows, sl] = _mx(dya * psh)
                dps = dps + jnp.sum(dya * yp, axis=0, keepdims=True)
                dp_ref[rows, sl] = _mx(dyb * t1 * (sz * (1.0 + zh * (1.0 - sz))))
                dt1 = dyb * silu
                dwo = dwo + jnp.sum(dt1 * on, axis=0, keepdims=True)
                don = dt1 * woh
                do_ref[rows, sl] = _mx(rs * (don - on * jnp.mean(don * on, axis=1, keepdims=True)))
            acc_ref[0:1, sl] += dps
        acc_ref[1:2, 0:HEAD_DIM] += dwo

    def col(blk):
        return pl.BlockSpec((tm, D_MODEL), lambda i: (i, blk))

    r = _row(tm, D_MODEL)
    return _pc(body, "mix_bwd", (t // tm,),
               [r, _const((D_MODEL, D_MODEL)), r, col(K_Z // D_MODEL), col(K_GA // D_MODEL), col(K_GB // D_MODEL), r,
                _const((1, D_MODEL)), _const((1, D_MODEL)), ANY],
               [r, pl.BlockSpec((tm, 3 * D_MODEL), lambda i: (i, K_Z // (3 * D_MODEL))), r, _const((8, D_MODEL))],
               [SDS((t, D_MODEL), MXU_DTYPE), SDS((t, CAT_WIDTH), MXU_DTYPE), SDS((t, D_MODEL), MXU_DTYPE),
                SDS((8, D_MODEL), F32)],
               scratch=[pltpu.VMEM((tm, D_MODEL), F32)],
               sem=("arbitrary",))(da1_bf, w_out, o, proj, proj, proj, ypre, pool_scale, wo_row, after)


def _mlp_up(h1_bf, w_up, tm):
    t = h1_bf.shape[0]
    tn = w_up.shape[2]

    def body(h_ref, w_ref, act_ref):
        r = jnp.maximum(_dot(h_ref[...], w_ref[...]), 0.0)
        act_ref[...] = _mx(r * r)

    return _pc(body, "mlp_up", (D_FF // tn, t // tm),
               [pl.BlockSpec((tm, D_MODEL), lambda j, i: (i, 0)),
                pl.BlockSpec((None, D_MODEL, tn), lambda j, i: (j, 0, 0))],
               pl.BlockSpec((tm, tn), lambda j, i: (i, j)), SDS((t, D_FF), MXU_DTYPE),
               sem=("parallel", "parallel"))(h1_bf, w_up)


def _tail(act, w_down, h1, w_gate, p, w_proj, tgt, g2, b2, tm):
    t = act.shape[0]

    def body(act_ref, wd_ref, h1_ref, wg_ref, p_ref, wp_ref, tgt_ref, g_ref, b_ref,
             dr_ref, drb_ref, dgp_ref, dpp_ref, rb_ref, acc_ref):
        i = pl.program_id(0)

        @pl.when(i == 0)
        def _():
            acc_ref[...] = jnp.zeros_like(acc_ref)

        r = ALPHA * h1_ref[...] + _dot(act_ref[...], wd_ref[...])
        rb = _mx(r)
        rb_ref[...] = rb
        gate = _sigmoid(_dot(rb, wg_ref[...]))
        pp = _dot(p_ref[...], wp_ref[...])
        xhat, rstd = _ln_stats(r + gate * pp)
        g = g_ref[...]
        diff = xhat * g + b_ref[...] - tgt_ref[...]
        dh2 = diff * (1.0 / D_MODEL)
        rowloss = jnp.sum(diff * diff, axis=1, keepdims=True) * (0.5 / D_MODEL)
        acc_ref[0:1, :] += jnp.sum(dh2 * xhat, axis=0, keepdims=True)
        acc_ref[1:2, :] += jnp.sum(dh2, axis=0, keepdims=True)
        acc_ref[2:3, :] += jnp.broadcast_to(jnp.sum(rowloss, axis=0, keepdims=True), (1, D_MODEL))
        da2 = _ln_bwd(dh2, xhat, rstd, g)
        dpp_ref[...] = _mx(da2 * gate)
        dgp = _mx(da2 * pp * gate * (1.0 - gate))
        dgp_ref[...] = dgp
        dr = da2 + _dot_nt(dgp, wg_ref[...])
        dr_ref[...] = dr
        drb_ref[...] = _mx(dr)

    r = _row(tm, D_MODEL)
    v = _const((1, D_MODEL))
    return _pc(body, "tail", (t // tm,),
               [_row(tm, D_FF), _const((D_FF, D_MODEL)), r, _const((D_MODEL, D_MODEL)), _row(tm, PLE_DIM),
                _const((PLE_DIM, D_MODEL)), r, v, v],
               [r, r, r, r, r, _const((8, D_MODEL))],
               [SDS((t, D_MODEL), F32)] + [SDS((t, D_MODEL), MXU_DTYPE)] * 4 + [SDS((8, D_MODEL), F32)],
               sem=("arbitrary",))(act, w_down, h1, w_gate, p, w_proj, tgt, g2, b2)


SQRT_GUARD = 1e-30


def _mlp_bwd1(dr_bf, w_down, act, tm, tn):
    t = act.shape[0]

    def body(dr_ref, w_ref, act_ref, dup_ref):
        dact = _dot_nt(dr_ref[...], w_ref[...])
        a = act_ref[...].astype(F32)
        dup_ref[...] = _mx(dact * (2.0 * a * lax.rsqrt(a + SQRT_GUARD)))

    o = pl.BlockSpec((tm, tn), lambda j, i: (i, j))
    return _pc(body, "mlp_bwd1", (D_FF // tn, t // tm),
               [pl.BlockSpec((tm, D_MODEL), lambda j, i: (i, 0)), pl.BlockSpec((tn, D_MODEL), lambda j, i: (j, 0)), o],
               o, SDS((t, D_FF), MXU_DTYPE), sem=("parallel", "parallel"))(dr_bf, w_down, act)


def _mlp_bwd2(dup, w_up, dr, a1, g1, tm):
    t = dr.shape[0]

    nk, tk = w_up.shape[0], w_up.shape[2]

    def body(dup_ref, w_ref, dr_ref, a1_ref, g_ref, da1_ref, da1b_ref, acc_ref):
        i = pl.program_id(0)

        @pl.when(i == 0)
        def _():
            acc_ref[...] = jnp.zeros_like(acc_ref)

        dh1 = ALPHA * dr_ref[...]
        for kk in range(nk):
            dh1 = dh1 + _dot_nt(dup_ref[:, kk * tk:(kk + 1) * tk], w_ref[kk])
        xhat, rstd = _ln_stats(a1_ref[...])
        acc_ref[0:1, :] += jnp.sum(dh1 * xhat, axis=0, keepdims=True)
        acc_ref[1:2, :] += jnp.sum(dh1, axis=0, keepdims=True)
        da1 = _ln_bwd(dh1, xhat, rstd, g_ref[...])
        da1_ref[...] = da1
        da1b_ref[...] = _mx(da1)

    r = _row(tm, D_MODEL)
    return _pc(body, "mlp_bwd2", (t // tm,),
               [_row(tm, D_FF), _const((nk, D_MODEL, tk)), r, r, _const((1, D_MODEL))],
               [r, r, _const((8, D_MODEL))],
               [SDS((t, D_MODEL), F32), SDS((t, D_MODEL), MXU_DTYPE), SDS((8, D_MODEL), F32)],
               sem=("arbitrary",))(dup, w_up, dr, a1, g1)


def _ln_in_bwd(dproj, w_cat, da1, x, g, tm, after):
    t = x.shape[0]

    def body(dp_ref, w_ref, da1_ref, x_ref, g_ref, after_ref, dx_ref, acc_ref):
        i = pl.program_id(0)

        @pl.when(i == 0)
        def _():
            acc_ref[...] = jnp.zeros_like(acc_ref)

        dh0 = _dot_nt(dp_ref[...], w_ref[...]) + ALPHA * da1_ref[...]
        xhat, rstd = _ln_stats(x_ref[...])
        acc_ref[0:1, :] += jnp.sum(dh0 * xhat, axis=0, keepdims=True)
        acc_ref[1:2, :] += jnp.sum(dh0, axis=0, keepdims=True)
        dx_ref[...] = _ln_bwd(dh0, xhat, rstd, g_ref[...])

    r = _row(tm, D_MODEL)
    return _pc(body, "ln_in_bwd", (t // tm,),
               [_row(tm, CAT_WIDTH), _const((D_MODEL, CAT_WIDTH)), r, r, _const((1, D_MODEL)), ANY],
               [r, _const((8, D_MODEL))], [SDS((t, D_MODEL), F32), SDS((8, D_MODEL), F32)],
               sem=("arbitrary",))(dproj, w_cat, da1, x, g, after)


def _local_step(x, p, tgt, wts, start_token, first_weights, late_weights, send_late_grads, send_early_grads):
    t = x.shape[0]
    tm = min(512, t)
    tms = min(256, t)
    row = lambda a: a.reshape(1, -1)
    pool_scale = row(wts["pool_scale"])
    wo_row = jnp.tile(row(wts["o_norm_w"]), (1, HEADS))
    pad8 = jnp.zeros((1, HEADS), F32)
    al_row = jnp.concatenate([pad8, row(wts["a_log"]), jnp.zeros((1, 128 - 2 * HEADS), F32)], axis=1)
    dtb_row = jnp.concatenate([pad8, row(wts["dt_bias"]), jnp.zeros((1, 128 - 2 * HEADS), F32)], axis=1)
    g_in, b_in = row(wts["ln_in_g"]), row(wts["ln_in_b"])
    g1, b1 = row(wts["ln1_g"]), row(wts["ln1_b"])
    g2, b2 = row(wts["ln2_g"]), row(wts["ln2_b"])

    h0, h0_bf = _ln_in(x, g_in, b_in, tm, start_token)
    first, first_token = first_weights(h0_bf)
    wts = {**wts, **first}
    w_cat = wts["w_cat"]
    proj, qkv_act, dsilu = _proj_conv(h0_bf, w_cat, wts["conv_w"], tms, first_token)
    ypre, d_bf = _pool_fwd(proj, wts["pool_w"], tm)
    bg = _ba_fwd(proj, al_row, dtb_row, tm)
    bgt = bg[:, :2 * HEADS].T
    o, u, w, qg, kg, attn, ymat, vn, states, egl = _dn_fwd(qkv_act, bg, bgt)
    wts = {**wts, **late_weights(o)}
    mixed, a1, h1, h1_bf = _mix_oproj_ln1(o, proj, ypre, pool_scale, wo_row, wts["w_out"], h0, g1, b1, tms)
    act = _mlp_up(h1_bf, wts["w_up"], tm)
    dr, dr_bf, dgp, dpp, r_bf, acc_tail = _tail(act, wts["w_down"], h1, wts["ple_gate_w"], p, wts["ple_proj_w"],
                                                tgt, g2, b2, tms)
    grads = {}
    grads["ple_proj_w"] = _matmul(p, dpp, "tn", "dw_ple_proj", WIRE_DTYPE, tm=256, tn=1024, tk=DW_TK)
    grads["ple_gate_w"] = _matmul(r_bf, dgp, "tn", "dw_ple_gate", WIRE_DTYPE, tm=DW_TM, tn=1024, tk=DW_TK)
    grads["w_down"] = _matmul(act, dr_bf, "tn", "dw_down", WIRE_DTYPE, tm=DW_TM, tn=1024, tk=DW_TK)
    dup = _mlp_bwd1(dr_bf, wts["w_down"], act, tm, 1024)
    grads["w_up"] = _matmul(h1_bf, dup, "tn", "dw_up", WIRE_DTYPE, tm=DW_TM, tn=1024, tk=DW_TK, stack_out=True)
    da1, da1_bf, acc_ln1 = _mlp_bwd2(dup, wts["w_up"], dr, a1, g1, tms)
    grads["w_out"] = _matmul(mixed, da1_bf, "tn", "dw_out", WIRE_DTYPE, tm=DW_TM, tn=1024, tk=DW_TK)
    sent = send_late_grads(grads)
    do, dproj, dyp, acc_mix = _mix_bwd(da1_bf, wts["w_out"], o, proj, ypre, pool_scale, wo_row, tms, sent)
    dproj, grads["pool_w"] = _pool_bwd(dyp, d_bf, wts["pool_w"], dproj, tm)
    dqkv_act, dbg = _dn_bwd(do, qkv_act, bg, bgt, u, w, qg, kg, attn, ymat, vn, states, egl)
    dproj, acc_conv = _conv_bwd(dqkv_act, dsilu, proj, wts["conv_w"], dproj, tm)
    dproj, acc_ba = _ba_bwd(dbg, bg, proj, al_row, dtb_row, dproj, tm)
    dw_cat = _matmul(h0_bf, dproj, "tn", "dw_in", WIRE_DTYPE, tm=DW_TM, tn=1152, tk=DW_TK)
    grads["w_in"] = _w_in_by_chip(dw_cat)
    sent = send_early_grads(grads)
    grad_x, acc_in = _ln_in_bwd(dproj, w_cat, da1, x, g_in, tms, sent)

    grads["conv_w"] = acc_conv[0:CONV_K]
    grads["ln_in_g"], grads["ln_in_b"] = acc_in[0], acc_in[1]
    grads["ln1_g"], grads["ln1_b"] = acc_ln1[0], acc_ln1[1]
    grads["ln2_g"], grads["ln2_b"] = acc_tail[0], acc_tail[1]
    grads["pool_scale"] = acc_mix[0]
    grads["o_norm_w"] = acc_mix[1, 0:HEAD_DIM]
    grads["a_log"] = acc_ba[0, HEADS:2 * HEADS]
    grads["dt_bias"] = acc_ba[1, HEADS:2 * HEADS]
    loss = acc_tail[2, 0]
    return grad_x, grads, loss


MESH = pl.DeviceIdType.MESH
ANY = pl.BlockSpec(memory_space=pl.ANY)


def _chip_of(k, x, y):
    chip = (2 * x + y + k) % N_CHIPS
    return chip // 2, chip % 2


def _place():
    x, y, c = lax.axis_index("x"), lax.axis_index("y"), lax.axis_index("c")
    return x, y, c, 2 * x + y


def _half(rows, c):
    return pl.ds(pl.multiple_of(c * (rows // 2), 16), rows // 2)


def _remote(src, dst, send_sem, recv_sem, device_id):
    return pltpu.make_async_remote_copy(src_ref=src, dst_ref=dst, send_sem=send_sem, recv_sem=recv_sem,
                                        device_id=device_id, device_id_type=MESH)


def _tile_rows(rows):
    for tr in (256, 128, 64, 32, 16):
        if rows % tr == 0:
            return tr
    raise ValueError(rows)


def _first_gather_copies(srcs, lands, send, recv, place):
    copies = []
    for a in range(len(srcs)):
        whole = a == len(srcs) - 1
        for k in range(N_CHIPS):
            if place is None:
                copies.append(None)
                continue
            x, y, c, me = place
            sems = (send.at[a * N_CHIPS + k], recv.at[a * N_CHIPS + k])
            if k == 0:
                copies.append(_remote(srcs[a], lands[a].at[me], *sems, (x, y, 1 - c)))
                continue
            tx, ty = _chip_of(k, x, y)
            if whole:
                copies.append(_remote(srcs[a], lands[a].at[me], *sems, (tx, ty, c)))
            else:
                mine = _half(srcs[a].shape[0], c)
                copies.append(_remote(srcs[a].at[mine], lands[a].at[me, mine], *sems, (tx, ty, c)))
    return copies


def _pass_halves(stacks):
    n = len(stacks)

    def body(*refs):
        outs = refs[n:2 * n]
        send, recv = refs[2 * n:]
        x, y, c, me = _place()
        copies = []
        for a in range(n):
            for k in range(1, N_CHIPS):
                landed = outs[a].at[(me + N_CHIPS - k) % N_CHIPS, _half(stacks[a].shape[1], c)]
                copies.append(_remote(landed, landed, send.at[a * N_CHIPS + k], recv.at[a * N_CHIPS + k],
                                      (x, y, 1 - c)))
        for cp in copies:
            cp.start()
        for cp in copies:
            cp.wait_send()
        for a in range(n):
            for k in range(1, N_CHIPS):
                passed = outs[a].at[(me + N_CHIPS - k) % N_CHIPS, _half(stacks[a].shape[1], 1 - c)]
                _remote(passed, passed, send.at[a * N_CHIPS + k], recv.at[a * N_CHIPS + k], (x, y, c)).wait_recv()

    sems = pltpu.SemaphoreType.DMA((n * N_CHIPS,))
    return pl.pallas_call(
        body, name="pass_halves", out_shape=[SDS(s.shape, s.dtype) for s in stacks],
        in_specs=[ANY] * n, out_specs=[ANY] * n, scratch_shapes=[sems, sems],
        input_output_aliases={a: a for a in range(n)},
    )(*stacks)


def _swap_halves(gs):
    n = len(gs)

    def body(*refs):
        ins, theirs = refs[0:n], refs[n:2 * n]
        send, recv = refs[2 * n:]
        x, y, c, _ = _place()
        copies = [_remote(ins[a].at[:, _half(gs[a].shape[1], 1 - c)], theirs[a], send.at[a], recv.at[a],
                          (x, y, 1 - c)) for a in range(n)]
        for cp in copies:
            cp.start()
        for cp in copies:
            cp.wait()

    return pl.pallas_call(
        body, name="swap_halves", out_shape=[SDS((N_CHIPS, g.shape[1] // 2, g.shape[2]), g.dtype) for g in gs],
        in_specs=[ANY] * n, out_specs=[ANY] * n, scratch_shapes=[pltpu.SemaphoreType.DMA((n,))] * 2,
    )(*gs)


def _send_to_sibling(hs):
    n = len(hs)

    def body(*refs):
        ins, outs = refs[0:n], refs[n:2 * n]
        send, recv = refs[2 * n:]
        x, y, c, _ = _place()
        copies = [_remote(ins[a], outs[a], send.at[a], recv.at[a], (x, y, 1 - c)) for a in range(n)]
        for cp in copies:
            cp.start()
        for cp in copies:
            cp.wait()

    return pl.pallas_call(
        body, name="send_to_sibling", out_shape=[SDS(h.shape, h.dtype) for h in hs],
        in_specs=[ANY] * n, out_specs=[ANY] * n, scratch_shapes=[pltpu.SemaphoreType.DMA((n,))] * 2,
    )(*hs)


HBM = pl.BlockSpec(memory_space=pltpu.HBM)
SEM = pl.BlockSpec(memory_space=pltpu.SEMAPHORE)
EFFECT = pltpu.SideEffectType.DATAFLOW_SIDE_EFFECTING


def _in_hbm(a):
    return pltpu.with_memory_space_constraint(a, pltpu.HBM)


def _split_copy_start(name, srcs, lands, copies_of, after):
    n = len(srcs)
    n_copies = len(copies_of(srcs, lands, None, None, None))

    def body(*refs):
        src_refs, land_refs = refs[0:n], refs[n:2 * n]
        send, recv = refs[2 * n + 1], refs[2 * n + 2]
        token = refs[-1]
        for cp in copies_of(src_refs, land_refs, send, recv, _place()):
            cp.start()
        token[...] = jnp.zeros_like(token)

    sems = pltpu.SemaphoreType.DMA((n_copies,))
    out = pl.pallas_call(
        body, name=name,
        out_shape=[sems, sems] + [pltpu.HBM(a.shape, a.dtype) for a in list(srcs) + list(lands)] + [SDS((8, 128), F32)],
        in_specs=[HBM] * (2 * n) + [ANY],
        out_specs=[SEM, SEM] + [HBM] * (2 * n) + [pl.BlockSpec(memory_space=pltpu.VMEM)],
        input_output_aliases={i: 2 + i for i in range(2 * n)},
        compiler_params=pltpu.CompilerParams(has_side_effects=EFFECT),
    )(*[_in_hbm(a) for a in list(srcs) + list(lands)], after)
    return out[0], out[1], out[2:2 + n], out[2 + n:2 + 2 * n], out[-1]


def _split_copy_wait(name, send, recv, srcs, lands, after, copies_of):
    n = len(srcs)
    after = list(after) if isinstance(after, (list, tuple)) else [after]

    def body(*refs):
        src_refs, land_refs = refs[0:n], refs[n:2 * n]
        send_ref, recv_ref = refs[2 * n], refs[2 * n + 1]
        for cp in copies_of(src_refs, land_refs, send_ref, recv_ref, _place()):
            cp.wait_send()
            cp.wait_recv()

    out = pl.pallas_call(
        body, name=name, out_shape=[pltpu.HBM(a.shape, a.dtype) for a in list(srcs) + list(lands)],
        in_specs=[HBM] * (2 * n) + [SEM, SEM] + [ANY] * len(after), out_specs=[HBM] * (2 * n),
        input_output_aliases={i: i for i in range(2 * n)},
        compiler_params=pltpu.CompilerParams(has_side_effects=EFFECT),
    )(*srcs, *lands, send, recv, *after)
    return out[0:n], out[n:2 * n]


def _late_gather_copies(srcs, lands, send, recv, place):
    copies = []
    for a in range(len(srcs)):
        for k in range(N_CHIPS):
            if place is None:
                copies.append(None)
                continue
            x, y, c, me = place
            if k == 0:
                target = (x, y, 1 - c)
            else:
                tx, ty = _chip_of(k, x, y)
                target = (tx, ty, c)
            copies.append(_remote(srcs[a], lands[a].at[me], send.at[a * N_CHIPS + k], recv.at[a * N_CHIPS + k], target))
    return copies


def _late_scatter_copies(srcs, lands, send, recv, place):
    copies = []
    for a in range(len(srcs)):
        for k in range(1, N_CHIPS):
            if place is None:
                copies.append(None)
                continue
            x, y, c, _ = place
            tx, ty = _chip_of(k, x, y)
            copies.append(_remote(srcs[a].at[2 * tx + ty], lands[a].at[k - 1], send.at[a * (N_CHIPS - 1) + k - 1],
                                  recv.at[a * (N_CHIPS - 1) + k - 1], (tx, ty, c)))
    return copies


def _add_pair(g, theirs, name):
    _, rows, cols = g.shape
    half = rows // 2
    tr = _tile_rows(half)

    def body(g_ref, t_ref, o_ref):
        own = g_ref[lax.axis_index("c")]
        o_ref[...] = (own.astype(F32) + t_ref[...].astype(F32)).astype(o_ref.dtype)

    blk = pl.BlockSpec((None, tr, cols), lambda j, i: (j, i, 0))
    return _pc(body, "add_" + name, (N_CHIPS, half // tr),
               [pl.BlockSpec((None, 2, tr, cols), lambda j, i: (j, 0, i, 0)), blk], blk,
               SDS((N_CHIPS, half, cols), g.dtype), sem=("parallel", "parallel"))(
                   g.reshape(N_CHIPS, 2, half, cols), theirs)


def _sum_slabs(pair, landed, name):
    _, rows, cols = pair.shape
    tr = _tile_rows(rows)

    def body(p_ref, r_ref, o_ref):
        acc = p_ref[2 * lax.axis_index("x") + lax.axis_index("y")].astype(F32)
        for k in range(N_CHIPS - 1):
            acc = acc + r_ref[k].astype(F32)
        o_ref[...] = acc

    return _pc(body, "sum_" + name, (rows // tr,),
               [pl.BlockSpec((N_CHIPS, tr, cols), lambda i: (0, i, 0)),
                pl.BlockSpec((N_CHIPS - 1, tr, cols), lambda i: (0, i, 0))],
               _row(tr, cols), SDS((rows, cols), F32), sem=("parallel",))(pair, landed)


def _adamw_math(w, g, m, v):
    m = ADAM_B1 * m + (1.0 - ADAM_B1) * g
    v = ADAM_B2 * v + (1.0 - ADAM_B2) * (g * g)
    m_hat = m / (1.0 - ADAM_B1 ** ADAM_STEP)
    v_hat = v / (1.0 - ADAM_B2 ** ADAM_STEP)
    delta = -ADAM_LR * (m_hat / (jnp.sqrt(v_hat) + ADAM_EPS) + ADAM_WD * w)
    return delta, m, v


def _adamw_2d(w, g_own, g_sib, m, v, name, halves):
    lead = w.ndim == 3
    rows, cols = w.shape[-2:]
    tr = _tile_rows(rows // 2)
    nh = rows // 2 // tr if halves else rows // tr

    def body(w_ref, go_ref, gs_ref, m_ref, v_ref, g_out, d_out, m_out, v_out):
        if halves:
            mine = (pl.program_id(0) // nh) == lax.axis_index("c")
            g = jnp.where(mine, go_ref[...], gs_ref[...])
        else:
            g = go_ref[...] + gs_ref[...]
        delta, mn, vn = _adamw_math(w_ref[...], g, m_ref[...], v_ref[...])
        g_out[...] = g
        d_out[...] = delta
        m_out[...] = mn
        v_out[...] = vn

    r = _row(tr, cols)
    p = pl.BlockSpec((None, tr, cols), lambda i: (0, i, 0)) if lead else r
    h = pl.BlockSpec((tr, cols), lambda i: (i % nh, 0))
    return _pc(body, "adamw_" + name, (rows // tr,), [p, h, h, p, p], [r] * 4, [SDS((rows, cols), F32)] * 4,
               sem=("parallel",))(w, g_own, g_sib, m, v)


def _small_allreduce_adamw(mine, w, m, v, sizes):
    shape = mine.shape
    n = len(sizes)

    def body(mine_ref, w_ref, m_ref, v_ref, *rest):
        outs, (buf_ref, res_ref, send_sems, recv_sems) = rest[:-4], rest[-4:]
        x, y, c = lax.axis_index("x"), lax.axis_index("y"), lax.axis_index("c")
        me = 4 * x + 2 * y + c
        buf_ref[me] = mine_ref[...]
        copies = []
        for k in range(1, N_DEV):
            tgt = (me + k) % N_DEV
            copies.append(pltpu.make_async_remote_copy(
                src_ref=mine_ref, dst_ref=buf_ref.at[me], send_sem=send_sems.at[k], recv_sem=recv_sems.at[k],
                device_id=(tgt // 4, (tgt // 2) % 2, tgt % 2), device_id_type=MESH))
        for cp in copies:
            cp.start()
        for k in range(1, N_DEV):
            src = (me + N_DEV - k) % N_DEV
            pltpu.make_async_remote_copy(
                src_ref=mine_ref, dst_ref=buf_ref.at[src], send_sem=send_sems.at[k], recv_sem=recv_sems.at[k],
                device_id=(x, y, c), device_id_type=MESH).wait_recv()
        for cp in copies:
            cp.wait_send()
        g = buf_ref[0]
        for j in range(1, N_DEV):
            g = g + buf_ref[j]
        delta, mn, vn = _adamw_math(w_ref[...], g, m_ref[...], v_ref[...])
        for kind, val in enumerate((g, delta, mn, vn)):
            res_ref[kind] = val
            for i, size in enumerate(sizes):
                outs[kind * (n + 1) + i][...] = res_ref[kind, i:i + 1, 0:size]
            outs[kind * (n + 1) + n][...] = res_ref[kind, SMALL_CONV_AT:SMALL_CONV_AT + SMALL_CONV_ROWS, :]
        outs[-1][...] = res_ref[0, n:n + 1, 0:1]

    vm = pl.BlockSpec(memory_space=pltpu.VMEM)
    per_kind = [SDS((1, size), F32) for size in sizes] + [SDS((SMALL_CONV_ROWS, D_MODEL), F32)]
    out_shape = per_kind * 4 + [SDS((1, 1), F32)]
    out = pl.pallas_call(
        body, name="small_allreduce_adamw", out_shape=out_shape, in_specs=[vm] * 4, out_specs=[vm] * len(out_shape),
        scratch_shapes=[pltpu.VMEM((N_DEV,) + shape, F32), pltpu.VMEM((4,) + shape, F32),
                        pltpu.SemaphoreType.DMA((N_DEV,)), pltpu.SemaphoreType.DMA((N_DEV,))],
    )(mine, w, m, v)
    return [out[kind * (n + 1):(kind + 1) * (n + 1)] for kind in range(4)], out[-1]


def _as2d(a):
    return a.reshape(-1, a.shape[-1])


SEGMENTS = ((C_POOL, K_U, POOL_WIDTH), (C_QKV, K_QKV, QKV_WIDTH), (C_Z, K_Z, DN_WIDTH), (C_BETA, K_BA, 2 * HEADS),
            (C_GA, K_GA, D_MODEL), (C_GB, K_GB, D_MODEL))
SHARD_COLS = IN_WIDTH // N_CHIPS


def _w_cat(stack):
    pieces = []
    for c0, _, width in sorted(SEGMENTS, key=lambda seg: seg[1]):
        a = c0
        while a < c0 + width:
            chip = a // SHARD_COLS
            b = min(c0 + width, (chip + 1) * SHARD_COLS)
            pieces.append(stack[chip][:, a - chip * SHARD_COLS:b - chip * SHARD_COLS])
            a = b
    pieces.append(jnp.zeros((D_MODEL, CAT_WIDTH - K_BA - 2 * HEADS), stack.dtype))
    return jnp.concatenate(pieces, axis=1)


def _w_in_by_chip(dw_cat):
    slabs = []
    for chip in range(N_CHIPS):
        lo, hi = chip * SHARD_COLS, (chip + 1) * SHARD_COLS
        pieces = []
        for c0, k0, width in sorted(SEGMENTS):
            a, b = max(c0, lo), min(c0 + width, hi)
            if a < b:
                pieces.append(dw_cat[:, k0 + a - c0:k0 + b - c0])
        slabs.append(jnp.concatenate(pieces, axis=1))
    return jnp.stack(slabs)


WEIGHT_LAYOUT = {
    "w_in": lambda s: ("w_cat", _w_cat(s)),
    "pool_w": lambda s: ("pool_w", s.reshape(N_CHIPS, 4, POOL_GROUP, POOL_OUT_GROUP // N_CHIPS)
                         .transpose(1, 2, 0, 3).reshape(4, POOL_GROUP, POOL_OUT_GROUP)),
    "w_out": lambda s: ("w_out", s.reshape(D_MODEL, D_MODEL)),
    "w_up": lambda s: ("w_up", s),
    "w_down": lambda s: ("w_down", s.reshape(D_FF, D_MODEL)),
    "ple_gate_w": lambda s: ("ple_gate_w", s.reshape(D_MODEL, D_MODEL)),
    "ple_proj_w": lambda s: ("ple_proj_w", s.transpose(1, 0, 2).reshape(PLE_DIM, D_MODEL)),
}

GRAD_LAYOUT = {
    "w_in": lambda g: g,
    "pool_w": lambda g: g.reshape(4, POOL_GROUP, N_CHIPS, POOL_OUT_GROUP // N_CHIPS)
                         .transpose(2, 0, 1, 3).reshape(N_CHIPS, 4 * POOL_GROUP, POOL_OUT_GROUP // N_CHIPS),
    "w_out": lambda g: g.reshape(N_CHIPS, D_MODEL // N_CHIPS, D_MODEL),
    "w_up": lambda g: g,
    "w_down": lambda g: g.reshape(N_CHIPS, D_FF // N_CHIPS, D_MODEL),
    "ple_gate_w": lambda g: g.reshape(N_CHIPS, D_MODEL // N_CHIPS, D_MODEL),
    "ple_proj_w": lambda g: g.reshape(PLE_DIM, N_CHIPS, D_MODEL // N_CHIPS).transpose(1, 0, 2),
}


def _full_weights(names, stacks):
    return dict(WEIGHT_LAYOUT[n](s.astype(MXU_DTYPE)) for n, s in zip(names, stacks))


def _grads_by_chip(names, grads):
    return [GRAD_LAYOUT[n](grads[n]).astype(WIRE_DTYPE) for n in names]


def _pack_small(rows, conv, name):
    n = len(rows)

    def body(*refs):
        out = refs[n + 1]
        out[...] = jnp.zeros_like(out)
        for i in range(n):
            out[i:i + 1, :] = refs[i][...]
        out[SMALL_CONV_AT:SMALL_CONV_AT + SMALL_CONV_ROWS, :] = refs[n][...]

    vm = pl.BlockSpec(memory_space=pltpu.VMEM)
    return pl.pallas_call(body, name=name, out_shape=SDS((SMALL_CONV_AT + SMALL_CONV_ROWS, D_MODEL), F32),
                          in_specs=[vm] * (n + 1), out_specs=vm)(*rows, conv)


def _pad_row(a):
    a = a.reshape(1, -1).astype(F32)
    return jnp.pad(a, ((0, 0), (0, D_MODEL - a.shape[1])))


def kernel(x, p, ln_in_g, ln_in_b, w_in, pool_w, pool_scale, conv_w, a_log, dt_bias, o_norm_w, w_out, ln1_g, ln1_b, w_up, w_down, ple_gate_w, ple_proj_w, ln2_g, ln2_b, loss_target, m_ln_in_g, m_ln_in_b, m_w_in, m_pool_w, m_pool_scale, m_conv_w, m_a_log, m_dt_bias, m_o_norm_w, m_w_out, m_ln1_g, m_ln1_b, m_w_up, m_w_down, m_ple_gate_w, m_ple_proj_w, m_ln2_g, m_ln2_b, v_ln_in_g, v_ln_in_b, v_w_in, v_pool_w, v_pool_scale, v_conv_w, v_a_log, v_dt_bias, v_o_norm_w, v_w_out, v_ln1_g, v_ln1_b, v_w_up, v_w_down, v_ple_gate_w, v_ple_proj_w, v_ln2_g, v_ln2_b):
    given = dict(locals())
    chip = 2 * lax.axis_index("x") + lax.axis_index("y")

    shard = lambda n: _as2d(given[n]).astype(WIRE_DTYPE)

    wts = {"ln_in_g": ln_in_g, "ln_in_b": ln_in_b, "pool_scale": pool_scale[0], "a_log": a_log[0],
           "dt_bias": dt_bias[0], "o_norm_w": o_norm_w[0], "ln1_g": ln1_g[0], "ln1_b": ln1_b[0],
           "ln2_g": ln2_g[0], "ln2_b": ln2_b[0]}

    conv_pad = jnp.pad(conv_w[0], ((0, 8 - CONV_K), (0, 0)))
    first_srcs = [shard(n) for n in EARLY] + [conv_pad]
    first_lands = [lax.empty((N_CHIPS,) + s.shape, s.dtype) for s in first_srcs]
    fsend, frecv, fsrcs, flands, start_token = _split_copy_start(
        "first_gather_start", first_srcs, first_lands, _first_gather_copies, first_srcs[0])
    late = {}
    for n in ("w_in", "m_w_in", "v_w_in"):
        given[n], _ = lax.optimization_barrier((given[n], start_token))

    def first_weights(after):
        _, lands = _split_copy_wait("first_gather_wait", fsend, frecv, fsrcs, flands,
                                    [after, given["w_in"], given["m_w_in"], given["v_w_in"]], _first_gather_copies)
        stacks = _pass_halves(lands[0:len(EARLY)])
        first = _full_weights(EARLY, stacks)
        first["conv_w"] = jnp.concatenate([lands[len(EARLY)][j, 0:CONV_K] for j in range(N_CHIPS)], axis=1)
        late_srcs = [shard(n) for n in LATE]
        late_lands = [lax.empty((N_CHIPS,) + s.shape, s.dtype) for s in late_srcs]
        late["send"], late["recv"], late["srcs"], late["lands"], token = _split_copy_start(
            "late_gather_start", late_srcs, late_lands, _late_gather_copies, stacks[0])
        return first, token

    def late_weights(after):
        _, stacks = _split_copy_wait("late_gather_wait", late["send"], late["recv"], late["srcs"], late["lands"],
                                     after, _late_gather_copies)
        return _full_weights(LATE, stacks)

    scatter = {}

    def send_late_grads(grads):
        srcs = _grads_by_chip(LATE, grads)
        lands = [lax.empty((N_CHIPS - 1,) + g.shape[1:], g.dtype) for g in srcs]
        scatter["send"], scatter["recv"], scatter["srcs"], scatter["lands"], token = _split_copy_start(
            "late_scatter_start", srcs, lands, _late_scatter_copies, srcs[0])
        return token

    last = {}

    def send_early_grads(grads):
        by_chip = _grads_by_chip(EARLY, grads)
        theirs = _swap_halves(by_chip)
        pair = [_add_pair(g, t, n) for g, t, n in zip(by_chip, theirs, EARLY)]
        lands = [lax.empty((N_CHIPS - 1,) + q.shape[1:], q.dtype) for q in pair]
        last["send"], last["recv"], last["srcs"], last["lands"], token = _split_copy_start(
            "early_scatter_start", pair, lands, _late_scatter_copies, pair[0])
        return token

    grad_x, grads, loss = _local_step(x[0], p[0, 0], loss_target[0], wts, start_token, first_weights, late_weights,
                                      send_late_grads, send_early_grads)

    late_mine, late_landed = _split_copy_wait("late_scatter_wait", scatter["send"], scatter["recv"], scatter["srcs"],
                                              scatter["lands"], grad_x, _late_scatter_copies)
    late_part = [_sum_slabs(q, r, n) for q, r, n in zip(late_mine, late_landed, LATE)]
    pair, landed = _split_copy_wait("early_scatter_wait", last["send"], last["recv"], last["srcs"], last["lands"],
                                    grad_x, _late_scatter_copies)
    reduced = [_sum_slabs(q, r, n) for q, r, n in zip(pair, landed, EARLY)]
    from_sibling = _send_to_sibling(reduced + late_part)
    big_out = {}
    for n, g_own, g_sib in zip(EARLY + LATE, reduced + late_part, from_sibling):
        view = (lambda a: a) if given[n].ndim == 3 else _as2d
        res = _adamw_2d(view(given[n]), g_own, g_sib, view(given["m_" + n]), view(given["v_" + n]), n,
                        halves=n in EARLY)
        big_out[n] = [r.reshape(given[n].shape) for r in res]

    conv_cols = QKV_WIDTH // N_CHIPS

    def small_pack(get, conv, extra, name):
        if conv.shape[1] != QKV_WIDTH:
            conv = lax.dynamic_update_slice(jnp.zeros((CONV_K, QKV_WIDTH), F32), conv, (0, chip * conv_cols))
        return _pack_small([_pad_row(get(n)) for n in SMALL_NAMES] + extra, conv.reshape(SMALL_CONV_ROWS, D_MODEL), name)

    mine_small = small_pack(lambda n: grads[n], grads["conv_w"], [jnp.full((1, D_MODEL), loss, F32)], "pack_small_g")
    packed_small = [small_pack(lambda n: given[prefix + n], given[prefix + "conv_w"][0], [], "pack_small_" + tag)
                    for prefix, tag in (("", "w"), ("m_", "m"), ("v_", "v"))]
    small_out, loss_sum = _small_allreduce_adamw(mine_small, *packed_small, [given[n].size for n in SMALL_NAMES])

    def small_get(k, n):
        if n == "conv_w":
            full = small_out[k][len(SMALL_NAMES)].reshape(CONV_K, QKV_WIDTH)
            return lax.dynamic_slice(full, (0, chip * conv_cols), (CONV_K, conv_cols)).reshape(given[n].shape)
        return small_out[k][SMALL_NAMES.index(n)].reshape(given[n].shape)

    order = ["ln_in_g", "ln_in_b", "w_in", "pool_w", "pool_scale", "conv_w", "a_log", "dt_bias", "o_norm_w", "w_out",
             "ln1_g", "ln1_b", "w_up", "w_down", "ple_gate_w", "ple_proj_w", "ln2_g", "ln2_b"]
    outs = [loss_sum.reshape(()), grad_x[None]]
    for k in range(4):
        for n in order:
            outs.append(big_out[n][k] if n in big_out else small_get(k, n))
    return tuple(outs)
```

```python
import jax
import jax.numpy as jnp
from jax import lax
from jax.experimental import pallas as pl
from jax.experimental.pallas import tpu as pltpu

F32 = jnp.float32
MXU_DTYPE = jnp.bfloat16
WIRE_DTYPE = jnp.bfloat16
SDS = jax.ShapeDtypeStruct

D_MODEL = 1024
POOL_WINDOWS = (2, 4, 8, 16)
POOL_WIDTH = 512
POOL_GROUP = 128
POOL_OUT_GROUP = 256
HEADS = 8
HEAD_DIM = 128
DN_WIDTH = HEADS * HEAD_DIM
QKV_WIDTH = 3 * DN_WIDTH
CONV_K = 4
CHUNK = 128
DN_BWD_CHUNKS = 2
DN_FWD_CHUNKS = 2
DW_TK = 1024
DW_TM = 1024
D_FF = 4096
PLE_DIM = 256
LN_EPS = 1e-5
RMS_EPS = 1e-6
L2_EPS = 1e-6
ALPHA = 2.0 ** 0.25
Q_SCALE = HEAD_DIM ** -0.5
IN_WIDTH = 6672
C_POOL, C_QKV, C_Z, C_BETA, C_A, C_GA, C_GB = 0, 512, 3584, 4608, 4616, 4624, 5648
K_QKV, K_Z, K_GA, K_GB, K_U, K_BA, CAT_WIDTH = 0, 3072, 4096, 5120, 6144, 6656, 6912

ADAM_LR, ADAM_B1, ADAM_B2, ADAM_EPS, ADAM_WD, ADAM_STEP = 0.001, 0.9, 0.999, 1e-08, 0.01, 10

N_CHIPS = 4
N_DEV = 8
VMEM_LIMIT = 56 * 1024 * 1024

EARLY = ("w_in", "pool_w")
LATE = ("w_out", "w_up", "w_down", "ple_gate_w", "ple_proj_w")
SMALL_NAMES = ("ln_in_g", "ln_in_b", "pool_scale", "ln1_g", "ln1_b", "ln2_g", "ln2_b", "o_norm_w", "a_log", "dt_bias")
SMALL_CONV_AT = 12
SMALL_CONV_ROWS = CONV_K * QKV_WIDTH // D_MODEL


def _mx(a):
    return a.astype(MXU_DTYPE)


def _dot(a, b):
    return lax.dot_general(_mx(a), _mx(b), (((1,), (0,)), ((), ())), preferred_element_type=F32)


def _dot_nt(a, b):
    return lax.dot_general(_mx(a), _mx(b), (((1,), (1,)), ((), ())), preferred_element_type=F32)


def _dot_tn(a, b):
    return lax.dot_general(_mx(a), _mx(b), (((0,), (0,)), ((), ())), preferred_element_type=F32)


def _sigmoid(x):
    return 0.5 * jnp.tanh(0.5 * x) + 0.5


def _softplus(x):
    return jnp.maximum(x, 0.0) + jnp.log(1.0 + jnp.exp(-jnp.abs(x)))


def _pc(body, name, grid, in_specs, out_specs, out_shape, scratch=(), sem=None, aliases=None):
    return pl.pallas_call(
        body, out_shape=out_shape, grid=grid, in_specs=in_specs, out_specs=out_specs,
        scratch_shapes=scratch, name=name, input_output_aliases=aliases or {},
        compiler_params=pltpu.CompilerParams(dimension_semantics=sem, vmem_limit_bytes=VMEM_LIMIT))


def _row(tm, n):
    return pl.BlockSpec((tm, n), lambda i: (i, 0))


def _const(shape):
    nd = len(shape)
    return pl.BlockSpec(shape, lambda *_: (0,) * nd)


def _matmul(a, b, mode, name, out_dtype=F32, tm=512, tn=512, tk=512, stack_out=False):
    if mode == "nn":
        (m, k), n = a.shape, b.shape[1]
    elif mode == "nt":
        (m, k), n = a.shape, b.shape[0]
    else:
        (k, m), n = a.shape, b.shape[1]
    tm, tn, tk = min(tm, m), min(tn, n), min(tk, k)
    assert m % tm == 0 and n % tn == 0 and k % tk == 0, (name, m, n, k, tm, tn, tk)
    nk = k // tk
    if mode == "nn":
        a_spec = pl.BlockSpec((tm, tk), lambda i, j, kk: (i, kk))
        b_spec = pl.BlockSpec((tk, tn), lambda i, j, kk: (kk, j))
        dot = _dot
    elif mode == "nt":
        a_spec = pl.BlockSpec((tm, tk), lambda i, j, kk: (i, kk))
        b_spec = pl.BlockSpec((tn, tk), lambda i, j, kk: (j, kk))
        dot = _dot_nt
    else:
        a_spec = pl.BlockSpec((tk, tm), lambda i, j, kk: (kk, i))
        b_spec = pl.BlockSpec((tk, tn), lambda i, j, kk: (kk, j))
        dot = _dot_tn

    def body(a_ref, b_ref, o_ref, *acc):
        if nk == 1:
            o_ref[...] = dot(a_ref[...], b_ref[...]).astype(out_dtype)
            return
        acc_ref, kk = acc[0], pl.program_id(2)

        @pl.when(kk == 0)
        def _():
            acc_ref[...] = dot(a_ref[...], b_ref[...])

        @pl.when((kk > 0) & (kk < nk - 1))
        def _():
            acc_ref[...] += dot(a_ref[...], b_ref[...])

        @pl.when(kk == nk - 1)
        def _():
            o_ref[...] = (acc_ref[...] + dot(a_ref[...], b_ref[...])).astype(out_dtype)

    if stack_out:
        o_spec, o_shape = pl.BlockSpec((None, tm, tn), lambda i, j, kk: (j, i, 0)), SDS((n // tn, m, tn), out_dtype)
    else:
        o_spec, o_shape = pl.BlockSpec((tm, tn), lambda i, j, kk: (i, j)), SDS((m, n), out_dtype)
    return _pc(body, name, (m // tm, n // tn, nk), [a_spec, b_spec], o_spec, o_shape,
               scratch=[pltpu.VMEM((tm, tn), F32)] if nk > 1 else [],
               sem=("parallel", "parallel", "arbitrary"))(a, b)


PROJ_TN = 768


def _proj_conv(h0_bf, w_cat, conv_w, tm, after):
    t = h0_bf.shape[0]
    n_qkv = QKV_WIDTH // PROJ_TN

    def body(h_ref, w_ref, cw_ref, after_ref, o_ref, act_ref, ds_ref, carry_ref, ext_ref):
        @pl.when(pl.program_id(0) == 0)
        def _():
            carry_ref[...] = jnp.zeros_like(carry_ref)

        h = h_ref[...]

        def project(cb):
            cols = slice(cb * PROJ_TN, (cb + 1) * PROJ_TN)
            o_ref[:, cols] = _dot(h, w_ref[:, cols])

        def conv(cb, part):
            cols = slice(cb * PROJ_TN, (cb + 1) * PROJ_TN)
            if part == 0:
                ext_ref[cb, 0:8, :] = carry_ref[:, cols]
                ext_ref[cb, 8:8 + tm, :] = o_ref[:, cols]
                carry_ref[:, cols] = o_ref[tm - 8:tm, cols]
            w = [cw_ref[pl.ds(k, 1), cols] for k in range(CONV_K)]
            for r in range(part * (tm // 2), (part + 1) * (tm // 2), CONV_ROWS):
                y = _conv_rows(ext_ref.at[cb], w, r, CONV_ROWS)
                s = _sigmoid(y)
                act_ref[pl.ds(r, CONV_ROWS), cols] = y * s
                ds_ref[pl.ds(r, CONV_ROWS), cols] = _mx(s * (1.0 + y * (1.0 - s)))

        pending = [(cb, part) for cb in range(n_qkv) for part in range(2)]
        project(0)
        for cb in range(1, CAT_WIDTH // PROJ_TN):
            project(cb)
            if pending and pending[0][0] < cb:
                conv(*pending.pop(0))
        for cb, part in pending:
            conv(cb, part)

    return _pc(body, "proj_conv", (t // tm,),
               [_row(tm, D_MODEL), _const((D_MODEL, CAT_WIDTH)), _const((CONV_K, QKV_WIDTH)), ANY],
               [_row(tm, CAT_WIDTH), _row(tm, QKV_WIDTH), _row(tm, QKV_WIDTH)],
               [SDS((t, CAT_WIDTH), F32), SDS((t, QKV_WIDTH), F32), SDS((t, QKV_WIDTH), MXU_DTYPE)],
               scratch=[pltpu.VMEM((8, QKV_WIDTH), F32), pltpu.VMEM((n_qkv, 8 + tm, PROJ_TN), F32)],
               sem=("arbitrary",))(h0_bf, w_cat, conv_w, after)


def _ln_stats(x):
    mu = jnp.mean(x, axis=-1, keepdims=True)
    xc = x - mu
    var = jnp.mean(xc * xc, axis=-1, keepdims=True)
    rstd = lax.rsqrt(var + LN_EPS)
    return xc * rstd, rstd


def _ln_bwd(dy, xhat, rstd, g):
    dxh = dy * g
    m1 = jnp.mean(dxh, axis=-1, keepdims=True)
    m2 = jnp.mean(dxh * xhat, axis=-1, keepdims=True)
    return rstd * (dxh - m1 - xhat * m2)


def _ln_in(x, g, b, tm, after):
    t, d = x.shape

    def body(x_ref, g_ref, b_ref, after_ref, h_ref, hb_ref):
        xhat, _ = _ln_stats(x_ref[...])
        h = xhat * g_ref[...] + b_ref[...]
        h_ref[...] = h
        hb_ref[...] = _mx(h)

    return _pc(body, "ln_in", (t // tm,), [_row(tm, d), _const((1, d)), _const((1, d)), ANY],
               [_row(tm, d), _row(tm, d)], [SDS((t, d), F32), SDS((t, d), MXU_DTYPE)],
               sem=("parallel",))(x, g, b, after)


def _pool_fwd(proj, pool_w, tm):
    t = proj.shape[0]
    ublk = K_U // POOL_WIDTH

    def body(u_ref, halo_ref, pw_ref, ypre_ref, d_ref, ext_ref):
        i = pl.program_id(0)
        ext_ref[0:16, :] = jnp.where(i > 0, halo_ref[...], 0.0)
        ext_ref[16:16 + tm, :] = u_ref[...]
        tok = i * tm + lax.broadcasted_iota(jnp.int32, (tm, POOL_GROUP), 0)
        for gi, w in enumerate(POOL_WINDOWS):
            cs = pl.ds(gi * POOL_GROUP, POOL_GROUP)
            ug = ext_ref[pl.ds(16, tm), cs]
            s = ug
            for k in range(1, w):
                s = s + ext_ref[pl.ds(16 - k, tm), cs]
            cnt = jnp.minimum(tok + 1, w).astype(F32)
            db = _mx(s / cnt - ug)
            d_ref[:, gi * POOL_GROUP:(gi + 1) * POOL_GROUP] = db
            ypre_ref[:, gi * POOL_OUT_GROUP:(gi + 1) * POOL_OUT_GROUP] = _dot(db, pw_ref[gi])

    halo = pl.BlockSpec((16, POOL_WIDTH), lambda i: (jnp.maximum(i * (tm // 16) - 1, 0), ublk))
    return _pc(body, "pool_fwd", (t // tm,),
               [pl.BlockSpec((tm, POOL_WIDTH), lambda i: (i, ublk)), halo, _const((4, POOL_GROUP, POOL_OUT_GROUP))],
               [_row(tm, D_MODEL), _row(tm, POOL_WIDTH)],
               [SDS((t, D_MODEL), F32), SDS((t, POOL_WIDTH), MXU_DTYPE)],
               scratch=[pltpu.VMEM((16 + tm, POOL_WIDTH), F32)], sem=("parallel",))(proj, proj, pool_w)


def _pool_bwd(dyp, d_bf, pool_w, dproj, tm):
    t = dyp.shape[0]
    n = t // tm

    def body(dy_ref, dyn_ref, d_ref, pw_ref, dproj_ref, du_ref, dpw_ref, ext_ref):
        i = pl.program_id(0)

        @pl.when(i == 0)
        def _():
            dpw_ref[...] = jnp.zeros_like(dpw_ref)

        tok = i * tm + lax.broadcasted_iota(jnp.int32, (tm + 16, POOL_GROUP), 0)
        for gi, w in enumerate(POOL_WINDOWS):
            dy = dy_ref[:, gi * POOL_OUT_GROUP:(gi + 1) * POOL_OUT_GROUP]
            dyn = dyn_ref[:, gi * POOL_OUT_GROUP:(gi + 1) * POOL_OUT_GROUP]
            pw = pw_ref[gi]
            dd = _dot_nt(dy, pw)
            ddn = jnp.where(i < n - 1, _dot_nt(dyn, pw), 0.0)
            cnt = jnp.minimum(tok + 1, w).astype(F32)
            ext_ref[0:tm, :] = dd / cnt[0:tm]
            ext_ref[tm:tm + 16, :] = ddn / cnt[tm:tm + 16]
            s = ext_ref[pl.ds(0, tm), :]
            for k in range(1, w):
                s = s + ext_ref[pl.ds(k, tm), :]
            du_ref[:, gi * POOL_GROUP:(gi + 1) * POOL_GROUP] = _mx(s - dd)
            dpw_ref[gi] += _dot_tn(d_ref[:, gi * POOL_GROUP:(gi + 1) * POOL_GROUP], dy)

    nxt = pl.BlockSpec((16, D_MODEL), lambda i: (jnp.minimum((i + 1) * (tm // 16), t // 16 - 1), 0))
    return _pc(body, "pool_bwd", (n,),
               [_row(tm, D_MODEL), nxt, _row(tm, POOL_WIDTH), _const((4, POOL_GROUP, POOL_OUT_GROUP)), ANY],
               [pl.BlockSpec((tm, POOL_WIDTH), lambda i: (i, K_U // POOL_WIDTH)),
                _const((4, POOL_GROUP, POOL_OUT_GROUP))],
               [SDS(dproj.shape, dproj.dtype), SDS((4, POOL_GROUP, POOL_OUT_GROUP), F32)],
               scratch=[pltpu.VMEM((tm + 16, POOL_GROUP), F32)], sem=("arbitrary",),
               aliases={4: 0})(dyp, dyp, d_bf, pool_w, dproj)


CONV_BLK = 512


CONV_ROWS = 32


def _conv_rows(ext_ref, w, r, rows):
    y = w[0] * ext_ref[pl.ds(r + 5, rows), :]
    for k in range(1, CONV_K):
        y = y + w[k] * ext_ref[pl.ds(r + 5 + k, rows), :]
    return y


def _conv_bwd(dact, dsilu, proj, conv_w, dproj, tm):
    t = proj.shape[0]
    n = t // tm

    def body(da_ref, dan_ref, ds_ref, dsn_ref, x_ref, xp_ref, w_ref, dproj_ref, dx_ref, dw_ref, ext_ref, dy_ref):
        i = pl.program_id(1)

        @pl.when(i == 0)
        def _():
            dw_ref[...] = jnp.zeros_like(dw_ref)

        ext_ref[0:8, :] = jnp.where(i > 0, xp_ref[...], 0.0)
        ext_ref[8:8 + tm, :] = x_ref[...]
        w = [w_ref[pl.ds(k, 1), :] for k in range(CONV_K)]

        acc = [jnp.zeros((8, CONV_BLK), F32) for _ in range(CONV_K)]
        for r in range(0, tm, CONV_ROWS):
            dy = da_ref[pl.ds(r, CONV_ROWS), :] * ds_ref[pl.ds(r, CONV_ROWS), :].astype(F32)
            dy_ref[pl.ds(r, CONV_ROWS), :] = dy
            for k in range(CONV_K):
                prod = dy * ext_ref[pl.ds(r + 5 + k, CONV_ROWS), :]
                for q in range(0, CONV_ROWS, 8):
                    acc[k] = acc[k] + prod[q:q + 8]
        dy_ref[tm:tm + 8, :] = jnp.where(i < n - 1, dan_ref[...] * dsn_ref[0:8, :].astype(F32), 0.0)
        for k in range(CONV_K):
            dw_ref[pl.ds(k, 1), :] += jnp.sum(acc[k], axis=0, keepdims=True)
        for r in range(0, tm, CONV_ROWS):
            dx = w[0] * dy_ref[pl.ds(r + 3, CONV_ROWS), :]
            for k in range(1, CONV_K):
                dx = dx + w[k] * dy_ref[pl.ds(r + 3 - k, CONV_ROWS), :]
            dx_ref[pl.ds(r, CONV_ROWS), :] = _mx(dx)

    blk = pl.BlockSpec((tm, CONV_BLK), lambda j, i: (i, j))
    prev = pl.BlockSpec((8, CONV_BLK), lambda j, i: (jnp.maximum(i * (tm // 8) - 1, 0), j))
    nxt = pl.BlockSpec((8, CONV_BLK), lambda j, i: (jnp.minimum((i + 1) * (tm // 8), t // 8 - 1), j))
    nxt16 = pl.BlockSpec((16, CONV_BLK), lambda j, i: (jnp.minimum((i + 1) * (tm // 16), t // 16 - 1), j))
    wspec = pl.BlockSpec((CONV_K, CONV_BLK), lambda j, i: (0, j))
    return _pc(body, "conv_bwd", (QKV_WIDTH // CONV_BLK, n),
               [blk, nxt, blk, nxt16, blk, prev, wspec, ANY],
               [blk, pl.BlockSpec((8, CONV_BLK), lambda j, i: (0, j))],
               [SDS(dproj.shape, dproj.dtype), SDS((8, QKV_WIDTH), F32)],
               scratch=[pltpu.VMEM((8 + tm, CONV_BLK), F32), pltpu.VMEM((8 + tm, CONV_BLK), F32)],
               sem=("parallel", "arbitrary"), aliases={7: 0})(dact, dact, dsilu, dsilu, proj, proj, conv_w, dproj)


def _lane(shape):
    return lax.broadcasted_iota(jnp.int32, shape, 1)


def _ba_fwd(proj, al_row, dtb_row, tm):
    t = proj.shape[0]
    bablk = K_BA // 128

    def body(ba_ref, al_ref, dtb_ref, bg_ref):
        ba = ba_ref[...]
        lane = _lane(ba.shape)
        g = -jnp.exp(al_ref[...]) * _softplus(ba + dtb_ref[...])
        bg_ref[...] = jnp.where(lane < HEADS, _sigmoid(ba), jnp.where(lane < 2 * HEADS, g, 0.0))

    return _pc(body, "ba_fwd", (t // tm,),
               [pl.BlockSpec((tm, 128), lambda i: (i, bablk)), _const((1, 128)), _const((1, 128))],
               _row(tm, 128), SDS((t, 128), F32), sem=("parallel",))(proj, al_row, dtb_row)


def _ba_bwd(dbg, bg, proj, al_row, dtb_row, dproj, tm):
    t = proj.shape[0]
    bablk = K_BA // 128

    def body(dbg_ref, bg_ref, ba_ref, al_ref, dtb_ref, dproj_ref, dba_ref, acc_ref):
        i = pl.program_id(0)

        @pl.when(i == 0)
        def _():
            acc_ref[...] = jnp.zeros_like(acc_ref)

        dbg_v, bg_v, ba = dbg_ref[...], bg_ref[...], ba_ref[...]
        lane = _lane(ba.shape)
        is_g = (lane >= HEADS) & (lane < 2 * HEADS)
        dbeta_raw = dbg_v * bg_v * (1.0 - bg_v)
        da_raw = dbg_v * (-jnp.exp(al_ref[...])) * _sigmoid(ba + dtb_ref[...])
        dba_ref[:, 0:128] = _mx(jnp.where(lane < HEADS, dbeta_raw, jnp.where(is_g, da_raw, 0.0)))
        dba_ref[:, 128:CAT_WIDTH - K_BA] = jnp.zeros((tm, CAT_WIDTH - K_BA - 128), dba_ref.dtype)
        acc_ref[0:1, :] += jnp.sum(jnp.where(is_g, dbg_v * bg_v, 0.0), axis=0, keepdims=True)
        acc_ref[1:2, :] += jnp.sum(jnp.where(is_g, da_raw, 0.0), axis=0, keepdims=True)

    tail = CAT_WIDTH - K_BA
    return _pc(body, "ba_bwd", (t // tm,),
               [_row(tm, 128), _row(tm, 128), pl.BlockSpec((tm, 128), lambda i: (i, bablk)),
                _const((1, 128)), _const((1, 128)), ANY],
               [pl.BlockSpec((tm, tail), lambda i: (i, K_BA // tail)), _const((8, 128))],
               [SDS(dproj.shape, dproj.dtype), SDS((8, 128), F32)],
               sem=("arbitrary",), aliases={5: 0})(dbg, bg, proj, al_row, dtb_row, dproj)


def _each(f, *lists):
    return [f(*a) for a in zip(*lists)]


def _rowsum(a):
    return jnp.sum(a, axis=1, keepdims=True)


def _chunk_terms(qs, ks, bgvs, g_rows, hs):
    c = CHUNK
    ii = lax.broadcasted_iota(jnp.int32, (c, c), 0)
    jj = lax.broadcasted_iota(jnp.int32, (c, c), 1)
    lane = _lane((c, 128))
    incl = ii >= jj
    beta = [_rowsum(jnp.where(lane == h, bgv, 0.0)) for h, bgv in zip(hs, bgvs)]
    g_col = [_rowsum(jnp.where(lane == HEADS + h, bgv, 0.0)) for h, bgv in zip(hs, bgvs)]
    rq = _each(lambda q: lax.rsqrt(_rowsum(q * q) + L2_EPS), qs)
    rk = _each(lambda k: lax.rsqrt(_rowsum(k * k) + L2_EPS), ks)
    yq = _each(jnp.multiply, qs, rq)
    kn = _each(jnp.multiply, ks, rk)
    qn = _each(lambda a: a * Q_SCALE, yq)
    gc_col = _each(lambda g: _rowsum(jnp.where(jj <= ii, g, 0.0)), g_rows)
    gc_row = _each(lambda g: jnp.sum(jnp.where(ii <= jj, g, 0.0), axis=0, keepdims=True), g_col)
    dm = _each(lambda a, b: jnp.where(incl, jnp.exp(jnp.where(incl, a - b, 0.0)), 0.0), gc_col, gc_row)
    gl = _each(_rowsum, g_rows)
    eg = _each(jnp.exp, gc_col)
    ek = _each(lambda a, b: jnp.exp(a - b), gl, gc_col)
    egl = _each(jnp.exp, gl)
    kb = _each(jnp.multiply, kn, beta)
    kk = _each(_dot_nt, kb, kn)
    qk = _each(_dot_nt, qn, kn)
    m = _each(lambda a, b: jnp.where(ii > jj, a * b, 0.0), kk, dm)
    attn = _each(jnp.multiply, qk, dm)
    return dict(ii=ii, jj=jj, beta=beta, rq=rq, rk=rk, yq=yq, kn=kn, qn=qn, dm=dm, eg=eg, ek=ek,
                egl=egl, kb=kb, m=m, attn=attn)


def _unit_lower_inverse_minus_identity(ms, ii, jj):
    pair = (ii >> 1) == (jj >> 1)
    ys = _each(lambda m: -jnp.where(pair, m, 0.0), ms)
    s = 1
    while (1 << s) < CHUNK:
        mask = ((ii >> (s + 1)) == (jj >> (s + 1))) & ((ii >> s) != (jj >> s))
        lbs = _each(lambda m: jnp.where(mask, m, 0.0), ms)
        zs = _each(lambda y, lb: lb + _dot(y, lb), ys, lbs)
        ys = _each(lambda y, z: y - z - _dot(z, y), ys, zs)
        s += 1
    return ys


def _dn_fwd(qkv_act, bg, bgt):
    t = qkv_act.shape[0]
    c = CHUNK
    per = DN_FWD_CHUNKS if t % (DN_FWD_CHUNKS * c) == 0 else 1
    nt = t // c
    hs = list(range(HEADS))
    entries = [(s_, h) for s_ in range(per) for h in hs]
    qo = [slice(h * HEAD_DIM, (h + 1) * HEAD_DIM) for h in hs]
    ko = [slice(DN_WIDTH + h * HEAD_DIM, DN_WIDTH + (h + 1) * HEAD_DIM) for h in hs]
    vo = [slice(2 * DN_WIDTH + h * HEAD_DIM, 2 * DN_WIDTH + (h + 1) * HEAD_DIM) for h in hs]

    def body(qkv_ref, bg_ref, bgt_ref, o_ref, u_ref, w_ref, qg_ref, kg_ref, attn_ref, y_ref, vn_ref, st_ref, egl_ref,
             s_ref):
        @pl.when(pl.program_id(0) == 0)
        def _():
            s_ref[...] = jnp.zeros_like(s_ref)

        rows = [pl.ds(s_ * c, c) for s_ in range(per)]
        qs = [qkv_ref[rows[s_], qo[h]] for s_, h in entries]
        ks = [qkv_ref[rows[s_], ko[h]] for s_, h in entries]
        vs = [qkv_ref[rows[s_], vo[h]] for s_, h in entries]
        bgvs = [bg_ref[rows[s_], :] for s_, _ in entries]
        g_rows = [bgt_ref[pl.ds(HEADS + h, 1), rows[s_]] for s_, h in entries]
        ct = _chunk_terms(qs, ks, bgvs, g_rows, [h for _, h in entries])
        ys = _unit_lower_inverse_minus_identity(ct["m"], ct["ii"], ct["jj"])
        vb = _each(jnp.multiply, vs, ct["beta"])
        kbe = _each(jnp.multiply, ct["kb"], ct["eg"])
        us = _each(lambda a, y: a + _dot(y, a), vb, ys)
        ws = _each(lambda a, y: _mx(a + _dot(y, a)), kbe, ys)
        qg = _each(lambda a, b: _mx(a * b), ct["qn"], ct["eg"])
        kg = _each(lambda a, b: _mx(a * b), ct["kn"], ct["ek"])
        attn = _each(_mx, ct["attn"])
        for e, (s_, h) in enumerate(entries):
            u_ref[rows[s_], qo[h]] = us[e]
            w_ref[rows[s_], qo[h]] = ws[e]
            qg_ref[rows[s_], qo[h]] = qg[e]
            kg_ref[rows[s_], qo[h]] = kg[e]
            attn_ref[rows[s_], qo[h]] = attn[e]
            y_ref[rows[s_], qo[h]] = _mx(ys[e])
            egl_ref[s_, h:h + 1, :] = jnp.broadcast_to(ct["egl"][e], (1, HEAD_DIM))
        ss = [s_ref[h] for h in hs]
        for s_ in range(per):
            pick = lambda xs: xs[s_ * HEADS:(s_ + 1) * HEADS]
            sb = _each(_mx, ss)
            vn = _each(lambda a, b, st: a - _dot(b, st), pick(us), pick(ws), sb)
            vnb = _each(_mx, vn)
            oa = _each(_dot, pick(qg), sb)
            ob = _each(_dot, pick(attn), vnb)
            upd = _each(_dot_tn, pick(kg), vnb)
            for h in hs:
                st_ref[s_, h] = ss[h]
                vn_ref[rows[s_], qo[h]] = vnb[h]
                o_ref[rows[s_], qo[h]] = oa[h] + ob[h]
            ss = _each(lambda st, g, d: st * g + d, ss, pick(ct["egl"]), upd)
        for h in hs:
            s_ref[h] = ss[h]

    wide = _row(per * c, DN_WIDTH)
    return _pc(body, "dn_fwd", (nt // per,),
               [_row(per * c, QKV_WIDTH), _row(per * c, 128), pl.BlockSpec((2 * HEADS, per * c), lambda i: (0, i))],
               [wide] * 8 + [pl.BlockSpec((per, HEADS, HEAD_DIM, HEAD_DIM), lambda i: (i, 0, 0, 0)),
                             pl.BlockSpec((per, HEADS, HEAD_DIM), lambda i: (i, 0, 0))],
               [SDS((t, DN_WIDTH), F32), SDS((t, DN_WIDTH), F32)] + [SDS((t, DN_WIDTH), MXU_DTYPE)] * 6
               + [SDS((nt, HEADS, HEAD_DIM, HEAD_DIM), F32), SDS((nt, HEADS, HEAD_DIM), F32)],
               scratch=[pltpu.VMEM((HEADS, HEAD_DIM, HEAD_DIM), F32)], sem=("arbitrary",))(qkv_act, bg, bgt)


def _dn_bwd(do, qkv_act, bg, bgt, u, w, qg, kg, attn, ymat, vn, states, egl):
    t = do.shape[0]
    c = CHUNK
    per = DN_BWD_CHUNKS if t % (DN_BWD_CHUNKS * c) == 0 else 1
    nt = t // c
    hs = list(range(HEADS))
    entries = [(s_, h) for s_ in range(per) for h in hs]
    qo = [slice(h * HEAD_DIM, (h + 1) * HEAD_DIM) for h in hs]
    ko = [slice(DN_WIDTH + h * HEAD_DIM, DN_WIDTH + (h + 1) * HEAD_DIM) for h in hs]
    vo = [slice(2 * DN_WIDTH + h * HEAD_DIM, 2 * DN_WIDTH + (h + 1) * HEAD_DIM) for h in hs]

    def body(do_ref, qkv_ref, bg_ref, bgt_ref, u_ref, w_ref, qg_ref, kg_ref, attn_ref, y_ref, vn_ref, st_ref, egl_ref,
             dqkv_ref, dbg_ref, ds_ref):
        @pl.when(pl.program_id(0) == 0)
        def _():
            ds_ref[...] = jnp.zeros_like(ds_ref)

        rows = [pl.ds(s_ * c, c) for s_ in range(per)]
        scan = {}
        dsp = [ds_ref[h] for h in hs]
        for s_ in reversed(range(per)):
            r = rows[s_]
            dsb = _each(_mx, dsp)
            ss = [st_ref[s_, h] for h in hs]
            sb = _each(_mx, ss)
            du_s = [_dot(kg_ref[r, sl], b) + _dot_tn(attn_ref[r, sl], do_ref[r, sl]) for sl, b in zip(qo, dsb)]
            dub = _each(_mx, du_s)
            scan[s_] = dict(
                du=du_s,
                dkg=[_dot_nt(vn_ref[r, sl], b) for sl, b in zip(qo, dsb)],
                dqg=[_dot_nt(do_ref[r, sl], b) for sl, b in zip(qo, sb)],
                dattn=[_dot_nt(do_ref[r, sl], vn_ref[r, sl]) for sl in qo],
                dw=[-_dot_nt(a, b) for a, b in zip(dub, sb)],
                degl=[jnp.sum(_rowsum(a * b), axis=0, keepdims=True) for a, b in zip(ss, dsp)])
            upd = [_dot_tn(qg_ref[r, sl], do_ref[r, sl]) - _dot_tn(w_ref[r, sl], a) for sl, a in zip(qo, dub)]
            dsp = [dsp[h] * egl_ref[s_, h:h + 1, :] + upd[h] for h in hs]
        for h in hs:
            ds_ref[h] = dsp[h]
        gather = lambda key: [scan[s_][key][h] for s_, h in entries]
        du, dkg_v, dqg_v, dattn_v, dwv, degl_v = (gather(k) for k in ("du", "dkg", "dqg", "dattn", "dw", "degl"))

        lane = _lane((c, 128))
        rowi = lax.broadcasted_iota(jnp.int32, (c, 1), 0)
        qs = [qkv_ref[rows[s_], qo[h]] for s_, h in entries]
        ks = [qkv_ref[rows[s_], ko[h]] for s_, h in entries]
        vs = [qkv_ref[rows[s_], vo[h]] for s_, h in entries]
        bgvs = [bg_ref[rows[s_], :] for s_, _ in entries]
        g_rows = [bgt_ref[pl.ds(HEADS + h, 1), rows[s_]] for s_, h in entries]
        ct = _chunk_terms(qs, ks, bgvs, g_rows, [h for _, h in entries])
        ii, jj = ct["ii"], ct["jj"]
        beta, eg, ek, kb, kn, qn, dm = ct["beta"], ct["eg"], ct["ek"], ct["kb"], ct["kn"], ct["qn"], ct["dm"]
        ys = [y_ref[rows[s_], qo[h]] for s_, h in entries]
        dvb = _each(lambda a, y: a + _dot_tn(y, a), du, ys)
        dkbe = _each(lambda a, y: a + _dot_tn(y, a), dwv, ys)
        dm_u = [_dot_nt(a, u_ref[rows[s_], qo[h]]) for a, (s_, h) in zip(dvb, entries)]
        dm_w = [_dot_nt(a, w_ref[rows[s_], qo[h]]) for a, (s_, h) in zip(dkbe, entries)]
        dms = _each(lambda a, b: jnp.where(ii > jj, -(a + b), 0.0), dm_u, dm_w)
        dkk = _each(jnp.multiply, dms, dm)
        dqk = _each(jnp.multiply, dattn_v, dm)
        gmat = _each(lambda a, b, c_, d: a * b + c_ * d, dms, ct["m"], dattn_v, ct["attn"])
        dkb = _each(lambda a, b, c_, d: _dot(a, b) + c_ * d, dkk, kn, dkbe, eg)
        dk1 = _each(_dot_tn, dkk, kb)
        dk2 = _each(_dot_tn, dqk, qn)
        dq1 = _each(_dot, dqk, kn)
        dk = _each(lambda a, b, c_, d: a + b + c_ * d, dk1, dk2, dkg_v, ek)
        dq = _each(lambda a, b, c_: a + b * c_, dq1, dqg_v, eg)
        deg = _each(lambda a, b, c_, d: _rowsum(a * b) + _rowsum(c_ * d), dqg_v, qn, dkbe, kb)
        dek = _each(lambda a, b: _rowsum(a * b), dkg_v, kn)
        dgl = _each(lambda a, b, c_, d: jnp.sum(a * b, axis=0, keepdims=True) + c_ * d, dek, ek, degl_v, ct["egl"])
        cs_row = _each(lambda g: jnp.sum(g, axis=0, keepdims=True), gmat)
        cs_col = _each(lambda r: _rowsum(jnp.where(ii == jj, r, 0.0)), cs_row)
        dgc = _each(lambda a, b, c_, d, g, e, f: a * b - c_ * d + _rowsum(g) - e + jnp.where(rowi == c - 1, f, 0.0),
                    deg, eg, dek, ek, gmat, cs_col, dgl)
        dgc_row = _each(lambda a: jnp.sum(jnp.where(ii == jj, a, 0.0), axis=0, keepdims=True), dgc)
        dg = _each(lambda r: _rowsum(jnp.where(jj >= ii, r, 0.0)), dgc_row)
        dbeta = _each(lambda a, b, c_, d: _rowsum(a * b) + _rowsum(c_ * d), dkb, kn, dvb, vs)
        dk = _each(lambda a, b, c_: a + b * c_, dk, dkb, beta)
        dbg = [jnp.zeros((c, 128), F32) for _ in range(per)]
        for e, (s_, h) in enumerate(entries):
            dyq = dq[e] * Q_SCALE
            yq = ct["yq"][e]
            dqkv_ref[rows[s_], qo[h]] = ct["rq"][e] * (dyq - yq * _rowsum(yq * dyq))
            dqkv_ref[rows[s_], ko[h]] = ct["rk"][e] * (dk[e] - kn[e] * _rowsum(kn[e] * dk[e]))
            dqkv_ref[rows[s_], vo[h]] = dvb[e] * beta[e]
            dbg[s_] = dbg[s_] + jnp.where(lane == h, dbeta[e], 0.0) + jnp.where(lane == HEADS + h, dg[e], 0.0)
        for s_ in range(per):
            dbg_ref[rows[s_], :] = dbg[s_]

    ns = nt // per
    rev = pl.BlockSpec((per * c, DN_WIDTH), lambda i: (ns - 1 - i, 0))
    return _pc(body, "dn_bwd", (ns,),
               [rev, pl.BlockSpec((per * c, QKV_WIDTH), lambda i: (ns - 1 - i, 0)),
                pl.BlockSpec((per * c, 128), lambda i: (ns - 1 - i, 0)),
                pl.BlockSpec((2 * HEADS, per * c), lambda i: (0, ns - 1 - i))]
               + [rev] * 7
               + [pl.BlockSpec((per, HEADS, HEAD_DIM, HEAD_DIM), lambda i: (ns - 1 - i, 0, 0, 0)),
                  pl.BlockSpec((per, HEADS, HEAD_DIM), lambda i: (ns - 1 - i, 0, 0))],
               [pl.BlockSpec((per * c, QKV_WIDTH), lambda i: (ns - 1 - i, 0)),
                pl.BlockSpec((per * c, 128), lambda i: (ns - 1 - i, 0))],
               [SDS((t, QKV_WIDTH), F32), SDS((t, 128), F32)],
               scratch=[pltpu.VMEM((HEADS, HEAD_DIM, HEAD_DIM), F32)],
               sem=("arbitrary",))(do, qkv_act, bg, bgt, u, w, qg, kg, attn, ymat, vn, states, egl)


MIX_ROWS = 64


def _mix_oproj_ln1(o, proj, ypre, pool_scale, wo_row, w_out, h0, g1, b1, tm):
    t = o.shape[0]

    def body(o_ref, z_ref, ga_ref, gb_ref, yp_ref, ps_ref, wo_ref, w_ref, h0_ref, g_ref, b_ref,
             mixed_ref, a1_ref, h1_ref, h1b_ref):
        for r in range(0, tm, MIX_ROWS):
            rows = pl.ds(r, MIX_ROWS)
            for h in range(HEADS):
                sl = slice(h * HEAD_DIM, (h + 1) * HEAD_DIM)
                oh = o_ref[rows, sl]
                on = oh * lax.rsqrt(jnp.mean(oh * oh, axis=1, keepdims=True) + RMS_EPS)
                zh = z_ref[rows, sl]
                yb = on * wo_ref[:, sl] * (zh * _sigmoid(zh))
                ya = yp_ref[rows, sl] * ps_ref[:, sl]
                mixed_ref[rows, sl] = _mx(_sigmoid(ga_ref[rows, sl]) * ya + _sigmoid(gb_ref[rows, sl]) * yb)
        a1 = ALPHA * h0_ref[...] + _dot(mixed_ref[...], w_ref[...])
        a1_ref[...] = a1
        xhat, _ = _ln_stats(a1)
        h1 = xhat * g_ref[...] + b_ref[...]
        h1_ref[...] = h1
        h1b_ref[...] = _mx(h1)

    def col(blk):
        return pl.BlockSpec((tm, D_MODEL), lambda i: (i, blk))

    r = _row(tm, D_MODEL)
    v = _const((1, D_MODEL))
    return _pc(body, "mix_oproj_ln1", (t // tm,),
               [r, col(K_Z // D_MODEL), col(K_GA // D_MODEL), col(K_GB // D_MODEL), r, v, v,
                _const((D_MODEL, D_MODEL)), r, v, v],
               [r, r, r, r],
               [SDS((t, D_MODEL), MXU_DTYPE), SDS((t, D_MODEL), F32), SDS((t, D_MODEL), F32),
                SDS((t, D_MODEL), MXU_DTYPE)],
               sem=("parallel",))(o, proj, proj, proj, ypre, pool_scale, wo_row, w_out, h0, g1, b1)


def _mix_bwd(da1_bf, w_out, o, proj, ypre, pool_scale, wo_row, tm, after):
    t = o.shape[0]

    def body(da_ref, wout_ref, o_ref, z_ref, ga_ref, gb_ref, yp_ref, ps_ref, wo_ref, after_ref,
             do_ref, dp_ref, dyp_ref, acc_ref, dm_ref):
        i = pl.program_id(0)

        @pl.when(i == 0)
        def _():
            acc_ref[...] = jnp.zeros_like(acc_ref)

        dm_ref[...] = _dot_nt(da_ref[...], wout_ref[...])
        dwo = jnp.zeros((1, HEAD_DIM), F32)
        for h in range(HEADS):
            sl = slice(h * HEAD_DIM, (h + 1) * HEAD_DIM)
            woh = wo_ref[:, sl]
            psh = ps_ref[:, sl]
            dps = jnp.zeros((1, HEAD_DIM), F32)
            for r in range(0, tm, MIX_ROWS):
                rows = pl.ds(r, MIX_ROWS)
                oh = o_ref[rows, sl]
                rs = lax.rsqrt(jnp.mean(oh * oh, axis=1, keepdims=True) + RMS_EPS)
                on = oh * rs
                zh = z_ref[rows, sl]
                sz = _sigmoid(zh)
                silu = zh * sz
                t1 = on * woh
                yb = t1 * silu
                sa = _sigmoid(ga_ref[rows, sl])
                sb = _sigmoid(gb_ref[rows, sl])
                yp = yp_ref[rows, sl]
                dm = dm_ref[rows, sl]
                ga_sl = slice(D_MODEL + h * HEAD_DIM, D_MODEL + (h + 1) * HEAD_DIM)
                gb_sl = slice(2 * D_MODEL + h * HEAD_DIM, 2 * D_MODEL + (h + 1) * HEAD_DIM)
                dp_ref[rows, ga_sl] = _mx(dm * (yp * psh) * sa * (1.0 - sa))
                dp_ref[rows, gb_sl] = _mx(dm * yb * sb * (1.0 - sb))
                dya = dm * sa
                dyb = dm * sb
                dyp_ref[rows, sl] = _mx(dya * psh)
                dps = dps + jnp.sum(dya * yp, axis=0, keepdims=True)
                dp_ref[rows, sl] = _mx(dyb * t1 * (sz * (1.0 + zh * (1.0 - sz))))
                dt1 = dyb * silu
                dwo = dwo + jnp.sum(dt1 * on, axis=0, keepdims=True)
                don = dt1 * woh
                do_ref[rows, sl] = _mx(rs * (don - on * jnp.mean(don * on, axis=1, keepdims=True)))
            acc_ref[0:1, sl] += dps
        acc_ref[1:2, 0:HEAD_DIM] += dwo

    def col(blk):
        return pl.BlockSpec((tm, D_MODEL), lambda i: (i, blk))

    r = _row(tm, D_MODEL)
    return _pc(body, "mix_bwd", (t // tm,),
               [r, _const((D_MODEL, D_MODEL)), r, col(K_Z // D_MODEL), col(K_GA // D_MODEL), col(K_GB // D_MODEL), r,
                _const((1, D_MODEL)), _const((1, D_MODEL)), ANY],
               [r, pl.BlockSpec((tm, 3 * D_MODEL), lambda i: (i, K_Z // (3 * D_MODEL))), r, _const((8, D_MODEL))],
               [SDS((t, D_MODEL), MXU_DTYPE), SDS((t, CAT_WIDTH), MXU_DTYPE), SDS((t, D_MODEL), MXU_DTYPE),
                SDS((8, D_MODEL), F32)],
               scratch=[pltpu.VMEM((tm, D_MODEL), F32)],
               sem=("arbitrary",))(da1_bf, w_out, o, proj, proj, proj, ypre, pool_scale, wo_row, after)


def _mlp_up(h1_bf, w_up, tm):
    t = h1_bf.shape[0]
    tn = w_up.shape[2]

    def body(h_ref, w_ref, act_ref):
        r = jnp.maximum(_dot(h_ref[...], w_ref[...]), 0.0)
        act_ref[...] = _mx(r * r)

    return _pc(body, "mlp_up", (D_FF // tn, t // tm),
               [pl.BlockSpec((tm, D_MODEL), lambda j, i: (i, 0)),
                pl.BlockSpec((None, D_MODEL, tn), lambda j, i: (j, 0, 0))],
               pl.BlockSpec((tm, tn), lambda j, i: (i, j)), SDS((t, D_FF), MXU_DTYPE),
               sem=("parallel", "parallel"))(h1_bf, w_up)


def _tail(act, w_down, h1, w_gate, p, w_proj, tgt, g2, b2, tm):
    t = act.shape[0]

    def body(act_ref, wd_ref, h1_ref, wg_ref, p_ref, wp_ref, tgt_ref, g_ref, b_ref,
             dr_ref, drb_ref, dgp_ref, dpp_ref, rb_ref, acc_ref):
        i = pl.program_id(0)

        @pl.when(i == 0)
        def _():
            acc_ref[...] = jnp.zeros_like(acc_ref)

        r = ALPHA * h1_ref[...] + _dot(act_ref[...], wd_ref[...])
        rb = _mx(r)
        rb_ref[...] = rb
        gate = _sigmoid(_dot(rb, wg_ref[...]))
        pp = _dot(p_ref[...], wp_ref[...])
        xhat, rstd = _ln_stats(r + gate * pp)
        g = g_ref[...]
        diff = xhat * g + b_ref[...] - tgt_ref[...]
        dh2 = diff * (1.0 / D_MODEL)
        rowloss = jnp.sum(diff * diff, axis=1, keepdims=True) * (0.5 / D_MODEL)
        acc_ref[0:1, :] += jnp.sum(dh2 * xhat, axis=0, keepdims=True)
        acc_ref[1:2, :] += jnp.sum(dh2, axis=0, keepdims=True)
        acc_ref[2:3, :] += jnp.broadcast_to(jnp.sum(rowloss, axis=0, keepdims=True), (1, D_MODEL))
        da2 = _ln_bwd(dh2, xhat, rstd, g)
        dpp_ref[...] = _mx(da2 * gate)
        dgp = _mx(da2 * pp * gate * (1.0 - gate))
        dgp_ref[...] = dgp
        dr = da2 + _dot_nt(dgp, wg_ref[...])
        dr_ref[...] = dr
        drb_ref[...] = _mx(dr)

    r = _row(tm, D_MODEL)
    v = _const((1, D_MODEL))
    return _pc(body, "tail", (t // tm,),
               [_row(tm, D_FF), _const((D_FF, D_MODEL)), r, _const((D_MODEL, D_MODEL)), _row(tm, PLE_DIM),
                _const((PLE_DIM, D_MODEL)), r, v, v],
               [r, r, r, r, r, _const((8, D_MODEL))],
               [SDS((t, D_MODEL), F32)] + [SDS((t, D_MODEL), MXU_DTYPE)] * 4 + [SDS((8, D_MODEL), F32)],
               sem=("arbitrary",))(act, w_down, h1, w_gate, p, w_proj, tgt, g2, b2)


SQRT_GUARD = 1e-30


def _mlp_bwd1(dr_bf, w_down, act, tm, tn):
    t = act.shape[0]

    def body(dr_ref, w_ref, act_ref, dup_ref):
        dact = _dot_nt(dr_ref[...], w_ref[...])
        a = act_ref[...].astype(F32)
        dup_ref[...] = _mx(dact * (2.0 * a * lax.rsqrt(a + SQRT_GUARD)))

    o = pl.BlockSpec((tm, tn), lambda j, i: (i, j))
    return _pc(body, "mlp_bwd1", (D_FF // tn, t // tm),
               [pl.BlockSpec((tm, D_MODEL), lambda j, i: (i, 0)), pl.BlockSpec((tn, D_MODEL), lambda j, i: (j, 0)), o],
               o, SDS((t, D_FF), MXU_DTYPE), sem=("parallel", "parallel"))(dr_bf, w_down, act)


def _mlp_bwd2(dup, w_up, dr, a1, g1, tm):
    t = dr.shape[0]

    nk, tk = w_up.shape[0], w_up.shape[2]

    def body(dup_ref, w_ref, dr_ref, a1_ref, g_ref, da1_ref, da1b_ref, acc_ref):
        i = pl.program_id(0)

        @pl.when(i == 0)
        def _():
            acc_ref[...] = jnp.zeros_like(acc_ref)

        dh1 = ALPHA * dr_ref[...]
        for kk in range(nk):
            dh1 = dh1 + _dot_nt(dup_ref[:, kk * tk:(kk + 1) * tk], w_ref[kk])
        xhat, rstd = _ln_stats(a1_ref[...])
        acc_ref[0:1, :] += jnp.sum(dh1 * xhat, axis=0, keepdims=True)
        acc_ref[1:2, :] += jnp.sum(dh1, axis=0, keepdims=True)
        da1 = _ln_bwd(dh1, xhat, rstd, g_ref[...])
        da1_ref[...] = da1
        da1b_ref[...] = _mx(da1)

    r = _row(tm, D_MODEL)
    return _pc(body, "mlp_bwd2", (t // tm,),
               [_row(tm, D_FF), _const((nk, D_MODEL, tk)), r, r, _const((1, D_MODEL))],
               [r, r, _const((8, D_MODEL))],
               [SDS((t, D_MODEL), F32), SDS((t, D_MODEL), MXU_DTYPE), SDS((8, D_MODEL), F32)],
               sem=("arbitrary",))(dup, w_up, dr, a1, g1)


def _ln_in_bwd(dproj, w_cat, da1, x, g, tm, after):
    t = x.shape[0]

    def body(dp_ref, w_ref, da1_ref, x_ref, g_ref, after_ref, dx_ref, acc_ref):
        i = pl.program_id(0)

        @pl.when(i == 0)
        def _():
            acc_ref[...] = jnp.zeros_like(acc_ref)

        dh0 = _dot_nt(dp_ref[...], w_ref[...]) + ALPHA * da1_ref[...]
        xhat, rstd = _ln_stats(x_ref[...])
        acc_ref[0:1, :] += jnp.sum(dh0 * xhat, axis=0, keepdims=True)
        acc_ref[1:2, :] += jnp.sum(dh0, axis=0, keepdims=True)
        dx_ref[...] = _ln_bwd(dh0, xhat, rstd, g_ref[...])

    r = _row(tm, D_MODEL)
    return _pc(body, "ln_in_bwd", (t // tm,),
               [_row(tm, CAT_WIDTH), _const((D_MODEL, CAT_WIDTH)), r, r, _const((1, D_MODEL)), ANY],
               [r, _const((8, D_MODEL))], [SDS((t, D_MODEL), F32), SDS((8, D_MODEL), F32)],
               sem=("arbitrary",))(dproj, w_cat, da1, x, g, after)


def _local_step(x, p, tgt, wts, start_token, first_weights, late_weights, send_late_grads, send_early_grads):
    t = x.shape[0]
    tm = min(512, t)
    tms = min(256, t)
    row = lambda a: a.reshape(1, -1)
    pool_scale = row(wts["pool_scale"])
    wo_row = jnp.tile(row(wts["o_norm_w"]), (1, HEADS))
    pad8 = jnp.zeros((1, HEADS), F32)
    al_row = jnp.concatenate([pad8, row(wts["a_log"]), jnp.zeros((1, 128 - 2 * HEADS), F32)], axis=1)
    dtb_row = jnp.concatenate([pad8, row(wts["dt_bias"]), jnp.zeros((1, 128 - 2 * HEADS), F32)], axis=1)
    g_in, b_in = row(wts["ln_in_g"]), row(wts["ln_in_b"])
    g1, b1 = row(wts["ln1_g"]), row(wts["ln1_b"])
    g2, b2 = row(wts["ln2_g"]), row(wts["ln2_b"])

    h0, h0_bf = _ln_in(x, g_in, b_in, tm, start_token)
    first, first_token = first_weights(h0_bf)
    wts = {**wts, **first}
    w_cat = wts["w_cat"]
    proj, qkv_act, dsilu = _proj_conv(h0_bf, w_cat, wts["conv_w"], tms, first_token)
    ypre, d_bf = _pool_fwd(proj, wts["pool_w"], tm)
    bg = _ba_fwd(proj, al_row, dtb_row, tm)
    bgt = bg[:, :2 * HEADS].T
    o, u, w, qg, kg, attn, ymat, vn, states, egl = _dn_fwd(qkv_act, bg, bgt)
    wts = {**wts, **late_weights(o)}
    mixed, a1, h1, h1_bf = _mix_oproj_ln1(o, proj, ypre, pool_scale, wo_row, wts["w_out"], h0, g1, b1, tms)
    act = _mlp_up(h1_bf, wts["w_up"], tm)
    dr, dr_bf, dgp, dpp, r_bf, acc_tail = _tail(act, wts["w_down"], h1, wts["ple_gate_w"], p, wts["ple_proj_w"],
                                                tgt, g2, b2, tms)
    grads = {}
    grads["ple_proj_w"] = _matmul(p, dpp, "tn", "dw_ple_proj", WIRE_DTYPE, tm=256, tn=1024, tk=DW_TK)
    grads["ple_gate_w"] = _matmul(r_bf, dgp, "tn", "dw_ple_gate", WIRE_DTYPE, tm=DW_TM, tn=1024, tk=DW_TK)
    grads["w_down"] = _matmul(act, dr_bf, "tn", "dw_down", WIRE_DTYPE, tm=DW_TM, tn=1024, tk=DW_TK)
    dup = _mlp_bwd1(dr_bf, wts["w_down"], act, tm, 1024)
    grads["w_up"] = _matmul(h1_bf, dup, "tn", "dw_up", WIRE_DTYPE, tm=DW_TM, tn=1024, tk=DW_TK, stack_out=True)
    da1, da1_bf, acc_ln1 = _mlp_bwd2(dup, wts["w_up"], dr, a1, g1, tms)
    grads["w_out"] = _matmul(mixed, da1_bf, "tn", "dw_out", WIRE_DTYPE, tm=DW_TM, tn=1024, tk=DW_TK)
    sent = send_late_grads(grads)
    do, dproj, dyp, acc_mix = _mix_bwd(da1_bf, wts["w_out"], o, proj, ypre, pool_scale, wo_row, tms, sent)
    dproj, grads["pool_w"] = _pool_bwd(dyp, d_bf, wts["pool_w"], dproj, tm)
    dqkv_act, dbg = _dn_bwd(do, qkv_act, bg, bgt, u, w, qg, kg, attn, ymat, vn, states, egl)
    dproj, acc_conv = _conv_bwd(dqkv_act, dsilu, proj, wts["conv_w"], dproj, tm)
    dproj, acc_ba = _ba_bwd(dbg, bg, proj, al_row, dtb_row, dproj, tm)
    dw_cat = _matmul(h0_bf, dproj, "tn", "dw_in", WIRE_DTYPE, tm=DW_TM, tn=1152, tk=DW_TK)
    grads["w_in"] = _w_in_by_chip(dw_cat)
    sent = send_early_grads(grads)
    grad_x, acc_in = _ln_in_bwd(dproj, w_cat, da1, x, g_in, tms, sent)

    grads["conv_w"] = acc_conv[0:CONV_K]
    grads["ln_in_g"], grads["ln_in_b"] = acc_in[0], acc_in[1]
    grads["ln1_g"], grads["ln1_b"] = acc_ln1[0], acc_ln1[1]
    grads["ln2_g"], grads["ln2_b"] = acc_tail[0], acc_tail[1]
    grads["pool_scale"] = acc_mix[0]
    grads["o_norm_w"] = acc_mix[1, 0:HEAD_DIM]
    grads["a_log"] = acc_ba[0, HEADS:2 * HEADS]
    grads["dt_bias"] = acc_ba[1, HEADS:2 * HEADS]
    loss = acc_tail[2, 0]
    return grad_x, grads, loss


MESH = pl.DeviceIdType.MESH
ANY = pl.BlockSpec(memory_space=pl.ANY)


def _chip_of(k, x, y):
    chip = (2 * x + y + k) % N_CHIPS
    return chip // 2, chip % 2


def _place():
    x, y, c = lax.axis_index("x"), lax.axis_index("y"), lax.axis_index("c")
    return x, y, c, 2 * x + y


def _half(rows, c):
    return pl.ds(pl.multiple_of(c * (rows // 2), 16), rows // 2)


def _remote(src, dst, send_sem, recv_sem, device_id):
    return pltpu.make_async_remote_copy(src_ref=src, dst_ref=dst, send_sem=send_sem, recv_sem=recv_sem,
                                        device_id=device_id, device_id_type=MESH)


def _tile_rows(rows):
    for tr in (256, 128, 64, 32, 16):
        if rows % tr == 0:
            return tr
    raise ValueError(rows)


def _first_gather_copies(srcs, lands, send, recv, place):
    copies = []
    for a in range(len(srcs)):
        whole = a == len(srcs) - 1
        for k in range(N_CHIPS):
            if place is None:
                copies.append(None)
                continue
            x, y, c, me = place
            sems = (send.at[a * N_CHIPS + k], recv.at[a * N_CHIPS + k])
            if k == 0:
                copies.append(_remote(srcs[a], lands[a].at[me], *sems, (x, y, 1 - c)))
                continue
            tx, ty = _chip_of(k, x, y)
            if whole:
                copies.append(_remote(srcs[a], lands[a].at[me], *sems, (tx, ty, c)))
            else:
                mine = _half(srcs[a].shape[0], c)
                copies.append(_remote(srcs[a].at[mine], lands[a].at[me, mine], *sems, (tx, ty, c)))
    return copies


def _pass_halves(stacks):
    n = len(stacks)

    def body(*refs):
        outs = refs[n:2 * n]
        send, recv = refs[2 * n:]
        x, y, c, me = _place()
        copies = []
        for a in range(n):
            for k in range(1, N_CHIPS):
                landed = outs[a].at[(me + N_CHIPS - k) % N_CHIPS, _half(stacks[a].shape[1], c)]
                copies.append(_remote(landed, landed, send.at[a * N_CHIPS + k], recv.at[a * N_CHIPS + k],
                                      (x, y, 1 - c)))
        for cp in copies:
            cp.start()
        for cp in copies:
            cp.wait_send()
        for a in range(n):
            for k in range(1, N_CHIPS):
                passed = outs[a].at[(me + N_CHIPS - k) % N_CHIPS, _half(stacks[a].shape[1], 1 - c)]
                _remote(passed, passed, send.at[a * N_CHIPS + k], recv.at[a * N_CHIPS + k], (x, y, c)).wait_recv()

    sems = pltpu.SemaphoreType.DMA((n * N_CHIPS,))
    return pl.pallas_call(
        body, name="pass_halves", out_shape=[SDS(s.shape, s.dtype) for s in stacks],
        in_specs=[ANY] * n, out_specs=[ANY] * n, scratch_shapes=[sems, sems],
        input_output_aliases={a: a for a in range(n)},
    )(*stacks)


def _swap_halves(gs):
    n = len(gs)

    def body(*refs):
        ins, theirs = refs[0:n], refs[n:2 * n]
        send, recv = refs[2 * n:]
        x, y, c, _ = _place()
        copies = [_remote(ins[a].at[:, _half(gs[a].shape[1], 1 - c)], theirs[a], send.at[a], recv.at[a],
                          (x, y, 1 - c)) for a in range(n)]
        for cp in copies:
            cp.start()
        for cp in copies:
            cp.wait()

    return pl.pallas_call(
        body, name="swap_halves", out_shape=[SDS((N_CHIPS, g.shape[1] // 2, g.shape[2]), g.dtype) for g in gs],
        in_specs=[ANY] * n, out_specs=[ANY] * n, scratch_shapes=[pltpu.SemaphoreType.DMA((n,))] * 2,
    )(*gs)


def _send_to_sibling(hs):
    n = len(hs)

    def body(*refs):
        ins, outs = refs[0:n], refs[n:2 * n]
        send, recv = refs[2 * n:]
        x, y, c, _ = _place()
        copies = [_remote(ins[a], outs[a], send.at[a], recv.at[a], (x, y, 1 - c)) for a in range(n)]
        for cp in copies:
            cp.start()
        for cp in copies:
            cp.wait()

    return pl.pallas_call(
        body, name="send_to_sibling", out_shape=[SDS(h.shape, h.dtype) for h in hs],
        in_specs=[ANY] * n, out_specs=[ANY] * n, scratch_shapes=[pltpu.SemaphoreType.DMA((n,))] * 2,
    )(*hs)


HBM = pl.BlockSpec(memory_space=pltpu.HBM)
SEM = pl.BlockSpec(memory_space=pltpu.SEMAPHORE)
EFFECT = pltpu.SideEffectType.DATAFLOW_SIDE_EFFECTING


def _in_hbm(a):
    return pltpu.with_memory_space_constraint(a, pltpu.HBM)


def _split_copy_start(name, srcs, lands, copies_of, after):
    n = len(srcs)
    n_copies = len(copies_of(srcs, lands, None, None, None))

    def body(*refs):
        src_refs, land_refs = refs[0:n], refs[n:2 * n]
        send, recv = refs[2 * n + 1], refs[2 * n + 2]
        token = refs[-1]
        for cp in copies_of(src_refs, land_refs, send, recv, _place()):
            cp.start()
        token[...] = jnp.zeros_like(token)

    sems = pltpu.SemaphoreType.DMA((n_copies,))
    out = pl.pallas_call(
        body, name=name,
        out_shape=[sems, sems] + [pltpu.HBM(a.shape, a.dtype) for a in list(srcs) + list(lands)] + [SDS((8, 128), F32)],
        in_specs=[HBM] * (2 * n) + [ANY],
        out_specs=[SEM, SEM] + [HBM] * (2 * n) + [pl.BlockSpec(memory_space=pltpu.VMEM)],
        input_output_aliases={i: 2 + i for i in range(2 * n)},
        compiler_params=pltpu.CompilerParams(has_side_effects=EFFECT),
    )(*[_in_hbm(a) for a in list(srcs) + list(lands)], after)
    return out[0], out[1], out[2:2 + n], out[2 + n:2 + 2 * n], out[-1]


def _split_copy_wait(name, send, recv, srcs, lands, after, copies_of):
    n = len(srcs)
    after = list(after) if isinstance(after, (list, tuple)) else [after]

    def body(*refs):
        src_refs, land_refs = refs[0:n], refs[n:2 * n]
        send_ref, recv_ref = refs[2 * n], refs[2 * n + 1]
        for cp in copies_of(src_refs, land_refs, send_ref, recv_ref, _place()):
            cp.wait_send()
            cp.wait_recv()

    out = pl.pallas_call(
        body, name=name, out_shape=[pltpu.HBM(a.shape, a.dtype) for a in list(srcs) + list(lands)],
        in_specs=[HBM] * (2 * n) + [SEM, SEM] + [ANY] * len(after), out_specs=[HBM] * (2 * n),
        input_output_aliases={i: i for i in range(2 * n)},
        compiler_params=pltpu.CompilerParams(has_side_effects=EFFECT),
    )(*srcs, *lands, send, recv, *after)
    return out[0:n], out[n:2 * n]


def _late_gather_copies(srcs, lands, send, recv, place):
    copies = []
    for a in range(len(srcs)):
        for k in range(N_CHIPS):
            if place is None:
                copies.append(None)
                continue
            x, y, c, me = place
            if k == 0:
                target = (x, y, 1 - c)
            else:
                tx, ty = _chip_of(k, x, y)
                target = (tx, ty, c)
            copies.append(_remote(srcs[a], lands[a].at[me], send.at[a * N_CHIPS + k], recv.at[a * N_CHIPS + k], target))
    return copies


def _late_scatter_copies(srcs, lands, send, recv, place):
    copies = []
    for a in range(len(srcs)):
        for k in range(1, N_CHIPS):
            if place is None:
                copies.append(None)
                continue
            x, y, c, _ = place
            tx, ty = _chip_of(k, x, y)
            copies.append(_remote(srcs[a].at[2 * tx + ty], lands[a].at[k - 1], send.at[a * (N_CHIPS - 1) + k - 1],
                                  recv.at[a * (N_CHIPS - 1) + k - 1], (tx, ty, c)))
    return copies


def _add_pair(g, theirs, name):
    _, rows, cols = g.shape
    half = rows // 2
    tr = _tile_rows(half)

    def body(g_ref, t_ref, o_ref):
        own = g_ref[lax.axis_index("c")]
        o_ref[...] = (own.astype(F32) + t_ref[...].astype(F32)).astype(o_ref.dtype)

    blk = pl.BlockSpec((None, tr, cols), lambda j, i: (j, i, 0))
    return _pc(body, "add_" + name, (N_CHIPS, half // tr),
               [pl.BlockSpec((None, 2, tr, cols), lambda j, i: (j, 0, i, 0)), blk], blk,
               SDS((N_CHIPS, half, cols), g.dtype), sem=("parallel", "parallel"))(
                   g.reshape(N_CHIPS, 2, half, cols), theirs)


def _sum_slabs(pair, landed, name):
    _, rows, cols = pair.shape
    tr = _tile_rows(rows)

    def body(p_ref, r_ref, o_ref):
        acc = p_ref[2 * lax.axis_index("x") + lax.axis_index("y")].astype(F32)
        for k in range(N_CHIPS - 1):
            acc = acc + r_ref[k].astype(F32)
        o_ref[...] = acc

    return _pc(body, "sum_" + name, (rows // tr,),
               [pl.BlockSpec((N_CHIPS, tr, cols), lambda i: (0, i, 0)),
                pl.BlockSpec((N_CHIPS - 1, tr, cols), lambda i: (0, i, 0))],
               _row(tr, cols), SDS((rows, cols), F32), sem=("parallel",))(pair, landed)


def _adamw_math(w, g, m, v):
    m = ADAM_B1 * m + (1.0 - ADAM_B1) * g
    v = ADAM_B2 * v + (1.0 - ADAM_B2) * (g * g)
    m_hat = m / (1.0 - ADAM_B1 ** ADAM_STEP)
    v_hat = v / (1.0 - ADAM_B2 ** ADAM_STEP)
    delta = -ADAM_LR * (m_hat / (jnp.sqrt(v_hat) + ADAM_EPS) + ADAM_WD * w)
    return delta, m, v


def _adamw_2d(w, g_own, g_sib, m, v, name, halves):
    lead = w.ndim == 3
    rows, cols = w.shape[-2:]
    tr = _tile_rows(rows // 2)
    nh = rows // 2 // tr if halves else rows // tr

    def body(w_ref, go_ref, gs_ref, m_ref, v_ref, g_out, d_out, m_out, v_out):
        if halves:
            mine = (pl.program_id(0) // nh) == lax.axis_index("c")
            g = jnp.where(mine, go_ref[...], gs_ref[...])
        else:
            g = go_ref[...] + gs_ref[...]
        delta, mn, vn = _adamw_math(w_ref[...], g, m_ref[...], v_ref[...])
        g_out[...] = g
        d_out[...] = delta
        m_out[...] = mn
        v_out[...] = vn

    r = _row(tr, cols)
    p = pl.BlockSpec((None, tr, cols), lambda i: (0, i, 0)) if lead else r
    h = pl.BlockSpec((tr, cols), lambda i: (i % nh, 0))
    return _pc(body, "adamw_" + name, (rows // tr,), [p, h, h, p, p], [r] * 4, [SDS((rows, cols), F32)] * 4,
               sem=("parallel",))(w, g_own, g_sib, m, v)


def _small_allreduce_adamw(mine, w, m, v, sizes):
    shape = mine.shape
    n = len(sizes)

    def body(mine_ref, w_ref, m_ref, v_ref, *rest):
        outs, (buf_ref, res_ref, send_sems, recv_sems) = rest[:-4], rest[-4:]
        x, y, c = lax.axis_index("x"), lax.axis_index("y"), lax.axis_index("c")
        me = 4 * x + 2 * y + c
        buf_ref[me] = mine_ref[...]
        copies = []
        for k in range(1, N_DEV):
            tgt = (me + k) % N_DEV
            copies.append(pltpu.make_async_remote_copy(
                src_ref=mine_ref, dst_ref=buf_ref.at[me], send_sem=send_sems.at[k], recv_sem=recv_sems.at[k],
                device_id=(tgt // 4, (tgt // 2) % 2, tgt % 2), device_id_type=MESH))
        for cp in copies:
            cp.start()
        for k in range(1, N_DEV):
            src = (me + N_DEV - k) % N_DEV
            pltpu.make_async_remote_copy(
                src_ref=mine_ref, dst_ref=buf_ref.at[src], send_sem=send_sems.at[k], recv_sem=recv_sems.at[k],
                device_id=(x, y, c), device_id_type=MESH).wait_recv()
        for cp in copies:
            cp.wait_send()
        g = buf_ref[0]
        for j in range(1, N_DEV):
            g = g + buf_ref[j]
        delta, mn, vn = _adamw_math(w_ref[...], g, m_ref[...], v_ref[...])
        for kind, val in enumerate((g, delta, mn, vn)):
            res_ref[kind] = val
            for i, size in enumerate(sizes):
                outs[kind * (n + 1) + i][...] = res_ref[kind, i:i + 1, 0:size]
            outs[kind * (n + 1) + n][...] = res_ref[kind, SMALL_CONV_AT:SMALL_CONV_AT + SMALL_CONV_ROWS, :]
        outs[-1][...] = res_ref[0, n:n + 1, 0:1]

    vm = pl.BlockSpec(memory_space=pltpu.VMEM)
    per_kind = [SDS((1, size), F32) for size in sizes] + [SDS((SMALL_CONV_ROWS, D_MODEL), F32)]
    out_shape = per_kind * 4 + [SDS((1, 1), F32)]
    out = pl.pallas_call(
        body, name="small_allreduce_adamw", out_shape=out_shape, in_specs=[vm] * 4, out_specs=[vm] * len(out_shape),
        scratch_shapes=[pltpu.VMEM((N_DEV,) + shape, F32), pltpu.VMEM((4,) + shape, F32),
                        pltpu.SemaphoreType.DMA((N_DEV,)), pltpu.SemaphoreType.DMA((N_DEV,))],
    )(mine, w, m, v)
    return [out[kind * (n + 1):(kind + 1) * (n + 1)] for kind in range(4)], out[-1]


def _as2d(a):
    return a.reshape(-1, a.shape[-1])


SEGMENTS = ((C_POOL, K_U, POOL_WIDTH), (C_QKV, K_QKV, QKV_WIDTH), (C_Z, K_Z, DN_WIDTH), (C_BETA, K_BA, 2 * HEADS),
            (C_GA, K_GA, D_MODEL), (C_GB, K_GB, D_MODEL))
SHARD_COLS = IN_WIDTH // N_CHIPS


def _w_cat(stack):
    pieces = []
    for c0, _, width in sorted(SEGMENTS, key=lambda seg: seg[1]):
        a = c0
        while a < c0 + width:
            chip = a // SHARD_COLS
            b = min(c0 + width, (chip + 1) * SHARD_COLS)
            pieces.append(stack[chip][:, a - chip * SHARD_COLS:b - chip * SHARD_COLS])
            a = b
    pieces.append(jnp.zeros((D_MODEL, CAT_WIDTH - K_BA - 2 * HEADS), stack.dtype))
    return jnp.concatenate(pieces, axis=1)


def _w_in_by_chip(dw_cat):
    slabs = []
    for chip in range(N_CHIPS):
        lo, hi = chip * SHARD_COLS, (chip + 1) * SHARD_COLS
        pieces = []
        for c0, k0, width in sorted(SEGMENTS):
            a, b = max(c0, lo), min(c0 + width, hi)
            if a < b:
                pieces.append(dw_cat[:, k0 + a - c0:k0 + b - c0])
        slabs.append(jnp.concatenate(pieces, axis=1))
    return jnp.stack(slabs)


WEIGHT_LAYOUT = {
    "w_in": lambda s: ("w_cat", _w_cat(s)),
    "pool_w": lambda s: ("pool_w", s.reshape(N_CHIPS, 4, POOL_GROUP, POOL_OUT_GROUP // N_CHIPS)
                         .transpose(1, 2, 0, 3).reshape(4, POOL_GROUP, POOL_OUT_GROUP)),
    "w_out": lambda s: ("w_out", s.reshape(D_MODEL, D_MODEL)),
    "w_up": lambda s: ("w_up", s),
    "w_down": lambda s: ("w_down", s.reshape(D_FF, D_MODEL)),
    "ple_gate_w": lambda s: ("ple_gate_w", s.reshape(D_MODEL, D_MODEL)),
    "ple_proj_w": lambda s: ("ple_proj_w", s.transpose(1, 0, 2).reshape(PLE_DIM, D_MODEL)),
}

GRAD_LAYOUT = {
    "w_in": lambda g: g,
    "pool_w": lambda g: g.reshape(4, POOL_GROUP, N_CHIPS, POOL_OUT_GROUP // N_CHIPS)
                         .transpose(2, 0, 1, 3).reshape(N_CHIPS, 4 * POOL_GROUP, POOL_OUT_GROUP // N_CHIPS),
    "w_out": lambda g: g.reshape(N_CHIPS, D_MODEL // N_CHIPS, D_MODEL),
    "w_up": lambda g: g,
    "w_down": lambda g: g.reshape(N_CHIPS, D_FF // N_CHIPS, D_MODEL),
    "ple_gate_w": lambda g: g.reshape(N_CHIPS, D_MODEL // N_CHIPS, D_MODEL),
    "ple_proj_w": lambda g: g.reshape(PLE_DIM, N_CHIPS, D_MODEL // N_CHIPS).transpose(1, 0, 2),
}


def _full_weights(names, stacks):
    return dict(WEIGHT_LAYOUT[n](s.astype(MXU_DTYPE)) for n, s in zip(names, stacks))


def _grads_by_chip(names, grads):
    return [GRAD_LAYOUT[n](grads[n]).astype(WIRE_DTYPE) for n in names]


def _pack_small(rows, conv, name):
    n = len(rows)

    def body(*refs):
        out = refs[n + 1]
        out[...] = jnp.zeros_like(out)
        for i in range(n):
            out[i:i + 1, :] = refs[i][...]
        out[SMALL_CONV_AT:SMALL_CONV_AT + SMALL_CONV_ROWS, :] = refs[n][...]

    vm = pl.BlockSpec(memory_space=pltpu.VMEM)
    return pl.pallas_call(body, name=name, out_shape=SDS((SMALL_CONV_AT + SMALL_CONV_ROWS, D_MODEL), F32),
                          in_specs=[vm] * (n + 1), out_specs=vm)(*rows, conv)


def _pad_row(a):
    a = a.reshape(1, -1).astype(F32)
    return jnp.pad(a, ((0, 0), (0, D_MODEL - a.shape[1])))


def kernel(x, p, ln_in_g, ln_in_b, w_in, pool_w, pool_scale, conv_w, a_log, dt_bias, o_norm_w, w_out, ln1_g, ln1_b, w_up, w_down, ple_gate_w, ple_proj_w, ln2_g, ln2_b, loss_target, m_ln_in_g, m_ln_in_b, m_w_in, m_pool_w, m_pool_scale, m_conv_w, m_a_log, m_dt_bias, m_o_norm_w, m_w_out, m_ln1_g, m_ln1_b, m_w_up, m_w_down, m_ple_gate_w, m_ple_proj_w, m_ln2_g, m_ln2_b, v_ln_in_g, v_ln_in_b, v_w_in, v_pool_w, v_pool_scale, v_conv_w, v_a_log, v_dt_bias, v_o_norm_w, v_w_out, v_ln1_g, v_ln1_b, v_w_up, v_w_down, v_ple_gate_w, v_ple_proj_w, v_ln2_g, v_ln2_b):
    given = dict(locals())
    chip = 2 * lax.axis_index("x") + lax.axis_index("y")

    shard = lambda n: _as2d(given[n]).astype(WIRE_DTYPE)

    wts = {"ln_in_g": ln_in_g, "ln_in_b": ln_in_b, "pool_scale": pool_scale[0], "a_log": a_log[0],
           "dt_bias": dt_bias[0], "o_norm_w": o_norm_w[0], "ln1_g": ln1_g[0], "ln1_b": ln1_b[0],
           "ln2_g": ln2_g[0], "ln2_b": ln2_b[0]}

    conv_pad = jnp.pad(conv_w[0], ((0, 8 - CONV_K), (0, 0)))
    first_srcs = [shard(n) for n in EARLY] + [conv_pad]
    first_lands = [lax.empty((N_CHIPS,) + s.shape, s.dtype) for s in first_srcs]
    fsend, frecv, fsrcs, flands, start_token = _split_copy_start(
        "first_gather_start", first_srcs, first_lands, _first_gather_copies, first_srcs[0])
    late = {}
    for n in ("w_in", "m_w_in", "v_w_in"):
        given[n], _ = lax.optimization_barrier((given[n], start_token))

    def first_weights(after):
        _, lands = _split_copy_wait("first_gather_wait", fsend, frecv, fsrcs, flands,
                                    [after, given["w_in"], given["m_w_in"], given["v_w_in"]], _first_gather_copies)
        stacks = _pass_halves(lands[0:len(EARLY)])
        first = _full_weights(EARLY, stacks)
        first["conv_w"] = jnp.concatenate([lands[len(EARLY)][j, 0:CONV_K] for j in range(N_CHIPS)], axis=1)
        late_srcs = [shard(n) for n in LATE]
        late_lands = [lax.empty((N_CHIPS,) + s.shape, s.dtype) for s in late_srcs]
        late["send"], late["recv"], late["srcs"], late["lands"], token = _split_copy_start(
            "late_gather_start", late_srcs, late_lands, _late_gather_copies, stacks[0])
        return first, token

    def late_weights(after):
        _, stacks = _split_copy_wait("late_gather_wait", late["send"], late["recv"], late["srcs"], late["lands"],
                                     after, _late_gather_copies)
        return _full_weights(LATE, stacks)

    scatter = {}

    def send_late_grads(grads):
        srcs = _grads_by_chip(LATE, grads)
        lands = [lax.empty((N_CHIPS - 1,) + g.shape[1:], g.dtype) for g in srcs]
        scatter["send"], scatter["recv"], scatter["srcs"], scatter["lands"], token = _split_copy_start(
            "late_scatter_start", srcs, lands, _late_scatter_copies, srcs[0])
        return token

    last = {}

    def send_early_grads(grads):
        by_chip = _grads_by_chip(EARLY, grads)
        theirs = _swap_halves(by_chip)
        pair = [_add_pair(g, t, n) for g, t, n in zip(by_chip, theirs, EARLY)]
        lands = [lax.empty((N_CHIPS - 1,) + q.shape[1:], q.dtype) for q in pair]
        last["send"], last["recv"], last["srcs"], last["lands"], token = _split_copy_start(
            "early_scatter_start", pair, lands, _late_scatter_copies, pair[0])
        return token

    grad_x, grads, loss = _local_step(x[0], p[0, 0], loss_target[0], wts, start_token, first_weights, late_weights,
                                      send_late_grads, send_early_grads)

    late_mine, late_landed = _split_copy_wait("late_scatter_wait", scatter["send"], scatter["recv"], scatter["srcs"],
                                              scatter["lands"], grad_x, _late_scatter_copies)
    late_part = [_sum_slabs(q, r, n) for q, r, n in zip(late_mine, late_landed, LATE)]
    pair, landed = _split_copy_wait("early_scatter_wait", last["send"], last["recv"], last["srcs"], last["lands"],
                                    grad_x, _late_scatter_copies)
    reduced = [_sum_slabs(q, r, n) for q, r, n in zip(pair, landed, EARLY)]
    from_sibling = _send_to_sibling(reduced + late_part)
    big_out = {}
    for n, g_own, g_sib in zip(EARLY + LATE, reduced + late_part, from_sibling):
        view = (lambda a: a) if given[n].ndim == 3 else _as2d
        res = _adamw_2d(view(given[n]), g_own, g_sib, view(given["m_" + n]), view(given["v_" + n]), n,
                        halves=n in EARLY)
        big_out[n] = [r.reshape(given[n].shape) for r in res]

    conv_cols = QKV_WIDTH // N_CHIPS

    def small_pack(get, conv, extra, name):
        if conv.shape[1] != QKV_WIDTH:
            conv = lax.dynamic_update_slice(jnp.zeros((CONV_K, QKV_WIDTH), F32), conv, (0, chip * conv_cols))
        return _pack_small([_pad_row(get(n)) for n in SMALL_NAMES] + extra, conv.reshape(SMALL_CONV_ROWS, D_MODEL), name)

    mine_small = small_pack(lambda n: grads[n], grads["conv_w"], [jnp.full((1, D_MODEL), loss, F32)], "pack_small_g")
    packed_small = [small_pack(lambda n: given[prefix + n], given[prefix + "conv_w"][0], [], "pack_small_" + tag)
                    for prefix, tag in (("", "w"), ("m_", "m"), ("v_", "v"))]
    small_out, loss_sum = _small_allreduce_adamw(mine_small, *packed_small, [given[n].size for n in SMALL_NAMES])

    def small_get(k, n):
        if n == "conv_w":
            full = small_out[k][len(SMALL_NAMES)].reshape(CONV_K, QKV_WIDTH)
            return lax.dynamic_slice(full, (0, chip * conv_cols), (CONV_K, conv_cols)).reshape(given[n].shape)
        return small_out[k][SMALL_NAMES.index(n)].reshape(given[n].shape)

    order = ["ln_in_g", "ln_in_b", "w_in", "pool_w", "pool_scale", "conv_w", "a_log", "dt_bias", "o_norm_w", "w_out",
             "ln1_g", "ln1_b", "w_up", "w_down", "ple_gate_w", "ple_proj_w", "ln2_g", "ln2_b"]
    outs = [loss_sum.reshape(()), grad_x[None]]
    for k in range(4):
        for n in order:
            outs.append(big_out[n][k] if n in big_out else small_get(k, n))
    return tuple(outs)
```

```python
import jax
import jax.numpy as jnp
from jax import lax
from jax.experimental import pallas as pl
from jax.experimental.pallas import tpu as pltpu

F32 = jnp.float32
MXU_DTYPE = jnp.bfloat16
WIRE_DTYPE = jnp.bfloat16
SDS = jax.ShapeDtypeStruct

D_MODEL = 1024
POOL_WINDOWS = (2, 4, 8, 16)
POOL_WIDTH = 512
POOL_GROUP = 128
POOL_OUT_GROUP = 256
HEADS = 8
HEAD_DIM = 128
DN_WIDTH = HEADS * HEAD_DIM
QKV_WIDTH = 3 * DN_WIDTH
CONV_K = 4
CHUNK = 128
DN_BWD_CHUNKS = 2
DN_FWD_CHUNKS = 2
DW_TK = 1024
DW_TM = 1024
D_FF = 4096
PLE_DIM = 256
LN_EPS = 1e-5
RMS_EPS = 1e-6
L2_EPS = 1e-6
ALPHA = 2.0 ** 0.25
Q_SCALE = HEAD_DIM ** -0.5
IN_WIDTH = 6672
C_POOL, C_QKV, C_Z, C_BETA, C_A, C_GA, C_GB = 0, 512, 3584, 4608, 4616, 4624, 5648
K_QKV, K_Z, K_GA, K_GB, K_U, K_BA, CAT_WIDTH = 0, 3072, 4096, 5120, 6144, 6656, 6912
P_U, P_BA, PROJ_F32_WIDTH = 3072, 3584, 3840

ADAM_LR, ADAM_B1, ADAM_B2, ADAM_EPS, ADAM_WD, ADAM_STEP = 0.001, 0.9, 0.999, 1e-08, 0.01, 10

N_CHIPS = 4
N_DEV = 8
VMEM_LIMIT = 56 * 1024 * 1024

EARLY = ("w_in", "pool_w")
LATE = ("w_out", "w_up", "w_down", "ple_gate_w", "ple_proj_w")
SMALL_NAMES = ("ln_in_g", "ln_in_b", "pool_scale", "ln1_g", "ln1_b", "ln2_g", "ln2_b", "o_norm_w", "a_log", "dt_bias")
SMALL_CONV_AT = 12
SMALL_CONV_ROWS = CONV_K * QKV_WIDTH // D_MODEL


def _mx(a):
    return a.astype(MXU_DTYPE)


def _dot(a, b):
    return lax.dot_general(_mx(a), _mx(b), (((1,), (0,)), ((), ())), preferred_element_type=F32)


def _dot_nt(a, b):
    return lax.dot_general(_mx(a), _mx(b), (((1,), (1,)), ((), ())), preferred_element_type=F32)


def _dot_tn(a, b):
    return lax.dot_general(_mx(a), _mx(b), (((0,), (0,)), ((), ())), preferred_element_type=F32)


def _sigmoid(x):
    return 0.5 * jnp.tanh(0.5 * x) + 0.5


def _softplus(x):
    return jnp.maximum(x, 0.0) + jnp.log(1.0 + jnp.exp(-jnp.abs(x)))


def _pc(body, name, grid, in_specs, out_specs, out_shape, scratch=(), sem=None, aliases=None):
    return pl.pallas_call(
        body, out_shape=out_shape, grid=grid, in_specs=in_specs, out_specs=out_specs,
        scratch_shapes=scratch, name=name, input_output_aliases=aliases or {},
        compiler_params=pltpu.CompilerParams(dimension_semantics=sem, vmem_limit_bytes=VMEM_LIMIT))


def _row(tm, n):
    return pl.BlockSpec((tm, n), lambda i: (i, 0))


def _const(shape):
    nd = len(shape)
    return pl.BlockSpec(shape, lambda *_: (0,) * nd)


def _matmul(a, b, mode, name, out_dtype=F32, tm=512, tn=512, tk=512, stack_out=False):
    if mode == "nn":
        (m, k), n = a.shape, b.shape[1]
    elif mode == "nt":
        (m, k), n = a.shape, b.shape[0]
    else:
        (k, m), n = a.shape, b.shape[1]
    tm, tn, tk = min(tm, m), min(tn, n), min(tk, k)
    assert m % tm == 0 and n % tn == 0 and k % tk == 0, (name, m, n, k, tm, tn, tk)
    nk = k // tk
    if mode == "nn":
        a_spec = pl.BlockSpec((tm, tk), lambda i, j, kk: (i, kk))
        b_spec = pl.BlockSpec((tk, tn), lambda i, j, kk: (kk, j))
        dot = _dot
    elif mode == "nt":
        a_spec = pl.BlockSpec((tm, tk), lambda i, j, kk: (i, kk))
        b_spec = pl.BlockSpec((tn, tk), lambda i, j, kk: (j, kk))
        dot = _dot_nt
    else:
        a_spec = pl.BlockSpec((tk, tm), lambda i, j, kk: (kk, i))
        b_spec = pl.BlockSpec((tk, tn), lambda i, j, kk: (kk, j))
        dot = _dot_tn

    def body(a_ref, b_ref, o_ref, *acc):
        if nk == 1:
            o_ref[...] = dot(a_ref[...], b_ref[...]).astype(out_dtype)
            return
        acc_ref, kk = acc[0], pl.program_id(2)

        @pl.when(kk == 0)
        def _():
            acc_ref[...] = dot(a_ref[...], b_ref[...])

        @pl.when((kk > 0) & (kk < nk - 1))
        def _():
            acc_ref[...] += dot(a_ref[...], b_ref[...])

        @pl.when(kk == nk - 1)
        def _():
            o_ref[...] = (acc_ref[...] + dot(a_ref[...], b_ref[...])).astype(out_dtype)

    if stack_out:
        o_spec, o_shape = pl.BlockSpec((None, tm, tn), lambda i, j, kk: (j, i, 0)), SDS((n // tn, m, tn), out_dtype)
    else:
        o_spec, o_shape = pl.BlockSpec((tm, tn), lambda i, j, kk: (i, j)), SDS((m, n), out_dtype)
    return _pc(body, name, (m // tm, n // tn, nk), [a_spec, b_spec], o_spec, o_shape,
               scratch=[pltpu.VMEM((tm, tn), F32)] if nk > 1 else [],
               sem=("parallel", "parallel", "arbitrary"))(a, b)


PROJ_TN = 768


def _proj_conv(h0_bf, w_cat, conv_w, tm, after):
    t = h0_bf.shape[0]
    n_qkv = QKV_WIDTH // PROJ_TN
    n_gate = 3 * D_MODEL // PROJ_TN

    def body(h_ref, w_ref, cw_ref, after_ref, o_ref, gate_ref, act_ref, ds_ref, carry_ref, ext_ref):
        @pl.when(pl.program_id(0) == 0)
        def _():
            carry_ref[...] = jnp.zeros_like(carry_ref)

        h = h_ref[...]

        def project(cb):
            cols = slice(cb * PROJ_TN, (cb + 1) * PROJ_TN)
            res = _dot(h, w_ref[:, cols])
            if cb < n_qkv:
                o_ref[:, cols] = res
            elif cb < n_qkv + n_gate:
                gate_ref[:, (cb - n_qkv) * PROJ_TN:(cb - n_qkv + 1) * PROJ_TN] = _mx(res)
            else:
                o_ref[:, P_U:PROJ_F32_WIDTH] = res

        def conv(cb, part):
            cols = slice(cb * PROJ_TN, (cb + 1) * PROJ_TN)
            if part == 0:
                ext_ref[cb, 0:8, :] = carry_ref[:, cols]
                ext_ref[cb, 8:8 + tm, :] = o_ref[:, cols]
                carry_ref[:, cols] = o_ref[tm - 8:tm, cols]
            w = [cw_ref[pl.ds(k, 1), cols] for k in range(CONV_K)]
            for r in range(part * (tm // 2), (part + 1) * (tm // 2), CONV_ROWS):
                y = _conv_rows(ext_ref.at[cb], w, r, CONV_ROWS)
                s = _sigmoid(y)
                act_ref[pl.ds(r, CONV_ROWS), cols] = y * s
                ds_ref[pl.ds(r, CONV_ROWS), cols] = _mx(s * (1.0 + y * (1.0 - s)))

        pending = [(cb, part) for cb in range(n_qkv) for part in range(2)]
        project(0)
        for cb in range(1, CAT_WIDTH // PROJ_TN):
            project(cb)
            if pending and pending[0][0] < cb:
                conv(*pending.pop(0))
        for cb, part in pending:
            conv(cb, part)

    return _pc(body, "proj_conv", (t // tm,),
               [_row(tm, D_MODEL), _const((D_MODEL, CAT_WIDTH)), _const((CONV_K, QKV_WIDTH)), ANY],
               [_row(tm, PROJ_F32_WIDTH), _row(tm, 3 * D_MODEL), _row(tm, QKV_WIDTH), _row(tm, QKV_WIDTH)],
               [SDS((t, PROJ_F32_WIDTH), F32), SDS((t, 3 * D_MODEL), MXU_DTYPE), SDS((t, QKV_WIDTH), F32),
                SDS((t, QKV_WIDTH), MXU_DTYPE)],
               scratch=[pltpu.VMEM((8, QKV_WIDTH), F32), pltpu.VMEM((n_qkv, 8 + tm, PROJ_TN), F32)],
               sem=("arbitrary",))(h0_bf, w_cat, conv_w, after)


def _ln_stats(x):
    mu = jnp.mean(x, axis=-1, keepdims=True)
    xc = x - mu
    var = jnp.mean(xc * xc, axis=-1, keepdims=True)
    rstd = lax.rsqrt(var + LN_EPS)
    return xc * rstd, rstd


def _ln_bwd(dy, xhat, rstd, g):
    dxh = dy * g
    m1 = jnp.mean(dxh, axis=-1, keepdims=True)
    m2 = jnp.mean(dxh * xhat, axis=-1, keepdims=True)
    return rstd * (dxh - m1 - xhat * m2)


def _ln_in(x, g, b, tm, after):
    t, d = x.shape

    def body(x_ref, g_ref, b_ref, after_ref, h_ref, hb_ref):
        xhat, _ = _ln_stats(x_ref[...])
        h = xhat * g_ref[...] + b_ref[...]
        h_ref[...] = h
        hb_ref[...] = _mx(h)

    return _pc(body, "ln_in", (t // tm,), [_row(tm, d), _const((1, d)), _const((1, d)), ANY],
               [_row(tm, d), _row(tm, d)], [SDS((t, d), F32), SDS((t, d), MXU_DTYPE)],
               sem=("parallel",))(x, g, b, after)


def _pool_fwd(proj, pool_w, tm):
    t = proj.shape[0]
    ublk = P_U // POOL_WIDTH

    def body(u_ref, halo_ref, pw_ref, ypre_ref, d_ref, ext_ref):
        i = pl.program_id(0)
        ext_ref[0:16, :] = jnp.where(i > 0, halo_ref[...], 0.0)
        ext_ref[16:16 + tm, :] = u_ref[...]
        tok = i * tm + lax.broadcasted_iota(jnp.int32, (tm, POOL_GROUP), 0)
        for gi, w in enumerate(POOL_WINDOWS):
            cs = pl.ds(gi * POOL_GROUP, POOL_GROUP)
            ug = ext_ref[pl.ds(16, tm), cs]
            s = ug
            for k in range(1, w):
                s = s + ext_ref[pl.ds(16 - k, tm), cs]
            cnt = jnp.minimum(tok + 1, w).astype(F32)
            db = _mx(s / cnt - ug)
            d_ref[:, gi * POOL_GROUP:(gi + 1) * POOL_GROUP] = db
            ypre_ref[:, gi * POOL_OUT_GROUP:(gi + 1) * POOL_OUT_GROUP] = _dot(db, pw_ref[gi])

    halo = pl.BlockSpec((16, POOL_WIDTH), lambda i: (jnp.maximum(i * (tm // 16) - 1, 0), ublk))
    return _pc(body, "pool_fwd", (t // tm,),
               [pl.BlockSpec((tm, POOL_WIDTH), lambda i: (i, ublk)), halo, _const((4, POOL_GROUP, POOL_OUT_GROUP))],
               [_row(tm, D_MODEL), _row(tm, POOL_WIDTH)],
               [SDS((t, D_MODEL), F32), SDS((t, POOL_WIDTH), MXU_DTYPE)],
               scratch=[pltpu.VMEM((16 + tm, POOL_WIDTH), F32)], sem=("parallel",))(proj, proj, pool_w)


def _pool_bwd(dyp, d_bf, pool_w, dproj, tm):
    t = dyp.shape[0]
    n = t // tm

    def body(dy_ref, dyn_ref, d_ref, pw_ref, dproj_ref, du_ref, dpw_ref, ext_ref):
        i = pl.program_id(0)

        @pl.when(i == 0)
        def _():
            dpw_ref[...] = jnp.zeros_like(dpw_ref)

        tok = i * tm + lax.broadcasted_iota(jnp.int32, (tm + 16, POOL_GROUP), 0)
        for gi, w in enumerate(POOL_WINDOWS):
            dy = dy_ref[:, gi * POOL_OUT_GROUP:(gi + 1) * POOL_OUT_GROUP]
            dyn = dyn_ref[:, gi * POOL_OUT_GROUP:(gi + 1) * POOL_OUT_GROUP]
            pw = pw_ref[gi]
            dd = _dot_nt(dy, pw)
            ddn = jnp.where(i < n - 1, _dot_nt(dyn, pw), 0.0)
            cnt = jnp.minimum(tok + 1, w).astype(F32)
            ext_ref[0:tm, :] = dd / cnt[0:tm]
            ext_ref[tm:tm + 16, :] = ddn / cnt[tm:tm + 16]
            s = ext_ref[pl.ds(0, tm), :]
            for k in range(1, w):
                s = s + ext_ref[pl.ds(k, tm), :]
            du_ref[:, gi * POOL_GROUP:(gi + 1) * POOL_GROUP] = _mx(s - dd)
            dpw_ref[gi] += _dot_tn(d_ref[:, gi * POOL_GROUP:(gi + 1) * POOL_GROUP], dy)

    nxt = pl.BlockSpec((16, D_MODEL), lambda i: (jnp.minimum((i + 1) * (tm // 16), t // 16 - 1), 0))
    return _pc(body, "pool_bwd", (n,),
               [_row(tm, D_MODEL), nxt, _row(tm, POOL_WIDTH), _const((4, POOL_GROUP, POOL_OUT_GROUP)), ANY],
               [pl.BlockSpec((tm, POOL_WIDTH), lambda i: (i, K_U // POOL_WIDTH)),
                _const((4, POOL_GROUP, POOL_OUT_GROUP))],
               [SDS(dproj.shape, dproj.dtype), SDS((4, POOL_GROUP, POOL_OUT_GROUP), F32)],
               scratch=[pltpu.VMEM((tm + 16, POOL_GROUP), F32)], sem=("arbitrary",),
               aliases={4: 0})(dyp, dyp, d_bf, pool_w, dproj)


CONV_BLK = 512


CONV_ROWS = 32


def _conv_rows(ext_ref, w, r, rows):
    y = w[0] * ext_ref[pl.ds(r + 5, rows), :]
    for k in range(1, CONV_K):
        y = y + w[k] * ext_ref[pl.ds(r + 5 + k, rows), :]
    return y


def _conv_bwd(dact, dsilu, proj, conv_w, dproj, tm):
    t = proj.shape[0]
    n = t // tm

    def body(da_ref, dan_ref, ds_ref, dsn_ref, x_ref, xp_ref, w_ref, dproj_ref, dx_ref, dw_ref, ext_ref, dy_ref):
        i = pl.program_id(1)

        @pl.when(i == 0)
        def _():
            dw_ref[...] = jnp.zeros_like(dw_ref)

        ext_ref[0:8, :] = jnp.where(i > 0, xp_ref[...], 0.0)
        ext_ref[8:8 + tm, :] = x_ref[...]
        w = [w_ref[pl.ds(k, 1), :] for k in range(CONV_K)]

        acc = [jnp.zeros((8, CONV_BLK), F32) for _ in range(CONV_K)]
        for r in range(0, tm, CONV_ROWS):
            dy = da_ref[pl.ds(r, CONV_ROWS), :] * ds_ref[pl.ds(r, CONV_ROWS), :].astype(F32)
            dy_ref[pl.ds(r, CONV_ROWS), :] = dy
            for k in range(CONV_K):
                prod = dy * ext_ref[pl.ds(r + 5 + k, CONV_ROWS), :]
                for q in range(0, CONV_ROWS, 8):
                    acc[k] = acc[k] + prod[q:q + 8]
        dy_ref[tm:tm + 8, :] = jnp.where(i < n - 1, dan_ref[...] * dsn_ref[0:8, :].astype(F32), 0.0)
        for k in range(CONV_K):
            dw_ref[pl.ds(k, 1), :] += jnp.sum(acc[k], axis=0, keepdims=True)
        for r in range(0, tm, CONV_ROWS):
            dx = w[0] * dy_ref[pl.ds(r + 3, CONV_ROWS), :]
            for k in range(1, CONV_K):
                dx = dx + w[k] * dy_ref[pl.ds(r + 3 - k, CONV_ROWS), :]
            dx_ref[pl.ds(r, CONV_ROWS), :] = _mx(dx)

    blk = pl.BlockSpec((tm, CONV_BLK), lambda j, i: (i, j))
    prev = pl.BlockSpec((8, CONV_BLK), lambda j, i: (jnp.maximum(i * (tm // 8) - 1, 0), j))
    nxt = pl.BlockSpec((8, CONV_BLK), lambda j, i: (jnp.minimum((i + 1) * (tm // 8), t // 8 - 1), j))
    nxt16 = pl.BlockSpec((16, CONV_BLK), lambda j, i: (jnp.minimum((i + 1) * (tm // 16), t // 16 - 1), j))
    wspec = pl.BlockSpec((CONV_K, CONV_BLK), lambda j, i: (0, j))
    return _pc(body, "conv_bwd", (QKV_WIDTH // CONV_BLK, n),
               [blk, nxt, blk, nxt16, blk, prev, wspec, ANY],
               [blk, pl.BlockSpec((8, CONV_BLK), lambda j, i: (0, j))],
               [SDS(dproj.shape, dproj.dtype), SDS((8, QKV_WIDTH), F32)],
               scratch=[pltpu.VMEM((8 + tm, CONV_BLK), F32), pltpu.VMEM((8 + tm, CONV_BLK), F32)],
               sem=("parallel", "arbitrary"), aliases={7: 0})(dact, dact, dsilu, dsilu, proj, proj, conv_w, dproj)


def _lane(shape):
    return lax.broadcasted_iota(jnp.int32, shape, 1)


def _ba_fwd(proj, al_row, dtb_row, tm):
    t = proj.shape[0]
    bablk = P_BA // 128

    def body(ba_ref, al_ref, dtb_ref, bg_ref):
        ba = ba_ref[...]
        lane = _lane(ba.shape)
        g = -jnp.exp(al_ref[...]) * _softplus(ba + dtb_ref[...])
        bg_ref[...] = jnp.where(lane < HEADS, _sigmoid(ba), jnp.where(lane < 2 * HEADS, g, 0.0))

    return _pc(body, "ba_fwd", (t // tm,),
               [pl.BlockSpec((tm, 128), lambda i: (i, bablk)), _const((1, 128)), _const((1, 128))],
               _row(tm, 128), SDS((t, 128), F32), sem=("parallel",))(proj, al_row, dtb_row)


def _ba_bwd(dbg, bg, proj, al_row, dtb_row, dproj, tm):
    t = proj.shape[0]
    bablk = P_BA // 128

    def body(dbg_ref, bg_ref, ba_ref, al_ref, dtb_ref, dproj_ref, dba_ref, acc_ref):
        i = pl.program_id(0)

        @pl.when(i == 0)
        def _():
            acc_ref[...] = jnp.zeros_like(acc_ref)

        dbg_v, bg_v, ba = dbg_ref[...], bg_ref[...], ba_ref[...]
        lane = _lane(ba.shape)
        is_g = (lane >= HEADS) & (lane < 2 * HEADS)
        dbeta_raw = dbg_v * bg_v * (1.0 - bg_v)
        da_raw = dbg_v * (-jnp.exp(al_ref[...])) * _sigmoid(ba + dtb_ref[...])
        dba_ref[:, 0:128] = _mx(jnp.where(lane < HEADS, dbeta_raw, jnp.where(is_g, da_raw, 0.0)))
        dba_ref[:, 128:CAT_WIDTH - K_BA] = jnp.zeros((tm, CAT_WIDTH - K_BA - 128), dba_ref.dtype)
        acc_ref[0:1, :] += jnp.sum(jnp.where(is_g, dbg_v * bg_v, 0.0), axis=0, keepdims=True)
        acc_ref[1:2, :] += jnp.sum(jnp.where(is_g, da_raw, 0.0), axis=0, keepdims=True)

    tail = CAT_WIDTH - K_BA
    return _pc(body, "ba_bwd", (t // tm,),
               [_row(tm, 128), _row(tm, 128), pl.BlockSpec((tm, 128), lambda i: (i, bablk)),
                _const((1, 128)), _const((1, 128)), ANY],
               [pl.BlockSpec((tm, tail), lambda i: (i, K_BA // tail)), _const((8, 128))],
               [SDS(dproj.shape, dproj.dtype), SDS((8, 128), F32)],
               sem=("arbitrary",), aliases={5: 0})(dbg, bg, proj, al_row, dtb_row, dproj)


def _each(f, *lists):
    return [f(*a) for a in zip(*lists)]


def _rowsum(a):
    return jnp.sum(a, axis=1, keepdims=True)


def _chunk_terms(qs, ks, bgvs, g_rows, hs):
    c = CHUNK
    ii = lax.broadcasted_iota(jnp.int32, (c, c), 0)
    jj = lax.broadcasted_iota(jnp.int32, (c, c), 1)
    lane = _lane((c, 128))
    incl = ii >= jj
    beta = [_rowsum(jnp.where(lane == h, bgv, 0.0)) for h, bgv in zip(hs, bgvs)]
    g_col = [_rowsum(jnp.where(lane == HEADS + h, bgv, 0.0)) for h, bgv in zip(hs, bgvs)]
    rq = _each(lambda q: lax.rsqrt(_rowsum(q * q) + L2_EPS), qs)
    rk = _each(lambda k: lax.rsqrt(_rowsum(k * k) + L2_EPS), ks)
    yq = _each(jnp.multiply, qs, rq)
    kn = _each(jnp.multiply, ks, rk)
    qn = _each(lambda a: a * Q_SCALE, yq)
    gc_col = _each(lambda g: _rowsum(jnp.where(jj <= ii, g, 0.0)), g_rows)
    gc_row = _each(lambda g: jnp.sum(jnp.where(ii <= jj, g, 0.0), axis=0, keepdims=True), g_col)
    dm = _each(lambda a, b: jnp.where(incl, jnp.exp(jnp.where(incl, a - b, 0.0)), 0.0), gc_col, gc_row)
    gl = _each(_rowsum, g_rows)
    eg = _each(jnp.exp, gc_col)
    ek = _each(lambda a, b: jnp.exp(a - b), gl, gc_col)
    egl = _each(jnp.exp, gl)
    kb = _each(jnp.multiply, kn, beta)
    kk = _each(_dot_nt, kb, kn)
    qk = _each(_dot_nt, qn, kn)
    m = _each(lambda a, b: jnp.where(ii > jj, a * b, 0.0), kk, dm)
    attn = _each(jnp.multiply, qk, dm)
    return dict(ii=ii, jj=jj, beta=beta, rq=rq, rk=rk, yq=yq, kn=kn, qn=qn, dm=dm, eg=eg, ek=ek,
                egl=egl, kb=kb, m=m, attn=attn)


def _unit_lower_inverse_minus_identity(ms, ii, jj):
    pair = (ii >> 1) == (jj >> 1)
    ys = _each(lambda m: -jnp.where(pair, m, 0.0), ms)
    s = 1
    while (1 << s) < CHUNK:
        mask = ((ii >> (s + 1)) == (jj >> (s + 1))) & ((ii >> s) != (jj >> s))
        lbs = _each(lambda m: jnp.where(mask, m, 0.0), ms)
        zs = _each(lambda y, lb: lb + _dot(y, lb), ys, lbs)
        ys = _each(lambda y, z: y - z - _dot(z, y), ys, zs)
        s += 1
    return ys


def _dn_fwd(qkv_act, bg, bgt):
    t = qkv_act.shape[0]
    c = CHUNK
    per = DN_FWD_CHUNKS if t % (DN_FWD_CHUNKS * c) == 0 else 1
    nt = t // c
    hs = list(range(HEADS))
    entries = [(s_, h) for s_ in range(per) for h in hs]
    qo = [slice(h * HEAD_DIM, (h + 1) * HEAD_DIM) for h in hs]
    ko = [slice(DN_WIDTH + h * HEAD_DIM, DN_WIDTH + (h + 1) * HEAD_DIM) for h in hs]
    vo = [slice(2 * DN_WIDTH + h * HEAD_DIM, 2 * DN_WIDTH + (h + 1) * HEAD_DIM) for h in hs]

    def body(qkv_ref, bg_ref, bgt_ref, o_ref, u_ref, w_ref, qg_ref, kg_ref, attn_ref, y_ref, vn_ref, st_ref, egl_ref,
             s_ref):
        @pl.when(pl.program_id(0) == 0)
        def _():
            s_ref[...] = jnp.zeros_like(s_ref)

        rows = [pl.ds(s_ * c, c) for s_ in range(per)]
        qs = [qkv_ref[rows[s_], qo[h]] for s_, h in entries]
        ks = [qkv_ref[rows[s_], ko[h]] for s_, h in entries]
        vs = [qkv_ref[rows[s_], vo[h]] for s_, h in entries]
        bgvs = [bg_ref[rows[s_], :] for s_, _ in entries]
        g_rows = [bgt_ref[pl.ds(HEADS + h, 1), rows[s_]] for s_, h in entries]
        ct = _chunk_terms(qs, ks, bgvs, g_rows, [h for _, h in entries])
        ys = _unit_lower_inverse_minus_identity(ct["m"], ct["ii"], ct["jj"])
        vb = _each(jnp.multiply, vs, ct["beta"])
        kbe = _each(jnp.multiply, ct["kb"], ct["eg"])
        us = _each(lambda a, y: a + _dot(y, a), vb, ys)
        ws = _each(lambda a, y: _mx(a + _dot(y, a)), kbe, ys)
        qg = _each(lambda a, b: _mx(a * b), ct["qn"], ct["eg"])
        kg = _each(lambda a, b: _mx(a * b), ct["kn"], ct["ek"])
        attn = _each(_mx, ct["attn"])
        for e, (s_, h) in enumerate(entries):
            u_ref[rows[s_], qo[h]] = us[e]
            w_ref[rows[s_], qo[h]] = ws[e]
            qg_ref[rows[s_], qo[h]] = qg[e]
            kg_ref[rows[s_], qo[h]] = kg[e]
            attn_ref[rows[s_], qo[h]] = attn[e]
            y_ref[rows[s_], qo[h]] = _mx(ys[e])
            egl_ref[s_, h:h + 1, :] = jnp.broadcast_to(ct["egl"][e], (1, HEAD_DIM))
        ss = [s_ref[h] for h in hs]
        for s_ in range(per):
            pick = lambda xs: xs[s_ * HEADS:(s_ + 1) * HEADS]
            sb = _each(_mx, ss)
            vn = _each(lambda a, b, st: a - _dot(b, st), pick(us), pick(ws), sb)
            vnb = _each(_mx, vn)
            oa = _each(_dot, pick(qg), sb)
            ob = _each(_dot, pick(attn), vnb)
            upd = _each(_dot_tn, pick(kg), vnb)
            for h in hs:
                st_ref[s_, h] = ss[h]
                vn_ref[rows[s_], qo[h]] = vnb[h]
                o_ref[rows[s_], qo[h]] = oa[h] + ob[h]
            ss = _each(lambda st, g, d: st * g + d, ss, pick(ct["egl"]), upd)
        for h in hs:
            s_ref[h] = ss[h]

    wide = _row(per * c, DN_WIDTH)
    return _pc(body, "dn_fwd", (nt // per,),
               [_row(per * c, QKV_WIDTH), _row(per * c, 128), pl.BlockSpec((2 * HEADS, per * c), lambda i: (0, i))],
               [wide] * 8 + [pl.BlockSpec((per, HEADS, HEAD_DIM, HEAD_DIM), lambda i: (i, 0, 0, 0)),
                             pl.BlockSpec((per, HEADS, HEAD_DIM), lambda i: (i, 0, 0))],
               [SDS((t, DN_WIDTH), F32), SDS((t, DN_WIDTH), F32)] + [SDS((t, DN_WIDTH), MXU_DTYPE)] * 6
               + [SDS((nt, HEADS, HEAD_DIM, HEAD_DIM), F32), SDS((nt, HEADS, HEAD_DIM), F32)],
               scratch=[pltpu.VMEM((HEADS, HEAD_DIM, HEAD_DIM), F32)], sem=("arbitrary",))(qkv_act, bg, bgt)


def _dn_bwd(do, qkv_act, bg, bgt, u, w, qg, kg, attn, ymat, vn, states, egl):
    t = do.shape[0]
    c = CHUNK
    per = DN_BWD_CHUNKS if t % (DN_BWD_CHUNKS * c) == 0 else 1
    nt = t // c
    hs = list(range(HEADS))
    entries = [(s_, h) for s_ in range(per) for h in hs]
    qo = [slice(h * HEAD_DIM, (h + 1) * HEAD_DIM) for h in hs]
    ko = [slice(DN_WIDTH + h * HEAD_DIM, DN_WIDTH + (h + 1) * HEAD_DIM) for h in hs]
    vo = [slice(2 * DN_WIDTH + h * HEAD_DIM, 2 * DN_WIDTH + (h + 1) * HEAD_DIM) for h in hs]

    def body(do_ref, qkv_ref, bg_ref, bgt_ref, u_ref, w_ref, qg_ref, kg_ref, attn_ref, y_ref, vn_ref, st_ref, egl_ref,
             dqkv_ref, dbg_ref, ds_ref):
        @pl.when(pl.program_id(0) == 0)
        def _():
            ds_ref[...] = jnp.zeros_like(ds_ref)

        rows = [pl.ds(s_ * c, c) for s_ in range(per)]
        scan = {}
        dsp = [ds_ref[h] for h in hs]
        for s_ in reversed(range(per)):
            r = rows[s_]
            dsb = _each(_mx, dsp)
            ss = [st_ref[s_, h] for h in hs]
            sb = _each(_mx, ss)
            du_s = [_dot(kg_ref[r, sl], b) + _dot_tn(attn_ref[r, sl], do_ref[r, sl]) for sl, b in zip(qo, dsb)]
            dub = _each(_mx, du_s)
            scan[s_] = dict(
                du=du_s,
                dkg=[_dot_nt(vn_ref[r, sl], b) for sl, b in zip(qo, dsb)],
                dqg=[_dot_nt(do_ref[r, sl], b) for sl, b in zip(qo, sb)],
                dattn=[_dot_nt(do_ref[r, sl], vn_ref[r, sl]) for sl in qo],
                dw=[-_dot_nt(a, b) for a, b in zip(dub, sb)],
                degl=[jnp.sum(_rowsum(a * b), axis=0, keepdims=True) for a, b in zip(ss, dsp)])
            upd = [_dot_tn(qg_ref[r, sl], do_ref[r, sl]) - _dot_tn(w_ref[r, sl], a) for sl, a in zip(qo, dub)]
            dsp = [dsp[h] * egl_ref[s_, h:h + 1, :] + upd[h] for h in hs]
        for h in hs:
            ds_ref[h] = dsp[h]
        gather = lambda key: [scan[s_][key][h] for s_, h in entries]
        du, dkg_v, dqg_v, dattn_v, dwv, degl_v = (gather(k) for k in ("du", "dkg", "dqg", "dattn", "dw", "degl"))

        lane = _lane((c, 128))
        rowi = lax.broadcasted_iota(jnp.int32, (c, 1), 0)
        qs = [qkv_ref[rows[s_], qo[h]] for s_, h in entries]
        ks = [qkv_ref[rows[s_], ko[h]] for s_, h in entries]
        vs = [qkv_ref[rows[s_], vo[h]] for s_, h in entries]
        bgvs = [bg_ref[rows[s_], :] for s_, _ in entries]
        g_rows = [bgt_ref[pl.ds(HEADS + h, 1), rows[s_]] for s_, h in entries]
        ct = _chunk_terms(qs, ks, bgvs, g_rows, [h for _, h in entries])
        ii, jj = ct["ii"], ct["jj"]
        beta, eg, ek, kb, kn, qn, dm = ct["beta"], ct["eg"], ct["ek"], ct["kb"], ct["kn"], ct["qn"], ct["dm"]
        ys = [y_ref[rows[s_], qo[h]] for s_, h in entries]
        dvb = _each(lambda a, y: a + _dot_tn(y, a), du, ys)
        dkbe = _each(lambda a, y: a + _dot_tn(y, a), dwv, ys)
        dm_u = [_dot_nt(a, u_ref[rows[s_], qo[h]]) for a, (s_, h) in zip(dvb, entries)]
        dm_w = [_dot_nt(a, w_ref[rows[s_], qo[h]]) for a, (s_, h) in zip(dkbe, entries)]
        dms = _each(lambda a, b: jnp.where(ii > jj, -(a + b), 0.0), dm_u, dm_w)
        dkk = _each(jnp.multiply, dms, dm)
        dqk = _each(jnp.multiply, dattn_v, dm)
        gmat = _each(lambda a, b, c_, d: a * b + c_ * d, dms, ct["m"], dattn_v, ct["attn"])
        dkb = _each(lambda a, b, c_, d: _dot(a, b) + c_ * d, dkk, kn, dkbe, eg)
        dk1 = _each(_dot_tn, dkk, kb)
        dk2 = _each(_dot_tn, dqk, qn)
        dq1 = _each(_dot, dqk, kn)
        dk = _each(lambda a, b, c_, d: a + b + c_ * d, dk1, dk2, dkg_v, ek)
        dq = _each(lambda a, b, c_: a + b * c_, dq1, dqg_v, eg)
        deg = _each(lambda a, b, c_, d: _rowsum(a * b) + _rowsum(c_ * d), dqg_v, qn, dkbe, kb)
        dek = _each(lambda a, b: _rowsum(a * b), dkg_v, kn)
        dgl = _each(lambda a, b, c_, d: jnp.sum(a * b, axis=0, keepdims=True) + c_ * d, dek, ek, degl_v, ct["egl"])
        cs_row = _each(lambda g: jnp.sum(g, axis=0, keepdims=True), gmat)
        cs_col = _each(lambda r: _rowsum(jnp.where(ii == jj, r, 0.0)), cs_row)
        dgc = _each(lambda a, b, c_, d, g, e, f: a * b - c_ * d + _rowsum(g) - e + jnp.where(rowi == c - 1, f, 0.0),
                    deg, eg, dek, ek, gmat, cs_col, dgl)
        dgc_row = _each(lambda a: jnp.sum(jnp.where(ii == jj, a, 0.0), axis=0, keepdims=True), dgc)
        dg = _each(lambda r: _rowsum(jnp.where(jj >= ii, r, 0.0)), dgc_row)
        dbeta = _each(lambda a, b, c_, d: _rowsum(a * b) + _rowsum(c_ * d), dkb, kn, dvb, vs)
        dk = _each(lambda a, b, c_: a + b * c_, dk, dkb, beta)
        dbg = [jnp.zeros((c, 128), F32) for _ in range(per)]
        for e, (s_, h) in enumerate(entries):
            dyq = dq[e] * Q_SCALE
            yq = ct["yq"][e]
            dqkv_ref[rows[s_], qo[h]] = ct["rq"][e] * (dyq - yq * _rowsum(yq * dyq))
            dqkv_ref[rows[s_], ko[h]] = ct["rk"][e] * (dk[e] - kn[e] * _rowsum(kn[e] * dk[e]))
            dqkv_ref[rows[s_], vo[h]] = dvb[e] * beta[e]
            dbg[s_] = dbg[s_] + jnp.where(lane == h, dbeta[e], 0.0) + jnp.where(lane == HEADS + h, dg[e], 0.0)
        for s_ in range(per):
            dbg_ref[rows[s_], :] = dbg[s_]

    ns = nt // per
    rev = pl.BlockSpec((per * c, DN_WIDTH), lambda i: (ns - 1 - i, 0))
    return _pc(body, "dn_bwd", (ns,),
               [rev, pl.BlockSpec((per * c, QKV_WIDTH), lambda i: (ns - 1 - i, 0)),
                pl.BlockSpec((per * c, 128), lambda i: (ns - 1 - i, 0)),
                pl.BlockSpec((2 * HEADS, per * c), lambda i: (0, ns - 1 - i))]
               + [rev] * 7
               + [pl.BlockSpec((per, HEADS, HEAD_DIM, HEAD_DIM), lambda i: (ns - 1 - i, 0, 0, 0)),
                  pl.BlockSpec((per, HEADS, HEAD_DIM), lambda i: (ns - 1 - i, 0, 0))],
               [pl.BlockSpec((per * c, QKV_WIDTH), lambda i: (ns - 1 - i, 0)),
                pl.BlockSpec((per * c, 128), lambda i: (ns - 1 - i, 0))],
               [SDS((t, QKV_WIDTH), F32), SDS((t, 128), F32)],
               scratch=[pltpu.VMEM((HEADS, HEAD_DIM, HEAD_DIM), F32)],
               sem=("arbitrary",))(do, qkv_act, bg, bgt, u, w, qg, kg, attn, ymat, vn, states, egl)


MIX_ROWS = 64


def _mix_oproj_ln1(o, gates, ypre, pool_scale, wo_row, w_out, h0, g1, b1, tm):
    t = o.shape[0]

    def body(o_ref, z_ref, ga_ref, gb_ref, yp_ref, ps_ref, wo_ref, w_ref, h0_ref, g_ref, b_ref,
             mixed_ref, a1_ref, h1_ref, h1b_ref):
        for r in range(0, tm, MIX_ROWS):
            rows = pl.ds(r, MIX_ROWS)
            for h in range(HEADS):
                sl = slice(h * HEAD_DIM, (h + 1) * HEAD_DIM)
                oh = o_ref[rows, sl]
                on = oh * lax.rsqrt(jnp.mean(oh * oh, axis=1, keepdims=True) + RMS_EPS)
                zh = z_ref[rows, sl].astype(F32)
                yb = on * wo_ref[:, sl] * (zh * _sigmoid(zh))
                ya = yp_ref[rows, sl] * ps_ref[:, sl]
                mixed_ref[rows, sl] = _mx(_sigmoid(ga_ref[rows, sl].astype(F32)) * ya
                                          + _sigmoid(gb_ref[rows, sl].astype(F32)) * yb)
        a1 = ALPHA * h0_ref[...] + _dot(mixed_ref[...], w_ref[...])
        a1_ref[...] = a1
        xhat, _ = _ln_stats(a1)
        h1 = xhat * g_ref[...] + b_ref[...]
        h1_ref[...] = h1
        h1b_ref[...] = _mx(h1)

    def col(blk):
        return pl.BlockSpec((tm, D_MODEL), lambda i: (i, blk))

    r = _row(tm, D_MODEL)
    v = _const((1, D_MODEL))
    return _pc(body, "mix_oproj_ln1", (t // tm,),
               [r, col(0), col(1), col(2), r, v, v,
                _const((D_MODEL, D_MODEL)), r, v, v],
               [r, r, r, r],
               [SDS((t, D_MODEL), MXU_DTYPE), SDS((t, D_MODEL), F32), SDS((t, D_MODEL), F32),
                SDS((t, D_MODEL), MXU_DTYPE)],
               sem=("parallel",))(o, gates, gates, gates, ypre, pool_scale, wo_row, w_out, h0, g1, b1)


def _mix_bwd(da1_bf, w_out, o, gates, ypre, pool_scale, wo_row, tm, after):
    t = o.shape[0]

    def body(da_ref, wout_ref, o_ref, z_ref, ga_ref, gb_ref, yp_ref, ps_ref, wo_ref, after_ref,
             do_ref, dp_ref, dyp_ref, acc_ref, dm_ref):
        i = pl.program_id(0)

        @pl.when(i == 0)
        def _():
            acc_ref[...] = jnp.zeros_like(acc_ref)

        dm_ref[...] = _dot_nt(da_ref[...], wout_ref[...])
        dwo = jnp.zeros((1, HEAD_DIM), F32)
        for h in range(HEADS):
            sl = slice(h * HEAD_DIM, (h + 1) * HEAD_DIM)
            woh = wo_ref[:, sl]
            psh = ps_ref[:, sl]
            dps = jnp.zeros((1, HEAD_DIM), F32)
            for r in range(0, tm, MIX_ROWS):
                rows = pl.ds(r, MIX_ROWS)
                oh = o_ref[rows, sl]
                rs = lax.rsqrt(jnp.mean(oh * oh, axis=1, keepdims=True) + RMS_EPS)
                on = oh * rs
                zh = z_ref[rows, sl].astype(F32)
                sz = _sigmoid(zh)
                silu = zh * sz
                t1 = on * woh
                yb = t1 * silu
                sa = _sigmoid(ga_ref[rows, sl].astype(F32))
                sb = _sigmoid(gb_ref[rows, sl].astype(F32))
                yp = yp_ref[rows, sl]
                dm = dm_ref[rows, sl]
                ga_sl = slice(D_MODEL + h * HEAD_DIM, D_MODEL + (h + 1) * HEAD_DIM)
                gb_sl = slice(2 * D_MODEL + h * HEAD_DIM, 2 * D_MODEL + (h + 1) * HEAD_DIM)
                dp_ref[rows, ga_sl] = _mx(dm * (yp * psh) * sa * (1.0 - sa))
                dp_ref[rows, gb_sl] = _mx(dm * yb * sb * (1.0 - sb))
                dya = dm * sa
                dyb = dm * sb
                dyp_ref[rows, sl] = _mx(dya * psh)
                dps = dps + jnp.sum(dya * yp, axis=0, keepdims=True)
                dp_ref[rows, sl] = _mx(dyb * t1 * (sz * (1.0 + zh * (1.0 - sz))))
                dt1 = dyb * silu
                dwo = dwo + jnp.sum(dt1 * on, axis=0, keepdims=True)
                don = dt1 * woh
                do_ref[rows, sl] = _mx(rs * (don - on * jnp.mean(don * on, axis=1, keepdims=True)))
            acc_ref[0:1, sl] += dps
        acc_ref[1:2, 0:HEAD_DIM] += dwo

    def col(blk):
        return pl.BlockSpec((tm, D_MODEL), lambda i: (i, blk))

    r = _row(tm, D_MODEL)
    return _pc(body, "mix_bwd", (t // tm,),
               [r, _const((D_MODEL, D_MODEL)), r, col(0), col(1), col(2), r,
                _const((1, D_MODEL)), _const((1, D_MODEL)), ANY],
               [r, pl.BlockSpec((tm, 3 * D_MODEL), lambda i: (i, K_Z // (3 * D_MODEL))), r, _const((8, D_MODEL))],
               [SDS((t, D_MODEL), MXU_DTYPE), SDS((t, CAT_WIDTH), MXU_DTYPE), SDS((t, D_MODEL), MXU_DTYPE),
                SDS((8, D_MODEL), F32)],
               scratch=[pltpu.VMEM((tm, D_MODEL), F32)],
               sem=("arbitrary",))(da1_bf, w_out, o, gates, gates, gates, ypre, pool_scale, wo_row, after)


def _mlp_up(h1_bf, w_up, tm):
    t = h1_bf.shape[0]
    tn = w_up.shape[2]

    def body(h_ref, w_ref, act_ref):
        r = jnp.maximum(_dot(h_ref[...], w_ref[...]), 0.0)
        act_ref[...] = _mx(r * r)

    return _pc(body, "mlp_up", (D_FF // tn, t // tm),
               [pl.BlockSpec((tm, D_MODEL), lambda j, i: (i, 0)),
                pl.BlockSpec((None, D_MODEL, tn), lambda j, i: (j, 0, 0))],
               pl.BlockSpec((tm, tn), lambda j, i: (i, j)), SDS((t, D_FF), MXU_DTYPE),
               sem=("parallel", "parallel"))(h1_bf, w_up)


def _tail(act, w_down, h1, w_gate, p, w_proj, tgt, g2, b2, tm):
    t = act.shape[0]

    def body(act_ref, wd_ref, h1_ref, wg_ref, p_ref, wp_ref, tgt_ref, g_ref, b_ref,
             dr_ref, drb_ref, dgp_ref, dpp_ref, rb_ref, acc_ref):
        i = pl.program_id(0)

        @pl.when(i == 0)
        def _():
            acc_ref[...] = jnp.zeros_like(acc_ref)

        r = ALPHA * h1_ref[...] + _dot(act_ref[...], wd_ref[...])
        rb = _mx(r)
        rb_ref[...] = rb
        gate = _sigmoid(_dot(rb, wg_ref[...]))
        pp = _dot(p_ref[...], wp_ref[...])
        xhat, rstd = _ln_stats(r + gate * pp)
        g = g_ref[...]
        diff = xhat * g + b_ref[...] - tgt_ref[...]
        dh2 = diff * (1.0 / D_MODEL)
        rowloss = jnp.sum(diff * diff, axis=1, keepdims=True) * (0.5 / D_MODEL)
        acc_ref[0:1, :] += jnp.sum(dh2 * xhat, axis=0, keepdims=True)
        acc_ref[1:2, :] += jnp.sum(dh2, axis=0, keepdims=True)
        acc_ref[2:3, :] += jnp.broadcast_to(jnp.sum(rowloss, axis=0, keepdims=True), (1, D_MODEL))
        da2 = _ln_bwd(dh2, xhat, rstd, g)
        dpp_ref[...] = _mx(da2 * gate)
        dgp = _mx(da2 * pp * gate * (1.0 - gate))
        dgp_ref[...] = dgp
        dr = da2 + _dot_nt(dgp, wg_ref[...])
        dr_ref[...] = dr
        drb_ref[...] = _mx(dr)

    r = _row(tm, D_MODEL)
    v = _const((1, D_MODEL))
    return _pc(body, "tail", (t // tm,),
               [_row(tm, D_FF), _const((D_FF, D_MODEL)), r, _const((D_MODEL, D_MODEL)), _row(tm, PLE_DIM),
                _const((PLE_DIM, D_MODEL)), r, v, v],
               [r, r, r, r, r, _const((8, D_MODEL))],
               [SDS((t, D_MODEL), F32)] + [SDS((t, D_MODEL), MXU_DTYPE)] * 4 + [SDS((8, D_MODEL), F32)],
               sem=("arbitrary",))(act, w_down, h1, w_gate, p, w_proj, tgt, g2, b2)


SQRT_GUARD = 1e-30


def _mlp_bwd1(dr_bf, w_down, act, tm, tn):
    t = act.shape[0]

    def body(dr_ref, w_ref, act_ref, dup_ref):
        dact = _dot_nt(dr_ref[...], w_ref[...])
        a = act_ref[...].astype(F32)
        dup_ref[...] = _mx(dact * (2.0 * a * lax.rsqrt(a + SQRT_GUARD)))

    o = pl.BlockSpec((tm, tn), lambda j, i: (i, j))
    return _pc(body, "mlp_bwd1", (D_FF // tn, t // tm),
               [pl.BlockSpec((tm, D_MODEL), lambda j, i: (i, 0)), pl.BlockSpec((tn, D_MODEL), lambda j, i: (j, 0)), o],
               o, SDS((t, D_FF), MXU_DTYPE), sem=("parallel", "parallel"))(dr_bf, w_down, act)


def _mlp_bwd2(dup, w_up, dr, a1, g1, tm):
    t = dr.shape[0]

    nk, tk = w_up.shape[0], w_up.shape[2]

    def body(dup_ref, w_ref, dr_ref, a1_ref, g_ref, da1_ref, da1b_ref, acc_ref):
        i = pl.program_id(0)

        @pl.when(i == 0)
        def _():
            acc_ref[...] = jnp.zeros_like(acc_ref)

        dh1 = ALPHA * dr_ref[...]
        for kk in range(nk):
            dh1 = dh1 + _dot_nt(dup_ref[:, kk * tk:(kk + 1) * tk], w_ref[kk])
        xhat, rstd = _ln_stats(a1_ref[...])
        acc_ref[0:1, :] += jnp.sum(dh1 * xhat, axis=0, keepdims=True)
        acc_ref[1:2, :] += jnp.sum(dh1, axis=0, keepdims=True)
        da1 = _ln_bwd(dh1, xhat, rstd, g_ref[...])
        da1_ref[...] = da1
        da1b_ref[...] = _mx(da1)

    r = _row(tm, D_MODEL)
    return _pc(body, "mlp_bwd2", (t // tm,),
               [_row(tm, D_FF), _const((nk, D_MODEL, tk)), r, r, _const((1, D_MODEL))],
               [r, r, _const((8, D_MODEL))],
               [SDS((t, D_MODEL), F32), SDS((t, D_MODEL), MXU_DTYPE), SDS((8, D_MODEL), F32)],
               sem=("arbitrary",))(dup, w_up, dr, a1, g1)


def _ln_in_bwd(dproj, w_cat, da1, x, g, tm, after):
    t = x.shape[0]

    def body(dp_ref, w_ref, da1_ref, x_ref, g_ref, after_ref, dx_ref, acc_ref):
        i = pl.program_id(0)

        @pl.when(i == 0)
        def _():
            acc_ref[...] = jnp.zeros_like(acc_ref)

        dh0 = _dot_nt(dp_ref[...], w_ref[...]) + ALPHA * da1_ref[...]
        xhat, rstd = _ln_stats(x_ref[...])
        acc_ref[0:1, :] += jnp.sum(dh0 * xhat, axis=0, keepdims=True)
        acc_ref[1:2, :] += jnp.sum(dh0, axis=0, keepdims=True)
        dx_ref[...] = _ln_bwd(dh0, xhat, rstd, g_ref[...])

    r = _row(tm, D_MODEL)
    return _pc(body, "ln_in_bwd", (t // tm,),
               [_row(tm, CAT_WIDTH), _const((D_MODEL, CAT_WIDTH)), r, r, _const((1, D_MODEL)), ANY],
               [r, _const((8, D_MODEL))], [SDS((t, D_MODEL), F32), SDS((8, D_MODEL), F32)],
               sem=("arbitrary",))(dproj, w_cat, da1, x, g, after)


def _local_step(x, p, tgt, wts, start_token, first_weights, late_weights, send_late_grads, send_early_grads):
    t = x.shape[0]
    tm = min(512, t)
    tms = min(256, t)
    row = lambda a: a.reshape(1, -1)
    pool_scale = row(wts["pool_scale"])
    wo_row = jnp.tile(row(wts["o_norm_w"]), (1, HEADS))
    pad8 = jnp.zeros((1, HEADS), F32)
    al_row = jnp.concatenate([pad8, row(wts["a_log"]), jnp.zeros((1, 128 - 2 * HEADS), F32)], axis=1)
    dtb_row = jnp.concatenate([pad8, row(wts["dt_bias"]), jnp.zeros((1, 128 - 2 * HEADS), F32)], axis=1)
    g_in, b_in = row(wts["ln_in_g"]), row(wts["ln_in_b"])
    g1, b1 = row(wts["ln1_g"]), row(wts["ln1_b"])
    g2, b2 = row(wts["ln2_g"]), row(wts["ln2_b"])

    h0, h0_bf = _ln_in(x, g_in, b_in, tm, start_token)
    first, first_token = first_weights(h0_bf)
    wts = {**wts, **first}
    w_cat = wts["w_cat"]
    proj, gates, qkv_act, dsilu = _proj_conv(h0_bf, w_cat, wts["conv_w"], tms, first_token)
    ypre, d_bf = _pool_fwd(proj, wts["pool_w"], tm)
    bg = _ba_fwd(proj, al_row, dtb_row, tm)
    bgt = bg[:, :2 * HEADS].T
    o, u, w, qg, kg, attn, ymat, vn, states, egl = _dn_fwd(qkv_act, bg, bgt)
    wts = {**wts, **late_weights(o)}
    mixed, a1, h1, h1_bf = _mix_oproj_ln1(o, gates, ypre, pool_scale, wo_row, wts["w_out"], h0, g1, b1, tms)
    act = _mlp_up(h1_bf, wts["w_up"], tm)
    dr, dr_bf, dgp, dpp, r_bf, acc_tail = _tail(act, wts["w_down"], h1, wts["ple_gate_w"], p, wts["ple_proj_w"],
                                                tgt, g2, b2, tms)
    grads = {}
    grads["ple_proj_w"] = _matmul(p, dpp, "tn", "dw_ple_proj", WIRE_DTYPE, tm=256, tn=1024, tk=DW_TK)
    grads["ple_gate_w"] = _matmul(r_bf, dgp, "tn", "dw_ple_gate", WIRE_DTYPE, tm=DW_TM, tn=1024, tk=DW_TK)
    grads["w_down"] = _matmul(act, dr_bf, "tn", "dw_down", WIRE_DTYPE, tm=DW_TM, tn=1024, tk=DW_TK)
    dup = _mlp_bwd1(dr_bf, wts["w_down"], act, tm, 1024)
    grads["w_up"] = _matmul(h1_bf, dup, "tn", "dw_up", WIRE_DTYPE, tm=DW_TM, tn=1024, tk=DW_TK, stack_out=True)
    da1, da1_bf, acc_ln1 = _mlp_bwd2(dup, wts["w_up"], dr, a1, g1, tms)
    grads["w_out"] = _matmul(mixed, da1_bf, "tn", "dw_out", WIRE_DTYPE, tm=DW_TM, tn=1024, tk=DW_TK)
    sent = send_late_grads(grads)
    do, dproj, dyp, acc_mix = _mix_bwd(da1_bf, wts["w_out"], o, gates, ypre, pool_scale, wo_row, tms, sent)
    dproj, grads["pool_w"] = _pool_bwd(dyp, d_bf, wts["pool_w"], dproj, tm)
    dqkv_act, dbg = _dn_bwd(do, qkv_act, bg, bgt, u, w, qg, kg, attn, ymat, vn, states, egl)
    dproj, acc_conv = _conv_bwd(dqkv_act, dsilu, proj, wts["conv_w"], dproj, tm)
    dproj, acc_ba = _ba_bwd(dbg, bg, proj, al_row, dtb_row, dproj, tm)
    dw_cat = _matmul(h0_bf, dproj, "tn", "dw_in", WIRE_DTYPE, tm=DW_TM, tn=1152, tk=DW_TK)
    grads["w_in"] = _w_in_by_chip(dw_cat)
    sent = send_early_grads(grads)
    grad_x, acc_in = _ln_in_bwd(dproj, w_cat, da1, x, g_in, tms, sent)

    grads["conv_w"] = acc_conv[0:CONV_K]
    grads["ln_in_g"], grads["ln_in_b"] = acc_in[0], acc_in[1]
    grads["ln1_g"], grads["ln1_b"] = acc_ln1[0], acc_ln1[1]
    grads["ln2_g"], grads["ln2_b"] = acc_tail[0], acc_tail[1]
    grads["pool_scale"] = acc_mix[0]
    grads["o_norm_w"] = acc_mix[1, 0:HEAD_DIM]
    grads["a_log"] = acc_ba[0, HEADS:2 * HEADS]
    grads["dt_bias"] = acc_ba[1, HEADS:2 * HEADS]
    loss = acc_tail[2, 0]
    return grad_x, grads, loss


MESH = pl.DeviceIdType.MESH
ANY = pl.BlockSpec(memory_space=pl.ANY)


def _chip_of(k, x, y):
    chip = (2 * x + y + k) % N_CHIPS
    return chip // 2, chip % 2


def _place():
    x, y, c = lax.axis_index("x"), lax.axis_index("y"), lax.axis_index("c")
    return x, y, c, 2 * x + y


def _half(rows, c):
    return pl.ds(pl.multiple_of(c * (rows // 2), 16), rows // 2)


def _remote(src, dst, send_sem, recv_sem, device_id):
    return pltpu.make_async_remote_copy(src_ref=src, dst_ref=dst, send_sem=send_sem, recv_sem=recv_sem,
                                        device_id=device_id, device_id_type=MESH)


def _tile_rows(rows):
    for tr in (256, 128, 64, 32, 16):
        if rows % tr == 0:
            return tr
    raise ValueError(rows)


def _first_gather_copies(srcs, lands, send, recv, place):
    copies = []
    for a in range(len(srcs)):
        whole = a == len(srcs) - 1
        for k in range(N_CHIPS):
            if place is None:
                copies.append(None)
                continue
            x, y, c, me = place
            sems = (send.at[a * N_CHIPS + k], recv.at[a * N_CHIPS + k])
            if k == 0:
                copies.append(_remote(srcs[a], lands[a].at[me], *sems, (x, y, 1 - c)))
                continue
            tx, ty = _chip_of(k, x, y)
            if whole:
                copies.append(_remote(srcs[a], lands[a].at[me], *sems, (tx, ty, c)))
            else:
                mine = _half(srcs[a].shape[0], c)
                copies.append(_remote(srcs[a].at[mine], lands[a].at[me, mine], *sems, (tx, ty, c)))
    return copies


def _pass_halves(stacks):
    n = len(stacks)

    def body(*refs):
        outs = refs[n:2 * n]
        send, recv = refs[2 * n:]
        x, y, c, me = _place()
        copies = []
        for a in range(n):
            for k in range(1, N_CHIPS):
                landed = outs[a].at[(me + N_CHIPS - k) % N_CHIPS, _half(stacks[a].shape[1], c)]
                copies.append(_remote(landed, landed, send.at[a * N_CHIPS + k], recv.at[a * N_CHIPS + k],
                                      (x, y, 1 - c)))
        for cp in copies:
            cp.start()
        for cp in copies:
            cp.wait_send()
        for a in range(n):
            for k in range(1, N_CHIPS):
                passed = outs[a].at[(me + N_CHIPS - k) % N_CHIPS, _half(stacks[a].shape[1], 1 - c)]
                _remote(passed, passed, send.at[a * N_CHIPS + k], recv.at[a * N_CHIPS + k], (x, y, c)).wait_recv()

    sems = pltpu.SemaphoreType.DMA((n * N_CHIPS,))
    return pl.pallas_call(
        body, name="pass_halves", out_shape=[SDS(s.shape, s.dtype) for s in stacks],
        in_specs=[ANY] * n, out_specs=[ANY] * n, scratch_shapes=[sems, sems],
        input_output_aliases={a: a for a in range(n)},
    )(*stacks)


def _swap_halves(gs):
    n = len(gs)

    def body(*refs):
        ins, theirs = refs[0:n], refs[n:2 * n]
        send, recv = refs[2 * n:]
        x, y, c, _ = _place()
        copies = [_remote(ins[a].at[:, _half(gs[a].shape[1], 1 - c)], theirs[a], send.at[a], recv.at[a],
                          (x, y, 1 - c)) for a in range(n)]
        for cp in copies:
            cp.start()
        for cp in copies:
            cp.wait()

    return pl.pallas_call(
        body, name="swap_halves", out_shape=[SDS((N_CHIPS, g.shape[1] // 2, g.shape[2]), g.dtype) for g in gs],
        in_specs=[ANY] * n, out_specs=[ANY] * n, scratch_shapes=[pltpu.SemaphoreType.DMA((n,))] * 2,
    )(*gs)


def _send_to_sibling(hs):
    n = len(hs)

    def body(*refs):
        ins, outs = refs[0:n], refs[n:2 * n]
        send, recv = refs[2 * n:]
        x, y, c, _ = _place()
        copies = [_remote(ins[a], outs[a], send.at[a], recv.at[a], (x, y, 1 - c)) for a in range(n)]
        for cp in copies:
            cp.start()
        for cp in copies:
            cp.wait()

    return pl.pallas_call(
        body, name="send_to_sibling", out_shape=[SDS(h.shape, h.dtype) for h in hs],
        in_specs=[ANY] * n, out_specs=[ANY] * n, scratch_shapes=[pltpu.SemaphoreType.DMA((n,))] * 2,
    )(*hs)


HBM = pl.BlockSpec(memory_space=pltpu.HBM)
SEM = pl.BlockSpec(memory_space=pltpu.SEMAPHORE)
EFFECT = pltpu.SideEffectType.DATAFLOW_SIDE_EFFECTING


def _in_hbm(a):
    return pltpu.with_memory_space_constraint(a, pltpu.HBM)


def _split_copy_start(name, srcs, lands, copies_of, after):
    n = len(srcs)
    n_copies = len(copies_of(srcs, lands, None, None, None))

    def body(*refs):
        src_refs, land_refs = refs[0:n], refs[n:2 * n]
        send, recv = refs[2 * n + 1], refs[2 * n + 2]
        token = refs[-1]
        for cp in copies_of(src_refs, land_refs, send, recv, _place()):
            cp.start()
        token[...] = jnp.zeros_like(token)

    sems = pltpu.SemaphoreType.DMA((n_copies,))
    out = pl.pallas_call(
        body, name=name,
        out_shape=[sems, sems] + [pltpu.HBM(a.shape, a.dtype) for a in list(srcs) + list(lands)] + [SDS((8, 128), F32)],
        in_specs=[HBM] * (2 * n) + [ANY],
        out_specs=[SEM, SEM] + [HBM] * (2 * n) + [pl.BlockSpec(memory_space=pltpu.VMEM)],
        input_output_aliases={i: 2 + i for i in range(2 * n)},
        compiler_params=pltpu.CompilerParams(has_side_effects=EFFECT),
    )(*[_in_hbm(a) for a in list(srcs) + list(lands)], after)
    return out[0], out[1], out[2:2 + n], out[2 + n:2 + 2 * n], out[-1]


def _split_copy_wait(name, send, recv, srcs, lands, after, copies_of):
    n = len(srcs)
    after = list(after) if isinstance(after, (list, tuple)) else [after]

    def body(*refs):
        src_refs, land_refs = refs[0:n], refs[n:2 * n]
        send_ref, recv_ref = refs[2 * n], refs[2 * n + 1]
        for cp in copies_of(src_refs, land_refs, send_ref, recv_ref, _place()):
            cp.wait_send()
            cp.wait_recv()

    out = pl.pallas_call(
        body, name=name, out_shape=[pltpu.HBM(a.shape, a.dtype) for a in list(srcs) + list(lands)],
        in_specs=[HBM] * (2 * n) + [SEM, SEM] + [ANY] * len(after), out_specs=[HBM] * (2 * n),
        input_output_aliases={i: i for i in range(2 * n)},
        compiler_params=pltpu.CompilerParams(has_side_effects=EFFECT),
    )(*srcs, *lands, send, recv, *after)
    return out[0:n], out[n:2 * n]


def _late_gather_copies(srcs, lands, send, recv, place):
    copies = []
    for a in range(len(srcs)):
        for k in range(N_CHIPS):
            if place is None:
                copies.append(None)
                continue
            x, y, c, me = place
            if k == 0:
                target = (x, y, 1 - c)
            else:
                tx, ty = _chip_of(k, x, y)
                target = (tx, ty, c)
            copies.append(_remote(srcs[a], lands[a].at[me], send.at[a * N_CHIPS + k], recv.at[a * N_CHIPS + k], target))
    return copies


def _late_scatter_copies(srcs, lands, send, recv, place):
    copies = []
    for a in range(len(srcs)):
        for k in range(1, N_CHIPS):
            if place is None:
                copies.append(None)
                continue
            x, y, c, _ = place
            tx, ty = _chip_of(k, x, y)
            copies.append(_remote(srcs[a].at[2 * tx + ty], lands[a].at[k - 1], send.at[a * (N_CHIPS - 1) + k - 1],
                                  recv.at[a * (N_CHIPS - 1) + k - 1], (tx, ty, c)))
    return copies


def _add_pair(g, theirs, name):
    _, rows, cols = g.shape
    half = rows // 2
    tr = _tile_rows(half)

    def body(g_ref, t_ref, o_ref):
        own = g_ref[lax.axis_index("c")]
        o_ref[...] = (own.astype(F32) + t_ref[...].astype(F32)).astype(o_ref.dtype)

    blk = pl.BlockSpec((None, tr, cols), lambda j, i: (j, i, 0))
    return _pc(body, "add_" + name, (N_CHIPS, half // tr),
               [pl.BlockSpec((None, 2, tr, cols), lambda j, i: (j, 0, i, 0)), blk], blk,
               SDS((N_CHIPS, half, cols), g.dtype), sem=("parallel", "parallel"))(
                   g.reshape(N_CHIPS, 2, half, cols), theirs)


def _sum_slabs(pair, landed, name):
    _, rows, cols = pair.shape
    tr = _tile_rows(rows)

    def body(p_ref, r_ref, o_ref):
        acc = p_ref[2 * lax.axis_index("x") + lax.axis_index("y")].astype(F32)
        for k in range(N_CHIPS - 1):
            acc = acc + r_ref[k].astype(F32)
        o_ref[...] = acc

    return _pc(body, "sum_" + name, (rows // tr,),
               [pl.BlockSpec((N_CHIPS, tr, cols), lambda i: (0, i, 0)),
                pl.BlockSpec((N_CHIPS - 1, tr, cols), lambda i: (0, i, 0))],
               _row(tr, cols), SDS((rows, cols), F32), sem=("parallel",))(pair, landed)


def _adamw_math(w, g, m, v):
    m = ADAM_B1 * m + (1.0 - ADAM_B1) * g
    v = ADAM_B2 * v + (1.0 - ADAM_B2) * (g * g)
    m_hat = m / (1.0 - ADAM_B1 ** ADAM_STEP)
    v_hat = v / (1.0 - ADAM_B2 ** ADAM_STEP)
    delta = -ADAM_LR * (m_hat / (jnp.sqrt(v_hat) + ADAM_EPS) + ADAM_WD * w)
    return delta, m, v


def _adamw_2d(w, g_own, g_sib, m, v, name, halves):
    lead = w.ndim == 3
    rows, cols = w.shape[-2:]
    tr = _tile_rows(rows // 2)
    nh = rows // 2 // tr if halves else rows // tr

    def body(w_ref, go_ref, gs_ref, m_ref, v_ref, g_out, d_out, m_out, v_out):
        if halves:
            mine = (pl.program_id(0) // nh) == lax.axis_index("c")
            g = jnp.where(mine, go_ref[...], gs_ref[...])
        else:
            g = go_ref[...] + gs_ref[...]
        delta, mn, vn = _adamw_math(w_ref[...], g, m_ref[...], v_ref[...])
        g_out[...] = g
        d_out[...] = delta
        m_out[...] = mn
        v_out[...] = vn

    r = _row(tr, cols)
    p = pl.BlockSpec((None, tr, cols), lambda i: (0, i, 0)) if lead else r
    h = pl.BlockSpec((tr, cols), lambda i: (i % nh, 0))
    return _pc(body, "adamw_" + name, (rows // tr,), [p, h, h, p, p], [r] * 4, [SDS((rows, cols), F32)] * 4,
               sem=("parallel",))(w, g_own, g_sib, m, v)


def _small_allreduce_adamw(mine, w, m, v, sizes):
    shape = mine.shape
    n = len(sizes)

    def body(mine_ref, w_ref, m_ref, v_ref, *rest):
        outs, (buf_ref, res_ref, send_sems, recv_sems) = rest[:-4], rest[-4:]
        x, y, c = lax.axis_index("x"), lax.axis_index("y"), lax.axis_index("c")
        me = 4 * x + 2 * y + c
        buf_ref[me] = mine_ref[...]
        copies = []
        for k in range(1, N_DEV):
            tgt = (me + k) % N_DEV
            copies.append(pltpu.make_async_remote_copy(
                src_ref=mine_ref, dst_ref=buf_ref.at[me], send_sem=send_sems.at[k], recv_sem=recv_sems.at[k],
                device_id=(tgt // 4, (tgt // 2) % 2, tgt % 2), device_id_type=MESH))
        for cp in copies:
            cp.start()
        for k in range(1, N_DEV):
            src = (me + N_DEV - k) % N_DEV
            pltpu.make_async_remote_copy(
                src_ref=mine_ref, dst_ref=buf_ref.at[src], send_sem=send_sems.at[k], recv_sem=recv_sems.at[k],
                device_id=(x, y, c), device_id_type=MESH).wait_recv()
        for cp in copies:
            cp.wait_send()
        g = buf_ref[0]
        for j in range(1, N_DEV):
            g = g + buf_ref[j]
        delta, mn, vn = _adamw_math(w_ref[...], g, m_ref[...], v_ref[...])
        for kind, val in enumerate((g, delta, mn, vn)):
            res_ref[kind] = val
            for i, size in enumerate(sizes):
                outs[kind * (n + 1) + i][...] = res_ref[kind, i:i + 1, 0:size]
            outs[kind * (n + 1) + n][...] = res_ref[kind, SMALL_CONV_AT:SMALL_CONV_AT + SMALL_CONV_ROWS, :]
        outs[-1][...] = res_ref[0, n:n + 1, 0:1]

    vm = pl.BlockSpec(memory_space=pltpu.VMEM)
    per_kind = [SDS((1, size), F32) for size in sizes] + [SDS((SMALL_CONV_ROWS, D_MODEL), F32)]
    out_shape = per_kind * 4 + [SDS((1, 1), F32)]
    out = pl.pallas_call(
        body, name="small_allreduce_adamw", out_shape=out_shape, in_specs=[vm] * 4, out_specs=[vm] * len(out_shape),
        scratch_shapes=[pltpu.VMEM((N_DEV,) + shape, F32), pltpu.VMEM((4,) + shape, F32),
                        pltpu.SemaphoreType.DMA((N_DEV,)), pltpu.SemaphoreType.DMA((N_DEV,))],
    )(mine, w, m, v)
    return [out[kind * (n + 1):(kind + 1) * (n + 1)] for kind in range(4)], out[-1]


def _as2d(a):
    return a.reshape(-1, a.shape[-1])


SEGMENTS = ((C_POOL, K_U, POOL_WIDTH), (C_QKV, K_QKV, QKV_WIDTH), (C_Z, K_Z, DN_WIDTH), (C_BETA, K_BA, 2 * HEADS),
            (C_GA, K_GA, D_MODEL), (C_GB, K_GB, D_MODEL))
SHARD_COLS = IN_WIDTH // N_CHIPS


def _w_cat(stack):
    pieces = []
    for c0, _, width in sorted(SEGMENTS, key=lambda seg: seg[1]):
        a = c0
        while a < c0 + width:
            chip = a // SHARD_COLS
            b = min(c0 + width, (chip + 1) * SHARD_COLS)
            pieces.append(stack[chip][:, a - chip * SHARD_COLS:b - chip * SHARD_COLS])
            a = b
    pieces.append(jnp.zeros((D_MODEL, CAT_WIDTH - K_BA - 2 * HEADS), stack.dtype))
    return jnp.concatenate(pieces, axis=1)


def _w_in_by_chip(dw_cat):
    slabs = []
    for chip in range(N_CHIPS):
        lo, hi = chip * SHARD_COLS, (chip + 1) * SHARD_COLS
        pieces = []
        for c0, k0, width in sorted(SEGMENTS):
            a, b = max(c0, lo), min(c0 + width, hi)
            if a < b:
                pieces.append(dw_cat[:, k0 + a - c0:k0 + b - c0])
        slabs.append(jnp.concatenate(pieces, axis=1))
    return jnp.stack(slabs)


WEIGHT_LAYOUT = {
    "w_in": lambda s: ("w_cat", _w_cat(s)),
    "pool_w": lambda s: ("pool_w", s.reshape(N_CHIPS, 4, POOL_GROUP, POOL_OUT_GROUP // N_CHIPS)
                         .transpose(1, 2, 0, 3).reshape(4, POOL_GROUP, POOL_OUT_GROUP)),
    "w_out": lambda s: ("w_out", s.reshape(D_MODEL, D_MODEL)),
    "w_up": lambda s: ("w_up", s),
    "w_down": lambda s: ("w_down", s.reshape(D_FF, D_MODEL)),
    "ple_gate_w": lambda s: ("ple_gate_w", s.reshape(D_MODEL, D_MODEL)),
    "ple_proj_w": lambda s: ("ple_proj_w", s.transpose(1, 0, 2).reshape(PLE_DIM, D_MODEL)),
}

GRAD_LAYOUT = {
    "w_in": lambda g: g,
    "pool_w": lambda g: g.reshape(4, POOL_GROUP, N_CHIPS, POOL_OUT_GROUP // N_CHIPS)
                         .transpose(2, 0, 1, 3).reshape(N_CHIPS, 4 * POOL_GROUP, POOL_OUT_GROUP // N_CHIPS),
    "w_out": lambda g: g.reshape(N_CHIPS, D_MODEL // N_CHIPS, D_MODEL),
    "w_up": lambda g: g,
    "w_down": lambda g: g.reshape(N_CHIPS, D_FF // N_CHIPS, D_MODEL),
    "ple_gate_w": lambda g: g.reshape(N_CHIPS, D_MODEL // N_CHIPS, D_MODEL),
    "ple_proj_w": lambda g: g.reshape(PLE_DIM, N_CHIPS, D_MODEL // N_CHIPS).transpose(1, 0, 2),
}


def _full_weights(names, stacks):
    return dict(WEIGHT_LAYOUT[n](s.astype(MXU_DTYPE)) for n, s in zip(names, stacks))


def _grads_by_chip(names, grads):
    return [GRAD_LAYOUT[n](grads[n]).astype(WIRE_DTYPE) for n in names]


def _pack_small(rows, conv, name):
    n = len(rows)

    def body(*refs):
        out = refs[n + 1]
        out[...] = jnp.zeros_like(out)
        for i in range(n):
            out[i:i + 1, :] = refs[i][...]
        out[SMALL_CONV_AT:SMALL_CONV_AT + SMALL_CONV_ROWS, :] = refs[n][...]

    vm = pl.BlockSpec(memory_space=pltpu.VMEM)
    return pl.pallas_call(body, name=name, out_shape=SDS((SMALL_CONV_AT + SMALL_CONV_ROWS, D_MODEL), F32),
                          in_specs=[vm] * (n + 1), out_specs=vm)(*rows, conv)


def _pad_row(a):
    a = a.reshape(1, -1).astype(F32)
    return jnp.pad(a, ((0, 0), (0, D_MODEL - a.shape[1])))


def kernel(x, p, ln_in_g, ln_in_b, w_in, pool_w, pool_scale, conv_w, a_log, dt_bias, o_norm_w, w_out, ln1_g, ln1_b, w_up, w_down, ple_gate_w, ple_proj_w, ln2_g, ln2_b, loss_target, m_ln_in_g, m_ln_in_b, m_w_in, m_pool_w, m_pool_scale, m_conv_w, m_a_log, m_dt_bias, m_o_norm_w, m_w_out, m_ln1_g, m_ln1_b, m_w_up, m_w_down, m_ple_gate_w, m_ple_proj_w, m_ln2_g, m_ln2_b, v_ln_in_g, v_ln_in_b, v_w_in, v_pool_w, v_pool_scale, v_conv_w, v_a_log, v_dt_bias, v_o_norm_w, v_w_out, v_ln1_g, v_ln1_b, v_w_up, v_w_down, v_ple_gate_w, v_ple_proj_w, v_ln2_g, v_ln2_b):
    given = dict(locals())
    chip = 2 * lax.axis_index("x") + lax.axis_index("y")

    shard = lambda n: _as2d(given[n]).astype(WIRE_DTYPE)

    wts = {"ln_in_g": ln_in_g, "ln_in_b": ln_in_b, "pool_scale": pool_scale[0], "a_log": a_log[0],
           "dt_bias": dt_bias[0], "o_norm_w": o_norm_w[0], "ln1_g": ln1_g[0], "ln1_b": ln1_b[0],
           "ln2_g": ln2_g[0], "ln2_b": ln2_b[0]}

    conv_pad = jnp.pad(conv_w[0], ((0, 8 - CONV_K), (0, 0)))
    first_srcs = [shard(n) for n in EARLY] + [conv_pad]
    first_lands = [lax.empty((N_CHIPS,) + s.shape, s.dtype) for s in first_srcs]
    fsend, frecv, fsrcs, flands, start_token = _split_copy_start(
        "first_gather_start", first_srcs, first_lands, _first_gather_copies, first_srcs[0])
    late = {}
    for n in ("w_in", "m_w_in", "v_w_in"):
        given[n], _ = lax.optimization_barrier((given[n], start_token))

    def first_weights(after):
        _, lands = _split_copy_wait("first_gather_wait", fsend, frecv, fsrcs, flands,
                                    [after, given["w_in"], given["m_w_in"], given["v_w_in"]], _first_gather_copies)
        stacks = _pass_halves(lands[0:len(EARLY)])
        first = _full_weights(EARLY, stacks)
        first["conv_w"] = jnp.concatenate([lands[len(EARLY)][j, 0:CONV_K] for j in range(N_CHIPS)], axis=1)
        late_srcs = [shard(n) for n in LATE]
        late_lands = [lax.empty((N_CHIPS,) + s.shape, s.dtype) for s in late_srcs]
        late["send"], late["recv"], late["srcs"], late["lands"], token = _split_copy_start(
            "late_gather_start", late_srcs, late_lands, _late_gather_copies, stacks[0])
        return first, token

    def late_weights(after):
        _, stacks = _split_copy_wait("late_gather_wait", late["send"], late["recv"], late["srcs"], late["lands"],
                                     after, _late_gather_copies)
        return _full_weights(LATE, stacks)

    scatter = {}

    def send_late_grads(grads):
        srcs = _grads_by_chip(LATE, grads)
        lands = [lax.empty((N_CHIPS - 1,) + g.shape[1:], g.dtype) for g in srcs]
        scatter["send"], scatter["recv"], scatter["srcs"], scatter["lands"], token = _split_copy_start(
            "late_scatter_start", srcs, lands, _late_scatter_copies, srcs[0])
        return token

    last = {}

    def send_early_grads(grads):
        by_chip = _grads_by_chip(EARLY, grads)
        theirs = _swap_halves(by_chip)
        pair = [_add_pair(g, t, n) for g, t, n in zip(by_chip, theirs, EARLY)]
        lands = [lax.empty((N_CHIPS - 1,) + q.shape[1:], q.dtype) for q in pair]
        last["send"], last["recv"], last["srcs"], last["lands"], token = _split_copy_start(
            "early_scatter_start", pair, lands, _late_scatter_copies, pair[0])
        return token

    grad_x, grads, loss = _local_step(x[0], p[0, 0], loss_target[0], wts, start_token, first_weights, late_weights,
                                      send_late_grads, send_early_grads)

    late_mine, late_landed = _split_copy_wait("late_scatter_wait", scatter["send"], scatter["recv"], scatter["srcs"],
                                              scatter["lands"], grad_x, _late_scatter_copies)
    late_part = [_sum_slabs(q, r, n) for q, r, n in zip(late_mine, late_landed, LATE)]
    pair, landed = _split_copy_wait("early_scatter_wait", last["send"], last["recv"], last["srcs"], last["lands"],
                                    grad_x, _late_scatter_copies)
    reduced = [_sum_slabs(q, r, n) for q, r, n in zip(pair, landed, EARLY)]
    from_sibling = _send_to_sibling(reduced + late_part)
    big_out = {}
    for n, g_own, g_sib in zip(EARLY + LATE, reduced + late_part, from_sibling):
        view = (lambda a: a) if given[n].ndim == 3 else _as2d
        res = _adamw_2d(view(given[n]), g_own, g_sib, view(given["m_" + n]), view(given["v_" + n]), n,
                        halves=n in EARLY)
        big_out[n] = [r.reshape(given[n].shape) for r in res]

    conv_cols = QKV_WIDTH // N_CHIPS

    def small_pack(get, conv, extra, name):
        if conv.shape[1] != QKV_WIDTH:
            conv = lax.dynamic_update_slice(jnp.zeros((CONV_K, QKV_WIDTH), F32), conv, (0, chip * conv_cols))
        return _pack_small([_pad_row(get(n)) for n in SMALL_NAMES] + extra, conv.reshape(SMALL_CONV_ROWS, D_MODEL), name)

    mine_small = small_pack(lambda n: grads[n], grads["conv_w"], [jnp.full((1, D_MODEL), loss, F32)], "pack_small_g")
    packed_small = [small_pack(lambda n: given[prefix + n], given[prefix + "conv_w"][0], [], "pack_small_" + tag)
                    for prefix, tag in (("", "w"), ("m_", "m"), ("v_", "v"))]
    small_out, loss_sum = _small_allreduce_adamw(mine_small, *packed_small, [given[n].size for n in SMALL_NAMES])

    def small_get(k, n):
        if n == "conv_w":
            full = small_out[k][len(SMALL_NAMES)].reshape(CONV_K, QKV_WIDTH)
            return lax.dynamic_slice(full, (0, chip * conv_cols), (CONV_K, conv_cols)).reshape(given[n].shape)
        return small_out[k][SMALL_NAMES.index(n)].reshape(given[n].shape)

    order = ["ln_in_g", "ln_in_b", "w_in", "pool_w", "pool_scale", "conv_w", "a_log", "dt_bias", "o_norm_w", "w_out",
             "ln1_g", "ln1_b", "w_up", "w_down", "ple_gate_w", "ple_proj_w", "ln2_g", "ln2_b"]
    outs = [loss_sum.reshape(()), grad_x[None]]
    for k in range(4):
        for n in order:
            outs.append(big_out[n][k] if n in big_out else small_get(k, n))
    return tuple(outs)
```

```python
import jax
import jax.numpy as jnp
from jax import lax
from jax.experimental import pallas as pl
from jax.experimental.pallas import tpu as pltpu

F32 = jnp.float32
MXU_DTYPE = jnp.bfloat16
WIRE_DTYPE = jnp.bfloat16
SDS = jax.ShapeDtypeStruct

D_MODEL = 1024
POOL_WINDOWS = (2, 4, 8, 16)
POOL_WIDTH = 512
POOL_GROUP = 128
POOL_OUT_GROUP = 256
HEADS = 8
HEAD_DIM = 128
DN_WIDTH = HEADS * HEAD_DIM
QKV_WIDTH = 3 * DN_WIDTH
CONV_K = 4
CHUNK = 128
DN_BWD_CHUNKS = 2
DN_FWD_CHUNKS = 2
DW_TK = 1024
DW_TM = 1024
D_FF = 4096
PLE_DIM = 256
LN_EPS = 1e-5
RMS_EPS = 1e-6
L2_EPS = 1e-6
ALPHA = 2.0 ** 0.25
Q_SCALE = HEAD_DIM ** -0.5
IN_WIDTH = 6672
C_POOL, C_QKV, C_Z, C_BETA, C_A, C_GA, C_GB = 0, 512, 3584, 4608, 4616, 4624, 5648
K_QKV, K_Z, K_GA, K_GB, K_U, K_BA, CAT_WIDTH = 0, 3072, 4096, 5120, 6144, 6656, 6912
P_U, P_BA, PROJ_F32_WIDTH = 3072, 3584, 3840

ADAM_LR, ADAM_B1, ADAM_B2, ADAM_EPS, ADAM_WD, ADAM_STEP = 0.001, 0.9, 0.999, 1e-08, 0.01, 10

N_CHIPS = 4
N_DEV = 8
VMEM_LIMIT = 56 * 1024 * 1024

EARLY = ("w_in", "pool_w")
LATE = ("w_out", "w_up", "w_down", "ple_gate_w", "ple_proj_w")
SMALL_NAMES = ("ln_in_g", "ln_in_b", "pool_scale", "ln1_g", "ln1_b", "ln2_g", "ln2_b", "o_norm_w", "a_log", "dt_bias")
SMALL_CONV_AT = 12
SMALL_CONV_ROWS = CONV_K * QKV_WIDTH // D_MODEL


def _mx(a):
    return a.astype(MXU_DTYPE)


def _dot(a, b):
    return lax.dot_general(_mx(a), _mx(b), (((1,), (0,)), ((), ())), preferred_element_type=F32)


def _dot_nt(a, b):
    return lax.dot_general(_mx(a), _mx(b), (((1,), (1,)), ((), ())), preferred_element_type=F32)


def _dot_tn(a, b):
    return lax.dot_general(_mx(a), _mx(b), (((0,), (0,)), ((), ())), preferred_element_type=F32)


def _sigmoid(x):
    return 0.5 * jnp.tanh(0.5 * x) + 0.5


def _softplus(x):
    return jnp.maximum(x, 0.0) + jnp.log(1.0 + jnp.exp(-jnp.abs(x)))


def _pc(body, name, grid, in_specs, out_specs, out_shape, scratch=(), sem=None, aliases=None):
    return pl.pallas_call(
        body, out_shape=out_shape, grid=grid, in_specs=in_specs, out_specs=out_specs,
        scratch_shapes=scratch, name=name, input_output_aliases=aliases or {},
        compiler_params=pltpu.CompilerParams(dimension_semantics=sem, vmem_limit_bytes=VMEM_LIMIT))


def _row(tm, n):
    return pl.BlockSpec((tm, n), lambda i: (i, 0))


def _const(shape):
    nd = len(shape)
    return pl.BlockSpec(shape, lambda *_: (0,) * nd)


def _matmul(a, b, mode, name, out_dtype=F32, tm=512, tn=512, tk=512, stack_out=False):
    if mode == "nn":
        (m, k), n = a.shape, b.shape[1]
    elif mode == "nt":
        (m, k), n = a.shape, b.shape[0]
    else:
        (k, m), n = a.shape, b.shape[1]
    tm, tn, tk = min(tm, m), min(tn, n), min(tk, k)
    assert m % tm == 0 and n % tn == 0 and k % tk == 0, (name, m, n, k, tm, tn, tk)
    nk = k // tk
    if mode == "nn":
        a_spec = pl.BlockSpec((tm, tk), lambda i, j, kk: (i, kk))
        b_spec = pl.BlockSpec((tk, tn), lambda i, j, kk: (kk, j))
        dot = _dot
    elif mode == "nt":
        a_spec = pl.BlockSpec((tm, tk), lambda i, j, kk: (i, kk))
        b_spec = pl.BlockSpec((tn, tk), lambda i, j, kk: (j, kk))
        dot = _dot_nt
    else:
        a_spec = pl.BlockSpec((tk, tm), lambda i, j, kk: (kk, i))
        b_spec = pl.BlockSpec((tk, tn), lambda i, j, kk: (kk, j))
        dot = _dot_tn

    def body(a_ref, b_ref, o_ref, *acc):
        if nk == 1:
            o_ref[...] = dot(a_ref[...], b_ref[...]).astype(out_dtype)
            return
        acc_ref, kk = acc[0], pl.program_id(2)

        @pl.when(kk == 0)
        def _():
            acc_ref[...] = dot(a_ref[...], b_ref[...])

        @pl.when((kk > 0) & (kk < nk - 1))
        def _():
            acc_ref[...] += dot(a_ref[...], b_ref[...])

        @pl.when(kk == nk - 1)
        def _():
            o_ref[...] = (acc_ref[...] + dot(a_ref[...], b_ref[...])).astype(out_dtype)

    if stack_out:
        o_spec, o_shape = pl.BlockSpec((None, tm, tn), lambda i, j, kk: (j, i, 0)), SDS((n // tn, m, tn), out_dtype)
    else:
        o_spec, o_shape = pl.BlockSpec((tm, tn), lambda i, j, kk: (i, j)), SDS((m, n), out_dtype)
    return _pc(body, name, (m // tm, n // tn, nk), [a_spec, b_spec], o_spec, o_shape,
               scratch=[pltpu.VMEM((tm, tn), F32)] if nk > 1 else [],
               sem=("parallel", "parallel", "arbitrary"))(a, b)


PROJ_TN = 768


def _proj_conv(h0_bf, w_cat, conv_w, tm, after):
    t = h0_bf.shape[0]
    n_qkv = QKV_WIDTH // PROJ_TN
    n_gate = 3 * D_MODEL // PROJ_TN

    def body(h_ref, w_ref, cw_ref, after_ref, o_ref, gate_ref, act_ref, ds_ref, carry_ref, ext_ref):
        @pl.when(pl.program_id(0) == 0)
        def _():
            carry_ref[...] = jnp.zeros_like(carry_ref)

        h = h_ref[...]

        def project(cb):
            cols = slice(cb * PROJ_TN, (cb + 1) * PROJ_TN)
            res = _dot(h, w_ref[:, cols])
            if cb < n_qkv:
                o_ref[:, cols] = res
            elif cb < n_qkv + n_gate:
                gate_ref[:, (cb - n_qkv) * PROJ_TN:(cb - n_qkv + 1) * PROJ_TN] = _mx(res)
            else:
                o_ref[:, P_U:PROJ_F32_WIDTH] = res

        def conv(cb, part):
            cols = slice(cb * PROJ_TN, (cb + 1) * PROJ_TN)
            if part == 0:
                ext_ref[cb, 0:8, :] = carry_ref[:, cols]
                ext_ref[cb, 8:8 + tm, :] = o_ref[:, cols]
                carry_ref[:, cols] = o_ref[tm - 8:tm, cols]
            w = [cw_ref[pl.ds(k, 1), cols] for k in range(CONV_K)]
            for r in range(part * (tm // 2), (part + 1) * (tm // 2), CONV_ROWS):
                y = _conv_rows(ext_ref.at[cb], w, r, CONV_ROWS)
                s = _sigmoid(y)
                act_ref[pl.ds(r, CONV_ROWS), cols] = y * s
                ds_ref[pl.ds(r, CONV_ROWS), cols] = _mx(s * (1.0 + y * (1.0 - s)))

        pending = [(cb, part) for cb in range(n_qkv) for part in range(2)]
        project(0)
        for cb in range(1, CAT_WIDTH // PROJ_TN):
            project(cb)
            if pending and pending[0][0] < cb:
                conv(*pending.pop(0))
        for cb, part in pending:
            conv(cb, part)

    return _pc(body, "proj_conv", (t // tm,),
               [_row(tm, D_MODEL), _const((D_MODEL, CAT_WIDTH)), _const((CONV_K, QKV_WIDTH)), ANY],
               [_row(tm, PROJ_F32_WIDTH), _row(tm, 3 * D_MODEL), _row(tm, QKV_WIDTH), _row(tm, QKV_WIDTH)],
               [SDS((t, PROJ_F32_WIDTH), F32), SDS((t, 3 * D_MODEL), MXU_DTYPE), SDS((t, QKV_WIDTH), F32),
                SDS((t, QKV_WIDTH), MXU_DTYPE)],
               scratch=[pltpu.VMEM((8, QKV_WIDTH), F32), pltpu.VMEM((n_qkv, 8 + tm, PROJ_TN), F32)],
               sem=("arbitrary",))(h0_bf, w_cat, conv_w, after)


def _ln_stats(x):
    mu = jnp.mean(x, axis=-1, keepdims=True)
    xc = x - mu
    var = jnp.mean(xc * xc, axis=-1, keepdims=True)
    rstd = lax.rsqrt(var + LN_EPS)
    return xc * rstd, rstd


def _ln_bwd(dy, xhat, rstd, g):
    dxh = dy * g
    m1 = jnp.mean(dxh, axis=-1, keepdims=True)
    m2 = jnp.mean(dxh * xhat, axis=-1, keepdims=True)
    return rstd * (dxh - m1 - xhat * m2)


def _ln_in(x, g, b, tm, after):
    t, d = x.shape

    def body(x_ref, g_ref, b_ref, after_ref, h_ref, hb_ref):
        xhat, _ = _ln_stats(x_ref[...])
        h = xhat * g_ref[...] + b_ref[...]
        h_ref[...] = h
        hb_ref[...] = _mx(h)

    return _pc(body, "ln_in", (t // tm,), [_row(tm, d), _const((1, d)), _const((1, d)), ANY],
               [_row(tm, d), _row(tm, d)], [SDS((t, d), F32), SDS((t, d), MXU_DTYPE)],
               sem=("parallel",))(x, g, b, after)


def _pool_fwd(proj, pool_w, tm):
    t = proj.shape[0]
    ublk = P_U // POOL_WIDTH

    def body(u_ref, halo_ref, pw_ref, ypre_ref, d_ref, ext_ref):
        i = pl.program_id(0)
        ext_ref[0:16, :] = jnp.where(i > 0, halo_ref[...], 0.0)
        ext_ref[16:16 + tm, :] = u_ref[...]
        tok = i * tm + lax.broadcasted_iota(jnp.int32, (tm, POOL_GROUP), 0)
        for gi, w in enumerate(POOL_WINDOWS):
            cs = pl.ds(gi * POOL_GROUP, POOL_GROUP)
            ug = ext_ref[pl.ds(16, tm), cs]
            s = ug
            for k in range(1, w):
                s = s + ext_ref[pl.ds(16 - k, tm), cs]
            cnt = jnp.minimum(tok + 1, w).astype(F32)
            db = _mx(s / cnt - ug)
            d_ref[:, gi * POOL_GROUP:(gi + 1) * POOL_GROUP] = db
            ypre_ref[:, gi * POOL_OUT_GROUP:(gi + 1) * POOL_OUT_GROUP] = _dot(db, pw_ref[gi])

    halo = pl.BlockSpec((16, POOL_WIDTH), lambda i: (jnp.maximum(i * (tm // 16) - 1, 0), ublk))
    return _pc(body, "pool_fwd", (t // tm,),
               [pl.BlockSpec((tm, POOL_WIDTH), lambda i: (i, ublk)), halo, _const((4, POOL_GROUP, POOL_OUT_GROUP))],
               [_row(tm, D_MODEL), _row(tm, POOL_WIDTH)],
               [SDS((t, D_MODEL), F32), SDS((t, POOL_WIDTH), MXU_DTYPE)],
               scratch=[pltpu.VMEM((16 + tm, POOL_WIDTH), F32)], sem=("parallel",))(proj, proj, pool_w)


def _pool_bwd(dyp, d_bf, pool_w, dproj, tm):
    t = dyp.shape[0]
    n = t // tm

    def body(dy_ref, dyn_ref, d_ref, pw_ref, dproj_ref, du_ref, dpw_ref, ext_ref):
        i = pl.program_id(0)

        @pl.when(i == 0)
        def _():
            dpw_ref[...] = jnp.zeros_like(dpw_ref)

        tok = i * tm + lax.broadcasted_iota(jnp.int32, (tm + 16, POOL_GROUP), 0)
        for gi, w in enumerate(POOL_WINDOWS):
            dy = dy_ref[:, gi * POOL_OUT_GROUP:(gi + 1) * POOL_OUT_GROUP]
            dyn = dyn_ref[:, gi * POOL_OUT_GROUP:(gi + 1) * POOL_OUT_GROUP]
            pw = pw_ref[gi]
            dd = _dot_nt(dy, pw)
            ddn = jnp.where(i < n - 1, _dot_nt(dyn, pw), 0.0)
            cnt = jnp.minimum(tok + 1, w).astype(F32)
            ext_ref[0:tm, :] = dd / cnt[0:tm]
            ext_ref[tm:tm + 16, :] = ddn / cnt[tm:tm + 16]
            s = ext_ref[pl.ds(0, tm), :]
            for k in range(1, w):
                s = s + ext_ref[pl.ds(k, tm), :]
            du_ref[:, gi * POOL_GROUP:(gi + 1) * POOL_GROUP] = _mx(s - dd)
            dpw_ref[gi] += _dot_tn(d_ref[:, gi * POOL_GROUP:(gi + 1) * POOL_GROUP], dy)

    nxt = pl.BlockSpec((16, D_MODEL), lambda i: (jnp.minimum((i + 1) * (tm // 16), t // 16 - 1), 0))
    return _pc(body, "pool_bwd", (n,),
               [_row(tm, D_MODEL), nxt, _row(tm, POOL_WIDTH), _const((4, POOL_GROUP, POOL_OUT_GROUP)), ANY],
               [pl.BlockSpec((tm, POOL_WIDTH), lambda i: (i, K_U // POOL_WIDTH)),
                _const((4, POOL_GROUP, POOL_OUT_GROUP))],
               [SDS(dproj.shape, dproj.dtype), SDS((4, POOL_GROUP, POOL_OUT_GROUP), F32)],
               scratch=[pltpu.VMEM((tm + 16, POOL_GROUP), F32)], sem=("arbitrary",),
               aliases={4: 0})(dyp, dyp, d_bf, pool_w, dproj)


CONV_BLK = 512


CONV_ROWS = 32


def _conv_rows(ext_ref, w, r, rows):
    y = w[0] * ext_ref[pl.ds(r + 5, rows), :]
    for k in range(1, CONV_K):
        y = y + w[k] * ext_ref[pl.ds(r + 5 + k, rows), :]
    return y


def _conv_bwd(dact, dsilu, proj, conv_w, dproj, tm):
    t = proj.shape[0]
    n = t // tm

    def body(da_ref, dan_ref, ds_ref, dsn_ref, x_ref, xp_ref, w_ref, dproj_ref, dx_ref, dw_ref, ext_ref, dy_ref):
        i = pl.program_id(1)

        @pl.when(i == 0)
        def _():
            dw_ref[...] = jnp.zeros_like(dw_ref)

        ext_ref[0:8, :] = jnp.where(i > 0, xp_ref[...], 0.0)
        ext_ref[8:8 + tm, :] = x_ref[...]
        w = [w_ref[pl.ds(k, 1), :] for k in range(CONV_K)]

        acc = [jnp.zeros((8, CONV_BLK), F32) for _ in range(CONV_K)]
        for r in range(0, tm, CONV_ROWS):
            dy = da_ref[pl.ds(r, CONV_ROWS), :] * ds_ref[pl.ds(r, CONV_ROWS), :].astype(F32)
            dy_ref[pl.ds(r, CONV_ROWS), :] = dy
            for k in range(CONV_K):
                prod = dy * ext_ref[pl.ds(r + 5 + k, CONV_ROWS), :]
                for q in range(0, CONV_ROWS, 8):
                    acc[k] = acc[k] + prod[q:q + 8]
        dy_ref[tm:tm + 8, :] = jnp.where(i < n - 1, dan_ref[...] * dsn_ref[0:8, :].astype(F32), 0.0)
        for k in range(CONV_K):
            dw_ref[pl.ds(k, 1), :] += jnp.sum(acc[k], axis=0, keepdims=True)
        for r in range(0, tm, CONV_ROWS):
            dx = w[0] * dy_ref[pl.ds(r + 3, CONV_ROWS), :]
            for k in range(1, CONV_K):
                dx = dx + w[k] * dy_ref[pl.ds(r + 3 - k, CONV_ROWS), :]
            dx_ref[pl.ds(r, CONV_ROWS), :] = _mx(dx)

    blk = pl.BlockSpec((tm, CONV_BLK), lambda j, i: (i, j))
    prev = pl.BlockSpec((8, CONV_BLK), lambda j, i: (jnp.maximum(i * (tm // 8) - 1, 0), j))
    nxt = pl.BlockSpec((8, CONV_BLK), lambda j, i: (jnp.minimum((i + 1) * (tm // 8), t // 8 - 1), j))
    nxt16 = pl.BlockSpec((16, CONV_BLK), lambda j, i: (jnp.minimum((i + 1) * (tm // 16), t // 16 - 1), j))
    wspec = pl.BlockSpec((CONV_K, CONV_BLK), lambda j, i: (0, j))
    return _pc(body, "conv_bwd", (QKV_WIDTH // CONV_BLK, n),
               [blk, nxt, blk, nxt16, blk, prev, wspec, ANY],
               [blk, pl.BlockSpec((8, CONV_BLK), lambda j, i: (0, j))],
               [SDS(dproj.shape, dproj.dtype), SDS((8, QKV_WIDTH), F32)],
               scratch=[pltpu.VMEM((8 + tm, CONV_BLK), F32), pltpu.VMEM((8 + tm, CONV_BLK), F32)],
               sem=("parallel", "arbitrary"), aliases={7: 0})(dact, dact, dsilu, dsilu, proj, proj, conv_w, dproj)


def _lane(shape):
    return lax.broadcasted_iota(jnp.int32, shape, 1)


def _ba_fwd(proj, al_row, dtb_row, tm):
    t = proj.shape[0]
    bablk = P_BA // 128

    def body(ba_ref, al_ref, dtb_ref, bg_ref):
        ba = ba_ref[...]
        lane = _lane(ba.shape)
        g = -jnp.exp(al_ref[...]) * _softplus(ba + dtb_ref[...])
        bg_ref[...] = jnp.where(lane < HEADS, _sigmoid(ba), jnp.where(lane < 2 * HEADS, g, 0.0))

    return _pc(body, "ba_fwd", (t // tm,),
               [pl.BlockSpec((tm, 128), lambda i: (i, bablk)), _const((1, 128)), _const((1, 128))],
               _row(tm, 128), SDS((t, 128), F32), sem=("parallel",))(proj, al_row, dtb_row)


def _ba_bwd(dbg, bg, proj, al_row, dtb_row, dproj, tm):
    t = proj.shape[0]
    bablk = P_BA // 128

    def body(dbg_ref, bg_ref, ba_ref, al_ref, dtb_ref, dproj_ref, dba_ref, acc_ref):
        i = pl.program_id(0)

        @pl.when(i == 0)
        def _():
            acc_ref[...] = jnp.zeros_like(acc_ref)

        dbg_v, bg_v, ba = dbg_ref[...], bg_ref[...], ba_ref[...]
        lane = _lane(ba.shape)
        is_g = (lane >= HEADS) & (lane < 2 * HEADS)
        dbeta_raw = dbg_v * bg_v * (1.0 - bg_v)
        da_raw = dbg_v * (-jnp.exp(al_ref[...])) * _sigmoid(ba + dtb_ref[...])
        dba_ref[:, 0:128] = _mx(jnp.where(lane < HEADS, dbeta_raw, jnp.where(is_g, da_raw, 0.0)))
        dba_ref[:, 128:CAT_WIDTH - K_BA] = jnp.zeros((tm, CAT_WIDTH - K_BA - 128), dba_ref.dtype)
        acc_ref[0:1, :] += jnp.sum(jnp.where(is_g, dbg_v * bg_v, 0.0), axis=0, keepdims=True)
        acc_ref[1:2, :] += jnp.sum(jnp.where(is_g, da_raw, 0.0), axis=0, keepdims=True)

    tail = CAT_WIDTH - K_BA
    return _pc(body, "ba_bwd", (t // tm,),
               [_row(tm, 128), _row(tm, 128), pl.BlockSpec((tm, 128), lambda i: (i, bablk)),
                _const((1, 128)), _const((1, 128)), ANY],
               [pl.BlockSpec((tm, tail), lambda i: (i, K_BA // tail)), _const((8, 128))],
               [SDS(dproj.shape, dproj.dtype), SDS((8, 128), F32)],
               sem=("arbitrary",), aliases={5: 0})(dbg, bg, proj, al_row, dtb_row, dproj)


def _each(f, *lists):
    return [f(*a) for a in zip(*lists)]


def _rowsum(a):
    return jnp.sum(a, axis=1, keepdims=True)


def _chunk_terms(qs, ks, bgvs, g_rows, hs):
    c = CHUNK
    ii = lax.broadcasted_iota(jnp.int32, (c, c), 0)
    jj = lax.broadcasted_iota(jnp.int32, (c, c), 1)
    lane = _lane((c, 128))
    incl = ii >= jj
    beta = [_rowsum(jnp.where(lane == h, bgv, 0.0)) for h, bgv in zip(hs, bgvs)]
    g_col = [_rowsum(jnp.where(lane == HEADS + h, bgv, 0.0)) for h, bgv in zip(hs, bgvs)]
    rq = _each(lambda q: lax.rsqrt(_rowsum(q * q) + L2_EPS), qs)
    rk = _each(lambda k: lax.rsqrt(_rowsum(k * k) + L2_EPS), ks)
    yq = _each(jnp.multiply, qs, rq)
    kn = _each(jnp.multiply, ks, rk)
    qn = _each(lambda a: a * Q_SCALE, yq)
    gc_col = _each(lambda g: _rowsum(jnp.where(jj <= ii, g, 0.0)), g_rows)
    gc_row = _each(lambda g: jnp.sum(jnp.where(ii <= jj, g, 0.0), axis=0, keepdims=True), g_col)
    dm = _each(lambda a, b: jnp.where(incl, jnp.exp(jnp.where(incl, a - b, 0.0)), 0.0), gc_col, gc_row)
    gl = _each(_rowsum, g_rows)
    eg = _each(jnp.exp, gc_col)
    ek = _each(lambda a, b: jnp.exp(a - b), gl, gc_col)
    egl = _each(jnp.exp, gl)
    kb = _each(jnp.multiply, kn, beta)
    kk = _each(_dot_nt, kb, kn)
    qk = _each(_dot_nt, qn, kn)
    m = _each(lambda a, b: jnp.where(ii > jj, a * b, 0.0), kk, dm)
    attn = _each(jnp.multiply, qk, dm)
    return dict(ii=ii, jj=jj, beta=beta, rq=rq, rk=rk, yq=yq, kn=kn, qn=qn, dm=dm, eg=eg, ek=ek,
                egl=egl, kb=kb, m=m, attn=attn)


def _unit_lower_inverse_minus_identity(ms, ii, jj):
    pair = (ii >> 1) == (jj >> 1)
    ys = _each(lambda m: -jnp.where(pair, m, 0.0), ms)
    s = 1
    while (1 << s) < CHUNK:
        mask = ((ii >> (s + 1)) == (jj >> (s + 1))) & ((ii >> s) != (jj >> s))
        lbs = _each(lambda m: jnp.where(mask, m, 0.0), ms)
        zs = _each(lambda y, lb: lb + _dot(y, lb), ys, lbs)
        ys = _each(lambda y, z: y - z - _dot(z, y), ys, zs)
        s += 1
    return ys


def _dn_fwd(qkv_act, bg, bgt):
    t = qkv_act.shape[0]
    c = CHUNK
    per = DN_FWD_CHUNKS if t % (DN_FWD_CHUNKS * c) == 0 else 1
    nt = t // c
    hs = list(range(HEADS))
    entries = [(s_, h) for s_ in range(per) for h in hs]
    qo = [slice(h * HEAD_DIM, (h + 1) * HEAD_DIM) for h in hs]
    ko = [slice(DN_WIDTH + h * HEAD_DIM, DN_WIDTH + (h + 1) * HEAD_DIM) for h in hs]
    vo = [slice(2 * DN_WIDTH + h * HEAD_DIM, 2 * DN_WIDTH + (h + 1) * HEAD_DIM) for h in hs]

    def body(qkv_ref, bg_ref, bgt_ref, o_ref, u_ref, w_ref, qg_ref, kg_ref, attn_ref, y_ref, vn_ref, st_ref, egl_ref,
             s_ref):
        @pl.when(pl.program_id(0) == 0)
        def _():
            s_ref[...] = jnp.zeros_like(s_ref)

        rows = [pl.ds(s_ * c, c) for s_ in range(per)]
        qs = [qkv_ref[rows[s_], qo[h]] for s_, h in entries]
        ks = [qkv_ref[rows[s_], ko[h]] for s_, h in entries]
        vs = [qkv_ref[rows[s_], vo[h]] for s_, h in entries]
        bgvs = [bg_ref[rows[s_], :] for s_, _ in entries]
        g_rows = [bgt_ref[pl.ds(HEADS + h, 1), rows[s_]] for s_, h in entries]
        ct = _chunk_terms(qs, ks, bgvs, g_rows, [h for _, h in entries])
        ys = _unit_lower_inverse_minus_identity(ct["m"], ct["ii"], ct["jj"])
        vb = _each(jnp.multiply, vs, ct["beta"])
        kbe = _each(jnp.multiply, ct["kb"], ct["eg"])
        us = _each(lambda a, y: a + _dot(y, a), vb, ys)
        ws = _each(lambda a, y: _mx(a + _dot(y, a)), kbe, ys)
        qg = _each(lambda a, b: _mx(a * b), ct["qn"], ct["eg"])
        kg = _each(lambda a, b: _mx(a * b), ct["kn"], ct["ek"])
        attn = _each(_mx, ct["attn"])
        for e, (s_, h) in enumerate(entries):
            u_ref[rows[s_], qo[h]] = us[e]
            w_ref[rows[s_], qo[h]] = ws[e]
            qg_ref[rows[s_], qo[h]] = qg[e]
            kg_ref[rows[s_], qo[h]] = kg[e]
            attn_ref[rows[s_], qo[h]] = attn[e]
            y_ref[rows[s_], qo[h]] = _mx(ys[e])
            egl_ref[s_, h:h + 1, :] = jnp.broadcast_to(ct["egl"][e], (1, HEAD_DIM))
        ss = [s_ref[h] for h in hs]
        for s_ in range(per):
            pick = lambda xs: xs[s_ * HEADS:(s_ + 1) * HEADS]
            sb = _each(_mx, ss)
            vn = _each(lambda a, b, st: a - _dot(b, st), pick(us), pick(ws), sb)
            vnb = _each(_mx, vn)
            oa = _each(_dot, pick(qg), sb)
            ob = _each(_dot, pick(attn), vnb)
            upd = _each(_dot_tn, pick(kg), vnb)
            for h in hs:
                st_ref[s_, h] = ss[h]
                vn_ref[rows[s_], qo[h]] = vnb[h]
                o_ref[rows[s_], qo[h]] = oa[h] + ob[h]
            ss = _each(lambda st, g, d: st * g + d, ss, pick(ct["egl"]), upd)
        for h in hs:
            s_ref[h] = ss[h]

    wide = _row(per * c, DN_WIDTH)
    return _pc(body, "dn_fwd", (nt // per,),
               [_row(per * c, QKV_WIDTH), _row(per * c, 128), pl.BlockSpec((2 * HEADS, per * c), lambda i: (0, i))],
               [wide] * 8 + [pl.BlockSpec((per, HEADS, HEAD_DIM, HEAD_DIM), lambda i: (i, 0, 0, 0)),
                             pl.BlockSpec((per, HEADS, HEAD_DIM), lambda i: (i, 0, 0))],
               [SDS((t, DN_WIDTH), F32), SDS((t, DN_WIDTH), F32)] + [SDS((t, DN_WIDTH), MXU_DTYPE)] * 6
               + [SDS((nt, HEADS, HEAD_DIM, HEAD_DIM), F32), SDS((nt, HEADS, HEAD_DIM), F32)],
               scratch=[pltpu.VMEM((HEADS, HEAD_DIM, HEAD_DIM), F32)], sem=("arbitrary",))(qkv_act, bg, bgt)


def _dn_bwd(do, qkv_act, bg, bgt, u, w, qg, kg, attn, ymat, vn, states, egl):
    t = do.shape[0]
    c = CHUNK
    per = DN_BWD_CHUNKS if t % (DN_BWD_CHUNKS * c) == 0 else 1
    nt = t // c
    hs = list(range(HEADS))
    entries = [(s_, h) for s_ in range(per) for h in hs]
    qo = [slice(h * HEAD_DIM, (h + 1) * HEAD_DIM) for h in hs]
    ko = [slice(DN_WIDTH + h * HEAD_DIM, DN_WIDTH + (h + 1) * HEAD_DIM) for h in hs]
    vo = [slice(2 * DN_WIDTH + h * HEAD_DIM, 2 * DN_WIDTH + (h + 1) * HEAD_DIM) for h in hs]

    def body(do_ref, qkv_ref, bg_ref, bgt_ref, u_ref, w_ref, qg_ref, kg_ref, attn_ref, y_ref, vn_ref, st_ref, egl_ref,
             dqkv_ref, dbg_ref, ds_ref):
        @pl.when(pl.program_id(0) == 0)
        def _():
            ds_ref[...] = jnp.zeros_like(ds_ref)

        rows = [pl.ds(s_ * c, c) for s_ in range(per)]
        scan = {}
        dsp = [ds_ref[h] for h in hs]
        for s_ in reversed(range(per)):
            r = rows[s_]
            dsb = _each(_mx, dsp)
            ss = [st_ref[s_, h] for h in hs]
            sb = _each(_mx, ss)
            du_s = [_dot(kg_ref[r, sl], b) + _dot_tn(attn_ref[r, sl], do_ref[r, sl]) for sl, b in zip(qo, dsb)]
            dub = _each(_mx, du_s)
            scan[s_] = dict(
                du=du_s,
                dkg=[_dot_nt(vn_ref[r, sl], b) for sl, b in zip(qo, dsb)],
                dqg=[_dot_nt(do_ref[r, sl], b) for sl, b in zip(qo, sb)],
                dattn=[_dot_nt(do_ref[r, sl], vn_ref[r, sl]) for sl in qo],
                dw=[-_dot_nt(a, b) for a, b in zip(dub, sb)],
                degl=[jnp.sum(_rowsum(a * b), axis=0, keepdims=True) for a, b in zip(ss, dsp)])
            upd = [_dot_tn(qg_ref[r, sl], do_ref[r, sl]) - _dot_tn(w_ref[r, sl], a) for sl, a in zip(qo, dub)]
            dsp = [dsp[h] * egl_ref[s_, h:h + 1, :] + upd[h] for h in hs]
        for h in hs:
            ds_ref[h] = dsp[h]
        gather = lambda key: [scan[s_][key][h] for s_, h in entries]
        du, dkg_v, dqg_v, dattn_v, dwv, degl_v = (gather(k) for k in ("du", "dkg", "dqg", "dattn", "dw", "degl"))

        lane = _lane((c, 128))
        rowi = lax.broadcasted_iota(jnp.int32, (c, 1), 0)
        qs = [qkv_ref[rows[s_], qo[h]] for s_, h in entries]
        ks = [qkv_ref[rows[s_], ko[h]] for s_, h in entries]
        vs = [qkv_ref[rows[s_], vo[h]] for s_, h in entries]
        bgvs = [bg_ref[rows[s_], :] for s_, _ in entries]
        g_rows = [bgt_ref[pl.ds(HEADS + h, 1), rows[s_]] for s_, h in entries]
        ct = _chunk_terms(qs, ks, bgvs, g_rows, [h for _, h in entries])
        ii, jj = ct["ii"], ct["jj"]
        beta, eg, ek, kb, kn, qn, dm = ct["beta"], ct["eg"], ct["ek"], ct["kb"], ct["kn"], ct["qn"], ct["dm"]
        ys = [y_ref[rows[s_], qo[h]] for s_, h in entries]
        dvb = _each(lambda a, y: a + _dot_tn(y, a), du, ys)
        dkbe = _each(lambda a, y: a + _dot_tn(y, a), dwv, ys)
        dm_u = [_dot_nt(a, u_ref[rows[s_], qo[h]]) for a, (s_, h) in zip(dvb, entries)]
        dm_w = [_dot_nt(a, w_ref[rows[s_], qo[h]]) for a, (s_, h) in zip(dkbe, entries)]
        dms = _each(lambda a, b: jnp.where(ii > jj, -(a + b), 0.0), dm_u, dm_w)
        dkk = _each(jnp.multiply, dms, dm)
        dqk = _each(jnp.multiply, dattn_v, dm)
        gmat = _each(lambda a, b, c_, d: a * b + c_ * d, dms, ct["m"], dattn_v, ct["attn"])
        dkb = _each(lambda a, b, c_, d: _dot(a, b) + c_ * d, dkk, kn, dkbe, eg)
        dk1 = _each(_dot_tn, dkk, kb)
        dk2 = _each(_dot_tn, dqk, qn)
        dq1 = _each(_dot, dqk, kn)
        dk = _each(lambda a, b, c_, d: a + b + c_ * d, dk1, dk2, dkg_v, ek)
        dq = _each(lambda a, b, c_: a + b * c_, dq1, dqg_v, eg)
        deg = _each(lambda a, b, c_, d: _rowsum(a * b) + _rowsum(c_ * d), dqg_v, qn, dkbe, kb)
        dek = _each(lambda a, b: _rowsum(a * b), dkg_v, kn)
        dgl = _each(lambda a, b, c_, d: jnp.sum(a * b, axis=0, keepdims=True) + c_ * d, dek, ek, degl_v, ct["egl"])
        cs_row = _each(lambda g: jnp.sum(g, axis=0, keepdims=True), gmat)
        cs_col = _each(lambda r: _rowsum(jnp.where(ii == jj, r, 0.0)), cs_row)
        dgc = _each(lambda a, b, c_, d, g, e, f: a * b - c_ * d + _rowsum(g) - e + jnp.where(rowi == c - 1, f, 0.0),
                    deg, eg, dek, ek, gmat, cs_col, dgl)
        dgc_row = _each(lambda a: jnp.sum(jnp.where(ii == jj, a, 0.0), axis=0, keepdims=True), dgc)
        dg = _each(lambda r: _rowsum(jnp.where(jj >= ii, r, 0.0)), dgc_row)
        dbeta = _each(lambda a, b, c_, d: _rowsum(a * b) + _rowsum(c_ * d), dkb, kn, dvb, vs)
        dk = _each(lambda a, b, c_: a + b * c_, dk, dkb, beta)
        dbg = [jnp.zeros((c, 128), F32) for _ in range(per)]
        for e, (s_, h) in enumerate(entries):
            dyq = dq[e] * Q_SCALE
            yq = ct["yq"][e]
            dqkv_ref[rows[s_], qo[h]] = ct["rq"][e] * (dyq - yq * _rowsum(yq * dyq))
            dqkv_ref[rows[s_], ko[h]] = ct["rk"][e] * (dk[e] - kn[e] * _rowsum(kn[e] * dk[e]))
            dqkv_ref[rows[s_], vo[h]] = dvb[e] * beta[e]
            dbg[s_] = dbg[s_] + jnp.where(lane == h, dbeta[e], 0.0) + jnp.where(lane == HEADS + h, dg[e], 0.0)
        for s_ in range(per):
            dbg_ref[rows[s_], :] = dbg[s_]

    ns = nt // per
    rev = pl.BlockSpec((per * c, DN_WIDTH), lambda i: (ns - 1 - i, 0))
    return _pc(body, "dn_bwd", (ns,),
               [rev, pl.BlockSpec((per * c, QKV_WIDTH), lambda i: (ns - 1 - i, 0)),
                pl.BlockSpec((per * c, 128), lambda i: (ns - 1 - i, 0)),
                pl.BlockSpec((2 * HEADS, per * c), lambda i: (0, ns - 1 - i))]
               + [rev] * 7
               + [pl.BlockSpec((per, HEADS, HEAD_DIM, HEAD_DIM), lambda i: (ns - 1 - i, 0, 0, 0)),
                  pl.BlockSpec((per, HEADS, HEAD_DIM), lambda i: (ns - 1 - i, 0, 0))],
               [pl.BlockSpec((per * c, QKV_WIDTH), lambda i: (ns - 1 - i, 0)),
                pl.BlockSpec((per * c, 128), lambda i: (ns - 1 - i, 0))],
               [SDS((t, QKV_WIDTH), F32), SDS((t, 128), F32)],
               scratch=[pltpu.VMEM((HEADS, HEAD_DIM, HEAD_DIM), F32)],
               sem=("arbitrary",))(do, qkv_act, bg, bgt, u, w, qg, kg, attn, ymat, vn, states, egl)


MIX_ROWS = 64


def _mix_oproj_ln1(o, gates, ypre, pool_scale, wo_row, w_out, h0, g1, b1, tm):
    t = o.shape[0]

    def body(o_ref, z_ref, ga_ref, gb_ref, yp_ref, ps_ref, wo_ref, w_ref, h0_ref, g_ref, b_ref,
             mixed_ref, a1_ref, h1_ref, h1b_ref):
        for r in range(0, tm, MIX_ROWS):
            rows = pl.ds(r, MIX_ROWS)
            for h in range(HEADS):
                sl = slice(h * HEAD_DIM, (h + 1) * HEAD_DIM)
                oh = o_ref[rows, sl]
                on = oh * lax.rsqrt(jnp.mean(oh * oh, axis=1, keepdims=True) + RMS_EPS)
                zh = z_ref[rows, sl].astype(F32)
                yb = on * wo_ref[:, sl] * (zh * _sigmoid(zh))
                ya = yp_ref[rows, sl] * ps_ref[:, sl]
                mixed_ref[rows, sl] = _mx(_sigmoid(ga_ref[rows, sl].astype(F32)) * ya
                                          + _sigmoid(gb_ref[rows, sl].astype(F32)) * yb)
        a1 = ALPHA * h0_ref[...] + _dot(mixed_ref[...], w_ref[...])
        a1_ref[...] = a1
        xhat, _ = _ln_stats(a1)
        h1 = xhat * g_ref[...] + b_ref[...]
        h1_ref[...] = h1
        h1b_ref[...] = _mx(h1)

    def col(blk):
        return pl.BlockSpec((tm, D_MODEL), lambda i: (i, blk))

    r = _row(tm, D_MODEL)
    v = _const((1, D_MODEL))
    return _pc(body, "mix_oproj_ln1", (t // tm,),
               [r, col(0), col(1), col(2), r, v, v,
                _const((D_MODEL, D_MODEL)), r, v, v],
               [r, r, r, r],
               [SDS((t, D_MODEL), MXU_DTYPE), SDS((t, D_MODEL), F32), SDS((t, D_MODEL), F32),
                SDS((t, D_MODEL), MXU_DTYPE)],
               sem=("parallel",))(o, gates, gates, gates, ypre, pool_scale, wo_row, w_out, h0, g1, b1)


def _mix_bwd(da1_bf, w_out, o, gates, ypre, pool_scale, wo_row, tm, after):
    t = o.shape[0]

    def body(da_ref, wout_ref, o_ref, z_ref, ga_ref, gb_ref, yp_ref, ps_ref, wo_ref, after_ref,
             do_ref, dp_ref, dyp_ref, acc_ref, dm_ref):
        i = pl.program_id(0)

        @pl.when(i == 0)
        def _():
            acc_ref[...] = jnp.zeros_like(acc_ref)

        dm_ref[...] = _dot_nt(da_ref[...], wout_ref[...])
        dwo = jnp.zeros((1, HEAD_DIM), F32)
        for h in range(HEADS):
            sl = slice(h * HEAD_DIM, (h + 1) * HEAD_DIM)
            woh = wo_ref[:, sl]
            psh = ps_ref[:, sl]
            dps = jnp.zeros((1, HEAD_DIM), F32)
            for r in range(0, tm, MIX_ROWS):
                rows = pl.ds(r, MIX_ROWS)
                oh = o_ref[rows, sl]
                rs = lax.rsqrt(jnp.mean(oh * oh, axis=1, keepdims=True) + RMS_EPS)
                on = oh * rs
                zh = z_ref[rows, sl].astype(F32)
                sz = _sigmoid(zh)
                silu = zh * sz
                t1 = on * woh
                yb = t1 * silu
                sa = _sigmoid(ga_ref[rows, sl].astype(F32))
                sb = _sigmoid(gb_ref[rows, sl].astype(F32))
                yp = yp_ref[rows, sl]
                dm = dm_ref[rows, sl]
                ga_sl = slice(D_MODEL + h * HEAD_DIM, D_MODEL + (h + 1) * HEAD_DIM)
                gb_sl = slice(2 * D_MODEL + h * HEAD_DIM, 2 * D_MODEL + (h + 1) * HEAD_DIM)
                dp_ref[rows, ga_sl] = _mx(dm * (yp * psh) * sa * (1.0 - sa))
                dp_ref[rows, gb_sl] = _mx(dm * yb * sb * (1.0 - sb))
                dya = dm * sa
                dyb = dm * sb
                dyp_ref[rows, sl] = _mx(dya * psh)
                dps = dps + jnp.sum(dya * yp, axis=0, keepdims=True)
                dp_ref[rows, sl] = _mx(dyb * t1 * (sz * (1.0 + zh * (1.0 - sz))))
                dt1 = dyb * silu
                dwo = dwo + jnp.sum(dt1 * on, axis=0, keepdims=True)
                don = dt1 * woh
                do_ref[rows, sl] = _mx(rs * (don - on * jnp.mean(don * on, axis=1, keepdims=True)))
            acc_ref[0:1, sl] += dps
        acc_ref[1:2, 0:HEAD_DIM] += dwo

    def col(blk):
        return pl.BlockSpec((tm, D_MODEL), lambda i: (i, blk))

    r = _row(tm, D_MODEL)
    return _pc(body, "mix_bwd", (t // tm,),
               [r, _const((D_MODEL, D_MODEL)), r, col(0), col(1), col(2), r,
                _const((1, D_MODEL)), _const((1, D_MODEL)), ANY],
               [r, pl.BlockSpec((tm, 3 * D_MODEL), lambda i: (i, K_Z // (3 * D_MODEL))), r, _const((8, D_MODEL))],
               [SDS((t, D_MODEL), MXU_DTYPE), SDS((t, CAT_WIDTH), MXU_DTYPE), SDS((t, D_MODEL), MXU_DTYPE),
                SDS((8, D_MODEL), F32)],
               scratch=[pltpu.VMEM((tm, D_MODEL), F32)],
               sem=("arbitrary",))(da1_bf, w_out, o, gates, gates, gates, ypre, pool_scale, wo_row, after)


def _mlp_up(h1_bf, w_up, tm):
    t = h1_bf.shape[0]
    tn = w_up.shape[2]

    def body(h_ref, w_ref, act_ref):
        r = jnp.maximum(_dot(h_ref[...], w_ref[...]), 0.0)
        act_ref[...] = _mx(r * r)

    return _pc(body, "mlp_up", (D_FF // tn, t // tm),
               [pl.BlockSpec((tm, D_MODEL), lambda j, i: (i, 0)),
                pl.BlockSpec((None, D_MODEL, tn), lambda j, i: (j, 0, 0))],
               pl.BlockSpec((tm, tn), lambda j, i: (i, j)), SDS((t, D_FF), MXU_DTYPE),
               sem=("parallel", "parallel"))(h1_bf, w_up)


def _tail(act, w_down, h1, w_gate, p, w_proj, tgt, g2, b2, tm):
    t = act.shape[0]

    def body(act_ref, wd_ref, h1_ref, wg_ref, p_ref, wp_ref, tgt_ref, g_ref, b_ref,
             dr_ref, drb_ref, dgp_ref, dpp_ref, rb_ref, acc_ref):
        i = pl.program_id(0)

        @pl.when(i == 0)
        def _():
            acc_ref[...] = jnp.zeros_like(acc_ref)

        r = ALPHA * h1_ref[...] + _dot(act_ref[...], wd_ref[...])
        rb = _mx(r)
        rb_ref[...] = rb
        gate = _sigmoid(_dot(rb, wg_ref[...]))
        pp = _dot(p_ref[...], wp_ref[...])
        xhat, rstd = _ln_stats(r + gate * pp)
        g = g_ref[...]
        diff = xhat * g + b_ref[...] - tgt_ref[...]
        dh2 = diff * (1.0 / D_MODEL)
        rowloss = jnp.sum(diff * diff, axis=1, keepdims=True) * (0.5 / D_MODEL)
        acc_ref[0:1, :] += jnp.sum(dh2 * xhat, axis=0, keepdims=True)
        acc_ref[1:2, :] += jnp.sum(dh2, axis=0, keepdims=True)
        acc_ref[2:3, :] += jnp.broadcast_to(jnp.sum(rowloss, axis=0, keepdims=True), (1, D_MODEL))
        da2 = _ln_bwd(dh2, xhat, rstd, g)
        dpp_ref[...] = _mx(da2 * gate)
        dgp = _mx(da2 * pp * gate * (1.0 - gate))
        dgp_ref[...] = dgp
        dr = da2 + _dot_nt(dgp, wg_ref[...])
        dr_ref[...] = dr
        drb_ref[...] = _mx(dr)

    r = _row(tm, D_MODEL)
    v = _const((1, D_MODEL))
    return _pc(body, "tail", (t // tm,),
               [_row(tm, D_FF), _const((D_FF, D_MODEL)), r, _const((D_MODEL, D_MODEL)), _row(tm, PLE_DIM),
                _const((PLE_DIM, D_MODEL)), r, v, v],
               [r, r, r, r, r, _const((8, D_MODEL))],
               [SDS((t, D_MODEL), F32)] + [SDS((t, D_MODEL), MXU_DTYPE)] * 4 + [SDS((8, D_MODEL), F32)],
               sem=("arbitrary",))(act, w_down, h1, w_gate, p, w_proj, tgt, g2, b2)


SQRT_GUARD = 1e-30


def _mlp_bwd1(dr_bf, w_down, act, tm, tn):
    t = act.shape[0]

    def body(dr_ref, w_ref, act_ref, dup_ref):
        dact = _dot_nt(dr_ref[...], w_ref[...])
        a = act_ref[...].astype(F32)
        dup_ref[...] = _mx(dact * (2.0 * a * lax.rsqrt(a + SQRT_GUARD)))

    o = pl.BlockSpec((tm, tn), lambda j, i: (i, j))
    return _pc(body, "mlp_bwd1", (D_FF // tn, t // tm),
               [pl.BlockSpec((tm, D_MODEL), lambda j, i: (i, 0)), pl.BlockSpec((tn, D_MODEL), lambda j, i: (j, 0)), o],
               o, SDS((t, D_FF), MXU_DTYPE), sem=("parallel", "parallel"))(dr_bf, w_down, act)


def _mlp_bwd2(dup, w_up, dr, a1, g1, tm):
    t = dr.shape[0]

    nk, tk = w_up.shape[0], w_up.shape[2]

    def body(dup_ref, w_ref, dr_ref, a1_ref, g_ref, da1_ref, da1b_ref, acc_ref):
        i = pl.program_id(0)

        @pl.when(i == 0)
        def _():
            acc_ref[...] = jnp.zeros_like(acc_ref)

        dh1 = ALPHA * dr_ref[...]
        for kk in range(nk):
            dh1 = dh1 + _dot_nt(dup_ref[:, kk * tk:(kk + 1) * tk], w_ref[kk])
        xhat, rstd = _ln_stats(a1_ref[...])
        acc_ref[0:1, :] += jnp.sum(dh1 * xhat, axis=0, keepdims=True)
        acc_ref[1:2, :] += jnp.sum(dh1, axis=0, keepdims=True)
        da1 = _ln_bwd(dh1, xhat, rstd, g_ref[...])
        da1_ref[...] = da1
        da1b_ref[...] = _mx(da1)

    r = _row(tm, D_MODEL)
    return _pc(body, "mlp_bwd2", (t // tm,),
               [_row(tm, D_FF), _const((nk, D_MODEL, tk)), r, r, _const((1, D_MODEL))],
               [r, r, _const((8, D_MODEL))],
               [SDS((t, D_MODEL), F32), SDS((t, D_MODEL), MXU_DTYPE), SDS((8, D_MODEL), F32)],
               sem=("arbitrary",))(dup, w_up, dr, a1, g1)


def _ln_in_bwd(dproj, w_cat, da1, x, g, tm, after):
    t = x.shape[0]

    def body(dp_ref, w_ref, da1_ref, x_ref, g_ref, after_ref, dx_ref, acc_ref):
        i = pl.program_id(0)

        @pl.when(i == 0)
        def _():
            acc_ref[...] = jnp.zeros_like(acc_ref)

        dh0 = _dot_nt(dp_ref[...], w_ref[...]) + ALPHA * da1_ref[...]
        xhat, rstd = _ln_stats(x_ref[...])
        acc_ref[0:1, :] += jnp.sum(dh0 * xhat, axis=0, keepdims=True)
        acc_ref[1:2, :] += jnp.sum(dh0, axis=0, keepdims=True)
        dx_ref[...] = _ln_bwd(dh0, xhat, rstd, g_ref[...])

    r = _row(tm, D_MODEL)
    return _pc(body, "ln_in_bwd", (t // tm,),
               [_row(tm, CAT_WIDTH), _const((D_MODEL, CAT_WIDTH)), r, r, _const((1, D_MODEL)), ANY],
               [r, _const((8, D_MODEL))], [SDS((t, D_MODEL), F32), SDS((8, D_MODEL), F32)],
               sem=("arbitrary",))(dproj, w_cat, da1, x, g, after)


def _local_step(x, p, tgt, wts, start_token, first_weights, late_weights, send_late_grads, send_early_grads):
    t = x.shape[0]
    tm = min(512, t)
    tms = min(256, t)
    row = lambda a: a.reshape(1, -1)
    pool_scale = row(wts["pool_scale"])
    wo_row = jnp.tile(row(wts["o_norm_w"]), (1, HEADS))
    pad8 = jnp.zeros((1, HEADS), F32)
    al_row = jnp.concatenate([pad8, row(wts["a_log"]), jnp.zeros((1, 128 - 2 * HEADS), F32)], axis=1)
    dtb_row = jnp.concatenate([pad8, row(wts["dt_bias"]), jnp.zeros((1, 128 - 2 * HEADS), F32)], axis=1)
    g_in, b_in = row(wts["ln_in_g"]), row(wts["ln_in_b"])
    g1, b1 = row(wts["ln1_g"]), row(wts["ln1_b"])
    g2, b2 = row(wts["ln2_g"]), row(wts["ln2_b"])

    h0, h0_bf = _ln_in(x, g_in, b_in, tm, start_token)
    first, first_token = first_weights(h0_bf)
    wts = {**wts, **first}
    w_cat = wts["w_cat"]
    proj, gates, qkv_act, dsilu = _proj_conv(h0_bf, w_cat, wts["conv_w"], tms, first_token)
    ypre, d_bf = _pool_fwd(proj, wts["pool_w"], tm)
    bg = _ba_fwd(proj, al_row, dtb_row, tm)
    bgt = bg[:, :2 * HEADS].T
    o, u, w, qg, kg, attn, ymat, vn, states, egl = _dn_fwd(qkv_act, bg, bgt)
    wts = {**wts, **late_weights(o)}
    mixed, a1, h1, h1_bf = _mix_oproj_ln1(o, gates, ypre, pool_scale, wo_row, wts["w_out"], h0, g1, b1, tm)
    act = _mlp_up(h1_bf, wts["w_up"], tm)
    dr, dr_bf, dgp, dpp, r_bf, acc_tail = _tail(act, wts["w_down"], h1, wts["ple_gate_w"], p, wts["ple_proj_w"],
                                                tgt, g2, b2, tms)
    grads = {}
    grads["ple_proj_w"] = _matmul(p, dpp, "tn", "dw_ple_proj", WIRE_DTYPE, tm=256, tn=1024, tk=DW_TK)
    grads["ple_gate_w"] = _matmul(r_bf, dgp, "tn", "dw_ple_gate", WIRE_DTYPE, tm=DW_TM, tn=1024, tk=DW_TK)
    grads["w_down"] = _matmul(act, dr_bf, "tn", "dw_down", WIRE_DTYPE, tm=DW_TM, tn=1024, tk=DW_TK)
    dup = _mlp_bwd1(dr_bf, wts["w_down"], act, tm, 1024)
    grads["w_up"] = _matmul(h1_bf, dup, "tn", "dw_up", WIRE_DTYPE, tm=DW_TM, tn=1024, tk=DW_TK, stack_out=True)
    da1, da1_bf, acc_ln1 = _mlp_bwd2(dup, wts["w_up"], dr, a1, g1, tms)
    grads["w_out"] = _matmul(mixed, da1_bf, "tn", "dw_out", WIRE_DTYPE, tm=DW_TM, tn=1024, tk=DW_TK)
    sent = send_late_grads(grads)
    do, dproj, dyp, acc_mix = _mix_bwd(da1_bf, wts["w_out"], o, gates, ypre, pool_scale, wo_row, tm, sent)
    dproj, grads["pool_w"] = _pool_bwd(dyp, d_bf, wts["pool_w"], dproj, tm)
    dqkv_act, dbg = _dn_bwd(do, qkv_act, bg, bgt, u, w, qg, kg, attn, ymat, vn, states, egl)
    dproj, acc_conv = _conv_bwd(dqkv_act, dsilu, proj, wts["conv_w"], dproj, tm)
    dproj, acc_ba = _ba_bwd(dbg, bg, proj, al_row, dtb_row, dproj, tm)
    dw_cat = _matmul(h0_bf, dproj, "tn", "dw_in", WIRE_DTYPE, tm=DW_TM, tn=1152, tk=DW_TK)
    grads["w_in"] = _w_in_by_chip(dw_cat)
    sent = send_early_grads(grads)
    grad_x, acc_in = _ln_in_bwd(dproj, w_cat, da1, x, g_in, tms, sent)

    grads["conv_w"] = acc_conv[0:CONV_K]
    grads["ln_in_g"], grads["ln_in_b"] = acc_in[0], acc_in[1]
    grads["ln1_g"], grads["ln1_b"] = acc_ln1[0], acc_ln1[1]
    grads["ln2_g"], grads["ln2_b"] = acc_tail[0], acc_tail[1]
    grads["pool_scale"] = acc_mix[0]
    grads["o_norm_w"] = acc_mix[1, 0:HEAD_DIM]
    grads["a_log"] = acc_ba[0, HEADS:2 * HEADS]
    grads["dt_bias"] = acc_ba[1, HEADS:2 * HEADS]
    loss = acc_tail[2, 0]
    return grad_x, grads, loss


MESH = pl.DeviceIdType.MESH
ANY = pl.BlockSpec(memory_space=pl.ANY)


def _chip_of(k, x, y):
    chip = (2 * x + y + k) % N_CHIPS
    return chip // 2, chip % 2


def _place():
    x, y, c = lax.axis_index("x"), lax.axis_index("y"), lax.axis_index("c")
    return x, y, c, 2 * x + y


def _half(rows, c):
    return pl.ds(pl.multiple_of(c * (rows // 2), 16), rows // 2)


def _remote(src, dst, send_sem, recv_sem, device_id):
    return pltpu.make_async_remote_copy(src_ref=src, dst_ref=dst, send_sem=send_sem, recv_sem=recv_sem,
                                        device_id=device_id, device_id_type=MESH)


def _tile_rows(rows):
    for tr in (256, 128, 64, 32, 16):
        if rows % tr == 0:
            return tr
    raise ValueError(rows)


def _first_gather_copies(srcs, lands, send, recv, place):
    copies = []
    for a in range(len(srcs)):
        whole = a == len(srcs) - 1
        for k in range(N_CHIPS):
            if place is None:
                copies.append(None)
                continue
            x, y, c, me = place
            sems = (send.at[a * N_CHIPS + k], recv.at[a * N_CHIPS + k])
            if k == 0:
                copies.append(_remote(srcs[a], lands[a].at[me], *sems, (x, y, 1 - c)))
                continue
            tx, ty = _chip_of(k, x, y)
            if whole:
                copies.append(_remote(srcs[a], lands[a].at[me], *sems, (tx, ty, c)))
            else:
                mine = _half(srcs[a].shape[0], c)
                copies.append(_remote(srcs[a].at[mine], lands[a].at[me, mine], *sems, (tx, ty, c)))
    return copies


def _pass_halves(stacks):
    n = len(stacks)

    def body(*refs):
        outs = refs[n:2 * n]
        send, recv = refs[2 * n:]
        x, y, c, me = _place()
        copies = []
        for a in range(n):
            for k in range(1, N_CHIPS):
                landed = outs[a].at[(me + N_CHIPS - k) % N_CHIPS, _half(stacks[a].shape[1], c)]
                copies.append(_remote(landed, landed, send.at[a * N_CHIPS + k], recv.at[a * N_CHIPS + k],
                                      (x, y, 1 - c)))
        for cp in copies:
            cp.start()
        for cp in copies:
            cp.wait_send()
        for a in range(n):
            for k in range(1, N_CHIPS):
                passed = outs[a].at[(me + N_CHIPS - k) % N_CHIPS, _half(stacks[a].shape[1], 1 - c)]
                _remote(passed, passed, send.at[a * N_CHIPS + k], recv.at[a * N_CHIPS + k], (x, y, c)).wait_recv()

    sems = pltpu.SemaphoreType.DMA((n * N_CHIPS,))
    return pl.pallas_call(
        body, name="pass_halves", out_shape=[SDS(s.shape, s.dtype) for s in stacks],
        in_specs=[ANY] * n, out_specs=[ANY] * n, scratch_shapes=[sems, sems],
        input_output_aliases={a: a for a in range(n)},
    )(*stacks)


def _swap_halves(gs):
    n = len(gs)

    def body(*refs):
        ins, theirs = refs[0:n], refs[n:2 * n]
        send, recv = refs[2 * n:]
        x, y, c, _ = _place()
        copies = [_remote(ins[a].at[:, _half(gs[a].shape[1], 1 - c)], theirs[a], send.at[a], recv.at[a],
                          (x, y, 1 - c)) for a in range(n)]
        for cp in copies:
            cp.start()
        for cp in copies:
            cp.wait()

    return pl.pallas_call(
        body, name="swap_halves", out_shape=[SDS((N_CHIPS, g.shape[1] // 2, g.shape[2]), g.dtype) for g in gs],
        in_specs=[ANY] * n, out_specs=[ANY] * n, scratch_shapes=[pltpu.SemaphoreType.DMA((n,))] * 2,
    )(*gs)


def _send_to_sibling(hs):
    n = len(hs)

    def body(*refs):
        ins, outs = refs[0:n], refs[n:2 * n]
        send, recv = refs[2 * n:]
        x, y, c, _ = _place()
        copies = [_remote(ins[a], outs[a], send.at[a], recv.at[a], (x, y, 1 - c)) for a in range(n)]
        for cp in copies:
            cp.start()
        for cp in copies:
            cp.wait()

    return pl.pallas_call(
        body, name="send_to_sibling", out_shape=[SDS(h.shape, h.dtype) for h in hs],
        in_specs=[ANY] * n, out_specs=[ANY] * n, scratch_shapes=[pltpu.SemaphoreType.DMA((n,))] * 2,
    )(*hs)


HBM = pl.BlockSpec(memory_space=pltpu.HBM)
SEM = pl.BlockSpec(memory_space=pltpu.SEMAPHORE)
EFFECT = pltpu.SideEffectType.DATAFLOW_SIDE_EFFECTING


def _in_hbm(a):
    return pltpu.with_memory_space_constraint(a, pltpu.HBM)


def _split_copy_start(name, srcs, lands, copies_of, after):
    n = len(srcs)
    n_copies = len(copies_of(srcs, lands, None, None, None))

    def body(*refs):
        src_refs, land_refs = refs[0:n], refs[n:2 * n]
        send, recv = refs[2 * n + 1], refs[2 * n + 2]
        token = refs[-1]
        for cp in copies_of(src_refs, land_refs, send, recv, _place()):
            cp.start()
        token[...] = jnp.zeros_like(token)

    sems = pltpu.SemaphoreType.DMA((n_copies,))
    out = pl.pallas_call(
        body, name=name,
        out_shape=[sems, sems] + [pltpu.HBM(a.shape, a.dtype) for a in list(srcs) + list(lands)] + [SDS((8, 128), F32)],
        in_specs=[HBM] * (2 * n) + [ANY],
        out_specs=[SEM, SEM] + [HBM] * (2 * n) + [pl.BlockSpec(memory_space=pltpu.VMEM)],
        input_output_aliases={i: 2 + i for i in range(2 * n)},
        compiler_params=pltpu.CompilerParams(has_side_effects=EFFECT),
    )(*[_in_hbm(a) for a in list(srcs) + list(lands)], after)
    return out[0], out[1], out[2:2 + n], out[2 + n:2 + 2 * n], out[-1]


def _split_copy_wait(name, send, recv, srcs, lands, after, copies_of):
    n = len(srcs)
    after = list(after) if isinstance(after, (list, tuple)) else [after]

    def body(*refs):
        src_refs, land_refs = refs[0:n], refs[n:2 * n]
        send_ref, recv_ref = refs[2 * n], refs[2 * n + 1]
        for cp in copies_of(src_refs, land_refs, send_ref, recv_ref, _place()):
            cp.wait_send()
            cp.wait_recv()

    out = pl.pallas_call(
        body, name=name, out_shape=[pltpu.HBM(a.shape, a.dtype) for a in list(srcs) + list(lands)],
        in_specs=[HBM] * (2 * n) + [SEM, SEM] + [ANY] * len(after), out_specs=[HBM] * (2 * n),
        input_output_aliases={i: i for i in range(2 * n)},
        compiler_params=pltpu.CompilerParams(has_side_effects=EFFECT),
    )(*srcs, *lands, send, recv, *after)
    return out[0:n], out[n:2 * n]


def _late_gather_copies(srcs, lands, send, recv, place):
    copies = []
    for a in range(len(srcs)):
        for k in range(N_CHIPS):
            if place is None:
                copies.append(None)
                continue
            x, y, c, me = place
            if k == 0:
                target = (x, y, 1 - c)
            else:
                tx, ty = _chip_of(k, x, y)
                target = (tx, ty, c)
            copies.append(_remote(srcs[a], lands[a].at[me], send.at[a * N_CHIPS + k], recv.at[a * N_CHIPS + k], target))
    return copies


def _late_scatter_copies(srcs, lands, send, recv, place):
    copies = []
    for a in range(len(srcs)):
        for k in range(1, N_CHIPS):
            if place is None:
                copies.append(None)
                continue
            x, y, c, _ = place
            tx, ty = _chip_of(k, x, y)
            copies.append(_remote(srcs[a].at[2 * tx + ty], lands[a].at[k - 1], send.at[a * (N_CHIPS - 1) + k - 1],
                                  recv.at[a * (N_CHIPS - 1) + k - 1], (tx, ty, c)))
    return copies


def _add_pair(g, theirs, name):
    _, rows, cols = g.shape
    half = rows // 2
    tr = _tile_rows(half)

    def body(g_ref, t_ref, o_ref):
        own = g_ref[lax.axis_index("c")]
        o_ref[...] = (own.astype(F32) + t_ref[...].astype(F32)).astype(o_ref.dtype)

    blk = pl.BlockSpec((None, tr, cols), lambda j, i: (j, i, 0))
    return _pc(body, "add_" + name, (N_CHIPS, half // tr),
               [pl.BlockSpec((None, 2, tr, cols), lambda j, i: (j, 0, i, 0)), blk], blk,
               SDS((N_CHIPS, half, cols), g.dtype), sem=("parallel", "parallel"))(
                   g.reshape(N_CHIPS, 2, half, cols), theirs)


def _sum_slabs(pair, landed, name):
    _, rows, cols = pair.shape
    tr = _tile_rows(rows)

    def body(p_ref, r_ref, o_ref):
        acc = p_ref[2 * lax.axis_index("x") + lax.axis_index("y")].astype(F32)
        for k in range(N_CHIPS - 1):
            acc = acc + r_ref[k].astype(F32)
        o_ref[...] = acc

    return _pc(body, "sum_" + name, (rows // tr,),
               [pl.BlockSpec((N_CHIPS, tr, cols), lambda i: (0, i, 0)),
                pl.BlockSpec((N_CHIPS - 1, tr, cols), lambda i: (0, i, 0))],
               _row(tr, cols), SDS((rows, cols), F32), sem=("parallel",))(pair, landed)


def _adamw_math(w, g, m, v):
    m = ADAM_B1 * m + (1.0 - ADAM_B1) * g
    v = ADAM_B2 * v + (1.0 - ADAM_B2) * (g * g)
    m_hat = m / (1.0 - ADAM_B1 ** ADAM_STEP)
    v_hat = v / (1.0 - ADAM_B2 ** ADAM_STEP)
    delta = -ADAM_LR * (m_hat / (jnp.sqrt(v_hat) + ADAM_EPS) + ADAM_WD * w)
    return delta, m, v


def _adamw_2d(w, g_own, g_sib, m, v, name, halves):
    lead = w.ndim == 3
    rows, cols = w.shape[-2:]
    tr = _tile_rows(rows // 2)
    nh = rows // 2 // tr if halves else rows // tr

    def body(w_ref, go_ref, gs_ref, m_ref, v_ref, g_out, d_out, m_out, v_out):
        if halves:
            mine = (pl.program_id(0) // nh) == lax.axis_index("c")
            g = jnp.where(mine, go_ref[...], gs_ref[...])
        else:
            g = go_ref[...] + gs_ref[...]
        delta, mn, vn = _adamw_math(w_ref[...], g, m_ref[...], v_ref[...])
        g_out[...] = g
        d_out[...] = delta
        m_out[...] = mn
        v_out[...] = vn

    r = _row(tr, cols)
    p = pl.BlockSpec((None, tr, cols), lambda i: (0, i, 0)) if lead else r
    h = pl.BlockSpec((tr, cols), lambda i: (i % nh, 0))
    return _pc(body, "adamw_" + name, (rows // tr,), [p, h, h, p, p], [r] * 4, [SDS((rows, cols), F32)] * 4,
               sem=("parallel",))(w, g_own, g_sib, m, v)


def _small_allreduce_adamw(mine, w, m, v, sizes):
    shape = mine.shape
    n = len(sizes)

    def body(mine_ref, w_ref, m_ref, v_ref, *rest):
        outs, (buf_ref, res_ref, send_sems, recv_sems) = rest[:-4], rest[-4:]
        x, y, c = lax.axis_index("x"), lax.axis_index("y"), lax.axis_index("c")
        me = 4 * x + 2 * y + c
        buf_ref[me] = mine_ref[...]
        copies = []
        for k in range(1, N_DEV):
            tgt = (me + k) % N_DEV
            copies.append(pltpu.make_async_remote_copy(
                src_ref=mine_ref, dst_ref=buf_ref.at[me], send_sem=send_sems.at[k], recv_sem=recv_sems.at[k],
                device_id=(tgt // 4, (tgt // 2) % 2, tgt % 2), device_id_type=MESH))
        for cp in copies:
            cp.start()
        for k in range(1, N_DEV):
            src = (me + N_DEV - k) % N_DEV
            pltpu.make_async_remote_copy(
                src_ref=mine_ref, dst_ref=buf_ref.at[src], send_sem=send_sems.at[k], recv_sem=recv_sems.at[k],
                device_id=(x, y, c), device_id_type=MESH).wait_recv()
        for cp in copies:
            cp.wait_send()
        g = buf_ref[0]
        for j in range(1, N_DEV):
            g = g + buf_ref[j]
        delta, mn, vn = _adamw_math(w_ref[...], g, m_ref[...], v_ref[...])
        for kind, val in enumerate((g, delta, mn, vn)):
            res_ref[kind] = val
            for i, size in enumerate(sizes):
                outs[kind * (n + 1) + i][...] = res_ref[kind, i:i + 1, 0:size]
            outs[kind * (n + 1) + n][...] = res_ref[kind, SMALL_CONV_AT:SMALL_CONV_AT + SMALL_CONV_ROWS, :]
        outs[-1][...] = res_ref[0, n:n + 1, 0:1]

    vm = pl.BlockSpec(memory_space=pltpu.VMEM)
    per_kind = [SDS((1, size), F32) for size in sizes] + [SDS((SMALL_CONV_ROWS, D_MODEL), F32)]
    out_shape = per_kind * 4 + [SDS((1, 1), F32)]
    out = pl.pallas_call(
        body, name="small_allreduce_adamw", out_shape=out_shape, in_specs=[vm] * 4, out_specs=[vm] * len(out_shape),
        scratch_shapes=[pltpu.VMEM((N_DEV,) + shape, F32), pltpu.VMEM((4,) + shape, F32),
                        pltpu.SemaphoreType.DMA((N_DEV,)), pltpu.SemaphoreType.DMA((N_DEV,))],
    )(mine, w, m, v)
    return [out[kind * (n + 1):(kind + 1) * (n + 1)] for kind in range(4)], out[-1]


def _as2d(a):
    return a.reshape(-1, a.shape[-1])


SEGMENTS = ((C_POOL, K_U, POOL_WIDTH), (C_QKV, K_QKV, QKV_WIDTH), (C_Z, K_Z, DN_WIDTH), (C_BETA, K_BA, 2 * HEADS),
            (C_GA, K_GA, D_MODEL), (C_GB, K_GB, D_MODEL))
SHARD_COLS = IN_WIDTH // N_CHIPS


def _w_cat(stack):
    pieces = []
    for c0, _, width in sorted(SEGMENTS, key=lambda seg: seg[1]):
        a = c0
        while a < c0 + width:
            chip = a // SHARD_COLS
            b = min(c0 + width, (chip + 1) * SHARD_COLS)
            pieces.append(stack[chip][:, a - chip * SHARD_COLS:b - chip * SHARD_COLS])
            a = b
    pieces.append(jnp.zeros((D_MODEL, CAT_WIDTH - K_BA - 2 * HEADS), stack.dtype))
    return jnp.concatenate(pieces, axis=1)


def _w_in_by_chip(dw_cat):
    slabs = []
    for chip in range(N_CHIPS):
        lo, hi = chip * SHARD_COLS, (chip + 1) * SHARD_COLS
        pieces = []
        for c0, k0, width in sorted(SEGMENTS):
            a, b = max(c0, lo), min(c0 + width, hi)
            if a < b:
                pieces.append(dw_cat[:, k0 + a - c0:k0 + b - c0])
        slabs.append(jnp.concatenate(pieces, axis=1))
    return jnp.stack(slabs)


WEIGHT_LAYOUT = {
    "w_in": lambda s: ("w_cat", _w_cat(s)),
    "pool_w": lambda s: ("pool_w", s.reshape(N_CHIPS, 4, POOL_GROUP, POOL_OUT_GROUP // N_CHIPS)
                         .transpose(1, 2, 0, 3).reshape(4, POOL_GROUP, POOL_OUT_GROUP)),
    "w_out": lambda s: ("w_out", s.reshape(D_MODEL, D_MODEL)),
    "w_up": lambda s: ("w_up", s),
    "w_down": lambda s: ("w_down", s.reshape(D_FF, D_MODEL)),
    "ple_gate_w": lambda s: ("ple_gate_w", s.reshape(D_MODEL, D_MODEL)),
    "ple_proj_w": lambda s: ("ple_proj_w", s.transpose(1, 0, 2).reshape(PLE_DIM, D_MODEL)),
}

GRAD_LAYOUT = {
    "w_in": lambda g: g,
    "pool_w": lambda g: g.reshape(4, POOL_GROUP, N_CHIPS, POOL_OUT_GROUP // N_CHIPS)
                         .transpose(2, 0, 1, 3).reshape(N_CHIPS, 4 * POOL_GROUP, POOL_OUT_GROUP // N_CHIPS),
    "w_out": lambda g: g.reshape(N_CHIPS, D_MODEL // N_CHIPS, D_MODEL),
    "w_up": lambda g: g,
    "w_down": lambda g: g.reshape(N_CHIPS, D_FF // N_CHIPS, D_MODEL),
    "ple_gate_w": lambda g: g.reshape(N_CHIPS, D_MODEL // N_CHIPS, D_MODEL),
    "ple_proj_w": lambda g: g.reshape(PLE_DIM, N_CHIPS, D_MODEL // N_CHIPS).transpose(1, 0, 2),
}


def _full_weights(names, stacks):
    return dict(WEIGHT_LAYOUT[n](s.astype(MXU_DTYPE)) for n, s in zip(names, stacks))


def _grads_by_chip(names, grads):
    return [GRAD_LAYOUT[n](grads[n]).astype(WIRE_DTYPE) for n in names]


def _pack_small(rows, conv, name):
    n = len(rows)

    def body(*refs):
        out = refs[n + 1]
        out[...] = jnp.zeros_like(out)
        for i in range(n):
            out[i:i + 1, :] = refs[i][...]
        out[SMALL_CONV_AT:SMALL_CONV_AT + SMALL_CONV_ROWS, :] = refs[n][...]

    vm = pl.BlockSpec(memory_space=pltpu.VMEM)
    return pl.pallas_call(body, name=name, out_shape=SDS((SMALL_CONV_AT + SMALL_CONV_ROWS, D_MODEL), F32),
                          in_specs=[vm] * (n + 1), out_specs=vm)(*rows, conv)


def _pad_row(a):
    a = a.reshape(1, -1).astype(F32)
    return jnp.pad(a, ((0, 0), (0, D_MODEL - a.shape[1])))


def kernel(x, p, ln_in_g, ln_in_b, w_in, pool_w, pool_scale, conv_w, a_log, dt_bias, o_norm_w, w_out, ln1_g, ln1_b, w_up, w_down, ple_gate_w, ple_proj_w, ln2_g, ln2_b, loss_target, m_ln_in_g, m_ln_in_b, m_w_in, m_pool_w, m_pool_scale, m_conv_w, m_a_log, m_dt_bias, m_o_norm_w, m_w_out, m_ln1_g, m_ln1_b, m_w_up, m_w_down, m_ple_gate_w, m_ple_proj_w, m_ln2_g, m_ln2_b, v_ln_in_g, v_ln_in_b, v_w_in, v_pool_w, v_pool_scale, v_conv_w, v_a_log, v_dt_bias, v_o_norm_w, v_w_out, v_ln1_g, v_ln1_b, v_w_up, v_w_down, v_ple_gate_w, v_ple_proj_w, v_ln2_g, v_ln2_b):
    given = dict(locals())
    chip = 2 * lax.axis_index("x") + lax.axis_index("y")

    shard = lambda n: _as2d(given[n]).astype(WIRE_DTYPE)

    wts = {"ln_in_g": ln_in_g, "ln_in_b": ln_in_b, "pool_scale": pool_scale[0], "a_log": a_log[0],
           "dt_bias": dt_bias[0], "o_norm_w": o_norm_w[0], "ln1_g": ln1_g[0], "ln1_b": ln1_b[0],
           "ln2_g": ln2_g[0], "ln2_b": ln2_b[0]}

    conv_pad = jnp.pad(conv_w[0], ((0, 8 - CONV_K), (0, 0)))
    first_srcs = [shard(n) for n in EARLY] + [conv_pad]
    first_lands = [lax.empty((N_CHIPS,) + s.shape, s.dtype) for s in first_srcs]
    fsend, frecv, fsrcs, flands, start_token = _split_copy_start(
        "first_gather_start", first_srcs, first_lands, _first_gather_copies, first_srcs[0])
    late = {}
    for n in ("w_in", "m_w_in", "v_w_in"):
        given[n], _ = lax.optimization_barrier((given[n], start_token))

    def first_weights(after):
        _, lands = _split_copy_wait("first_gather_wait", fsend, frecv, fsrcs, flands,
                                    [after, given["w_in"], given["m_w_in"], given["v_w_in"]], _first_gather_copies)
        stacks = _pass_halves(lands[0:len(EARLY)])
        first = _full_weights(EARLY, stacks)
        first["conv_w"] = jnp.concatenate([lands[len(EARLY)][j, 0:CONV_K] for j in range(N_CHIPS)], axis=1)
        late_srcs = [shard(n) for n in LATE]
        late_lands = [lax.empty((N_CHIPS,) + s.shape, s.dtype) for s in late_srcs]
        late["send"], late["recv"], late["srcs"], late["lands"], token = _split_copy_start(
            "late_gather_start", late_srcs, late_lands, _late_gather_copies, stacks[0])
        return first, token

    def late_weights(after):
        _, stacks = _split_copy_wait("late_gather_wait", late["send"], late["recv"], late["srcs"], late["lands"],
                                     after, _late_gather_copies)
        return _full_weights(LATE, stacks)

    scatter = {}

    def send_late_grads(grads):
        srcs = _grads_by_chip(LATE, grads)
        lands = [lax.empty((N_CHIPS - 1,) + g.shape[1:], g.dtype) for g in srcs]
        scatter["send"], scatter["recv"], scatter["srcs"], scatter["lands"], token = _split_copy_start(
            "late_scatter_start", srcs, lands, _late_scatter_copies, srcs[0])
        return token

    last = {}

    def send_early_grads(grads):
        by_chip = _grads_by_chip(EARLY, grads)
        theirs = _swap_halves(by_chip)
        pair = [_add_pair(g, t, n) for g, t, n in zip(by_chip, theirs, EARLY)]
        lands = [lax.empty((N_CHIPS - 1,) + q.shape[1:], q.dtype) for q in pair]
        last["send"], last["recv"], last["srcs"], last["lands"], token = _split_copy_start(
            "early_scatter_start", pair, lands, _late_scatter_copies, pair[0])
        return token

    grad_x, grads, loss = _local_step(x[0], p[0, 0], loss_target[0], wts, start_token, first_weights, late_weights,
                                      send_late_grads, send_early_grads)

    late_mine, late_landed = _split_copy_wait("late_scatter_wait", scatter["send"], scatter["recv"], scatter["srcs"],
                                              scatter["lands"], grad_x, _late_scatter_copies)
    late_part = [_sum_slabs(q, r, n) for q, r, n in zip(late_mine, late_landed, LATE)]
    pair, landed = _split_copy_wait("early_scatter_wait", last["send"], last["recv"], last["srcs"], last["lands"],
                                    grad_x, _late_scatter_copies)
    reduced = [_sum_slabs(q, r, n) for q, r, n in zip(pair, landed, EARLY)]
    from_sibling = _send_to_sibling(reduced + late_part)
    big_out = {}
    for n, g_own, g_sib in zip(EARLY + LATE, reduced + late_part, from_sibling):
        view = (lambda a: a) if given[n].ndim == 3 else _as2d
        res = _adamw_2d(view(given[n]), g_own, g_sib, view(given["m_" + n]), view(given["v_" + n]), n,
                        halves=n in EARLY)
        big_out[n] = [r.reshape(given[n].shape) for r in res]

    conv_cols = QKV_WIDTH // N_CHIPS

    def small_pack(get, conv, extra, name):
        if conv.shape[1] != QKV_WIDTH:
            conv = lax.dynamic_update_slice(jnp.zeros((CONV_K, QKV_WIDTH), F32), conv, (0, chip * conv_cols))
        return _pack_small([_pad_row(get(n)) for n in SMALL_NAMES] + extra, conv.reshape(SMALL_CONV_ROWS, D_MODEL), name)

    mine_small = small_pack(lambda n: grads[n], grads["conv_w"], [jnp.full((1, D_MODEL), loss, F32)], "pack_small_g")
    packed_small = [small_pack(lambda n: given[prefix + n], given[prefix + "conv_w"][0], [], "pack_small_" + tag)
                    for prefix, tag in (("", "w"), ("m_", "m"), ("v_", "v"))]
    small_out, loss_sum = _small_allreduce_adamw(mine_small, *packed_small, [given[n].size for n in SMALL_NAMES])

    def small_get(k, n):
        if n == "conv_w":
            full = small_out[k][len(SMALL_NAMES)].reshape(CONV_K, QKV_WIDTH)
            return lax.dynamic_slice(full, (0, chip * conv_cols), (CONV_K, conv_cols)).reshape(given[n].shape)
        return small_out[k][SMALL_NAMES.index(n)].reshape(given[n].shape)

    order = ["ln_in_g", "ln_in_b", "w_in", "pool_w", "pool_scale", "conv_w", "a_log", "dt_bias", "o_norm_w", "w_out",
             "ln1_g", "ln1_b", "w_up", "w_down", "ple_gate_w", "ple_proj_w", "ln2_g", "ln2_b"]
    outs = [loss_sum.reshape(()), grad_x[None]]
    for k in range(4):
        for n in order:
            outs.append(big_out[n][k] if n in big_out else small_get(k, n))
    return tuple(outs)
```

```python
import jax
import jax.numpy as jnp
from jax import lax
from jax.experimental import pallas as pl
from jax.experimental.pallas import tpu as pltpu

F32 = jnp.float32
MXU_DTYPE = jnp.bfloat16
WIRE_DTYPE = jnp.bfloat16
SDS = jax.ShapeDtypeStruct

D_MODEL = 1024
POOL_WINDOWS = (2, 4, 8, 16)
POOL_WIDTH = 512
POOL_GROUP = 128
POOL_OUT_GROUP = 256
HEADS = 8
HEAD_DIM = 128
DN_WIDTH = HEADS * HEAD_DIM
QKV_WIDTH = 3 * DN_WIDTH
CONV_K = 4
CHUNK = 128
DN_BWD_CHUNKS = 2
DN_FWD_CHUNKS = 2
DW_TK = 2048
DW_TM = 1024
D_FF = 4096
PLE_DIM = 256
LN_EPS = 1e-5
RMS_EPS = 1e-6
L2_EPS = 1e-6
ALPHA = 2.0 ** 0.25
Q_SCALE = HEAD_DIM ** -0.5
IN_WIDTH = 6672
C_POOL, C_QKV, C_Z, C_BETA, C_A, C_GA, C_GB = 0, 512, 3584, 4608, 4616, 4624, 5648
K_QKV, K_Z, K_GA, K_GB, K_U, K_BA, CAT_WIDTH = 0, 3072, 4096, 5120, 6144, 6656, 6912
P_U, P_BA, PROJ_F32_WIDTH = 3072, 3584, 3840

ADAM_LR, ADAM_B1, ADAM_B2, ADAM_EPS, ADAM_WD, ADAM_STEP = 0.001, 0.9, 0.999, 1e-08, 0.01, 10

N_CHIPS = 4
N_DEV = 8
VMEM_LIMIT = 56 * 1024 * 1024

EARLY = ("w_in", "pool_w")
LATE = ("w_out", "w_up", "w_down", "ple_gate_w", "ple_proj_w")
SMALL_NAMES = ("ln_in_g", "ln_in_b", "pool_scale", "ln1_g", "ln1_b", "ln2_g", "ln2_b", "o_norm_w", "a_log", "dt_bias")
SMALL_CONV_AT = 12
SMALL_CONV_ROWS = CONV_K * QKV_WIDTH // D_MODEL


def _mx(a):
    return a.astype(MXU_DTYPE)


def _dot(a, b):
    return lax.dot_general(_mx(a), _mx(b), (((1,), (0,)), ((), ())), preferred_element_type=F32)


def _dot_nt(a, b):
    return lax.dot_general(_mx(a), _mx(b), (((1,), (1,)), ((), ())), preferred_element_type=F32)


def _dot_tn(a, b):
    return lax.dot_general(_mx(a), _mx(b), (((0,), (0,)), ((), ())), preferred_element_type=F32)


def _sigmoid(x):
    return 0.5 * jnp.tanh(0.5 * x) + 0.5


def _softplus(x):
    return jnp.maximum(x, 0.0) + jnp.log(1.0 + jnp.exp(-jnp.abs(x)))


def _pc(body, name, grid, in_specs, out_specs, out_shape, scratch=(), sem=None, aliases=None):
    return pl.pallas_call(
        body, out_shape=out_shape, grid=grid, in_specs=in_specs, out_specs=out_specs,
        scratch_shapes=scratch, name=name, input_output_aliases=aliases or {},
        compiler_params=pltpu.CompilerParams(dimension_semantics=sem, vmem_limit_bytes=VMEM_LIMIT))


def _row(tm, n):
    return pl.BlockSpec((tm, n), lambda i: (i, 0))


def _const(shape):
    nd = len(shape)
    return pl.BlockSpec(shape, lambda *_: (0,) * nd)


def _matmul(a, b, mode, name, out_dtype=F32, tm=512, tn=512, tk=512, stack_out=False):
    if mode == "nn":
        (m, k), n = a.shape, b.shape[1]
    elif mode == "nt":
        (m, k), n = a.shape, b.shape[0]
    else:
        (k, m), n = a.shape, b.shape[1]
    tm, tn, tk = min(tm, m), min(tn, n), min(tk, k)
    assert m % tm == 0 and n % tn == 0 and k % tk == 0, (name, m, n, k, tm, tn, tk)
    nk = k // tk
    if mode == "nn":
        a_spec = pl.BlockSpec((tm, tk), lambda i, j, kk: (i, kk))
        b_spec = pl.BlockSpec((tk, tn), lambda i, j, kk: (kk, j))
        dot = _dot
    elif mode == "nt":
        a_spec = pl.BlockSpec((tm, tk), lambda i, j, kk: (i, kk))
        b_spec = pl.BlockSpec((tn, tk), lambda i, j, kk: (j, kk))
        dot = _dot_nt
    else:
        a_spec = pl.BlockSpec((tk, tm), lambda i, j, kk: (kk, i))
        b_spec = pl.BlockSpec((tk, tn), lambda i, j, kk: (kk, j))
        dot = _dot_tn

    def body(a_ref, b_ref, o_ref, *acc):
        if nk == 1:
            o_ref[...] = dot(a_ref[...], b_ref[...]).astype(out_dtype)
            return
        acc_ref, kk = acc[0], pl.program_id(2)

        @pl.when(kk == 0)
        def _():
            acc_ref[...] = dot(a_ref[...], b_ref[...])

        @pl.when((kk > 0) & (kk < nk - 1))
        def _():
            acc_ref[...] += dot(a_ref[...], b_ref[...])

        @pl.when(kk == nk - 1)
        def _():
            o_ref[...] = (acc_ref[...] + dot(a_ref[...], b_ref[...])).astype(out_dtype)

    if stack_out:
        o_spec, o_shape = pl.BlockSpec((None, tm, tn), lambda i, j, kk: (j, i, 0)), SDS((n // tn, m, tn), out_dtype)
    else:
        o_spec, o_shape = pl.BlockSpec((tm, tn), lambda i, j, kk: (i, j)), SDS((m, n), out_dtype)
    return _pc(body, name, (m // tm, n // tn, nk), [a_spec, b_spec], o_spec, o_shape,
               scratch=[pltpu.VMEM((tm, tn), F32)] if nk > 1 else [],
               sem=("parallel", "parallel", "arbitrary"))(a, b)


PROJ_TN = 768


def _proj_conv(h0_bf, w_cat, conv_w, tm, after):
    t = h0_bf.shape[0]
    n_qkv = QKV_WIDTH // PROJ_TN
    n_gate = 3 * D_MODEL // PROJ_TN

    def body(h_ref, w_ref, cw_ref, after_ref, o_ref, gate_ref, act_ref, ds_ref, carry_ref, ext_ref):
        @pl.when(pl.program_id(0) == 0)
        def _():
            carry_ref[...] = jnp.zeros_like(carry_ref)

        h = h_ref[...]

        def project(cb):
            cols = slice(cb * PROJ_TN, (cb + 1) * PROJ_TN)
            res = _dot(h, w_ref[:, cols])
            if cb < n_qkv:
                o_ref[:, cols] = res
            elif cb < n_qkv + n_gate:
                gate_ref[:, (cb - n_qkv) * PROJ_TN:(cb - n_qkv + 1) * PROJ_TN] = _mx(res)
            else:
                o_ref[:, P_U:PROJ_F32_WIDTH] = res

        def conv(cb, part):
            cols = slice(cb * PROJ_TN, (cb + 1) * PROJ_TN)
            if part == 0:
                ext_ref[cb, 0:8, :] = carry_ref[:, cols]
                ext_ref[cb, 8:8 + tm, :] = o_ref[:, cols]
                carry_ref[:, cols] = o_ref[tm - 8:tm, cols]
            w = [cw_ref[pl.ds(k, 1), cols] for k in range(CONV_K)]
            for r in range(part * (tm // 2), (part + 1) * (tm // 2), CONV_ROWS):
                y = _conv_rows(ext_ref.at[cb], w, r, CONV_ROWS)
                s = _sigmoid(y)
                act_ref[pl.ds(r, CONV_ROWS), cols] = y * s
                ds_ref[pl.ds(r, CONV_ROWS), cols] = _mx(s * (1.0 + y * (1.0 - s)))

        pending = [(cb, part) for cb in range(n_qkv) for part in range(2)]
        project(0)
        for cb in range(1, CAT_WIDTH // PROJ_TN):
            project(cb)
            if pending and pending[0][0] < cb:
                conv(*pending.pop(0))
        for cb, part in pending:
            conv(cb, part)

    return _pc(body, "proj_conv", (t // tm,),
               [_row(tm, D_MODEL), _const((D_MODEL, CAT_WIDTH)), _const((CONV_K, QKV_WIDTH)), ANY],
               [_row(tm, PROJ_F32_WIDTH), _row(tm, 3 * D_MODEL), _row(tm, QKV_WIDTH), _row(tm, QKV_WIDTH)],
               [SDS((t, PROJ_F32_WIDTH), F32), SDS((t, 3 * D_MODEL), MXU_DTYPE), SDS((t, QKV_WIDTH), F32),
                SDS((t, QKV_WIDTH), MXU_DTYPE)],
               scratch=[pltpu.VMEM((8, QKV_WIDTH), F32), pltpu.VMEM((n_qkv, 8 + tm, PROJ_TN), F32)],
               sem=("arbitrary",))(h0_bf, w_cat, conv_w, after)


def _ln_stats(x):
    mu = jnp.mean(x, axis=-1, keepdims=True)
    xc = x - mu
    var = jnp.mean(xc * xc, axis=-1, keepdims=True)
    rstd = lax.rsqrt(var + LN_EPS)
    return xc * rstd, rstd


def _ln_bwd(dy, xhat, rstd, g):
    dxh = dy * g
    m1 = jnp.mean(dxh, axis=-1, keepdims=True)
    m2 = jnp.mean(dxh * xhat, axis=-1, keepdims=True)
    return rstd * (dxh - m1 - xhat * m2)


def _ln_in(x, g, b, tm, after):
    t, d = x.shape

    def body(x_ref, g_ref, b_ref, after_ref, h_ref, hb_ref):
        xhat, _ = _ln_stats(x_ref[...])
        h = xhat * g_ref[...] + b_ref[...]
        h_ref[...] = h
        hb_ref[...] = _mx(h)

    return _pc(body, "ln_in", (t // tm,), [_row(tm, d), _const((1, d)), _const((1, d)), ANY],
               [_row(tm, d), _row(tm, d)], [SDS((t, d), F32), SDS((t, d), MXU_DTYPE)],
               sem=("parallel",))(x, g, b, after)


def _pool_fwd(proj, pool_w, tm):
    t = proj.shape[0]
    ublk = P_U // POOL_WIDTH

    def body(u_ref, halo_ref, pw_ref, ypre_ref, d_ref, ext_ref):
        i = pl.program_id(0)
        ext_ref[0:16, :] = jnp.where(i > 0, halo_ref[...], 0.0)
        ext_ref[16:16 + tm, :] = u_ref[...]
        tok = i * tm + lax.broadcasted_iota(jnp.int32, (tm, POOL_GROUP), 0)
        for gi, w in enumerate(POOL_WINDOWS):
            cs = pl.ds(gi * POOL_GROUP, POOL_GROUP)
            ug = ext_ref[pl.ds(16, tm), cs]
            s = ug
            for k in range(1, w):
                s = s + ext_ref[pl.ds(16 - k, tm), cs]
            cnt = jnp.minimum(tok + 1, w).astype(F32)
            db = _mx(s / cnt - ug)
            d_ref[:, gi * POOL_GROUP:(gi + 1) * POOL_GROUP] = db
            ypre_ref[:, gi * POOL_OUT_GROUP:(gi + 1) * POOL_OUT_GROUP] = _dot(db, pw_ref[gi])

    halo = pl.BlockSpec((16, POOL_WIDTH), lambda i: (jnp.maximum(i * (tm // 16) - 1, 0), ublk))
    return _pc(body, "pool_fwd", (t // tm,),
               [pl.BlockSpec((tm, POOL_WIDTH), lambda i: (i, ublk)), halo, _const((4, POOL_GROUP, POOL_OUT_GROUP))],
               [_row(tm, D_MODEL), _row(tm, POOL_WIDTH)],
               [SDS((t, D_MODEL), F32), SDS((t, POOL_WIDTH), MXU_DTYPE)],
               scratch=[pltpu.VMEM((16 + tm, POOL_WIDTH), F32)], sem=("parallel",))(proj, proj, pool_w)


def _pool_bwd(dyp, d_bf, pool_w, dproj, tm):
    t = dyp.shape[0]
    n = t // tm

    def body(dy_ref, dyn_ref, d_ref, pw_ref, dproj_ref, du_ref, dpw_ref, ext_ref):
        i = pl.program_id(0)

        @pl.when(i == 0)
        def _():
            dpw_ref[...] = jnp.zeros_like(dpw_ref)

        tok = i * tm + lax.broadcasted_iota(jnp.int32, (tm + 16, POOL_GROUP), 0)
        for gi, w in enumerate(POOL_WINDOWS):
            dy = dy_ref[:, gi * POOL_OUT_GROUP:(gi + 1) * POOL_OUT_GROUP]
            dyn = dyn_ref[:, gi * POOL_OUT_GROUP:(gi + 1) * POOL_OUT_GROUP]
            pw = pw_ref[gi]
            dd = _dot_nt(dy, pw)
            ddn = jnp.where(i < n - 1, _dot_nt(dyn, pw), 0.0)
            cnt = jnp.minimum(tok + 1, w).astype(F32)
            ext_ref[0:tm, :] = dd / cnt[0:tm]
            ext_ref[tm:tm + 16, :] = ddn / cnt[tm:tm + 16]
            s = ext_ref[pl.ds(0, tm), :]
            for k in range(1, w):
                s = s + ext_ref[pl.ds(k, tm), :]
            du_ref[:, gi * POOL_GROUP:(gi + 1) * POOL_GROUP] = _mx(s - dd)
            dpw_ref[gi] += _dot_tn(d_ref[:, gi * POOL_GROUP:(gi + 1) * POOL_GROUP], dy)

    nxt = pl.BlockSpec((16, D_MODEL), lambda i: (jnp.minimum((i + 1) * (tm // 16), t // 16 - 1), 0))
    return _pc(body, "pool_bwd", (n,),
               [_row(tm, D_MODEL), nxt, _row(tm, POOL_WIDTH), _const((4, POOL_GROUP, POOL_OUT_GROUP)), ANY],
               [pl.BlockSpec((tm, POOL_WIDTH), lambda i: (i, K_U // POOL_WIDTH)),
                _const((4, POOL_GROUP, POOL_OUT_GROUP))],
               [SDS(dproj.shape, dproj.dtype), SDS((4, POOL_GROUP, POOL_OUT_GROUP), F32)],
               scratch=[pltpu.VMEM((tm + 16, POOL_GROUP), F32)], sem=("arbitrary",),
               aliases={4: 0})(dyp, dyp, d_bf, pool_w, dproj)


CONV_BLK = 512


CONV_ROWS = 32


def _conv_rows(ext_ref, w, r, rows):
    y = w[0] * ext_ref[pl.ds(r + 5, rows), :]
    for k in range(1, CONV_K):
        y = y + w[k] * ext_ref[pl.ds(r + 5 + k, rows), :]
    return y


def _conv_bwd(dact, dsilu, proj, conv_w, dproj, tm):
    t = proj.shape[0]
    n = t // tm

    def body(da_ref, dan_ref, ds_ref, dsn_ref, x_ref, xp_ref, w_ref, dproj_ref, dx_ref, dw_ref, ext_ref, dy_ref):
        i = pl.program_id(1)

        @pl.when(i == 0)
        def _():
            dw_ref[...] = jnp.zeros_like(dw_ref)

        ext_ref[0:8, :] = jnp.where(i > 0, xp_ref[...], 0.0)
        ext_ref[8:8 + tm, :] = x_ref[...]
        w = [w_ref[pl.ds(k, 1), :] for k in range(CONV_K)]

        acc = [jnp.zeros((8, CONV_BLK), F32) for _ in range(CONV_K)]
        for r in range(0, tm, CONV_ROWS):
            dy = da_ref[pl.ds(r, CONV_ROWS), :] * ds_ref[pl.ds(r, CONV_ROWS), :].astype(F32)
            dy_ref[pl.ds(r, CONV_ROWS), :] = dy
            for k in range(CONV_K):
                prod = dy * ext_ref[pl.ds(r + 5 + k, CONV_ROWS), :]
                for q in range(0, CONV_ROWS, 8):
                    acc[k] = acc[k] + prod[q:q + 8]
        dy_ref[tm:tm + 8, :] = jnp.where(i < n - 1, dan_ref[...] * dsn_ref[0:8, :].astype(F32), 0.0)
        for k in range(CONV_K):
            dw_ref[pl.ds(k, 1), :] += jnp.sum(acc[k], axis=0, keepdims=True)
        for r in range(0, tm, CONV_ROWS):
            dx = w[0] * dy_ref[pl.ds(r + 3, CONV_ROWS), :]
            for k in range(1, CONV_K):
                dx = dx + w[k] * dy_ref[pl.ds(r + 3 - k, CONV_ROWS), :]
            dx_ref[pl.ds(r, CONV_ROWS), :] = _mx(dx)

    blk = pl.BlockSpec((tm, CONV_BLK), lambda j, i: (i, j))
    prev = pl.BlockSpec((8, CONV_BLK), lambda j, i: (jnp.maximum(i * (tm // 8) - 1, 0), j))
    nxt = pl.BlockSpec((8, CONV_BLK), lambda j, i: (jnp.minimum((i + 1) * (tm // 8), t // 8 - 1), j))
    nxt16 = pl.BlockSpec((16, CONV_BLK), lambda j, i: (jnp.minimum((i + 1) * (tm // 16), t // 16 - 1), j))
    wspec = pl.BlockSpec((CONV_K, CONV_BLK), lambda j, i: (0, j))
    return _pc(body, "conv_bwd", (QKV_WIDTH // CONV_BLK, n),
               [blk, nxt, blk, nxt16, blk, prev, wspec, ANY],
               [blk, pl.BlockSpec((8, CONV_BLK), lambda j, i: (0, j))],
               [SDS(dproj.shape, dproj.dtype), SDS((8, QKV_WIDTH), F32)],
               scratch=[pltpu.VMEM((8 + tm, CONV_BLK), F32), pltpu.VMEM((8 + tm, CONV_BLK), F32)],
               sem=("parallel", "arbitrary"), aliases={7: 0})(dact, dact, dsilu, dsilu, proj, proj, conv_w, dproj)


def _lane(shape):
    return lax.broadcasted_iota(jnp.int32, shape, 1)


def _ba_fwd(proj, al_row, dtb_row, tm):
    t = proj.shape[0]
    bablk = P_BA // 128

    def body(ba_ref, al_ref, dtb_ref, bg_ref):
        ba = ba_ref[...]
        lane = _lane(ba.shape)
        g = -jnp.exp(al_ref[...]) * _softplus(ba + dtb_ref[...])
        bg_ref[...] = jnp.where(lane < HEADS, _sigmoid(ba), jnp.where(lane < 2 * HEADS, g, 0.0))

    return _pc(body, "ba_fwd", (t // tm,),
               [pl.BlockSpec((tm, 128), lambda i: (i, bablk)), _const((1, 128)), _const((1, 128))],
               _row(tm, 128), SDS((t, 128), F32), sem=("parallel",))(proj, al_row, dtb_row)


def _ba_bwd(dbg, bg, proj, al_row, dtb_row, dproj, tm):
    t = proj.shape[0]
    bablk = P_BA // 128

    def body(dbg_ref, bg_ref, ba_ref, al_ref, dtb_ref, dproj_ref, dba_ref, acc_ref):
        i = pl.program_id(0)

        @pl.when(i == 0)
        def _():
            acc_ref[...] = jnp.zeros_like(acc_ref)

        dbg_v, bg_v, ba = dbg_ref[...], bg_ref[...], ba_ref[...]
        lane = _lane(ba.shape)
        is_g = (lane >= HEADS) & (lane < 2 * HEADS)
        dbeta_raw = dbg_v * bg_v * (1.0 - bg_v)
        da_raw = dbg_v * (-jnp.exp(al_ref[...])) * _sigmoid(ba + dtb_ref[...])
        dba_ref[:, 0:128] = _mx(jnp.where(lane < HEADS, dbeta_raw, jnp.where(is_g, da_raw, 0.0)))
        dba_ref[:, 128:CAT_WIDTH - K_BA] = jnp.zeros((tm, CAT_WIDTH - K_BA - 128), dba_ref.dtype)
        acc_ref[0:1, :] += jnp.sum(jnp.where(is_g, dbg_v * bg_v, 0.0), axis=0, keepdims=True)
        acc_ref[1:2, :] += jnp.sum(jnp.where(is_g, da_raw, 0.0), axis=0, keepdims=True)

    tail = CAT_WIDTH - K_BA
    return _pc(body, "ba_bwd", (t // tm,),
               [_row(tm, 128), _row(tm, 128), pl.BlockSpec((tm, 128), lambda i: (i, bablk)),
                _const((1, 128)), _const((1, 128)), ANY],
               [pl.BlockSpec((tm, tail), lambda i: (i, K_BA // tail)), _const((8, 128))],
               [SDS(dproj.shape, dproj.dtype), SDS((8, 128), F32)],
               sem=("arbitrary",), aliases={5: 0})(dbg, bg, proj, al_row, dtb_row, dproj)


def _each(f, *lists):
    return [f(*a) for a in zip(*lists)]


def _rowsum(a):
    return jnp.sum(a, axis=1, keepdims=True)


def _chunk_terms(qs, ks, bgvs, g_rows, hs):
    c = CHUNK
    ii = lax.broadcasted_iota(jnp.int32, (c, c), 0)
    jj = lax.broadcasted_iota(jnp.int32, (c, c), 1)
    lane = _lane((c, 128))
    incl = ii >= jj
    beta = [_rowsum(jnp.where(lane == h, bgv, 0.0)) for h, bgv in zip(hs, bgvs)]
    g_col = [_rowsum(jnp.where(lane == HEADS + h, bgv, 0.0)) for h, bgv in zip(hs, bgvs)]
    rq = _each(lambda q: lax.rsqrt(_rowsum(q * q) + L2_EPS), qs)
    rk = _each(lambda k: lax.rsqrt(_rowsum(k * k) + L2_EPS), ks)
    yq = _each(jnp.multiply, qs, rq)
    kn = _each(jnp.multiply, ks, rk)
    qn = _each(lambda a: a * Q_SCALE, yq)
    gc_col = _each(lambda g: _rowsum(jnp.where(jj <= ii, g, 0.0)), g_rows)
    gc_row = _each(lambda g: jnp.sum(jnp.where(ii <= jj, g, 0.0), axis=0, keepdims=True), g_col)
    dm = _each(lambda a, b: jnp.where(incl, jnp.exp(jnp.where(incl, a - b, 0.0)), 0.0), gc_col, gc_row)
    gl = _each(_rowsum, g_rows)
    eg = _each(jnp.exp, gc_col)
    ek = _each(lambda a, b: jnp.exp(a - b), gl, gc_col)
    egl = _each(jnp.exp, gl)
    kb = _each(jnp.multiply, kn, beta)
    kk = _each(_dot_nt, kb, kn)
    qk = _each(_dot_nt, qn, kn)
    m = _each(lambda a, b: jnp.where(ii > jj, a * b, 0.0), kk, dm)
    attn = _each(jnp.multiply, qk, dm)
    return dict(ii=ii, jj=jj, beta=beta, rq=rq, rk=rk, yq=yq, kn=kn, qn=qn, dm=dm, eg=eg, ek=ek,
                egl=egl, kb=kb, m=m, attn=attn)


def _unit_lower_inverse_minus_identity(ms, ii, jj):
    pair = (ii >> 1) == (jj >> 1)
    ys = _each(lambda m: -jnp.where(pair, m, 0.0), ms)
    s = 1
    while (1 << s) < CHUNK:
        mask = ((ii >> (s + 1)) == (jj >> (s + 1))) & ((ii >> s) != (jj >> s))
        lbs = _each(lambda m: jnp.where(mask, m, 0.0), ms)
        zs = _each(lambda y, lb: lb + _dot(y, lb), ys, lbs)
        ys = _each(lambda y, z: y - z - _dot(z, y), ys, zs)
        s += 1
    return ys


def _dn_fwd(qkv_act, bg, bgt):
    t = qkv_act.shape[0]
    c = CHUNK
    per = DN_FWD_CHUNKS if t % (DN_FWD_CHUNKS * c) == 0 else 1
    nt = t // c
    hs = list(range(HEADS))
    entries = [(s_, h) for s_ in range(per) for h in hs]
    qo = [slice(h * HEAD_DIM, (h + 1) * HEAD_DIM) for h in hs]
    ko = [slice(DN_WIDTH + h * HEAD_DIM, DN_WIDTH + (h + 1) * HEAD_DIM) for h in hs]
    vo = [slice(2 * DN_WIDTH + h * HEAD_DIM, 2 * DN_WIDTH + (h + 1) * HEAD_DIM) for h in hs]

    def body(qkv_ref, bg_ref, bgt_ref, o_ref, u_ref, w_ref, qg_ref, kg_ref, attn_ref, y_ref, vn_ref, st_ref, egl_ref,
             s_ref):
        @pl.when(pl.program_id(0) == 0)
        def _():
            s_ref[...] = jnp.zeros_like(s_ref)

        rows = [pl.ds(s_ * c, c) for s_ in range(per)]
        qs = [qkv_ref[rows[s_], qo[h]] for s_, h in entries]
        ks = [qkv_ref[rows[s_], ko[h]] for s_, h in entries]
        vs = [qkv_ref[rows[s_], vo[h]] for s_, h in entries]
        bgvs = [bg_ref[rows[s_], :] for s_, _ in entries]
        g_rows = [bgt_ref[pl.ds(HEADS + h, 1), rows[s_]] for s_, h in entries]
        ct = _chunk_terms(qs, ks, bgvs, g_rows, [h for _, h in entries])
        ys = _unit_lower_inverse_minus_identity(ct["m"], ct["ii"], ct["jj"])
        vb = _each(jnp.multiply, vs, ct["beta"])
        kbe = _each(jnp.multiply, ct["kb"], ct["eg"])
        us = _each(lambda a, y: a + _dot(y, a), vb, ys)
        ws = _each(lambda a, y: _mx(a + _dot(y, a)), kbe, ys)
        qg = _each(lambda a, b: _mx(a * b), ct["qn"], ct["eg"])
        kg = _each(lambda a, b: _mx(a * b), ct["kn"], ct["ek"])
        attn = _each(_mx, ct["attn"])
        for e, (s_, h) in enumerate(entries):
            u_ref[rows[s_], qo[h]] = us[e]
            w_ref[rows[s_], qo[h]] = ws[e]
            qg_ref[rows[s_], qo[h]] = qg[e]
            kg_ref[rows[s_], qo[h]] = kg[e]
            attn_ref[rows[s_], qo[h]] = attn[e]
            y_ref[rows[s_], qo[h]] = _mx(ys[e])
            egl_ref[s_, h:h + 1, :] = jnp.broadcast_to(ct["egl"][e], (1, HEAD_DIM))
        ss = [s_ref[h] for h in hs]
        for s_ in range(per):
            pick = lambda xs: xs[s_ * HEADS:(s_ + 1) * HEADS]
            sb = _each(_mx, ss)
            vn = _each(lambda a, b, st: a - _dot(b, st), pick(us), pick(ws), sb)
            vnb = _each(_mx, vn)
            oa = _each(_dot, pick(qg), sb)
            ob = _each(_dot, pick(attn), vnb)
            upd = _each(_dot_tn, pick(kg), vnb)
            for h in hs:
                st_ref[s_, h] = ss[h]
                vn_ref[rows[s_], qo[h]] = vnb[h]
                o_ref[rows[s_], qo[h]] = oa[h] + ob[h]
            ss = _each(lambda st, g, d: st * g + d, ss, pick(ct["egl"]), upd)
        for h in hs:
            s_ref[h] = ss[h]

    wide = _row(per * c, DN_WIDTH)
    return _pc(body, "dn_fwd", (nt // per,),
               [_row(per * c, QKV_WIDTH), _row(per * c, 128), pl.BlockSpec((2 * HEADS, per * c), lambda i: (0, i))],
               [wide] * 8 + [pl.BlockSpec((per, HEADS, HEAD_DIM, HEAD_DIM), lambda i: (i, 0, 0, 0)),
                             pl.BlockSpec((per, HEADS, HEAD_DIM), lambda i: (i, 0, 0))],
               [SDS((t, DN_WIDTH), F32), SDS((t, DN_WIDTH), F32)] + [SDS((t, DN_WIDTH), MXU_DTYPE)] * 6
               + [SDS((nt, HEADS, HEAD_DIM, HEAD_DIM), F32), SDS((nt, HEADS, HEAD_DIM), F32)],
               scratch=[pltpu.VMEM((HEADS, HEAD_DIM, HEAD_DIM), F32)], sem=("arbitrary",))(qkv_act, bg, bgt)


def _dn_bwd(do, qkv_act, bg, bgt, u, w, qg, kg, attn, ymat, vn, states, egl):
    t = do.shape[0]
    c = CHUNK
    per = DN_BWD_CHUNKS if t % (DN_BWD_CHUNKS * c) == 0 else 1
    nt = t // c
    hs = list(range(HEADS))
    entries = [(s_, h) for s_ in range(per) for h in hs]
    qo = [slice(h * HEAD_DIM, (h + 1) * HEAD_DIM) for h in hs]
    ko = [slice(DN_WIDTH + h * HEAD_DIM, DN_WIDTH + (h + 1) * HEAD_DIM) for h in hs]
    vo = [slice(2 * DN_WIDTH + h * HEAD_DIM, 2 * DN_WIDTH + (h + 1) * HEAD_DIM) for h in hs]

    def body(do_ref, qkv_ref, bg_ref, bgt_ref, u_ref, w_ref, qg_ref, kg_ref, attn_ref, y_ref, vn_ref, st_ref, egl_ref,
             dqkv_ref, dbg_ref, ds_ref):
        @pl.when(pl.program_id(0) == 0)
        def _():
            ds_ref[...] = jnp.zeros_like(ds_ref)

        rows = [pl.ds(s_ * c, c) for s_ in range(per)]
        scan = {}
        dsp = [ds_ref[h] for h in hs]
        for s_ in reversed(range(per)):
            r = rows[s_]
            dsb = _each(_mx, dsp)
            ss = [st_ref[s_, h] for h in hs]
            sb = _each(_mx, ss)
            du_s = [_dot(kg_ref[r, sl], b) + _dot_tn(attn_ref[r, sl], do_ref[r, sl]) for sl, b in zip(qo, dsb)]
            dub = _each(_mx, du_s)
            scan[s_] = dict(
                du=du_s,
                dkg=[_dot_nt(vn_ref[r, sl], b) for sl, b in zip(qo, dsb)],
                dqg=[_dot_nt(do_ref[r, sl], b) for sl, b in zip(qo, sb)],
                dattn=[_dot_nt(do_ref[r, sl], vn_ref[r, sl]) for sl in qo],
                dw=[-_dot_nt(a, b) for a, b in zip(dub, sb)],
                degl=[jnp.sum(_rowsum(a * b), axis=0, keepdims=True) for a, b in zip(ss, dsp)])
            upd = [_dot_tn(qg_ref[r, sl], do_ref[r, sl]) - _dot_tn(w_ref[r, sl], a) for sl, a in zip(qo, dub)]
            dsp = [dsp[h] * egl_ref[s_, h:h + 1, :] + upd[h] for h in hs]
        for h in hs:
            ds_ref[h] = dsp[h]
        gather = lambda key: [scan[s_][key][h] for s_, h in entries]
        du, dkg_v, dqg_v, dattn_v, dwv, degl_v = (gather(k) for k in ("du", "dkg", "dqg", "dattn", "dw", "degl"))

        lane = _lane((c, 128))
        rowi = lax.broadcasted_iota(jnp.int32, (c, 1), 0)
        qs = [qkv_ref[rows[s_], qo[h]] for s_, h in entries]
        ks = [qkv_ref[rows[s_], ko[h]] for s_, h in entries]
        vs = [qkv_ref[rows[s_], vo[h]] for s_, h in entries]
        bgvs = [bg_ref[rows[s_], :] for s_, _ in entries]
        g_rows = [bgt_ref[pl.ds(HEADS + h, 1), rows[s_]] for s_, h in entries]
        ct = _chunk_terms(qs, ks, bgvs, g_rows, [h for _, h in entries])
        ii, jj = ct["ii"], ct["jj"]
        beta, eg, ek, kb, kn, qn, dm = ct["beta"], ct["eg"], ct["ek"], ct["kb"], ct["kn"], ct["qn"], ct["dm"]
        ys = [y_ref[rows[s_], qo[h]] for s_, h in entries]
        dvb = _each(lambda a, y: a + _dot_tn(y, a), du, ys)
        dkbe = _each(lambda a, y: a + _dot_tn(y, a), dwv, ys)
        dm_u = [_dot_nt(a, u_ref[rows[s_], qo[h]]) for a, (s_, h) in zip(dvb, entries)]
        dm_w = [_dot_nt(a, w_ref[rows[s_], qo[h]]) for a, (s_, h) in zip(dkbe, entries)]
        dms = _each(lambda a, b: jnp.where(ii > jj, -(a + b), 0.0), dm_u, dm_w)
        dkk = _each(jnp.multiply, dms, dm)
        dqk = _each(jnp.multiply, dattn_v, dm)
        gmat = _each(lambda a, b, c_, d: a * b + c_ * d, dms, ct["m"], dattn_v, ct["attn"])
        dkb = _each(lambda a, b, c_, d: _dot(a, b) + c_ * d, dkk, kn, dkbe, eg)
        dk1 = _each(_dot_tn, dkk, kb)
        dk2 = _each(_dot_tn, dqk, qn)
        dq1 = _each(_dot, dqk, kn)
        dk = _each(lambda a, b, c_, d: a + b + c_ * d, dk1, dk2, dkg_v, ek)
        dq = _each(lambda a, b, c_: a + b * c_, dq1, dqg_v, eg)
        deg = _each(lambda a, b, c_, d: _rowsum(a * b) + _rowsum(c_ * d), dqg_v, qn, dkbe, kb)
        dek = _each(lambda a, b: _rowsum(a * b), dkg_v, kn)
        dgl = _each(lambda a, b, c_, d: jnp.sum(a * b, axis=0, keepdims=True) + c_ * d, dek, ek, degl_v, ct["egl"])
        cs_row = _each(lambda g: jnp.sum(g, axis=0, keepdims=True), gmat)
        cs_col = _each(lambda r: _rowsum(jnp.where(ii == jj, r, 0.0)), cs_row)
        dgc = _each(lambda a, b, c_, d, g, e, f: a * b - c_ * d + _rowsum(g) - e + jnp.where(rowi == c - 1, f, 0.0),
                    deg, eg, dek, ek, gmat, cs_col, dgl)
        dgc_row = _each(lambda a: jnp.sum(jnp.where(ii == jj, a, 0.0), axis=0, keepdims=True), dgc)
        dg = _each(lambda r: _rowsum(jnp.where(jj >= ii, r, 0.0)), dgc_row)
        dbeta = _each(lambda a, b, c_, d: _rowsum(a * b) + _rowsum(c_ * d), dkb, kn, dvb, vs)
        dk = _each(lambda a, b, c_: a + b * c_, dk, dkb, beta)
        dbg = [jnp.zeros((c, 128), F32) for _ in range(per)]
        for e, (s_, h) in enumerate(entries):
            dyq = dq[e] * Q_SCALE
            yq = ct["yq"][e]
            dqkv_ref[rows[s_], qo[h]] = ct["rq"][e] * (dyq - yq * _rowsum(yq * dyq))
            dqkv_ref[rows[s_], ko[h]] = ct["rk"][e] * (dk[e] - kn[e] * _rowsum(kn[e] * dk[e]))
            dqkv_ref[rows[s_], vo[h]] = dvb[e] * beta[e]
            dbg[s_] = dbg[s_] + jnp.where(lane == h, dbeta[e], 0.0) + jnp.where(lane == HEADS + h, dg[e], 0.0)
        for s_ in range(per):
            dbg_ref[rows[s_], :] = dbg[s_]

    ns = nt // per
    rev = pl.BlockSpec((per * c, DN_WIDTH), lambda i: (ns - 1 - i, 0))
    return _pc(body, "dn_bwd", (ns,),
               [rev, pl.BlockSpec((per * c, QKV_WIDTH), lambda i: (ns - 1 - i, 0)),
                pl.BlockSpec((per * c, 128), lambda i: (ns - 1 - i, 0)),
                pl.BlockSpec((2 * HEADS, per * c), lambda i: (0, ns - 1 - i))]
               + [rev] * 7
               + [pl.BlockSpec((per, HEADS, HEAD_DIM, HEAD_DIM), lambda i: (ns - 1 - i, 0, 0, 0)),
                  pl.BlockSpec((per, HEADS, HEAD_DIM), lambda i: (ns - 1 - i, 0, 0))],
               [pl.BlockSpec((per * c, QKV_WIDTH), lambda i: (ns - 1 - i, 0)),
                pl.BlockSpec((per * c, 128), lambda i: (ns - 1 - i, 0))],
               [SDS((t, QKV_WIDTH), F32), SDS((t, 128), F32)],
               scratch=[pltpu.VMEM((HEADS, HEAD_DIM, HEAD_DIM), F32)],
               sem=("arbitrary",))(do, qkv_act, bg, bgt, u, w, qg, kg, attn, ymat, vn, states, egl)


MIX_ROWS = 64


def _mix_oproj_ln1(o, gates, ypre, pool_scale, wo_row, w_out, h0, g1, b1, tm):
    t = o.shape[0]

    def body(o_ref, z_ref, ga_ref, gb_ref, yp_ref, ps_ref, wo_ref, w_ref, h0_ref, g_ref, b_ref,
             mixed_ref, a1_ref, h1_ref, h1b_ref):
        for r in range(0, tm, MIX_ROWS):
            rows = pl.ds(r, MIX_ROWS)
            for h in range(HEADS):
                sl = slice(h * HEAD_DIM, (h + 1) * HEAD_DIM)
                oh = o_ref[rows, sl]
                on = oh * lax.rsqrt(jnp.mean(oh * oh, axis=1, keepdims=True) + RMS_EPS)
                zh = z_ref[rows, sl].astype(F32)
                yb = on * wo_ref[:, sl] * (zh * _sigmoid(zh))
                ya = yp_ref[rows, sl] * ps_ref[:, sl]
                mixed_ref[rows, sl] = _mx(_sigmoid(ga_ref[rows, sl].astype(F32)) * ya
                                          + _sigmoid(gb_ref[rows, sl].astype(F32)) * yb)
        a1 = ALPHA * h0_ref[...] + _dot(mixed_ref[...], w_ref[...])
        a1_ref[...] = a1
        xhat, _ = _ln_stats(a1)
        h1 = xhat * g_ref[...] + b_ref[...]
        h1_ref[...] = h1
        h1b_ref[...] = _mx(h1)

    def col(blk):
        return pl.BlockSpec((tm, D_MODEL), lambda i: (i, blk))

    r = _row(tm, D_MODEL)
    v = _const((1, D_MODEL))
    return _pc(body, "mix_oproj_ln1", (t // tm,),
               [r, col(0), col(1), col(2), r, v, v,
                _const((D_MODEL, D_MODEL)), r, v, v],
               [r, r, r, r],
               [SDS((t, D_MODEL), MXU_DTYPE), SDS((t, D_MODEL), F32), SDS((t, D_MODEL), F32),
                SDS((t, D_MODEL), MXU_DTYPE)],
               sem=("parallel",))(o, gates, gates, gates, ypre, pool_scale, wo_row, w_out, h0, g1, b1)


def _mix_bwd(da1_bf, w_out, o, gates, ypre, pool_scale, wo_row, tm, after):
    t = o.shape[0]

    def body(da_ref, wout_ref, o_ref, z_ref, ga_ref, gb_ref, yp_ref, ps_ref, wo_ref, after_ref,
             do_ref, dp_ref, dyp_ref, acc_ref, dm_ref):
        i = pl.program_id(0)

        @pl.when(i == 0)
        def _():
            acc_ref[...] = jnp.zeros_like(acc_ref)

        dm_ref[...] = _dot_nt(da_ref[...], wout_ref[...])
        dwo = jnp.zeros((1, HEAD_DIM), F32)
        for h in range(HEADS):
            sl = slice(h * HEAD_DIM, (h + 1) * HEAD_DIM)
            woh = wo_ref[:, sl]
            psh = ps_ref[:, sl]
            dps = jnp.zeros((1, HEAD_DIM), F32)
            for r in range(0, tm, MIX_ROWS):
                rows = pl.ds(r, MIX_ROWS)
                oh = o_ref[rows, sl]
                rs = lax.rsqrt(jnp.mean(oh * oh, axis=1, keepdims=True) + RMS_EPS)
                on = oh * rs
                zh = z_ref[rows, sl].astype(F32)
                sz = _sigmoid(zh)
                silu = zh * sz
                t1 = on * woh
                yb = t1 * silu
                sa = _sigmoid(ga_ref[rows, sl].astype(F32))
                sb = _sigmoid(gb_ref[rows, sl].astype(F32))
                yp = yp_ref[rows, sl]
                dm = dm_ref[rows, sl]
                ga_sl = slice(D_MODEL + h * HEAD_DIM, D_MODEL + (h + 1) * HEAD_DIM)
                gb_sl = slice(2 * D_MODEL + h * HEAD_DIM, 2 * D_MODEL + (h + 1) * HEAD_DIM)
                dp_ref[rows, ga_sl] = _mx(dm * (yp * psh) * sa * (1.0 - sa))
                dp_ref[rows, gb_sl] = _mx(dm * yb * sb * (1.0 - sb))
                dya = dm * sa
                dyb = dm * sb
                dyp_ref[rows, sl] = _mx(dya * psh)
                dps = dps + jnp.sum(dya * yp, axis=0, keepdims=True)
                dp_ref[rows, sl] = _mx(dyb * t1 * (sz * (1.0 + zh * (1.0 - sz))))
                dt1 = dyb * silu
                dwo = dwo + jnp.sum(dt1 * on, axis=0, keepdims=True)
                don = dt1 * woh
                do_ref[rows, sl] = _mx(rs * (don - on * jnp.mean(don * on, axis=1, keepdims=True)))
            acc_ref[0:1, sl] += dps
        acc_ref[1:2, 0:HEAD_DIM] += dwo

    def col(blk):
        return pl.BlockSpec((tm, D_MODEL), lambda i: (i, blk))

    r = _row(tm, D_MODEL)
    return _pc(body, "mix_bwd", (t // tm,),
               [r, _const((D_MODEL, D_MODEL)), r, col(0), col(1), col(2), r,
                _const((1, D_MODEL)), _const((1, D_MODEL)), ANY],
               [r, pl.BlockSpec((tm, 3 * D_MODEL), lambda i: (i, K_Z // (3 * D_MODEL))), r, _const((8, D_MODEL))],
               [SDS((t, D_MODEL), MXU_DTYPE), SDS((t, CAT_WIDTH), MXU_DTYPE), SDS((t, D_MODEL), MXU_DTYPE),
                SDS((8, D_MODEL), F32)],
               scratch=[pltpu.VMEM((tm, D_MODEL), F32)],
               sem=("arbitrary",))(da1_bf, w_out, o, gates, gates, gates, ypre, pool_scale, wo_row, after)


def _mlp_up(h1_bf, w_up, tm):
    t = h1_bf.shape[0]
    tn = w_up.shape[2]

    def body(h_ref, w_ref, act_ref):
        r = jnp.maximum(_dot(h_ref[...], w_ref[...]), 0.0)
        act_ref[...] = _mx(r * r)

    return _pc(body, "mlp_up", (D_FF // tn, t // tm),
               [pl.BlockSpec((tm, D_MODEL), lambda j, i: (i, 0)),
                pl.BlockSpec((None, D_MODEL, tn), lambda j, i: (j, 0, 0))],
               pl.BlockSpec((tm, tn), lambda j, i: (i, j)), SDS((t, D_FF), MXU_DTYPE),
               sem=("parallel", "parallel"))(h1_bf, w_up)


def _tail(act, w_down, h1, w_gate, p, w_proj, tgt, g2, b2, tm):
    t = act.shape[0]

    def body(act_ref, wd_ref, h1_ref, wg_ref, p_ref, wp_ref, tgt_ref, g_ref, b_ref,
             dr_ref, drb_ref, dgp_ref, dpp_ref, rb_ref, acc_ref):
        i = pl.program_id(0)

        @pl.when(i == 0)
        def _():
            acc_ref[...] = jnp.zeros_like(acc_ref)

        r = ALPHA * h1_ref[...] + _dot(act_ref[...], wd_ref[...])
        rb = _mx(r)
        rb_ref[...] = rb
        gate = _sigmoid(_dot(rb, wg_ref[...]))
        pp = _dot(p_ref[...], wp_ref[...])
        xhat, rstd = _ln_stats(r + gate * pp)
        g = g_ref[...]
        diff = xhat * g + b_ref[...] - tgt_ref[...]
        dh2 = diff * (1.0 / D_MODEL)
        rowloss = jnp.sum(diff * diff, axis=1, keepdims=True) * (0.5 / D_MODEL)
        acc_ref[0:1, :] += jnp.sum(dh2 * xhat, axis=0, keepdims=True)
        acc_ref[1:2, :] += jnp.sum(dh2, axis=0, keepdims=True)
        acc_ref[2:3, :] += jnp.broadcast_to(jnp.sum(rowloss, axis=0, keepdims=True), (1, D_MODEL))
        da2 = _ln_bwd(dh2, xhat, rstd, g)
        dpp_ref[...] = _mx(da2 * gate)
        dgp = _mx(da2 * pp * gate * (1.0 - gate))
        dgp_ref[...] = dgp
        dr = da2 + _dot_nt(dgp, wg_ref[...])
        dr_ref[...] = dr
        drb_ref[...] = _mx(dr)

    r = _row(tm, D_MODEL)
    v = _const((1, D_MODEL))
    return _pc(body, "tail", (t // tm,),
               [_row(tm, D_FF), _const((D_FF, D_MODEL)), r, _const((D_MODEL, D_MODEL)), _row(tm, PLE_DIM),
                _const((PLE_DIM, D_MODEL)), r, v, v],
               [r, r, r, r, r, _const((8, D_MODEL))],
               [SDS((t, D_MODEL), F32)] + [SDS((t, D_MODEL), MXU_DTYPE)] * 4 + [SDS((8, D_MODEL), F32)],
               sem=("arbitrary",))(act, w_down, h1, w_gate, p, w_proj, tgt, g2, b2)


SQRT_GUARD = 1e-30


def _mlp_bwd1(dr_bf, w_down, act, tm, tn):
    t = act.shape[0]

    def body(dr_ref, w_ref, act_ref, dup_ref):
        dact = _dot_nt(dr_ref[...], w_ref[...])
        a = act_ref[...].astype(F32)
        dup_ref[...] = _mx(dact * (2.0 * a * lax.rsqrt(a + SQRT_GUARD)))

    o = pl.BlockSpec((tm, tn), lambda j, i: (i, j))
    return _pc(body, "mlp_bwd1", (D_FF // tn, t // tm),
               [pl.BlockSpec((tm, D_MODEL), lambda j, i: (i, 0)), pl.BlockSpec((tn, D_MODEL), lambda j, i: (j, 0)), o],
               o, SDS((t, D_FF), MXU_DTYPE), sem=("parallel", "parallel"))(dr_bf, w_down, act)


def _mlp_bwd2(dup, w_up, dr, a1, g1, tm):
    t = dr.shape[0]

    nk, tk = w_up.shape[0], w_up.shape[2]

    def body(dup_ref, w_ref, dr_ref, a1_ref, g_ref, da1_ref, da1b_ref, acc_ref):
        i = pl.program_id(0)

        @pl.when(i == 0)
        def _():
            acc_ref[...] = jnp.zeros_like(acc_ref)

        dh1 = ALPHA * dr_ref[...]
        for kk in range(nk):
            dh1 = dh1 + _dot_nt(dup_ref[:, kk * tk:(kk + 1) * tk], w_ref[kk])
        xhat, rstd = _ln_stats(a1_ref[...])
        acc_ref[0:1, :] += jnp.sum(dh1 * xhat, axis=0, keepdims=True)
        acc_ref[1:2, :] += jnp.sum(dh1, axis=0, keepdims=True)
        da1 = _ln_bwd(dh1, xhat, rstd, g_ref[...])
        da1_ref[...] = da1
        da1b_ref[...] = _mx(da1)

    r = _row(tm, D_MODEL)
    return _pc(body, "mlp_bwd2", (t // tm,),
               [_row(tm, D_FF), _const((nk, D_MODEL, tk)), r, r, _const((1, D_MODEL))],
               [r, r, _const((8, D_MODEL))],
               [SDS((t, D_MODEL), F32), SDS((t, D_MODEL), MXU_DTYPE), SDS((8, D_MODEL), F32)],
               sem=("arbitrary",))(dup, w_up, dr, a1, g1)


def _ln_in_bwd(dproj, w_cat, da1, x, g, tm, after):
    t = x.shape[0]

    def body(dp_ref, w_ref, da1_ref, x_ref, g_ref, after_ref, dx_ref, acc_ref):
        i = pl.program_id(0)

        @pl.when(i == 0)
        def _():
            acc_ref[...] = jnp.zeros_like(acc_ref)

        dh0 = _dot_nt(dp_ref[...], w_ref[...]) + ALPHA * da1_ref[...]
        xhat, rstd = _ln_stats(x_ref[...])
        acc_ref[0:1, :] += jnp.sum(dh0 * xhat, axis=0, keepdims=True)
        acc_ref[1:2, :] += jnp.sum(dh0, axis=0, keepdims=True)
        dx_ref[...] = _ln_bwd(dh0, xhat, rstd, g_ref[...])

    r = _row(tm, D_MODEL)
    return _pc(body, "ln_in_bwd", (t // tm,),
               [_row(tm, CAT_WIDTH), _const((D_MODEL, CAT_WIDTH)), r, r, _const((1, D_MODEL)), ANY],
               [r, _const((8, D_MODEL))], [SDS((t, D_MODEL), F32), SDS((8, D_MODEL), F32)],
               sem=("arbitrary",))(dproj, w_cat, da1, x, g, after)


def _local_step(x, p, tgt, wts, start_token, first_weights, late_weights, send_late_grads, send_early_grads):
    t = x.shape[0]
    tm = min(512, t)
    tms = min(256, t)
    row = lambda a: a.reshape(1, -1)
    pool_scale = row(wts["pool_scale"])
    wo_row = jnp.tile(row(wts["o_norm_w"]), (1, HEADS))
    pad8 = jnp.zeros((1, HEADS), F32)
    al_row = jnp.concatenate([pad8, row(wts["a_log"]), jnp.zeros((1, 128 - 2 * HEADS), F32)], axis=1)
    dtb_row = jnp.concatenate([pad8, row(wts["dt_bias"]), jnp.zeros((1, 128 - 2 * HEADS), F32)], axis=1)
    g_in, b_in = row(wts["ln_in_g"]), row(wts["ln_in_b"])
    g1, b1 = row(wts["ln1_g"]), row(wts["ln1_b"])
    g2, b2 = row(wts["ln2_g"]), row(wts["ln2_b"])

    h0, h0_bf = _ln_in(x, g_in, b_in, tm, start_token)
    first, first_token = first_weights(h0_bf)
    wts = {**wts, **first}
    w_cat = wts["w_cat"]
    proj, gates, qkv_act, dsilu = _proj_conv(h0_bf, w_cat, wts["conv_w"], tms, first_token)
    ypre, d_bf = _pool_fwd(proj, wts["pool_w"], tm)
    bg = _ba_fwd(proj, al_row, dtb_row, tm)
    bgt = bg[:, :2 * HEADS].T
    o, u, w, qg, kg, attn, ymat, vn, states, egl = _dn_fwd(qkv_act, bg, bgt)
    wts = {**wts, **late_weights(o)}
    mixed, a1, h1, h1_bf = _mix_oproj_ln1(o, gates, ypre, pool_scale, wo_row, wts["w_out"], h0, g1, b1, tm)
    act = _mlp_up(h1_bf, wts["w_up"], tm)
    dr, dr_bf, dgp, dpp, r_bf, acc_tail = _tail(act, wts["w_down"], h1, wts["ple_gate_w"], p, wts["ple_proj_w"],
                                                tgt, g2, b2, tms)
    grads = {}
    grads["ple_proj_w"] = _matmul(p, dpp, "tn", "dw_ple_proj", WIRE_DTYPE, tm=256, tn=1024, tk=DW_TK)
    grads["ple_gate_w"] = _matmul(r_bf, dgp, "tn", "dw_ple_gate", WIRE_DTYPE, tm=DW_TM, tn=1024, tk=DW_TK)
    grads["w_down"] = _matmul(act, dr_bf, "tn", "dw_down", WIRE_DTYPE, tm=DW_TM, tn=1024, tk=DW_TK)
    dup = _mlp_bwd1(dr_bf, wts["w_down"], act, tm, 1024)
    grads["w_up"] = _matmul(h1_bf, dup, "tn", "dw_up", WIRE_DTYPE, tm=DW_TM, tn=1024, tk=DW_TK, stack_out=True)
    da1, da1_bf, acc_ln1 = _mlp_bwd2(dup, wts["w_up"], dr, a1, g1, tms)
    grads["w_out"] = _matmul(mixed, da1_bf, "tn", "dw_out", WIRE_DTYPE, tm=DW_TM, tn=1024, tk=DW_TK)
    sent = send_late_grads(grads)
    do, dproj, dyp, acc_mix = _mix_bwd(da1_bf, wts["w_out"], o, gates, ypre, pool_scale, wo_row, tm, sent)
    dproj, grads["pool_w"] = _pool_bwd(dyp, d_bf, wts["pool_w"], dproj, tm)
    dqkv_act, dbg = _dn_bwd(do, qkv_act, bg, bgt, u, w, qg, kg, attn, ymat, vn, states, egl)
    dproj, acc_conv = _conv_bwd(dqkv_act, dsilu, proj, wts["conv_w"], dproj, tm)
    dproj, acc_ba = _ba_bwd(dbg, bg, proj, al_row, dtb_row, dproj, tm)
    dw_cat = _matmul(h0_bf, dproj, "tn", "dw_in", WIRE_DTYPE, tm=DW_TM, tn=1152, tk=DW_TK)
    grads["w_in"] = _w_in_by_chip(dw_cat)
    sent = send_early_grads(grads)
    grad_x, acc_in = _ln_in_bwd(dproj, w_cat, da1, x, g_in, tms, sent)

    grads["conv_w"] = acc_conv[0:CONV_K]
    grads["ln_in_g"], grads["ln_in_b"] = acc_in[0], acc_in[1]
    grads["ln1_g"], grads["ln1_b"] = acc_ln1[0], acc_ln1[1]
    grads["ln2_g"], grads["ln2_b"] = acc_tail[0], acc_tail[1]
    grads["pool_scale"] = acc_mix[0]
    grads["o_norm_w"] = acc_mix[1, 0:HEAD_DIM]
    grads["a_log"] = acc_ba[0, HEADS:2 * HEADS]
    grads["dt_bias"] = acc_ba[1, HEADS:2 * HEADS]
    loss = acc_tail[2, 0]
    return grad_x, grads, loss


MESH = pl.DeviceIdType.MESH
ANY = pl.BlockSpec(memory_space=pl.ANY)


def _chip_of(k, x, y):
    chip = (2 * x + y + k) % N_CHIPS
    return chip // 2, chip % 2


def _place():
    x, y, c = lax.axis_index("x"), lax.axis_index("y"), lax.axis_index("c")
    return x, y, c, 2 * x + y


def _half(rows, c):
    return pl.ds(pl.multiple_of(c * (rows // 2), 16), rows // 2)


def _remote(src, dst, send_sem, recv_sem, device_id):
    return pltpu.make_async_remote_copy(src_ref=src, dst_ref=dst, send_sem=send_sem, recv_sem=recv_sem,
                                        device_id=device_id, device_id_type=MESH)


def _tile_rows(rows):
    for tr in (256, 128, 64, 32, 16):
        if rows % tr == 0:
            return tr
    raise ValueError(rows)


def _first_gather_copies(srcs, lands, send, recv, place):
    copies = []
    for a in range(len(srcs)):
        whole = a == len(srcs) - 1
        for k in range(N_CHIPS):
            if place is None:
                copies.append(None)
                continue
            x, y, c, me = place
            sems = (send.at[a * N_CHIPS + k], recv.at[a * N_CHIPS + k])
            if k == 0:
                copies.append(_remote(srcs[a], lands[a].at[me], *sems, (x, y, 1 - c)))
                continue
            tx, ty = _chip_of(k, x, y)
            if whole:
                copies.append(_remote(srcs[a], lands[a].at[me], *sems, (tx, ty, c)))
            else:
                mine = _half(srcs[a].shape[0], c)
                copies.append(_remote(srcs[a].at[mine], lands[a].at[me, mine], *sems, (tx, ty, c)))
    return copies


def _pass_halves(stacks):
    n = len(stacks)

    def body(*refs):
        outs = refs[n:2 * n]
        send, recv = refs[2 * n:]
        x, y, c, me = _place()
        copies = []
        for a in range(n):
            for k in range(1, N_CHIPS):
                landed = outs[a].at[(me + N_CHIPS - k) % N_CHIPS, _half(stacks[a].shape[1], c)]
                copies.append(_remote(landed, landed, send.at[a * N_CHIPS + k], recv.at[a * N_CHIPS + k],
                                      (x, y, 1 - c)))
        for cp in copies:
            cp.start()
        for cp in copies:
            cp.wait_send()
        for a in range(n):
            for k in range(1, N_CHIPS):
                passed = outs[a].at[(me + N_CHIPS - k) % N_CHIPS, _half(stacks[a].shape[1], 1 - c)]
                _remote(passed, passed, send.at[a * N_CHIPS + k], recv.at[a * N_CHIPS + k], (x, y, c)).wait_recv()

    sems = pltpu.SemaphoreType.DMA((n * N_CHIPS,))
    return pl.pallas_call(
        body, name="pass_halves", out_shape=[SDS(s.shape, s.dtype) for s in stacks],
        in_specs=[ANY] * n, out_specs=[ANY] * n, scratch_shapes=[sems, sems],
        input_output_aliases={a: a for a in range(n)},
    )(*stacks)


def _swap_halves(gs):
    n = len(gs)

    def body(*refs):
        ins, theirs = refs[0:n], refs[n:2 * n]
        send, recv = refs[2 * n:]
        x, y, c, _ = _place()
        copies = [_remote(ins[a].at[:, _half(gs[a].shape[1], 1 - c)], theirs[a], send.at[a], recv.at[a],
                          (x, y, 1 - c)) for a in range(n)]
        for cp in copies:
            cp.start()
        for cp in copies:
            cp.wait()

    return pl.pallas_call(
        body, name="swap_halves", out_shape=[SDS((N_CHIPS, g.shape[1] // 2, g.shape[2]), g.dtype) for g in gs],
        in_specs=[ANY] * n, out_specs=[ANY] * n, scratch_shapes=[pltpu.SemaphoreType.DMA((n,))] * 2,
    )(*gs)


def _send_to_sibling(hs):
    n = len(hs)

    def body(*refs):
        ins, outs = refs[0:n], refs[n:2 * n]
        send, recv = refs[2 * n:]
        x, y, c, _ = _place()
        copies = [_remote(ins[a], outs[a], send.at[a], recv.at[a], (x, y, 1 - c)) for a in range(n)]
        for cp in copies:
            cp.start()
        for cp in copies:
            cp.wait()

    return pl.pallas_call(
        body, name="send_to_sibling", out_shape=[SDS(h.shape, h.dtype) for h in hs],
        in_specs=[ANY] * n, out_specs=[ANY] * n, scratch_shapes=[pltpu.SemaphoreType.DMA((n,))] * 2,
    )(*hs)


HBM = pl.BlockSpec(memory_space=pltpu.HBM)
SEM = pl.BlockSpec(memory_space=pltpu.SEMAPHORE)
EFFECT = pltpu.SideEffectType.DATAFLOW_SIDE_EFFECTING


def _in_hbm(a):
    return pltpu.with_memory_space_constraint(a, pltpu.HBM)


def _split_copy_start(name, srcs, lands, copies_of, after):
    n = len(srcs)
    n_copies = len(copies_of(srcs, lands, None, None, None))

    def body(*refs):
        src_refs, land_refs = refs[0:n], refs[n:2 * n]
        send, recv = refs[2 * n + 1], refs[2 * n + 2]
        token = refs[-1]
        for cp in copies_of(src_refs, land_refs, send, recv, _place()):
            cp.start()
        token[...] = jnp.zeros_like(token)

    sems = pltpu.SemaphoreType.DMA((n_copies,))
    out = pl.pallas_call(
        body, name=name,
        out_shape=[sems, sems] + [pltpu.HBM(a.shape, a.dtype) for a in list(srcs) + list(lands)] + [SDS((8, 128), F32)],
        in_specs=[HBM] * (2 * n) + [ANY],
        out_specs=[SEM, SEM] + [HBM] * (2 * n) + [pl.BlockSpec(memory_space=pltpu.VMEM)],
        input_output_aliases={i: 2 + i for i in range(2 * n)},
        compiler_params=pltpu.CompilerParams(has_side_effects=EFFECT),
    )(*[_in_hbm(a) for a in list(srcs) + list(lands)], after)
    return out[0], out[1], out[2:2 + n], out[2 + n:2 + 2 * n], out[-1]


def _split_copy_wait(name, send, recv, srcs, lands, after, copies_of):
    n = len(srcs)
    after = list(after) if isinstance(after, (list, tuple)) else [after]

    def body(*refs):
        src_refs, land_refs = refs[0:n], refs[n:2 * n]
        send_ref, recv_ref = refs[2 * n], refs[2 * n + 1]
        for cp in copies_of(src_refs, land_refs, send_ref, recv_ref, _place()):
            cp.wait_send()
            cp.wait_recv()

    out = pl.pallas_call(
        body, name=name, out_shape=[pltpu.HBM(a.shape, a.dtype) for a in list(srcs) + list(lands)],
        in_specs=[HBM] * (2 * n) + [SEM, SEM] + [ANY] * len(after), out_specs=[HBM] * (2 * n),
        input_output_aliases={i: i for i in range(2 * n)},
        compiler_params=pltpu.CompilerParams(has_side_effects=EFFECT),
    )(*srcs, *lands, send, recv, *after)
    return out[0:n], out[n:2 * n]


def _late_gather_copies(srcs, lands, send, recv, place):
    copies = []
    for a in range(len(srcs)):
        for k in range(N_CHIPS):
            if place is None:
                copies.append(None)
                continue
            x, y, c, me = place
            if k == 0:
                target = (x, y, 1 - c)
            else:
                tx, ty = _chip_of(k, x, y)
                target = (tx, ty, c)
            copies.append(_remote(srcs[a], lands[a].at[me], send.at[a * N_CHIPS + k], recv.at[a * N_CHIPS + k], target))
    return copies


def _late_scatter_copies(srcs, lands, send, recv, place):
    copies = []
    for a in range(len(srcs)):
        for k in range(1, N_CHIPS):
            if place is None:
                copies.append(None)
                continue
            x, y, c, _ = place
            tx, ty = _chip_of(k, x, y)
            copies.append(_remote(srcs[a].at[2 * tx + ty], lands[a].at[k - 1], send.at[a * (N_CHIPS - 1) + k - 1],
                                  recv.at[a * (N_CHIPS - 1) + k - 1], (tx, ty, c)))
    return copies


def _add_pair(g, theirs, name):
    _, rows, cols = g.shape
    half = rows // 2
    tr = _tile_rows(half)

    def body(g_ref, t_ref, o_ref):
        own = g_ref[lax.axis_index("c")]
        o_ref[...] = (own.astype(F32) + t_ref[...].astype(F32)).astype(o_ref.dtype)

    blk = pl.BlockSpec((None, tr, cols), lambda j, i: (j, i, 0))
    return _pc(body, "add_" + name, (N_CHIPS, half // tr),
               [pl.BlockSpec((None, 2, tr, cols), lambda j, i: (j, 0, i, 0)), blk], blk,
               SDS((N_CHIPS, half, cols), g.dtype), sem=("parallel", "parallel"))(
                   g.reshape(N_CHIPS, 2, half, cols), theirs)


def _sum_slabs(pair, landed, name):
    _, rows, cols = pair.shape
    tr = _tile_rows(rows)

    def body(p_ref, r_ref, o_ref):
        acc = p_ref[2 * lax.axis_index("x") + lax.axis_index("y")].astype(F32)
        for k in range(N_CHIPS - 1):
            acc = acc + r_ref[k].astype(F32)
        o_ref[...] = acc

    return _pc(body, "sum_" + name, (rows // tr,),
               [pl.BlockSpec((N_CHIPS, tr, cols), lambda i: (0, i, 0)),
                pl.BlockSpec((N_CHIPS - 1, tr, cols), lambda i: (0, i, 0))],
               _row(tr, cols), SDS((rows, cols), F32), sem=("parallel",))(pair, landed)


def _adamw_math(w, g, m, v):
    m = ADAM_B1 * m + (1.0 - ADAM_B1) * g
    v = ADAM_B2 * v + (1.0 - ADAM_B2) * (g * g)
    m_hat = m / (1.0 - ADAM_B1 ** ADAM_STEP)
    v_hat = v / (1.0 - ADAM_B2 ** ADAM_STEP)
    delta = -ADAM_LR * (m_hat / (jnp.sqrt(v_hat) + ADAM_EPS) + ADAM_WD * w)
    return delta, m, v


def _adamw_2d(w, g_own, g_sib, m, v, name, halves):
    lead = w.ndim == 3
    rows, cols = w.shape[-2:]
    tr = _tile_rows(rows // 2)
    nh = rows // 2 // tr if halves else rows // tr

    def body(w_ref, go_ref, gs_ref, m_ref, v_ref, g_out, d_out, m_out, v_out):
        if halves:
            mine = (pl.program_id(0) // nh) == lax.axis_index("c")
            g = jnp.where(mine, go_ref[...], gs_ref[...])
        else:
            g = go_ref[...] + gs_ref[...]
        delta, mn, vn = _adamw_math(w_ref[...], g, m_ref[...], v_ref[...])
        g_out[...] = g
        d_out[...] = delta
        m_out[...] = mn
        v_out[...] = vn

    r = _row(tr, cols)
    p = pl.BlockSpec((None, tr, cols), lambda i: (0, i, 0)) if lead else r
    h = pl.BlockSpec((tr, cols), lambda i: (i % nh, 0))
    return _pc(body, "adamw_" + name, (rows // tr,), [p, h, h, p, p], [r] * 4, [SDS((rows, cols), F32)] * 4,
               sem=("parallel",))(w, g_own, g_sib, m, v)


def _small_allreduce_adamw(mine, w, m, v, sizes):
    shape = mine.shape
    n = len(sizes)

    def body(mine_ref, w_ref, m_ref, v_ref, *rest):
        outs, (buf_ref, res_ref, send_sems, recv_sems) = rest[:-4], rest[-4:]
        x, y, c = lax.axis_index("x"), lax.axis_index("y"), lax.axis_index("c")
        me = 4 * x + 2 * y + c
        buf_ref[me] = mine_ref[...]
        copies = []
        for k in range(1, N_DEV):
            tgt = (me + k) % N_DEV
            copies.append(pltpu.make_async_remote_copy(
                src_ref=mine_ref, dst_ref=buf_ref.at[me], send_sem=send_sems.at[k], recv_sem=recv_sems.at[k],
                device_id=(tgt // 4, (tgt // 2) % 2, tgt % 2), device_id_type=MESH))
        for cp in copies:
            cp.start()
        for k in range(1, N_DEV):
            src = (me + N_DEV - k) % N_DEV
            pltpu.make_async_remote_copy(
                src_ref=mine_ref, dst_ref=buf_ref.at[src], send_sem=send_sems.at[k], recv_sem=recv_sems.at[k],
                device_id=(x, y, c), device_id_type=MESH).wait_recv()
        for cp in copies:
            cp.wait_send()
        g = buf_ref[0]
        for j in range(1, N_DEV):
            g = g + buf_ref[j]
        delta, mn, vn = _adamw_math(w_ref[...], g, m_ref[...], v_ref[...])
        for kind, val in enumerate((g, delta, mn, vn)):
            res_ref[kind] = val
            for i, size in enumerate(sizes):
                outs[kind * (n + 1) + i][...] = res_ref[kind, i:i + 1, 0:size]
            outs[kind * (n + 1) + n][...] = res_ref[kind, SMALL_CONV_AT:SMALL_CONV_AT + SMALL_CONV_ROWS, :]
        outs[-1][...] = res_ref[0, n:n + 1, 0:1]

    vm = pl.BlockSpec(memory_space=pltpu.VMEM)
    per_kind = [SDS((1, size), F32) for size in sizes] + [SDS((SMALL_CONV_ROWS, D_MODEL), F32)]
    out_shape = per_kind * 4 + [SDS((1, 1), F32)]
    out = pl.pallas_call(
        body, name="small_allreduce_adamw", out_shape=out_shape, in_specs=[vm] * 4, out_specs=[vm] * len(out_shape),
        scratch_shapes=[pltpu.VMEM((N_DEV,) + shape, F32), pltpu.VMEM((4,) + shape, F32),
                        pltpu.SemaphoreType.DMA((N_DEV,)), pltpu.SemaphoreType.DMA((N_DEV,))],
    )(mine, w, m, v)
    return [out[kind * (n + 1):(kind + 1) * (n + 1)] for kind in range(4)], out[-1]


def _as2d(a):
    return a.reshape(-1, a.shape[-1])


SEGMENTS = ((C_POOL, K_U, POOL_WIDTH), (C_QKV, K_QKV, QKV_WIDTH), (C_Z, K_Z, DN_WIDTH), (C_BETA, K_BA, 2 * HEADS),
            (C_GA, K_GA, D_MODEL), (C_GB, K_GB, D_MODEL))
SHARD_COLS = IN_WIDTH // N_CHIPS


def _w_cat(stack):
    pieces = []
    for c0, _, width in sorted(SEGMENTS, key=lambda seg: seg[1]):
        a = c0
        while a < c0 + width:
            chip = a // SHARD_COLS
            b = min(c0 + width, (chip + 1) * SHARD_COLS)
            pieces.append(stack[chip][:, a - chip * SHARD_COLS:b - chip * SHARD_COLS])
            a = b
    pieces.append(jnp.zeros((D_MODEL, CAT_WIDTH - K_BA - 2 * HEADS), stack.dtype))
    return jnp.concatenate(pieces, axis=1)


def _w_in_by_chip(dw_cat):
    slabs = []
    for chip in range(N_CHIPS):
        lo, hi = chip * SHARD_COLS, (chip + 1) * SHARD_COLS
        pieces = []
        for c0, k0, width in sorted(SEGMENTS):
            a, b = max(c0, lo), min(c0 + width, hi)
            if a < b:
                pieces.append(dw_cat[:, k0 + a - c0:k0 + b - c0])
        slabs.append(jnp.concatenate(pieces, axis=1))
    return jnp.stack(slabs)


WEIGHT_LAYOUT = {
    "w_in": lambda s: ("w_cat", _w_cat(s)),
    "pool_w": lambda s: ("pool_w", s.reshape(N_CHIPS, 4, POOL_GROUP, POOL_OUT_GROUP // N_CHIPS)
                         .transpose(1, 2, 0, 3).reshape(4, POOL_GROUP, POOL_OUT_GROUP)),
    "w_out": lambda s: ("w_out", s.reshape(D_MODEL, D_MODEL)),
    "w_up": lambda s: ("w_up", s),
    "w_down": lambda s: ("w_down", s.reshape(D_FF, D_MODEL)),
    "ple_gate_w": lambda s: ("ple_gate_w", s.reshape(D_MODEL, D_MODEL)),
    "ple_proj_w": lambda s: ("ple_proj_w", s.transpose(1, 0, 2).reshape(PLE_DIM, D_MODEL)),
}

GRAD_LAYOUT = {
    "w_in": lambda g: g,
    "pool_w": lambda g: g.reshape(4, POOL_GROUP, N_CHIPS, POOL_OUT_GROUP // N_CHIPS)
                         .transpose(2, 0, 1, 3).reshape(N_CHIPS, 4 * POOL_GROUP, POOL_OUT_GROUP // N_CHIPS),
    "w_out": lambda g: g.reshape(N_CHIPS, D_MODEL // N_CHIPS, D_MODEL),
    "w_up": lambda g: g,
    "w_down": lambda g: g.reshape(N_CHIPS, D_FF // N_CHIPS, D_MODEL),
    "ple_gate_w": lambda g: g.reshape(N_CHIPS, D_MODEL // N_CHIPS, D_MODEL),
    "ple_proj_w": lambda g: g.reshape(PLE_DIM, N_CHIPS, D_MODEL // N_CHIPS).transpose(1, 0, 2),
}


def _full_weights(names, stacks):
    return dict(WEIGHT_LAYOUT[n](s.astype(MXU_DTYPE)) for n, s in zip(names, stacks))


def _grads_by_chip(names, grads):
    return [GRAD_LAYOUT[n](grads[n]).astype(WIRE_DTYPE) for n in names]


def _pack_small(rows, conv, name):
    n = len(rows)

    def body(*refs):
        out = refs[n + 1]
        out[...] = jnp.zeros_like(out)
        for i in range(n):
            out[i:i + 1, :] = refs[i][...]
        out[SMALL_CONV_AT:SMALL_CONV_AT + SMALL_CONV_ROWS, :] = refs[n][...]

    vm = pl.BlockSpec(memory_space=pltpu.VMEM)
    return pl.pallas_call(body, name=name, out_shape=SDS((SMALL_CONV_AT + SMALL_CONV_ROWS, D_MODEL), F32),
                          in_specs=[vm] * (n + 1), out_specs=vm)(*rows, conv)


def _pad_row(a):
    a = a.reshape(1, -1).astype(F32)
    return jnp.pad(a, ((0, 0), (0, D_MODEL - a.shape[1])))


def kernel(x, p, ln_in_g, ln_in_b, w_in, pool_w, pool_scale, conv_w, a_log, dt_bias, o_norm_w, w_out, ln1_g, ln1_b, w_up, w_down, ple_gate_w, ple_proj_w, ln2_g, ln2_b, loss_target, m_ln_in_g, m_ln_in_b, m_w_in, m_pool_w, m_pool_scale, m_conv_w, m_a_log, m_dt_bias, m_o_norm_w, m_w_out, m_ln1_g, m_ln1_b, m_w_up, m_w_down, m_ple_gate_w, m_ple_proj_w, m_ln2_g, m_ln2_b, v_ln_in_g, v_ln_in_b, v_w_in, v_pool_w, v_pool_scale, v_conv_w, v_a_log, v_dt_bias, v_o_norm_w, v_w_out, v_ln1_g, v_ln1_b, v_w_up, v_w_down, v_ple_gate_w, v_ple_proj_w, v_ln2_g, v_ln2_b):
    given = dict(locals())
    chip = 2 * lax.axis_index("x") + lax.axis_index("y")

    shard = lambda n: _as2d(given[n]).astype(WIRE_DTYPE)

    wts = {"ln_in_g": ln_in_g, "ln_in_b": ln_in_b, "pool_scale": pool_scale[0], "a_log": a_log[0],
           "dt_bias": dt_bias[0], "o_norm_w": o_norm_w[0], "ln1_g": ln1_g[0], "ln1_b": ln1_b[0],
           "ln2_g": ln2_g[0], "ln2_b": ln2_b[0]}

    conv_pad = jnp.pad(conv_w[0], ((0, 8 - CONV_K), (0, 0)))
    first_srcs = [shard(n) for n in EARLY] + [conv_pad]
    first_lands = [lax.empty((N_CHIPS,) + s.shape, s.dtype) for s in first_srcs]
    fsend, frecv, fsrcs, flands, start_token = _split_copy_start(
        "first_gather_start", first_srcs, first_lands, _first_gather_copies, first_srcs[0])
    late = {}
    for n in ("w_in", "m_w_in", "v_w_in"):
        given[n], _ = lax.optimization_barrier((given[n], start_token))

    def first_weights(after):
        _, lands = _split_copy_wait("first_gather_wait", fsend, frecv, fsrcs, flands,
                                    [after, given["w_in"], given["m_w_in"], given["v_w_in"]], _first_gather_copies)
        stacks = _pass_halves(lands[0:len(EARLY)])
        first = _full_weights(EARLY, stacks)
        first["conv_w"] = jnp.concatenate([lands[len(EARLY)][j, 0:CONV_K] for j in range(N_CHIPS)], axis=1)
        late_srcs = [shard(n) for n in LATE]
        late_lands = [lax.empty((N_CHIPS,) + s.shape, s.dtype) for s in late_srcs]
        late["send"], late["recv"], late["srcs"], late["lands"], token = _split_copy_start(
            "late_gather_start", late_srcs, late_lands, _late_gather_copies, stacks[0])
        return first, token

    def late_weights(after):
        _, stacks = _split_copy_wait("late_gather_wait", late["send"], late["recv"], late["srcs"], late["lands"],
                                     after, _late_gather_copies)
        return _full_weights(LATE, stacks)

    scatter = {}

    def send_late_grads(grads):
        srcs = _grads_by_chip(LATE, grads)
        lands = [lax.empty((N_CHIPS - 1,) + g.shape[1:], g.dtype) for g in srcs]
        scatter["send"], scatter["recv"], scatter["srcs"], scatter["lands"], token = _split_copy_start(
            "late_scatter_start", srcs, lands, _late_scatter_copies, srcs[0])
        return token

    last = {}

    def send_early_grads(grads):
        by_chip = _grads_by_chip(EARLY, grads)
        theirs = _swap_halves(by_chip)
        pair = [_add_pair(g, t, n) for g, t, n in zip(by_chip, theirs, EARLY)]
        lands = [lax.empty((N_CHIPS - 1,) + q.shape[1:], q.dtype) for q in pair]
        last["send"], last["recv"], last["srcs"], last["lands"], token = _split_copy_start(
            "early_scatter_start", pair, lands, _late_scatter_copies, pair[0])
        return token

    grad_x, grads, loss = _local_step(x[0], p[0, 0], loss_target[0], wts, start_token, first_weights, late_weights,
                                      send_late_grads, send_early_grads)

    late_mine, late_landed = _split_copy_wait("late_scatter_wait", scatter["send"], scatter["recv"], scatter["srcs"],
                                              scatter["lands"], grad_x, _late_scatter_copies)
    late_part = [_sum_slabs(q, r, n) for q, r, n in zip(late_mine, late_landed, LATE)]
    pair, landed = _split_copy_wait("early_scatter_wait", last["send"], last["recv"], last["srcs"], last["lands"],
                                    grad_x, _late_scatter_copies)
    reduced = [_sum_slabs(q, r, n) for q, r, n in zip(pair, landed, EARLY)]
    from_sibling = _send_to_sibling(reduced + late_part)
    big_out = {}
    for n, g_own, g_sib in zip(EARLY + LATE, reduced + late_part, from_sibling):
        view = (lambda a: a) if given[n].ndim == 3 else _as2d
        res = _adamw_2d(view(given[n]), g_own, g_sib, view(given["m_" + n]), view(given["v_" + n]), n,
                        halves=n in EARLY)
        big_out[n] = [r.reshape(given[n].shape) for r in res]

    conv_cols = QKV_WIDTH // N_CHIPS

    def small_pack(get, conv, extra, name):
        if conv.shape[1] != QKV_WIDTH:
            conv = lax.dynamic_update_slice(jnp.zeros((CONV_K, QKV_WIDTH), F32), conv, (0, chip * conv_cols))
        return _pack_small([_pad_row(get(n)) for n in SMALL_NAMES] + extra, conv.reshape(SMALL_CONV_ROWS, D_MODEL), name)

    mine_small = small_pack(lambda n: grads[n], grads["conv_w"], [jnp.full((1, D_MODEL), loss, F32)], "pack_small_g")
    packed_small = [small_pack(lambda n: given[prefix + n], given[prefix + "conv_w"][0], [], "pack_small_" + tag)
                    for prefix, tag in (("", "w"), ("m_", "m"), ("v_", "v"))]
    small_out, loss_sum = _small_allreduce_adamw(mine_small, *packed_small, [given[n].size for n in SMALL_NAMES])

    def small_get(k, n):
        if n == "conv_w":
            full = small_out[k][len(SMALL_NAMES)].reshape(CONV_K, QKV_WIDTH)
            return lax.dynamic_slice(full, (0, chip * conv_cols), (CONV_K, conv_cols)).reshape(given[n].shape)
        return small_out[k][SMALL_NAMES.index(n)].reshape(given[n].shape)

    order = ["ln_in_g", "ln_in_b", "w_in", "pool_w", "pool_scale", "conv_w", "a_log", "dt_bias", "o_norm_w", "w_out",
             "ln1_g", "ln1_b", "w_up", "w_down", "ple_gate_w", "ple_proj_w", "ln2_g", "ln2_b"]
    outs = [loss_sum.reshape(()), grad_x[None]]
    for k in range(4):
        for n in order:
            outs.append(big_out[n][k] if n in big_out else small_get(k, n))
    return tuple(outs)
```

```python
import jax
import jax.numpy as jnp
from jax import lax
from jax.experimental import pallas as pl
from jax.experimental.pallas import tpu as pltpu

F32 = jnp.float32
MXU_DTYPE = jnp.bfloat16
WIRE_DTYPE = jnp.bfloat16
SDS = jax.ShapeDtypeStruct

D_MODEL = 1024
POOL_WINDOWS = (2, 4, 8, 16)
POOL_WIDTH = 512
POOL_GROUP = 128
POOL_OUT_GROUP = 256
HEADS = 8
HEAD_DIM = 128
DN_WIDTH = HEADS * HEAD_DIM
QKV_WIDTH = 3 * DN_WIDTH
CONV_K = 4
CHUNK = 128
DN_BWD_CHUNKS = 2
DN_FWD_CHUNKS = 2
DW_TK = 2048
DW_TM = 1024
D_FF = 4096
PLE_DIM = 256
LN_EPS = 1e-5
RMS_EPS = 1e-6
L2_EPS = 1e-6
ALPHA = 2.0 ** 0.25
Q_SCALE = HEAD_DIM ** -0.5
IN_WIDTH = 6672
C_POOL, C_QKV, C_Z, C_BETA, C_A, C_GA, C_GB = 0, 512, 3584, 4608, 4616, 4624, 5648
K_QKV, K_Z, K_GA, K_GB, K_U, K_BA, CAT_WIDTH = 0, 3072, 4096, 5120, 6144, 6656, 6912
P_U, P_BA, PROJ_F32_WIDTH = 3072, 3584, 3840

ADAM_LR, ADAM_B1, ADAM_B2, ADAM_EPS, ADAM_WD, ADAM_STEP = 0.001, 0.9, 0.999, 1e-08, 0.01, 10

N_CHIPS = 4
N_DEV = 8
VMEM_LIMIT = 56 * 1024 * 1024

EARLY = ("w_in", "pool_w")
LATE = ("w_out", "w_up", "w_down", "ple_gate_w", "ple_proj_w")
SMALL_NAMES = ("ln_in_g", "ln_in_b", "pool_scale", "ln1_g", "ln1_b", "ln2_g", "ln2_b", "o_norm_w", "a_log", "dt_bias")
SMALL_CONV_AT = 12
SMALL_CONV_ROWS = CONV_K * QKV_WIDTH // D_MODEL


def _mx(a):
    return a.astype(MXU_DTYPE)


def _dot(a, b):
    return lax.dot_general(_mx(a), _mx(b), (((1,), (0,)), ((), ())), preferred_element_type=F32)


def _dot_nt(a, b):
    return lax.dot_general(_mx(a), _mx(b), (((1,), (1,)), ((), ())), preferred_element_type=F32)


def _dot_tn(a, b):
    return lax.dot_general(_mx(a), _mx(b), (((0,), (0,)), ((), ())), preferred_element_type=F32)


def _sigmoid(x):
    return 0.5 * jnp.tanh(0.5 * x) + 0.5


def _softplus(x):
    return jnp.maximum(x, 0.0) + jnp.log(1.0 + jnp.exp(-jnp.abs(x)))


def _pc(body, name, grid, in_specs, out_specs, out_shape, scratch=(), sem=None, aliases=None):
    return pl.pallas_call(
        body, out_shape=out_shape, grid=grid, in_specs=in_specs, out_specs=out_specs,
        scratch_shapes=scratch, name=name, input_output_aliases=aliases or {},
        compiler_params=pltpu.CompilerParams(dimension_semantics=sem, vmem_limit_bytes=VMEM_LIMIT))


def _row(tm, n):
    return pl.BlockSpec((tm, n), lambda i: (i, 0))


def _const(shape):
    nd = len(shape)
    return pl.BlockSpec(shape, lambda *_: (0,) * nd)


def _matmul(a, b, mode, name, out_dtype=F32, tm=512, tn=512, tk=512, stack_out=False):
    if mode == "nn":
        (m, k), n = a.shape, b.shape[1]
    elif mode == "nt":
        (m, k), n = a.shape, b.shape[0]
    else:
        (k, m), n = a.shape, b.shape[1]
    tm, tn, tk = min(tm, m), min(tn, n), min(tk, k)
    assert m % tm == 0 and n % tn == 0 and k % tk == 0, (name, m, n, k, tm, tn, tk)
    nk = k // tk
    if mode == "nn":
        a_spec = pl.BlockSpec((tm, tk), lambda i, j, kk: (i, kk))
        b_spec = pl.BlockSpec((tk, tn), lambda i, j, kk: (kk, j))
        dot = _dot
    elif mode == "nt":
        a_spec = pl.BlockSpec((tm, tk), lambda i, j, kk: (i, kk))
        b_spec = pl.BlockSpec((tn, tk), lambda i, j, kk: (j, kk))
        dot = _dot_nt
    else:
        a_spec = pl.BlockSpec((tk, tm), lambda i, j, kk: (kk, i))
        b_spec = pl.BlockSpec((tk, tn), lambda i, j, kk: (kk, j))
        dot = _dot_tn

    def body(a_ref, b_ref, o_ref, *acc):
        if nk == 1:
            o_ref[...] = dot(a_ref[...], b_ref[...]).astype(out_dtype)
            return
        acc_ref, kk = acc[0], pl.program_id(2)

        @pl.when(kk == 0)
        def _():
            acc_ref[...] = dot(a_ref[...], b_ref[...])

        @pl.when((kk > 0) & (kk < nk - 1))
        def _():
            acc_ref[...] += dot(a_ref[...], b_ref[...])

        @pl.when(kk == nk - 1)
        def _():
            o_ref[...] = (acc_ref[...] + dot(a_ref[...], b_ref[...])).astype(out_dtype)

    if stack_out:
        o_spec, o_shape = pl.BlockSpec((None, tm, tn), lambda i, j, kk: (j, i, 0)), SDS((n // tn, m, tn), out_dtype)
    else:
        o_spec, o_shape = pl.BlockSpec((tm, tn), lambda i, j, kk: (i, j)), SDS((m, n), out_dtype)
    return _pc(body, name, (m // tm, n // tn, nk), [a_spec, b_spec], o_spec, o_shape,
               scratch=[pltpu.VMEM((tm, tn), F32)] if nk > 1 else [],
               sem=("parallel", "parallel", "arbitrary"))(a, b)


PROJ_TN = 768


def _proj_conv(h0_bf, w_cat, conv_w, tm, after):
    t = h0_bf.shape[0]
    n_qkv = QKV_WIDTH // PROJ_TN
    n_gate = 3 * D_MODEL // PROJ_TN

    def body(h_ref, w_ref, cw_ref, after_ref, o_ref, gate_ref, act_ref, ds_ref, carry_ref, ext_ref):
        @pl.when(pl.program_id(0) == 0)
        def _():
            carry_ref[...] = jnp.zeros_like(carry_ref)

        h = h_ref[...]

        def project(cb):
            cols = slice(cb * PROJ_TN, (cb + 1) * PROJ_TN)
            res = _dot(h, w_ref[:, cols])
            if cb < n_qkv:
                o_ref[:, cols] = res
            elif cb < n_qkv + n_gate:
                gate_ref[:, (cb - n_qkv) * PROJ_TN:(cb - n_qkv + 1) * PROJ_TN] = _mx(res)
            else:
                o_ref[:, P_U:PROJ_F32_WIDTH] = res

        def conv(cb, part):
            cols = slice(cb * PROJ_TN, (cb + 1) * PROJ_TN)
            if part == 0:
                ext_ref[cb, 0:8, :] = carry_ref[:, cols]
                ext_ref[cb, 8:8 + tm, :] = o_ref[:, cols]
                carry_ref[:, cols] = o_ref[tm - 8:tm, cols]
            w = [cw_ref[pl.ds(k, 1), cols] for k in range(CONV_K)]
            for r in range(part * (tm // 2), (part + 1) * (tm // 2), CONV_ROWS):
                y = _conv_rows(ext_ref.at[cb], w, r, CONV_ROWS)
                s = _sigmoid(y)
                act_ref[pl.ds(r, CONV_ROWS), cols] = y * s
                ds_ref[pl.ds(r, CONV_ROWS), cols] = _mx(s * (1.0 + y * (1.0 - s)))

        pending = [(cb, part) for cb in range(n_qkv) for part in range(2)]
        project(0)
        for cb in range(1, CAT_WIDTH // PROJ_TN):
            project(cb)
            if pending and pending[0][0] < cb:
                conv(*pending.pop(0))
        for cb, part in pending:
            conv(cb, part)

    return _pc(body, "proj_conv", (t // tm,),
               [_row(tm, D_MODEL), _const((D_MODEL, CAT_WIDTH)), _const((CONV_K, QKV_WIDTH)), ANY],
               [_row(tm, PROJ_F32_WIDTH), _row(tm, 3 * D_MODEL), _row(tm, QKV_WIDTH), _row(tm, QKV_WIDTH)],
               [SDS((t, PROJ_F32_WIDTH), F32), SDS((t, 3 * D_MODEL), MXU_DTYPE), SDS((t, QKV_WIDTH), F32),
                SDS((t, QKV_WIDTH), MXU_DTYPE)],
               scratch=[pltpu.VMEM((8, QKV_WIDTH), F32), pltpu.VMEM((n_qkv, 8 + tm, PROJ_TN), F32)],
               sem=("arbitrary",))(h0_bf, w_cat, conv_w, after)


def _ln_stats(x):
    mu = jnp.mean(x, axis=-1, keepdims=True)
    xc = x - mu
    var = jnp.mean(xc * xc, axis=-1, keepdims=True)
    rstd = lax.rsqrt(var + LN_EPS)
    return xc * rstd, rstd


def _ln_bwd(dy, xhat, rstd, g):
    dxh = dy * g
    m1 = jnp.mean(dxh, axis=-1, keepdims=True)
    m2 = jnp.mean(dxh * xhat, axis=-1, keepdims=True)
    return rstd * (dxh - m1 - xhat * m2)


def _ln_in(x, g, b, tm, after):
    t, d = x.shape

    def body(x_ref, g_ref, b_ref, after_ref, h_ref, hb_ref):
        xhat, _ = _ln_stats(x_ref[...])
        h = xhat * g_ref[...] + b_ref[...]
        h_ref[...] = h
        hb_ref[...] = _mx(h)

    return _pc(body, "ln_in", (t // tm,), [_row(tm, d), _const((1, d)), _const((1, d)), ANY],
               [_row(tm, d), _row(tm, d)], [SDS((t, d), F32), SDS((t, d), MXU_DTYPE)],
               sem=("parallel",))(x, g, b, after)


def _pool_fwd(proj, pool_w, tm):
    t = proj.shape[0]
    ublk = P_U // POOL_WIDTH

    def body(u_ref, halo_ref, pw_ref, ypre_ref, d_ref, ext_ref):
        i = pl.program_id(0)
        ext_ref[0:16, :] = jnp.where(i > 0, halo_ref[...], 0.0)
        ext_ref[16:16 + tm, :] = u_ref[...]
        tok = i * tm + lax.broadcasted_iota(jnp.int32, (tm, POOL_GROUP), 0)
        for gi, w in enumerate(POOL_WINDOWS):
            cs = pl.ds(gi * POOL_GROUP, POOL_GROUP)
            ug = ext_ref[pl.ds(16, tm), cs]
            s = ug
            for k in range(1, w):
                s = s + ext_ref[pl.ds(16 - k, tm), cs]
            cnt = jnp.minimum(tok + 1, w).astype(F32)
            db = _mx(s / cnt - ug)
            d_ref[:, gi * POOL_GROUP:(gi + 1) * POOL_GROUP] = db
            ypre_ref[:, gi * POOL_OUT_GROUP:(gi + 1) * POOL_OUT_GROUP] = _dot(db, pw_ref[gi])

    halo = pl.BlockSpec((16, POOL_WIDTH), lambda i: (jnp.maximum(i * (tm // 16) - 1, 0), ublk))
    return _pc(body, "pool_fwd", (t // tm,),
               [pl.BlockSpec((tm, POOL_WIDTH), lambda i: (i, ublk)), halo, _const((4, POOL_GROUP, POOL_OUT_GROUP))],
               [_row(tm, D_MODEL), _row(tm, POOL_WIDTH)],
               [SDS((t, D_MODEL), F32), SDS((t, POOL_WIDTH), MXU_DTYPE)],
               scratch=[pltpu.VMEM((16 + tm, POOL_WIDTH), F32)], sem=("parallel",))(proj, proj, pool_w)


def _pool_bwd(dyp, d_bf, pool_w, dproj, tm):
    t = dyp.shape[0]
    n = t // tm

    def body(dy_ref, dyn_ref, d_ref, pw_ref, dproj_ref, du_ref, dpw_ref, ext_ref):
        i = pl.program_id(0)

        @pl.when(i == 0)
        def _():
            dpw_ref[...] = jnp.zeros_like(dpw_ref)

        tok = i * tm + lax.broadcasted_iota(jnp.int32, (tm + 16, POOL_GROUP), 0)
        for gi, w in enumerate(POOL_WINDOWS):
            dy = dy_ref[:, gi * POOL_OUT_GROUP:(gi + 1) * POOL_OUT_GROUP]
            dyn = dyn_ref[:, gi * POOL_OUT_GROUP:(gi + 1) * POOL_OUT_GROUP]
            pw = pw_ref[gi]
            dd = _dot_nt(dy, pw)
            ddn = jnp.where(i < n - 1, _dot_nt(dyn, pw), 0.0)
            cnt = jnp.minimum(tok + 1, w).astype(F32)
            ext_ref[0:tm, :] = dd / cnt[0:tm]
            ext_ref[tm:tm + 16, :] = ddn / cnt[tm:tm + 16]
            s = ext_ref[pl.ds(0, tm), :]
            for k in range(1, w):
                s = s + ext_ref[pl.ds(k, tm), :]
            du_ref[:, gi * POOL_GROUP:(gi + 1) * POOL_GROUP] = _mx(s - dd)
            dpw_ref[gi] += _dot_tn(d_ref[:, gi * POOL_GROUP:(gi + 1) * POOL_GROUP], dy)

    nxt = pl.BlockSpec((16, D_MODEL), lambda i: (jnp.minimum((i + 1) * (tm // 16), t // 16 - 1), 0))
    return _pc(body, "pool_bwd", (n,),
               [_row(tm, D_MODEL), nxt, _row(tm, POOL_WIDTH), _const((4, POOL_GROUP, POOL_OUT_GROUP)), ANY],
               [pl.BlockSpec((tm, POOL_WIDTH), lambda i: (i, K_U // POOL_WIDTH)),
                _const((4, POOL_GROUP, POOL_OUT_GROUP))],
               [SDS(dproj.shape, dproj.dtype), SDS((4, POOL_GROUP, POOL_OUT_GROUP), F32)],
               scratch=[pltpu.VMEM((tm + 16, POOL_GROUP), F32)], sem=("arbitrary",),
               aliases={4: 0})(dyp, dyp, d_bf, pool_w, dproj)


CONV_BLK = 512


CONV_ROWS = 32


def _conv_rows(ext_ref, w, r, rows):
    y = w[0] * ext_ref[pl.ds(r + 5, rows), :]
    for k in range(1, CONV_K):
        y = y + w[k] * ext_ref[pl.ds(r + 5 + k, rows), :]
    return y


def _conv_bwd(dact, dsilu, proj, conv_w, dproj, tm):
    t = proj.shape[0]
    n = t // tm

    def body(da_ref, dan_ref, ds_ref, dsn_ref, x_ref, xp_ref, w_ref, dproj_ref, dx_ref, dw_ref, ext_ref, dy_ref):
        i = pl.program_id(1)

        @pl.when(i == 0)
        def _():
            dw_ref[...] = jnp.zeros_like(dw_ref)

        ext_ref[0:8, :] = jnp.where(i > 0, xp_ref[...], 0.0)
        ext_ref[8:8 + tm, :] = x_ref[...]
        w = [w_ref[pl.ds(k, 1), :] for k in range(CONV_K)]

        acc = [jnp.zeros((8, CONV_BLK), F32) for _ in range(CONV_K)]
        for r in range(0, tm, CONV_ROWS):
            dy = da_ref[pl.ds(r, CONV_ROWS), :] * ds_ref[pl.ds(r, CONV_ROWS), :].astype(F32)
            dy_ref[pl.ds(r, CONV_ROWS), :] = dy
            for k in range(CONV_K):
                prod = dy * ext_ref[pl.ds(r + 5 + k, CONV_ROWS), :]
                for q in range(0, CONV_ROWS, 8):
                    acc[k] = acc[k] + prod[q:q + 8]
        dy_ref[tm:tm + 8, :] = jnp.where(i < n - 1, dan_ref[...] * dsn_ref[0:8, :].astype(F32), 0.0)
        for k in range(CONV_K):
            dw_ref[pl.ds(k, 1), :] += jnp.sum(acc[k], axis=0, keepdims=True)
        for r in range(0, tm, CONV_ROWS):
            dx = w[0] * dy_ref[pl.ds(r + 3, CONV_ROWS), :]
            for k in range(1, CONV_K):
                dx = dx + w[k] * dy_ref[pl.ds(r + 3 - k, CONV_ROWS), :]
            dx_ref[pl.ds(r, CONV_ROWS), :] = _mx(dx)

    blk = pl.BlockSpec((tm, CONV_BLK), lambda j, i: (i, j))
    prev = pl.BlockSpec((8, CONV_BLK), lambda j, i: (jnp.maximum(i * (tm // 8) - 1, 0), j))
    nxt = pl.BlockSpec((8, CONV_BLK), lambda j, i: (jnp.minimum((i + 1) * (tm // 8), t // 8 - 1), j))
    nxt16 = pl.BlockSpec((16, CONV_BLK), lambda j, i: (jnp.minimum((i + 1) * (tm // 16), t // 16 - 1), j))
    wspec = pl.BlockSpec((CONV_K, CONV_BLK), lambda j, i: (0, j))
    return _pc(body, "conv_bwd", (QKV_WIDTH // CONV_BLK, n),
               [blk, nxt, blk, nxt16, blk, prev, wspec, ANY],
               [blk, pl.BlockSpec((8, CONV_BLK), lambda j, i: (0, j))],
               [SDS(dproj.shape, dproj.dtype), SDS((8, QKV_WIDTH), F32)],
               scratch=[pltpu.VMEM((8 + tm, CONV_BLK), F32), pltpu.VMEM((8 + tm, CONV_BLK), F32)],
               sem=("parallel", "arbitrary"), aliases={7: 0})(dact, dact, dsilu, dsilu, proj, proj, conv_w, dproj)


def _lane(shape):
    return lax.broadcasted_iota(jnp.int32, shape, 1)


def _ba_fwd(proj, al_row, dtb_row, tm):
    t = proj.shape[0]
    bablk = P_BA // 128

    def body(ba_ref, al_ref, dtb_ref, bg_ref):
        ba = ba_ref[...]
        lane = _lane(ba.shape)
        g = -jnp.exp(al_ref[...]) * _softplus(ba + dtb_ref[...])
        bg_ref[...] = jnp.where(lane < HEADS, _sigmoid(ba), jnp.where(lane < 2 * HEADS, g, 0.0))

    return _pc(body, "ba_fwd", (t // tm,),
               [pl.BlockSpec((tm, 128), lambda i: (i, bablk)), _const((1, 128)), _const((1, 128))],
               _row(tm, 128), SDS((t, 128), F32), sem=("parallel",))(proj, al_row, dtb_row)


def _ba_bwd(dbg, bg, proj, al_row, dtb_row, dproj, tm):
    t = proj.shape[0]
    bablk = P_BA // 128

    def body(dbg_ref, bg_ref, ba_ref, al_ref, dtb_ref, dproj_ref, dba_ref, acc_ref):
        i = pl.program_id(0)

        @pl.when(i == 0)
        def _():
            acc_ref[...] = jnp.zeros_like(acc_ref)

        dbg_v, bg_v, ba = dbg_ref[...], bg_ref[...], ba_ref[...]
        lane = _lane(ba.shape)
        is_g = (lane >= HEADS) & (lane < 2 * HEADS)
        dbeta_raw = dbg_v * bg_v * (1.0 - bg_v)
        da_raw = dbg_v * (-jnp.exp(al_ref[...])) * _sigmoid(ba + dtb_ref[...])
        dba_ref[:, 0:128] = _mx(jnp.where(lane < HEADS, dbeta_raw, jnp.where(is_g, da_raw, 0.0)))
        dba_ref[:, 128:CAT_WIDTH - K_BA] = jnp.zeros((tm, CAT_WIDTH - K_BA - 128), dba_ref.dtype)
        acc_ref[0:1, :] += jnp.sum(jnp.where(is_g, dbg_v * bg_v, 0.0), axis=0, keepdims=True)
        acc_ref[1:2, :] += jnp.sum(jnp.where(is_g, da_raw, 0.0), axis=0, keepdims=True)

    tail = CAT_WIDTH - K_BA
    return _pc(body, "ba_bwd", (t // tm,),
               [_row(tm, 128), _row(tm, 128), pl.BlockSpec((tm, 128), lambda i: (i, bablk)),
                _const((1, 128)), _const((1, 128)), ANY],
               [pl.BlockSpec((tm, tail), lambda i: (i, K_BA // tail)), _const((8, 128))],
               [SDS(dproj.shape, dproj.dtype), SDS((8, 128), F32)],
               sem=("arbitrary",), aliases={5: 0})(dbg, bg, proj, al_row, dtb_row, dproj)


def _each(f, *lists):
    return [f(*a) for a in zip(*lists)]


def _rowsum(a):
    return jnp.sum(a, axis=1, keepdims=True)


def _chunk_terms(qs, ks, bgvs, g_rows, hs):
    c = CHUNK
    ii = lax.broadcasted_iota(jnp.int32, (c, c), 0)
    jj = lax.broadcasted_iota(jnp.int32, (c, c), 1)
    lane = _lane((c, 128))
    incl = ii >= jj
    beta = [_rowsum(jnp.where(lane == h, bgv, 0.0)) for h, bgv in zip(hs, bgvs)]
    g_col = [_rowsum(jnp.where(lane == HEADS + h, bgv, 0.0)) for h, bgv in zip(hs, bgvs)]
    rq = _each(lambda q: lax.rsqrt(_rowsum(q * q) + L2_EPS), qs)
    rk = _each(lambda k: lax.rsqrt(_rowsum(k * k) + L2_EPS), ks)
    yq = _each(jnp.multiply, qs, rq)
    kn = _each(jnp.multiply, ks, rk)
    qn = _each(lambda a: a * Q_SCALE, yq)
    gc_col = _each(lambda g: _rowsum(jnp.where(jj <= ii, g, 0.0)), g_rows)
    gc_row = _each(lambda g: jnp.sum(jnp.where(ii <= jj, g, 0.0), axis=0, keepdims=True), g_col)
    dm = _each(lambda a, b: jnp.where(incl, jnp.exp(jnp.where(incl, a - b, 0.0)), 0.0), gc_col, gc_row)
    gl = _each(_rowsum, g_rows)
    eg = _each(jnp.exp, gc_col)
    ek = _each(lambda a, b: jnp.exp(a - b), gl, gc_col)
    egl = _each(jnp.exp, gl)
    kb = _each(jnp.multiply, kn, beta)
    kk = _each(_dot_nt, kb, kn)
    qk = _each(_dot_nt, qn, kn)
    m = _each(lambda a, b: jnp.where(ii > jj, a * b, 0.0), kk, dm)
    attn = _each(jnp.multiply, qk, dm)
    return dict(ii=ii, jj=jj, beta=beta, rq=rq, rk=rk, yq=yq, kn=kn, qn=qn, dm=dm, eg=eg, ek=ek,
                egl=egl, kb=kb, m=m, attn=attn)


def _unit_lower_inverse_minus_identity(ms, ii, jj):
    pair = (ii >> 1) == (jj >> 1)
    ys = _each(lambda m: -jnp.where(pair, m, 0.0), ms)
    s = 1
    while (1 << s) < CHUNK:
        mask = ((ii >> (s + 1)) == (jj >> (s + 1))) & ((ii >> s) != (jj >> s))
        lbs = _each(lambda m: jnp.where(mask, m, 0.0), ms)
        zs = _each(lambda y, lb: lb + _dot(y, lb), ys, lbs)
        ys = _each(lambda y, z: y - z - _dot(z, y), ys, zs)
        s += 1
    return ys


def _dn_fwd(qkv_act, bg, bgt):
    t = qkv_act.shape[0]
    c = CHUNK
    per = DN_FWD_CHUNKS if t % (DN_FWD_CHUNKS * c) == 0 else 1
    nt = t // c
    hs = list(range(HEADS))
    entries = [(s_, h) for s_ in range(per) for h in hs]
    qo = [slice(h * HEAD_DIM, (h + 1) * HEAD_DIM) for h in hs]
    ko = [slice(DN_WIDTH + h * HEAD_DIM, DN_WIDTH + (h + 1) * HEAD_DIM) for h in hs]
    vo = [slice(2 * DN_WIDTH + h * HEAD_DIM, 2 * DN_WIDTH + (h + 1) * HEAD_DIM) for h in hs]

    def body(qkv_ref, bg_ref, bgt_ref, o_ref, u_ref, w_ref, qg_ref, kg_ref, attn_ref, y_ref, vn_ref, st_ref, egl_ref,
             s_ref):
        @pl.when(pl.program_id(0) == 0)
        def _():
            s_ref[...] = jnp.zeros_like(s_ref)

        rows = [pl.ds(s_ * c, c) for s_ in range(per)]
        qs = [qkv_ref[rows[s_], qo[h]] for s_, h in entries]
        ks = [qkv_ref[rows[s_], ko[h]] for s_, h in entries]
        vs = [qkv_ref[rows[s_], vo[h]] for s_, h in entries]
        bgvs = [bg_ref[rows[s_], :] for s_, _ in entries]
        g_rows = [bgt_ref[pl.ds(HEADS + h, 1), rows[s_]] for s_, h in entries]
        ct = _chunk_terms(qs, ks, bgvs, g_rows, [h for _, h in entries])
        ys = _unit_lower_inverse_minus_identity(ct["m"], ct["ii"], ct["jj"])
        vb = _each(jnp.multiply, vs, ct["beta"])
        kbe = _each(jnp.multiply, ct["kb"], ct["eg"])
        us = _each(lambda a, y: a + _dot(y, a), vb, ys)
        ws = _each(lambda a, y: _mx(a + _dot(y, a)), kbe, ys)
        qg = _each(lambda a, b: _mx(a * b), ct["qn"], ct["eg"])
        kg = _each(lambda a, b: _mx(a * b), ct["kn"], ct["ek"])
        attn = _each(_mx, ct["attn"])
        for e, (s_, h) in enumerate(entries):
            u_ref[rows[s_], qo[h]] = us[e]
            w_ref[rows[s_], qo[h]] = ws[e]
            qg_ref[rows[s_], qo[h]] = qg[e]
            kg_ref[rows[s_], qo[h]] = kg[e]
            attn_ref[rows[s_], qo[h]] = attn[e]
            y_ref[rows[s_], qo[h]] = _mx(ys[e])
            egl_ref[s_, h:h + 1, :] = jnp.broadcast_to(ct["egl"][e], (1, HEAD_DIM))
        ss = [s_ref[h] for h in hs]
        for s_ in range(per):
            pick = lambda xs: xs[s_ * HEADS:(s_ + 1) * HEADS]
            sb = _each(_mx, ss)
            vn = _each(lambda a, b, st: a - _dot(b, st), pick(us), pick(ws), sb)
            vnb = _each(_mx, vn)
            oa = _each(_dot, pick(qg), sb)
            ob = _each(_dot, pick(attn), vnb)
            upd = _each(_dot_tn, pick(kg), vnb)
            for h in hs:
                st_ref[s_, h] = ss[h]
                vn_ref[rows[s_], qo[h]] = vnb[h]
                o_ref[rows[s_], qo[h]] = oa[h] + ob[h]
            ss = _each(lambda st, g, d: st * g + d, ss, pick(ct["egl"]), upd)
        for h in hs:
            s_ref[h] = ss[h]

    wide = _row(per * c, DN_WIDTH)
    return _pc(body, "dn_fwd", (nt // per,),
               [_row(per * c, QKV_WIDTH), _row(per * c, 128), pl.BlockSpec((2 * HEADS, per * c), lambda i: (0, i))],
               [wide] * 8 + [pl.BlockSpec((per, HEADS, HEAD_DIM, HEAD_DIM), lambda i: (i, 0, 0, 0)),
                             pl.BlockSpec((per, HEADS, HEAD_DIM), lambda i: (i, 0, 0))],
               [SDS((t, DN_WIDTH), F32), SDS((t, DN_WIDTH), F32)] + [SDS((t, DN_WIDTH), MXU_DTYPE)] * 6
               + [SDS((nt, HEADS, HEAD_DIM, HEAD_DIM), F32), SDS((nt, HEADS, HEAD_DIM), F32)],
               scratch=[pltpu.VMEM((HEADS, HEAD_DIM, HEAD_DIM), F32)], sem=("arbitrary",))(qkv_act, bg, bgt)


def _dn_bwd(do, qkv_act, bg, bgt, u, w, qg, kg, attn, ymat, vn, states, egl):
    t = do.shape[0]
    c = CHUNK
    per = DN_BWD_CHUNKS if t % (DN_BWD_CHUNKS * c) == 0 else 1
    nt = t // c
    hs = list(range(HEADS))
    entries = [(s_, h) for s_ in range(per) for h in hs]
    qo = [slice(h * HEAD_DIM, (h + 1) * HEAD_DIM) for h in hs]
    ko = [slice(DN_WIDTH + h * HEAD_DIM, DN_WIDTH + (h + 1) * HEAD_DIM) for h in hs]
    vo = [slice(2 * DN_WIDTH + h * HEAD_DIM, 2 * DN_WIDTH + (h + 1) * HEAD_DIM) for h in hs]

    def body(do_ref, qkv_ref, bg_ref, bgt_ref, u_ref, w_ref, qg_ref, kg_ref, attn_ref, y_ref, vn_ref, st_ref, egl_ref,
             dqkv_ref, dbg_ref, ds_ref):
        @pl.when(pl.program_id(0) == 0)
        def _():
            ds_ref[...] = jnp.zeros_like(ds_ref)

        rows = [pl.ds(s_ * c, c) for s_ in range(per)]
        scan = {}
        dsp = [ds_ref[h] for h in hs]
        for s_ in reversed(range(per)):
            r = rows[s_]
            dsb = _each(_mx, dsp)
            ss = [st_ref[s_, h] for h in hs]
            sb = _each(_mx, ss)
            du_s = [_dot(kg_ref[r, sl], b) + _dot_tn(attn_ref[r, sl], do_ref[r, sl]) for sl, b in zip(qo, dsb)]
            dub = _each(_mx, du_s)
            scan[s_] = dict(
                du=du_s,
                dkg=[_dot_nt(vn_ref[r, sl], b) for sl, b in zip(qo, dsb)],
                dqg=[_dot_nt(do_ref[r, sl], b) for sl, b in zip(qo, sb)],
                dattn=[_dot_nt(do_ref[r, sl], vn_ref[r, sl]) for sl in qo],
                dw=[-_dot_nt(a, b) for a, b in zip(dub, sb)],
                degl=[jnp.sum(_rowsum(a * b), axis=0, keepdims=True) for a, b in zip(ss, dsp)])
            upd = [_dot_tn(qg_ref[r, sl], do_ref[r, sl]) - _dot_tn(w_ref[r, sl], a) for sl, a in zip(qo, dub)]
            dsp = [dsp[h] * egl_ref[s_, h:h + 1, :] + upd[h] for h in hs]
        for h in hs:
            ds_ref[h] = dsp[h]
        gather = lambda key: [scan[s_][key][h] for s_, h in entries]
        du, dkg_v, dqg_v, dattn_v, dwv, degl_v = (gather(k) for k in ("du", "dkg", "dqg", "dattn", "dw", "degl"))

        lane = _lane((c, 128))
        rowi = lax.broadcasted_iota(jnp.int32, (c, 1), 0)
        qs = [qkv_ref[rows[s_], qo[h]] for s_, h in entries]
        ks = [qkv_ref[rows[s_], ko[h]] for s_, h in entries]
        vs = [qkv_ref[rows[s_], vo[h]] for s_, h in entries]
        bgvs = [bg_ref[rows[s_], :] for s_, _ in entries]
        g_rows = [bgt_ref[pl.ds(HEADS + h, 1), rows[s_]] for s_, h in entries]
        ct = _chunk_terms(qs, ks, bgvs, g_rows, [h for _, h in entries])
        ii, jj = ct["ii"], ct["jj"]
        beta, eg, ek, kb, kn, qn, dm = ct["beta"], ct["eg"], ct["ek"], ct["kb"], ct["kn"], ct["qn"], ct["dm"]
        ys = [y_ref[rows[s_], qo[h]] for s_, h in entries]
        dvb = _each(lambda a, y: a + _dot_tn(y, a), du, ys)
        dkbe = _each(lambda a, y: a + _dot_tn(y, a), dwv, ys)
        dm_u = [_dot_nt(a, u_ref[rows[s_], qo[h]]) for a, (s_, h) in zip(dvb, entries)]
        dm_w = [_dot_nt(a, w_ref[rows[s_], qo[h]]) for a, (s_, h) in zip(dkbe, entries)]
        dms = _each(lambda a, b: jnp.where(ii > jj, -(a + b), 0.0), dm_u, dm_w)
        dkk = _each(jnp.multiply, dms, dm)
        dqk = _each(jnp.multiply, dattn_v, dm)
        gmat = _each(lambda a, b, c_, d: a * b + c_ * d, dms, ct["m"], dattn_v, ct["attn"])
        dkb = _each(lambda a, b, c_, d: _dot(a, b) + c_ * d, dkk, kn, dkbe, eg)
        dk1 = _each(_dot_tn, dkk, kb)
        dk2 = _each(_dot_tn, dqk, qn)
        dq1 = _each(_dot, dqk, kn)
        dk = _each(lambda a, b, c_, d: a + b + c_ * d, dk1, dk2, dkg_v, ek)
        dq = _each(lambda a, b, c_: a + b * c_, dq1, dqg_v, eg)
        deg = _each(lambda a, b, c_, d: _rowsum(a * b) + _rowsum(c_ * d), dqg_v, qn, dkbe, kb)
        dek = _each(lambda a, b: _rowsum(a * b), dkg_v, kn)
        dgl = _each(lambda a, b, c_, d: jnp.sum(a * b, axis=0, keepdims=True) + c_ * d, dek, ek, degl_v, ct["egl"])
        cs_row = _each(lambda g: jnp.sum(g, axis=0, keepdims=True), gmat)
        cs_col = _each(lambda r: _rowsum(jnp.where(ii == jj, r, 0.0)), cs_row)
        dgc = _each(lambda a, b, c_, d, g, e, f: a * b - c_ * d + _rowsum(g) - e + jnp.where(rowi == c - 1, f, 0.0),
                    deg, eg, dek, ek, gmat, cs_col, dgl)
        dgc_row = _each(lambda a: jnp.sum(jnp.where(ii == jj, a, 0.0), axis=0, keepdims=True), dgc)
        dg = _each(lambda r: _rowsum(jnp.where(jj >= ii, r, 0.0)), dgc_row)
        dbeta = _each(lambda a, b, c_, d: _rowsum(a * b) + _rowsum(c_ * d), dkb, kn, dvb, vs)
        dk = _each(lambda a, b, c_: a + b * c_, dk, dkb, beta)
        dbg = [jnp.zeros((c, 128), F32) for _ in range(per)]
        for e, (s_, h) in enumerate(entries):
            dyq = dq[e] * Q_SCALE
            yq = ct["yq"][e]
            dqkv_ref[rows[s_], qo[h]] = ct["rq"][e] * (dyq - yq * _rowsum(yq * dyq))
            dqkv_ref[rows[s_], ko[h]] = ct["rk"][e] * (dk[e] - kn[e] * _rowsum(kn[e] * dk[e]))
            dqkv_ref[rows[s_], vo[h]] = dvb[e] * beta[e]
            dbg[s_] = dbg[s_] + jnp.where(lane == h, dbeta[e], 0.0) + jnp.where(lane == HEADS + h, dg[e], 0.0)
        for s_ in range(per):
            dbg_ref[rows[s_], :] = dbg[s_]

    ns = nt // per
    rev = pl.BlockSpec((per * c, DN_WIDTH), lambda i: (ns - 1 - i, 0))
    return _pc(body, "dn_bwd", (ns,),
               [rev, pl.BlockSpec((per * c, QKV_WIDTH), lambda i: (ns - 1 - i, 0)),
                pl.BlockSpec((per * c, 128), lambda i: (ns - 1 - i, 0)),
                pl.BlockSpec((2 * HEADS, per * c), lambda i: (0, ns - 1 - i))]
               + [rev] * 7
               + [pl.BlockSpec((per, HEADS, HEAD_DIM, HEAD_DIM), lambda i: (ns - 1 - i, 0, 0, 0)),
                  pl.BlockSpec((per, HEADS, HEAD_DIM), lambda i: (ns - 1 - i, 0, 0))],
               [pl.BlockSpec((per * c, QKV_WIDTH), lambda i: (ns - 1 - i, 0)),
                pl.BlockSpec((per * c, 128), lambda i: (ns - 1 - i, 0))],
               [SDS((t, QKV_WIDTH), F32), SDS((t, 128), F32)],
               scratch=[pltpu.VMEM((HEADS, HEAD_DIM, HEAD_DIM), F32)],
               sem=("arbitrary",))(do, qkv_act, bg, bgt, u, w, qg, kg, attn, ymat, vn, states, egl)


MIX_ROWS = 64


def _mix_oproj_ln1(o, gates, ypre, pool_scale, wo_row, w_out, h0, g1, b1, tm):
    t = o.shape[0]

    def body(o_ref, z_ref, ga_ref, gb_ref, yp_ref, ps_ref, wo_ref, w_ref, h0_ref, g_ref, b_ref,
             mixed_ref, a1_ref, h1_ref, h1b_ref):
        for r in range(0, tm, MIX_ROWS):
            rows = pl.ds(r, MIX_ROWS)
            for h in range(HEADS):
                sl = slice(h * HEAD_DIM, (h + 1) * HEAD_DIM)
                oh = o_ref[rows, sl]
                on = oh * lax.rsqrt(jnp.mean(oh * oh, axis=1, keepdims=True) + RMS_EPS)
                zh = z_ref[rows, sl].astype(F32)
                yb = on * wo_ref[:, sl] * (zh * _sigmoid(zh))
                ya = yp_ref[rows, sl] * ps_ref[:, sl]
                mixed_ref[rows, sl] = _mx(_sigmoid(ga_ref[rows, sl].astype(F32)) * ya
                                          + _sigmoid(gb_ref[rows, sl].astype(F32)) * yb)
        a1 = ALPHA * h0_ref[...] + _dot(mixed_ref[...], w_ref[...])
        a1_ref[...] = a1
        xhat, _ = _ln_stats(a1)
        h1 = xhat * g_ref[...] + b_ref[...]
        h1_ref[...] = h1
        h1b_ref[...] = _mx(h1)

    def col(blk):
        return pl.BlockSpec((tm, D_MODEL), lambda i: (i, blk))

    r = _row(tm, D_MODEL)
    v = _const((1, D_MODEL))
    return _pc(body, "mix_oproj_ln1", (t // tm,),
               [r, col(0), col(1), col(2), r, v, v,
                _const((D_MODEL, D_MODEL)), r, v, v],
               [r, r, r, r],
               [SDS((t, D_MODEL), MXU_DTYPE), SDS((t, D_MODEL), F32), SDS((t, D_MODEL), F32),
                SDS((t, D_MODEL), MXU_DTYPE)],
               sem=("parallel",))(o, gates, gates, gates, ypre, pool_scale, wo_row, w_out, h0, g1, b1)


def _mix_bwd(da1_bf, w_out, o, gates, ypre, pool_scale, wo_row, tm, after):
    t = o.shape[0]

    def body(da_ref, wout_ref, o_ref, z_ref, ga_ref, gb_ref, yp_ref, ps_ref, wo_ref, after_ref,
             do_ref, dp_ref, dyp_ref, acc_ref, dm_ref):
        i = pl.program_id(0)

        @pl.when(i == 0)
        def _():
            acc_ref[...] = jnp.zeros_like(acc_ref)

        dm_ref[...] = _dot_nt(da_ref[...], wout_ref[...])
        dwo = jnp.zeros((1, HEAD_DIM), F32)
        for h in range(HEADS):
            sl = slice(h * HEAD_DIM, (h + 1) * HEAD_DIM)
            woh = wo_ref[:, sl]
            psh = ps_ref[:, sl]
            dps = jnp.zeros((1, HEAD_DIM), F32)
            for r in range(0, tm, MIX_ROWS):
                rows = pl.ds(r, MIX_ROWS)
                oh = o_ref[rows, sl]
                rs = lax.rsqrt(jnp.mean(oh * oh, axis=1, keepdims=True) + RMS_EPS)
                on = oh * rs
                zh = z_ref[rows, sl].astype(F32)
                sz = _sigmoid(zh)
                silu = zh * sz
                t1 = on * woh
                yb = t1 * silu
                sa = _sigmoid(ga_ref[rows, sl].astype(F32))
                sb = _sigmoid(gb_ref[rows, sl].astype(F32))
                yp = yp_ref[rows, sl]
                dm = dm_ref[rows, sl]
                ga_sl = slice(D_MODEL + h * HEAD_DIM, D_MODEL + (h + 1) * HEAD_DIM)
                gb_sl = slice(2 * D_MODEL + h * HEAD_DIM, 2 * D_MODEL + (h + 1) * HEAD_DIM)
                dp_ref[rows, ga_sl] = _mx(dm * (yp * psh) * sa * (1.0 - sa))
                dp_ref[rows, gb_sl] = _mx(dm * yb * sb * (1.0 - sb))
                dya = dm * sa
                dyb = dm * sb
                dyp_ref[rows, sl] = _mx(dya * psh)
                dps = dps + jnp.sum(dya * yp, axis=0, keepdims=True)
                dp_ref[rows, sl] = _mx(dyb * t1 * (sz * (1.0 + zh * (1.0 - sz))))
                dt1 = dyb * silu
                dwo = dwo + jnp.sum(dt1 * on, axis=0, keepdims=True)
                don = dt1 * woh
                do_ref[rows, sl] = _mx(rs * (don - on * jnp.mean(don * on, axis=1, keepdims=True)))
            acc_ref[0:1, sl] += dps
        acc_ref[1:2, 0:HEAD_DIM] += dwo

    def col(blk):
        return pl.BlockSpec((tm, D_MODEL), lambda i: (i, blk))

    r = _row(tm, D_MODEL)
    return _pc(body, "mix_bwd", (t // tm,),
               [r, _const((D_MODEL, D_MODEL)), r, col(0), col(1), col(2), r,
                _const((1, D_MODEL)), _const((1, D_MODEL)), ANY],
               [r, pl.BlockSpec((tm, 3 * D_MODEL), lambda i: (i, K_Z // (3 * D_MODEL))), r, _const((8, D_MODEL))],
               [SDS((t, D_MODEL), MXU_DTYPE), SDS((t, CAT_WIDTH), MXU_DTYPE), SDS((t, D_MODEL), MXU_DTYPE),
                SDS((8, D_MODEL), F32)],
               scratch=[pltpu.VMEM((tm, D_MODEL), F32)],
               sem=("arbitrary",))(da1_bf, w_out, o, gates, gates, gates, ypre, pool_scale, wo_row, after)


def _mlp_up(h1_bf, w_up, tm):
    t = h1_bf.shape[0]
    tn = w_up.shape[2]

    def body(h_ref, w_ref, act_ref):
        r = jnp.maximum(_dot(h_ref[...], w_ref[...]), 0.0)
        act_ref[...] = _mx(r * r)

    return _pc(body, "mlp_up", (D_FF // tn, t // tm),
               [pl.BlockSpec((tm, D_MODEL), lambda j, i: (i, 0)),
                pl.BlockSpec((None, D_MODEL, tn), lambda j, i: (j, 0, 0))],
               pl.BlockSpec((tm, tn), lambda j, i: (i, j)), SDS((t, D_FF), MXU_DTYPE),
               sem=("parallel", "parallel"))(h1_bf, w_up)


def _tail(act, w_down, h1, w_gate, p, w_proj, tgt, g2, b2, tm):
    t = act.shape[0]

    def body(act_ref, wd_ref, h1_ref, wg_ref, p_ref, wp_ref, tgt_ref, g_ref, b_ref,
             dr_ref, drb_ref, dgp_ref, dpp_ref, rb_ref, acc_ref):
        i = pl.program_id(0)

        @pl.when(i == 0)
        def _():
            acc_ref[...] = jnp.zeros_like(acc_ref)

        r = ALPHA * h1_ref[...] + _dot(act_ref[...], wd_ref[...])
        rb = _mx(r)
        rb_ref[...] = rb
        gate = _sigmoid(_dot(rb, wg_ref[...]))
        pp = _dot(p_ref[...], wp_ref[...])
        xhat, rstd = _ln_stats(r + gate * pp)
        g = g_ref[...]
        diff = xhat * g + b_ref[...] - tgt_ref[...]
        dh2 = diff * (1.0 / D_MODEL)
        rowloss = jnp.sum(diff * diff, axis=1, keepdims=True) * (0.5 / D_MODEL)
        acc_ref[0:1, :] += jnp.sum(dh2 * xhat, axis=0, keepdims=True)
        acc_ref[1:2, :] += jnp.sum(dh2, axis=0, keepdims=True)
        acc_ref[2:3, :] += jnp.broadcast_to(jnp.sum(rowloss, axis=0, keepdims=True), (1, D_MODEL))
        da2 = _ln_bwd(dh2, xhat, rstd, g)
        dpp_ref[...] = _mx(da2 * gate)
        dgp = _mx(da2 * pp * gate * (1.0 - gate))
        dgp_ref[...] = dgp
        dr = da2 + _dot_nt(dgp, wg_ref[...])
        dr_ref[...] = dr
        drb_ref[...] = _mx(dr)

    r = _row(tm, D_MODEL)
    v = _const((1, D_MODEL))
    return _pc(body, "tail", (t // tm,),
               [_row(tm, D_FF), _const((D_FF, D_MODEL)), r, _const((D_MODEL, D_MODEL)), _row(tm, PLE_DIM),
                _const((PLE_DIM, D_MODEL)), r, v, v],
               [r, r, r, r, r, _const((8, D_MODEL))],
               [SDS((t, D_MODEL), F32)] + [SDS((t, D_MODEL), MXU_DTYPE)] * 4 + [SDS((8, D_MODEL), F32)],
               sem=("arbitrary",))(act, w_down, h1, w_gate, p, w_proj, tgt, g2, b2)


SQRT_GUARD = 1e-30


def _mlp_bwd1(dr_bf, w_down, act, tm, tn):
    t = act.shape[0]

    def body(dr_ref, w_ref, act_ref, dup_ref):
        dact = _dot_nt(dr_ref[...], w_ref[...])
        a = act_ref[...].astype(F32)
        dup_ref[...] = _mx(dact * (2.0 * a * lax.rsqrt(a + SQRT_GUARD)))

    o = pl.BlockSpec((tm, tn), lambda j, i: (i, j))
    return _pc(body, "mlp_bwd1", (D_FF // tn, t // tm),
               [pl.BlockSpec((tm, D_MODEL), lambda j, i: (i, 0)), pl.BlockSpec((tn, D_MODEL), lambda j, i: (j, 0)), o],
               o, SDS((t, D_FF), MXU_DTYPE), sem=("parallel", "parallel"))(dr_bf, w_down, act)


def _mlp_bwd2(dup, w_up, dr, a1, g1, tm):
    t = dr.shape[0]

    nk, tk = w_up.shape[0], w_up.shape[2]

    def body(dup_ref, w_ref, dr_ref, a1_ref, g_ref, da1_ref, da1b_ref, acc_ref):
        i = pl.program_id(0)

        @pl.when(i == 0)
        def _():
            acc_ref[...] = jnp.zeros_like(acc_ref)

        dh1 = ALPHA * dr_ref[...]
        for kk in range(nk):
            dh1 = dh1 + _dot_nt(dup_ref[:, kk * tk:(kk + 1) * tk], w_ref[kk])
        xhat, rstd = _ln_stats(a1_ref[...])
        acc_ref[0:1, :] += jnp.sum(dh1 * xhat, axis=0, keepdims=True)
        acc_ref[1:2, :] += jnp.sum(dh1, axis=0, keepdims=True)
        da1 = _ln_bwd(dh1, xhat, rstd, g_ref[...])
        da1_ref[...] = da1
        da1b_ref[...] = _mx(da1)

    r = _row(tm, D_MODEL)
    return _pc(body, "mlp_bwd2", (t // tm,),
               [_row(tm, D_FF), _const((nk, D_MODEL, tk)), r, r, _const((1, D_MODEL))],
               [r, r, _const((8, D_MODEL))],
               [SDS((t, D_MODEL), F32), SDS((t, D_MODEL), MXU_DTYPE), SDS((8, D_MODEL), F32)],
               sem=("arbitrary",))(dup, w_up, dr, a1, g1)


def _ln_in_bwd(dproj, w_cat, da1, x, g, tm, after):
    t = x.shape[0]

    def body(dp_ref, w_ref, da1_ref, x_ref, g_ref, after_ref, dx_ref, acc_ref):
        i = pl.program_id(0)

        @pl.when(i == 0)
        def _():
            acc_ref[...] = jnp.zeros_like(acc_ref)

        dh0 = _dot_nt(dp_ref[...], w_ref[...]) + ALPHA * da1_ref[...]
        xhat, rstd = _ln_stats(x_ref[...])
        acc_ref[0:1, :] += jnp.sum(dh0 * xhat, axis=0, keepdims=True)
        acc_ref[1:2, :] += jnp.sum(dh0, axis=0, keepdims=True)
        dx_ref[...] = _ln_bwd(dh0, xhat, rstd, g_ref[...])

    r = _row(tm, D_MODEL)
    return _pc(body, "ln_in_bwd", (t // tm,),
               [_row(tm, CAT_WIDTH), _const((D_MODEL, CAT_WIDTH)), r, r, _const((1, D_MODEL)), ANY],
               [r, _const((8, D_MODEL))], [SDS((t, D_MODEL), F32), SDS((8, D_MODEL), F32)],
               sem=("arbitrary",))(dproj, w_cat, da1, x, g, after)


def _local_step(x, p, tgt, wts, start_token, first_weights, late_weights, send_late_grads, send_early_grads):
    t = x.shape[0]
    tm = min(512, t)
    tms = min(256, t)
    row = lambda a: a.reshape(1, -1)
    pool_scale = row(wts["pool_scale"])
    wo_row = jnp.tile(row(wts["o_norm_w"]), (1, HEADS))
    pad8 = jnp.zeros((1, HEADS), F32)
    al_row = jnp.concatenate([pad8, row(wts["a_log"]), jnp.zeros((1, 128 - 2 * HEADS), F32)], axis=1)
    dtb_row = jnp.concatenate([pad8, row(wts["dt_bias"]), jnp.zeros((1, 128 - 2 * HEADS), F32)], axis=1)
    g_in, b_in = row(wts["ln_in_g"]), row(wts["ln_in_b"])
    g1, b1 = row(wts["ln1_g"]), row(wts["ln1_b"])
    g2, b2 = row(wts["ln2_g"]), row(wts["ln2_b"])

    h0, h0_bf = _ln_in(x, g_in, b_in, tm, start_token)
    first, first_token = first_weights(h0_bf)
    wts = {**wts, **first}
    w_cat = wts["w_cat"]
    proj, gates, qkv_act, dsilu = _proj_conv(h0_bf, w_cat, wts["conv_w"], tms, first_token)
    ypre, d_bf = _pool_fwd(proj, wts["pool_w"], tm)
    bg = _ba_fwd(proj, al_row, dtb_row, tm)
    bgt = bg[:, :2 * HEADS].T
    o, u, w, qg, kg, attn, ymat, vn, states, egl = _dn_fwd(qkv_act, bg, bgt)
    wts = {**wts, **late_weights(o)}
    mixed, a1, h1, h1_bf = _mix_oproj_ln1(o, gates, ypre, pool_scale, wo_row, wts["w_out"], h0, g1, b1, tm)
    act = _mlp_up(h1_bf, wts["w_up"], tm)
    dr, dr_bf, dgp, dpp, r_bf, acc_tail = _tail(act, wts["w_down"], h1, wts["ple_gate_w"], p, wts["ple_proj_w"],
                                                tgt, g2, b2, tms)
    grads = {}
    grads["ple_proj_w"] = _matmul(p, dpp, "tn", "dw_ple_proj", WIRE_DTYPE, tm=256, tn=1024, tk=DW_TK)
    grads["ple_gate_w"] = _matmul(r_bf, dgp, "tn", "dw_ple_gate", WIRE_DTYPE, tm=DW_TM, tn=1024, tk=DW_TK)
    grads["w_down"] = _matmul(act, dr_bf, "tn", "dw_down", WIRE_DTYPE, tm=DW_TM, tn=1024, tk=DW_TK)
    dup = _mlp_bwd1(dr_bf, wts["w_down"], act, tm, 1024)
    grads["w_up"] = _matmul(h1_bf, dup, "tn", "dw_up", WIRE_DTYPE, tm=DW_TM, tn=1024, tk=DW_TK, stack_out=True)
    da1, da1_bf, acc_ln1 = _mlp_bwd2(dup, wts["w_up"], dr, a1, g1, tm)
    grads["w_out"] = _matmul(mixed, da1_bf, "tn", "dw_out", WIRE_DTYPE, tm=DW_TM, tn=1024, tk=DW_TK)
    sent = send_late_grads(grads)
    do, dproj, dyp, acc_mix = _mix_bwd(da1_bf, wts["w_out"], o, gates, ypre, pool_scale, wo_row, tm, sent)
    dproj, grads["pool_w"] = _pool_bwd(dyp, d_bf, wts["pool_w"], dproj, tm)
    dqkv_act, dbg = _dn_bwd(do, qkv_act, bg, bgt, u, w, qg, kg, attn, ymat, vn, states, egl)
    dproj, acc_conv = _conv_bwd(dqkv_act, dsilu, proj, wts["conv_w"], dproj, tm)
    dproj, acc_ba = _ba_bwd(dbg, bg, proj, al_row, dtb_row, dproj, tm)
    dw_cat = _matmul(h0_bf, dproj, "tn", "dw_in", WIRE_DTYPE, tm=DW_TM, tn=1152, tk=DW_TK)
    grads["w_in"] = _w_in_by_chip(dw_cat)
    sent = send_early_grads(grads)
    grad_x, acc_in = _ln_in_bwd(dproj, w_cat, da1, x, g_in, tms, sent)

    grads["conv_w"] = acc_conv[0:CONV_K]
    grads["ln_in_g"], grads["ln_in_b"] = acc_in[0], acc_in[1]
    grads["ln1_g"], grads["ln1_b"] = acc_ln1[0], acc_ln1[1]
    grads["ln2_g"], grads["ln2_b"] = acc_tail[0], acc_tail[1]
    grads["pool_scale"] = acc_mix[0]
    grads["o_norm_w"] = acc_mix[1, 0:HEAD_DIM]
    grads["a_log"] = acc_ba[0, HEADS:2 * HEADS]
    grads["dt_bias"] = acc_ba[1, HEADS:2 * HEADS]
    loss = acc_tail[2, 0]
    return grad_x, grads, loss


MESH = pl.DeviceIdType.MESH
ANY = pl.BlockSpec(memory_space=pl.ANY)


def _chip_of(k, x, y):
    chip = (2 * x + y + k) % N_CHIPS
    return chip // 2, chip % 2


def _place():
    x, y, c = lax.axis_index("x"), lax.axis_index("y"), lax.axis_index("c")
    return x, y, c, 2 * x + y


def _half(rows, c):
    return pl.ds(pl.multiple_of(c * (rows // 2), 16), rows // 2)


def _remote(src, dst, send_sem, recv_sem, device_id):
    return pltpu.make_async_remote_copy(src_ref=src, dst_ref=dst, send_sem=send_sem, recv_sem=recv_sem,
                                        device_id=device_id, device_id_type=MESH)


def _tile_rows(rows):
    for tr in (256, 128, 64, 32, 16):
        if rows % tr == 0:
            return tr
    raise ValueError(rows)


def _first_gather_copies(srcs, lands, send, recv, place):
    copies = []
    for a in range(len(srcs)):
        whole = a == len(srcs) - 1
        for k in range(N_CHIPS):
            if place is None:
                copies.append(None)
                continue
            x, y, c, me = place
            sems = (send.at[a * N_CHIPS + k], recv.at[a * N_CHIPS + k])
            if k == 0:
                copies.append(_remote(srcs[a], lands[a].at[me], *sems, (x, y, 1 - c)))
                continue
            tx, ty = _chip_of(k, x, y)
            if whole:
                copies.append(_remote(srcs[a], lands[a].at[me], *sems, (tx, ty, c)))
            else:
                mine = _half(srcs[a].shape[0], c)
                copies.append(_remote(srcs[a].at[mine], lands[a].at[me, mine], *sems, (tx, ty, c)))
    return copies


def _pass_halves(stacks):
    n = len(stacks)

    def body(*refs):
        outs = refs[n:2 * n]
        send, recv = refs[2 * n:]
        x, y, c, me = _place()
        copies = []
        for a in range(n):
            for k in range(1, N_CHIPS):
                landed = outs[a].at[(me + N_CHIPS - k) % N_CHIPS, _half(stacks[a].shape[1], c)]
                copies.append(_remote(landed, landed, send.at[a * N_CHIPS + k], recv.at[a * N_CHIPS + k],
                                      (x, y, 1 - c)))
        for cp in copies:
            cp.start()
        for cp in copies:
            cp.wait_send()
        for a in range(n):
            for k in range(1, N_CHIPS):
                passed = outs[a].at[(me + N_CHIPS - k) % N_CHIPS, _half(stacks[a].shape[1], 1 - c)]
                _remote(passed, passed, send.at[a * N_CHIPS + k], recv.at[a * N_CHIPS + k], (x, y, c)).wait_recv()

    sems = pltpu.SemaphoreType.DMA((n * N_CHIPS,))
    return pl.pallas_call(
        body, name="pass_halves", out_shape=[SDS(s.shape, s.dtype) for s in stacks],
        in_specs=[ANY] * n, out_specs=[ANY] * n, scratch_shapes=[sems, sems],
        input_output_aliases={a: a for a in range(n)},
    )(*stacks)


def _swap_halves(gs):
    n = len(gs)

    def body(*refs):
        ins, theirs = refs[0:n], refs[n:2 * n]
        send, recv = refs[2 * n:]
        x, y, c, _ = _place()
        copies = [_remote(ins[a].at[:, _half(gs[a].shape[1], 1 - c)], theirs[a], send.at[a], recv.at[a],
                          (x, y, 1 - c)) for a in range(n)]
        for cp in copies:
            cp.start()
        for cp in copies:
            cp.wait()

    return pl.pallas_call(
        body, name="swap_halves", out_shape=[SDS((N_CHIPS, g.shape[1] // 2, g.shape[2]), g.dtype) for g in gs],
        in_specs=[ANY] * n, out_specs=[ANY] * n, scratch_shapes=[pltpu.SemaphoreType.DMA((n,))] * 2,
    )(*gs)


def _send_to_sibling(hs):
    n = len(hs)

    def body(*refs):
        ins, outs = refs[0:n], refs[n:2 * n]
        send, recv = refs[2 * n:]
        x, y, c, _ = _place()
        copies = [_remote(ins[a], outs[a], send.at[a], recv.at[a], (x, y, 1 - c)) for a in range(n)]
        for cp in copies:
            cp.start()
        for cp in copies:
            cp.wait()

    return pl.pallas_call(
        body, name="send_to_sibling", out_shape=[SDS(h.shape, h.dtype) for h in hs],
        in_specs=[ANY] * n, out_specs=[ANY] * n, scratch_shapes=[pltpu.SemaphoreType.DMA((n,))] * 2,
    )(*hs)


HBM = pl.BlockSpec(memory_space=pltpu.HBM)
SEM = pl.BlockSpec(memory_space=pltpu.SEMAPHORE)
EFFECT = pltpu.SideEffectType.DATAFLOW_SIDE_EFFECTING


def _in_hbm(a):
    return pltpu.with_memory_space_constraint(a, pltpu.HBM)


def _split_copy_start(name, srcs, lands, copies_of, after):
    n = len(srcs)
    n_copies = len(copies_of(srcs, lands, None, None, None))

    def body(*refs):
        src_refs, land_refs = refs[0:n], refs[n:2 * n]
        send, recv = refs[2 * n + 1], refs[2 * n + 2]
        token = refs[-1]
        for cp in copies_of(src_refs, land_refs, send, recv, _place()):
            cp.start()
        token[...] = jnp.zeros_like(token)

    sems = pltpu.SemaphoreType.DMA((n_copies,))
    out = pl.pallas_call(
        body, name=name,
        out_shape=[sems, sems] + [pltpu.HBM(a.shape, a.dtype) for a in list(srcs) + list(lands)] + [SDS((8, 128), F32)],
        in_specs=[HBM] * (2 * n) + [ANY],
        out_specs=[SEM, SEM] + [HBM] * (2 * n) + [pl.BlockSpec(memory_space=pltpu.VMEM)],
        input_output_aliases={i: 2 + i for i in range(2 * n)},
        compiler_params=pltpu.CompilerParams(has_side_effects=EFFECT),
    )(*[_in_hbm(a) for a in list(srcs) + list(lands)], after)
    return out[0], out[1], out[2:2 + n], out[2 + n:2 + 2 * n], out[-1]


def _split_copy_wait(name, send, recv, srcs, lands, after, copies_of):
    n = len(srcs)
    after = list(after) if isinstance(after, (list, tuple)) else [after]

    def body(*refs):
        src_refs, land_refs = refs[0:n], refs[n:2 * n]
        send_ref, recv_ref = refs[2 * n], refs[2 * n + 1]
        for cp in copies_of(src_refs, land_refs, send_ref, recv_ref, _place()):
            cp.wait_send()
            cp.wait_recv()

    out = pl.pallas_call(
        body, name=name, out_shape=[pltpu.HBM(a.shape, a.dtype) for a in list(srcs) + list(lands)],
        in_specs=[HBM] * (2 * n) + [SEM, SEM] + [ANY] * len(after), out_specs=[HBM] * (2 * n),
        input_output_aliases={i: i for i in range(2 * n)},
        compiler_params=pltpu.CompilerParams(has_side_effects=EFFECT),
    )(*srcs, *lands, send, recv, *after)
    return out[0:n], out[n:2 * n]


def _late_gather_copies(srcs, lands, send, recv, place):
    copies = []
    for a in range(len(srcs)):
        for k in range(N_CHIPS):
            if place is None:
                copies.append(None)
                continue
            x, y, c, me = place
            if k == 0:
                target = (x, y, 1 - c)
            else:
                tx, ty = _chip_of(k, x, y)
                target = (tx, ty, c)
            copies.append(_remote(srcs[a], lands[a].at[me], send.at[a * N_CHIPS + k], recv.at[a * N_CHIPS + k], target))
    return copies


def _late_scatter_copies(srcs, lands, send, recv, place):
    copies = []
    for a in range(len(srcs)):
        for k in range(1, N_CHIPS):
            if place is None:
                copies.append(None)
                continue
            x, y, c, _ = place
            tx, ty = _chip_of(k, x, y)
            copies.append(_remote(srcs[a].at[2 * tx + ty], lands[a].at[k - 1], send.at[a * (N_CHIPS - 1) + k - 1],
                                  recv.at[a * (N_CHIPS - 1) + k - 1], (tx, ty, c)))
    return copies


def _add_pair(g, theirs, name):
    _, rows, cols = g.shape
    half = rows // 2
    tr = _tile_rows(half)

    def body(g_ref, t_ref, o_ref):
        own = g_ref[lax.axis_index("c")]
        o_ref[...] = (own.astype(F32) + t_ref[...].astype(F32)).astype(o_ref.dtype)

    blk = pl.BlockSpec((None, tr, cols), lambda j, i: (j, i, 0))
    return _pc(body, "add_" + name, (N_CHIPS, half // tr),
               [pl.BlockSpec((None, 2, tr, cols), lambda j, i: (j, 0, i, 0)), blk], blk,
               SDS((N_CHIPS, half, cols), g.dtype), sem=("parallel", "parallel"))(
                   g.reshape(N_CHIPS, 2, half, cols), theirs)


def _sum_slabs(pair, landed, name):
    _, rows, cols = pair.shape
    tr = _tile_rows(rows)

    def body(p_ref, r_ref, o_ref):
        acc = p_ref[2 * lax.axis_index("x") + lax.axis_index("y")].astype(F32)
        for k in range(N_CHIPS - 1):
            acc = acc + r_ref[k].astype(F32)
        o_ref[...] = acc

    return _pc(body, "sum_" + name, (rows // tr,),
               [pl.BlockSpec((N_CHIPS, tr, cols), lambda i: (0, i, 0)),
                pl.BlockSpec((N_CHIPS - 1, tr, cols), lambda i: (0, i, 0))],
               _row(tr, cols), SDS((rows, cols), F32), sem=("parallel",))(pair, landed)


def _adamw_math(w, g, m, v):
    m = ADAM_B1 * m + (1.0 - ADAM_B1) * g
    v = ADAM_B2 * v + (1.0 - ADAM_B2) * (g * g)
    m_hat = m / (1.0 - ADAM_B1 ** ADAM_STEP)
    v_hat = v / (1.0 - ADAM_B2 ** ADAM_STEP)
    delta = -ADAM_LR * (m_hat / (jnp.sqrt(v_hat) + ADAM_EPS) + ADAM_WD * w)
    return delta, m, v


def _adamw_2d(w, g_own, g_sib, m, v, name, halves):
    lead = w.ndim == 3
    rows, cols = w.shape[-2:]
    tr = _tile_rows(rows // 2)
    nh = rows // 2 // tr if halves else rows // tr

    def body(w_ref, go_ref, gs_ref, m_ref, v_ref, g_out, d_out, m_out, v_out):
        if halves:
            mine = (pl.program_id(0) // nh) == lax.axis_index("c")
            g = jnp.where(mine, go_ref[...], gs_ref[...])
        else:
            g = go_ref[...] + gs_ref[...]
        delta, mn, vn = _adamw_math(w_ref[...], g, m_ref[...], v_ref[...])
        g_out[...] = g
        d_out[...] = delta
        m_out[...] = mn
        v_out[...] = vn

    r = _row(tr, cols)
    p = pl.BlockSpec((None, tr, cols), lambda i: (0, i, 0)) if lead else r
    h = pl.BlockSpec((tr, cols), lambda i: (i % nh, 0))
    return _pc(body, "adamw_" + name, (rows // tr,), [p, h, h, p, p], [r] * 4, [SDS((rows, cols), F32)] * 4,
               sem=("parallel",))(w, g_own, g_sib, m, v)


def _small_allreduce_adamw(mine, w, m, v, sizes):
    shape = mine.shape
    n = len(sizes)

    def body(mine_ref, w_ref, m_ref, v_ref, *rest):
        outs, (buf_ref, res_ref, send_sems, recv_sems) = rest[:-4], rest[-4:]
        x, y, c = lax.axis_index("x"), lax.axis_index("y"), lax.axis_index("c")
        me = 4 * x + 2 * y + c
        buf_ref[me] = mine_ref[...]
        copies = []
        for k in range(1, N_DEV):
            tgt = (me + k) % N_DEV
            copies.append(pltpu.make_async_remote_copy(
                src_ref=mine_ref, dst_ref=buf_ref.at[me], send_sem=send_sems.at[k], recv_sem=recv_sems.at[k],
                device_id=(tgt // 4, (tgt // 2) % 2, tgt % 2), device_id_type=MESH))
        for cp in copies:
            cp.start()
        for k in range(1, N_DEV):
            src = (me + N_DEV - k) % N_DEV
            pltpu.make_async_remote_copy(
                src_ref=mine_ref, dst_ref=buf_ref.at[src], send_sem=send_sems.at[k], recv_sem=recv_sems.at[k],
                device_id=(x, y, c), device_id_type=MESH).wait_recv()
        for cp in copies:
            cp.wait_send()
        g = buf_ref[0]
        for j in range(1, N_DEV):
            g = g + buf_ref[j]
        delta, mn, vn = _adamw_math(w_ref[...], g, m_ref[...], v_ref[...])
        for kind, val in enumerate((g, delta, mn, vn)):
            res_ref[kind] = val
            for i, size in enumerate(sizes):
                outs[kind * (n + 1) + i][...] = res_ref[kind, i:i + 1, 0:size]
            outs[kind * (n + 1) + n][...] = res_ref[kind, SMALL_CONV_AT:SMALL_CONV_AT + SMALL_CONV_ROWS, :]
        outs[-1][...] = res_ref[0, n:n + 1, 0:1]

    vm = pl.BlockSpec(memory_space=pltpu.VMEM)
    per_kind = [SDS((1, size), F32) for size in sizes] + [SDS((SMALL_CONV_ROWS, D_MODEL), F32)]
    out_shape = per_kind * 4 + [SDS((1, 1), F32)]
    out = pl.pallas_call(
        body, name="small_allreduce_adamw", out_shape=out_shape, in_specs=[vm] * 4, out_specs=[vm] * len(out_shape),
        scratch_shapes=[pltpu.VMEM((N_DEV,) + shape, F32), pltpu.VMEM((4,) + shape, F32),
                        pltpu.SemaphoreType.DMA((N_DEV,)), pltpu.SemaphoreType.DMA((N_DEV,))],
    )(mine, w, m, v)
    return [out[kind * (n + 1):(kind + 1) * (n + 1)] for kind in range(4)], out[-1]


def _as2d(a):
    return a.reshape(-1, a.shape[-1])


SEGMENTS = ((C_POOL, K_U, POOL_WIDTH), (C_QKV, K_QKV, QKV_WIDTH), (C_Z, K_Z, DN_WIDTH), (C_BETA, K_BA, 2 * HEADS),
            (C_GA, K_GA, D_MODEL), (C_GB, K_GB, D_MODEL))
SHARD_COLS = IN_WIDTH // N_CHIPS


def _w_cat(stack):
    pieces = []
    for c0, _, width in sorted(SEGMENTS, key=lambda seg: seg[1]):
        a = c0
        while a < c0 + width:
            chip = a // SHARD_COLS
            b = min(c0 + width, (chip + 1) * SHARD_COLS)
            pieces.append(stack[chip][:, a - chip * SHARD_COLS:b - chip * SHARD_COLS])
            a = b
    pieces.append(jnp.zeros((D_MODEL, CAT_WIDTH - K_BA - 2 * HEADS), stack.dtype))
    return jnp.concatenate(pieces, axis=1)


def _w_in_by_chip(dw_cat):
    slabs = []
    for chip in range(N_CHIPS):
        lo, hi = chip * SHARD_COLS, (chip + 1) * SHARD_COLS
        pieces = []
        for c0, k0, width in sorted(SEGMENTS):
            a, b = max(c0, lo), min(c0 + width, hi)
            if a < b:
                pieces.append(dw_cat[:, k0 + a - c0:k0 + b - c0])
        slabs.append(jnp.concatenate(pieces, axis=1))
    return jnp.stack(slabs)


WEIGHT_LAYOUT = {
    "w_in": lambda s: ("w_cat", _w_cat(s)),
    "pool_w": lambda s: ("pool_w", s.reshape(N_CHIPS, 4, POOL_GROUP, POOL_OUT_GROUP // N_CHIPS)
                         .transpose(1, 2, 0, 3).reshape(4, POOL_GROUP, POOL_OUT_GROUP)),
    "w_out": lambda s: ("w_out", s.reshape(D_MODEL, D_MODEL)),
    "w_up": lambda s: ("w_up", s),
    "w_down": lambda s: ("w_down", s.reshape(D_FF, D_MODEL)),
    "ple_gate_w": lambda s: ("ple_gate_w", s.reshape(D_MODEL, D_MODEL)),
    "ple_proj_w": lambda s: ("ple_proj_w", s.transpose(1, 0, 2).reshape(PLE_DIM, D_MODEL)),
}

GRAD_LAYOUT = {
    "w_in": lambda g: g,
    "pool_w": lambda g: g.reshape(4, POOL_GROUP, N_CHIPS, POOL_OUT_GROUP // N_CHIPS)
                         .transpose(2, 0, 1, 3).reshape(N_CHIPS, 4 * POOL_GROUP, POOL_OUT_GROUP // N_CHIPS),
    "w_out": lambda g: g.reshape(N_CHIPS, D_MODEL // N_CHIPS, D_MODEL),
    "w_up": lambda g: g,
    "w_down": lambda g: g.reshape(N_CHIPS, D_FF // N_CHIPS, D_MODEL),
    "ple_gate_w": lambda g: g.reshape(N_CHIPS, D_MODEL // N_CHIPS, D_MODEL),
    "ple_proj_w": lambda g: g.reshape(PLE_DIM, N_CHIPS, D_MODEL // N_CHIPS).transpose(1, 0, 2),
}


def _full_weights(names, stacks):
    return dict(WEIGHT_LAYOUT[n](s.astype(MXU_DTYPE)) for n, s in zip(names, stacks))


def _grads_by_chip(names, grads):
    return [GRAD_LAYOUT[n](grads[n]).astype(WIRE_DTYPE) for n in names]


def _pack_small(rows, conv, name):
    n = len(rows)

    def body(*refs):
        out = refs[n + 1]
        out[...] = jnp.zeros_like(out)
        for i in range(n):
            out[i:i + 1, :] = refs[i][...]
        out[SMALL_CONV_AT:SMALL_CONV_AT + SMALL_CONV_ROWS, :] = refs[n][...]

    vm = pl.BlockSpec(memory_space=pltpu.VMEM)
    return pl.pallas_call(body, name=name, out_shape=SDS((SMALL_CONV_AT + SMALL_CONV_ROWS, D_MODEL), F32),
                          in_specs=[vm] * (n + 1), out_specs=vm)(*rows, conv)


def _pad_row(a):
    a = a.reshape(1, -1).astype(F32)
    return jnp.pad(a, ((0, 0), (0, D_MODEL - a.shape[1])))


def kernel(x, p, ln_in_g, ln_in_b, w_in, pool_w, pool_scale, conv_w, a_log, dt_bias, o_norm_w, w_out, ln1_g, ln1_b, w_up, w_down, ple_gate_w, ple_proj_w, ln2_g, ln2_b, loss_target, m_ln_in_g, m_ln_in_b, m_w_in, m_pool_w, m_pool_scale, m_conv_w, m_a_log, m_dt_bias, m_o_norm_w, m_w_out, m_ln1_g, m_ln1_b, m_w_up, m_w_down, m_ple_gate_w, m_ple_proj_w, m_ln2_g, m_ln2_b, v_ln_in_g, v_ln_in_b, v_w_in, v_pool_w, v_pool_scale, v_conv_w, v_a_log, v_dt_bias, v_o_norm_w, v_w_out, v_ln1_g, v_ln1_b, v_w_up, v_w_down, v_ple_gate_w, v_ple_proj_w, v_ln2_g, v_ln2_b):
    given = dict(locals())
    chip = 2 * lax.axis_index("x") + lax.axis_index("y")

    shard = lambda n: _as2d(given[n]).astype(WIRE_DTYPE)

    wts = {"ln_in_g": ln_in_g, "ln_in_b": ln_in_b, "pool_scale": pool_scale[0], "a_log": a_log[0],
           "dt_bias": dt_bias[0], "o_norm_w": o_norm_w[0], "ln1_g": ln1_g[0], "ln1_b": ln1_b[0],
           "ln2_g": ln2_g[0], "ln2_b": ln2_b[0]}

    conv_pad = jnp.pad(conv_w[0], ((0, 8 - CONV_K), (0, 0)))
    first_srcs = [shard(n) for n in EARLY] + [conv_pad]
    first_lands = [lax.empty((N_CHIPS,) + s.shape, s.dtype) for s in first_srcs]
    fsend, frecv, fsrcs, flands, start_token = _split_copy_start(
        "first_gather_start", first_srcs, first_lands, _first_gather_copies, first_srcs[0])
    late = {}
    for n in ("w_in", "m_w_in", "v_w_in"):
        given[n], _ = lax.optimization_barrier((given[n], start_token))

    def first_weights(after):
        _, lands = _split_copy_wait("first_gather_wait", fsend, frecv, fsrcs, flands,
                                    [after, given["w_in"], given["m_w_in"], given["v_w_in"]], _first_gather_copies)
        stacks = _pass_halves(lands[0:len(EARLY)])
        first = _full_weights(EARLY, stacks)
        first["conv_w"] = jnp.concatenate([lands[len(EARLY)][j, 0:CONV_K] for j in range(N_CHIPS)], axis=1)
        late_srcs = [shard(n) for n in LATE]
        late_lands = [lax.empty((N_CHIPS,) + s.shape, s.dtype) for s in late_srcs]
        late["send"], late["recv"], late["srcs"], late["lands"], token = _split_copy_start(
            "late_gather_start", late_srcs, late_lands, _late_gather_copies, stacks[0])
        return first, token

    def late_weights(after):
        _, stacks = _split_copy_wait("late_gather_wait", late["send"], late["recv"], late["srcs"], late["lands"],
                                     after, _late_gather_copies)
        return _full_weights(LATE, stacks)

    scatter = {}

    def send_late_grads(grads):
        srcs = _grads_by_chip(LATE, grads)
        lands = [lax.empty((N_CHIPS - 1,) + g.shape[1:], g.dtype) for g in srcs]
        scatter["send"], scatter["recv"], scatter["srcs"], scatter["lands"], token = _split_copy_start(
            "late_scatter_start", srcs, lands, _late_scatter_copies, srcs[0])
        return token

    last = {}

    def send_early_grads(grads):
        by_chip = _grads_by_chip(EARLY, grads)
        theirs = _swap_halves(by_chip)
        pair = [_add_pair(g, t, n) for g, t, n in zip(by_chip, theirs, EARLY)]
        lands = [lax.empty((N_CHIPS - 1,) + q.shape[1:], q.dtype) for q in pair]
        last["send"], last["recv"], last["srcs"], last["lands"], token = _split_copy_start(
            "early_scatter_start", pair, lands, _late_scatter_copies, pair[0])
        return token

    grad_x, grads, loss = _local_step(x[0], p[0, 0], loss_target[0], wts, start_token, first_weights, late_weights,
                                      send_late_grads, send_early_grads)

    late_mine, late_landed = _split_copy_wait("late_scatter_wait", scatter["send"], scatter["recv"], scatter["srcs"],
                                              scatter["lands"], grad_x, _late_scatter_copies)
    late_part = [_sum_slabs(q, r, n) for q, r, n in zip(late_mine, late_landed, LATE)]
    pair, landed = _split_copy_wait("early_scatter_wait", last["send"], last["recv"], last["srcs"], last["lands"],
                                    grad_x, _late_scatter_copies)
    reduced = [_sum_slabs(q, r, n) for q, r, n in zip(pair, landed, EARLY)]
    from_sibling = _send_to_sibling(reduced + late_part)
    big_out = {}
    for n, g_own, g_sib in zip(EARLY + LATE, reduced + late_part, from_sibling):
        view = (lambda a: a) if given[n].ndim == 3 else _as2d
        res = _adamw_2d(view(given[n]), g_own, g_sib, view(given["m_" + n]), view(given["v_" + n]), n,
                        halves=n in EARLY)
        big_out[n] = [r.reshape(given[n].shape) for r in res]

    conv_cols = QKV_WIDTH // N_CHIPS

    def small_pack(get, conv, extra, name):
        if conv.shape[1] != QKV_WIDTH:
            conv = lax.dynamic_update_slice(jnp.zeros((CONV_K, QKV_WIDTH), F32), conv, (0, chip * conv_cols))
        return _pack_small([_pad_row(get(n)) for n in SMALL_NAMES] + extra, conv.reshape(SMALL_CONV_ROWS, D_MODEL), name)

    mine_small = small_pack(lambda n: grads[n], grads["conv_w"], [jnp.full((1, D_MODEL), loss, F32)], "pack_small_g")
    packed_small = [small_pack(lambda n: given[prefix + n], given[prefix + "conv_w"][0], [], "pack_small_" + tag)
                    for prefix, tag in (("", "w"), ("m_", "m"), ("v_", "v"))]
    small_out, loss_sum = _small_allreduce_adamw(mine_small, *packed_small, [given[n].size for n in SMALL_NAMES])

    def small_get(k, n):
        if n == "conv_w":
            full = small_out[k][len(SMALL_NAMES)].reshape(CONV_K, QKV_WIDTH)
            return lax.dynamic_slice(full, (0, chip * conv_cols), (CONV_K, conv_cols)).reshape(given[n].shape)
        return small_out[k][SMALL_NAMES.index(n)].reshape(given[n].shape)

    order = ["ln_in_g", "ln_in_b", "w_in", "pool_w", "pool_scale", "conv_w", "a_log", "dt_bias", "o_norm_w", "w_out",
             "ln1_g", "ln1_b", "w_up", "w_down", "ple_gate_w", "ple_proj_w", "ln2_g", "ln2_b"]
    outs = [loss_sum.reshape(()), grad_x[None]]
    for k in range(4):
        for n in order:
            outs.append(big_out[n][k] if n in big_out else small_get(k, n))
    return tuple(outs)
```
